```python
import jax, jax.numpy as jnp
from jax import lax
import numpy as np

D_MODEL = 1024
BATCH = 8
SEQ = 2048
DEPTH = 1

A_HEADS = 16
A_KV_HEADS = 2
A_HEAD_DIM = 64
A_GROUP = A_HEADS // A_KV_HEADS
A_WIDTH = A_HEADS * A_HEAD_DIM
A_KV_WIDTH = A_KV_HEADS * A_HEAD_DIM
WINDOW = 128
BLOCK = 128
ROPE_THETA = 10000.0

B_HEADS = 4
B_QK_WIDTH = D_MODEL // 2
B_V_WIDTH = D_MODEL
B_KEY_DIM = B_QK_WIDTH // B_HEADS
B_VAL_DIM = B_V_WIDTH // B_HEADS
GATE_RANK = 16
GATE_TAU = 16.0
CHUNK = 64

EPS = 1e-5
NEG_INF = -1e30

IN_SPLITS = (A_WIDTH, A_KV_WIDTH, A_KV_WIDTH, A_WIDTH,
             B_QK_WIDTH, B_QK_WIDTH, B_V_WIDTH, B_V_WIDTH,
             GATE_RANK,
             D_MODEL, D_MODEL)
IN_WIDTH = (2 * A_WIDTH + 2 * A_KV_WIDTH + 2 * B_QK_WIDTH + 2 * B_V_WIDTH
            + GATE_RANK + 2 * D_MODEL)

kernel_name = 'hybrid_swa_sink_gla_gated_block'


def rms_norm(x, w):
    xf = x.astype(jnp.float32)
    y = xf * lax.rsqrt(jnp.mean(xf * xf, axis=-1, keepdims=True) + EPS)
    return (y * w.astype(jnp.float32)).astype(x.dtype)


def rope(x, positions):
    half = A_HEAD_DIM // 2
    inv_freq = ROPE_THETA ** (-jnp.arange(half, dtype=jnp.float32) / half)
    ang = positions.astype(jnp.float32)[..., None] * inv_freq
    cos = jnp.cos(ang)[:, :, None, :]
    sin = jnp.sin(ang)[:, :, None, :]
    xf = x.astype(jnp.float32)
    x1, x2 = xf[..., :half], xf[..., half:]
    return jnp.concatenate([x1 * cos - x2 * sin, x2 * cos + x1 * sin], axis=-1).astype(x.dtype)


def sliding_window_attention(q, k, v, sinks):
    bsz, t = q.shape[0], q.shape[1]
    nb = t // BLOCK
    qb = q.reshape(bsz, nb, BLOCK, A_KV_HEADS, A_GROUP, A_HEAD_DIM)
    kb = k.reshape(bsz, nb, BLOCK, A_KV_HEADS, A_HEAD_DIM)
    vb = v.reshape(bsz, nb, BLOCK, A_KV_HEADS, A_HEAD_DIM)
    pad = ((0, 0), (1, 0), (0, 0), (0, 0), (0, 0))
    keys = jnp.concatenate([jnp.pad(kb, pad)[:, :-1], kb], axis=2)
    vals = jnp.concatenate([jnp.pad(vb, pad)[:, :-1], vb], axis=2)
    s = jnp.einsum('bnqhgd,bnkhd->bnhgqk', qb, keys).astype(jnp.float32) * (A_HEAD_DIM ** -0.5)
    qi = jnp.arange(BLOCK)[:, None]
    ki = jnp.arange(2 * BLOCK)[None, :]
    rel = qi + BLOCK - ki
    band = (rel >= 0) & (rel < WINDOW)
    blk = jnp.arange(nb)[:, None, None]
    valid = band[None] & ((blk > 0) | (ki >= BLOCK)[None])
    s = jnp.where(valid[None, :, None, None], s, NEG_INF)
    sink = jnp.broadcast_to(
        sinks.astype(jnp.float32).reshape(A_KV_HEADS, A_GROUP)[None, None, :, :, None, None],
        s.shape[:-1] + (1,))
    p = jax.nn.softmax(jnp.concatenate([s, sink], axis=-1), axis=-1)[..., :-1]
    o = jnp.einsum('bnhgqk,bnkhd->bnqhgd', p.astype(v.dtype), vals)
    return o.reshape(bsz, t, A_WIDTH)


def gated_linear_attention(q, k, v, log_a):
    bsz, t = q.shape[0], q.shape[1]
    n = t // CHUNK

    def chunks(a):
        return a.astype(jnp.float32).reshape(bsz, n, CHUNK, B_HEADS, -1).transpose(0, 1, 3, 2, 4)

    qc = chunks(q) * (B_KEY_DIM ** -0.5)
    kc, vc = chunks(k), chunks(v)
    b = jnp.cumsum(chunks(log_a), axis=3)
    b_last = b[:, :, :, -1:, :]
    q_e = qc * jnp.exp(b)
    k_e = kc * jnp.exp(-b)
    k_s = kc * jnp.exp(b_last - b)
    causal = jnp.tril(jnp.ones((CHUNK, CHUNK), dtype=bool))
    att = jnp.where(causal, jnp.einsum('bnhid,bnhjd->bnhij', q_e, k_e), 0.0)
    o_intra = jnp.einsum('bnhij,bnhjv->bnhiv', att, vc)
    inc = jnp.einsum('bnhjd,bnhjv->bnhdv', k_s, vc)
    decay = jnp.exp(b_last[:, :, :, 0, :])

    def step(state, inp):
        dec, dS = inp
        return dec[..., None] * state + dS, state

    s0 = jnp.zeros((bsz, B_HEADS, B_KEY_DIM, B_VAL_DIM), jnp.float32)
    _, states = lax.scan(step, s0, (decay.transpose(1, 0, 2, 3), inc.transpose(1, 0, 2, 3, 4)))
    states = states.transpose(1, 0, 2, 3, 4)
    o = o_intra + jnp.einsum('bnhid,bnhdv->bnhiv', q_e, states)
    return o.transpose(0, 1, 3, 2, 4).reshape(bsz, t, B_HEADS, B_VAL_DIM)


def _fwd_setup_inputs(seed: int = 0) -> dict:
    key = jax.random.key(seed)
    ks = jax.random.split(key, 13)
    f32 = jnp.float32

    def lin(k, shape, fan_in):
        return jax.random.normal(k, shape, f32) * (fan_in ** -0.5)

    x = jax.random.normal(ks[0], (BATCH, SEQ, D_MODEL), f32)
    positions = jnp.broadcast_to(jnp.arange(SEQ, dtype=jnp.int32)[None, :], (BATCH, SEQ))
    norm_w = 1.0 + 0.02 * jax.random.normal(ks[1], (DEPTH, D_MODEL), f32)
    w_in = lin(ks[2], (DEPTH, D_MODEL, IN_WIDTH), D_MODEL)
    a_sinks = 0.5 * jax.random.normal(ks[3], (DEPTH, A_HEADS), f32)
    b_gate_up = lin(ks[4], (DEPTH, GATE_RANK, B_QK_WIDTH), GATE_RANK)
    b_gate_bias = 0.1 * jax.random.normal(ks[5], (DEPTH, B_QK_WIDTH), f32)
    b_out_norm_w = 1.0 + 0.02 * jax.random.normal(ks[6], (DEPTH, B_VAL_DIM), f32)
    w_a_proj = lin(ks[7], (DEPTH, A_WIDTH, D_MODEL), A_WIDTH)
    w_b_proj = lin(ks[8], (DEPTH, B_V_WIDTH, D_MODEL), B_V_WIDTH)
    w_out = lin(ks[9], (DEPTH, D_MODEL, D_MODEL), D_MODEL)
    final_norm_w = 1.0 + 0.02 * jax.random.normal(ks[10], (D_MODEL,), f32)
    return {'x': x, 'positions': positions, 'norm_w': norm_w, 'w_in': w_in,
            'a_sinks': a_sinks, 'b_gate_up': b_gate_up, 'b_gate_bias': b_gate_bias,
            'b_out_norm_w': b_out_norm_w, 'w_a_proj': w_a_proj, 'w_b_proj': w_b_proj,
            'w_out': w_out, 'final_norm_w': final_norm_w}


def _fwd_reference(x, positions, norm_w, w_in, a_sinks, b_gate_up, b_gate_bias,
              b_out_norm_w, w_a_proj, w_b_proj, w_out, final_norm_w):
    bsz, t = x.shape[0], x.shape[1]
    offsets = [int(o) for o in np.cumsum(IN_SPLITS)[:-1]]
    for layer in range(DEPTH):
        h = rms_norm(x, norm_w[layer])
        proj = jnp.einsum('btd,de->bte', h, w_in[layer])
        (a_q, a_k, a_v, a_gate, b_q, b_k, b_v, b_gate, b_low,
         m_a, m_b) = jnp.split(proj, offsets, axis=-1)

        q = rope(a_q.reshape(bsz, t, A_HEADS, A_HEAD_DIM), positions)
        k = rope(a_k.reshape(bsz, t, A_KV_HEADS, A_HEAD_DIM), positions)
        v = a_v.reshape(bsz, t, A_KV_HEADS, A_HEAD_DIM)
        o_a = sliding_window_attention(q, k, v, a_sinks[layer]) * jax.nn.silu(a_gate)
        y_a = jnp.einsum('bte,ed->btd', o_a, w_a_proj[layer])

        gk = jnp.einsum('btr,re->bte', b_low, b_gate_up[layer]) + b_gate_bias[layer]
        log_a = jax.nn.log_sigmoid(gk.astype(jnp.float32)) / GATE_TAU
        o_b = gated_linear_attention(b_q.reshape(bsz, t, B_HEADS, B_KEY_DIM),
                                     b_k.reshape(bsz, t, B_HEADS, B_KEY_DIM),
                                     b_v.reshape(bsz, t, B_HEADS, B_VAL_DIM),
                                     log_a.reshape(bsz, t, B_HEADS, B_KEY_DIM))
        o_b = rms_norm(o_b.astype(x.dtype), b_out_norm_w[layer]).reshape(bsz, t, B_V_WIDTH)
        o_b = o_b * jax.nn.silu(b_gate)
        y_b = jnp.einsum('bte,ed->btd', o_b, w_b_proj[layer])

        merged = jax.nn.sigmoid(m_a) * y_a + jax.nn.sigmoid(m_b) * y_b
        x = x + jnp.einsum('btd,de->bte', merged, w_out[layer])
    return rms_norm(x, final_norm_w)


import jax as _jax
import jax.numpy as _jnp

TWIN_FORMAT = 'train_step'
FWD_PARAMS = ['x', 'positions', 'norm_w', 'w_in', 'a_sinks', 'b_gate_up', 'b_gate_bias', 'b_out_norm_w', 'w_a_proj', 'w_b_proj', 'w_out', 'final_norm_w']
TWIN_WEIGHTS = ['norm_w', 'w_in', 'a_sinks', 'b_gate_up', 'b_gate_bias', 'b_out_norm_w', 'w_a_proj', 'w_b_proj', 'w_out', 'final_norm_w']
TWIN_DIFF_INPUT = 'x'
TWIN_INPUTS = ['x', 'positions', 'norm_w', 'w_in', 'a_sinks', 'b_gate_up', 'b_gate_bias', 'b_out_norm_w', 'w_a_proj', 'w_b_proj', 'w_out', 'final_norm_w', 'loss_target', 'm_norm_w', 'm_w_in', 'm_a_sinks', 'm_b_gate_up', 'm_b_gate_bias', 'm_b_out_norm_w', 'm_w_a_proj', 'm_w_b_proj', 'm_w_out', 'm_final_norm_w', 'v_norm_w', 'v_w_in', 'v_a_sinks', 'v_b_gate_up', 'v_b_gate_bias', 'v_b_out_norm_w', 'v_w_a_proj', 'v_w_b_proj', 'v_w_out', 'v_final_norm_w']
TWIN_OUTPUTS = ['loss', 'grad_x', 'grad_norm_w', 'grad_w_in', 'grad_a_sinks', 'grad_b_gate_up', 'grad_b_gate_bias', 'grad_b_out_norm_w', 'grad_w_a_proj', 'grad_w_b_proj', 'grad_w_out', 'grad_final_norm_w', 'delta_norm_w', 'delta_w_in', 'delta_a_sinks', 'delta_b_gate_up', 'delta_b_gate_bias', 'delta_b_out_norm_w', 'delta_w_a_proj', 'delta_w_b_proj', 'delta_w_out', 'delta_final_norm_w', 'new_m_norm_w', 'new_m_w_in', 'new_m_a_sinks', 'new_m_b_gate_up', 'new_m_b_gate_bias', 'new_m_b_out_norm_w', 'new_m_w_a_proj', 'new_m_w_b_proj', 'new_m_w_out', 'new_m_final_norm_w', 'new_v_norm_w', 'new_v_w_in', 'new_v_a_sinks', 'new_v_b_gate_up', 'new_v_b_gate_bias', 'new_v_b_out_norm_w', 'new_v_w_a_proj', 'new_v_w_b_proj', 'new_v_w_out', 'new_v_final_norm_w']
TWIN_LEAF_KINDS = {'loss': 'loss', 'grad_x': 'grad_x', 'grad_norm_w': 'grad_w', 'grad_w_in': 'grad_w', 'grad_a_sinks': 'grad_w', 'grad_b_gate_up': 'grad_w', 'grad_b_gate_bias': 'grad_w', 'grad_b_out_norm_w': 'grad_w', 'grad_w_a_proj': 'grad_w', 'grad_w_b_proj': 'grad_w', 'grad_w_out': 'grad_w', 'grad_final_norm_w': 'grad_w', 'delta_norm_w': 'delta_w', 'delta_w_in': 'delta_w', 'delta_a_sinks': 'delta_w', 'delta_b_gate_up': 'delta_w', 'delta_b_gate_bias': 'delta_w', 'delta_b_out_norm_w': 'delta_w', 'delta_w_a_proj': 'delta_w', 'delta_w_b_proj': 'delta_w', 'delta_w_out': 'delta_w', 'delta_final_norm_w': 'delta_w', 'new_m_norm_w': 'new_m', 'new_m_w_in': 'new_m', 'new_m_a_sinks': 'new_m', 'new_m_b_gate_up': 'new_m', 'new_m_b_gate_bias': 'new_m', 'new_m_b_out_norm_w': 'new_m', 'new_m_w_a_proj': 'new_m', 'new_m_w_b_proj': 'new_m', 'new_m_w_out': 'new_m', 'new_m_final_norm_w': 'new_m', 'new_v_norm_w': 'new_v', 'new_v_w_in': 'new_v', 'new_v_a_sinks': 'new_v', 'new_v_b_gate_up': 'new_v', 'new_v_b_gate_bias': 'new_v', 'new_v_b_out_norm_w': 'new_v', 'new_v_w_a_proj': 'new_v', 'new_v_w_b_proj': 'new_v', 'new_v_w_out': 'new_v', 'new_v_final_norm_w': 'new_v'}


def _forward(args):
    return _fwd_reference(*[args[k] for k in FWD_PARAMS])


def _output_shape():
    out = _jax.eval_shape(lambda: _forward(_fwd_setup_inputs(0)))
    return out.shape, out.dtype

N_MICROBATCH = 1
ADAM_LR = 0.001
ADAM_B1 = 0.9
ADAM_B2 = 0.999
ADAM_EPS = 1e-08
ADAM_WD = 0.01
ADAM_STEP = 10
PER_EXAMPLE_BATCH_AXIS = {'x': 0, 'positions': 0, 'loss_target': 0}
SHARED_INPUTS = []
_WEIGHT_DTYPES = {'norm_w': _jnp.float32, 'w_in': _jnp.float32, 'a_sinks': _jnp.float32, 'b_gate_up': _jnp.float32, 'b_gate_bias': _jnp.float32, 'b_out_norm_w': _jnp.float32, 'w_a_proj': _jnp.float32, 'w_b_proj': _jnp.float32, 'w_out': _jnp.float32, 'final_norm_w': _jnp.float32}
MOMENT_SCALE = {'norm_w': 8.110337e-02, 'w_in': 3.071672e-02, 'a_sinks': 6.318827e-03, 'b_gate_up': 6.221141e-03, 'b_gate_bias': 2.751294e-02, 'b_out_norm_w': 8.337099e-02, 'w_a_proj': 7.808026e-03, 'w_b_proj': 3.851709e-02, 'w_out': 3.927821e-02, 'final_norm_w': 1.599830e+01}


def _to_microbatches(a, axis):
    t = _jnp.moveaxis(a, axis, 0)
    t = t.reshape((N_MICROBATCH, t.shape[0] // N_MICROBATCH) + t.shape[1:])
    return _jnp.moveaxis(t, 1, axis + 1)


def setup_inputs(seed: int = 0) -> dict:
    inp = _fwd_setup_inputs(seed)
    key = _jax.random.fold_in(_jax.random.key(seed), 7919)
    shape, _ = _output_shape()
    out = dict(inp)
    out["loss_target"] = _jax.random.normal(_jax.random.fold_in(key, 0), shape, _jnp.float32)
    for i, name in enumerate(TWIN_WEIGHTS):
        w = inp[name].astype(_jnp.float32)
        if MOMENT_SCALE is None:
            s = _jnp.sqrt(_jnp.mean(_jnp.square(w)) + 1e-30)
        else:
            s = MOMENT_SCALE[name]
        km, kv = _jax.random.split(_jax.random.fold_in(key, i + 1))
        out[name] = w
        out["m_" + name] = s * _jax.random.normal(km, w.shape, _jnp.float32)
        out["v_" + name] = (s * s) * _jax.random.uniform(kv, w.shape, _jnp.float32, 0.5, 1.5)
    if N_MICROBATCH > 1:
        for name, axis in PER_EXAMPLE_BATCH_AXIS.items():
            out[name] = _to_microbatches(out[name], axis)
    return {'x': out['x'], 'positions': out['positions'], 'norm_w': out['norm_w'], 'w_in': out['w_in'], 'a_sinks': out['a_sinks'], 'b_gate_up': out['b_gate_up'], 'b_gate_bias': out['b_gate_bias'], 'b_out_norm_w': out['b_out_norm_w'], 'w_a_proj': out['w_a_proj'], 'w_b_proj': out['w_b_proj'], 'w_out': out['w_out'], 'final_norm_w': out['final_norm_w'], 'loss_target': out['loss_target'], 'm_norm_w': out['m_norm_w'], 'm_w_in': out['m_w_in'], 'm_a_sinks': out['m_a_sinks'], 'm_b_gate_up': out['m_b_gate_up'], 'm_b_gate_bias': out['m_b_gate_bias'], 'm_b_out_norm_w': out['m_b_out_norm_w'], 'm_w_a_proj': out['m_w_a_proj'], 'm_w_b_proj': out['m_w_b_proj'], 'm_w_out': out['m_w_out'], 'm_final_norm_w': out['m_final_norm_w'], 'v_norm_w': out['v_norm_w'], 'v_w_in': out['v_w_in'], 'v_a_sinks': out['v_a_sinks'], 'v_b_gate_up': out['v_b_gate_up'], 'v_b_gate_bias': out['v_b_gate_bias'], 'v_b_out_norm_w': out['v_b_out_norm_w'], 'v_w_a_proj': out['v_w_a_proj'], 'v_w_b_proj': out['v_w_b_proj'], 'v_w_out': out['v_w_out'], 'v_final_norm_w': out['v_final_norm_w']}


def _loss(weights, diff, rest, loss_target):
    with _jax.named_scope("forward"):
        args = {**rest, TWIN_DIFF_INPUT: diff, **{k: w.astype(_WEIGHT_DTYPES[k]) for k, w in weights.items()}}
        y = _forward(args)
    with _jax.named_scope("loss_head"):
        err = _jnp.square(y.astype(_jnp.float32) - loss_target)
        return 0.5 * _jnp.sum(_jnp.mean(err, axis=-1)) if err.ndim else 0.5 * err


def _adamw(w, g, m, v):
    m = ADAM_B1 * m + (1.0 - ADAM_B1) * g
    v = ADAM_B2 * v + (1.0 - ADAM_B2) * _jnp.square(g)
    m_hat = m / (1.0 - ADAM_B1 ** ADAM_STEP)
    v_hat = v / (1.0 - ADAM_B2 ** ADAM_STEP)
    delta = -ADAM_LR * (m_hat / (_jnp.sqrt(v_hat) + ADAM_EPS) + ADAM_WD * w)
    return delta, m, v


def reference(x, positions, norm_w, w_in, a_sinks, b_gate_up, b_gate_bias, b_out_norm_w, w_a_proj, w_b_proj, w_out, final_norm_w, loss_target, m_norm_w, m_w_in, m_a_sinks, m_b_gate_up, m_b_gate_bias, m_b_out_norm_w, m_w_a_proj, m_w_b_proj, m_w_out, m_final_norm_w, v_norm_w, v_w_in, v_a_sinks, v_b_gate_up, v_b_gate_bias, v_b_out_norm_w, v_w_a_proj, v_w_b_proj, v_w_out, v_final_norm_w):
    given = dict(x=x, positions=positions, norm_w=norm_w, w_in=w_in, a_sinks=a_sinks, b_gate_up=b_gate_up, b_gate_bias=b_gate_bias, b_out_norm_w=b_out_norm_w, w_a_proj=w_a_proj, w_b_proj=w_b_proj, w_out=w_out, final_norm_w=final_norm_w, loss_target=loss_target, m_norm_w=m_norm_w, m_w_in=m_w_in, m_a_sinks=m_a_sinks, m_b_gate_up=m_b_gate_up, m_b_gate_bias=m_b_gate_bias, m_b_out_norm_w=m_b_out_norm_w, m_w_a_proj=m_w_a_proj, m_w_b_proj=m_w_b_proj, m_w_out=m_w_out, m_final_norm_w=m_final_norm_w, v_norm_w=v_norm_w, v_w_in=v_w_in, v_a_sinks=v_a_sinks, v_b_gate_up=v_b_gate_up, v_b_gate_bias=v_b_gate_bias, v_b_out_norm_w=v_b_out_norm_w, v_w_a_proj=v_w_a_proj, v_w_b_proj=v_w_b_proj, v_w_out=v_w_out, v_final_norm_w=v_final_norm_w)
    weights = {n: given[n] for n in TWIN_WEIGHTS}
    shared = {n: given[n] for n in SHARED_INPUTS}
    per_example = {n: given[n] for n in ['x', 'positions']}
    grad_fn = _jax.value_and_grad(_loss, argnums=(0, 1))

    def one_microbatch(ex, loss_target):
        ex = dict(ex)
        diff = ex.pop(TWIN_DIFF_INPUT)
        return grad_fn(weights, diff, {**shared, **ex}, loss_target)

    if N_MICROBATCH == 1:
        loss, (grad_w, grad_x) = one_microbatch(per_example, given["loss_target"])
    else:
        def body(carry, xs):
            loss_sum, grad_sum = carry
            l_k, (gw_k, gx_k) = one_microbatch(xs[0], xs[1])
            with _jax.named_scope("update"):
                return (loss_sum + l_k, _jax.tree.map(_jnp.add, grad_sum, gw_k)), gx_k

        init = (_jnp.zeros((), _jnp.float32), _jax.tree.map(_jnp.zeros_like, weights))
        (loss, grad_w), grad_x = _jax.lax.scan(body, init, (per_example, given["loss_target"]))
    with _jax.named_scope("update"):
        delta_w, new_m, new_v = {}, {}, {}
        for n in TWIN_WEIGHTS:
            delta_w[n], new_m[n], new_v[n] = _adamw(weights[n], grad_w[n], given["m_" + n], given["v_" + n])
    return (loss, grad_x, *[grad_w[n] for n in TWIN_WEIGHTS], *[delta_w[n] for n in TWIN_WEIGHTS],
            *[new_m[n] for n in TWIN_WEIGHTS], *[new_v[n] for n in TWIN_WEIGHTS])
```

```python
import functools

import numpy as np
import jax
import jax.numpy as jnp
from jax import lax
from jax.experimental import pallas as pl
from jax.experimental.pallas import tpu as pltpu

F32 = jnp.float32
MXU = jnp.bfloat16
WIRE = jnp.bfloat16

D = 1024
A_HEADS, A_KV, A_HD = 16, 2, 64
BLK = 128
B_HEADS, B_DK, B_DV = 4, 128, 256
RANK, TAU, CHUNK = 16, 16.0, 64
EPS, NEG = 1e-5, -1e30
ROPE_THETA = 10000.0
IN_WIDTH, NDEV = 7440, 8
SHARD = IN_WIDTH // NDEV
LANE = 128

C_Q, C_AG, C_BV, C_BG, C_MA, C_MB = 0, 1024, 2048, 3072, 4096, 5120
C_BQ, C_BK, C_KD, C_VD, C_BL = 6144, 6656, 7168, 7424, 7680
NF = 8192
W_BL = 128

R_IN, R_A, R_B, R_O, ROWS = 0, 1024, 1152, 1280, 1408
GU_COL = 930
SMALL_ROWS = 40

ADAM_LR, ADAM_B1, ADAM_B2, ADAM_EPS, ADAM_WD, ADAM_STEP = 0.001, 0.9, 0.999, 1e-08, 0.01, 10

MESH = pl.DeviceIdType.MESH
VMEM_LIMIT = 56 * 1024 * 1024


def _cp(sem=None, **kw):
    if sem is not None:
        kw["dimension_semantics"] = sem
    return pltpu.CompilerParams(vmem_limit_bytes=VMEM_LIMIT, **kw)


def _dot(a, b):
    return jnp.dot(a, b, preferred_element_type=F32)


def _dot_nt(a, b):
    return lax.dot_general(a, b, (((1,), (1,)), ((), ())), preferred_element_type=F32)


def _dot_tn(a, b):
    return lax.dot_general(a, b, (((0,), (0,)), ((), ())), preferred_element_type=F32)


def _dot_f32(a, b):
    return jnp.dot(a, b, preferred_element_type=F32, precision=lax.Precision.HIGHEST)


def _sigmoid(z):
    return 1.0 / (1.0 + jnp.exp(-z))


def _rope(xp, cos, sin):
    return xp * cos + pltpu.roll(xp, 64, 1) * sin


def _rope_bwd(dy, cos, sin):
    return dy * cos - pltpu.roll(dy, 64, 1) * sin


def _vmem():
    return pl.BlockSpec(memory_space=pltpu.VMEM)


def _any():
    return pl.BlockSpec(memory_space=pl.ANY)


def _rope_tables(pos_col):
    T = pos_col.shape[0]
    tT = min(T, 512)
    half = A_HD // 2
    inv = (np.float32(ROPE_THETA) ** (-np.arange(half, dtype=np.float32) / np.float32(half))).astype(np.float32)
    inv_row = jnp.asarray(np.tile(inv, 4)[None, :])
    sign_row = jnp.asarray(np.concatenate([-np.ones(64, np.float32), np.ones(64, np.float32)])[None, :])

    def body(pos_ref, inv_ref, sign_ref, cos_ref, sin_ref):
        ang = pos_ref[...].astype(F32) * inv_ref[...]
        cos_ref[...] = jnp.cos(ang)
        sin_ref[...] = jnp.sin(ang) * sign_ref[...]

    row = pl.BlockSpec((1, LANE), lambda i: (0, 0))
    tile = pl.BlockSpec((tT, LANE), lambda i: (i, 0))
    return pl.pallas_call(
        body, name="rope_tables", grid=(T // tT,),
        in_specs=[pl.BlockSpec((tT, 1), lambda i: (i, 0)), row, row],
        out_specs=[tile, tile],
        out_shape=[jax.ShapeDtypeStruct((T, LANE), F32)] * 2,
        compiler_params=_cp(("parallel",)),
    )(pos_col, inv_row, sign_row)


def _norm1(x, norm_w):
    T = x.shape[0]
    tT = min(T, 256)

    def body(x_ref, w_ref, h_ref):
        xv = x_ref[...]
        r = lax.rsqrt(jnp.mean(xv * xv, axis=-1, keepdims=True) + EPS)
        h_ref[...] = ((xv * r) * w_ref[...]).astype(h_ref.dtype)

    return pl.pallas_call(
        body, name="norm1", grid=(T // tT,),
        in_specs=[pl.BlockSpec((tT, D), lambda i: (i, 0)), pl.BlockSpec((1, D), lambda i: (0, 0))],
        out_specs=pl.BlockSpec((tT, D), lambda i: (i, 0)),
        out_shape=jax.ShapeDtypeStruct((T, D), MXU),
        compiler_params=_cp(("parallel",)),
    )(x, norm_w)


def _proj(h, wf):
    T = h.shape[0]
    tT, tN = min(T, 512), 512

    def body(h_ref, w_ref, o_ref):
        o_ref[...] = _dot(h_ref[...], w_ref[...])

    return pl.pallas_call(
        body, name="proj", grid=(T // tT, NF // tN),
        in_specs=[pl.BlockSpec((tT, D), lambda i, j: (i, 0)), pl.BlockSpec((D, tN), lambda i, j: (0, j))],
        out_specs=pl.BlockSpec((tT, tN), lambda i, j: (i, j)),
        out_shape=jax.ShapeDtypeStruct((T, NF), F32),
        compiler_params=_cp(("parallel", "parallel")),
    )(h, wf)


def _swa_masks():
    lane = lax.broadcasted_iota(jnp.int32, (BLK, LANE), 1)
    rope_sub0 = ((lane // 32) % 2) == 0
    std_sub0 = lane < 64
    return lane, rope_sub0, std_sub0


def _swa_valid(n):
    qi = lax.broadcasted_iota(jnp.int32, (BLK, 2 * BLK), 0)
    ki = lax.broadcasted_iota(jnp.int32, (BLK, 2 * BLK), 1)
    rel = qi + BLK - ki
    return (rel >= 0) & (rel < BLK) & ((n > 0) | (ki >= BLK))


def _swa_keys(kc_ref, kp_ref, vc_ref, vp_ref, cq, sq, cp, sp):
    def ropek(kref, c, s):
        kv = kref[...]
        return jnp.concatenate([_rope(kv[:, :LANE], c, s), _rope(kv[:, LANE:], c, s)], axis=1)

    K = jnp.concatenate([ropek(kp_ref, cp, sp), ropek(kc_ref, cq, sq)], axis=0).astype(MXU)
    V = jnp.concatenate([vp_ref[...], vc_ref[...]], axis=0).astype(MXU)
    return K, V


def _swa_in_specs(nb, last):
    def cur(n):
        return jnp.minimum(n, last)

    def prev(n):
        return jnp.maximum(cur(n) - 1, 0)

    kd, vd = C_KD // 256, C_VD // 256
    return [
        pl.BlockSpec((BLK, D), lambda n: (cur(n), C_Q // D)),
        pl.BlockSpec((BLK, 256), lambda n: (cur(n), kd)),
        pl.BlockSpec((BLK, 256), lambda n: (prev(n), kd)),
        pl.BlockSpec((BLK, 256), lambda n: (cur(n), vd)),
        pl.BlockSpec((BLK, 256), lambda n: (prev(n), vd)),
        pl.BlockSpec((BLK, LANE), lambda n: (cur(n), 0)),
        pl.BlockSpec((BLK, LANE), lambda n: (cur(n), 0)),
        pl.BlockSpec((BLK, LANE), lambda n: (prev(n), 0)),
        pl.BlockSpec((BLK, LANE), lambda n: (prev(n), 0)),
    ]


def _swa_fwd(proj, cos, sin, sinks):
    T = proj.shape[0]
    nb = T // BLK
    scale = A_HD ** -0.5

    def body(sinks_ref, q_ref, kc_ref, kp_ref, vc_ref, vp_ref, cq_ref, sq_ref, cp_ref, sp_ref, o_ref, l_ref):
        n = pl.program_id(0)
        cq, sq = cq_ref[...], sq_ref[...]
        K, V = _swa_keys(kc_ref, kp_ref, vc_ref, vp_ref, cq, sq, cp_ref[...], sp_ref[...])
        valid = _swa_valid(n)
        lane, rope_sub0, std_sub0 = _swa_masks()
        lacc = jnp.zeros((BLK, LANE), F32)
        for pb in range(A_HEADS // 2):
            g = pb // (A_HEADS // 2 // A_KV)
            Kg, Vg = K[:, g * LANE:(g + 1) * LANE], V[:, g * LANE:(g + 1) * LANE]
            qp = _rope(q_ref[:, pb * LANE:(pb + 1) * LANE], cq, sq)
            outs = []
            for sub in range(2):
                head = 2 * pb + sub
                qm = jnp.where(rope_sub0 if sub == 0 else ~rope_sub0, qp, 0.0).astype(MXU)
                s = jnp.where(valid, _dot_nt(qm, Kg) * scale, NEG)
                sink = sinks_ref[0, head]
                m = jnp.maximum(jnp.max(s, axis=1, keepdims=True), sink)
                e = jnp.exp(s - m)
                den = jnp.sum(e, axis=1, keepdims=True) + jnp.exp(sink - m)
                p = e / den
                outs.append(_dot(p.astype(MXU), Vg))
                lacc = jnp.where(lane == head, m + jnp.log(den), lacc)
            o_ref[:, pb * LANE:(pb + 1) * LANE] = jnp.where(std_sub0, outs[0], outs[1])
        l_ref[...] = lacc

    return pl.pallas_call(
        body, name="swa_fwd", grid=(nb,),
        in_specs=[pl.BlockSpec(memory_space=pltpu.SMEM)] + _swa_in_specs(nb, nb - 1),
        out_specs=[pl.BlockSpec((BLK, D), lambda n: (n, 0)), pl.BlockSpec((BLK, LANE), lambda n: (n, 0))],
        out_shape=[jax.ShapeDtypeStruct((T, D), F32), jax.ShapeDtypeStruct((T, LANE), F32)],
        compiler_params=_cp(("parallel",)),
    )(sinks, proj, proj, proj, proj, proj, cos, sin, cos, sin)


def _swa_bwd(proj, cos, sin, sinks, do_a, o_a, lse):
    T = proj.shape[0]
    nb = T // BLK
    scale = A_HD ** -0.5

    def body(sinks_ref, q_ref, kc_ref, kp_ref, vc_ref, vp_ref, cq_ref, sq_ref, cp_ref, sp_ref,
             do_ref, o_ref, l_ref, dq_ref, dk_ref, dv_ref, ds_ref, ck_ref, cv_ref):
        n = pl.program_id(0)

        @pl.when(n == 0)
        def _():
            ck_ref[...] = jnp.zeros_like(ck_ref)
            cv_ref[...] = jnp.zeros_like(cv_ref)
            ds_ref[...] = jnp.zeros_like(ds_ref)

        @pl.when(n < nb)
        def _():
            cq, sq, cp, sp = cq_ref[...], sq_ref[...], cp_ref[...], sp_ref[...]
            K, V = _swa_keys(kc_ref, kp_ref, vc_ref, vp_ref, cq, sq, cp, sp)
            valid = _swa_valid(n)
            lane, rope_sub0, std_sub0 = _swa_masks()
            lane_row = lax.broadcasted_iota(jnp.int32, (1, LANE), 1)
            lse_v = l_ref[...]
            dK = [jnp.zeros((2 * BLK, LANE), F32) for _ in range(A_KV)]
            dV = [jnp.zeros((2 * BLK, LANE), F32) for _ in range(A_KV)]
            dsink = jnp.zeros((1, LANE), F32)
            for pb in range(A_HEADS // 2):
                g = pb // (A_HEADS // 2 // A_KV)
                Kg, Vg = K[:, g * LANE:(g + 1) * LANE], V[:, g * LANE:(g + 1) * LANE]
                qp = _rope(q_ref[:, pb * LANE:(pb + 1) * LANE], cq, sq)
                dop = do_ref[:, pb * LANE:(pb + 1) * LANE]
                op = o_ref[:, pb * LANE:(pb + 1) * LANE]
                dqs = []
                for sub in range(2):
                    head = 2 * pb + sub
                    rmask = rope_sub0 if sub == 0 else ~rope_sub0
                    smask = std_sub0 if sub == 0 else ~std_sub0
                    qm = jnp.where(rmask, qp, 0.0).astype(MXU)
                    lh = jnp.sum(jnp.where(lane == head, lse_v, 0.0), axis=1, keepdims=True)
                    s = _dot_nt(qm, Kg) * scale
                    p = jnp.where(valid, jnp.exp(s - lh), 0.0)
                    dov = jnp.where(smask, dop, 0.0)
                    delta = jnp.sum(dov * op, axis=1, keepdims=True)
                    dovb = dov.astype(MXU)
                    dp = _dot_nt(dovb, Vg)
                    dsc = ((p * (dp - delta)) * scale).astype(MXU)
                    psink = jnp.exp(sinks_ref[0, head] - lh)
                    dsink = jnp.where(lane_row == head, jnp.sum(-psink * delta, axis=0, keepdims=True), dsink)
                    dqs.append(_dot(dsc, Kg))
                    dK[g] = dK[g] + _dot_tn(dsc, qm)
                    dV[g] = dV[g] + _dot_tn(p.astype(MXU), dovb)
                dqp = jnp.where(rope_sub0, dqs[0], dqs[1])
                dq_ref[:, pb * LANE:(pb + 1) * LANE] = _rope_bwd(dqp, cq, sq).astype(dq_ref.dtype)
            dk_prev = jnp.concatenate([_rope_bwd(dK[g][:BLK], cp, sp) for g in range(A_KV)], axis=1)
            dk_cur = jnp.concatenate([_rope_bwd(dK[g][BLK:], cq, sq) for g in range(A_KV)], axis=1)
            dv_prev = jnp.concatenate([dV[g][:BLK] for g in range(A_KV)], axis=1)
            dv_cur = jnp.concatenate([dV[g][BLK:] for g in range(A_KV)], axis=1)
            dk_ref[...] = (ck_ref[...] + dk_prev).astype(dk_ref.dtype)
            dv_ref[...] = (cv_ref[...] + dv_prev).astype(dv_ref.dtype)
            ck_ref[...] = dk_cur
            cv_ref[...] = dv_cur
            ds_ref[...] = ds_ref[...] + jnp.broadcast_to(dsink, ds_ref.shape)

        @pl.when(n == nb)
        def _():
            dk_ref[...] = ck_ref[...].astype(dk_ref.dtype)
            dv_ref[...] = cv_ref[...].astype(dv_ref.dtype)

    last = nb - 1

    def cur(n):
        return jnp.minimum(n, last)

    def out_kv(n):
        return (jnp.maximum(n - 1, 0), 0)

    return pl.pallas_call(
        body, name="swa_bwd", grid=(nb + 1,),
        in_specs=[pl.BlockSpec(memory_space=pltpu.SMEM)] + _swa_in_specs(nb, last) + [
            pl.BlockSpec((BLK, D), lambda n: (cur(n), 0)),
            pl.BlockSpec((BLK, D), lambda n: (cur(n), 0)),
            pl.BlockSpec((BLK, LANE), lambda n: (cur(n), 0)),
        ],
        out_specs=[
            pl.BlockSpec((BLK, D), lambda n: (cur(n), 0)),
            pl.BlockSpec((BLK, 256), out_kv),
            pl.BlockSpec((BLK, 256), out_kv),
            pl.BlockSpec((8, LANE), lambda n: (0, 0)),
        ],
        out_shape=[
            jax.ShapeDtypeStruct((T, D), MXU),
            jax.ShapeDtypeStruct((T, 256), MXU),
            jax.ShapeDtypeStruct((T, 256), MXU),
            jax.ShapeDtypeStruct((8, LANE), F32),
        ],
        scratch_shapes=[pltpu.VMEM((BLK, 256), F32), pltpu.VMEM((BLK, 256), F32)],
        compiler_params=_cp(("arbitrary",)),
    )(sinks, proj, proj, proj, proj, proj, cos, sin, cos, sin, do_a, o_a, lse)


def _gla_gate(bl_ref, gu_ref, bias_ref):
    gk = _dot(bl_ref[...].astype(MXU), gu_ref[...]) + bias_ref[...]
    la = (jnp.minimum(gk, 0.0) - jnp.log(1.0 + jnp.exp(-jnp.abs(gk)))) / TAU
    ri = lax.broadcasted_iota(jnp.int32, (CHUNK, CHUNK), 0)
    ci = lax.broadcasted_iota(jnp.int32, (CHUNK, CHUNK), 1)
    b = _dot_f32(jnp.where(ci <= ri, 1.0, 0.0).astype(F32), la)
    return gk, la, b, ri, ci


def _gla_head(q_ref, k_ref, la, b, h):
    sl = slice(h * B_DK, (h + 1) * B_DK)
    bh = b[:, sl]
    blast = jnp.sum(la[:, sl], axis=0, keepdims=True)
    qc = q_ref[:, sl] * (B_DK ** -0.5)
    kh = k_ref[:, sl]
    eb, enb, esb = jnp.exp(bh), jnp.exp(-bh), jnp.exp(blast - bh)
    return qc * eb, kh * enb, kh * esb, eb, enb, esb, jnp.exp(blast)


def _gla_specs(chunk_of):
    return [
        pl.BlockSpec((CHUNK, 512), lambda i: (chunk_of(i), C_BQ // 512)),
        pl.BlockSpec((CHUNK, 512), lambda i: (chunk_of(i), C_BK // 512)),
        pl.BlockSpec((CHUNK, D), lambda i: (chunk_of(i), C_BV // D)),
        pl.BlockSpec((CHUNK, W_BL), lambda i: (chunk_of(i), C_BL // W_BL)),
        pl.BlockSpec((W_BL, 512), lambda i: (0, 0)),
        pl.BlockSpec((1, 512), lambda i: (0, 0)),
    ]


def _gla_fwd(proj, gu_pad, bias):
    T = proj.shape[0]
    nc = T // CHUNK

    def body(q_ref, k_ref, v_ref, bl_ref, gu_ref, bias_ref, o_ref, st_ref, state_ref):
        @pl.when(pl.program_id(0) == 0)
        def _():
            state_ref[...] = jnp.zeros_like(state_ref)

        _, la, b, ri, ci = _gla_gate(bl_ref, gu_ref, bias_ref)
        st_ref[...] = state_ref[...]
        for h in range(B_HEADS):
            q_e, k_e, k_s, _, _, _, decay = _gla_head(q_ref, k_ref, la, b, h)
            vh = v_ref[:, h * B_DV:(h + 1) * B_DV].astype(MXU)
            rows = slice(h * B_DV, (h + 1) * B_DV)
            q_eb = q_e.astype(MXU)
            att = jnp.where(ci <= ri, _dot_nt(q_eb, k_e.astype(MXU)), 0.0)
            st = state_ref[rows, :]
            o_ref[:, rows] = _dot(att.astype(MXU), vh) + _dot_nt(q_eb, st.astype(MXU))
            state_ref[rows, :] = st * decay + _dot_tn(vh, k_s.astype(MXU))

    return pl.pallas_call(
        body, name="gla_fwd", grid=(nc,),
        in_specs=_gla_specs(lambda i: i),
        out_specs=[pl.BlockSpec((CHUNK, D), lambda i: (i, 0)),
                   pl.BlockSpec((B_HEADS * B_DV, B_DK), lambda i: (i, 0))],
        out_shape=[jax.ShapeDtypeStruct((T, D), F32),
                   jax.ShapeDtypeStruct((nc * B_HEADS * B_DV, B_DK), F32)],
        scratch_shapes=[pltpu.VMEM((B_HEADS * B_DV, B_DK), F32)],
        compiler_params=_cp(("arbitrary",)),
    )(proj, proj, proj, proj, gu_pad, bias)


def _gla_bwd(proj, gu_pad, bias, states, do_b):
    T = proj.shape[0]
    nc = T // CHUNK

    def body(q_ref, k_ref, v_ref, bl_ref, gu_ref, bias_ref, st_ref, do_ref,
             dq_ref, dk_ref, dv_ref, dbl_ref, ggu_ref, gbias_ref, gt_ref):
        @pl.when(pl.program_id(0) == 0)
        def _():
            gt_ref[...] = jnp.zeros_like(gt_ref)
            ggu_ref[...] = jnp.zeros_like(ggu_ref)
            gbias_ref[...] = jnp.zeros_like(gbias_ref)

        gk, la, b, ri, ci = _gla_gate(bl_ref, gu_ref, bias_ref)
        causal = ci <= ri
        upper = jnp.where(ci >= ri, 1.0, 0.0).astype(F32)
        dla_parts = []
        for h in range(B_HEADS):
            q_e, k_e, k_s, eb, enb, esb, decay = _gla_head(q_ref, k_ref, la, b, h)
            rows = slice(h * B_DV, (h + 1) * B_DV)
            sl = slice(h * B_DK, (h + 1) * B_DK)
            vh = v_ref[:, rows].astype(MXU)
            doh = do_ref[:, rows].astype(MXU)
            q_eb, k_eb, k_sb = q_e.astype(MXU), k_e.astype(MXU), k_s.astype(MXU)
            st = st_ref[rows, :]
            gt = gt_ref[rows, :]
            gtb = gt.astype(MXU)
            att = jnp.where(causal, _dot_nt(q_eb, k_eb), 0.0).astype(MXU)
            datt = jnp.where(causal, _dot_nt(doh, vh), 0.0).astype(MXU)
            dq_e = _dot(datt, k_eb) + _dot(doh, st.astype(MXU))
            dk_e = _dot_tn(datt, q_eb)
            dk_s = _dot(vh, gtb)
            dv_ref[:, rows] = (_dot_tn(att, doh) + _dot_nt(k_sb, gtb)).astype(dv_ref.dtype)
            ddecay = jnp.sum(gt * st, axis=0, keepdims=True)
            gt_ref[rows, :] = gt * decay + _dot_tn(doh, q_eb)
            dq_ref[:, sl] = (dq_e * eb * (B_DK ** -0.5)).astype(dq_ref.dtype)
            dk_ref[:, sl] = (dk_e * enb + dk_s * esb).astype(dk_ref.dtype)
            dks_ks = dk_s * k_s
            db = dq_e * q_e - dk_e * k_e - dks_ks
            dblast = jnp.sum(dks_ks, axis=0, keepdims=True) + ddecay * decay
            dla_parts.append(_dot_f32(upper, db) + dblast)
        dla = jnp.concatenate(dla_parts, axis=1)
        dgk = dla * (1.0 / TAU) * _sigmoid(-gk)
        dgkb = dgk.astype(MXU)
        dbl_ref[...] = _dot_nt(dgkb, gu_ref[...]).astype(dbl_ref.dtype)
        ggu_ref[...] = ggu_ref[...] + _dot_tn(bl_ref[...].astype(MXU), dgkb)
        gbias_ref[...] = gbias_ref[...] + jnp.broadcast_to(jnp.sum(dgk, axis=0, keepdims=True), gbias_ref.shape)

    def rev(i):
        return nc - 1 - i

    return pl.pallas_call(
        body, name="gla_bwd", grid=(nc,),
        in_specs=_gla_specs(rev) + [
            pl.BlockSpec((B_HEADS * B_DV, B_DK), lambda i: (rev(i), 0)),
            pl.BlockSpec((CHUNK, D), lambda i: (rev(i), 0)),
        ],
        out_specs=[
            pl.BlockSpec((CHUNK, 512), lambda i: (rev(i), 0)),
            pl.BlockSpec((CHUNK, 512), lambda i: (rev(i), 0)),
            pl.BlockSpec((CHUNK, D), lambda i: (rev(i), 0)),
            pl.BlockSpec((CHUNK, W_BL), lambda i: (rev(i), 0)),
            pl.BlockSpec((W_BL, 512), lambda i: (0, 0)),
            pl.BlockSpec((8, 512), lambda i: (0, 0)),
        ],
        out_shape=[
            jax.ShapeDtypeStruct((T, 512), MXU),
            jax.ShapeDtypeStruct((T, 512), MXU),
            jax.ShapeDtypeStruct((T, D), MXU),
            jax.ShapeDtypeStruct((T, W_BL), MXU),
            jax.ShapeDtypeStruct((W_BL, 512), F32),
            jax.ShapeDtypeStruct((8, 512), F32),
        ],
        scratch_shapes=[pltpu.VMEM((B_HEADS * B_DV, B_DK), F32)],
        compiler_params=_cp(("arbitrary",)),
    )(proj, proj, proj, proj, gu_pad, bias, states, do_b)


def _mid(x, target, proj, o_a, o_b, w_a, w_b, w_out, w_bn4, fnw):
    T = x.shape[0]
    tT = min(T, 128)

    def body(x_ref, t_ref, oa_ref, ag_ref, ob_ref, bg_ref, ma_ref, mb_ref, wa_ref, wb_ref, wo_ref, wbn_ref, fnw_ref,
             dx2_ref, doa_ref, dag_ref, dob_ref, dbg_ref, dma_ref, dmb_ref,
             gwa_ref, gwb_ref, gwo_ref, gfn_ref, gbn_ref, loss_ref):
        @pl.when(pl.program_id(0) == 0)
        def _():
            for r in (gwa_ref, gwb_ref, gwo_ref, gfn_ref, gbn_ref, loss_ref):
                r[...] = jnp.zeros_like(r)

        oa, ag = oa_ref[...], ag_ref[...]
        sg_a = _sigmoid(ag)
        silu_a = ag * sg_a
        oag_b = (oa * silu_a).astype(MXU)
        y_a = _dot(oag_b, wa_ref[...])

        ob, bg = ob_ref[...], bg_ref[...]
        rbs, obhats = [], []
        for h in range(B_HEADS):
            obh = ob[:, h * B_DV:(h + 1) * B_DV]
            rb = lax.rsqrt(jnp.mean(obh * obh, axis=-1, keepdims=True) + EPS)
            rbs.append(rb)
            obhats.append(obh * rb)
        obhat = jnp.concatenate(obhats, axis=1)
        wbn = wbn_ref[...]
        obn = obhat * wbn
        sg_b = _sigmoid(bg)
        silu_b = bg * sg_b
        obg_b = (obn * silu_b).astype(MXU)
        y_b = _dot(obg_b, wb_ref[...])

        sa, sb = _sigmoid(ma_ref[...]), _sigmoid(mb_ref[...])
        mg_b = (sa * y_a + sb * y_b).astype(MXU)
        x2 = x_ref[...] + _dot(mg_b, wo_ref[...])
        r2 = lax.rsqrt(jnp.mean(x2 * x2, axis=-1, keepdims=True) + EPS)
        xh2 = x2 * r2
        fw = fnw_ref[...]
        err = xh2 * fw - t_ref[...]
        tok = jnp.mean(err * err, axis=-1, keepdims=True)
        loss_ref[...] = loss_ref[...] + 0.5 * jnp.sum(tok, axis=0, keepdims=True)

        dy = err * (1.0 / D)
        gfn_ref[...] = gfn_ref[...] + jnp.broadcast_to(jnp.sum(dy * xh2, axis=0, keepdims=True), gfn_ref.shape)
        gy = dy * fw
        dx2 = r2 * (gy - xh2 * jnp.mean(gy * xh2, axis=-1, keepdims=True))
        dx2_ref[...] = dx2
        dx2_b = dx2.astype(MXU)
        dmg = _dot_nt(dx2_b, wo_ref[...])
        gwo_ref[...] = gwo_ref[...] + _dot_tn(mg_b, dx2_b)

        dma_ref[...] = (dmg * y_a * sa * (1.0 - sa)).astype(dma_ref.dtype)
        dmb_ref[...] = (dmg * y_b * sb * (1.0 - sb)).astype(dmb_ref.dtype)
        dya_b = (dmg * sa).astype(MXU)
        dyb_b = (dmg * sb).astype(MXU)
        doag = _dot_nt(dya_b, wa_ref[...])
        gwa_ref[...] = gwa_ref[...] + _dot_tn(oag_b, dya_b)
        dobg = _dot_nt(dyb_b, wb_ref[...])
        gwb_ref[...] = gwb_ref[...] + _dot_tn(obg_b, dyb_b)

        doa_ref[...] = doag * silu_a
        dag_ref[...] = (doag * oa * (sg_a * (1.0 + ag * (1.0 - sg_a)))).astype(dag_ref.dtype)
        dobn = dobg * silu_b
        dbg_ref[...] = (dobg * obn * (sg_b * (1.0 + bg * (1.0 - sg_b)))).astype(dbg_ref.dtype)
        gg = dobn * wbn
        gbn = jnp.zeros((1, B_DV), F32)
        for h in range(B_HEADS):
            sl = slice(h * B_DV, (h + 1) * B_DV)
            gbn = gbn + jnp.sum(dobn[:, sl] * obhats[h], axis=0, keepdims=True)
            ggh = gg[:, sl]
            dob_ref[:, sl] = rbs[h] * (ggh - obhats[h] * jnp.mean(ggh * obhats[h], axis=-1, keepdims=True))
        gbn_ref[...] = gbn_ref[...] + jnp.broadcast_to(gbn, gbn_ref.shape)

    def col(c):
        return pl.BlockSpec((tT, D), lambda i: (i, c // D))

    tile = pl.BlockSpec((tT, D), lambda i: (i, 0))
    row = pl.BlockSpec((1, D), lambda i: (0, 0))
    acc8 = pl.BlockSpec((8, D), lambda i: (0, 0))
    return pl.pallas_call(
        body, name="mid", grid=(T // tT,),
        in_specs=[tile, tile, tile, col(C_AG), tile, col(C_BG), col(C_MA), col(C_MB),
                  _vmem(), _vmem(), _vmem(), row, row],
        out_specs=[tile, tile, tile, tile, tile, tile, tile, _vmem(), _vmem(), _vmem(),
                   acc8, pl.BlockSpec((8, B_DV), lambda i: (0, 0)), pl.BlockSpec((8, LANE), lambda i: (0, 0))],
        out_shape=[
            jax.ShapeDtypeStruct((T, D), F32),
            jax.ShapeDtypeStruct((T, D), F32),
            jax.ShapeDtypeStruct((T, D), MXU),
            jax.ShapeDtypeStruct((T, D), F32),
            jax.ShapeDtypeStruct((T, D), MXU),
            jax.ShapeDtypeStruct((T, D), MXU),
            jax.ShapeDtypeStruct((T, D), MXU),
            jax.ShapeDtypeStruct((D, D), F32),
            jax.ShapeDtypeStruct((D, D), F32),
            jax.ShapeDtypeStruct((D, D), F32),
            jax.ShapeDtypeStruct((8, D), F32),
            jax.ShapeDtypeStruct((8, B_DV), F32),
            jax.ShapeDtypeStruct((8, LANE), F32),
        ],
        compiler_params=_cp(("arbitrary",)),
    )(x, target, o_a, proj, o_b, proj, proj, proj, w_a, w_b, w_out, w_bn4, fnw)


def _gw_piece(ht, dp, idx):
    T, w = dp.shape
    tn = min(w, 256)

    def body(h_ref, dp_ref, o_ref):
        o_ref[...] = _dot(h_ref[...], dp_ref[...])

    return pl.pallas_call(
        body, name=f"gw_in_{idx}", grid=(w // tn,),
        in_specs=[pl.BlockSpec((D, T), lambda j: (0, 0)), pl.BlockSpec((T, tn), lambda j: (0, j))],
        out_specs=pl.BlockSpec((D, tn), lambda j: (0, j)),
        out_shape=jax.ShapeDtypeStruct((D, w), F32),
        compiler_params=_cp(("parallel",)),
    )(ht, dp)


def _dh_norm(pieces, offsets, wf, x, dx2, norm_w):
    T = x.shape[0]
    tT = min(T, 256)
    widths = [p.shape[1] for p in pieces]
    npc = len(pieces)

    def body(*refs):
        dp_refs = refs[:npc]
        wf_ref, x_ref, dx2_ref, nw_ref, gx_ref, gnw_ref = refs[npc:]

        @pl.when(pl.program_id(0) == 0)
        def _():
            gnw_ref[...] = jnp.zeros_like(gnw_ref)

        dh = jnp.zeros((tT, D), F32)
        for dp_ref, off, w in zip(dp_refs, offsets, widths):
            dh = dh + _dot_nt(dp_ref[...], wf_ref[:, off:off + w])
        xv = x_ref[...]
        r = lax.rsqrt(jnp.mean(xv * xv, axis=-1, keepdims=True) + EPS)
        xh = xv * r
        gnw_ref[...] = gnw_ref[...] + jnp.broadcast_to(jnp.sum(dh * xh, axis=0, keepdims=True), gnw_ref.shape)
        g = dh * nw_ref[...]
        gx_ref[...] = r * (g - xh * jnp.mean(g * xh, axis=-1, keepdims=True)) + dx2_ref[...]

    tile = pl.BlockSpec((tT, D), lambda i: (i, 0))
    return pl.pallas_call(
        body, name="dh_norm", grid=(T // tT,),
        in_specs=[pl.BlockSpec((tT, w), lambda i: (i, 0)) for w in widths]
        + [_vmem(), tile, tile, pl.BlockSpec((1, D), lambda i: (0, 0))],
        out_specs=[tile, pl.BlockSpec((8, D), lambda i: (0, 0))],
        out_shape=[jax.ShapeDtypeStruct((T, D), F32), jax.ShapeDtypeStruct((8, D), F32)],
        compiler_params=_cp(("arbitrary",)),
    )(*pieces, wf, x, dx2, norm_w)


def _adamw_math(w, g, m, v):
    m = ADAM_B1 * m + (1.0 - ADAM_B1) * g
    v = ADAM_B2 * v + (1.0 - ADAM_B2) * (g * g)
    m_hat = m / (1.0 - ADAM_B1 ** ADAM_STEP)
    v_hat = v / (1.0 - ADAM_B2 ** ADAM_STEP)
    delta = -ADAM_LR * (m_hat / (jnp.sqrt(v_hat) + ADAM_EPS) + ADAM_WD * w)
    return delta, m, v


def _adamw(w, g, m, v, idx):
    rows, cols = w.shape
    tr = 128 if rows % 128 == 0 else rows

    def body(w_ref, g_ref, m_ref, v_ref, d_ref, nm_ref, nv_ref):
        d, nm, nv = _adamw_math(w_ref[...], g_ref[...], m_ref[...], v_ref[...])
        d_ref[...] = d
        nm_ref[...] = nm
        nv_ref[...] = nv

    blk = pl.BlockSpec((tr, cols), lambda i: (i, 0))
    return pl.pallas_call(
        body, name=f"adamw_{idx}", grid=(rows // tr,),
        in_specs=[blk] * 4, out_specs=[blk] * 3,
        out_shape=[jax.ShapeDtypeStruct((rows, cols), F32)] * 3,
        compiler_params=_cp(("parallel",)),
    )(w, g, m, v)


def _place():
    x, y, c = lax.axis_index("x"), lax.axis_index("y"), lax.axis_index("c")
    return x, y, c


def _gather_blocks(blk):
    rows, cols = blk.shape

    def body(x_ref, out_ref, send_sems, recv_sems, local_sem):
        x, y, c = _place()
        me, sibling = (x, y, c), (x, y, 1 - c)
        chips = [(1 - x, y), (x, 1 - y), (1 - x, 1 - y)]

        def slot(px, py, pc):
            return out_ref.at[4 * px + 2 * py + pc]

        def copy(k, block, to, src=None):
            return pltpu.make_async_remote_copy(
                src_ref=slot(*block) if src is None else src, dst_ref=slot(*block),
                send_sem=send_sems.at[k], recv_sem=recv_sems.at[k], device_id=to, device_id_type=MESH)

        mine = pltpu.make_async_copy(x_ref, slot(*me), local_sem)
        mine.start()
        first = [copy(0, me, sibling, src=x_ref)]
        first += [copy(1 + j, me, (*chip, c), src=x_ref) for j, chip in enumerate(chips)]
        for cp in first:
            cp.start()
        passed = [copy(4 + j, (*chip, c), sibling) for j, chip in enumerate(chips)]
        for j, chip in enumerate(chips):
            copy(1 + j, (*chip, c), me).wait_recv()
            passed[j].start()
        copy(0, sibling, me).wait_recv()
        for j, chip in enumerate(chips):
            copy(4 + j, (*chip, 1 - c), me).wait_recv()
        for cp in first + passed:
            cp.wait_send()
        mine.wait()

    return pl.pallas_call(
        body, name="gather_weights",
        in_specs=[_any()], out_specs=_any(),
        out_shape=jax.ShapeDtypeStruct((NDEV, rows, cols), blk.dtype),
        scratch_shapes=[pltpu.SemaphoreType.DMA((7,)), pltpu.SemaphoreType.DMA((7,)), pltpu.SemaphoreType.DMA],
        compiler_params=pltpu.CompilerParams(has_side_effects=True),
    )(blk)


def _pair_exchange(packed, small):
    def body(p_ref, s_ref, got_ref, smalls_ref, send_sems, recv_sems, s_send, s_recv, local_sem):
        x, y, c = _place()
        sibling = (x, y, 1 - c)
        sends = []
        for chip in range(4):
            sends.append(pltpu.make_async_remote_copy(
                src_ref=p_ref.at[2 * chip + (1 - c)], dst_ref=got_ref.at[chip],
                send_sem=send_sems.at[chip], recv_sem=recv_sems.at[chip], device_id=sibling, device_id_type=MESH))
        for cp in sends:
            cp.start()
        me_slot = 4 * x + 2 * y + c
        mine = pltpu.make_async_copy(s_ref, smalls_ref.at[me_slot], local_sem)
        mine.start()
        small_sends = []
        k = 0
        for dx in range(2):
            for dy in range(2):
                for dc in range(2):
                    if dx == 0 and dy == 0 and dc == 0:
                        continue
                    to = (x ^ dx, y ^ dy, c ^ dc)
                    small_sends.append(pltpu.make_async_remote_copy(
                        src_ref=s_ref, dst_ref=smalls_ref.at[me_slot],
                        send_sem=s_send.at[k], recv_sem=s_recv.at[k], device_id=to, device_id_type=MESH))
                    k += 1
        for cp in small_sends:
            cp.start()
        for cp in sends:
            cp.wait_recv()
        for cp in small_sends:
            cp.wait_recv()
        for cp in sends + small_sends:
            cp.wait_send()
        mine.wait()

    return pl.pallas_call(
        body, name="pair_exchange",
        in_specs=[_any(), _any()], out_specs=[_any(), _any()],
        out_shape=[jax.ShapeDtypeStruct((4, ROWS, D), packed.dtype),
                   jax.ShapeDtypeStruct((NDEV, SMALL_ROWS, D), F32)],
        scratch_shapes=[pltpu.SemaphoreType.DMA((4,)), pltpu.SemaphoreType.DMA((4,)),
                        pltpu.SemaphoreType.DMA((7,)), pltpu.SemaphoreType.DMA((7,)), pltpu.SemaphoreType.DMA],
        compiler_params=pltpu.CompilerParams(has_side_effects=True),
    )(packed, small)


def _pair_sum(packed, got):
    tr = ROWS // 4

    def body(c_ref, p_ref, g_ref, o_ref):
        o_ref[...] = (p_ref[...].astype(F32) + g_ref[...].astype(F32)).astype(o_ref.dtype)

    c_arr = lax.axis_index("c").astype(jnp.int32).reshape((1,))
    return pl.pallas_call(
        body, name="pair_sum",
        grid_spec=pltpu.PrefetchScalarGridSpec(
            num_scalar_prefetch=1, grid=(4, ROWS // tr),
            in_specs=[pl.BlockSpec((None, tr, D), lambda chip, r, c_ref: (2 * chip + c_ref[0], r, 0)),
                      pl.BlockSpec((None, tr, D), lambda chip, r, c_ref: (chip, r, 0))],
            out_specs=pl.BlockSpec((None, tr, D), lambda chip, r, c_ref: (chip, r, 0))),
        out_shape=jax.ShapeDtypeStruct((4, ROWS, D), packed.dtype),
        compiler_params=_cp(("parallel", "parallel")),
    )(c_arr, packed, got)


def _chip_exchange(sums):
    def body(s_ref, got_ref, send_sems, recv_sems):
        x, y, c = _place()
        chips = [(1 - x, y), (x, 1 - y), (1 - x, 1 - y)]
        sends = []
        for j, (px, py) in enumerate(chips):
            sends.append(pltpu.make_async_remote_copy(
                src_ref=s_ref.at[2 * px + py], dst_ref=got_ref.at[j],
                send_sem=send_sems.at[j], recv_sem=recv_sems.at[j], device_id=(px, py, c), device_id_type=MESH))
        for cp in sends:
            cp.start()
        for cp in sends:
            cp.wait_recv()
        for cp in sends:
            cp.wait_send()

    return pl.pallas_call(
        body, name="chip_exchange",
        in_specs=[_any()], out_specs=_any(),
        out_shape=jax.ShapeDtypeStruct((3, ROWS, D), sums.dtype),
        scratch_shapes=[pltpu.SemaphoreType.DMA((3,)), pltpu.SemaphoreType.DMA((3,))],
        compiler_params=pltpu.CompilerParams(has_side_effects=True),
    )(sums)


def _final_sum(sums, got, smalls):
    tr = ROWS // 4

    def body(me_ref, own_ref, got_ref, sm_ref, g_ref, gs_ref):
        acc = own_ref[...].astype(F32)
        for j in range(3):
            acc = acc + got_ref[j].astype(F32)
        g_ref[...] = acc

        @pl.when(pl.program_id(0) == 0)
        def _():
            tot = sm_ref[0]
            for d in range(1, NDEV):
                tot = tot + sm_ref[d]
            gs_ref[...] = tot

    x, y, _ = _place()
    me_chip = (2 * x + y).astype(jnp.int32).reshape((1,))
    return pl.pallas_call(
        body, name="final_sum",
        grid_spec=pltpu.PrefetchScalarGridSpec(
            num_scalar_prefetch=1, grid=(ROWS // tr,),
            in_specs=[pl.BlockSpec((None, tr, D), lambda r, me: (me[0], r, 0)),
                      pl.BlockSpec((3, tr, D), lambda r, me: (0, r, 0)),
                      pl.BlockSpec((NDEV, SMALL_ROWS, D), lambda r, me: (0, 0, 0))],
            out_specs=[pl.BlockSpec((tr, D), lambda r, me: (r, 0)),
                       pl.BlockSpec((SMALL_ROWS, D), lambda r, me: (0, 0))]),
        out_shape=[jax.ShapeDtypeStruct((ROWS, D), F32), jax.ShapeDtypeStruct((SMALL_ROWS, D), F32)],
        compiler_params=_cp(("arbitrary",)),
    )(me_chip, sums, got, smalls)


def _pad_cols(a, cols):
    return jnp.pad(a, ((0, 0), (0, cols - a.shape[1])))


def _pad_rows(a, rows):
    return jnp.pad(a, ((0, rows - a.shape[0]), (0, 0)))


def _pack_block(w_in_s, w_a_s, w_b_s, w_o_s, gu_s, dtype):
    extra = jnp.pad(gu_s, ((0, D - RANK), (0, D - SHARD - 64)))
    return jnp.concatenate([
        jnp.concatenate([w_in_s, extra], axis=1).astype(dtype),
        w_a_s.astype(dtype), w_b_s.astype(dtype), w_o_s.astype(dtype)], axis=0)


def _build_wf(w_in):
    q = w_in[:, 0:1024].reshape(D, 8, 2, 2, 32).transpose(0, 1, 3, 2, 4).reshape(D, 1024)
    k = w_in[:, 1024:1152].reshape(D, 2, 2, 1, 32)
    kd = jnp.broadcast_to(k, (D, 2, 2, 2, 32)).reshape(D, 256)
    v = w_in[:, 1152:1280].reshape(D, 2, 1, 64)
    vd = jnp.broadcast_to(v, (D, 2, 2, 64)).reshape(D, 256)
    ag, bq, bk = w_in[:, 1280:2304], w_in[:, 2304:2816], w_in[:, 2816:3328]
    bv, bg, bl = w_in[:, 3328:4352], w_in[:, 4352:5376], w_in[:, 5376:5392]
    ma, mb = w_in[:, 5392:6416], w_in[:, 6416:7440]
    return jnp.concatenate([q, ag, bv, bg, ma, mb, bq, bk, kd, vd, _pad_cols(bl, NF - C_BL)], axis=1)


def _unbuild_gw(g):
    q = g["q"].reshape(D, 8, 2, 2, 32).transpose(0, 1, 3, 2, 4).reshape(D, 1024)
    k = g["kd"].reshape(D, 2, 2, 2, 32).sum(axis=3).reshape(D, 128)
    v = g["vd"].reshape(D, 2, 2, 64).sum(axis=2).reshape(D, 128)
    return jnp.concatenate([q, k, v, g["ag"], g["bq"], g["bk"], g["bv"], g["bg"], g["bl"][:, :RANK],
                            g["ma"], g["mb"]], axis=1)


def _local_step(x, pos_col, target, norm_w, wf, w_a, w_b, w_out, gu_pad, bias, w_bn, sinks, fnw):
    cos, sin = _rope_tables(pos_col)
    h = _norm1(x, norm_w)
    proj = _proj(h, wf)
    o_a, lse = _swa_fwd(proj, cos, sin, sinks)
    o_b, states = _gla_fwd(proj, gu_pad, bias)
    w_bn4 = jnp.tile(w_bn, (1, B_HEADS))
    (dx2, do_a, d_ag, do_b, d_bg, d_ma, d_mb, g_wa, g_wb, g_wo, g_fn, g_bn, loss) = _mid(
        x, target, proj, o_a, o_b, w_a, w_b, w_out, w_bn4, fnw)
    d_q, d_kd, d_vd, g_sinks = _swa_bwd(proj, cos, sin, sinks, do_a, o_a, lse)
    d_bq, d_bk, d_bv, d_bl, g_gu, g_bias = _gla_bwd(proj, gu_pad, bias, states, do_b)
    names = ["q", "ag", "bv", "bg", "ma", "mb", "bq", "bk", "kd", "vd", "bl"]
    pieces = [d_q, d_ag, d_bv, d_bg, d_ma, d_mb, d_bq, d_bk, d_kd, d_vd, d_bl]
    offsets = [C_Q, C_AG, C_BV, C_BG, C_MA, C_MB, C_BQ, C_BK, C_KD, C_VD, C_BL]
    ht = h.T
    gw = {nm: _gw_piece(ht, dp, i) for i, (nm, dp) in enumerate(zip(names, pieces))}
    grad_x, g_nw = _dh_norm(pieces, offsets, wf, x, dx2, norm_w)
    grads = dict(w_in=_unbuild_gw(gw), w_a=g_wa, w_b=g_wb, w_out=g_wo, gate_up=g_gu[:RANK],
                 norm_w=g_nw[0:1], sinks=g_sinks[0:1, :A_HEADS], bias=g_bias[0:1], bn=g_bn[0:1], fnw=g_fn[0:1])
    return loss[0, 0], grad_x, grads


def kernel(x, positions, norm_w, w_in, a_sinks, b_gate_up, b_gate_bias, b_out_norm_w, w_a_proj, w_b_proj, w_out, final_norm_w, loss_target, m_norm_w, m_w_in, m_a_sinks, m_b_gate_up, m_b_gate_bias, m_b_out_norm_w, m_w_a_proj, m_w_b_proj, m_w_out, m_final_norm_w, v_norm_w, v_w_in, v_a_sinks, v_b_gate_up, v_b_gate_bias, v_b_out_norm_w, v_w_a_proj, v_w_b_proj, v_w_out, v_final_norm_w):
    T = x.shape[1]
    blk = _pack_block(w_in[0], w_a_proj[0], w_b_proj[0], w_out[0], b_gate_up[0], WIRE)
    allw = _gather_blocks(blk)
    w_in_full = allw[:, R_IN:R_IN + D, :SHARD].transpose(1, 0, 2).reshape(D, IN_WIDTH)
    wf = _build_wf(w_in_full)
    w_a = allw[:, R_A:R_A + 128, :].reshape(D, D)
    w_b = allw[:, R_B:R_B + 128, :].reshape(D, D)
    w_o = allw[:, R_O:R_O + 128, :].reshape(D, D)
    gu = allw[:, :RANK, GU_COL:GU_COL + 64].transpose(1, 0, 2).reshape(RANK, 512)
    gu_pad = _pad_rows(gu, W_BL)

    loss, grad_x, g = _local_step(
        x[0], positions.reshape(T, 1), loss_target[0], norm_w, wf, w_a, w_b, w_o, gu_pad,
        b_gate_bias, b_out_norm_w, a_sinks, final_norm_w.reshape(1, D))
    loss = lax.psum(loss, ("x", "y", "c"))

    gin = g["w_in"].reshape(D, NDEV, SHARD).transpose(1, 0, 2)
    ggu = g["gate_up"].reshape(RANK, NDEV, 64).transpose(1, 0, 2)
    extra = jnp.pad(ggu, ((0, 0), (0, D - RANK), (0, D - SHARD - 64)))
    packed = jnp.concatenate([
        jnp.concatenate([gin, extra], axis=2).astype(WIRE),
        g["w_a"].reshape(NDEV, 128, D).astype(WIRE),
        g["w_b"].reshape(NDEV, 128, D).astype(WIRE),
        g["w_out"].reshape(NDEV, 128, D).astype(WIRE),
    ], axis=1)

    def tile8(a):
        a = a.reshape(1, -1)
        return jnp.pad(a, ((0, 7), (0, D - a.shape[1])))

    small = jnp.concatenate([tile8(g["norm_w"]), tile8(g["fnw"]), tile8(g["bias"]), tile8(g["bn"]),
                             tile8(g["sinks"])], axis=0)
    from_sib, smalls = _pair_exchange(packed, small)
    sums = _pair_sum(packed, from_sib)
    from_chips = _chip_exchange(sums)
    gsh, gsm = _final_sum(sums, from_chips, smalls)

    g_w_in = gsh[R_IN:R_IN + D, :SHARD][None]
    g_w_a, g_w_b, g_w_o = gsh[R_A:R_A + 128][None], gsh[R_B:R_B + 128][None], gsh[R_O:R_O + 128][None]
    g_gu = gsh[:RANK, GU_COL:GU_COL + 64]
    g_norm_w, g_fnw = gsm[0:1], gsm[8]
    g_bias, g_bn, g_sinks = gsm[16:17, :512], gsm[24:25, :B_DV], gsm[32:33, :A_HEADS]

    def pack_small(nw, fw, bi, bn, si, gu_s):
        return jnp.concatenate([tile8(nw), tile8(fw), tile8(bi), tile8(bn), tile8(si),
                                _pad_cols(gu_s, D)], axis=0)

    ws = pack_small(norm_w, final_norm_w, b_gate_bias, b_out_norm_w, a_sinks, b_gate_up[0])
    gs = pack_small(g_norm_w, g_fnw, g_bias, g_bn, g_sinks, g_gu)
    ms = pack_small(m_norm_w, m_final_norm_w, m_b_gate_bias, m_b_out_norm_w, m_a_sinks, m_b_gate_up[0])
    vs = pack_small(v_norm_w, v_final_norm_w, v_b_gate_bias, v_b_out_norm_w, v_a_sinks, v_b_gate_up[0])
    small_out = _adamw(ws, gs, ms, vs, "small")

    def unpack_small(a):
        return dict(norm_w=a[0:1], fnw=a[8], bias=a[16:17, :512], bn=a[24:25, :B_DV], sinks=a[32:33, :A_HEADS],
                    gate_up=a[40:56, :64][None])

    sm = [unpack_small(a) for a in small_out]
    big = {
        "w_in": _adamw(w_in[0], g_w_in[0], m_w_in[0], v_w_in[0], "w_in"),
        "w_a": _adamw(w_a_proj[0], g_w_a[0], m_w_a_proj[0], v_w_a_proj[0], "w_a"),
        "w_b": _adamw(w_b_proj[0], g_w_b[0], m_w_b_proj[0], v_w_b_proj[0], "w_b"),
        "w_out": _adamw(w_out[0], g_w_o[0], m_w_out[0], v_w_out[0], "w_out"),
    }
    grads_out = [g_norm_w, g_w_in, g_sinks, g_gu[None], g_bias, g_bn, g_w_a, g_w_b, g_w_o, g_fnw]

    def triple(k):
        return [sm[k]["norm_w"], big["w_in"][k][None], sm[k]["sinks"], sm[k]["gate_up"], sm[k]["bias"], sm[k]["bn"],
                big["w_a"][k][None], big["w_b"][k][None], big["w_out"][k][None], sm[k]["fnw"]]

    return (loss, grad_x[None], *grads_out, *triple(0), *triple(1), *triple(2))
```

```python
import functools

import numpy as np
import jax
import jax.numpy as jnp
from jax import lax
from jax.experimental import pallas as pl
from jax.experimental.pallas import tpu as pltpu

F32 = jnp.float32
MXU = jnp.bfloat16
WIRE = jnp.bfloat16

D = 1024
A_HEADS, A_KV, A_HD = 16, 2, 64
BLK = 128
B_HEADS, B_DK, B_DV = 4, 128, 256
RANK, TAU, CHUNK = 16, 16.0, 64
EPS, NEG = 1e-5, -1e30
ROPE_THETA = 10000.0
IN_WIDTH, NDEV = 7440, 8
SHARD = IN_WIDTH // NDEV
LANE = 128

C_Q, C_AG, C_BV, C_BG, C_MA, C_MB = 0, 1024, 2048, 3072, 4096, 5120
C_BQ, C_BK, C_KD, C_VD, C_BL = 6144, 6656, 7168, 7424, 7680
NF = 8192
W_BL = 128

SHARD_PAD = 944
R_IN, R_A, R_B, R_O, R_GU, ROWS = 0, 944, 1072, 1200, 1328, 1344
SMALL_ROWS = 48

ADAM_LR, ADAM_B1, ADAM_B2, ADAM_EPS, ADAM_WD, ADAM_STEP = 0.001, 0.9, 0.999, 1e-08, 0.01, 10

MESH = pl.DeviceIdType.MESH
VMEM_LIMIT = 56 * 1024 * 1024


def _cp(sem=None, **kw):
    if sem is not None:
        kw["dimension_semantics"] = sem
    return pltpu.CompilerParams(vmem_limit_bytes=VMEM_LIMIT, **kw)


def _dot(a, b):
    return jnp.dot(a, b, preferred_element_type=F32)


def _dot_nt(a, b):
    return lax.dot_general(a, b, (((1,), (1,)), ((), ())), preferred_element_type=F32)


def _dot_tn(a, b):
    return lax.dot_general(a, b, (((0,), (0,)), ((), ())), preferred_element_type=F32)


def _dot_f32(a, b):
    return jnp.dot(a, b, preferred_element_type=F32, precision=lax.Precision.HIGHEST)


def _sigmoid(z):
    return 1.0 / (1.0 + jnp.exp(-z))


def _rope(xp, cos, sin):
    return xp * cos + pltpu.roll(xp, 64, 1) * sin


def _rope_bwd(dy, cos, sin):
    return dy * cos - pltpu.roll(dy, 64, 1) * sin


def _vmem():
    return pl.BlockSpec(memory_space=pltpu.VMEM)


def _any():
    return pl.BlockSpec(memory_space=pl.ANY)


def _rope_tables(pos_col):
    T = pos_col.shape[0]
    tT = min(T, 512)
    half = A_HD // 2
    inv = (np.float32(ROPE_THETA) ** (-np.arange(half, dtype=np.float32) / np.float32(half))).astype(np.float32)
    inv_row = jnp.asarray(np.tile(inv, 4)[None, :])
    sign_row = jnp.asarray(np.concatenate([-np.ones(64, np.float32), np.ones(64, np.float32)])[None, :])

    def body(pos_ref, inv_ref, sign_ref, cos_ref, sin_ref):
        ang = pos_ref[...].astype(F32) * inv_ref[...]
        cos_ref[...] = jnp.cos(ang)
        sin_ref[...] = jnp.sin(ang) * sign_ref[...]

    row = pl.BlockSpec((1, LANE), lambda i: (0, 0))
    tile = pl.BlockSpec((tT, LANE), lambda i: (i, 0))
    return pl.pallas_call(
        body, name="rope_tables", grid=(T // tT,),
        in_specs=[pl.BlockSpec((tT, 1), lambda i: (i, 0)), row, row],
        out_specs=[tile, tile],
        out_shape=[jax.ShapeDtypeStruct((T, LANE), F32)] * 2,
        compiler_params=_cp(("parallel",)),
    )(pos_col, inv_row, sign_row)


def _norm1(x, norm_w):
    T = x.shape[0]
    tT = min(T, 256)

    def body(x_ref, w_ref, h_ref):
        xv = x_ref[...]
        r = lax.rsqrt(jnp.mean(xv * xv, axis=-1, keepdims=True) + EPS)
        h_ref[...] = ((xv * r) * w_ref[...]).astype(h_ref.dtype)

    return pl.pallas_call(
        body, name="norm1", grid=(T // tT,),
        in_specs=[pl.BlockSpec((tT, D), lambda i: (i, 0)), pl.BlockSpec((1, D), lambda i: (0, 0))],
        out_specs=pl.BlockSpec((tT, D), lambda i: (i, 0)),
        out_shape=jax.ShapeDtypeStruct((T, D), MXU),
        compiler_params=_cp(("parallel",)),
    )(x, norm_w)


def _proj(h, wft):
    T = h.shape[0]
    tT, tN = min(T, 512), 512

    def body(h_ref, w_ref, o_ref):
        o_ref[...] = _dot_nt(h_ref[...], w_ref[...])

    return pl.pallas_call(
        body, name="proj", grid=(T // tT, NF // tN),
        in_specs=[pl.BlockSpec((tT, D), lambda i, j: (i, 0)), pl.BlockSpec((tN, D), lambda i, j: (j, 0))],
        out_specs=pl.BlockSpec((tT, tN), lambda i, j: (i, j)),
        out_shape=jax.ShapeDtypeStruct((T, NF), F32),
        compiler_params=_cp(("parallel", "parallel")),
    )(h, wft)


def _swa_masks():
    lane = lax.broadcasted_iota(jnp.int32, (BLK, LANE), 1)
    rope_sub0 = ((lane // 32) % 2) == 0
    std_sub0 = lane < 64
    return lane, rope_sub0, std_sub0


def _swa_valid(n):
    qi = lax.broadcasted_iota(jnp.int32, (BLK, 2 * BLK), 0)
    ki = lax.broadcasted_iota(jnp.int32, (BLK, 2 * BLK), 1)
    rel = qi + BLK - ki
    return (rel >= 0) & (rel < BLK) & ((n > 0) | (ki >= BLK))


def _swa_keys(kc_ref, kp_ref, vc_ref, vp_ref, cq, sq, cp, sp):
    def ropek(kref, c, s):
        kv = kref[...]
        return jnp.concatenate([_rope(kv[:, :LANE], c, s), _rope(kv[:, LANE:], c, s)], axis=1)

    K = jnp.concatenate([ropek(kp_ref, cp, sp), ropek(kc_ref, cq, sq)], axis=0).astype(MXU)
    V = jnp.concatenate([vp_ref[...], vc_ref[...]], axis=0).astype(MXU)
    return K, V


def _swa_in_specs(nb, last):
    def cur(n):
        return jnp.minimum(n, last)

    def prev(n):
        return jnp.maximum(cur(n) - 1, 0)

    kd, vd = C_KD // 256, C_VD // 256
    return [
        pl.BlockSpec((BLK, D), lambda n: (cur(n), C_Q // D)),
        pl.BlockSpec((BLK, 256), lambda n: (cur(n), kd)),
        pl.BlockSpec((BLK, 256), lambda n: (prev(n), kd)),
        pl.BlockSpec((BLK, 256), lambda n: (cur(n), vd)),
        pl.BlockSpec((BLK, 256), lambda n: (prev(n), vd)),
        pl.BlockSpec((BLK, LANE), lambda n: (cur(n), 0)),
        pl.BlockSpec((BLK, LANE), lambda n: (cur(n), 0)),
        pl.BlockSpec((BLK, LANE), lambda n: (prev(n), 0)),
        pl.BlockSpec((BLK, LANE), lambda n: (prev(n), 0)),
    ]


def _swa_fwd(proj, cos, sin, sinks):
    T = proj.shape[0]
    nb = T // BLK
    scale = A_HD ** -0.5

    def body(sinks_ref, q_ref, kc_ref, kp_ref, vc_ref, vp_ref, cq_ref, sq_ref, cp_ref, sp_ref, o_ref, l_ref):
        n = pl.program_id(0)
        cq, sq = cq_ref[...], sq_ref[...]
        K, V = _swa_keys(kc_ref, kp_ref, vc_ref, vp_ref, cq, sq, cp_ref[...], sp_ref[...])
        valid = _swa_valid(n)
        lane, rope_sub0, std_sub0 = _swa_masks()
        lacc = jnp.zeros((BLK, LANE), F32)
        for pb in range(A_HEADS // 2):
            g = pb // (A_HEADS // 2 // A_KV)
            Kg, Vg = K[:, g * LANE:(g + 1) * LANE], V[:, g * LANE:(g + 1) * LANE]
            qp = _rope(q_ref[:, pb * LANE:(pb + 1) * LANE], cq, sq)
            outs = []
            for sub in range(2):
                head = 2 * pb + sub
                qm = jnp.where(rope_sub0 if sub == 0 else ~rope_sub0, qp, 0.0).astype(MXU)
                s = jnp.where(valid, _dot_nt(qm, Kg) * scale, NEG)
                sink = sinks_ref[0, head]
                m = jnp.maximum(jnp.max(s, axis=1, keepdims=True), sink)
                e = jnp.exp(s - m)
                den = jnp.sum(e, axis=1, keepdims=True) + jnp.exp(sink - m)
                p = e / den
                outs.append(_dot(p.astype(MXU), Vg))
                lacc = jnp.where(lane == head, m + jnp.log(den), lacc)
            o_ref[:, pb * LANE:(pb + 1) * LANE] = jnp.where(std_sub0, outs[0], outs[1])
        l_ref[...] = lacc

    return pl.pallas_call(
        body, name="swa_fwd", grid=(nb,),
        in_specs=[pl.BlockSpec(memory_space=pltpu.SMEM)] + _swa_in_specs(nb, nb - 1),
        out_specs=[pl.BlockSpec((BLK, D), lambda n: (n, 0)), pl.BlockSpec((BLK, LANE), lambda n: (n, 0))],
        out_shape=[jax.ShapeDtypeStruct((T, D), F32), jax.ShapeDtypeStruct((T, LANE), F32)],
        compiler_params=_cp(("parallel",)),
    )(sinks, proj, proj, proj, proj, proj, cos, sin, cos, sin)


def _swa_bwd(proj, cos, sin, sinks, do_a, o_a, lse):
    T = proj.shape[0]
    nb = T // BLK
    scale = A_HD ** -0.5

    def body(sinks_ref, q_ref, kc_ref, kp_ref, vc_ref, vp_ref, cq_ref, sq_ref, cp_ref, sp_ref,
             do_ref, o_ref, l_ref, dq_ref, dk_ref, dv_ref, ds_ref, ck_ref, cv_ref):
        n = pl.program_id(0)

        @pl.when(n == 0)
        def _():
            ck_ref[...] = jnp.zeros_like(ck_ref)
            cv_ref[...] = jnp.zeros_like(cv_ref)
            ds_ref[...] = jnp.zeros_like(ds_ref)

        @pl.when(n < nb)
        def _():
            cq, sq, cp, sp = cq_ref[...], sq_ref[...], cp_ref[...], sp_ref[...]
            K, V = _swa_keys(kc_ref, kp_ref, vc_ref, vp_ref, cq, sq, cp, sp)
            valid = _swa_valid(n)
            lane, rope_sub0, std_sub0 = _swa_masks()
            lane_row = lax.broadcasted_iota(jnp.int32, (1, LANE), 1)
            lse_v = l_ref[...]
            dK = [jnp.zeros((2 * BLK, LANE), F32) for _ in range(A_KV)]
            dV = [jnp.zeros((2 * BLK, LANE), F32) for _ in range(A_KV)]
            dsink = jnp.zeros((1, LANE), F32)
            for pb in range(A_HEADS // 2):
                g = pb // (A_HEADS // 2 // A_KV)
                Kg, Vg = K[:, g * LANE:(g + 1) * LANE], V[:, g * LANE:(g + 1) * LANE]
                qp = _rope(q_ref[:, pb * LANE:(pb + 1) * LANE], cq, sq)
                dop = do_ref[:, pb * LANE:(pb + 1) * LANE]
                op = o_ref[:, pb * LANE:(pb + 1) * LANE]
                dqs = []
                for sub in range(2):
                    head = 2 * pb + sub
                    rmask = rope_sub0 if sub == 0 else ~rope_sub0
                    smask = std_sub0 if sub == 0 else ~std_sub0
                    qm = jnp.where(rmask, qp, 0.0).astype(MXU)
                    lh = jnp.sum(jnp.where(lane == head, lse_v, 0.0), axis=1, keepdims=True)
                    s = _dot_nt(qm, Kg) * scale
                    p = jnp.where(valid, jnp.exp(s - lh), 0.0)
                    dov = jnp.where(smask, dop, 0.0)
                    delta = jnp.sum(dov * op, axis=1, keepdims=True)
                    dovb = dov.astype(MXU)
                    dp = _dot_nt(dovb, Vg)
                    dsc = ((p * (dp - delta)) * scale).astype(MXU)
                    psink = jnp.exp(sinks_ref[0, head] - lh)
                    dsink = jnp.where(lane_row == head, jnp.sum(-psink * delta, axis=0, keepdims=True), dsink)
                    dqs.append(_dot(dsc, Kg))
                    dK[g] = dK[g] + _dot_tn(dsc, qm)
                    dV[g] = dV[g] + _dot_tn(p.astype(MXU), dovb)
                dqp = jnp.where(rope_sub0, dqs[0], dqs[1])
                dq_ref[:, pb * LANE:(pb + 1) * LANE] = _rope_bwd(dqp, cq, sq).astype(dq_ref.dtype)
            dk_prev = jnp.concatenate([_rope_bwd(dK[g][:BLK], cp, sp) for g in range(A_KV)], axis=1)
            dk_cur = jnp.concatenate([_rope_bwd(dK[g][BLK:], cq, sq) for g in range(A_KV)], axis=1)
            dv_prev = jnp.concatenate([dV[g][:BLK] for g in range(A_KV)], axis=1)
            dv_cur = jnp.concatenate([dV[g][BLK:] for g in range(A_KV)], axis=1)
            dk_ref[...] = (ck_ref[...] + dk_prev).astype(dk_ref.dtype)
            dv_ref[...] = (cv_ref[...] + dv_prev).astype(dv_ref.dtype)
            ck_ref[...] = dk_cur
            cv_ref[...] = dv_cur
            ds_ref[...] = ds_ref[...] + jnp.broadcast_to(dsink, ds_ref.shape)

        @pl.when(n == nb)
        def _():
            dk_ref[...] = ck_ref[...].astype(dk_ref.dtype)
            dv_ref[...] = cv_ref[...].astype(dv_ref.dtype)

    last = nb - 1

    def cur(n):
        return jnp.minimum(n, last)

    def out_kv(n):
        return (jnp.maximum(n - 1, 0), 0)

    return pl.pallas_call(
        body, name="swa_bwd", grid=(nb + 1,),
        in_specs=[pl.BlockSpec(memory_space=pltpu.SMEM)] + _swa_in_specs(nb, last) + [
            pl.BlockSpec((BLK, D), lambda n: (cur(n), 0)),
            pl.BlockSpec((BLK, D), lambda n: (cur(n), 0)),
            pl.BlockSpec((BLK, LANE), lambda n: (cur(n), 0)),
        ],
        out_specs=[
            pl.BlockSpec((BLK, D), lambda n: (cur(n), 0)),
            pl.BlockSpec((BLK, 256), out_kv),
            pl.BlockSpec((BLK, 256), out_kv),
            pl.BlockSpec((8, LANE), lambda n: (0, 0)),
        ],
        out_shape=[
            jax.ShapeDtypeStruct((T, D), MXU),
            jax.ShapeDtypeStruct((T, 256), MXU),
            jax.ShapeDtypeStruct((T, 256), MXU),
            jax.ShapeDtypeStruct((8, LANE), F32),
        ],
        scratch_shapes=[pltpu.VMEM((BLK, 256), F32), pltpu.VMEM((BLK, 256), F32)],
        compiler_params=_cp(("arbitrary",)),
    )(sinks, proj, proj, proj, proj, proj, cos, sin, cos, sin, do_a, o_a, lse)


def _gla_gate(bl_ref, gu_ref, bias_ref):
    gk = _dot(bl_ref[...].astype(MXU), gu_ref[...]) + bias_ref[...]
    la = (jnp.minimum(gk, 0.0) - jnp.log(1.0 + jnp.exp(-jnp.abs(gk)))) / TAU
    ri = lax.broadcasted_iota(jnp.int32, (CHUNK, CHUNK), 0)
    ci = lax.broadcasted_iota(jnp.int32, (CHUNK, CHUNK), 1)
    b = _dot_f32(jnp.where(ci <= ri, 1.0, 0.0).astype(F32), la)
    return gk, la, b, ri, ci


def _gla_head(q_ref, k_ref, la, b, h):
    sl = slice(h * B_DK, (h + 1) * B_DK)
    bh = b[:, sl]
    blast = jnp.sum(la[:, sl], axis=0, keepdims=True)
    qc = q_ref[:, sl] * (B_DK ** -0.5)
    kh = k_ref[:, sl]
    eb, enb, esb = jnp.exp(bh), jnp.exp(-bh), jnp.exp(blast - bh)
    return qc * eb, kh * enb, kh * esb, eb, enb, esb, jnp.exp(blast)


def _gla_specs(chunk_of):
    return [
        pl.BlockSpec((CHUNK, 512), lambda i: (chunk_of(i), C_BQ // 512)),
        pl.BlockSpec((CHUNK, 512), lambda i: (chunk_of(i), C_BK // 512)),
        pl.BlockSpec((CHUNK, D), lambda i: (chunk_of(i), C_BV // D)),
        pl.BlockSpec((CHUNK, W_BL), lambda i: (chunk_of(i), C_BL // W_BL)),
        pl.BlockSpec((W_BL, 512), lambda i: (0, 0)),
        pl.BlockSpec((1, 512), lambda i: (0, 0)),
    ]


def _gla_fwd(proj, gu_pad, bias):
    T = proj.shape[0]
    nc = T // CHUNK

    def body(q_ref, k_ref, v_ref, bl_ref, gu_ref, bias_ref, o_ref, st_ref, state_ref):
        @pl.when(pl.program_id(0) == 0)
        def _():
            state_ref[...] = jnp.zeros_like(state_ref)

        _, la, b, ri, ci = _gla_gate(bl_ref, gu_ref, bias_ref)
        st_ref[...] = state_ref[...]
        for h in range(B_HEADS):
            q_e, k_e, k_s, _, _, _, decay = _gla_head(q_ref, k_ref, la, b, h)
            vh = v_ref[:, h * B_DV:(h + 1) * B_DV].astype(MXU)
            rows = slice(h * B_DV, (h + 1) * B_DV)
            q_eb = q_e.astype(MXU)
            att = jnp.where(ci <= ri, _dot_nt(q_eb, k_e.astype(MXU)), 0.0)
            st = state_ref[rows, :]
            o_ref[:, rows] = _dot(att.astype(MXU), vh) + _dot_nt(q_eb, st.astype(MXU))
            state_ref[rows, :] = st * decay + _dot_tn(vh, k_s.astype(MXU))

    return pl.pallas_call(
        body, name="gla_fwd", grid=(nc,),
        in_specs=_gla_specs(lambda i: i),
        out_specs=[pl.BlockSpec((CHUNK, D), lambda i: (i, 0)),
                   pl.BlockSpec((B_HEADS * B_DV, B_DK), lambda i: (i, 0))],
        out_shape=[jax.ShapeDtypeStruct((T, D), F32),
                   jax.ShapeDtypeStruct((nc * B_HEADS * B_DV, B_DK), F32)],
        scratch_shapes=[pltpu.VMEM((B_HEADS * B_DV, B_DK), F32)],
        compiler_params=_cp(("arbitrary",)),
    )(proj, proj, proj, proj, gu_pad, bias)


def _gla_bwd(proj, gu_pad, bias, states, do_b):
    T = proj.shape[0]
    nc = T // CHUNK

    def body(q_ref, k_ref, v_ref, bl_ref, gu_ref, bias_ref, st_ref, do_ref,
             dq_ref, dk_ref, dv_ref, dbl_ref, ggu_ref, gbias_ref, gt_ref):
        @pl.when(pl.program_id(0) == 0)
        def _():
            gt_ref[...] = jnp.zeros_like(gt_ref)
            ggu_ref[...] = jnp.zeros_like(ggu_ref)
            gbias_ref[...] = jnp.zeros_like(gbias_ref)

        gk, la, b, ri, ci = _gla_gate(bl_ref, gu_ref, bias_ref)
        causal = ci <= ri
        upper = jnp.where(ci >= ri, 1.0, 0.0).astype(F32)
        dla_parts = []
        for h in range(B_HEADS):
            q_e, k_e, k_s, eb, enb, esb, decay = _gla_head(q_ref, k_ref, la, b, h)
            rows = slice(h * B_DV, (h + 1) * B_DV)
            sl = slice(h * B_DK, (h + 1) * B_DK)
            vh = v_ref[:, rows].astype(MXU)
            doh = do_ref[:, rows].astype(MXU)
            q_eb, k_eb, k_sb = q_e.astype(MXU), k_e.astype(MXU), k_s.astype(MXU)
            st = st_ref[rows, :]
            gt = gt_ref[rows, :]
            gtb = gt.astype(MXU)
            att = jnp.where(causal, _dot_nt(q_eb, k_eb), 0.0).astype(MXU)
            datt = jnp.where(causal, _dot_nt(doh, vh), 0.0).astype(MXU)
            dq_e = _dot(datt, k_eb) + _dot(doh, st.astype(MXU))
            dk_e = _dot_tn(datt, q_eb)
            dk_s = _dot(vh, gtb)
            dv_ref[:, rows] = (_dot_tn(att, doh) + _dot_nt(k_sb, gtb)).astype(dv_ref.dtype)
            ddecay = jnp.sum(gt * st, axis=0, keepdims=True)
            gt_ref[rows, :] = gt * decay + _dot_tn(doh, q_eb)
            dq_ref[:, sl] = (dq_e * eb * (B_DK ** -0.5)).astype(dq_ref.dtype)
            dk_ref[:, sl] = (dk_e * enb + dk_s * esb).astype(dk_ref.dtype)
            dks_ks = dk_s * k_s
            db = dq_e * q_e - dk_e * k_e - dks_ks
            dblast = jnp.sum(dks_ks, axis=0, keepdims=True) + ddecay * decay
            dla_parts.append(_dot_f32(upper, db) + dblast)
        dla = jnp.concatenate(dla_parts, axis=1)
        dgk = dla * (1.0 / TAU) * _sigmoid(-gk)
        dgkb = dgk.astype(MXU)
        dbl_ref[...] = _dot_nt(dgkb, gu_ref[...]).astype(dbl_ref.dtype)
        ggu_ref[...] = ggu_ref[...] + _dot_tn(bl_ref[...].astype(MXU), dgkb)
        gbias_ref[...] = gbias_ref[...] + jnp.broadcast_to(jnp.sum(dgk, axis=0, keepdims=True), gbias_ref.shape)

    def rev(i):
        return nc - 1 - i

    return pl.pallas_call(
        body, name="gla_bwd", grid=(nc,),
        in_specs=_gla_specs(rev) + [
            pl.BlockSpec((B_HEADS * B_DV, B_DK), lambda i: (rev(i), 0)),
            pl.BlockSpec((CHUNK, D), lambda i: (rev(i), 0)),
        ],
        out_specs=[
            pl.BlockSpec((CHUNK, 512), lambda i: (rev(i), 0)),
            pl.BlockSpec((CHUNK, 512), lambda i: (rev(i), 0)),
            pl.BlockSpec((CHUNK, D), lambda i: (rev(i), 0)),
            pl.BlockSpec((CHUNK, W_BL), lambda i: (rev(i), 0)),
            pl.BlockSpec((W_BL, 512), lambda i: (0, 0)),
            pl.BlockSpec((8, 512), lambda i: (0, 0)),
        ],
        out_shape=[
            jax.ShapeDtypeStruct((T, 512), MXU),
            jax.ShapeDtypeStruct((T, 512), MXU),
            jax.ShapeDtypeStruct((T, D), MXU),
            jax.ShapeDtypeStruct((T, W_BL), MXU),
            jax.ShapeDtypeStruct((W_BL, 512), F32),
            jax.ShapeDtypeStruct((8, 512), F32),
        ],
        scratch_shapes=[pltpu.VMEM((B_HEADS * B_DV, B_DK), F32)],
        compiler_params=_cp(("arbitrary",)),
    )(proj, proj, proj, proj, gu_pad, bias, states, do_b)


def _mid(x, target, proj, o_a, o_b, w_a, w_b, w_out, w_bn4, fnw):
    T = x.shape[0]
    tT = min(T, 128)

    def body(x_ref, t_ref, oa_ref, ag_ref, ob_ref, bg_ref, ma_ref, mb_ref, wa_ref, wb_ref, wo_ref, wbn_ref, fnw_ref,
             dx2_ref, doa_ref, dag_ref, dob_ref, dbg_ref, dma_ref, dmb_ref,
             gwa_ref, gwb_ref, gwo_ref, gfn_ref, gbn_ref, loss_ref):
        @pl.when(pl.program_id(0) == 0)
        def _():
            for r in (gwa_ref, gwb_ref, gwo_ref, gfn_ref, gbn_ref, loss_ref):
                r[...] = jnp.zeros_like(r)

        oa, ag = oa_ref[...], ag_ref[...]
        sg_a = _sigmoid(ag)
        silu_a = ag * sg_a
        oag_b = (oa * silu_a).astype(MXU)
        y_a = _dot(oag_b, wa_ref[...])

        ob, bg = ob_ref[...], bg_ref[...]
        rbs, obhats = [], []
        for h in range(B_HEADS):
            obh = ob[:, h * B_DV:(h + 1) * B_DV]
            rb = lax.rsqrt(jnp.mean(obh * obh, axis=-1, keepdims=True) + EPS)
            rbs.append(rb)
            obhats.append(obh * rb)
        obhat = jnp.concatenate(obhats, axis=1)
        wbn = wbn_ref[...]
        obn = obhat * wbn
        sg_b = _sigmoid(bg)
        silu_b = bg * sg_b
        obg_b = (obn * silu_b).astype(MXU)
        y_b = _dot(obg_b, wb_ref[...])

        sa, sb = _sigmoid(ma_ref[...]), _sigmoid(mb_ref[...])
        mg_b = (sa * y_a + sb * y_b).astype(MXU)
        x2 = x_ref[...] + _dot(mg_b, wo_ref[...])
        r2 = lax.rsqrt(jnp.mean(x2 * x2, axis=-1, keepdims=True) + EPS)
        xh2 = x2 * r2
        fw = fnw_ref[...]
        err = xh2 * fw - t_ref[...]
        tok = jnp.mean(err * err, axis=-1, keepdims=True)
        loss_ref[...] = loss_ref[...] + 0.5 * jnp.sum(tok, axis=0, keepdims=True)

        dy = err * (1.0 / D)
        gfn_ref[...] = gfn_ref[...] + jnp.broadcast_to(jnp.sum(dy * xh2, axis=0, keepdims=True), gfn_ref.shape)
        gy = dy * fw
        dx2 = r2 * (gy - xh2 * jnp.mean(gy * xh2, axis=-1, keepdims=True))
        dx2_ref[...] = dx2
        dx2_b = dx2.astype(MXU)
        dmg = _dot_nt(dx2_b, wo_ref[...])
        gwo_ref[...] = gwo_ref[...] + _dot_tn(mg_b, dx2_b)

        dma_ref[...] = (dmg * y_a * sa * (1.0 - sa)).astype(dma_ref.dtype)
        dmb_ref[...] = (dmg * y_b * sb * (1.0 - sb)).astype(dmb_ref.dtype)
        dya_b = (dmg * sa).astype(MXU)
        dyb_b = (dmg * sb).astype(MXU)
        doag = _dot_nt(dya_b, wa_ref[...])
        gwa_ref[...] = gwa_ref[...] + _dot_tn(oag_b, dya_b)
        dobg = _dot_nt(dyb_b, wb_ref[...])
        gwb_ref[...] = gwb_ref[...] + _dot_tn(obg_b, dyb_b)

        doa_ref[...] = doag * silu_a
        dag_ref[...] = (doag * oa * (sg_a * (1.0 + ag * (1.0 - sg_a)))).astype(dag_ref.dtype)
        dobn = dobg * silu_b
        dbg_ref[...] = (dobg * obn * (sg_b * (1.0 + bg * (1.0 - sg_b)))).astype(dbg_ref.dtype)
        gg = dobn * wbn
        gbn = jnp.zeros((1, B_DV), F32)
        for h in range(B_HEADS):
            sl = slice(h * B_DV, (h + 1) * B_DV)
            gbn = gbn + jnp.sum(dobn[:, sl] * obhats[h], axis=0, keepdims=True)
            ggh = gg[:, sl]
            dob_ref[:, sl] = rbs[h] * (ggh - obhats[h] * jnp.mean(ggh * obhats[h], axis=-1, keepdims=True))
        gbn_ref[...] = gbn_ref[...] + jnp.broadcast_to(gbn, gbn_ref.shape)

    def col(c):
        return pl.BlockSpec((tT, D), lambda i: (i, c // D))

    tile = pl.BlockSpec((tT, D), lambda i: (i, 0))
    row = pl.BlockSpec((1, D), lambda i: (0, 0))
    acc8 = pl.BlockSpec((8, D), lambda i: (0, 0))
    return pl.pallas_call(
        body, name="mid", grid=(T // tT,),
        in_specs=[tile, tile, tile, col(C_AG), tile, col(C_BG), col(C_MA), col(C_MB),
                  _vmem(), _vmem(), _vmem(), row, row],
        out_specs=[tile, tile, tile, tile, tile, tile, tile, _vmem(), _vmem(), _vmem(),
                   acc8, pl.BlockSpec((8, B_DV), lambda i: (0, 0)), pl.BlockSpec((8, LANE), lambda i: (0, 0))],
        out_shape=[
            jax.ShapeDtypeStruct((T, D), F32),
            jax.ShapeDtypeStruct((T, D), F32),
            jax.ShapeDtypeStruct((T, D), MXU),
            jax.ShapeDtypeStruct((T, D), F32),
            jax.ShapeDtypeStruct((T, D), MXU),
            jax.ShapeDtypeStruct((T, D), MXU),
            jax.ShapeDtypeStruct((T, D), MXU),
            jax.ShapeDtypeStruct((D, D), F32),
            jax.ShapeDtypeStruct((D, D), F32),
            jax.ShapeDtypeStruct((D, D), F32),
            jax.ShapeDtypeStruct((8, D), F32),
            jax.ShapeDtypeStruct((8, B_DV), F32),
            jax.ShapeDtypeStruct((8, LANE), F32),
        ],
        compiler_params=_cp(("arbitrary",)),
    )(x, target, o_a, proj, o_b, proj, proj, proj, w_a, w_b, w_out, w_bn4, fnw)


def _gw_piece(h, dp, idx):
    T, w = dp.shape
    tn = min(w, 256)

    def body(h_ref, dp_ref, o_ref):
        o_ref[...] = _dot_tn(dp_ref[...], h_ref[...])

    return pl.pallas_call(
        body, name=f"gw_in_{idx}", grid=(w // tn,),
        in_specs=[pl.BlockSpec((T, D), lambda j: (0, 0)), pl.BlockSpec((T, tn), lambda j: (0, j))],
        out_specs=pl.BlockSpec((tn, D), lambda j: (j, 0)),
        out_shape=jax.ShapeDtypeStruct((w, D), F32),
        compiler_params=_cp(("parallel",)),
    )(h, dp)


def _dh_norm(pieces, offsets, wf, x, dx2, norm_w):
    T = x.shape[0]
    tT = min(T, 256)
    widths = [p.shape[1] for p in pieces]
    npc = len(pieces)

    def body(*refs):
        dp_refs = refs[:npc]
        wf_ref, x_ref, dx2_ref, nw_ref, gx_ref, gnw_ref = refs[npc:]

        @pl.when(pl.program_id(0) == 0)
        def _():
            gnw_ref[...] = jnp.zeros_like(gnw_ref)

        dh = jnp.zeros((tT, D), F32)
        for dp_ref, off, w in zip(dp_refs, offsets, widths):
            dh = dh + _dot(dp_ref[...], wf_ref[off:off + w, :])
        xv = x_ref[...]
        r = lax.rsqrt(jnp.mean(xv * xv, axis=-1, keepdims=True) + EPS)
        xh = xv * r
        gnw_ref[...] = gnw_ref[...] + jnp.broadcast_to(jnp.sum(dh * xh, axis=0, keepdims=True), gnw_ref.shape)
        g = dh * nw_ref[...]
        gx_ref[...] = r * (g - xh * jnp.mean(g * xh, axis=-1, keepdims=True)) + dx2_ref[...]

    tile = pl.BlockSpec((tT, D), lambda i: (i, 0))
    return pl.pallas_call(
        body, name="dh_norm", grid=(T // tT,),
        in_specs=[pl.BlockSpec((tT, w), lambda i: (i, 0)) for w in widths]
        + [_vmem(), tile, tile, pl.BlockSpec((1, D), lambda i: (0, 0))],
        out_specs=[tile, pl.BlockSpec((8, D), lambda i: (0, 0))],
        out_shape=[jax.ShapeDtypeStruct((T, D), F32), jax.ShapeDtypeStruct((8, D), F32)],
        compiler_params=_cp(("arbitrary",)),
    )(*pieces, wf, x, dx2, norm_w)


def _adamw_math(w, g, m, v):
    m = ADAM_B1 * m + (1.0 - ADAM_B1) * g
    v = ADAM_B2 * v + (1.0 - ADAM_B2) * (g * g)
    m_hat = m / (1.0 - ADAM_B1 ** ADAM_STEP)
    v_hat = v / (1.0 - ADAM_B2 ** ADAM_STEP)
    delta = -ADAM_LR * (m_hat / (jnp.sqrt(v_hat) + ADAM_EPS) + ADAM_WD * w)
    return delta, m, v


def _adamw(w, g, m, v, idx):
    rows, cols = w.shape
    tr = next((t for t in (1488, 1024, 512, 256, 128) if rows % t == 0), rows)

    def body(w_ref, g_ref, m_ref, v_ref, d_ref, nm_ref, nv_ref):
        d, nm, nv = _adamw_math(w_ref[...], g_ref[...], m_ref[...], v_ref[...])
        d_ref[...] = d
        nm_ref[...] = nm
        nv_ref[...] = nv

    blk = pl.BlockSpec((tr, cols), lambda i: (i, 0))
    return pl.pallas_call(
        body, name=f"adamw_{idx}", grid=(rows // tr,),
        in_specs=[blk] * 4, out_specs=[blk] * 3,
        out_shape=[jax.ShapeDtypeStruct((rows, cols), F32)] * 3,
        compiler_params=_cp(("parallel",)),
    )(w, g, m, v)


def _place():
    x, y, c = lax.axis_index("x"), lax.axis_index("y"), lax.axis_index("c")
    return x, y, c


def _gather_blocks(blk):
    rows, cols = blk.shape

    def body(x_ref, out_ref, send_sems, recv_sems, local_sem):
        x, y, c = _place()
        me, sibling = (x, y, c), (x, y, 1 - c)
        chips = [(1 - x, y), (x, 1 - y), (1 - x, 1 - y)]

        def slot(px, py, pc):
            return out_ref.at[4 * px + 2 * py + pc]

        def copy(k, block, to, src=None):
            return pltpu.make_async_remote_copy(
                src_ref=slot(*block) if src is None else src, dst_ref=slot(*block),
                send_sem=send_sems.at[k], recv_sem=recv_sems.at[k], device_id=to, device_id_type=MESH)

        mine = pltpu.make_async_copy(x_ref, slot(*me), local_sem)
        mine.start()
        first = [copy(0, me, sibling, src=x_ref)]
        first += [copy(1 + j, me, (*chip, c), src=x_ref) for j, chip in enumerate(chips)]
        for cp in first:
            cp.start()
        passed = [copy(4 + j, (*chip, c), sibling) for j, chip in enumerate(chips)]
        for j, chip in enumerate(chips):
            copy(1 + j, (*chip, c), me).wait_recv()
            passed[j].start()
        copy(0, sibling, me).wait_recv()
        for j, chip in enumerate(chips):
            copy(4 + j, (*chip, 1 - c), me).wait_recv()
        for cp in first + passed:
            cp.wait_send()
        mine.wait()

    return pl.pallas_call(
        body, name="gather_weights",
        in_specs=[_any()], out_specs=_any(),
        out_shape=jax.ShapeDtypeStruct((NDEV, rows, cols), blk.dtype),
        scratch_shapes=[pltpu.SemaphoreType.DMA((7,)), pltpu.SemaphoreType.DMA((7,)), pltpu.SemaphoreType.DMA],
        compiler_params=pltpu.CompilerParams(has_side_effects=True),
    )(blk)


def _pair_exchange(packed, small):
    def body(p_ref, s_ref, got_ref, smalls_ref, send_sems, recv_sems, s_send, s_recv, local_sem):
        x, y, c = _place()
        sibling = (x, y, 1 - c)
        sends = []
        for chip in range(4):
            sends.append(pltpu.make_async_remote_copy(
                src_ref=p_ref.at[2 * chip + (1 - c)], dst_ref=got_ref.at[chip],
                send_sem=send_sems.at[chip], recv_sem=recv_sems.at[chip], device_id=sibling, device_id_type=MESH))
        for cp in sends:
            cp.start()
        me_slot = 4 * x + 2 * y + c
        mine = pltpu.make_async_copy(s_ref, smalls_ref.at[me_slot], local_sem)
        mine.start()
        small_sends = []
        k = 0
        for dx in range(2):
            for dy in range(2):
                for dc in range(2):
                    if dx == 0 and dy == 0 and dc == 0:
                        continue
                    to = (x ^ dx, y ^ dy, c ^ dc)
                    small_sends.append(pltpu.make_async_remote_copy(
                        src_ref=s_ref, dst_ref=smalls_ref.at[me_slot],
                        send_sem=s_send.at[k], recv_sem=s_recv.at[k], device_id=to, device_id_type=MESH))
                    k += 1
        for cp in small_sends:
            cp.start()
        for cp in sends:
            cp.wait_recv()
        for cp in small_sends:
            cp.wait_recv()
        for cp in sends + small_sends:
            cp.wait_send()
        mine.wait()

    return pl.pallas_call(
        body, name="pair_exchange",
        in_specs=[_any(), _any()], out_specs=[_any(), _any()],
        out_shape=[jax.ShapeDtypeStruct((4, ROWS, D), packed.dtype),
                   jax.ShapeDtypeStruct((NDEV, SMALL_ROWS, D), F32)],
        scratch_shapes=[pltpu.SemaphoreType.DMA((4,)), pltpu.SemaphoreType.DMA((4,)),
                        pltpu.SemaphoreType.DMA((7,)), pltpu.SemaphoreType.DMA((7,)), pltpu.SemaphoreType.DMA],
        compiler_params=pltpu.CompilerParams(has_side_effects=True),
    )(packed, small)


def _pair_sum(packed, got):
    tr = ROWS // 4

    def body(c_ref, p_ref, g_ref, o_ref):
        o_ref[...] = (p_ref[...].astype(F32) + g_ref[...].astype(F32)).astype(o_ref.dtype)

    c_arr = lax.axis_index("c").astype(jnp.int32).reshape((1,))
    return pl.pallas_call(
        body, name="pair_sum",
        grid_spec=pltpu.PrefetchScalarGridSpec(
            num_scalar_prefetch=1, grid=(4, ROWS // tr),
            in_specs=[pl.BlockSpec((None, tr, D), lambda chip, r, c_ref: (2 * chip + c_ref[0], r, 0)),
                      pl.BlockSpec((None, tr, D), lambda chip, r, c_ref: (chip, r, 0))],
            out_specs=pl.BlockSpec((None, tr, D), lambda chip, r, c_ref: (chip, r, 0))),
        out_shape=jax.ShapeDtypeStruct((4, ROWS, D), packed.dtype),
        compiler_params=_cp(("parallel", "parallel")),
    )(c_arr, packed, got)


def _chip_exchange(sums):
    def body(s_ref, got_ref, send_sems, recv_sems):
        x, y, c = _place()
        chips = [(1 - x, y), (x, 1 - y), (1 - x, 1 - y)]
        sends = []
        for j, (px, py) in enumerate(chips):
            sends.append(pltpu.make_async_remote_copy(
                src_ref=s_ref.at[2 * px + py], dst_ref=got_ref.at[j],
                send_sem=send_sems.at[j], recv_sem=recv_sems.at[j], device_id=(px, py, c), device_id_type=MESH))
        for cp in sends:
            cp.start()
        for cp in sends:
            cp.wait_recv()
        for cp in sends:
            cp.wait_send()

    return pl.pallas_call(
        body, name="chip_exchange",
        in_specs=[_any()], out_specs=_any(),
        out_shape=jax.ShapeDtypeStruct((3, ROWS, D), sums.dtype),
        scratch_shapes=[pltpu.SemaphoreType.DMA((3,)), pltpu.SemaphoreType.DMA((3,))],
        compiler_params=pltpu.CompilerParams(has_side_effects=True),
    )(sums)


def _final_sum(sums, got, smalls):
    tr = ROWS // 4

    def body(me_ref, own_ref, got_ref, sm_ref, g_ref, gs_ref):
        acc = own_ref[...].astype(F32)
        for j in range(3):
            acc = acc + got_ref[j].astype(F32)
        g_ref[...] = acc

        @pl.when(pl.program_id(0) == 0)
        def _():
            tot = sm_ref[0]
            for d in range(1, NDEV):
                tot = tot + sm_ref[d]
            gs_ref[...] = tot

    x, y, _ = _place()
    me_chip = (2 * x + y).astype(jnp.int32).reshape((1,))
    return pl.pallas_call(
        body, name="final_sum",
        grid_spec=pltpu.PrefetchScalarGridSpec(
            num_scalar_prefetch=1, grid=(ROWS // tr,),
            in_specs=[pl.BlockSpec((None, tr, D), lambda r, me: (me[0], r, 0)),
                      pl.BlockSpec((3, tr, D), lambda r, me: (0, r, 0)),
                      pl.BlockSpec((NDEV, SMALL_ROWS, D), lambda r, me: (0, 0, 0))],
            out_specs=[pl.BlockSpec((tr, D), lambda r, me: (r, 0)),
                       pl.BlockSpec((SMALL_ROWS, D), lambda r, me: (0, 0))]),
        out_shape=[jax.ShapeDtypeStruct((ROWS, D), F32), jax.ShapeDtypeStruct((SMALL_ROWS, D), F32)],
        compiler_params=_cp(("arbitrary",)),
    )(me_chip, sums, got, smalls)


def _pad_cols(a, cols):
    return jnp.pad(a, ((0, 0), (0, cols - a.shape[1])))


def _pad_rows(a, rows):
    return jnp.pad(a, ((0, rows - a.shape[0]), (0, 0)))


def _pack_block(w_in_t, w_a_s, w_b_s, w_o_s, gu_s, dtype):
    return jnp.concatenate([
        _pad_rows(w_in_t, SHARD_PAD).astype(dtype), w_a_s.astype(dtype), w_b_s.astype(dtype), w_o_s.astype(dtype),
        _pad_cols(gu_s, D).astype(dtype)], axis=0)


def _build_wft(wt):
    q = wt[0:1024].reshape(8, 2, 2, 32, D).transpose(0, 2, 1, 3, 4).reshape(1024, D)
    k = wt[1024:1152].reshape(2, 2, 1, 32, D)
    kd = jnp.broadcast_to(k, (2, 2, 2, 32, D)).reshape(256, D)
    v = wt[1152:1280].reshape(2, 1, 64, D)
    vd = jnp.broadcast_to(v, (2, 2, 64, D)).reshape(256, D)
    ag, bq, bk = wt[1280:2304], wt[2304:2816], wt[2816:3328]
    bv, bg, bl = wt[3328:4352], wt[4352:5376], wt[5376:5392]
    ma, mb = wt[5392:6416], wt[6416:7440]
    return jnp.concatenate([q, ag, bv, bg, ma, mb, bq, bk, kd, vd, _pad_rows(bl, NF - C_BL)], axis=0)


def _unbuild_gwt(g):
    q = g["q"].reshape(8, 2, 2, 32, D).transpose(0, 2, 1, 3, 4).reshape(1024, D)
    k = g["kd"].reshape(2, 2, 2, 32, D).sum(axis=2).reshape(128, D)
    v = g["vd"].reshape(2, 2, 64, D).sum(axis=1).reshape(128, D)
    return jnp.concatenate([q, k, v, g["ag"], g["bq"], g["bk"], g["bv"], g["bg"], g["bl"][:RANK],
                            g["ma"], g["mb"]], axis=0)


def _local_step(x, pos_col, target, norm_w, wf, w_a, w_b, w_out, gu_pad, bias, w_bn, sinks, fnw):
    cos, sin = _rope_tables(pos_col)
    h = _norm1(x, norm_w)
    proj = _proj(h, wf)
    o_a, lse = _swa_fwd(proj, cos, sin, sinks)
    o_b, states = _gla_fwd(proj, gu_pad, bias)
    w_bn4 = jnp.tile(w_bn, (1, B_HEADS))
    (dx2, do_a, d_ag, do_b, d_bg, d_ma, d_mb, g_wa, g_wb, g_wo, g_fn, g_bn, loss) = _mid(
        x, target, proj, o_a, o_b, w_a, w_b, w_out, w_bn4, fnw)
    d_q, d_kd, d_vd, g_sinks = _swa_bwd(proj, cos, sin, sinks, do_a, o_a, lse)
    d_bq, d_bk, d_bv, d_bl, g_gu, g_bias = _gla_bwd(proj, gu_pad, bias, states, do_b)
    names = ["q", "ag", "bv", "bg", "ma", "mb", "bq", "bk", "kd", "vd", "bl"]
    pieces = [d_q, d_ag, d_bv, d_bg, d_ma, d_mb, d_bq, d_bk, d_kd, d_vd, d_bl]
    offsets = [C_Q, C_AG, C_BV, C_BG, C_MA, C_MB, C_BQ, C_BK, C_KD, C_VD, C_BL]
    gw = {nm: _gw_piece(h, dp, i) for i, (nm, dp) in enumerate(zip(names, pieces))}
    grad_x, g_nw = _dh_norm(pieces, offsets, wf, x, dx2, norm_w)
    grads = dict(w_in_t=_unbuild_gwt(gw), w_a=g_wa, w_b=g_wb, w_out=g_wo, gate_up=g_gu[:RANK],
                 norm_w=g_nw[0:1], sinks=g_sinks[0:1, :A_HEADS], bias=g_bias[0:1], bn=g_bn[0:1], fnw=g_fn[0:1])
    return loss[0:1, 0:1], grad_x, grads


def kernel(x, positions, norm_w, w_in, a_sinks, b_gate_up, b_gate_bias, b_out_norm_w, w_a_proj, w_b_proj, w_out, final_norm_w, loss_target, m_norm_w, m_w_in, m_a_sinks, m_b_gate_up, m_b_gate_bias, m_b_out_norm_w, m_w_a_proj, m_w_b_proj, m_w_out, m_final_norm_w, v_norm_w, v_w_in, v_a_sinks, v_b_gate_up, v_b_gate_bias, v_b_out_norm_w, v_w_a_proj, v_w_b_proj, v_w_out, v_final_norm_w):
    T = x.shape[1]

    def view(a):
        return a[0].T.reshape(SHARD * 8, LANE)

    def unview(a):
        return a.reshape(SHARD, D).T[None]

    blk = _pack_block(w_in[0].T, w_a_proj[0], w_b_proj[0], w_out[0], b_gate_up[0], WIRE)
    allw = _gather_blocks(blk)
    wf = _build_wft(allw[:, :SHARD, :].reshape(IN_WIDTH, D))
    w_a = allw[:, R_A:R_A + 128, :].reshape(D, D)
    w_b = allw[:, R_B:R_B + 128, :].reshape(D, D)
    w_o = allw[:, R_O:R_O + 128, :].reshape(D, D)
    gu = allw[:, R_GU:R_GU + RANK, :64].transpose(1, 0, 2).reshape(RANK, 512)
    gu_pad = _pad_rows(gu, W_BL)

    loss_part, grad_x, g = _local_step(
        x[0], positions.reshape(T, 1), loss_target[0], norm_w, wf, w_a, w_b, w_o, gu_pad,
        b_gate_bias, b_out_norm_w, a_sinks, final_norm_w.reshape(1, D))

    gin = g["w_in_t"].reshape(NDEV, SHARD, D)
    ggu = g["gate_up"].reshape(RANK, NDEV, 64).transpose(1, 0, 2)
    packed = jnp.concatenate([
        jnp.pad(gin, ((0, 0), (0, SHARD_PAD - SHARD), (0, 0))).astype(WIRE),
        g["w_a"].reshape(NDEV, 128, D).astype(WIRE),
        g["w_b"].reshape(NDEV, 128, D).astype(WIRE),
        g["w_out"].reshape(NDEV, 128, D).astype(WIRE),
        jnp.pad(ggu, ((0, 0), (0, 0), (0, D - 64))).astype(WIRE),
    ], axis=1)

    def tile8(a):
        a = a.reshape(1, -1)
        return jnp.pad(a, ((0, 7), (0, D - a.shape[1])))

    small = jnp.concatenate([tile8(g["norm_w"]), tile8(g["fnw"]), tile8(g["bias"]), tile8(g["bn"]),
                             tile8(g["sinks"]), tile8(loss_part)], axis=0)
    from_sib, smalls = _pair_exchange(packed, small)
    sums = _pair_sum(packed, from_sib)
    from_chips = _chip_exchange(sums)
    gsh, gsm = _final_sum(sums, from_chips, smalls)

    g_w_in_v = gsh[:SHARD].reshape(SHARD * 8, LANE)
    g_w_a, g_w_b, g_w_o = gsh[R_A:R_A + 128][None], gsh[R_B:R_B + 128][None], gsh[R_O:R_O + 128][None]
    g_gu = gsh[R_GU:R_GU + RANK, :64]
    g_norm_w, g_fnw = gsm[0:1], gsm[8]
    g_bias, g_bn, g_sinks = gsm[16:17, :512], gsm[24:25, :B_DV], gsm[32:33, :A_HEADS]
    loss = gsm[40, 0]

    def pack_small(nw, fw, bi, bn, si, gu_s):
        return jnp.concatenate([tile8(nw), tile8(fw), tile8(bi), tile8(bn), tile8(si),
                                _pad_cols(gu_s, D)], axis=0)

    ws = pack_small(norm_w, final_norm_w, b_gate_bias, b_out_norm_w, a_sinks, b_gate_up[0])
    gs = pack_small(g_norm_w, g_fnw, g_bias, g_bn, g_sinks, g_gu)
    ms = pack_small(m_norm_w, m_final_norm_w, m_b_gate_bias, m_b_out_norm_w, m_a_sinks, m_b_gate_up[0])
    vs = pack_small(v_norm_w, v_final_norm_w, v_b_gate_bias, v_b_out_norm_w, v_a_sinks, v_b_gate_up[0])
    small_out = _adamw(ws, gs, ms, vs, "small")

    def unpack_small(a):
        return dict(norm_w=a[0:1], fnw=a[8], bias=a[16:17, :512], bn=a[24:25, :B_DV], sinks=a[32:33, :A_HEADS],
                    gate_up=a[40:56, :64][None])

    sm = [unpack_small(a) for a in small_out]
    big = {
        "w_in": [unview(a) for a in _adamw(view(w_in), g_w_in_v, view(m_w_in), view(v_w_in), "w_in")],
        "w_a": _adamw(w_a_proj[0], g_w_a[0], m_w_a_proj[0], v_w_a_proj[0], "w_a"),
        "w_b": _adamw(w_b_proj[0], g_w_b[0], m_w_b_proj[0], v_w_b_proj[0], "w_b"),
        "w_out": _adamw(w_out[0], g_w_o[0], m_w_out[0], v_w_out[0], "w_out"),
    }
    grads_out = [g_norm_w, unview(g_w_in_v), g_sinks, g_gu[None], g_bias, g_bn, g_w_a, g_w_b, g_w_o, g_fnw]

    def triple(k):
        return [sm[k]["norm_w"], big["w_in"][k], sm[k]["sinks"], sm[k]["gate_up"], sm[k]["bias"], sm[k]["bn"],
                big["w_a"][k][None], big["w_b"][k][None], big["w_out"][k][None], sm[k]["fnw"]]

    return (loss, grad_x[None], *grads_out, *triple(0), *triple(1), *triple(2))
```

```python
import functools

import numpy as np
import jax
import jax.numpy as jnp
from jax import lax
from jax.experimental import pallas as pl
from jax.experimental.pallas import tpu as pltpu

F32 = jnp.float32
MXU = jnp.bfloat16
WIRE = jnp.bfloat16

D = 1024
A_HEADS, A_KV, A_HD = 16, 2, 64
BLK = 128
B_HEADS, B_DK, B_DV = 4, 128, 256
RANK, TAU, CHUNK = 16, 16.0, 64
EPS, NEG = 1e-5, -1e30
ROPE_THETA = 10000.0
IN_WIDTH, NDEV = 7440, 8
SHARD = IN_WIDTH // NDEV
LANE = 128

C_Q, C_KD, C_VD, C_BL = 0, 1024, 1280, 1536
C_BV, C_BQ, C_BK = 2048, 3072, 3584
C_AG, C_BG, C_MA, C_MB = 4096, 5120, 6144, 7168
C_GLA, W_GLA, C_GATES, W_GATES = 2048, 2048, 4096, 4096
NF = 8192
W_BL = 128

SHARD_PAD = 944
R_IN, R_A, R_B, R_O, R_GU, ROWS = 0, 944, 1072, 1200, 1328, 1344
SMALL_ROWS = 48

ADAM_LR, ADAM_B1, ADAM_B2, ADAM_EPS, ADAM_WD, ADAM_STEP = 0.001, 0.9, 0.999, 1e-08, 0.01, 10

MESH = pl.DeviceIdType.MESH
VMEM_LIMIT = 56 * 1024 * 1024


def _cp(sem=None, **kw):
    if sem is not None:
        kw["dimension_semantics"] = sem
    return pltpu.CompilerParams(vmem_limit_bytes=VMEM_LIMIT, **kw)


def _dot(a, b):
    return jnp.dot(a, b, preferred_element_type=F32)


def _dot_nt(a, b):
    return lax.dot_general(a, b, (((1,), (1,)), ((), ())), preferred_element_type=F32)


def _dot_tn(a, b):
    return lax.dot_general(a, b, (((0,), (0,)), ((), ())), preferred_element_type=F32)


def _dot_f32(a, b):
    return jnp.dot(a, b, preferred_element_type=F32, precision=lax.Precision.HIGHEST)


def _sigmoid(z):
    return 1.0 / (1.0 + jnp.exp(-z))


def _rope(xp, cos, sin):
    return xp * cos + pltpu.roll(xp, 64, 1) * sin


def _rope_bwd(dy, cos, sin):
    return dy * cos - pltpu.roll(dy, 64, 1) * sin


def _vmem():
    return pl.BlockSpec(memory_space=pltpu.VMEM)


def _any():
    return pl.BlockSpec(memory_space=pl.ANY)


def _rope_tables(pos_col):
    T = pos_col.shape[0]
    tT = min(T, 512)
    half = A_HD // 2
    inv = (np.float32(ROPE_THETA) ** (-np.arange(half, dtype=np.float32) / np.float32(half))).astype(np.float32)
    inv_row = jnp.asarray(np.tile(inv, 4)[None, :])
    sign_row = jnp.asarray(np.concatenate([-np.ones(64, np.float32), np.ones(64, np.float32)])[None, :])

    def body(pos_ref, inv_ref, sign_ref, cos_ref, sin_ref):
        ang = pos_ref[...].astype(F32) * inv_ref[...]
        cos_ref[...] = jnp.cos(ang)
        sin_ref[...] = jnp.sin(ang) * sign_ref[...]

    row = pl.BlockSpec((1, LANE), lambda i: (0, 0))
    tile = pl.BlockSpec((tT, LANE), lambda i: (i, 0))
    return pl.pallas_call(
        body, name="rope_tables", grid=(T // tT,),
        in_specs=[pl.BlockSpec((tT, 1), lambda i: (i, 0)), row, row],
        out_specs=[tile, tile],
        out_shape=[jax.ShapeDtypeStruct((T, LANE), F32)] * 2,
        compiler_params=_cp(("parallel",)),
    )(pos_col, inv_row, sign_row)


def _norm1(x, norm_w):
    T = x.shape[0]
    tT = min(T, 256)

    def body(x_ref, w_ref, h_ref):
        xv = x_ref[...]
        r = lax.rsqrt(jnp.mean(xv * xv, axis=-1, keepdims=True) + EPS)
        h_ref[...] = ((xv * r) * w_ref[...]).astype(h_ref.dtype)

    return pl.pallas_call(
        body, name="norm1", grid=(T // tT,),
        in_specs=[pl.BlockSpec((tT, D), lambda i: (i, 0)), pl.BlockSpec((1, D), lambda i: (0, 0))],
        out_specs=pl.BlockSpec((tT, D), lambda i: (i, 0)),
        out_shape=jax.ShapeDtypeStruct((T, D), MXU),
        compiler_params=_cp(("parallel",)),
    )(x, norm_w)


def _proj(h, wft):
    T = h.shape[0]
    tT, tN = T, 512

    def body(h_ref, w_ref, o_ref):
        o_ref[...] = _dot_nt(h_ref[...], w_ref[...])

    return pl.pallas_call(
        body, name="proj", grid=(T // tT, NF // tN),
        in_specs=[pl.BlockSpec((tT, D), lambda i, j: (i, 0)), pl.BlockSpec((tN, D), lambda i, j: (j, 0))],
        out_specs=pl.BlockSpec((tT, tN), lambda i, j: (i, j)),
        out_shape=jax.ShapeDtypeStruct((T, NF), F32),
        compiler_params=_cp(("parallel", "parallel")),
    )(h, wft)


def _swa_masks():
    lane = lax.broadcasted_iota(jnp.int32, (BLK, LANE), 1)
    rope_sub0 = ((lane // 32) % 2) == 0
    std_sub0 = lane < 64
    return lane, rope_sub0, std_sub0


def _swa_valid(n):
    qi = lax.broadcasted_iota(jnp.int32, (BLK, 2 * BLK), 0)
    ki = lax.broadcasted_iota(jnp.int32, (BLK, 2 * BLK), 1)
    rel = qi + BLK - ki
    return (rel >= 0) & (rel < BLK) & ((n > 0) | (ki >= BLK))


def _swa_keys(kc_ref, kp_ref, vc_ref, vp_ref, cq, sq, cp, sp):
    def ropek(kref, c, s):
        kv = kref[...]
        return jnp.concatenate([_rope(kv[:, :LANE], c, s), _rope(kv[:, LANE:], c, s)], axis=1)

    K = jnp.concatenate([ropek(kp_ref, cp, sp), ropek(kc_ref, cq, sq)], axis=0).astype(MXU)
    V = jnp.concatenate([vp_ref[...], vc_ref[...]], axis=0).astype(MXU)
    return K, V


def _swa_in_specs(nb, last):
    def cur(n):
        return jnp.minimum(n, last)

    def prev(n):
        return jnp.maximum(cur(n) - 1, 0)

    kd, vd = C_KD // 256, C_VD // 256
    return [
        pl.BlockSpec((BLK, D), lambda n: (cur(n), C_Q // D)),
        pl.BlockSpec((BLK, 256), lambda n: (cur(n), kd)),
        pl.BlockSpec((BLK, 256), lambda n: (prev(n), kd)),
        pl.BlockSpec((BLK, 256), lambda n: (cur(n), vd)),
        pl.BlockSpec((BLK, 256), lambda n: (prev(n), vd)),
        pl.BlockSpec((BLK, LANE), lambda n: (cur(n), 0)),
        pl.BlockSpec((BLK, LANE), lambda n: (cur(n), 0)),
        pl.BlockSpec((BLK, LANE), lambda n: (prev(n), 0)),
        pl.BlockSpec((BLK, LANE), lambda n: (prev(n), 0)),
    ]


def _swa_fwd(proj, cos, sin, sinks):
    T = proj.shape[0]
    nb = T // BLK
    scale = A_HD ** -0.5

    def body(sinks_ref, q_ref, kc_ref, kp_ref, vc_ref, vp_ref, cq_ref, sq_ref, cp_ref, sp_ref, o_ref, l_ref):
        n = pl.program_id(0)
        cq, sq = cq_ref[...], sq_ref[...]
        K, V = _swa_keys(kc_ref, kp_ref, vc_ref, vp_ref, cq, sq, cp_ref[...], sp_ref[...])
        valid = _swa_valid(n)
        lane, rope_sub0, std_sub0 = _swa_masks()
        lacc = jnp.zeros((BLK, LANE), F32)
        for pb in range(A_HEADS // 2):
            g = pb // (A_HEADS // 2 // A_KV)
            Kg, Vg = K[:, g * LANE:(g + 1) * LANE], V[:, g * LANE:(g + 1) * LANE]
            qp = _rope(q_ref[:, pb * LANE:(pb + 1) * LANE], cq, sq)
            outs = []
            for sub in range(2):
                head = 2 * pb + sub
                qm = jnp.where(rope_sub0 if sub == 0 else ~rope_sub0, qp, 0.0).astype(MXU)
                s = jnp.where(valid, _dot_nt(qm, Kg) * scale, NEG)
                sink = sinks_ref[0, head]
                m = jnp.maximum(jnp.max(s, axis=1, keepdims=True), sink)
                e = jnp.exp(s - m)
                den = jnp.sum(e, axis=1, keepdims=True) + jnp.exp(sink - m)
                p = e / den
                outs.append(_dot(p.astype(MXU), Vg))
                lacc = jnp.where(lane == head, m + jnp.log(den), lacc)
            o_ref[:, pb * LANE:(pb + 1) * LANE] = jnp.where(std_sub0, outs[0], outs[1])
        l_ref[...] = lacc

    return pl.pallas_call(
        body, name="swa_fwd", grid=(nb,),
        in_specs=[pl.BlockSpec(memory_space=pltpu.SMEM)] + _swa_in_specs(nb, nb - 1),
        out_specs=[pl.BlockSpec((BLK, D), lambda n: (n, 0)), pl.BlockSpec((BLK, LANE), lambda n: (n, 0))],
        out_shape=[jax.ShapeDtypeStruct((T, D), F32), jax.ShapeDtypeStruct((T, LANE), F32)],
        compiler_params=_cp(("parallel",)),
    )(sinks, proj, proj, proj, proj, proj, cos, sin, cos, sin)


def _swa_bwd(proj, cos, sin, sinks, do_a, o_a, lse):
    T = proj.shape[0]
    nb = T // BLK
    scale = A_HD ** -0.5

    def body(sinks_ref, q_ref, kc_ref, kp_ref, vc_ref, vp_ref, cq_ref, sq_ref, cp_ref, sp_ref,
             do_ref, o_ref, l_ref, dq_ref, dkv_ref, ds_ref, ckv_ref):
        n = pl.program_id(0)

        @pl.when(n == 0)
        def _():
            ckv_ref[...] = jnp.zeros_like(ckv_ref)
            ds_ref[...] = jnp.zeros_like(ds_ref)

        @pl.when(n < nb)
        def _():
            cq, sq, cp, sp = cq_ref[...], sq_ref[...], cp_ref[...], sp_ref[...]
            K, V = _swa_keys(kc_ref, kp_ref, vc_ref, vp_ref, cq, sq, cp, sp)
            valid = _swa_valid(n)
            lane, rope_sub0, std_sub0 = _swa_masks()
            lane_row = lax.broadcasted_iota(jnp.int32, (1, LANE), 1)
            lse_v = l_ref[...]
            dK = [jnp.zeros((2 * BLK, LANE), F32) for _ in range(A_KV)]
            dV = [jnp.zeros((2 * BLK, LANE), F32) for _ in range(A_KV)]
            dsink = jnp.zeros((1, LANE), F32)
            for pb in range(A_HEADS // 2):
                g = pb // (A_HEADS // 2 // A_KV)
                Kg, Vg = K[:, g * LANE:(g + 1) * LANE], V[:, g * LANE:(g + 1) * LANE]
                qp = _rope(q_ref[:, pb * LANE:(pb + 1) * LANE], cq, sq)
                dop = do_ref[:, pb * LANE:(pb + 1) * LANE]
                op = o_ref[:, pb * LANE:(pb + 1) * LANE]
                dqs = []
                for sub in range(2):
                    head = 2 * pb + sub
                    rmask = rope_sub0 if sub == 0 else ~rope_sub0
                    smask = std_sub0 if sub == 0 else ~std_sub0
                    qm = jnp.where(rmask, qp, 0.0).astype(MXU)
                    lh = jnp.sum(jnp.where(lane == head, lse_v, 0.0), axis=1, keepdims=True)
                    s = _dot_nt(qm, Kg) * scale
                    p = jnp.where(valid, jnp.exp(s - lh), 0.0)
                    dov = jnp.where(smask, dop, 0.0)
                    delta = jnp.sum(dov * op, axis=1, keepdims=True)
                    dovb = dov.astype(MXU)
                    dp = _dot_nt(dovb, Vg)
                    dsc = ((p * (dp - delta)) * scale).astype(MXU)
                    psink = jnp.exp(sinks_ref[0, head] - lh)
                    dsink = jnp.where(lane_row == head, jnp.sum(-psink * delta, axis=0, keepdims=True), dsink)
                    dqs.append(_dot(dsc, Kg))
                    dK[g] = dK[g] + _dot_tn(dsc, qm)
                    dV[g] = dV[g] + _dot_tn(p.astype(MXU), dovb)
                dqp = jnp.where(rope_sub0, dqs[0], dqs[1])
                dq_ref[:, pb * LANE:(pb + 1) * LANE] = _rope_bwd(dqp, cq, sq).astype(dq_ref.dtype)
            prev = ([_rope_bwd(dK[g][:BLK], cp, sp) for g in range(A_KV)] + [dV[g][:BLK] for g in range(A_KV)])
            cur_ = ([_rope_bwd(dK[g][BLK:], cq, sq) for g in range(A_KV)] + [dV[g][BLK:] for g in range(A_KV)])
            dkv_ref[...] = (ckv_ref[...] + jnp.concatenate(prev, axis=1)).astype(dkv_ref.dtype)
            ckv_ref[...] = jnp.concatenate(cur_, axis=1)
            ds_ref[...] = ds_ref[...] + jnp.broadcast_to(dsink, ds_ref.shape)

        @pl.when(n == nb)
        def _():
            dkv_ref[...] = ckv_ref[...].astype(dkv_ref.dtype)

    last = nb - 1

    def cur(n):
        return jnp.minimum(n, last)

    def out_kv(n):
        return (jnp.maximum(n - 1, 0), 0)

    return pl.pallas_call(
        body, name="swa_bwd", grid=(nb + 1,),
        in_specs=[pl.BlockSpec(memory_space=pltpu.SMEM)] + _swa_in_specs(nb, last) + [
            pl.BlockSpec((BLK, D), lambda n: (cur(n), 0)),
            pl.BlockSpec((BLK, D), lambda n: (cur(n), 0)),
            pl.BlockSpec((BLK, LANE), lambda n: (cur(n), 0)),
        ],
        out_specs=[
            pl.BlockSpec((BLK, D), lambda n: (cur(n), 0)),
            pl.BlockSpec((BLK, 512), out_kv),
            pl.BlockSpec((8, LANE), lambda n: (0, 0)),
        ],
        out_shape=[
            jax.ShapeDtypeStruct((T, D), MXU),
            jax.ShapeDtypeStruct((T, 512), MXU),
            jax.ShapeDtypeStruct((8, LANE), F32),
        ],
        scratch_shapes=[pltpu.VMEM((BLK, 512), F32)],
        compiler_params=_cp(("arbitrary",)),
    )(sinks, proj, proj, proj, proj, proj, cos, sin, cos, sin, do_a, o_a, lse)


def _gla_gate(bl_ref, gu_ref, bias_ref):
    gk = _dot(bl_ref[...].astype(MXU), gu_ref[...]) + bias_ref[...]
    la = (jnp.minimum(gk, 0.0) - jnp.log(1.0 + jnp.exp(-jnp.abs(gk)))) / TAU
    ri = lax.broadcasted_iota(jnp.int32, (CHUNK, CHUNK), 0)
    ci = lax.broadcasted_iota(jnp.int32, (CHUNK, CHUNK), 1)
    b = _dot_f32(jnp.where(ci <= ri, 1.0, 0.0).astype(F32), la)
    return gk, la, b, ri, ci


def _gla_head(q_ref, k_ref, la, b, h):
    sl = slice(h * B_DK, (h + 1) * B_DK)
    bh = b[:, sl]
    blast = jnp.sum(la[:, sl], axis=0, keepdims=True)
    qc = q_ref[:, sl] * (B_DK ** -0.5)
    kh = k_ref[:, sl]
    eb, enb, esb = jnp.exp(bh), jnp.exp(-bh), jnp.exp(blast - bh)
    return qc * eb, kh * enb, kh * esb, eb, enb, esb, jnp.exp(blast)


def _gla_specs(chunk_of):
    return [
        pl.BlockSpec((CHUNK, 512), lambda i: (chunk_of(i), C_BQ // 512)),
        pl.BlockSpec((CHUNK, 512), lambda i: (chunk_of(i), C_BK // 512)),
        pl.BlockSpec((CHUNK, D), lambda i: (chunk_of(i), C_BV // D)),
        pl.BlockSpec((CHUNK, W_BL), lambda i: (chunk_of(i), C_BL // W_BL)),
        pl.BlockSpec((W_BL, 512), lambda i: (0, 0)),
        pl.BlockSpec((1, 512), lambda i: (0, 0)),
    ]


def _gla_fwd(proj, gu_pad, bias):
    T = proj.shape[0]
    nc = T // CHUNK

    def body(q_ref, k_ref, v_ref, bl_ref, gu_ref, bias_ref, o_ref, st_ref, state_ref):
        @pl.when(pl.program_id(0) == 0)
        def _():
            state_ref[...] = jnp.zeros_like(state_ref)

        _, la, b, ri, ci = _gla_gate(bl_ref, gu_ref, bias_ref)
        st_ref[...] = state_ref[...]
        for h in range(B_HEADS):
            q_e, k_e, k_s, _, _, _, decay = _gla_head(q_ref, k_ref, la, b, h)
            vh = v_ref[:, h * B_DV:(h + 1) * B_DV].astype(MXU)
            rows = slice(h * B_DV, (h + 1) * B_DV)
            q_eb = q_e.astype(MXU)
            att = jnp.where(ci <= ri, _dot_nt(q_eb, k_e.astype(MXU)), 0.0)
            st = state_ref[rows, :]
            o_ref[:, rows] = _dot(att.astype(MXU), vh) + _dot_nt(q_eb, st.astype(MXU))
            state_ref[rows, :] = st * decay + _dot_tn(vh, k_s.astype(MXU))

    return pl.pallas_call(
        body, name="gla_fwd", grid=(nc,),
        in_specs=_gla_specs(lambda i: i),
        out_specs=[pl.BlockSpec((CHUNK, D), lambda i: (i, 0)),
                   pl.BlockSpec((B_HEADS * B_DV, B_DK), lambda i: (i, 0))],
        out_shape=[jax.ShapeDtypeStruct((T, D), F32),
                   jax.ShapeDtypeStruct((nc * B_HEADS * B_DV, B_DK), F32)],
        scratch_shapes=[pltpu.VMEM((B_HEADS * B_DV, B_DK), F32)],
        compiler_params=_cp(("arbitrary",)),
    )(proj, proj, proj, proj, gu_pad, bias)


def _gla_bwd(proj, gu_pad, bias, states, do_b):
    T = proj.shape[0]
    nc = T // CHUNK
    o_q, o_k = C_BQ - C_GLA, C_BK - C_GLA

    def body(q_ref, k_ref, v_ref, bl_ref, gu_ref, bias_ref, st_ref, do_ref,
             dg_ref, dbl_ref, ggu_ref, gbias_ref, gt_ref):
        @pl.when(pl.program_id(0) == 0)
        def _():
            gt_ref[...] = jnp.zeros_like(gt_ref)
            ggu_ref[...] = jnp.zeros_like(ggu_ref)
            gbias_ref[...] = jnp.zeros_like(gbias_ref)

        gk, la, b, ri, ci = _gla_gate(bl_ref, gu_ref, bias_ref)
        causal = ci <= ri
        upper = jnp.where(ci >= ri, 1.0, 0.0).astype(F32)
        dla_parts = []
        for h in range(B_HEADS):
            q_e, k_e, k_s, eb, enb, esb, decay = _gla_head(q_ref, k_ref, la, b, h)
            rows = slice(h * B_DV, (h + 1) * B_DV)
            sl = slice(h * B_DK, (h + 1) * B_DK)
            vh = v_ref[:, rows].astype(MXU)
            doh = do_ref[:, rows].astype(MXU)
            q_eb, k_eb, k_sb = q_e.astype(MXU), k_e.astype(MXU), k_s.astype(MXU)
            st = st_ref[rows, :]
            gt = gt_ref[rows, :]
            gtb = gt.astype(MXU)
            att = jnp.where(causal, _dot_nt(q_eb, k_eb), 0.0).astype(MXU)
            datt = jnp.where(causal, _dot_nt(doh, vh), 0.0).astype(MXU)
            dq_e = _dot(datt, k_eb) + _dot(doh, st.astype(MXU))
            dk_e = _dot_tn(datt, q_eb)
            dk_s = _dot(vh, gtb)
            dg_ref[:, rows] = (_dot_tn(att, doh) + _dot_nt(k_sb, gtb)).astype(dg_ref.dtype)
            ddecay = jnp.sum(gt * st, axis=0, keepdims=True)
            gt_ref[rows, :] = gt * decay + _dot_tn(doh, q_eb)
            dg_ref[:, o_q + h * B_DK:o_q + (h + 1) * B_DK] = (dq_e * eb * (B_DK ** -0.5)).astype(dg_ref.dtype)
            dg_ref[:, o_k + h * B_DK:o_k + (h + 1) * B_DK] = (dk_e * enb + dk_s * esb).astype(dg_ref.dtype)
            dks_ks = dk_s * k_s
            db = dq_e * q_e - dk_e * k_e - dks_ks
            dblast = jnp.sum(dks_ks, axis=0, keepdims=True) + ddecay * decay
            dla_parts.append(_dot_f32(upper, db) + dblast)
        dla = jnp.concatenate(dla_parts, axis=1)
        dgk = dla * (1.0 / TAU) * _sigmoid(-gk)
        dgkb = dgk.astype(MXU)
        dbl_ref[...] = _dot_nt(dgkb, gu_ref[...]).astype(dbl_ref.dtype)
        ggu_ref[...] = ggu_ref[...] + _dot_tn(bl_ref[...].astype(MXU), dgkb)
        gbias_ref[...] = gbias_ref[...] + jnp.broadcast_to(jnp.sum(dgk, axis=0, keepdims=True), gbias_ref.shape)

    def rev(i):
        return nc - 1 - i

    return pl.pallas_call(
        body, name="gla_bwd", grid=(nc,),
        in_specs=_gla_specs(rev) + [
            pl.BlockSpec((B_HEADS * B_DV, B_DK), lambda i: (rev(i), 0)),
            pl.BlockSpec((CHUNK, D), lambda i: (rev(i), 0)),
        ],
        out_specs=[
            pl.BlockSpec((CHUNK, W_GLA), lambda i: (rev(i), 0)),
            pl.BlockSpec((CHUNK, W_BL), lambda i: (rev(i), 0)),
            pl.BlockSpec((W_BL, 512), lambda i: (0, 0)),
            pl.BlockSpec((8, 512), lambda i: (0, 0)),
        ],
        out_shape=[
            jax.ShapeDtypeStruct((T, W_GLA), MXU),
            jax.ShapeDtypeStruct((T, W_BL), MXU),
            jax.ShapeDtypeStruct((W_BL, 512), F32),
            jax.ShapeDtypeStruct((8, 512), F32),
        ],
        scratch_shapes=[pltpu.VMEM((B_HEADS * B_DV, B_DK), F32)],
        compiler_params=_cp(("arbitrary",)),
    )(proj, proj, proj, proj, gu_pad, bias, states, do_b)


def _mid(x, target, proj, o_a, o_b, w_a, w_b, w_out, w_bn4, fnw):
    T = x.shape[0]
    tT = min(T, 128)
    nbuf = 4
    o_ag, o_bg, o_ma, o_mb = (c - C_GATES for c in (C_AG, C_BG, C_MA, C_MB))

    def body(x_ref, t_ref, oa_ref, ob_ref, gates_ref, wa_ref, wb_ref, wo_ref, wbn_ref, fnw_ref,
             dx2_ref, doa_ref, dob_ref, dgates_ref,
             gwa_ref, gwb_ref, gwo_ref, gfn_ref, gbn_ref, loss_ref, buf_ref):
        i = pl.program_id(0)

        @pl.when(i == 0)
        def _():
            for r in (gwa_ref, gwb_ref, gwo_ref, gfn_ref, gbn_ref, loss_ref):
                r[...] = jnp.zeros_like(r)

        rows = pl.ds(pl.multiple_of((i % nbuf) * tT, tT), tT)

        def keep(k, val):
            buf_ref[k, rows, :] = val

        oa, ag = oa_ref[...], gates_ref[:, o_ag:o_ag + D]
        sg_a = _sigmoid(ag)
        silu_a = ag * sg_a
        oag_b = (oa * silu_a).astype(MXU)
        keep(0, oag_b)
        y_a = _dot(oag_b, wa_ref[...])

        ob, bg = ob_ref[...], gates_ref[:, o_bg:o_bg + D]
        rbs, obhats = [], []
        for h in range(B_HEADS):
            obh = ob[:, h * B_DV:(h + 1) * B_DV]
            rb = lax.rsqrt(jnp.mean(obh * obh, axis=-1, keepdims=True) + EPS)
            rbs.append(rb)
            obhats.append(obh * rb)
        obhat = jnp.concatenate(obhats, axis=1)
        wbn = wbn_ref[...]
        obn = obhat * wbn
        sg_b = _sigmoid(bg)
        silu_b = bg * sg_b
        obg_b = (obn * silu_b).astype(MXU)
        keep(1, obg_b)
        y_b = _dot(obg_b, wb_ref[...])

        sa, sb = _sigmoid(gates_ref[:, o_ma:o_ma + D]), _sigmoid(gates_ref[:, o_mb:o_mb + D])
        mg_b = (sa * y_a + sb * y_b).astype(MXU)
        keep(2, mg_b)
        x2 = x_ref[...] + _dot(mg_b, wo_ref[...])
        r2 = lax.rsqrt(jnp.mean(x2 * x2, axis=-1, keepdims=True) + EPS)
        xh2 = x2 * r2
        fw = fnw_ref[...]
        err = xh2 * fw - t_ref[...]
        tok = jnp.mean(err * err, axis=-1, keepdims=True)
        loss_ref[...] = loss_ref[...] + 0.5 * jnp.sum(tok, axis=0, keepdims=True)

        dy = err * (1.0 / D)
        gfn_ref[...] = gfn_ref[...] + jnp.broadcast_to(jnp.sum(dy * xh2, axis=0, keepdims=True), gfn_ref.shape)
        gy = dy * fw
        dx2 = r2 * (gy - xh2 * jnp.mean(gy * xh2, axis=-1, keepdims=True))
        dx2_ref[...] = dx2
        dx2_b = dx2.astype(MXU)
        keep(5, dx2_b)
        dmg = _dot_nt(dx2_b, wo_ref[...])

        dgates_ref[:, o_ma:o_ma + D] = (dmg * y_a * sa * (1.0 - sa)).astype(dgates_ref.dtype)
        dgates_ref[:, o_mb:o_mb + D] = (dmg * y_b * sb * (1.0 - sb)).astype(dgates_ref.dtype)
        dya_b = (dmg * sa).astype(MXU)
        dyb_b = (dmg * sb).astype(MXU)
        keep(3, dya_b)
        keep(4, dyb_b)
        doag = _dot_nt(dya_b, wa_ref[...])
        dobg = _dot_nt(dyb_b, wb_ref[...])

        @pl.when(i % nbuf == nbuf - 1)
        def _():
            gwa_ref[...] = gwa_ref[...] + _dot_tn(buf_ref[0], buf_ref[3])
            gwb_ref[...] = gwb_ref[...] + _dot_tn(buf_ref[1], buf_ref[4])
            gwo_ref[...] = gwo_ref[...] + _dot_tn(buf_ref[2], buf_ref[5])

        doa_ref[...] = doag * silu_a
        dgates_ref[:, o_ag:o_ag + D] = (doag * oa * (sg_a * (1.0 + ag * (1.0 - sg_a)))).astype(dgates_ref.dtype)
        dobn = dobg * silu_b
        dgates_ref[:, o_bg:o_bg + D] = (dobg * obn * (sg_b * (1.0 + bg * (1.0 - sg_b)))).astype(dgates_ref.dtype)
        gg = dobn * wbn
        gbn = jnp.zeros((1, B_DV), F32)
        for h in range(B_HEADS):
            sl = slice(h * B_DV, (h + 1) * B_DV)
            gbn = gbn + jnp.sum(dobn[:, sl] * obhats[h], axis=0, keepdims=True)
            ggh = gg[:, sl]
            dob_ref[:, sl] = rbs[h] * (ggh - obhats[h] * jnp.mean(ggh * obhats[h], axis=-1, keepdims=True))
        gbn_ref[...] = gbn_ref[...] + jnp.broadcast_to(gbn, gbn_ref.shape)

    assert (T // tT) % nbuf == 0
    tile = pl.BlockSpec((tT, D), lambda i: (i, 0))
    row = pl.BlockSpec((1, D), lambda i: (0, 0))
    acc8 = pl.BlockSpec((8, D), lambda i: (0, 0))
    return pl.pallas_call(
        body, name="mid", grid=(T // tT,),
        in_specs=[tile, tile, tile, tile, pl.BlockSpec((tT, W_GATES), lambda i: (i, C_GATES // W_GATES)),
                  _vmem(), _vmem(), _vmem(), row, row],
        out_specs=[tile, tile, tile, pl.BlockSpec((tT, W_GATES), lambda i: (i, 0)), _vmem(), _vmem(), _vmem(),
                   acc8, pl.BlockSpec((8, B_DV), lambda i: (0, 0)), pl.BlockSpec((8, LANE), lambda i: (0, 0))],
        out_shape=[
            jax.ShapeDtypeStruct((T, D), F32),
            jax.ShapeDtypeStruct((T, D), F32),
            jax.ShapeDtypeStruct((T, D), F32),
            jax.ShapeDtypeStruct((T, W_GATES), MXU),
            jax.ShapeDtypeStruct((D, D), F32),
            jax.ShapeDtypeStruct((D, D), F32),
            jax.ShapeDtypeStruct((D, D), F32),
            jax.ShapeDtypeStruct((8, D), F32),
            jax.ShapeDtypeStruct((8, B_DV), F32),
            jax.ShapeDtypeStruct((8, LANE), F32),
        ],
        scratch_shapes=[pltpu.VMEM((6, nbuf * tT, D), MXU)],
        compiler_params=_cp(("arbitrary",)),
    )(x, target, o_a, o_b, proj, w_a, w_b, w_out, w_bn4, fnw)


def _gw_piece(h, dp, idx):
    T, w = dp.shape
    tn = min(w, 512)

    def body(h_ref, dp_ref, o_ref):
        o_ref[...] = _dot_tn(dp_ref[...], h_ref[...])

    return pl.pallas_call(
        body, name=f"gw_in_{idx}", grid=(w // tn,),
        in_specs=[pl.BlockSpec((T, D), lambda j: (0, 0)), pl.BlockSpec((T, tn), lambda j: (0, j))],
        out_specs=pl.BlockSpec((tn, D), lambda j: (j, 0)),
        out_shape=jax.ShapeDtypeStruct((w, D), F32),
        compiler_params=_cp(("parallel",)),
    )(h, dp)


def _dh_norm(pieces, offsets, wf, x, dx2, norm_w):
    T = x.shape[0]
    tT = min(T, 256)
    widths = [p.shape[1] for p in pieces]
    npc = len(pieces)

    def body(*refs):
        dp_refs = refs[:npc]
        wf_ref, x_ref, dx2_ref, nw_ref, gx_ref, gnw_ref = refs[npc:]

        @pl.when(pl.program_id(0) == 0)
        def _():
            gnw_ref[...] = jnp.zeros_like(gnw_ref)

        dh = jnp.zeros((tT, D), F32)
        for dp_ref, off, w in zip(dp_refs, offsets, widths):
            dh = dh + _dot(dp_ref[...], wf_ref[off:off + w, :])
        xv = x_ref[...]
        r = lax.rsqrt(jnp.mean(xv * xv, axis=-1, keepdims=True) + EPS)
        xh = xv * r
        gnw_ref[...] = gnw_ref[...] + jnp.broadcast_to(jnp.sum(dh * xh, axis=0, keepdims=True), gnw_ref.shape)
        g = dh * nw_ref[...]
        gx_ref[...] = r * (g - xh * jnp.mean(g * xh, axis=-1, keepdims=True)) + dx2_ref[...]

    tile = pl.BlockSpec((tT, D), lambda i: (i, 0))
    return pl.pallas_call(
        body, name="dh_norm", grid=(T // tT,),
        in_specs=[pl.BlockSpec((tT, w), lambda i: (i, 0)) for w in widths]
        + [_vmem(), tile, tile, pl.BlockSpec((1, D), lambda i: (0, 0))],
        out_specs=[tile, pl.BlockSpec((8, D), lambda i: (0, 0))],
        out_shape=[jax.ShapeDtypeStruct((T, D), F32), jax.ShapeDtypeStruct((8, D), F32)],
        compiler_params=_cp(("arbitrary",)),
    )(*pieces, wf, x, dx2, norm_w)


def _adamw_math(w, g, m, v):
    m = ADAM_B1 * m + (1.0 - ADAM_B1) * g
    v = ADAM_B2 * v + (1.0 - ADAM_B2) * (g * g)
    m_hat = m / (1.0 - ADAM_B1 ** ADAM_STEP)
    v_hat = v / (1.0 - ADAM_B2 ** ADAM_STEP)
    delta = -ADAM_LR * (m_hat / (jnp.sqrt(v_hat) + ADAM_EPS) + ADAM_WD * w)
    return delta, m, v


def _adamw(w, g, m, v, idx):
    rows, cols = w.shape
    tr = next((t for t in (1488, 1024, 512, 256, 128) if rows % t == 0), rows)

    def body(w_ref, g_ref, m_ref, v_ref, d_ref, nm_ref, nv_ref):
        d, nm, nv = _adamw_math(w_ref[...], g_ref[...], m_ref[...], v_ref[...])
        d_ref[...] = d
        nm_ref[...] = nm
        nv_ref[...] = nv

    blk = pl.BlockSpec((tr, cols), lambda i: (i, 0))
    return pl.pallas_call(
        body, name=f"adamw_{idx}", grid=(rows // tr,),
        in_specs=[blk] * 4, out_specs=[blk] * 3,
        out_shape=[jax.ShapeDtypeStruct((rows, cols), F32)] * 3,
        compiler_params=_cp(("parallel",)),
    )(w, g, m, v)


def _place():
    x, y, c = lax.axis_index("x"), lax.axis_index("y"), lax.axis_index("c")
    return x, y, c


def _gather_blocks(blk):
    rows, cols = blk.shape

    def body(x_ref, out_ref, send_sems, recv_sems, local_sem):
        x, y, c = _place()
        me, sibling = (x, y, c), (x, y, 1 - c)
        chips = [(1 - x, y), (x, 1 - y), (1 - x, 1 - y)]

        def slot(px, py, pc):
            return out_ref.at[4 * px + 2 * py + pc]

        def copy(k, block, to, src=None):
            return pltpu.make_async_remote_copy(
                src_ref=slot(*block) if src is None else src, dst_ref=slot(*block),
                send_sem=send_sems.at[k], recv_sem=recv_sems.at[k], device_id=to, device_id_type=MESH)

        mine = pltpu.make_async_copy(x_ref, slot(*me), local_sem)
        mine.start()
        first = [copy(0, me, sibling, src=x_ref)]
        first += [copy(1 + j, me, (*chip, c), src=x_ref) for j, chip in enumerate(chips)]
        for cp in first:
            cp.start()
        passed = [copy(4 + j, (*chip, c), sibling) for j, chip in enumerate(chips)]
        for j, chip in enumerate(chips):
            copy(1 + j, (*chip, c), me).wait_recv()
            passed[j].start()
        copy(0, sibling, me).wait_recv()
        for j, chip in enumerate(chips):
            copy(4 + j, (*chip, 1 - c), me).wait_recv()
        for cp in first + passed:
            cp.wait_send()
        mine.wait()

    return pl.pallas_call(
        body, name="gather_weights",
        in_specs=[_any()], out_specs=_any(),
        out_shape=jax.ShapeDtypeStruct((NDEV, rows, cols), blk.dtype),
        scratch_shapes=[pltpu.SemaphoreType.DMA((7,)), pltpu.SemaphoreType.DMA((7,)), pltpu.SemaphoreType.DMA],
        compiler_params=pltpu.CompilerParams(has_side_effects=True),
    )(blk)


def _pair_exchange(packed, small):
    def body(p_ref, s_ref, got_ref, smalls_ref, send_sems, recv_sems, s_send, s_recv, local_sem):
        x, y, c = _place()
        sibling = (x, y, 1 - c)
        sends = []
        for chip in range(4):
            sends.append(pltpu.make_async_remote_copy(
                src_ref=p_ref.at[2 * chip + (1 - c)], dst_ref=got_ref.at[chip],
                send_sem=send_sems.at[chip], recv_sem=recv_sems.at[chip], device_id=sibling, device_id_type=MESH))
        for cp in sends:
            cp.start()
        me_slot = 4 * x + 2 * y + c
        mine = pltpu.make_async_copy(s_ref, smalls_ref.at[me_slot], local_sem)
        mine.start()
        small_sends = []
        k = 0
        for dx in range(2):
            for dy in range(2):
                for dc in range(2):
                    if dx == 0 and dy == 0 and dc == 0:
                        continue
                    to = (x ^ dx, y ^ dy, c ^ dc)
                    small_sends.append(pltpu.make_async_remote_copy(
                        src_ref=s_ref, dst_ref=smalls_ref.at[me_slot],
                        send_sem=s_send.at[k], recv_sem=s_recv.at[k], device_id=to, device_id_type=MESH))
                    k += 1
        for cp in small_sends:
            cp.start()
        for cp in sends:
            cp.wait_recv()
        for cp in small_sends:
            cp.wait_recv()
        for cp in sends + small_sends:
            cp.wait_send()
        mine.wait()

    return pl.pallas_call(
        body, name="pair_exchange",
        in_specs=[_any(), _any()], out_specs=[_any(), _any()],
        out_shape=[jax.ShapeDtypeStruct((4, ROWS, D), packed.dtype),
                   jax.ShapeDtypeStruct((NDEV, SMALL_ROWS, D), F32)],
        scratch_shapes=[pltpu.SemaphoreType.DMA((4,)), pltpu.SemaphoreType.DMA((4,)),
                        pltpu.SemaphoreType.DMA((7,)), pltpu.SemaphoreType.DMA((7,)), pltpu.SemaphoreType.DMA],
        compiler_params=pltpu.CompilerParams(has_side_effects=True),
    )(packed, small)


def _pair_sum(packed, got):
    tr = ROWS // 4

    def body(c_ref, p_ref, g_ref, o_ref):
        o_ref[...] = (p_ref[...].astype(F32) + g_ref[...].astype(F32)).astype(o_ref.dtype)

    c_arr = lax.axis_index("c").astype(jnp.int32).reshape((1,))
    return pl.pallas_call(
        body, name="pair_sum",
        grid_spec=pltpu.PrefetchScalarGridSpec(
            num_scalar_prefetch=1, grid=(4, ROWS // tr),
            in_specs=[pl.BlockSpec((None, tr, D), lambda chip, r, c_ref: (2 * chip + c_ref[0], r, 0)),
                      pl.BlockSpec((None, tr, D), lambda chip, r, c_ref: (chip, r, 0))],
            out_specs=pl.BlockSpec((None, tr, D), lambda chip, r, c_ref: (chip, r, 0))),
        out_shape=jax.ShapeDtypeStruct((4, ROWS, D), packed.dtype),
        compiler_params=_cp(("parallel", "parallel")),
    )(c_arr, packed, got)


def _chip_exchange(sums):
    def body(s_ref, got_ref, send_sems, recv_sems):
        x, y, c = _place()
        chips = [(1 - x, y), (x, 1 - y), (1 - x, 1 - y)]
        sends = []
        for j, (px, py) in enumerate(chips):
            sends.append(pltpu.make_async_remote_copy(
                src_ref=s_ref.at[2 * px + py], dst_ref=got_ref.at[j],
                send_sem=send_sems.at[j], recv_sem=recv_sems.at[j], device_id=(px, py, c), device_id_type=MESH))
        for cp in sends:
            cp.start()
        for cp in sends:
            cp.wait_recv()
        for cp in sends:
            cp.wait_send()

    return pl.pallas_call(
        body, name="chip_exchange",
        in_specs=[_any()], out_specs=_any(),
        out_shape=jax.ShapeDtypeStruct((3, ROWS, D), sums.dtype),
        scratch_shapes=[pltpu.SemaphoreType.DMA((3,)), pltpu.SemaphoreType.DMA((3,))],
        compiler_params=pltpu.CompilerParams(has_side_effects=True),
    )(sums)


def _final_sum(sums, got, smalls):
    tr = ROWS // 4

    def body(me_ref, own_ref, got_ref, sm_ref, g_ref, gs_ref):
        acc = own_ref[...].astype(F32)
        for j in range(3):
            acc = acc + got_ref[j].astype(F32)
        g_ref[...] = acc

        @pl.when(pl.program_id(0) == 0)
        def _():
            tot = sm_ref[0]
            for d in range(1, NDEV):
                tot = tot + sm_ref[d]
            gs_ref[...] = tot

    x, y, _ = _place()
    me_chip = (2 * x + y).astype(jnp.int32).reshape((1,))
    return pl.pallas_call(
        body, name="final_sum",
        grid_spec=pltpu.PrefetchScalarGridSpec(
            num_scalar_prefetch=1, grid=(ROWS // tr,),
            in_specs=[pl.BlockSpec((None, tr, D), lambda r, me: (me[0], r, 0)),
                      pl.BlockSpec((3, tr, D), lambda r, me: (0, r, 0)),
                      pl.BlockSpec((NDEV, SMALL_ROWS, D), lambda r, me: (0, 0, 0))],
            out_specs=[pl.BlockSpec((tr, D), lambda r, me: (r, 0)),
                       pl.BlockSpec((SMALL_ROWS, D), lambda r, me: (0, 0))]),
        out_shape=[jax.ShapeDtypeStruct((ROWS, D), F32), jax.ShapeDtypeStruct((SMALL_ROWS, D), F32)],
        compiler_params=_cp(("arbitrary",)),
    )(me_chip, sums, got, smalls)


def _pad_cols(a, cols):
    return jnp.pad(a, ((0, 0), (0, cols - a.shape[1])))


def _pad_rows(a, rows):
    return jnp.pad(a, ((0, rows - a.shape[0]), (0, 0)))


def _pack_block(w_in_t, w_a_s, w_b_s, w_o_s, gu_s, dtype):
    return jnp.concatenate([
        _pad_rows(w_in_t, SHARD_PAD).astype(dtype), w_a_s.astype(dtype), w_b_s.astype(dtype), w_o_s.astype(dtype),
        _pad_cols(gu_s, D).astype(dtype)], axis=0)


def _build_wft(wt):
    q = wt[0:1024].reshape(8, 2, 2, 32, D).transpose(0, 2, 1, 3, 4).reshape(1024, D)
    k = wt[1024:1152].reshape(2, 2, 1, 32, D)
    kd = jnp.broadcast_to(k, (2, 2, 2, 32, D)).reshape(256, D)
    v = wt[1152:1280].reshape(2, 1, 64, D)
    vd = jnp.broadcast_to(v, (2, 2, 64, D)).reshape(256, D)
    ag, bq, bk = wt[1280:2304], wt[2304:2816], wt[2816:3328]
    bv, bg, bl = wt[3328:4352], wt[4352:5376], wt[5376:5392]
    ma, mb = wt[5392:6416], wt[6416:7440]
    return jnp.concatenate([q, kd, vd, _pad_rows(bl, C_GLA - C_BL), bv, bq, bk, ag, bg, ma, mb], axis=0)


def _unbuild_gwt(gq, gkv, gbl, ggla, ggates):
    q = gq.reshape(8, 2, 2, 32, D).transpose(0, 2, 1, 3, 4).reshape(1024, D)
    k = gkv[:256].reshape(2, 2, 2, 32, D).sum(axis=2).reshape(128, D)
    v = gkv[256:].reshape(2, 2, 64, D).sum(axis=1).reshape(128, D)
    bv, bq, bk = ggla[:1024], ggla[1024:1536], ggla[1536:]
    ag, bg, ma, mb = (ggates[i * D:(i + 1) * D] for i in range(4))
    return jnp.concatenate([q, k, v, ag, bq, bk, bv, bg, gbl[:RANK], ma, mb], axis=0)


def _local_step(x, pos_col, target, norm_w, wf, w_a, w_b, w_out, gu_pad, bias, w_bn, sinks, fnw):
    cos, sin = _rope_tables(pos_col)
    h = _norm1(x, norm_w)
    proj = _proj(h, wf)
    o_a, lse = _swa_fwd(proj, cos, sin, sinks)
    o_b, states = _gla_fwd(proj, gu_pad, bias)
    w_bn4 = jnp.tile(w_bn, (1, B_HEADS))
    (dx2, do_a, do_b, d_gates, g_wa, g_wb, g_wo, g_fn, g_bn, loss) = _mid(
        x, target, proj, o_a, o_b, w_a, w_b, w_out, w_bn4, fnw)
    d_q, d_kv, g_sinks = _swa_bwd(proj, cos, sin, sinks, do_a, o_a, lse)
    d_gla, d_bl, g_gu, g_bias = _gla_bwd(proj, gu_pad, bias, states, do_b)
    pieces = [d_q, d_kv, d_bl, d_gla, d_gates]
    offsets = [C_Q, C_KD, C_BL, C_GLA, C_GATES]
    gw = [_gw_piece(h, dp, nm) for nm, dp in zip(["q", "kv", "bl", "gla", "gates"], pieces)]
    grad_x, g_nw = _dh_norm(pieces, offsets, wf, x, dx2, norm_w)
    grads = dict(w_in_t=_unbuild_gwt(*gw), w_a=g_wa, w_b=g_wb, w_out=g_wo, gate_up=g_gu[:RANK],
                 norm_w=g_nw[0:1], sinks=g_sinks[0:1, :A_HEADS], bias=g_bias[0:1], bn=g_bn[0:1], fnw=g_fn[0:1])
    return loss[0:1, 0:1], grad_x, grads


def kernel(x, positions, norm_w, w_in, a_sinks, b_gate_up, b_gate_bias, b_out_norm_w, w_a_proj, w_b_proj, w_out, final_norm_w, loss_target, m_norm_w, m_w_in, m_a_sinks, m_b_gate_up, m_b_gate_bias, m_b_out_norm_w, m_w_a_proj, m_w_b_proj, m_w_out, m_final_norm_w, v_norm_w, v_w_in, v_a_sinks, v_b_gate_up, v_b_gate_bias, v_b_out_norm_w, v_w_a_proj, v_w_b_proj, v_w_out, v_final_norm_w):
    T = x.shape[1]

    def view(a):
        return a[0].T.reshape(SHARD * 8, LANE)

    def unview(a):
        return a.reshape(SHARD, D).T[None]

    blk = _pack_block(w_in[0].T, w_a_proj[0], w_b_proj[0], w_out[0], b_gate_up[0], WIRE)
    allw = _gather_blocks(blk)
    wf = _build_wft(allw[:, :SHARD, :].reshape(IN_WIDTH, D))
    w_a = allw[:, R_A:R_A + 128, :].reshape(D, D)
    w_b = allw[:, R_B:R_B + 128, :].reshape(D, D)
    w_o = allw[:, R_O:R_O + 128, :].reshape(D, D)
    gu = allw[:, R_GU:R_GU + RANK, :64].transpose(1, 0, 2).reshape(RANK, 512)
    gu_pad = _pad_rows(gu, W_BL)

    loss_part, grad_x, g = _local_step(
        x[0], positions.reshape(T, 1), loss_target[0], norm_w, wf, w_a, w_b, w_o, gu_pad,
        b_gate_bias, b_out_norm_w, a_sinks, final_norm_w.reshape(1, D))

    gin = g["w_in_t"].reshape(NDEV, SHARD, D)
    ggu = g["gate_up"].reshape(RANK, NDEV, 64).transpose(1, 0, 2)
    packed = jnp.concatenate([
        jnp.pad(gin, ((0, 0), (0, SHARD_PAD - SHARD), (0, 0))).astype(WIRE),
        g["w_a"].reshape(NDEV, 128, D).astype(WIRE),
        g["w_b"].reshape(NDEV, 128, D).astype(WIRE),
        g["w_out"].reshape(NDEV, 128, D).astype(WIRE),
        jnp.pad(ggu, ((0, 0), (0, 0), (0, D - 64))).astype(WIRE),
    ], axis=1)

    def tile8(a):
        a = a.reshape(1, -1)
        return jnp.pad(a, ((0, 7), (0, D - a.shape[1])))

    small = jnp.concatenate([tile8(g["norm_w"]), tile8(g["fnw"]), tile8(g["bias"]), tile8(g["bn"]),
                             tile8(g["sinks"]), tile8(loss_part)], axis=0)
    from_sib, smalls = _pair_exchange(packed, small)
    sums = _pair_sum(packed, from_sib)
    from_chips = _chip_exchange(sums)
    gsh, gsm = _final_sum(sums, from_chips, smalls)

    g_w_in_v = gsh[:SHARD].reshape(SHARD * 8, LANE)
    g_w_a, g_w_b, g_w_o = gsh[R_A:R_A + 128][None], gsh[R_B:R_B + 128][None], gsh[R_O:R_O + 128][None]
    g_gu = gsh[R_GU:R_GU + RANK, :64]
    g_norm_w, g_fnw = gsm[0:1], gsm[8]
    g_bias, g_bn, g_sinks = gsm[16:17, :512], gsm[24:25, :B_DV], gsm[32:33, :A_HEADS]
    loss = gsm[40, 0]

    def pack_small(nw, fw, bi, bn, si, gu_s):
        return jnp.concatenate([tile8(nw), tile8(fw), tile8(bi), tile8(bn), tile8(si),
                                _pad_cols(gu_s, D)], axis=0)

    ws = pack_small(norm_w, final_norm_w, b_gate_bias, b_out_norm_w, a_sinks, b_gate_up[0])
    gs = pack_small(g_norm_w, g_fnw, g_bias, g_bn, g_sinks, g_gu)
    ms = pack_small(m_norm_w, m_final_norm_w, m_b_gate_bias, m_b_out_norm_w, m_a_sinks, m_b_gate_up[0])
    vs = pack_small(v_norm_w, v_final_norm_w, v_b_gate_bias, v_b_out_norm_w, v_a_sinks, v_b_gate_up[0])
    small_out = _adamw(ws, gs, ms, vs, "small")

    def unpack_small(a):
        return dict(norm_w=a[0:1], fnw=a[8], bias=a[16:17, :512], bn=a[24:25, :B_DV], sinks=a[32:33, :A_HEADS],
                    gate_up=a[40:56, :64][None])

    sm = [unpack_small(a) for a in small_out]
    big = {
        "w_in": [unview(a) for a in _adamw(view(w_in), g_w_in_v, view(m_w_in), view(v_w_in), "w_in")],
        "w_a": _adamw(w_a_proj[0], g_w_a[0], m_w_a_proj[0], v_w_a_proj[0], "w_a"),
        "w_b": _adamw(w_b_proj[0], g_w_b[0], m_w_b_proj[0], v_w_b_proj[0], "w_b"),
        "w_out": _adamw(w_out[0], g_w_o[0], m_w_out[0], v_w_out[0], "w_out"),
    }
    grads_out = [g_norm_w, unview(g_w_in_v), g_sinks, g_gu[None], g_bias, g_bn, g_w_a, g_w_b, g_w_o, g_fnw]

    def triple(k):
        return [sm[k]["norm_w"], big["w_in"][k], sm[k]["sinks"], sm[k]["gate_up"], sm[k]["bias"], sm[k]["bn"],
                big["w_a"][k][None], big["w_b"][k][None], big["w_out"][k][None], sm[k]["fnw"]]

    return (loss, grad_x[None], *grads_out, *triple(0), *triple(1), *triple(2))
```

```python
import functools

import numpy as np
import jax
import jax.numpy as jnp
from jax import lax
from jax.experimental import pallas as pl
from jax.experimental.pallas import tpu as pltpu

F32 = jnp.float32
MXU = jnp.bfloat16
WIRE = jnp.bfloat16

D = 1024
A_HEADS, A_KV, A_HD = 16, 2, 64
BLK = 128
B_HEADS, B_DK, B_DV = 4, 128, 256
RANK, TAU, CHUNK = 16, 16.0, 64
EPS, NEG = 1e-5, -1e30
ROPE_THETA = 10000.0
IN_WIDTH, NDEV = 7440, 8
SHARD = IN_WIDTH // NDEV
LANE = 128

C_Q, C_KD, C_VD, C_BL = 0, 1024, 1280, 1536
C_BV, C_BQ, C_BK = 2048, 3072, 3584
C_AG, C_BG, C_MA, C_MB = 4096, 5120, 6144, 7168
C_GLA, W_GLA, C_GATES, W_GATES = 2048, 2048, 4096, 4096
NF = 8192
W_BL = 128

SHARD_PAD = 944
R_IN, R_A, R_B, R_O, R_GU, ROWS = 0, 944, 1072, 1200, 1328, 1344
SMALL_ROWS = 48

ADAM_LR, ADAM_B1, ADAM_B2, ADAM_EPS, ADAM_WD, ADAM_STEP = 0.001, 0.9, 0.999, 1e-08, 0.01, 10

MESH = pl.DeviceIdType.MESH
VMEM_LIMIT = 56 * 1024 * 1024


def _cp(sem=None, **kw):
    if sem is not None:
        kw["dimension_semantics"] = sem
    return pltpu.CompilerParams(vmem_limit_bytes=VMEM_LIMIT, **kw)


def _dot(a, b):
    return jnp.dot(a, b, preferred_element_type=F32)


def _dot_nt(a, b):
    return lax.dot_general(a, b, (((1,), (1,)), ((), ())), preferred_element_type=F32)


def _dot_tn(a, b):
    return lax.dot_general(a, b, (((0,), (0,)), ((), ())), preferred_element_type=F32)


def _dot_f32(a, b):
    return jnp.dot(a, b, preferred_element_type=F32, precision=lax.Precision.HIGHEST)


def _sigmoid(z):
    return 1.0 / (1.0 + jnp.exp(-z))


def _rope(xp, cos, sin):
    return xp * cos + pltpu.roll(xp, 64, 1) * sin


def _rope_bwd(dy, cos, sin):
    return dy * cos - pltpu.roll(dy, 64, 1) * sin


def _vmem():
    return pl.BlockSpec(memory_space=pltpu.VMEM)


def _any():
    return pl.BlockSpec(memory_space=pl.ANY)


def _rope_tables(pos_col):
    T = pos_col.shape[0]
    tT = min(T, 512)
    half = A_HD // 2
    inv = (np.float32(ROPE_THETA) ** (-np.arange(half, dtype=np.float32) / np.float32(half))).astype(np.float32)
    inv_row = jnp.asarray(np.tile(inv, 4)[None, :])
    sign_row = jnp.asarray(np.concatenate([-np.ones(64, np.float32), np.ones(64, np.float32)])[None, :])

    def body(pos_ref, inv_ref, sign_ref, cos_ref, sin_ref):
        ang = pos_ref[...].astype(F32) * inv_ref[...]
        cos_ref[...] = jnp.cos(ang)
        sin_ref[...] = jnp.sin(ang) * sign_ref[...]

    row = pl.BlockSpec((1, LANE), lambda i: (0, 0))
    tile = pl.BlockSpec((tT, LANE), lambda i: (i, 0))
    return pl.pallas_call(
        body, name="rope_tables", grid=(T // tT,),
        in_specs=[pl.BlockSpec((tT, 1), lambda i: (i, 0)), row, row],
        out_specs=[tile, tile],
        out_shape=[jax.ShapeDtypeStruct((T, LANE), F32)] * 2,
        compiler_params=_cp(("parallel",)),
    )(pos_col, inv_row, sign_row)


def _norm1(x, norm_w):
    T = x.shape[0]
    tT = min(T, 256)

    def body(x_ref, w_ref, h_ref):
        xv = x_ref[...]
        r = lax.rsqrt(jnp.mean(xv * xv, axis=-1, keepdims=True) + EPS)
        h_ref[...] = ((xv * r) * w_ref[...]).astype(h_ref.dtype)

    return pl.pallas_call(
        body, name="norm1", grid=(T // tT,),
        in_specs=[pl.BlockSpec((tT, D), lambda i: (i, 0)), pl.BlockSpec((1, D), lambda i: (0, 0))],
        out_specs=pl.BlockSpec((tT, D), lambda i: (i, 0)),
        out_shape=jax.ShapeDtypeStruct((T, D), MXU),
        compiler_params=_cp(("parallel",)),
    )(x, norm_w)


def _proj(h, wft):
    T = h.shape[0]
    tT, tN = T, 512

    def body(h_ref, w_ref, o_ref):
        o_ref[...] = _dot_nt(h_ref[...], w_ref[...])

    return pl.pallas_call(
        body, name="proj", grid=(T // tT, NF // tN),
        in_specs=[pl.BlockSpec((tT, D), lambda i, j: (i, 0)), pl.BlockSpec((tN, D), lambda i, j: (j, 0))],
        out_specs=pl.BlockSpec((tT, tN), lambda i, j: (i, j)),
        out_shape=jax.ShapeDtypeStruct((T, NF), F32),
        compiler_params=_cp(("parallel", "parallel")),
    )(h, wft)


def _swa_masks():
    lane = lax.broadcasted_iota(jnp.int32, (BLK, LANE), 1)
    rope_sub0 = ((lane // 32) % 2) == 0
    std_sub0 = lane < 64
    return lane, rope_sub0, std_sub0


def _swa_valid(n):
    qi = lax.broadcasted_iota(jnp.int32, (BLK, 2 * BLK), 0)
    ki = lax.broadcasted_iota(jnp.int32, (BLK, 2 * BLK), 1)
    rel = qi + BLK - ki
    return (rel >= 0) & (rel < BLK) & ((n > 0) | (ki >= BLK))


def _swa_keys(kc_ref, kp_ref, vc_ref, vp_ref, cq, sq, cp, sp):
    def ropek(kref, c, s):
        kv = kref[...]
        return jnp.concatenate([_rope(kv[:, :LANE], c, s), _rope(kv[:, LANE:], c, s)], axis=1)

    K = jnp.concatenate([ropek(kp_ref, cp, sp), ropek(kc_ref, cq, sq)], axis=0).astype(MXU)
    V = jnp.concatenate([vp_ref[...], vc_ref[...]], axis=0).astype(MXU)
    return K, V


def _swa_in_specs(nb, last):
    def cur(n):
        return jnp.minimum(n, last)

    def prev(n):
        return jnp.maximum(cur(n) - 1, 0)

    kd, vd = C_KD // 256, C_VD // 256
    return [
        pl.BlockSpec((BLK, D), lambda n: (cur(n), C_Q // D)),
        pl.BlockSpec((BLK, 256), lambda n: (cur(n), kd)),
        pl.BlockSpec((BLK, 256), lambda n: (prev(n), kd)),
        pl.BlockSpec((BLK, 256), lambda n: (cur(n), vd)),
        pl.BlockSpec((BLK, 256), lambda n: (prev(n), vd)),
        pl.BlockSpec((BLK, LANE), lambda n: (cur(n), 0)),
        pl.BlockSpec((BLK, LANE), lambda n: (cur(n), 0)),
        pl.BlockSpec((BLK, LANE), lambda n: (prev(n), 0)),
        pl.BlockSpec((BLK, LANE), lambda n: (prev(n), 0)),
    ]


def _swa_fwd(proj, cos, sin, sinks):
    T = proj.shape[0]
    nb = T // BLK
    scale = A_HD ** -0.5

    def body(sinks_ref, q_ref, kc_ref, kp_ref, vc_ref, vp_ref, cq_ref, sq_ref, cp_ref, sp_ref, o_ref, l_ref):
        n = pl.program_id(0)
        cq, sq = cq_ref[...], sq_ref[...]
        K, V = _swa_keys(kc_ref, kp_ref, vc_ref, vp_ref, cq, sq, cp_ref[...], sp_ref[...])
        valid = _swa_valid(n)
        lane, rope_sub0, std_sub0 = _swa_masks()
        lacc = jnp.zeros((BLK, LANE), F32)
        for pb in range(A_HEADS // 2):
            g = pb // (A_HEADS // 2 // A_KV)
            Kg, Vg = K[:, g * LANE:(g + 1) * LANE], V[:, g * LANE:(g + 1) * LANE]
            qp = _rope(q_ref[:, pb * LANE:(pb + 1) * LANE], cq, sq)
            outs = []
            for sub in range(2):
                head = 2 * pb + sub
                qm = jnp.where(rope_sub0 if sub == 0 else ~rope_sub0, qp, 0.0).astype(MXU)
                s = jnp.where(valid, _dot_nt(qm, Kg) * scale, NEG)
                sink = sinks_ref[0, head]
                m = jnp.maximum(jnp.max(s, axis=1, keepdims=True), sink)
                e = jnp.exp(s - m)
                den = jnp.sum(e, axis=1, keepdims=True) + jnp.exp(sink - m)
                p = e / den
                outs.append(_dot(p.astype(MXU), Vg))
                lacc = jnp.where(lane == head, m + jnp.log(den), lacc)
            o_ref[:, pb * LANE:(pb + 1) * LANE] = jnp.where(std_sub0, outs[0], outs[1])
        l_ref[...] = lacc

    return pl.pallas_call(
        body, name="swa_fwd", grid=(nb,),
        in_specs=[pl.BlockSpec(memory_space=pltpu.SMEM)] + _swa_in_specs(nb, nb - 1),
        out_specs=[pl.BlockSpec((BLK, D), lambda n: (n, 0)), pl.BlockSpec((BLK, LANE), lambda n: (n, 0))],
        out_shape=[jax.ShapeDtypeStruct((T, D), F32), jax.ShapeDtypeStruct((T, LANE), F32)],
        compiler_params=_cp(("parallel",)),
    )(sinks, proj, proj, proj, proj, proj, cos, sin, cos, sin)


def _swa_bwd(proj, cos, sin, sinks, do_a, o_a, lse):
    T = proj.shape[0]
    nb = T // BLK
    scale = A_HD ** -0.5

    def body(sinks_ref, q_ref, kc_ref, kp_ref, vc_ref, vp_ref, cq_ref, sq_ref, cp_ref, sp_ref,
             do_ref, o_ref, l_ref, dq_ref, dkv_ref, ds_ref, ckv_ref):
        n = pl.program_id(0)

        @pl.when(n == 0)
        def _():
            ckv_ref[...] = jnp.zeros_like(ckv_ref)
            ds_ref[...] = jnp.zeros_like(ds_ref)

        @pl.when(n < nb)
        def _():
            cq, sq, cp, sp = cq_ref[...], sq_ref[...], cp_ref[...], sp_ref[...]
            K, V = _swa_keys(kc_ref, kp_ref, vc_ref, vp_ref, cq, sq, cp, sp)
            valid = _swa_valid(n)
            lane, rope_sub0, std_sub0 = _swa_masks()
            lane_row = lax.broadcasted_iota(jnp.int32, (1, LANE), 1)
            lse_v = l_ref[...]
            dK = [jnp.zeros((2 * BLK, LANE), F32) for _ in range(A_KV)]
            dV = [jnp.zeros((2 * BLK, LANE), F32) for _ in range(A_KV)]
            dsink = jnp.zeros((1, LANE), F32)
            for pb in range(A_HEADS // 2):
                g = pb // (A_HEADS // 2 // A_KV)
                Kg, Vg = K[:, g * LANE:(g + 1) * LANE], V[:, g * LANE:(g + 1) * LANE]
                qp = _rope(q_ref[:, pb * LANE:(pb + 1) * LANE], cq, sq)
                dop = do_ref[:, pb * LANE:(pb + 1) * LANE]
                op = o_ref[:, pb * LANE:(pb + 1) * LANE]
                dqs = []
                for sub in range(2):
                    head = 2 * pb + sub
                    rmask = rope_sub0 if sub == 0 else ~rope_sub0
                    smask = std_sub0 if sub == 0 else ~std_sub0
                    qm = jnp.where(rmask, qp, 0.0).astype(MXU)
                    lh = jnp.sum(jnp.where(lane == head, lse_v, 0.0), axis=1, keepdims=True)
                    s = _dot_nt(qm, Kg) * scale
                    p = jnp.where(valid, jnp.exp(s - lh), 0.0)
                    dov = jnp.where(smask, dop, 0.0)
                    delta = jnp.sum(dov * op, axis=1, keepdims=True)
                    dovb = dov.astype(MXU)
                    dp = _dot_nt(dovb, Vg)
                    dsc = ((p * (dp - delta)) * scale).astype(MXU)
                    psink = jnp.exp(sinks_ref[0, head] - lh)
                    dsink = jnp.where(lane_row == head, jnp.sum(-psink * delta, axis=0, keepdims=True), dsink)
                    dqs.append(_dot(dsc, Kg))
                    dK[g] = dK[g] + _dot_tn(dsc, qm)
                    dV[g] = dV[g] + _dot_tn(p.astype(MXU), dovb)
                dqp = jnp.where(rope_sub0, dqs[0], dqs[1])
                dq_ref[:, pb * LANE:(pb + 1) * LANE] = _rope_bwd(dqp, cq, sq).astype(dq_ref.dtype)
            prev = ([_rope_bwd(dK[g][:BLK], cp, sp) for g in range(A_KV)] + [dV[g][:BLK] for g in range(A_KV)])
            cur_ = ([_rope_bwd(dK[g][BLK:], cq, sq) for g in range(A_KV)] + [dV[g][BLK:] for g in range(A_KV)])
            dkv_ref[...] = (ckv_ref[...] + jnp.concatenate(prev, axis=1)).astype(dkv_ref.dtype)
            ckv_ref[...] = jnp.concatenate(cur_, axis=1)
            ds_ref[...] = ds_ref[...] + jnp.broadcast_to(dsink, ds_ref.shape)

        @pl.when(n == nb)
        def _():
            dkv_ref[...] = ckv_ref[...].astype(dkv_ref.dtype)

    last = nb - 1

    def cur(n):
        return jnp.minimum(n, last)

    def out_kv(n):
        return (jnp.maximum(n - 1, 0), 0)

    return pl.pallas_call(
        body, name="swa_bwd", grid=(nb + 1,),
        in_specs=[pl.BlockSpec(memory_space=pltpu.SMEM)] + _swa_in_specs(nb, last) + [
            pl.BlockSpec((BLK, D), lambda n: (cur(n), 0)),
            pl.BlockSpec((BLK, D), lambda n: (cur(n), 0)),
            pl.BlockSpec((BLK, LANE), lambda n: (cur(n), 0)),
        ],
        out_specs=[
            pl.BlockSpec((BLK, D), lambda n: (cur(n), 0)),
            pl.BlockSpec((BLK, 512), out_kv),
            pl.BlockSpec((8, LANE), lambda n: (0, 0)),
        ],
        out_shape=[
            jax.ShapeDtypeStruct((T, D), MXU),
            jax.ShapeDtypeStruct((T, 512), MXU),
            jax.ShapeDtypeStruct((8, LANE), F32),
        ],
        scratch_shapes=[pltpu.VMEM((BLK, 512), F32)],
        compiler_params=_cp(("arbitrary",)),
    )(sinks, proj, proj, proj, proj, proj, cos, sin, cos, sin, do_a, o_a, lse)


def _gla_gate(bl_ref, gu_ref, bias_ref):
    gk = _dot(bl_ref[...].astype(MXU), gu_ref[...]) + bias_ref[...]
    la = (jnp.minimum(gk, 0.0) - jnp.log(1.0 + jnp.exp(-jnp.abs(gk)))) / TAU
    ri = lax.broadcasted_iota(jnp.int32, (CHUNK, CHUNK), 0)
    ci = lax.broadcasted_iota(jnp.int32, (CHUNK, CHUNK), 1)
    b = _dot_f32(jnp.where(ci <= ri, 1.0, 0.0).astype(F32), la)
    return gk, la, b, ri, ci


def _gla_head(q_ref, k_ref, la, b, h):
    sl = slice(h * B_DK, (h + 1) * B_DK)
    bh = b[:, sl]
    blast = jnp.sum(la[:, sl], axis=0, keepdims=True)
    qc = q_ref[:, sl] * (B_DK ** -0.5)
    kh = k_ref[:, sl]
    eb, enb, esb = jnp.exp(bh), jnp.exp(-bh), jnp.exp(blast - bh)
    return qc * eb, kh * enb, kh * esb, eb, enb, esb, jnp.exp(blast)


def _gla_specs(chunk_of):
    return [
        pl.BlockSpec((CHUNK, 512), lambda i: (chunk_of(i), C_BQ // 512)),
        pl.BlockSpec((CHUNK, 512), lambda i: (chunk_of(i), C_BK // 512)),
        pl.BlockSpec((CHUNK, D), lambda i: (chunk_of(i), C_BV // D)),
        pl.BlockSpec((CHUNK, W_BL), lambda i: (chunk_of(i), C_BL // W_BL)),
        pl.BlockSpec((W_BL, 512), lambda i: (0, 0)),
        pl.BlockSpec((1, 512), lambda i: (0, 0)),
    ]


def _gla_fwd(proj, gu_pad, bias):
    T = proj.shape[0]
    nc = T // CHUNK

    def body(q_ref, k_ref, v_ref, bl_ref, gu_ref, bias_ref, o_ref, st_ref, state_ref):
        @pl.when(pl.program_id(0) == 0)
        def _():
            state_ref[...] = jnp.zeros_like(state_ref)

        _, la, b, ri, ci = _gla_gate(bl_ref, gu_ref, bias_ref)
        st_ref[...] = state_ref[...]
        for h in range(B_HEADS):
            q_e, k_e, k_s, _, _, _, decay = _gla_head(q_ref, k_ref, la, b, h)
            vh = v_ref[:, h * B_DV:(h + 1) * B_DV].astype(MXU)
            rows = slice(h * B_DV, (h + 1) * B_DV)
            q_eb = q_e.astype(MXU)
            att = jnp.where(ci <= ri, _dot_nt(q_eb, k_e.astype(MXU)), 0.0)
            st = state_ref[rows, :]
            o_ref[:, rows] = _dot(att.astype(MXU), vh) + _dot_nt(q_eb, st.astype(MXU))
            state_ref[rows, :] = st * decay + _dot_tn(vh, k_s.astype(MXU))

    return pl.pallas_call(
        body, name="gla_fwd", grid=(nc,),
        in_specs=_gla_specs(lambda i: i),
        out_specs=[pl.BlockSpec((CHUNK, D), lambda i: (i, 0)),
                   pl.BlockSpec((B_HEADS * B_DV, B_DK), lambda i: (i, 0))],
        out_shape=[jax.ShapeDtypeStruct((T, D), F32),
                   jax.ShapeDtypeStruct((nc * B_HEADS * B_DV, B_DK), F32)],
        scratch_shapes=[pltpu.VMEM((B_HEADS * B_DV, B_DK), F32)],
        compiler_params=_cp(("arbitrary",)),
    )(proj, proj, proj, proj, gu_pad, bias)


def _gla_bwd(proj, gu_pad, bias, states, do_b):
    T = proj.shape[0]
    nc = T // CHUNK
    o_q, o_k = C_BQ - C_GLA, C_BK - C_GLA

    def body(q_ref, k_ref, v_ref, bl_ref, gu_ref, bias_ref, st_ref, do_ref,
             dg_ref, dbl_ref, ggu_ref, gbias_ref, gt_ref):
        @pl.when(pl.program_id(0) == 0)
        def _():
            gt_ref[...] = jnp.zeros_like(gt_ref)
            ggu_ref[...] = jnp.zeros_like(ggu_ref)
            gbias_ref[...] = jnp.zeros_like(gbias_ref)

        gk, la, b, ri, ci = _gla_gate(bl_ref, gu_ref, bias_ref)
        causal = ci <= ri
        upper = jnp.where(ci >= ri, 1.0, 0.0).astype(F32)
        dla_parts = []
        for h in range(B_HEADS):
            q_e, k_e, k_s, eb, enb, esb, decay = _gla_head(q_ref, k_ref, la, b, h)
            rows = slice(h * B_DV, (h + 1) * B_DV)
            sl = slice(h * B_DK, (h + 1) * B_DK)
            vh = v_ref[:, rows].astype(MXU)
            doh = do_ref[:, rows].astype(MXU)
            q_eb, k_eb, k_sb = q_e.astype(MXU), k_e.astype(MXU), k_s.astype(MXU)
            st = st_ref[rows, :]
            gt = gt_ref[rows, :]
            gtb = gt.astype(MXU)
            att = jnp.where(causal, _dot_nt(q_eb, k_eb), 0.0).astype(MXU)
            datt = jnp.where(causal, _dot_nt(doh, vh), 0.0).astype(MXU)
            dq_e = _dot(datt, k_eb) + _dot(doh, st.astype(MXU))
            dk_e = _dot_tn(datt, q_eb)
            dk_s = _dot(vh, gtb)
            dg_ref[:, rows] = (_dot_tn(att, doh) + _dot_nt(k_sb, gtb)).astype(dg_ref.dtype)
            ddecay = jnp.sum(gt * st, axis=0, keepdims=True)
            gt_ref[rows, :] = gt * decay + _dot_tn(doh, q_eb)
            dg_ref[:, o_q + h * B_DK:o_q + (h + 1) * B_DK] = (dq_e * eb * (B_DK ** -0.5)).astype(dg_ref.dtype)
            dg_ref[:, o_k + h * B_DK:o_k + (h + 1) * B_DK] = (dk_e * enb + dk_s * esb).astype(dg_ref.dtype)
            dks_ks = dk_s * k_s
            db = dq_e * q_e - dk_e * k_e - dks_ks
            dblast = jnp.sum(dks_ks, axis=0, keepdims=True) + ddecay * decay
            dla_parts.append(_dot_f32(upper, db) + dblast)
        dla = jnp.concatenate(dla_parts, axis=1)
        dgk = dla * (1.0 / TAU) * _sigmoid(-gk)
        dgkb = dgk.astype(MXU)
        dbl_ref[...] = _dot_nt(dgkb, gu_ref[...]).astype(dbl_ref.dtype)
        ggu_ref[...] = ggu_ref[...] + _dot_tn(bl_ref[...].astype(MXU), dgkb)
        gbias_ref[...] = gbias_ref[...] + jnp.broadcast_to(jnp.sum(dgk, axis=0, keepdims=True), gbias_ref.shape)

    def rev(i):
        return nc - 1 - i

    return pl.pallas_call(
        body, name="gla_bwd", grid=(nc,),
        in_specs=_gla_specs(rev) + [
            pl.BlockSpec((B_HEADS * B_DV, B_DK), lambda i: (rev(i), 0)),
            pl.BlockSpec((CHUNK, D), lambda i: (rev(i), 0)),
        ],
        out_specs=[
            pl.BlockSpec((CHUNK, W_GLA), lambda i: (rev(i), 0)),
            pl.BlockSpec((CHUNK, W_BL), lambda i: (rev(i), 0)),
            pl.BlockSpec((W_BL, 512), lambda i: (0, 0)),
            pl.BlockSpec((8, 512), lambda i: (0, 0)),
        ],
        out_shape=[
            jax.ShapeDtypeStruct((T, W_GLA), MXU),
            jax.ShapeDtypeStruct((T, W_BL), MXU),
            jax.ShapeDtypeStruct((W_BL, 512), F32),
            jax.ShapeDtypeStruct((8, 512), F32),
        ],
        scratch_shapes=[pltpu.VMEM((B_HEADS * B_DV, B_DK), F32)],
        compiler_params=_cp(("arbitrary",)),
    )(proj, proj, proj, proj, gu_pad, bias, states, do_b)


def _mid(x, target, proj, o_a, o_b, w_a, w_b, w_out, w_bn4, fnw):
    T = x.shape[0]
    tT = min(T, 128)
    nbuf = 4
    o_ag, o_bg, o_ma, o_mb = (c - C_GATES for c in (C_AG, C_BG, C_MA, C_MB))

    def body(x_ref, t_ref, oa_ref, ob_ref, gates_ref, wa_ref, wb_ref, wo_ref, wbn_ref, fnw_ref,
             dx2_ref, doa_ref, dob_ref, dgates_ref,
             gwa_ref, gwb_ref, gwo_ref, gfn_ref, gbn_ref, loss_ref, buf_ref):
        i = pl.program_id(0)

        @pl.when(i == 0)
        def _():
            for r in (gwa_ref, gwb_ref, gwo_ref, gfn_ref, gbn_ref, loss_ref):
                r[...] = jnp.zeros_like(r)

        rows = pl.ds(pl.multiple_of((i % nbuf) * tT, tT), tT)

        def keep(k, val):
            buf_ref[k, rows, :] = val

        oa, ag = oa_ref[...], gates_ref[:, o_ag:o_ag + D]
        sg_a = _sigmoid(ag)
        silu_a = ag * sg_a
        oag_b = (oa * silu_a).astype(MXU)
        keep(0, oag_b)
        y_a = _dot(oag_b, wa_ref[...])

        ob, bg = ob_ref[...], gates_ref[:, o_bg:o_bg + D]
        rbs, obhats = [], []
        for h in range(B_HEADS):
            obh = ob[:, h * B_DV:(h + 1) * B_DV]
            rb = lax.rsqrt(jnp.mean(obh * obh, axis=-1, keepdims=True) + EPS)
            rbs.append(rb)
            obhats.append(obh * rb)
        obhat = jnp.concatenate(obhats, axis=1)
        wbn = wbn_ref[...]
        obn = obhat * wbn
        sg_b = _sigmoid(bg)
        silu_b = bg * sg_b
        obg_b = (obn * silu_b).astype(MXU)
        keep(1, obg_b)
        y_b = _dot(obg_b, wb_ref[...])

        sa, sb = _sigmoid(gates_ref[:, o_ma:o_ma + D]), _sigmoid(gates_ref[:, o_mb:o_mb + D])
        mg_b = (sa * y_a + sb * y_b).astype(MXU)
        keep(2, mg_b)
        x2 = x_ref[...] + _dot(mg_b, wo_ref[...])
        r2 = lax.rsqrt(jnp.mean(x2 * x2, axis=-1, keepdims=True) + EPS)
        xh2 = x2 * r2
        fw = fnw_ref[...]
        err = xh2 * fw - t_ref[...]
        tok = jnp.mean(err * err, axis=-1, keepdims=True)
        loss_ref[...] = loss_ref[...] + 0.5 * jnp.sum(tok, axis=0, keepdims=True)

        dy = err * (1.0 / D)
        gfn_ref[...] = gfn_ref[...] + jnp.broadcast_to(jnp.sum(dy * xh2, axis=0, keepdims=True), gfn_ref.shape)
        gy = dy * fw
        dx2 = r2 * (gy - xh2 * jnp.mean(gy * xh2, axis=-1, keepdims=True))
        dx2_ref[...] = dx2
        dx2_b = dx2.astype(MXU)
        keep(5, dx2_b)
        dmg = _dot_nt(dx2_b, wo_ref[...])

        dgates_ref[:, o_ma:o_ma + D] = (dmg * y_a * sa * (1.0 - sa)).astype(dgates_ref.dtype)
        dgates_ref[:, o_mb:o_mb + D] = (dmg * y_b * sb * (1.0 - sb)).astype(dgates_ref.dtype)
        dya_b = (dmg * sa).astype(MXU)
        dyb_b = (dmg * sb).astype(MXU)
        keep(3, dya_b)
        keep(4, dyb_b)
        doag = _dot_nt(dya_b, wa_ref[...])
        dobg = _dot_nt(dyb_b, wb_ref[...])

        @pl.when(i % nbuf == nbuf - 1)
        def _():
            gwa_ref[...] = gwa_ref[...] + _dot_tn(buf_ref[0], buf_ref[3])
            gwb_ref[...] = gwb_ref[...] + _dot_tn(buf_ref[1], buf_ref[4])
            gwo_ref[...] = gwo_ref[...] + _dot_tn(buf_ref[2], buf_ref[5])

        doa_ref[...] = doag * silu_a
        dgates_ref[:, o_ag:o_ag + D] = (doag * oa * (sg_a * (1.0 + ag * (1.0 - sg_a)))).astype(dgates_ref.dtype)
        dobn = dobg * silu_b
        dgates_ref[:, o_bg:o_bg + D] = (dobg * obn * (sg_b * (1.0 + bg * (1.0 - sg_b)))).astype(dgates_ref.dtype)
        gg = dobn * wbn
        gbn = jnp.zeros((1, B_DV), F32)
        for h in range(B_HEADS):
            sl = slice(h * B_DV, (h + 1) * B_DV)
            gbn = gbn + jnp.sum(dobn[:, sl] * obhats[h], axis=0, keepdims=True)
            ggh = gg[:, sl]
            dob_ref[:, sl] = rbs[h] * (ggh - obhats[h] * jnp.mean(ggh * obhats[h], axis=-1, keepdims=True))
        gbn_ref[...] = gbn_ref[...] + jnp.broadcast_to(gbn, gbn_ref.shape)

    assert (T // tT) % nbuf == 0
    tile = pl.BlockSpec((tT, D), lambda i: (i, 0))
    row = pl.BlockSpec((1, D), lambda i: (0, 0))
    acc8 = pl.BlockSpec((8, D), lambda i: (0, 0))
    return pl.pallas_call(
        body, name="mid", grid=(T // tT,),
        in_specs=[tile, tile, tile, tile, pl.BlockSpec((tT, W_GATES), lambda i: (i, C_GATES // W_GATES)),
                  _vmem(), _vmem(), _vmem(), row, row],
        out_specs=[tile, tile, tile, pl.BlockSpec((tT, W_GATES), lambda i: (i, 0)), _vmem(), _vmem(), _vmem(),
                   acc8, pl.BlockSpec((8, B_DV), lambda i: (0, 0)), pl.BlockSpec((8, LANE), lambda i: (0, 0))],
        out_shape=[
            jax.ShapeDtypeStruct((T, D), F32),
            jax.ShapeDtypeStruct((T, D), F32),
            jax.ShapeDtypeStruct((T, D), F32),
            jax.ShapeDtypeStruct((T, W_GATES), MXU),
            jax.ShapeDtypeStruct((D, D), F32),
            jax.ShapeDtypeStruct((D, D), F32),
            jax.ShapeDtypeStruct((D, D), F32),
            jax.ShapeDtypeStruct((8, D), F32),
            jax.ShapeDtypeStruct((8, B_DV), F32),
            jax.ShapeDtypeStruct((8, LANE), F32),
        ],
        scratch_shapes=[pltpu.VMEM((6, nbuf * tT, D), MXU)],
        compiler_params=_cp(("arbitrary",)),
    )(x, target, o_a, o_b, proj, w_a, w_b, w_out, w_bn4, fnw)


def _gw_piece(h, dp, idx):
    T, w = dp.shape
    tn = min(w, 512)

    def body(h_ref, dp_ref, o_ref):
        o_ref[...] = _dot_tn(dp_ref[...], h_ref[...])

    return pl.pallas_call(
        body, name=f"gw_in_{idx}", grid=(w // tn,),
        in_specs=[pl.BlockSpec((T, D), lambda j: (0, 0)), pl.BlockSpec((T, tn), lambda j: (0, j))],
        out_specs=pl.BlockSpec((tn, D), lambda j: (j, 0)),
        out_shape=jax.ShapeDtypeStruct((w, D), F32),
        compiler_params=_cp(("parallel",)),
    )(h, dp)


def _chip_copies(s_ref, got_ref, send_sems, recv_sems):
    x, y, c = _place()
    chips = [(1 - x, y), (x, 1 - y), (1 - x, 1 - y)]
    return [pltpu.make_async_remote_copy(
        src_ref=s_ref.at[2 * px + py], dst_ref=got_ref.at[j],
        send_sem=send_sems.at[j], recv_sem=recv_sems.at[j], device_id=(px, py, c), device_id_type=MESH)
        for j, (px, py) in enumerate(chips)]


def _dh_norm(pieces, offsets, wf, x, dx2, norm_w, sums):
    T = x.shape[0]
    tT = min(T, 256)
    widths = [p.shape[1] for p in pieces]
    npc = len(pieces)
    last = T // tT - 1

    def body(*refs):
        dp_refs = refs[:npc]
        wf_ref, x_ref, dx2_ref, nw_ref, s_ref, gx_ref, gnw_ref, got_ref, send_sems, recv_sems = refs[npc:]

        @pl.when(pl.program_id(0) == 0)
        def _():
            gnw_ref[...] = jnp.zeros_like(gnw_ref)
            for cp in _chip_copies(s_ref, got_ref, send_sems, recv_sems):
                cp.start()

        dh = jnp.zeros((tT, D), F32)
        for dp_ref, off, w in zip(dp_refs, offsets, widths):
            dh = dh + _dot(dp_ref[...], wf_ref[off:off + w, :])
        xv = x_ref[...]
        r = lax.rsqrt(jnp.mean(xv * xv, axis=-1, keepdims=True) + EPS)
        xh = xv * r
        gnw_ref[...] = gnw_ref[...] + jnp.broadcast_to(jnp.sum(dh * xh, axis=0, keepdims=True), gnw_ref.shape)
        g = dh * nw_ref[...]
        gx_ref[...] = r * (g - xh * jnp.mean(g * xh, axis=-1, keepdims=True)) + dx2_ref[...]

        @pl.when(pl.program_id(0) == last)
        def _():
            copies = _chip_copies(s_ref, got_ref, send_sems, recv_sems)
            for cp in copies:
                cp.wait_recv()
            for cp in copies:
                cp.wait_send()

    tile = pl.BlockSpec((tT, D), lambda i: (i, 0))
    return pl.pallas_call(
        body, name="dh_norm", grid=(T // tT,),
        in_specs=[pl.BlockSpec((tT, w), lambda i: (i, 0)) for w in widths]
        + [_vmem(), tile, tile, pl.BlockSpec((1, D), lambda i: (0, 0)), _any()],
        out_specs=[tile, pl.BlockSpec((8, D), lambda i: (0, 0)), _any()],
        out_shape=[jax.ShapeDtypeStruct((T, D), F32), jax.ShapeDtypeStruct((8, D), F32),
                   jax.ShapeDtypeStruct((3, ROWS, D), sums.dtype)],
        scratch_shapes=[pltpu.SemaphoreType.DMA((3,)), pltpu.SemaphoreType.DMA((3,))],
        compiler_params=_cp(("arbitrary",)),
    )(*pieces, wf, x, dx2, norm_w, sums)


def _adamw_math(w, g, m, v):
    m = ADAM_B1 * m + (1.0 - ADAM_B1) * g
    v = ADAM_B2 * v + (1.0 - ADAM_B2) * (g * g)
    m_hat = m / (1.0 - ADAM_B1 ** ADAM_STEP)
    v_hat = v / (1.0 - ADAM_B2 ** ADAM_STEP)
    delta = -ADAM_LR * (m_hat / (jnp.sqrt(v_hat) + ADAM_EPS) + ADAM_WD * w)
    return delta, m, v


def _fetch_partials(s_ref, got_ref, buf, sems, r0, nrows):
    x, y, _ = _place()
    cps = [pltpu.make_async_copy(s_ref.at[2 * x + y, pl.ds(r0, nrows)], buf.at[0], sems.at[0])]
    cps += [pltpu.make_async_copy(got_ref.at[j, pl.ds(r0, nrows)], buf.at[1 + j], sems.at[1 + j]) for j in range(3)]
    for cp in cps:
        cp.start()
    for cp in cps:
        cp.wait()


def _finish_rows(w_list, m_list, v_list, sums, got, r0, name):
    shapes = [w.shape for w in w_list]
    npar = len(w_list)
    pads = [-(-s[0] // 16) * 16 for s in shapes]
    nrows = sum(pads)

    def body(*refs):
        w_refs, m_refs, v_refs = refs[:npar], refs[npar:2 * npar], refs[2 * npar:3 * npar]
        s_ref, got_ref = refs[3 * npar:3 * npar + 2]
        outs = refs[3 * npar + 2:7 * npar + 2]
        buf, sems = refs[7 * npar + 2:]
        _fetch_partials(s_ref, got_ref, buf, sems, r0, nrows)
        off = 0
        for p in range(npar):
            n = shapes[p][0]
            for c in range(D // LANE):
                cols = slice(c * LANE, (c + 1) * LANE)
                g = buf[0, off:off + n, cols].astype(F32)
                for j in range(1, 4):
                    g = g + buf[j, off:off + n, cols].astype(F32)
                d, nm, nv = _adamw_math(w_refs[p][:, cols], g, m_refs[p][:, cols], v_refs[p][:, cols])
                for o, val in zip(outs[4 * p:4 * p + 4], (g, d, nm, nv)):
                    o[:, cols] = val
            off += pads[p]

    out_shape = [jax.ShapeDtypeStruct(s, F32) for s in shapes for _ in range(4)]
    return pl.pallas_call(
        body, name=name,
        in_specs=[_vmem()] * (3 * npar) + [_any(), _any()],
        out_specs=[_vmem()] * (4 * npar),
        out_shape=out_shape,
        scratch_shapes=[pltpu.VMEM((4, nrows, D), sums.dtype), pltpu.SemaphoreType.DMA((4,))],
        compiler_params=_cp(),
    )(*w_list, *m_list, *v_list, sums, got)


SMALL_AT = dict(norm_w=0, fnw=8, bias=16, bn=24, sinks=32, loss=40)


def _finish_small(ws, ms, vs, gu_w, gu_m, gu_v, small, sums, got):
    names = ["norm_w", "fnw", "bias", "bn", "sinks"]
    widths = [ws[n].shape[1] for n in names]

    def body(*refs):
        w_refs, m_refs, v_refs = refs[0:5], refs[5:10], refs[10:15]
        guw_ref, gum_ref, guv_ref, small_ref, s_ref, got_ref = refs[15:21]
        loss_ref = refs[21]
        outs = refs[22:42]
        gu_outs = refs[42:46]
        smalls, tot, buf, send_sems, recv_sems, sems = refs[46:]
        x, y, c = _place()
        me_slot = 4 * x + 2 * y + c
        sends = []
        k = 0
        for dx in range(2):
            for dy in range(2):
                for dc in range(2):
                    if dx == 0 and dy == 0 and dc == 0:
                        continue
                    sends.append(pltpu.make_async_remote_copy(
                        src_ref=small_ref, dst_ref=smalls.at[me_slot],
                        send_sem=send_sems.at[k], recv_sem=recv_sems.at[k],
                        device_id=(x ^ dx, y ^ dy, c ^ dc), device_id_type=MESH))
                    k += 1
        for cp in sends:
            cp.start()
        smalls[me_slot] = small_ref[...]
        _fetch_partials(s_ref, got_ref, buf, sems, R_GU, RANK)
        for cp in sends:
            cp.wait_recv()
        for cp in sends:
            cp.wait_send()
        acc = smalls[0]
        for d in range(1, NDEV):
            acc = acc + smalls[d]
        tot[...] = acc
        loss_ref[...] = tot[SMALL_AT["loss"]:SMALL_AT["loss"] + 1, 0:1]
        for p, (nm_, wd) in enumerate(zip(names, widths)):
            r = SMALL_AT[nm_]
            g = tot[r:r + 1, 0:wd]
            d, nm, nv = _adamw_math(w_refs[p][...], g, m_refs[p][...], v_refs[p][...])
            for o, val in zip(outs[4 * p:4 * p + 4], (g, d, nm, nv)):
                o[...] = val
        g = buf[0, :, 0:64].astype(F32)
        for j in range(1, 4):
            g = g + buf[j, :, 0:64].astype(F32)
        d, nm, nv = _adamw_math(guw_ref[...], g, gum_ref[...], guv_ref[...])
        for o, val in zip(gu_outs, (g, d, nm, nv)):
            o[...] = val

    out_shape = ([jax.ShapeDtypeStruct((1, 1), F32)]
                 + [jax.ShapeDtypeStruct((1, wd), F32) for wd in widths for _ in range(4)]
                 + [jax.ShapeDtypeStruct((RANK, 64), F32)] * 4)
    res = pl.pallas_call(
        body, name="finish_small",
        in_specs=[_vmem()] * 19 + [_any(), _any()],
        out_specs=[_vmem()] * 25,
        out_shape=out_shape,
        scratch_shapes=[pltpu.VMEM((NDEV, SMALL_ROWS, D), F32), pltpu.VMEM((SMALL_ROWS, D), F32),
                        pltpu.VMEM((4, RANK, D), sums.dtype),
                        pltpu.SemaphoreType.DMA((7,)), pltpu.SemaphoreType.DMA((7,)), pltpu.SemaphoreType.DMA((4,))],
        compiler_params=_cp(),
    )(*[ws[n] for n in names], *[ms[n] for n in names], *[vs[n] for n in names], gu_w, gu_m, gu_v, small, sums, got)
    loss = res[0]
    per = {n: tuple(res[1 + 4 * p:5 + 4 * p]) for p, n in enumerate(names)}
    return loss, per, tuple(res[21:25])


def _place():
    x, y, c = lax.axis_index("x"), lax.axis_index("y"), lax.axis_index("c")
    return x, y, c


def _gather_blocks(blk):
    rows, cols = blk.shape

    def body(x_ref, out_ref, send_sems, recv_sems, local_sem):
        x, y, c = _place()
        me, sibling = (x, y, c), (x, y, 1 - c)
        chips = [(1 - x, y), (x, 1 - y), (1 - x, 1 - y)]

        def slot(px, py, pc):
            return out_ref.at[4 * px + 2 * py + pc]

        def copy(k, block, to, src=None):
            return pltpu.make_async_remote_copy(
                src_ref=slot(*block) if src is None else src, dst_ref=slot(*block),
                send_sem=send_sems.at[k], recv_sem=recv_sems.at[k], device_id=to, device_id_type=MESH)

        mine = pltpu.make_async_copy(x_ref, slot(*me), local_sem)
        mine.start()
        first = [copy(0, me, sibling, src=x_ref)]
        first += [copy(1 + j, me, (*chip, c), src=x_ref) for j, chip in enumerate(chips)]
        for cp in first:
            cp.start()
        passed = [copy(4 + j, (*chip, c), sibling) for j, chip in enumerate(chips)]
        for j, chip in enumerate(chips):
            copy(1 + j, (*chip, c), me).wait_recv()
            passed[j].start()
        copy(0, sibling, me).wait_recv()
        for j, chip in enumerate(chips):
            copy(4 + j, (*chip, 1 - c), me).wait_recv()
        for cp in first + passed:
            cp.wait_send()
        mine.wait()

    return pl.pallas_call(
        body, name="gather_weights",
        in_specs=[_any()], out_specs=_any(),
        out_shape=jax.ShapeDtypeStruct((NDEV, rows, cols), blk.dtype),
        scratch_shapes=[pltpu.SemaphoreType.DMA((7,)), pltpu.SemaphoreType.DMA((7,)), pltpu.SemaphoreType.DMA],
        compiler_params=pltpu.CompilerParams(has_side_effects=True),
    )(blk)


def _pair_exchange(packed):
    def body(p_ref, got_ref, send_sems, recv_sems):
        x, y, c = _place()
        sends = [pltpu.make_async_remote_copy(
            src_ref=p_ref.at[2 * chip + (1 - c)], dst_ref=got_ref.at[chip],
            send_sem=send_sems.at[chip], recv_sem=recv_sems.at[chip], device_id=(x, y, 1 - c), device_id_type=MESH)
            for chip in range(4)]
        for cp in sends:
            cp.start()
        for cp in sends:
            cp.wait_recv()
        for cp in sends:
            cp.wait_send()

    return pl.pallas_call(
        body, name="pair_exchange",
        in_specs=[_any()], out_specs=_any(),
        out_shape=jax.ShapeDtypeStruct((4, ROWS, D), packed.dtype),
        scratch_shapes=[pltpu.SemaphoreType.DMA((4,)), pltpu.SemaphoreType.DMA((4,))],
    )(packed)


def _pair_sum(packed, got):
    tr = ROWS // 4

    def body(c_ref, p_ref, g_ref, o_ref):
        o_ref[...] = (p_ref[...].astype(F32) + g_ref[...].astype(F32)).astype(o_ref.dtype)

    c_arr = lax.axis_index("c").astype(jnp.int32).reshape((1,))
    return pl.pallas_call(
        body, name="pair_sum",
        grid_spec=pltpu.PrefetchScalarGridSpec(
            num_scalar_prefetch=1, grid=(4, ROWS // tr),
            in_specs=[pl.BlockSpec((None, tr, D), lambda chip, r, c_ref: (2 * chip + c_ref[0], r, 0)),
                      pl.BlockSpec((None, tr, D), lambda chip, r, c_ref: (chip, r, 0))],
            out_specs=pl.BlockSpec((None, tr, D), lambda chip, r, c_ref: (chip, r, 0))),
        out_shape=jax.ShapeDtypeStruct((4, ROWS, D), packed.dtype),
        compiler_params=_cp(("parallel", "parallel")),
    )(c_arr, packed, got)


def _pad_cols(a, cols):
    return jnp.pad(a, ((0, 0), (0, cols - a.shape[1])))


def _pad_rows(a, rows):
    return jnp.pad(a, ((0, rows - a.shape[0]), (0, 0)))


def _pack_block(w_in_t, w_a_s, w_b_s, w_o_s, gu_s, dtype):
    return jnp.concatenate([
        _pad_rows(w_in_t, SHARD_PAD).astype(dtype), w_a_s.astype(dtype), w_b_s.astype(dtype), w_o_s.astype(dtype),
        _pad_cols(gu_s, D).astype(dtype)], axis=0)


def _build_wft(wt):
    q = wt[0:1024].reshape(8, 2, 2, 32, D).transpose(0, 2, 1, 3, 4).reshape(1024, D)
    k = wt[1024:1152].reshape(2, 2, 1, 32, D)
    kd = jnp.broadcast_to(k, (2, 2, 2, 32, D)).reshape(256, D)
    v = wt[1152:1280].reshape(2, 1, 64, D)
    vd = jnp.broadcast_to(v, (2, 2, 64, D)).reshape(256, D)
    ag, bq, bk = wt[1280:2304], wt[2304:2816], wt[2816:3328]
    bv, bg, bl = wt[3328:4352], wt[4352:5376], wt[5376:5392]
    ma, mb = wt[5392:6416], wt[6416:7440]
    return jnp.concatenate([q, kd, vd, _pad_rows(bl, C_GLA - C_BL), bv, bq, bk, ag, bg, ma, mb], axis=0)


def _unbuild_gwt(gq, gkv, gbl, ggla, ggates):
    q = gq.reshape(8, 2, 2, 32, D).transpose(0, 2, 1, 3, 4).reshape(1024, D)
    k = gkv[:256].reshape(2, 2, 2, 32, D).sum(axis=2).reshape(128, D)
    v = gkv[256:].reshape(2, 2, 64, D).sum(axis=1).reshape(128, D)
    bv, bq, bk = ggla[:1024], ggla[1024:1536], ggla[1536:]
    ag, bg, ma, mb = (ggates[i * D:(i + 1) * D] for i in range(4))
    return jnp.concatenate([q, k, v, ag, bq, bk, bv, bg, gbl[:RANK], ma, mb], axis=0)


def kernel(x, positions, norm_w, w_in, a_sinks, b_gate_up, b_gate_bias, b_out_norm_w, w_a_proj, w_b_proj, w_out, final_norm_w, loss_target, m_norm_w, m_w_in, m_a_sinks, m_b_gate_up, m_b_gate_bias, m_b_out_norm_w, m_w_a_proj, m_w_b_proj, m_w_out, m_final_norm_w, v_norm_w, v_w_in, v_a_sinks, v_b_gate_up, v_b_gate_bias, v_b_out_norm_w, v_w_a_proj, v_w_b_proj, v_w_out, v_final_norm_w):
    T = x.shape[1]
    blk = _pack_block(w_in[0].T, w_a_proj[0], w_b_proj[0], w_out[0], b_gate_up[0], WIRE)
    allw = _gather_blocks(blk)
    wf = _build_wft(allw[:, :SHARD, :].reshape(IN_WIDTH, D))
    w_a = allw[:, R_A:R_A + 128, :].reshape(D, D)
    w_b = allw[:, R_B:R_B + 128, :].reshape(D, D)
    w_o = allw[:, R_O:R_O + 128, :].reshape(D, D)
    gu = allw[:, R_GU:R_GU + RANK, :64].transpose(1, 0, 2).reshape(RANK, 512)
    gu_pad = _pad_rows(gu, W_BL)

    xs, target = x[0], loss_target[0]
    fnw = final_norm_w.reshape(1, D)
    cos, sin = _rope_tables(positions.reshape(T, 1))
    h = _norm1(xs, norm_w)
    proj = _proj(h, wf)
    o_a, lse = _swa_fwd(proj, cos, sin, a_sinks)
    o_b, states = _gla_fwd(proj, gu_pad, b_gate_bias)
    (dx2, do_a, do_b, d_gates, g_wa, g_wb, g_wo, g_fn, g_bn, loss_part) = _mid(
        xs, target, proj, o_a, o_b, w_a, w_b, w_o, jnp.tile(b_out_norm_w, (1, B_HEADS)), fnw)
    d_q, d_kv, g_sinks = _swa_bwd(proj, cos, sin, a_sinks, do_a, o_a, lse)
    d_gla, d_bl, g_gu, g_bias = _gla_bwd(proj, gu_pad, b_gate_bias, states, do_b)
    pieces = [d_q, d_kv, d_bl, d_gla, d_gates]
    offsets = [C_Q, C_KD, C_BL, C_GLA, C_GATES]
    gw = [_gw_piece(h, dp, nm) for nm, dp in zip(["q", "kv", "bl", "gla", "gates"], pieces)]

    gin = _unbuild_gwt(*gw).reshape(NDEV, SHARD, D)
    ggu = g_gu[:RANK].reshape(RANK, NDEV, 64).transpose(1, 0, 2)
    packed = jnp.concatenate([
        jnp.pad(gin, ((0, 0), (0, SHARD_PAD - SHARD), (0, 0))).astype(WIRE),
        g_wa.reshape(NDEV, 128, D).astype(WIRE),
        g_wb.reshape(NDEV, 128, D).astype(WIRE),
        g_wo.reshape(NDEV, 128, D).astype(WIRE),
        jnp.pad(ggu, ((0, 0), (0, 0), (0, D - 64))).astype(WIRE),
    ], axis=1)
    sums = _pair_sum(packed, _pair_exchange(packed))
    grad_x, g_nw, from_chips = _dh_norm(pieces, offsets, wf, xs, dx2, norm_w, sums)

    small = jnp.concatenate([g_nw, g_fn, _pad_cols(g_bias, D), _pad_cols(g_bn, D), _pad_cols(g_sinks, D),
                             _pad_cols(loss_part, D)], axis=0)
    t_in = _finish_rows([w_in[0].T], [m_w_in[0].T], [v_w_in[0].T], sums, from_chips, R_IN, "finish_in")
    t_pr = _finish_rows([w_a_proj[0], w_b_proj[0], w_out[0]], [m_w_a_proj[0], m_w_b_proj[0], m_w_out[0]],
                        [v_w_a_proj[0], v_w_b_proj[0], v_w_out[0]], sums, from_chips, R_A, "finish_proj")
    ws = dict(norm_w=norm_w, fnw=fnw, bias=b_gate_bias, bn=b_out_norm_w, sinks=a_sinks)
    ms = dict(norm_w=m_norm_w, fnw=m_final_norm_w.reshape(1, D), bias=m_b_gate_bias, bn=m_b_out_norm_w,
              sinks=m_a_sinks)
    vs = dict(norm_w=v_norm_w, fnw=v_final_norm_w.reshape(1, D), bias=v_b_gate_bias, bn=v_b_out_norm_w,
              sinks=v_a_sinks)
    loss, sm, t_gu = _finish_small(ws, ms, vs, b_gate_up[0], m_b_gate_up[0], v_b_gate_up[0], small, sums, from_chips)

    def outputs(k):
        return [sm["norm_w"][k], t_in[k].T[None], sm["sinks"][k], t_gu[k][None], sm["bias"][k], sm["bn"][k],
                t_pr[k][None], t_pr[4 + k][None], t_pr[8 + k][None], sm["fnw"][k].reshape(D)]

    return (loss[0, 0], grad_x[None], *outputs(0), *outputs(1), *outputs(2), *outputs(3))
```

```python
import functools

import numpy as np
import jax
import jax.numpy as jnp
from jax import lax
from jax.experimental import pallas as pl
from jax.experimental.pallas import tpu as pltpu

F32 = jnp.float32
MXU = jnp.bfloat16
WIRE = jnp.bfloat16

D = 1024
A_HEADS, A_KV, A_HD = 16, 2, 64
BLK = 128
B_HEADS, B_DK, B_DV = 4, 128, 256
RANK, TAU, CHUNK = 16, 16.0, 64
EPS, NEG = 1e-5, -1e30
ROPE_THETA = 10000.0
IN_WIDTH, NDEV = 7440, 8
SHARD = IN_WIDTH // NDEV
LANE = 128

C_Q, C_KD, C_VD, C_BL = 0, 1024, 1280, 1536
C_BV, C_BQ, C_BK = 2048, 3072, 3584
C_AG, C_BG, C_MA, C_MB = 4096, 5120, 6144, 7168
C_GLA, W_GLA, C_GATES, W_GATES = 2048, 2048, 4096, 4096
NF = 8192
W_BL = 128

SHARD_PAD = 944
R_IN, R_A, R_B, R_O, R_GU, ROWS = 0, 944, 1072, 1200, 1328, 1344
SMALL_ROWS = 48

ADAM_LR, ADAM_B1, ADAM_B2, ADAM_EPS, ADAM_WD, ADAM_STEP = 0.001, 0.9, 0.999, 1e-08, 0.01, 10

MESH = pl.DeviceIdType.MESH
VMEM_LIMIT = 56 * 1024 * 1024


def _cp(sem=None, **kw):
    if sem is not None:
        kw["dimension_semantics"] = sem
    return pltpu.CompilerParams(vmem_limit_bytes=VMEM_LIMIT, **kw)


def _dot(a, b):
    return jnp.dot(a, b, preferred_element_type=F32)


def _dot_nt(a, b):
    return lax.dot_general(a, b, (((1,), (1,)), ((), ())), preferred_element_type=F32)


def _dot_tn(a, b):
    return lax.dot_general(a, b, (((0,), (0,)), ((), ())), preferred_element_type=F32)


def _dot_f32(a, b):
    return jnp.dot(a, b, preferred_element_type=F32, precision=lax.Precision.HIGHEST)


def _sigmoid(z):
    return 1.0 / (1.0 + jnp.exp(-z))


def _rope(xp, cos, sin):
    return xp * cos + pltpu.roll(xp, 64, 1) * sin


def _rope_bwd(dy, cos, sin):
    return dy * cos - pltpu.roll(dy, 64, 1) * sin


def _vmem():
    return pl.BlockSpec(memory_space=pltpu.VMEM)


def _any():
    return pl.BlockSpec(memory_space=pl.ANY)


def _rope_rows():
    half = A_HD // 2
    inv = (np.float32(ROPE_THETA) ** (-np.arange(half, dtype=np.float32) / np.float32(half))).astype(np.float32)
    inv_row = jnp.asarray(np.tile(inv, 4)[None, :])
    sign_row = jnp.asarray(np.concatenate([-np.ones(64, np.float32), np.ones(64, np.float32)])[None, :])
    return inv_row, sign_row


def _prologue_rows(rows, x_ref, nw_ref, pos_ref, inv_ref, sign_ref, h_ref, cos_ref, sin_ref):
    xv = x_ref[rows, :]
    r = lax.rsqrt(jnp.mean(xv * xv, axis=-1, keepdims=True) + EPS)
    h_ref[rows, :] = ((xv * r) * nw_ref[...]).astype(h_ref.dtype)
    ang = pos_ref[rows, :].astype(F32) * inv_ref[...]
    cos_ref[rows, :] = jnp.cos(ang)
    sin_ref[rows, :] = jnp.sin(ang) * sign_ref[...]


def _proj(h, wft):
    T = h.shape[0]
    tT, tN = T, 512

    def body(h_ref, w_ref, o_ref):
        o_ref[...] = _dot_nt(h_ref[...], w_ref[...])

    return pl.pallas_call(
        body, name="proj", grid=(T // tT, NF // tN),
        in_specs=[pl.BlockSpec((tT, D), lambda i, j: (i, 0)), pl.BlockSpec((tN, D), lambda i, j: (j, 0))],
        out_specs=pl.BlockSpec((tT, tN), lambda i, j: (i, j)),
        out_shape=jax.ShapeDtypeStruct((T, NF), F32),
        compiler_params=_cp(("parallel", "parallel")),
    )(h, wft)


def _swa_masks():
    lane = lax.broadcasted_iota(jnp.int32, (BLK, LANE), 1)
    rope_sub0 = ((lane // 32) % 2) == 0
    std_sub0 = lane < 64
    return lane, rope_sub0, std_sub0


def _swa_valid(n):
    qi = lax.broadcasted_iota(jnp.int32, (BLK, 2 * BLK), 0)
    ki = lax.broadcasted_iota(jnp.int32, (BLK, 2 * BLK), 1)
    rel = qi + BLK - ki
    return (rel >= 0) & (rel < BLK) & ((n > 0) | (ki >= BLK))


def _swa_keys(kc_ref, kp_ref, vc_ref, vp_ref, cq, sq, cp, sp):
    def ropek(kref, c, s):
        kv = kref[...]
        return jnp.concatenate([_rope(kv[:, :LANE], c, s), _rope(kv[:, LANE:], c, s)], axis=1)

    K = jnp.concatenate([ropek(kp_ref, cp, sp), ropek(kc_ref, cq, sq)], axis=0).astype(MXU)
    V = jnp.concatenate([vp_ref[...], vc_ref[...]], axis=0).astype(MXU)
    return K, V


def _swa_in_specs(nb, last):
    def cur(n):
        return jnp.minimum(n, last)

    def prev(n):
        return jnp.maximum(cur(n) - 1, 0)

    kd, vd = C_KD // 256, C_VD // 256
    return [
        pl.BlockSpec((BLK, D), lambda n: (cur(n), C_Q // D)),
        pl.BlockSpec((BLK, 256), lambda n: (cur(n), kd)),
        pl.BlockSpec((BLK, 256), lambda n: (prev(n), kd)),
        pl.BlockSpec((BLK, 256), lambda n: (cur(n), vd)),
        pl.BlockSpec((BLK, 256), lambda n: (prev(n), vd)),
        pl.BlockSpec((BLK, LANE), lambda n: (cur(n), 0)),
        pl.BlockSpec((BLK, LANE), lambda n: (cur(n), 0)),
        pl.BlockSpec((BLK, LANE), lambda n: (prev(n), 0)),
        pl.BlockSpec((BLK, LANE), lambda n: (prev(n), 0)),
    ]


def _swa_fwd(proj, cos, sin, sinks):
    T = proj.shape[0]
    nb = T // BLK
    scale = A_HD ** -0.5

    def body(sinks_ref, q_ref, kc_ref, kp_ref, vc_ref, vp_ref, cq_ref, sq_ref, cp_ref, sp_ref, o_ref, l_ref):
        n = pl.program_id(0)
        cq, sq = cq_ref[...], sq_ref[...]
        K, V = _swa_keys(kc_ref, kp_ref, vc_ref, vp_ref, cq, sq, cp_ref[...], sp_ref[...])
        valid = _swa_valid(n)
        lane, rope_sub0, std_sub0 = _swa_masks()
        lacc = jnp.zeros((BLK, LANE), F32)
        for pb in range(A_HEADS // 2):
            g = pb // (A_HEADS // 2 // A_KV)
            Kg, Vg = K[:, g * LANE:(g + 1) * LANE], V[:, g * LANE:(g + 1) * LANE]
            qp = _rope(q_ref[:, pb * LANE:(pb + 1) * LANE], cq, sq)
            outs = []
            for sub in range(2):
                head = 2 * pb + sub
                qm = jnp.where(rope_sub0 if sub == 0 else ~rope_sub0, qp, 0.0).astype(MXU)
                s = jnp.where(valid, _dot_nt(qm, Kg) * scale, NEG)
                sink = sinks_ref[0, head]
                m = jnp.maximum(jnp.max(s, axis=1, keepdims=True), sink)
                e = jnp.exp(s - m)
                den = jnp.sum(e, axis=1, keepdims=True) + jnp.exp(sink - m)
                p = e / den
                outs.append(_dot(p.astype(MXU), Vg))
                lacc = jnp.where(lane == head, m + jnp.log(den), lacc)
            o_ref[:, pb * LANE:(pb + 1) * LANE] = jnp.where(std_sub0, outs[0], outs[1])
        l_ref[...] = lacc

    return pl.pallas_call(
        body, name="swa_fwd", grid=(nb,),
        in_specs=[pl.BlockSpec(memory_space=pltpu.SMEM)] + _swa_in_specs(nb, nb - 1),
        out_specs=[pl.BlockSpec((BLK, D), lambda n: (n, 0)), pl.BlockSpec((BLK, LANE), lambda n: (n, 0))],
        out_shape=[jax.ShapeDtypeStruct((T, D), F32), jax.ShapeDtypeStruct((T, LANE), F32)],
        compiler_params=_cp(("parallel",)),
    )(sinks, proj, proj, proj, proj, proj, cos, sin, cos, sin)


def _swa_bwd(proj, cos, sin, sinks, do_a, o_a, lse):
    T = proj.shape[0]
    nb = T // BLK
    scale = A_HD ** -0.5

    def body(sinks_ref, q_ref, kc_ref, kp_ref, vc_ref, vp_ref, cq_ref, sq_ref, cp_ref, sp_ref,
             do_ref, o_ref, l_ref, dq_ref, dkv_ref, ds_ref, ckv_ref):
        n = pl.program_id(0)

        @pl.when(n == 0)
        def _():
            ckv_ref[...] = jnp.zeros_like(ckv_ref)
            ds_ref[...] = jnp.zeros_like(ds_ref)

        @pl.when(n < nb)
        def _():
            cq, sq, cp, sp = cq_ref[...], sq_ref[...], cp_ref[...], sp_ref[...]
            K, V = _swa_keys(kc_ref, kp_ref, vc_ref, vp_ref, cq, sq, cp, sp)
            valid = _swa_valid(n)
            lane, rope_sub0, std_sub0 = _swa_masks()
            lane_row = lax.broadcasted_iota(jnp.int32, (1, LANE), 1)
            lse_v = l_ref[...]
            dK = [jnp.zeros((2 * BLK, LANE), F32) for _ in range(A_KV)]
            dV = [jnp.zeros((2 * BLK, LANE), F32) for _ in range(A_KV)]
            dsink = jnp.zeros((1, LANE), F32)
            for pb in range(A_HEADS // 2):
                g = pb // (A_HEADS // 2 // A_KV)
                Kg, Vg = K[:, g * LANE:(g + 1) * LANE], V[:, g * LANE:(g + 1) * LANE]
                qp = _rope(q_ref[:, pb * LANE:(pb + 1) * LANE], cq, sq)
                dop = do_ref[:, pb * LANE:(pb + 1) * LANE]
                op = o_ref[:, pb * LANE:(pb + 1) * LANE]
                dqs = []
                for sub in range(2):
                    head = 2 * pb + sub
                    rmask = rope_sub0 if sub == 0 else ~rope_sub0
                    smask = std_sub0 if sub == 0 else ~std_sub0
                    qm = jnp.where(rmask, qp, 0.0).astype(MXU)
                    lh = jnp.sum(jnp.where(lane == head, lse_v, 0.0), axis=1, keepdims=True)
                    s = _dot_nt(qm, Kg) * scale
                    p = jnp.where(valid, jnp.exp(s - lh), 0.0)
                    dov = jnp.where(smask, dop, 0.0)
                    delta = jnp.sum(dov * op, axis=1, keepdims=True)
                    dovb = dov.astype(MXU)
                    dp = _dot_nt(dovb, Vg)
                    dsc = ((p * (dp - delta)) * scale).astype(MXU)
                    psink = jnp.exp(sinks_ref[0, head] - lh)
                    dsink = jnp.where(lane_row == head, jnp.sum(-psink * delta, axis=0, keepdims=True), dsink)
                    dqs.append(_dot(dsc, Kg))
                    dK[g] = dK[g] + _dot_tn(dsc, qm)
                    dV[g] = dV[g] + _dot_tn(p.astype(MXU), dovb)
                dqp = jnp.where(rope_sub0, dqs[0], dqs[1])
                dq_ref[:, pb * LANE:(pb + 1) * LANE] = _rope_bwd(dqp, cq, sq).astype(dq_ref.dtype)
            prev = ([_rope_bwd(dK[g][:BLK], cp, sp) for g in range(A_KV)] + [dV[g][:BLK] for g in range(A_KV)])
            cur_ = ([_rope_bwd(dK[g][BLK:], cq, sq) for g in range(A_KV)] + [dV[g][BLK:] for g in range(A_KV)])
            dkv_ref[...] = (ckv_ref[...] + jnp.concatenate(prev, axis=1)).astype(dkv_ref.dtype)
            ckv_ref[...] = jnp.concatenate(cur_, axis=1)
            ds_ref[...] = ds_ref[...] + jnp.broadcast_to(dsink, ds_ref.shape)

        @pl.when(n == nb)
        def _():
            dkv_ref[...] = ckv_ref[...].astype(dkv_ref.dtype)

    last = nb - 1

    def cur(n):
        return jnp.minimum(n, last)

    def out_kv(n):
        return (jnp.maximum(n - 1, 0), 0)

    return pl.pallas_call(
        body, name="swa_bwd", grid=(nb + 1,),
        in_specs=[pl.BlockSpec(memory_space=pltpu.SMEM)] + _swa_in_specs(nb, last) + [
            pl.BlockSpec((BLK, D), lambda n: (cur(n), 0)),
            pl.BlockSpec((BLK, D), lambda n: (cur(n), 0)),
            pl.BlockSpec((BLK, LANE), lambda n: (cur(n), 0)),
        ],
        out_specs=[
            pl.BlockSpec((BLK, D), lambda n: (cur(n), 0)),
            pl.BlockSpec((BLK, 512), out_kv),
            pl.BlockSpec((8, LANE), lambda n: (0, 0)),
        ],
        out_shape=[
            jax.ShapeDtypeStruct((T, D), MXU),
            jax.ShapeDtypeStruct((T, 512), MXU),
            jax.ShapeDtypeStruct((8, LANE), F32),
        ],
        scratch_shapes=[pltpu.VMEM((BLK, 512), F32)],
        compiler_params=_cp(("arbitrary",)),
    )(sinks, proj, proj, proj, proj, proj, cos, sin, cos, sin, do_a, o_a, lse)


def _gla_gate(bl_ref, gu_ref, bias_ref):
    gk = _dot(bl_ref[...].astype(MXU), gu_ref[...]) + bias_ref[...]
    la = (jnp.minimum(gk, 0.0) - jnp.log(1.0 + jnp.exp(-jnp.abs(gk)))) / TAU
    ri = lax.broadcasted_iota(jnp.int32, (CHUNK, CHUNK), 0)
    ci = lax.broadcasted_iota(jnp.int32, (CHUNK, CHUNK), 1)
    b = _dot_f32(jnp.where(ci <= ri, 1.0, 0.0).astype(F32), la)
    return gk, la, b, ri, ci


def _gla_head(q_ref, k_ref, la, b, h):
    sl = slice(h * B_DK, (h + 1) * B_DK)
    bh = b[:, sl]
    blast = jnp.sum(la[:, sl], axis=0, keepdims=True)
    qc = q_ref[:, sl] * (B_DK ** -0.5)
    kh = k_ref[:, sl]
    eb, enb, esb = jnp.exp(bh), jnp.exp(-bh), jnp.exp(blast - bh)
    return qc * eb, kh * enb, kh * esb, eb, enb, esb, jnp.exp(blast)


def _gla_specs(chunk_of):
    return [
        pl.BlockSpec((CHUNK, 512), lambda i: (chunk_of(i), C_BQ // 512)),
        pl.BlockSpec((CHUNK, 512), lambda i: (chunk_of(i), C_BK // 512)),
        pl.BlockSpec((CHUNK, D), lambda i: (chunk_of(i), C_BV // D)),
        pl.BlockSpec((CHUNK, W_BL), lambda i: (chunk_of(i), C_BL // W_BL)),
        pl.BlockSpec((W_BL, 512), lambda i: (0, 0)),
        pl.BlockSpec((1, 512), lambda i: (0, 0)),
    ]


def _gla_fwd(proj, gu_pad, bias):
    T = proj.shape[0]
    nc = T // CHUNK

    def body(q_ref, k_ref, v_ref, bl_ref, gu_ref, bias_ref, o_ref, st_ref, state_ref):
        @pl.when(pl.program_id(0) == 0)
        def _():
            state_ref[...] = jnp.zeros_like(state_ref)

        _, la, b, ri, ci = _gla_gate(bl_ref, gu_ref, bias_ref)
        st_ref[...] = state_ref[...]
        for h in range(B_HEADS):
            q_e, k_e, k_s, _, _, _, decay = _gla_head(q_ref, k_ref, la, b, h)
            vh = v_ref[:, h * B_DV:(h + 1) * B_DV].astype(MXU)
            rows = slice(h * B_DV, (h + 1) * B_DV)
            q_eb = q_e.astype(MXU)
            att = jnp.where(ci <= ri, _dot_nt(q_eb, k_e.astype(MXU)), 0.0)
            st = state_ref[rows, :]
            o_ref[:, rows] = _dot(att.astype(MXU), vh) + _dot_nt(q_eb, st.astype(MXU))
            state_ref[rows, :] = st * decay + _dot_tn(vh, k_s.astype(MXU))

    return pl.pallas_call(
        body, name="gla_fwd", grid=(nc,),
        in_specs=_gla_specs(lambda i: i),
        out_specs=[pl.BlockSpec((CHUNK, D), lambda i: (i, 0)),
                   pl.BlockSpec((B_HEADS * B_DV, B_DK), lambda i: (i, 0))],
        out_shape=[jax.ShapeDtypeStruct((T, D), F32),
                   jax.ShapeDtypeStruct((nc * B_HEADS * B_DV, B_DK), F32)],
        scratch_shapes=[pltpu.VMEM((B_HEADS * B_DV, B_DK), F32)],
        compiler_params=_cp(("arbitrary",)),
    )(proj, proj, proj, proj, gu_pad, bias)


def _gla_bwd(proj, gu_pad, bias, states, do_b):
    T = proj.shape[0]
    nc = T // CHUNK
    o_q, o_k = C_BQ - C_GLA, C_BK - C_GLA

    def body(q_ref, k_ref, v_ref, bl_ref, gu_ref, bias_ref, st_ref, do_ref,
             dg_ref, dbl_ref, ggu_ref, gbias_ref, gt_ref):
        @pl.when(pl.program_id(0) == 0)
        def _():
            gt_ref[...] = jnp.zeros_like(gt_ref)
            ggu_ref[...] = jnp.zeros_like(ggu_ref)
            gbias_ref[...] = jnp.zeros_like(gbias_ref)

        gk, la, b, ri, ci = _gla_gate(bl_ref, gu_ref, bias_ref)
        causal = ci <= ri
        upper = jnp.where(ci >= ri, 1.0, 0.0).astype(F32)
        dla_parts = []
        for h in range(B_HEADS):
            q_e, k_e, k_s, eb, enb, esb, decay = _gla_head(q_ref, k_ref, la, b, h)
            rows = slice(h * B_DV, (h + 1) * B_DV)
            sl = slice(h * B_DK, (h + 1) * B_DK)
            vh = v_ref[:, rows].astype(MXU)
            doh = do_ref[:, rows].astype(MXU)
            q_eb, k_eb, k_sb = q_e.astype(MXU), k_e.astype(MXU), k_s.astype(MXU)
            st = st_ref[rows, :]
            gt = gt_ref[rows, :]
            gtb = gt.astype(MXU)
            att = jnp.where(causal, _dot_nt(q_eb, k_eb), 0.0).astype(MXU)
            datt = jnp.where(causal, _dot_nt(doh, vh), 0.0).astype(MXU)
            dq_e = _dot(datt, k_eb) + _dot(doh, st.astype(MXU))
            dk_e = _dot_tn(datt, q_eb)
            dk_s = _dot(vh, gtb)
            dg_ref[:, rows] = (_dot_tn(att, doh) + _dot_nt(k_sb, gtb)).astype(dg_ref.dtype)
            ddecay = jnp.sum(gt * st, axis=0, keepdims=True)
            gt_ref[rows, :] = gt * decay + _dot_tn(doh, q_eb)
            dg_ref[:, o_q + h * B_DK:o_q + (h + 1) * B_DK] = (dq_e * eb * (B_DK ** -0.5)).astype(dg_ref.dtype)
            dg_ref[:, o_k + h * B_DK:o_k + (h + 1) * B_DK] = (dk_e * enb + dk_s * esb).astype(dg_ref.dtype)
            dks_ks = dk_s * k_s
            db = dq_e * q_e - dk_e * k_e - dks_ks
            dblast = jnp.sum(dks_ks, axis=0, keepdims=True) + ddecay * decay
            dla_parts.append(_dot_f32(upper, db) + dblast)
        dla = jnp.concatenate(dla_parts, axis=1)
        dgk = dla * (1.0 / TAU) * _sigmoid(-gk)
        dgkb = dgk.astype(MXU)
        dbl_ref[...] = _dot_nt(dgkb, gu_ref[...]).astype(dbl_ref.dtype)
        ggu_ref[...] = ggu_ref[...] + _dot_tn(bl_ref[...].astype(MXU), dgkb)
        gbias_ref[...] = gbias_ref[...] + jnp.broadcast_to(jnp.sum(dgk, axis=0, keepdims=True), gbias_ref.shape)

    def rev(i):
        return nc - 1 - i

    return pl.pallas_call(
        body, name="gla_bwd", grid=(nc,),
        in_specs=_gla_specs(rev) + [
            pl.BlockSpec((B_HEADS * B_DV, B_DK), lambda i: (rev(i), 0)),
            pl.BlockSpec((CHUNK, D), lambda i: (rev(i), 0)),
        ],
        out_specs=[
            pl.BlockSpec((CHUNK, W_GLA), lambda i: (rev(i), 0)),
            pl.BlockSpec((CHUNK, W_BL), lambda i: (rev(i), 0)),
            pl.BlockSpec((W_BL, 512), lambda i: (0, 0)),
            pl.BlockSpec((8, 512), lambda i: (0, 0)),
        ],
        out_shape=[
            jax.ShapeDtypeStruct((T, W_GLA), MXU),
            jax.ShapeDtypeStruct((T, W_BL), MXU),
            jax.ShapeDtypeStruct((W_BL, 512), F32),
            jax.ShapeDtypeStruct((8, 512), F32),
        ],
        scratch_shapes=[pltpu.VMEM((B_HEADS * B_DV, B_DK), F32)],
        compiler_params=_cp(("arbitrary",)),
    )(proj, proj, proj, proj, gu_pad, bias, states, do_b)


def _mid(x, target, proj, o_a, o_b, w_a, w_b, w_out, w_bn4, fnw):
    T = x.shape[0]
    tT = min(T, 128)
    nbuf = 4
    o_ag, o_bg, o_ma, o_mb = (c - C_GATES for c in (C_AG, C_BG, C_MA, C_MB))

    def body(x_ref, t_ref, oa_ref, ob_ref, gates_ref, wa_ref, wb_ref, wo_ref, wbn_ref, fnw_ref,
             dx2_ref, doa_ref, dob_ref, dgates_ref,
             gwa_ref, gwb_ref, gwo_ref, gfn_ref, gbn_ref, loss_ref, buf_ref):
        i = pl.program_id(0)

        @pl.when(i == 0)
        def _():
            for r in (gwa_ref, gwb_ref, gwo_ref, gfn_ref, gbn_ref, loss_ref):
                r[...] = jnp.zeros_like(r)

        rows = pl.ds(pl.multiple_of((i % nbuf) * tT, tT), tT)

        def keep(k, val):
            buf_ref[k, rows, :] = val

        oa, ag = oa_ref[...], gates_ref[:, o_ag:o_ag + D]
        sg_a = _sigmoid(ag)
        silu_a = ag * sg_a
        oag_b = (oa * silu_a).astype(MXU)
        keep(0, oag_b)
        y_a = _dot(oag_b, wa_ref[...])

        ob, bg = ob_ref[...], gates_ref[:, o_bg:o_bg + D]
        rbs, obhats = [], []
        for h in range(B_HEADS):
            obh = ob[:, h * B_DV:(h + 1) * B_DV]
            rb = lax.rsqrt(jnp.mean(obh * obh, axis=-1, keepdims=True) + EPS)
            rbs.append(rb)
            obhats.append(obh * rb)
        obhat = jnp.concatenate(obhats, axis=1)
        wbn = wbn_ref[...]
        obn = obhat * wbn
        sg_b = _sigmoid(bg)
        silu_b = bg * sg_b
        obg_b = (obn * silu_b).astype(MXU)
        keep(1, obg_b)
        y_b = _dot(obg_b, wb_ref[...])

        sa, sb = _sigmoid(gates_ref[:, o_ma:o_ma + D]), _sigmoid(gates_ref[:, o_mb:o_mb + D])
        mg_b = (sa * y_a + sb * y_b).astype(MXU)
        keep(2, mg_b)
        x2 = x_ref[...] + _dot(mg_b, wo_ref[...])
        r2 = lax.rsqrt(jnp.mean(x2 * x2, axis=-1, keepdims=True) + EPS)
        xh2 = x2 * r2
        fw = fnw_ref[...]
        err = xh2 * fw - t_ref[...]
        tok = jnp.mean(err * err, axis=-1, keepdims=True)
        loss_ref[...] = loss_ref[...] + 0.5 * jnp.sum(tok, axis=0, keepdims=True)

        dy = err * (1.0 / D)
        gfn_ref[...] = gfn_ref[...] + jnp.broadcast_to(jnp.sum(dy * xh2, axis=0, keepdims=True), gfn_ref.shape)
        gy = dy * fw
        dx2 = r2 * (gy - xh2 * jnp.mean(gy * xh2, axis=-1, keepdims=True))
        dx2_ref[...] = dx2
        dx2_b = dx2.astype(MXU)
        keep(5, dx2_b)
        dmg = _dot_nt(dx2_b, wo_ref[...])

        dgates_ref[:, o_ma:o_ma + D] = (dmg * y_a * sa * (1.0 - sa)).astype(dgates_ref.dtype)
        dgates_ref[:, o_mb:o_mb + D] = (dmg * y_b * sb * (1.0 - sb)).astype(dgates_ref.dtype)
        dya_b = (dmg * sa).astype(MXU)
        dyb_b = (dmg * sb).astype(MXU)
        keep(3, dya_b)
        keep(4, dyb_b)
        doag = _dot_nt(dya_b, wa_ref[...])
        dobg = _dot_nt(dyb_b, wb_ref[...])

        @pl.when(i % nbuf == nbuf - 1)
        def _():
            gwa_ref[...] = gwa_ref[...] + _dot_tn(buf_ref[0], buf_ref[3])
            gwb_ref[...] = gwb_ref[...] + _dot_tn(buf_ref[1], buf_ref[4])
            gwo_ref[...] = gwo_ref[...] + _dot_tn(buf_ref[2], buf_ref[5])

        doa_ref[...] = doag * silu_a
        dgates_ref[:, o_ag:o_ag + D] = (doag * oa * (sg_a * (1.0 + ag * (1.0 - sg_a)))).astype(dgates_ref.dtype)
        dobn = dobg * silu_b
        dgates_ref[:, o_bg:o_bg + D] = (dobg * obn * (sg_b * (1.0 + bg * (1.0 - sg_b)))).astype(dgates_ref.dtype)
        gg = dobn * wbn
        gbn = jnp.zeros((1, B_DV), F32)
        for h in range(B_HEADS):
            sl = slice(h * B_DV, (h + 1) * B_DV)
            gbn = gbn + jnp.sum(dobn[:, sl] * obhats[h], axis=0, keepdims=True)
            ggh = gg[:, sl]
            dob_ref[:, sl] = rbs[h] * (ggh - obhats[h] * jnp.mean(ggh * obhats[h], axis=-1, keepdims=True))
        gbn_ref[...] = gbn_ref[...] + jnp.broadcast_to(gbn, gbn_ref.shape)

    assert (T // tT) % nbuf == 0
    tile = pl.BlockSpec((tT, D), lambda i: (i, 0))
    row = pl.BlockSpec((1, D), lambda i: (0, 0))
    acc8 = pl.BlockSpec((8, D), lambda i: (0, 0))
    return pl.pallas_call(
        body, name="mid", grid=(T // tT,),
        in_specs=[tile, tile, tile, tile, pl.BlockSpec((tT, W_GATES), lambda i: (i, C_GATES // W_GATES)),
                  _vmem(), _vmem(), _vmem(), row, row],
        out_specs=[tile, tile, tile, pl.BlockSpec((tT, W_GATES), lambda i: (i, 0)), _vmem(), _vmem(), _vmem(),
                   acc8, pl.BlockSpec((8, B_DV), lambda i: (0, 0)), pl.BlockSpec((8, LANE), lambda i: (0, 0))],
        out_shape=[
            jax.ShapeDtypeStruct((T, D), F32),
            jax.ShapeDtypeStruct((T, D), F32),
            jax.ShapeDtypeStruct((T, D), F32),
            jax.ShapeDtypeStruct((T, W_GATES), MXU),
            jax.ShapeDtypeStruct((D, D), F32),
            jax.ShapeDtypeStruct((D, D), F32),
            jax.ShapeDtypeStruct((D, D), F32),
            jax.ShapeDtypeStruct((8, D), F32),
            jax.ShapeDtypeStruct((8, B_DV), F32),
            jax.ShapeDtypeStruct((8, LANE), F32),
        ],
        scratch_shapes=[pltpu.VMEM((6, nbuf * tT, D), MXU)],
        compiler_params=_cp(("arbitrary",)),
    )(x, target, o_a, o_b, proj, w_a, w_b, w_out, w_bn4, fnw)


def _gw_piece(h, dp, idx):
    T, w = dp.shape
    tn = min(w, 512)

    def body(h_ref, dp_ref, o_ref):
        o_ref[...] = _dot_tn(dp_ref[...], h_ref[...])

    return pl.pallas_call(
        body, name=f"gw_in_{idx}", grid=(w // tn,),
        in_specs=[pl.BlockSpec((T, D), lambda j: (0, 0)), pl.BlockSpec((T, tn), lambda j: (0, j))],
        out_specs=pl.BlockSpec((tn, D), lambda j: (j, 0)),
        out_shape=jax.ShapeDtypeStruct((w, D), F32),
        compiler_params=_cp(("parallel",)),
    )(h, dp)


def _chip_copies(s_ref, got_ref, send_sems, recv_sems):
    x, y, c = _place()
    chips = [(1 - x, y), (x, 1 - y), (1 - x, 1 - y)]
    return [pltpu.make_async_remote_copy(
        src_ref=s_ref.at[2 * px + py], dst_ref=got_ref.at[j],
        send_sem=send_sems.at[j], recv_sem=recv_sems.at[j], device_id=(px, py, c), device_id_type=MESH)
        for j, (px, py) in enumerate(chips)]


def _dh_norm(pieces, offsets, wf, x, dx2, norm_w, sums):
    T = x.shape[0]
    tT = min(T, 256)
    widths = [p.shape[1] for p in pieces]
    npc = len(pieces)
    last = T // tT - 1

    def body(*refs):
        dp_refs = refs[:npc]
        wf_ref, x_ref, dx2_ref, nw_ref, s_ref, gx_ref, gnw_ref, got_ref, send_sems, recv_sems = refs[npc:]

        @pl.when(pl.program_id(0) == 0)
        def _():
            gnw_ref[...] = jnp.zeros_like(gnw_ref)
            for cp in _chip_copies(s_ref, got_ref, send_sems, recv_sems):
                cp.start()

        dh = jnp.zeros((tT, D), F32)
        for dp_ref, off, w in zip(dp_refs, offsets, widths):
            dh = dh + _dot(dp_ref[...], wf_ref[off:off + w, :])
        xv = x_ref[...]
        r = lax.rsqrt(jnp.mean(xv * xv, axis=-1, keepdims=True) + EPS)
        xh = xv * r
        gnw_ref[...] = gnw_ref[...] + jnp.broadcast_to(jnp.sum(dh * xh, axis=0, keepdims=True), gnw_ref.shape)
        g = dh * nw_ref[...]
        gx_ref[...] = r * (g - xh * jnp.mean(g * xh, axis=-1, keepdims=True)) + dx2_ref[...]

        @pl.when(pl.program_id(0) == last)
        def _():
            copies = _chip_copies(s_ref, got_ref, send_sems, recv_sems)
            for cp in copies:
                cp.wait_recv()
            for cp in copies:
                cp.wait_send()

    tile = pl.BlockSpec((tT, D), lambda i: (i, 0))
    return pl.pallas_call(
        body, name="dh_norm", grid=(T // tT,),
        in_specs=[pl.BlockSpec((tT, w), lambda i: (i, 0)) for w in widths]
        + [_vmem(), tile, tile, pl.BlockSpec((1, D), lambda i: (0, 0)), _any()],
        out_specs=[tile, pl.BlockSpec((8, D), lambda i: (0, 0)), _any()],
        out_shape=[jax.ShapeDtypeStruct((T, D), F32), jax.ShapeDtypeStruct((8, D), F32),
                   jax.ShapeDtypeStruct((3, ROWS, D), sums.dtype)],
        scratch_shapes=[pltpu.SemaphoreType.DMA((3,)), pltpu.SemaphoreType.DMA((3,))],
        compiler_params=_cp(("arbitrary",)),
    )(*pieces, wf, x, dx2, norm_w, sums)


def _adamw_math(w, g, m, v):
    m = ADAM_B1 * m + (1.0 - ADAM_B1) * g
    v = ADAM_B2 * v + (1.0 - ADAM_B2) * (g * g)
    m_hat = m / (1.0 - ADAM_B1 ** ADAM_STEP)
    v_hat = v / (1.0 - ADAM_B2 ** ADAM_STEP)
    delta = -ADAM_LR * (m_hat / (jnp.sqrt(v_hat) + ADAM_EPS) + ADAM_WD * w)
    return delta, m, v


def _fetch_partials(s_ref, got_ref, buf, sems, r0, nrows):
    x, y, _ = _place()
    cps = [pltpu.make_async_copy(s_ref.at[2 * x + y, pl.ds(r0, nrows)], buf.at[0], sems.at[0])]
    cps += [pltpu.make_async_copy(got_ref.at[j, pl.ds(r0, nrows)], buf.at[1 + j], sems.at[1 + j]) for j in range(3)]
    for cp in cps:
        cp.start()
    for cp in cps:
        cp.wait()


SMALL_AT = dict(norm_w=0, fnw=8, bias=16, bn=24, sinks=32, loss=40)
ROW_AT = (R_IN, R_A, R_B, R_O)


def _finish(w_rows, m_rows, v_rows, ws, ms, vs, gu_w, gu_m, gu_v, small, sums, got):
    names = ["norm_w", "fnw", "bias", "bn", "sinks"]
    widths = [ws[n].shape[1] for n in names]
    shapes = [w.shape for w in w_rows]

    def body(*refs):
        wr_refs, mr_refs, vr_refs = refs[0:4], refs[4:8], refs[8:12]
        refs = refs[12:]
        w_refs, m_refs, v_refs = refs[0:5], refs[5:10], refs[10:15]
        guw_ref, gum_ref, guv_ref, small_ref, s_ref, got_ref = refs[15:21]
        loss_ref = refs[21]
        row_outs = refs[22:38]
        outs = refs[38:58]
        gu_outs = refs[58:62]
        smalls, tot, buf, send_sems, recv_sems, sems = refs[62:]
        x, y, c = _place()
        me_slot = 4 * x + 2 * y + c
        sends = []
        k = 0
        for dx in range(2):
            for dy in range(2):
                for dc in range(2):
                    if dx == 0 and dy == 0 and dc == 0:
                        continue
                    sends.append(pltpu.make_async_remote_copy(
                        src_ref=small_ref, dst_ref=smalls.at[me_slot],
                        send_sem=send_sems.at[k], recv_sem=recv_sems.at[k],
                        device_id=(x ^ dx, y ^ dy, c ^ dc), device_id_type=MESH))
                    k += 1
        for cp in sends:
            cp.start()
        smalls[me_slot] = small_ref[...]
        _fetch_partials(s_ref, got_ref, buf, sems, 0, ROWS)
        for p in range(4):
            n, off = shapes[p][0], ROW_AT[p]
            for cc in range(D // LANE):
                cols = slice(cc * LANE, (cc + 1) * LANE)
                g = buf[0, off:off + n, cols].astype(F32)
                for j in range(1, 4):
                    g = g + buf[j, off:off + n, cols].astype(F32)
                d, nm, nv = _adamw_math(wr_refs[p][:, cols], g, mr_refs[p][:, cols], vr_refs[p][:, cols])
                for o, val in zip(row_outs[4 * p:4 * p + 4], (g, d, nm, nv)):
                    o[:, cols] = val
        for cp in sends:
            cp.wait_recv()
        for cp in sends:
            cp.wait_send()
        acc = smalls[0]
        for d in range(1, NDEV):
            acc = acc + smalls[d]
        tot[...] = acc
        loss_ref[...] = tot[SMALL_AT["loss"]:SMALL_AT["loss"] + 1, 0:1]
        for p, (nm_, wd) in enumerate(zip(names, widths)):
            r = SMALL_AT[nm_]
            g = tot[r:r + 1, 0:wd]
            d, nm, nv = _adamw_math(w_refs[p][...], g, m_refs[p][...], v_refs[p][...])
            for o, val in zip(outs[4 * p:4 * p + 4], (g, d, nm, nv)):
                o[...] = val
        g = buf[0, R_GU:R_GU + RANK, 0:64].astype(F32)
        for j in range(1, 4):
            g = g + buf[j, R_GU:R_GU + RANK, 0:64].astype(F32)
        d, nm, nv = _adamw_math(guw_ref[...], g, gum_ref[...], guv_ref[...])
        for o, val in zip(gu_outs, (g, d, nm, nv)):
            o[...] = val

    out_shape = ([jax.ShapeDtypeStruct((1, 1), F32)]
                 + [jax.ShapeDtypeStruct(s, F32) for s in shapes for _ in range(4)]
                 + [jax.ShapeDtypeStruct((1, wd), F32) for wd in widths for _ in range(4)]
                 + [jax.ShapeDtypeStruct((RANK, 64), F32)] * 4)
    res = pl.pallas_call(
        body, name="finish",
        in_specs=[_vmem()] * 31 + [_any(), _any()],
        out_specs=[_vmem()] * 41,
        out_shape=out_shape,
        scratch_shapes=[pltpu.VMEM((NDEV, SMALL_ROWS, D), F32), pltpu.VMEM((SMALL_ROWS, D), F32),
                        pltpu.VMEM((4, ROWS, D), sums.dtype),
                        pltpu.SemaphoreType.DMA((7,)), pltpu.SemaphoreType.DMA((7,)), pltpu.SemaphoreType.DMA((4,))],
        compiler_params=_cp(),
    )(*w_rows, *m_rows, *v_rows, *[ws[n] for n in names], *[ms[n] for n in names], *[vs[n] for n in names],
      gu_w, gu_m, gu_v, small, sums, got)
    loss = res[0]
    per = {n: tuple(res[17 + 4 * p:21 + 4 * p]) for p, n in enumerate(names)}
    return loss, tuple(res[1:17]), per, tuple(res[37:41])


def _place():
    x, y, c = lax.axis_index("x"), lax.axis_index("y"), lax.axis_index("c")
    return x, y, c


def _gather_blocks(blk, xs, norm_w, pos_col):
    rows, cols = blk.shape
    T = xs.shape[0]
    tT = min(T, 256)
    inv_row, sign_row = _rope_rows()

    def body(x_ref, xs_ref, nw_ref, pos_ref, inv_ref, sign_ref, out_ref, h_ref, cos_ref, sin_ref,
             send_sems, recv_sems, local_sem):
        x, y, c = _place()
        me, sibling = (x, y, c), (x, y, 1 - c)
        chips = [(1 - x, y), (x, 1 - y), (1 - x, 1 - y)]

        def slot(px, py, pc):
            return out_ref.at[4 * px + 2 * py + pc]

        def copy(k, block, to, src=None):
            return pltpu.make_async_remote_copy(
                src_ref=slot(*block) if src is None else src, dst_ref=slot(*block),
                send_sem=send_sems.at[k], recv_sem=recv_sems.at[k], device_id=to, device_id_type=MESH)

        mine = pltpu.make_async_copy(x_ref, slot(*me), local_sem)
        mine.start()
        first = [copy(0, me, sibling, src=x_ref)]
        first += [copy(1 + j, me, (*chip, c), src=x_ref) for j, chip in enumerate(chips)]
        for cp in first:
            cp.start()

        @pl.loop(0, T // tT)
        def _(i):
            rows_i = pl.ds(pl.multiple_of(i * tT, tT), tT)
            _prologue_rows(rows_i, xs_ref, nw_ref, pos_ref, inv_ref, sign_ref, h_ref, cos_ref, sin_ref)

        passed = [copy(4 + j, (*chip, c), sibling) for j, chip in enumerate(chips)]
        for j, chip in enumerate(chips):
            copy(1 + j, (*chip, c), me).wait_recv()
            passed[j].start()
        copy(0, sibling, me).wait_recv()
        for j, chip in enumerate(chips):
            copy(4 + j, (*chip, 1 - c), me).wait_recv()
        for cp in first + passed:
            cp.wait_send()
        mine.wait()

    return pl.pallas_call(
        body, name="gather_weights",
        in_specs=[_any()] + [_vmem()] * 5, out_specs=[_any()] + [_vmem()] * 3,
        out_shape=[jax.ShapeDtypeStruct((NDEV, rows, cols), blk.dtype), jax.ShapeDtypeStruct((T, D), MXU),
                   jax.ShapeDtypeStruct((T, LANE), F32), jax.ShapeDtypeStruct((T, LANE), F32)],
        scratch_shapes=[pltpu.SemaphoreType.DMA((7,)), pltpu.SemaphoreType.DMA((7,)), pltpu.SemaphoreType.DMA],
        compiler_params=_cp(),
    )(blk, xs, norm_w, pos_col, inv_row, sign_row)


def _pair_reduce(packed):
    def body(p_ref, out_ref, got, own, send_sems, recv_sems, own_sems):
        x, y, c = _place()
        sends = [pltpu.make_async_remote_copy(
            src_ref=p_ref.at[2 * chip + (1 - c)], dst_ref=got.at[chip],
            send_sem=send_sems.at[chip], recv_sem=recv_sems.at[chip], device_id=(x, y, 1 - c), device_id_type=MESH)
            for chip in range(4)]
        loads = [pltpu.make_async_copy(p_ref.at[2 * chip + c], own.at[chip], own_sems.at[chip]) for chip in range(4)]
        for cp in sends + loads:
            cp.start()
        for chip in range(4):
            loads[chip].wait()
            sends[chip].wait_recv()
            out_ref[chip] = (own[chip].astype(F32) + got[chip].astype(F32)).astype(out_ref.dtype)
        for cp in sends:
            cp.wait_send()

    return pl.pallas_call(
        body, name="pair_reduce",
        in_specs=[_any()], out_specs=_vmem(),
        out_shape=jax.ShapeDtypeStruct((4, ROWS, D), packed.dtype),
        scratch_shapes=[pltpu.VMEM((4, ROWS, D), packed.dtype), pltpu.VMEM((4, ROWS, D), packed.dtype),
                        pltpu.SemaphoreType.DMA((4,)), pltpu.SemaphoreType.DMA((4,)), pltpu.SemaphoreType.DMA((4,))],
        compiler_params=_cp(),
    )(packed)


def _pad_cols(a, cols):
    return jnp.pad(a, ((0, 0), (0, cols - a.shape[1])))


def _pad_rows(a, rows):
    return jnp.pad(a, ((0, rows - a.shape[0]), (0, 0)))


def _pack_block(w_in_t, w_a_s, w_b_s, w_o_s, gu_s, dtype):
    return jnp.concatenate([
        _pad_rows(w_in_t, SHARD_PAD).astype(dtype), w_a_s.astype(dtype), w_b_s.astype(dtype), w_o_s.astype(dtype),
        _pad_cols(gu_s, D).astype(dtype)], axis=0)


def _build_wft(wt):
    q = wt[0:1024].reshape(8, 2, 2, 32, D).transpose(0, 2, 1, 3, 4).reshape(1024, D)
    k = wt[1024:1152].reshape(2, 2, 1, 32, D)
    kd = jnp.broadcast_to(k, (2, 2, 2, 32, D)).reshape(256, D)
    v = wt[1152:1280].reshape(2, 1, 64, D)
    vd = jnp.broadcast_to(v, (2, 2, 64, D)).reshape(256, D)
    ag, bq, bk = wt[1280:2304], wt[2304:2816], wt[2816:3328]
    bv, bg, bl = wt[3328:4352], wt[4352:5376], wt[5376:5392]
    ma, mb = wt[5392:6416], wt[6416:7440]
    return jnp.concatenate([q, kd, vd, _pad_rows(bl, C_GLA - C_BL), bv, bq, bk, ag, bg, ma, mb], axis=0)


def _unbuild_gwt(gq, gkv, gbl, ggla, ggates):
    q = gq.reshape(8, 2, 2, 32, D).transpose(0, 2, 1, 3, 4).reshape(1024, D)
    k = gkv[:256].reshape(2, 2, 2, 32, D).sum(axis=2).reshape(128, D)
    v = gkv[256:].reshape(2, 2, 64, D).sum(axis=1).reshape(128, D)
    bv, bq, bk = ggla[:1024], ggla[1024:1536], ggla[1536:]
    ag, bg, ma, mb = (ggates[i * D:(i + 1) * D] for i in range(4))
    return jnp.concatenate([q, k, v, ag, bq, bk, bv, bg, gbl[:RANK], ma, mb], axis=0)


def kernel(x, positions, norm_w, w_in, a_sinks, b_gate_up, b_gate_bias, b_out_norm_w, w_a_proj, w_b_proj, w_out, final_norm_w, loss_target, m_norm_w, m_w_in, m_a_sinks, m_b_gate_up, m_b_gate_bias, m_b_out_norm_w, m_w_a_proj, m_w_b_proj, m_w_out, m_final_norm_w, v_norm_w, v_w_in, v_a_sinks, v_b_gate_up, v_b_gate_bias, v_b_out_norm_w, v_w_a_proj, v_w_b_proj, v_w_out, v_final_norm_w):
    T = x.shape[1]
    xs, target = x[0], loss_target[0]
    fnw = final_norm_w.reshape(1, D)
    blk = _pack_block(w_in[0].T, w_a_proj[0], w_b_proj[0], w_out[0], b_gate_up[0], WIRE)
    allw, h, cos, sin = _gather_blocks(blk, xs, norm_w, positions.reshape(T, 1))
    wf = _build_wft(allw[:, :SHARD, :].reshape(IN_WIDTH, D))
    w_a = allw[:, R_A:R_A + 128, :].reshape(D, D)
    w_b = allw[:, R_B:R_B + 128, :].reshape(D, D)
    w_o = allw[:, R_O:R_O + 128, :].reshape(D, D)
    gu = allw[:, R_GU:R_GU + RANK, :64].transpose(1, 0, 2).reshape(RANK, 512)
    gu_pad = _pad_rows(gu, W_BL)

    proj = _proj(h, wf)
    o_a, lse = _swa_fwd(proj, cos, sin, a_sinks)
    o_b, states = _gla_fwd(proj, gu_pad, b_gate_bias)
    (dx2, do_a, do_b, d_gates, g_wa, g_wb, g_wo, g_fn, g_bn, loss_part) = _mid(
        xs, target, proj, o_a, o_b, w_a, w_b, w_o, jnp.tile(b_out_norm_w, (1, B_HEADS)), fnw)
    d_q, d_kv, g_sinks = _swa_bwd(proj, cos, sin, a_sinks, do_a, o_a, lse)
    d_gla, d_bl, g_gu, g_bias = _gla_bwd(proj, gu_pad, b_gate_bias, states, do_b)
    pieces = [d_q, d_kv, d_bl, d_gla, d_gates]
    offsets = [C_Q, C_KD, C_BL, C_GLA, C_GATES]
    gw = [_gw_piece(h, dp, nm) for nm, dp in zip(["q", "kv", "bl", "gla", "gates"], pieces)]

    gin = _unbuild_gwt(*gw).reshape(NDEV, SHARD, D)
    ggu = g_gu[:RANK].reshape(RANK, NDEV, 64).transpose(1, 0, 2)
    packed = jnp.concatenate([
        jnp.pad(gin, ((0, 0), (0, SHARD_PAD - SHARD), (0, 0))).astype(WIRE),
        g_wa.reshape(NDEV, 128, D).astype(WIRE),
        g_wb.reshape(NDEV, 128, D).astype(WIRE),
        g_wo.reshape(NDEV, 128, D).astype(WIRE),
        jnp.pad(ggu, ((0, 0), (0, 0), (0, D - 64))).astype(WIRE),
    ], axis=1)
    sums = _pair_reduce(packed)
    grad_x, g_nw, from_chips = _dh_norm(pieces, offsets, wf, xs, dx2, norm_w, sums)

    small = jnp.concatenate([g_nw, g_fn, _pad_cols(g_bias, D), _pad_cols(g_bn, D), _pad_cols(g_sinks, D),
                             _pad_cols(loss_part, D)], axis=0)
    ws = dict(norm_w=norm_w, fnw=fnw, bias=b_gate_bias, bn=b_out_norm_w, sinks=a_sinks)
    ms = dict(norm_w=m_norm_w, fnw=m_final_norm_w.reshape(1, D), bias=m_b_gate_bias, bn=m_b_out_norm_w,
              sinks=m_a_sinks)
    vs = dict(norm_w=v_norm_w, fnw=v_final_norm_w.reshape(1, D), bias=v_b_gate_bias, bn=v_b_out_norm_w,
              sinks=v_a_sinks)
    loss, t_rows, sm, t_gu = _finish(
        [w_in[0].T, w_a_proj[0], w_b_proj[0], w_out[0]], [m_w_in[0].T, m_w_a_proj[0], m_w_b_proj[0], m_w_out[0]],
        [v_w_in[0].T, v_w_a_proj[0], v_w_b_proj[0], v_w_out[0]],
        ws, ms, vs, b_gate_up[0], m_b_gate_up[0], v_b_gate_up[0], small, sums, from_chips)

    def outputs(k):
        return [sm["norm_w"][k], t_rows[k].T[None], sm["sinks"][k], t_gu[k][None], sm["bias"][k], sm["bn"][k],
                t_rows[4 + k][None], t_rows[8 + k][None], t_rows[12 + k][None], sm["fnw"][k].reshape(D)]

    return (loss[0, 0], grad_x[None], *outputs(0), *outputs(1), *outputs(2), *outputs(3))
```

```python
import functools

import numpy as np
import jax
import jax.numpy as jnp
from jax import lax
from jax.experimental import pallas as pl
from jax.experimental.pallas import tpu as pltpu

F32 = jnp.float32
MXU = jnp.bfloat16
WIRE = jnp.bfloat16

D = 1024
A_HEADS, A_KV, A_HD = 16, 2, 64
BLK = 128
B_HEADS, B_DK, B_DV = 4, 128, 256
RANK, TAU, CHUNK = 16, 16.0, 64
EPS, NEG = 1e-5, -1e30
ROPE_THETA = 10000.0
IN_WIDTH, NDEV = 7440, 8
SHARD = IN_WIDTH // NDEV
LANE = 128

C_Q, C_KD, C_VD, C_BL = 0, 1024, 1280, 1536
C_BV, C_BQ, C_BK = 2048, 3072, 3584
C_AG, C_BG, C_MA, C_MB = 4096, 5120, 6144, 7168
C_GLA, W_GLA, C_GATES, W_GATES = 2048, 2048, 4096, 4096
NF = 8192
W_BL = 128

SHARD_PAD = 944
R_IN, R_A, R_B, R_O, R_GU, ROWS = 0, 944, 1072, 1200, 1328, 1344
SMALL_ROWS = 48

ADAM_LR, ADAM_B1, ADAM_B2, ADAM_EPS, ADAM_WD, ADAM_STEP = 0.001, 0.9, 0.999, 1e-08, 0.01, 10

MESH = pl.DeviceIdType.MESH
VMEM_LIMIT = 56 * 1024 * 1024


def _cp(sem=None, **kw):
    if sem is not None:
        kw["dimension_semantics"] = sem
    return pltpu.CompilerParams(vmem_limit_bytes=VMEM_LIMIT, **kw)


def _dot(a, b):
    return jnp.dot(a, b, preferred_element_type=F32)


def _dot_nt(a, b):
    return lax.dot_general(a, b, (((1,), (1,)), ((), ())), preferred_element_type=F32)


def _dot_tn(a, b):
    return lax.dot_general(a, b, (((0,), (0,)), ((), ())), preferred_element_type=F32)


def _dot_f32(a, b):
    return jnp.dot(a, b, preferred_element_type=F32, precision=lax.Precision.HIGHEST)


def _sigmoid(z):
    return 0.5 * jnp.tanh(0.5 * z) + 0.5


def _rope(xp, cos, sin):
    return xp * cos + pltpu.roll(xp, 64, 1) * sin


def _rope_bwd(dy, cos, sin):
    return dy * cos - pltpu.roll(dy, 64, 1) * sin


def _vmem():
    return pl.BlockSpec(memory_space=pltpu.VMEM)


def _any():
    return pl.BlockSpec(memory_space=pl.ANY)


def _rope_rows():
    half = A_HD // 2
    inv = (np.float32(ROPE_THETA) ** (-np.arange(half, dtype=np.float32) / np.float32(half))).astype(np.float32)
    inv_row = jnp.asarray(np.tile(inv, 4)[None, :])
    sign_row = jnp.asarray(np.concatenate([-np.ones(64, np.float32), np.ones(64, np.float32)])[None, :])
    return inv_row, sign_row


def _prologue_rows(rows, x_ref, nw_ref, pos_ref, inv_ref, sign_ref, h_ref, cos_ref, sin_ref):
    xv = x_ref[rows, :]
    r = lax.rsqrt(jnp.mean(xv * xv, axis=-1, keepdims=True) + EPS)
    h_ref[rows, :] = ((xv * r) * nw_ref[...]).astype(h_ref.dtype)
    ang = pos_ref[rows, :].astype(F32) * inv_ref[...]
    cos_ref[rows, :] = jnp.cos(ang)
    sin_ref[rows, :] = jnp.sin(ang) * sign_ref[...]


def _proj(h, wft):
    T = h.shape[0]
    tT, tN = T, 512

    def body(h_ref, w_ref, o_ref):
        o_ref[...] = _dot_nt(h_ref[...], w_ref[...])

    return pl.pallas_call(
        body, name="proj", grid=(T // tT, NF // tN),
        in_specs=[pl.BlockSpec((tT, D), lambda i, j: (i, 0)), pl.BlockSpec((tN, D), lambda i, j: (j, 0))],
        out_specs=pl.BlockSpec((tT, tN), lambda i, j: (i, j)),
        out_shape=jax.ShapeDtypeStruct((T, NF), F32),
        compiler_params=_cp(("parallel", "parallel")),
    )(h, wft)


def _swa_masks():
    lane = lax.broadcasted_iota(jnp.int32, (BLK, LANE), 1)
    rope_sub0 = ((lane // 32) % 2) == 0
    std_sub0 = lane < 64
    return lane, rope_sub0, std_sub0


def _swa_tri():
    qi = lax.broadcasted_iota(jnp.int32, (BLK, BLK), 0)
    kj = lax.broadcasted_iota(jnp.int32, (BLK, BLK), 1)
    return kj <= qi


def _swa_fold(full, tri):
    return jnp.where(tri, full[:, BLK:], full[:, :BLK])


def _swa_unfold(sq, tri):
    return jnp.concatenate([jnp.where(tri, 0.0, sq), jnp.where(tri, sq, 0.0)], axis=1)


def _swa_keys(kc_ref, kp_ref, vc_ref, vp_ref, cq, sq, cp, sp):
    def ropek(kref, c, s):
        kv = kref[...]
        return jnp.concatenate([_rope(kv[:, :LANE], c, s), _rope(kv[:, LANE:], c, s)], axis=1)

    K = jnp.concatenate([ropek(kp_ref, cp, sp), ropek(kc_ref, cq, sq)], axis=0).astype(MXU)
    V = jnp.concatenate([vp_ref[...], vc_ref[...]], axis=0).astype(MXU)
    return K, V


def _swa_in_specs(nb, last):
    def cur(n):
        return jnp.minimum(n, last)

    def prev(n):
        return jnp.maximum(cur(n) - 1, 0)

    kd, vd = C_KD // 256, C_VD // 256
    return [
        pl.BlockSpec((BLK, D), lambda n: (cur(n), C_Q // D)),
        pl.BlockSpec((BLK, 256), lambda n: (cur(n), kd)),
        pl.BlockSpec((BLK, 256), lambda n: (prev(n), kd)),
        pl.BlockSpec((BLK, 256), lambda n: (cur(n), vd)),
        pl.BlockSpec((BLK, 256), lambda n: (prev(n), vd)),
        pl.BlockSpec((BLK, LANE), lambda n: (cur(n), 0)),
        pl.BlockSpec((BLK, LANE), lambda n: (cur(n), 0)),
        pl.BlockSpec((BLK, LANE), lambda n: (prev(n), 0)),
        pl.BlockSpec((BLK, LANE), lambda n: (prev(n), 0)),
    ]


def _swa_fwd(proj, cos, sin, sinks):
    T = proj.shape[0]
    nb = T // BLK
    scale = A_HD ** -0.5

    def body(sinks_ref, q_ref, kc_ref, kp_ref, vc_ref, vp_ref, cq_ref, sq_ref, cp_ref, sp_ref, o_ref, l_ref):
        n = pl.program_id(0)
        cq, sq = cq_ref[...], sq_ref[...]
        K, V = _swa_keys(kc_ref, kp_ref, vc_ref, vp_ref, cq, sq, cp_ref[...], sp_ref[...])
        tri = _swa_tri()
        valid = tri | (n > 0)
        lane, rope_sub0, std_sub0 = _swa_masks()
        group = A_HEADS // A_KV
        roped, lses = {}, []

        def products(head):
            pb, sub, g = head // 2, head % 2, head // group
            if sub == 0:
                roped[pb] = _rope(q_ref[:, pb * LANE:(pb + 1) * LANE], cq, sq)
            qm = jnp.where(rope_sub0 if sub == 0 else ~rope_sub0, roped[pb], 0.0).astype(MXU)
            return _dot_nt(qm, K[:, g * LANE:(g + 1) * LANE])

        def softmax(head, s_full):
            s = jnp.where(valid, _swa_fold(s_full, tri) * scale, NEG)
            sink = sinks_ref[0, head]
            m = jnp.maximum(jnp.max(s, axis=1, keepdims=True), sink)
            e = jnp.exp(s - m)
            den = jnp.sum(e, axis=1, keepdims=True) + jnp.exp(sink - m)
            lses.append(m + jnp.log(den))
            return _swa_unfold(e / den, tri).astype(MXU)

        outs = {}
        st1 = {0: products(0), 1: products(1)}
        st2 = {0: softmax(0, st1.pop(0))}
        for head in range(A_HEADS):
            if head + 2 < A_HEADS:
                st1[head + 2] = products(head + 2)
            if head + 1 < A_HEADS:
                st2[head + 1] = softmax(head + 1, st1.pop(head + 1))
            g = head // group
            outs[head] = _dot(st2.pop(head), V[:, g * LANE:(g + 1) * LANE])
            if head % 2 == 1:
                pb = head // 2
                o_ref[:, pb * LANE:(pb + 1) * LANE] = jnp.where(std_sub0, outs[head - 1], outs[head])
        lacc = jnp.zeros((BLK, LANE), F32)
        for head in range(A_HEADS):
            lacc = jnp.where(lane == head, lses[head], lacc)
        l_ref[...] = lacc

    return pl.pallas_call(
        body, name="swa_fwd", grid=(nb,),
        in_specs=[pl.BlockSpec(memory_space=pltpu.SMEM)] + _swa_in_specs(nb, nb - 1),
        out_specs=[pl.BlockSpec((BLK, D), lambda n: (n, 0)), pl.BlockSpec((BLK, LANE), lambda n: (n, 0))],
        out_shape=[jax.ShapeDtypeStruct((T, D), F32), jax.ShapeDtypeStruct((T, LANE), F32)],
        compiler_params=_cp(("parallel",)),
    )(sinks, proj, proj, proj, proj, proj, cos, sin, cos, sin)


def _swa_bwd(proj, cos, sin, sinks, do_a, o_a, lse):
    T = proj.shape[0]
    nb = T // BLK
    scale = A_HD ** -0.5

    def body(sinks_ref, q_ref, kc_ref, kp_ref, vc_ref, vp_ref, cq_ref, sq_ref, cp_ref, sp_ref,
             do_ref, o_ref, l_ref, dq_ref, dkv_ref, ds_ref, ckv_ref):
        n = pl.program_id(0)

        @pl.when(n == 0)
        def _():
            ckv_ref[...] = jnp.zeros_like(ckv_ref)
            ds_ref[...] = jnp.zeros_like(ds_ref)

        @pl.when(n < nb)
        def _():
            cq, sq, cp, sp = cq_ref[...], sq_ref[...], cp_ref[...], sp_ref[...]
            K, V = _swa_keys(kc_ref, kp_ref, vc_ref, vp_ref, cq, sq, cp, sp)
            tri = _swa_tri()
            valid = tri | (n > 0)
            lane, rope_sub0, std_sub0 = _swa_masks()
            lane_row = lax.broadcasted_iota(jnp.int32, (1, LANE), 1)
            lse_v = l_ref[...]
            dKt = [jnp.zeros((LANE, 2 * BLK), F32) for _ in range(A_KV)]
            dVt = [jnp.zeros((LANE, 2 * BLK), F32) for _ in range(A_KV)]
            dsinks, roped, roped_t, do_t = [], {}, {}, {}
            group = A_HEADS // A_KV
            dim = lax.broadcasted_iota(jnp.int32, (LANE, BLK), 0)
            rope_row0, std_row0 = ((dim // 32) % 2) == 0, dim < 64

            def products(head):
                pb, sub, g = head // 2, head % 2, head // group
                cols = slice(pb * LANE, (pb + 1) * LANE)
                Kg, Vg = K[:, g * LANE:(g + 1) * LANE], V[:, g * LANE:(g + 1) * LANE]
                if sub == 0:
                    roped[pb] = _rope(q_ref[:, cols], cq, sq)
                    roped_t[pb] = roped[pb].T
                    do_t[pb] = do_ref[:, cols].T
                qm = jnp.where(rope_sub0 if sub == 0 else ~rope_sub0, roped[pb], 0.0).astype(MXU)
                qmt = jnp.where(rope_row0 if sub == 0 else ~rope_row0, roped_t[pb], 0.0).astype(MXU)
                dov = jnp.where(std_sub0 if sub == 0 else ~std_sub0, do_ref[:, cols], 0.0)
                dovt = jnp.where(std_row0 if sub == 0 else ~std_row0, do_t[pb], 0.0).astype(MXU)
                delta = jnp.sum(dov * o_ref[:, cols], axis=1, keepdims=True)
                return qmt, dovt, delta, _dot_nt(qm, Kg), _dot_nt(dov.astype(MXU), Vg)

            def scores(head, qmt, dovt, delta, s_full, dp_full):
                lh = jnp.sum(jnp.where(lane == head, lse_v, 0.0), axis=1, keepdims=True)
                p = jnp.where(valid, jnp.exp(_swa_fold(s_full, tri) * scale - lh), 0.0)
                psink = jnp.exp(sinks_ref[0, head] - lh)
                dsinks.append(jnp.sum(-psink * delta, axis=0, keepdims=True))
                dsq = (p * (_swa_fold(dp_full, tri) - delta)) * scale
                return qmt, dovt, _swa_unfold(p, tri).astype(MXU), _swa_unfold(dsq, tri).astype(MXU)

            def grads(head, qmt, dovt, pb16, dsc):
                g = head // group
                dKt[g] = dKt[g] + _dot(qmt, dsc)
                dVt[g] = dVt[g] + _dot(dovt, pb16)
                return _dot(dsc, K[:, g * LANE:(g + 1) * LANE])

            dqs = {}
            st1 = {0: products(0), 1: products(1)}
            st2 = {0: scores(0, *st1.pop(0))}
            for head in range(A_HEADS):
                if head + 2 < A_HEADS:
                    st1[head + 2] = products(head + 2)
                if head + 1 < A_HEADS:
                    st2[head + 1] = scores(head + 1, *st1.pop(head + 1))
                dqs[head] = grads(head, *st2.pop(head))
                if head % 2 == 1:
                    pb = head // 2
                    dqp = jnp.where(rope_sub0, dqs[head - 1], dqs[head])
                    dq_ref[:, pb * LANE:(pb + 1) * LANE] = _rope_bwd(dqp, cq, sq).astype(dq_ref.dtype)
            dsink = jnp.zeros((1, LANE), F32)
            for head in range(A_HEADS):
                dsink = jnp.where(lane_row == head, dsinks[head], dsink)
            dK, dV = [a.T for a in dKt], [a.T for a in dVt]
            prev = ([_rope_bwd(dK[g][:BLK], cp, sp) for g in range(A_KV)] + [dV[g][:BLK] for g in range(A_KV)])
            cur_ = ([_rope_bwd(dK[g][BLK:], cq, sq) for g in range(A_KV)] + [dV[g][BLK:] for g in range(A_KV)])
            dkv_ref[...] = (ckv_ref[...] + jnp.concatenate(prev, axis=1)).astype(dkv_ref.dtype)
            ckv_ref[...] = jnp.concatenate(cur_, axis=1)
            ds_ref[...] = ds_ref[...] + jnp.broadcast_to(dsink, ds_ref.shape)

        @pl.when(n == nb)
        def _():
            dkv_ref[...] = ckv_ref[...].astype(dkv_ref.dtype)

    last = nb - 1

    def cur(n):
        return jnp.minimum(n, last)

    def out_kv(n):
        return (jnp.maximum(n - 1, 0), 0)

    return pl.pallas_call(
        body, name="swa_bwd", grid=(nb + 1,),
        in_specs=[pl.BlockSpec(memory_space=pltpu.SMEM)] + _swa_in_specs(nb, last) + [
            pl.BlockSpec((BLK, D), lambda n: (cur(n), 0)),
            pl.BlockSpec((BLK, D), lambda n: (cur(n), 0)),
            pl.BlockSpec((BLK, LANE), lambda n: (cur(n), 0)),
        ],
        out_specs=[
            pl.BlockSpec((BLK, D), lambda n: (cur(n), 0)),
            pl.BlockSpec((BLK, 512), out_kv),
            pl.BlockSpec((8, LANE), lambda n: (0, 0)),
        ],
        out_shape=[
            jax.ShapeDtypeStruct((T, D), MXU),
            jax.ShapeDtypeStruct((T, 512), MXU),
            jax.ShapeDtypeStruct((8, LANE), F32),
        ],
        scratch_shapes=[pltpu.VMEM((BLK, 512), F32)],
        compiler_params=_cp(("arbitrary",)),
    )(sinks, proj, proj, proj, proj, proj, cos, sin, cos, sin, do_a, o_a, lse)


def _gla_gate(bl_ref, gu_ref, bias_ref):
    gk = _dot(bl_ref[...].astype(MXU), gu_ref[...]) + bias_ref[...]
    la = (jnp.minimum(gk, 0.0) - jnp.log(1.0 + jnp.exp(-jnp.abs(gk)))) / TAU
    ri = lax.broadcasted_iota(jnp.int32, (CHUNK, CHUNK), 0)
    ci = lax.broadcasted_iota(jnp.int32, (CHUNK, CHUNK), 1)
    b = _dot_f32(jnp.where(ci <= ri, 1.0, 0.0).astype(F32), la)
    return gk, la, b, ri, ci


def _gla_head(q_ref, k_ref, la, b, h):
    sl = slice(h * B_DK, (h + 1) * B_DK)
    bh = b[:, sl]
    blast = jnp.sum(la[:, sl], axis=0, keepdims=True)
    qc = q_ref[:, sl] * (B_DK ** -0.5)
    kh = k_ref[:, sl]
    eb, enb, esb = jnp.exp(bh), jnp.exp(-bh), jnp.exp(blast - bh)
    return qc * eb, kh * enb, kh * esb, eb, enb, esb, jnp.exp(blast)


def _gla_specs(chunk_of):
    return [
        pl.BlockSpec((CHUNK, 512), lambda i: (chunk_of(i), C_BQ // 512)),
        pl.BlockSpec((CHUNK, 512), lambda i: (chunk_of(i), C_BK // 512)),
        pl.BlockSpec((CHUNK, D), lambda i: (chunk_of(i), C_BV // D)),
        pl.BlockSpec((CHUNK, W_BL), lambda i: (chunk_of(i), C_BL // W_BL)),
        pl.BlockSpec((W_BL, 512), lambda i: (0, 0)),
        pl.BlockSpec((1, 512), lambda i: (0, 0)),
    ]


def _gla_fwd(proj, gu_pad, bias):
    T = proj.shape[0]
    nc = T // CHUNK

    def body(q_ref, k_ref, v_ref, bl_ref, gu_ref, bias_ref, o_ref, st_ref, state_ref):
        @pl.when(pl.program_id(0) == 0)
        def _():
            state_ref[...] = jnp.zeros_like(state_ref)

        _, la, b, ri, ci = _gla_gate(bl_ref, gu_ref, bias_ref)
        st_ref[...] = state_ref[...]
        for h in range(B_HEADS):
            q_e, k_e, k_s, _, _, _, decay = _gla_head(q_ref, k_ref, la, b, h)
            vh = v_ref[:, h * B_DV:(h + 1) * B_DV].astype(MXU)
            rows = slice(h * B_DV, (h + 1) * B_DV)
            q_eb = q_e.astype(MXU)
            att = jnp.where(ci <= ri, _dot_nt(q_eb, k_e.astype(MXU)), 0.0)
            st = state_ref[rows, :]
            o_ref[:, rows] = _dot(att.astype(MXU), vh) + _dot_nt(q_eb, st.astype(MXU))
            state_ref[rows, :] = st * decay + _dot_tn(vh, k_s.astype(MXU))

    return pl.pallas_call(
        body, name="gla_fwd", grid=(nc,),
        in_specs=_gla_specs(lambda i: i),
        out_specs=[pl.BlockSpec((CHUNK, D), lambda i: (i, 0)),
                   pl.BlockSpec((B_HEADS * B_DV, B_DK), lambda i: (i, 0))],
        out_shape=[jax.ShapeDtypeStruct((T, D), F32),
                   jax.ShapeDtypeStruct((nc * B_HEADS * B_DV, B_DK), F32)],
        scratch_shapes=[pltpu.VMEM((B_HEADS * B_DV, B_DK), F32)],
        compiler_params=_cp(("arbitrary",)),
    )(proj, proj, proj, proj, gu_pad, bias)


def _gla_bwd(proj, gu_pad, bias, states, do_b):
    T = proj.shape[0]
    nc = T // CHUNK
    o_q, o_k = C_BQ - C_GLA, C_BK - C_GLA

    def body(q_ref, k_ref, v_ref, bl_ref, gu_ref, bias_ref, st_ref, do_ref,
             dg_ref, dbl_ref, ggu_ref, gbias_ref, gt_ref):
        @pl.when(pl.program_id(0) == 0)
        def _():
            gt_ref[...] = jnp.zeros_like(gt_ref)
            ggu_ref[...] = jnp.zeros_like(ggu_ref)
            gbias_ref[...] = jnp.zeros_like(gbias_ref)

        gk, la, b, ri, ci = _gla_gate(bl_ref, gu_ref, bias_ref)
        causal = ci <= ri
        upper = jnp.where(ci >= ri, 1.0, 0.0).astype(F32)
        dla_parts = []
        for h in range(B_HEADS):
            q_e, k_e, k_s, eb, enb, esb, decay = _gla_head(q_ref, k_ref, la, b, h)
            rows = slice(h * B_DV, (h + 1) * B_DV)
            sl = slice(h * B_DK, (h + 1) * B_DK)
            vh = v_ref[:, rows].astype(MXU)
            doh = do_ref[:, rows].astype(MXU)
            q_eb, k_eb, k_sb = q_e.astype(MXU), k_e.astype(MXU), k_s.astype(MXU)
            st = st_ref[rows, :]
            gt = gt_ref[rows, :]
            gtb = gt.astype(MXU)
            att = jnp.where(causal, _dot_nt(q_eb, k_eb), 0.0).astype(MXU)
            datt = jnp.where(causal, _dot_nt(doh, vh), 0.0).astype(MXU)
            dq_e = _dot(datt, k_eb) + _dot(doh, st.astype(MXU))
            dk_e = _dot_tn(datt, q_eb)
            dk_s = _dot(vh, gtb)
            dg_ref[:, rows] = (_dot_tn(att, doh) + _dot_nt(k_sb, gtb)).astype(dg_ref.dtype)
            ddecay = jnp.sum(gt * st, axis=0, keepdims=True)
            gt_ref[rows, :] = gt * decay + _dot_tn(doh, q_eb)
            dg_ref[:, o_q + h * B_DK:o_q + (h + 1) * B_DK] = (dq_e * eb * (B_DK ** -0.5)).astype(dg_ref.dtype)
            dg_ref[:, o_k + h * B_DK:o_k + (h + 1) * B_DK] = (dk_e * enb + dk_s * esb).astype(dg_ref.dtype)
            dks_ks = dk_s * k_s
            db = dq_e * q_e - dk_e * k_e - dks_ks
            dblast = jnp.sum(dks_ks, axis=0, keepdims=True) + ddecay * decay
            dla_parts.append(_dot_f32(upper, db) + dblast)
        dla = jnp.concatenate(dla_parts, axis=1)
        dgk = dla * (1.0 / TAU) * _sigmoid(-gk)
        dgkb = dgk.astype(MXU)
        dbl_ref[...] = _dot_nt(dgkb, gu_ref[...]).astype(dbl_ref.dtype)
        ggu_ref[...] = ggu_ref[...] + _dot_tn(bl_ref[...].astype(MXU), dgkb)
        gbias_ref[...] = gbias_ref[...] + jnp.broadcast_to(jnp.sum(dgk, axis=0, keepdims=True), gbias_ref.shape)

    def rev(i):
        return nc - 1 - i

    return pl.pallas_call(
        body, name="gla_bwd", grid=(nc,),
        in_specs=_gla_specs(rev) + [
            pl.BlockSpec((B_HEADS * B_DV, B_DK), lambda i: (rev(i), 0)),
            pl.BlockSpec((CHUNK, D), lambda i: (rev(i), 0)),
        ],
        out_specs=[
            pl.BlockSpec((CHUNK, W_GLA), lambda i: (rev(i), 0)),
            pl.BlockSpec((CHUNK, W_BL), lambda i: (rev(i), 0)),
            pl.BlockSpec((W_BL, 512), lambda i: (0, 0)),
            pl.BlockSpec((8, 512), lambda i: (0, 0)),
        ],
        out_shape=[
            jax.ShapeDtypeStruct((T, W_GLA), MXU),
            jax.ShapeDtypeStruct((T, W_BL), MXU),
            jax.ShapeDtypeStruct((W_BL, 512), F32),
            jax.ShapeDtypeStruct((8, 512), F32),
        ],
        scratch_shapes=[pltpu.VMEM((B_HEADS * B_DV, B_DK), F32)],
        compiler_params=_cp(("arbitrary",)),
    )(proj, proj, proj, proj, gu_pad, bias, states, do_b)


def _mid(x, target, proj, o_a, o_b, w_a, w_b, w_out, w_bn4, fnw):
    T = x.shape[0]
    tT = min(T, 128)
    nbuf = 4
    o_ag, o_bg, o_ma, o_mb = (c - C_GATES for c in (C_AG, C_BG, C_MA, C_MB))

    def body(x_ref, t_ref, oa_ref, ob_ref, gates_ref, wa_ref, wb_ref, wo_ref, wbn_ref, fnw_ref,
             dx2_ref, doa_ref, dob_ref, dgates_ref,
             gwa_ref, gwb_ref, gwo_ref, gfn_ref, gbn_ref, loss_ref, buf_ref):
        i = pl.program_id(0)

        @pl.when(i == 0)
        def _():
            for r in (gwa_ref, gwb_ref, gwo_ref, gfn_ref, gbn_ref, loss_ref):
                r[...] = jnp.zeros_like(r)

        rows = pl.ds(pl.multiple_of((i % nbuf) * tT, tT), tT)

        def keep(k, val):
            buf_ref[k, rows, :] = val

        oa, ag = oa_ref[...], gates_ref[:, o_ag:o_ag + D]
        sg_a = _sigmoid(ag)
        silu_a = ag * sg_a
        oag_b = (oa * silu_a).astype(MXU)
        keep(0, oag_b)
        y_a = _dot(oag_b, wa_ref[...])

        ob, bg = ob_ref[...], gates_ref[:, o_bg:o_bg + D]
        rbs, obhats = [], []
        for h in range(B_HEADS):
            obh = ob[:, h * B_DV:(h + 1) * B_DV]
            rb = lax.rsqrt(jnp.mean(obh * obh, axis=-1, keepdims=True) + EPS)
            rbs.append(rb)
            obhats.append(obh * rb)
        obhat = jnp.concatenate(obhats, axis=1)
        wbn = wbn_ref[...]
        obn = obhat * wbn
        sg_b = _sigmoid(bg)
        silu_b = bg * sg_b
        obg_b = (obn * silu_b).astype(MXU)
        keep(1, obg_b)
        y_b = _dot(obg_b, wb_ref[...])

        sa, sb = _sigmoid(gates_ref[:, o_ma:o_ma + D]), _sigmoid(gates_ref[:, o_mb:o_mb + D])
        mg_b = (sa * y_a + sb * y_b).astype(MXU)
        keep(2, mg_b)
        x2 = x_ref[...] + _dot(mg_b, wo_ref[...])
        r2 = lax.rsqrt(jnp.mean(x2 * x2, axis=-1, keepdims=True) + EPS)
        xh2 = x2 * r2
        fw = fnw_ref[...]
        err = xh2 * fw - t_ref[...]
        tok = jnp.mean(err * err, axis=-1, keepdims=True)
        loss_ref[...] = loss_ref[...] + 0.5 * jnp.sum(tok, axis=0, keepdims=True)

        dy = err * (1.0 / D)
        gfn_ref[...] = gfn_ref[...] + jnp.broadcast_to(jnp.sum(dy * xh2, axis=0, keepdims=True), gfn_ref.shape)
        gy = dy * fw
        dx2 = r2 * (gy - xh2 * jnp.mean(gy * xh2, axis=-1, keepdims=True))
        dx2_ref[...] = dx2
        dx2_b = dx2.astype(MXU)
        keep(5, dx2_b)
        dmg = _dot_nt(dx2_b, wo_ref[...])

        dgates_ref[:, o_ma:o_ma + D] = (dmg * y_a * sa * (1.0 - sa)).astype(dgates_ref.dtype)
        dgates_ref[:, o_mb:o_mb + D] = (dmg * y_b * sb * (1.0 - sb)).astype(dgates_ref.dtype)
        dya_b = (dmg * sa).astype(MXU)
        dyb_b = (dmg * sb).astype(MXU)
        keep(3, dya_b)
        keep(4, dyb_b)
        doag = _dot_nt(dya_b, wa_ref[...])
        dobg = _dot_nt(dyb_b, wb_ref[...])

        @pl.when(i % nbuf == nbuf - 1)
        def _():
            gwa_ref[...] = gwa_ref[...] + _dot_tn(buf_ref[0], buf_ref[3])
            gwb_ref[...] = gwb_ref[...] + _dot_tn(buf_ref[1], buf_ref[4])
            gwo_ref[...] = gwo_ref[...] + _dot_tn(buf_ref[2], buf_ref[5])

        doa_ref[...] = doag * silu_a
        dgates_ref[:, o_ag:o_ag + D] = (doag * oa * (sg_a * (1.0 + ag * (1.0 - sg_a)))).astype(dgates_ref.dtype)
        dobn = dobg * silu_b
        dgates_ref[:, o_bg:o_bg + D] = (dobg * obn * (sg_b * (1.0 + bg * (1.0 - sg_b)))).astype(dgates_ref.dtype)
        gg = dobn * wbn
        gbn = jnp.zeros((1, B_DV), F32)
        for h in range(B_HEADS):
            sl = slice(h * B_DV, (h + 1) * B_DV)
            gbn = gbn + jnp.sum(dobn[:, sl] * obhats[h], axis=0, keepdims=True)
            ggh = gg[:, sl]
            dob_ref[:, sl] = rbs[h] * (ggh - obhats[h] * jnp.mean(ggh * obhats[h], axis=-1, keepdims=True))
        gbn_ref[...] = gbn_ref[...] + jnp.broadcast_to(gbn, gbn_ref.shape)

    assert (T // tT) % nbuf == 0
    tile = pl.BlockSpec((tT, D), lambda i: (i, 0))
    row = pl.BlockSpec((1, D), lambda i: (0, 0))
    acc8 = pl.BlockSpec((8, D), lambda i: (0, 0))
    return pl.pallas_call(
        body, name="mid", grid=(T // tT,),
        in_specs=[tile, tile, tile, tile, pl.BlockSpec((tT, W_GATES), lambda i: (i, C_GATES // W_GATES)),
                  _vmem(), _vmem(), _vmem(), row, row],
        out_specs=[tile, tile, tile, pl.BlockSpec((tT, W_GATES), lambda i: (i, 0)), _vmem(), _vmem(), _vmem(),
                   acc8, pl.BlockSpec((8, B_DV), lambda i: (0, 0)), pl.BlockSpec((8, LANE), lambda i: (0, 0))],
        out_shape=[
            jax.ShapeDtypeStruct((T, D), F32),
            jax.ShapeDtypeStruct((T, D), F32),
            jax.ShapeDtypeStruct((T, D), F32),
            jax.ShapeDtypeStruct((T, W_GATES), MXU),
            jax.ShapeDtypeStruct((D, D), F32),
            jax.ShapeDtypeStruct((D, D), F32),
            jax.ShapeDtypeStruct((D, D), F32),
            jax.ShapeDtypeStruct((8, D), F32),
            jax.ShapeDtypeStruct((8, B_DV), F32),
            jax.ShapeDtypeStruct((8, LANE), F32),
        ],
        scratch_shapes=[pltpu.VMEM((6, nbuf * tT, D), MXU)],
        compiler_params=_cp(("arbitrary",)),
    )(x, target, o_a, o_b, proj, w_a, w_b, w_out, w_bn4, fnw)


def _gw_piece(h, dp, idx):
    T, w = dp.shape
    tn = min(w, 512)

    def body(h_ref, dp_ref, o_ref):
        o_ref[...] = _dot_tn(dp_ref[...], h_ref[...])

    return pl.pallas_call(
        body, name=f"gw_in_{idx}", grid=(w // tn,),
        in_specs=[pl.BlockSpec((T, D), lambda j: (0, 0)), pl.BlockSpec((T, tn), lambda j: (0, j))],
        out_specs=pl.BlockSpec((tn, D), lambda j: (j, 0)),
        out_shape=jax.ShapeDtypeStruct((w, D), F32),
        compiler_params=_cp(("parallel",)),
    )(h, dp)


def _chip_copies(s_ref, got_ref, send_sems, recv_sems):
    x, y, c = _place()
    chips = [(1 - x, y), (x, 1 - y), (1 - x, 1 - y)]
    return [pltpu.make_async_remote_copy(
        src_ref=s_ref.at[2 * px + py], dst_ref=got_ref.at[j],
        send_sem=send_sems.at[j], recv_sem=recv_sems.at[j], device_id=(px, py, c), device_id_type=MESH)
        for j, (px, py) in enumerate(chips)]


def _dh_norm(pieces, offsets, wf, x, dx2, norm_w, sums):
    T = x.shape[0]
    tT = min(T, 256)
    widths = [p.shape[1] for p in pieces]
    npc = len(pieces)
    last = T // tT - 1

    def body(*refs):
        dp_refs = refs[:npc]
        wf_ref, x_ref, dx2_ref, nw_ref, s_ref, gx_ref, gnw_ref, got_ref, send_sems, recv_sems = refs[npc:]

        @pl.when(pl.program_id(0) == 0)
        def _():
            gnw_ref[...] = jnp.zeros_like(gnw_ref)
            for cp in _chip_copies(s_ref, got_ref, send_sems, recv_sems):
                cp.start()

        dh = jnp.zeros((tT, D), F32)
        for dp_ref, off, w in zip(dp_refs, offsets, widths):
            dh = dh + _dot(dp_ref[...], wf_ref[off:off + w, :])
        xv = x_ref[...]
        r = lax.rsqrt(jnp.mean(xv * xv, axis=-1, keepdims=True) + EPS)
        xh = xv * r
        gnw_ref[...] = gnw_ref[...] + jnp.broadcast_to(jnp.sum(dh * xh, axis=0, keepdims=True), gnw_ref.shape)
        g = dh * nw_ref[...]
        gx_ref[...] = r * (g - xh * jnp.mean(g * xh, axis=-1, keepdims=True)) + dx2_ref[...]

        @pl.when(pl.program_id(0) == last)
        def _():
            copies = _chip_copies(s_ref, got_ref, send_sems, recv_sems)
            for cp in copies:
                cp.wait_recv()
            for cp in copies:
                cp.wait_send()

    tile = pl.BlockSpec((tT, D), lambda i: (i, 0))
    return pl.pallas_call(
        body, name="dh_norm", grid=(T // tT,),
        in_specs=[pl.BlockSpec((tT, w), lambda i: (i, 0)) for w in widths]
        + [_vmem(), tile, tile, pl.BlockSpec((1, D), lambda i: (0, 0)), _any()],
        out_specs=[tile, pl.BlockSpec((8, D), lambda i: (0, 0)), _any()],
        out_shape=[jax.ShapeDtypeStruct((T, D), F32), jax.ShapeDtypeStruct((8, D), F32),
                   jax.ShapeDtypeStruct((3, ROWS, D), sums.dtype)],
        scratch_shapes=[pltpu.SemaphoreType.DMA((3,)), pltpu.SemaphoreType.DMA((3,))],
        compiler_params=_cp(("arbitrary",)),
    )(*pieces, wf, x, dx2, norm_w, sums)


def _adamw_math(w, g, m, v):
    m = ADAM_B1 * m + (1.0 - ADAM_B1) * g
    v = ADAM_B2 * v + (1.0 - ADAM_B2) * (g * g)
    m_hat = m / (1.0 - ADAM_B1 ** ADAM_STEP)
    v_hat = v / (1.0 - ADAM_B2 ** ADAM_STEP)
    delta = -ADAM_LR * (m_hat / (jnp.sqrt(v_hat) + ADAM_EPS) + ADAM_WD * w)
    return delta, m, v


def _fetch_partials(s_ref, got_ref, buf, sems, r0, nrows):
    x, y, _ = _place()
    cps = [pltpu.make_async_copy(s_ref.at[2 * x + y, pl.ds(r0, nrows)], buf.at[0], sems.at[0])]
    cps += [pltpu.make_async_copy(got_ref.at[j, pl.ds(r0, nrows)], buf.at[1 + j], sems.at[1 + j]) for j in range(3)]
    for cp in cps:
        cp.start()
    for cp in cps:
        cp.wait()


SMALL_AT = dict(norm_w=0, fnw=8, bias=16, bn=24, sinks=32, loss=40)
ROW_AT = (R_IN, R_A, R_B, R_O)


def _finish(w_rows, m_rows, v_rows, ws, ms, vs, gu_w, gu_m, gu_v, small, sums, got):
    names = ["norm_w", "fnw", "bias", "bn", "sinks"]
    widths = [ws[n].shape[1] for n in names]
    shapes = [w.shape for w in w_rows]

    def body(*refs):
        wr_refs, mr_refs, vr_refs = refs[0:4], refs[4:8], refs[8:12]
        refs = refs[12:]
        w_refs, m_refs, v_refs = refs[0:5], refs[5:10], refs[10:15]
        guw_ref, gum_ref, guv_ref, small_ref, s_ref, got_ref = refs[15:21]
        loss_ref = refs[21]
        row_outs = refs[22:38]
        outs = refs[38:58]
        gu_outs = refs[58:62]
        smalls, tot, buf, send_sems, recv_sems, sems = refs[62:]
        x, y, c = _place()
        me_slot = 4 * x + 2 * y + c
        sends = []
        k = 0
        for dx in range(2):
            for dy in range(2):
                for dc in range(2):
                    if dx == 0 and dy == 0 and dc == 0:
                        continue
                    sends.append(pltpu.make_async_remote_copy(
                        src_ref=small_ref, dst_ref=smalls.at[me_slot],
                        send_sem=send_sems.at[k], recv_sem=recv_sems.at[k],
                        device_id=(x ^ dx, y ^ dy, c ^ dc), device_id_type=MESH))
                    k += 1
        for cp in sends:
            cp.start()
        smalls[me_slot] = small_ref[...]
        _fetch_partials(s_ref, got_ref, buf, sems, 0, ROWS)
        for p in range(4):
            n, off = shapes[p][0], ROW_AT[p]
            for cc in range(D // LANE):
                cols = slice(cc * LANE, (cc + 1) * LANE)
                g = buf[0, off:off + n, cols].astype(F32)
                for j in range(1, 4):
                    g = g + buf[j, off:off + n, cols].astype(F32)
                d, nm, nv = _adamw_math(wr_refs[p][:, cols], g, mr_refs[p][:, cols], vr_refs[p][:, cols])
                for o, val in zip(row_outs[4 * p:4 * p + 4], (g, d, nm, nv)):
                    o[:, cols] = val
        for cp in sends:
            cp.wait_recv()
        for cp in sends:
            cp.wait_send()
        acc = smalls[0]
        for d in range(1, NDEV):
            acc = acc + smalls[d]
        tot[...] = acc
        loss_ref[...] = tot[SMALL_AT["loss"]:SMALL_AT["loss"] + 1, 0:1]
        for p, (nm_, wd) in enumerate(zip(names, widths)):
            r = SMALL_AT[nm_]
            g = tot[r:r + 1, 0:wd]
            d, nm, nv = _adamw_math(w_refs[p][...], g, m_refs[p][...], v_refs[p][...])
            for o, val in zip(outs[4 * p:4 * p + 4], (g, d, nm, nv)):
                o[...] = val
        g = buf[0, R_GU:R_GU + RANK, 0:64].astype(F32)
        for j in range(1, 4):
            g = g + buf[j, R_GU:R_GU + RANK, 0:64].astype(F32)
        d, nm, nv = _adamw_math(guw_ref[...], g, gum_ref[...], guv_ref[...])
        for o, val in zip(gu_outs, (g, d, nm, nv)):
            o[...] = val

    out_shape = ([jax.ShapeDtypeStruct((1, 1), F32)]
                 + [jax.ShapeDtypeStruct(s, F32) for s in shapes for _ in range(4)]
                 + [jax.ShapeDtypeStruct((1, wd), F32) for wd in widths for _ in range(4)]
                 + [jax.ShapeDtypeStruct((RANK, 64), F32)] * 4)
    res = pl.pallas_call(
        body, name="finish",
        in_specs=[_vmem()] * 31 + [_any(), _any()],
        out_specs=[_vmem()] * 41,
        out_shape=out_shape,
        scratch_shapes=[pltpu.VMEM((NDEV, SMALL_ROWS, D), F32), pltpu.VMEM((SMALL_ROWS, D), F32),
                        pltpu.VMEM((4, ROWS, D), sums.dtype),
                        pltpu.SemaphoreType.DMA((7,)), pltpu.SemaphoreType.DMA((7,)), pltpu.SemaphoreType.DMA((4,))],
        compiler_params=_cp(),
    )(*w_rows, *m_rows, *v_rows, *[ws[n] for n in names], *[ms[n] for n in names], *[vs[n] for n in names],
      gu_w, gu_m, gu_v, small, sums, got)
    loss = res[0]
    per = {n: tuple(res[17 + 4 * p:21 + 4 * p]) for p, n in enumerate(names)}
    return loss, tuple(res[1:17]), per, tuple(res[37:41])


def _place():
    x, y, c = lax.axis_index("x"), lax.axis_index("y"), lax.axis_index("c")
    return x, y, c


def _gather_blocks(blk, xs, norm_w, pos_col):
    rows, cols = blk.shape
    T = xs.shape[0]
    tT = min(T, 256)
    inv_row, sign_row = _rope_rows()

    def body(x_ref, xs_ref, nw_ref, pos_ref, inv_ref, sign_ref, out_ref, h_ref, cos_ref, sin_ref,
             send_sems, recv_sems, local_sem):
        x, y, c = _place()
        me, sibling = (x, y, c), (x, y, 1 - c)
        chips = [(1 - x, y), (x, 1 - y), (1 - x, 1 - y)]

        def slot(px, py, pc):
            return out_ref.at[4 * px + 2 * py + pc]

        def copy(k, block, to, src=None):
            return pltpu.make_async_remote_copy(
                src_ref=slot(*block) if src is None else src, dst_ref=slot(*block),
                send_sem=send_sems.at[k], recv_sem=recv_sems.at[k], device_id=to, device_id_type=MESH)

        mine = pltpu.make_async_copy(x_ref, slot(*me), local_sem)
        mine.start()
        first = [copy(0, me, sibling, src=x_ref)]
        first += [copy(1 + j, me, (*chip, c), src=x_ref) for j, chip in enumerate(chips)]
        for cp in first:
            cp.start()

        @pl.loop(0, T // tT)
        def _(i):
            rows_i = pl.ds(pl.multiple_of(i * tT, tT), tT)
            _prologue_rows(rows_i, xs_ref, nw_ref, pos_ref, inv_ref, sign_ref, h_ref, cos_ref, sin_ref)

        passed = [copy(4 + j, (*chip, c), sibling) for j, chip in enumerate(chips)]
        for j, chip in enumerate(chips):
            copy(1 + j, (*chip, c), me).wait_recv()
            passed[j].start()
        copy(0, sibling, me).wait_recv()
        for j, chip in enumerate(chips):
            copy(4 + j, (*chip, 1 - c), me).wait_recv()
        for cp in first + passed:
            cp.wait_send()
        mine.wait()

    return pl.pallas_call(
        body, name="gather_weights",
        in_specs=[_any()] + [_vmem()] * 5, out_specs=[_any()] + [_vmem()] * 3,
        out_shape=[jax.ShapeDtypeStruct((NDEV, rows, cols), blk.dtype), jax.ShapeDtypeStruct((T, D), MXU),
                   jax.ShapeDtypeStruct((T, LANE), F32), jax.ShapeDtypeStruct((T, LANE), F32)],
        scratch_shapes=[pltpu.SemaphoreType.DMA((7,)), pltpu.SemaphoreType.DMA((7,)), pltpu.SemaphoreType.DMA],
        compiler_params=_cp(),
    )(blk, xs, norm_w, pos_col, inv_row, sign_row)


def _pair_reduce(packed):
    def body(p_ref, out_ref, got, own, send_sems, recv_sems, own_sems):
        x, y, c = _place()
        sends = [pltpu.make_async_remote_copy(
            src_ref=p_ref.at[2 * chip + (1 - c)], dst_ref=got.at[chip],
            send_sem=send_sems.at[chip], recv_sem=recv_sems.at[chip], device_id=(x, y, 1 - c), device_id_type=MESH)
            for chip in range(4)]
        loads = [pltpu.make_async_copy(p_ref.at[2 * chip + c], own.at[chip], own_sems.at[chip]) for chip in range(4)]
        for cp in sends + loads:
            cp.start()
        for chip in range(4):
            loads[chip].wait()
            sends[chip].wait_recv()
            out_ref[chip] = (own[chip].astype(F32) + got[chip].astype(F32)).astype(out_ref.dtype)
        for cp in sends:
            cp.wait_send()

    return pl.pallas_call(
        body, name="pair_reduce",
        in_specs=[_any()], out_specs=_vmem(),
        out_shape=jax.ShapeDtypeStruct((4, ROWS, D), packed.dtype),
        scratch_shapes=[pltpu.VMEM((4, ROWS, D), packed.dtype), pltpu.VMEM((4, ROWS, D), packed.dtype),
                        pltpu.SemaphoreType.DMA((4,)), pltpu.SemaphoreType.DMA((4,)), pltpu.SemaphoreType.DMA((4,))],
        compiler_params=_cp(),
    )(packed)


def _pad_cols(a, cols):
    return jnp.pad(a, ((0, 0), (0, cols - a.shape[1])))


def _pad_rows(a, rows):
    return jnp.pad(a, ((0, rows - a.shape[0]), (0, 0)))


def _pack_block(w_in_t, w_a_s, w_b_s, w_o_s, gu_s, dtype):
    return jnp.concatenate([
        _pad_rows(w_in_t, SHARD_PAD).astype(dtype), w_a_s.astype(dtype), w_b_s.astype(dtype), w_o_s.astype(dtype),
        _pad_cols(gu_s, D).astype(dtype)], axis=0)


def _build_wft(wt):
    q = wt[0:1024].reshape(8, 2, 2, 32, D).transpose(0, 2, 1, 3, 4).reshape(1024, D)
    k = wt[1024:1152].reshape(2, 2, 1, 32, D)
    kd = jnp.broadcast_to(k, (2, 2, 2, 32, D)).reshape(256, D)
    v = wt[1152:1280].reshape(2, 1, 64, D)
    vd = jnp.broadcast_to(v, (2, 2, 64, D)).reshape(256, D)
    ag, bq, bk = wt[1280:2304], wt[2304:2816], wt[2816:3328]
    bv, bg, bl = wt[3328:4352], wt[4352:5376], wt[5376:5392]
    ma, mb = wt[5392:6416], wt[6416:7440]
    return jnp.concatenate([q, kd, vd, _pad_rows(bl, C_GLA - C_BL), bv, bq, bk, ag, bg, ma, mb], axis=0)


def _unbuild_gwt(gq, gkv, gbl, ggla, ggates):
    q = gq.reshape(8, 2, 2, 32, D).transpose(0, 2, 1, 3, 4).reshape(1024, D)
    k = gkv[:256].reshape(2, 2, 2, 32, D).sum(axis=2).reshape(128, D)
    v = gkv[256:].reshape(2, 2, 64, D).sum(axis=1).reshape(128, D)
    bv, bq, bk = ggla[:1024], ggla[1024:1536], ggla[1536:]
    ag, bg, ma, mb = (ggates[i * D:(i + 1) * D] for i in range(4))
    return jnp.concatenate([q, k, v, ag, bq, bk, bv, bg, gbl[:RANK], ma, mb], axis=0)


def kernel(x, positions, norm_w, w_in, a_sinks, b_gate_up, b_gate_bias, b_out_norm_w, w_a_proj, w_b_proj, w_out, final_norm_w, loss_target, m_norm_w, m_w_in, m_a_sinks, m_b_gate_up, m_b_gate_bias, m_b_out_norm_w, m_w_a_proj, m_w_b_proj, m_w_out, m_final_norm_w, v_norm_w, v_w_in, v_a_sinks, v_b_gate_up, v_b_gate_bias, v_b_out_norm_w, v_w_a_proj, v_w_b_proj, v_w_out, v_final_norm_w):
    T = x.shape[1]
    xs, target = x[0], loss_target[0]
    fnw = final_norm_w.reshape(1, D)
    blk = _pack_block(w_in[0].T, w_a_proj[0], w_b_proj[0], w_out[0], b_gate_up[0], WIRE)
    allw, h, cos, sin = _gather_blocks(blk, xs, norm_w, positions.reshape(T, 1))
    wf = _build_wft(allw[:, :SHARD, :].reshape(IN_WIDTH, D))
    w_a = allw[:, R_A:R_A + 128, :].reshape(D, D)
    w_b = allw[:, R_B:R_B + 128, :].reshape(D, D)
    w_o = allw[:, R_O:R_O + 128, :].reshape(D, D)
    gu = allw[:, R_GU:R_GU + RANK, :64].transpose(1, 0, 2).reshape(RANK, 512)
    gu_pad = _pad_rows(gu, W_BL)

    proj = _proj(h, wf)
    o_a, lse = _swa_fwd(proj, cos, sin, a_sinks)
    o_b, states = _gla_fwd(proj, gu_pad, b_gate_bias)
    (dx2, do_a, do_b, d_gates, g_wa, g_wb, g_wo, g_fn, g_bn, loss_part) = _mid(
        xs, target, proj, o_a, o_b, w_a, w_b, w_o, jnp.tile(b_out_norm_w, (1, B_HEADS)), fnw)
    d_q, d_kv, g_sinks = _swa_bwd(proj, cos, sin, a_sinks, do_a, o_a, lse)
    d_gla, d_bl, g_gu, g_bias = _gla_bwd(proj, gu_pad, b_gate_bias, states, do_b)
    pieces = [d_q, d_kv, d_bl, d_gla, d_gates]
    offsets = [C_Q, C_KD, C_BL, C_GLA, C_GATES]
    gw = [_gw_piece(h, dp, nm) for nm, dp in zip(["q", "kv", "bl", "gla", "gates"], pieces)]

    gin = _unbuild_gwt(*gw).reshape(NDEV, SHARD, D)
    ggu = g_gu[:RANK].reshape(RANK, NDEV, 64).transpose(1, 0, 2)
    packed = jnp.concatenate([
        jnp.pad(gin, ((0, 0), (0, SHARD_PAD - SHARD), (0, 0))).astype(WIRE),
        g_wa.reshape(NDEV, 128, D).astype(WIRE),
        g_wb.reshape(NDEV, 128, D).astype(WIRE),
        g_wo.reshape(NDEV, 128, D).astype(WIRE),
        jnp.pad(ggu, ((0, 0), (0, 0), (0, D - 64))).astype(WIRE),
    ], axis=1)
    sums = _pair_reduce(packed)
    grad_x, g_nw, from_chips = _dh_norm(pieces, offsets, wf, xs, dx2, norm_w, sums)

    small = jnp.concatenate([g_nw, g_fn, _pad_cols(g_bias, D), _pad_cols(g_bn, D), _pad_cols(g_sinks, D),
                             _pad_cols(loss_part, D)], axis=0)
    ws = dict(norm_w=norm_w, fnw=fnw, bias=b_gate_bias, bn=b_out_norm_w, sinks=a_sinks)
    ms = dict(norm_w=m_norm_w, fnw=m_final_norm_w.reshape(1, D), bias=m_b_gate_bias, bn=m_b_out_norm_w,
              sinks=m_a_sinks)
    vs = dict(norm_w=v_norm_w, fnw=v_final_norm_w.reshape(1, D), bias=v_b_gate_bias, bn=v_b_out_norm_w,
              sinks=v_a_sinks)
    loss, t_rows, sm, t_gu = _finish(
        [w_in[0].T, w_a_proj[0], w_b_proj[0], w_out[0]], [m_w_in[0].T, m_w_a_proj[0], m_w_b_proj[0], m_w_out[0]],
        [v_w_in[0].T, v_w_a_proj[0], v_w_b_proj[0], v_w_out[0]],
        ws, ms, vs, b_gate_up[0], m_b_gate_up[0], v_b_gate_up[0], small, sums, from_chips)

    def outputs(k):
        return [sm["norm_w"][k], t_rows[k].T[None], sm["sinks"][k], t_gu[k][None], sm["bias"][k], sm["bn"][k],
                t_rows[4 + k][None], t_rows[8 + k][None], t_rows[12 + k][None], sm["fnw"][k].reshape(D)]

    return (loss[0, 0], grad_x[None], *outputs(0), *outputs(1), *outputs(2), *outputs(3))
```

```python
import functools

import numpy as np
import jax
import jax.numpy as jnp
from jax import lax
from jax.experimental import pallas as pl
from jax.experimental.pallas import tpu as pltpu

F32 = jnp.float32
MXU = jnp.bfloat16
WIRE = jnp.bfloat16

D = 1024
A_HEADS, A_KV, A_HD = 16, 2, 64
BLK = 128
B_HEADS, B_DK, B_DV = 4, 128, 256
RANK, TAU, CHUNK = 16, 16.0, 64
EPS, NEG = 1e-5, -1e30
ROPE_THETA = 10000.0
IN_WIDTH, NDEV = 7440, 8
SHARD = IN_WIDTH // NDEV
LANE = 128

C_Q, C_KD, C_VD, C_BL = 0, 1024, 1280, 1536
C_BV, C_BQ, C_BK = 2048, 3072, 3584
C_AG, C_BG, C_MA, C_MB = 4096, 5120, 6144, 7168
C_GLA, W_GLA, C_GATES, W_GATES = 2048, 2048, 4096, 4096
NF = 8192
W_BL = 128

SHARD_PAD = 944
R_IN, R_A, R_B, R_O, R_GU, ROWS = 0, 944, 1072, 1200, 1328, 1344
SMALL_ROWS = 48

ADAM_LR, ADAM_B1, ADAM_B2, ADAM_EPS, ADAM_WD, ADAM_STEP = 0.001, 0.9, 0.999, 1e-08, 0.01, 10

MESH = pl.DeviceIdType.MESH
VMEM_LIMIT = 56 * 1024 * 1024


def _cp(sem=None, **kw):
    if sem is not None:
        kw["dimension_semantics"] = sem
    return pltpu.CompilerParams(vmem_limit_bytes=VMEM_LIMIT, **kw)


def _dot(a, b):
    return jnp.dot(a, b, preferred_element_type=F32)


def _dot_nt(a, b):
    return lax.dot_general(a, b, (((1,), (1,)), ((), ())), preferred_element_type=F32)


def _dot_tn(a, b):
    return lax.dot_general(a, b, (((0,), (0,)), ((), ())), preferred_element_type=F32)


def _dot_f32(a, b):
    return jnp.dot(a, b, preferred_element_type=F32, precision=lax.Precision.HIGHEST)


def _sigmoid(z):
    return 0.5 * jnp.tanh(0.5 * z) + 0.5


def _rope(xp, cos, sin):
    return xp * cos + pltpu.roll(xp, 64, 1) * sin


def _rope_bwd(dy, cos, sin):
    return dy * cos - pltpu.roll(dy, 64, 1) * sin


def _vmem():
    return pl.BlockSpec(memory_space=pltpu.VMEM)


def _any():
    return pl.BlockSpec(memory_space=pl.ANY)


def _rope_rows():
    half = A_HD // 2
    inv = (np.float32(ROPE_THETA) ** (-np.arange(half, dtype=np.float32) / np.float32(half))).astype(np.float32)
    inv_row = jnp.asarray(np.tile(inv, 4)[None, :])
    sign_row = jnp.asarray(np.concatenate([-np.ones(64, np.float32), np.ones(64, np.float32)])[None, :])
    return inv_row, sign_row


def _prologue_rows(rows, x_ref, nw_ref, pos_ref, inv_ref, sign_ref, h_ref, cos_ref, sin_ref):
    xv = x_ref[rows, :]
    r = lax.rsqrt(jnp.mean(xv * xv, axis=-1, keepdims=True) + EPS)
    h_ref[rows, :] = ((xv * r) * nw_ref[...]).astype(h_ref.dtype)
    ang = pos_ref[rows, :].astype(F32) * inv_ref[...]
    cos_ref[rows, :] = jnp.cos(ang)
    sin_ref[rows, :] = jnp.sin(ang) * sign_ref[...]


def _proj(h, wft):
    T = h.shape[0]
    tT, tN = T, 512

    def body(h_ref, w_ref, o_ref):
        o_ref[...] = _dot_nt(h_ref[...], w_ref[...])

    return pl.pallas_call(
        body, name="proj", grid=(T // tT, NF // tN),
        in_specs=[pl.BlockSpec((tT, D), lambda i, j: (i, 0)), pl.BlockSpec((tN, D), lambda i, j: (j, 0))],
        out_specs=pl.BlockSpec((tT, tN), lambda i, j: (i, j)),
        out_shape=jax.ShapeDtypeStruct((T, NF), F32),
        compiler_params=_cp(("parallel", "parallel")),
    )(h, wft)


def _swa_masks():
    lane = lax.broadcasted_iota(jnp.int32, (BLK, LANE), 1)
    rope_sub0 = ((lane // 32) % 2) == 0
    std_sub0 = lane < 64
    return lane, rope_sub0, std_sub0


def _swa_tri():
    qi = lax.broadcasted_iota(jnp.int32, (BLK, BLK), 0)
    kj = lax.broadcasted_iota(jnp.int32, (BLK, BLK), 1)
    return kj <= qi


def _swa_fold(full, tri):
    return jnp.where(tri, full[:, BLK:], full[:, :BLK])


def _swa_unfold(sq, tri):
    return jnp.concatenate([jnp.where(tri, 0.0, sq), jnp.where(tri, sq, 0.0)], axis=1)


def _swa_keys(kc_ref, kp_ref, vc_ref, vp_ref, cq, sq, cp, sp):
    def ropek(kref, c, s):
        kv = kref[...]
        return jnp.concatenate([_rope(kv[:, :LANE], c, s), _rope(kv[:, LANE:], c, s)], axis=1)

    K = jnp.concatenate([ropek(kp_ref, cp, sp), ropek(kc_ref, cq, sq)], axis=0).astype(MXU)
    V = jnp.concatenate([vp_ref[...], vc_ref[...]], axis=0).astype(MXU)
    return K, V


def _swa_in_specs(nb, last):
    def cur(n):
        return jnp.minimum(n, last)

    def prev(n):
        return jnp.maximum(cur(n) - 1, 0)

    kd, vd = C_KD // 256, C_VD // 256
    return [
        pl.BlockSpec((BLK, D), lambda n: (cur(n), C_Q // D)),
        pl.BlockSpec((BLK, 256), lambda n: (cur(n), kd)),
        pl.BlockSpec((BLK, 256), lambda n: (prev(n), kd)),
        pl.BlockSpec((BLK, 256), lambda n: (cur(n), vd)),
        pl.BlockSpec((BLK, 256), lambda n: (prev(n), vd)),
        pl.BlockSpec((BLK, LANE), lambda n: (cur(n), 0)),
        pl.BlockSpec((BLK, LANE), lambda n: (cur(n), 0)),
        pl.BlockSpec((BLK, LANE), lambda n: (prev(n), 0)),
        pl.BlockSpec((BLK, LANE), lambda n: (prev(n), 0)),
    ]


def _swa_fwd(proj, cos, sin, sinks):
    T = proj.shape[0]
    nb = T // BLK
    scale = A_HD ** -0.5

    def body(sinks_ref, q_ref, kc_ref, kp_ref, vc_ref, vp_ref, cq_ref, sq_ref, cp_ref, sp_ref, o_ref, l_ref):
        n = pl.program_id(0)
        cq, sq = cq_ref[...], sq_ref[...]
        K, V = _swa_keys(kc_ref, kp_ref, vc_ref, vp_ref, cq, sq, cp_ref[...], sp_ref[...])
        tri = _swa_tri()
        valid = tri | (n > 0)
        lane, rope_sub0, std_sub0 = _swa_masks()
        group = A_HEADS // A_KV
        roped, lses = {}, []

        def products(head):
            pb, sub, g = head // 2, head % 2, head // group
            if sub == 0:
                roped[pb] = _rope(q_ref[:, pb * LANE:(pb + 1) * LANE], cq, sq)
            qm = jnp.where(rope_sub0 if sub == 0 else ~rope_sub0, roped[pb], 0.0).astype(MXU)
            return _dot_nt(qm, K[:, g * LANE:(g + 1) * LANE])

        def softmax(head, s_full):
            s = jnp.where(valid, _swa_fold(s_full, tri) * scale, NEG)
            sink = sinks_ref[0, head]
            m = jnp.maximum(jnp.max(s, axis=1, keepdims=True), sink)
            e = jnp.exp(s - m)
            den = jnp.sum(e, axis=1, keepdims=True) + jnp.exp(sink - m)
            lses.append(m + jnp.log(den))
            return _swa_unfold(e / den, tri).astype(MXU)

        outs = {}
        st1 = {0: products(0), 1: products(1)}
        st2 = {0: softmax(0, st1.pop(0))}
        for head in range(A_HEADS):
            if head + 2 < A_HEADS:
                st1[head + 2] = products(head + 2)
            if head + 1 < A_HEADS:
                st2[head + 1] = softmax(head + 1, st1.pop(head + 1))
            g = head // group
            outs[head] = _dot(st2.pop(head), V[:, g * LANE:(g + 1) * LANE])
            if head % 2 == 1:
                pb = head // 2
                o_ref[:, pb * LANE:(pb + 1) * LANE] = jnp.where(std_sub0, outs[head - 1], outs[head])
        lacc = jnp.zeros((BLK, LANE), F32)
        for head in range(A_HEADS):
            lacc = jnp.where(lane == head, lses[head], lacc)
        l_ref[...] = lacc

    return pl.pallas_call(
        body, name="swa_fwd", grid=(nb,),
        in_specs=[pl.BlockSpec(memory_space=pltpu.SMEM)] + _swa_in_specs(nb, nb - 1),
        out_specs=[pl.BlockSpec((BLK, D), lambda n: (n, 0)), pl.BlockSpec((BLK, LANE), lambda n: (n, 0))],
        out_shape=[jax.ShapeDtypeStruct((T, D), F32), jax.ShapeDtypeStruct((T, LANE), F32)],
        compiler_params=_cp(("parallel",)),
    )(sinks, proj, proj, proj, proj, proj, cos, sin, cos, sin)


def _swa_bwd(proj, cos, sin, sinks, do_a, o_a, lse):
    T = proj.shape[0]
    nb = T // BLK
    scale = A_HD ** -0.5

    def body(sinks_ref, q_ref, kc_ref, kp_ref, vc_ref, vp_ref, cq_ref, sq_ref, cp_ref, sp_ref,
             do_ref, o_ref, l_ref, dq_ref, dkv_ref, ds_ref, ckv_ref):
        n = pl.program_id(0)

        @pl.when(n == 0)
        def _():
            ckv_ref[...] = jnp.zeros_like(ckv_ref)
            ds_ref[...] = jnp.zeros_like(ds_ref)

        @pl.when(n < nb)
        def _():
            cq, sq, cp, sp = cq_ref[...], sq_ref[...], cp_ref[...], sp_ref[...]
            K, V = _swa_keys(kc_ref, kp_ref, vc_ref, vp_ref, cq, sq, cp, sp)
            tri = _swa_tri()
            valid = tri | (n > 0)
            lane, rope_sub0, std_sub0 = _swa_masks()
            lane_row = lax.broadcasted_iota(jnp.int32, (1, LANE), 1)
            lse_v = l_ref[...]
            dKt = [jnp.zeros((LANE, 2 * BLK), F32) for _ in range(A_KV)]
            dVt = [jnp.zeros((LANE, 2 * BLK), F32) for _ in range(A_KV)]
            dsinks, roped, roped_t, do_t = [], {}, {}, {}
            group = A_HEADS // A_KV
            dim = lax.broadcasted_iota(jnp.int32, (LANE, BLK), 0)
            rope_row0, std_row0 = ((dim // 32) % 2) == 0, dim < 64

            def products(head):
                pb, sub, g = head // 2, head % 2, head // group
                cols = slice(pb * LANE, (pb + 1) * LANE)
                Kg, Vg = K[:, g * LANE:(g + 1) * LANE], V[:, g * LANE:(g + 1) * LANE]
                if sub == 0:
                    roped[pb] = _rope(q_ref[:, cols], cq, sq)
                    roped_t[pb] = roped[pb].T
                    do_t[pb] = do_ref[:, cols].T
                qm = jnp.where(rope_sub0 if sub == 0 else ~rope_sub0, roped[pb], 0.0).astype(MXU)
                qmt = jnp.where(rope_row0 if sub == 0 else ~rope_row0, roped_t[pb], 0.0).astype(MXU)
                dov = jnp.where(std_sub0 if sub == 0 else ~std_sub0, do_ref[:, cols], 0.0)
                dovt = jnp.where(std_row0 if sub == 0 else ~std_row0, do_t[pb], 0.0).astype(MXU)
                delta = jnp.sum(dov * o_ref[:, cols], axis=1, keepdims=True)
                return qmt, dovt, delta, _dot_nt(qm, Kg), _dot_nt(dov.astype(MXU), Vg)

            def scores(head, qmt, dovt, delta, s_full, dp_full):
                lh = jnp.sum(jnp.where(lane == head, lse_v, 0.0), axis=1, keepdims=True)
                p = jnp.where(valid, jnp.exp(_swa_fold(s_full, tri) * scale - lh), 0.0)
                psink = jnp.exp(sinks_ref[0, head] - lh)
                dsinks.append(jnp.sum(-psink * delta, axis=0, keepdims=True))
                dsq = (p * (_swa_fold(dp_full, tri) - delta)) * scale
                return qmt, dovt, _swa_unfold(p, tri).astype(MXU), _swa_unfold(dsq, tri).astype(MXU)

            def grads(head, qmt, dovt, pb16, dsc):
                g = head // group
                dKt[g] = dKt[g] + _dot(qmt, dsc)
                dVt[g] = dVt[g] + _dot(dovt, pb16)
                return _dot(dsc, K[:, g * LANE:(g + 1) * LANE])

            dqs = {}
            st1 = {0: products(0), 1: products(1)}
            st2 = {0: scores(0, *st1.pop(0))}
            for head in range(A_HEADS):
                if head + 2 < A_HEADS:
                    st1[head + 2] = products(head + 2)
                if head + 1 < A_HEADS:
                    st2[head + 1] = scores(head + 1, *st1.pop(head + 1))
                dqs[head] = grads(head, *st2.pop(head))
                if head % 2 == 1:
                    pb = head // 2
                    dqp = jnp.where(rope_sub0, dqs[head - 1], dqs[head])
                    dq_ref[:, pb * LANE:(pb + 1) * LANE] = _rope_bwd(dqp, cq, sq).astype(dq_ref.dtype)
            dsink = jnp.zeros((1, LANE), F32)
            for head in range(A_HEADS):
                dsink = jnp.where(lane_row == head, dsinks[head], dsink)
            dK, dV = [a.T for a in dKt], [a.T for a in dVt]
            prev = ([_rope_bwd(dK[g][:BLK], cp, sp) for g in range(A_KV)] + [dV[g][:BLK] for g in range(A_KV)])
            cur_ = ([_rope_bwd(dK[g][BLK:], cq, sq) for g in range(A_KV)] + [dV[g][BLK:] for g in range(A_KV)])
            dkv_ref[...] = (ckv_ref[...] + jnp.concatenate(prev, axis=1)).astype(dkv_ref.dtype)
            ckv_ref[...] = jnp.concatenate(cur_, axis=1)
            ds_ref[...] = ds_ref[...] + jnp.broadcast_to(dsink, ds_ref.shape)

        @pl.when(n == nb)
        def _():
            dkv_ref[...] = ckv_ref[...].astype(dkv_ref.dtype)

    last = nb - 1

    def cur(n):
        return jnp.minimum(n, last)

    def out_kv(n):
        return (jnp.maximum(n - 1, 0), 0)

    return pl.pallas_call(
        body, name="swa_bwd", grid=(nb + 1,),
        in_specs=[pl.BlockSpec(memory_space=pltpu.SMEM)] + _swa_in_specs(nb, last) + [
            pl.BlockSpec((BLK, D), lambda n: (cur(n), 0)),
            pl.BlockSpec((BLK, D), lambda n: (cur(n), 0)),
            pl.BlockSpec((BLK, LANE), lambda n: (cur(n), 0)),
        ],
        out_specs=[
            pl.BlockSpec((BLK, D), lambda n: (cur(n), 0)),
            pl.BlockSpec((BLK, 512), out_kv),
            pl.BlockSpec((8, LANE), lambda n: (0, 0)),
        ],
        out_shape=[
            jax.ShapeDtypeStruct((T, D), MXU),
            jax.ShapeDtypeStruct((T, 512), MXU),
            jax.ShapeDtypeStruct((8, LANE), F32),
        ],
        scratch_shapes=[pltpu.VMEM((BLK, 512), F32)],
        compiler_params=_cp(("arbitrary",)),
    )(sinks, proj, proj, proj, proj, proj, cos, sin, cos, sin, do_a, o_a, lse)


def _gla_gate(bl_ref, gu_ref, bias_ref):
    gk = _dot(bl_ref[...].astype(MXU), gu_ref[...]) + bias_ref[...]
    la = (jnp.minimum(gk, 0.0) - jnp.log(1.0 + jnp.exp(-jnp.abs(gk)))) / TAU
    ri = lax.broadcasted_iota(jnp.int32, (CHUNK, CHUNK), 0)
    ci = lax.broadcasted_iota(jnp.int32, (CHUNK, CHUNK), 1)
    b = _dot_f32(jnp.where(ci <= ri, 1.0, 0.0).astype(F32), la)
    return gk, la, b, ri, ci


def _gla_head(q_ref, k_ref, la, b, h):
    sl = slice(h * B_DK, (h + 1) * B_DK)
    bh = b[:, sl]
    blast = jnp.sum(la[:, sl], axis=0, keepdims=True)
    qc = q_ref[:, sl] * (B_DK ** -0.5)
    kh = k_ref[:, sl]
    eb, enb, esb = jnp.exp(bh), jnp.exp(-bh), jnp.exp(blast - bh)
    return qc * eb, kh * enb, kh * esb, eb, enb, esb, jnp.exp(blast)


def _gla_specs(chunk_of):
    return [
        pl.BlockSpec((CHUNK, 512), lambda i: (chunk_of(i), C_BQ // 512)),
        pl.BlockSpec((CHUNK, 512), lambda i: (chunk_of(i), C_BK // 512)),
        pl.BlockSpec((CHUNK, D), lambda i: (chunk_of(i), C_BV // D)),
        pl.BlockSpec((CHUNK, W_BL), lambda i: (chunk_of(i), C_BL // W_BL)),
        pl.BlockSpec((W_BL, 512), lambda i: (0, 0)),
        pl.BlockSpec((1, 512), lambda i: (0, 0)),
    ]


def _gla_fwd(proj, gu_pad, bias):
    T = proj.shape[0]
    nc = T // CHUNK

    def body(q_ref, k_ref, v_ref, bl_ref, gu_ref, bias_ref, o_ref, st_ref, state_ref):
        @pl.when(pl.program_id(0) == 0)
        def _():
            state_ref[...] = jnp.zeros_like(state_ref)

        _, la, b, ri, ci = _gla_gate(bl_ref, gu_ref, bias_ref)
        st_ref[...] = state_ref[...]
        for h in range(B_HEADS):
            q_e, k_e, k_s, _, _, _, decay = _gla_head(q_ref, k_ref, la, b, h)
            vh = v_ref[:, h * B_DV:(h + 1) * B_DV].astype(MXU)
            rows = slice(h * B_DV, (h + 1) * B_DV)
            q_eb = q_e.astype(MXU)
            att = jnp.where(ci <= ri, _dot_nt(q_eb, k_e.astype(MXU)), 0.0)
            st = state_ref[rows, :]
            o_ref[:, rows] = _dot(att.astype(MXU), vh) + _dot_nt(q_eb, st.astype(MXU))
            state_ref[rows, :] = st * decay + _dot_tn(vh, k_s.astype(MXU))

    return pl.pallas_call(
        body, name="gla_fwd", grid=(nc,),
        in_specs=_gla_specs(lambda i: i),
        out_specs=[pl.BlockSpec((CHUNK, D), lambda i: (i, 0)),
                   pl.BlockSpec((B_HEADS * B_DV, B_DK), lambda i: (i, 0))],
        out_shape=[jax.ShapeDtypeStruct((T, D), F32),
                   jax.ShapeDtypeStruct((nc * B_HEADS * B_DV, B_DK), F32)],
        scratch_shapes=[pltpu.VMEM((B_HEADS * B_DV, B_DK), F32)],
        compiler_params=_cp(("arbitrary",)),
    )(proj, proj, proj, proj, gu_pad, bias)


def _gla_bwd(proj, gu_pad, bias, states, do_b):
    T = proj.shape[0]
    nc = T // CHUNK
    o_q, o_k = C_BQ - C_GLA, C_BK - C_GLA

    def body(q_ref, k_ref, v_ref, bl_ref, gu_ref, bias_ref, st_ref, do_ref,
             dg_ref, dbl_ref, ggu_ref, gbias_ref, gt_ref):
        @pl.when(pl.program_id(0) == 0)
        def _():
            gt_ref[...] = jnp.zeros_like(gt_ref)
            ggu_ref[...] = jnp.zeros_like(ggu_ref)
            gbias_ref[...] = jnp.zeros_like(gbias_ref)

        gk, la, b, ri, ci = _gla_gate(bl_ref, gu_ref, bias_ref)
        causal = ci <= ri
        upper = jnp.where(ci >= ri, 1.0, 0.0).astype(F32)
        dla_parts = []
        for h in range(B_HEADS):
            q_e, k_e, k_s, eb, enb, esb, decay = _gla_head(q_ref, k_ref, la, b, h)
            rows = slice(h * B_DV, (h + 1) * B_DV)
            sl = slice(h * B_DK, (h + 1) * B_DK)
            vh = v_ref[:, rows].astype(MXU)
            doh = do_ref[:, rows].astype(MXU)
            q_eb, k_eb, k_sb = q_e.astype(MXU), k_e.astype(MXU), k_s.astype(MXU)
            st = st_ref[rows, :]
            gt = gt_ref[rows, :]
            gtb = gt.astype(MXU)
            att = jnp.where(causal, _dot_nt(q_eb, k_eb), 0.0).astype(MXU)
            datt = jnp.where(causal, _dot_nt(doh, vh), 0.0).astype(MXU)
            dq_e = _dot(datt, k_eb) + _dot(doh, st.astype(MXU))
            dk_e = _dot_tn(datt, q_eb)
            dk_s = _dot(vh, gtb)
            dg_ref[:, rows] = (_dot_tn(att, doh) + _dot_nt(k_sb, gtb)).astype(dg_ref.dtype)
            ddecay = jnp.sum(gt * st, axis=0, keepdims=True)
            gt_ref[rows, :] = gt * decay + _dot_tn(doh, q_eb)
            dg_ref[:, o_q + h * B_DK:o_q + (h + 1) * B_DK] = (dq_e * eb * (B_DK ** -0.5)).astype(dg_ref.dtype)
            dg_ref[:, o_k + h * B_DK:o_k + (h + 1) * B_DK] = (dk_e * enb + dk_s * esb).astype(dg_ref.dtype)
            dks_ks = dk_s * k_s
            db = dq_e * q_e - dk_e * k_e - dks_ks
            dblast = jnp.sum(dks_ks, axis=0, keepdims=True) + ddecay * decay
            dla_parts.append(_dot_f32(upper, db) + dblast)
        dla = jnp.concatenate(dla_parts, axis=1)
        dgk = dla * (1.0 / TAU) * _sigmoid(-gk)
        dgkb = dgk.astype(MXU)
        dbl_ref[...] = _dot_nt(dgkb, gu_ref[...]).astype(dbl_ref.dtype)
        ggu_ref[...] = ggu_ref[...] + _dot_tn(bl_ref[...].astype(MXU), dgkb)
        gbias_ref[...] = gbias_ref[...] + jnp.broadcast_to(jnp.sum(dgk, axis=0, keepdims=True), gbias_ref.shape)

    def rev(i):
        return nc - 1 - i

    return pl.pallas_call(
        body, name="gla_bwd", grid=(nc,),
        in_specs=_gla_specs(rev) + [
            pl.BlockSpec((B_HEADS * B_DV, B_DK), lambda i: (rev(i), 0)),
            pl.BlockSpec((CHUNK, D), lambda i: (rev(i), 0)),
        ],
        out_specs=[
            pl.BlockSpec((CHUNK, W_GLA), lambda i: (rev(i), 0)),
            pl.BlockSpec((CHUNK, W_BL), lambda i: (rev(i), 0)),
            pl.BlockSpec((W_BL, 512), lambda i: (0, 0)),
            pl.BlockSpec((8, 512), lambda i: (0, 0)),
        ],
        out_shape=[
            jax.ShapeDtypeStruct((T, W_GLA), MXU),
            jax.ShapeDtypeStruct((T, W_BL), MXU),
            jax.ShapeDtypeStruct((W_BL, 512), F32),
            jax.ShapeDtypeStruct((8, 512), F32),
        ],
        scratch_shapes=[pltpu.VMEM((B_HEADS * B_DV, B_DK), F32)],
        compiler_params=_cp(("arbitrary",)),
    )(proj, proj, proj, proj, gu_pad, bias, states, do_b)


def _mid(x, target, proj, o_a, o_b, w_a, w_b, w_out, w_bn4, fnw):
    T = x.shape[0]
    tT = min(T, 128)
    nbuf = 4
    o_ag, o_bg, o_ma, o_mb = (c - C_GATES for c in (C_AG, C_BG, C_MA, C_MB))

    def body(x_ref, t_ref, oa_ref, ob_ref, gates_ref, wa_ref, wb_ref, wo_ref, wbn_ref, fnw_ref,
             dx2_ref, doa_ref, dob_ref, dgates_ref,
             gwa_ref, gwb_ref, gwo_ref, gfn_ref, gbn_ref, loss_ref, buf_ref):
        i = pl.program_id(0)

        @pl.when(i == 0)
        def _():
            for r in (gwa_ref, gwb_ref, gwo_ref, gfn_ref, gbn_ref, loss_ref):
                r[...] = jnp.zeros_like(r)

        rows = pl.ds(pl.multiple_of((i % nbuf) * tT, tT), tT)

        def keep(k, val):
            buf_ref[k, rows, :] = val

        oa, ag = oa_ref[...], gates_ref[:, o_ag:o_ag + D]
        sg_a = _sigmoid(ag)
        silu_a = ag * sg_a
        oag_b = (oa * silu_a).astype(MXU)
        keep(0, oag_b)
        y_a = _dot(oag_b, wa_ref[...])

        ob, bg = ob_ref[...], gates_ref[:, o_bg:o_bg + D]
        rbs, obhats = [], []
        for h in range(B_HEADS):
            obh = ob[:, h * B_DV:(h + 1) * B_DV]
            rb = lax.rsqrt(jnp.mean(obh * obh, axis=-1, keepdims=True) + EPS)
            rbs.append(rb)
            obhats.append(obh * rb)
        obhat = jnp.concatenate(obhats, axis=1)
        wbn = wbn_ref[...]
        obn = obhat * wbn
        sg_b = _sigmoid(bg)
        silu_b = bg * sg_b
        obg_b = (obn * silu_b).astype(MXU)
        keep(1, obg_b)
        y_b = _dot(obg_b, wb_ref[...])

        sa, sb = _sigmoid(gates_ref[:, o_ma:o_ma + D]), _sigmoid(gates_ref[:, o_mb:o_mb + D])
        mg_b = (sa * y_a + sb * y_b).astype(MXU)
        keep(2, mg_b)
        x2 = x_ref[...] + _dot(mg_b, wo_ref[...])
        r2 = lax.rsqrt(jnp.mean(x2 * x2, axis=-1, keepdims=True) + EPS)
        xh2 = x2 * r2
        fw = fnw_ref[...]
        err = xh2 * fw - t_ref[...]
        tok = jnp.mean(err * err, axis=-1, keepdims=True)
        loss_ref[...] = loss_ref[...] + 0.5 * jnp.sum(tok, axis=0, keepdims=True)

        dy = err * (1.0 / D)
        gfn_ref[...] = gfn_ref[...] + jnp.broadcast_to(jnp.sum(dy * xh2, axis=0, keepdims=True), gfn_ref.shape)
        gy = dy * fw
        dx2 = r2 * (gy - xh2 * jnp.mean(gy * xh2, axis=-1, keepdims=True))
        dx2_ref[...] = dx2
        dx2_b = dx2.astype(MXU)
        keep(5, dx2_b)
        dmg = _dot_nt(dx2_b, wo_ref[...])

        dgates_ref[:, o_ma:o_ma + D] = (dmg * y_a * sa * (1.0 - sa)).astype(dgates_ref.dtype)
        dgates_ref[:, o_mb:o_mb + D] = (dmg * y_b * sb * (1.0 - sb)).astype(dgates_ref.dtype)
        dya_b = (dmg * sa).astype(MXU)
        dyb_b = (dmg * sb).astype(MXU)
        keep(3, dya_b)
        keep(4, dyb_b)
        doag = _dot_nt(dya_b, wa_ref[...])
        dobg = _dot_nt(dyb_b, wb_ref[...])

        @pl.when(i % nbuf == nbuf - 1)
        def _():
            gwa_ref[...] = gwa_ref[...] + _dot_tn(buf_ref[0], buf_ref[3])
            gwb_ref[...] = gwb_ref[...] + _dot_tn(buf_ref[1], buf_ref[4])
            gwo_ref[...] = gwo_ref[...] + _dot_tn(buf_ref[2], buf_ref[5])

        doa_ref[...] = doag * silu_a
        dgates_ref[:, o_ag:o_ag + D] = (doag * oa * (sg_a * (1.0 + ag * (1.0 - sg_a)))).astype(dgates_ref.dtype)
        dobn = dobg * silu_b
        dgates_ref[:, o_bg:o_bg + D] = (dobg * obn * (sg_b * (1.0 + bg * (1.0 - sg_b)))).astype(dgates_ref.dtype)
        gg = dobn * wbn
        gbn = jnp.zeros((1, B_DV), F32)
        for h in range(B_HEADS):
            sl = slice(h * B_DV, (h + 1) * B_DV)
            gbn = gbn + jnp.sum(dobn[:, sl] * obhats[h], axis=0, keepdims=True)
            ggh = gg[:, sl]
            dob_ref[:, sl] = rbs[h] * (ggh - obhats[h] * jnp.mean(ggh * obhats[h], axis=-1, keepdims=True))
        gbn_ref[...] = gbn_ref[...] + jnp.broadcast_to(gbn, gbn_ref.shape)

    assert (T // tT) % nbuf == 0
    tile = pl.BlockSpec((tT, D), lambda i: (i, 0))
    row = pl.BlockSpec((1, D), lambda i: (0, 0))
    acc8 = pl.BlockSpec((8, D), lambda i: (0, 0))
    return pl.pallas_call(
        body, name="mid", grid=(T // tT,),
        in_specs=[tile, tile, tile, tile, pl.BlockSpec((tT, W_GATES), lambda i: (i, C_GATES // W_GATES)),
                  _vmem(), _vmem(), _vmem(), row, row],
        out_specs=[tile, tile, tile, pl.BlockSpec((tT, W_GATES), lambda i: (i, 0)), _vmem(), _vmem(), _vmem(),
                   acc8, pl.BlockSpec((8, B_DV), lambda i: (0, 0)), pl.BlockSpec((8, LANE), lambda i: (0, 0))],
        out_shape=[
            jax.ShapeDtypeStruct((T, D), F32),
            jax.ShapeDtypeStruct((T, D), F32),
            jax.ShapeDtypeStruct((T, D), F32),
            jax.ShapeDtypeStruct((T, W_GATES), MXU),
            jax.ShapeDtypeStruct((D, D), F32),
            jax.ShapeDtypeStruct((D, D), F32),
            jax.ShapeDtypeStruct((D, D), F32),
            jax.ShapeDtypeStruct((8, D), F32),
            jax.ShapeDtypeStruct((8, B_DV), F32),
            jax.ShapeDtypeStruct((8, LANE), F32),
        ],
        scratch_shapes=[pltpu.VMEM((6, nbuf * tT, D), MXU)],
        compiler_params=_cp(("arbitrary",)),
    )(x, target, o_a, o_b, proj, w_a, w_b, w_out, w_bn4, fnw)


def _gw_piece(h, dp, idx):
    T, w = dp.shape
    tn = min(w, 512)

    def body(h_ref, dp_ref, o_ref):
        o_ref[...] = _dot_tn(dp_ref[...], h_ref[...])

    return pl.pallas_call(
        body, name=f"gw_in_{idx}", grid=(w // tn,),
        in_specs=[pl.BlockSpec((T, D), lambda j: (0, 0)), pl.BlockSpec((T, tn), lambda j: (0, j))],
        out_specs=pl.BlockSpec((tn, D), lambda j: (j, 0)),
        out_shape=jax.ShapeDtypeStruct((w, D), F32),
        compiler_params=_cp(("parallel",)),
    )(h, dp)


def _chip_copies(s_ref, got_ref, send_sems, recv_sems):
    x, y, c = _place()
    chips = [(1 - x, y), (x, 1 - y), (1 - x, 1 - y)]
    return [pltpu.make_async_remote_copy(
        src_ref=s_ref.at[2 * px + py], dst_ref=got_ref.at[j],
        send_sem=send_sems.at[j], recv_sem=recv_sems.at[j], device_id=(px, py, c), device_id_type=MESH)
        for j, (px, py) in enumerate(chips)]


def _dh_norm(pieces, offsets, wf, x, dx2, norm_w, sums):
    T = x.shape[0]
    tT = min(T, 256)
    widths = [p.shape[1] for p in pieces]
    npc = len(pieces)
    last = T // tT - 1

    def body(*refs):
        dp_refs = refs[:npc]
        wf_ref, x_ref, dx2_ref, nw_ref, s_ref, gx_ref, gnw_ref, got_ref, send_sems, recv_sems = refs[npc:]

        @pl.when(pl.program_id(0) == 0)
        def _():
            gnw_ref[...] = jnp.zeros_like(gnw_ref)
            for cp in _chip_copies(s_ref, got_ref, send_sems, recv_sems):
                cp.start()

        dh = jnp.zeros((tT, D), F32)
        for dp_ref, off, w in zip(dp_refs, offsets, widths):
            dh = dh + _dot(dp_ref[...], wf_ref[off:off + w, :])
        xv = x_ref[...]
        r = lax.rsqrt(jnp.mean(xv * xv, axis=-1, keepdims=True) + EPS)
        xh = xv * r
        gnw_ref[...] = gnw_ref[...] + jnp.broadcast_to(jnp.sum(dh * xh, axis=0, keepdims=True), gnw_ref.shape)
        g = dh * nw_ref[...]
        gx_ref[...] = r * (g - xh * jnp.mean(g * xh, axis=-1, keepdims=True)) + dx2_ref[...]

        @pl.when(pl.program_id(0) == last)
        def _():
            copies = _chip_copies(s_ref, got_ref, send_sems, recv_sems)
            for cp in copies:
                cp.wait_recv()
            for cp in copies:
                cp.wait_send()

    tile = pl.BlockSpec((tT, D), lambda i: (i, 0))
    return pl.pallas_call(
        body, name="dh_norm", grid=(T // tT,),
        in_specs=[pl.BlockSpec((tT, w), lambda i: (i, 0)) for w in widths]
        + [_vmem(), tile, tile, pl.BlockSpec((1, D), lambda i: (0, 0)), _any()],
        out_specs=[tile, pl.BlockSpec((8, D), lambda i: (0, 0)), _any()],
        out_shape=[jax.ShapeDtypeStruct((T, D), F32), jax.ShapeDtypeStruct((8, D), F32),
                   jax.ShapeDtypeStruct((3, ROWS, D), sums.dtype)],
        scratch_shapes=[pltpu.SemaphoreType.DMA((3,)), pltpu.SemaphoreType.DMA((3,))],
        compiler_params=_cp(("arbitrary",)),
    )(*pieces, wf, x, dx2, norm_w, sums)


def _adamw_math(w, g, m, v):
    m = ADAM_B1 * m + (1.0 - ADAM_B1) * g
    v = ADAM_B2 * v + (1.0 - ADAM_B2) * (g * g)
    m_hat = m / (1.0 - ADAM_B1 ** ADAM_STEP)
    v_hat = v / (1.0 - ADAM_B2 ** ADAM_STEP)
    delta = -ADAM_LR * (m_hat / (jnp.sqrt(v_hat) + ADAM_EPS) + ADAM_WD * w)
    return delta, m, v


def _fetch_partials(s_ref, got_ref, buf, sems, r0, nrows):
    x, y, _ = _place()
    cps = [pltpu.make_async_copy(s_ref.at[2 * x + y, pl.ds(r0, nrows)], buf.at[0], sems.at[0])]
    cps += [pltpu.make_async_copy(got_ref.at[j, pl.ds(r0, nrows)], buf.at[1 + j], sems.at[1 + j]) for j in range(3)]
    for cp in cps:
        cp.start()
    for cp in cps:
        cp.wait()


SMALL_AT = dict(norm_w=0, fnw=8, bias=16, bn=24, sinks=32, loss=40)
ROW_AT = (R_IN, R_A, R_B, R_O)


def _finish(w_rows, m_rows, v_rows, ws, ms, vs, gu_w, gu_m, gu_v, small, sums, got):
    names = ["norm_w", "fnw", "bias", "bn", "sinks"]
    widths = [ws[n].shape[1] for n in names]
    shapes = [w.shape for w in w_rows]

    def body(*refs):
        wr_refs, mr_refs, vr_refs = refs[0:4], refs[4:8], refs[8:12]
        refs = refs[12:]
        w_refs, m_refs, v_refs = refs[0:5], refs[5:10], refs[10:15]
        guw_ref, gum_ref, guv_ref, small_ref, s_ref, got_ref = refs[15:21]
        loss_ref = refs[21]
        row_outs = refs[22:38]
        outs = refs[38:58]
        gu_outs = refs[58:62]
        smalls, tot, buf, send_sems, recv_sems, sems = refs[62:]
        x, y, c = _place()
        me_slot = 4 * x + 2 * y + c
        sends = []
        k = 0
        for dx in range(2):
            for dy in range(2):
                for dc in range(2):
                    if dx == 0 and dy == 0 and dc == 0:
                        continue
                    sends.append(pltpu.make_async_remote_copy(
                        src_ref=small_ref, dst_ref=smalls.at[me_slot],
                        send_sem=send_sems.at[k], recv_sem=recv_sems.at[k],
                        device_id=(x ^ dx, y ^ dy, c ^ dc), device_id_type=MESH))
                    k += 1
        for cp in sends:
            cp.start()
        smalls[me_slot] = small_ref[...]
        _fetch_partials(s_ref, got_ref, buf, sems, 0, ROWS)
        for p in range(4):
            n, off = shapes[p][0], ROW_AT[p]
            for cc in range(D // LANE):
                cols = slice(cc * LANE, (cc + 1) * LANE)
                g = buf[0, off:off + n, cols].astype(F32)
                for j in range(1, 4):
                    g = g + buf[j, off:off + n, cols].astype(F32)
                d, nm, nv = _adamw_math(wr_refs[p][:, cols], g, mr_refs[p][:, cols], vr_refs[p][:, cols])
                for o, val in zip(row_outs[4 * p:4 * p + 4], (g, d, nm, nv)):
                    o[:, cols] = val
        for cp in sends:
            cp.wait_recv()
        for cp in sends:
            cp.wait_send()
        acc = smalls[0]
        for d in range(1, NDEV):
            acc = acc + smalls[d]
        tot[...] = acc
        loss_ref[...] = tot[SMALL_AT["loss"]:SMALL_AT["loss"] + 1, 0:1]
        for p, (nm_, wd) in enumerate(zip(names, widths)):
            r = SMALL_AT[nm_]
            g = tot[r:r + 1, 0:wd]
            d, nm, nv = _adamw_math(w_refs[p][...], g, m_refs[p][...], v_refs[p][...])
            for o, val in zip(outs[4 * p:4 * p + 4], (g, d, nm, nv)):
                o[...] = val
        g = buf[0, R_GU:R_GU + RANK, 0:64].astype(F32)
        for j in range(1, 4):
            g = g + buf[j, R_GU:R_GU + RANK, 0:64].astype(F32)
        d, nm, nv = _adamw_math(guw_ref[...], g, gum_ref[...], guv_ref[...])
        for o, val in zip(gu_outs, (g, d, nm, nv)):
            o[...] = val

    out_shape = ([jax.ShapeDtypeStruct((1, 1), F32)]
                 + [jax.ShapeDtypeStruct(s, F32) for s in shapes for _ in range(4)]
                 + [jax.ShapeDtypeStruct((1, wd), F32) for wd in widths for _ in range(4)]
                 + [jax.ShapeDtypeStruct((RANK, 64), F32)] * 4)
    res = pl.pallas_call(
        body, name="finish",
        in_specs=[_vmem()] * 31 + [_any(), _any()],
        out_specs=[_vmem()] * 41,
        out_shape=out_shape,
        scratch_shapes=[pltpu.VMEM((NDEV, SMALL_ROWS, D), F32), pltpu.VMEM((SMALL_ROWS, D), F32),
                        pltpu.VMEM((4, ROWS, D), sums.dtype),
                        pltpu.SemaphoreType.DMA((7,)), pltpu.SemaphoreType.DMA((7,)), pltpu.SemaphoreType.DMA((4,))],
        compiler_params=_cp(),
    )(*w_rows, *m_rows, *v_rows, *[ws[n] for n in names], *[ms[n] for n in names], *[vs[n] for n in names],
      gu_w, gu_m, gu_v, small, sums, got)
    loss = res[0]
    per = {n: tuple(res[17 + 4 * p:21 + 4 * p]) for p, n in enumerate(names)}
    return loss, tuple(res[1:17]), per, tuple(res[37:41])


def _place():
    x, y, c = lax.axis_index("x"), lax.axis_index("y"), lax.axis_index("c")
    return x, y, c


def _gather_blocks(blk, xs, norm_w, pos_col):
    rows, cols = blk.shape
    T = xs.shape[0]
    tT = min(T, 256)
    inv_row, sign_row = _rope_rows()

    def body(x_ref, xs_ref, nw_ref, pos_ref, inv_ref, sign_ref, out_ref, h_ref, cos_ref, sin_ref,
             send_sems, recv_sems, local_sem):
        x, y, c = _place()
        me, sibling = (x, y, c), (x, y, 1 - c)
        chips = [(1 - x, y), (x, 1 - y), (1 - x, 1 - y)]

        def slot(px, py, pc):
            return out_ref.at[4 * px + 2 * py + pc]

        def copy(k, block, to, src=None):
            return pltpu.make_async_remote_copy(
                src_ref=slot(*block) if src is None else src, dst_ref=slot(*block),
                send_sem=send_sems.at[k], recv_sem=recv_sems.at[k], device_id=to, device_id_type=MESH)

        mine = pltpu.make_async_copy(x_ref, slot(*me), local_sem)
        mine.start()
        first = [copy(0, me, sibling, src=x_ref)]
        first += [copy(1 + j, me, (*chip, c), src=x_ref) for j, chip in enumerate(chips)]
        for cp in first:
            cp.start()

        @pl.loop(0, T // tT)
        def _(i):
            rows_i = pl.ds(pl.multiple_of(i * tT, tT), tT)
            _prologue_rows(rows_i, xs_ref, nw_ref, pos_ref, inv_ref, sign_ref, h_ref, cos_ref, sin_ref)

        passed = [copy(4 + j, (*chip, c), sibling) for j, chip in enumerate(chips)]
        for j, chip in enumerate(chips):
            copy(1 + j, (*chip, c), me).wait_recv()
            passed[j].start()
        copy(0, sibling, me).wait_recv()
        for j, chip in enumerate(chips):
            copy(4 + j, (*chip, 1 - c), me).wait_recv()
        for cp in first + passed:
            cp.wait_send()
        mine.wait()

    return pl.pallas_call(
        body, name="gather_weights",
        in_specs=[_any()] + [_vmem()] * 5, out_specs=[_any()] + [_vmem()] * 3,
        out_shape=[jax.ShapeDtypeStruct((NDEV, rows, cols), blk.dtype), jax.ShapeDtypeStruct((T, D), MXU),
                   jax.ShapeDtypeStruct((T, LANE), F32), jax.ShapeDtypeStruct((T, LANE), F32)],
        scratch_shapes=[pltpu.SemaphoreType.DMA((7,)), pltpu.SemaphoreType.DMA((7,)), pltpu.SemaphoreType.DMA],
        compiler_params=_cp(),
    )(blk, xs, norm_w, pos_col, inv_row, sign_row)


def _pair_reduce(packed):
    def body(p_ref, out_ref, got, own, send_sems, recv_sems, own_sems):
        x, y, c = _place()
        sends = [pltpu.make_async_remote_copy(
            src_ref=p_ref.at[2 * chip + (1 - c)], dst_ref=got.at[chip],
            send_sem=send_sems.at[chip], recv_sem=recv_sems.at[chip], device_id=(x, y, 1 - c), device_id_type=MESH)
            for chip in range(4)]
        loads = [pltpu.make_async_copy(p_ref.at[2 * chip + c], own.at[chip], own_sems.at[chip]) for chip in range(4)]
        for cp in sends + loads:
            cp.start()
        for chip in range(4):
            loads[chip].wait()
            sends[chip].wait_recv()
            out_ref[chip] = (own[chip].astype(F32) + got[chip].astype(F32)).astype(out_ref.dtype)
        for cp in sends:
            cp.wait_send()

    return pl.pallas_call(
        body, name="pair_reduce",
        in_specs=[_any()], out_specs=_vmem(),
        out_shape=jax.ShapeDtypeStruct((4, ROWS, D), packed.dtype),
        scratch_shapes=[pltpu.VMEM((4, ROWS, D), packed.dtype), pltpu.VMEM((4, ROWS, D), packed.dtype),
                        pltpu.SemaphoreType.DMA((4,)), pltpu.SemaphoreType.DMA((4,)), pltpu.SemaphoreType.DMA((4,))],
        compiler_params=_cp(),
    )(packed)


def _pad_cols(a, cols):
    return jnp.pad(a, ((0, 0), (0, cols - a.shape[1])))


def _pad_rows(a, rows):
    return jnp.pad(a, ((0, rows - a.shape[0]), (0, 0)))


FRAME = 928


def _frame(a_t, me):
    return lax.dynamic_update_slice(jnp.zeros((SHARD_PAD, D), a_t.dtype), a_t, (2 * me, 0))


def _unframe(a, me):
    return lax.dynamic_slice(a, (2 * me, 0), (SHARD, D))


def _pack_block(w_in_t, w_a_s, w_b_s, w_o_s, gu_s, me, dtype):
    return jnp.concatenate([
        _frame(w_in_t.astype(dtype), me), w_a_s.astype(dtype), w_b_s.astype(dtype), w_o_s.astype(dtype),
        _pad_cols(gu_s, D).astype(dtype)], axis=0)


def _join_frames(frames):
    head = frames[:, :FRAME].at[1:, :16].add(frames[:-1, FRAME:])
    return jnp.concatenate([head.reshape(NDEV * FRAME, D), frames[NDEV - 1, FRAME:]], axis=0)


def _build_wft(wt):
    q = wt[0:1024].reshape(8, 2, 2, 32, D).transpose(0, 2, 1, 3, 4).reshape(1024, D)
    k = wt[1024:1152].reshape(2, 2, 1, 32, D)
    kd = jnp.broadcast_to(k, (2, 2, 2, 32, D)).reshape(256, D)
    v = wt[1152:1280].reshape(2, 1, 64, D)
    vd = jnp.broadcast_to(v, (2, 2, 64, D)).reshape(256, D)
    ag, bq, bk = wt[1280:2304], wt[2304:2816], wt[2816:3328]
    bv, bg, bl = wt[3328:4352], wt[4352:5376], wt[5376:5392]
    ma, mb = wt[5392:6416], wt[6416:7440]
    return jnp.concatenate([q, kd, vd, _pad_rows(bl, C_GLA - C_BL), bv, bq, bk, ag, bg, ma, mb], axis=0)


def _unbuild_gwt(gq, gkv, gbl, ggla, ggates):
    q = gq.reshape(8, 2, 2, 32, D).transpose(0, 2, 1, 3, 4).reshape(1024, D)
    k = gkv[:256].reshape(2, 2, 2, 32, D).sum(axis=2).reshape(128, D)
    v = gkv[256:].reshape(2, 2, 64, D).sum(axis=1).reshape(128, D)
    bv, bq, bk = ggla[:1024], ggla[1024:1536], ggla[1536:]
    ag, bg, ma, mb = (ggates[i * D:(i + 1) * D] for i in range(4))
    return jnp.concatenate([q, k, v, ag, bq, bk, bv, bg, gbl[:RANK], ma, mb], axis=0)


def kernel(x, positions, norm_w, w_in, a_sinks, b_gate_up, b_gate_bias, b_out_norm_w, w_a_proj, w_b_proj, w_out, final_norm_w, loss_target, m_norm_w, m_w_in, m_a_sinks, m_b_gate_up, m_b_gate_bias, m_b_out_norm_w, m_w_a_proj, m_w_b_proj, m_w_out, m_final_norm_w, v_norm_w, v_w_in, v_a_sinks, v_b_gate_up, v_b_gate_bias, v_b_out_norm_w, v_w_a_proj, v_w_b_proj, v_w_out, v_final_norm_w):
    T = x.shape[1]
    xs, target = x[0], loss_target[0]
    fnw = final_norm_w.reshape(1, D)
    me = 4 * lax.axis_index("x") + 2 * lax.axis_index("y") + lax.axis_index("c")
    blk = _pack_block(w_in[0].T, w_a_proj[0], w_b_proj[0], w_out[0], b_gate_up[0], me, WIRE)
    allw, h, cos, sin = _gather_blocks(blk, xs, norm_w, positions.reshape(T, 1))
    wf = _build_wft(_join_frames(allw[:, :SHARD_PAD]))
    w_a = allw[:, R_A:R_A + 128, :].reshape(D, D)
    w_b = allw[:, R_B:R_B + 128, :].reshape(D, D)
    w_o = allw[:, R_O:R_O + 128, :].reshape(D, D)
    gu = allw[:, R_GU:R_GU + RANK, :64].transpose(1, 0, 2).reshape(RANK, 512)
    gu_pad = _pad_rows(gu, W_BL)

    proj = _proj(h, wf)
    o_a, lse = _swa_fwd(proj, cos, sin, a_sinks)
    o_b, states = _gla_fwd(proj, gu_pad, b_gate_bias)
    (dx2, do_a, do_b, d_gates, g_wa, g_wb, g_wo, g_fn, g_bn, loss_part) = _mid(
        xs, target, proj, o_a, o_b, w_a, w_b, w_o, jnp.tile(b_out_norm_w, (1, B_HEADS)), fnw)
    d_q, d_kv, g_sinks = _swa_bwd(proj, cos, sin, a_sinks, do_a, o_a, lse)
    d_gla, d_bl, g_gu, g_bias = _gla_bwd(proj, gu_pad, b_gate_bias, states, do_b)
    pieces = [d_q, d_kv, d_bl, d_gla, d_gates]
    offsets = [C_Q, C_KD, C_BL, C_GLA, C_GATES]
    gw = [_gw_piece(h, dp, nm) for nm, dp in zip(["q", "kv", "bl", "gla", "gates"], pieces)]

    gwt = _unbuild_gwt(*gw).astype(WIRE)
    ggu = g_gu[:RANK].reshape(RANK, NDEV, 64).transpose(1, 0, 2)
    packed = jnp.concatenate([
        jnp.stack([gwt[FRAME * d:FRAME * d + SHARD_PAD] for d in range(NDEV)]),
        g_wa.reshape(NDEV, 128, D).astype(WIRE),
        g_wb.reshape(NDEV, 128, D).astype(WIRE),
        g_wo.reshape(NDEV, 128, D).astype(WIRE),
        jnp.pad(ggu, ((0, 0), (0, 0), (0, D - 64))).astype(WIRE),
    ], axis=1)
    sums = _pair_reduce(packed)
    grad_x, g_nw, from_chips = _dh_norm(pieces, offsets, wf, xs, dx2, norm_w, sums)

    small = jnp.concatenate([g_nw, g_fn, _pad_cols(g_bias, D), _pad_cols(g_bn, D), _pad_cols(g_sinks, D),
                             _pad_cols(loss_part, D)], axis=0)
    ws = dict(norm_w=norm_w, fnw=fnw, bias=b_gate_bias, bn=b_out_norm_w, sinks=a_sinks)
    ms = dict(norm_w=m_norm_w, fnw=m_final_norm_w.reshape(1, D), bias=m_b_gate_bias, bn=m_b_out_norm_w,
              sinks=m_a_sinks)
    vs = dict(norm_w=v_norm_w, fnw=v_final_norm_w.reshape(1, D), bias=v_b_gate_bias, bn=v_b_out_norm_w,
              sinks=v_a_sinks)
    loss, t_rows, sm, t_gu = _finish(
        [_frame(w_in[0].T, me), w_a_proj[0], w_b_proj[0], w_out[0]],
        [_frame(m_w_in[0].T, me), m_w_a_proj[0], m_w_b_proj[0], m_w_out[0]],
        [_frame(v_w_in[0].T, me), v_w_a_proj[0], v_w_b_proj[0], v_w_out[0]],
        ws, ms, vs, b_gate_up[0], m_b_gate_up[0], v_b_gate_up[0], small, sums, from_chips)

    def outputs(k):
        return [sm["norm_w"][k], _unframe(t_rows[k], me).T[None], sm["sinks"][k], t_gu[k][None], sm["bias"][k],
                sm["bn"][k],
                t_rows[4 + k][None], t_rows[8 + k][None], t_rows[12 + k][None], sm["fnw"][k].reshape(D)]

    return (loss[0, 0], grad_x[None], *outputs(0), *outputs(1), *outputs(2), *outputs(3))
```

```python
import functools

import numpy as np
import jax
import jax.numpy as jnp
from jax import lax
from jax.experimental import pallas as pl
from jax.experimental.pallas import tpu as pltpu

F32 = jnp.float32
MXU = jnp.bfloat16
WIRE = jnp.bfloat16

D = 1024
A_HEADS, A_KV, A_HD = 16, 2, 64
BLK = 128
B_HEADS, B_DK, B_DV = 4, 128, 256
RANK, TAU, CHUNK = 16, 16.0, 64
EPS, NEG = 1e-5, -1e30
ROPE_THETA = 10000.0
IN_WIDTH, NDEV = 7440, 8
SHARD = IN_WIDTH // NDEV
LANE = 128

C_Q, C_KD, C_VD, C_BL = 0, 1024, 1280, 1536
C_BV, C_BQ, C_BK = 2048, 3072, 3584
C_AG, C_BG, C_MA, C_MB = 4096, 5120, 6144, 7168
C_GLA, W_GLA, C_GATES, W_GATES = 2048, 2048, 4096, 4096
NF = 8192
W_BL = 128

SHARD_PAD = 944
R_IN, R_A, R_B, R_O, R_GU, ROWS = 0, 944, 1072, 1200, 1328, 1344
SMALL_ROWS = 48

ADAM_LR, ADAM_B1, ADAM_B2, ADAM_EPS, ADAM_WD, ADAM_STEP = 0.001, 0.9, 0.999, 1e-08, 0.01, 10

MESH = pl.DeviceIdType.MESH
VMEM_LIMIT = 56 * 1024 * 1024


def _cp(sem=None, **kw):
    if sem is not None:
        kw["dimension_semantics"] = sem
    return pltpu.CompilerParams(vmem_limit_bytes=VMEM_LIMIT, **kw)


def _dot(a, b):
    return jnp.dot(a, b, preferred_element_type=F32)


def _dot_nt(a, b):
    return lax.dot_general(a, b, (((1,), (1,)), ((), ())), preferred_element_type=F32)


def _dot_tn(a, b):
    return lax.dot_general(a, b, (((0,), (0,)), ((), ())), preferred_element_type=F32)


def _dot_f32(a, b):
    return jnp.dot(a, b, preferred_element_type=F32, precision=lax.Precision.HIGHEST)


def _sigmoid(z):
    return 0.5 * jnp.tanh(0.5 * z) + 0.5


def _rope(xp, cos, sin):
    return xp * cos + pltpu.roll(xp, 64, 1) * sin


def _rope_bwd(dy, cos, sin):
    return dy * cos - pltpu.roll(dy, 64, 1) * sin


def _vmem():
    return pl.BlockSpec(memory_space=pltpu.VMEM)


def _any():
    return pl.BlockSpec(memory_space=pl.ANY)


def _rope_rows():
    half = A_HD // 2
    inv = (np.float32(ROPE_THETA) ** (-np.arange(half, dtype=np.float32) / np.float32(half))).astype(np.float32)
    inv_row = jnp.asarray(np.tile(inv, 4)[None, :])
    sign_row = jnp.asarray(np.concatenate([-np.ones(64, np.float32), np.ones(64, np.float32)])[None, :])
    return inv_row, sign_row


def _prologue_rows(rows, x_ref, nw_ref, pos_ref, inv_ref, sign_ref, h_ref, cos_ref, sin_ref):
    xv = x_ref[rows, :]
    r = lax.rsqrt(jnp.mean(xv * xv, axis=-1, keepdims=True) + EPS)
    h_ref[rows, :] = ((xv * r) * nw_ref[...]).astype(h_ref.dtype)
    ang = pos_ref[rows, :].astype(F32) * inv_ref[...]
    cos_ref[rows, :] = jnp.cos(ang)
    sin_ref[rows, :] = jnp.sin(ang) * sign_ref[...]


def _proj(h, wft):
    T = h.shape[0]
    tT, tN = T, 512

    def body(h_ref, w_ref, o_ref):
        o_ref[...] = _dot_nt(h_ref[...], w_ref[...])

    return pl.pallas_call(
        body, name="proj", grid=(T // tT, NF // tN),
        in_specs=[pl.BlockSpec((tT, D), lambda i, j: (i, 0)), pl.BlockSpec((tN, D), lambda i, j: (j, 0))],
        out_specs=pl.BlockSpec((tT, tN), lambda i, j: (i, j)),
        out_shape=jax.ShapeDtypeStruct((T, NF), F32),
        compiler_params=_cp(("parallel", "parallel")),
    )(h, wft)


def _swa_masks():
    lane = lax.broadcasted_iota(jnp.int32, (BLK, LANE), 1)
    rope_sub0 = ((lane // 32) % 2) == 0
    std_sub0 = lane < 64
    return lane, rope_sub0, std_sub0


def _swa_tri():
    qi = lax.broadcasted_iota(jnp.int32, (BLK, BLK), 0)
    kj = lax.broadcasted_iota(jnp.int32, (BLK, BLK), 1)
    return kj <= qi


def _swa_fold(full, tri):
    return jnp.where(tri, full[:, BLK:], full[:, :BLK])


def _swa_unfold(sq, tri):
    return jnp.concatenate([jnp.where(tri, 0.0, sq), jnp.where(tri, sq, 0.0)], axis=1)


def _swa_keys(kc_ref, kp_ref, vc_ref, vp_ref, cq, sq, cp, sp):
    def ropek(kref, c, s):
        kv = kref[...]
        return jnp.concatenate([_rope(kv[:, :LANE], c, s), _rope(kv[:, LANE:], c, s)], axis=1)

    K = jnp.concatenate([ropek(kp_ref, cp, sp), ropek(kc_ref, cq, sq)], axis=0).astype(MXU)
    V = jnp.concatenate([vp_ref[...], vc_ref[...]], axis=0).astype(MXU)
    return K, V


def _swa_in_specs(nb, last):
    def cur(n):
        return jnp.minimum(n, last)

    def prev(n):
        return jnp.maximum(cur(n) - 1, 0)

    kd, vd = C_KD // 256, C_VD // 256
    return [
        pl.BlockSpec((BLK, D), lambda n: (cur(n), C_Q // D)),
        pl.BlockSpec((BLK, 256), lambda n: (cur(n), kd)),
        pl.BlockSpec((BLK, 256), lambda n: (prev(n), kd)),
        pl.BlockSpec((BLK, 256), lambda n: (cur(n), vd)),
        pl.BlockSpec((BLK, 256), lambda n: (prev(n), vd)),
        pl.BlockSpec((BLK, LANE), lambda n: (cur(n), 0)),
        pl.BlockSpec((BLK, LANE), lambda n: (cur(n), 0)),
        pl.BlockSpec((BLK, LANE), lambda n: (prev(n), 0)),
        pl.BlockSpec((BLK, LANE), lambda n: (prev(n), 0)),
    ]


def _swa_fwd(proj, cos, sin, sinks):
    T = proj.shape[0]
    nb = T // BLK
    scale = A_HD ** -0.5

    def body(sinks_ref, q_ref, kc_ref, kp_ref, vc_ref, vp_ref, cq_ref, sq_ref, cp_ref, sp_ref, o_ref, l_ref):
        n = pl.program_id(0)
        cq, sq = cq_ref[...], sq_ref[...]
        K, V = _swa_keys(kc_ref, kp_ref, vc_ref, vp_ref, cq, sq, cp_ref[...], sp_ref[...])
        tri = _swa_tri()
        valid = tri | (n > 0)
        lane, rope_sub0, std_sub0 = _swa_masks()
        group = A_HEADS // A_KV
        roped, lses = {}, []

        def products(head):
            pb, sub, g = head // 2, head % 2, head // group
            if sub == 0:
                roped[pb] = _rope(q_ref[:, pb * LANE:(pb + 1) * LANE], cq, sq)
            qm = jnp.where(rope_sub0 if sub == 0 else ~rope_sub0, roped[pb], 0.0).astype(MXU)
            return _dot_nt(qm, K[:, g * LANE:(g + 1) * LANE])

        def softmax(head, s_full):
            s = jnp.where(valid, _swa_fold(s_full, tri) * scale, NEG)
            sink = sinks_ref[0, head]
            m = jnp.maximum(jnp.max(s, axis=1, keepdims=True), sink)
            e = jnp.exp(s - m)
            den = jnp.sum(e, axis=1, keepdims=True) + jnp.exp(sink - m)
            lses.append(m + jnp.log(den))
            return _swa_unfold(e / den, tri).astype(MXU)

        outs = {}
        st1 = {0: products(0), 1: products(1)}
        st2 = {0: softmax(0, st1.pop(0))}
        for head in range(A_HEADS):
            if head + 2 < A_HEADS:
                st1[head + 2] = products(head + 2)
            if head + 1 < A_HEADS:
                st2[head + 1] = softmax(head + 1, st1.pop(head + 1))
            g = head // group
            outs[head] = _dot(st2.pop(head), V[:, g * LANE:(g + 1) * LANE])
            if head % 2 == 1:
                pb = head // 2
                o_ref[:, pb * LANE:(pb + 1) * LANE] = jnp.where(std_sub0, outs[head - 1], outs[head])
        lacc = jnp.zeros((BLK, LANE), F32)
        for head in range(A_HEADS):
            lacc = jnp.where(lane == head, lses[head], lacc)
        l_ref[...] = lacc

    return pl.pallas_call(
        body, name="swa_fwd", grid=(nb,),
        in_specs=[pl.BlockSpec(memory_space=pltpu.SMEM)] + _swa_in_specs(nb, nb - 1),
        out_specs=[pl.BlockSpec((BLK, D), lambda n: (n, 0)), pl.BlockSpec((BLK, LANE), lambda n: (n, 0))],
        out_shape=[jax.ShapeDtypeStruct((T, D), F32), jax.ShapeDtypeStruct((T, LANE), F32)],
        compiler_params=_cp(("parallel",)),
    )(sinks, proj, proj, proj, proj, proj, cos, sin, cos, sin)


def _swa_bwd(proj, cos, sin, sinks, do_a, o_a, lse):
    T = proj.shape[0]
    nb = T // BLK
    scale = A_HD ** -0.5

    def body(sinks_ref, q_ref, kc_ref, kp_ref, vc_ref, vp_ref, cq_ref, sq_ref, cp_ref, sp_ref,
             do_ref, o_ref, l_ref, dq_ref, dkv_ref, ds_ref, ckv_ref):
        n = pl.program_id(0)

        @pl.when(n == 0)
        def _():
            ckv_ref[...] = jnp.zeros_like(ckv_ref)
            ds_ref[...] = jnp.zeros_like(ds_ref)

        @pl.when(n < nb)
        def _():
            cq, sq, cp, sp = cq_ref[...], sq_ref[...], cp_ref[...], sp_ref[...]
            K, V = _swa_keys(kc_ref, kp_ref, vc_ref, vp_ref, cq, sq, cp, sp)
            tri = _swa_tri()
            valid = tri | (n > 0)
            lane, rope_sub0, std_sub0 = _swa_masks()
            lane_row = lax.broadcasted_iota(jnp.int32, (1, LANE), 1)
            lse_v = l_ref[...]
            dKt = [jnp.zeros((LANE, 2 * BLK), F32) for _ in range(A_KV)]
            dVt = [jnp.zeros((LANE, 2 * BLK), F32) for _ in range(A_KV)]
            dsinks, roped, roped_t, do_t = [], {}, {}, {}
            group = A_HEADS // A_KV
            dim = lax.broadcasted_iota(jnp.int32, (LANE, BLK), 0)
            rope_row0, std_row0 = ((dim // 32) % 2) == 0, dim < 64

            def products(head):
                pb, sub, g = head // 2, head % 2, head // group
                cols = slice(pb * LANE, (pb + 1) * LANE)
                Kg, Vg = K[:, g * LANE:(g + 1) * LANE], V[:, g * LANE:(g + 1) * LANE]
                if sub == 0:
                    roped[pb] = _rope(q_ref[:, cols], cq, sq)
                    roped_t[pb] = roped[pb].T
                    do_t[pb] = do_ref[:, cols].T
                qm = jnp.where(rope_sub0 if sub == 0 else ~rope_sub0, roped[pb], 0.0).astype(MXU)
                qmt = jnp.where(rope_row0 if sub == 0 else ~rope_row0, roped_t[pb], 0.0).astype(MXU)
                dov = jnp.where(std_sub0 if sub == 0 else ~std_sub0, do_ref[:, cols], 0.0)
                dovt = jnp.where(std_row0 if sub == 0 else ~std_row0, do_t[pb], 0.0).astype(MXU)
                delta = jnp.sum(dov * o_ref[:, cols], axis=1, keepdims=True)
                return qmt, dovt, delta, _dot_nt(qm, Kg), _dot_nt(dov.astype(MXU), Vg)

            def scores(head, qmt, dovt, delta, s_full, dp_full):
                lh = jnp.sum(jnp.where(lane == head, lse_v, 0.0), axis=1, keepdims=True)
                p = jnp.where(valid, jnp.exp(_swa_fold(s_full, tri) * scale - lh), 0.0)
                psink = jnp.exp(sinks_ref[0, head] - lh)
                dsinks.append(jnp.sum(-psink * delta, axis=0, keepdims=True))
                dsq = (p * (_swa_fold(dp_full, tri) - delta)) * scale
                return qmt, dovt, _swa_unfold(p, tri).astype(MXU), _swa_unfold(dsq, tri).astype(MXU)

            def grads(head, qmt, dovt, pb16, dsc):
                g = head // group
                dKt[g] = dKt[g] + _dot(qmt, dsc)
                dVt[g] = dVt[g] + _dot(dovt, pb16)
                return _dot(dsc, K[:, g * LANE:(g + 1) * LANE])

            dqs = {}
            st1 = {0: products(0), 1: products(1)}
            st2 = {0: scores(0, *st1.pop(0))}
            for head in range(A_HEADS):
                if head + 2 < A_HEADS:
                    st1[head + 2] = products(head + 2)
                if head + 1 < A_HEADS:
                    st2[head + 1] = scores(head + 1, *st1.pop(head + 1))
                dqs[head] = grads(head, *st2.pop(head))
                if head % 2 == 1:
                    pb = head // 2
                    dqp = jnp.where(rope_sub0, dqs[head - 1], dqs[head])
                    dq_ref[:, pb * LANE:(pb + 1) * LANE] = _rope_bwd(dqp, cq, sq).astype(dq_ref.dtype)
            dsink = jnp.zeros((1, LANE), F32)
            for head in range(A_HEADS):
                dsink = jnp.where(lane_row == head, dsinks[head], dsink)
            dK, dV = [a.T for a in dKt], [a.T for a in dVt]
            prev = ([_rope_bwd(dK[g][:BLK], cp, sp) for g in range(A_KV)] + [dV[g][:BLK] for g in range(A_KV)])
            cur_ = ([_rope_bwd(dK[g][BLK:], cq, sq) for g in range(A_KV)] + [dV[g][BLK:] for g in range(A_KV)])
            dkv_ref[...] = (ckv_ref[...] + jnp.concatenate(prev, axis=1)).astype(dkv_ref.dtype)
            ckv_ref[...] = jnp.concatenate(cur_, axis=1)
            ds_ref[...] = ds_ref[...] + jnp.broadcast_to(dsink, ds_ref.shape)

        @pl.when(n == nb)
        def _():
            dkv_ref[...] = ckv_ref[...].astype(dkv_ref.dtype)

    last = nb - 1

    def cur(n):
        return jnp.minimum(n, last)

    def out_kv(n):
        return (jnp.maximum(n - 1, 0), 0)

    return pl.pallas_call(
        body, name="swa_bwd", grid=(nb + 1,),
        in_specs=[pl.BlockSpec(memory_space=pltpu.SMEM)] + _swa_in_specs(nb, last) + [
            pl.BlockSpec((BLK, D), lambda n: (cur(n), 0)),
            pl.BlockSpec((BLK, D), lambda n: (cur(n), 0)),
            pl.BlockSpec((BLK, LANE), lambda n: (cur(n), 0)),
        ],
        out_specs=[
            pl.BlockSpec((BLK, D), lambda n: (cur(n), 0)),
            pl.BlockSpec((BLK, 512), out_kv),
            pl.BlockSpec((8, LANE), lambda n: (0, 0)),
        ],
        out_shape=[
            jax.ShapeDtypeStruct((T, D), MXU),
            jax.ShapeDtypeStruct((T, 512), MXU),
            jax.ShapeDtypeStruct((8, LANE), F32),
        ],
        scratch_shapes=[pltpu.VMEM((BLK, 512), F32)],
        compiler_params=_cp(("arbitrary",)),
    )(sinks, proj, proj, proj, proj, proj, cos, sin, cos, sin, do_a, o_a, lse)


def _gla_gate(bl_ref, gu_ref, bias_ref):
    gk = _dot(bl_ref[...].astype(MXU), gu_ref[...]) + bias_ref[...]
    la = (jnp.minimum(gk, 0.0) - jnp.log(1.0 + jnp.exp(-jnp.abs(gk)))) / TAU
    ri = lax.broadcasted_iota(jnp.int32, (CHUNK, CHUNK), 0)
    ci = lax.broadcasted_iota(jnp.int32, (CHUNK, CHUNK), 1)
    b = _dot_f32(jnp.where(ci <= ri, 1.0, 0.0).astype(F32), la)
    return gk, la, b, ri, ci


def _gla_head(q_ref, k_ref, la, b, h):
    sl = slice(h * B_DK, (h + 1) * B_DK)
    bh = b[:, sl]
    blast = jnp.sum(la[:, sl], axis=0, keepdims=True)
    qc = q_ref[:, sl] * (B_DK ** -0.5)
    kh = k_ref[:, sl]
    eb, enb, esb = jnp.exp(bh), jnp.exp(-bh), jnp.exp(blast - bh)
    return qc * eb, kh * enb, kh * esb, eb, enb, esb, jnp.exp(blast)


def _gla_specs(chunk_of):
    return [
        pl.BlockSpec((CHUNK, 512), lambda i: (chunk_of(i), C_BQ // 512)),
        pl.BlockSpec((CHUNK, 512), lambda i: (chunk_of(i), C_BK // 512)),
        pl.BlockSpec((CHUNK, D), lambda i: (chunk_of(i), C_BV // D)),
        pl.BlockSpec((CHUNK, W_BL), lambda i: (chunk_of(i), C_BL // W_BL)),
        pl.BlockSpec((W_BL, 512), lambda i: (0, 0)),
        pl.BlockSpec((1, 512), lambda i: (0, 0)),
    ]


def _gla_fwd(proj, gu_pad, bias):
    T = proj.shape[0]
    nc = T // CHUNK

    def body(q_ref, k_ref, v_ref, bl_ref, gu_ref, bias_ref, o_ref, st_ref, state_ref):
        @pl.when(pl.program_id(0) == 0)
        def _():
            state_ref[...] = jnp.zeros_like(state_ref)

        _, la, b, ri, ci = _gla_gate(bl_ref, gu_ref, bias_ref)
        st_ref[...] = state_ref[...]
        for h in range(B_HEADS):
            q_e, k_e, k_s, _, _, _, decay = _gla_head(q_ref, k_ref, la, b, h)
            vh = v_ref[:, h * B_DV:(h + 1) * B_DV].astype(MXU)
            rows = slice(h * B_DV, (h + 1) * B_DV)
            q_eb = q_e.astype(MXU)
            att = jnp.where(ci <= ri, _dot_nt(q_eb, k_e.astype(MXU)), 0.0)
            st = state_ref[rows, :]
            o_ref[:, rows] = _dot(att.astype(MXU), vh) + _dot_nt(q_eb, st.astype(MXU))
            state_ref[rows, :] = st * decay + _dot_tn(vh, k_s.astype(MXU))

    return pl.pallas_call(
        body, name="gla_fwd", grid=(nc,),
        in_specs=_gla_specs(lambda i: i),
        out_specs=[pl.BlockSpec((CHUNK, D), lambda i: (i, 0)),
                   pl.BlockSpec((B_HEADS * B_DV, B_DK), lambda i: (i, 0))],
        out_shape=[jax.ShapeDtypeStruct((T, D), F32),
                   jax.ShapeDtypeStruct((nc * B_HEADS * B_DV, B_DK), F32)],
        scratch_shapes=[pltpu.VMEM((B_HEADS * B_DV, B_DK), F32)],
        compiler_params=_cp(("arbitrary",)),
    )(proj, proj, proj, proj, gu_pad, bias)


def _gla_bwd(proj, gu_pad, bias, states, do_b):
    T = proj.shape[0]
    nc = T // CHUNK
    o_q, o_k = C_BQ - C_GLA, C_BK - C_GLA

    def body(q_ref, k_ref, v_ref, bl_ref, gu_ref, bias_ref, st_ref, do_ref,
             dg_ref, dbl_ref, ggu_ref, gbias_ref, gt_ref):
        @pl.when(pl.program_id(0) == 0)
        def _():
            gt_ref[...] = jnp.zeros_like(gt_ref)
            ggu_ref[...] = jnp.zeros_like(ggu_ref)
            gbias_ref[...] = jnp.zeros_like(gbias_ref)

        gk, la, b, ri, ci = _gla_gate(bl_ref, gu_ref, bias_ref)
        causal = ci <= ri
        upper = jnp.where(ci >= ri, 1.0, 0.0).astype(F32)
        dla_parts = []
        for h in range(B_HEADS):
            q_e, k_e, k_s, eb, enb, esb, decay = _gla_head(q_ref, k_ref, la, b, h)
            rows = slice(h * B_DV, (h + 1) * B_DV)
            sl = slice(h * B_DK, (h + 1) * B_DK)
            vh = v_ref[:, rows].astype(MXU)
            doh = do_ref[:, rows].astype(MXU)
            q_eb, k_eb, k_sb = q_e.astype(MXU), k_e.astype(MXU), k_s.astype(MXU)
            st = st_ref[rows, :]
            gt = gt_ref[rows, :]
            gtb = gt.astype(MXU)
            att = jnp.where(causal, _dot_nt(q_eb, k_eb), 0.0).astype(MXU)
            datt = jnp.where(causal, _dot_nt(doh, vh), 0.0).astype(MXU)
            dq_e = _dot(datt, k_eb) + _dot(doh, st.astype(MXU))
            dk_e = _dot_tn(datt, q_eb)
            dk_s = _dot(vh, gtb)
            dg_ref[:, rows] = (_dot_tn(att, doh) + _dot_nt(k_sb, gtb)).astype(dg_ref.dtype)
            ddecay = jnp.sum(gt * st, axis=0, keepdims=True)
            gt_ref[rows, :] = gt * decay + _dot_tn(doh, q_eb)
            dg_ref[:, o_q + h * B_DK:o_q + (h + 1) * B_DK] = (dq_e * eb * (B_DK ** -0.5)).astype(dg_ref.dtype)
            dg_ref[:, o_k + h * B_DK:o_k + (h + 1) * B_DK] = (dk_e * enb + dk_s * esb).astype(dg_ref.dtype)
            dks_ks = dk_s * k_s
            db = dq_e * q_e - dk_e * k_e - dks_ks
            dblast = jnp.sum(dks_ks, axis=0, keepdims=True) + ddecay * decay
            dla_parts.append(_dot_f32(upper, db) + dblast)
        dla = jnp.concatenate(dla_parts, axis=1)
        dgk = dla * (1.0 / TAU) * _sigmoid(-gk)
        dgkb = dgk.astype(MXU)
        dbl_ref[...] = _dot_nt(dgkb, gu_ref[...]).astype(dbl_ref.dtype)
        ggu_ref[...] = ggu_ref[...] + _dot_tn(bl_ref[...].astype(MXU), dgkb)
        gbias_ref[...] = gbias_ref[...] + jnp.broadcast_to(jnp.sum(dgk, axis=0, keepdims=True), gbias_ref.shape)

    def rev(i):
        return nc - 1 - i

    return pl.pallas_call(
        body, name="gla_bwd", grid=(nc,),
        in_specs=_gla_specs(rev) + [
            pl.BlockSpec((B_HEADS * B_DV, B_DK), lambda i: (rev(i), 0)),
            pl.BlockSpec((CHUNK, D), lambda i: (rev(i), 0)),
        ],
        out_specs=[
            pl.BlockSpec((CHUNK, W_GLA), lambda i: (rev(i), 0)),
            pl.BlockSpec((CHUNK, W_BL), lambda i: (rev(i), 0)),
            pl.BlockSpec((W_BL, 512), lambda i: (0, 0)),
            pl.BlockSpec((8, 512), lambda i: (0, 0)),
        ],
        out_shape=[
            jax.ShapeDtypeStruct((T, W_GLA), MXU),
            jax.ShapeDtypeStruct((T, W_BL), MXU),
            jax.ShapeDtypeStruct((W_BL, 512), F32),
            jax.ShapeDtypeStruct((8, 512), F32),
        ],
        scratch_shapes=[pltpu.VMEM((B_HEADS * B_DV, B_DK), F32)],
        compiler_params=_cp(("arbitrary",)),
    )(proj, proj, proj, proj, gu_pad, bias, states, do_b)


def _mid(x, target, proj, o_a, o_b, w_a, w_b, w_out, w_bn4, fnw):
    T = x.shape[0]
    tT = min(T, 128)
    nbuf = 4
    o_ag, o_bg, o_ma, o_mb = (c - C_GATES for c in (C_AG, C_BG, C_MA, C_MB))

    def body(x_ref, t_ref, oa_ref, ob_ref, gates_ref, wa_ref, wb_ref, wo_ref, wbn_ref, fnw_ref,
             dx2_ref, doa_ref, dob_ref, dgates_ref,
             gwa_ref, gwb_ref, gwo_ref, gfn_ref, gbn_ref, loss_ref, buf_ref):
        i = pl.program_id(0)

        @pl.when(i == 0)
        def _():
            for r in (gwa_ref, gwb_ref, gwo_ref, gfn_ref, gbn_ref, loss_ref):
                r[...] = jnp.zeros_like(r)

        rows = pl.ds(pl.multiple_of((i % nbuf) * tT, tT), tT)

        def keep(k, val):
            buf_ref[k, rows, :] = val

        oa, ag = oa_ref[...], gates_ref[:, o_ag:o_ag + D]
        sg_a = _sigmoid(ag)
        silu_a = ag * sg_a
        oag_b = (oa * silu_a).astype(MXU)
        keep(0, oag_b)
        y_a = _dot(oag_b, wa_ref[...])

        ob, bg = ob_ref[...], gates_ref[:, o_bg:o_bg + D]
        rbs, obhats = [], []
        for h in range(B_HEADS):
            obh = ob[:, h * B_DV:(h + 1) * B_DV]
            rb = lax.rsqrt(jnp.mean(obh * obh, axis=-1, keepdims=True) + EPS)
            rbs.append(rb)
            obhats.append(obh * rb)
        obhat = jnp.concatenate(obhats, axis=1)
        wbn = wbn_ref[...]
        obn = obhat * wbn
        sg_b = _sigmoid(bg)
        silu_b = bg * sg_b
        obg_b = (obn * silu_b).astype(MXU)
        keep(1, obg_b)
        y_b = _dot(obg_b, wb_ref[...])

        sa, sb = _sigmoid(gates_ref[:, o_ma:o_ma + D]), _sigmoid(gates_ref[:, o_mb:o_mb + D])
        mg_b = (sa * y_a + sb * y_b).astype(MXU)
        keep(2, mg_b)
        x2 = x_ref[...] + _dot(mg_b, wo_ref[...])
        r2 = lax.rsqrt(jnp.mean(x2 * x2, axis=-1, keepdims=True) + EPS)
        xh2 = x2 * r2
        fw = fnw_ref[...]
        err = xh2 * fw - t_ref[...]
        tok = jnp.mean(err * err, axis=-1, keepdims=True)
        loss_ref[...] = loss_ref[...] + 0.5 * jnp.sum(tok, axis=0, keepdims=True)

        dy = err * (1.0 / D)
        gfn_ref[...] = gfn_ref[...] + jnp.broadcast_to(jnp.sum(dy * xh2, axis=0, keepdims=True), gfn_ref.shape)
        gy = dy * fw
        dx2 = r2 * (gy - xh2 * jnp.mean(gy * xh2, axis=-1, keepdims=True))
        dx2_ref[...] = dx2
        dx2_b = dx2.astype(MXU)
        keep(5, dx2_b)
        dmg = _dot_nt(dx2_b, wo_ref[...])

        dgates_ref[:, o_ma:o_ma + D] = (dmg * y_a * sa * (1.0 - sa)).astype(dgates_ref.dtype)
        dgates_ref[:, o_mb:o_mb + D] = (dmg * y_b * sb * (1.0 - sb)).astype(dgates_ref.dtype)
        dya_b = (dmg * sa).astype(MXU)
        dyb_b = (dmg * sb).astype(MXU)
        keep(3, dya_b)
        keep(4, dyb_b)
        doag = _dot_nt(dya_b, wa_ref[...])
        dobg = _dot_nt(dyb_b, wb_ref[...])

        @pl.when(i % nbuf == nbuf - 1)
        def _():
            gwa_ref[...] = gwa_ref[...] + _dot_tn(buf_ref[0], buf_ref[3])
            gwb_ref[...] = gwb_ref[...] + _dot_tn(buf_ref[1], buf_ref[4])
            gwo_ref[...] = gwo_ref[...] + _dot_tn(buf_ref[2], buf_ref[5])

        doa_ref[...] = doag * silu_a
        dgates_ref[:, o_ag:o_ag + D] = (doag * oa * (sg_a * (1.0 + ag * (1.0 - sg_a)))).astype(dgates_ref.dtype)
        dobn = dobg * silu_b
        dgates_ref[:, o_bg:o_bg + D] = (dobg * obn * (sg_b * (1.0 + bg * (1.0 - sg_b)))).astype(dgates_ref.dtype)
        gg = dobn * wbn
        gbn = jnp.zeros((1, B_DV), F32)
        for h in range(B_HEADS):
            sl = slice(h * B_DV, (h + 1) * B_DV)
            gbn = gbn + jnp.sum(dobn[:, sl] * obhats[h], axis=0, keepdims=True)
            ggh = gg[:, sl]
            dob_ref[:, sl] = rbs[h] * (ggh - obhats[h] * jnp.mean(ggh * obhats[h], axis=-1, keepdims=True))
        gbn_ref[...] = gbn_ref[...] + jnp.broadcast_to(gbn, gbn_ref.shape)

    assert (T // tT) % nbuf == 0
    tile = pl.BlockSpec((tT, D), lambda i: (i, 0))
    row = pl.BlockSpec((1, D), lambda i: (0, 0))
    acc8 = pl.BlockSpec((8, D), lambda i: (0, 0))
    return pl.pallas_call(
        body, name="mid", grid=(T // tT,),
        in_specs=[tile, tile, tile, tile, pl.BlockSpec((tT, W_GATES), lambda i: (i, C_GATES // W_GATES)),
                  _vmem(), _vmem(), _vmem(), row, row],
        out_specs=[tile, tile, tile, pl.BlockSpec((tT, W_GATES), lambda i: (i, 0)), _vmem(), _vmem(), _vmem(),
                   acc8, pl.BlockSpec((8, B_DV), lambda i: (0, 0)), pl.BlockSpec((8, LANE), lambda i: (0, 0))],
        out_shape=[
            jax.ShapeDtypeStruct((T, D), F32),
            jax.ShapeDtypeStruct((T, D), F32),
            jax.ShapeDtypeStruct((T, D), F32),
            jax.ShapeDtypeStruct((T, W_GATES), MXU),
            jax.ShapeDtypeStruct((D, D), F32),
            jax.ShapeDtypeStruct((D, D), F32),
            jax.ShapeDtypeStruct((D, D), F32),
            jax.ShapeDtypeStruct((8, D), F32),
            jax.ShapeDtypeStruct((8, B_DV), F32),
            jax.ShapeDtypeStruct((8, LANE), F32),
        ],
        scratch_shapes=[pltpu.VMEM((6, nbuf * tT, D), MXU)],
        compiler_params=_cp(("arbitrary",)),
    )(x, target, o_a, o_b, proj, w_a, w_b, w_out, w_bn4, fnw)


def _gw_piece(h, dp, idx):
    T, w = dp.shape
    tn = min(w, 512)

    def body(h_ref, dp_ref, o_ref):
        o_ref[...] = _dot_tn(dp_ref[...], h_ref[...])

    return pl.pallas_call(
        body, name=f"gw_in_{idx}", grid=(w // tn,),
        in_specs=[pl.BlockSpec((T, D), lambda j: (0, 0)), pl.BlockSpec((T, tn), lambda j: (0, j))],
        out_specs=pl.BlockSpec((tn, D), lambda j: (j, 0)),
        out_shape=jax.ShapeDtypeStruct((w, D), F32),
        compiler_params=_cp(("parallel",)),
    )(h, dp)


def _chip_copies(s_ref, got_ref, send_sems, recv_sems):
    x, y, c = _place()
    chips = [(1 - x, y), (x, 1 - y), (1 - x, 1 - y)]
    return [pltpu.make_async_remote_copy(
        src_ref=s_ref.at[2 * px + py], dst_ref=got_ref.at[j],
        send_sem=send_sems.at[j], recv_sem=recv_sems.at[j], device_id=(px, py, c), device_id_type=MESH)
        for j, (px, py) in enumerate(chips)]


def _dh_norm(pieces, offsets, wf, x, dx2, norm_w, sums):
    T = x.shape[0]
    tT = min(T, 256)
    widths = [p.shape[1] for p in pieces]
    npc = len(pieces)
    last = T // tT - 1

    def body(*refs):
        dp_refs = refs[:npc]
        wf_ref, x_ref, dx2_ref, nw_ref, s_ref, gx_ref, gnw_ref, got_ref, send_sems, recv_sems = refs[npc:]

        @pl.when(pl.program_id(0) == 0)
        def _():
            gnw_ref[...] = jnp.zeros_like(gnw_ref)
            for cp in _chip_copies(s_ref, got_ref, send_sems, recv_sems):
                cp.start()

        dh = jnp.zeros((tT, D), F32)
        for dp_ref, off, w in zip(dp_refs, offsets, widths):
            dh = dh + _dot(dp_ref[...], wf_ref[off:off + w, :])
        xv = x_ref[...]
        r = lax.rsqrt(jnp.mean(xv * xv, axis=-1, keepdims=True) + EPS)
        xh = xv * r
        gnw_ref[...] = gnw_ref[...] + jnp.broadcast_to(jnp.sum(dh * xh, axis=0, keepdims=True), gnw_ref.shape)
        g = dh * nw_ref[...]
        gx_ref[...] = r * (g - xh * jnp.mean(g * xh, axis=-1, keepdims=True)) + dx2_ref[...]

        @pl.when(pl.program_id(0) == last)
        def _():
            copies = _chip_copies(s_ref, got_ref, send_sems, recv_sems)
            for cp in copies:
                cp.wait_recv()
            for cp in copies:
                cp.wait_send()

    tile = pl.BlockSpec((tT, D), lambda i: (i, 0))
    return pl.pallas_call(
        body, name="dh_norm", grid=(T // tT,),
        in_specs=[pl.BlockSpec((tT, w), lambda i: (i, 0)) for w in widths]
        + [_vmem(), tile, tile, pl.BlockSpec((1, D), lambda i: (0, 0)), _any()],
        out_specs=[tile, pl.BlockSpec((8, D), lambda i: (0, 0)), _any()],
        out_shape=[jax.ShapeDtypeStruct((T, D), F32), jax.ShapeDtypeStruct((8, D), F32),
                   jax.ShapeDtypeStruct((3, ROWS, D), sums.dtype)],
        scratch_shapes=[pltpu.SemaphoreType.DMA((3,)), pltpu.SemaphoreType.DMA((3,))],
        compiler_params=_cp(("arbitrary",)),
    )(*pieces, wf, x, dx2, norm_w, sums)


def _adamw_math(w, g, m, v):
    m = ADAM_B1 * m + (1.0 - ADAM_B1) * g
    v = ADAM_B2 * v + (1.0 - ADAM_B2) * (g * g)
    m_hat = m / (1.0 - ADAM_B1 ** ADAM_STEP)
    v_hat = v / (1.0 - ADAM_B2 ** ADAM_STEP)
    delta = -ADAM_LR * (m_hat / (jnp.sqrt(v_hat) + ADAM_EPS) + ADAM_WD * w)
    return delta, m, v


def _fetch_partials(s_ref, got_ref, buf, sems, r0, nrows):
    x, y, _ = _place()
    cps = [pltpu.make_async_copy(s_ref.at[2 * x + y, pl.ds(r0, nrows)], buf.at[0], sems.at[0])]
    cps += [pltpu.make_async_copy(got_ref.at[j, pl.ds(r0, nrows)], buf.at[1 + j], sems.at[1 + j]) for j in range(3)]
    for cp in cps:
        cp.start()
    for cp in cps:
        cp.wait()


SMALL_AT = dict(norm_w=0, fnw=8, bias=16, bn=24, sinks=32, loss=40)
ROW_AT = (R_IN, R_A, R_B, R_O)


def _finish(w_rows, m_rows, v_rows, ws, ms, vs, gu_w, gu_m, gu_v, small, sums, got):
    names = ["norm_w", "fnw", "bias", "bn", "sinks"]
    widths = [ws[n].shape[1] for n in names]
    shapes = [w.shape for w in w_rows]

    def body(*refs):
        wr_refs, mr_refs, vr_refs = refs[0:4], refs[4:8], refs[8:12]
        refs = refs[12:]
        w_refs, m_refs, v_refs = refs[0:5], refs[5:10], refs[10:15]
        guw_ref, gum_ref, guv_ref, small_ref, s_ref, got_ref = refs[15:21]
        loss_ref = refs[21]
        row_outs = refs[22:38]
        outs = refs[38:58]
        gu_outs = refs[58:62]
        smalls, tot, buf, gsh, send_sems, recv_sems, sems = refs[62:]
        x, y, c = _place()
        me_slot = 4 * x + 2 * y + c
        sends = []
        k = 0
        for dx in range(2):
            for dy in range(2):
                for dc in range(2):
                    if dx == 0 and dy == 0 and dc == 0:
                        continue
                    sends.append(pltpu.make_async_remote_copy(
                        src_ref=small_ref, dst_ref=smalls.at[me_slot],
                        send_sem=send_sems.at[k], recv_sem=recv_sems.at[k],
                        device_id=(x ^ dx, y ^ dy, c ^ dc), device_id_type=MESH))
                    k += 1
        for cp in sends:
            cp.start()
        smalls[me_slot] = small_ref[...]
        _fetch_partials(s_ref, got_ref, buf, sems, 0, ROWS)
        unshift = lax.rem(SHARD_PAD - 2 * me_slot, SHARD_PAD)
        for p in range(4):
            n, off = shapes[p][0], ROW_AT[p]
            nf = SHARD_PAD if p == 0 else n
            for cc in range(D // LANE):
                cols = slice(cc * LANE, (cc + 1) * LANE)
                g = buf[0, off:off + nf, cols].astype(F32)
                for j in range(1, 4):
                    g = g + buf[j, off:off + nf, cols].astype(F32)
                if p == 0:
                    gsh[...] = pltpu.roll(g, unshift, 0)
                    g = gsh[0:n, :]
                d, nm, nv = _adamw_math(wr_refs[p][:, cols], g, mr_refs[p][:, cols], vr_refs[p][:, cols])
                for o, val in zip(row_outs[4 * p:4 * p + 4], (g, d, nm, nv)):
                    o[:, cols] = val
        for cp in sends:
            cp.wait_recv()
        for cp in sends:
            cp.wait_send()
        acc = smalls[0]
        for d in range(1, NDEV):
            acc = acc + smalls[d]
        tot[...] = acc
        loss_ref[...] = tot[SMALL_AT["loss"]:SMALL_AT["loss"] + 1, 0:1]
        for p, (nm_, wd) in enumerate(zip(names, widths)):
            r = SMALL_AT[nm_]
            g = tot[r:r + 1, 0:wd]
            d, nm, nv = _adamw_math(w_refs[p][...], g, m_refs[p][...], v_refs[p][...])
            for o, val in zip(outs[4 * p:4 * p + 4], (g, d, nm, nv)):
                o[...] = val
        g = buf[0, R_GU:R_GU + RANK, 0:64].astype(F32)
        for j in range(1, 4):
            g = g + buf[j, R_GU:R_GU + RANK, 0:64].astype(F32)
        d, nm, nv = _adamw_math(guw_ref[...], g, gum_ref[...], guv_ref[...])
        for o, val in zip(gu_outs, (g, d, nm, nv)):
            o[...] = val

    out_shape = ([jax.ShapeDtypeStruct((1, 1), F32)]
                 + [jax.ShapeDtypeStruct(s, F32) for s in shapes for _ in range(4)]
                 + [jax.ShapeDtypeStruct((1, wd), F32) for wd in widths for _ in range(4)]
                 + [jax.ShapeDtypeStruct((RANK, 64), F32)] * 4)
    res = pl.pallas_call(
        body, name="finish",
        in_specs=[_vmem()] * 31 + [_any(), _any()],
        out_specs=[_vmem()] * 41,
        out_shape=out_shape,
        scratch_shapes=[pltpu.VMEM((NDEV, SMALL_ROWS, D), F32), pltpu.VMEM((SMALL_ROWS, D), F32),
                        pltpu.VMEM((4, ROWS, D), sums.dtype), pltpu.VMEM((SHARD_PAD, LANE), F32),
                        pltpu.SemaphoreType.DMA((7,)), pltpu.SemaphoreType.DMA((7,)), pltpu.SemaphoreType.DMA((4,))],
        compiler_params=_cp(),
    )(*w_rows, *m_rows, *v_rows, *[ws[n] for n in names], *[ms[n] for n in names], *[vs[n] for n in names],
      gu_w, gu_m, gu_v, small, sums, got)
    loss = res[0]
    per = {n: tuple(res[17 + 4 * p:21 + 4 * p]) for p, n in enumerate(names)}
    return loss, tuple(res[1:17]), per, tuple(res[37:41])


def _place():
    x, y, c = lax.axis_index("x"), lax.axis_index("y"), lax.axis_index("c")
    return x, y, c


def _gather_blocks(w_in_t, w_a_s, w_b_s, w_o_s, gu_s, xs, norm_w, pos_col):
    rows, cols = ROWS, D
    T = xs.shape[0]
    tT = min(T, 256)
    inv_row, sign_row = _rope_rows()

    def body(wi_ref, wa_ref, wb_ref, wo_ref, gu_ref, xs_ref, nw_ref, pos_ref, inv_ref, sign_ref,
             out_ref, h_ref, cos_ref, sin_ref, x_ref, frame_ref, send_sems, recv_sems, local_sem):
        x, y, c = _place()
        me, sibling = (x, y, c), (x, y, 1 - c)
        chips = [(1 - x, y), (x, 1 - y), (1 - x, 1 - y)]
        shift = 2 * (4 * x + 2 * y + c)
        frame_ref[SHARD - SHARD % 8:, :] = jnp.zeros((SHARD_PAD - SHARD + SHARD % 8, D), F32)
        frame_ref[:SHARD, :] = wi_ref[...]
        for cc in range(D // LANE):
            cs = slice(cc * LANE, (cc + 1) * LANE)
            x_ref[R_IN:R_IN + SHARD_PAD, cs] = pltpu.roll(frame_ref[:, cs], shift, 0).astype(x_ref.dtype)
        x_ref[R_A:R_A + 128, :] = wa_ref[...].astype(x_ref.dtype)
        x_ref[R_B:R_B + 128, :] = wb_ref[...].astype(x_ref.dtype)
        x_ref[R_O:R_O + 128, :] = wo_ref[...].astype(x_ref.dtype)
        x_ref[R_GU:R_GU + RANK, :] = jnp.zeros((RANK, D), x_ref.dtype)
        x_ref[R_GU:R_GU + RANK, 0:64] = gu_ref[...].astype(x_ref.dtype)

        def slot(px, py, pc):
            return out_ref.at[4 * px + 2 * py + pc]

        def copy(k, block, to, src=None):
            return pltpu.make_async_remote_copy(
                src_ref=slot(*block) if src is None else src, dst_ref=slot(*block),
                send_sem=send_sems.at[k], recv_sem=recv_sems.at[k], device_id=to, device_id_type=MESH)

        mine = pltpu.make_async_copy(x_ref, slot(*me), local_sem)
        mine.start()
        first = [copy(0, me, sibling, src=x_ref)]
        first += [copy(1 + j, me, (*chip, c), src=x_ref) for j, chip in enumerate(chips)]
        for cp in first:
            cp.start()

        @pl.loop(0, T // tT)
        def _(i):
            rows_i = pl.ds(pl.multiple_of(i * tT, tT), tT)
            _prologue_rows(rows_i, xs_ref, nw_ref, pos_ref, inv_ref, sign_ref, h_ref, cos_ref, sin_ref)

        passed = [copy(4 + j, (*chip, c), sibling) for j, chip in enumerate(chips)]
        for j, chip in enumerate(chips):
            copy(1 + j, (*chip, c), me).wait_recv()
            passed[j].start()
        copy(0, sibling, me).wait_recv()
        for j, chip in enumerate(chips):
            copy(4 + j, (*chip, 1 - c), me).wait_recv()
        for cp in first + passed:
            cp.wait_send()
        mine.wait()

    return pl.pallas_call(
        body, name="gather_weights",
        in_specs=[_vmem()] * 10, out_specs=[_any()] + [_vmem()] * 3,
        out_shape=[jax.ShapeDtypeStruct((NDEV, rows, cols), WIRE), jax.ShapeDtypeStruct((T, D), MXU),
                   jax.ShapeDtypeStruct((T, LANE), F32), jax.ShapeDtypeStruct((T, LANE), F32)],
        scratch_shapes=[pltpu.VMEM((rows, cols), WIRE), pltpu.VMEM((SHARD_PAD, D), F32),
                        pltpu.SemaphoreType.DMA((7,)), pltpu.SemaphoreType.DMA((7,)), pltpu.SemaphoreType.DMA],
        compiler_params=_cp(),
    )(w_in_t, w_a_s, w_b_s, w_o_s, gu_s, xs, norm_w, pos_col, inv_row, sign_row)


def _pair_reduce(packed):
    def body(p_ref, out_ref, got, own, send_sems, recv_sems, own_sems):
        x, y, c = _place()
        sends = [pltpu.make_async_remote_copy(
            src_ref=p_ref.at[2 * chip + (1 - c)], dst_ref=got.at[chip],
            send_sem=send_sems.at[chip], recv_sem=recv_sems.at[chip], device_id=(x, y, 1 - c), device_id_type=MESH)
            for chip in range(4)]
        loads = [pltpu.make_async_copy(p_ref.at[2 * chip + c], own.at[chip], own_sems.at[chip]) for chip in range(4)]
        for cp in sends + loads:
            cp.start()
        for chip in range(4):
            loads[chip].wait()
            sends[chip].wait_recv()
            out_ref[chip] = (own[chip].astype(F32) + got[chip].astype(F32)).astype(out_ref.dtype)
        for cp in sends:
            cp.wait_send()

    return pl.pallas_call(
        body, name="pair_reduce",
        in_specs=[_any()], out_specs=_vmem(),
        out_shape=jax.ShapeDtypeStruct((4, ROWS, D), packed.dtype),
        scratch_shapes=[pltpu.VMEM((4, ROWS, D), packed.dtype), pltpu.VMEM((4, ROWS, D), packed.dtype),
                        pltpu.SemaphoreType.DMA((4,)), pltpu.SemaphoreType.DMA((4,)), pltpu.SemaphoreType.DMA((4,))],
        compiler_params=_cp(),
    )(packed)


def _pad_cols(a, cols):
    return jnp.pad(a, ((0, 0), (0, cols - a.shape[1])))


def _pad_rows(a, rows):
    return jnp.pad(a, ((0, rows - a.shape[0]), (0, 0)))


FRAME = 928


def _join_frames(frames):
    head = frames[:, :FRAME].at[1:, :16].add(frames[:-1, FRAME:])
    return jnp.concatenate([head.reshape(NDEV * FRAME, D), frames[NDEV - 1, FRAME:]], axis=0)


def _build_wft(wt):
    q = wt[0:1024].reshape(8, 2, 2, 32, D).transpose(0, 2, 1, 3, 4).reshape(1024, D)
    k = wt[1024:1152].reshape(2, 2, 1, 32, D)
    kd = jnp.broadcast_to(k, (2, 2, 2, 32, D)).reshape(256, D)
    v = wt[1152:1280].reshape(2, 1, 64, D)
    vd = jnp.broadcast_to(v, (2, 2, 64, D)).reshape(256, D)
    ag, bq, bk = wt[1280:2304], wt[2304:2816], wt[2816:3328]
    bv, bg, bl = wt[3328:4352], wt[4352:5376], wt[5376:5392]
    ma, mb = wt[5392:6416], wt[6416:7440]
    return jnp.concatenate([q, kd, vd, _pad_rows(bl, C_GLA - C_BL), bv, bq, bk, ag, bg, ma, mb], axis=0)


def _unbuild_gwt(gq, gkv, gbl, ggla, ggates):
    q = gq.reshape(8, 2, 2, 32, D).transpose(0, 2, 1, 3, 4).reshape(1024, D)
    k = gkv[:256].reshape(2, 2, 2, 32, D).sum(axis=2).reshape(128, D)
    v = gkv[256:].reshape(2, 2, 64, D).sum(axis=1).reshape(128, D)
    bv, bq, bk = ggla[:1024], ggla[1024:1536], ggla[1536:]
    ag, bg, ma, mb = (ggates[i * D:(i + 1) * D] for i in range(4))
    return jnp.concatenate([q, k, v, ag, bq, bk, bv, bg, gbl[:RANK], ma, mb], axis=0)


def kernel(x, positions, norm_w, w_in, a_sinks, b_gate_up, b_gate_bias, b_out_norm_w, w_a_proj, w_b_proj, w_out, final_norm_w, loss_target, m_norm_w, m_w_in, m_a_sinks, m_b_gate_up, m_b_gate_bias, m_b_out_norm_w, m_w_a_proj, m_w_b_proj, m_w_out, m_final_norm_w, v_norm_w, v_w_in, v_a_sinks, v_b_gate_up, v_b_gate_bias, v_b_out_norm_w, v_w_a_proj, v_w_b_proj, v_w_out, v_final_norm_w):
    T = x.shape[1]
    xs, target = x[0], loss_target[0]
    fnw = final_norm_w.reshape(1, D)
    allw, h, cos, sin = _gather_blocks(w_in[0].T, w_a_proj[0], w_b_proj[0], w_out[0], b_gate_up[0],
                                       xs, norm_w, positions.reshape(T, 1))
    wf = _build_wft(_join_frames(allw[:, :SHARD_PAD]))
    w_a = allw[:, R_A:R_A + 128, :].reshape(D, D)
    w_b = allw[:, R_B:R_B + 128, :].reshape(D, D)
    w_o = allw[:, R_O:R_O + 128, :].reshape(D, D)
    gu = allw[:, R_GU:R_GU + RANK, :64].transpose(1, 0, 2).reshape(RANK, 512)
    gu_pad = _pad_rows(gu, W_BL)

    proj = _proj(h, wf)
    o_a, lse = _swa_fwd(proj, cos, sin, a_sinks)
    o_b, states = _gla_fwd(proj, gu_pad, b_gate_bias)
    (dx2, do_a, do_b, d_gates, g_wa, g_wb, g_wo, g_fn, g_bn, loss_part) = _mid(
        xs, target, proj, o_a, o_b, w_a, w_b, w_o, jnp.tile(b_out_norm_w, (1, B_HEADS)), fnw)
    d_q, d_kv, g_sinks = _swa_bwd(proj, cos, sin, a_sinks, do_a, o_a, lse)
    d_gla, d_bl, g_gu, g_bias = _gla_bwd(proj, gu_pad, b_gate_bias, states, do_b)
    pieces = [d_q, d_kv, d_bl, d_gla, d_gates]
    offsets = [C_Q, C_KD, C_BL, C_GLA, C_GATES]
    gw = [_gw_piece(h, dp, nm) for nm, dp in zip(["q", "kv", "bl", "gla", "gates"], pieces)]

    gwt = _unbuild_gwt(*gw).astype(WIRE)
    ggu = g_gu[:RANK].reshape(RANK, NDEV, 64).transpose(1, 0, 2)
    packed = jnp.concatenate([
        jnp.stack([gwt[FRAME * d:FRAME * d + SHARD_PAD] for d in range(NDEV)]),
        g_wa.reshape(NDEV, 128, D).astype(WIRE),
        g_wb.reshape(NDEV, 128, D).astype(WIRE),
        g_wo.reshape(NDEV, 128, D).astype(WIRE),
        jnp.pad(ggu, ((0, 0), (0, 0), (0, D - 64))).astype(WIRE),
    ], axis=1)
    sums = _pair_reduce(packed)
    grad_x, g_nw, from_chips = _dh_norm(pieces, offsets, wf, xs, dx2, norm_w, sums)

    small = jnp.concatenate([g_nw, g_fn, _pad_cols(g_bias, D), _pad_cols(g_bn, D), _pad_cols(g_sinks, D),
                             _pad_cols(loss_part, D)], axis=0)
    ws = dict(norm_w=norm_w, fnw=fnw, bias=b_gate_bias, bn=b_out_norm_w, sinks=a_sinks)
    ms = dict(norm_w=m_norm_w, fnw=m_final_norm_w.reshape(1, D), bias=m_b_gate_bias, bn=m_b_out_norm_w,
              sinks=m_a_sinks)
    vs = dict(norm_w=v_norm_w, fnw=v_final_norm_w.reshape(1, D), bias=v_b_gate_bias, bn=v_b_out_norm_w,
              sinks=v_a_sinks)
    loss, t_rows, sm, t_gu = _finish(
        [w_in[0].T, w_a_proj[0], w_b_proj[0], w_out[0]], [m_w_in[0].T, m_w_a_proj[0], m_w_b_proj[0], m_w_out[0]],
        [v_w_in[0].T, v_w_a_proj[0], v_w_b_proj[0], v_w_out[0]],
        ws, ms, vs, b_gate_up[0], m_b_gate_up[0], v_b_gate_up[0], small, sums, from_chips)

    def outputs(k):
        return [sm["norm_w"][k], t_rows[k].T[None], sm["sinks"][k], t_gu[k][None], sm["bias"][k], sm["bn"][k],
                t_rows[4 + k][None], t_rows[8 + k][None], t_rows[12 + k][None], sm["fnw"][k].reshape(D)]

    return (loss[0, 0], grad_x[None], *outputs(0), *outputs(1), *outputs(2), *outputs(3))
```

```python
import functools

import numpy as np
import jax
import jax.numpy as jnp
from jax import lax
from jax.experimental import pallas as pl
from jax.experimental.pallas import tpu as pltpu

F32 = jnp.float32
MXU = jnp.bfloat16
WIRE = jnp.bfloat16

D = 1024
A_HEADS, A_KV, A_HD = 16, 2, 64
BLK = 128
B_HEADS, B_DK, B_DV = 4, 128, 256
RANK, TAU, CHUNK = 16, 16.0, 64
EPS, NEG = 1e-5, -1e30
ROPE_THETA = 10000.0
IN_WIDTH, NDEV = 7440, 8
SHARD = IN_WIDTH // NDEV
LANE = 128

C_Q, C_KD, C_VD, C_BL = 0, 1024, 1280, 1536
C_BV, C_BQ, C_BK = 2048, 3072, 3584
C_AG, C_BG, C_MA, C_MB = 4096, 5120, 6144, 7168
C_GLA, W_GLA, C_GATES, W_GATES = 2048, 2048, 4096, 4096
NF = 8192
W_BL = 128

SHARD_PAD = 944
R_IN, R_A, R_B, R_O, R_GU, ROWS = 0, 944, 1072, 1200, 1328, 1344
SMALL_ROWS = 48

ADAM_LR, ADAM_B1, ADAM_B2, ADAM_EPS, ADAM_WD, ADAM_STEP = 0.001, 0.9, 0.999, 1e-08, 0.01, 10

MESH = pl.DeviceIdType.MESH
VMEM_LIMIT = 56 * 1024 * 1024


def _cp(sem=None, **kw):
    if sem is not None:
        kw["dimension_semantics"] = sem
    return pltpu.CompilerParams(vmem_limit_bytes=VMEM_LIMIT, **kw)


def _dot(a, b):
    return jnp.dot(a, b, preferred_element_type=F32)


def _dot_nt(a, b):
    return lax.dot_general(a, b, (((1,), (1,)), ((), ())), preferred_element_type=F32)


def _dot_tn(a, b):
    return lax.dot_general(a, b, (((0,), (0,)), ((), ())), preferred_element_type=F32)


def _dot_f32(a, b):
    return jnp.dot(a, b, preferred_element_type=F32, precision=lax.Precision.HIGHEST)


def _sigmoid(z):
    return 0.5 * jnp.tanh(0.5 * z) + 0.5


def _rope(xp, cos, sin):
    return xp * cos + pltpu.roll(xp, 64, 1) * sin


def _rope_bwd(dy, cos, sin):
    return dy * cos - pltpu.roll(dy, 64, 1) * sin


def _vmem():
    return pl.BlockSpec(memory_space=pltpu.VMEM)


def _any():
    return pl.BlockSpec(memory_space=pl.ANY)


def _rope_rows():
    half = A_HD // 2
    inv = (np.float32(ROPE_THETA) ** (-np.arange(half, dtype=np.float32) / np.float32(half))).astype(np.float32)
    inv_row = jnp.asarray(np.tile(inv, 4)[None, :])
    sign_row = jnp.asarray(np.concatenate([-np.ones(64, np.float32), np.ones(64, np.float32)])[None, :])
    return inv_row, sign_row


def _prologue_rows(rows, x_ref, nw_ref, pos_ref, inv_ref, sign_ref, h_ref, cos_ref, sin_ref):
    xv = x_ref[rows, :]
    r = lax.rsqrt(jnp.mean(xv * xv, axis=-1, keepdims=True) + EPS)
    h_ref[rows, :] = ((xv * r) * nw_ref[...]).astype(h_ref.dtype)
    ang = pos_ref[rows, :].astype(F32) * inv_ref[...]
    cos_ref[rows, :] = jnp.cos(ang)
    sin_ref[rows, :] = jnp.sin(ang) * sign_ref[...]


def _proj(h, wft):
    T = h.shape[0]
    tT, tN = T, 512

    def body(h_ref, w_ref, o_ref):
        o_ref[...] = _dot_nt(h_ref[...], w_ref[...])

    return pl.pallas_call(
        body, name="proj", grid=(T // tT, NF // tN),
        in_specs=[pl.BlockSpec((tT, D), lambda i, j: (i, 0)), pl.BlockSpec((tN, D), lambda i, j: (j, 0))],
        out_specs=pl.BlockSpec((tT, tN), lambda i, j: (i, j)),
        out_shape=jax.ShapeDtypeStruct((T, NF), F32),
        compiler_params=_cp(("parallel", "parallel")),
    )(h, wft)


def _swa_masks():
    lane = lax.broadcasted_iota(jnp.int32, (BLK, LANE), 1)
    rope_sub0 = ((lane // 32) % 2) == 0
    std_sub0 = lane < 64
    return lane, rope_sub0, std_sub0


def _swa_tri():
    qi = lax.broadcasted_iota(jnp.int32, (BLK, BLK), 0)
    kj = lax.broadcasted_iota(jnp.int32, (BLK, BLK), 1)
    return kj <= qi


def _swa_fold(full, tri):
    return jnp.where(tri, full[:, BLK:], full[:, :BLK])


def _swa_unfold(sq, tri):
    return jnp.concatenate([jnp.where(tri, 0.0, sq), jnp.where(tri, sq, 0.0)], axis=1)


def _swa_keys(kc_ref, kp_ref, vc_ref, vp_ref, cq, sq, cp, sp):
    def ropek(kref, c, s):
        kv = kref[...]
        return jnp.concatenate([_rope(kv[:, :LANE], c, s), _rope(kv[:, LANE:], c, s)], axis=1)

    K = jnp.concatenate([ropek(kp_ref, cp, sp), ropek(kc_ref, cq, sq)], axis=0).astype(MXU)
    V = jnp.concatenate([vp_ref[...], vc_ref[...]], axis=0).astype(MXU)
    return K, V


def _swa_in_specs(nb, last):
    def cur(n):
        return jnp.minimum(n, last)

    def prev(n):
        return jnp.maximum(cur(n) - 1, 0)

    kd, vd = C_KD // 256, C_VD // 256
    return [
        pl.BlockSpec((BLK, D), lambda n: (cur(n), C_Q // D)),
        pl.BlockSpec((BLK, 256), lambda n: (cur(n), kd)),
        pl.BlockSpec((BLK, 256), lambda n: (prev(n), kd)),
        pl.BlockSpec((BLK, 256), lambda n: (cur(n), vd)),
        pl.BlockSpec((BLK, 256), lambda n: (prev(n), vd)),
        pl.BlockSpec((BLK, LANE), lambda n: (cur(n), 0)),
        pl.BlockSpec((BLK, LANE), lambda n: (cur(n), 0)),
        pl.BlockSpec((BLK, LANE), lambda n: (prev(n), 0)),
        pl.BlockSpec((BLK, LANE), lambda n: (prev(n), 0)),
    ]


def _swa_fwd(proj, cos, sin, sinks):
    T = proj.shape[0]
    nb = T // BLK
    scale = A_HD ** -0.5

    def body(sinks_ref, q_ref, kc_ref, kp_ref, vc_ref, vp_ref, cq_ref, sq_ref, cp_ref, sp_ref, o_ref, l_ref):
        n = pl.program_id(0)
        cq, sq = cq_ref[...], sq_ref[...]
        K, V = _swa_keys(kc_ref, kp_ref, vc_ref, vp_ref, cq, sq, cp_ref[...], sp_ref[...])
        tri = _swa_tri()
        valid = tri | (n > 0)
        lane, rope_sub0, std_sub0 = _swa_masks()
        group = A_HEADS // A_KV
        roped, lses = {}, []

        def products(head):
            pb, sub, g = head // 2, head % 2, head // group
            if sub == 0:
                roped[pb] = _rope(q_ref[:, pb * LANE:(pb + 1) * LANE], cq, sq)
            qm = jnp.where(rope_sub0 if sub == 0 else ~rope_sub0, roped[pb], 0.0).astype(MXU)
            return _dot_nt(qm, K[:, g * LANE:(g + 1) * LANE])

        def softmax(head, s_full):
            s = jnp.where(valid, _swa_fold(s_full, tri) * scale, NEG)
            sink = sinks_ref[0, head]
            m = jnp.maximum(jnp.max(s, axis=1, keepdims=True), sink)
            e = jnp.exp(s - m)
            den = jnp.sum(e, axis=1, keepdims=True) + jnp.exp(sink - m)
            lses.append(m + jnp.log(den))
            return _swa_unfold(e / den, tri).astype(MXU)

        outs = {}
        st1 = {0: products(0), 1: products(1)}
        st2 = {0: softmax(0, st1.pop(0))}
        for head in range(A_HEADS):
            if head + 2 < A_HEADS:
                st1[head + 2] = products(head + 2)
            if head + 1 < A_HEADS:
                st2[head + 1] = softmax(head + 1, st1.pop(head + 1))
            g = head // group
            outs[head] = _dot(st2.pop(head), V[:, g * LANE:(g + 1) * LANE])
            if head % 2 == 1:
                pb = head // 2
                o_ref[:, pb * LANE:(pb + 1) * LANE] = jnp.where(std_sub0, outs[head - 1], outs[head])
        lacc = jnp.zeros((BLK, LANE), F32)
        for head in range(A_HEADS):
            lacc = jnp.where(lane == head, lses[head], lacc)
        l_ref[...] = lacc

    return pl.pallas_call(
        body, name="swa_fwd", grid=(nb,),
        in_specs=[pl.BlockSpec(memory_space=pltpu.SMEM)] + _swa_in_specs(nb, nb - 1),
        out_specs=[pl.BlockSpec((BLK, D), lambda n: (n, 0)), pl.BlockSpec((BLK, LANE), lambda n: (n, 0))],
        out_shape=[jax.ShapeDtypeStruct((T, D), F32), jax.ShapeDtypeStruct((T, LANE), F32)],
        compiler_params=_cp(("parallel",)),
    )(sinks, proj, proj, proj, proj, proj, cos, sin, cos, sin)


def _swa_bwd(proj, cos, sin, sinks, do_a, o_a, lse):
    T = proj.shape[0]
    nb = T // BLK
    scale = A_HD ** -0.5

    def body(sinks_ref, q_ref, kc_ref, kp_ref, vc_ref, vp_ref, cq_ref, sq_ref, cp_ref, sp_ref,
             do_ref, o_ref, l_ref, dq_ref, dkv_ref, ds_ref, ckv_ref):
        n = pl.program_id(0)

        @pl.when(n == 0)
        def _():
            ckv_ref[...] = jnp.zeros_like(ckv_ref)
            ds_ref[...] = jnp.zeros_like(ds_ref)

        @pl.when(n < nb)
        def _():
            cq, sq, cp, sp = cq_ref[...], sq_ref[...], cp_ref[...], sp_ref[...]
            K, V = _swa_keys(kc_ref, kp_ref, vc_ref, vp_ref, cq, sq, cp, sp)
            tri = _swa_tri()
            valid = tri | (n > 0)
            lane, rope_sub0, std_sub0 = _swa_masks()
            lane_row = lax.broadcasted_iota(jnp.int32, (1, LANE), 1)
            lse_v = l_ref[...]
            dKt = [jnp.zeros((LANE, 2 * BLK), F32) for _ in range(A_KV)]
            dVt = [jnp.zeros((LANE, 2 * BLK), F32) for _ in range(A_KV)]
            dsinks, roped, roped_t, do_t = [], {}, {}, {}
            group = A_HEADS // A_KV
            dim = lax.broadcasted_iota(jnp.int32, (LANE, BLK), 0)
            rope_row0, std_row0 = ((dim // 32) % 2) == 0, dim < 64

            def products(head):
                pb, sub, g = head // 2, head % 2, head // group
                cols = slice(pb * LANE, (pb + 1) * LANE)
                Kg, Vg = K[:, g * LANE:(g + 1) * LANE], V[:, g * LANE:(g + 1) * LANE]
                if sub == 0:
                    roped[pb] = _rope(q_ref[:, cols], cq, sq)
                    roped_t[pb] = roped[pb].T
                    do_t[pb] = do_ref[:, cols].T
                qm = jnp.where(rope_sub0 if sub == 0 else ~rope_sub0, roped[pb], 0.0).astype(MXU)
                qmt = jnp.where(rope_row0 if sub == 0 else ~rope_row0, roped_t[pb], 0.0).astype(MXU)
                dov = jnp.where(std_sub0 if sub == 0 else ~std_sub0, do_ref[:, cols], 0.0)
                dovt = jnp.where(std_row0 if sub == 0 else ~std_row0, do_t[pb], 0.0).astype(MXU)
                delta = jnp.sum(dov * o_ref[:, cols], axis=1, keepdims=True)
                return qmt, dovt, delta, _dot_nt(qm, Kg), _dot_nt(dov.astype(MXU), Vg)

            def scores(head, qmt, dovt, delta, s_full, dp_full):
                lh = jnp.sum(jnp.where(lane == head, lse_v, 0.0), axis=1, keepdims=True)
                p = jnp.where(valid, jnp.exp(_swa_fold(s_full, tri) * scale - lh), 0.0)
                psink = jnp.exp(sinks_ref[0, head] - lh)
                dsinks.append(jnp.sum(-psink * delta, axis=0, keepdims=True))
                dsq = (p * (_swa_fold(dp_full, tri) - delta)) * scale
                return qmt, dovt, _swa_unfold(p, tri).astype(MXU), _swa_unfold(dsq, tri).astype(MXU)

            def grads(head, qmt, dovt, pb16, dsc):
                g = head // group
                dKt[g] = dKt[g] + _dot(qmt, dsc)
                dVt[g] = dVt[g] + _dot(dovt, pb16)
                return _dot(dsc, K[:, g * LANE:(g + 1) * LANE])

            dqs = {}
            st1 = {0: products(0), 1: products(1)}
            st2 = {0: scores(0, *st1.pop(0))}
            for head in range(A_HEADS):
                if head + 2 < A_HEADS:
                    st1[head + 2] = products(head + 2)
                if head + 1 < A_HEADS:
                    st2[head + 1] = scores(head + 1, *st1.pop(head + 1))
                dqs[head] = grads(head, *st2.pop(head))
                if head % 2 == 1:
                    pb = head // 2
                    dqp = jnp.where(rope_sub0, dqs[head - 1], dqs[head])
                    dq_ref[:, pb * LANE:(pb + 1) * LANE] = _rope_bwd(dqp, cq, sq).astype(dq_ref.dtype)
            dsink = jnp.zeros((1, LANE), F32)
            for head in range(A_HEADS):
                dsink = jnp.where(lane_row == head, dsinks[head], dsink)
            dK, dV = [a.T for a in dKt], [a.T for a in dVt]
            prev = ([_rope_bwd(dK[g][:BLK], cp, sp) for g in range(A_KV)] + [dV[g][:BLK] for g in range(A_KV)])
            cur_ = ([_rope_bwd(dK[g][BLK:], cq, sq) for g in range(A_KV)] + [dV[g][BLK:] for g in range(A_KV)])
            dkv_ref[...] = (ckv_ref[...] + jnp.concatenate(prev, axis=1)).astype(dkv_ref.dtype)
            ckv_ref[...] = jnp.concatenate(cur_, axis=1)
            ds_ref[...] = ds_ref[...] + jnp.broadcast_to(dsink, ds_ref.shape)

        @pl.when(n == nb)
        def _():
            dkv_ref[...] = ckv_ref[...].astype(dkv_ref.dtype)

    last = nb - 1

    def cur(n):
        return jnp.minimum(n, last)

    def out_kv(n):
        return (jnp.maximum(n - 1, 0), 0)

    return pl.pallas_call(
        body, name="swa_bwd", grid=(nb + 1,),
        in_specs=[pl.BlockSpec(memory_space=pltpu.SMEM)] + _swa_in_specs(nb, last) + [
            pl.BlockSpec((BLK, D), lambda n: (cur(n), 0)),
            pl.BlockSpec((BLK, D), lambda n: (cur(n), 0)),
            pl.BlockSpec((BLK, LANE), lambda n: (cur(n), 0)),
        ],
        out_specs=[
            pl.BlockSpec((BLK, D), lambda n: (cur(n), 0)),
            pl.BlockSpec((BLK, 512), out_kv),
            pl.BlockSpec((8, LANE), lambda n: (0, 0)),
        ],
        out_shape=[
            jax.ShapeDtypeStruct((T, D), MXU),
            jax.ShapeDtypeStruct((T, 512), MXU),
            jax.ShapeDtypeStruct((8, LANE), F32),
        ],
        scratch_shapes=[pltpu.VMEM((BLK, 512), F32)],
        compiler_params=_cp(("arbitrary",)),
    )(sinks, proj, proj, proj, proj, proj, cos, sin, cos, sin, do_a, o_a, lse)


def _gla_gate(bl_ref, gu_ref, bias_ref):
    gk = _dot(bl_ref[...].astype(MXU), gu_ref[...]) + bias_ref[...]
    la = (jnp.minimum(gk, 0.0) - jnp.log(1.0 + jnp.exp(-jnp.abs(gk)))) / TAU
    ri = lax.broadcasted_iota(jnp.int32, (CHUNK, CHUNK), 0)
    ci = lax.broadcasted_iota(jnp.int32, (CHUNK, CHUNK), 1)
    b = _dot_f32(jnp.where(ci <= ri, 1.0, 0.0).astype(F32), la)
    return gk, la, b, ri, ci


def _gla_head(q_ref, k_ref, la, b, h):
    sl = slice(h * B_DK, (h + 1) * B_DK)
    bh = b[:, sl]
    blast = jnp.sum(la[:, sl], axis=0, keepdims=True)
    qc = q_ref[:, sl] * (B_DK ** -0.5)
    kh = k_ref[:, sl]
    eb, enb, esb = jnp.exp(bh), jnp.exp(-bh), jnp.exp(blast - bh)
    return qc * eb, kh * enb, kh * esb, eb, enb, esb, jnp.exp(blast)


def _gla_specs(chunk_of):
    return [
        pl.BlockSpec((CHUNK, 512), lambda i: (chunk_of(i), C_BQ // 512)),
        pl.BlockSpec((CHUNK, 512), lambda i: (chunk_of(i), C_BK // 512)),
        pl.BlockSpec((CHUNK, D), lambda i: (chunk_of(i), C_BV // D)),
        pl.BlockSpec((CHUNK, W_BL), lambda i: (chunk_of(i), C_BL // W_BL)),
        pl.BlockSpec((W_BL, 512), lambda i: (0, 0)),
        pl.BlockSpec((1, 512), lambda i: (0, 0)),
    ]


def _gla_fwd(proj, gu_pad, bias):
    T = proj.shape[0]
    nc = T // CHUNK

    def body(q_ref, k_ref, v_ref, bl_ref, gu_ref, bias_ref, o_ref, st_ref, state_ref):
        @pl.when(pl.program_id(0) == 0)
        def _():
            state_ref[...] = jnp.zeros_like(state_ref)

        _, la, b, ri, ci = _gla_gate(bl_ref, gu_ref, bias_ref)
        st_ref[...] = state_ref[...]
        for h in range(B_HEADS):
            q_e, k_e, k_s, _, _, _, decay = _gla_head(q_ref, k_ref, la, b, h)
            vh = v_ref[:, h * B_DV:(h + 1) * B_DV].astype(MXU)
            rows = slice(h * B_DV, (h + 1) * B_DV)
            q_eb = q_e.astype(MXU)
            att = jnp.where(ci <= ri, _dot_nt(q_eb, k_e.astype(MXU)), 0.0)
            st = state_ref[rows, :]
            o_ref[:, rows] = _dot(att.astype(MXU), vh) + _dot_nt(q_eb, st.astype(MXU))
            state_ref[rows, :] = st * decay + _dot_tn(vh, k_s.astype(MXU))

    return pl.pallas_call(
        body, name="gla_fwd", grid=(nc,),
        in_specs=_gla_specs(lambda i: i),
        out_specs=[pl.BlockSpec((CHUNK, D), lambda i: (i, 0)),
                   pl.BlockSpec((B_HEADS * B_DV, B_DK), lambda i: (i, 0))],
        out_shape=[jax.ShapeDtypeStruct((T, D), F32),
                   jax.ShapeDtypeStruct((nc * B_HEADS * B_DV, B_DK), F32)],
        scratch_shapes=[pltpu.VMEM((B_HEADS * B_DV, B_DK), F32)],
        compiler_params=_cp(("arbitrary",)),
    )(proj, proj, proj, proj, gu_pad, bias)


def _gla_bwd(proj, gu_pad, bias, states, do_b):
    T = proj.shape[0]
    nc = T // CHUNK
    o_q, o_k = C_BQ - C_GLA, C_BK - C_GLA

    def body(q_ref, k_ref, v_ref, bl_ref, gu_ref, bias_ref, st_ref, do_ref,
             dg_ref, dbl_ref, ggu_ref, gbias_ref, gt_ref):
        @pl.when(pl.program_id(0) == 0)
        def _():
            gt_ref[...] = jnp.zeros_like(gt_ref)
            ggu_ref[...] = jnp.zeros_like(ggu_ref)
            gbias_ref[...] = jnp.zeros_like(gbias_ref)

        gk, la, b, ri, ci = _gla_gate(bl_ref, gu_ref, bias_ref)
        causal = ci <= ri
        upper = jnp.where(ci >= ri, 1.0, 0.0).astype(F32)
        dla_parts = []
        for h in range(B_HEADS):
            q_e, k_e, k_s, eb, enb, esb, decay = _gla_head(q_ref, k_ref, la, b, h)
            rows = slice(h * B_DV, (h + 1) * B_DV)
            sl = slice(h * B_DK, (h + 1) * B_DK)
            vh = v_ref[:, rows].astype(MXU)
            doh = do_ref[:, rows].astype(MXU)
            q_eb, k_eb, k_sb = q_e.astype(MXU), k_e.astype(MXU), k_s.astype(MXU)
            st = st_ref[rows, :]
            gt = gt_ref[rows, :]
            gtb = gt.astype(MXU)
            att = jnp.where(causal, _dot_nt(q_eb, k_eb), 0.0).astype(MXU)
            datt = jnp.where(causal, _dot_nt(doh, vh), 0.0).astype(MXU)
            dq_e = _dot(datt, k_eb) + _dot(doh, st.astype(MXU))
            dk_e = _dot_tn(datt, q_eb)
            dk_s = _dot(vh, gtb)
            dg_ref[:, rows] = (_dot_tn(att, doh) + _dot_nt(k_sb, gtb)).astype(dg_ref.dtype)
            ddecay = jnp.sum(gt * st, axis=0, keepdims=True)
            gt_ref[rows, :] = gt * decay + _dot_tn(doh, q_eb)
            dg_ref[:, o_q + h * B_DK:o_q + (h + 1) * B_DK] = (dq_e * eb * (B_DK ** -0.5)).astype(dg_ref.dtype)
            dg_ref[:, o_k + h * B_DK:o_k + (h + 1) * B_DK] = (dk_e * enb + dk_s * esb).astype(dg_ref.dtype)
            dks_ks = dk_s * k_s
            db = dq_e * q_e - dk_e * k_e - dks_ks
            dblast = jnp.sum(dks_ks, axis=0, keepdims=True) + ddecay * decay
            dla_parts.append(_dot_f32(upper, db) + dblast)
        dla = jnp.concatenate(dla_parts, axis=1)
        dgk = dla * (1.0 / TAU) * _sigmoid(-gk)
        dgkb = dgk.astype(MXU)
        dbl_ref[...] = _dot_nt(dgkb, gu_ref[...]).astype(dbl_ref.dtype)
        ggu_ref[...] = ggu_ref[...] + _dot_tn(bl_ref[...].astype(MXU), dgkb)
        gbias_ref[...] = gbias_ref[...] + jnp.broadcast_to(jnp.sum(dgk, axis=0, keepdims=True), gbias_ref.shape)

    def rev(i):
        return nc - 1 - i

    return pl.pallas_call(
        body, name="gla_bwd", grid=(nc,),
        in_specs=_gla_specs(rev) + [
            pl.BlockSpec((B_HEADS * B_DV, B_DK), lambda i: (rev(i), 0)),
            pl.BlockSpec((CHUNK, D), lambda i: (rev(i), 0)),
        ],
        out_specs=[
            pl.BlockSpec((CHUNK, W_GLA), lambda i: (rev(i), 0)),
            pl.BlockSpec((CHUNK, W_BL), lambda i: (rev(i), 0)),
            pl.BlockSpec((W_BL, 512), lambda i: (0, 0)),
            pl.BlockSpec((8, 512), lambda i: (0, 0)),
        ],
        out_shape=[
            jax.ShapeDtypeStruct((T, W_GLA), MXU),
            jax.ShapeDtypeStruct((T, W_BL), MXU),
            jax.ShapeDtypeStruct((W_BL, 512), F32),
            jax.ShapeDtypeStruct((8, 512), F32),
        ],
        scratch_shapes=[pltpu.VMEM((B_HEADS * B_DV, B_DK), F32)],
        compiler_params=_cp(("arbitrary",)),
    )(proj, proj, proj, proj, gu_pad, bias, states, do_b)


def _mid(x, target, proj, o_a, o_b, w_a, w_b, w_out, w_bn4, fnw):
    T = x.shape[0]
    tT = min(T, 128)
    nbuf = 4
    o_ag, o_bg, o_ma, o_mb = (c - C_GATES for c in (C_AG, C_BG, C_MA, C_MB))

    def body(x_ref, t_ref, oa_ref, ob_ref, gates_ref, wa_ref, wb_ref, wo_ref, wbn_ref, fnw_ref,
             dx2_ref, doa_ref, dob_ref, dgates_ref,
             gwa_ref, gwb_ref, gwo_ref, gfn_ref, gbn_ref, loss_ref, buf_ref):
        i = pl.program_id(0)

        @pl.when(i == 0)
        def _():
            for r in (gwa_ref, gwb_ref, gwo_ref, gfn_ref, gbn_ref, loss_ref):
                r[...] = jnp.zeros_like(r)

        rows = pl.ds(pl.multiple_of((i % nbuf) * tT, tT), tT)

        def keep(k, val):
            buf_ref[k, rows, :] = val

        oa, ag = oa_ref[...], gates_ref[:, o_ag:o_ag + D]
        sg_a = _sigmoid(ag)
        silu_a = ag * sg_a
        oag_b = (oa * silu_a).astype(MXU)
        keep(0, oag_b)
        y_a = _dot(oag_b, wa_ref[...])

        ob, bg = ob_ref[...], gates_ref[:, o_bg:o_bg + D]
        rbs, obhats = [], []
        for h in range(B_HEADS):
            obh = ob[:, h * B_DV:(h + 1) * B_DV]
            rb = lax.rsqrt(jnp.mean(obh * obh, axis=-1, keepdims=True) + EPS)
            rbs.append(rb)
            obhats.append(obh * rb)
        obhat = jnp.concatenate(obhats, axis=1)
        wbn = wbn_ref[...]
        obn = obhat * wbn
        sg_b = _sigmoid(bg)
        silu_b = bg * sg_b
        obg_b = (obn * silu_b).astype(MXU)
        keep(1, obg_b)
        y_b = _dot(obg_b, wb_ref[...])

        sa, sb = _sigmoid(gates_ref[:, o_ma:o_ma + D]), _sigmoid(gates_ref[:, o_mb:o_mb + D])
        mg_b = (sa * y_a + sb * y_b).astype(MXU)
        keep(2, mg_b)
        x2 = x_ref[...] + _dot(mg_b, wo_ref[...])
        r2 = lax.rsqrt(jnp.mean(x2 * x2, axis=-1, keepdims=True) + EPS)
        xh2 = x2 * r2
        fw = fnw_ref[...]
        err = xh2 * fw - t_ref[...]
        tok = jnp.mean(err * err, axis=-1, keepdims=True)
        loss_ref[...] = loss_ref[...] + 0.5 * jnp.sum(tok, axis=0, keepdims=True)

        dy = err * (1.0 / D)
        gfn_ref[...] = gfn_ref[...] + jnp.broadcast_to(jnp.sum(dy * xh2, axis=0, keepdims=True), gfn_ref.shape)
        gy = dy * fw
        dx2 = r2 * (gy - xh2 * jnp.mean(gy * xh2, axis=-1, keepdims=True))
        dx2_ref[...] = dx2
        dx2_b = dx2.astype(MXU)
        keep(5, dx2_b)
        dmg = _dot_nt(dx2_b, wo_ref[...])

        dgates_ref[:, o_ma:o_ma + D] = (dmg * y_a * sa * (1.0 - sa)).astype(dgates_ref.dtype)
        dgates_ref[:, o_mb:o_mb + D] = (dmg * y_b * sb * (1.0 - sb)).astype(dgates_ref.dtype)
        dya_b = (dmg * sa).astype(MXU)
        dyb_b = (dmg * sb).astype(MXU)
        keep(3, dya_b)
        keep(4, dyb_b)
        doag = _dot_nt(dya_b, wa_ref[...])
        dobg = _dot_nt(dyb_b, wb_ref[...])

        @pl.when(i % nbuf == nbuf - 1)
        def _():
            gwa_ref[...] = gwa_ref[...] + _dot_tn(buf_ref[0], buf_ref[3])
            gwb_ref[...] = gwb_ref[...] + _dot_tn(buf_ref[1], buf_ref[4])
            gwo_ref[...] = gwo_ref[...] + _dot_tn(buf_ref[2], buf_ref[5])

        doa_ref[...] = doag * silu_a
        dgates_ref[:, o_ag:o_ag + D] = (doag * oa * (sg_a * (1.0 + ag * (1.0 - sg_a)))).astype(dgates_ref.dtype)
        dobn = dobg * silu_b
        dgates_ref[:, o_bg:o_bg + D] = (dobg * obn * (sg_b * (1.0 + bg * (1.0 - sg_b)))).astype(dgates_ref.dtype)
        gg = dobn * wbn
        gbn = jnp.zeros((1, B_DV), F32)
        for h in range(B_HEADS):
            sl = slice(h * B_DV, (h + 1) * B_DV)
            gbn = gbn + jnp.sum(dobn[:, sl] * obhats[h], axis=0, keepdims=True)
            ggh = gg[:, sl]
            dob_ref[:, sl] = rbs[h] * (ggh - obhats[h] * jnp.mean(ggh * obhats[h], axis=-1, keepdims=True))
        gbn_ref[...] = gbn_ref[...] + jnp.broadcast_to(gbn, gbn_ref.shape)

    assert (T // tT) % nbuf == 0
    tile = pl.BlockSpec((tT, D), lambda i: (i, 0))
    row = pl.BlockSpec((1, D), lambda i: (0, 0))
    acc8 = pl.BlockSpec((8, D), lambda i: (0, 0))
    return pl.pallas_call(
        body, name="mid", grid=(T // tT,),
        in_specs=[tile, tile, tile, tile, pl.BlockSpec((tT, W_GATES), lambda i: (i, C_GATES // W_GATES)),
                  _vmem(), _vmem(), _vmem(), row, row],
        out_specs=[tile, tile, tile, pl.BlockSpec((tT, W_GATES), lambda i: (i, 0)), _vmem(), _vmem(), _vmem(),
                   acc8, pl.BlockSpec((8, B_DV), lambda i: (0, 0)), pl.BlockSpec((8, LANE), lambda i: (0, 0))],
        out_shape=[
            jax.ShapeDtypeStruct((T, D), F32),
            jax.ShapeDtypeStruct((T, D), F32),
            jax.ShapeDtypeStruct((T, D), F32),
            jax.ShapeDtypeStruct((T, W_GATES), MXU),
            jax.ShapeDtypeStruct((D, D), F32),
            jax.ShapeDtypeStruct((D, D), F32),
            jax.ShapeDtypeStruct((D, D), F32),
            jax.ShapeDtypeStruct((8, D), F32),
            jax.ShapeDtypeStruct((8, B_DV), F32),
            jax.ShapeDtypeStruct((8, LANE), F32),
        ],
        scratch_shapes=[pltpu.VMEM((6, nbuf * tT, D), MXU)],
        compiler_params=_cp(("arbitrary",)),
    )(x, target, o_a, o_b, proj, w_a, w_b, w_out, w_bn4, fnw)


DH = D // 2


def _gw_half(h, pieces, half, sums_prev=None):
    T = h.shape[0]
    steps = NF // 512
    carry = sums_prev is not None
    tiles = ((0, 2), (2, 3), (4, 8), (8, 16))

    def body(*refs):
        h_ref, q_ref, kv_ref, bl_ref, gla_ref, gates_ref = refs[:6]
        if carry:
            s_ref, o_ref, got_ref, send_sems, recv_sems = refs[6:]
        else:
            (o_ref,) = refs[6:]
        j = pl.program_id(0)

        if carry:
            @pl.when(j == 0)
            def _():
                for cp in _chip_copies(s_ref, got_ref, send_sems, recv_sems):
                    cp.start()

        for (lo, hi), ref in zip(tiles, (q_ref, kv_ref, gla_ref, gates_ref)):
            @pl.when((j >= lo) & (j < hi))
            def _(ref=ref):
                o_ref[...] = _dot_tn(ref[...], h_ref[...])

        @pl.when(j == 3)
        def _():
            o_ref[0:W_BL, :] = _dot_tn(bl_ref[...], h_ref[...])
            o_ref[W_BL:, :] = jnp.zeros((512 - W_BL, DH), F32)

        if carry:
            @pl.when(j == steps - 1)
            def _():
                copies = _chip_copies(s_ref, got_ref, send_sems, recv_sems)
                for cp in copies:
                    cp.wait_recv()
                for cp in copies:
                    cp.wait_send()

    def tile_of(lo, hi):
        return lambda j: (0, jnp.clip(j - lo, 0, hi - lo - 1))

    in_specs = [pl.BlockSpec((T, DH), lambda j: (0, half)),
                pl.BlockSpec((T, 512), tile_of(0, 2)), pl.BlockSpec((T, 512), lambda j: (0, 0)),
                pl.BlockSpec((T, W_BL), lambda j: (0, 0)),
                pl.BlockSpec((T, 512), tile_of(4, 8)), pl.BlockSpec((T, 512), tile_of(8, 16))]
    out_specs = [pl.BlockSpec((512, DH), lambda j: (j, 0))]
    out_shape = [jax.ShapeDtypeStruct((NF, DH), F32)]
    scratch = []
    args = [h, *pieces]
    if carry:
        in_specs.append(_any())
        out_specs.append(_any())
        out_shape.append(jax.ShapeDtypeStruct((3, ROWS, DH), sums_prev.dtype))
        scratch = [pltpu.SemaphoreType.DMA((3,)), pltpu.SemaphoreType.DMA((3,))]
        args.append(sums_prev)
    res = pl.pallas_call(
        body, name=f"gw_in_half{half}", grid=(steps,),
        in_specs=in_specs, out_specs=out_specs, out_shape=out_shape, scratch_shapes=scratch,
        compiler_params=_cp(("arbitrary",)),
    )(*args)
    return res if carry else res[0]


def _chip_copies(s_ref, got_ref, send_sems, recv_sems):
    x, y, c = _place()
    chips = [(1 - x, y), (x, 1 - y), (1 - x, 1 - y)]
    return [pltpu.make_async_remote_copy(
        src_ref=s_ref.at[2 * px + py], dst_ref=got_ref.at[j],
        send_sem=send_sems.at[j], recv_sem=recv_sems.at[j], device_id=(px, py, c), device_id_type=MESH)
        for j, (px, py) in enumerate(chips)]


def _dh_norm(pieces, offsets, wf, x, dx2, norm_w, sums):
    T = x.shape[0]
    tT = min(T, 256)
    widths = [p.shape[1] for p in pieces]
    npc = len(pieces)
    last = T // tT - 1

    def body(*refs):
        dp_refs = refs[:npc]
        wf_ref, x_ref, dx2_ref, nw_ref, s_ref, gx_ref, gnw_ref, got_ref, send_sems, recv_sems = refs[npc:]

        @pl.when(pl.program_id(0) == 0)
        def _():
            gnw_ref[...] = jnp.zeros_like(gnw_ref)
            for cp in _chip_copies(s_ref, got_ref, send_sems, recv_sems):
                cp.start()

        dh = jnp.zeros((tT, D), F32)
        for dp_ref, off, w in zip(dp_refs, offsets, widths):
            dh = dh + _dot(dp_ref[...], wf_ref[off:off + w, :])
        xv = x_ref[...]
        r = lax.rsqrt(jnp.mean(xv * xv, axis=-1, keepdims=True) + EPS)
        xh = xv * r
        gnw_ref[...] = gnw_ref[...] + jnp.broadcast_to(jnp.sum(dh * xh, axis=0, keepdims=True), gnw_ref.shape)
        g = dh * nw_ref[...]
        gx_ref[...] = r * (g - xh * jnp.mean(g * xh, axis=-1, keepdims=True)) + dx2_ref[...]

        @pl.when(pl.program_id(0) == last)
        def _():
            copies = _chip_copies(s_ref, got_ref, send_sems, recv_sems)
            for cp in copies:
                cp.wait_recv()
            for cp in copies:
                cp.wait_send()

    tile = pl.BlockSpec((tT, D), lambda i: (i, 0))
    return pl.pallas_call(
        body, name="dh_norm", grid=(T // tT,),
        in_specs=[pl.BlockSpec((tT, w), lambda i: (i, 0)) for w in widths]
        + [_vmem(), tile, tile, pl.BlockSpec((1, D), lambda i: (0, 0)), _any()],
        out_specs=[tile, pl.BlockSpec((8, D), lambda i: (0, 0)), _any()],
        out_shape=[jax.ShapeDtypeStruct((T, D), F32), jax.ShapeDtypeStruct((8, D), F32),
                   jax.ShapeDtypeStruct((3,) + sums.shape[1:], sums.dtype)],
        scratch_shapes=[pltpu.SemaphoreType.DMA((3,)), pltpu.SemaphoreType.DMA((3,))],
        compiler_params=_cp(("arbitrary",)),
    )(*pieces, wf, x, dx2, norm_w, sums)


def _adamw_math(w, g, m, v):
    m = ADAM_B1 * m + (1.0 - ADAM_B1) * g
    v = ADAM_B2 * v + (1.0 - ADAM_B2) * (g * g)
    m_hat = m / (1.0 - ADAM_B1 ** ADAM_STEP)
    v_hat = v / (1.0 - ADAM_B2 ** ADAM_STEP)
    delta = -ADAM_LR * (m_hat / (jnp.sqrt(v_hat) + ADAM_EPS) + ADAM_WD * w)
    return delta, m, v


def _fetch_partials(s_refs, got_refs, buf, sems):
    x, y, _ = _place()
    cps = []
    for hf, (s_ref, got_ref) in enumerate(zip(s_refs, got_refs)):
        cols = pl.ds(hf * DH, DH)
        cps.append(pltpu.make_async_copy(s_ref.at[2 * x + y], buf.at[0, :, cols], sems.at[4 * hf]))
        cps += [pltpu.make_async_copy(got_ref.at[j], buf.at[1 + j, :, cols], sems.at[4 * hf + 1 + j])
                for j in range(3)]
    for cp in cps:
        cp.start()
    for cp in cps:
        cp.wait()


SMALL_AT = dict(norm_w=0, fnw=8, bias=16, bn=24, sinks=32, loss=40)
ROW_AT = (R_IN, R_A, R_B, R_O)


def _finish(w_rows, m_rows, v_rows, ws, ms, vs, gu_w, gu_m, gu_v, small, sums, got):
    names = ["norm_w", "fnw", "bias", "bn", "sinks"]
    widths = [ws[n].shape[1] for n in names]
    shapes = [w.shape for w in w_rows]

    def body(*refs):
        wr_refs, mr_refs, vr_refs = refs[0:4], refs[4:8], refs[8:12]
        refs = refs[12:]
        w_refs, m_refs, v_refs = refs[0:5], refs[5:10], refs[10:15]
        guw_ref, gum_ref, guv_ref, small_ref = refs[15:19]
        s_refs, got_refs = refs[19:21], refs[21:23]
        loss_ref = refs[23]
        row_outs = refs[24:40]
        outs = refs[40:60]
        gu_outs = refs[60:64]
        smalls, tot, buf, gsh, send_sems, recv_sems, sems = refs[64:]
        x, y, c = _place()
        me_slot = 4 * x + 2 * y + c
        sends = []
        k = 0
        for dx in range(2):
            for dy in range(2):
                for dc in range(2):
                    if dx == 0 and dy == 0 and dc == 0:
                        continue
                    sends.append(pltpu.make_async_remote_copy(
                        src_ref=small_ref, dst_ref=smalls.at[me_slot],
                        send_sem=send_sems.at[k], recv_sem=recv_sems.at[k],
                        device_id=(x ^ dx, y ^ dy, c ^ dc), device_id_type=MESH))
                    k += 1
        for cp in sends:
            cp.start()
        smalls[me_slot] = small_ref[...]
        _fetch_partials(s_refs, got_refs, buf, sems)
        unshift = lax.rem(SHARD_PAD - 2 * me_slot, SHARD_PAD)
        for p in range(4):
            n, off = shapes[p][0], ROW_AT[p]
            nf = SHARD_PAD if p == 0 else n
            for cc in range(D // LANE):
                cols = slice(cc * LANE, (cc + 1) * LANE)
                g = buf[0, off:off + nf, cols].astype(F32)
                for j in range(1, 4):
                    g = g + buf[j, off:off + nf, cols].astype(F32)
                if p == 0:
                    gsh[...] = pltpu.roll(g, unshift, 0)
                    g = gsh[0:n, :]
                d, nm, nv = _adamw_math(wr_refs[p][:, cols], g, mr_refs[p][:, cols], vr_refs[p][:, cols])
                for o, val in zip(row_outs[4 * p:4 * p + 4], (g, d, nm, nv)):
                    o[:, cols] = val
        for cp in sends:
            cp.wait_recv()
        for cp in sends:
            cp.wait_send()
        acc = smalls[0]
        for d in range(1, NDEV):
            acc = acc + smalls[d]
        tot[...] = acc
        loss_ref[...] = tot[SMALL_AT["loss"]:SMALL_AT["loss"] + 1, 0:1]
        for p, (nm_, wd) in enumerate(zip(names, widths)):
            r = SMALL_AT[nm_]
            g = tot[r:r + 1, 0:wd]
            d, nm, nv = _adamw_math(w_refs[p][...], g, m_refs[p][...], v_refs[p][...])
            for o, val in zip(outs[4 * p:4 * p + 4], (g, d, nm, nv)):
                o[...] = val
        g = buf[0, R_GU:R_GU + RANK, 0:64].astype(F32)
        for j in range(1, 4):
            g = g + buf[j, R_GU:R_GU + RANK, 0:64].astype(F32)
        d, nm, nv = _adamw_math(guw_ref[...], g, gum_ref[...], guv_ref[...])
        for o, val in zip(gu_outs, (g, d, nm, nv)):
            o[...] = val

    out_shape = ([jax.ShapeDtypeStruct((1, 1), F32)]
                 + [jax.ShapeDtypeStruct(s, F32) for s in shapes for _ in range(4)]
                 + [jax.ShapeDtypeStruct((1, wd), F32) for wd in widths for _ in range(4)]
                 + [jax.ShapeDtypeStruct((RANK, 64), F32)] * 4)
    res = pl.pallas_call(
        body, name="finish",
        in_specs=[_vmem()] * 31 + [_any()] * 4,
        out_specs=[_vmem()] * 41,
        out_shape=out_shape,
        scratch_shapes=[pltpu.VMEM((NDEV, SMALL_ROWS, D), F32), pltpu.VMEM((SMALL_ROWS, D), F32),
                        pltpu.VMEM((4, ROWS, D), sums[0].dtype), pltpu.VMEM((SHARD_PAD, LANE), F32),
                        pltpu.SemaphoreType.DMA((7,)), pltpu.SemaphoreType.DMA((7,)), pltpu.SemaphoreType.DMA((8,))],
        compiler_params=_cp(),
    )(*w_rows, *m_rows, *v_rows, *[ws[n] for n in names], *[ms[n] for n in names], *[vs[n] for n in names],
      gu_w, gu_m, gu_v, small, *sums, *got)
    loss = res[0]
    per = {n: tuple(res[17 + 4 * p:21 + 4 * p]) for p, n in enumerate(names)}
    return loss, tuple(res[1:17]), per, tuple(res[37:41])


def _place():
    x, y, c = lax.axis_index("x"), lax.axis_index("y"), lax.axis_index("c")
    return x, y, c


def _gather_blocks(w_in_t, w_a_s, w_b_s, w_o_s, gu_s, xs, norm_w, pos_col):
    rows, cols = ROWS, D
    T = xs.shape[0]
    tT = min(T, 256)
    inv_row, sign_row = _rope_rows()

    def body(wi_ref, wa_ref, wb_ref, wo_ref, gu_ref, xs_ref, nw_ref, pos_ref, inv_ref, sign_ref,
             out_ref, h_ref, cos_ref, sin_ref, x_ref, frame_ref, send_sems, recv_sems, local_sem):
        x, y, c = _place()
        me, sibling = (x, y, c), (x, y, 1 - c)
        chips = [(1 - x, y), (x, 1 - y), (1 - x, 1 - y)]
        shift = 2 * (4 * x + 2 * y + c)
        frame_ref[SHARD - SHARD % 8:, :] = jnp.zeros((SHARD_PAD - SHARD + SHARD % 8, D), F32)
        frame_ref[:SHARD, :] = wi_ref[...]
        for cc in range(D // LANE):
            cs = slice(cc * LANE, (cc + 1) * LANE)
            x_ref[R_IN:R_IN + SHARD_PAD, cs] = pltpu.roll(frame_ref[:, cs], shift, 0).astype(x_ref.dtype)
        x_ref[R_A:R_A + 128, :] = wa_ref[...].astype(x_ref.dtype)
        x_ref[R_B:R_B + 128, :] = wb_ref[...].astype(x_ref.dtype)
        x_ref[R_O:R_O + 128, :] = wo_ref[...].astype(x_ref.dtype)
        x_ref[R_GU:R_GU + RANK, :] = jnp.zeros((RANK, D), x_ref.dtype)
        x_ref[R_GU:R_GU + RANK, 0:64] = gu_ref[...].astype(x_ref.dtype)

        def slot(px, py, pc):
            return out_ref.at[4 * px + 2 * py + pc]

        def copy(k, block, to, src=None):
            return pltpu.make_async_remote_copy(
                src_ref=slot(*block) if src is None else src, dst_ref=slot(*block),
                send_sem=send_sems.at[k], recv_sem=recv_sems.at[k], device_id=to, device_id_type=MESH)

        mine = pltpu.make_async_copy(x_ref, slot(*me), local_sem)
        mine.start()
        first = [copy(0, me, sibling, src=x_ref)]
        first += [copy(1 + j, me, (*chip, c), src=x_ref) for j, chip in enumerate(chips)]
        for cp in first:
            cp.start()

        @pl.loop(0, T // tT)
        def _(i):
            rows_i = pl.ds(pl.multiple_of(i * tT, tT), tT)
            _prologue_rows(rows_i, xs_ref, nw_ref, pos_ref, inv_ref, sign_ref, h_ref, cos_ref, sin_ref)

        passed = [copy(4 + j, (*chip, c), sibling) for j, chip in enumerate(chips)]
        for j, chip in enumerate(chips):
            copy(1 + j, (*chip, c), me).wait_recv()
            passed[j].start()
        copy(0, sibling, me).wait_recv()
        for j, chip in enumerate(chips):
            copy(4 + j, (*chip, 1 - c), me).wait_recv()
        for cp in first + passed:
            cp.wait_send()
        mine.wait()

    return pl.pallas_call(
        body, name="gather_weights",
        in_specs=[_vmem()] * 10, out_specs=[_any()] + [_vmem()] * 3,
        out_shape=[jax.ShapeDtypeStruct((NDEV, rows, cols), WIRE), jax.ShapeDtypeStruct((T, D), MXU),
                   jax.ShapeDtypeStruct((T, LANE), F32), jax.ShapeDtypeStruct((T, LANE), F32)],
        scratch_shapes=[pltpu.VMEM((rows, cols), WIRE), pltpu.VMEM((SHARD_PAD, D), F32),
                        pltpu.SemaphoreType.DMA((7,)), pltpu.SemaphoreType.DMA((7,)), pltpu.SemaphoreType.DMA],
        compiler_params=_cp(),
    )(w_in_t, w_a_s, w_b_s, w_o_s, gu_s, xs, norm_w, pos_col, inv_row, sign_row)


def _pair_reduce(packed):
    def body(p_ref, out_ref, got, own, send_sems, recv_sems, own_sems):
        x, y, c = _place()
        sends = [pltpu.make_async_remote_copy(
            src_ref=p_ref.at[2 * chip + (1 - c)], dst_ref=got.at[chip],
            send_sem=send_sems.at[chip], recv_sem=recv_sems.at[chip], device_id=(x, y, 1 - c), device_id_type=MESH)
            for chip in range(4)]
        loads = [pltpu.make_async_copy(p_ref.at[2 * chip + c], own.at[chip], own_sems.at[chip]) for chip in range(4)]
        for cp in sends + loads:
            cp.start()
        for chip in range(4):
            loads[chip].wait()
            sends[chip].wait_recv()
            out_ref[chip] = (own[chip].astype(F32) + got[chip].astype(F32)).astype(out_ref.dtype)
        for cp in sends:
            cp.wait_send()

    return pl.pallas_call(
        body, name="pair_reduce",
        in_specs=[_any()], out_specs=_vmem(),
        out_shape=jax.ShapeDtypeStruct((4,) + packed.shape[1:], packed.dtype),
        scratch_shapes=[pltpu.VMEM((4,) + packed.shape[1:], packed.dtype), pltpu.VMEM((4,) + packed.shape[1:], packed.dtype),
                        pltpu.SemaphoreType.DMA((4,)), pltpu.SemaphoreType.DMA((4,)), pltpu.SemaphoreType.DMA((4,))],
        compiler_params=_cp(),
    )(packed)


def _pad_cols(a, cols):
    return jnp.pad(a, ((0, 0), (0, cols - a.shape[1])))


def _pad_rows(a, rows):
    return jnp.pad(a, ((0, rows - a.shape[0]), (0, 0)))


FRAME = 928


def _join_frames(frames):
    head = frames[:, :FRAME].at[1:, :16].add(frames[:-1, FRAME:])
    return jnp.concatenate([head.reshape(NDEV * FRAME, D), frames[NDEV - 1, FRAME:]], axis=0)


def _build_wft(wt):
    q = wt[0:1024].reshape(8, 2, 2, 32, D).transpose(0, 2, 1, 3, 4).reshape(1024, D)
    k = wt[1024:1152].reshape(2, 2, 1, 32, D)
    kd = jnp.broadcast_to(k, (2, 2, 2, 32, D)).reshape(256, D)
    v = wt[1152:1280].reshape(2, 1, 64, D)
    vd = jnp.broadcast_to(v, (2, 2, 64, D)).reshape(256, D)
    ag, bq, bk = wt[1280:2304], wt[2304:2816], wt[2816:3328]
    bv, bg, bl = wt[3328:4352], wt[4352:5376], wt[5376:5392]
    ma, mb = wt[5392:6416], wt[6416:7440]
    return jnp.concatenate([q, kd, vd, _pad_rows(bl, C_GLA - C_BL), bv, bq, bk, ag, bg, ma, mb], axis=0)


def _unbuild_gwt(g):
    n = g.shape[1]
    q = g[C_Q:C_Q + 1024].reshape(8, 2, 2, 32, n).transpose(0, 2, 1, 3, 4).reshape(1024, n)
    k = g[C_KD:C_KD + 256].reshape(2, 2, 2, 32, n).sum(axis=2).reshape(128, n)
    v = g[C_VD:C_VD + 256].reshape(2, 2, 64, n).sum(axis=1).reshape(128, n)
    bv, bq, bk = g[C_BV:C_BV + 1024], g[C_BQ:C_BQ + 512], g[C_BK:C_BK + 512]
    ag, bg, ma, mb = (g[c:c + 1024] for c in (C_AG, C_BG, C_MA, C_MB))
    return jnp.concatenate([q, k, v, ag, bq, bk, bv, bg, g[C_BL:C_BL + RANK], ma, mb], axis=0)


def kernel(x, positions, norm_w, w_in, a_sinks, b_gate_up, b_gate_bias, b_out_norm_w, w_a_proj, w_b_proj, w_out, final_norm_w, loss_target, m_norm_w, m_w_in, m_a_sinks, m_b_gate_up, m_b_gate_bias, m_b_out_norm_w, m_w_a_proj, m_w_b_proj, m_w_out, m_final_norm_w, v_norm_w, v_w_in, v_a_sinks, v_b_gate_up, v_b_gate_bias, v_b_out_norm_w, v_w_a_proj, v_w_b_proj, v_w_out, v_final_norm_w):
    T = x.shape[1]
    xs, target = x[0], loss_target[0]
    fnw = final_norm_w.reshape(1, D)
    allw, h, cos, sin = _gather_blocks(w_in[0].T, w_a_proj[0], w_b_proj[0], w_out[0], b_gate_up[0],
                                       xs, norm_w, positions.reshape(T, 1))
    wf = _build_wft(_join_frames(allw[:, :SHARD_PAD]))
    w_a = allw[:, R_A:R_A + 128, :].reshape(D, D)
    w_b = allw[:, R_B:R_B + 128, :].reshape(D, D)
    w_o = allw[:, R_O:R_O + 128, :].reshape(D, D)
    gu = allw[:, R_GU:R_GU + RANK, :64].transpose(1, 0, 2).reshape(RANK, 512)
    gu_pad = _pad_rows(gu, W_BL)

    proj = _proj(h, wf)
    o_a, lse = _swa_fwd(proj, cos, sin, a_sinks)
    o_b, states = _gla_fwd(proj, gu_pad, b_gate_bias)
    (dx2, do_a, do_b, d_gates, g_wa, g_wb, g_wo, g_fn, g_bn, loss_part) = _mid(
        xs, target, proj, o_a, o_b, w_a, w_b, w_o, jnp.tile(b_out_norm_w, (1, B_HEADS)), fnw)
    d_q, d_kv, g_sinks = _swa_bwd(proj, cos, sin, a_sinks, do_a, o_a, lse)
    d_gla, d_bl, g_gu, g_bias = _gla_bwd(proj, gu_pad, b_gate_bias, states, do_b)
    pieces = [d_q, d_kv, d_bl, d_gla, d_gates]
    offsets = [C_Q, C_KD, C_BL, C_GLA, C_GATES]

    ggu = g_gu[:RANK].reshape(RANK, NDEV, 64).transpose(1, 0, 2)
    ggu_half = [jnp.pad(ggu, ((0, 0), (0, 0), (0, DH - 64))), jnp.zeros((NDEV, RANK, DH), F32)]

    def pack(gw_half, hf):
        gwt = _unbuild_gwt(gw_half).astype(WIRE)
        cols = slice(hf * DH, (hf + 1) * DH)
        return jnp.concatenate([
            jnp.stack([gwt[FRAME * d:FRAME * d + SHARD_PAD] for d in range(NDEV)]),
            g_wa[:, cols].reshape(NDEV, 128, DH).astype(WIRE),
            g_wb[:, cols].reshape(NDEV, 128, DH).astype(WIRE),
            g_wo[:, cols].reshape(NDEV, 128, DH).astype(WIRE),
            ggu_half[hf].astype(WIRE)], axis=1)

    sums0 = _pair_reduce(pack(_gw_half(h, pieces, 0), 0))
    gw1, got0 = _gw_half(h, pieces, 1, sums0)
    sums1 = _pair_reduce(pack(gw1, 1))
    grad_x, g_nw, got1 = _dh_norm(pieces, offsets, wf, xs, dx2, norm_w, sums1)
    sums, from_chips = [sums0, sums1], [got0, got1]

    small = jnp.concatenate([g_nw, g_fn, _pad_cols(g_bias, D), _pad_cols(g_bn, D), _pad_cols(g_sinks, D),
                             _pad_cols(loss_part, D)], axis=0)
    ws = dict(norm_w=norm_w, fnw=fnw, bias=b_gate_bias, bn=b_out_norm_w, sinks=a_sinks)
    ms = dict(norm_w=m_norm_w, fnw=m_final_norm_w.reshape(1, D), bias=m_b_gate_bias, bn=m_b_out_norm_w,
              sinks=m_a_sinks)
    vs = dict(norm_w=v_norm_w, fnw=v_final_norm_w.reshape(1, D), bias=v_b_gate_bias, bn=v_b_out_norm_w,
              sinks=v_a_sinks)
    loss, t_rows, sm, t_gu = _finish(
        [w_in[0].T, w_a_proj[0], w_b_proj[0], w_out[0]], [m_w_in[0].T, m_w_a_proj[0], m_w_b_proj[0], m_w_out[0]],
        [v_w_in[0].T, v_w_a_proj[0], v_w_b_proj[0], v_w_out[0]],
        ws, ms, vs, b_gate_up[0], m_b_gate_up[0], v_b_gate_up[0], small, sums, from_chips)

    def outputs(k):
        return [sm["norm_w"][k], t_rows[k].T[None], sm["sinks"][k], t_gu[k][None], sm["bias"][k], sm["bn"][k],
                t_rows[4 + k][None], t_rows[8 + k][None], t_rows[12 + k][None], sm["fnw"][k].reshape(D)]

    return (loss[0, 0], grad_x[None], *outputs(0), *outputs(1), *outputs(2), *outputs(3))
```

```python
import functools

import numpy as np
import jax
import jax.numpy as jnp
from jax import lax
from jax.experimental import pallas as pl
from jax.experimental.pallas import tpu as pltpu

F32 = jnp.float32
MXU = jnp.bfloat16
WIRE = jnp.bfloat16

D = 1024
A_HEADS, A_KV, A_HD = 16, 2, 64
BLK = 128
B_HEADS, B_DK, B_DV = 4, 128, 256
RANK, TAU, CHUNK = 16, 16.0, 64
EPS, NEG = 1e-5, -1e30
ROPE_THETA = 10000.0
IN_WIDTH, NDEV = 7440, 8
SHARD = IN_WIDTH // NDEV
LANE = 128

C_Q, C_KD, C_VD, C_BL = 0, 1024, 1280, 1536
C_BV, C_BQ, C_BK = 2048, 3072, 3584
C_AG, C_BG, C_MA, C_MB = 4096, 5120, 6144, 7168
C_GLA, W_GLA, C_GATES, W_GATES = 2048, 2048, 4096, 4096
NF = 8192
W_BL = 128

SHARD_PAD = 944
R_IN, R_A, R_B, R_O, R_GU, ROWS = 0, 944, 1072, 1200, 1328, 1344
SMALL_ROWS = 48

ADAM_LR, ADAM_B1, ADAM_B2, ADAM_EPS, ADAM_WD, ADAM_STEP = 0.001, 0.9, 0.999, 1e-08, 0.01, 10

MESH = pl.DeviceIdType.MESH
VMEM_LIMIT = 56 * 1024 * 1024


def _cp(sem=None, **kw):
    if sem is not None:
        kw["dimension_semantics"] = sem
    return pltpu.CompilerParams(vmem_limit_bytes=VMEM_LIMIT, **kw)


def _dot(a, b):
    return jnp.dot(a, b, preferred_element_type=F32)


def _dot_nt(a, b):
    return lax.dot_general(a, b, (((1,), (1,)), ((), ())), preferred_element_type=F32)


def _dot_tn(a, b):
    return lax.dot_general(a, b, (((0,), (0,)), ((), ())), preferred_element_type=F32)


def _dot_f32(a, b):
    return jnp.dot(a, b, preferred_element_type=F32, precision=lax.Precision.HIGHEST)


def _sigmoid(z):
    return 0.5 * jnp.tanh(0.5 * z) + 0.5


def _rope(xp, cos, sin):
    return xp * cos + pltpu.roll(xp, 64, 1) * sin


def _rope_bwd(dy, cos, sin):
    return dy * cos - pltpu.roll(dy, 64, 1) * sin


def _vmem():
    return pl.BlockSpec(memory_space=pltpu.VMEM)


def _any():
    return pl.BlockSpec(memory_space=pl.ANY)


def _rope_rows():
    half = A_HD // 2
    inv = (np.float32(ROPE_THETA) ** (-np.arange(half, dtype=np.float32) / np.float32(half))).astype(np.float32)
    inv_row = jnp.asarray(np.tile(inv, 4)[None, :])
    sign_row = jnp.asarray(np.concatenate([-np.ones(64, np.float32), np.ones(64, np.float32)])[None, :])
    return inv_row, sign_row


def _prologue_rows(rows, x_ref, nw_ref, pos_ref, inv_ref, sign_ref, h_ref, cos_ref, sin_ref):
    xv = x_ref[rows, :]
    r = lax.rsqrt(jnp.mean(xv * xv, axis=-1, keepdims=True) + EPS)
    h_ref[rows, :] = ((xv * r) * nw_ref[...]).astype(h_ref.dtype)
    ang = pos_ref[rows, :].astype(F32) * inv_ref[...]
    cos_ref[rows, :] = jnp.cos(ang)
    sin_ref[rows, :] = jnp.sin(ang) * sign_ref[...]


def _proj(h, wft):
    T = h.shape[0]
    tT, tN = T, 512

    def body(h_ref, w_ref, o_ref):
        o_ref[...] = _dot_nt(h_ref[...], w_ref[...])

    return pl.pallas_call(
        body, name="proj", grid=(T // tT, NF // tN),
        in_specs=[pl.BlockSpec((tT, D), lambda i, j: (i, 0)), pl.BlockSpec((tN, D), lambda i, j: (j, 0))],
        out_specs=pl.BlockSpec((tT, tN), lambda i, j: (i, j)),
        out_shape=jax.ShapeDtypeStruct((T, NF), F32),
        compiler_params=_cp(("parallel", "parallel")),
    )(h, wft)


def _swa_masks():
    lane = lax.broadcasted_iota(jnp.int32, (BLK, LANE), 1)
    rope_sub0 = ((lane // 32) % 2) == 0
    std_sub0 = lane < 64
    return lane, rope_sub0, std_sub0


def _swa_tri():
    qi = lax.broadcasted_iota(jnp.int32, (BLK, BLK), 0)
    kj = lax.broadcasted_iota(jnp.int32, (BLK, BLK), 1)
    return kj <= qi


def _swa_fold(full, tri):
    return jnp.where(tri, full[:, BLK:], full[:, :BLK])


def _swa_unfold(sq, tri):
    return jnp.concatenate([jnp.where(tri, 0.0, sq), jnp.where(tri, sq, 0.0)], axis=1)


def _swa_keys(kc_ref, kp_ref, vc_ref, vp_ref, cq, sq, cp, sp):
    def ropek(kref, c, s):
        kv = kref[...]
        return jnp.concatenate([_rope(kv[:, :LANE], c, s), _rope(kv[:, LANE:], c, s)], axis=1)

    K = jnp.concatenate([ropek(kp_ref, cp, sp), ropek(kc_ref, cq, sq)], axis=0).astype(MXU)
    V = jnp.concatenate([vp_ref[...], vc_ref[...]], axis=0).astype(MXU)
    return K, V


def _swa_in_specs(nb, last):
    def cur(n):
        return jnp.minimum(n, last)

    def prev(n):
        return jnp.maximum(cur(n) - 1, 0)

    kd, vd = C_KD // 256, C_VD // 256
    return [
        pl.BlockSpec((BLK, D), lambda n: (cur(n), C_Q // D)),
        pl.BlockSpec((BLK, 256), lambda n: (cur(n), kd)),
        pl.BlockSpec((BLK, 256), lambda n: (prev(n), kd)),
        pl.BlockSpec((BLK, 256), lambda n: (cur(n), vd)),
        pl.BlockSpec((BLK, 256), lambda n: (prev(n), vd)),
        pl.BlockSpec((BLK, LANE), lambda n: (cur(n), 0)),
        pl.BlockSpec((BLK, LANE), lambda n: (cur(n), 0)),
        pl.BlockSpec((BLK, LANE), lambda n: (prev(n), 0)),
        pl.BlockSpec((BLK, LANE), lambda n: (prev(n), 0)),
    ]


def _swa_fwd(proj, cos, sin, sinks):
    T = proj.shape[0]
    nb = T // BLK
    scale = A_HD ** -0.5

    def body(sinks_ref, q_ref, kc_ref, kp_ref, vc_ref, vp_ref, cq_ref, sq_ref, cp_ref, sp_ref, o_ref, l_ref):
        n = pl.program_id(0)
        cq, sq = cq_ref[...], sq_ref[...]
        K, V = _swa_keys(kc_ref, kp_ref, vc_ref, vp_ref, cq, sq, cp_ref[...], sp_ref[...])
        tri = _swa_tri()
        valid = tri | (n > 0)
        lane, rope_sub0, std_sub0 = _swa_masks()
        group = A_HEADS // A_KV
        roped, lses = {}, []

        def products(head):
            pb, sub, g = head // 2, head % 2, head // group
            if sub == 0:
                roped[pb] = _rope(q_ref[:, pb * LANE:(pb + 1) * LANE], cq, sq)
            qm = jnp.where(rope_sub0 if sub == 0 else ~rope_sub0, roped[pb], 0.0).astype(MXU)
            return _dot_nt(qm, K[:, g * LANE:(g + 1) * LANE])

        def softmax(head, s_full):
            s = jnp.where(valid, _swa_fold(s_full, tri) * scale, NEG)
            sink = sinks_ref[0, head]
            m = jnp.maximum(jnp.max(s, axis=1, keepdims=True), sink)
            e = jnp.exp(s - m)
            den = jnp.sum(e, axis=1, keepdims=True) + jnp.exp(sink - m)
            lses.append(m + jnp.log(den))
            return _swa_unfold(e / den, tri).astype(MXU)

        outs = {}
        st1 = {0: products(0), 1: products(1)}
        st2 = {0: softmax(0, st1.pop(0))}
        for head in range(A_HEADS):
            if head + 2 < A_HEADS:
                st1[head + 2] = products(head + 2)
            if head + 1 < A_HEADS:
                st2[head + 1] = softmax(head + 1, st1.pop(head + 1))
            g = head // group
            outs[head] = _dot(st2.pop(head), V[:, g * LANE:(g + 1) * LANE])
            if head % 2 == 1:
                pb = head // 2
                o_ref[:, pb * LANE:(pb + 1) * LANE] = jnp.where(std_sub0, outs[head - 1], outs[head])
        lacc = jnp.zeros((BLK, LANE), F32)
        for head in range(A_HEADS):
            lacc = jnp.where(lane == head, lses[head], lacc)
        l_ref[...] = lacc

    return pl.pallas_call(
        body, name="swa_fwd", grid=(nb,),
        in_specs=[pl.BlockSpec(memory_space=pltpu.SMEM)] + _swa_in_specs(nb, nb - 1),
        out_specs=[pl.BlockSpec((BLK, D), lambda n: (n, 0)), pl.BlockSpec((BLK, LANE), lambda n: (n, 0))],
        out_shape=[jax.ShapeDtypeStruct((T, D), F32), jax.ShapeDtypeStruct((T, LANE), F32)],
        compiler_params=_cp(("parallel",)),
    )(sinks, proj, proj, proj, proj, proj, cos, sin, cos, sin)


def _swa_bwd(proj, cos, sin, sinks, do_a, o_a, lse):
    T = proj.shape[0]
    nb = T // BLK
    scale = A_HD ** -0.5

    def body(sinks_ref, q_ref, kc_ref, kp_ref, vc_ref, vp_ref, cq_ref, sq_ref, cp_ref, sp_ref,
             do_ref, o_ref, l_ref, dq_ref, dkv_ref, ds_ref, ckv_ref):
        n = pl.program_id(0)

        @pl.when(n == 0)
        def _():
            ckv_ref[...] = jnp.zeros_like(ckv_ref)
            ds_ref[...] = jnp.zeros_like(ds_ref)

        @pl.when(n < nb)
        def _():
            cq, sq, cp, sp = cq_ref[...], sq_ref[...], cp_ref[...], sp_ref[...]
            K, V = _swa_keys(kc_ref, kp_ref, vc_ref, vp_ref, cq, sq, cp, sp)
            tri = _swa_tri()
            valid = tri | (n > 0)
            lane, rope_sub0, std_sub0 = _swa_masks()
            lane_row = lax.broadcasted_iota(jnp.int32, (1, LANE), 1)
            lse_v = l_ref[...]
            dKt = [jnp.zeros((LANE, 2 * BLK), F32) for _ in range(A_KV)]
            dVt = [jnp.zeros((LANE, 2 * BLK), F32) for _ in range(A_KV)]
            dsinks, roped, roped_t, do_t = [], {}, {}, {}
            group = A_HEADS // A_KV
            dim = lax.broadcasted_iota(jnp.int32, (LANE, BLK), 0)
            rope_row0, std_row0 = ((dim // 32) % 2) == 0, dim < 64

            def products(head):
                pb, sub, g = head // 2, head % 2, head // group
                cols = slice(pb * LANE, (pb + 1) * LANE)
                Kg, Vg = K[:, g * LANE:(g + 1) * LANE], V[:, g * LANE:(g + 1) * LANE]
                if sub == 0:
                    roped[pb] = _rope(q_ref[:, cols], cq, sq)
                    roped_t[pb] = roped[pb].T
                    do_t[pb] = do_ref[:, cols].T
                qm = jnp.where(rope_sub0 if sub == 0 else ~rope_sub0, roped[pb], 0.0).astype(MXU)
                qmt = jnp.where(rope_row0 if sub == 0 else ~rope_row0, roped_t[pb], 0.0).astype(MXU)
                dov = jnp.where(std_sub0 if sub == 0 else ~std_sub0, do_ref[:, cols], 0.0)
                dovt = jnp.where(std_row0 if sub == 0 else ~std_row0, do_t[pb], 0.0).astype(MXU)
                delta = jnp.sum(dov * o_ref[:, cols], axis=1, keepdims=True)
                return qmt, dovt, delta, _dot_nt(qm, Kg), _dot_nt(dov.astype(MXU), Vg)

            def scores(head, qmt, dovt, delta, s_full, dp_full):
                lh = jnp.sum(jnp.where(lane == head, lse_v, 0.0), axis=1, keepdims=True)
                p = jnp.where(valid, jnp.exp(_swa_fold(s_full, tri) * scale - lh), 0.0)
                psink = jnp.exp(sinks_ref[0, head] - lh)
                dsinks.append(jnp.sum(-psink * delta, axis=0, keepdims=True))
                dsq = (p * (_swa_fold(dp_full, tri) - delta)) * scale
                return qmt, dovt, _swa_unfold(p, tri).astype(MXU), _swa_unfold(dsq, tri).astype(MXU)

            def grads(head, qmt, dovt, pb16, dsc):
                g = head // group
                dKt[g] = dKt[g] + _dot(qmt, dsc)
                dVt[g] = dVt[g] + _dot(dovt, pb16)
                return _dot(dsc, K[:, g * LANE:(g + 1) * LANE])

            dqs = {}
            st1 = {0: products(0), 1: products(1)}
            st2 = {0: scores(0, *st1.pop(0))}
            for head in range(A_HEADS):
                if head + 2 < A_HEADS:
                    st1[head + 2] = products(head + 2)
                if head + 1 < A_HEADS:
                    st2[head + 1] = scores(head + 1, *st1.pop(head + 1))
                dqs[head] = grads(head, *st2.pop(head))
                if head % 2 == 1:
                    pb = head // 2
                    dqp = jnp.where(rope_sub0, dqs[head - 1], dqs[head])
                    dq_ref[:, pb * LANE:(pb + 1) * LANE] = _rope_bwd(dqp, cq, sq).astype(dq_ref.dtype)
            dsink = jnp.zeros((1, LANE), F32)
            for head in range(A_HEADS):
                dsink = jnp.where(lane_row == head, dsinks[head], dsink)
            dK, dV = [a.T for a in dKt], [a.T for a in dVt]
            prev = ([_rope_bwd(dK[g][:BLK], cp, sp) for g in range(A_KV)] + [dV[g][:BLK] for g in range(A_KV)])
            cur_ = ([_rope_bwd(dK[g][BLK:], cq, sq) for g in range(A_KV)] + [dV[g][BLK:] for g in range(A_KV)])
            dkv_ref[...] = (ckv_ref[...] + jnp.concatenate(prev, axis=1)).astype(dkv_ref.dtype)
            ckv_ref[...] = jnp.concatenate(cur_, axis=1)
            ds_ref[...] = ds_ref[...] + jnp.broadcast_to(dsink, ds_ref.shape)

        @pl.when(n == nb)
        def _():
            dkv_ref[...] = ckv_ref[...].astype(dkv_ref.dtype)

    last = nb - 1

    def cur(n):
        return jnp.minimum(n, last)

    def out_kv(n):
        return (jnp.maximum(n - 1, 0), 0)

    return pl.pallas_call(
        body, name="swa_bwd", grid=(nb + 1,),
        in_specs=[pl.BlockSpec(memory_space=pltpu.SMEM)] + _swa_in_specs(nb, last) + [
            pl.BlockSpec((BLK, D), lambda n: (cur(n), 0)),
            pl.BlockSpec((BLK, D), lambda n: (cur(n), 0)),
            pl.BlockSpec((BLK, LANE), lambda n: (cur(n), 0)),
        ],
        out_specs=[
            pl.BlockSpec((BLK, D), lambda n: (cur(n), 0)),
            pl.BlockSpec((BLK, 512), out_kv),
            pl.BlockSpec((8, LANE), lambda n: (0, 0)),
        ],
        out_shape=[
            jax.ShapeDtypeStruct((T, D), MXU),
            jax.ShapeDtypeStruct((T, 512), MXU),
            jax.ShapeDtypeStruct((8, LANE), F32),
        ],
        scratch_shapes=[pltpu.VMEM((BLK, 512), F32)],
        compiler_params=_cp(("arbitrary",)),
    )(sinks, proj, proj, proj, proj, proj, cos, sin, cos, sin, do_a, o_a, lse)


def _gla_gate(bl_ref, gu_ref, bias_ref):
    gk = _dot(bl_ref[...].astype(MXU), gu_ref[...]) + bias_ref[...]
    la = (jnp.minimum(gk, 0.0) - jnp.log(1.0 + jnp.exp(-jnp.abs(gk)))) / TAU
    ri = lax.broadcasted_iota(jnp.int32, (CHUNK, CHUNK), 0)
    ci = lax.broadcasted_iota(jnp.int32, (CHUNK, CHUNK), 1)
    b = _dot_f32(jnp.where(ci <= ri, 1.0, 0.0).astype(F32), la)
    return gk, la, b, ri, ci


def _gla_head(q_ref, k_ref, la, b, h):
    sl = slice(h * B_DK, (h + 1) * B_DK)
    bh = b[:, sl]
    blast = jnp.sum(la[:, sl], axis=0, keepdims=True)
    qc = q_ref[:, sl] * (B_DK ** -0.5)
    kh = k_ref[:, sl]
    eb, enb, esb = jnp.exp(bh), jnp.exp(-bh), jnp.exp(blast - bh)
    return qc * eb, kh * enb, kh * esb, eb, enb, esb, jnp.exp(blast)


def _gla_specs(chunk_of):
    return [
        pl.BlockSpec((CHUNK, 512), lambda i: (chunk_of(i), C_BQ // 512)),
        pl.BlockSpec((CHUNK, 512), lambda i: (chunk_of(i), C_BK // 512)),
        pl.BlockSpec((CHUNK, D), lambda i: (chunk_of(i), C_BV // D)),
        pl.BlockSpec((CHUNK, W_BL), lambda i: (chunk_of(i), C_BL // W_BL)),
        pl.BlockSpec((W_BL, 512), lambda i: (0, 0)),
        pl.BlockSpec((1, 512), lambda i: (0, 0)),
    ]


def _gla_fwd(proj, gu_pad, bias):
    T = proj.shape[0]
    nc = T // CHUNK

    def body(q_ref, k_ref, v_ref, bl_ref, gu_ref, bias_ref, o_ref, st_ref, state_ref):
        @pl.when(pl.program_id(0) == 0)
        def _():
            state_ref[...] = jnp.zeros_like(state_ref)

        _, la, b, ri, ci = _gla_gate(bl_ref, gu_ref, bias_ref)
        st_ref[...] = state_ref[...]
        for h in range(B_HEADS):
            q_e, k_e, k_s, _, _, _, decay = _gla_head(q_ref, k_ref, la, b, h)
            vh = v_ref[:, h * B_DV:(h + 1) * B_DV].astype(MXU)
            rows = slice(h * B_DV, (h + 1) * B_DV)
            q_eb = q_e.astype(MXU)
            att = jnp.where(ci <= ri, _dot_nt(q_eb, k_e.astype(MXU)), 0.0)
            st = state_ref[rows, :]
            o_ref[:, rows] = _dot(att.astype(MXU), vh) + _dot_nt(q_eb, st.astype(MXU))
            state_ref[rows, :] = st * decay + _dot_tn(vh, k_s.astype(MXU))

    return pl.pallas_call(
        body, name="gla_fwd", grid=(nc,),
        in_specs=_gla_specs(lambda i: i),
        out_specs=[pl.BlockSpec((CHUNK, D), lambda i: (i, 0)),
                   pl.BlockSpec((B_HEADS * B_DV, B_DK), lambda i: (i, 0))],
        out_shape=[jax.ShapeDtypeStruct((T, D), F32),
                   jax.ShapeDtypeStruct((nc * B_HEADS * B_DV, B_DK), F32)],
        scratch_shapes=[pltpu.VMEM((B_HEADS * B_DV, B_DK), F32)],
        compiler_params=_cp(("arbitrary",)),
    )(proj, proj, proj, proj, gu_pad, bias)


def _gla_bwd(proj, gu_pad, bias, states, do_b):
    T = proj.shape[0]
    nc = T // CHUNK
    o_q, o_k = C_BQ - C_GLA, C_BK - C_GLA

    def body(q_ref, k_ref, v_ref, bl_ref, gu_ref, bias_ref, st_ref, do_ref,
             dg_ref, dbl_ref, ggu_ref, gbias_ref, gt_ref):
        @pl.when(pl.program_id(0) == 0)
        def _():
            gt_ref[...] = jnp.zeros_like(gt_ref)
            ggu_ref[...] = jnp.zeros_like(ggu_ref)
            gbias_ref[...] = jnp.zeros_like(gbias_ref)

        gk, la, b, ri, ci = _gla_gate(bl_ref, gu_ref, bias_ref)
        causal = ci <= ri
        upper = jnp.where(ci >= ri, 1.0, 0.0).astype(F32)
        dla_parts = []
        for h in range(B_HEADS):
            q_e, k_e, k_s, eb, enb, esb, decay = _gla_head(q_ref, k_ref, la, b, h)
            rows = slice(h * B_DV, (h + 1) * B_DV)
            sl = slice(h * B_DK, (h + 1) * B_DK)
            vh = v_ref[:, rows].astype(MXU)
            doh = do_ref[:, rows].astype(MXU)
            q_eb, k_eb, k_sb = q_e.astype(MXU), k_e.astype(MXU), k_s.astype(MXU)
            st = st_ref[rows, :]
            gt = gt_ref[rows, :]
            gtb = gt.astype(MXU)
            att = jnp.where(causal, _dot_nt(q_eb, k_eb), 0.0).astype(MXU)
            datt = jnp.where(causal, _dot_nt(doh, vh), 0.0).astype(MXU)
            dq_e = _dot(datt, k_eb) + _dot(doh, st.astype(MXU))
            dk_e = _dot_tn(datt, q_eb)
            dk_s = _dot(vh, gtb)
            dg_ref[:, rows] = (_dot_tn(att, doh) + _dot_nt(k_sb, gtb)).astype(dg_ref.dtype)
            ddecay = jnp.sum(gt * st, axis=0, keepdims=True)
            gt_ref[rows, :] = gt * decay + _dot_tn(doh, q_eb)
            dg_ref[:, o_q + h * B_DK:o_q + (h + 1) * B_DK] = (dq_e * eb * (B_DK ** -0.5)).astype(dg_ref.dtype)
            dg_ref[:, o_k + h * B_DK:o_k + (h + 1) * B_DK] = (dk_e * enb + dk_s * esb).astype(dg_ref.dtype)
            dks_ks = dk_s * k_s
            db = dq_e * q_e - dk_e * k_e - dks_ks
            dblast = jnp.sum(dks_ks, axis=0, keepdims=True) + ddecay * decay
            dla_parts.append(_dot_f32(upper, db) + dblast)
        dla = jnp.concatenate(dla_parts, axis=1)
        dgk = dla * (1.0 / TAU) * _sigmoid(-gk)
        dgkb = dgk.astype(MXU)
        dbl_ref[...] = _dot_nt(dgkb, gu_ref[...]).astype(dbl_ref.dtype)
        ggu_ref[...] = ggu_ref[...] + _dot_tn(bl_ref[...].astype(MXU), dgkb)
        gbias_ref[...] = gbias_ref[...] + jnp.broadcast_to(jnp.sum(dgk, axis=0, keepdims=True), gbias_ref.shape)

    def rev(i):
        return nc - 1 - i

    return pl.pallas_call(
        body, name="gla_bwd", grid=(nc,),
        in_specs=_gla_specs(rev) + [
            pl.BlockSpec((B_HEADS * B_DV, B_DK), lambda i: (rev(i), 0)),
            pl.BlockSpec((CHUNK, D), lambda i: (rev(i), 0)),
        ],
        out_specs=[
            pl.BlockSpec((CHUNK, W_GLA), lambda i: (rev(i), 0)),
            pl.BlockSpec((CHUNK, W_BL), lambda i: (rev(i), 0)),
            pl.BlockSpec((W_BL, 512), lambda i: (0, 0)),
            pl.BlockSpec((8, 512), lambda i: (0, 0)),
        ],
        out_shape=[
            jax.ShapeDtypeStruct((T, W_GLA), MXU),
            jax.ShapeDtypeStruct((T, W_BL), MXU),
            jax.ShapeDtypeStruct((W_BL, 512), F32),
            jax.ShapeDtypeStruct((8, 512), F32),
        ],
        scratch_shapes=[pltpu.VMEM((B_HEADS * B_DV, B_DK), F32)],
        compiler_params=_cp(("arbitrary",)),
    )(proj, proj, proj, proj, gu_pad, bias, states, do_b)


def _mid(x, target, proj, o_a, o_b, w_a, w_b, w_out, w_bn4, fnw):
    T = x.shape[0]
    tT = min(T, 128)
    nbuf = 4
    o_ag, o_bg, o_ma, o_mb = (c - C_GATES for c in (C_AG, C_BG, C_MA, C_MB))

    def body(x_ref, t_ref, oa_ref, ob_ref, gates_ref, wa_ref, wb_ref, wo_ref, wbn_ref, fnw_ref,
             dx2_ref, doa_ref, dob_ref, dgates_ref,
             gwa_ref, gwb_ref, gwo_ref, gfn_ref, gbn_ref, loss_ref, buf_ref):
        i = pl.program_id(0)

        @pl.when(i == 0)
        def _():
            for r in (gwa_ref, gwb_ref, gwo_ref, gfn_ref, gbn_ref, loss_ref):
                r[...] = jnp.zeros_like(r)

        rows = pl.ds(pl.multiple_of((i % nbuf) * tT, tT), tT)

        def keep(k, val):
            buf_ref[k, rows, :] = val

        oa, ag = oa_ref[...], gates_ref[:, o_ag:o_ag + D]
        sg_a = _sigmoid(ag)
        silu_a = ag * sg_a
        oag_b = (oa * silu_a).astype(MXU)
        keep(0, oag_b)
        y_a = _dot(oag_b, wa_ref[...])

        ob, bg = ob_ref[...], gates_ref[:, o_bg:o_bg + D]
        rbs, obhats = [], []
        for h in range(B_HEADS):
            obh = ob[:, h * B_DV:(h + 1) * B_DV]
            rb = lax.rsqrt(jnp.mean(obh * obh, axis=-1, keepdims=True) + EPS)
            rbs.append(rb)
            obhats.append(obh * rb)
        obhat = jnp.concatenate(obhats, axis=1)
        wbn = wbn_ref[...]
        obn = obhat * wbn
        sg_b = _sigmoid(bg)
        silu_b = bg * sg_b
        obg_b = (obn * silu_b).astype(MXU)
        keep(1, obg_b)
        y_b = _dot(obg_b, wb_ref[...])

        sa, sb = _sigmoid(gates_ref[:, o_ma:o_ma + D]), _sigmoid(gates_ref[:, o_mb:o_mb + D])
        mg_b = (sa * y_a + sb * y_b).astype(MXU)
        keep(2, mg_b)
        x2 = x_ref[...] + _dot(mg_b, wo_ref[...])
        r2 = lax.rsqrt(jnp.mean(x2 * x2, axis=-1, keepdims=True) + EPS)
        xh2 = x2 * r2
        fw = fnw_ref[...]
        err = xh2 * fw - t_ref[...]
        tok = jnp.mean(err * err, axis=-1, keepdims=True)
        loss_ref[...] = loss_ref[...] + 0.5 * jnp.sum(tok, axis=0, keepdims=True)

        dy = err * (1.0 / D)
        gfn_ref[...] = gfn_ref[...] + jnp.broadcast_to(jnp.sum(dy * xh2, axis=0, keepdims=True), gfn_ref.shape)
        gy = dy * fw
        dx2 = r2 * (gy - xh2 * jnp.mean(gy * xh2, axis=-1, keepdims=True))
        dx2_ref[...] = dx2
        dx2_b = dx2.astype(MXU)
        keep(5, dx2_b)
        dmg = _dot_nt(dx2_b, wo_ref[...])

        dgates_ref[:, o_ma:o_ma + D] = (dmg * y_a * sa * (1.0 - sa)).astype(dgates_ref.dtype)
        dgates_ref[:, o_mb:o_mb + D] = (dmg * y_b * sb * (1.0 - sb)).astype(dgates_ref.dtype)
        dya_b = (dmg * sa).astype(MXU)
        dyb_b = (dmg * sb).astype(MXU)
        keep(3, dya_b)
        keep(4, dyb_b)
        doag = _dot_nt(dya_b, wa_ref[...])
        dobg = _dot_nt(dyb_b, wb_ref[...])

        @pl.when(i % nbuf == nbuf - 1)
        def _():
            gwa_ref[...] = gwa_ref[...] + _dot_tn(buf_ref[0], buf_ref[3])
            gwb_ref[...] = gwb_ref[...] + _dot_tn(buf_ref[1], buf_ref[4])
            gwo_ref[...] = gwo_ref[...] + _dot_tn(buf_ref[2], buf_ref[5])

        doa_ref[...] = doag * silu_a
        dgates_ref[:, o_ag:o_ag + D] = (doag * oa * (sg_a * (1.0 + ag * (1.0 - sg_a)))).astype(dgates_ref.dtype)
        dobn = dobg * silu_b
        dgates_ref[:, o_bg:o_bg + D] = (dobg * obn * (sg_b * (1.0 + bg * (1.0 - sg_b)))).astype(dgates_ref.dtype)
        gg = dobn * wbn
        gbn = jnp.zeros((1, B_DV), F32)
        for h in range(B_HEADS):
            sl = slice(h * B_DV, (h + 1) * B_DV)
            gbn = gbn + jnp.sum(dobn[:, sl] * obhats[h], axis=0, keepdims=True)
            ggh = gg[:, sl]
            dob_ref[:, sl] = rbs[h] * (ggh - obhats[h] * jnp.mean(ggh * obhats[h], axis=-1, keepdims=True))
        gbn_ref[...] = gbn_ref[...] + jnp.broadcast_to(gbn, gbn_ref.shape)

    assert (T // tT) % nbuf == 0
    tile = pl.BlockSpec((tT, D), lambda i: (i, 0))
    row = pl.BlockSpec((1, D), lambda i: (0, 0))
    acc8 = pl.BlockSpec((8, D), lambda i: (0, 0))
    return pl.pallas_call(
        body, name="mid", grid=(T // tT,),
        in_specs=[tile, tile, tile, tile, pl.BlockSpec((tT, W_GATES), lambda i: (i, C_GATES // W_GATES)),
                  _vmem(), _vmem(), _vmem(), row, row],
        out_specs=[tile, tile, tile, pl.BlockSpec((tT, W_GATES), lambda i: (i, 0)), _vmem(), _vmem(), _vmem(),
                   acc8, pl.BlockSpec((8, B_DV), lambda i: (0, 0)), pl.BlockSpec((8, LANE), lambda i: (0, 0))],
        out_shape=[
            jax.ShapeDtypeStruct((T, D), F32),
            jax.ShapeDtypeStruct((T, D), F32),
            jax.ShapeDtypeStruct((T, D), F32),
            jax.ShapeDtypeStruct((T, W_GATES), MXU),
            jax.ShapeDtypeStruct((D, D), F32),
            jax.ShapeDtypeStruct((D, D), F32),
            jax.ShapeDtypeStruct((D, D), F32),
            jax.ShapeDtypeStruct((8, D), F32),
            jax.ShapeDtypeStruct((8, B_DV), F32),
            jax.ShapeDtypeStruct((8, LANE), F32),
        ],
        scratch_shapes=[pltpu.VMEM((6, nbuf * tT, D), MXU)],
        compiler_params=_cp(("arbitrary",)),
    )(x, target, o_a, o_b, proj, w_a, w_b, w_out, w_bn4, fnw)


DH = D // 2


def _gw_half(h, pieces, half, after=None):
    T = h.shape[0]
    steps = NF // 512
    tiles = ((0, 2), (2, 3), (4, 8), (8, 16))

    def body(*refs):
        h_ref, q_ref, kv_ref, bl_ref, gla_ref, gates_ref = refs[:6]
        o_ref = refs[-1]
        j = pl.program_id(0)

        for (lo, hi), ref in zip(tiles, (q_ref, kv_ref, gla_ref, gates_ref)):
            @pl.when((j >= lo) & (j < hi))
            def _(ref=ref):
                o_ref[...] = _dot_tn(ref[...], h_ref[...])

        @pl.when(j == 3)
        def _():
            o_ref[0:W_BL, :] = _dot_tn(bl_ref[...], h_ref[...])
            o_ref[W_BL:, :] = jnp.zeros((512 - W_BL, DH), F32)

    def tile_of(lo, hi):
        return lambda j: (0, jnp.clip(j - lo, 0, hi - lo - 1))

    in_specs = [pl.BlockSpec((T, DH), lambda j: (0, half)),
                pl.BlockSpec((T, 512), tile_of(0, 2)), pl.BlockSpec((T, 512), lambda j: (0, 0)),
                pl.BlockSpec((T, W_BL), lambda j: (0, 0)),
                pl.BlockSpec((T, 512), tile_of(4, 8)), pl.BlockSpec((T, 512), tile_of(8, 16))]
    args = [h, *pieces]
    if after is not None:
        in_specs.append(_any())
        args.append(after)
    return pl.pallas_call(
        body, name=f"gw_in_half{half}", grid=(steps,),
        in_specs=in_specs, out_specs=pl.BlockSpec((512, DH), lambda j: (j, 0)),
        out_shape=jax.ShapeDtypeStruct((NF, DH), F32),
        compiler_params=_cp(("parallel",)),
    )(*args)


def _chip_copies(s_ref, got_ref, send_sems, recv_sems):
    x, y, c = _place()
    chips = [(1 - x, y), (x, 1 - y), (1 - x, 1 - y)]
    return [pltpu.make_async_remote_copy(
        src_ref=s_ref.at[2 * px + py], dst_ref=got_ref.at[j],
        send_sem=send_sems.at[j], recv_sem=recv_sems.at[j], device_id=(px, py, c), device_id_type=MESH)
        for j, (px, py) in enumerate(chips)]


_EFFECT = pltpu.SideEffectType.DATAFLOW_SIDE_EFFECTING


def _hbm():
    return pl.BlockSpec(memory_space=pltpu.HBM)


def _sem():
    return pl.BlockSpec(memory_space=pltpu.SEMAPHORE)


def _chip_start(sums, half):
    land = pltpu.with_memory_space_constraint(lax.empty((3,) + sums.shape[1:], sums.dtype), pltpu.HBM)

    def body(s_ref, land_ref, send_sems, recv_sems, s_thru, land_thru, token):
        for cp in _chip_copies(s_ref, land_ref, send_sems, recv_sems):
            cp.start()
        token[...] = jnp.zeros_like(token)

    return pl.pallas_call(
        body, name=f"chip_start{half}",
        out_shape=(pltpu.SemaphoreType.DMA((3,)), pltpu.SemaphoreType.DMA((3,)),
                   pltpu.HBM(sums.shape, sums.dtype), pltpu.HBM(land.shape, land.dtype),
                   jax.ShapeDtypeStruct((8, LANE), F32)),
        in_specs=(_hbm(), _hbm()), out_specs=(_sem(), _sem(), _hbm(), _hbm(), _vmem()),
        input_output_aliases={0: 2, 1: 3},
        compiler_params=pltpu.CompilerParams(has_side_effects=_EFFECT),
    )(pltpu.with_memory_space_constraint(sums, pltpu.HBM), land)


def _chip_wait(send_sems, recv_sems, s_thru, land_thru, after, half):
    def body(s_ref, land_ref, send_sems, recv_sems, after_ref, s_out, got_ref):
        copies = _chip_copies(s_ref, land_ref, send_sems, recv_sems)
        for cp in copies:
            cp.wait_send()
        for cp in copies:
            cp.wait_recv()

    return pl.pallas_call(
        body, name=f"chip_wait{half}",
        out_shape=(pltpu.HBM(s_thru.shape, s_thru.dtype), pltpu.HBM(land_thru.shape, land_thru.dtype)),
        in_specs=(_hbm(), _hbm(), _sem(), _sem(), _any()), out_specs=(_hbm(), _hbm()),
        input_output_aliases={0: 0, 1: 1},
        compiler_params=pltpu.CompilerParams(has_side_effects=_EFFECT),
    )(s_thru, land_thru, send_sems, recv_sems, after)


def _dh_norm(pieces, offsets, wf, x, dx2, norm_w, after):
    T = x.shape[0]
    tT = min(T, 256)
    widths = [p.shape[1] for p in pieces]
    npc = len(pieces)

    def body(*refs):
        dp_refs = refs[:npc]
        wf_ref, x_ref, dx2_ref, nw_ref, _, gx_ref, gnw_ref = refs[npc:]

        @pl.when(pl.program_id(0) == 0)
        def _():
            gnw_ref[...] = jnp.zeros_like(gnw_ref)

        dh = jnp.zeros((tT, D), F32)
        for dp_ref, off, w in zip(dp_refs, offsets, widths):
            dh = dh + _dot(dp_ref[...], wf_ref[off:off + w, :])
        xv = x_ref[...]
        r = lax.rsqrt(jnp.mean(xv * xv, axis=-1, keepdims=True) + EPS)
        xh = xv * r
        gnw_ref[...] = gnw_ref[...] + jnp.broadcast_to(jnp.sum(dh * xh, axis=0, keepdims=True), gnw_ref.shape)
        g = dh * nw_ref[...]
        gx_ref[...] = r * (g - xh * jnp.mean(g * xh, axis=-1, keepdims=True)) + dx2_ref[...]

    tile = pl.BlockSpec((tT, D), lambda i: (i, 0))
    return pl.pallas_call(
        body, name="dh_norm", grid=(T // tT,),
        in_specs=[pl.BlockSpec((tT, w), lambda i: (i, 0)) for w in widths]
        + [_vmem(), tile, tile, pl.BlockSpec((1, D), lambda i: (0, 0)), _any()],
        out_specs=[tile, pl.BlockSpec((8, D), lambda i: (0, 0))],
        out_shape=[jax.ShapeDtypeStruct((T, D), F32), jax.ShapeDtypeStruct((8, D), F32)],
        compiler_params=_cp(("arbitrary",)),
    )(*pieces, wf, x, dx2, norm_w, after)


def _adamw_math(w, g, m, v):
    m = ADAM_B1 * m + (1.0 - ADAM_B1) * g
    v = ADAM_B2 * v + (1.0 - ADAM_B2) * (g * g)
    m_hat = m / (1.0 - ADAM_B1 ** ADAM_STEP)
    v_hat = v / (1.0 - ADAM_B2 ** ADAM_STEP)
    delta = -ADAM_LR * (m_hat / (jnp.sqrt(v_hat) + ADAM_EPS) + ADAM_WD * w)
    return delta, m, v


def _fetch_partials(s_refs, got_refs, buf, sems):
    x, y, _ = _place()
    cps = []
    for hf, (s_ref, got_ref) in enumerate(zip(s_refs, got_refs)):
        cols = pl.ds(hf * DH, DH)
        cps.append(pltpu.make_async_copy(s_ref.at[2 * x + y], buf.at[0, :, cols], sems.at[4 * hf]))
        cps += [pltpu.make_async_copy(got_ref.at[j], buf.at[1 + j, :, cols], sems.at[4 * hf + 1 + j])
                for j in range(3)]
    for cp in cps:
        cp.start()
    for cp in cps:
        cp.wait()


SMALL_AT = dict(norm_w=0, fnw=8, bias=16, bn=24, sinks=32, loss=40)
ROW_AT = (R_IN, R_A, R_B, R_O)


def _finish(w_rows, m_rows, v_rows, ws, ms, vs, gu_w, gu_m, gu_v, small, sums, got):
    names = ["norm_w", "fnw", "bias", "bn", "sinks"]
    widths = [ws[n].shape[1] for n in names]
    shapes = [w.shape for w in w_rows]

    def body(*refs):
        wr_refs, mr_refs, vr_refs = refs[0:4], refs[4:8], refs[8:12]
        refs = refs[12:]
        w_refs, m_refs, v_refs = refs[0:5], refs[5:10], refs[10:15]
        guw_ref, gum_ref, guv_ref, small_ref = refs[15:19]
        s_refs, got_refs = refs[19:21], refs[21:23]
        loss_ref = refs[23]
        row_outs = refs[24:40]
        outs = refs[40:60]
        gu_outs = refs[60:64]
        smalls, tot, buf, gsh, send_sems, recv_sems, sems = refs[64:]
        x, y, c = _place()
        me_slot = 4 * x + 2 * y + c
        sends = []
        k = 0
        for dx in range(2):
            for dy in range(2):
                for dc in range(2):
                    if dx == 0 and dy == 0 and dc == 0:
                        continue
                    sends.append(pltpu.make_async_remote_copy(
                        src_ref=small_ref, dst_ref=smalls.at[me_slot],
                        send_sem=send_sems.at[k], recv_sem=recv_sems.at[k],
                        device_id=(x ^ dx, y ^ dy, c ^ dc), device_id_type=MESH))
                    k += 1
        for cp in sends:
            cp.start()
        smalls[me_slot] = small_ref[...]
        _fetch_partials(s_refs, got_refs, buf, sems)
        unshift = lax.rem(SHARD_PAD - 2 * me_slot, SHARD_PAD)
        for p in range(4):
            n, off = shapes[p][0], ROW_AT[p]
            nf = SHARD_PAD if p == 0 else n
            for cc in range(D // LANE):
                cols = slice(cc * LANE, (cc + 1) * LANE)
                g = buf[0, off:off + nf, cols].astype(F32)
                for j in range(1, 4):
                    g = g + buf[j, off:off + nf, cols].astype(F32)
                if p == 0:
                    gsh[...] = pltpu.roll(g, unshift, 0)
                    g = gsh[0:n, :]
                d, nm, nv = _adamw_math(wr_refs[p][:, cols], g, mr_refs[p][:, cols], vr_refs[p][:, cols])
                for o, val in zip(row_outs[4 * p:4 * p + 4], (g, d, nm, nv)):
                    o[:, cols] = val
        for cp in sends:
            cp.wait_recv()
        for cp in sends:
            cp.wait_send()
        acc = smalls[0]
        for d in range(1, NDEV):
            acc = acc + smalls[d]
        tot[...] = acc
        loss_ref[...] = tot[SMALL_AT["loss"]:SMALL_AT["loss"] + 1, 0:1]
        for p, (nm_, wd) in enumerate(zip(names, widths)):
            r = SMALL_AT[nm_]
            g = tot[r:r + 1, 0:wd]
            d, nm, nv = _adamw_math(w_refs[p][...], g, m_refs[p][...], v_refs[p][...])
            for o, val in zip(outs[4 * p:4 * p + 4], (g, d, nm, nv)):
                o[...] = val
        g = buf[0, R_GU:R_GU + RANK, 0:64].astype(F32)
        for j in range(1, 4):
            g = g + buf[j, R_GU:R_GU + RANK, 0:64].astype(F32)
        d, nm, nv = _adamw_math(guw_ref[...], g, gum_ref[...], guv_ref[...])
        for o, val in zip(gu_outs, (g, d, nm, nv)):
            o[...] = val

    out_shape = ([jax.ShapeDtypeStruct((1, 1), F32)]
                 + [jax.ShapeDtypeStruct(s, F32) for s in shapes for _ in range(4)]
                 + [jax.ShapeDtypeStruct((1, wd), F32) for wd in widths for _ in range(4)]
                 + [jax.ShapeDtypeStruct((RANK, 64), F32)] * 4)
    res = pl.pallas_call(
        body, name="finish",
        in_specs=[_vmem()] * 31 + [_any()] * 4,
        out_specs=[_vmem()] * 41,
        out_shape=out_shape,
        scratch_shapes=[pltpu.VMEM((NDEV, SMALL_ROWS, D), F32), pltpu.VMEM((SMALL_ROWS, D), F32),
                        pltpu.VMEM((4, ROWS, D), sums[0].dtype), pltpu.VMEM((SHARD_PAD, LANE), F32),
                        pltpu.SemaphoreType.DMA((7,)), pltpu.SemaphoreType.DMA((7,)), pltpu.SemaphoreType.DMA((8,))],
        compiler_params=_cp(),
    )(*w_rows, *m_rows, *v_rows, *[ws[n] for n in names], *[ms[n] for n in names], *[vs[n] for n in names],
      gu_w, gu_m, gu_v, small, *sums, *got)
    loss = res[0]
    per = {n: tuple(res[17 + 4 * p:21 + 4 * p]) for p, n in enumerate(names)}
    return loss, tuple(res[1:17]), per, tuple(res[37:41])


def _place():
    x, y, c = lax.axis_index("x"), lax.axis_index("y"), lax.axis_index("c")
    return x, y, c


def _gather_blocks(w_in_t, w_a_s, w_b_s, w_o_s, gu_s, xs, norm_w, pos_col):
    rows, cols = ROWS, D
    T = xs.shape[0]
    tT = min(T, 256)
    inv_row, sign_row = _rope_rows()

    def body(wi_ref, wa_ref, wb_ref, wo_ref, gu_ref, xs_ref, nw_ref, pos_ref, inv_ref, sign_ref,
             out_ref, h_ref, cos_ref, sin_ref, x_ref, frame_ref, send_sems, recv_sems, local_sem):
        x, y, c = _place()
        me, sibling = (x, y, c), (x, y, 1 - c)
        chips = [(1 - x, y), (x, 1 - y), (1 - x, 1 - y)]
        shift = 2 * (4 * x + 2 * y + c)
        frame_ref[SHARD - SHARD % 8:, :] = jnp.zeros((SHARD_PAD - SHARD + SHARD % 8, D), F32)
        frame_ref[:SHARD, :] = wi_ref[...]
        for cc in range(D // LANE):
            cs = slice(cc * LANE, (cc + 1) * LANE)
            x_ref[R_IN:R_IN + SHARD_PAD, cs] = pltpu.roll(frame_ref[:, cs], shift, 0).astype(x_ref.dtype)
        x_ref[R_A:R_A + 128, :] = wa_ref[...].astype(x_ref.dtype)
        x_ref[R_B:R_B + 128, :] = wb_ref[...].astype(x_ref.dtype)
        x_ref[R_O:R_O + 128, :] = wo_ref[...].astype(x_ref.dtype)
        x_ref[R_GU:R_GU + RANK, :] = jnp.zeros((RANK, D), x_ref.dtype)
        x_ref[R_GU:R_GU + RANK, 0:64] = gu_ref[...].astype(x_ref.dtype)

        def slot(px, py, pc):
            return out_ref.at[4 * px + 2 * py + pc]

        def copy(k, block, to, src=None):
            return pltpu.make_async_remote_copy(
                src_ref=slot(*block) if src is None else src, dst_ref=slot(*block),
                send_sem=send_sems.at[k], recv_sem=recv_sems.at[k], device_id=to, device_id_type=MESH)

        mine = pltpu.make_async_copy(x_ref, slot(*me), local_sem)
        mine.start()
        first = [copy(0, me, sibling, src=x_ref)]
        first += [copy(1 + j, me, (*chip, c), src=x_ref) for j, chip in enumerate(chips)]
        for cp in first:
            cp.start()

        @pl.loop(0, T // tT)
        def _(i):
            rows_i = pl.ds(pl.multiple_of(i * tT, tT), tT)
            _prologue_rows(rows_i, xs_ref, nw_ref, pos_ref, inv_ref, sign_ref, h_ref, cos_ref, sin_ref)

        passed = [copy(4 + j, (*chip, c), sibling) for j, chip in enumerate(chips)]
        for j, chip in enumerate(chips):
            copy(1 + j, (*chip, c), me).wait_recv()
            passed[j].start()
        copy(0, sibling, me).wait_recv()
        for j, chip in enumerate(chips):
            copy(4 + j, (*chip, 1 - c), me).wait_recv()
        for cp in first + passed:
            cp.wait_send()
        mine.wait()

    return pl.pallas_call(
        body, name="gather_weights",
        in_specs=[_vmem()] * 10, out_specs=[_any()] + [_vmem()] * 3,
        out_shape=[jax.ShapeDtypeStruct((NDEV, rows, cols), WIRE), jax.ShapeDtypeStruct((T, D), MXU),
                   jax.ShapeDtypeStruct((T, LANE), F32), jax.ShapeDtypeStruct((T, LANE), F32)],
        scratch_shapes=[pltpu.VMEM((rows, cols), WIRE), pltpu.VMEM((SHARD_PAD, D), F32),
                        pltpu.SemaphoreType.DMA((7,)), pltpu.SemaphoreType.DMA((7,)), pltpu.SemaphoreType.DMA],
        compiler_params=_cp(),
    )(w_in_t, w_a_s, w_b_s, w_o_s, gu_s, xs, norm_w, pos_col, inv_row, sign_row)


def _pair_reduce(packed):
    def body(p_ref, out_ref, got, own, send_sems, recv_sems, own_sems):
        x, y, c = _place()
        sends = [pltpu.make_async_remote_copy(
            src_ref=p_ref.at[2 * chip + (1 - c)], dst_ref=got.at[chip],
            send_sem=send_sems.at[chip], recv_sem=recv_sems.at[chip], device_id=(x, y, 1 - c), device_id_type=MESH)
            for chip in range(4)]
        loads = [pltpu.make_async_copy(p_ref.at[2 * chip + c], own.at[chip], own_sems.at[chip]) for chip in range(4)]
        for cp in sends + loads:
            cp.start()
        for chip in range(4):
            loads[chip].wait()
            sends[chip].wait_recv()
            out_ref[chip] = (own[chip].astype(F32) + got[chip].astype(F32)).astype(out_ref.dtype)
        for cp in sends:
            cp.wait_send()

    return pl.pallas_call(
        body, name="pair_reduce",
        in_specs=[_any()], out_specs=_vmem(),
        out_shape=jax.ShapeDtypeStruct((4,) + packed.shape[1:], packed.dtype),
        scratch_shapes=[pltpu.VMEM((4,) + packed.shape[1:], packed.dtype), pltpu.VMEM((4,) + packed.shape[1:], packed.dtype),
                        pltpu.SemaphoreType.DMA((4,)), pltpu.SemaphoreType.DMA((4,)), pltpu.SemaphoreType.DMA((4,))],
        compiler_params=_cp(),
    )(packed)


def _pad_cols(a, cols):
    return jnp.pad(a, ((0, 0), (0, cols - a.shape[1])))


def _pad_rows(a, rows):
    return jnp.pad(a, ((0, rows - a.shape[0]), (0, 0)))


FRAME = 928


def _join_frames(frames):
    head = frames[:, :FRAME].at[1:, :16].add(frames[:-1, FRAME:])
    return jnp.concatenate([head.reshape(NDEV * FRAME, D), frames[NDEV - 1, FRAME:]], axis=0)


def _build_wft(wt):
    q = wt[0:1024].reshape(8, 2, 2, 32, D).transpose(0, 2, 1, 3, 4).reshape(1024, D)
    k = wt[1024:1152].reshape(2, 2, 1, 32, D)
    kd = jnp.broadcast_to(k, (2, 2, 2, 32, D)).reshape(256, D)
    v = wt[1152:1280].reshape(2, 1, 64, D)
    vd = jnp.broadcast_to(v, (2, 2, 64, D)).reshape(256, D)
    ag, bq, bk = wt[1280:2304], wt[2304:2816], wt[2816:3328]
    bv, bg, bl = wt[3328:4352], wt[4352:5376], wt[5376:5392]
    ma, mb = wt[5392:6416], wt[6416:7440]
    return jnp.concatenate([q, kd, vd, _pad_rows(bl, C_GLA - C_BL), bv, bq, bk, ag, bg, ma, mb], axis=0)


def _unbuild_gwt(g):
    n = g.shape[1]
    q = g[C_Q:C_Q + 1024].reshape(8, 2, 2, 32, n).transpose(0, 2, 1, 3, 4).reshape(1024, n)
    k = g[C_KD:C_KD + 256].reshape(2, 2, 2, 32, n).sum(axis=2).reshape(128, n)
    v = g[C_VD:C_VD + 256].reshape(2, 2, 64, n).sum(axis=1).reshape(128, n)
    bv, bq, bk = g[C_BV:C_BV + 1024], g[C_BQ:C_BQ + 512], g[C_BK:C_BK + 512]
    ag, bg, ma, mb = (g[c:c + 1024] for c in (C_AG, C_BG, C_MA, C_MB))
    return jnp.concatenate([q, k, v, ag, bq, bk, bv, bg, g[C_BL:C_BL + RANK], ma, mb], axis=0)


def kernel(x, positions, norm_w, w_in, a_sinks, b_gate_up, b_gate_bias, b_out_norm_w, w_a_proj, w_b_proj, w_out, final_norm_w, loss_target, m_norm_w, m_w_in, m_a_sinks, m_b_gate_up, m_b_gate_bias, m_b_out_norm_w, m_w_a_proj, m_w_b_proj, m_w_out, m_final_norm_w, v_norm_w, v_w_in, v_a_sinks, v_b_gate_up, v_b_gate_bias, v_b_out_norm_w, v_w_a_proj, v_w_b_proj, v_w_out, v_final_norm_w):
    T = x.shape[1]
    xs, target = x[0], loss_target[0]
    fnw = final_norm_w.reshape(1, D)
    allw, h, cos, sin = _gather_blocks(w_in[0].T, w_a_proj[0], w_b_proj[0], w_out[0], b_gate_up[0],
                                       xs, norm_w, positions.reshape(T, 1))
    wf = _build_wft(_join_frames(allw[:, :SHARD_PAD]))
    w_a = allw[:, R_A:R_A + 128, :].reshape(D, D)
    w_b = allw[:, R_B:R_B + 128, :].reshape(D, D)
    w_o = allw[:, R_O:R_O + 128, :].reshape(D, D)
    gu = allw[:, R_GU:R_GU + RANK, :64].transpose(1, 0, 2).reshape(RANK, 512)
    gu_pad = _pad_rows(gu, W_BL)

    proj = _proj(h, wf)
    o_a, lse = _swa_fwd(proj, cos, sin, a_sinks)
    o_b, states = _gla_fwd(proj, gu_pad, b_gate_bias)
    (dx2, do_a, do_b, d_gates, g_wa, g_wb, g_wo, g_fn, g_bn, loss_part) = _mid(
        xs, target, proj, o_a, o_b, w_a, w_b, w_o, jnp.tile(b_out_norm_w, (1, B_HEADS)), fnw)
    d_q, d_kv, g_sinks = _swa_bwd(proj, cos, sin, a_sinks, do_a, o_a, lse)
    d_gla, d_bl, g_gu, g_bias = _gla_bwd(proj, gu_pad, b_gate_bias, states, do_b)
    pieces = [d_q, d_kv, d_bl, d_gla, d_gates]
    offsets = [C_Q, C_KD, C_BL, C_GLA, C_GATES]

    ggu = g_gu[:RANK].reshape(RANK, NDEV, 64).transpose(1, 0, 2)
    ggu_half = [jnp.pad(ggu, ((0, 0), (0, 0), (0, DH - 64))), jnp.zeros((NDEV, RANK, DH), F32)]

    def pack(gw_half, hf):
        gwt = _unbuild_gwt(gw_half).astype(WIRE)
        cols = slice(hf * DH, (hf + 1) * DH)
        return jnp.concatenate([
            jnp.stack([gwt[FRAME * d:FRAME * d + SHARD_PAD] for d in range(NDEV)]),
            g_wa[:, cols].reshape(NDEV, 128, DH).astype(WIRE),
            g_wb[:, cols].reshape(NDEV, 128, DH).astype(WIRE),
            g_wo[:, cols].reshape(NDEV, 128, DH).astype(WIRE),
            ggu_half[hf].astype(WIRE)], axis=1)

    send0, recv0, s_thru0, land0, started0 = _chip_start(_pair_reduce(pack(_gw_half(h, pieces, 0), 0)), 0)
    send1, recv1, s_thru1, land1, started1 = _chip_start(
        _pair_reduce(pack(_gw_half(h, pieces, 1, after=started0), 1)), 1)
    grad_x, g_nw = _dh_norm(pieces, offsets, wf, xs, dx2, norm_w, started1)
    sums0, got0 = _chip_wait(send0, recv0, s_thru0, land0, g_nw, 0)
    sums1, got1 = _chip_wait(send1, recv1, s_thru1, land1, got0, 1)
    sums, from_chips = [sums0, sums1], [got0, got1]

    small = jnp.concatenate([g_nw, g_fn, _pad_cols(g_bias, D), _pad_cols(g_bn, D), _pad_cols(g_sinks, D),
                             _pad_cols(loss_part, D)], axis=0)
    ws = dict(norm_w=norm_w, fnw=fnw, bias=b_gate_bias, bn=b_out_norm_w, sinks=a_sinks)
    ms = dict(norm_w=m_norm_w, fnw=m_final_norm_w.reshape(1, D), bias=m_b_gate_bias, bn=m_b_out_norm_w,
              sinks=m_a_sinks)
    vs = dict(norm_w=v_norm_w, fnw=v_final_norm_w.reshape(1, D), bias=v_b_gate_bias, bn=v_b_out_norm_w,
              sinks=v_a_sinks)
    loss, t_rows, sm, t_gu = _finish(
        [w_in[0].T, w_a_proj[0], w_b_proj[0], w_out[0]], [m_w_in[0].T, m_w_a_proj[0], m_w_b_proj[0], m_w_out[0]],
        [v_w_in[0].T, v_w_a_proj[0], v_w_b_proj[0], v_w_out[0]],
        ws, ms, vs, b_gate_up[0], m_b_gate_up[0], v_b_gate_up[0], small, sums, from_chips)

    def outputs(k):
        return [sm["norm_w"][k], t_rows[k].T[None], sm["sinks"][k], t_gu[k][None], sm["bias"][k], sm["bn"][k],
                t_rows[4 + k][None], t_rows[8 + k][None], t_rows[12 + k][None], sm["fnw"][k].reshape(D)]

    return (loss[0, 0], grad_x[None], *outputs(0), *outputs(1), *outputs(2), *outputs(3))
```

```python
import functools

import numpy as np
import jax
import jax.numpy as jnp
from jax import lax
from jax.experimental import pallas as pl
from jax.experimental.pallas import tpu as pltpu

F32 = jnp.float32
MXU = jnp.bfloat16
WIRE = jnp.bfloat16

D = 1024
A_HEADS, A_KV, A_HD = 16, 2, 64
BLK = 128
B_HEADS, B_DK, B_DV = 4, 128, 256
RANK, TAU, CHUNK = 16, 16.0, 64
EPS, NEG = 1e-5, -1e30
ROPE_THETA = 10000.0
IN_WIDTH, NDEV = 7440, 8
SHARD = IN_WIDTH // NDEV
LANE = 128

C_Q, C_KD, C_VD, C_BL = 0, 1024, 1280, 1536
C_BV, C_BQ, C_BK = 2048, 3072, 3584
C_AG, C_BG, C_MA, C_MB = 4096, 5120, 6144, 7168
C_GLA, W_GLA, C_GATES, W_GATES = 2048, 2048, 4096, 4096
NF = 8192
W_BL = 128

SHARD_PAD = 944
R_IN, R_A, R_B, R_O, R_GU, ROWS = 0, 944, 1072, 1200, 1328, 1344
SMALL_ROWS = 48

ADAM_LR, ADAM_B1, ADAM_B2, ADAM_EPS, ADAM_WD, ADAM_STEP = 0.001, 0.9, 0.999, 1e-08, 0.01, 10

MESH = pl.DeviceIdType.MESH
VMEM_LIMIT = 56 * 1024 * 1024


def _cp(sem=None, **kw):
    if sem is not None:
        kw["dimension_semantics"] = sem
    return pltpu.CompilerParams(vmem_limit_bytes=VMEM_LIMIT, **kw)


def _dot(a, b):
    return jnp.dot(a, b, preferred_element_type=F32)


def _dot_nt(a, b):
    return lax.dot_general(a, b, (((1,), (1,)), ((), ())), preferred_element_type=F32)


def _dot_tn(a, b):
    return lax.dot_general(a, b, (((0,), (0,)), ((), ())), preferred_element_type=F32)


def _dot_f32(a, b):
    return jnp.dot(a, b, preferred_element_type=F32, precision=lax.Precision.HIGHEST)


def _sigmoid(z):
    return 0.5 * jnp.tanh(0.5 * z) + 0.5


def _rope(xp, cos, sin):
    return xp * cos + pltpu.roll(xp, 64, 1) * sin


def _rope_bwd(dy, cos, sin):
    return dy * cos - pltpu.roll(dy, 64, 1) * sin


def _vmem():
    return pl.BlockSpec(memory_space=pltpu.VMEM)


def _any():
    return pl.BlockSpec(memory_space=pl.ANY)


def _rope_rows():
    half = A_HD // 2
    inv = (np.float32(ROPE_THETA) ** (-np.arange(half, dtype=np.float32) / np.float32(half))).astype(np.float32)
    inv_row = jnp.asarray(np.tile(inv, 4)[None, :])
    sign_row = jnp.asarray(np.concatenate([-np.ones(64, np.float32), np.ones(64, np.float32)])[None, :])
    return inv_row, sign_row


def _prologue_rows(rows, x_ref, nw_ref, pos_ref, inv_ref, sign_ref, h_ref, cos_ref, sin_ref):
    xv = x_ref[rows, :]
    r = lax.rsqrt(jnp.mean(xv * xv, axis=-1, keepdims=True) + EPS)
    h_ref[rows, :] = ((xv * r) * nw_ref[...]).astype(h_ref.dtype)
    ang = pos_ref[rows, :].astype(F32) * inv_ref[...]
    cos_ref[rows, :] = jnp.cos(ang)
    sin_ref[rows, :] = jnp.sin(ang) * sign_ref[...]


def _proj(h, wft):
    T = h.shape[0]
    tT, tN = T, 512

    def body(h_ref, w_ref, o_ref):
        o_ref[...] = _dot_nt(h_ref[...], w_ref[...])

    return pl.pallas_call(
        body, name="proj", grid=(T // tT, NF // tN),
        in_specs=[pl.BlockSpec((tT, D), lambda i, j: (i, 0)), pl.BlockSpec((tN, D), lambda i, j: (j, 0))],
        out_specs=pl.BlockSpec((tT, tN), lambda i, j: (i, j)),
        out_shape=jax.ShapeDtypeStruct((T, NF), F32),
        compiler_params=_cp(("parallel", "parallel")),
    )(h, wft)


def _swa_masks():
    lane = lax.broadcasted_iota(jnp.int32, (BLK, LANE), 1)
    rope_sub0 = ((lane // 32) % 2) == 0
    std_sub0 = lane < 64
    return lane, rope_sub0, std_sub0


def _swa_tri():
    qi = lax.broadcasted_iota(jnp.int32, (BLK, BLK), 0)
    kj = lax.broadcasted_iota(jnp.int32, (BLK, BLK), 1)
    return kj <= qi


def _swa_fold(full, tri):
    return jnp.where(tri, full[:, BLK:], full[:, :BLK])


def _swa_unfold(sq, tri):
    return jnp.concatenate([jnp.where(tri, 0.0, sq), jnp.where(tri, sq, 0.0)], axis=1)


def _swa_keys(kc_ref, kp_ref, vc_ref, vp_ref, cq, sq, cp, sp):
    def ropek(kref, c, s):
        kv = kref[...]
        return jnp.concatenate([_rope(kv[:, :LANE], c, s), _rope(kv[:, LANE:], c, s)], axis=1)

    K = jnp.concatenate([ropek(kp_ref, cp, sp), ropek(kc_ref, cq, sq)], axis=0).astype(MXU)
    V = jnp.concatenate([vp_ref[...], vc_ref[...]], axis=0).astype(MXU)
    return K, V


def _swa_in_specs(nb, last):
    def cur(n):
        return jnp.minimum(n, last)

    def prev(n):
        return jnp.maximum(cur(n) - 1, 0)

    kd, vd = C_KD // 256, C_VD // 256
    return [
        pl.BlockSpec((BLK, D), lambda n: (cur(n), C_Q // D)),
        pl.BlockSpec((BLK, 256), lambda n: (cur(n), kd)),
        pl.BlockSpec((BLK, 256), lambda n: (prev(n), kd)),
        pl.BlockSpec((BLK, 256), lambda n: (cur(n), vd)),
        pl.BlockSpec((BLK, 256), lambda n: (prev(n), vd)),
        pl.BlockSpec((BLK, LANE), lambda n: (cur(n), 0)),
        pl.BlockSpec((BLK, LANE), lambda n: (cur(n), 0)),
        pl.BlockSpec((BLK, LANE), lambda n: (prev(n), 0)),
        pl.BlockSpec((BLK, LANE), lambda n: (prev(n), 0)),
    ]


def _swa_fwd(proj, cos, sin, sinks):
    T = proj.shape[0]
    nb = T // BLK
    scale = A_HD ** -0.5

    def body(sinks_ref, q_ref, kc_ref, kp_ref, vc_ref, vp_ref, cq_ref, sq_ref, cp_ref, sp_ref, o_ref, l_ref):
        n = pl.program_id(0)
        cq, sq = cq_ref[...], sq_ref[...]
        K, V = _swa_keys(kc_ref, kp_ref, vc_ref, vp_ref, cq, sq, cp_ref[...], sp_ref[...])
        tri = _swa_tri()
        valid = tri | (n > 0)
        lane, rope_sub0, std_sub0 = _swa_masks()
        group = A_HEADS // A_KV
        roped, lses = {}, []

        def products(head):
            pb, sub, g = head // 2, head % 2, head // group
            if sub == 0:
                roped[pb] = _rope(q_ref[:, pb * LANE:(pb + 1) * LANE], cq, sq)
            qm = jnp.where(rope_sub0 if sub == 0 else ~rope_sub0, roped[pb], 0.0).astype(MXU)
            return _dot_nt(qm, K[:, g * LANE:(g + 1) * LANE])

        def softmax(head, s_full):
            s = jnp.where(valid, _swa_fold(s_full, tri) * scale, NEG)
            sink = sinks_ref[0, head]
            m = jnp.maximum(jnp.max(s, axis=1, keepdims=True), sink)
            e = jnp.exp(s - m)
            den = jnp.sum(e, axis=1, keepdims=True) + jnp.exp(sink - m)
            lses.append(m + jnp.log(den))
            return _swa_unfold(e / den, tri).astype(MXU)

        outs = {}
        st1 = {0: products(0), 1: products(1)}
        st2 = {0: softmax(0, st1.pop(0))}
        for head in range(A_HEADS):
            if head + 2 < A_HEADS:
                st1[head + 2] = products(head + 2)
            if head + 1 < A_HEADS:
                st2[head + 1] = softmax(head + 1, st1.pop(head + 1))
            g = head // group
            outs[head] = _dot(st2.pop(head), V[:, g * LANE:(g + 1) * LANE])
            if head % 2 == 1:
                pb = head // 2
                o_ref[:, pb * LANE:(pb + 1) * LANE] = jnp.where(std_sub0, outs[head - 1], outs[head])
        lacc = jnp.zeros((BLK, LANE), F32)
        for head in range(A_HEADS):
            lacc = jnp.where(lane == head, lses[head], lacc)
        l_ref[...] = lacc

    return pl.pallas_call(
        body, name="swa_fwd", grid=(nb,),
        in_specs=[pl.BlockSpec(memory_space=pltpu.SMEM)] + _swa_in_specs(nb, nb - 1),
        out_specs=[pl.BlockSpec((BLK, D), lambda n: (n, 0)), pl.BlockSpec((BLK, LANE), lambda n: (n, 0))],
        out_shape=[jax.ShapeDtypeStruct((T, D), F32), jax.ShapeDtypeStruct((T, LANE), F32)],
        compiler_params=_cp(("parallel",)),
    )(sinks, proj, proj, proj, proj, proj, cos, sin, cos, sin)


def _swa_bwd(proj, cos, sin, sinks, do_a, o_a, lse):
    T = proj.shape[0]
    nb = T // BLK
    scale = A_HD ** -0.5

    def body(sinks_ref, q_ref, kc_ref, kp_ref, vc_ref, vp_ref, cq_ref, sq_ref, cp_ref, sp_ref,
             do_ref, o_ref, l_ref, dq_ref, dkv_ref, ds_ref, ckv_ref):
        n = pl.program_id(0)

        @pl.when(n == 0)
        def _():
            ckv_ref[...] = jnp.zeros_like(ckv_ref)
            ds_ref[...] = jnp.zeros_like(ds_ref)

        @pl.when(n < nb)
        def _():
            cq, sq, cp, sp = cq_ref[...], sq_ref[...], cp_ref[...], sp_ref[...]
            K, V = _swa_keys(kc_ref, kp_ref, vc_ref, vp_ref, cq, sq, cp, sp)
            tri = _swa_tri()
            valid = tri | (n > 0)
            lane, rope_sub0, std_sub0 = _swa_masks()
            lane_row = lax.broadcasted_iota(jnp.int32, (1, LANE), 1)
            lse_v = l_ref[...]
            dKt = [jnp.zeros((LANE, 2 * BLK), F32) for _ in range(A_KV)]
            dVt = [jnp.zeros((LANE, 2 * BLK), F32) for _ in range(A_KV)]
            dsinks, roped, roped_t, do_t = [], {}, {}, {}
            group = A_HEADS // A_KV
            dim = lax.broadcasted_iota(jnp.int32, (LANE, BLK), 0)
            rope_row0, std_row0 = ((dim // 32) % 2) == 0, dim < 64

            def products(head):
                pb, sub, g = head // 2, head % 2, head // group
                cols = slice(pb * LANE, (pb + 1) * LANE)
                Kg, Vg = K[:, g * LANE:(g + 1) * LANE], V[:, g * LANE:(g + 1) * LANE]
                if sub == 0:
                    roped[pb] = _rope(q_ref[:, cols], cq, sq)
                    roped_t[pb] = roped[pb].T
                    do_t[pb] = do_ref[:, cols].T
                qm = jnp.where(rope_sub0 if sub == 0 else ~rope_sub0, roped[pb], 0.0).astype(MXU)
                qmt = jnp.where(rope_row0 if sub == 0 else ~rope_row0, roped_t[pb], 0.0).astype(MXU)
                dov = jnp.where(std_sub0 if sub == 0 else ~std_sub0, do_ref[:, cols], 0.0)
                dovt = jnp.where(std_row0 if sub == 0 else ~std_row0, do_t[pb], 0.0).astype(MXU)
                delta = jnp.sum(dov * o_ref[:, cols], axis=1, keepdims=True)
                return qmt, dovt, delta, _dot_nt(qm, Kg), _dot_nt(dov.astype(MXU), Vg)

            def scores(head, qmt, dovt, delta, s_full, dp_full):
                lh = jnp.sum(jnp.where(lane == head, lse_v, 0.0), axis=1, keepdims=True)
                p = jnp.where(valid, jnp.exp(_swa_fold(s_full, tri) * scale - lh), 0.0)
                psink = jnp.exp(sinks_ref[0, head] - lh)
                dsinks.append(jnp.sum(-psink * delta, axis=0, keepdims=True))
                dsq = (p * (_swa_fold(dp_full, tri) - delta)) * scale
                return qmt, dovt, _swa_unfold(p, tri).astype(MXU), _swa_unfold(dsq, tri).astype(MXU)

            def grads(head, qmt, dovt, pb16, dsc):
                g = head // group
                dKt[g] = dKt[g] + _dot(qmt, dsc)
                dVt[g] = dVt[g] + _dot(dovt, pb16)
                return _dot(dsc, K[:, g * LANE:(g + 1) * LANE])

            dqs = {}
            st1 = {0: products(0), 1: products(1)}
            st2 = {0: scores(0, *st1.pop(0))}
            for head in range(A_HEADS):
                if head + 2 < A_HEADS:
                    st1[head + 2] = products(head + 2)
                if head + 1 < A_HEADS:
                    st2[head + 1] = scores(head + 1, *st1.pop(head + 1))
                dqs[head] = grads(head, *st2.pop(head))
                if head % 2 == 1:
                    pb = head // 2
                    dqp = jnp.where(rope_sub0, dqs[head - 1], dqs[head])
                    dq_ref[:, pb * LANE:(pb + 1) * LANE] = _rope_bwd(dqp, cq, sq).astype(dq_ref.dtype)
            dsink = jnp.zeros((1, LANE), F32)
            for head in range(A_HEADS):
                dsink = jnp.where(lane_row == head, dsinks[head], dsink)
            dK, dV = [a.T for a in dKt], [a.T for a in dVt]
            prev = ([_rope_bwd(dK[g][:BLK], cp, sp) for g in range(A_KV)] + [dV[g][:BLK] for g in range(A_KV)])
            cur_ = ([_rope_bwd(dK[g][BLK:], cq, sq) for g in range(A_KV)] + [dV[g][BLK:] for g in range(A_KV)])
            dkv_ref[...] = (ckv_ref[...] + jnp.concatenate(prev, axis=1)).astype(dkv_ref.dtype)
            ckv_ref[...] = jnp.concatenate(cur_, axis=1)
            ds_ref[...] = ds_ref[...] + jnp.broadcast_to(dsink, ds_ref.shape)

        @pl.when(n == nb)
        def _():
            dkv_ref[...] = ckv_ref[...].astype(dkv_ref.dtype)

    last = nb - 1

    def cur(n):
        return jnp.minimum(n, last)

    def out_kv(n):
        return (jnp.maximum(n - 1, 0), 0)

    return pl.pallas_call(
        body, name="swa_bwd", grid=(nb + 1,),
        in_specs=[pl.BlockSpec(memory_space=pltpu.SMEM)] + _swa_in_specs(nb, last) + [
            pl.BlockSpec((BLK, D), lambda n: (cur(n), 0)),
            pl.BlockSpec((BLK, D), lambda n: (cur(n), 0)),
            pl.BlockSpec((BLK, LANE), lambda n: (cur(n), 0)),
        ],
        out_specs=[
            pl.BlockSpec((BLK, D), lambda n: (cur(n), 0)),
            pl.BlockSpec((BLK, 512), out_kv),
            pl.BlockSpec((8, LANE), lambda n: (0, 0)),
        ],
        out_shape=[
            jax.ShapeDtypeStruct((T, D), MXU),
            jax.ShapeDtypeStruct((T, 512), MXU),
            jax.ShapeDtypeStruct((8, LANE), F32),
        ],
        scratch_shapes=[pltpu.VMEM((BLK, 512), F32)],
        compiler_params=_cp(("arbitrary",)),
    )(sinks, proj, proj, proj, proj, proj, cos, sin, cos, sin, do_a, o_a, lse)


def _gla_gate(bl_ref, gu_ref, bias_ref):
    gk = _dot(bl_ref[...].astype(MXU), gu_ref[...]) + bias_ref[...]
    la = (jnp.minimum(gk, 0.0) - jnp.log(1.0 + jnp.exp(-jnp.abs(gk)))) / TAU
    ri = lax.broadcasted_iota(jnp.int32, (CHUNK, CHUNK), 0)
    ci = lax.broadcasted_iota(jnp.int32, (CHUNK, CHUNK), 1)
    b = _dot_f32(jnp.where(ci <= ri, 1.0, 0.0).astype(F32), la)
    return gk, la, b, ri, ci


def _gla_head(q_ref, k_ref, la, b, h):
    sl = slice(h * B_DK, (h + 1) * B_DK)
    bh = b[:, sl]
    blast = jnp.sum(la[:, sl], axis=0, keepdims=True)
    qc = q_ref[:, sl] * (B_DK ** -0.5)
    kh = k_ref[:, sl]
    eb, enb, esb = jnp.exp(bh), jnp.exp(-bh), jnp.exp(blast - bh)
    return qc * eb, kh * enb, kh * esb, eb, enb, esb, jnp.exp(blast)


def _gla_specs(chunk_of):
    return [
        pl.BlockSpec((CHUNK, 512), lambda i: (chunk_of(i), C_BQ // 512)),
        pl.BlockSpec((CHUNK, 512), lambda i: (chunk_of(i), C_BK // 512)),
        pl.BlockSpec((CHUNK, D), lambda i: (chunk_of(i), C_BV // D)),
        pl.BlockSpec((CHUNK, W_BL), lambda i: (chunk_of(i), C_BL // W_BL)),
        pl.BlockSpec((W_BL, 512), lambda i: (0, 0)),
        pl.BlockSpec((1, 512), lambda i: (0, 0)),
    ]


def _gla_fwd(proj, gu_pad, bias):
    T = proj.shape[0]
    nc = T // CHUNK

    def body(q_ref, k_ref, v_ref, bl_ref, gu_ref, bias_ref, o_ref, st_ref, state_ref):
        @pl.when(pl.program_id(0) == 0)
        def _():
            state_ref[...] = jnp.zeros_like(state_ref)

        _, la, b, ri, ci = _gla_gate(bl_ref, gu_ref, bias_ref)
        st_ref[...] = state_ref[...]
        for h in range(B_HEADS):
            q_e, k_e, k_s, _, _, _, decay = _gla_head(q_ref, k_ref, la, b, h)
            vh = v_ref[:, h * B_DV:(h + 1) * B_DV].astype(MXU)
            rows = slice(h * B_DV, (h + 1) * B_DV)
            q_eb = q_e.astype(MXU)
            att = jnp.where(ci <= ri, _dot_nt(q_eb, k_e.astype(MXU)), 0.0)
            st = state_ref[rows, :]
            o_ref[:, rows] = _dot(att.astype(MXU), vh) + _dot_nt(q_eb, st.astype(MXU))
            state_ref[rows, :] = st * decay + _dot_tn(vh, k_s.astype(MXU))

    return pl.pallas_call(
        body, name="gla_fwd", grid=(nc,),
        in_specs=_gla_specs(lambda i: i),
        out_specs=[pl.BlockSpec((CHUNK, D), lambda i: (i, 0)),
                   pl.BlockSpec((B_HEADS * B_DV, B_DK), lambda i: (i, 0))],
        out_shape=[jax.ShapeDtypeStruct((T, D), F32),
                   jax.ShapeDtypeStruct((nc * B_HEADS * B_DV, B_DK), F32)],
        scratch_shapes=[pltpu.VMEM((B_HEADS * B_DV, B_DK), F32)],
        compiler_params=_cp(("arbitrary",)),
    )(proj, proj, proj, proj, gu_pad, bias)


def _gla_bwd(proj, gu_pad, bias, states, do_b):
    T = proj.shape[0]
    nc = T // CHUNK
    o_q, o_k = C_BQ - C_GLA, C_BK - C_GLA

    def body(q_ref, k_ref, v_ref, bl_ref, gu_ref, bias_ref, st_ref, do_ref,
             dg_ref, dbl_ref, ggu_ref, gbias_ref, gt_ref):
        @pl.when(pl.program_id(0) == 0)
        def _():
            gt_ref[...] = jnp.zeros_like(gt_ref)
            ggu_ref[...] = jnp.zeros_like(ggu_ref)
            gbias_ref[...] = jnp.zeros_like(gbias_ref)

        gk, la, b, ri, ci = _gla_gate(bl_ref, gu_ref, bias_ref)
        causal = ci <= ri
        upper = jnp.where(ci >= ri, 1.0, 0.0).astype(F32)
        dla_parts = []
        for h in range(B_HEADS):
            q_e, k_e, k_s, eb, enb, esb, decay = _gla_head(q_ref, k_ref, la, b, h)
            rows = slice(h * B_DV, (h + 1) * B_DV)
            sl = slice(h * B_DK, (h + 1) * B_DK)
            vh = v_ref[:, rows].astype(MXU)
            doh = do_ref[:, rows].astype(MXU)
            q_eb, k_eb, k_sb = q_e.astype(MXU), k_e.astype(MXU), k_s.astype(MXU)
            st = st_ref[rows, :]
            gt = gt_ref[rows, :]
            gtb = gt.astype(MXU)
            att = jnp.where(causal, _dot_nt(q_eb, k_eb), 0.0).astype(MXU)
            datt = jnp.where(causal, _dot_nt(doh, vh), 0.0).astype(MXU)
            dq_e = _dot(datt, k_eb) + _dot(doh, st.astype(MXU))
            dk_e = _dot_tn(datt, q_eb)
            dk_s = _dot(vh, gtb)
            dg_ref[:, rows] = (_dot_tn(att, doh) + _dot_nt(k_sb, gtb)).astype(dg_ref.dtype)
            ddecay = jnp.sum(gt * st, axis=0, keepdims=True)
            gt_ref[rows, :] = gt * decay + _dot_tn(doh, q_eb)
            dg_ref[:, o_q + h * B_DK:o_q + (h + 1) * B_DK] = (dq_e * eb * (B_DK ** -0.5)).astype(dg_ref.dtype)
            dg_ref[:, o_k + h * B_DK:o_k + (h + 1) * B_DK] = (dk_e * enb + dk_s * esb).astype(dg_ref.dtype)
            dks_ks = dk_s * k_s
            db = dq_e * q_e - dk_e * k_e - dks_ks
            dblast = jnp.sum(dks_ks, axis=0, keepdims=True) + ddecay * decay
            dla_parts.append(_dot_f32(upper, db) + dblast)
        dla = jnp.concatenate(dla_parts, axis=1)
        dgk = dla * (1.0 / TAU) * _sigmoid(-gk)
        dgkb = dgk.astype(MXU)
        dbl_ref[...] = _dot_nt(dgkb, gu_ref[...]).astype(dbl_ref.dtype)
        ggu_ref[...] = ggu_ref[...] + _dot_tn(bl_ref[...].astype(MXU), dgkb)
        gbias_ref[...] = gbias_ref[...] + jnp.broadcast_to(jnp.sum(dgk, axis=0, keepdims=True), gbias_ref.shape)

    def rev(i):
        return nc - 1 - i

    return pl.pallas_call(
        body, name="gla_bwd", grid=(nc,),
        in_specs=_gla_specs(rev) + [
            pl.BlockSpec((B_HEADS * B_DV, B_DK), lambda i: (rev(i), 0)),
            pl.BlockSpec((CHUNK, D), lambda i: (rev(i), 0)),
        ],
        out_specs=[
            pl.BlockSpec((CHUNK, W_GLA), lambda i: (rev(i), 0)),
            pl.BlockSpec((CHUNK, W_BL), lambda i: (rev(i), 0)),
            pl.BlockSpec((W_BL, 512), lambda i: (0, 0)),
            pl.BlockSpec((8, 512), lambda i: (0, 0)),
        ],
        out_shape=[
            jax.ShapeDtypeStruct((T, W_GLA), MXU),
            jax.ShapeDtypeStruct((T, W_BL), MXU),
            jax.ShapeDtypeStruct((W_BL, 512), F32),
            jax.ShapeDtypeStruct((8, 512), F32),
        ],
        scratch_shapes=[pltpu.VMEM((B_HEADS * B_DV, B_DK), F32)],
        compiler_params=_cp(("arbitrary",)),
    )(proj, proj, proj, proj, gu_pad, bias, states, do_b)


def _mid(x, target, proj, o_a, o_b, w_a, w_b, w_out, w_bn4, fnw):
    T = x.shape[0]
    tT = min(T, 128)
    nbuf = 4
    o_ag, o_bg, o_ma, o_mb = (c - C_GATES for c in (C_AG, C_BG, C_MA, C_MB))

    def body(x_ref, t_ref, oa_ref, ob_ref, gates_ref, wa_ref, wb_ref, wo_ref, wbn_ref, fnw_ref,
             dx2_ref, doa_ref, dob_ref, dgates_ref,
             gwa_ref, gwb_ref, gwo_ref, gfn_ref, gbn_ref, loss_ref, buf_ref):
        i = pl.program_id(0)

        @pl.when(i == 0)
        def _():
            for r in (gwa_ref, gwb_ref, gwo_ref, gfn_ref, gbn_ref, loss_ref):
                r[...] = jnp.zeros_like(r)

        rows = pl.ds(pl.multiple_of((i % nbuf) * tT, tT), tT)

        def keep(k, val):
            buf_ref[k, rows, :] = val

        oa, ag = oa_ref[...], gates_ref[:, o_ag:o_ag + D]
        sg_a = _sigmoid(ag)
        silu_a = ag * sg_a
        oag_b = (oa * silu_a).astype(MXU)
        keep(0, oag_b)
        y_a = _dot(oag_b, wa_ref[...])

        ob, bg = ob_ref[...], gates_ref[:, o_bg:o_bg + D]
        rbs, obhats = [], []
        for h in range(B_HEADS):
            obh = ob[:, h * B_DV:(h + 1) * B_DV]
            rb = lax.rsqrt(jnp.mean(obh * obh, axis=-1, keepdims=True) + EPS)
            rbs.append(rb)
            obhats.append(obh * rb)
        obhat = jnp.concatenate(obhats, axis=1)
        wbn = wbn_ref[...]
        obn = obhat * wbn
        sg_b = _sigmoid(bg)
        silu_b = bg * sg_b
        obg_b = (obn * silu_b).astype(MXU)
        keep(1, obg_b)
        y_b = _dot(obg_b, wb_ref[...])

        sa, sb = _sigmoid(gates_ref[:, o_ma:o_ma + D]), _sigmoid(gates_ref[:, o_mb:o_mb + D])
        mg_b = (sa * y_a + sb * y_b).astype(MXU)
        keep(2, mg_b)
        x2 = x_ref[...] + _dot(mg_b, wo_ref[...])
        r2 = lax.rsqrt(jnp.mean(x2 * x2, axis=-1, keepdims=True) + EPS)
        xh2 = x2 * r2
        fw = fnw_ref[...]
        err = xh2 * fw - t_ref[...]
        tok = jnp.mean(err * err, axis=-1, keepdims=True)
        loss_ref[...] = loss_ref[...] + 0.5 * jnp.sum(tok, axis=0, keepdims=True)

        dy = err * (1.0 / D)
        gfn_ref[...] = gfn_ref[...] + jnp.broadcast_to(jnp.sum(dy * xh2, axis=0, keepdims=True), gfn_ref.shape)
        gy = dy * fw
        dx2 = r2 * (gy - xh2 * jnp.mean(gy * xh2, axis=-1, keepdims=True))
        dx2_ref[...] = dx2
        dx2_b = dx2.astype(MXU)
        keep(5, dx2_b)
        dmg = _dot_nt(dx2_b, wo_ref[...])

        dgates_ref[:, o_ma:o_ma + D] = (dmg * y_a * sa * (1.0 - sa)).astype(dgates_ref.dtype)
        dgates_ref[:, o_mb:o_mb + D] = (dmg * y_b * sb * (1.0 - sb)).astype(dgates_ref.dtype)
        dya_b = (dmg * sa).astype(MXU)
        dyb_b = (dmg * sb).astype(MXU)
        keep(3, dya_b)
        keep(4, dyb_b)
        doag = _dot_nt(dya_b, wa_ref[...])
        dobg = _dot_nt(dyb_b, wb_ref[...])

        @pl.when(i % nbuf == nbuf - 1)
        def _():
            gwa_ref[...] = gwa_ref[...] + _dot_tn(buf_ref[0], buf_ref[3])
            gwb_ref[...] = gwb_ref[...] + _dot_tn(buf_ref[1], buf_ref[4])
            gwo_ref[...] = gwo_ref[...] + _dot_tn(buf_ref[2], buf_ref[5])

        doa_ref[...] = doag * silu_a
        dgates_ref[:, o_ag:o_ag + D] = (doag * oa * (sg_a * (1.0 + ag * (1.0 - sg_a)))).astype(dgates_ref.dtype)
        dobn = dobg * silu_b
        dgates_ref[:, o_bg:o_bg + D] = (dobg * obn * (sg_b * (1.0 + bg * (1.0 - sg_b)))).astype(dgates_ref.dtype)
        gg = dobn * wbn
        gbn = jnp.zeros((1, B_DV), F32)
        for h in range(B_HEADS):
            sl = slice(h * B_DV, (h + 1) * B_DV)
            gbn = gbn + jnp.sum(dobn[:, sl] * obhats[h], axis=0, keepdims=True)
            ggh = gg[:, sl]
            dob_ref[:, sl] = rbs[h] * (ggh - obhats[h] * jnp.mean(ggh * obhats[h], axis=-1, keepdims=True))
        gbn_ref[...] = gbn_ref[...] + jnp.broadcast_to(gbn, gbn_ref.shape)

    assert (T // tT) % nbuf == 0
    tile = pl.BlockSpec((tT, D), lambda i: (i, 0))
    row = pl.BlockSpec((1, D), lambda i: (0, 0))
    acc8 = pl.BlockSpec((8, D), lambda i: (0, 0))
    return pl.pallas_call(
        body, name="mid", grid=(T // tT,),
        in_specs=[tile, tile, tile, tile, pl.BlockSpec((tT, W_GATES), lambda i: (i, C_GATES // W_GATES)),
                  _vmem(), _vmem(), _vmem(), row, row],
        out_specs=[tile, tile, tile, pl.BlockSpec((tT, W_GATES), lambda i: (i, 0)), _vmem(), _vmem(), _vmem(),
                   acc8, pl.BlockSpec((8, B_DV), lambda i: (0, 0)), pl.BlockSpec((8, LANE), lambda i: (0, 0))],
        out_shape=[
            jax.ShapeDtypeStruct((T, D), F32),
            jax.ShapeDtypeStruct((T, D), F32),
            jax.ShapeDtypeStruct((T, D), F32),
            jax.ShapeDtypeStruct((T, W_GATES), MXU),
            jax.ShapeDtypeStruct((D, D), F32),
            jax.ShapeDtypeStruct((D, D), F32),
            jax.ShapeDtypeStruct((D, D), F32),
            jax.ShapeDtypeStruct((8, D), F32),
            jax.ShapeDtypeStruct((8, B_DV), F32),
            jax.ShapeDtypeStruct((8, LANE), F32),
        ],
        scratch_shapes=[pltpu.VMEM((6, nbuf * tT, D), MXU)],
        compiler_params=_cp(("arbitrary",)),
    )(x, target, o_a, o_b, proj, w_a, w_b, w_out, w_bn4, fnw)


DH = D // 2


def _gw_half(h, pieces, half, after=None):
    T = h.shape[0]
    steps = NF // 512
    tiles = ((0, 2), (2, 3), (4, 8), (8, 16))

    def body(*refs):
        h_ref, q_ref, kv_ref, bl_ref, gla_ref, gates_ref = refs[:6]
        o_ref = refs[-1]
        j = pl.program_id(0)

        for (lo, hi), ref in zip(tiles, (q_ref, kv_ref, gla_ref, gates_ref)):
            @pl.when((j >= lo) & (j < hi))
            def _(ref=ref):
                o_ref[...] = _dot_tn(ref[...], h_ref[...])

        @pl.when(j == 3)
        def _():
            o_ref[0:W_BL, :] = _dot_tn(bl_ref[...], h_ref[...])
            o_ref[W_BL:, :] = jnp.zeros((512 - W_BL, DH), F32)

    def tile_of(lo, hi):
        return lambda j: (0, jnp.clip(j - lo, 0, hi - lo - 1))

    in_specs = [pl.BlockSpec((T, DH), lambda j: (0, half)),
                pl.BlockSpec((T, 512), tile_of(0, 2)), pl.BlockSpec((T, 512), lambda j: (0, 0)),
                pl.BlockSpec((T, W_BL), lambda j: (0, 0)),
                pl.BlockSpec((T, 512), tile_of(4, 8)), pl.BlockSpec((T, 512), tile_of(8, 16))]
    args = [h, *pieces]
    if after is not None:
        in_specs.append(_any())
        args.append(after)
    return pl.pallas_call(
        body, name=f"gw_in_half{half}", grid=(steps,),
        in_specs=in_specs, out_specs=pl.BlockSpec((512, DH), lambda j: (j, 0)),
        out_shape=jax.ShapeDtypeStruct((NF, DH), F32),
        compiler_params=_cp(("parallel",)),
    )(*args)


def _chip_copies(s_ref, got_ref, send_sems, recv_sems):
    x, y, c = _place()
    chips = [(1 - x, y), (x, 1 - y), (1 - x, 1 - y)]
    return [pltpu.make_async_remote_copy(
        src_ref=s_ref.at[2 * px + py], dst_ref=got_ref.at[j],
        send_sem=send_sems.at[j], recv_sem=recv_sems.at[j], device_id=(px, py, c), device_id_type=MESH)
        for j, (px, py) in enumerate(chips)]


_EFFECT = pltpu.SideEffectType.DATAFLOW_SIDE_EFFECTING


def _hbm():
    return pl.BlockSpec(memory_space=pltpu.HBM)


def _sem():
    return pl.BlockSpec(memory_space=pltpu.SEMAPHORE)


def _chip_start(sums, half):
    land = pltpu.with_memory_space_constraint(lax.empty((3,) + sums.shape[1:], sums.dtype), pltpu.HBM)

    def body(s_ref, land_ref, send_sems, recv_sems, s_thru, land_thru, token):
        for cp in _chip_copies(s_ref, land_ref, send_sems, recv_sems):
            cp.start()
        token[...] = jnp.zeros_like(token)

    return pl.pallas_call(
        body, name=f"chip_start{half}",
        out_shape=(pltpu.SemaphoreType.DMA((3,)), pltpu.SemaphoreType.DMA((3,)),
                   pltpu.HBM(sums.shape, sums.dtype), pltpu.HBM(land.shape, land.dtype),
                   jax.ShapeDtypeStruct((8, LANE), F32)),
        in_specs=(_hbm(), _hbm()), out_specs=(_sem(), _sem(), _hbm(), _hbm(), _vmem()),
        input_output_aliases={0: 2, 1: 3},
        compiler_params=pltpu.CompilerParams(has_side_effects=_EFFECT),
    )(pltpu.with_memory_space_constraint(sums, pltpu.HBM), land)


def _chip_wait(send_sems, recv_sems, s_thru, land_thru, after, half):
    def body(s_ref, land_ref, send_sems, recv_sems, after_ref, s_out, got_ref):
        copies = _chip_copies(s_ref, land_ref, send_sems, recv_sems)
        for cp in copies:
            cp.wait_send()
        for cp in copies:
            cp.wait_recv()

    return pl.pallas_call(
        body, name=f"chip_wait{half}",
        out_shape=(pltpu.HBM(s_thru.shape, s_thru.dtype), pltpu.HBM(land_thru.shape, land_thru.dtype)),
        in_specs=(_hbm(), _hbm(), _sem(), _sem(), _any()), out_specs=(_hbm(), _hbm()),
        input_output_aliases={0: 0, 1: 1},
        compiler_params=pltpu.CompilerParams(has_side_effects=_EFFECT),
    )(s_thru, land_thru, send_sems, recv_sems, after)


def _dh_norm(pieces, offsets, wf, x, dx2, norm_w, after):
    T = x.shape[0]
    tT = min(T, 256)
    widths = [p.shape[1] for p in pieces]
    npc = len(pieces)

    def body(*refs):
        dp_refs = refs[:npc]
        wf_ref, x_ref, dx2_ref, nw_ref, _, gx_ref, gnw_ref = refs[npc:]

        @pl.when(pl.program_id(0) == 0)
        def _():
            gnw_ref[...] = jnp.zeros_like(gnw_ref)

        dh = jnp.zeros((tT, D), F32)
        for dp_ref, off, w in zip(dp_refs, offsets, widths):
            dh = dh + _dot(dp_ref[...], wf_ref[off:off + w, :])
        xv = x_ref[...]
        r = lax.rsqrt(jnp.mean(xv * xv, axis=-1, keepdims=True) + EPS)
        xh = xv * r
        gnw_ref[...] = gnw_ref[...] + jnp.broadcast_to(jnp.sum(dh * xh, axis=0, keepdims=True), gnw_ref.shape)
        g = dh * nw_ref[...]
        gx_ref[...] = r * (g - xh * jnp.mean(g * xh, axis=-1, keepdims=True)) + dx2_ref[...]

    tile = pl.BlockSpec((tT, D), lambda i: (i, 0))
    return pl.pallas_call(
        body, name="dh_norm", grid=(T // tT,),
        in_specs=[pl.BlockSpec((tT, w), lambda i: (i, 0)) for w in widths]
        + [_vmem(), tile, tile, pl.BlockSpec((1, D), lambda i: (0, 0)), _any()],
        out_specs=[tile, pl.BlockSpec((8, D), lambda i: (0, 0))],
        out_shape=[jax.ShapeDtypeStruct((T, D), F32), jax.ShapeDtypeStruct((8, D), F32)],
        compiler_params=_cp(("arbitrary",)),
    )(*pieces, wf, x, dx2, norm_w, after)


def _adamw_math(w, g, m, v):
    m = ADAM_B1 * m + (1.0 - ADAM_B1) * g
    v = ADAM_B2 * v + (1.0 - ADAM_B2) * (g * g)
    m_hat = m / (1.0 - ADAM_B1 ** ADAM_STEP)
    v_hat = v / (1.0 - ADAM_B2 ** ADAM_STEP)
    delta = -ADAM_LR * (m_hat / (jnp.sqrt(v_hat) + ADAM_EPS) + ADAM_WD * w)
    return delta, m, v


def _fetch_partials(s_ref, got_ref, buf, sems):
    x, y, _ = _place()
    cps = [pltpu.make_async_copy(s_ref.at[2 * x + y], buf.at[0], sems.at[0])]
    cps += [pltpu.make_async_copy(got_ref.at[j], buf.at[1 + j], sems.at[1 + j]) for j in range(3)]
    for cp in cps:
        cp.start()
    for cp in cps:
        cp.wait()


SMALL_AT = dict(norm_w=0, fnw=8, bias=16, bn=24, sinks=32, loss=40)
ROW_AT = (R_IN, R_A, R_B, R_O)


def _small_exchange(small):
    def body(small_ref, out_ref, send_sems, recv_sems):
        x, y, c = _place()
        me_slot = 4 * x + 2 * y + c
        sends = []
        k = 0
        for dx in range(2):
            for dy in range(2):
                for dc in range(2):
                    if dx == 0 and dy == 0 and dc == 0:
                        continue
                    sends.append(pltpu.make_async_remote_copy(
                        src_ref=small_ref, dst_ref=out_ref.at[me_slot],
                        send_sem=send_sems.at[k], recv_sem=recv_sems.at[k],
                        device_id=(x ^ dx, y ^ dy, c ^ dc), device_id_type=MESH))
                    k += 1
        for cp in sends:
            cp.start()
        out_ref[me_slot] = small_ref[...]
        for cp in sends:
            cp.wait_recv()
        for cp in sends:
            cp.wait_send()

    return pl.pallas_call(
        body, name="small_exchange",
        in_specs=[_vmem()], out_specs=_vmem(),
        out_shape=jax.ShapeDtypeStruct((NDEV, SMALL_ROWS, D), F32),
        scratch_shapes=[pltpu.SemaphoreType.DMA((7,)), pltpu.SemaphoreType.DMA((7,))],
    )(small)


def _finish(w_rows, m_rows, v_rows, ws, ms, vs, gu_w, gu_m, gu_v, smalls, sums, got):
    names = ["norm_w", "fnw", "bias", "bn", "sinks"]
    widths = [ws[n].shape[1] for n in names]
    shapes = [w.shape for w in w_rows]

    def body(*refs):
        wr_refs, mr_refs, vr_refs = refs[0:4], refs[4:8], refs[8:12]
        refs = refs[12:]
        w_refs, m_refs, v_refs = refs[0:5], refs[5:10], refs[10:15]
        guw_ref, gum_ref, guv_ref, smalls = refs[15:19]
        s_refs, got_refs = refs[19:21], refs[21:23]
        loss_ref = refs[23]
        row_outs = refs[24:40]
        outs = refs[40:60]
        gu_outs = refs[60:64]
        tot, buf, gsh, sems = refs[64:]
        x, y, c = _place()
        me_slot = 4 * x + 2 * y + c
        unshift = lax.rem(SHARD_PAD - 2 * me_slot, SHARD_PAD)
        for hf in range(2):
            _fetch_partials(s_refs[hf], got_refs[hf], buf, sems)
            for p in range(4):
                n, off = shapes[p][0], ROW_AT[p]
                nf = SHARD_PAD if p == 0 else n
                for cc in range(DH // LANE):
                    src = slice(cc * LANE, (cc + 1) * LANE)
                    cols = slice(hf * DH + cc * LANE, hf * DH + (cc + 1) * LANE)
                    g = buf[0, off:off + nf, src].astype(F32)
                    for j in range(1, 4):
                        g = g + buf[j, off:off + nf, src].astype(F32)
                    if p == 0:
                        gsh[...] = pltpu.roll(g, unshift, 0)
                        g = gsh[0:n, :]
                    d, nm, nv = _adamw_math(wr_refs[p][:, cols], g, mr_refs[p][:, cols], vr_refs[p][:, cols])
                    for o, val in zip(row_outs[4 * p:4 * p + 4], (g, d, nm, nv)):
                        o[:, cols] = val
            if hf == 0:
                g = buf[0, R_GU:R_GU + RANK, 0:64].astype(F32)
                for j in range(1, 4):
                    g = g + buf[j, R_GU:R_GU + RANK, 0:64].astype(F32)
                d, nm, nv = _adamw_math(guw_ref[...], g, gum_ref[...], guv_ref[...])
                for o, val in zip(gu_outs, (g, d, nm, nv)):
                    o[...] = val
        acc = smalls[0]
        for d in range(1, NDEV):
            acc = acc + smalls[d]
        tot[...] = acc
        loss_ref[...] = tot[SMALL_AT["loss"]:SMALL_AT["loss"] + 1, 0:1]
        for p, (nm_, wd) in enumerate(zip(names, widths)):
            r = SMALL_AT[nm_]
            g = tot[r:r + 1, 0:wd]
            d, nm, nv = _adamw_math(w_refs[p][...], g, m_refs[p][...], v_refs[p][...])
            for o, val in zip(outs[4 * p:4 * p + 4], (g, d, nm, nv)):
                o[...] = val

    out_shape = ([jax.ShapeDtypeStruct((1, 1), F32)]
                 + [jax.ShapeDtypeStruct(s, F32) for s in shapes for _ in range(4)]
                 + [jax.ShapeDtypeStruct((1, wd), F32) for wd in widths for _ in range(4)]
                 + [jax.ShapeDtypeStruct((RANK, 64), F32)] * 4)
    res = pl.pallas_call(
        body, name="finish",
        in_specs=[_vmem()] * 31 + [_any()] * 4,
        out_specs=[_vmem()] * 41,
        out_shape=out_shape,
        scratch_shapes=[pltpu.VMEM((SMALL_ROWS, D), F32),
                        pltpu.VMEM((4, ROWS, DH), sums[0].dtype), pltpu.VMEM((SHARD_PAD, LANE), F32),
                        pltpu.SemaphoreType.DMA((4,))],
        compiler_params=_cp(),
    )(*w_rows, *m_rows, *v_rows, *[ws[n] for n in names], *[ms[n] for n in names], *[vs[n] for n in names],
      gu_w, gu_m, gu_v, smalls, *sums, *got)
    loss = res[0]
    per = {n: tuple(res[17 + 4 * p:21 + 4 * p]) for p, n in enumerate(names)}
    return loss, tuple(res[1:17]), per, tuple(res[37:41])


def _place():
    x, y, c = lax.axis_index("x"), lax.axis_index("y"), lax.axis_index("c")
    return x, y, c


def _gather_blocks(w_in_t, w_a_s, w_b_s, w_o_s, gu_s, xs, norm_w, pos_col):
    rows, cols = ROWS, D
    T = xs.shape[0]
    tT = min(T, 256)
    inv_row, sign_row = _rope_rows()

    def body(wi_ref, wa_ref, wb_ref, wo_ref, gu_ref, xs_ref, nw_ref, pos_ref, inv_ref, sign_ref,
             out_ref, h_ref, cos_ref, sin_ref, x_ref, frame_ref, send_sems, recv_sems, local_sem):
        x, y, c = _place()
        me, sibling = (x, y, c), (x, y, 1 - c)
        chips = [(1 - x, y), (x, 1 - y), (1 - x, 1 - y)]
        shift = 2 * (4 * x + 2 * y + c)
        frame_ref[SHARD - SHARD % 8:, :] = jnp.zeros((SHARD_PAD - SHARD + SHARD % 8, D), F32)
        frame_ref[:SHARD, :] = wi_ref[...]
        for cc in range(D // LANE):
            cs = slice(cc * LANE, (cc + 1) * LANE)
            x_ref[R_IN:R_IN + SHARD_PAD, cs] = pltpu.roll(frame_ref[:, cs], shift, 0).astype(x_ref.dtype)
        x_ref[R_A:R_A + 128, :] = wa_ref[...].astype(x_ref.dtype)
        x_ref[R_B:R_B + 128, :] = wb_ref[...].astype(x_ref.dtype)
        x_ref[R_O:R_O + 128, :] = wo_ref[...].astype(x_ref.dtype)
        x_ref[R_GU:R_GU + RANK, :] = jnp.zeros((RANK, D), x_ref.dtype)
        x_ref[R_GU:R_GU + RANK, 0:64] = gu_ref[...].astype(x_ref.dtype)

        def slot(px, py, pc):
            return out_ref.at[4 * px + 2 * py + pc]

        def copy(k, block, to, src=None):
            return pltpu.make_async_remote_copy(
                src_ref=slot(*block) if src is None else src, dst_ref=slot(*block),
                send_sem=send_sems.at[k], recv_sem=recv_sems.at[k], device_id=to, device_id_type=MESH)

        mine = pltpu.make_async_copy(x_ref, slot(*me), local_sem)
        mine.start()
        first = [copy(0, me, sibling, src=x_ref)]
        first += [copy(1 + j, me, (*chip, c), src=x_ref) for j, chip in enumerate(chips)]
        for cp in first:
            cp.start()

        @pl.loop(0, T // tT)
        def _(i):
            rows_i = pl.ds(pl.multiple_of(i * tT, tT), tT)
            _prologue_rows(rows_i, xs_ref, nw_ref, pos_ref, inv_ref, sign_ref, h_ref, cos_ref, sin_ref)

        passed = [copy(4 + j, (*chip, c), sibling) for j, chip in enumerate(chips)]
        for j, chip in enumerate(chips):
            copy(1 + j, (*chip, c), me).wait_recv()
            passed[j].start()
        copy(0, sibling, me).wait_recv()
        for j, chip in enumerate(chips):
            copy(4 + j, (*chip, 1 - c), me).wait_recv()
        for cp in first + passed:
            cp.wait_send()
        mine.wait()

    return pl.pallas_call(
        body, name="gather_weights",
        in_specs=[_vmem()] * 10, out_specs=[_any()] + [_vmem()] * 3,
        out_shape=[jax.ShapeDtypeStruct((NDEV, rows, cols), WIRE), jax.ShapeDtypeStruct((T, D), MXU),
                   jax.ShapeDtypeStruct((T, LANE), F32), jax.ShapeDtypeStruct((T, LANE), F32)],
        scratch_shapes=[pltpu.VMEM((rows, cols), WIRE), pltpu.VMEM((SHARD_PAD, D), F32),
                        pltpu.SemaphoreType.DMA((7,)), pltpu.SemaphoreType.DMA((7,)), pltpu.SemaphoreType.DMA],
        compiler_params=_cp(),
    )(w_in_t, w_a_s, w_b_s, w_o_s, gu_s, xs, norm_w, pos_col, inv_row, sign_row)


def _pair_reduce(packed):
    def body(p_ref, out_ref, got, own, send_sems, recv_sems, own_sems):
        x, y, c = _place()
        sends = [pltpu.make_async_remote_copy(
            src_ref=p_ref.at[2 * chip + (1 - c)], dst_ref=got.at[chip],
            send_sem=send_sems.at[chip], recv_sem=recv_sems.at[chip], device_id=(x, y, 1 - c), device_id_type=MESH)
            for chip in range(4)]
        loads = [pltpu.make_async_copy(p_ref.at[2 * chip + c], own.at[chip], own_sems.at[chip]) for chip in range(4)]
        for cp in sends + loads:
            cp.start()
        for chip in range(4):
            loads[chip].wait()
            sends[chip].wait_recv()
            out_ref[chip] = (own[chip].astype(F32) + got[chip].astype(F32)).astype(out_ref.dtype)
        for cp in sends:
            cp.wait_send()

    return pl.pallas_call(
        body, name="pair_reduce",
        in_specs=[_any()], out_specs=_vmem(),
        out_shape=jax.ShapeDtypeStruct((4,) + packed.shape[1:], packed.dtype),
        scratch_shapes=[pltpu.VMEM((4,) + packed.shape[1:], packed.dtype), pltpu.VMEM((4,) + packed.shape[1:], packed.dtype),
                        pltpu.SemaphoreType.DMA((4,)), pltpu.SemaphoreType.DMA((4,)), pltpu.SemaphoreType.DMA((4,))],
        compiler_params=_cp(),
    )(packed)


def _pad_cols(a, cols):
    return jnp.pad(a, ((0, 0), (0, cols - a.shape[1])))


def _pad_rows(a, rows):
    return jnp.pad(a, ((0, rows - a.shape[0]), (0, 0)))


FRAME = 928


def _join_frames(frames):
    head = frames[:, :FRAME].at[1:, :16].add(frames[:-1, FRAME:])
    return jnp.concatenate([head.reshape(NDEV * FRAME, D), frames[NDEV - 1, FRAME:]], axis=0)


def _build_wft(wt):
    q = wt[0:1024].reshape(8, 2, 2, 32, D).transpose(0, 2, 1, 3, 4).reshape(1024, D)
    k = wt[1024:1152].reshape(2, 2, 1, 32, D)
    kd = jnp.broadcast_to(k, (2, 2, 2, 32, D)).reshape(256, D)
    v = wt[1152:1280].reshape(2, 1, 64, D)
    vd = jnp.broadcast_to(v, (2, 2, 64, D)).reshape(256, D)
    ag, bq, bk = wt[1280:2304], wt[2304:2816], wt[2816:3328]
    bv, bg, bl = wt[3328:4352], wt[4352:5376], wt[5376:5392]
    ma, mb = wt[5392:6416], wt[6416:7440]
    return jnp.concatenate([q, kd, vd, _pad_rows(bl, C_GLA - C_BL), bv, bq, bk, ag, bg, ma, mb], axis=0)


def _unbuild_gwt(g):
    n = g.shape[1]
    q = g[C_Q:C_Q + 1024].reshape(8, 2, 2, 32, n).transpose(0, 2, 1, 3, 4).reshape(1024, n)
    k = g[C_KD:C_KD + 256].reshape(2, 2, 2, 32, n).sum(axis=2).reshape(128, n)
    v = g[C_VD:C_VD + 256].reshape(2, 2, 64, n).sum(axis=1).reshape(128, n)
    bv, bq, bk = g[C_BV:C_BV + 1024], g[C_BQ:C_BQ + 512], g[C_BK:C_BK + 512]
    ag, bg, ma, mb = (g[c:c + 1024] for c in (C_AG, C_BG, C_MA, C_MB))
    return jnp.concatenate([q, k, v, ag, bq, bk, bv, bg, g[C_BL:C_BL + RANK], ma, mb], axis=0)


def kernel(x, positions, norm_w, w_in, a_sinks, b_gate_up, b_gate_bias, b_out_norm_w, w_a_proj, w_b_proj, w_out, final_norm_w, loss_target, m_norm_w, m_w_in, m_a_sinks, m_b_gate_up, m_b_gate_bias, m_b_out_norm_w, m_w_a_proj, m_w_b_proj, m_w_out, m_final_norm_w, v_norm_w, v_w_in, v_a_sinks, v_b_gate_up, v_b_gate_bias, v_b_out_norm_w, v_w_a_proj, v_w_b_proj, v_w_out, v_final_norm_w):
    T = x.shape[1]
    xs, target = x[0], loss_target[0]
    fnw = final_norm_w.reshape(1, D)
    allw, h, cos, sin = _gather_blocks(w_in[0].T, w_a_proj[0], w_b_proj[0], w_out[0], b_gate_up[0],
                                       xs, norm_w, positions.reshape(T, 1))
    wf = _build_wft(_join_frames(allw[:, :SHARD_PAD]))
    w_a = allw[:, R_A:R_A + 128, :].reshape(D, D)
    w_b = allw[:, R_B:R_B + 128, :].reshape(D, D)
    w_o = allw[:, R_O:R_O + 128, :].reshape(D, D)
    gu = allw[:, R_GU:R_GU + RANK, :64].transpose(1, 0, 2).reshape(RANK, 512)
    gu_pad = _pad_rows(gu, W_BL)

    proj = _proj(h, wf)
    o_a, lse = _swa_fwd(proj, cos, sin, a_sinks)
    o_b, states = _gla_fwd(proj, gu_pad, b_gate_bias)
    (dx2, do_a, do_b, d_gates, g_wa, g_wb, g_wo, g_fn, g_bn, loss_part) = _mid(
        xs, target, proj, o_a, o_b, w_a, w_b, w_o, jnp.tile(b_out_norm_w, (1, B_HEADS)), fnw)
    d_q, d_kv, g_sinks = _swa_bwd(proj, cos, sin, a_sinks, do_a, o_a, lse)
    d_gla, d_bl, g_gu, g_bias = _gla_bwd(proj, gu_pad, b_gate_bias, states, do_b)
    pieces = [d_q, d_kv, d_bl, d_gla, d_gates]
    offsets = [C_Q, C_KD, C_BL, C_GLA, C_GATES]

    ggu = g_gu[:RANK].reshape(RANK, NDEV, 64).transpose(1, 0, 2)
    ggu_half = [jnp.pad(ggu, ((0, 0), (0, 0), (0, DH - 64))), jnp.zeros((NDEV, RANK, DH), F32)]

    def pack(gw_half, hf):
        gwt = _unbuild_gwt(gw_half).astype(WIRE)
        cols = slice(hf * DH, (hf + 1) * DH)
        return jnp.concatenate([
            jnp.stack([gwt[FRAME * d:FRAME * d + SHARD_PAD] for d in range(NDEV)]),
            g_wa[:, cols].reshape(NDEV, 128, DH).astype(WIRE),
            g_wb[:, cols].reshape(NDEV, 128, DH).astype(WIRE),
            g_wo[:, cols].reshape(NDEV, 128, DH).astype(WIRE),
            ggu_half[hf].astype(WIRE)], axis=1)

    send0, recv0, s_thru0, land0, started0 = _chip_start(_pair_reduce(pack(_gw_half(h, pieces, 0), 0)), 0)
    send1, recv1, s_thru1, land1, started1 = _chip_start(
        _pair_reduce(pack(_gw_half(h, pieces, 1, after=started0), 1)), 1)
    grad_x, g_nw = _dh_norm(pieces, offsets, wf, xs, dx2, norm_w, started1)
    small = jnp.concatenate([g_nw, g_fn, _pad_cols(g_bias, D), _pad_cols(g_bn, D), _pad_cols(g_sinks, D),
                             _pad_cols(loss_part, D)], axis=0)
    smalls = _small_exchange(small)
    sums0, got0 = _chip_wait(send0, recv0, s_thru0, land0, smalls, 0)
    sums1, got1 = _chip_wait(send1, recv1, s_thru1, land1, got0, 1)
    sums, from_chips = [sums0, sums1], [got0, got1]

    ws = dict(norm_w=norm_w, fnw=fnw, bias=b_gate_bias, bn=b_out_norm_w, sinks=a_sinks)
    ms = dict(norm_w=m_norm_w, fnw=m_final_norm_w.reshape(1, D), bias=m_b_gate_bias, bn=m_b_out_norm_w,
              sinks=m_a_sinks)
    vs = dict(norm_w=v_norm_w, fnw=v_final_norm_w.reshape(1, D), bias=v_b_gate_bias, bn=v_b_out_norm_w,
              sinks=v_a_sinks)
    loss, t_rows, sm, t_gu = _finish(
        [w_in[0].T, w_a_proj[0], w_b_proj[0], w_out[0]], [m_w_in[0].T, m_w_a_proj[0], m_w_b_proj[0], m_w_out[0]],
        [v_w_in[0].T, v_w_a_proj[0], v_w_b_proj[0], v_w_out[0]],
        ws, ms, vs, b_gate_up[0], m_b_gate_up[0], v_b_gate_up[0], smalls, sums, from_chips)

    def outputs(k):
        return [sm["norm_w"][k], t_rows[k].T[None], sm["sinks"][k], t_gu[k][None], sm["bias"][k], sm["bn"][k],
                t_rows[4 + k][None], t_rows[8 + k][None], t_rows[12 + k][None], sm["fnw"][k].reshape(D)]

    return (loss[0, 0], grad_x[None], *outputs(0), *outputs(1), *outputs(2), *outputs(3))
```

```python
import functools

import numpy as np
import jax
import jax.numpy as jnp
from jax import lax
from jax.experimental import pallas as pl
from jax.experimental.pallas import tpu as pltpu

F32 = jnp.float32
MXU = jnp.bfloat16
WIRE = jnp.bfloat16

D = 1024
A_HEADS, A_KV, A_HD = 16, 2, 64
BLK = 128
B_HEADS, B_DK, B_DV = 4, 128, 256
RANK, TAU, CHUNK = 16, 16.0, 64
EPS, NEG = 1e-5, -1e30
ROPE_THETA = 10000.0
IN_WIDTH, NDEV = 7440, 8
SHARD = IN_WIDTH // NDEV
LANE = 128

C_Q, C_KD, C_VD, C_BL = 0, 1024, 1280, 1536
C_BV, C_BQ, C_BK = 2048, 3072, 3584
C_AG, C_BG, C_MA, C_MB = 4096, 5120, 6144, 7168
C_GLA, W_GLA, C_GATES, W_GATES = 2048, 2048, 4096, 4096
NF = 8192
W_BL = 128

SHARD_PAD = 944
R_IN, R_A, R_B, R_O, R_GU, ROWS = 0, 944, 1072, 1200, 1328, 1344
SMALL_ROWS = 48

ADAM_LR, ADAM_B1, ADAM_B2, ADAM_EPS, ADAM_WD, ADAM_STEP = 0.001, 0.9, 0.999, 1e-08, 0.01, 10

MESH = pl.DeviceIdType.MESH
VMEM_LIMIT = 56 * 1024 * 1024


def _cp(sem=None, **kw):
    if sem is not None:
        kw["dimension_semantics"] = sem
    return pltpu.CompilerParams(vmem_limit_bytes=VMEM_LIMIT, **kw)


def _dot(a, b):
    return jnp.dot(a, b, preferred_element_type=F32)


def _dot_nt(a, b):
    return lax.dot_general(a, b, (((1,), (1,)), ((), ())), preferred_element_type=F32)


def _dot_tn(a, b):
    return lax.dot_general(a, b, (((0,), (0,)), ((), ())), preferred_element_type=F32)


def _dot_f32(a, b):
    return jnp.dot(a, b, preferred_element_type=F32, precision=lax.Precision.HIGHEST)


def _sigmoid(z):
    return 0.5 * jnp.tanh(0.5 * z) + 0.5


def _rope(xp, cos, sin):
    return xp * cos + pltpu.roll(xp, 64, 1) * sin


def _rope_bwd(dy, cos, sin):
    return dy * cos - pltpu.roll(dy, 64, 1) * sin


def _vmem():
    return pl.BlockSpec(memory_space=pltpu.VMEM)


def _any():
    return pl.BlockSpec(memory_space=pl.ANY)


def _rope_rows():
    half = A_HD // 2
    inv = (np.float32(ROPE_THETA) ** (-np.arange(half, dtype=np.float32) / np.float32(half))).astype(np.float32)
    inv_row = jnp.asarray(np.tile(inv, 4)[None, :])
    sign_row = jnp.asarray(np.concatenate([-np.ones(64, np.float32), np.ones(64, np.float32)])[None, :])
    return inv_row, sign_row


def _prologue_rows(rows, x_ref, nw_ref, pos_ref, inv_ref, sign_ref, h_ref, cos_ref, sin_ref):
    xv = x_ref[rows, :]
    r = lax.rsqrt(jnp.mean(xv * xv, axis=-1, keepdims=True) + EPS)
    h_ref[rows, :] = ((xv * r) * nw_ref[...]).astype(h_ref.dtype)
    ang = pos_ref[rows, :].astype(F32) * inv_ref[...]
    cos_ref[rows, :] = jnp.cos(ang)
    sin_ref[rows, :] = jnp.sin(ang) * sign_ref[...]


def _proj(h, wft):
    T = h.shape[0]
    tT, tN = T, 512

    def body(h_ref, w_ref, o_ref):
        o_ref[...] = _dot_nt(h_ref[...], w_ref[...])

    return pl.pallas_call(
        body, name="proj", grid=(T // tT, NF // tN),
        in_specs=[pl.BlockSpec((tT, D), lambda i, j: (i, 0)), pl.BlockSpec((tN, D), lambda i, j: (j, 0))],
        out_specs=pl.BlockSpec((tT, tN), lambda i, j: (i, j)),
        out_shape=jax.ShapeDtypeStruct((T, NF), F32),
        compiler_params=_cp(("parallel", "parallel")),
    )(h, wft)


def _swa_masks():
    lane = lax.broadcasted_iota(jnp.int32, (BLK, LANE), 1)
    rope_sub0 = ((lane // 32) % 2) == 0
    std_sub0 = lane < 64
    return lane, rope_sub0, std_sub0


def _swa_tri():
    qi = lax.broadcasted_iota(jnp.int32, (BLK, BLK), 0)
    kj = lax.broadcasted_iota(jnp.int32, (BLK, BLK), 1)
    return kj <= qi


def _swa_fold(full, tri):
    return jnp.where(tri, full[:, BLK:], full[:, :BLK])


def _swa_unfold(sq, tri):
    return jnp.concatenate([jnp.where(tri, 0.0, sq), jnp.where(tri, sq, 0.0)], axis=1)


def _swa_keys(kc_ref, kp_ref, vc_ref, vp_ref, cq, sq, cp, sp):
    def ropek(kref, c, s):
        kv = kref[...]
        return jnp.concatenate([_rope(kv[:, :LANE], c, s), _rope(kv[:, LANE:], c, s)], axis=1)

    K = jnp.concatenate([ropek(kp_ref, cp, sp), ropek(kc_ref, cq, sq)], axis=0).astype(MXU)
    V = jnp.concatenate([vp_ref[...], vc_ref[...]], axis=0).astype(MXU)
    return K, V


def _swa_in_specs(nb, last):
    def cur(n):
        return jnp.minimum(n, last)

    def prev(n):
        return jnp.maximum(cur(n) - 1, 0)

    kd, vd = C_KD // 256, C_VD // 256
    return [
        pl.BlockSpec((BLK, D), lambda n: (cur(n), C_Q // D)),
        pl.BlockSpec((BLK, 256), lambda n: (cur(n), kd)),
        pl.BlockSpec((BLK, 256), lambda n: (prev(n), kd)),
        pl.BlockSpec((BLK, 256), lambda n: (cur(n), vd)),
        pl.BlockSpec((BLK, 256), lambda n: (prev(n), vd)),
        pl.BlockSpec((BLK, LANE), lambda n: (cur(n), 0)),
        pl.BlockSpec((BLK, LANE), lambda n: (cur(n), 0)),
        pl.BlockSpec((BLK, LANE), lambda n: (prev(n), 0)),
        pl.BlockSpec((BLK, LANE), lambda n: (prev(n), 0)),
    ]


def _swa_fwd(proj, cos, sin, sinks):
    T = proj.shape[0]
    nb = T // BLK
    scale = A_HD ** -0.5

    def body(sinks_ref, q_ref, kc_ref, kp_ref, vc_ref, vp_ref, cq_ref, sq_ref, cp_ref, sp_ref, o_ref, l_ref):
        n = pl.program_id(0)
        cq, sq = cq_ref[...], sq_ref[...]
        K, V = _swa_keys(kc_ref, kp_ref, vc_ref, vp_ref, cq, sq, cp_ref[...], sp_ref[...])
        tri = _swa_tri()
        valid = tri | (n > 0)
        lane, rope_sub0, std_sub0 = _swa_masks()
        group = A_HEADS // A_KV
        roped, lses = {}, []

        def products(head):
            pb, sub, g = head // 2, head % 2, head // group
            if sub == 0:
                roped[pb] = _rope(q_ref[:, pb * LANE:(pb + 1) * LANE], cq, sq)
            qm = jnp.where(rope_sub0 if sub == 0 else ~rope_sub0, roped[pb], 0.0).astype(MXU)
            return _dot_nt(qm, K[:, g * LANE:(g + 1) * LANE])

        def softmax(head, s_full):
            s = jnp.where(valid, _swa_fold(s_full, tri) * scale, NEG)
            sink = sinks_ref[0, head]
            m = jnp.maximum(jnp.max(s, axis=1, keepdims=True), sink)
            e = jnp.exp(s - m)
            den = jnp.sum(e, axis=1, keepdims=True) + jnp.exp(sink - m)
            lses.append(m + jnp.log(den))
            return _swa_unfold(e / den, tri).astype(MXU)

        outs = {}
        st1 = {0: products(0), 1: products(1)}
        st2 = {0: softmax(0, st1.pop(0))}
        for head in range(A_HEADS):
            if head + 2 < A_HEADS:
                st1[head + 2] = products(head + 2)
            if head + 1 < A_HEADS:
                st2[head + 1] = softmax(head + 1, st1.pop(head + 1))
            g = head // group
            outs[head] = _dot(st2.pop(head), V[:, g * LANE:(g + 1) * LANE])
            if head % 2 == 1:
                pb = head // 2
                o_ref[:, pb * LANE:(pb + 1) * LANE] = jnp.where(std_sub0, outs[head - 1], outs[head])
        lacc = jnp.zeros((BLK, LANE), F32)
        for head in range(A_HEADS):
            lacc = jnp.where(lane == head, lses[head], lacc)
        l_ref[...] = lacc

    return pl.pallas_call(
        body, name="swa_fwd", grid=(nb,),
        in_specs=[pl.BlockSpec(memory_space=pltpu.SMEM)] + _swa_in_specs(nb, nb - 1),
        out_specs=[pl.BlockSpec((BLK, D), lambda n: (n, 0)), pl.BlockSpec((BLK, LANE), lambda n: (n, 0))],
        out_shape=[jax.ShapeDtypeStruct((T, D), F32), jax.ShapeDtypeStruct((T, LANE), F32)],
        compiler_params=_cp(("parallel",)),
    )(sinks, proj, proj, proj, proj, proj, cos, sin, cos, sin)


def _swa_bwd(proj, cos, sin, sinks, do_a, o_a, lse):
    T = proj.shape[0]
    nb = T // BLK
    scale = A_HD ** -0.5

    def body(sinks_ref, q_ref, kc_ref, kp_ref, vc_ref, vp_ref, cq_ref, sq_ref, cp_ref, sp_ref,
             do_ref, o_ref, l_ref, dq_ref, dkv_ref, ds_ref, ckv_ref):
        n = pl.program_id(0)

        @pl.when(n == 0)
        def _():
            ckv_ref[...] = jnp.zeros_like(ckv_ref)
            ds_ref[...] = jnp.zeros_like(ds_ref)

        @pl.when(n < nb)
        def _():
            cq, sq, cp, sp = cq_ref[...], sq_ref[...], cp_ref[...], sp_ref[...]
            K, V = _swa_keys(kc_ref, kp_ref, vc_ref, vp_ref, cq, sq, cp, sp)
            tri = _swa_tri()
            valid = tri | (n > 0)
            lane, rope_sub0, std_sub0 = _swa_masks()
            lane_row = lax.broadcasted_iota(jnp.int32, (1, LANE), 1)
            lse_v = l_ref[...]
            dKt = [jnp.zeros((LANE, 2 * BLK), F32) for _ in range(A_KV)]
            dVt = [jnp.zeros((LANE, 2 * BLK), F32) for _ in range(A_KV)]
            dsinks, roped, roped_t, do_t = [], {}, {}, {}
            group = A_HEADS // A_KV
            dim = lax.broadcasted_iota(jnp.int32, (LANE, BLK), 0)
            rope_row0, std_row0 = ((dim // 32) % 2) == 0, dim < 64

            def products(head):
                pb, sub, g = head // 2, head % 2, head // group
                cols = slice(pb * LANE, (pb + 1) * LANE)
                Kg, Vg = K[:, g * LANE:(g + 1) * LANE], V[:, g * LANE:(g + 1) * LANE]
                if sub == 0:
                    roped[pb] = _rope(q_ref[:, cols], cq, sq)
                    roped_t[pb] = roped[pb].T
                    do_t[pb] = do_ref[:, cols].T
                qm = jnp.where(rope_sub0 if sub == 0 else ~rope_sub0, roped[pb], 0.0).astype(MXU)
                qmt = jnp.where(rope_row0 if sub == 0 else ~rope_row0, roped_t[pb], 0.0).astype(MXU)
                dov = jnp.where(std_sub0 if sub == 0 else ~std_sub0, do_ref[:, cols], 0.0)
                dovt = jnp.where(std_row0 if sub == 0 else ~std_row0, do_t[pb], 0.0).astype(MXU)
                delta = jnp.sum(dov * o_ref[:, cols], axis=1, keepdims=True)
                return qmt, dovt, delta, _dot_nt(qm, Kg), _dot_nt(dov.astype(MXU), Vg)

            def scores(head, qmt, dovt, delta, s_full, dp_full):
                lh = jnp.sum(jnp.where(lane == head, lse_v, 0.0), axis=1, keepdims=True)
                p = jnp.where(valid, jnp.exp(_swa_fold(s_full, tri) * scale - lh), 0.0)
                psink = jnp.exp(sinks_ref[0, head] - lh)
                dsinks.append(jnp.sum(-psink * delta, axis=0, keepdims=True))
                dsq = (p * (_swa_fold(dp_full, tri) - delta)) * scale
                return qmt, dovt, _swa_unfold(p, tri).astype(MXU), _swa_unfold(dsq, tri).astype(MXU)

            def grads(head, qmt, dovt, pb16, dsc):
                g = head // group
                dKt[g] = dKt[g] + _dot(qmt, dsc)
                dVt[g] = dVt[g] + _dot(dovt, pb16)
                return _dot(dsc, K[:, g * LANE:(g + 1) * LANE])

            dqs = {}
            st1 = {0: products(0), 1: products(1)}
            st2 = {0: scores(0, *st1.pop(0))}
            for head in range(A_HEADS):
                if head + 2 < A_HEADS:
                    st1[head + 2] = products(head + 2)
                if head + 1 < A_HEADS:
                    st2[head + 1] = scores(head + 1, *st1.pop(head + 1))
                dqs[head] = grads(head, *st2.pop(head))
                if head % 2 == 1:
                    pb = head // 2
                    dqp = jnp.where(rope_sub0, dqs[head - 1], dqs[head])
                    dq_ref[:, pb * LANE:(pb + 1) * LANE] = _rope_bwd(dqp, cq, sq).astype(dq_ref.dtype)
            dsink = jnp.zeros((1, LANE), F32)
            for head in range(A_HEADS):
                dsink = jnp.where(lane_row == head, dsinks[head], dsink)
            dK, dV = [a.T for a in dKt], [a.T for a in dVt]
            prev = ([_rope_bwd(dK[g][:BLK], cp, sp) for g in range(A_KV)] + [dV[g][:BLK] for g in range(A_KV)])
            cur_ = ([_rope_bwd(dK[g][BLK:], cq, sq) for g in range(A_KV)] + [dV[g][BLK:] for g in range(A_KV)])
            dkv_ref[...] = (ckv_ref[...] + jnp.concatenate(prev, axis=1)).astype(dkv_ref.dtype)
            ckv_ref[...] = jnp.concatenate(cur_, axis=1)
            ds_ref[...] = ds_ref[...] + jnp.broadcast_to(dsink, ds_ref.shape)

        @pl.when(n == nb)
        def _():
            dkv_ref[...] = ckv_ref[...].astype(dkv_ref.dtype)

    last = nb - 1

    def cur(n):
        return jnp.minimum(n, last)

    def out_kv(n):
        return (jnp.maximum(n - 1, 0), 0)

    return pl.pallas_call(
        body, name="swa_bwd", grid=(nb + 1,),
        in_specs=[pl.BlockSpec(memory_space=pltpu.SMEM)] + _swa_in_specs(nb, last) + [
            pl.BlockSpec((BLK, D), lambda n: (cur(n), 0)),
            pl.BlockSpec((BLK, D), lambda n: (cur(n), 0)),
            pl.BlockSpec((BLK, LANE), lambda n: (cur(n), 0)),
        ],
        out_specs=[
            pl.BlockSpec((BLK, D), lambda n: (cur(n), 0)),
            pl.BlockSpec((BLK, 512), out_kv),
            pl.BlockSpec((8, LANE), lambda n: (0, 0)),
        ],
        out_shape=[
            jax.ShapeDtypeStruct((T, D), MXU),
            jax.ShapeDtypeStruct((T, 512), MXU),
            jax.ShapeDtypeStruct((8, LANE), F32),
        ],
        scratch_shapes=[pltpu.VMEM((BLK, 512), F32)],
        compiler_params=_cp(("arbitrary",)),
    )(sinks, proj, proj, proj, proj, proj, cos, sin, cos, sin, do_a, o_a, lse)


def _gla_gate(bl_ref, gu_ref, bias_ref):
    gk = _dot(bl_ref[...].astype(MXU), gu_ref[...]) + bias_ref[...]
    la = (jnp.minimum(gk, 0.0) - jnp.log(1.0 + jnp.exp(-jnp.abs(gk)))) / TAU
    ri = lax.broadcasted_iota(jnp.int32, (CHUNK, CHUNK), 0)
    ci = lax.broadcasted_iota(jnp.int32, (CHUNK, CHUNK), 1)
    b = _dot_f32(jnp.where(ci <= ri, 1.0, 0.0).astype(F32), la)
    return gk, la, b, ri, ci


def _gla_head(q_ref, k_ref, la, b, h):
    sl = slice(h * B_DK, (h + 1) * B_DK)
    bh = b[:, sl]
    blast = jnp.sum(la[:, sl], axis=0, keepdims=True)
    qc = q_ref[:, sl] * (B_DK ** -0.5)
    kh = k_ref[:, sl]
    eb, enb, esb = jnp.exp(bh), jnp.exp(-bh), jnp.exp(blast - bh)
    return qc * eb, kh * enb, kh * esb, eb, enb, esb, jnp.exp(blast)


def _gla_specs(chunk_of):
    return [
        pl.BlockSpec((CHUNK, 512), lambda i: (chunk_of(i), C_BQ // 512)),
        pl.BlockSpec((CHUNK, 512), lambda i: (chunk_of(i), C_BK // 512)),
        pl.BlockSpec((CHUNK, D), lambda i: (chunk_of(i), C_BV // D)),
        pl.BlockSpec((CHUNK, W_BL), lambda i: (chunk_of(i), C_BL // W_BL)),
        pl.BlockSpec((W_BL, 512), lambda i: (0, 0)),
        pl.BlockSpec((1, 512), lambda i: (0, 0)),
    ]


def _gla_fwd(proj, gu_pad, bias):
    T = proj.shape[0]
    nc = T // CHUNK

    def body(q_ref, k_ref, v_ref, bl_ref, gu_ref, bias_ref, o_ref, st_ref, state_ref):
        @pl.when(pl.program_id(0) == 0)
        def _():
            state_ref[...] = jnp.zeros_like(state_ref)

        _, la, b, ri, ci = _gla_gate(bl_ref, gu_ref, bias_ref)
        st_ref[...] = state_ref[...]
        for h in range(B_HEADS):
            q_e, k_e, k_s, _, _, _, decay = _gla_head(q_ref, k_ref, la, b, h)
            vh = v_ref[:, h * B_DV:(h + 1) * B_DV].astype(MXU)
            rows = slice(h * B_DV, (h + 1) * B_DV)
            q_eb = q_e.astype(MXU)
            att = jnp.where(ci <= ri, _dot_nt(q_eb, k_e.astype(MXU)), 0.0)
            st = state_ref[rows, :]
            o_ref[:, rows] = _dot(att.astype(MXU), vh) + _dot_nt(q_eb, st.astype(MXU))
            state_ref[rows, :] = st * decay + _dot_tn(vh, k_s.astype(MXU))

    return pl.pallas_call(
        body, name="gla_fwd", grid=(nc,),
        in_specs=_gla_specs(lambda i: i),
        out_specs=[pl.BlockSpec((CHUNK, D), lambda i: (i, 0)),
                   pl.BlockSpec((B_HEADS * B_DV, B_DK), lambda i: (i, 0))],
        out_shape=[jax.ShapeDtypeStruct((T, D), F32),
                   jax.ShapeDtypeStruct((nc * B_HEADS * B_DV, B_DK), F32)],
        scratch_shapes=[pltpu.VMEM((B_HEADS * B_DV, B_DK), F32)],
        compiler_params=_cp(("arbitrary",)),
    )(proj, proj, proj, proj, gu_pad, bias)


def _gla_bwd(proj, gu_pad, bias, states, do_b):
    T = proj.shape[0]
    nc = T // CHUNK
    o_q, o_k = C_BQ - C_GLA, C_BK - C_GLA

    def body(q_ref, k_ref, v_ref, bl_ref, gu_ref, bias_ref, st_ref, do_ref,
             dg_ref, dbl_ref, ggu_ref, gbias_ref, gt_ref):
        @pl.when(pl.program_id(0) == 0)
        def _():
            gt_ref[...] = jnp.zeros_like(gt_ref)
            ggu_ref[...] = jnp.zeros_like(ggu_ref)
            gbias_ref[...] = jnp.zeros_like(gbias_ref)

        gk, la, b, ri, ci = _gla_gate(bl_ref, gu_ref, bias_ref)
        causal = ci <= ri
        upper = jnp.where(ci >= ri, 1.0, 0.0).astype(F32)
        dla_parts = []
        for h in range(B_HEADS):
            q_e, k_e, k_s, eb, enb, esb, decay = _gla_head(q_ref, k_ref, la, b, h)
            rows = slice(h * B_DV, (h + 1) * B_DV)
            sl = slice(h * B_DK, (h + 1) * B_DK)
            vh = v_ref[:, rows].astype(MXU)
            doh = do_ref[:, rows].astype(MXU)
            q_eb, k_eb, k_sb = q_e.astype(MXU), k_e.astype(MXU), k_s.astype(MXU)
            st = st_ref[rows, :]
            gt = gt_ref[rows, :]
            gtb = gt.astype(MXU)
            att = jnp.where(causal, _dot_nt(q_eb, k_eb), 0.0).astype(MXU)
            datt = jnp.where(causal, _dot_nt(doh, vh), 0.0).astype(MXU)
            dq_e = _dot(datt, k_eb) + _dot(doh, st.astype(MXU))
            dk_e = _dot_tn(datt, q_eb)
            dk_s = _dot(vh, gtb)
            dg_ref[:, rows] = (_dot_tn(att, doh) + _dot_nt(k_sb, gtb)).astype(dg_ref.dtype)
            ddecay = jnp.sum(gt * st, axis=0, keepdims=True)
            gt_ref[rows, :] = gt * decay + _dot_tn(doh, q_eb)
            dg_ref[:, o_q + h * B_DK:o_q + (h + 1) * B_DK] = (dq_e * eb * (B_DK ** -0.5)).astype(dg_ref.dtype)
            dg_ref[:, o_k + h * B_DK:o_k + (h + 1) * B_DK] = (dk_e * enb + dk_s * esb).astype(dg_ref.dtype)
            dks_ks = dk_s * k_s
            db = dq_e * q_e - dk_e * k_e - dks_ks
            dblast = jnp.sum(dks_ks, axis=0, keepdims=True) + ddecay * decay
            dla_parts.append(_dot_f32(upper, db) + dblast)
        dla = jnp.concatenate(dla_parts, axis=1)
        dgk = dla * (1.0 / TAU) * _sigmoid(-gk)
        dgkb = dgk.astype(MXU)
        dbl_ref[...] = _dot_nt(dgkb, gu_ref[...]).astype(dbl_ref.dtype)
        ggu_ref[...] = ggu_ref[...] + _dot_tn(bl_ref[...].astype(MXU), dgkb)
        gbias_ref[...] = gbias_ref[...] + jnp.broadcast_to(jnp.sum(dgk, axis=0, keepdims=True), gbias_ref.shape)

    def rev(i):
        return nc - 1 - i

    return pl.pallas_call(
        body, name="gla_bwd", grid=(nc,),
        in_specs=_gla_specs(rev) + [
            pl.BlockSpec((B_HEADS * B_DV, B_DK), lambda i: (rev(i), 0)),
            pl.BlockSpec((CHUNK, D), lambda i: (rev(i), 0)),
        ],
        out_specs=[
            pl.BlockSpec((CHUNK, W_GLA), lambda i: (rev(i), 0)),
            pl.BlockSpec((CHUNK, W_BL), lambda i: (rev(i), 0)),
            pl.BlockSpec((W_BL, 512), lambda i: (0, 0)),
            pl.BlockSpec((8, 512), lambda i: (0, 0)),
        ],
        out_shape=[
            jax.ShapeDtypeStruct((T, W_GLA), MXU),
            jax.ShapeDtypeStruct((T, W_BL), MXU),
            jax.ShapeDtypeStruct((W_BL, 512), F32),
            jax.ShapeDtypeStruct((8, 512), F32),
        ],
        scratch_shapes=[pltpu.VMEM((B_HEADS * B_DV, B_DK), F32)],
        compiler_params=_cp(("arbitrary",)),
    )(proj, proj, proj, proj, gu_pad, bias, states, do_b)


def _mid(x, target, proj, o_a, o_b, w_a, w_b, w_out, w_bn4, fnw):
    T = x.shape[0]
    tT = min(T, 128)
    nbuf = 4
    o_ag, o_bg, o_ma, o_mb = (c - C_GATES for c in (C_AG, C_BG, C_MA, C_MB))

    def body(x_ref, t_ref, oa_ref, ob_ref, gates_ref, wa_ref, wb_ref, wo_ref, wbn_ref, fnw_ref,
             dx2_ref, doa_ref, dob_ref, dgates_ref,
             gwa_ref, gwb_ref, gwo_ref, gfn_ref, gbn_ref, loss_ref, buf_ref):
        i = pl.program_id(0)

        @pl.when(i == 0)
        def _():
            for r in (gwa_ref, gwb_ref, gwo_ref, gfn_ref, gbn_ref, loss_ref):
                r[...] = jnp.zeros_like(r)

        rows = pl.ds(pl.multiple_of((i % nbuf) * tT, tT), tT)

        def keep(k, val):
            buf_ref[k, rows, :] = val

        oa, ag = oa_ref[...], gates_ref[:, o_ag:o_ag + D]
        sg_a = _sigmoid(ag)
        silu_a = ag * sg_a
        oag_b = (oa * silu_a).astype(MXU)
        keep(0, oag_b)
        y_a = _dot(oag_b, wa_ref[...])

        ob, bg = ob_ref[...], gates_ref[:, o_bg:o_bg + D]
        rbs, obhats = [], []
        for h in range(B_HEADS):
            obh = ob[:, h * B_DV:(h + 1) * B_DV]
            rb = lax.rsqrt(jnp.mean(obh * obh, axis=-1, keepdims=True) + EPS)
            rbs.append(rb)
            obhats.append(obh * rb)
        obhat = jnp.concatenate(obhats, axis=1)
        wbn = wbn_ref[...]
        obn = obhat * wbn
        sg_b = _sigmoid(bg)
        silu_b = bg * sg_b
        obg_b = (obn * silu_b).astype(MXU)
        keep(1, obg_b)
        y_b = _dot(obg_b, wb_ref[...])

        sa, sb = _sigmoid(gates_ref[:, o_ma:o_ma + D]), _sigmoid(gates_ref[:, o_mb:o_mb + D])
        mg_b = (sa * y_a + sb * y_b).astype(MXU)
        keep(2, mg_b)
        x2 = x_ref[...] + _dot(mg_b, wo_ref[...])
        r2 = lax.rsqrt(jnp.mean(x2 * x2, axis=-1, keepdims=True) + EPS)
        xh2 = x2 * r2
        fw = fnw_ref[...]
        err = xh2 * fw - t_ref[...]
        tok = jnp.mean(err * err, axis=-1, keepdims=True)
        loss_ref[...] = loss_ref[...] + 0.5 * jnp.sum(tok, axis=0, keepdims=True)

        dy = err * (1.0 / D)
        gfn_ref[...] = gfn_ref[...] + jnp.broadcast_to(jnp.sum(dy * xh2, axis=0, keepdims=True), gfn_ref.shape)
        gy = dy * fw
        dx2 = r2 * (gy - xh2 * jnp.mean(gy * xh2, axis=-1, keepdims=True))
        dx2_ref[...] = dx2
        dx2_b = dx2.astype(MXU)
        keep(5, dx2_b)
        dmg = _dot_nt(dx2_b, wo_ref[...])

        dgates_ref[:, o_ma:o_ma + D] = (dmg * y_a * sa * (1.0 - sa)).astype(dgates_ref.dtype)
        dgates_ref[:, o_mb:o_mb + D] = (dmg * y_b * sb * (1.0 - sb)).astype(dgates_ref.dtype)
        dya_b = (dmg * sa).astype(MXU)
        dyb_b = (dmg * sb).astype(MXU)
        keep(3, dya_b)
        keep(4, dyb_b)
        doag = _dot_nt(dya_b, wa_ref[...])
        dobg = _dot_nt(dyb_b, wb_ref[...])

        @pl.when(i % nbuf == nbuf - 1)
        def _():
            gwa_ref[...] = gwa_ref[...] + _dot_tn(buf_ref[0], buf_ref[3])
            gwb_ref[...] = gwb_ref[...] + _dot_tn(buf_ref[1], buf_ref[4])
            gwo_ref[...] = gwo_ref[...] + _dot_tn(buf_ref[2], buf_ref[5])

        doa_ref[...] = doag * silu_a
        dgates_ref[:, o_ag:o_ag + D] = (doag * oa * (sg_a * (1.0 + ag * (1.0 - sg_a)))).astype(dgates_ref.dtype)
        dobn = dobg * silu_b
        dgates_ref[:, o_bg:o_bg + D] = (dobg * obn * (sg_b * (1.0 + bg * (1.0 - sg_b)))).astype(dgates_ref.dtype)
        gg = dobn * wbn
        gbn = jnp.zeros((1, B_DV), F32)
        for h in range(B_HEADS):
            sl = slice(h * B_DV, (h + 1) * B_DV)
            gbn = gbn + jnp.sum(dobn[:, sl] * obhats[h], axis=0, keepdims=True)
            ggh = gg[:, sl]
            dob_ref[:, sl] = rbs[h] * (ggh - obhats[h] * jnp.mean(ggh * obhats[h], axis=-1, keepdims=True))
        gbn_ref[...] = gbn_ref[...] + jnp.broadcast_to(gbn, gbn_ref.shape)

    assert (T // tT) % nbuf == 0
    tile = pl.BlockSpec((tT, D), lambda i: (i, 0))
    row = pl.BlockSpec((1, D), lambda i: (0, 0))
    acc8 = pl.BlockSpec((8, D), lambda i: (0, 0))
    return pl.pallas_call(
        body, name="mid", grid=(T // tT,),
        in_specs=[tile, tile, tile, tile, pl.BlockSpec((tT, W_GATES), lambda i: (i, C_GATES // W_GATES)),
                  _vmem(), _vmem(), _vmem(), row, row],
        out_specs=[tile, tile, tile, pl.BlockSpec((tT, W_GATES), lambda i: (i, 0)), _vmem(), _vmem(), _vmem(),
                   acc8, pl.BlockSpec((8, B_DV), lambda i: (0, 0)), pl.BlockSpec((8, LANE), lambda i: (0, 0))],
        out_shape=[
            jax.ShapeDtypeStruct((T, D), F32),
            jax.ShapeDtypeStruct((T, D), F32),
            jax.ShapeDtypeStruct((T, D), F32),
            jax.ShapeDtypeStruct((T, W_GATES), MXU),
            jax.ShapeDtypeStruct((D, D), F32),
            jax.ShapeDtypeStruct((D, D), F32),
            jax.ShapeDtypeStruct((D, D), F32),
            jax.ShapeDtypeStruct((8, D), F32),
            jax.ShapeDtypeStruct((8, B_DV), F32),
            jax.ShapeDtypeStruct((8, LANE), F32),
        ],
        scratch_shapes=[pltpu.VMEM((6, nbuf * tT, D), MXU)],
        compiler_params=_cp(("arbitrary",)),
    )(x, target, o_a, o_b, proj, w_a, w_b, w_out, w_bn4, fnw)


DH = D // 2


def _gw_half(h, pieces, half, after=None):
    T = h.shape[0]
    steps = NF // 512
    tiles = ((0, 2), (2, 3), (4, 8), (8, 16))

    def body(*refs):
        h_ref, q_ref, kv_ref, bl_ref, gla_ref, gates_ref = refs[:6]
        o_ref = refs[-1]
        j = pl.program_id(0)

        for (lo, hi), ref in zip(tiles, (q_ref, kv_ref, gla_ref, gates_ref)):
            @pl.when((j >= lo) & (j < hi))
            def _(ref=ref):
                o_ref[...] = _dot_tn(ref[...], h_ref[...])

        @pl.when(j == 3)
        def _():
            o_ref[0:W_BL, :] = _dot_tn(bl_ref[...], h_ref[...])
            o_ref[W_BL:, :] = jnp.zeros((512 - W_BL, DH), F32)

    def tile_of(lo, hi):
        return lambda j: (0, jnp.clip(j - lo, 0, hi - lo - 1))

    in_specs = [pl.BlockSpec((T, DH), lambda j: (0, half)),
                pl.BlockSpec((T, 512), tile_of(0, 2)), pl.BlockSpec((T, 512), lambda j: (0, 0)),
                pl.BlockSpec((T, W_BL), lambda j: (0, 0)),
                pl.BlockSpec((T, 512), tile_of(4, 8)), pl.BlockSpec((T, 512), tile_of(8, 16))]
    args = [h, *pieces]
    if after is not None:
        in_specs.append(_any())
        args.append(after)
    return pl.pallas_call(
        body, name=f"gw_in_half{half}", grid=(steps,),
        in_specs=in_specs, out_specs=pl.BlockSpec((512, DH), lambda j: (j, 0)),
        out_shape=jax.ShapeDtypeStruct((NF, DH), F32),
        compiler_params=_cp(("parallel",)),
    )(*args)


def _chip_copies(s_ref, got_ref, send_sems, recv_sems):
    x, y, c = _place()
    chips = [(1 - x, y), (x, 1 - y), (1 - x, 1 - y)]
    return [pltpu.make_async_remote_copy(
        src_ref=s_ref.at[2 * px + py], dst_ref=got_ref.at[j],
        send_sem=send_sems.at[j], recv_sem=recv_sems.at[j], device_id=(px, py, c), device_id_type=MESH)
        for j, (px, py) in enumerate(chips)]


_EFFECT = pltpu.SideEffectType.DATAFLOW_SIDE_EFFECTING


def _hbm():
    return pl.BlockSpec(memory_space=pltpu.HBM)


def _sem():
    return pl.BlockSpec(memory_space=pltpu.SEMAPHORE)


def _chip_start(sums, half):
    land = pltpu.with_memory_space_constraint(lax.empty((3,) + sums.shape[1:], sums.dtype), pltpu.HBM)

    def body(s_ref, land_ref, send_sems, recv_sems, s_thru, land_thru, token):
        for cp in _chip_copies(s_ref, land_ref, send_sems, recv_sems):
            cp.start()
        token[...] = jnp.zeros_like(token)

    return pl.pallas_call(
        body, name=f"chip_start{half}",
        out_shape=(pltpu.SemaphoreType.DMA((3,)), pltpu.SemaphoreType.DMA((3,)),
                   pltpu.HBM(sums.shape, sums.dtype), pltpu.HBM(land.shape, land.dtype),
                   jax.ShapeDtypeStruct((8, LANE), F32)),
        in_specs=(_hbm(), _hbm()), out_specs=(_sem(), _sem(), _hbm(), _hbm(), _vmem()),
        input_output_aliases={0: 2, 1: 3},
        compiler_params=pltpu.CompilerParams(has_side_effects=_EFFECT),
    )(pltpu.with_memory_space_constraint(sums, pltpu.HBM), land)


def _chip_wait(send_sems, recv_sems, s_thru, land_thru, after, half):
    def body(s_ref, land_ref, send_sems, recv_sems, after_ref, s_out, got_ref):
        copies = _chip_copies(s_ref, land_ref, send_sems, recv_sems)
        for cp in copies:
            cp.wait_send()
        for cp in copies:
            cp.wait_recv()

    return pl.pallas_call(
        body, name=f"chip_wait{half}",
        out_shape=(pltpu.HBM(s_thru.shape, s_thru.dtype), pltpu.HBM(land_thru.shape, land_thru.dtype)),
        in_specs=(_hbm(), _hbm(), _sem(), _sem(), _any()), out_specs=(_hbm(), _hbm()),
        input_output_aliases={0: 0, 1: 1},
        compiler_params=pltpu.CompilerParams(has_side_effects=_EFFECT),
    )(s_thru, land_thru, send_sems, recv_sems, after)


def _dh_norm(pieces, offsets, wf, x, dx2, norm_w, after):
    T = x.shape[0]
    tT = min(T, 256)
    widths = [p.shape[1] for p in pieces]
    npc = len(pieces)

    def body(*refs):
        dp_refs = refs[:npc]
        wf_ref, x_ref, dx2_ref, nw_ref, _, gx_ref, gnw_ref = refs[npc:]

        @pl.when(pl.program_id(0) == 0)
        def _():
            gnw_ref[...] = jnp.zeros_like(gnw_ref)

        dh = jnp.zeros((tT, D), F32)
        for dp_ref, off, w in zip(dp_refs, offsets, widths):
            dh = dh + _dot(dp_ref[...], wf_ref[off:off + w, :])
        xv = x_ref[...]
        r = lax.rsqrt(jnp.mean(xv * xv, axis=-1, keepdims=True) + EPS)
        xh = xv * r
        gnw_ref[...] = gnw_ref[...] + jnp.broadcast_to(jnp.sum(dh * xh, axis=0, keepdims=True), gnw_ref.shape)
        g = dh * nw_ref[...]
        gx_ref[...] = r * (g - xh * jnp.mean(g * xh, axis=-1, keepdims=True)) + dx2_ref[...]

    tile = pl.BlockSpec((tT, D), lambda i: (i, 0))
    return pl.pallas_call(
        body, name="dh_norm", grid=(T // tT,),
        in_specs=[pl.BlockSpec((tT, w), lambda i: (i, 0)) for w in widths]
        + [_vmem(), tile, tile, pl.BlockSpec((1, D), lambda i: (0, 0)), _any()],
        out_specs=[tile, pl.BlockSpec((8, D), lambda i: (0, 0))],
        out_shape=[jax.ShapeDtypeStruct((T, D), F32), jax.ShapeDtypeStruct((8, D), F32)],
        compiler_params=_cp(("arbitrary",)),
    )(*pieces, wf, x, dx2, norm_w, after)


def _adamw_math(w, g, m, v):
    m = ADAM_B1 * m + (1.0 - ADAM_B1) * g
    v = ADAM_B2 * v + (1.0 - ADAM_B2) * (g * g)
    m_hat = m / (1.0 - ADAM_B1 ** ADAM_STEP)
    v_hat = v / (1.0 - ADAM_B2 ** ADAM_STEP)
    delta = -ADAM_LR * (m_hat / (jnp.sqrt(v_hat) + ADAM_EPS) + ADAM_WD * w)
    return delta, m, v


def _fetch_partials(s_ref, got_ref, buf, sems):
    x, y, _ = _place()
    cps = [pltpu.make_async_copy(s_ref.at[2 * x + y], buf.at[0], sems.at[0])]
    cps += [pltpu.make_async_copy(got_ref.at[j], buf.at[1 + j], sems.at[1 + j]) for j in range(3)]
    for cp in cps:
        cp.start()
    for cp in cps:
        cp.wait()


SMALL_AT = dict(norm_w=0, fnw=8, bias=16, bn=24, sinks=32, loss=40)
ROW_AT = (R_IN, R_A, R_B, R_O)


def _small_exchange(small):
    def body(small_ref, out_ref, send_sems, recv_sems):
        x, y, c = _place()
        me_slot = 4 * x + 2 * y + c
        sends = []
        k = 0
        for dx in range(2):
            for dy in range(2):
                for dc in range(2):
                    if dx == 0 and dy == 0 and dc == 0:
                        continue
                    sends.append(pltpu.make_async_remote_copy(
                        src_ref=small_ref, dst_ref=out_ref.at[me_slot],
                        send_sem=send_sems.at[k], recv_sem=recv_sems.at[k],
                        device_id=(x ^ dx, y ^ dy, c ^ dc), device_id_type=MESH))
                    k += 1
        for cp in sends:
            cp.start()
        out_ref[me_slot] = small_ref[...]
        for cp in sends:
            cp.wait_recv()
        for cp in sends:
            cp.wait_send()

    return pl.pallas_call(
        body, name="small_exchange",
        in_specs=[_vmem()], out_specs=_vmem(),
        out_shape=jax.ShapeDtypeStruct((NDEV, SMALL_ROWS, D), F32),
        scratch_shapes=[pltpu.SemaphoreType.DMA((7,)), pltpu.SemaphoreType.DMA((7,))],
    )(small)


def _finish(w_rows, m_rows, v_rows, ws, ms, vs, gu_w, gu_m, gu_v, smalls, sums, got):
    names = ["norm_w", "fnw", "bias", "bn", "sinks"]
    widths = [ws[n].shape[1] for n in names]
    shapes = [w.shape for w in w_rows]

    def body(*refs):
        wr_refs, mr_refs, vr_refs = refs[0:4], refs[4:8], refs[8:12]
        refs = refs[12:]
        w_refs, m_refs, v_refs = refs[0:5], refs[5:10], refs[10:15]
        guw_ref, gum_ref, guv_ref, smalls = refs[15:19]
        s_refs, got_refs = refs[19:21], refs[21:23]
        loss_ref = refs[23]
        row_outs = refs[24:40]
        outs = refs[40:60]
        gu_outs = refs[60:64]
        tot, buf, gsh, sems = refs[64:]
        x, y, c = _place()
        me_slot = 4 * x + 2 * y + c
        unshift = lax.rem(SHARD_PAD - 2 * me_slot, SHARD_PAD)
        for hf in range(2):
            _fetch_partials(s_refs[hf], got_refs[hf], buf, sems)
            for p in range(4):
                n, off = shapes[p][0], ROW_AT[p]
                nf = SHARD_PAD if p == 0 else n
                for cc in range(DH // LANE):
                    src = slice(cc * LANE, (cc + 1) * LANE)
                    cols = slice(hf * DH + cc * LANE, hf * DH + (cc + 1) * LANE)
                    g = buf[0, off:off + nf, src].astype(F32)
                    for j in range(1, 4):
                        g = g + buf[j, off:off + nf, src].astype(F32)
                    if p == 0:
                        gsh[...] = pltpu.roll(g, unshift, 0)
                        g = gsh[0:n, :]
                    d, nm, nv = _adamw_math(wr_refs[p][:, cols], g, mr_refs[p][:, cols], vr_refs[p][:, cols])
                    for o, val in zip(row_outs[4 * p:4 * p + 4], (g, d, nm, nv)):
                        o[:, cols] = val
            if hf == 0:
                g = buf[0, R_GU:R_GU + RANK, 0:64].astype(F32)
                for j in range(1, 4):
                    g = g + buf[j, R_GU:R_GU + RANK, 0:64].astype(F32)
                d, nm, nv = _adamw_math(guw_ref[...], g, gum_ref[...], guv_ref[...])
                for o, val in zip(gu_outs, (g, d, nm, nv)):
                    o[...] = val
        acc = smalls[0]
        for d in range(1, NDEV):
            acc = acc + smalls[d]
        tot[...] = acc
        loss_ref[...] = tot[SMALL_AT["loss"]:SMALL_AT["loss"] + 1, 0:1]
        for p, (nm_, wd) in enumerate(zip(names, widths)):
            r = SMALL_AT[nm_]
            g = tot[r:r + 1, 0:wd]
            d, nm, nv = _adamw_math(w_refs[p][...], g, m_refs[p][...], v_refs[p][...])
            for o, val in zip(outs[4 * p:4 * p + 4], (g, d, nm, nv)):
                o[...] = val

    out_shape = ([jax.ShapeDtypeStruct((1, 1), F32)]
                 + [jax.ShapeDtypeStruct(s, F32) for s in shapes for _ in range(4)]
                 + [jax.ShapeDtypeStruct((1, wd), F32) for wd in widths for _ in range(4)]
                 + [jax.ShapeDtypeStruct((RANK, 64), F32)] * 4)
    res = pl.pallas_call(
        body, name="finish",
        in_specs=[_vmem()] * 31 + [_any()] * 4,
        out_specs=[_vmem()] * 41,
        out_shape=out_shape,
        scratch_shapes=[pltpu.VMEM((SMALL_ROWS, D), F32),
                        pltpu.VMEM((4, ROWS, DH), sums[0].dtype), pltpu.VMEM((SHARD_PAD, LANE), F32),
                        pltpu.SemaphoreType.DMA((4,))],
        compiler_params=_cp(),
    )(*w_rows, *m_rows, *v_rows, *[ws[n] for n in names], *[ms[n] for n in names], *[vs[n] for n in names],
      gu_w, gu_m, gu_v, smalls, *sums, *got)
    loss = res[0]
    per = {n: tuple(res[17 + 4 * p:21 + 4 * p]) for p, n in enumerate(names)}
    return loss, tuple(res[1:17]), per, tuple(res[37:41])


def _place():
    x, y, c = lax.axis_index("x"), lax.axis_index("y"), lax.axis_index("c")
    return x, y, c


def _peers(x, y, c):
    return [(x ^ dx, y ^ dy, c ^ dc) for dx in range(2) for dy in range(2) for dc in range(2) if dx + dy + dc]


def _late_gather_start(blk, after):
    land = pltpu.with_memory_space_constraint(lax.empty((NDEV,) + blk.shape, blk.dtype), pltpu.HBM)

    def body(b_ref, land_ref, after_ref, send_sems, recv_sems, b_thru, land_thru, token):
        x, y, c = _place()
        for k, to in enumerate(_peers(x, y, c)):
            pltpu.make_async_remote_copy(
                src_ref=b_ref, dst_ref=land_ref.at[4 * x + 2 * y + c], send_sem=send_sems.at[k],
                recv_sem=recv_sems.at[k], device_id=to, device_id_type=MESH).start()
        token[...] = jnp.zeros_like(token)

    return pl.pallas_call(
        body, name="late_gather_start",
        out_shape=(pltpu.SemaphoreType.DMA((7,)), pltpu.SemaphoreType.DMA((7,)),
                   pltpu.HBM(blk.shape, blk.dtype), pltpu.HBM(land.shape, land.dtype),
                   jax.ShapeDtypeStruct((8, LANE), F32)),
        in_specs=(_hbm(), _hbm(), _any()), out_specs=(_sem(), _sem(), _hbm(), _hbm(), _vmem()),
        input_output_aliases={0: 2, 1: 3},
        compiler_params=pltpu.CompilerParams(has_side_effects=_EFFECT),
    )(pltpu.with_memory_space_constraint(blk, pltpu.HBM), land, after)


def _late_gather_wait(send_sems, recv_sems, b_thru, land_thru, after):
    def body(b_ref, land_ref, send_sems, recv_sems, after_ref, b_out, got_ref):
        x, y, c = _place()
        copies = [pltpu.make_async_remote_copy(
            src_ref=b_ref, dst_ref=land_ref.at[4 * x + 2 * y + c], send_sem=send_sems.at[k],
            recv_sem=recv_sems.at[k], device_id=to, device_id_type=MESH)
            for k, to in enumerate(_peers(x, y, c))]
        for cp in copies:
            cp.wait_send()
        for cp in copies:
            cp.wait_recv()

    return pl.pallas_call(
        body, name="late_gather_wait",
        out_shape=(pltpu.HBM(b_thru.shape, b_thru.dtype), pltpu.HBM(land_thru.shape, land_thru.dtype)),
        in_specs=(_hbm(), _hbm(), _sem(), _sem(), _any()), out_specs=(_hbm(), _hbm()),
        input_output_aliases={0: 0, 1: 1},
        compiler_params=pltpu.CompilerParams(has_side_effects=_EFFECT),
    )(b_thru, land_thru, send_sems, recv_sems, after)


G_ROWS = SHARD_PAD + RANK


def _gather_blocks(w_in_t, gu_s, xs, norm_w, pos_col):
    rows, cols = G_ROWS, D
    T = xs.shape[0]
    tT = min(T, 256)
    inv_row, sign_row = _rope_rows()

    def body(wi_ref, gu_ref, xs_ref, nw_ref, pos_ref, inv_ref, sign_ref,
             out_ref, h_ref, cos_ref, sin_ref, x_ref, frame_ref, send_sems, recv_sems, local_sem):
        x, y, c = _place()
        me, sibling = (x, y, c), (x, y, 1 - c)
        chips = [(1 - x, y), (x, 1 - y), (1 - x, 1 - y)]
        shift = 2 * (4 * x + 2 * y + c)
        frame_ref[SHARD - SHARD % 8:, :] = jnp.zeros((SHARD_PAD - SHARD + SHARD % 8, D), F32)
        frame_ref[:SHARD, :] = wi_ref[...]
        for cc in range(D // LANE):
            cs = slice(cc * LANE, (cc + 1) * LANE)
            x_ref[0:SHARD_PAD, cs] = pltpu.roll(frame_ref[:, cs], shift, 0).astype(x_ref.dtype)
        x_ref[SHARD_PAD:G_ROWS, :] = jnp.zeros((RANK, D), x_ref.dtype)
        x_ref[SHARD_PAD:G_ROWS, 0:64] = gu_ref[...].astype(x_ref.dtype)

        def slot(px, py, pc):
            return out_ref.at[4 * px + 2 * py + pc]

        def copy(k, block, to, src=None):
            return pltpu.make_async_remote_copy(
                src_ref=slot(*block) if src is None else src, dst_ref=slot(*block),
                send_sem=send_sems.at[k], recv_sem=recv_sems.at[k], device_id=to, device_id_type=MESH)

        mine = pltpu.make_async_copy(x_ref, slot(*me), local_sem)
        mine.start()
        first = [copy(0, me, sibling, src=x_ref)]
        first += [copy(1 + j, me, (*chip, c), src=x_ref) for j, chip in enumerate(chips)]
        for cp in first:
            cp.start()

        @pl.loop(0, T // tT)
        def _(i):
            rows_i = pl.ds(pl.multiple_of(i * tT, tT), tT)
            _prologue_rows(rows_i, xs_ref, nw_ref, pos_ref, inv_ref, sign_ref, h_ref, cos_ref, sin_ref)

        passed = [copy(4 + j, (*chip, c), sibling) for j, chip in enumerate(chips)]
        for j, chip in enumerate(chips):
            copy(1 + j, (*chip, c), me).wait_recv()
            passed[j].start()
        copy(0, sibling, me).wait_recv()
        for j, chip in enumerate(chips):
            copy(4 + j, (*chip, 1 - c), me).wait_recv()
        for cp in first + passed:
            cp.wait_send()
        mine.wait()

    return pl.pallas_call(
        body, name="gather_weights",
        in_specs=[_vmem()] * 7, out_specs=[_any()] + [_vmem()] * 3,
        out_shape=[jax.ShapeDtypeStruct((NDEV, rows, cols), WIRE), jax.ShapeDtypeStruct((T, D), MXU),
                   jax.ShapeDtypeStruct((T, LANE), F32), jax.ShapeDtypeStruct((T, LANE), F32)],
        scratch_shapes=[pltpu.VMEM((rows, cols), WIRE), pltpu.VMEM((SHARD_PAD, D), F32),
                        pltpu.SemaphoreType.DMA((7,)), pltpu.SemaphoreType.DMA((7,)), pltpu.SemaphoreType.DMA],
        compiler_params=_cp(),
    )(w_in_t, gu_s, xs, norm_w, pos_col, inv_row, sign_row)


def _pair_reduce(packed):
    def body(p_ref, out_ref, got, own, send_sems, recv_sems, own_sems):
        x, y, c = _place()
        sends = [pltpu.make_async_remote_copy(
            src_ref=p_ref.at[2 * chip + (1 - c)], dst_ref=got.at[chip],
            send_sem=send_sems.at[chip], recv_sem=recv_sems.at[chip], device_id=(x, y, 1 - c), device_id_type=MESH)
            for chip in range(4)]
        loads = [pltpu.make_async_copy(p_ref.at[2 * chip + c], own.at[chip], own_sems.at[chip]) for chip in range(4)]
        for cp in sends + loads:
            cp.start()
        for chip in range(4):
            loads[chip].wait()
            sends[chip].wait_recv()
            out_ref[chip] = (own[chip].astype(F32) + got[chip].astype(F32)).astype(out_ref.dtype)
        for cp in sends:
            cp.wait_send()

    return pl.pallas_call(
        body, name="pair_reduce",
        in_specs=[_any()], out_specs=_vmem(),
        out_shape=jax.ShapeDtypeStruct((4,) + packed.shape[1:], packed.dtype),
        scratch_shapes=[pltpu.VMEM((4,) + packed.shape[1:], packed.dtype), pltpu.VMEM((4,) + packed.shape[1:], packed.dtype),
                        pltpu.SemaphoreType.DMA((4,)), pltpu.SemaphoreType.DMA((4,)), pltpu.SemaphoreType.DMA((4,))],
        compiler_params=_cp(),
    )(packed)


def _pad_cols(a, cols):
    return jnp.pad(a, ((0, 0), (0, cols - a.shape[1])))


def _pad_rows(a, rows):
    return jnp.pad(a, ((0, rows - a.shape[0]), (0, 0)))


FRAME = 928


def _join_frames(frames):
    head = frames[:, :FRAME].at[1:, :16].add(frames[:-1, FRAME:])
    return jnp.concatenate([head.reshape(NDEV * FRAME, D), frames[NDEV - 1, FRAME:]], axis=0)


def _build_wft(wt):
    q = wt[0:1024].reshape(8, 2, 2, 32, D).transpose(0, 2, 1, 3, 4).reshape(1024, D)
    k = wt[1024:1152].reshape(2, 2, 1, 32, D)
    kd = jnp.broadcast_to(k, (2, 2, 2, 32, D)).reshape(256, D)
    v = wt[1152:1280].reshape(2, 1, 64, D)
    vd = jnp.broadcast_to(v, (2, 2, 64, D)).reshape(256, D)
    ag, bq, bk = wt[1280:2304], wt[2304:2816], wt[2816:3328]
    bv, bg, bl = wt[3328:4352], wt[4352:5376], wt[5376:5392]
    ma, mb = wt[5392:6416], wt[6416:7440]
    return jnp.concatenate([q, kd, vd, _pad_rows(bl, C_GLA - C_BL), bv, bq, bk, ag, bg, ma, mb], axis=0)


def _unbuild_gwt(g):
    n = g.shape[1]
    q = g[C_Q:C_Q + 1024].reshape(8, 2, 2, 32, n).transpose(0, 2, 1, 3, 4).reshape(1024, n)
    k = g[C_KD:C_KD + 256].reshape(2, 2, 2, 32, n).sum(axis=2).reshape(128, n)
    v = g[C_VD:C_VD + 256].reshape(2, 2, 64, n).sum(axis=1).reshape(128, n)
    bv, bq, bk = g[C_BV:C_BV + 1024], g[C_BQ:C_BQ + 512], g[C_BK:C_BK + 512]
    ag, bg, ma, mb = (g[c:c + 1024] for c in (C_AG, C_BG, C_MA, C_MB))
    return jnp.concatenate([q, k, v, ag, bq, bk, bv, bg, g[C_BL:C_BL + RANK], ma, mb], axis=0)


def kernel(x, positions, norm_w, w_in, a_sinks, b_gate_up, b_gate_bias, b_out_norm_w, w_a_proj, w_b_proj, w_out, final_norm_w, loss_target, m_norm_w, m_w_in, m_a_sinks, m_b_gate_up, m_b_gate_bias, m_b_out_norm_w, m_w_a_proj, m_w_b_proj, m_w_out, m_final_norm_w, v_norm_w, v_w_in, v_a_sinks, v_b_gate_up, v_b_gate_bias, v_b_out_norm_w, v_w_a_proj, v_w_b_proj, v_w_out, v_final_norm_w):
    T = x.shape[1]
    xs, target = x[0], loss_target[0]
    fnw = final_norm_w.reshape(1, D)
    me = 4 * lax.axis_index("x") + 2 * lax.axis_index("y") + lax.axis_index("c")
    allw, h, cos, sin = _gather_blocks(w_in[0].T, b_gate_up[0], xs, norm_w, positions.reshape(T, 1))
    late_blk = jnp.concatenate([w_a_proj[0], w_b_proj[0], w_out[0]], axis=0).astype(WIRE)
    l_send, l_recv, l_blk, l_land, _ = _late_gather_start(late_blk, cos)
    wf = _build_wft(_join_frames(allw[:, :SHARD_PAD]))
    gu = allw[:, SHARD_PAD:G_ROWS, :64].transpose(1, 0, 2).reshape(RANK, 512)
    gu_pad = _pad_rows(gu, W_BL)

    proj = _proj(h, wf)
    o_a, lse = _swa_fwd(proj, cos, sin, a_sinks)
    o_b, states = _gla_fwd(proj, gu_pad, b_gate_bias)
    l_blk, l_land = _late_gather_wait(l_send, l_recv, l_blk, l_land, states)
    late = lax.dynamic_update_slice(l_land, l_blk[None], (me, 0, 0))
    w_a, w_b, w_o = (late[:, 128 * i:128 * (i + 1), :].reshape(D, D) for i in range(3))
    (dx2, do_a, do_b, d_gates, g_wa, g_wb, g_wo, g_fn, g_bn, loss_part) = _mid(
        xs, target, proj, o_a, o_b, w_a, w_b, w_o, jnp.tile(b_out_norm_w, (1, B_HEADS)), fnw)
    d_q, d_kv, g_sinks = _swa_bwd(proj, cos, sin, a_sinks, do_a, o_a, lse)
    d_gla, d_bl, g_gu, g_bias = _gla_bwd(proj, gu_pad, b_gate_bias, states, do_b)
    pieces = [d_q, d_kv, d_bl, d_gla, d_gates]
    offsets = [C_Q, C_KD, C_BL, C_GLA, C_GATES]

    ggu = g_gu[:RANK].reshape(RANK, NDEV, 64).transpose(1, 0, 2)
    ggu_half = [jnp.pad(ggu, ((0, 0), (0, 0), (0, DH - 64))), jnp.zeros((NDEV, RANK, DH), F32)]

    def pack(gw_half, hf):
        gwt = _unbuild_gwt(gw_half).astype(WIRE)
        cols = slice(hf * DH, (hf + 1) * DH)
        return jnp.concatenate([
            jnp.stack([gwt[FRAME * d:FRAME * d + SHARD_PAD] for d in range(NDEV)]),
            g_wa[:, cols].reshape(NDEV, 128, DH).astype(WIRE),
            g_wb[:, cols].reshape(NDEV, 128, DH).astype(WIRE),
            g_wo[:, cols].reshape(NDEV, 128, DH).astype(WIRE),
            ggu_half[hf].astype(WIRE)], axis=1)

    send0, recv0, s_thru0, land0, started0 = _chip_start(_pair_reduce(pack(_gw_half(h, pieces, 0), 0)), 0)
    send1, recv1, s_thru1, land1, started1 = _chip_start(
        _pair_reduce(pack(_gw_half(h, pieces, 1, after=started0), 1)), 1)
    grad_x, g_nw = _dh_norm(pieces, offsets, wf, xs, dx2, norm_w, started1)
    small = jnp.concatenate([g_nw, g_fn, _pad_cols(g_bias, D), _pad_cols(g_bn, D), _pad_cols(g_sinks, D),
                             _pad_cols(loss_part, D)], axis=0)
    smalls = _small_exchange(small)
    sums0, got0 = _chip_wait(send0, recv0, s_thru0, land0, smalls, 0)
    sums1, got1 = _chip_wait(send1, recv1, s_thru1, land1, got0, 1)
    sums, from_chips = [sums0, sums1], [got0, got1]

    ws = dict(norm_w=norm_w, fnw=fnw, bias=b_gate_bias, bn=b_out_norm_w, sinks=a_sinks)
    ms = dict(norm_w=m_norm_w, fnw=m_final_norm_w.reshape(1, D), bias=m_b_gate_bias, bn=m_b_out_norm_w,
              sinks=m_a_sinks)
    vs = dict(norm_w=v_norm_w, fnw=v_final_norm_w.reshape(1, D), bias=v_b_gate_bias, bn=v_b_out_norm_w,
              sinks=v_a_sinks)
    loss, t_rows, sm, t_gu = _finish(
        [w_in[0].T, w_a_proj[0], w_b_proj[0], w_out[0]], [m_w_in[0].T, m_w_a_proj[0], m_w_b_proj[0], m_w_out[0]],
        [v_w_in[0].T, v_w_a_proj[0], v_w_b_proj[0], v_w_out[0]],
        ws, ms, vs, b_gate_up[0], m_b_gate_up[0], v_b_gate_up[0], smalls, sums, from_chips)

    def outputs(k):
        return [sm["norm_w"][k], t_rows[k].T[None], sm["sinks"][k], t_gu[k][None], sm["bias"][k], sm["bn"][k],
                t_rows[4 + k][None], t_rows[8 + k][None], t_rows[12 + k][None], sm["fnw"][k].reshape(D)]

    return (loss[0, 0], grad_x[None], *outputs(0), *outputs(1), *outputs(2), *outputs(3))
```

```python
import functools

import numpy as np
import jax
import jax.numpy as jnp
from jax import lax
from jax.experimental import pallas as pl
from jax.experimental.pallas import tpu as pltpu

F32 = jnp.float32
MXU = jnp.bfloat16
WIRE = jnp.bfloat16

D = 1024
A_HEADS, A_KV, A_HD = 16, 2, 64
BLK = 128
B_HEADS, B_DK, B_DV = 4, 128, 256
RANK, TAU, CHUNK = 16, 16.0, 64
EPS, NEG = 1e-5, -1e30
ROPE_THETA = 10000.0
IN_WIDTH, NDEV = 7440, 8
SHARD = IN_WIDTH // NDEV
LANE = 128

C_Q, C_KD, C_VD, C_BL = 0, 1024, 1280, 1536
C_BV, C_BQ, C_BK = 2048, 3072, 3584
C_AG, C_BG, C_MA, C_MB = 4096, 5120, 6144, 7168
C_GLA, W_GLA, C_GATES, W_GATES = 2048, 2048, 4096, 4096
NF = 8192
W_BL = 128

SHARD_PAD = 944
R_IN, R_A, R_B, R_O, R_GU, ROWS = 0, 944, 1072, 1200, 1328, 1344
SMALL_ROWS = 48

ADAM_LR, ADAM_B1, ADAM_B2, ADAM_EPS, ADAM_WD, ADAM_STEP = 0.001, 0.9, 0.999, 1e-08, 0.01, 10

MESH = pl.DeviceIdType.MESH
VMEM_LIMIT = 56 * 1024 * 1024


def _cp(sem=None, **kw):
    if sem is not None:
        kw["dimension_semantics"] = sem
    return pltpu.CompilerParams(vmem_limit_bytes=VMEM_LIMIT, **kw)


def _dot(a, b):
    return jnp.dot(a, b, preferred_element_type=F32)


def _dot_nt(a, b):
    return lax.dot_general(a, b, (((1,), (1,)), ((), ())), preferred_element_type=F32)


def _dot_tn(a, b):
    return lax.dot_general(a, b, (((0,), (0,)), ((), ())), preferred_element_type=F32)


def _dot_f32(a, b):
    return jnp.dot(a, b, preferred_element_type=F32, precision=lax.Precision.HIGHEST)


def _sigmoid(z):
    return 0.5 * jnp.tanh(0.5 * z) + 0.5


def _rope(xp, cos, sin):
    return xp * cos + pltpu.roll(xp, 64, 1) * sin


def _rope_bwd(dy, cos, sin):
    return dy * cos - pltpu.roll(dy, 64, 1) * sin


def _vmem():
    return pl.BlockSpec(memory_space=pltpu.VMEM)


def _any():
    return pl.BlockSpec(memory_space=pl.ANY)


def _rope_rows():
    half = A_HD // 2
    inv = (np.float32(ROPE_THETA) ** (-np.arange(half, dtype=np.float32) / np.float32(half))).astype(np.float32)
    inv_row = jnp.asarray(np.tile(inv, 4)[None, :])
    sign_row = jnp.asarray(np.concatenate([-np.ones(64, np.float32), np.ones(64, np.float32)])[None, :])
    return inv_row, sign_row


def _prologue_rows(rows, x_ref, nw_ref, pos_ref, inv_ref, sign_ref, h_ref, cos_ref, sin_ref):
    xv = x_ref[rows, :]
    r = lax.rsqrt(jnp.mean(xv * xv, axis=-1, keepdims=True) + EPS)
    h_ref[rows, :] = ((xv * r) * nw_ref[...]).astype(h_ref.dtype)
    ang = pos_ref[rows, :].astype(F32) * inv_ref[...]
    cos_ref[rows, :] = jnp.cos(ang)
    sin_ref[rows, :] = jnp.sin(ang) * sign_ref[...]


def _proj(h, wft, after):
    T = h.shape[0]
    tT, tN = T, 512

    def body(h_ref, w_ref, after_ref, o_ref):
        o_ref[...] = _dot_nt(h_ref[...], w_ref[...])

    return pl.pallas_call(
        body, name="proj", grid=(T // tT, NF // tN),
        in_specs=[pl.BlockSpec((tT, D), lambda i, j: (i, 0)), pl.BlockSpec((tN, D), lambda i, j: (j, 0)), _any()],
        out_specs=pl.BlockSpec((tT, tN), lambda i, j: (i, j)),
        out_shape=jax.ShapeDtypeStruct((T, NF), F32),
        compiler_params=_cp(("parallel", "parallel")),
    )(h, wft, after)


def _swa_masks():
    lane = lax.broadcasted_iota(jnp.int32, (BLK, LANE), 1)
    rope_sub0 = ((lane // 32) % 2) == 0
    std_sub0 = lane < 64
    return lane, rope_sub0, std_sub0


def _swa_tri():
    qi = lax.broadcasted_iota(jnp.int32, (BLK, BLK), 0)
    kj = lax.broadcasted_iota(jnp.int32, (BLK, BLK), 1)
    return kj <= qi


def _swa_fold(full, tri):
    return jnp.where(tri, full[:, BLK:], full[:, :BLK])


def _swa_unfold(sq, tri):
    return jnp.concatenate([jnp.where(tri, 0.0, sq), jnp.where(tri, sq, 0.0)], axis=1)


def _swa_keys(kc_ref, kp_ref, vc_ref, vp_ref, cq, sq, cp, sp):
    def ropek(kref, c, s):
        kv = kref[...]
        return jnp.concatenate([_rope(kv[:, :LANE], c, s), _rope(kv[:, LANE:], c, s)], axis=1)

    K = jnp.concatenate([ropek(kp_ref, cp, sp), ropek(kc_ref, cq, sq)], axis=0).astype(MXU)
    V = jnp.concatenate([vp_ref[...], vc_ref[...]], axis=0).astype(MXU)
    return K, V


def _swa_in_specs(nb, last):
    def cur(n):
        return jnp.minimum(n, last)

    def prev(n):
        return jnp.maximum(cur(n) - 1, 0)

    kd, vd = C_KD // 256, C_VD // 256
    return [
        pl.BlockSpec((BLK, D), lambda n: (cur(n), C_Q // D)),
        pl.BlockSpec((BLK, 256), lambda n: (cur(n), kd)),
        pl.BlockSpec((BLK, 256), lambda n: (prev(n), kd)),
        pl.BlockSpec((BLK, 256), lambda n: (cur(n), vd)),
        pl.BlockSpec((BLK, 256), lambda n: (prev(n), vd)),
        pl.BlockSpec((BLK, LANE), lambda n: (cur(n), 0)),
        pl.BlockSpec((BLK, LANE), lambda n: (cur(n), 0)),
        pl.BlockSpec((BLK, LANE), lambda n: (prev(n), 0)),
        pl.BlockSpec((BLK, LANE), lambda n: (prev(n), 0)),
    ]


def _swa_fwd(proj, cos, sin, sinks):
    T = proj.shape[0]
    nb = T // BLK
    scale = A_HD ** -0.5

    def body(sinks_ref, q_ref, kc_ref, kp_ref, vc_ref, vp_ref, cq_ref, sq_ref, cp_ref, sp_ref, o_ref, l_ref):
        n = pl.program_id(0)
        cq, sq = cq_ref[...], sq_ref[...]
        K, V = _swa_keys(kc_ref, kp_ref, vc_ref, vp_ref, cq, sq, cp_ref[...], sp_ref[...])
        tri = _swa_tri()
        valid = tri | (n > 0)
        lane, rope_sub0, std_sub0 = _swa_masks()
        group = A_HEADS // A_KV
        roped, lses = {}, []

        def products(head):
            pb, sub, g = head // 2, head % 2, head // group
            if sub == 0:
                roped[pb] = _rope(q_ref[:, pb * LANE:(pb + 1) * LANE], cq, sq)
            qm = jnp.where(rope_sub0 if sub == 0 else ~rope_sub0, roped[pb], 0.0).astype(MXU)
            return _dot_nt(qm, K[:, g * LANE:(g + 1) * LANE])

        def softmax(head, s_full):
            s = jnp.where(valid, _swa_fold(s_full, tri) * scale, NEG)
            sink = sinks_ref[0, head]
            m = jnp.maximum(jnp.max(s, axis=1, keepdims=True), sink)
            e = jnp.exp(s - m)
            den = jnp.sum(e, axis=1, keepdims=True) + jnp.exp(sink - m)
            lses.append(m + jnp.log(den))
            return _swa_unfold(e / den, tri).astype(MXU)

        outs = {}
        st1 = {0: products(0), 1: products(1)}
        st2 = {0: softmax(0, st1.pop(0))}
        for head in range(A_HEADS):
            if head + 2 < A_HEADS:
                st1[head + 2] = products(head + 2)
            if head + 1 < A_HEADS:
                st2[head + 1] = softmax(head + 1, st1.pop(head + 1))
            g = head // group
            outs[head] = _dot(st2.pop(head), V[:, g * LANE:(g + 1) * LANE])
            if head % 2 == 1:
                pb = head // 2
                o_ref[:, pb * LANE:(pb + 1) * LANE] = jnp.where(std_sub0, outs[head - 1], outs[head])
        lacc = jnp.zeros((BLK, LANE), F32)
        for head in range(A_HEADS):
            lacc = jnp.where(lane == head, lses[head], lacc)
        l_ref[...] = lacc

    return pl.pallas_call(
        body, name="swa_fwd", grid=(nb,),
        in_specs=[pl.BlockSpec(memory_space=pltpu.SMEM)] + _swa_in_specs(nb, nb - 1),
        out_specs=[pl.BlockSpec((BLK, D), lambda n: (n, 0)), pl.BlockSpec((BLK, LANE), lambda n: (n, 0))],
        out_shape=[jax.ShapeDtypeStruct((T, D), F32), jax.ShapeDtypeStruct((T, LANE), F32)],
        compiler_params=_cp(("parallel",)),
    )(sinks, proj, proj, proj, proj, proj, cos, sin, cos, sin)


def _swa_bwd(proj, cos, sin, sinks, do_a, o_a, lse):
    T = proj.shape[0]
    nb = T // BLK
    scale = A_HD ** -0.5

    def body(sinks_ref, q_ref, kc_ref, kp_ref, vc_ref, vp_ref, cq_ref, sq_ref, cp_ref, sp_ref,
             do_ref, o_ref, l_ref, dq_ref, dkv_ref, ds_ref, ckv_ref):
        n = pl.program_id(0)

        @pl.when(n == 0)
        def _():
            ckv_ref[...] = jnp.zeros_like(ckv_ref)
            ds_ref[...] = jnp.zeros_like(ds_ref)

        @pl.when(n < nb)
        def _():
            cq, sq, cp, sp = cq_ref[...], sq_ref[...], cp_ref[...], sp_ref[...]
            K, V = _swa_keys(kc_ref, kp_ref, vc_ref, vp_ref, cq, sq, cp, sp)
            tri = _swa_tri()
            valid = tri | (n > 0)
            lane, rope_sub0, std_sub0 = _swa_masks()
            lane_row = lax.broadcasted_iota(jnp.int32, (1, LANE), 1)
            lse_v = l_ref[...]
            dKt = [jnp.zeros((LANE, 2 * BLK), F32) for _ in range(A_KV)]
            dVt = [jnp.zeros((LANE, 2 * BLK), F32) for _ in range(A_KV)]
            dsinks, roped, roped_t, do_t = [], {}, {}, {}
            group = A_HEADS // A_KV
            dim = lax.broadcasted_iota(jnp.int32, (LANE, BLK), 0)
            rope_row0, std_row0 = ((dim // 32) % 2) == 0, dim < 64

            def products(head):
                pb, sub, g = head // 2, head % 2, head // group
                cols = slice(pb * LANE, (pb + 1) * LANE)
                Kg, Vg = K[:, g * LANE:(g + 1) * LANE], V[:, g * LANE:(g + 1) * LANE]
                if sub == 0:
                    roped[pb] = _rope(q_ref[:, cols], cq, sq)
                    roped_t[pb] = roped[pb].T
                    do_t[pb] = do_ref[:, cols].T
                qm = jnp.where(rope_sub0 if sub == 0 else ~rope_sub0, roped[pb], 0.0).astype(MXU)
                qmt = jnp.where(rope_row0 if sub == 0 else ~rope_row0, roped_t[pb], 0.0).astype(MXU)
                dov = jnp.where(std_sub0 if sub == 0 else ~std_sub0, do_ref[:, cols], 0.0)
                dovt = jnp.where(std_row0 if sub == 0 else ~std_row0, do_t[pb], 0.0).astype(MXU)
                delta = jnp.sum(dov * o_ref[:, cols], axis=1, keepdims=True)
                return qmt, dovt, delta, _dot_nt(qm, Kg), _dot_nt(dov.astype(MXU), Vg)

            def scores(head, qmt, dovt, delta, s_full, dp_full):
                lh = jnp.sum(jnp.where(lane == head, lse_v, 0.0), axis=1, keepdims=True)
                p = jnp.where(valid, jnp.exp(_swa_fold(s_full, tri) * scale - lh), 0.0)
                psink = jnp.exp(sinks_ref[0, head] - lh)
                dsinks.append(jnp.sum(-psink * delta, axis=0, keepdims=True))
                dsq = (p * (_swa_fold(dp_full, tri) - delta)) * scale
                return qmt, dovt, _swa_unfold(p, tri).astype(MXU), _swa_unfold(dsq, tri).astype(MXU)

            def grads(head, qmt, dovt, pb16, dsc):
                g = head // group
                dKt[g] = dKt[g] + _dot(qmt, dsc)
                dVt[g] = dVt[g] + _dot(dovt, pb16)
                return _dot(dsc, K[:, g * LANE:(g + 1) * LANE])

            dqs = {}
            st1 = {0: products(0), 1: products(1)}
            st2 = {0: scores(0, *st1.pop(0))}
            for head in range(A_HEADS):
                if head + 2 < A_HEADS:
                    st1[head + 2] = products(head + 2)
                if head + 1 < A_HEADS:
                    st2[head + 1] = scores(head + 1, *st1.pop(head + 1))
                dqs[head] = grads(head, *st2.pop(head))
                if head % 2 == 1:
                    pb = head // 2
                    dqp = jnp.where(rope_sub0, dqs[head - 1], dqs[head])
                    dq_ref[:, pb * LANE:(pb + 1) * LANE] = _rope_bwd(dqp, cq, sq).astype(dq_ref.dtype)
            dsink = jnp.zeros((1, LANE), F32)
            for head in range(A_HEADS):
                dsink = jnp.where(lane_row == head, dsinks[head], dsink)
            dK, dV = [a.T for a in dKt], [a.T for a in dVt]
            prev = ([_rope_bwd(dK[g][:BLK], cp, sp) for g in range(A_KV)] + [dV[g][:BLK] for g in range(A_KV)])
            cur_ = ([_rope_bwd(dK[g][BLK:], cq, sq) for g in range(A_KV)] + [dV[g][BLK:] for g in range(A_KV)])
            dkv_ref[...] = (ckv_ref[...] + jnp.concatenate(prev, axis=1)).astype(dkv_ref.dtype)
            ckv_ref[...] = jnp.concatenate(cur_, axis=1)
            ds_ref[...] = ds_ref[...] + jnp.broadcast_to(dsink, ds_ref.shape)

        @pl.when(n == nb)
        def _():
            dkv_ref[...] = ckv_ref[...].astype(dkv_ref.dtype)

    last = nb - 1

    def cur(n):
        return jnp.minimum(n, last)

    def out_kv(n):
        return (jnp.maximum(n - 1, 0), 0)

    return pl.pallas_call(
        body, name="swa_bwd", grid=(nb + 1,),
        in_specs=[pl.BlockSpec(memory_space=pltpu.SMEM)] + _swa_in_specs(nb, last) + [
            pl.BlockSpec((BLK, D), lambda n: (cur(n), 0)),
            pl.BlockSpec((BLK, D), lambda n: (cur(n), 0)),
            pl.BlockSpec((BLK, LANE), lambda n: (cur(n), 0)),
        ],
        out_specs=[
            pl.BlockSpec((BLK, D), lambda n: (cur(n), 0)),
            pl.BlockSpec((BLK, 512), out_kv),
            pl.BlockSpec((8, LANE), lambda n: (0, 0)),
        ],
        out_shape=[
            jax.ShapeDtypeStruct((T, D), MXU),
            jax.ShapeDtypeStruct((T, 512), MXU),
            jax.ShapeDtypeStruct((8, LANE), F32),
        ],
        scratch_shapes=[pltpu.VMEM((BLK, 512), F32)],
        compiler_params=_cp(("arbitrary",)),
    )(sinks, proj, proj, proj, proj, proj, cos, sin, cos, sin, do_a, o_a, lse)


def _gla_gate(bl_ref, gu_ref, bias_ref):
    gk = _dot(bl_ref[...].astype(MXU), gu_ref[...]) + bias_ref[...]
    la = (jnp.minimum(gk, 0.0) - jnp.log(1.0 + jnp.exp(-jnp.abs(gk)))) / TAU
    ri = lax.broadcasted_iota(jnp.int32, (CHUNK, CHUNK), 0)
    ci = lax.broadcasted_iota(jnp.int32, (CHUNK, CHUNK), 1)
    b = _dot_f32(jnp.where(ci <= ri, 1.0, 0.0).astype(F32), la)
    return gk, la, b, ri, ci


def _gla_head(q_ref, k_ref, la, b, h):
    sl = slice(h * B_DK, (h + 1) * B_DK)
    bh = b[:, sl]
    blast = jnp.sum(la[:, sl], axis=0, keepdims=True)
    qc = q_ref[:, sl] * (B_DK ** -0.5)
    kh = k_ref[:, sl]
    eb, enb, esb = jnp.exp(bh), jnp.exp(-bh), jnp.exp(blast - bh)
    return qc * eb, kh * enb, kh * esb, eb, enb, esb, jnp.exp(blast)


def _gla_specs(chunk_of):
    return [
        pl.BlockSpec((CHUNK, 512), lambda i: (chunk_of(i), C_BQ // 512)),
        pl.BlockSpec((CHUNK, 512), lambda i: (chunk_of(i), C_BK // 512)),
        pl.BlockSpec((CHUNK, D), lambda i: (chunk_of(i), C_BV // D)),
        pl.BlockSpec((CHUNK, W_BL), lambda i: (chunk_of(i), C_BL // W_BL)),
        pl.BlockSpec((W_BL, 512), lambda i: (0, 0)),
        pl.BlockSpec((1, 512), lambda i: (0, 0)),
    ]


def _gla_fwd(proj, gu_pad, bias):
    T = proj.shape[0]
    nc = T // CHUNK

    def body(q_ref, k_ref, v_ref, bl_ref, gu_ref, bias_ref, o_ref, st_ref, state_ref):
        @pl.when(pl.program_id(0) == 0)
        def _():
            state_ref[...] = jnp.zeros_like(state_ref)

        _, la, b, ri, ci = _gla_gate(bl_ref, gu_ref, bias_ref)
        st_ref[...] = state_ref[...]
        for h in range(B_HEADS):
            q_e, k_e, k_s, _, _, _, decay = _gla_head(q_ref, k_ref, la, b, h)
            vh = v_ref[:, h * B_DV:(h + 1) * B_DV].astype(MXU)
            rows = slice(h * B_DV, (h + 1) * B_DV)
            q_eb = q_e.astype(MXU)
            att = jnp.where(ci <= ri, _dot_nt(q_eb, k_e.astype(MXU)), 0.0)
            st = state_ref[rows, :]
            o_ref[:, rows] = _dot(att.astype(MXU), vh) + _dot_nt(q_eb, st.astype(MXU))
            state_ref[rows, :] = st * decay + _dot_tn(vh, k_s.astype(MXU))

    return pl.pallas_call(
        body, name="gla_fwd", grid=(nc,),
        in_specs=_gla_specs(lambda i: i),
        out_specs=[pl.BlockSpec((CHUNK, D), lambda i: (i, 0)),
                   pl.BlockSpec((B_HEADS * B_DV, B_DK), lambda i: (i, 0))],
        out_shape=[jax.ShapeDtypeStruct((T, D), F32),
                   jax.ShapeDtypeStruct((nc * B_HEADS * B_DV, B_DK), F32)],
        scratch_shapes=[pltpu.VMEM((B_HEADS * B_DV, B_DK), F32)],
        compiler_params=_cp(("arbitrary",)),
    )(proj, proj, proj, proj, gu_pad, bias)


def _gla_bwd(proj, gu_pad, bias, states, do_b):
    T = proj.shape[0]
    nc = T // CHUNK
    o_q, o_k = C_BQ - C_GLA, C_BK - C_GLA

    def body(q_ref, k_ref, v_ref, bl_ref, gu_ref, bias_ref, st_ref, do_ref,
             dg_ref, dbl_ref, ggu_ref, gbias_ref, gt_ref):
        @pl.when(pl.program_id(0) == 0)
        def _():
            gt_ref[...] = jnp.zeros_like(gt_ref)
            ggu_ref[...] = jnp.zeros_like(ggu_ref)
            gbias_ref[...] = jnp.zeros_like(gbias_ref)

        gk, la, b, ri, ci = _gla_gate(bl_ref, gu_ref, bias_ref)
        causal = ci <= ri
        upper = jnp.where(ci >= ri, 1.0, 0.0).astype(F32)
        dla_parts = []
        for h in range(B_HEADS):
            q_e, k_e, k_s, eb, enb, esb, decay = _gla_head(q_ref, k_ref, la, b, h)
            rows = slice(h * B_DV, (h + 1) * B_DV)
            sl = slice(h * B_DK, (h + 1) * B_DK)
            vh = v_ref[:, rows].astype(MXU)
            doh = do_ref[:, rows].astype(MXU)
            q_eb, k_eb, k_sb = q_e.astype(MXU), k_e.astype(MXU), k_s.astype(MXU)
            st = st_ref[rows, :]
            gt = gt_ref[rows, :]
            gtb = gt.astype(MXU)
            att = jnp.where(causal, _dot_nt(q_eb, k_eb), 0.0).astype(MXU)
            datt = jnp.where(causal, _dot_nt(doh, vh), 0.0).astype(MXU)
            dq_e = _dot(datt, k_eb) + _dot(doh, st.astype(MXU))
            dk_e = _dot_tn(datt, q_eb)
            dk_s = _dot(vh, gtb)
            dg_ref[:, rows] = (_dot_tn(att, doh) + _dot_nt(k_sb, gtb)).astype(dg_ref.dtype)
            ddecay = jnp.sum(gt * st, axis=0, keepdims=True)
            gt_ref[rows, :] = gt * decay + _dot_tn(doh, q_eb)
            dg_ref[:, o_q + h * B_DK:o_q + (h + 1) * B_DK] = (dq_e * eb * (B_DK ** -0.5)).astype(dg_ref.dtype)
            dg_ref[:, o_k + h * B_DK:o_k + (h + 1) * B_DK] = (dk_e * enb + dk_s * esb).astype(dg_ref.dtype)
            dks_ks = dk_s * k_s
            db = dq_e * q_e - dk_e * k_e - dks_ks
            dblast = jnp.sum(dks_ks, axis=0, keepdims=True) + ddecay * decay
            dla_parts.append(_dot_f32(upper, db) + dblast)
        dla = jnp.concatenate(dla_parts, axis=1)
        dgk = dla * (1.0 / TAU) * _sigmoid(-gk)
        dgkb = dgk.astype(MXU)
        dbl_ref[...] = _dot_nt(dgkb, gu_ref[...]).astype(dbl_ref.dtype)
        ggu_ref[...] = ggu_ref[...] + _dot_tn(bl_ref[...].astype(MXU), dgkb)
        gbias_ref[...] = gbias_ref[...] + jnp.broadcast_to(jnp.sum(dgk, axis=0, keepdims=True), gbias_ref.shape)

    def rev(i):
        return nc - 1 - i

    return pl.pallas_call(
        body, name="gla_bwd", grid=(nc,),
        in_specs=_gla_specs(rev) + [
            pl.BlockSpec((B_HEADS * B_DV, B_DK), lambda i: (rev(i), 0)),
            pl.BlockSpec((CHUNK, D), lambda i: (rev(i), 0)),
        ],
        out_specs=[
            pl.BlockSpec((CHUNK, W_GLA), lambda i: (rev(i), 0)),
            pl.BlockSpec((CHUNK, W_BL), lambda i: (rev(i), 0)),
            pl.BlockSpec((W_BL, 512), lambda i: (0, 0)),
            pl.BlockSpec((8, 512), lambda i: (0, 0)),
        ],
        out_shape=[
            jax.ShapeDtypeStruct((T, W_GLA), MXU),
            jax.ShapeDtypeStruct((T, W_BL), MXU),
            jax.ShapeDtypeStruct((W_BL, 512), F32),
            jax.ShapeDtypeStruct((8, 512), F32),
        ],
        scratch_shapes=[pltpu.VMEM((B_HEADS * B_DV, B_DK), F32)],
        compiler_params=_cp(("arbitrary",)),
    )(proj, proj, proj, proj, gu_pad, bias, states, do_b)


def _mid(x, target, proj, o_a, o_b, w_a, w_b, w_out, w_bn4, fnw):
    T = x.shape[0]
    tT = min(T, 128)
    nbuf = 4
    o_ag, o_bg, o_ma, o_mb = (c - C_GATES for c in (C_AG, C_BG, C_MA, C_MB))

    def body(x_ref, t_ref, oa_ref, ob_ref, gates_ref, wa_ref, wb_ref, wo_ref, wbn_ref, fnw_ref,
             dx2_ref, doa_ref, dob_ref, dgates_ref,
             gwa_ref, gwb_ref, gwo_ref, gfn_ref, gbn_ref, loss_ref, buf_ref):
        i = pl.program_id(0)

        @pl.when(i == 0)
        def _():
            for r in (gwa_ref, gwb_ref, gwo_ref, gfn_ref, gbn_ref, loss_ref):
                r[...] = jnp.zeros_like(r)

        rows = pl.ds(pl.multiple_of((i % nbuf) * tT, tT), tT)

        def keep(k, val):
            buf_ref[k, rows, :] = val

        oa, ag = oa_ref[...], gates_ref[:, o_ag:o_ag + D]
        sg_a = _sigmoid(ag)
        silu_a = ag * sg_a
        oag_b = (oa * silu_a).astype(MXU)
        keep(0, oag_b)
        y_a = _dot(oag_b, wa_ref[...])

        ob, bg = ob_ref[...], gates_ref[:, o_bg:o_bg + D]
        rbs, obhats = [], []
        for h in range(B_HEADS):
            obh = ob[:, h * B_DV:(h + 1) * B_DV]
            rb = lax.rsqrt(jnp.mean(obh * obh, axis=-1, keepdims=True) + EPS)
            rbs.append(rb)
            obhats.append(obh * rb)
        obhat = jnp.concatenate(obhats, axis=1)
        wbn = wbn_ref[...]
        obn = obhat * wbn
        sg_b = _sigmoid(bg)
        silu_b = bg * sg_b
        obg_b = (obn * silu_b).astype(MXU)
        keep(1, obg_b)
        y_b = _dot(obg_b, wb_ref[...])

        sa, sb = _sigmoid(gates_ref[:, o_ma:o_ma + D]), _sigmoid(gates_ref[:, o_mb:o_mb + D])
        mg_b = (sa * y_a + sb * y_b).astype(MXU)
        keep(2, mg_b)
        x2 = x_ref[...] + _dot(mg_b, wo_ref[...])
        r2 = lax.rsqrt(jnp.mean(x2 * x2, axis=-1, keepdims=True) + EPS)
        xh2 = x2 * r2
        fw = fnw_ref[...]
        err = xh2 * fw - t_ref[...]
        tok = jnp.mean(err * err, axis=-1, keepdims=True)
        loss_ref[...] = loss_ref[...] + 0.5 * jnp.sum(tok, axis=0, keepdims=True)

        dy = err * (1.0 / D)
        gfn_ref[...] = gfn_ref[...] + jnp.broadcast_to(jnp.sum(dy * xh2, axis=0, keepdims=True), gfn_ref.shape)
        gy = dy * fw
        dx2 = r2 * (gy - xh2 * jnp.mean(gy * xh2, axis=-1, keepdims=True))
        dx2_ref[...] = dx2
        dx2_b = dx2.astype(MXU)
        keep(5, dx2_b)
        dmg = _dot_nt(dx2_b, wo_ref[...])

        dgates_ref[:, o_ma:o_ma + D] = (dmg * y_a * sa * (1.0 - sa)).astype(dgates_ref.dtype)
        dgates_ref[:, o_mb:o_mb + D] = (dmg * y_b * sb * (1.0 - sb)).astype(dgates_ref.dtype)
        dya_b = (dmg * sa).astype(MXU)
        dyb_b = (dmg * sb).astype(MXU)
        keep(3, dya_b)
        keep(4, dyb_b)
        doag = _dot_nt(dya_b, wa_ref[...])
        dobg = _dot_nt(dyb_b, wb_ref[...])

        @pl.when(i % nbuf == nbuf - 1)
        def _():
            gwa_ref[...] = gwa_ref[...] + _dot_tn(buf_ref[0], buf_ref[3])
            gwb_ref[...] = gwb_ref[...] + _dot_tn(buf_ref[1], buf_ref[4])
            gwo_ref[...] = gwo_ref[...] + _dot_tn(buf_ref[2], buf_ref[5])

        doa_ref[...] = doag * silu_a
        dgates_ref[:, o_ag:o_ag + D] = (doag * oa * (sg_a * (1.0 + ag * (1.0 - sg_a)))).astype(dgates_ref.dtype)
        dobn = dobg * silu_b
        dgates_ref[:, o_bg:o_bg + D] = (dobg * obn * (sg_b * (1.0 + bg * (1.0 - sg_b)))).astype(dgates_ref.dtype)
        gg = dobn * wbn
        gbn = jnp.zeros((1, B_DV), F32)
        for h in range(B_HEADS):
            sl = slice(h * B_DV, (h + 1) * B_DV)
            gbn = gbn + jnp.sum(dobn[:, sl] * obhats[h], axis=0, keepdims=True)
            ggh = gg[:, sl]
            dob_ref[:, sl] = rbs[h] * (ggh - obhats[h] * jnp.mean(ggh * obhats[h], axis=-1, keepdims=True))
        gbn_ref[...] = gbn_ref[...] + jnp.broadcast_to(gbn, gbn_ref.shape)

    assert (T // tT) % nbuf == 0
    tile = pl.BlockSpec((tT, D), lambda i: (i, 0))
    row = pl.BlockSpec((1, D), lambda i: (0, 0))
    acc8 = pl.BlockSpec((8, D), lambda i: (0, 0))
    return pl.pallas_call(
        body, name="mid", grid=(T // tT,),
        in_specs=[tile, tile, tile, tile, pl.BlockSpec((tT, W_GATES), lambda i: (i, C_GATES // W_GATES)),
                  _vmem(), _vmem(), _vmem(), row, row],
        out_specs=[tile, tile, tile, pl.BlockSpec((tT, W_GATES), lambda i: (i, 0)), _vmem(), _vmem(), _vmem(),
                   acc8, pl.BlockSpec((8, B_DV), lambda i: (0, 0)), pl.BlockSpec((8, LANE), lambda i: (0, 0))],
        out_shape=[
            jax.ShapeDtypeStruct((T, D), F32),
            jax.ShapeDtypeStruct((T, D), F32),
            jax.ShapeDtypeStruct((T, D), F32),
            jax.ShapeDtypeStruct((T, W_GATES), MXU),
            jax.ShapeDtypeStruct((D, D), F32),
            jax.ShapeDtypeStruct((D, D), F32),
            jax.ShapeDtypeStruct((D, D), F32),
            jax.ShapeDtypeStruct((8, D), F32),
            jax.ShapeDtypeStruct((8, B_DV), F32),
            jax.ShapeDtypeStruct((8, LANE), F32),
        ],
        scratch_shapes=[pltpu.VMEM((6, nbuf * tT, D), MXU)],
        compiler_params=_cp(("arbitrary",)),
    )(x, target, o_a, o_b, proj, w_a, w_b, w_out, w_bn4, fnw)


DH = D // 2


def _gw_half(h, pieces, half, after=None):
    T = h.shape[0]
    steps = NF // 512
    tiles = ((0, 2), (2, 3), (4, 8), (8, 16))

    def body(*refs):
        h_ref, q_ref, kv_ref, bl_ref, gla_ref, gates_ref = refs[:6]
        o_ref = refs[-1]
        j = pl.program_id(0)

        for (lo, hi), ref in zip(tiles, (q_ref, kv_ref, gla_ref, gates_ref)):
            @pl.when((j >= lo) & (j < hi))
            def _(ref=ref):
                o_ref[...] = _dot_tn(ref[...], h_ref[...])

        @pl.when(j == 3)
        def _():
            o_ref[0:W_BL, :] = _dot_tn(bl_ref[...], h_ref[...])
            o_ref[W_BL:, :] = jnp.zeros((512 - W_BL, DH), F32)

    def tile_of(lo, hi):
        return lambda j: (0, jnp.clip(j - lo, 0, hi - lo - 1))

    in_specs = [pl.BlockSpec((T, DH), lambda j: (0, half)),
                pl.BlockSpec((T, 512), tile_of(0, 2)), pl.BlockSpec((T, 512), lambda j: (0, 0)),
                pl.BlockSpec((T, W_BL), lambda j: (0, 0)),
                pl.BlockSpec((T, 512), tile_of(4, 8)), pl.BlockSpec((T, 512), tile_of(8, 16))]
    args = [h, *pieces]
    if after is not None:
        in_specs.append(_any())
        args.append(after)
    return pl.pallas_call(
        body, name=f"gw_in_half{half}", grid=(steps,),
        in_specs=in_specs, out_specs=pl.BlockSpec((512, DH), lambda j: (j, 0)),
        out_shape=jax.ShapeDtypeStruct((NF, DH), F32),
        compiler_params=_cp(("parallel",)),
    )(*args)


def _chip_copies(s_ref, got_ref, send_sems, recv_sems):
    x, y, c = _place()
    chips = [(1 - x, y), (x, 1 - y), (1 - x, 1 - y)]
    return [pltpu.make_async_remote_copy(
        src_ref=s_ref.at[2 * px + py], dst_ref=got_ref.at[j],
        send_sem=send_sems.at[j], recv_sem=recv_sems.at[j], device_id=(px, py, c), device_id_type=MESH)
        for j, (px, py) in enumerate(chips)]


_EFFECT = pltpu.SideEffectType.DATAFLOW_SIDE_EFFECTING


def _hbm():
    return pl.BlockSpec(memory_space=pltpu.HBM)


def _sem():
    return pl.BlockSpec(memory_space=pltpu.SEMAPHORE)


def _chip_start(sums, half):
    land = pltpu.with_memory_space_constraint(lax.empty((3,) + sums.shape[1:], sums.dtype), pltpu.HBM)

    def body(s_ref, land_ref, send_sems, recv_sems, s_thru, land_thru, token):
        for cp in _chip_copies(s_ref, land_ref, send_sems, recv_sems):
            cp.start()
        token[...] = jnp.zeros_like(token)

    return pl.pallas_call(
        body, name=f"chip_start{half}",
        out_shape=(pltpu.SemaphoreType.DMA((3,)), pltpu.SemaphoreType.DMA((3,)),
                   pltpu.HBM(sums.shape, sums.dtype), pltpu.HBM(land.shape, land.dtype),
                   jax.ShapeDtypeStruct((8, LANE), F32)),
        in_specs=(_hbm(), _hbm()), out_specs=(_sem(), _sem(), _hbm(), _hbm(), _vmem()),
        input_output_aliases={0: 2, 1: 3},
        compiler_params=pltpu.CompilerParams(has_side_effects=_EFFECT),
    )(pltpu.with_memory_space_constraint(sums, pltpu.HBM), land)


def _chip_wait(send_sems, recv_sems, s_thru, land_thru, after, half):
    def body(s_ref, land_ref, send_sems, recv_sems, after_ref, s_out, got_ref):
        copies = _chip_copies(s_ref, land_ref, send_sems, recv_sems)
        for cp in copies:
            cp.wait_send()
        for cp in copies:
            cp.wait_recv()

    return pl.pallas_call(
        body, name=f"chip_wait{half}",
        out_shape=(pltpu.HBM(s_thru.shape, s_thru.dtype), pltpu.HBM(land_thru.shape, land_thru.dtype)),
        in_specs=(_hbm(), _hbm(), _sem(), _sem(), _any()), out_specs=(_hbm(), _hbm()),
        input_output_aliases={0: 0, 1: 1},
        compiler_params=pltpu.CompilerParams(has_side_effects=_EFFECT),
    )(s_thru, land_thru, send_sems, recv_sems, after)


def _dh_norm(pieces, offsets, wf, x, dx2, norm_w, after):
    T = x.shape[0]
    tT = min(T, 256)
    widths = [p.shape[1] for p in pieces]
    npc = len(pieces)

    def body(*refs):
        dp_refs = refs[:npc]
        wf_ref, x_ref, dx2_ref, nw_ref, _, gx_ref, gnw_ref = refs[npc:]

        @pl.when(pl.program_id(0) == 0)
        def _():
            gnw_ref[...] = jnp.zeros_like(gnw_ref)

        dh = jnp.zeros((tT, D), F32)
        for dp_ref, off, w in zip(dp_refs, offsets, widths):
            dh = dh + _dot(dp_ref[...], wf_ref[off:off + w, :])
        xv = x_ref[...]
        r = lax.rsqrt(jnp.mean(xv * xv, axis=-1, keepdims=True) + EPS)
        xh = xv * r
        gnw_ref[...] = gnw_ref[...] + jnp.broadcast_to(jnp.sum(dh * xh, axis=0, keepdims=True), gnw_ref.shape)
        g = dh * nw_ref[...]
        gx_ref[...] = r * (g - xh * jnp.mean(g * xh, axis=-1, keepdims=True)) + dx2_ref[...]

    tile = pl.BlockSpec((tT, D), lambda i: (i, 0))
    return pl.pallas_call(
        body, name="dh_norm", grid=(T // tT,),
        in_specs=[pl.BlockSpec((tT, w), lambda i: (i, 0)) for w in widths]
        + [_vmem(), tile, tile, pl.BlockSpec((1, D), lambda i: (0, 0)), _any()],
        out_specs=[tile, pl.BlockSpec((8, D), lambda i: (0, 0))],
        out_shape=[jax.ShapeDtypeStruct((T, D), F32), jax.ShapeDtypeStruct((8, D), F32)],
        compiler_params=_cp(("arbitrary",)),
    )(*pieces, wf, x, dx2, norm_w, after)


def _adamw_math(w, g, m, v):
    m = ADAM_B1 * m + (1.0 - ADAM_B1) * g
    v = ADAM_B2 * v + (1.0 - ADAM_B2) * (g * g)
    m_hat = m / (1.0 - ADAM_B1 ** ADAM_STEP)
    v_hat = v / (1.0 - ADAM_B2 ** ADAM_STEP)
    delta = -ADAM_LR * (m_hat / (jnp.sqrt(v_hat) + ADAM_EPS) + ADAM_WD * w)
    return delta, m, v


def _fetch_partials(s_ref, got_ref, buf, sems):
    x, y, _ = _place()
    cps = [pltpu.make_async_copy(s_ref.at[2 * x + y], buf.at[0], sems.at[0])]
    cps += [pltpu.make_async_copy(got_ref.at[j], buf.at[1 + j], sems.at[1 + j]) for j in range(3)]
    for cp in cps:
        cp.start()
    for cp in cps:
        cp.wait()


SMALL_AT = dict(norm_w=0, fnw=8, bias=16, bn=24, sinks=32, loss=40)
ROW_AT = (R_IN, R_A, R_B, R_O)


def _small_exchange(small):
    def body(small_ref, out_ref, send_sems, recv_sems):
        x, y, c = _place()
        me_slot = 4 * x + 2 * y + c
        sends = []
        k = 0
        for dx in range(2):
            for dy in range(2):
                for dc in range(2):
                    if dx == 0 and dy == 0 and dc == 0:
                        continue
                    sends.append(pltpu.make_async_remote_copy(
                        src_ref=small_ref, dst_ref=out_ref.at[me_slot],
                        send_sem=send_sems.at[k], recv_sem=recv_sems.at[k],
                        device_id=(x ^ dx, y ^ dy, c ^ dc), device_id_type=MESH))
                    k += 1
        for cp in sends:
            cp.start()
        out_ref[me_slot] = small_ref[...]
        for cp in sends:
            cp.wait_recv()
        for cp in sends:
            cp.wait_send()

    return pl.pallas_call(
        body, name="small_exchange",
        in_specs=[_vmem()], out_specs=_vmem(),
        out_shape=jax.ShapeDtypeStruct((NDEV, SMALL_ROWS, D), F32),
        scratch_shapes=[pltpu.SemaphoreType.DMA((7,)), pltpu.SemaphoreType.DMA((7,))],
    )(small)


def _finish(w_rows, m_rows, v_rows, ws, ms, vs, gu_w, gu_m, gu_v, smalls, sums, got):
    names = ["norm_w", "fnw", "bias", "bn", "sinks"]
    widths = [ws[n].shape[1] for n in names]
    shapes = [w.shape for w in w_rows]

    def body(*refs):
        wr_refs, mr_refs, vr_refs = refs[0:4], refs[4:8], refs[8:12]
        refs = refs[12:]
        w_refs, m_refs, v_refs = refs[0:5], refs[5:10], refs[10:15]
        guw_ref, gum_ref, guv_ref, smalls = refs[15:19]
        s_refs, got_refs = refs[19:21], refs[21:23]
        loss_ref = refs[23]
        row_outs = refs[24:40]
        outs = refs[40:60]
        gu_outs = refs[60:64]
        tot, buf, gsh, sems = refs[64:]
        x, y, c = _place()
        me_slot = 4 * x + 2 * y + c
        unshift = lax.rem(SHARD_PAD - 2 * me_slot, SHARD_PAD)
        for hf in range(2):
            _fetch_partials(s_refs[hf], got_refs[hf], buf, sems)
            for p in range(4):
                n, off = shapes[p][0], ROW_AT[p]
                nf = SHARD_PAD if p == 0 else n
                for cc in range(DH // LANE):
                    src = slice(cc * LANE, (cc + 1) * LANE)
                    cols = slice(hf * DH + cc * LANE, hf * DH + (cc + 1) * LANE)
                    g = buf[0, off:off + nf, src].astype(F32)
                    for j in range(1, 4):
                        g = g + buf[j, off:off + nf, src].astype(F32)
                    if p == 0:
                        gsh[...] = pltpu.roll(g, unshift, 0)
                        g = gsh[0:n, :]
                    d, nm, nv = _adamw_math(wr_refs[p][:, cols], g, mr_refs[p][:, cols], vr_refs[p][:, cols])
                    for o, val in zip(row_outs[4 * p:4 * p + 4], (g, d, nm, nv)):
                        o[:, cols] = val
            if hf == 0:
                g = buf[0, R_GU:R_GU + RANK, 0:64].astype(F32)
                for j in range(1, 4):
                    g = g + buf[j, R_GU:R_GU + RANK, 0:64].astype(F32)
                d, nm, nv = _adamw_math(guw_ref[...], g, gum_ref[...], guv_ref[...])
                for o, val in zip(gu_outs, (g, d, nm, nv)):
                    o[...] = val
        acc = smalls[0]
        for d in range(1, NDEV):
            acc = acc + smalls[d]
        tot[...] = acc
        loss_ref[...] = tot[SMALL_AT["loss"]:SMALL_AT["loss"] + 1, 0:1]
        for p, (nm_, wd) in enumerate(zip(names, widths)):
            r = SMALL_AT[nm_]
            g = tot[r:r + 1, 0:wd]
            d, nm, nv = _adamw_math(w_refs[p][...], g, m_refs[p][...], v_refs[p][...])
            for o, val in zip(outs[4 * p:4 * p + 4], (g, d, nm, nv)):
                o[...] = val

    out_shape = ([jax.ShapeDtypeStruct((1, 1), F32)]
                 + [jax.ShapeDtypeStruct(s, F32) for s in shapes for _ in range(4)]
                 + [jax.ShapeDtypeStruct((1, wd), F32) for wd in widths for _ in range(4)]
                 + [jax.ShapeDtypeStruct((RANK, 64), F32)] * 4)
    res = pl.pallas_call(
        body, name="finish",
        in_specs=[_vmem()] * 31 + [_any()] * 4,
        out_specs=[_vmem()] * 41,
        out_shape=out_shape,
        scratch_shapes=[pltpu.VMEM((SMALL_ROWS, D), F32),
                        pltpu.VMEM((4, ROWS, DH), sums[0].dtype), pltpu.VMEM((SHARD_PAD, LANE), F32),
                        pltpu.SemaphoreType.DMA((4,))],
        compiler_params=_cp(),
    )(*w_rows, *m_rows, *v_rows, *[ws[n] for n in names], *[ms[n] for n in names], *[vs[n] for n in names],
      gu_w, gu_m, gu_v, smalls, *sums, *got)
    loss = res[0]
    per = {n: tuple(res[17 + 4 * p:21 + 4 * p]) for p, n in enumerate(names)}
    return loss, tuple(res[1:17]), per, tuple(res[37:41])


def _place():
    x, y, c = lax.axis_index("x"), lax.axis_index("y"), lax.axis_index("c")
    return x, y, c


def _peers(x, y, c):
    return [(x ^ dx, y ^ dy, c ^ dc) for dx in range(2) for dy in range(2) for dc in range(2) if dx + dy + dc]


def _late_gather_start(blk, after):
    land = pltpu.with_memory_space_constraint(lax.empty((NDEV,) + blk.shape, blk.dtype), pltpu.HBM)

    def body(b_ref, land_ref, after_ref, send_sems, recv_sems, b_thru, land_thru, token):
        x, y, c = _place()
        for k, to in enumerate(_peers(x, y, c)):
            pltpu.make_async_remote_copy(
                src_ref=b_ref, dst_ref=land_ref.at[4 * x + 2 * y + c], send_sem=send_sems.at[k],
                recv_sem=recv_sems.at[k], device_id=to, device_id_type=MESH).start()
        token[...] = jnp.zeros_like(token)

    return pl.pallas_call(
        body, name="late_gather_start",
        out_shape=(pltpu.SemaphoreType.DMA((7,)), pltpu.SemaphoreType.DMA((7,)),
                   pltpu.HBM(blk.shape, blk.dtype), pltpu.HBM(land.shape, land.dtype),
                   jax.ShapeDtypeStruct((8, LANE), F32)),
        in_specs=(_hbm(), _hbm(), _any()), out_specs=(_sem(), _sem(), _hbm(), _hbm(), _vmem()),
        input_output_aliases={0: 2, 1: 3},
        compiler_params=pltpu.CompilerParams(has_side_effects=_EFFECT),
    )(pltpu.with_memory_space_constraint(blk, pltpu.HBM), land, after)


def _late_gather_wait(send_sems, recv_sems, b_thru, land_thru, after, after2):
    def body(b_ref, land_ref, send_sems, recv_sems, after_ref, after2_ref, b_out, got_ref):
        x, y, c = _place()
        copies = [pltpu.make_async_remote_copy(
            src_ref=b_ref, dst_ref=land_ref.at[4 * x + 2 * y + c], send_sem=send_sems.at[k],
            recv_sem=recv_sems.at[k], device_id=to, device_id_type=MESH)
            for k, to in enumerate(_peers(x, y, c))]
        for cp in copies:
            cp.wait_send()
        for cp in copies:
            cp.wait_recv()

    return pl.pallas_call(
        body, name="late_gather_wait",
        out_shape=(pltpu.HBM(b_thru.shape, b_thru.dtype), pltpu.HBM(land_thru.shape, land_thru.dtype)),
        in_specs=(_hbm(), _hbm(), _sem(), _sem(), _any(), _any()), out_specs=(_hbm(), _hbm()),
        input_output_aliases={0: 0, 1: 1},
        compiler_params=pltpu.CompilerParams(has_side_effects=_EFFECT),
    )(b_thru, land_thru, send_sems, recv_sems, after, after2)


G_ROWS = SHARD_PAD + RANK


def _gather_blocks(w_in_t, gu_s, xs, norm_w, pos_col):
    rows, cols = G_ROWS, D
    T = xs.shape[0]
    tT = min(T, 256)
    inv_row, sign_row = _rope_rows()

    def body(wi_ref, gu_ref, xs_ref, nw_ref, pos_ref, inv_ref, sign_ref,
             out_ref, h_ref, cos_ref, sin_ref, x_ref, frame_ref, send_sems, recv_sems, local_sem):
        x, y, c = _place()
        me, sibling = (x, y, c), (x, y, 1 - c)
        chips = [(1 - x, y), (x, 1 - y), (1 - x, 1 - y)]
        shift = 2 * (4 * x + 2 * y + c)
        frame_ref[SHARD - SHARD % 8:, :] = jnp.zeros((SHARD_PAD - SHARD + SHARD % 8, D), F32)
        frame_ref[:SHARD, :] = wi_ref[...]
        for cc in range(D // LANE):
            cs = slice(cc * LANE, (cc + 1) * LANE)
            x_ref[0:SHARD_PAD, cs] = pltpu.roll(frame_ref[:, cs], shift, 0).astype(x_ref.dtype)
        x_ref[SHARD_PAD:G_ROWS, :] = jnp.zeros((RANK, D), x_ref.dtype)
        x_ref[SHARD_PAD:G_ROWS, 0:64] = gu_ref[...].astype(x_ref.dtype)

        def slot(px, py, pc):
            return out_ref.at[4 * px + 2 * py + pc]

        def copy(k, block, to, src=None):
            return pltpu.make_async_remote_copy(
                src_ref=slot(*block) if src is None else src, dst_ref=slot(*block),
                send_sem=send_sems.at[k], recv_sem=recv_sems.at[k], device_id=to, device_id_type=MESH)

        mine = pltpu.make_async_copy(x_ref, slot(*me), local_sem)
        mine.start()
        first = [copy(0, me, sibling, src=x_ref)]
        first += [copy(1 + j, me, (*chip, c), src=x_ref) for j, chip in enumerate(chips)]
        for cp in first:
            cp.start()

        @pl.loop(0, T // tT)
        def _(i):
            rows_i = pl.ds(pl.multiple_of(i * tT, tT), tT)
            _prologue_rows(rows_i, xs_ref, nw_ref, pos_ref, inv_ref, sign_ref, h_ref, cos_ref, sin_ref)

        passed = [copy(4 + j, (*chip, c), sibling) for j, chip in enumerate(chips)]
        for j, chip in enumerate(chips):
            copy(1 + j, (*chip, c), me).wait_recv()
            passed[j].start()
        copy(0, sibling, me).wait_recv()
        for j, chip in enumerate(chips):
            copy(4 + j, (*chip, 1 - c), me).wait_recv()
        for cp in first + passed:
            cp.wait_send()
        mine.wait()

    return pl.pallas_call(
        body, name="gather_weights",
        in_specs=[_vmem()] * 7, out_specs=[_any()] + [_vmem()] * 3,
        out_shape=[jax.ShapeDtypeStruct((NDEV, rows, cols), WIRE), jax.ShapeDtypeStruct((T, D), MXU),
                   jax.ShapeDtypeStruct((T, LANE), F32), jax.ShapeDtypeStruct((T, LANE), F32)],
        scratch_shapes=[pltpu.VMEM((rows, cols), WIRE), pltpu.VMEM((SHARD_PAD, D), F32),
                        pltpu.SemaphoreType.DMA((7,)), pltpu.SemaphoreType.DMA((7,)), pltpu.SemaphoreType.DMA],
        compiler_params=_cp(),
    )(w_in_t, gu_s, xs, norm_w, pos_col, inv_row, sign_row)


def _pair_reduce(packed):
    def body(p_ref, out_ref, got, own, send_sems, recv_sems, own_sems):
        x, y, c = _place()
        sends = [pltpu.make_async_remote_copy(
            src_ref=p_ref.at[2 * chip + (1 - c)], dst_ref=got.at[chip],
            send_sem=send_sems.at[chip], recv_sem=recv_sems.at[chip], device_id=(x, y, 1 - c), device_id_type=MESH)
            for chip in range(4)]
        loads = [pltpu.make_async_copy(p_ref.at[2 * chip + c], own.at[chip], own_sems.at[chip]) for chip in range(4)]
        for cp in sends + loads:
            cp.start()
        for chip in range(4):
            loads[chip].wait()
            sends[chip].wait_recv()
            out_ref[chip] = (own[chip].astype(F32) + got[chip].astype(F32)).astype(out_ref.dtype)
        for cp in sends:
            cp.wait_send()

    return pl.pallas_call(
        body, name="pair_reduce",
        in_specs=[_any()], out_specs=_vmem(),
        out_shape=jax.ShapeDtypeStruct((4,) + packed.shape[1:], packed.dtype),
        scratch_shapes=[pltpu.VMEM((4,) + packed.shape[1:], packed.dtype), pltpu.VMEM((4,) + packed.shape[1:], packed.dtype),
                        pltpu.SemaphoreType.DMA((4,)), pltpu.SemaphoreType.DMA((4,)), pltpu.SemaphoreType.DMA((4,))],
        compiler_params=_cp(),
    )(packed)


def _pad_cols(a, cols):
    return jnp.pad(a, ((0, 0), (0, cols - a.shape[1])))


def _pad_rows(a, rows):
    return jnp.pad(a, ((0, rows - a.shape[0]), (0, 0)))


FRAME = 928


def _join_frames(frames):
    head = frames[:, :FRAME].at[1:, :16].add(frames[:-1, FRAME:])
    return jnp.concatenate([head.reshape(NDEV * FRAME, D), frames[NDEV - 1, FRAME:]], axis=0)


def _build_wft(wt):
    q = wt[0:1024].reshape(8, 2, 2, 32, D).transpose(0, 2, 1, 3, 4).reshape(1024, D)
    k = wt[1024:1152].reshape(2, 2, 1, 32, D)
    kd = jnp.broadcast_to(k, (2, 2, 2, 32, D)).reshape(256, D)
    v = wt[1152:1280].reshape(2, 1, 64, D)
    vd = jnp.broadcast_to(v, (2, 2, 64, D)).reshape(256, D)
    ag, bq, bk = wt[1280:2304], wt[2304:2816], wt[2816:3328]
    bv, bg, bl = wt[3328:4352], wt[4352:5376], wt[5376:5392]
    ma, mb = wt[5392:6416], wt[6416:7440]
    return jnp.concatenate([q, kd, vd, _pad_rows(bl, C_GLA - C_BL), bv, bq, bk, ag, bg, ma, mb], axis=0)


def _unbuild_gwt(g):
    n = g.shape[1]
    q = g[C_Q:C_Q + 1024].reshape(8, 2, 2, 32, n).transpose(0, 2, 1, 3, 4).reshape(1024, n)
    k = g[C_KD:C_KD + 256].reshape(2, 2, 2, 32, n).sum(axis=2).reshape(128, n)
    v = g[C_VD:C_VD + 256].reshape(2, 2, 64, n).sum(axis=1).reshape(128, n)
    bv, bq, bk = g[C_BV:C_BV + 1024], g[C_BQ:C_BQ + 512], g[C_BK:C_BK + 512]
    ag, bg, ma, mb = (g[c:c + 1024] for c in (C_AG, C_BG, C_MA, C_MB))
    return jnp.concatenate([q, k, v, ag, bq, bk, bv, bg, g[C_BL:C_BL + RANK], ma, mb], axis=0)


def kernel(x, positions, norm_w, w_in, a_sinks, b_gate_up, b_gate_bias, b_out_norm_w, w_a_proj, w_b_proj, w_out, final_norm_w, loss_target, m_norm_w, m_w_in, m_a_sinks, m_b_gate_up, m_b_gate_bias, m_b_out_norm_w, m_w_a_proj, m_w_b_proj, m_w_out, m_final_norm_w, v_norm_w, v_w_in, v_a_sinks, v_b_gate_up, v_b_gate_bias, v_b_out_norm_w, v_w_a_proj, v_w_b_proj, v_w_out, v_final_norm_w):
    T = x.shape[1]
    xs, target = x[0], loss_target[0]
    fnw = final_norm_w.reshape(1, D)
    me = 4 * lax.axis_index("x") + 2 * lax.axis_index("y") + lax.axis_index("c")
    allw, h, cos, sin = _gather_blocks(w_in[0].T, b_gate_up[0], xs, norm_w, positions.reshape(T, 1))
    late_blk = jnp.concatenate([w_a_proj[0], w_b_proj[0], w_out[0]], axis=0).astype(WIRE)
    l_send, l_recv, l_blk, l_land, l_started = _late_gather_start(late_blk, cos)
    wf = _build_wft(_join_frames(allw[:, :SHARD_PAD]))
    gu = allw[:, SHARD_PAD:G_ROWS, :64].transpose(1, 0, 2).reshape(RANK, 512)
    gu_pad = _pad_rows(gu, W_BL)

    proj = _proj(h, wf, l_started)
    o_a, lse = _swa_fwd(proj, cos, sin, a_sinks)
    o_b, states = _gla_fwd(proj, gu_pad, b_gate_bias)
    l_blk, l_land = _late_gather_wait(l_send, l_recv, l_blk, l_land, states, lse)
    late = lax.dynamic_update_slice(l_land, l_blk[None], (me, 0, 0))
    w_a, w_b, w_o = (late[:, 128 * i:128 * (i + 1), :].reshape(D, D) for i in range(3))
    (dx2, do_a, do_b, d_gates, g_wa, g_wb, g_wo, g_fn, g_bn, loss_part) = _mid(
        xs, target, proj, o_a, o_b, w_a, w_b, w_o, jnp.tile(b_out_norm_w, (1, B_HEADS)), fnw)
    d_q, d_kv, g_sinks = _swa_bwd(proj, cos, sin, a_sinks, do_a, o_a, lse)
    d_gla, d_bl, g_gu, g_bias = _gla_bwd(proj, gu_pad, b_gate_bias, states, do_b)
    pieces = [d_q, d_kv, d_bl, d_gla, d_gates]
    offsets = [C_Q, C_KD, C_BL, C_GLA, C_GATES]

    ggu = g_gu[:RANK].reshape(RANK, NDEV, 64).transpose(1, 0, 2)
    ggu_half = [jnp.pad(ggu, ((0, 0), (0, 0), (0, DH - 64))), jnp.zeros((NDEV, RANK, DH), F32)]

    def pack(gw_half, hf):
        gwt = _unbuild_gwt(gw_half).astype(WIRE)
        cols = slice(hf * DH, (hf + 1) * DH)
        return jnp.concatenate([
            jnp.stack([gwt[FRAME * d:FRAME * d + SHARD_PAD] for d in range(NDEV)]),
            g_wa[:, cols].reshape(NDEV, 128, DH).astype(WIRE),
            g_wb[:, cols].reshape(NDEV, 128, DH).astype(WIRE),
            g_wo[:, cols].reshape(NDEV, 128, DH).astype(WIRE),
            ggu_half[hf].astype(WIRE)], axis=1)

    send0, recv0, s_thru0, land0, started0 = _chip_start(_pair_reduce(pack(_gw_half(h, pieces, 0), 0)), 0)
    send1, recv1, s_thru1, land1, started1 = _chip_start(
        _pair_reduce(pack(_gw_half(h, pieces, 1, after=started0), 1)), 1)
    grad_x, g_nw = _dh_norm(pieces, offsets, wf, xs, dx2, norm_w, started1)
    small = jnp.concatenate([g_nw, g_fn, _pad_cols(g_bias, D), _pad_cols(g_bn, D), _pad_cols(g_sinks, D),
                             _pad_cols(loss_part, D)], axis=0)
    smalls = _small_exchange(small)
    sums0, got0 = _chip_wait(send0, recv0, s_thru0, land0, smalls, 0)
    sums1, got1 = _chip_wait(send1, recv1, s_thru1, land1, got0, 1)
    sums, from_chips = [sums0, sums1], [got0, got1]

    ws = dict(norm_w=norm_w, fnw=fnw, bias=b_gate_bias, bn=b_out_norm_w, sinks=a_sinks)
    ms = dict(norm_w=m_norm_w, fnw=m_final_norm_w.reshape(1, D), bias=m_b_gate_bias, bn=m_b_out_norm_w,
              sinks=m_a_sinks)
    vs = dict(norm_w=v_norm_w, fnw=v_final_norm_w.reshape(1, D), bias=v_b_gate_bias, bn=v_b_out_norm_w,
              sinks=v_a_sinks)
    loss, t_rows, sm, t_gu = _finish(
        [w_in[0].T, w_a_proj[0], w_b_proj[0], w_out[0]], [m_w_in[0].T, m_w_a_proj[0], m_w_b_proj[0], m_w_out[0]],
        [v_w_in[0].T, v_w_a_proj[0], v_w_b_proj[0], v_w_out[0]],
        ws, ms, vs, b_gate_up[0], m_b_gate_up[0], v_b_gate_up[0], smalls, sums, from_chips)

    def outputs(k):
        return [sm["norm_w"][k], t_rows[k].T[None], sm["sinks"][k], t_gu[k][None], sm["bias"][k], sm["bn"][k],
                t_rows[4 + k][None], t_rows[8 + k][None], t_rows[12 + k][None], sm["fnw"][k].reshape(D)]

    return (loss[0, 0], grad_x[None], *outputs(0), *outputs(1), *outputs(2), *outputs(3))
```

```python
import functools

import numpy as np
import jax
import jax.numpy as jnp
from jax import lax
from jax.experimental import pallas as pl
from jax.experimental.pallas import tpu as pltpu

F32 = jnp.float32
MXU = jnp.bfloat16
WIRE = jnp.bfloat16

D = 1024
A_HEADS, A_KV, A_HD = 16, 2, 64
BLK = 128
B_HEADS, B_DK, B_DV = 4, 128, 256
RANK, TAU, CHUNK = 16, 16.0, 64
EPS, NEG = 1e-5, -1e30
ROPE_THETA = 10000.0
IN_WIDTH, NDEV = 7440, 8
SHARD = IN_WIDTH // NDEV
LANE = 128

C_Q, C_KD, C_VD, C_BL = 0, 1024, 1280, 1536
C_BV, C_BQ, C_BK = 2048, 3072, 3584
C_AG, C_BG, C_MA, C_MB = 4096, 5120, 6144, 7168
C_GLA, W_GLA, C_GATES, W_GATES = 2048, 2048, 4096, 4096
NF = 8192
W_BL = 128

SHARD_PAD = 944
R_IN, R_A, R_B, R_O, R_GU, ROWS = 0, 944, 1072, 1200, 1328, 1344
SMALL_ROWS = 48

ADAM_LR, ADAM_B1, ADAM_B2, ADAM_EPS, ADAM_WD, ADAM_STEP = 0.001, 0.9, 0.999, 1e-08, 0.01, 10

MESH = pl.DeviceIdType.MESH
VMEM_LIMIT = 56 * 1024 * 1024


def _cp(sem=None, **kw):
    if sem is not None:
        kw["dimension_semantics"] = sem
    return pltpu.CompilerParams(vmem_limit_bytes=VMEM_LIMIT, **kw)


def _dot(a, b):
    return jnp.dot(a, b, preferred_element_type=F32)


def _dot_nt(a, b):
    return lax.dot_general(a, b, (((1,), (1,)), ((), ())), preferred_element_type=F32)


def _dot_tn(a, b):
    return lax.dot_general(a, b, (((0,), (0,)), ((), ())), preferred_element_type=F32)


def _dot_f32(a, b):
    return jnp.dot(a, b, preferred_element_type=F32, precision=lax.Precision.HIGHEST)


def _sigmoid(z):
    return 0.5 * jnp.tanh(0.5 * z) + 0.5


def _rope(xp, cos, sin):
    return xp * cos + pltpu.roll(xp, 64, 1) * sin


def _rope_bwd(dy, cos, sin):
    return dy * cos - pltpu.roll(dy, 64, 1) * sin


def _vmem():
    return pl.BlockSpec(memory_space=pltpu.VMEM)


def _any():
    return pl.BlockSpec(memory_space=pl.ANY)


def _rope_rows():
    half = A_HD // 2
    inv = (np.float32(ROPE_THETA) ** (-np.arange(half, dtype=np.float32) / np.float32(half))).astype(np.float32)
    inv_row = jnp.asarray(np.tile(inv, 4)[None, :])
    sign_row = jnp.asarray(np.concatenate([-np.ones(64, np.float32), np.ones(64, np.float32)])[None, :])
    return inv_row, sign_row


def _prologue_rows(rows, x_ref, nw_ref, pos_ref, inv_ref, sign_ref, h_ref, cos_ref, sin_ref):
    xv = x_ref[rows, :]
    r = lax.rsqrt(jnp.mean(xv * xv, axis=-1, keepdims=True) + EPS)
    h_ref[rows, :] = ((xv * r) * nw_ref[...]).astype(h_ref.dtype)
    ang = pos_ref[rows, :].astype(F32) * inv_ref[...]
    cos_ref[rows, :] = jnp.cos(ang)
    sin_ref[rows, :] = jnp.sin(ang) * sign_ref[...]


def _proj(h, wft, after):
    T = h.shape[0]
    tT, tN = T, 512

    def body(h_ref, w_ref, after_ref, o_ref):
        o_ref[...] = _dot_nt(h_ref[...], w_ref[...])

    return pl.pallas_call(
        body, name="proj", grid=(T // tT, NF // tN),
        in_specs=[pl.BlockSpec((tT, D), lambda i, j: (i, 0)), pl.BlockSpec((tN, D), lambda i, j: (j, 0)), _any()],
        out_specs=pl.BlockSpec((tT, tN), lambda i, j: (i, j)),
        out_shape=jax.ShapeDtypeStruct((T, NF), F32),
        compiler_params=_cp(("parallel", "parallel")),
    )(h, wft, after)


def _swa_masks():
    lane = lax.broadcasted_iota(jnp.int32, (BLK, LANE), 1)
    rope_sub0 = ((lane // 32) % 2) == 0
    std_sub0 = lane < 64
    return lane, rope_sub0, std_sub0


def _swa_tri():
    qi = lax.broadcasted_iota(jnp.int32, (BLK, BLK), 0)
    kj = lax.broadcasted_iota(jnp.int32, (BLK, BLK), 1)
    return kj <= qi


def _swa_fold(full, tri):
    return jnp.where(tri, full[:, BLK:], full[:, :BLK])


def _swa_unfold(sq, tri):
    return jnp.concatenate([jnp.where(tri, 0.0, sq), jnp.where(tri, sq, 0.0)], axis=1)


def _swa_keys(kc_ref, kp_ref, vc_ref, vp_ref, cq, sq, cp, sp):
    def ropek(kref, c, s):
        kv = kref[...]
        return jnp.concatenate([_rope(kv[:, :LANE], c, s), _rope(kv[:, LANE:], c, s)], axis=1)

    K = jnp.concatenate([ropek(kp_ref, cp, sp), ropek(kc_ref, cq, sq)], axis=0).astype(MXU)
    V = jnp.concatenate([vp_ref[...], vc_ref[...]], axis=0).astype(MXU)
    return K, V


def _swa_in_specs(nb, last):
    def cur(n):
        return jnp.minimum(n, last)

    def prev(n):
        return jnp.maximum(cur(n) - 1, 0)

    kd, vd = C_KD // 256, C_VD // 256
    return [
        pl.BlockSpec((BLK, D), lambda n: (cur(n), C_Q // D)),
        pl.BlockSpec((BLK, 256), lambda n: (cur(n), kd)),
        pl.BlockSpec((BLK, 256), lambda n: (prev(n), kd)),
        pl.BlockSpec((BLK, 256), lambda n: (cur(n), vd)),
        pl.BlockSpec((BLK, 256), lambda n: (prev(n), vd)),
        pl.BlockSpec((BLK, LANE), lambda n: (cur(n), 0)),
        pl.BlockSpec((BLK, LANE), lambda n: (cur(n), 0)),
        pl.BlockSpec((BLK, LANE), lambda n: (prev(n), 0)),
        pl.BlockSpec((BLK, LANE), lambda n: (prev(n), 0)),
    ]


def _swa_fwd(proj, cos, sin, sinks):
    T = proj.shape[0]
    nb = T // BLK
    scale = A_HD ** -0.5

    def body(sinks_ref, q_ref, kc_ref, kp_ref, vc_ref, vp_ref, cq_ref, sq_ref, cp_ref, sp_ref, o_ref, l_ref):
        n = pl.program_id(0)
        cq, sq = cq_ref[...], sq_ref[...]
        K, V = _swa_keys(kc_ref, kp_ref, vc_ref, vp_ref, cq, sq, cp_ref[...], sp_ref[...])
        tri = _swa_tri()
        valid = tri | (n > 0)
        lane, rope_sub0, std_sub0 = _swa_masks()
        group = A_HEADS // A_KV
        roped, lses = {}, []

        def products(head):
            pb, sub, g = head // 2, head % 2, head // group
            if sub == 0:
                roped[pb] = _rope(q_ref[:, pb * LANE:(pb + 1) * LANE], cq, sq)
            qm = jnp.where(rope_sub0 if sub == 0 else ~rope_sub0, roped[pb], 0.0).astype(MXU)
            return _dot_nt(qm, K[:, g * LANE:(g + 1) * LANE])

        def softmax(head, s_full):
            s = jnp.where(valid, _swa_fold(s_full, tri) * scale, NEG)
            sink = sinks_ref[0, head]
            m = jnp.maximum(jnp.max(s, axis=1, keepdims=True), sink)
            e = jnp.exp(s - m)
            den = jnp.sum(e, axis=1, keepdims=True) + jnp.exp(sink - m)
            lses.append(m + jnp.log(den))
            return _swa_unfold(e / den, tri).astype(MXU)

        outs = {}
        st1 = {0: products(0), 1: products(1)}
        st2 = {0: softmax(0, st1.pop(0))}
        for head in range(A_HEADS):
            if head + 2 < A_HEADS:
                st1[head + 2] = products(head + 2)
            if head + 1 < A_HEADS:
                st2[head + 1] = softmax(head + 1, st1.pop(head + 1))
            g = head // group
            outs[head] = _dot(st2.pop(head), V[:, g * LANE:(g + 1) * LANE])
            if head % 2 == 1:
                pb = head // 2
                o_ref[:, pb * LANE:(pb + 1) * LANE] = jnp.where(std_sub0, outs[head - 1], outs[head])
        lacc = jnp.zeros((BLK, LANE), F32)
        for head in range(A_HEADS):
            lacc = jnp.where(lane == head, lses[head], lacc)
        l_ref[...] = lacc

    return pl.pallas_call(
        body, name="swa_fwd", grid=(nb,),
        in_specs=[pl.BlockSpec(memory_space=pltpu.SMEM)] + _swa_in_specs(nb, nb - 1),
        out_specs=[pl.BlockSpec((BLK, D), lambda n: (n, 0)), pl.BlockSpec((BLK, LANE), lambda n: (n, 0))],
        out_shape=[jax.ShapeDtypeStruct((T, D), F32), jax.ShapeDtypeStruct((T, LANE), F32)],
        compiler_params=_cp(("parallel",)),
    )(sinks, proj, proj, proj, proj, proj, cos, sin, cos, sin)


def _swa_bwd(proj, cos, sin, sinks, do_a, o_a, lse, after):
    T = proj.shape[0]
    nb = T // BLK
    scale = A_HD ** -0.5

    def body(sinks_ref, q_ref, kc_ref, kp_ref, vc_ref, vp_ref, cq_ref, sq_ref, cp_ref, sp_ref,
             do_ref, o_ref, l_ref, after_ref, dq_ref, dkv_ref, ds_ref, ckv_ref):
        n = pl.program_id(0)

        @pl.when(n == 0)
        def _():
            ckv_ref[...] = jnp.zeros_like(ckv_ref)
            ds_ref[...] = jnp.zeros_like(ds_ref)

        @pl.when(n < nb)
        def _():
            cq, sq, cp, sp = cq_ref[...], sq_ref[...], cp_ref[...], sp_ref[...]
            K, V = _swa_keys(kc_ref, kp_ref, vc_ref, vp_ref, cq, sq, cp, sp)
            tri = _swa_tri()
            valid = tri | (n > 0)
            lane, rope_sub0, std_sub0 = _swa_masks()
            lane_row = lax.broadcasted_iota(jnp.int32, (1, LANE), 1)
            lse_v = l_ref[...]
            dKt = [jnp.zeros((LANE, 2 * BLK), F32) for _ in range(A_KV)]
            dVt = [jnp.zeros((LANE, 2 * BLK), F32) for _ in range(A_KV)]
            dsinks, roped, roped_t, do_t = [], {}, {}, {}
            group = A_HEADS // A_KV
            dim = lax.broadcasted_iota(jnp.int32, (LANE, BLK), 0)
            rope_row0, std_row0 = ((dim // 32) % 2) == 0, dim < 64

            def products(head):
                pb, sub, g = head // 2, head % 2, head // group
                cols = slice(pb * LANE, (pb + 1) * LANE)
                Kg, Vg = K[:, g * LANE:(g + 1) * LANE], V[:, g * LANE:(g + 1) * LANE]
                if sub == 0:
                    roped[pb] = _rope(q_ref[:, cols], cq, sq)
                    roped_t[pb] = roped[pb].T
                    do_t[pb] = do_ref[:, cols].T
                qm = jnp.where(rope_sub0 if sub == 0 else ~rope_sub0, roped[pb], 0.0).astype(MXU)
                qmt = jnp.where(rope_row0 if sub == 0 else ~rope_row0, roped_t[pb], 0.0).astype(MXU)
                dov = jnp.where(std_sub0 if sub == 0 else ~std_sub0, do_ref[:, cols], 0.0)
                dovt = jnp.where(std_row0 if sub == 0 else ~std_row0, do_t[pb], 0.0).astype(MXU)
                delta = jnp.sum(dov * o_ref[:, cols], axis=1, keepdims=True)
                return qmt, dovt, delta, _dot_nt(qm, Kg), _dot_nt(dov.astype(MXU), Vg)

            def scores(head, qmt, dovt, delta, s_full, dp_full):
                lh = jnp.sum(jnp.where(lane == head, lse_v, 0.0), axis=1, keepdims=True)
                p = jnp.where(valid, jnp.exp(_swa_fold(s_full, tri) * scale - lh), 0.0)
                psink = jnp.exp(sinks_ref[0, head] - lh)
                dsinks.append(jnp.sum(-psink * delta, axis=0, keepdims=True))
                dsq = (p * (_swa_fold(dp_full, tri) - delta)) * scale
                return qmt, dovt, _swa_unfold(p, tri).astype(MXU), _swa_unfold(dsq, tri).astype(MXU)

            def grads(head, qmt, dovt, pb16, dsc):
                g = head // group
                dKt[g] = dKt[g] + _dot(qmt, dsc)
                dVt[g] = dVt[g] + _dot(dovt, pb16)
                return _dot(dsc, K[:, g * LANE:(g + 1) * LANE])

            dqs = {}
            st1 = {0: products(0), 1: products(1)}
            st2 = {0: scores(0, *st1.pop(0))}
            for head in range(A_HEADS):
                if head + 2 < A_HEADS:
                    st1[head + 2] = products(head + 2)
                if head + 1 < A_HEADS:
                    st2[head + 1] = scores(head + 1, *st1.pop(head + 1))
                dqs[head] = grads(head, *st2.pop(head))
                if head % 2 == 1:
                    pb = head // 2
                    dqp = jnp.where(rope_sub0, dqs[head - 1], dqs[head])
                    dq_ref[:, pb * LANE:(pb + 1) * LANE] = _rope_bwd(dqp, cq, sq).astype(dq_ref.dtype)
            dsink = jnp.zeros((1, LANE), F32)
            for head in range(A_HEADS):
                dsink = jnp.where(lane_row == head, dsinks[head], dsink)
            dK, dV = [a.T for a in dKt], [a.T for a in dVt]
            prev = ([_rope_bwd(dK[g][:BLK], cp, sp) for g in range(A_KV)] + [dV[g][:BLK] for g in range(A_KV)])
            cur_ = ([_rope_bwd(dK[g][BLK:], cq, sq) for g in range(A_KV)] + [dV[g][BLK:] for g in range(A_KV)])
            dkv_ref[...] = (ckv_ref[...] + jnp.concatenate(prev, axis=1)).astype(dkv_ref.dtype)
            ckv_ref[...] = jnp.concatenate(cur_, axis=1)
            ds_ref[...] = ds_ref[...] + jnp.broadcast_to(dsink, ds_ref.shape)

        @pl.when(n == nb)
        def _():
            dkv_ref[...] = ckv_ref[...].astype(dkv_ref.dtype)

    last = nb - 1

    def cur(n):
        return jnp.minimum(n, last)

    def out_kv(n):
        return (jnp.maximum(n - 1, 0), 0)

    return pl.pallas_call(
        body, name="swa_bwd", grid=(nb + 1,),
        in_specs=[pl.BlockSpec(memory_space=pltpu.SMEM)] + _swa_in_specs(nb, last) + [
            pl.BlockSpec((BLK, D), lambda n: (cur(n), 0)),
            pl.BlockSpec((BLK, D), lambda n: (cur(n), 0)),
            pl.BlockSpec((BLK, LANE), lambda n: (cur(n), 0)),
            _any(),
        ],
        out_specs=[
            pl.BlockSpec((BLK, D), lambda n: (cur(n), 0)),
            pl.BlockSpec((BLK, 512), out_kv),
            pl.BlockSpec((8, LANE), lambda n: (0, 0)),
        ],
        out_shape=[
            jax.ShapeDtypeStruct((T, D), MXU),
            jax.ShapeDtypeStruct((T, 512), MXU),
            jax.ShapeDtypeStruct((8, LANE), F32),
        ],
        scratch_shapes=[pltpu.VMEM((BLK, 512), F32)],
        compiler_params=_cp(("arbitrary",)),
    )(sinks, proj, proj, proj, proj, proj, cos, sin, cos, sin, do_a, o_a, lse, after)


def _gla_gate(bl_ref, gu_ref, bias_ref):
    gk = _dot(bl_ref[...].astype(MXU), gu_ref[...]) + bias_ref[...]
    la = (jnp.minimum(gk, 0.0) - jnp.log(1.0 + jnp.exp(-jnp.abs(gk)))) / TAU
    ri = lax.broadcasted_iota(jnp.int32, (CHUNK, CHUNK), 0)
    ci = lax.broadcasted_iota(jnp.int32, (CHUNK, CHUNK), 1)
    b = _dot_f32(jnp.where(ci <= ri, 1.0, 0.0).astype(F32), la)
    return gk, la, b, ri, ci


def _gla_head(q_ref, k_ref, la, b, h):
    sl = slice(h * B_DK, (h + 1) * B_DK)
    bh = b[:, sl]
    blast = jnp.sum(la[:, sl], axis=0, keepdims=True)
    qc = q_ref[:, sl] * (B_DK ** -0.5)
    kh = k_ref[:, sl]
    eb, enb, esb = jnp.exp(bh), jnp.exp(-bh), jnp.exp(blast - bh)
    return qc * eb, kh * enb, kh * esb, eb, enb, esb, jnp.exp(blast)


def _gla_specs(chunk_of):
    return [
        pl.BlockSpec((CHUNK, 512), lambda i: (chunk_of(i), C_BQ // 512)),
        pl.BlockSpec((CHUNK, 512), lambda i: (chunk_of(i), C_BK // 512)),
        pl.BlockSpec((CHUNK, D), lambda i: (chunk_of(i), C_BV // D)),
        pl.BlockSpec((CHUNK, W_BL), lambda i: (chunk_of(i), C_BL // W_BL)),
        pl.BlockSpec((W_BL, 512), lambda i: (0, 0)),
        pl.BlockSpec((1, 512), lambda i: (0, 0)),
    ]


def _gla_fwd(proj, gu_pad, bias):
    T = proj.shape[0]
    nc = T // CHUNK

    def body(q_ref, k_ref, v_ref, bl_ref, gu_ref, bias_ref, o_ref, st_ref, state_ref):
        @pl.when(pl.program_id(0) == 0)
        def _():
            state_ref[...] = jnp.zeros_like(state_ref)

        _, la, b, ri, ci = _gla_gate(bl_ref, gu_ref, bias_ref)
        st_ref[...] = state_ref[...]
        for h in range(B_HEADS):
            q_e, k_e, k_s, _, _, _, decay = _gla_head(q_ref, k_ref, la, b, h)
            vh = v_ref[:, h * B_DV:(h + 1) * B_DV].astype(MXU)
            rows = slice(h * B_DV, (h + 1) * B_DV)
            q_eb = q_e.astype(MXU)
            att = jnp.where(ci <= ri, _dot_nt(q_eb, k_e.astype(MXU)), 0.0)
            st = state_ref[rows, :]
            o_ref[:, rows] = _dot(att.astype(MXU), vh) + _dot_nt(q_eb, st.astype(MXU))
            state_ref[rows, :] = st * decay + _dot_tn(vh, k_s.astype(MXU))

    return pl.pallas_call(
        body, name="gla_fwd", grid=(nc,),
        in_specs=_gla_specs(lambda i: i),
        out_specs=[pl.BlockSpec((CHUNK, D), lambda i: (i, 0)),
                   pl.BlockSpec((B_HEADS * B_DV, B_DK), lambda i: (i, 0))],
        out_shape=[jax.ShapeDtypeStruct((T, D), F32),
                   jax.ShapeDtypeStruct((nc * B_HEADS * B_DV, B_DK), F32)],
        scratch_shapes=[pltpu.VMEM((B_HEADS * B_DV, B_DK), F32)],
        compiler_params=_cp(("arbitrary",)),
    )(proj, proj, proj, proj, gu_pad, bias)


def _gla_bwd(proj, gu_pad, bias, states, do_b):
    T = proj.shape[0]
    nc = T // CHUNK
    o_q, o_k = C_BQ - C_GLA, C_BK - C_GLA

    def body(q_ref, k_ref, v_ref, bl_ref, gu_ref, bias_ref, st_ref, do_ref,
             dg_ref, dbl_ref, ggu_ref, gbias_ref, gt_ref):
        @pl.when(pl.program_id(0) == 0)
        def _():
            gt_ref[...] = jnp.zeros_like(gt_ref)
            ggu_ref[...] = jnp.zeros_like(ggu_ref)
            gbias_ref[...] = jnp.zeros_like(gbias_ref)

        gk, la, b, ri, ci = _gla_gate(bl_ref, gu_ref, bias_ref)
        causal = ci <= ri
        upper = jnp.where(ci >= ri, 1.0, 0.0).astype(F32)
        dla_parts = []
        for h in range(B_HEADS):
            q_e, k_e, k_s, eb, enb, esb, decay = _gla_head(q_ref, k_ref, la, b, h)
            rows = slice(h * B_DV, (h + 1) * B_DV)
            sl = slice(h * B_DK, (h + 1) * B_DK)
            vh = v_ref[:, rows].astype(MXU)
            doh = do_ref[:, rows].astype(MXU)
            q_eb, k_eb, k_sb = q_e.astype(MXU), k_e.astype(MXU), k_s.astype(MXU)
            st = st_ref[rows, :]
            gt = gt_ref[rows, :]
            gtb = gt.astype(MXU)
            att = jnp.where(causal, _dot_nt(q_eb, k_eb), 0.0).astype(MXU)
            datt = jnp.where(causal, _dot_nt(doh, vh), 0.0).astype(MXU)
            dq_e = _dot(datt, k_eb) + _dot(doh, st.astype(MXU))
            dk_e = _dot_tn(datt, q_eb)
            dk_s = _dot(vh, gtb)
            dg_ref[:, rows] = (_dot_tn(att, doh) + _dot_nt(k_sb, gtb)).astype(dg_ref.dtype)
            ddecay = jnp.sum(gt * st, axis=0, keepdims=True)
            gt_ref[rows, :] = gt * decay + _dot_tn(doh, q_eb)
            dg_ref[:, o_q + h * B_DK:o_q + (h + 1) * B_DK] = (dq_e * eb * (B_DK ** -0.5)).astype(dg_ref.dtype)
            dg_ref[:, o_k + h * B_DK:o_k + (h + 1) * B_DK] = (dk_e * enb + dk_s * esb).astype(dg_ref.dtype)
            dks_ks = dk_s * k_s
            db = dq_e * q_e - dk_e * k_e - dks_ks
            dblast = jnp.sum(dks_ks, axis=0, keepdims=True) + ddecay * decay
            dla_parts.append(_dot_f32(upper, db) + dblast)
        dla = jnp.concatenate(dla_parts, axis=1)
        dgk = dla * (1.0 / TAU) * _sigmoid(-gk)
        dgkb = dgk.astype(MXU)
        dbl_ref[...] = _dot_nt(dgkb, gu_ref[...]).astype(dbl_ref.dtype)
        ggu_ref[...] = ggu_ref[...] + _dot_tn(bl_ref[...].astype(MXU), dgkb)
        gbias_ref[...] = gbias_ref[...] + jnp.broadcast_to(jnp.sum(dgk, axis=0, keepdims=True), gbias_ref.shape)

    def rev(i):
        return nc - 1 - i

    return pl.pallas_call(
        body, name="gla_bwd", grid=(nc,),
        in_specs=_gla_specs(rev) + [
            pl.BlockSpec((B_HEADS * B_DV, B_DK), lambda i: (rev(i), 0)),
            pl.BlockSpec((CHUNK, D), lambda i: (rev(i), 0)),
        ],
        out_specs=[
            pl.BlockSpec((CHUNK, W_GLA), lambda i: (rev(i), 0)),
            pl.BlockSpec((CHUNK, W_BL), lambda i: (rev(i), 0)),
            pl.BlockSpec((W_BL, 512), lambda i: (0, 0)),
            pl.BlockSpec((8, 512), lambda i: (0, 0)),
        ],
        out_shape=[
            jax.ShapeDtypeStruct((T, W_GLA), MXU),
            jax.ShapeDtypeStruct((T, W_BL), MXU),
            jax.ShapeDtypeStruct((W_BL, 512), F32),
            jax.ShapeDtypeStruct((8, 512), F32),
        ],
        scratch_shapes=[pltpu.VMEM((B_HEADS * B_DV, B_DK), F32)],
        compiler_params=_cp(("arbitrary",)),
    )(proj, proj, proj, proj, gu_pad, bias, states, do_b)


def _mid(x, target, proj, o_a, o_b, w_a, w_b, w_out, w_bn4, fnw):
    T = x.shape[0]
    tT = min(T, 128)
    nbuf = 4
    o_ag, o_bg, o_ma, o_mb = (c - C_GATES for c in (C_AG, C_BG, C_MA, C_MB))

    def body(x_ref, t_ref, oa_ref, ob_ref, gates_ref, wa_ref, wb_ref, wo_ref, wbn_ref, fnw_ref,
             dx2_ref, doa_ref, dob_ref, dgates_ref,
             gwa_ref, gwb_ref, gwo_ref, gfn_ref, gbn_ref, loss_ref, buf_ref):
        i = pl.program_id(0)

        @pl.when(i == 0)
        def _():
            for r in (gwa_ref, gwb_ref, gwo_ref, gfn_ref, gbn_ref, loss_ref):
                r[...] = jnp.zeros_like(r)

        rows = pl.ds(pl.multiple_of((i % nbuf) * tT, tT), tT)

        def keep(k, val):
            buf_ref[k, rows, :] = val

        oa, ag = oa_ref[...], gates_ref[:, o_ag:o_ag + D]
        sg_a = _sigmoid(ag)
        silu_a = ag * sg_a
        oag_b = (oa * silu_a).astype(MXU)
        keep(0, oag_b)
        y_a = _dot(oag_b, wa_ref[...])

        ob, bg = ob_ref[...], gates_ref[:, o_bg:o_bg + D]
        rbs, obhats = [], []
        for h in range(B_HEADS):
            obh = ob[:, h * B_DV:(h + 1) * B_DV]
            rb = lax.rsqrt(jnp.mean(obh * obh, axis=-1, keepdims=True) + EPS)
            rbs.append(rb)
            obhats.append(obh * rb)
        obhat = jnp.concatenate(obhats, axis=1)
        wbn = wbn_ref[...]
        obn = obhat * wbn
        sg_b = _sigmoid(bg)
        silu_b = bg * sg_b
        obg_b = (obn * silu_b).astype(MXU)
        keep(1, obg_b)
        y_b = _dot(obg_b, wb_ref[...])

        sa, sb = _sigmoid(gates_ref[:, o_ma:o_ma + D]), _sigmoid(gates_ref[:, o_mb:o_mb + D])
        mg_b = (sa * y_a + sb * y_b).astype(MXU)
        keep(2, mg_b)
        x2 = x_ref[...] + _dot(mg_b, wo_ref[...])
        r2 = lax.rsqrt(jnp.mean(x2 * x2, axis=-1, keepdims=True) + EPS)
        xh2 = x2 * r2
        fw = fnw_ref[...]
        err = xh2 * fw - t_ref[...]
        tok = jnp.mean(err * err, axis=-1, keepdims=True)
        loss_ref[...] = loss_ref[...] + 0.5 * jnp.sum(tok, axis=0, keepdims=True)

        dy = err * (1.0 / D)
        gfn_ref[...] = gfn_ref[...] + jnp.broadcast_to(jnp.sum(dy * xh2, axis=0, keepdims=True), gfn_ref.shape)
        gy = dy * fw
        dx2 = r2 * (gy - xh2 * jnp.mean(gy * xh2, axis=-1, keepdims=True))
        dx2_ref[...] = dx2
        dx2_b = dx2.astype(MXU)
        keep(5, dx2_b)
        dmg = _dot_nt(dx2_b, wo_ref[...])

        dgates_ref[:, o_ma:o_ma + D] = (dmg * y_a * sa * (1.0 - sa)).astype(dgates_ref.dtype)
        dgates_ref[:, o_mb:o_mb + D] = (dmg * y_b * sb * (1.0 - sb)).astype(dgates_ref.dtype)
        dya_b = (dmg * sa).astype(MXU)
        dyb_b = (dmg * sb).astype(MXU)
        keep(3, dya_b)
        keep(4, dyb_b)
        doag = _dot_nt(dya_b, wa_ref[...])
        dobg = _dot_nt(dyb_b, wb_ref[...])

        @pl.when(i % nbuf == nbuf - 1)
        def _():
            gwa_ref[...] = gwa_ref[...] + _dot_tn(buf_ref[0], buf_ref[3])
            gwb_ref[...] = gwb_ref[...] + _dot_tn(buf_ref[1], buf_ref[4])
            gwo_ref[...] = gwo_ref[...] + _dot_tn(buf_ref[2], buf_ref[5])

        doa_ref[...] = doag * silu_a
        dgates_ref[:, o_ag:o_ag + D] = (doag * oa * (sg_a * (1.0 + ag * (1.0 - sg_a)))).astype(dgates_ref.dtype)
        dobn = dobg * silu_b
        dgates_ref[:, o_bg:o_bg + D] = (dobg * obn * (sg_b * (1.0 + bg * (1.0 - sg_b)))).astype(dgates_ref.dtype)
        gg = dobn * wbn
        gbn = jnp.zeros((1, B_DV), F32)
        for h in range(B_HEADS):
            sl = slice(h * B_DV, (h + 1) * B_DV)
            gbn = gbn + jnp.sum(dobn[:, sl] * obhats[h], axis=0, keepdims=True)
            ggh = gg[:, sl]
            dob_ref[:, sl] = rbs[h] * (ggh - obhats[h] * jnp.mean(ggh * obhats[h], axis=-1, keepdims=True))
        gbn_ref[...] = gbn_ref[...] + jnp.broadcast_to(gbn, gbn_ref.shape)

    assert (T // tT) % nbuf == 0
    tile = pl.BlockSpec((tT, D), lambda i: (i, 0))
    row = pl.BlockSpec((1, D), lambda i: (0, 0))
    acc8 = pl.BlockSpec((8, D), lambda i: (0, 0))
    return pl.pallas_call(
        body, name="mid", grid=(T // tT,),
        in_specs=[tile, tile, tile, tile, pl.BlockSpec((tT, W_GATES), lambda i: (i, C_GATES // W_GATES)),
                  _vmem(), _vmem(), _vmem(), row, row],
        out_specs=[tile, tile, tile, pl.BlockSpec((tT, W_GATES), lambda i: (i, 0)), _vmem(), _vmem(), _vmem(),
                   acc8, pl.BlockSpec((8, B_DV), lambda i: (0, 0)), pl.BlockSpec((8, LANE), lambda i: (0, 0))],
        out_shape=[
            jax.ShapeDtypeStruct((T, D), F32),
            jax.ShapeDtypeStruct((T, D), F32),
            jax.ShapeDtypeStruct((T, D), F32),
            jax.ShapeDtypeStruct((T, W_GATES), MXU),
            jax.ShapeDtypeStruct((D, D), F32),
            jax.ShapeDtypeStruct((D, D), F32),
            jax.ShapeDtypeStruct((D, D), F32),
            jax.ShapeDtypeStruct((8, D), F32),
            jax.ShapeDtypeStruct((8, B_DV), F32),
            jax.ShapeDtypeStruct((8, LANE), F32),
        ],
        scratch_shapes=[pltpu.VMEM((6, nbuf * tT, D), MXU)],
        compiler_params=_cp(("arbitrary",)),
    )(x, target, o_a, o_b, proj, w_a, w_b, w_out, w_bn4, fnw)


DH = D // 2


def _gw_half(h, pieces, half, after=None):
    T = h.shape[0]
    steps = NF // 512
    tiles = ((0, 2), (2, 3), (4, 8), (8, 16))

    def body(*refs):
        h_ref, q_ref, kv_ref, bl_ref, gla_ref, gates_ref = refs[:6]
        o_ref = refs[-1]
        j = pl.program_id(0)

        for (lo, hi), ref in zip(tiles, (q_ref, kv_ref, gla_ref, gates_ref)):
            @pl.when((j >= lo) & (j < hi))
            def _(ref=ref):
                o_ref[...] = _dot_tn(ref[...], h_ref[...])

        @pl.when(j == 3)
        def _():
            o_ref[0:W_BL, :] = _dot_tn(bl_ref[...], h_ref[...])
            o_ref[W_BL:, :] = jnp.zeros((512 - W_BL, DH), F32)

    def tile_of(lo, hi):
        return lambda j: (0, jnp.clip(j - lo, 0, hi - lo - 1))

    in_specs = [pl.BlockSpec((T, DH), lambda j: (0, half)),
                pl.BlockSpec((T, 512), tile_of(0, 2)), pl.BlockSpec((T, 512), lambda j: (0, 0)),
                pl.BlockSpec((T, W_BL), lambda j: (0, 0)),
                pl.BlockSpec((T, 512), tile_of(4, 8)), pl.BlockSpec((T, 512), tile_of(8, 16))]
    args = [h, *pieces]
    if after is not None:
        in_specs.append(_any())
        args.append(after)
    return pl.pallas_call(
        body, name=f"gw_in_half{half}", grid=(steps,),
        in_specs=in_specs, out_specs=pl.BlockSpec((512, DH), lambda j: (j, 0)),
        out_shape=jax.ShapeDtypeStruct((NF, DH), F32),
        compiler_params=_cp(("parallel",)),
    )(*args)


def _chip_copies(s_ref, got_ref, send_sems, recv_sems):
    x, y, c = _place()
    chips = [(1 - x, y), (x, 1 - y), (1 - x, 1 - y)]
    return [pltpu.make_async_remote_copy(
        src_ref=s_ref.at[2 * px + py], dst_ref=got_ref.at[j],
        send_sem=send_sems.at[j], recv_sem=recv_sems.at[j], device_id=(px, py, c), device_id_type=MESH)
        for j, (px, py) in enumerate(chips)]


_EFFECT = pltpu.SideEffectType.DATAFLOW_SIDE_EFFECTING


def _hbm():
    return pl.BlockSpec(memory_space=pltpu.HBM)


def _sem():
    return pl.BlockSpec(memory_space=pltpu.SEMAPHORE)


def _chip_start(sums, half):
    land = pltpu.with_memory_space_constraint(lax.empty((3,) + sums.shape[1:], sums.dtype), pltpu.HBM)

    def body(s_ref, land_ref, send_sems, recv_sems, s_thru, land_thru, token):
        for cp in _chip_copies(s_ref, land_ref, send_sems, recv_sems):
            cp.start()
        token[...] = jnp.zeros_like(token)

    return pl.pallas_call(
        body, name=f"chip_start{half}",
        out_shape=(pltpu.SemaphoreType.DMA((3,)), pltpu.SemaphoreType.DMA((3,)),
                   pltpu.HBM(sums.shape, sums.dtype), pltpu.HBM(land.shape, land.dtype),
                   jax.ShapeDtypeStruct((8, LANE), F32)),
        in_specs=(_hbm(), _hbm()), out_specs=(_sem(), _sem(), _hbm(), _hbm(), _vmem()),
        input_output_aliases={0: 2, 1: 3},
        compiler_params=pltpu.CompilerParams(has_side_effects=_EFFECT),
    )(pltpu.with_memory_space_constraint(sums, pltpu.HBM), land)


def _chip_wait(send_sems, recv_sems, s_thru, land_thru, after, half):
    def body(s_ref, land_ref, send_sems, recv_sems, after_ref, s_out, got_ref):
        copies = _chip_copies(s_ref, land_ref, send_sems, recv_sems)
        for cp in copies:
            cp.wait_send()
        for cp in copies:
            cp.wait_recv()

    return pl.pallas_call(
        body, name=f"chip_wait{half}",
        out_shape=(pltpu.HBM(s_thru.shape, s_thru.dtype), pltpu.HBM(land_thru.shape, land_thru.dtype)),
        in_specs=(_hbm(), _hbm(), _sem(), _sem(), _any()), out_specs=(_hbm(), _hbm()),
        input_output_aliases={0: 0, 1: 1},
        compiler_params=pltpu.CompilerParams(has_side_effects=_EFFECT),
    )(s_thru, land_thru, send_sems, recv_sems, after)


def _dh_norm(pieces, offsets, wf, x, dx2, norm_w, after):
    T = x.shape[0]
    tT = min(T, 256)
    widths = [p.shape[1] for p in pieces]
    npc = len(pieces)

    def body(*refs):
        dp_refs = refs[:npc]
        wf_ref, x_ref, dx2_ref, nw_ref, _, gx_ref, gnw_ref = refs[npc:]

        @pl.when(pl.program_id(0) == 0)
        def _():
            gnw_ref[...] = jnp.zeros_like(gnw_ref)

        dh = jnp.zeros((tT, D), F32)
        for dp_ref, off, w in zip(dp_refs, offsets, widths):
            dh = dh + _dot(dp_ref[...], wf_ref[off:off + w, :])
        xv = x_ref[...]
        r = lax.rsqrt(jnp.mean(xv * xv, axis=-1, keepdims=True) + EPS)
        xh = xv * r
        gnw_ref[...] = gnw_ref[...] + jnp.broadcast_to(jnp.sum(dh * xh, axis=0, keepdims=True), gnw_ref.shape)
        g = dh * nw_ref[...]
        gx_ref[...] = r * (g - xh * jnp.mean(g * xh, axis=-1, keepdims=True)) + dx2_ref[...]

    tile = pl.BlockSpec((tT, D), lambda i: (i, 0))
    return pl.pallas_call(
        body, name="dh_norm", grid=(T // tT,),
        in_specs=[pl.BlockSpec((tT, w), lambda i: (i, 0)) for w in widths]
        + [_vmem(), tile, tile, pl.BlockSpec((1, D), lambda i: (0, 0)), _any()],
        out_specs=[tile, pl.BlockSpec((8, D), lambda i: (0, 0))],
        out_shape=[jax.ShapeDtypeStruct((T, D), F32), jax.ShapeDtypeStruct((8, D), F32)],
        compiler_params=_cp(("arbitrary",)),
    )(*pieces, wf, x, dx2, norm_w, after)


def _adamw_math(w, g, m, v):
    m = ADAM_B1 * m + (1.0 - ADAM_B1) * g
    v = ADAM_B2 * v + (1.0 - ADAM_B2) * (g * g)
    m_hat = m / (1.0 - ADAM_B1 ** ADAM_STEP)
    v_hat = v / (1.0 - ADAM_B2 ** ADAM_STEP)
    delta = -ADAM_LR * (m_hat / (jnp.sqrt(v_hat) + ADAM_EPS) + ADAM_WD * w)
    return delta, m, v


def _fetch_partials(s_ref, got_ref, buf, sems):
    x, y, _ = _place()
    cps = [pltpu.make_async_copy(s_ref.at[2 * x + y], buf.at[0], sems.at[0])]
    cps += [pltpu.make_async_copy(got_ref.at[j], buf.at[1 + j], sems.at[1 + j]) for j in range(3)]
    for cp in cps:
        cp.start()
    for cp in cps:
        cp.wait()


SMALL_AT = dict(norm_w=0, fnw=8, bias=16, bn=24, sinks=32, loss=40)
ROW_AT = (R_IN, R_A, R_B, R_O)


def _small_exchange(small):
    def body(small_ref, out_ref, send_sems, recv_sems):
        x, y, c = _place()
        me_slot = 4 * x + 2 * y + c
        sends = []
        k = 0
        for dx in range(2):
            for dy in range(2):
                for dc in range(2):
                    if dx == 0 and dy == 0 and dc == 0:
                        continue
                    sends.append(pltpu.make_async_remote_copy(
                        src_ref=small_ref, dst_ref=out_ref.at[me_slot],
                        send_sem=send_sems.at[k], recv_sem=recv_sems.at[k],
                        device_id=(x ^ dx, y ^ dy, c ^ dc), device_id_type=MESH))
                    k += 1
        for cp in sends:
            cp.start()
        out_ref[me_slot] = small_ref[...]
        for cp in sends:
            cp.wait_recv()
        for cp in sends:
            cp.wait_send()

    return pl.pallas_call(
        body, name="small_exchange",
        in_specs=[_vmem()], out_specs=_vmem(),
        out_shape=jax.ShapeDtypeStruct((NDEV, SMALL_ROWS, D), F32),
        scratch_shapes=[pltpu.SemaphoreType.DMA((7,)), pltpu.SemaphoreType.DMA((7,))],
    )(small)


def _finish(w_rows, m_rows, v_rows, ws, ms, vs, gu_w, gu_m, gu_v, smalls, sums, got):
    names = ["norm_w", "fnw", "bias", "bn", "sinks"]
    widths = [ws[n].shape[1] for n in names]
    shapes = [w.shape for w in w_rows]

    def body(*refs):
        wr_refs, mr_refs, vr_refs = refs[0:4], refs[4:8], refs[8:12]
        refs = refs[12:]
        w_refs, m_refs, v_refs = refs[0:5], refs[5:10], refs[10:15]
        guw_ref, gum_ref, guv_ref, smalls = refs[15:19]
        s_refs, got_refs = refs[19:22], refs[22:25]
        loss_ref = refs[25]
        row_outs = refs[26:42]
        outs = refs[42:62]
        gu_outs = refs[62:66]
        tot, buf, bufp, gsh, sems = refs[66:]
        x, y, c = _place()
        me_slot = 4 * x + 2 * y + c
        unshift = lax.rem(SHARD_PAD - 2 * me_slot, SHARD_PAD)

        def update(p, g, cols):
            d, nm, nv = _adamw_math(wr_refs[p][:, cols], g, mr_refs[p][:, cols], vr_refs[p][:, cols])
            for o, val in zip(row_outs[4 * p:4 * p + 4], (g, d, nm, nv)):
                o[:, cols] = val

        def total(b, rows, cols):
            g = b[0, rows, cols].astype(F32)
            for j in range(1, 4):
                g = g + b[j, rows, cols].astype(F32)
            return g

        _fetch_partials(s_refs[2], got_refs[2], bufp, sems)
        for p in range(1, 4):
            for cc in range(D // LANE):
                cols = slice(cc * LANE, (cc + 1) * LANE)
                update(p, total(bufp, slice(128 * (p - 1), 128 * p), cols), cols)
        for hf in range(2):
            _fetch_partials(s_refs[hf], got_refs[hf], buf, sems)
            for cc in range(DH // LANE):
                src = slice(cc * LANE, (cc + 1) * LANE)
                gsh[...] = pltpu.roll(total(buf, slice(0, SHARD_PAD), src), unshift, 0)
                update(0, gsh[0:SHARD, :], slice(hf * DH + cc * LANE, hf * DH + (cc + 1) * LANE))
            if hf == 0:
                g = total(buf, slice(SHARD_PAD, G_ROWS), slice(0, 64))
                d, nm, nv = _adamw_math(guw_ref[...], g, gum_ref[...], guv_ref[...])
                for o, val in zip(gu_outs, (g, d, nm, nv)):
                    o[...] = val
        acc = smalls[0]
        for d in range(1, NDEV):
            acc = acc + smalls[d]
        tot[...] = acc
        loss_ref[...] = tot[SMALL_AT["loss"]:SMALL_AT["loss"] + 1, 0:1]
        for p, (nm_, wd) in enumerate(zip(names, widths)):
            r = SMALL_AT[nm_]
            g = tot[r:r + 1, 0:wd]
            d, nm, nv = _adamw_math(w_refs[p][...], g, m_refs[p][...], v_refs[p][...])
            for o, val in zip(outs[4 * p:4 * p + 4], (g, d, nm, nv)):
                o[...] = val

    out_shape = ([jax.ShapeDtypeStruct((1, 1), F32)]
                 + [jax.ShapeDtypeStruct(s, F32) for s in shapes for _ in range(4)]
                 + [jax.ShapeDtypeStruct((1, wd), F32) for wd in widths for _ in range(4)]
                 + [jax.ShapeDtypeStruct((RANK, 64), F32)] * 4)
    res = pl.pallas_call(
        body, name="finish",
        in_specs=[_vmem()] * 31 + [_any()] * 6,
        out_specs=[_vmem()] * 41,
        out_shape=out_shape,
        scratch_shapes=[pltpu.VMEM((SMALL_ROWS, D), F32),
                        pltpu.VMEM((4, G_ROWS, DH), sums[0].dtype), pltpu.VMEM((4, 384, D), sums[2].dtype),
                        pltpu.VMEM((SHARD_PAD, LANE), F32), pltpu.SemaphoreType.DMA((4,))],
        compiler_params=_cp(),
    )(*w_rows, *m_rows, *v_rows, *[ws[n] for n in names], *[ms[n] for n in names], *[vs[n] for n in names],
      gu_w, gu_m, gu_v, smalls, *sums, *got)
    loss = res[0]
    per = {n: tuple(res[17 + 4 * p:21 + 4 * p]) for p, n in enumerate(names)}
    return loss, tuple(res[1:17]), per, tuple(res[37:41])


def _place():
    x, y, c = lax.axis_index("x"), lax.axis_index("y"), lax.axis_index("c")
    return x, y, c


def _peers(x, y, c):
    return [(x ^ dx, y ^ dy, c ^ dc) for dx in range(2) for dy in range(2) for dc in range(2) if dx + dy + dc]


def _late_gather_start(blk, after):
    land = pltpu.with_memory_space_constraint(lax.empty((NDEV,) + blk.shape, blk.dtype), pltpu.HBM)

    def body(b_ref, land_ref, after_ref, send_sems, recv_sems, b_thru, land_thru, token):
        x, y, c = _place()
        for k, to in enumerate(_peers(x, y, c)):
            pltpu.make_async_remote_copy(
                src_ref=b_ref, dst_ref=land_ref.at[4 * x + 2 * y + c], send_sem=send_sems.at[k],
                recv_sem=recv_sems.at[k], device_id=to, device_id_type=MESH).start()
        token[...] = jnp.zeros_like(token)

    return pl.pallas_call(
        body, name="late_gather_start",
        out_shape=(pltpu.SemaphoreType.DMA((7,)), pltpu.SemaphoreType.DMA((7,)),
                   pltpu.HBM(blk.shape, blk.dtype), pltpu.HBM(land.shape, land.dtype),
                   jax.ShapeDtypeStruct((8, LANE), F32)),
        in_specs=(_hbm(), _hbm(), _any()), out_specs=(_sem(), _sem(), _hbm(), _hbm(), _vmem()),
        input_output_aliases={0: 2, 1: 3},
        compiler_params=pltpu.CompilerParams(has_side_effects=_EFFECT),
    )(pltpu.with_memory_space_constraint(blk, pltpu.HBM), land, after)


def _late_gather_wait(send_sems, recv_sems, b_thru, land_thru, after, after2):
    def body(b_ref, land_ref, send_sems, recv_sems, after_ref, after2_ref, b_out, got_ref):
        x, y, c = _place()
        copies = [pltpu.make_async_remote_copy(
            src_ref=b_ref, dst_ref=land_ref.at[4 * x + 2 * y + c], send_sem=send_sems.at[k],
            recv_sem=recv_sems.at[k], device_id=to, device_id_type=MESH)
            for k, to in enumerate(_peers(x, y, c))]
        for cp in copies:
            cp.wait_send()
        for cp in copies:
            cp.wait_recv()

    return pl.pallas_call(
        body, name="late_gather_wait",
        out_shape=(pltpu.HBM(b_thru.shape, b_thru.dtype), pltpu.HBM(land_thru.shape, land_thru.dtype)),
        in_specs=(_hbm(), _hbm(), _sem(), _sem(), _any(), _any()), out_specs=(_hbm(), _hbm()),
        input_output_aliases={0: 0, 1: 1},
        compiler_params=pltpu.CompilerParams(has_side_effects=_EFFECT),
    )(b_thru, land_thru, send_sems, recv_sems, after, after2)


G_ROWS = SHARD_PAD + RANK


def _gather_blocks(w_in_t, gu_s, xs, norm_w, pos_col):
    rows, cols = G_ROWS, D
    T = xs.shape[0]
    tT = min(T, 256)
    inv_row, sign_row = _rope_rows()

    def body(wi_ref, gu_ref, xs_ref, nw_ref, pos_ref, inv_ref, sign_ref,
             out_ref, h_ref, cos_ref, sin_ref, x_ref, frame_ref, send_sems, recv_sems, local_sem):
        x, y, c = _place()
        me, sibling = (x, y, c), (x, y, 1 - c)
        chips = [(1 - x, y), (x, 1 - y), (1 - x, 1 - y)]
        shift = 2 * (4 * x + 2 * y + c)
        frame_ref[SHARD - SHARD % 8:, :] = jnp.zeros((SHARD_PAD - SHARD + SHARD % 8, D), F32)
        frame_ref[:SHARD, :] = wi_ref[...]
        for cc in range(D // LANE):
            cs = slice(cc * LANE, (cc + 1) * LANE)
            x_ref[0:SHARD_PAD, cs] = pltpu.roll(frame_ref[:, cs], shift, 0).astype(x_ref.dtype)
        x_ref[SHARD_PAD:G_ROWS, :] = jnp.zeros((RANK, D), x_ref.dtype)
        x_ref[SHARD_PAD:G_ROWS, 0:64] = gu_ref[...].astype(x_ref.dtype)

        def slot(px, py, pc):
            return out_ref.at[4 * px + 2 * py + pc]

        def copy(k, block, to, src=None):
            return pltpu.make_async_remote_copy(
                src_ref=slot(*block) if src is None else src, dst_ref=slot(*block),
                send_sem=send_sems.at[k], recv_sem=recv_sems.at[k], device_id=to, device_id_type=MESH)

        mine = pltpu.make_async_copy(x_ref, slot(*me), local_sem)
        mine.start()
        first = [copy(0, me, sibling, src=x_ref)]
        first += [copy(1 + j, me, (*chip, c), src=x_ref) for j, chip in enumerate(chips)]
        for cp in first:
            cp.start()

        @pl.loop(0, T // tT)
        def _(i):
            rows_i = pl.ds(pl.multiple_of(i * tT, tT), tT)
            _prologue_rows(rows_i, xs_ref, nw_ref, pos_ref, inv_ref, sign_ref, h_ref, cos_ref, sin_ref)

        passed = [copy(4 + j, (*chip, c), sibling) for j, chip in enumerate(chips)]
        for j, chip in enumerate(chips):
            copy(1 + j, (*chip, c), me).wait_recv()
            passed[j].start()
        copy(0, sibling, me).wait_recv()
        for j, chip in enumerate(chips):
            copy(4 + j, (*chip, 1 - c), me).wait_recv()
        for cp in first + passed:
            cp.wait_send()
        mine.wait()

    return pl.pallas_call(
        body, name="gather_weights",
        in_specs=[_vmem()] * 7, out_specs=[_any()] + [_vmem()] * 3,
        out_shape=[jax.ShapeDtypeStruct((NDEV, rows, cols), WIRE), jax.ShapeDtypeStruct((T, D), MXU),
                   jax.ShapeDtypeStruct((T, LANE), F32), jax.ShapeDtypeStruct((T, LANE), F32)],
        scratch_shapes=[pltpu.VMEM((rows, cols), WIRE), pltpu.VMEM((SHARD_PAD, D), F32),
                        pltpu.SemaphoreType.DMA((7,)), pltpu.SemaphoreType.DMA((7,)), pltpu.SemaphoreType.DMA],
        compiler_params=_cp(),
    )(w_in_t, gu_s, xs, norm_w, pos_col, inv_row, sign_row)


def _pair_reduce(packed):
    def body(p_ref, out_ref, got, own, send_sems, recv_sems, own_sems):
        x, y, c = _place()
        sends = [pltpu.make_async_remote_copy(
            src_ref=p_ref.at[2 * chip + (1 - c)], dst_ref=got.at[chip],
            send_sem=send_sems.at[chip], recv_sem=recv_sems.at[chip], device_id=(x, y, 1 - c), device_id_type=MESH)
            for chip in range(4)]
        loads = [pltpu.make_async_copy(p_ref.at[2 * chip + c], own.at[chip], own_sems.at[chip]) for chip in range(4)]
        for cp in sends + loads:
            cp.start()
        for chip in range(4):
            loads[chip].wait()
            sends[chip].wait_recv()
            out_ref[chip] = (own[chip].astype(F32) + got[chip].astype(F32)).astype(out_ref.dtype)
        for cp in sends:
            cp.wait_send()

    return pl.pallas_call(
        body, name="pair_reduce",
        in_specs=[_any()], out_specs=_vmem(),
        out_shape=jax.ShapeDtypeStruct((4,) + packed.shape[1:], packed.dtype),
        scratch_shapes=[pltpu.VMEM((4,) + packed.shape[1:], packed.dtype), pltpu.VMEM((4,) + packed.shape[1:], packed.dtype),
                        pltpu.SemaphoreType.DMA((4,)), pltpu.SemaphoreType.DMA((4,)), pltpu.SemaphoreType.DMA((4,))],
        compiler_params=_cp(),
    )(packed)


def _pad_cols(a, cols):
    return jnp.pad(a, ((0, 0), (0, cols - a.shape[1])))


def _pad_rows(a, rows):
    return jnp.pad(a, ((0, rows - a.shape[0]), (0, 0)))


FRAME = 928


def _join_frames(frames):
    head = frames[:, :FRAME].at[1:, :16].add(frames[:-1, FRAME:])
    return jnp.concatenate([head.reshape(NDEV * FRAME, D), frames[NDEV - 1, FRAME:]], axis=0)


def _build_wft(wt):
    q = wt[0:1024].reshape(8, 2, 2, 32, D).transpose(0, 2, 1, 3, 4).reshape(1024, D)
    k = wt[1024:1152].reshape(2, 2, 1, 32, D)
    kd = jnp.broadcast_to(k, (2, 2, 2, 32, D)).reshape(256, D)
    v = wt[1152:1280].reshape(2, 1, 64, D)
    vd = jnp.broadcast_to(v, (2, 2, 64, D)).reshape(256, D)
    ag, bq, bk = wt[1280:2304], wt[2304:2816], wt[2816:3328]
    bv, bg, bl = wt[3328:4352], wt[4352:5376], wt[5376:5392]
    ma, mb = wt[5392:6416], wt[6416:7440]
    return jnp.concatenate([q, kd, vd, _pad_rows(bl, C_GLA - C_BL), bv, bq, bk, ag, bg, ma, mb], axis=0)


def _unbuild_gwt(g):
    n = g.shape[1]
    q = g[C_Q:C_Q + 1024].reshape(8, 2, 2, 32, n).transpose(0, 2, 1, 3, 4).reshape(1024, n)
    k = g[C_KD:C_KD + 256].reshape(2, 2, 2, 32, n).sum(axis=2).reshape(128, n)
    v = g[C_VD:C_VD + 256].reshape(2, 2, 64, n).sum(axis=1).reshape(128, n)
    bv, bq, bk = g[C_BV:C_BV + 1024], g[C_BQ:C_BQ + 512], g[C_BK:C_BK + 512]
    ag, bg, ma, mb = (g[c:c + 1024] for c in (C_AG, C_BG, C_MA, C_MB))
    return jnp.concatenate([q, k, v, ag, bq, bk, bv, bg, g[C_BL:C_BL + RANK], ma, mb], axis=0)


def kernel(x, positions, norm_w, w_in, a_sinks, b_gate_up, b_gate_bias, b_out_norm_w, w_a_proj, w_b_proj, w_out, final_norm_w, loss_target, m_norm_w, m_w_in, m_a_sinks, m_b_gate_up, m_b_gate_bias, m_b_out_norm_w, m_w_a_proj, m_w_b_proj, m_w_out, m_final_norm_w, v_norm_w, v_w_in, v_a_sinks, v_b_gate_up, v_b_gate_bias, v_b_out_norm_w, v_w_a_proj, v_w_b_proj, v_w_out, v_final_norm_w):
    T = x.shape[1]
    xs, target = x[0], loss_target[0]
    fnw = final_norm_w.reshape(1, D)
    me = 4 * lax.axis_index("x") + 2 * lax.axis_index("y") + lax.axis_index("c")
    allw, h, cos, sin = _gather_blocks(w_in[0].T, b_gate_up[0], xs, norm_w, positions.reshape(T, 1))
    late_blk = jnp.concatenate([w_a_proj[0], w_b_proj[0], w_out[0]], axis=0).astype(WIRE)
    l_send, l_recv, l_blk, l_land, l_started = _late_gather_start(late_blk, cos)
    wf = _build_wft(_join_frames(allw[:, :SHARD_PAD]))
    gu = allw[:, SHARD_PAD:G_ROWS, :64].transpose(1, 0, 2).reshape(RANK, 512)
    gu_pad = _pad_rows(gu, W_BL)

    proj = _proj(h, wf, l_started)
    o_a, lse = _swa_fwd(proj, cos, sin, a_sinks)
    o_b, states = _gla_fwd(proj, gu_pad, b_gate_bias)
    l_blk, l_land = _late_gather_wait(l_send, l_recv, l_blk, l_land, states, lse)
    late = lax.dynamic_update_slice(l_land, l_blk[None], (me, 0, 0))
    w_a, w_b, w_o = (late[:, 128 * i:128 * (i + 1), :].reshape(D, D) for i in range(3))
    (dx2, do_a, do_b, d_gates, g_wa, g_wb, g_wo, g_fn, g_bn, loss_part) = _mid(
        xs, target, proj, o_a, o_b, w_a, w_b, w_o, jnp.tile(b_out_norm_w, (1, B_HEADS)), fnw)
    packed_p = jnp.concatenate([g.reshape(NDEV, 128, D) for g in (g_wa, g_wb, g_wo)], axis=1).astype(WIRE)
    send_p, recv_p, s_thru_p, land_p, started_p = _chip_start(_pair_reduce(packed_p), "p")
    d_q, d_kv, g_sinks = _swa_bwd(proj, cos, sin, a_sinks, do_a, o_a, lse, started_p)
    d_gla, d_bl, g_gu, g_bias = _gla_bwd(proj, gu_pad, b_gate_bias, states, do_b)
    pieces = [d_q, d_kv, d_bl, d_gla, d_gates]
    offsets = [C_Q, C_KD, C_BL, C_GLA, C_GATES]

    ggu = g_gu[:RANK].reshape(RANK, NDEV, 64).transpose(1, 0, 2)
    ggu_half = [jnp.pad(ggu, ((0, 0), (0, 0), (0, DH - 64))), jnp.zeros((NDEV, RANK, DH), F32)]

    def pack(gw_half, hf):
        gwt = _unbuild_gwt(gw_half).astype(WIRE)
        return jnp.concatenate([
            jnp.stack([gwt[FRAME * d:FRAME * d + SHARD_PAD] for d in range(NDEV)]),
            ggu_half[hf].astype(WIRE)], axis=1)

    send0, recv0, s_thru0, land0, started0 = _chip_start(_pair_reduce(pack(_gw_half(h, pieces, 0), 0)), 0)
    send1, recv1, s_thru1, land1, started1 = _chip_start(
        _pair_reduce(pack(_gw_half(h, pieces, 1, after=started0), 1)), 1)
    grad_x, g_nw = _dh_norm(pieces, offsets, wf, xs, dx2, norm_w, started1)
    small = jnp.concatenate([g_nw, g_fn, _pad_cols(g_bias, D), _pad_cols(g_bn, D), _pad_cols(g_sinks, D),
                             _pad_cols(loss_part, D)], axis=0)
    smalls = _small_exchange(small)
    sums0, got0 = _chip_wait(send0, recv0, s_thru0, land0, smalls, 0)
    sums1, got1 = _chip_wait(send1, recv1, s_thru1, land1, got0, 1)
    sums_p, got_p = _chip_wait(send_p, recv_p, s_thru_p, land_p, got1, "p")
    sums, from_chips = [sums0, sums1, sums_p], [got0, got1, got_p]

    ws = dict(norm_w=norm_w, fnw=fnw, bias=b_gate_bias, bn=b_out_norm_w, sinks=a_sinks)
    ms = dict(norm_w=m_norm_w, fnw=m_final_norm_w.reshape(1, D), bias=m_b_gate_bias, bn=m_b_out_norm_w,
              sinks=m_a_sinks)
    vs = dict(norm_w=v_norm_w, fnw=v_final_norm_w.reshape(1, D), bias=v_b_gate_bias, bn=v_b_out_norm_w,
              sinks=v_a_sinks)
    loss, t_rows, sm, t_gu = _finish(
        [w_in[0].T, w_a_proj[0], w_b_proj[0], w_out[0]], [m_w_in[0].T, m_w_a_proj[0], m_w_b_proj[0], m_w_out[0]],
        [v_w_in[0].T, v_w_a_proj[0], v_w_b_proj[0], v_w_out[0]],
        ws, ms, vs, b_gate_up[0], m_b_gate_up[0], v_b_gate_up[0], smalls, sums, from_chips)

    def outputs(k):
        return [sm["norm_w"][k], t_rows[k].T[None], sm["sinks"][k], t_gu[k][None], sm["bias"][k], sm["bn"][k],
                t_rows[4 + k][None], t_rows[8 + k][None], t_rows[12 + k][None], sm["fnw"][k].reshape(D)]

    return (loss[0, 0], grad_x[None], *outputs(0), *outputs(1), *outputs(2), *outputs(3))
```

```python
import functools

import numpy as np
import jax
import jax.numpy as jnp
from jax import lax
from jax.experimental import pallas as pl
from jax.experimental.pallas import tpu as pltpu

F32 = jnp.float32
MXU = jnp.bfloat16
WIRE = jnp.bfloat16

D = 1024
A_HEADS, A_KV, A_HD = 16, 2, 64
BLK = 128
B_HEADS, B_DK, B_DV = 4, 128, 256
RANK, TAU, CHUNK = 16, 16.0, 64
EPS, NEG = 1e-5, -1e30
ROPE_THETA = 10000.0
IN_WIDTH, NDEV = 7440, 8
SHARD = IN_WIDTH // NDEV
LANE = 128

C_Q, C_KD, C_VD, C_BL = 0, 1024, 1280, 1536
C_BV, C_BQ, C_BK = 2048, 3072, 3584
C_AG, C_BG, C_MA, C_MB = 4096, 5120, 6144, 7168
C_GLA, W_GLA, C_GATES, W_GATES = 2048, 2048, 4096, 4096
NF = 8192
W_BL = 128

SHARD_PAD = 944
R_IN, R_A, R_B, R_O, R_GU, ROWS = 0, 944, 1072, 1200, 1328, 1344
SMALL_ROWS = 48

ADAM_LR, ADAM_B1, ADAM_B2, ADAM_EPS, ADAM_WD, ADAM_STEP = 0.001, 0.9, 0.999, 1e-08, 0.01, 10

MESH = pl.DeviceIdType.MESH
VMEM_LIMIT = 56 * 1024 * 1024


def _cp(sem=None, **kw):
    if sem is not None:
        kw["dimension_semantics"] = sem
    return pltpu.CompilerParams(vmem_limit_bytes=VMEM_LIMIT, **kw)


def _dot(a, b):
    return jnp.dot(a, b, preferred_element_type=F32)


def _dot_nt(a, b):
    return lax.dot_general(a, b, (((1,), (1,)), ((), ())), preferred_element_type=F32)


def _dot_tn(a, b):
    return lax.dot_general(a, b, (((0,), (0,)), ((), ())), preferred_element_type=F32)


def _dot_f32(a, b):
    return jnp.dot(a, b, preferred_element_type=F32, precision=lax.Precision.HIGHEST)


def _sigmoid(z):
    return 0.5 * jnp.tanh(0.5 * z) + 0.5


def _rope(xp, cos, sin):
    return xp * cos + pltpu.roll(xp, 64, 1) * sin


def _rope_bwd(dy, cos, sin):
    return dy * cos - pltpu.roll(dy, 64, 1) * sin


def _vmem():
    return pl.BlockSpec(memory_space=pltpu.VMEM)


def _any():
    return pl.BlockSpec(memory_space=pl.ANY)


def _rope_rows():
    half = A_HD // 2
    inv = (np.float32(ROPE_THETA) ** (-np.arange(half, dtype=np.float32) / np.float32(half))).astype(np.float32)
    inv_row = jnp.asarray(np.tile(inv, 4)[None, :])
    sign_row = jnp.asarray(np.concatenate([-np.ones(64, np.float32), np.ones(64, np.float32)])[None, :])
    return inv_row, sign_row


def _prologue_rows(rows, x_ref, nw_ref, pos_ref, inv_ref, sign_ref, h_ref, cos_ref, sin_ref):
    xv = x_ref[rows, :]
    r = lax.rsqrt(jnp.mean(xv * xv, axis=-1, keepdims=True) + EPS)
    h_ref[rows, :] = ((xv * r) * nw_ref[...]).astype(h_ref.dtype)
    ang = pos_ref[rows, :].astype(F32) * inv_ref[...]
    cos_ref[rows, :] = jnp.cos(ang)
    sin_ref[rows, :] = jnp.sin(ang) * sign_ref[...]


def _proj(h, wft, after):
    T = h.shape[0]
    tT, tN = T, 512

    def body(h_ref, w_ref, after_ref, o_ref):
        o_ref[...] = _dot_nt(h_ref[...], w_ref[...])

    return pl.pallas_call(
        body, name="proj", grid=(T // tT, NF // tN),
        in_specs=[pl.BlockSpec((tT, D), lambda i, j: (i, 0)), pl.BlockSpec((tN, D), lambda i, j: (j, 0)), _any()],
        out_specs=pl.BlockSpec((tT, tN), lambda i, j: (i, j)),
        out_shape=jax.ShapeDtypeStruct((T, NF), F32),
        compiler_params=_cp(("parallel", "parallel")),
    )(h, wft, after)


def _swa_masks():
    lane = lax.broadcasted_iota(jnp.int32, (BLK, LANE), 1)
    rope_sub0 = ((lane // 32) % 2) == 0
    std_sub0 = lane < 64
    return lane, rope_sub0, std_sub0


def _swa_tri():
    qi = lax.broadcasted_iota(jnp.int32, (BLK, BLK), 0)
    kj = lax.broadcasted_iota(jnp.int32, (BLK, BLK), 1)
    return kj <= qi


def _swa_fold(full, tri):
    return jnp.where(tri, full[:, BLK:], full[:, :BLK])


def _swa_unfold(sq, tri):
    return jnp.concatenate([jnp.where(tri, 0.0, sq), jnp.where(tri, sq, 0.0)], axis=1)


def _swa_keys(kc_ref, kp_ref, vc_ref, vp_ref, cq, sq, cp, sp):
    def ropek(kref, c, s):
        kv = kref[...]
        return jnp.concatenate([_rope(kv[:, :LANE], c, s), _rope(kv[:, LANE:], c, s)], axis=1)

    K = jnp.concatenate([ropek(kp_ref, cp, sp), ropek(kc_ref, cq, sq)], axis=0).astype(MXU)
    V = jnp.concatenate([vp_ref[...], vc_ref[...]], axis=0).astype(MXU)
    return K, V


def _swa_in_specs(nb, last):
    def cur(n):
        return jnp.minimum(n, last)

    def prev(n):
        return jnp.maximum(cur(n) - 1, 0)

    kd, vd = C_KD // 256, C_VD // 256
    return [
        pl.BlockSpec((BLK, D), lambda n: (cur(n), C_Q // D)),
        pl.BlockSpec((BLK, 256), lambda n: (cur(n), kd)),
        pl.BlockSpec((BLK, 256), lambda n: (prev(n), kd)),
        pl.BlockSpec((BLK, 256), lambda n: (cur(n), vd)),
        pl.BlockSpec((BLK, 256), lambda n: (prev(n), vd)),
        pl.BlockSpec((BLK, LANE), lambda n: (cur(n), 0)),
        pl.BlockSpec((BLK, LANE), lambda n: (cur(n), 0)),
        pl.BlockSpec((BLK, LANE), lambda n: (prev(n), 0)),
        pl.BlockSpec((BLK, LANE), lambda n: (prev(n), 0)),
    ]


def _swa_fwd(proj, cos, sin, sinks):
    T = proj.shape[0]
    nb = T // BLK
    scale = A_HD ** -0.5

    def body(sinks_ref, q_ref, kc_ref, kp_ref, vc_ref, vp_ref, cq_ref, sq_ref, cp_ref, sp_ref, o_ref, l_ref):
        n = pl.program_id(0)
        cq, sq = cq_ref[...], sq_ref[...]
        K, V = _swa_keys(kc_ref, kp_ref, vc_ref, vp_ref, cq, sq, cp_ref[...], sp_ref[...])
        tri = _swa_tri()
        valid = tri | (n > 0)
        lane, rope_sub0, std_sub0 = _swa_masks()
        group = A_HEADS // A_KV
        roped, lses = {}, []

        def products(head):
            pb, sub, g = head // 2, head % 2, head // group
            if sub == 0:
                roped[pb] = _rope(q_ref[:, pb * LANE:(pb + 1) * LANE], cq, sq)
            qm = jnp.where(rope_sub0 if sub == 0 else ~rope_sub0, roped[pb], 0.0).astype(MXU)
            return _dot_nt(qm, K[:, g * LANE:(g + 1) * LANE])

        def softmax(head, s_full):
            s = jnp.where(valid, _swa_fold(s_full, tri) * scale, NEG)
            sink = sinks_ref[0, head]
            m = jnp.maximum(jnp.max(s, axis=1, keepdims=True), sink)
            e = jnp.exp(s - m)
            den = jnp.sum(e, axis=1, keepdims=True) + jnp.exp(sink - m)
            lses.append(m + jnp.log(den))
            return _swa_unfold(e / den, tri).astype(MXU)

        outs = {}
        st1 = {0: products(0), 1: products(1)}
        st2 = {0: softmax(0, st1.pop(0))}
        for head in range(A_HEADS):
            if head + 2 < A_HEADS:
                st1[head + 2] = products(head + 2)
            if head + 1 < A_HEADS:
                st2[head + 1] = softmax(head + 1, st1.pop(head + 1))
            g = head // group
            outs[head] = _dot(st2.pop(head), V[:, g * LANE:(g + 1) * LANE])
            if head % 2 == 1:
                pb = head // 2
                o_ref[:, pb * LANE:(pb + 1) * LANE] = jnp.where(std_sub0, outs[head - 1], outs[head])
        lacc = jnp.zeros((BLK, LANE), F32)
        for head in range(A_HEADS):
            lacc = jnp.where(lane == head, lses[head], lacc)
        l_ref[...] = lacc

    return pl.pallas_call(
        body, name="swa_fwd", grid=(nb,),
        in_specs=[pl.BlockSpec(memory_space=pltpu.SMEM)] + _swa_in_specs(nb, nb - 1),
        out_specs=[pl.BlockSpec((BLK, D), lambda n: (n, 0)), pl.BlockSpec((BLK, LANE), lambda n: (n, 0))],
        out_shape=[jax.ShapeDtypeStruct((T, D), F32), jax.ShapeDtypeStruct((T, LANE), F32)],
        compiler_params=_cp(("parallel",)),
    )(sinks, proj, proj, proj, proj, proj, cos, sin, cos, sin)


def _swa_bwd(proj, cos, sin, sinks, do_a, o_a, lse, after):
    T = proj.shape[0]
    nb = T // BLK
    scale = A_HD ** -0.5

    def body(sinks_ref, q_ref, kc_ref, kp_ref, vc_ref, vp_ref, cq_ref, sq_ref, cp_ref, sp_ref,
             do_ref, o_ref, l_ref, after_ref, dq_ref, dkv_ref, ds_ref, ckv_ref):
        n = pl.program_id(0)

        @pl.when(n == 0)
        def _():
            ckv_ref[...] = jnp.zeros_like(ckv_ref)
            ds_ref[...] = jnp.zeros_like(ds_ref)

        @pl.when(n < nb)
        def _():
            cq, sq, cp, sp = cq_ref[...], sq_ref[...], cp_ref[...], sp_ref[...]
            K, V = _swa_keys(kc_ref, kp_ref, vc_ref, vp_ref, cq, sq, cp, sp)
            tri = _swa_tri()
            valid = tri | (n > 0)
            lane, rope_sub0, std_sub0 = _swa_masks()
            lane_row = lax.broadcasted_iota(jnp.int32, (1, LANE), 1)
            lse_v = l_ref[...]
            dKt = [jnp.zeros((LANE, 2 * BLK), F32) for _ in range(A_KV)]
            dVt = [jnp.zeros((LANE, 2 * BLK), F32) for _ in range(A_KV)]
            dsinks, roped, roped_t, do_t = [], {}, {}, {}
            group = A_HEADS // A_KV
            dim = lax.broadcasted_iota(jnp.int32, (LANE, BLK), 0)
            rope_row0, std_row0 = ((dim // 32) % 2) == 0, dim < 64

            def products(head):
                pb, sub, g = head // 2, head % 2, head // group
                cols = slice(pb * LANE, (pb + 1) * LANE)
                Kg, Vg = K[:, g * LANE:(g + 1) * LANE], V[:, g * LANE:(g + 1) * LANE]
                if sub == 0:
                    roped[pb] = _rope(q_ref[:, cols], cq, sq)
                    roped_t[pb] = roped[pb].T
                    do_t[pb] = do_ref[:, cols].T
                qm = jnp.where(rope_sub0 if sub == 0 else ~rope_sub0, roped[pb], 0.0).astype(MXU)
                qmt = jnp.where(rope_row0 if sub == 0 else ~rope_row0, roped_t[pb], 0.0).astype(MXU)
                dov = jnp.where(std_sub0 if sub == 0 else ~std_sub0, do_ref[:, cols], 0.0)
                dovt = jnp.where(std_row0 if sub == 0 else ~std_row0, do_t[pb], 0.0).astype(MXU)
                delta = jnp.sum(dov * o_ref[:, cols], axis=1, keepdims=True)
                return qmt, dovt, delta, _dot_nt(qm, Kg), _dot_nt(dov.astype(MXU), Vg)

            def scores(head, qmt, dovt, delta, s_full, dp_full):
                lh = jnp.sum(jnp.where(lane == head, lse_v, 0.0), axis=1, keepdims=True)
                p = jnp.where(valid, jnp.exp(_swa_fold(s_full, tri) * scale - lh), 0.0)
                psink = jnp.exp(sinks_ref[0, head] - lh)
                dsinks.append(jnp.sum(-psink * delta, axis=0, keepdims=True))
                dsq = (p * (_swa_fold(dp_full, tri) - delta)) * scale
                return qmt, dovt, _swa_unfold(p, tri).astype(MXU), _swa_unfold(dsq, tri).astype(MXU)

            def grads(head, qmt, dovt, pb16, dsc):
                g = head // group
                dKt[g] = dKt[g] + _dot(qmt, dsc)
                dVt[g] = dVt[g] + _dot(dovt, pb16)
                return _dot(dsc, K[:, g * LANE:(g + 1) * LANE])

            dqs = {}
            st1 = {0: products(0), 1: products(1)}
            st2 = {0: scores(0, *st1.pop(0))}
            for head in range(A_HEADS):
                if head + 2 < A_HEADS:
                    st1[head + 2] = products(head + 2)
                if head + 1 < A_HEADS:
                    st2[head + 1] = scores(head + 1, *st1.pop(head + 1))
                dqs[head] = grads(head, *st2.pop(head))
                if head % 2 == 1:
                    pb = head // 2
                    dqp = jnp.where(rope_sub0, dqs[head - 1], dqs[head])
                    dq_ref[:, pb * LANE:(pb + 1) * LANE] = _rope_bwd(dqp, cq, sq).astype(dq_ref.dtype)
            dsink = jnp.zeros((1, LANE), F32)
            for head in range(A_HEADS):
                dsink = jnp.where(lane_row == head, dsinks[head], dsink)
            dK, dV = [a.T for a in dKt], [a.T for a in dVt]
            prev = ([_rope_bwd(dK[g][:BLK], cp, sp) for g in range(A_KV)] + [dV[g][:BLK] for g in range(A_KV)])
            cur_ = ([_rope_bwd(dK[g][BLK:], cq, sq) for g in range(A_KV)] + [dV[g][BLK:] for g in range(A_KV)])
            dkv_ref[...] = (ckv_ref[...] + jnp.concatenate(prev, axis=1)).astype(dkv_ref.dtype)
            ckv_ref[...] = jnp.concatenate(cur_, axis=1)
            ds_ref[...] = ds_ref[...] + jnp.broadcast_to(dsink, ds_ref.shape)

        @pl.when(n == nb)
        def _():
            dkv_ref[...] = ckv_ref[...].astype(dkv_ref.dtype)

    last = nb - 1

    def cur(n):
        return jnp.minimum(n, last)

    def out_kv(n):
        return (jnp.maximum(n - 1, 0), 0)

    return pl.pallas_call(
        body, name="swa_bwd", grid=(nb + 1,),
        in_specs=[pl.BlockSpec(memory_space=pltpu.SMEM)] + _swa_in_specs(nb, last) + [
            pl.BlockSpec((BLK, D), lambda n: (cur(n), 0)),
            pl.BlockSpec((BLK, D), lambda n: (cur(n), 0)),
            pl.BlockSpec((BLK, LANE), lambda n: (cur(n), 0)),
            _any(),
        ],
        out_specs=[
            pl.BlockSpec((BLK, D), lambda n: (cur(n), 0)),
            pl.BlockSpec((BLK, 512), out_kv),
            pl.BlockSpec((8, LANE), lambda n: (0, 0)),
        ],
        out_shape=[
            jax.ShapeDtypeStruct((T, D), MXU),
            jax.ShapeDtypeStruct((T, 512), MXU),
            jax.ShapeDtypeStruct((8, LANE), F32),
        ],
        scratch_shapes=[pltpu.VMEM((BLK, 512), F32)],
        compiler_params=_cp(("arbitrary",)),
    )(sinks, proj, proj, proj, proj, proj, cos, sin, cos, sin, do_a, o_a, lse, after)


def _gla_gate(bl_ref, gu_ref, bias_ref):
    gk = _dot(bl_ref[...].astype(MXU), gu_ref[...]) + bias_ref[...]
    la = (jnp.minimum(gk, 0.0) - jnp.log(1.0 + jnp.exp(-jnp.abs(gk)))) / TAU
    ri = lax.broadcasted_iota(jnp.int32, (CHUNK, CHUNK), 0)
    ci = lax.broadcasted_iota(jnp.int32, (CHUNK, CHUNK), 1)
    b = _dot_f32(jnp.where(ci <= ri, 1.0, 0.0).astype(F32), la)
    return gk, la, b, ri, ci


def _gla_head(q_ref, k_ref, la, b, h):
    sl = slice(h * B_DK, (h + 1) * B_DK)
    bh = b[:, sl]
    blast = jnp.sum(la[:, sl], axis=0, keepdims=True)
    qc = q_ref[:, sl] * (B_DK ** -0.5)
    kh = k_ref[:, sl]
    eb, enb, esb = jnp.exp(bh), jnp.exp(-bh), jnp.exp(blast - bh)
    return qc * eb, kh * enb, kh * esb, eb, enb, esb, jnp.exp(blast)


def _gla_specs(chunk_of):
    return [
        pl.BlockSpec((CHUNK, 512), lambda i: (chunk_of(i), C_BQ // 512)),
        pl.BlockSpec((CHUNK, 512), lambda i: (chunk_of(i), C_BK // 512)),
        pl.BlockSpec((CHUNK, D), lambda i: (chunk_of(i), C_BV // D)),
        pl.BlockSpec((CHUNK, W_BL), lambda i: (chunk_of(i), C_BL // W_BL)),
        pl.BlockSpec((W_BL, 512), lambda i: (0, 0)),
        pl.BlockSpec((1, 512), lambda i: (0, 0)),
    ]


def _gla_fwd(proj, gu_pad, bias):
    T = proj.shape[0]
    nc = T // CHUNK

    def body(q_ref, k_ref, v_ref, bl_ref, gu_ref, bias_ref, o_ref, st_ref, state_ref):
        @pl.when(pl.program_id(0) == 0)
        def _():
            state_ref[...] = jnp.zeros_like(state_ref)

        _, la, b, ri, ci = _gla_gate(bl_ref, gu_ref, bias_ref)
        st_ref[...] = state_ref[...]
        for h in range(B_HEADS):
            q_e, k_e, k_s, _, _, _, decay = _gla_head(q_ref, k_ref, la, b, h)
            vh = v_ref[:, h * B_DV:(h + 1) * B_DV].astype(MXU)
            rows = slice(h * B_DV, (h + 1) * B_DV)
            q_eb = q_e.astype(MXU)
            att = jnp.where(ci <= ri, _dot_nt(q_eb, k_e.astype(MXU)), 0.0)
            st = state_ref[rows, :]
            o_ref[:, rows] = _dot(att.astype(MXU), vh) + _dot_nt(q_eb, st.astype(MXU))
            state_ref[rows, :] = st * decay + _dot_tn(vh, k_s.astype(MXU))

    return pl.pallas_call(
        body, name="gla_fwd", grid=(nc,),
        in_specs=_gla_specs(lambda i: i),
        out_specs=[pl.BlockSpec((CHUNK, D), lambda i: (i, 0)),
                   pl.BlockSpec((B_HEADS * B_DV, B_DK), lambda i: (i, 0))],
        out_shape=[jax.ShapeDtypeStruct((T, D), F32),
                   jax.ShapeDtypeStruct((nc * B_HEADS * B_DV, B_DK), F32)],
        scratch_shapes=[pltpu.VMEM((B_HEADS * B_DV, B_DK), F32)],
        compiler_params=_cp(("arbitrary",)),
    )(proj, proj, proj, proj, gu_pad, bias)


def _gla_bwd(proj, gu_pad, bias, states, do_b):
    T = proj.shape[0]
    nc = T // CHUNK
    o_q, o_k = C_BQ - C_GLA, C_BK - C_GLA

    def body(q_ref, k_ref, v_ref, bl_ref, gu_ref, bias_ref, st_ref, do_ref,
             dg_ref, dbl_ref, ggu_ref, gbias_ref, gt_ref):
        @pl.when(pl.program_id(0) == 0)
        def _():
            gt_ref[...] = jnp.zeros_like(gt_ref)
            ggu_ref[...] = jnp.zeros_like(ggu_ref)
            gbias_ref[...] = jnp.zeros_like(gbias_ref)

        gk, la, b, ri, ci = _gla_gate(bl_ref, gu_ref, bias_ref)
        causal = ci <= ri
        upper = jnp.where(ci >= ri, 1.0, 0.0).astype(F32)
        dla_parts = []
        for h in range(B_HEADS):
            q_e, k_e, k_s, eb, enb, esb, decay = _gla_head(q_ref, k_ref, la, b, h)
            rows = slice(h * B_DV, (h + 1) * B_DV)
            sl = slice(h * B_DK, (h + 1) * B_DK)
            vh = v_ref[:, rows].astype(MXU)
            doh = do_ref[:, rows].astype(MXU)
            q_eb, k_eb, k_sb = q_e.astype(MXU), k_e.astype(MXU), k_s.astype(MXU)
            st = st_ref[rows, :]
            gt = gt_ref[rows, :]
            gtb = gt.astype(MXU)
            att = jnp.where(causal, _dot_nt(q_eb, k_eb), 0.0).astype(MXU)
            datt = jnp.where(causal, _dot_nt(doh, vh), 0.0).astype(MXU)
            dq_e = _dot(datt, k_eb) + _dot(doh, st.astype(MXU))
            dk_e = _dot_tn(datt, q_eb)
            dk_s = _dot(vh, gtb)
            dg_ref[:, rows] = (_dot_tn(att, doh) + _dot_nt(k_sb, gtb)).astype(dg_ref.dtype)
            ddecay = jnp.sum(gt * st, axis=0, keepdims=True)
            gt_ref[rows, :] = gt * decay + _dot_tn(doh, q_eb)
            dg_ref[:, o_q + h * B_DK:o_q + (h + 1) * B_DK] = (dq_e * eb * (B_DK ** -0.5)).astype(dg_ref.dtype)
            dg_ref[:, o_k + h * B_DK:o_k + (h + 1) * B_DK] = (dk_e * enb + dk_s * esb).astype(dg_ref.dtype)
            dks_ks = dk_s * k_s
            db = dq_e * q_e - dk_e * k_e - dks_ks
            dblast = jnp.sum(dks_ks, axis=0, keepdims=True) + ddecay * decay
            dla_parts.append(_dot_f32(upper, db) + dblast)
        dla = jnp.concatenate(dla_parts, axis=1)
        dgk = dla * (1.0 / TAU) * _sigmoid(-gk)
        dgkb = dgk.astype(MXU)
        dbl_ref[...] = _dot_nt(dgkb, gu_ref[...]).astype(dbl_ref.dtype)
        ggu_ref[...] = ggu_ref[...] + _dot_tn(bl_ref[...].astype(MXU), dgkb)
        gbias_ref[...] = gbias_ref[...] + jnp.broadcast_to(jnp.sum(dgk, axis=0, keepdims=True), gbias_ref.shape)

    def rev(i):
        return nc - 1 - i

    return pl.pallas_call(
        body, name="gla_bwd", grid=(nc,),
        in_specs=_gla_specs(rev) + [
            pl.BlockSpec((B_HEADS * B_DV, B_DK), lambda i: (rev(i), 0)),
            pl.BlockSpec((CHUNK, D), lambda i: (rev(i), 0)),
        ],
        out_specs=[
            pl.BlockSpec((CHUNK, W_GLA), lambda i: (rev(i), 0)),
            pl.BlockSpec((CHUNK, W_BL), lambda i: (rev(i), 0)),
            pl.BlockSpec((W_BL, 512), lambda i: (0, 0)),
            pl.BlockSpec((8, 512), lambda i: (0, 0)),
        ],
        out_shape=[
            jax.ShapeDtypeStruct((T, W_GLA), MXU),
            jax.ShapeDtypeStruct((T, W_BL), MXU),
            jax.ShapeDtypeStruct((W_BL, 512), F32),
            jax.ShapeDtypeStruct((8, 512), F32),
        ],
        scratch_shapes=[pltpu.VMEM((B_HEADS * B_DV, B_DK), F32)],
        compiler_params=_cp(("arbitrary",)),
    )(proj, proj, proj, proj, gu_pad, bias, states, do_b)


def _mid(x, target, proj, o_a, o_b, w_a, w_b, w_out, w_bn4, fnw):
    T = x.shape[0]
    tT = min(T, 128)
    nbuf = 4
    o_ag, o_bg, o_ma, o_mb = (c - C_GATES for c in (C_AG, C_BG, C_MA, C_MB))

    def body(x_ref, t_ref, oa_ref, ob_ref, gates_ref, wa_ref, wb_ref, wo_ref, wbn_ref, fnw_ref,
             dx2_ref, doa_ref, dob_ref, dgates_ref,
             gwa_ref, gwb_ref, gwo_ref, gfn_ref, gbn_ref, loss_ref, buf_ref):
        i = pl.program_id(0)

        @pl.when(i == 0)
        def _():
            for r in (gwa_ref, gwb_ref, gwo_ref, gfn_ref, gbn_ref, loss_ref):
                r[...] = jnp.zeros_like(r)

        rows = pl.ds(pl.multiple_of((i % nbuf) * tT, tT), tT)

        def keep(k, val):
            buf_ref[k, rows, :] = val

        oa, ag = oa_ref[...], gates_ref[:, o_ag:o_ag + D]
        sg_a = _sigmoid(ag)
        silu_a = ag * sg_a
        oag_b = (oa * silu_a).astype(MXU)
        keep(0, oag_b)
        y_a = _dot(oag_b, wa_ref[...])

        ob, bg = ob_ref[...], gates_ref[:, o_bg:o_bg + D]
        rbs, obhats = [], []
        for h in range(B_HEADS):
            obh = ob[:, h * B_DV:(h + 1) * B_DV]
            rb = lax.rsqrt(jnp.mean(obh * obh, axis=-1, keepdims=True) + EPS)
            rbs.append(rb)
            obhats.append(obh * rb)
        obhat = jnp.concatenate(obhats, axis=1)
        wbn = wbn_ref[...]
        obn = obhat * wbn
        sg_b = _sigmoid(bg)
        silu_b = bg * sg_b
        obg_b = (obn * silu_b).astype(MXU)
        keep(1, obg_b)
        y_b = _dot(obg_b, wb_ref[...])

        sa, sb = _sigmoid(gates_ref[:, o_ma:o_ma + D]), _sigmoid(gates_ref[:, o_mb:o_mb + D])
        mg_b = (sa * y_a + sb * y_b).astype(MXU)
        keep(2, mg_b)
        x2 = x_ref[...] + _dot(mg_b, wo_ref[...])
        r2 = lax.rsqrt(jnp.mean(x2 * x2, axis=-1, keepdims=True) + EPS)
        xh2 = x2 * r2
        fw = fnw_ref[...]
        err = xh2 * fw - t_ref[...]
        tok = jnp.mean(err * err, axis=-1, keepdims=True)
        loss_ref[...] = loss_ref[...] + 0.5 * jnp.sum(tok, axis=0, keepdims=True)

        dy = err * (1.0 / D)
        gfn_ref[...] = gfn_ref[...] + jnp.broadcast_to(jnp.sum(dy * xh2, axis=0, keepdims=True), gfn_ref.shape)
        gy = dy * fw
        dx2 = r2 * (gy - xh2 * jnp.mean(gy * xh2, axis=-1, keepdims=True))
        dx2_ref[...] = dx2
        dx2_b = dx2.astype(MXU)
        keep(5, dx2_b)
        dmg = _dot_nt(dx2_b, wo_ref[...])

        dgates_ref[:, o_ma:o_ma + D] = (dmg * y_a * sa * (1.0 - sa)).astype(dgates_ref.dtype)
        dgates_ref[:, o_mb:o_mb + D] = (dmg * y_b * sb * (1.0 - sb)).astype(dgates_ref.dtype)
        dya_b = (dmg * sa).astype(MXU)
        dyb_b = (dmg * sb).astype(MXU)
        keep(3, dya_b)
        keep(4, dyb_b)
        doag = _dot_nt(dya_b, wa_ref[...])
        dobg = _dot_nt(dyb_b, wb_ref[...])

        @pl.when(i % nbuf == nbuf - 1)
        def _():
            gwa_ref[...] = gwa_ref[...] + _dot_tn(buf_ref[0], buf_ref[3])
            gwb_ref[...] = gwb_ref[...] + _dot_tn(buf_ref[1], buf_ref[4])
            gwo_ref[...] = gwo_ref[...] + _dot_tn(buf_ref[2], buf_ref[5])

        doa_ref[...] = doag * silu_a
        dgates_ref[:, o_ag:o_ag + D] = (doag * oa * (sg_a * (1.0 + ag * (1.0 - sg_a)))).astype(dgates_ref.dtype)
        dobn = dobg * silu_b
        dgates_ref[:, o_bg:o_bg + D] = (dobg * obn * (sg_b * (1.0 + bg * (1.0 - sg_b)))).astype(dgates_ref.dtype)
        gg = dobn * wbn
        gbn = jnp.zeros((1, B_DV), F32)
        for h in range(B_HEADS):
            sl = slice(h * B_DV, (h + 1) * B_DV)
            gbn = gbn + jnp.sum(dobn[:, sl] * obhats[h], axis=0, keepdims=True)
            ggh = gg[:, sl]
            dob_ref[:, sl] = rbs[h] * (ggh - obhats[h] * jnp.mean(ggh * obhats[h], axis=-1, keepdims=True))
        gbn_ref[...] = gbn_ref[...] + jnp.broadcast_to(gbn, gbn_ref.shape)

    assert (T // tT) % nbuf == 0
    tile = pl.BlockSpec((tT, D), lambda i: (i, 0))
    row = pl.BlockSpec((1, D), lambda i: (0, 0))
    acc8 = pl.BlockSpec((8, D), lambda i: (0, 0))
    return pl.pallas_call(
        body, name="mid", grid=(T // tT,),
        in_specs=[tile, tile, tile, tile, pl.BlockSpec((tT, W_GATES), lambda i: (i, C_GATES // W_GATES)),
                  _vmem(), _vmem(), _vmem(), row, row],
        out_specs=[tile, tile, tile, pl.BlockSpec((tT, W_GATES), lambda i: (i, 0)), _vmem(), _vmem(), _vmem(),
                   acc8, pl.BlockSpec((8, B_DV), lambda i: (0, 0)), pl.BlockSpec((8, LANE), lambda i: (0, 0))],
        out_shape=[
            jax.ShapeDtypeStruct((T, D), F32),
            jax.ShapeDtypeStruct((T, D), F32),
            jax.ShapeDtypeStruct((T, D), F32),
            jax.ShapeDtypeStruct((T, W_GATES), MXU),
            jax.ShapeDtypeStruct((D, D), F32),
            jax.ShapeDtypeStruct((D, D), F32),
            jax.ShapeDtypeStruct((D, D), F32),
            jax.ShapeDtypeStruct((8, D), F32),
            jax.ShapeDtypeStruct((8, B_DV), F32),
            jax.ShapeDtypeStruct((8, LANE), F32),
        ],
        scratch_shapes=[pltpu.VMEM((6, nbuf * tT, D), MXU)],
        compiler_params=_cp(("arbitrary",)),
    )(x, target, o_a, o_b, proj, w_a, w_b, w_out, w_bn4, fnw)


DH = D // 2


def _gw_half(h, pieces, half, after=None):
    T = h.shape[0]
    steps = NF // 512
    tiles = ((0, 2), (2, 3), (4, 8), (8, 16))

    def body(*refs):
        h_ref, q_ref, kv_ref, bl_ref, gla_ref, gates_ref = refs[:6]
        o_ref = refs[-1]
        j = pl.program_id(0)

        for (lo, hi), ref in zip(tiles, (q_ref, kv_ref, gla_ref, gates_ref)):
            @pl.when((j >= lo) & (j < hi))
            def _(ref=ref):
                o_ref[...] = _dot_tn(ref[...], h_ref[...])

        @pl.when(j == 3)
        def _():
            o_ref[0:W_BL, :] = _dot_tn(bl_ref[...], h_ref[...])
            o_ref[W_BL:, :] = jnp.zeros((512 - W_BL, DH), F32)

    def tile_of(lo, hi):
        return lambda j: (0, jnp.clip(j - lo, 0, hi - lo - 1))

    in_specs = [pl.BlockSpec((T, DH), lambda j: (0, half)),
                pl.BlockSpec((T, 512), tile_of(0, 2)), pl.BlockSpec((T, 512), lambda j: (0, 0)),
                pl.BlockSpec((T, W_BL), lambda j: (0, 0)),
                pl.BlockSpec((T, 512), tile_of(4, 8)), pl.BlockSpec((T, 512), tile_of(8, 16))]
    args = [h, *pieces]
    if after is not None:
        in_specs.append(_any())
        args.append(after)
    return pl.pallas_call(
        body, name=f"gw_in_half{half}", grid=(steps,),
        in_specs=in_specs, out_specs=pl.BlockSpec((512, DH), lambda j: (j, 0)),
        out_shape=jax.ShapeDtypeStruct((NF, DH), F32),
        compiler_params=_cp(("parallel",)),
    )(*args)


def _chip_copies(s_ref, got_ref, send_sems, recv_sems):
    x, y, c = _place()
    chips = [(1 - x, y), (x, 1 - y), (1 - x, 1 - y)]
    return [pltpu.make_async_remote_copy(
        src_ref=s_ref.at[2 * px + py], dst_ref=got_ref.at[j],
        send_sem=send_sems.at[j], recv_sem=recv_sems.at[j], device_id=(px, py, c), device_id_type=MESH)
        for j, (px, py) in enumerate(chips)]


_EFFECT = pltpu.SideEffectType.DATAFLOW_SIDE_EFFECTING


def _hbm():
    return pl.BlockSpec(memory_space=pltpu.HBM)


def _sem():
    return pl.BlockSpec(memory_space=pltpu.SEMAPHORE)


def _chip_start(sums, half):
    land = pltpu.with_memory_space_constraint(lax.empty((3,) + sums.shape[1:], sums.dtype), pltpu.HBM)

    def body(s_ref, land_ref, send_sems, recv_sems, s_thru, land_thru, token):
        for cp in _chip_copies(s_ref, land_ref, send_sems, recv_sems):
            cp.start()
        token[...] = jnp.zeros_like(token)

    return pl.pallas_call(
        body, name=f"chip_start{half}",
        out_shape=(pltpu.SemaphoreType.DMA((3,)), pltpu.SemaphoreType.DMA((3,)),
                   pltpu.HBM(sums.shape, sums.dtype), pltpu.HBM(land.shape, land.dtype),
                   jax.ShapeDtypeStruct((8, LANE), F32)),
        in_specs=(_hbm(), _hbm()), out_specs=(_sem(), _sem(), _hbm(), _hbm(), _vmem()),
        input_output_aliases={0: 2, 1: 3},
        compiler_params=pltpu.CompilerParams(has_side_effects=_EFFECT),
    )(pltpu.with_memory_space_constraint(sums, pltpu.HBM), land)


def _chip_wait(send_sems, recv_sems, s_thru, land_thru, after, half):
    def body(s_ref, land_ref, send_sems, recv_sems, after_ref, s_out, got_ref):
        copies = _chip_copies(s_ref, land_ref, send_sems, recv_sems)
        for cp in copies:
            cp.wait_send()
        for cp in copies:
            cp.wait_recv()

    return pl.pallas_call(
        body, name=f"chip_wait{half}",
        out_shape=(pltpu.HBM(s_thru.shape, s_thru.dtype), pltpu.HBM(land_thru.shape, land_thru.dtype)),
        in_specs=(_hbm(), _hbm(), _sem(), _sem(), _any()), out_specs=(_hbm(), _hbm()),
        input_output_aliases={0: 0, 1: 1},
        compiler_params=pltpu.CompilerParams(has_side_effects=_EFFECT),
    )(s_thru, land_thru, send_sems, recv_sems, after)


def _dh_norm(pieces, offsets, wf, x, dx2, norm_w, after):
    T = x.shape[0]
    tT = min(T, 256)
    widths = [p.shape[1] for p in pieces]
    npc = len(pieces)

    def body(*refs):
        dp_refs = refs[:npc]
        wf_ref, x_ref, dx2_ref, nw_ref, _, gx_ref, gnw_ref = refs[npc:]

        @pl.when(pl.program_id(0) == 0)
        def _():
            gnw_ref[...] = jnp.zeros_like(gnw_ref)

        dh = jnp.zeros((tT, D), F32)
        for dp_ref, off, w in zip(dp_refs, offsets, widths):
            dh = dh + _dot(dp_ref[...], wf_ref[off:off + w, :])
        xv = x_ref[...]
        r = lax.rsqrt(jnp.mean(xv * xv, axis=-1, keepdims=True) + EPS)
        xh = xv * r
        gnw_ref[...] = gnw_ref[...] + jnp.broadcast_to(jnp.sum(dh * xh, axis=0, keepdims=True), gnw_ref.shape)
        g = dh * nw_ref[...]
        gx_ref[...] = r * (g - xh * jnp.mean(g * xh, axis=-1, keepdims=True)) + dx2_ref[...]

    tile = pl.BlockSpec((tT, D), lambda i: (i, 0))
    return pl.pallas_call(
        body, name="dh_norm", grid=(T // tT,),
        in_specs=[pl.BlockSpec((tT, w), lambda i: (i, 0)) for w in widths]
        + [_vmem(), tile, tile, pl.BlockSpec((1, D), lambda i: (0, 0)), _any()],
        out_specs=[tile, pl.BlockSpec((8, D), lambda i: (0, 0))],
        out_shape=[jax.ShapeDtypeStruct((T, D), F32), jax.ShapeDtypeStruct((8, D), F32)],
        compiler_params=_cp(("arbitrary",)),
    )(*pieces, wf, x, dx2, norm_w, after)


def _adamw_math(w, g, m, v):
    m = ADAM_B1 * m + (1.0 - ADAM_B1) * g
    v = ADAM_B2 * v + (1.0 - ADAM_B2) * (g * g)
    m_hat = m / (1.0 - ADAM_B1 ** ADAM_STEP)
    v_hat = v / (1.0 - ADAM_B2 ** ADAM_STEP)
    delta = -ADAM_LR * (m_hat / (jnp.sqrt(v_hat) + ADAM_EPS) + ADAM_WD * w)
    return delta, m, v


def _fetch_partials(s_ref, got_ref, buf, sems):
    x, y, _ = _place()
    cps = [pltpu.make_async_copy(s_ref.at[2 * x + y], buf.at[0], sems.at[0])]
    cps += [pltpu.make_async_copy(got_ref.at[j], buf.at[1 + j], sems.at[1 + j]) for j in range(3)]
    for cp in cps:
        cp.start()
    for cp in cps:
        cp.wait()


SMALL_AT = dict(norm_w=0, fnw=8, bias=16, bn=24, sinks=32, loss=40)
ROW_AT = (R_IN, R_A, R_B, R_O)


def _small_exchange(small):
    def body(small_ref, out_ref, send_sems, recv_sems):
        x, y, c = _place()
        me_slot = 4 * x + 2 * y + c
        sends = []
        k = 0
        for dx in range(2):
            for dy in range(2):
                for dc in range(2):
                    if dx == 0 and dy == 0 and dc == 0:
                        continue
                    sends.append(pltpu.make_async_remote_copy(
                        src_ref=small_ref, dst_ref=out_ref.at[me_slot],
                        send_sem=send_sems.at[k], recv_sem=recv_sems.at[k],
                        device_id=(x ^ dx, y ^ dy, c ^ dc), device_id_type=MESH))
                    k += 1
        for cp in sends:
            cp.start()
        out_ref[me_slot] = small_ref[...]
        for cp in sends:
            cp.wait_recv()
        for cp in sends:
            cp.wait_send()

    return pl.pallas_call(
        body, name="small_exchange",
        in_specs=[_vmem()], out_specs=_vmem(),
        out_shape=jax.ShapeDtypeStruct((NDEV, SMALL_ROWS, D), F32),
        scratch_shapes=[pltpu.SemaphoreType.DMA((7,)), pltpu.SemaphoreType.DMA((7,))],
    )(small)


def _finish_small(ws, ms, vs, smalls):
    names = ["norm_w", "fnw", "bias", "bn", "sinks"]
    widths = [ws[n].shape[1] for n in names]

    def body(*refs):
        w_refs, m_refs, v_refs = refs[0:5], refs[5:10], refs[10:15]
        smalls_ref, loss_ref = refs[15], refs[16]
        outs, tot = refs[17:37], refs[37]
        acc = smalls_ref[0]
        for d in range(1, NDEV):
            acc = acc + smalls_ref[d]
        tot[...] = acc
        loss_ref[...] = tot[SMALL_AT["loss"]:SMALL_AT["loss"] + 1, 0:1]
        for p, (nm_, wd) in enumerate(zip(names, widths)):
            r = SMALL_AT[nm_]
            g = tot[r:r + 1, 0:wd]
            d, nm, nv = _adamw_math(w_refs[p][...], g, m_refs[p][...], v_refs[p][...])
            for o, val in zip(outs[4 * p:4 * p + 4], (g, d, nm, nv)):
                o[...] = val

    res = pl.pallas_call(
        body, name="finish_small",
        in_specs=[_vmem()] * 16, out_specs=[_vmem()] * 21,
        out_shape=[jax.ShapeDtypeStruct((1, 1), F32)]
        + [jax.ShapeDtypeStruct((1, wd), F32) for wd in widths for _ in range(4)],
        scratch_shapes=[pltpu.VMEM((SMALL_ROWS, D), F32)],
        compiler_params=_cp(),
    )(*[ws[n] for n in names], *[ms[n] for n in names], *[vs[n] for n in names], smalls)
    return res[0], {n: tuple(res[1 + 4 * p:5 + 4 * p]) for p, n in enumerate(names)}


def _finish(w_rows, m_rows, v_rows, gu_w, gu_m, gu_v, sums, got):
    shapes = [w.shape for w in w_rows]

    def body(*refs):
        wr_refs, mr_refs, vr_refs = refs[0:4], refs[4:8], refs[8:12]
        guw_ref, gum_ref, guv_ref = refs[12:15]
        s_refs, got_refs = refs[15:17], refs[17:19]
        row_outs = refs[19:35]
        gu_outs = refs[35:39]
        buf, gsh, sems = refs[39:]
        x, y, c = _place()
        me_slot = 4 * x + 2 * y + c
        unshift = lax.rem(SHARD_PAD - 2 * me_slot, SHARD_PAD)
        for hf in range(2):
            _fetch_partials(s_refs[hf], got_refs[hf], buf, sems)
            for p in range(4):
                n, off = shapes[p][0], ROW_AT[p]
                nf = SHARD_PAD if p == 0 else n
                for cc in range(DH // LANE):
                    src = slice(cc * LANE, (cc + 1) * LANE)
                    cols = slice(hf * DH + cc * LANE, hf * DH + (cc + 1) * LANE)
                    g = buf[0, off:off + nf, src].astype(F32)
                    for j in range(1, 4):
                        g = g + buf[j, off:off + nf, src].astype(F32)
                    if p == 0:
                        gsh[...] = pltpu.roll(g, unshift, 0)
                        g = gsh[0:n, :]
                    d, nm, nv = _adamw_math(wr_refs[p][:, cols], g, mr_refs[p][:, cols], vr_refs[p][:, cols])
                    for o, val in zip(row_outs[4 * p:4 * p + 4], (g, d, nm, nv)):
                        o[:, cols] = val
            if hf == 0:
                g = buf[0, R_GU:R_GU + RANK, 0:64].astype(F32)
                for j in range(1, 4):
                    g = g + buf[j, R_GU:R_GU + RANK, 0:64].astype(F32)
                d, nm, nv = _adamw_math(guw_ref[...], g, gum_ref[...], guv_ref[...])
                for o, val in zip(gu_outs, (g, d, nm, nv)):
                    o[...] = val

    res = pl.pallas_call(
        body, name="finish",
        in_specs=[_vmem()] * 15 + [_any()] * 4,
        out_specs=[_vmem()] * 20,
        out_shape=[jax.ShapeDtypeStruct(s, F32) for s in shapes for _ in range(4)]
        + [jax.ShapeDtypeStruct((RANK, 64), F32)] * 4,
        scratch_shapes=[pltpu.VMEM((4, ROWS, DH), sums[0].dtype), pltpu.VMEM((SHARD_PAD, LANE), F32),
                        pltpu.SemaphoreType.DMA((4,))],
        compiler_params=_cp(),
    )(*w_rows, *m_rows, *v_rows, gu_w, gu_m, gu_v, *sums, *got)
    return tuple(res[0:16]), tuple(res[16:20])


def _place():
    x, y, c = lax.axis_index("x"), lax.axis_index("y"), lax.axis_index("c")
    return x, y, c


def _peers(x, y, c):
    return [(x ^ dx, y ^ dy, c ^ dc) for dx in range(2) for dy in range(2) for dc in range(2) if dx + dy + dc]


def _late_gather_start(blk, after, name="late_gather"):
    land = pltpu.with_memory_space_constraint(lax.empty((NDEV,) + blk.shape, blk.dtype), pltpu.HBM)

    def body(b_ref, land_ref, after_ref, send_sems, recv_sems, b_thru, land_thru, token):
        x, y, c = _place()
        for k, to in enumerate(_peers(x, y, c)):
            pltpu.make_async_remote_copy(
                src_ref=b_ref, dst_ref=land_ref.at[4 * x + 2 * y + c], send_sem=send_sems.at[k],
                recv_sem=recv_sems.at[k], device_id=to, device_id_type=MESH).start()
        token[...] = jnp.zeros_like(token)

    return pl.pallas_call(
        body, name=name + "_start",
        out_shape=(pltpu.SemaphoreType.DMA((7,)), pltpu.SemaphoreType.DMA((7,)),
                   pltpu.HBM(blk.shape, blk.dtype), pltpu.HBM(land.shape, land.dtype),
                   jax.ShapeDtypeStruct((8, LANE), F32)),
        in_specs=(_hbm(), _hbm(), _any()), out_specs=(_sem(), _sem(), _hbm(), _hbm(), _vmem()),
        input_output_aliases={0: 2, 1: 3},
        compiler_params=pltpu.CompilerParams(has_side_effects=_EFFECT),
    )(pltpu.with_memory_space_constraint(blk, pltpu.HBM), land, after)


def _late_gather_wait(send_sems, recv_sems, b_thru, land_thru, after, after2, name="late_gather"):
    def body(b_ref, land_ref, send_sems, recv_sems, after_ref, after2_ref, b_out, got_ref):
        x, y, c = _place()
        copies = [pltpu.make_async_remote_copy(
            src_ref=b_ref, dst_ref=land_ref.at[4 * x + 2 * y + c], send_sem=send_sems.at[k],
            recv_sem=recv_sems.at[k], device_id=to, device_id_type=MESH)
            for k, to in enumerate(_peers(x, y, c))]
        for cp in copies:
            cp.wait_send()
        for cp in copies:
            cp.wait_recv()

    return pl.pallas_call(
        body, name=name + "_wait",
        out_shape=(pltpu.HBM(b_thru.shape, b_thru.dtype), pltpu.HBM(land_thru.shape, land_thru.dtype)),
        in_specs=(_hbm(), _hbm(), _sem(), _sem(), _any(), _any()), out_specs=(_hbm(), _hbm()),
        input_output_aliases={0: 0, 1: 1},
        compiler_params=pltpu.CompilerParams(has_side_effects=_EFFECT),
    )(b_thru, land_thru, send_sems, recv_sems, after, after2)


G_ROWS = SHARD_PAD + RANK


def _gather_blocks(w_in_t, gu_s, xs, norm_w, pos_col):
    rows, cols = G_ROWS, D
    T = xs.shape[0]
    tT = min(T, 256)
    inv_row, sign_row = _rope_rows()

    def body(wi_ref, gu_ref, xs_ref, nw_ref, pos_ref, inv_ref, sign_ref,
             out_ref, h_ref, cos_ref, sin_ref, x_ref, frame_ref, send_sems, recv_sems, local_sem):
        x, y, c = _place()
        me, sibling = (x, y, c), (x, y, 1 - c)
        chips = [(1 - x, y), (x, 1 - y), (1 - x, 1 - y)]
        shift = 2 * (4 * x + 2 * y + c)
        frame_ref[SHARD - SHARD % 8:, :] = jnp.zeros((SHARD_PAD - SHARD + SHARD % 8, D), F32)
        frame_ref[:SHARD, :] = wi_ref[...]
        for cc in range(D // LANE):
            cs = slice(cc * LANE, (cc + 1) * LANE)
            x_ref[0:SHARD_PAD, cs] = pltpu.roll(frame_ref[:, cs], shift, 0).astype(x_ref.dtype)
        x_ref[SHARD_PAD:G_ROWS, :] = jnp.zeros((RANK, D), x_ref.dtype)
        x_ref[SHARD_PAD:G_ROWS, 0:64] = gu_ref[...].astype(x_ref.dtype)

        def slot(px, py, pc):
            return out_ref.at[4 * px + 2 * py + pc]

        def copy(k, block, to, src=None):
            return pltpu.make_async_remote_copy(
                src_ref=slot(*block) if src is None else src, dst_ref=slot(*block),
                send_sem=send_sems.at[k], recv_sem=recv_sems.at[k], device_id=to, device_id_type=MESH)

        mine = pltpu.make_async_copy(x_ref, slot(*me), local_sem)
        mine.start()
        first = [copy(0, me, sibling, src=x_ref)]
        first += [copy(1 + j, me, (*chip, c), src=x_ref) for j, chip in enumerate(chips)]
        for cp in first:
            cp.start()

        @pl.loop(0, T // tT)
        def _(i):
            rows_i = pl.ds(pl.multiple_of(i * tT, tT), tT)
            _prologue_rows(rows_i, xs_ref, nw_ref, pos_ref, inv_ref, sign_ref, h_ref, cos_ref, sin_ref)

        passed = [copy(4 + j, (*chip, c), sibling) for j, chip in enumerate(chips)]
        for j, chip in enumerate(chips):
            copy(1 + j, (*chip, c), me).wait_recv()
            passed[j].start()
        copy(0, sibling, me).wait_recv()
        for j, chip in enumerate(chips):
            copy(4 + j, (*chip, 1 - c), me).wait_recv()
        for cp in first + passed:
            cp.wait_send()
        mine.wait()

    return pl.pallas_call(
        body, name="gather_weights",
        in_specs=[_vmem()] * 7, out_specs=[_any()] + [_vmem()] * 3,
        out_shape=[jax.ShapeDtypeStruct((NDEV, rows, cols), WIRE), jax.ShapeDtypeStruct((T, D), MXU),
                   jax.ShapeDtypeStruct((T, LANE), F32), jax.ShapeDtypeStruct((T, LANE), F32)],
        scratch_shapes=[pltpu.VMEM((rows, cols), WIRE), pltpu.VMEM((SHARD_PAD, D), F32),
                        pltpu.SemaphoreType.DMA((7,)), pltpu.SemaphoreType.DMA((7,)), pltpu.SemaphoreType.DMA],
        compiler_params=_cp(),
    )(w_in_t, gu_s, xs, norm_w, pos_col, inv_row, sign_row)


def _pair_reduce(packed):
    def body(p_ref, out_ref, got, own, send_sems, recv_sems, own_sems):
        x, y, c = _place()
        sends = [pltpu.make_async_remote_copy(
            src_ref=p_ref.at[2 * chip + (1 - c)], dst_ref=got.at[chip],
            send_sem=send_sems.at[chip], recv_sem=recv_sems.at[chip], device_id=(x, y, 1 - c), device_id_type=MESH)
            for chip in range(4)]
        loads = [pltpu.make_async_copy(p_ref.at[2 * chip + c], own.at[chip], own_sems.at[chip]) for chip in range(4)]
        for cp in sends + loads:
            cp.start()
        for chip in range(4):
            loads[chip].wait()
            sends[chip].wait_recv()
            out_ref[chip] = (own[chip].astype(F32) + got[chip].astype(F32)).astype(out_ref.dtype)
        for cp in sends:
            cp.wait_send()

    return pl.pallas_call(
        body, name="pair_reduce",
        in_specs=[_any()], out_specs=_vmem(),
        out_shape=jax.ShapeDtypeStruct((4,) + packed.shape[1:], packed.dtype),
        scratch_shapes=[pltpu.VMEM((4,) + packed.shape[1:], packed.dtype), pltpu.VMEM((4,) + packed.shape[1:], packed.dtype),
                        pltpu.SemaphoreType.DMA((4,)), pltpu.SemaphoreType.DMA((4,)), pltpu.SemaphoreType.DMA((4,))],
        compiler_params=_cp(),
    )(packed)


def _pad_cols(a, cols):
    return jnp.pad(a, ((0, 0), (0, cols - a.shape[1])))


def _pad_rows(a, rows):
    return jnp.pad(a, ((0, rows - a.shape[0]), (0, 0)))


FRAME = 928


def _join_frames(frames):
    head = frames[:, :FRAME].at[1:, :16].add(frames[:-1, FRAME:])
    return jnp.concatenate([head.reshape(NDEV * FRAME, D), frames[NDEV - 1, FRAME:]], axis=0)


def _build_wft(wt):
    q = wt[0:1024].reshape(8, 2, 2, 32, D).transpose(0, 2, 1, 3, 4).reshape(1024, D)
    k = wt[1024:1152].reshape(2, 2, 1, 32, D)
    kd = jnp.broadcast_to(k, (2, 2, 2, 32, D)).reshape(256, D)
    v = wt[1152:1280].reshape(2, 1, 64, D)
    vd = jnp.broadcast_to(v, (2, 2, 64, D)).reshape(256, D)
    ag, bq, bk = wt[1280:2304], wt[2304:2816], wt[2816:3328]
    bv, bg, bl = wt[3328:4352], wt[4352:5376], wt[5376:5392]
    ma, mb = wt[5392:6416], wt[6416:7440]
    return jnp.concatenate([q, kd, vd, _pad_rows(bl, C_GLA - C_BL), bv, bq, bk, ag, bg, ma, mb], axis=0)


def _unbuild_gwt(g):
    n = g.shape[1]
    q = g[C_Q:C_Q + 1024].reshape(8, 2, 2, 32, n).transpose(0, 2, 1, 3, 4).reshape(1024, n)
    k = g[C_KD:C_KD + 256].reshape(2, 2, 2, 32, n).sum(axis=2).reshape(128, n)
    v = g[C_VD:C_VD + 256].reshape(2, 2, 64, n).sum(axis=1).reshape(128, n)
    bv, bq, bk = g[C_BV:C_BV + 1024], g[C_BQ:C_BQ + 512], g[C_BK:C_BK + 512]
    ag, bg, ma, mb = (g[c:c + 1024] for c in (C_AG, C_BG, C_MA, C_MB))
    return jnp.concatenate([q, k, v, ag, bq, bk, bv, bg, g[C_BL:C_BL + RANK], ma, mb], axis=0)


def kernel(x, positions, norm_w, w_in, a_sinks, b_gate_up, b_gate_bias, b_out_norm_w, w_a_proj, w_b_proj, w_out, final_norm_w, loss_target, m_norm_w, m_w_in, m_a_sinks, m_b_gate_up, m_b_gate_bias, m_b_out_norm_w, m_w_a_proj, m_w_b_proj, m_w_out, m_final_norm_w, v_norm_w, v_w_in, v_a_sinks, v_b_gate_up, v_b_gate_bias, v_b_out_norm_w, v_w_a_proj, v_w_b_proj, v_w_out, v_final_norm_w):
    T = x.shape[1]
    xs, target = x[0], loss_target[0]
    fnw = final_norm_w.reshape(1, D)
    me = 4 * lax.axis_index("x") + 2 * lax.axis_index("y") + lax.axis_index("c")
    allw, h, cos, sin = _gather_blocks(w_in[0].T, b_gate_up[0], xs, norm_w, positions.reshape(T, 1))
    late_blk = jnp.concatenate([w_a_proj[0], w_b_proj[0], w_out[0]], axis=0).astype(WIRE)
    l_send, l_recv, l_blk, l_land, l_started = _late_gather_start(late_blk, cos)
    wf = _build_wft(_join_frames(allw[:, :SHARD_PAD]))
    gu = allw[:, SHARD_PAD:G_ROWS, :64].transpose(1, 0, 2).reshape(RANK, 512)
    gu_pad = _pad_rows(gu, W_BL)

    proj = _proj(h, wf, l_started)
    o_a, lse = _swa_fwd(proj, cos, sin, a_sinks)
    o_b, states = _gla_fwd(proj, gu_pad, b_gate_bias)
    l_blk, l_land = _late_gather_wait(l_send, l_recv, l_blk, l_land, states, lse)
    late = lax.dynamic_update_slice(l_land, l_blk[None], (me, 0, 0))
    w_a, w_b, w_o = (late[:, 128 * i:128 * (i + 1), :].reshape(D, D) for i in range(3))
    (dx2, do_a, do_b, d_gates, g_wa, g_wb, g_wo, g_fn, g_bn, loss_part) = _mid(
        xs, target, proj, o_a, o_b, w_a, w_b, w_o, jnp.tile(b_out_norm_w, (1, B_HEADS)), fnw)
    d_q, d_kv, g_sinks = _swa_bwd(proj, cos, sin, a_sinks, do_a, o_a, lse, cos)
    d_gla, d_bl, g_gu, g_bias = _gla_bwd(proj, gu_pad, b_gate_bias, states, do_b)
    pieces = [d_q, d_kv, d_bl, d_gla, d_gates]
    offsets = [C_Q, C_KD, C_BL, C_GLA, C_GATES]

    ggu = g_gu[:RANK].reshape(RANK, NDEV, 64).transpose(1, 0, 2)
    ggu_half = [jnp.pad(ggu, ((0, 0), (0, 0), (0, DH - 64))), jnp.zeros((NDEV, RANK, DH), F32)]

    def pack(gw_half, hf):
        gwt = _unbuild_gwt(gw_half).astype(WIRE)
        cols = slice(hf * DH, (hf + 1) * DH)
        return jnp.concatenate([
            jnp.stack([gwt[FRAME * d:FRAME * d + SHARD_PAD] for d in range(NDEV)]),
            g_wa[:, cols].reshape(NDEV, 128, DH).astype(WIRE),
            g_wb[:, cols].reshape(NDEV, 128, DH).astype(WIRE),
            g_wo[:, cols].reshape(NDEV, 128, DH).astype(WIRE),
            ggu_half[hf].astype(WIRE)], axis=1)

    send0, recv0, s_thru0, land0, started0 = _chip_start(_pair_reduce(pack(_gw_half(h, pieces, 0), 0)), 0)
    send1, recv1, s_thru1, land1, started1 = _chip_start(
        _pair_reduce(pack(_gw_half(h, pieces, 1, after=started0), 1)), 1)
    grad_x, g_nw = _dh_norm(pieces, offsets, wf, xs, dx2, norm_w, started1)
    small = jnp.concatenate([g_nw, g_fn, _pad_cols(g_bias, D), _pad_cols(g_bn, D), _pad_cols(g_sinks, D),
                             _pad_cols(loss_part, D)], axis=0)
    sm_send, sm_recv, sm_blk, sm_land, sm_started = _late_gather_start(small, g_nw, name="small_gather")
    sums0, got0 = _chip_wait(send0, recv0, s_thru0, land0, sm_started, 0)
    sums1, got1 = _chip_wait(send1, recv1, s_thru1, land1, got0, 1)
    sums, from_chips = [sums0, sums1], [got0, got1]

    ws = dict(norm_w=norm_w, fnw=fnw, bias=b_gate_bias, bn=b_out_norm_w, sinks=a_sinks)
    ms = dict(norm_w=m_norm_w, fnw=m_final_norm_w.reshape(1, D), bias=m_b_gate_bias, bn=m_b_out_norm_w,
              sinks=m_a_sinks)
    vs = dict(norm_w=v_norm_w, fnw=v_final_norm_w.reshape(1, D), bias=v_b_gate_bias, bn=v_b_out_norm_w,
              sinks=v_a_sinks)
    t_rows, t_gu = _finish(
        [w_in[0].T, w_a_proj[0], w_b_proj[0], w_out[0]], [m_w_in[0].T, m_w_a_proj[0], m_w_b_proj[0], m_w_out[0]],
        [v_w_in[0].T, v_w_a_proj[0], v_w_b_proj[0], v_w_out[0]],
        b_gate_up[0], m_b_gate_up[0], v_b_gate_up[0], sums, from_chips)
    sm_blk, sm_land = _late_gather_wait(sm_send, sm_recv, sm_blk, sm_land, t_rows[0], t_gu[0], name="small_gather")
    loss, sm = _finish_small(ws, ms, vs, lax.dynamic_update_slice(sm_land, sm_blk[None], (me, 0, 0)))

    def outputs(k):
        return [sm["norm_w"][k], t_rows[k].T[None], sm["sinks"][k], t_gu[k][None], sm["bias"][k], sm["bn"][k],
                t_rows[4 + k][None], t_rows[8 + k][None], t_rows[12 + k][None], sm["fnw"][k].reshape(D)]

    return (loss[0, 0], grad_x[None], *outputs(0), *outputs(1), *outputs(2), *outputs(3))
```

```python
import functools

import numpy as np
import jax
import jax.numpy as jnp
from jax import lax
from jax.experimental import pallas as pl
from jax.experimental.pallas import tpu as pltpu

F32 = jnp.float32
MXU = jnp.bfloat16
WIRE = jnp.bfloat16

D = 1024
A_HEADS, A_KV, A_HD = 16, 2, 64
BLK = 128
B_HEADS, B_DK, B_DV = 4, 128, 256
RANK, TAU, CHUNK = 16, 16.0, 64
EPS, NEG = 1e-5, -1e30
ROPE_THETA = 10000.0
IN_WIDTH, NDEV = 7440, 8
SHARD = IN_WIDTH // NDEV
LANE = 128

C_Q, C_KD, C_VD, C_BL = 0, 1024, 1280, 1536
C_BV, C_BQ, C_BK = 2048, 3072, 3584
C_AG, C_BG, C_MA, C_MB = 4096, 5120, 6144, 7168
C_GLA, W_GLA, C_GATES, W_GATES = 2048, 2048, 4096, 4096
NF = 8192
W_BL = 128

SHARD_PAD = 944
R_IN, R_A, R_B, R_O, R_GU, ROWS = 0, 944, 1072, 1200, 1328, 1344
SMALL_ROWS = 48

ADAM_LR, ADAM_B1, ADAM_B2, ADAM_EPS, ADAM_WD, ADAM_STEP = 0.001, 0.9, 0.999, 1e-08, 0.01, 10

MESH = pl.DeviceIdType.MESH
VMEM_LIMIT = 56 * 1024 * 1024


def _cp(sem=None, **kw):
    if sem is not None:
        kw["dimension_semantics"] = sem
    return pltpu.CompilerParams(vmem_limit_bytes=VMEM_LIMIT, **kw)


def _dot(a, b):
    return jnp.dot(a, b, preferred_element_type=F32)


def _dot_nt(a, b):
    return lax.dot_general(a, b, (((1,), (1,)), ((), ())), preferred_element_type=F32)


def _dot_tn(a, b):
    return lax.dot_general(a, b, (((0,), (0,)), ((), ())), preferred_element_type=F32)


def _dot_f32(a, b):
    return jnp.dot(a, b, preferred_element_type=F32, precision=lax.Precision.HIGHEST)


def _sigmoid(z):
    return 0.5 * jnp.tanh(0.5 * z) + 0.5


def _rope(xp, cos, sin):
    return xp * cos + pltpu.roll(xp, 64, 1) * sin


def _rope_bwd(dy, cos, sin):
    return dy * cos - pltpu.roll(dy, 64, 1) * sin


def _vmem():
    return pl.BlockSpec(memory_space=pltpu.VMEM)


def _any():
    return pl.BlockSpec(memory_space=pl.ANY)


def _rope_rows():
    half = A_HD // 2
    inv = (np.float32(ROPE_THETA) ** (-np.arange(half, dtype=np.float32) / np.float32(half))).astype(np.float32)
    inv_row = jnp.asarray(np.tile(inv, 4)[None, :])
    sign_row = jnp.asarray(np.concatenate([-np.ones(64, np.float32), np.ones(64, np.float32)])[None, :])
    return inv_row, sign_row


def _prologue_rows(rows, x_ref, nw_ref, pos_ref, inv_ref, sign_ref, h_ref, cos_ref, sin_ref):
    xv = x_ref[rows, :]
    r = lax.rsqrt(jnp.mean(xv * xv, axis=-1, keepdims=True) + EPS)
    h_ref[rows, :] = ((xv * r) * nw_ref[...]).astype(h_ref.dtype)
    ang = pos_ref[rows, :].astype(F32) * inv_ref[...]
    cos_ref[rows, :] = jnp.cos(ang)
    sin_ref[rows, :] = jnp.sin(ang) * sign_ref[...]


def _proj(h, wft, after):
    T = h.shape[0]
    tT, tN = T, 512

    def body(h_ref, w_ref, after_ref, o_ref):
        o_ref[...] = _dot_nt(h_ref[...], w_ref[...])

    return pl.pallas_call(
        body, name="proj", grid=(T // tT, NF // tN),
        in_specs=[pl.BlockSpec((tT, D), lambda i, j: (i, 0)), pl.BlockSpec((tN, D), lambda i, j: (j, 0)), _any()],
        out_specs=pl.BlockSpec((tT, tN), lambda i, j: (i, j)),
        out_shape=jax.ShapeDtypeStruct((T, NF), F32),
        compiler_params=_cp(("parallel", "parallel")),
    )(h, wft, after)


def _swa_masks():
    lane = lax.broadcasted_iota(jnp.int32, (BLK, LANE), 1)
    rope_sub0 = ((lane // 32) % 2) == 0
    std_sub0 = lane < 64
    return lane, rope_sub0, std_sub0


def _swa_tri():
    qi = lax.broadcasted_iota(jnp.int32, (BLK, BLK), 0)
    kj = lax.broadcasted_iota(jnp.int32, (BLK, BLK), 1)
    return kj <= qi


def _swa_fold(full, tri):
    return jnp.where(tri, full[:, BLK:], full[:, :BLK])


def _swa_unfold(sq, tri):
    return jnp.concatenate([jnp.where(tri, 0.0, sq), jnp.where(tri, sq, 0.0)], axis=1)


def _swa_keys(kc_ref, kp_ref, vc_ref, vp_ref, cq, sq, cp, sp):
    def ropek(kref, c, s):
        kv = kref[...]
        return jnp.concatenate([_rope(kv[:, :LANE], c, s), _rope(kv[:, LANE:], c, s)], axis=1)

    K = jnp.concatenate([ropek(kp_ref, cp, sp), ropek(kc_ref, cq, sq)], axis=0).astype(MXU)
    V = jnp.concatenate([vp_ref[...], vc_ref[...]], axis=0).astype(MXU)
    return K, V


def _swa_in_specs(nb, last):
    def cur(n):
        return jnp.minimum(n, last)

    def prev(n):
        return jnp.maximum(cur(n) - 1, 0)

    kd, vd = C_KD // 256, C_VD // 256
    return [
        pl.BlockSpec((BLK, D), lambda n: (cur(n), C_Q // D)),
        pl.BlockSpec((BLK, 256), lambda n: (cur(n), kd)),
        pl.BlockSpec((BLK, 256), lambda n: (prev(n), kd)),
        pl.BlockSpec((BLK, 256), lambda n: (cur(n), vd)),
        pl.BlockSpec((BLK, 256), lambda n: (prev(n), vd)),
        pl.BlockSpec((BLK, LANE), lambda n: (cur(n), 0)),
        pl.BlockSpec((BLK, LANE), lambda n: (cur(n), 0)),
        pl.BlockSpec((BLK, LANE), lambda n: (prev(n), 0)),
        pl.BlockSpec((BLK, LANE), lambda n: (prev(n), 0)),
    ]


def _swa_fwd(proj, cos, sin, sinks):
    T = proj.shape[0]
    nb = T // BLK
    scale = A_HD ** -0.5

    def body(sinks_ref, q_ref, kc_ref, kp_ref, vc_ref, vp_ref, cq_ref, sq_ref, cp_ref, sp_ref, o_ref, l_ref):
        n = pl.program_id(0)
        cq, sq = cq_ref[...], sq_ref[...]
        K, V = _swa_keys(kc_ref, kp_ref, vc_ref, vp_ref, cq, sq, cp_ref[...], sp_ref[...])
        tri = _swa_tri()
        valid = tri | (n > 0)
        lane, rope_sub0, std_sub0 = _swa_masks()
        group = A_HEADS // A_KV
        roped, lses = {}, []

        def products(head):
            pb, sub, g = head // 2, head % 2, head // group
            if sub == 0:
                roped[pb] = _rope(q_ref[:, pb * LANE:(pb + 1) * LANE], cq, sq)
            qm = jnp.where(rope_sub0 if sub == 0 else ~rope_sub0, roped[pb], 0.0).astype(MXU)
            return _dot_nt(qm, K[:, g * LANE:(g + 1) * LANE])

        def softmax(head, s_full):
            s = jnp.where(valid, _swa_fold(s_full, tri) * scale, NEG)
            sink = sinks_ref[0, head]
            m = jnp.maximum(jnp.max(s, axis=1, keepdims=True), sink)
            e = jnp.exp(s - m)
            den = jnp.sum(e, axis=1, keepdims=True) + jnp.exp(sink - m)
            lses.append(m + jnp.log(den))
            return _swa_unfold(e / den, tri).astype(MXU)

        outs = {}
        st1 = {0: products(0), 1: products(1)}
        st2 = {0: softmax(0, st1.pop(0))}
        for head in range(A_HEADS):
            if head + 2 < A_HEADS:
                st1[head + 2] = products(head + 2)
            if head + 1 < A_HEADS:
                st2[head + 1] = softmax(head + 1, st1.pop(head + 1))
            g = head // group
            outs[head] = _dot(st2.pop(head), V[:, g * LANE:(g + 1) * LANE])
            if head % 2 == 1:
                pb = head // 2
                o_ref[:, pb * LANE:(pb + 1) * LANE] = jnp.where(std_sub0, outs[head - 1], outs[head])
        lacc = jnp.zeros((BLK, LANE), F32)
        for head in range(A_HEADS):
            lacc = jnp.where(lane == head, lses[head], lacc)
        l_ref[...] = lacc

    return pl.pallas_call(
        body, name="swa_fwd", grid=(nb,),
        in_specs=[pl.BlockSpec(memory_space=pltpu.SMEM)] + _swa_in_specs(nb, nb - 1),
        out_specs=[pl.BlockSpec((BLK, D), lambda n: (n, 0)), pl.BlockSpec((BLK, LANE), lambda n: (n, 0))],
        out_shape=[jax.ShapeDtypeStruct((T, D), F32), jax.ShapeDtypeStruct((T, LANE), F32)],
        compiler_params=_cp(("parallel",)),
    )(sinks, proj, proj, proj, proj, proj, cos, sin, cos, sin)


def _swa_bwd(proj, cos, sin, sinks, do_a, o_a, lse, after):
    T = proj.shape[0]
    nb = T // BLK
    scale = A_HD ** -0.5

    def body(sinks_ref, q_ref, kc_ref, kp_ref, vc_ref, vp_ref, cq_ref, sq_ref, cp_ref, sp_ref,
             do_ref, o_ref, l_ref, after_ref, dq_ref, dkv_ref, ds_ref, ckv_ref):
        n = pl.program_id(0)

        @pl.when(n == 0)
        def _():
            ckv_ref[...] = jnp.zeros_like(ckv_ref)
            ds_ref[...] = jnp.zeros_like(ds_ref)

        @pl.when(n < nb)
        def _():
            cq, sq, cp, sp = cq_ref[...], sq_ref[...], cp_ref[...], sp_ref[...]
            K, V = _swa_keys(kc_ref, kp_ref, vc_ref, vp_ref, cq, sq, cp, sp)
            tri = _swa_tri()
            valid = tri | (n > 0)
            lane, rope_sub0, std_sub0 = _swa_masks()
            lane_row = lax.broadcasted_iota(jnp.int32, (1, LANE), 1)
            lse_v = l_ref[...]
            dKt = [jnp.zeros((LANE, 2 * BLK), F32) for _ in range(A_KV)]
            dVt = [jnp.zeros((LANE, 2 * BLK), F32) for _ in range(A_KV)]
            dsinks, roped, roped_t, do_t = [], {}, {}, {}
            group = A_HEADS // A_KV
            dim = lax.broadcasted_iota(jnp.int32, (LANE, BLK), 0)
            rope_row0, std_row0 = ((dim // 32) % 2) == 0, dim < 64

            def products(head):
                pb, sub, g = head // 2, head % 2, head // group
                cols = slice(pb * LANE, (pb + 1) * LANE)
                Kg, Vg = K[:, g * LANE:(g + 1) * LANE], V[:, g * LANE:(g + 1) * LANE]
                if sub == 0:
                    roped[pb] = _rope(q_ref[:, cols], cq, sq)
                    roped_t[pb] = roped[pb].T
                    do_t[pb] = do_ref[:, cols].T
                qm = jnp.where(rope_sub0 if sub == 0 else ~rope_sub0, roped[pb], 0.0).astype(MXU)
                qmt = jnp.where(rope_row0 if sub == 0 else ~rope_row0, roped_t[pb], 0.0).astype(MXU)
                dov = jnp.where(std_sub0 if sub == 0 else ~std_sub0, do_ref[:, cols], 0.0)
                dovt = jnp.where(std_row0 if sub == 0 else ~std_row0, do_t[pb], 0.0).astype(MXU)
                delta = jnp.sum(dov * o_ref[:, cols], axis=1, keepdims=True)
                return qmt, dovt, delta, _dot_nt(qm, Kg), _dot_nt(dov.astype(MXU), Vg)

            def scores(head, qmt, dovt, delta, s_full, dp_full):
                lh = jnp.sum(jnp.where(lane == head, lse_v, 0.0), axis=1, keepdims=True)
                p = jnp.where(valid, jnp.exp(_swa_fold(s_full, tri) * scale - lh), 0.0)
                psink = jnp.exp(sinks_ref[0, head] - lh)
                dsinks.append(jnp.sum(-psink * delta, axis=0, keepdims=True))
                dsq = (p * (_swa_fold(dp_full, tri) - delta)) * scale
                return qmt, dovt, _swa_unfold(p, tri).astype(MXU), _swa_unfold(dsq, tri).astype(MXU)

            def grads(head, qmt, dovt, pb16, dsc):
                g = head // group
                dKt[g] = dKt[g] + _dot(qmt, dsc)
                dVt[g] = dVt[g] + _dot(dovt, pb16)
                return _dot(dsc, K[:, g * LANE:(g + 1) * LANE])

            dqs = {}
            st1 = {0: products(0), 1: products(1)}
            st2 = {0: scores(0, *st1.pop(0))}
            for head in range(A_HEADS):
                if head + 2 < A_HEADS:
                    st1[head + 2] = products(head + 2)
                if head + 1 < A_HEADS:
                    st2[head + 1] = scores(head + 1, *st1.pop(head + 1))
                dqs[head] = grads(head, *st2.pop(head))
                if head % 2 == 1:
                    pb = head // 2
                    dqp = jnp.where(rope_sub0, dqs[head - 1], dqs[head])
                    dq_ref[:, pb * LANE:(pb + 1) * LANE] = _rope_bwd(dqp, cq, sq).astype(dq_ref.dtype)
            dsink = jnp.zeros((1, LANE), F32)
            for head in range(A_HEADS):
                dsink = jnp.where(lane_row == head, dsinks[head], dsink)
            dK, dV = [a.T for a in dKt], [a.T for a in dVt]
            prev = ([_rope_bwd(dK[g][:BLK], cp, sp) for g in range(A_KV)] + [dV[g][:BLK] for g in range(A_KV)])
            cur_ = ([_rope_bwd(dK[g][BLK:], cq, sq) for g in range(A_KV)] + [dV[g][BLK:] for g in range(A_KV)])
            dkv_ref[...] = (ckv_ref[...] + jnp.concatenate(prev, axis=1)).astype(dkv_ref.dtype)
            ckv_ref[...] = jnp.concatenate(cur_, axis=1)
            ds_ref[...] = ds_ref[...] + jnp.broadcast_to(dsink, ds_ref.shape)

        @pl.when(n == nb)
        def _():
            dkv_ref[...] = ckv_ref[...].astype(dkv_ref.dtype)

    last = nb - 1

    def cur(n):
        return jnp.minimum(n, last)

    def out_kv(n):
        return (jnp.maximum(n - 1, 0), 0)

    return pl.pallas_call(
        body, name="swa_bwd", grid=(nb + 1,),
        in_specs=[pl.BlockSpec(memory_space=pltpu.SMEM)] + _swa_in_specs(nb, last) + [
            pl.BlockSpec((BLK, D), lambda n: (cur(n), 0)),
            pl.BlockSpec((BLK, D), lambda n: (cur(n), 0)),
            pl.BlockSpec((BLK, LANE), lambda n: (cur(n), 0)),
            _any(),
        ],
        out_specs=[
            pl.BlockSpec((BLK, D), lambda n: (cur(n), 0)),
            pl.BlockSpec((BLK, 512), out_kv),
            pl.BlockSpec((8, LANE), lambda n: (0, 0)),
        ],
        out_shape=[
            jax.ShapeDtypeStruct((T, D), MXU),
            jax.ShapeDtypeStruct((T, 512), MXU),
            jax.ShapeDtypeStruct((8, LANE), F32),
        ],
        scratch_shapes=[pltpu.VMEM((BLK, 512), F32)],
        compiler_params=_cp(("arbitrary",)),
    )(sinks, proj, proj, proj, proj, proj, cos, sin, cos, sin, do_a, o_a, lse, after)


GSTEP = 2 * CHUNK
ST_ROWS = B_HEADS * B_DV


def _gla_gate(bl_ref, gu_ref, bias_ref):
    gk = _dot(bl_ref[...].astype(MXU), gu_ref[...]) + bias_ref[...]
    la = (jnp.minimum(gk, 0.0) - jnp.log(1.0 + jnp.exp(-jnp.abs(gk)))) / TAU
    ri = lax.broadcasted_iota(jnp.int32, (GSTEP, GSTEP), 0)
    ci = lax.broadcasted_iota(jnp.int32, (GSTEP, GSTEP), 1)
    same = (ri // CHUNK) == (ci // CHUNK)
    lower, upper = same & (ci <= ri), same & (ci >= ri)
    b = _dot_f32(jnp.where(lower, 1.0, 0.0).astype(F32), la)
    first = lax.broadcasted_iota(jnp.int32, (GSTEP, 1), 0) < CHUNK
    return gk, la, b, lower, upper, first


def _gla_head(q_ref, k_ref, la, b, first, h):
    sl = slice(h * B_DK, (h + 1) * B_DK)
    bh, lah = b[:, sl], la[:, sl]
    bl_a = jnp.sum(lah[:CHUNK], axis=0, keepdims=True)
    bl_b = jnp.sum(lah[CHUNK:], axis=0, keepdims=True)
    blast = jnp.where(first, bl_a, bl_b)
    qc = q_ref[:, sl] * (B_DK ** -0.5)
    kh = k_ref[:, sl]
    eb, enb, esb = jnp.exp(bh), jnp.exp(-bh), jnp.exp(blast - bh)
    return qc * eb, kh * enb, kh * esb, eb, enb, esb, (jnp.exp(bl_a), jnp.exp(bl_b))


def _gla_specs(step_of):
    return [
        pl.BlockSpec((GSTEP, 512), lambda i: (step_of(i), C_BQ // 512)),
        pl.BlockSpec((GSTEP, 512), lambda i: (step_of(i), C_BK // 512)),
        pl.BlockSpec((GSTEP, D), lambda i: (step_of(i), C_BV // D)),
        pl.BlockSpec((GSTEP, W_BL), lambda i: (step_of(i), C_BL // W_BL)),
        pl.BlockSpec((W_BL, 512), lambda i: (0, 0)),
        pl.BlockSpec((1, 512), lambda i: (0, 0)),
    ]


def _gla_fwd(proj, gu_pad, bias):
    T = proj.shape[0]
    ns = T // GSTEP

    def body(q_ref, k_ref, v_ref, bl_ref, gu_ref, bias_ref, o_ref, st_ref, state_ref):
        @pl.when(pl.program_id(0) == 0)
        def _():
            state_ref[...] = jnp.zeros_like(state_ref)

        _, la, b, lower, _, first = _gla_gate(bl_ref, gu_ref, bias_ref)
        st_ref[0:ST_ROWS, :] = state_ref[...]

        def within(h):
            q_e, k_e, k_s, _, _, _, decays = _gla_head(q_ref, k_ref, la, b, first, h)
            vh = v_ref[:, h * B_DV:(h + 1) * B_DV].astype(MXU)
            q_eb = q_e.astype(MXU)
            att = jnp.where(lower, _dot_nt(q_eb, k_e.astype(MXU)), 0.0)
            return vh, q_eb, k_s.astype(MXU), _dot(att.astype(MXU), vh), decays

        def across(h, vh, q_eb, k_sb, o_intra, decays):
            rows = slice(h * B_DV, (h + 1) * B_DV)
            s0 = state_ref[rows, :]
            o_a = o_intra[:CHUNK] + _dot_nt(q_eb[:CHUNK], s0.astype(MXU))
            s1 = s0 * decays[0] + _dot_tn(vh[:CHUNK], k_sb[:CHUNK])
            st_ref[ST_ROWS + h * B_DV:ST_ROWS + (h + 1) * B_DV, :] = s1
            o_b = o_intra[CHUNK:] + _dot_nt(q_eb[CHUNK:], s1.astype(MXU))
            state_ref[rows, :] = s1 * decays[1] + _dot_tn(vh[CHUNK:], k_sb[CHUNK:])
            o_ref[:, rows] = jnp.concatenate([o_a, o_b], axis=0)

        for h in range(B_HEADS):
            across(h, *within(h))

    return pl.pallas_call(
        body, name="gla_fwd", grid=(ns,),
        in_specs=_gla_specs(lambda i: i),
        out_specs=[pl.BlockSpec((GSTEP, D), lambda i: (i, 0)),
                   pl.BlockSpec((2 * ST_ROWS, B_DK), lambda i: (i, 0))],
        out_shape=[jax.ShapeDtypeStruct((T, D), F32),
                   jax.ShapeDtypeStruct((ns * 2 * ST_ROWS, B_DK), F32)],
        scratch_shapes=[pltpu.VMEM((ST_ROWS, B_DK), F32)],
        compiler_params=_cp(("arbitrary",)),
    )(proj, proj, proj, proj, gu_pad, bias)


def _gla_bwd(proj, gu_pad, bias, states, do_b):
    T = proj.shape[0]
    ns = T // GSTEP
    o_q, o_k = C_BQ - C_GLA, C_BK - C_GLA

    def body(q_ref, k_ref, v_ref, bl_ref, gu_ref, bias_ref, st_ref, do_ref,
             dg_ref, dbl_ref, ggu_ref, gbias_ref, gt_ref):
        @pl.when(pl.program_id(0) == 0)
        def _():
            gt_ref[...] = jnp.zeros_like(gt_ref)
            ggu_ref[...] = jnp.zeros_like(ggu_ref)
            gbias_ref[...] = jnp.zeros_like(gbias_ref)

        gk, la, b, lower, upper_mask, first = _gla_gate(bl_ref, gu_ref, bias_ref)
        upper = jnp.where(upper_mask, 1.0, 0.0).astype(F32)
        lo, hi = slice(0, CHUNK), slice(CHUNK, GSTEP)
        dla_parts = []

        def within(h):
            q_e, k_e, k_s, eb, enb, esb, decays = _gla_head(q_ref, k_ref, la, b, first, h)
            vh = v_ref[:, h * B_DV:(h + 1) * B_DV].astype(MXU)
            doh = do_ref[:, h * B_DV:(h + 1) * B_DV].astype(MXU)
            q_eb, k_eb = q_e.astype(MXU), k_e.astype(MXU)
            att = jnp.where(lower, _dot_nt(q_eb, k_eb), 0.0).astype(MXU)
            datt = jnp.where(lower, _dot_nt(doh, vh), 0.0).astype(MXU)
            return (q_e, k_e, k_s, eb, enb, esb, decays, vh, doh, q_eb, k_s.astype(MXU),
                    _dot(datt, k_eb), _dot_tn(datt, q_eb), _dot_tn(att, doh))

        def across(h, q_e, k_e, k_s, eb, enb, esb, decays, vh, doh, q_eb, k_sb, dq_i, dk_e, dv_i):
            dec_a, dec_b = decays
            rows = slice(h * B_DV, (h + 1) * B_DV)
            s0 = st_ref[rows, :]
            s1 = st_ref[ST_ROWS + h * B_DV:ST_ROWS + (h + 1) * B_DV, :]
            g2 = gt_ref[rows, :]
            g2b = g2.astype(MXU)
            dq_b = dq_i[hi] + _dot(doh[hi], s1.astype(MXU))
            dks_b = _dot(vh[hi], g2b)
            dv_b = dv_i[hi] + _dot_nt(k_sb[hi], g2b)
            ddec_b = jnp.sum(g2 * s1, axis=0, keepdims=True)
            g1 = g2 * dec_b + _dot_tn(doh[hi], q_eb[hi])
            g1b = g1.astype(MXU)
            dq_a = dq_i[lo] + _dot(doh[lo], s0.astype(MXU))
            dks_a = _dot(vh[lo], g1b)
            dv_a = dv_i[lo] + _dot_nt(k_sb[lo], g1b)
            ddec_a = jnp.sum(g1 * s0, axis=0, keepdims=True)
            gt_ref[rows, :] = g1 * dec_a + _dot_tn(doh[lo], q_eb[lo])
            dq_e = jnp.concatenate([dq_a, dq_b], axis=0)
            dk_s = jnp.concatenate([dks_a, dks_b], axis=0)
            dg_ref[:, rows] = jnp.concatenate([dv_a, dv_b], axis=0).astype(dg_ref.dtype)
            dg_ref[:, o_q + h * B_DK:o_q + (h + 1) * B_DK] = (dq_e * eb * (B_DK ** -0.5)).astype(dg_ref.dtype)
            dg_ref[:, o_k + h * B_DK:o_k + (h + 1) * B_DK] = (dk_e * enb + dk_s * esb).astype(dg_ref.dtype)
            dks_ks = dk_s * k_s
            db = dq_e * q_e - dk_e * k_e - dks_ks
            dbl_a = jnp.sum(dks_ks[lo], axis=0, keepdims=True) + ddec_a * dec_a
            dbl_b = jnp.sum(dks_ks[hi], axis=0, keepdims=True) + ddec_b * dec_b
            dla_parts.append(_dot_f32(upper, db) + jnp.where(first, dbl_a, dbl_b))

        for h in range(B_HEADS):
            across(h, *within(h))
        dla = jnp.concatenate(dla_parts, axis=1)
        dgk = dla * (1.0 / TAU) * _sigmoid(-gk)
        dgkb = dgk.astype(MXU)
        dbl_ref[...] = _dot_nt(dgkb, gu_ref[...]).astype(dbl_ref.dtype)
        ggu_ref[...] = ggu_ref[...] + _dot_tn(bl_ref[...].astype(MXU), dgkb)
        gbias_ref[...] = gbias_ref[...] + jnp.broadcast_to(jnp.sum(dgk, axis=0, keepdims=True), gbias_ref.shape)

    def rev(i):
        return ns - 1 - i

    return pl.pallas_call(
        body, name="gla_bwd", grid=(ns,),
        in_specs=_gla_specs(rev) + [
            pl.BlockSpec((2 * ST_ROWS, B_DK), lambda i: (rev(i), 0)),
            pl.BlockSpec((GSTEP, D), lambda i: (rev(i), 0)),
        ],
        out_specs=[
            pl.BlockSpec((GSTEP, W_GLA), lambda i: (rev(i), 0)),
            pl.BlockSpec((GSTEP, W_BL), lambda i: (rev(i), 0)),
            pl.BlockSpec((W_BL, 512), lambda i: (0, 0)),
            pl.BlockSpec((8, 512), lambda i: (0, 0)),
        ],
        out_shape=[
            jax.ShapeDtypeStruct((T, W_GLA), MXU),
            jax.ShapeDtypeStruct((T, W_BL), MXU),
            jax.ShapeDtypeStruct((W_BL, 512), F32),
            jax.ShapeDtypeStruct((8, 512), F32),
        ],
        scratch_shapes=[pltpu.VMEM((B_HEADS * B_DV, B_DK), F32)],
        compiler_params=_cp(("arbitrary",)),
    )(proj, proj, proj, proj, gu_pad, bias, states, do_b)


def _mid(x, target, proj, o_a, o_b, w_a, w_b, w_out, w_bn4, fnw):
    T = x.shape[0]
    tT = min(T, 128)
    nbuf = 4
    o_ag, o_bg, o_ma, o_mb = (c - C_GATES for c in (C_AG, C_BG, C_MA, C_MB))

    def body(x_ref, t_ref, oa_ref, ob_ref, gates_ref, wa_ref, wb_ref, wo_ref, wbn_ref, fnw_ref,
             dx2_ref, doa_ref, dob_ref, dgates_ref,
             gwa_ref, gwb_ref, gwo_ref, gfn_ref, gbn_ref, loss_ref, buf_ref):
        i = pl.program_id(0)

        @pl.when(i == 0)
        def _():
            for r in (gwa_ref, gwb_ref, gwo_ref, gfn_ref, gbn_ref, loss_ref):
                r[...] = jnp.zeros_like(r)

        rows = pl.ds(pl.multiple_of((i % nbuf) * tT, tT), tT)

        def keep(k, val):
            buf_ref[k, rows, :] = val

        oa, ag = oa_ref[...], gates_ref[:, o_ag:o_ag + D]
        sg_a = _sigmoid(ag)
        silu_a = ag * sg_a
        oag_b = (oa * silu_a).astype(MXU)
        keep(0, oag_b)
        y_a = _dot(oag_b, wa_ref[...])

        ob, bg = ob_ref[...], gates_ref[:, o_bg:o_bg + D]
        rbs, obhats = [], []
        for h in range(B_HEADS):
            obh = ob[:, h * B_DV:(h + 1) * B_DV]
            rb = lax.rsqrt(jnp.mean(obh * obh, axis=-1, keepdims=True) + EPS)
            rbs.append(rb)
            obhats.append(obh * rb)
        obhat = jnp.concatenate(obhats, axis=1)
        wbn = wbn_ref[...]
        obn = obhat * wbn
        sg_b = _sigmoid(bg)
        silu_b = bg * sg_b
        obg_b = (obn * silu_b).astype(MXU)
        keep(1, obg_b)
        y_b = _dot(obg_b, wb_ref[...])

        sa, sb = _sigmoid(gates_ref[:, o_ma:o_ma + D]), _sigmoid(gates_ref[:, o_mb:o_mb + D])
        mg_b = (sa * y_a + sb * y_b).astype(MXU)
        keep(2, mg_b)
        x2 = x_ref[...] + _dot(mg_b, wo_ref[...])
        r2 = lax.rsqrt(jnp.mean(x2 * x2, axis=-1, keepdims=True) + EPS)
        xh2 = x2 * r2
        fw = fnw_ref[...]
        err = xh2 * fw - t_ref[...]
        tok = jnp.mean(err * err, axis=-1, keepdims=True)
        loss_ref[...] = loss_ref[...] + 0.5 * jnp.sum(tok, axis=0, keepdims=True)

        dy = err * (1.0 / D)
        gfn_ref[...] = gfn_ref[...] + jnp.broadcast_to(jnp.sum(dy * xh2, axis=0, keepdims=True), gfn_ref.shape)
        gy = dy * fw
        dx2 = r2 * (gy - xh2 * jnp.mean(gy * xh2, axis=-1, keepdims=True))
        dx2_ref[...] = dx2
        dx2_b = dx2.astype(MXU)
        keep(5, dx2_b)
        dmg = _dot_nt(dx2_b, wo_ref[...])

        dgates_ref[:, o_ma:o_ma + D] = (dmg * y_a * sa * (1.0 - sa)).astype(dgates_ref.dtype)
        dgates_ref[:, o_mb:o_mb + D] = (dmg * y_b * sb * (1.0 - sb)).astype(dgates_ref.dtype)
        dya_b = (dmg * sa).astype(MXU)
        dyb_b = (dmg * sb).astype(MXU)
        keep(3, dya_b)
        keep(4, dyb_b)
        doag = _dot_nt(dya_b, wa_ref[...])
        dobg = _dot_nt(dyb_b, wb_ref[...])

        @pl.when(i % nbuf == nbuf - 1)
        def _():
            gwa_ref[...] = gwa_ref[...] + _dot_tn(buf_ref[0], buf_ref[3])
            gwb_ref[...] = gwb_ref[...] + _dot_tn(buf_ref[1], buf_ref[4])
            gwo_ref[...] = gwo_ref[...] + _dot_tn(buf_ref[2], buf_ref[5])

        doa_ref[...] = doag * silu_a
        dgates_ref[:, o_ag:o_ag + D] = (doag * oa * (sg_a * (1.0 + ag * (1.0 - sg_a)))).astype(dgates_ref.dtype)
        dobn = dobg * silu_b
        dgates_ref[:, o_bg:o_bg + D] = (dobg * obn * (sg_b * (1.0 + bg * (1.0 - sg_b)))).astype(dgates_ref.dtype)
        gg = dobn * wbn
        gbn = jnp.zeros((1, B_DV), F32)
        for h in range(B_HEADS):
            sl = slice(h * B_DV, (h + 1) * B_DV)
            gbn = gbn + jnp.sum(dobn[:, sl] * obhats[h], axis=0, keepdims=True)
            ggh = gg[:, sl]
            dob_ref[:, sl] = rbs[h] * (ggh - obhats[h] * jnp.mean(ggh * obhats[h], axis=-1, keepdims=True))
        gbn_ref[...] = gbn_ref[...] + jnp.broadcast_to(gbn, gbn_ref.shape)

    assert (T // tT) % nbuf == 0
    tile = pl.BlockSpec((tT, D), lambda i: (i, 0))
    row = pl.BlockSpec((1, D), lambda i: (0, 0))
    acc8 = pl.BlockSpec((8, D), lambda i: (0, 0))
    return pl.pallas_call(
        body, name="mid", grid=(T // tT,),
        in_specs=[tile, tile, tile, tile, pl.BlockSpec((tT, W_GATES), lambda i: (i, C_GATES // W_GATES)),
                  _vmem(), _vmem(), _vmem(), row, row],
        out_specs=[tile, tile, tile, pl.BlockSpec((tT, W_GATES), lambda i: (i, 0)), _vmem(), _vmem(), _vmem(),
                   acc8, pl.BlockSpec((8, B_DV), lambda i: (0, 0)), pl.BlockSpec((8, LANE), lambda i: (0, 0))],
        out_shape=[
            jax.ShapeDtypeStruct((T, D), F32),
            jax.ShapeDtypeStruct((T, D), F32),
            jax.ShapeDtypeStruct((T, D), F32),
            jax.ShapeDtypeStruct((T, W_GATES), MXU),
            jax.ShapeDtypeStruct((D, D), F32),
            jax.ShapeDtypeStruct((D, D), F32),
            jax.ShapeDtypeStruct((D, D), F32),
            jax.ShapeDtypeStruct((8, D), F32),
            jax.ShapeDtypeStruct((8, B_DV), F32),
            jax.ShapeDtypeStruct((8, LANE), F32),
        ],
        scratch_shapes=[pltpu.VMEM((6, nbuf * tT, D), MXU)],
        compiler_params=_cp(("arbitrary",)),
    )(x, target, o_a, o_b, proj, w_a, w_b, w_out, w_bn4, fnw)


DH = D // 2


def _gw_half(h, pieces, half, after=None):
    T = h.shape[0]
    steps = NF // 512
    tiles = ((0, 2), (2, 3), (4, 8), (8, 16))

    def body(*refs):
        h_ref, q_ref, kv_ref, bl_ref, gla_ref, gates_ref = refs[:6]
        o_ref = refs[-1]
        j = pl.program_id(0)

        for (lo, hi), ref in zip(tiles, (q_ref, kv_ref, gla_ref, gates_ref)):
            @pl.when((j >= lo) & (j < hi))
            def _(ref=ref):
                o_ref[...] = _dot_tn(ref[...], h_ref[...])

        @pl.when(j == 3)
        def _():
            o_ref[0:W_BL, :] = _dot_tn(bl_ref[...], h_ref[...])
            o_ref[W_BL:, :] = jnp.zeros((512 - W_BL, DH), F32)

    def tile_of(lo, hi):
        return lambda j: (0, jnp.clip(j - lo, 0, hi - lo - 1))

    in_specs = [pl.BlockSpec((T, DH), lambda j: (0, half)),
                pl.BlockSpec((T, 512), tile_of(0, 2)), pl.BlockSpec((T, 512), lambda j: (0, 0)),
                pl.BlockSpec((T, W_BL), lambda j: (0, 0)),
                pl.BlockSpec((T, 512), tile_of(4, 8)), pl.BlockSpec((T, 512), tile_of(8, 16))]
    args = [h, *pieces]
    if after is not None:
        in_specs.append(_any())
        args.append(after)
    return pl.pallas_call(
        body, name=f"gw_in_half{half}", grid=(steps,),
        in_specs=in_specs, out_specs=pl.BlockSpec((512, DH), lambda j: (j, 0)),
        out_shape=jax.ShapeDtypeStruct((NF, DH), F32),
        compiler_params=_cp(("parallel",)),
    )(*args)


def _chip_copies(s_ref, got_ref, send_sems, recv_sems):
    x, y, c = _place()
    chips = [(1 - x, y), (x, 1 - y), (1 - x, 1 - y)]
    return [pltpu.make_async_remote_copy(
        src_ref=s_ref.at[2 * px + py], dst_ref=got_ref.at[j],
        send_sem=send_sems.at[j], recv_sem=recv_sems.at[j], device_id=(px, py, c), device_id_type=MESH)
        for j, (px, py) in enumerate(chips)]


_EFFECT = pltpu.SideEffectType.DATAFLOW_SIDE_EFFECTING


def _hbm():
    return pl.BlockSpec(memory_space=pltpu.HBM)


def _sem():
    return pl.BlockSpec(memory_space=pltpu.SEMAPHORE)


def _chip_start(sums, half):
    land = pltpu.with_memory_space_constraint(lax.empty((3,) + sums.shape[1:], sums.dtype), pltpu.HBM)

    def body(s_ref, land_ref, send_sems, recv_sems, s_thru, land_thru, token):
        for cp in _chip_copies(s_ref, land_ref, send_sems, recv_sems):
            cp.start()
        token[...] = jnp.zeros_like(token)

    return pl.pallas_call(
        body, name=f"chip_start{half}",
        out_shape=(pltpu.SemaphoreType.DMA((3,)), pltpu.SemaphoreType.DMA((3,)),
                   pltpu.HBM(sums.shape, sums.dtype), pltpu.HBM(land.shape, land.dtype),
                   jax.ShapeDtypeStruct((8, LANE), F32)),
        in_specs=(_hbm(), _hbm()), out_specs=(_sem(), _sem(), _hbm(), _hbm(), _vmem()),
        input_output_aliases={0: 2, 1: 3},
        compiler_params=pltpu.CompilerParams(has_side_effects=_EFFECT),
    )(pltpu.with_memory_space_constraint(sums, pltpu.HBM), land)


def _chip_wait(send_sems, recv_sems, s_thru, land_thru, after, half):
    def body(s_ref, land_ref, send_sems, recv_sems, after_ref, s_out, got_ref):
        copies = _chip_copies(s_ref, land_ref, send_sems, recv_sems)
        for cp in copies:
            cp.wait_send()
        for cp in copies:
            cp.wait_recv()

    return pl.pallas_call(
        body, name=f"chip_wait{half}",
        out_shape=(pltpu.HBM(s_thru.shape, s_thru.dtype), pltpu.HBM(land_thru.shape, land_thru.dtype)),
        in_specs=(_hbm(), _hbm(), _sem(), _sem(), _any()), out_specs=(_hbm(), _hbm()),
        input_output_aliases={0: 0, 1: 1},
        compiler_params=pltpu.CompilerParams(has_side_effects=_EFFECT),
    )(s_thru, land_thru, send_sems, recv_sems, after)


def _dh_norm(pieces, offsets, wf, x, dx2, norm_w, after):
    T = x.shape[0]
    tT = min(T, 256)
    widths = [p.shape[1] for p in pieces]
    npc = len(pieces)

    def body(*refs):
        dp_refs = refs[:npc]
        wf_ref, x_ref, dx2_ref, nw_ref, _, gx_ref, gnw_ref = refs[npc:]

        @pl.when(pl.program_id(0) == 0)
        def _():
            gnw_ref[...] = jnp.zeros_like(gnw_ref)

        dh = jnp.zeros((tT, D), F32)
        for dp_ref, off, w in zip(dp_refs, offsets, widths):
            dh = dh + _dot(dp_ref[...], wf_ref[off:off + w, :])
        xv = x_ref[...]
        r = lax.rsqrt(jnp.mean(xv * xv, axis=-1, keepdims=True) + EPS)
        xh = xv * r
        gnw_ref[...] = gnw_ref[...] + jnp.broadcast_to(jnp.sum(dh * xh, axis=0, keepdims=True), gnw_ref.shape)
        g = dh * nw_ref[...]
        gx_ref[...] = r * (g - xh * jnp.mean(g * xh, axis=-1, keepdims=True)) + dx2_ref[...]

    tile = pl.BlockSpec((tT, D), lambda i: (i, 0))
    return pl.pallas_call(
        body, name="dh_norm", grid=(T // tT,),
        in_specs=[pl.BlockSpec((tT, w), lambda i: (i, 0)) for w in widths]
        + [_vmem(), tile, tile, pl.BlockSpec((1, D), lambda i: (0, 0)), _any()],
        out_specs=[tile, pl.BlockSpec((8, D), lambda i: (0, 0))],
        out_shape=[jax.ShapeDtypeStruct((T, D), F32), jax.ShapeDtypeStruct((8, D), F32)],
        compiler_params=_cp(("arbitrary",)),
    )(*pieces, wf, x, dx2, norm_w, after)


def _adamw_math(w, g, m, v):
    m = ADAM_B1 * m + (1.0 - ADAM_B1) * g
    v = ADAM_B2 * v + (1.0 - ADAM_B2) * (g * g)
    m_hat = m / (1.0 - ADAM_B1 ** ADAM_STEP)
    v_hat = v / (1.0 - ADAM_B2 ** ADAM_STEP)
    delta = -ADAM_LR * (m_hat / (jnp.sqrt(v_hat) + ADAM_EPS) + ADAM_WD * w)
    return delta, m, v


def _fetch_partials(s_ref, got_ref, buf, sems):
    x, y, _ = _place()
    cps = [pltpu.make_async_copy(s_ref.at[2 * x + y], buf.at[0], sems.at[0])]
    cps += [pltpu.make_async_copy(got_ref.at[j], buf.at[1 + j], sems.at[1 + j]) for j in range(3)]
    for cp in cps:
        cp.start()
    for cp in cps:
        cp.wait()


SMALL_AT = dict(norm_w=0, fnw=8, bias=16, bn=24, sinks=32, loss=40)
ROW_AT = (R_IN, R_A, R_B, R_O)


def _finish_small(ws, ms, vs, smalls):
    names = ["norm_w", "fnw", "bias", "bn", "sinks"]
    widths = [ws[n].shape[1] for n in names]

    def body(*refs):
        w_refs, m_refs, v_refs = refs[0:5], refs[5:10], refs[10:15]
        smalls_ref, loss_ref = refs[15], refs[16]
        outs, tot = refs[17:37], refs[37]
        acc = smalls_ref[0]
        for d in range(1, NDEV):
            acc = acc + smalls_ref[d]
        tot[...] = acc
        loss_ref[...] = tot[SMALL_AT["loss"]:SMALL_AT["loss"] + 1, 0:1]
        for p, (nm_, wd) in enumerate(zip(names, widths)):
            r = SMALL_AT[nm_]
            g = tot[r:r + 1, 0:wd]
            d, nm, nv = _adamw_math(w_refs[p][...], g, m_refs[p][...], v_refs[p][...])
            for o, val in zip(outs[4 * p:4 * p + 4], (g, d, nm, nv)):
                o[...] = val

    res = pl.pallas_call(
        body, name="finish_small",
        in_specs=[_vmem()] * 16, out_specs=[_vmem()] * 21,
        out_shape=[jax.ShapeDtypeStruct((1, 1), F32)]
        + [jax.ShapeDtypeStruct((1, wd), F32) for wd in widths for _ in range(4)],
        scratch_shapes=[pltpu.VMEM((SMALL_ROWS, D), F32)],
        compiler_params=_cp(),
    )(*[ws[n] for n in names], *[ms[n] for n in names], *[vs[n] for n in names], smalls)
    return res[0], {n: tuple(res[1 + 4 * p:5 + 4 * p]) for p, n in enumerate(names)}


def _finish(w_rows, m_rows, v_rows, gu_w, gu_m, gu_v, sums, got):
    shapes = [w.shape for w in w_rows]

    def body(*refs):
        wr_refs, mr_refs, vr_refs = refs[0:4], refs[4:8], refs[8:12]
        guw_ref, gum_ref, guv_ref = refs[12:15]
        s_refs, got_refs = refs[15:17], refs[17:19]
        row_outs = refs[19:35]
        gu_outs = refs[35:39]
        buf, gsh, sems = refs[39:]
        x, y, c = _place()
        me_slot = 4 * x + 2 * y + c
        unshift = lax.rem(SHARD_PAD - 2 * me_slot, SHARD_PAD)
        for hf in range(2):
            _fetch_partials(s_refs[hf], got_refs[hf], buf, sems)
            for p in range(4):
                n, off = shapes[p][0], ROW_AT[p]
                nf = SHARD_PAD if p == 0 else n
                for cc in range(DH // LANE):
                    src = slice(cc * LANE, (cc + 1) * LANE)
                    cols = slice(hf * DH + cc * LANE, hf * DH + (cc + 1) * LANE)
                    g = buf[0, off:off + nf, src].astype(F32)
                    for j in range(1, 4):
                        g = g + buf[j, off:off + nf, src].astype(F32)
                    if p == 0:
                        gsh[...] = pltpu.roll(g, unshift, 0)
                        g = gsh[0:n, :]
                    d, nm, nv = _adamw_math(wr_refs[p][:, cols], g, mr_refs[p][:, cols], vr_refs[p][:, cols])
                    for o, val in zip(row_outs[4 * p:4 * p + 4], (g, d, nm, nv)):
                        o[:, cols] = val
            if hf == 0:
                g = buf[0, R_GU:R_GU + RANK, 0:64].astype(F32)
                for j in range(1, 4):
                    g = g + buf[j, R_GU:R_GU + RANK, 0:64].astype(F32)
                d, nm, nv = _adamw_math(guw_ref[...], g, gum_ref[...], guv_ref[...])
                for o, val in zip(gu_outs, (g, d, nm, nv)):
                    o[...] = val

    res = pl.pallas_call(
        body, name="finish",
        in_specs=[_vmem()] * 15 + [_any()] * 4,
        out_specs=[_vmem()] * 20,
        out_shape=[jax.ShapeDtypeStruct(s, F32) for s in shapes for _ in range(4)]
        + [jax.ShapeDtypeStruct((RANK, 64), F32)] * 4,
        scratch_shapes=[pltpu.VMEM((4, ROWS, DH), sums[0].dtype), pltpu.VMEM((SHARD_PAD, LANE), F32),
                        pltpu.SemaphoreType.DMA((4,))],
        compiler_params=_cp(),
    )(*w_rows, *m_rows, *v_rows, gu_w, gu_m, gu_v, *sums, *got)
    return tuple(res[0:16]), tuple(res[16:20])


def _place():
    x, y, c = lax.axis_index("x"), lax.axis_index("y"), lax.axis_index("c")
    return x, y, c


def _peers(x, y, c):
    return [(x ^ dx, y ^ dy, c ^ dc) for dx in range(2) for dy in range(2) for dc in range(2) if dx + dy + dc]


def _late_gather_start(blk, after, name="late_gather"):
    land = pltpu.with_memory_space_constraint(lax.empty((NDEV,) + blk.shape, blk.dtype), pltpu.HBM)

    def body(b_ref, land_ref, after_ref, send_sems, recv_sems, b_thru, land_thru, token):
        x, y, c = _place()
        for k, to in enumerate(_peers(x, y, c)):
            pltpu.make_async_remote_copy(
                src_ref=b_ref, dst_ref=land_ref.at[4 * x + 2 * y + c], send_sem=send_sems.at[k],
                recv_sem=recv_sems.at[k], device_id=to, device_id_type=MESH).start()
        token[...] = jnp.zeros_like(token)

    return pl.pallas_call(
        body, name=name + "_start",
        out_shape=(pltpu.SemaphoreType.DMA((7,)), pltpu.SemaphoreType.DMA((7,)),
                   pltpu.HBM(blk.shape, blk.dtype), pltpu.HBM(land.shape, land.dtype),
                   jax.ShapeDtypeStruct((8, LANE), F32)),
        in_specs=(_hbm(), _hbm(), _any()), out_specs=(_sem(), _sem(), _hbm(), _hbm(), _vmem()),
        input_output_aliases={0: 2, 1: 3},
        compiler_params=pltpu.CompilerParams(has_side_effects=_EFFECT),
    )(pltpu.with_memory_space_constraint(blk, pltpu.HBM), land, after)


def _late_gather_wait(send_sems, recv_sems, b_thru, land_thru, after, after2, name="late_gather"):
    def body(b_ref, land_ref, send_sems, recv_sems, after_ref, after2_ref, b_out, got_ref):
        x, y, c = _place()
        copies = [pltpu.make_async_remote_copy(
            src_ref=b_ref, dst_ref=land_ref.at[4 * x + 2 * y + c], send_sem=send_sems.at[k],
            recv_sem=recv_sems.at[k], device_id=to, device_id_type=MESH)
            for k, to in enumerate(_peers(x, y, c))]
        for cp in copies:
            cp.wait_send()
        for cp in copies:
            cp.wait_recv()

    return pl.pallas_call(
        body, name=name + "_wait",
        out_shape=(pltpu.HBM(b_thru.shape, b_thru.dtype), pltpu.HBM(land_thru.shape, land_thru.dtype)),
        in_specs=(_hbm(), _hbm(), _sem(), _sem(), _any(), _any()), out_specs=(_hbm(), _hbm()),
        input_output_aliases={0: 0, 1: 1},
        compiler_params=pltpu.CompilerParams(has_side_effects=_EFFECT),
    )(b_thru, land_thru, send_sems, recv_sems, after, after2)


G_ROWS = SHARD_PAD + RANK


def _gather_blocks(w_in_t, gu_s, xs, norm_w, pos_col):
    rows, cols = G_ROWS, D
    T = xs.shape[0]
    tT = min(T, 256)
    inv_row, sign_row = _rope_rows()

    def body(wi_ref, gu_ref, xs_ref, nw_ref, pos_ref, inv_ref, sign_ref,
             out_ref, h_ref, cos_ref, sin_ref, x_ref, frame_ref, send_sems, recv_sems, local_sem):
        x, y, c = _place()
        me, sibling = (x, y, c), (x, y, 1 - c)
        chips = [(1 - x, y), (x, 1 - y), (1 - x, 1 - y)]
        shift = 2 * (4 * x + 2 * y + c)
        frame_ref[SHARD - SHARD % 8:, :] = jnp.zeros((SHARD_PAD - SHARD + SHARD % 8, D), F32)
        frame_ref[:SHARD, :] = wi_ref[...]
        for cc in range(D // LANE):
            cs = slice(cc * LANE, (cc + 1) * LANE)
            x_ref[0:SHARD_PAD, cs] = pltpu.roll(frame_ref[:, cs], shift, 0).astype(x_ref.dtype)
        x_ref[SHARD_PAD:G_ROWS, :] = jnp.zeros((RANK, D), x_ref.dtype)
        x_ref[SHARD_PAD:G_ROWS, 0:64] = gu_ref[...].astype(x_ref.dtype)

        def slot(px, py, pc):
            return out_ref.at[4 * px + 2 * py + pc]

        def copy(k, block, to, src=None):
            return pltpu.make_async_remote_copy(
                src_ref=slot(*block) if src is None else src, dst_ref=slot(*block),
                send_sem=send_sems.at[k], recv_sem=recv_sems.at[k], device_id=to, device_id_type=MESH)

        mine = pltpu.make_async_copy(x_ref, slot(*me), local_sem)
        mine.start()
        first = [copy(0, me, sibling, src=x_ref)]
        first += [copy(1 + j, me, (*chip, c), src=x_ref) for j, chip in enumerate(chips)]
        for cp in first:
            cp.start()

        @pl.loop(0, T // tT)
        def _(i):
            rows_i = pl.ds(pl.multiple_of(i * tT, tT), tT)
            _prologue_rows(rows_i, xs_ref, nw_ref, pos_ref, inv_ref, sign_ref, h_ref, cos_ref, sin_ref)

        passed = [copy(4 + j, (*chip, c), sibling) for j, chip in enumerate(chips)]
        for j, chip in enumerate(chips):
            copy(1 + j, (*chip, c), me).wait_recv()
            passed[j].start()
        copy(0, sibling, me).wait_recv()
        for j, chip in enumerate(chips):
            copy(4 + j, (*chip, 1 - c), me).wait_recv()
        for cp in first + passed:
            cp.wait_send()
        mine.wait()

    return pl.pallas_call(
        body, name="gather_weights",
        in_specs=[_vmem()] * 7, out_specs=[_any()] + [_vmem()] * 3,
        out_shape=[jax.ShapeDtypeStruct((NDEV, rows, cols), WIRE), jax.ShapeDtypeStruct((T, D), MXU),
                   jax.ShapeDtypeStruct((T, LANE), F32), jax.ShapeDtypeStruct((T, LANE), F32)],
        scratch_shapes=[pltpu.VMEM((rows, cols), WIRE), pltpu.VMEM((SHARD_PAD, D), F32),
                        pltpu.SemaphoreType.DMA((7,)), pltpu.SemaphoreType.DMA((7,)), pltpu.SemaphoreType.DMA],
        compiler_params=_cp(),
    )(w_in_t, gu_s, xs, norm_w, pos_col, inv_row, sign_row)


def _pair_reduce(packed):
    def body(p_ref, out_ref, got, own, send_sems, recv_sems, own_sems):
        x, y, c = _place()
        sends = [pltpu.make_async_remote_copy(
            src_ref=p_ref.at[2 * chip + (1 - c)], dst_ref=got.at[chip],
            send_sem=send_sems.at[chip], recv_sem=recv_sems.at[chip], device_id=(x, y, 1 - c), device_id_type=MESH)
            for chip in range(4)]
        loads = [pltpu.make_async_copy(p_ref.at[2 * chip + c], own.at[chip], own_sems.at[chip]) for chip in range(4)]
        for cp in sends + loads:
            cp.start()
        for chip in range(4):
            loads[chip].wait()
            sends[chip].wait_recv()
            out_ref[chip] = (own[chip].astype(F32) + got[chip].astype(F32)).astype(out_ref.dtype)
        for cp in sends:
            cp.wait_send()

    return pl.pallas_call(
        body, name="pair_reduce",
        in_specs=[_any()], out_specs=_vmem(),
        out_shape=jax.ShapeDtypeStruct((4,) + packed.shape[1:], packed.dtype),
        scratch_shapes=[pltpu.VMEM((4,) + packed.shape[1:], packed.dtype), pltpu.VMEM((4,) + packed.shape[1:], packed.dtype),
                        pltpu.SemaphoreType.DMA((4,)), pltpu.SemaphoreType.DMA((4,)), pltpu.SemaphoreType.DMA((4,))],
        compiler_params=_cp(),
    )(packed)


def _pad_cols(a, cols):
    return jnp.pad(a, ((0, 0), (0, cols - a.shape[1])))


def _pad_rows(a, rows):
    return jnp.pad(a, ((0, rows - a.shape[0]), (0, 0)))


FRAME = 928


def _join_frames(frames):
    head = frames[:, :FRAME].at[1:, :16].add(frames[:-1, FRAME:])
    return jnp.concatenate([head.reshape(NDEV * FRAME, D), frames[NDEV - 1, FRAME:]], axis=0)


def _build_wft(wt):
    q = wt[0:1024].reshape(8, 2, 2, 32, D).transpose(0, 2, 1, 3, 4).reshape(1024, D)
    k = wt[1024:1152].reshape(2, 2, 1, 32, D)
    kd = jnp.broadcast_to(k, (2, 2, 2, 32, D)).reshape(256, D)
    v = wt[1152:1280].reshape(2, 1, 64, D)
    vd = jnp.broadcast_to(v, (2, 2, 64, D)).reshape(256, D)
    ag, bq, bk = wt[1280:2304], wt[2304:2816], wt[2816:3328]
    bv, bg, bl = wt[3328:4352], wt[4352:5376], wt[5376:5392]
    ma, mb = wt[5392:6416], wt[6416:7440]
    return jnp.concatenate([q, kd, vd, _pad_rows(bl, C_GLA - C_BL), bv, bq, bk, ag, bg, ma, mb], axis=0)


def _unbuild_gwt(g):
    n = g.shape[1]
    q = g[C_Q:C_Q + 1024].reshape(8, 2, 2, 32, n).transpose(0, 2, 1, 3, 4).reshape(1024, n)
    k = g[C_KD:C_KD + 256].reshape(2, 2, 2, 32, n).sum(axis=2).reshape(128, n)
    v = g[C_VD:C_VD + 256].reshape(2, 2, 64, n).sum(axis=1).reshape(128, n)
    bv, bq, bk = g[C_BV:C_BV + 1024], g[C_BQ:C_BQ + 512], g[C_BK:C_BK + 512]
    ag, bg, ma, mb = (g[c:c + 1024] for c in (C_AG, C_BG, C_MA, C_MB))
    return jnp.concatenate([q, k, v, ag, bq, bk, bv, bg, g[C_BL:C_BL + RANK], ma, mb], axis=0)


def kernel(x, positions, norm_w, w_in, a_sinks, b_gate_up, b_gate_bias, b_out_norm_w, w_a_proj, w_b_proj, w_out, final_norm_w, loss_target, m_norm_w, m_w_in, m_a_sinks, m_b_gate_up, m_b_gate_bias, m_b_out_norm_w, m_w_a_proj, m_w_b_proj, m_w_out, m_final_norm_w, v_norm_w, v_w_in, v_a_sinks, v_b_gate_up, v_b_gate_bias, v_b_out_norm_w, v_w_a_proj, v_w_b_proj, v_w_out, v_final_norm_w):
    T = x.shape[1]
    xs, target = x[0], loss_target[0]
    fnw = final_norm_w.reshape(1, D)
    me = 4 * lax.axis_index("x") + 2 * lax.axis_index("y") + lax.axis_index("c")
    allw, h, cos, sin = _gather_blocks(w_in[0].T, b_gate_up[0], xs, norm_w, positions.reshape(T, 1))
    late_blk = jnp.concatenate([w_a_proj[0], w_b_proj[0], w_out[0]], axis=0).astype(WIRE)
    l_send, l_recv, l_blk, l_land, l_started = _late_gather_start(late_blk, cos)
    wf = _build_wft(_join_frames(allw[:, :SHARD_PAD]))
    gu = allw[:, SHARD_PAD:G_ROWS, :64].transpose(1, 0, 2).reshape(RANK, 512)
    gu_pad = _pad_rows(gu, W_BL)

    proj = _proj(h, wf, l_started)
    o_a, lse = _swa_fwd(proj, cos, sin, a_sinks)
    o_b, states = _gla_fwd(proj, gu_pad, b_gate_bias)
    l_blk, l_land = _late_gather_wait(l_send, l_recv, l_blk, l_land, states, lse)
    late = lax.dynamic_update_slice(l_land, l_blk[None], (me, 0, 0))
    w_a, w_b, w_o = (late[:, 128 * i:128 * (i + 1), :].reshape(D, D) for i in range(3))
    (dx2, do_a, do_b, d_gates, g_wa, g_wb, g_wo, g_fn, g_bn, loss_part) = _mid(
        xs, target, proj, o_a, o_b, w_a, w_b, w_o, jnp.tile(b_out_norm_w, (1, B_HEADS)), fnw)
    d_q, d_kv, g_sinks = _swa_bwd(proj, cos, sin, a_sinks, do_a, o_a, lse, cos)
    d_gla, d_bl, g_gu, g_bias = _gla_bwd(proj, gu_pad, b_gate_bias, states, do_b)
    pieces = [d_q, d_kv, d_bl, d_gla, d_gates]
    offsets = [C_Q, C_KD, C_BL, C_GLA, C_GATES]

    ggu = g_gu[:RANK].reshape(RANK, NDEV, 64).transpose(1, 0, 2)
    ggu_half = [jnp.pad(ggu, ((0, 0), (0, 0), (0, DH - 64))), jnp.zeros((NDEV, RANK, DH), F32)]

    def pack(gw_half, hf):
        gwt = _unbuild_gwt(gw_half).astype(WIRE)
        cols = slice(hf * DH, (hf + 1) * DH)
        return jnp.concatenate([
            jnp.stack([gwt[FRAME * d:FRAME * d + SHARD_PAD] for d in range(NDEV)]),
            g_wa[:, cols].reshape(NDEV, 128, DH).astype(WIRE),
            g_wb[:, cols].reshape(NDEV, 128, DH).astype(WIRE),
            g_wo[:, cols].reshape(NDEV, 128, DH).astype(WIRE),
            ggu_half[hf].astype(WIRE)], axis=1)

    send0, recv0, s_thru0, land0, started0 = _chip_start(_pair_reduce(pack(_gw_half(h, pieces, 0), 0)), 0)
    send1, recv1, s_thru1, land1, started1 = _chip_start(
        _pair_reduce(pack(_gw_half(h, pieces, 1, after=started0), 1)), 1)
    grad_x, g_nw = _dh_norm(pieces, offsets, wf, xs, dx2, norm_w, started1)
    small = jnp.concatenate([g_nw, g_fn, _pad_cols(g_bias, D), _pad_cols(g_bn, D), _pad_cols(g_sinks, D),
                             _pad_cols(loss_part, D)], axis=0)
    sm_send, sm_recv, sm_blk, sm_land, sm_started = _late_gather_start(small, g_nw, name="small_gather")
    sums0, got0 = _chip_wait(send0, recv0, s_thru0, land0, sm_started, 0)
    sums1, got1 = _chip_wait(send1, recv1, s_thru1, land1, got0, 1)
    sums, from_chips = [sums0, sums1], [got0, got1]

    ws = dict(norm_w=norm_w, fnw=fnw, bias=b_gate_bias, bn=b_out_norm_w, sinks=a_sinks)
    ms = dict(norm_w=m_norm_w, fnw=m_final_norm_w.reshape(1, D), bias=m_b_gate_bias, bn=m_b_out_norm_w,
              sinks=m_a_sinks)
    vs = dict(norm_w=v_norm_w, fnw=v_final_norm_w.reshape(1, D), bias=v_b_gate_bias, bn=v_b_out_norm_w,
              sinks=v_a_sinks)
    t_rows, t_gu = _finish(
        [w_in[0].T, w_a_proj[0], w_b_proj[0], w_out[0]], [m_w_in[0].T, m_w_a_proj[0], m_w_b_proj[0], m_w_out[0]],
        [v_w_in[0].T, v_w_a_proj[0], v_w_b_proj[0], v_w_out[0]],
        b_gate_up[0], m_b_gate_up[0], v_b_gate_up[0], sums, from_chips)
    sm_blk, sm_land = _late_gather_wait(sm_send, sm_recv, sm_blk, sm_land, t_rows[0], t_gu[0], name="small_gather")
    loss, sm = _finish_small(ws, ms, vs, lax.dynamic_update_slice(sm_land, sm_blk[None], (me, 0, 0)))

    def outputs(k):
        return [sm["norm_w"][k], t_rows[k].T[None], sm["sinks"][k], t_gu[k][None], sm["bias"][k], sm["bn"][k],
                t_rows[4 + k][None], t_rows[8 + k][None], t_rows[12 + k][None], sm["fnw"][k].reshape(D)]

    return (loss[0, 0], grad_x[None], *outputs(0), *outputs(1), *outputs(2), *outputs(3))
```

```python
import functools

import numpy as np
import jax
import jax.numpy as jnp
from jax import lax
from jax.experimental import pallas as pl
from jax.experimental.pallas import tpu as pltpu

F32 = jnp.float32
MXU = jnp.bfloat16
WIRE = jnp.bfloat16

D = 1024
A_HEADS, A_KV, A_HD = 16, 2, 64
BLK = 128
B_HEADS, B_DK, B_DV = 4, 128, 256
RANK, TAU, CHUNK = 16, 16.0, 64
EPS, NEG = 1e-5, -1e30
ROPE_THETA = 10000.0
IN_WIDTH, NDEV = 7440, 8
SHARD = IN_WIDTH // NDEV
LANE = 128

C_Q, C_KD, C_VD, C_BL = 0, 1024, 1280, 1536
C_BV, C_BQ, C_BK = 2048, 3072, 3584
C_AG, C_BG, C_MA, C_MB = 4096, 5120, 6144, 7168
C_GLA, W_GLA, C_GATES, W_GATES = 2048, 2048, 4096, 4096
NF = 8192
W_BL = 128

SHARD_PAD = 944
R_IN, R_A, R_B, R_O, R_GU, ROWS = 0, 944, 1072, 1200, 1328, 1344
SMALL_ROWS = 48

ADAM_LR, ADAM_B1, ADAM_B2, ADAM_EPS, ADAM_WD, ADAM_STEP = 0.001, 0.9, 0.999, 1e-08, 0.01, 10

MESH = pl.DeviceIdType.MESH
VMEM_LIMIT = 56 * 1024 * 1024


def _cp(sem=None, **kw):
    if sem is not None:
        kw["dimension_semantics"] = sem
    return pltpu.CompilerParams(vmem_limit_bytes=VMEM_LIMIT, **kw)


def _dot(a, b):
    return jnp.dot(a, b, preferred_element_type=F32)


def _dot_nt(a, b):
    return lax.dot_general(a, b, (((1,), (1,)), ((), ())), preferred_element_type=F32)


def _dot_tn(a, b):
    return lax.dot_general(a, b, (((0,), (0,)), ((), ())), preferred_element_type=F32)


def _dot_f32(a, b):
    return jnp.dot(a, b, preferred_element_type=F32, precision=lax.Precision.HIGHEST)


def _sigmoid(z):
    return 0.5 * jnp.tanh(0.5 * z) + 0.5


def _rope(xp, cos, sin):
    return xp * cos + pltpu.roll(xp, 64, 1) * sin


def _rope_bwd(dy, cos, sin):
    return dy * cos - pltpu.roll(dy, 64, 1) * sin


def _vmem():
    return pl.BlockSpec(memory_space=pltpu.VMEM)


def _any():
    return pl.BlockSpec(memory_space=pl.ANY)


def _rope_rows():
    half = A_HD // 2
    inv = (np.float32(ROPE_THETA) ** (-np.arange(half, dtype=np.float32) / np.float32(half))).astype(np.float32)
    inv_row = jnp.asarray(np.tile(inv, 4)[None, :])
    sign_row = jnp.asarray(np.concatenate([-np.ones(64, np.float32), np.ones(64, np.float32)])[None, :])
    return inv_row, sign_row


def _prologue_rows(rows, x_ref, nw_ref, pos_ref, inv_ref, sign_ref, h_ref, cos_ref, sin_ref):
    xv = x_ref[rows, :]
    r = lax.rsqrt(jnp.mean(xv * xv, axis=-1, keepdims=True) + EPS)
    h_ref[rows, :] = ((xv * r) * nw_ref[...]).astype(h_ref.dtype)
    ang = pos_ref[rows, :].astype(F32) * inv_ref[...]
    cos_ref[rows, :] = jnp.cos(ang)
    sin_ref[rows, :] = jnp.sin(ang) * sign_ref[...]


def _proj(h, wft, after):
    T = h.shape[0]
    tT, tN = T, 512

    def body(h_ref, w_ref, after_ref, o_ref):
        o_ref[...] = _dot_nt(h_ref[...], w_ref[...])

    return pl.pallas_call(
        body, name="proj", grid=(T // tT, NF // tN),
        in_specs=[pl.BlockSpec((tT, D), lambda i, j: (i, 0)), pl.BlockSpec((tN, D), lambda i, j: (j, 0)), _any()],
        out_specs=pl.BlockSpec((tT, tN), lambda i, j: (i, j)),
        out_shape=jax.ShapeDtypeStruct((T, NF), F32),
        compiler_params=_cp(("parallel", "parallel")),
    )(h, wft, after)


def _swa_masks():
    lane = lax.broadcasted_iota(jnp.int32, (BLK, LANE), 1)
    rope_sub0 = ((lane // 32) % 2) == 0
    std_sub0 = lane < 64
    return lane, rope_sub0, std_sub0


def _swa_tri():
    qi = lax.broadcasted_iota(jnp.int32, (BLK, BLK), 0)
    kj = lax.broadcasted_iota(jnp.int32, (BLK, BLK), 1)
    return kj <= qi


def _swa_fold(full, tri):
    return jnp.where(tri, full[:, BLK:], full[:, :BLK])


def _swa_unfold(sq, tri):
    return jnp.concatenate([jnp.where(tri, 0.0, sq), jnp.where(tri, sq, 0.0)], axis=1)


def _swa_keys(kc_ref, kp_ref, vc_ref, vp_ref, cq, sq, cp, sp):
    def ropek(kref, c, s):
        kv = kref[...]
        return jnp.concatenate([_rope(kv[:, :LANE], c, s), _rope(kv[:, LANE:], c, s)], axis=1)

    K = jnp.concatenate([ropek(kp_ref, cp, sp), ropek(kc_ref, cq, sq)], axis=0).astype(MXU)
    V = jnp.concatenate([vp_ref[...], vc_ref[...]], axis=0).astype(MXU)
    return K, V


def _swa_in_specs(nb, last):
    def cur(n):
        return jnp.minimum(n, last)

    def prev(n):
        return jnp.maximum(cur(n) - 1, 0)

    kd, vd = C_KD // 256, C_VD // 256
    return [
        pl.BlockSpec((BLK, D), lambda n: (cur(n), C_Q // D)),
        pl.BlockSpec((BLK, 256), lambda n: (cur(n), kd)),
        pl.BlockSpec((BLK, 256), lambda n: (prev(n), kd)),
        pl.BlockSpec((BLK, 256), lambda n: (cur(n), vd)),
        pl.BlockSpec((BLK, 256), lambda n: (prev(n), vd)),
        pl.BlockSpec((BLK, LANE), lambda n: (cur(n), 0)),
        pl.BlockSpec((BLK, LANE), lambda n: (cur(n), 0)),
        pl.BlockSpec((BLK, LANE), lambda n: (prev(n), 0)),
        pl.BlockSpec((BLK, LANE), lambda n: (prev(n), 0)),
    ]


def _swa_fwd(proj, cos, sin, sinks):
    T = proj.shape[0]
    nb = T // BLK
    scale = A_HD ** -0.5

    def body(sinks_ref, q_ref, kc_ref, kp_ref, vc_ref, vp_ref, cq_ref, sq_ref, cp_ref, sp_ref, o_ref, l_ref):
        n = pl.program_id(0)
        cq, sq = cq_ref[...], sq_ref[...]
        K, V = _swa_keys(kc_ref, kp_ref, vc_ref, vp_ref, cq, sq, cp_ref[...], sp_ref[...])
        tri = _swa_tri()
        valid = tri | (n > 0)
        lane, rope_sub0, std_sub0 = _swa_masks()
        group = A_HEADS // A_KV
        roped, lses = {}, []

        def products(head):
            pb, sub, g = head // 2, head % 2, head // group
            if sub == 0:
                roped[pb] = _rope(q_ref[:, pb * LANE:(pb + 1) * LANE], cq, sq)
            qm = jnp.where(rope_sub0 if sub == 0 else ~rope_sub0, roped[pb], 0.0).astype(MXU)
            return _dot_nt(qm, K[:, g * LANE:(g + 1) * LANE])

        def softmax(head, s_full):
            s = jnp.where(valid, _swa_fold(s_full, tri) * scale, NEG)
            sink = sinks_ref[0, head]
            m = jnp.maximum(jnp.max(s, axis=1, keepdims=True), sink)
            e = jnp.exp(s - m)
            den = jnp.sum(e, axis=1, keepdims=True) + jnp.exp(sink - m)
            lses.append(m + jnp.log(den))
            return _swa_unfold(e / den, tri).astype(MXU)

        outs = {}
        st1 = {0: products(0), 1: products(1)}
        st2 = {0: softmax(0, st1.pop(0))}
        for head in range(A_HEADS):
            if head + 2 < A_HEADS:
                st1[head + 2] = products(head + 2)
            if head + 1 < A_HEADS:
                st2[head + 1] = softmax(head + 1, st1.pop(head + 1))
            g = head // group
            outs[head] = _dot(st2.pop(head), V[:, g * LANE:(g + 1) * LANE])
            if head % 2 == 1:
                pb = head // 2
                o_ref[:, pb * LANE:(pb + 1) * LANE] = jnp.where(std_sub0, outs[head - 1], outs[head])
        lacc = jnp.zeros((BLK, LANE), F32)
        for head in range(A_HEADS):
            lacc = jnp.where(lane == head, lses[head], lacc)
        l_ref[...] = lacc

    return pl.pallas_call(
        body, name="swa_fwd", grid=(nb,),
        in_specs=[pl.BlockSpec(memory_space=pltpu.SMEM)] + _swa_in_specs(nb, nb - 1),
        out_specs=[pl.BlockSpec((BLK, D), lambda n: (n, 0)), pl.BlockSpec((BLK, LANE), lambda n: (n, 0))],
        out_shape=[jax.ShapeDtypeStruct((T, D), F32), jax.ShapeDtypeStruct((T, LANE), F32)],
        compiler_params=_cp(("parallel",)),
    )(sinks, proj, proj, proj, proj, proj, cos, sin, cos, sin)


def _swa_bwd(proj, cos, sin, sinks, do_a, o_a, lse, after):
    T = proj.shape[0]
    nb = T // BLK
    scale = A_HD ** -0.5

    def body(sinks_ref, q_ref, kc_ref, kp_ref, vc_ref, vp_ref, cq_ref, sq_ref, cp_ref, sp_ref,
             do_ref, o_ref, l_ref, after_ref, dq_ref, dkv_ref, ds_ref, ckv_ref):
        n = pl.program_id(0)

        @pl.when(n == 0)
        def _():
            ckv_ref[...] = jnp.zeros_like(ckv_ref)
            ds_ref[...] = jnp.zeros_like(ds_ref)

        @pl.when(n < nb)
        def _():
            cq, sq, cp, sp = cq_ref[...], sq_ref[...], cp_ref[...], sp_ref[...]
            K, V = _swa_keys(kc_ref, kp_ref, vc_ref, vp_ref, cq, sq, cp, sp)
            tri = _swa_tri()
            valid = tri | (n > 0)
            lane, rope_sub0, std_sub0 = _swa_masks()
            lane_row = lax.broadcasted_iota(jnp.int32, (1, LANE), 1)
            lse_v = l_ref[...]
            dKt = [jnp.zeros((LANE, 2 * BLK), F32) for _ in range(A_KV)]
            dVt = [jnp.zeros((LANE, 2 * BLK), F32) for _ in range(A_KV)]
            dsinks, roped, roped_t, do_t = [], {}, {}, {}
            group = A_HEADS // A_KV
            dim = lax.broadcasted_iota(jnp.int32, (LANE, BLK), 0)
            rope_row0, std_row0 = ((dim // 32) % 2) == 0, dim < 64

            def products(head):
                pb, sub, g = head // 2, head % 2, head // group
                cols = slice(pb * LANE, (pb + 1) * LANE)
                Kg, Vg = K[:, g * LANE:(g + 1) * LANE], V[:, g * LANE:(g + 1) * LANE]
                if sub == 0:
                    roped[pb] = _rope(q_ref[:, cols], cq, sq)
                    roped_t[pb] = roped[pb].T
                    do_t[pb] = do_ref[:, cols].T
                qm = jnp.where(rope_sub0 if sub == 0 else ~rope_sub0, roped[pb], 0.0).astype(MXU)
                qmt = jnp.where(rope_row0 if sub == 0 else ~rope_row0, roped_t[pb], 0.0).astype(MXU)
                dov = jnp.where(std_sub0 if sub == 0 else ~std_sub0, do_ref[:, cols], 0.0)
                dovt = jnp.where(std_row0 if sub == 0 else ~std_row0, do_t[pb], 0.0).astype(MXU)
                delta = jnp.sum(dov * o_ref[:, cols], axis=1, keepdims=True)
                return qmt, dovt, delta, _dot_nt(qm, Kg), _dot_nt(dov.astype(MXU), Vg)

            def scores(head, qmt, dovt, delta, s_full, dp_full):
                lh = jnp.sum(jnp.where(lane == head, lse_v, 0.0), axis=1, keepdims=True)
                p = jnp.where(valid, jnp.exp(_swa_fold(s_full, tri) * scale - lh), 0.0)
                psink = jnp.exp(sinks_ref[0, head] - lh)
                dsinks.append(jnp.sum(-psink * delta, axis=0, keepdims=True))
                dsq = (p * (_swa_fold(dp_full, tri) - delta)) * scale
                return qmt, dovt, _swa_unfold(p, tri).astype(MXU), _swa_unfold(dsq, tri).astype(MXU)

            def grads(head, qmt, dovt, pb16, dsc):
                g = head // group
                dKt[g] = dKt[g] + _dot(qmt, dsc)
                dVt[g] = dVt[g] + _dot(dovt, pb16)
                return _dot(dsc, K[:, g * LANE:(g + 1) * LANE])

            dqs = {}
            st1 = {0: products(0), 1: products(1)}
            st2 = {0: scores(0, *st1.pop(0))}
            for head in range(A_HEADS):
                if head + 2 < A_HEADS:
                    st1[head + 2] = products(head + 2)
                if head + 1 < A_HEADS:
                    st2[head + 1] = scores(head + 1, *st1.pop(head + 1))
                dqs[head] = grads(head, *st2.pop(head))
                if head % 2 == 1:
                    pb = head // 2
                    dqp = jnp.where(rope_sub0, dqs[head - 1], dqs[head])
                    dq_ref[:, pb * LANE:(pb + 1) * LANE] = _rope_bwd(dqp, cq, sq).astype(dq_ref.dtype)
            dsink = jnp.zeros((1, LANE), F32)
            for head in range(A_HEADS):
                dsink = jnp.where(lane_row == head, dsinks[head], dsink)
            dK, dV = [a.T for a in dKt], [a.T for a in dVt]
            prev = ([_rope_bwd(dK[g][:BLK], cp, sp) for g in range(A_KV)] + [dV[g][:BLK] for g in range(A_KV)])
            cur_ = ([_rope_bwd(dK[g][BLK:], cq, sq) for g in range(A_KV)] + [dV[g][BLK:] for g in range(A_KV)])
            dkv_ref[...] = (ckv_ref[...] + jnp.concatenate(prev, axis=1)).astype(dkv_ref.dtype)
            ckv_ref[...] = jnp.concatenate(cur_, axis=1)
            ds_ref[...] = ds_ref[...] + jnp.broadcast_to(dsink, ds_ref.shape)

        @pl.when(n == nb)
        def _():
            dkv_ref[...] = ckv_ref[...].astype(dkv_ref.dtype)

    last = nb - 1

    def cur(n):
        return jnp.minimum(n, last)

    def out_kv(n):
        return (jnp.maximum(n - 1, 0), 0)

    return pl.pallas_call(
        body, name="swa_bwd", grid=(nb + 1,),
        in_specs=[pl.BlockSpec(memory_space=pltpu.SMEM)] + _swa_in_specs(nb, last) + [
            pl.BlockSpec((BLK, D), lambda n: (cur(n), 0)),
            pl.BlockSpec((BLK, D), lambda n: (cur(n), 0)),
            pl.BlockSpec((BLK, LANE), lambda n: (cur(n), 0)),
            _any(),
        ],
        out_specs=[
            pl.BlockSpec((BLK, D), lambda n: (cur(n), 0)),
            pl.BlockSpec((BLK, 512), out_kv),
            pl.BlockSpec((8, LANE), lambda n: (0, 0)),
        ],
        out_shape=[
            jax.ShapeDtypeStruct((T, D), MXU),
            jax.ShapeDtypeStruct((T, 512), MXU),
            jax.ShapeDtypeStruct((8, LANE), F32),
        ],
        scratch_shapes=[pltpu.VMEM((BLK, 512), F32)],
        compiler_params=_cp(("arbitrary",)),
    )(sinks, proj, proj, proj, proj, proj, cos, sin, cos, sin, do_a, o_a, lse, after)


GSTEP = 2 * CHUNK
ST_ROWS = B_HEADS * B_DV


def _gla_gate(bl_ref, gu_ref, bias_ref):
    gk = _dot(bl_ref[...].astype(MXU), gu_ref[...]) + bias_ref[...]
    la = (jnp.minimum(gk, 0.0) - jnp.log(1.0 + jnp.exp(-jnp.abs(gk)))) / TAU
    ri = lax.broadcasted_iota(jnp.int32, (GSTEP, GSTEP), 0)
    ci = lax.broadcasted_iota(jnp.int32, (GSTEP, GSTEP), 1)
    same = (ri // CHUNK) == (ci // CHUNK)
    lower, upper = same & (ci <= ri), same & (ci >= ri)
    b = _dot_f32(jnp.where(lower, 1.0, 0.0).astype(F32), la)
    first = lax.broadcasted_iota(jnp.int32, (GSTEP, 1), 0) < CHUNK
    return gk, la, b, lower, upper, first


def _gla_head(q_ref, k_ref, la, b, first, h):
    sl = slice(h * B_DK, (h + 1) * B_DK)
    bh, lah = b[:, sl], la[:, sl]
    bl_a = jnp.sum(lah[:CHUNK], axis=0, keepdims=True)
    bl_b = jnp.sum(lah[CHUNK:], axis=0, keepdims=True)
    blast = jnp.where(first, bl_a, bl_b)
    qc = q_ref[:, sl] * (B_DK ** -0.5)
    kh = k_ref[:, sl]
    eb, enb, esb = jnp.exp(bh), jnp.exp(-bh), jnp.exp(blast - bh)
    return qc * eb, kh * enb, kh * esb, eb, enb, esb, (jnp.exp(bl_a), jnp.exp(bl_b))


def _gla_specs(step_of):
    return [
        pl.BlockSpec((GSTEP, 512), lambda i: (step_of(i), C_BQ // 512)),
        pl.BlockSpec((GSTEP, 512), lambda i: (step_of(i), C_BK // 512)),
        pl.BlockSpec((GSTEP, D), lambda i: (step_of(i), C_BV // D)),
        pl.BlockSpec((GSTEP, W_BL), lambda i: (step_of(i), C_BL // W_BL)),
        pl.BlockSpec((W_BL, 512), lambda i: (0, 0)),
        pl.BlockSpec((1, 512), lambda i: (0, 0)),
    ]


def _gla_fwd(proj, gu_pad, bias):
    T = proj.shape[0]
    ns = T // GSTEP

    def body(q_ref, k_ref, v_ref, bl_ref, gu_ref, bias_ref, o_ref, st_ref, state_ref):
        @pl.when(pl.program_id(0) == 0)
        def _():
            state_ref[...] = jnp.zeros_like(state_ref)

        _, la, b, lower, _, first = _gla_gate(bl_ref, gu_ref, bias_ref)
        st_ref[0:ST_ROWS, :] = state_ref[...]

        def within(h):
            q_e, k_e, k_s, _, _, _, decays = _gla_head(q_ref, k_ref, la, b, first, h)
            vh = v_ref[:, h * B_DV:(h + 1) * B_DV].astype(MXU)
            q_eb = q_e.astype(MXU)
            att = jnp.where(lower, _dot_nt(q_eb, k_e.astype(MXU)), 0.0)
            return vh, q_eb, k_s.astype(MXU), _dot(att.astype(MXU), vh), decays

        def across(h, vh, q_eb, k_sb, o_intra, decays):
            rows = slice(h * B_DV, (h + 1) * B_DV)
            s0 = state_ref[rows, :]
            o_a = o_intra[:CHUNK] + _dot_nt(q_eb[:CHUNK], s0.astype(MXU))
            s1 = s0 * decays[0] + _dot_tn(vh[:CHUNK], k_sb[:CHUNK])
            st_ref[ST_ROWS + h * B_DV:ST_ROWS + (h + 1) * B_DV, :] = s1
            o_b = o_intra[CHUNK:] + _dot_nt(q_eb[CHUNK:], s1.astype(MXU))
            state_ref[rows, :] = s1 * decays[1] + _dot_tn(vh[CHUNK:], k_sb[CHUNK:])
            o_ref[:, rows] = jnp.concatenate([o_a, o_b], axis=0)

        for h in range(B_HEADS):
            across(h, *within(h))

    return pl.pallas_call(
        body, name="gla_fwd", grid=(ns,),
        in_specs=_gla_specs(lambda i: i),
        out_specs=[pl.BlockSpec((GSTEP, D), lambda i: (i, 0)),
                   pl.BlockSpec((2 * ST_ROWS, B_DK), lambda i: (i, 0))],
        out_shape=[jax.ShapeDtypeStruct((T, D), F32),
                   jax.ShapeDtypeStruct((ns * 2 * ST_ROWS, B_DK), F32)],
        scratch_shapes=[pltpu.VMEM((ST_ROWS, B_DK), F32)],
        compiler_params=_cp(("arbitrary",)),
    )(proj, proj, proj, proj, gu_pad, bias)


def _gla_bwd(proj, gu_pad, bias, states, do_b):
    T = proj.shape[0]
    ns = T // GSTEP
    o_q, o_k = C_BQ - C_GLA, C_BK - C_GLA

    def body(q_ref, k_ref, v_ref, bl_ref, gu_ref, bias_ref, st_ref, do_ref,
             dg_ref, dbl_ref, ggu_ref, gbias_ref, gt_ref):
        @pl.when(pl.program_id(0) == 0)
        def _():
            gt_ref[...] = jnp.zeros_like(gt_ref)
            ggu_ref[...] = jnp.zeros_like(ggu_ref)
            gbias_ref[...] = jnp.zeros_like(gbias_ref)

        gk, la, b, lower, upper_mask, first = _gla_gate(bl_ref, gu_ref, bias_ref)
        upper = jnp.where(upper_mask, 1.0, 0.0).astype(F32)
        lo, hi = slice(0, CHUNK), slice(CHUNK, GSTEP)
        dla_parts = []

        def within(h):
            q_e, k_e, k_s, eb, enb, esb, decays = _gla_head(q_ref, k_ref, la, b, first, h)
            vh = v_ref[:, h * B_DV:(h + 1) * B_DV].astype(MXU)
            doh = do_ref[:, h * B_DV:(h + 1) * B_DV].astype(MXU)
            q_eb, k_eb = q_e.astype(MXU), k_e.astype(MXU)
            att = jnp.where(lower, _dot_nt(q_eb, k_eb), 0.0).astype(MXU)
            datt = jnp.where(lower, _dot_nt(doh, vh), 0.0).astype(MXU)
            return (q_e, k_e, k_s, eb, enb, esb, decays, vh, doh, q_eb, k_s.astype(MXU),
                    _dot(datt, k_eb), _dot_tn(datt, q_eb), _dot_tn(att, doh))

        def across(h, q_e, k_e, k_s, eb, enb, esb, decays, vh, doh, q_eb, k_sb, dq_i, dk_e, dv_i):
            dec_a, dec_b = decays
            rows = slice(h * B_DV, (h + 1) * B_DV)
            s0 = st_ref[rows, :]
            s1 = st_ref[ST_ROWS + h * B_DV:ST_ROWS + (h + 1) * B_DV, :]
            g2 = gt_ref[rows, :]
            g2b = g2.astype(MXU)
            dq_b = dq_i[hi] + _dot(doh[hi], s1.astype(MXU))
            dks_b = _dot(vh[hi], g2b)
            dv_b = dv_i[hi] + _dot_nt(k_sb[hi], g2b)
            ddec_b = jnp.sum(g2 * s1, axis=0, keepdims=True)
            g1 = g2 * dec_b + _dot_tn(doh[hi], q_eb[hi])
            g1b = g1.astype(MXU)
            dq_a = dq_i[lo] + _dot(doh[lo], s0.astype(MXU))
            dks_a = _dot(vh[lo], g1b)
            dv_a = dv_i[lo] + _dot_nt(k_sb[lo], g1b)
            ddec_a = jnp.sum(g1 * s0, axis=0, keepdims=True)
            gt_ref[rows, :] = g1 * dec_a + _dot_tn(doh[lo], q_eb[lo])
            dq_e = jnp.concatenate([dq_a, dq_b], axis=0)
            dk_s = jnp.concatenate([dks_a, dks_b], axis=0)
            dg_ref[:, rows] = jnp.concatenate([dv_a, dv_b], axis=0).astype(dg_ref.dtype)
            dg_ref[:, o_q + h * B_DK:o_q + (h + 1) * B_DK] = (dq_e * eb * (B_DK ** -0.5)).astype(dg_ref.dtype)
            dg_ref[:, o_k + h * B_DK:o_k + (h + 1) * B_DK] = (dk_e * enb + dk_s * esb).astype(dg_ref.dtype)
            dks_ks = dk_s * k_s
            db = dq_e * q_e - dk_e * k_e - dks_ks
            dbl_a = jnp.sum(dks_ks[lo], axis=0, keepdims=True) + ddec_a * dec_a
            dbl_b = jnp.sum(dks_ks[hi], axis=0, keepdims=True) + ddec_b * dec_b
            dla_parts.append(_dot_f32(upper, db) + jnp.where(first, dbl_a, dbl_b))

        for h in range(B_HEADS):
            across(h, *within(h))
        dla = jnp.concatenate(dla_parts, axis=1)
        dgk = dla * (1.0 / TAU) * _sigmoid(-gk)
        dgkb = dgk.astype(MXU)
        dbl_ref[...] = _dot_nt(dgkb, gu_ref[...]).astype(dbl_ref.dtype)
        ggu_ref[...] = ggu_ref[...] + _dot_tn(bl_ref[...].astype(MXU), dgkb)
        gbias_ref[...] = gbias_ref[...] + jnp.broadcast_to(jnp.sum(dgk, axis=0, keepdims=True), gbias_ref.shape)

    def rev(i):
        return ns - 1 - i

    return pl.pallas_call(
        body, name="gla_bwd", grid=(ns,),
        in_specs=_gla_specs(rev) + [
            pl.BlockSpec((2 * ST_ROWS, B_DK), lambda i: (rev(i), 0)),
            pl.BlockSpec((GSTEP, D), lambda i: (rev(i), 0)),
        ],
        out_specs=[
            pl.BlockSpec((GSTEP, W_GLA), lambda i: (rev(i), 0)),
            pl.BlockSpec((GSTEP, W_BL), lambda i: (rev(i), 0)),
            pl.BlockSpec((W_BL, 512), lambda i: (0, 0)),
            pl.BlockSpec((8, 512), lambda i: (0, 0)),
        ],
        out_shape=[
            jax.ShapeDtypeStruct((T, W_GLA), MXU),
            jax.ShapeDtypeStruct((T, W_BL), MXU),
            jax.ShapeDtypeStruct((W_BL, 512), F32),
            jax.ShapeDtypeStruct((8, 512), F32),
        ],
        scratch_shapes=[pltpu.VMEM((B_HEADS * B_DV, B_DK), F32)],
        compiler_params=_cp(("arbitrary",)),
    )(proj, proj, proj, proj, gu_pad, bias, states, do_b)


def _mid(x, target, proj, o_a, o_b, w_a, w_b, w_out, w_bn4, fnw):
    T = x.shape[0]
    tT = min(T, 128)
    nbuf = 4
    o_ag, o_bg, o_ma, o_mb = (c - C_GATES for c in (C_AG, C_BG, C_MA, C_MB))

    def body(x_ref, t_ref, oa_ref, ob_ref, gates_ref, wa_ref, wb_ref, wo_ref, wbn_ref, fnw_ref,
             dx2_ref, doa_ref, dob_ref, dgates_ref,
             gwa_ref, gwb_ref, gwo_ref, gfn_ref, gbn_ref, loss_ref, buf_ref):
        i = pl.program_id(0)

        @pl.when(i == 0)
        def _():
            for r in (gwa_ref, gwb_ref, gwo_ref, gfn_ref, gbn_ref, loss_ref):
                r[...] = jnp.zeros_like(r)

        rows = pl.ds(pl.multiple_of((i % nbuf) * tT, tT), tT)

        def keep(k, val):
            buf_ref[k, rows, :] = val

        oa, ag = oa_ref[...], gates_ref[:, o_ag:o_ag + D]
        sg_a = _sigmoid(ag)
        silu_a = ag * sg_a
        oag_b = (oa * silu_a).astype(MXU)
        keep(0, oag_b)
        y_a = _dot(oag_b, wa_ref[...])

        ob, bg = ob_ref[...], gates_ref[:, o_bg:o_bg + D]
        rbs, obhats = [], []
        for h in range(B_HEADS):
            obh = ob[:, h * B_DV:(h + 1) * B_DV]
            rb = lax.rsqrt(jnp.mean(obh * obh, axis=-1, keepdims=True) + EPS)
            rbs.append(rb)
            obhats.append(obh * rb)
        obhat = jnp.concatenate(obhats, axis=1)
        wbn = wbn_ref[...]
        obn = obhat * wbn
        sg_b = _sigmoid(bg)
        silu_b = bg * sg_b
        obg_b = (obn * silu_b).astype(MXU)
        keep(1, obg_b)
        y_b = _dot(obg_b, wb_ref[...])

        sa, sb = _sigmoid(gates_ref[:, o_ma:o_ma + D]), _sigmoid(gates_ref[:, o_mb:o_mb + D])
        mg_b = (sa * y_a + sb * y_b).astype(MXU)
        keep(2, mg_b)
        x2 = x_ref[...] + _dot(mg_b, wo_ref[...])
        r2 = lax.rsqrt(jnp.mean(x2 * x2, axis=-1, keepdims=True) + EPS)
        xh2 = x2 * r2
        fw = fnw_ref[...]
        err = xh2 * fw - t_ref[...]
        tok = jnp.mean(err * err, axis=-1, keepdims=True)
        loss_ref[...] = loss_ref[...] + 0.5 * jnp.sum(tok, axis=0, keepdims=True)

        dy = err * (1.0 / D)
        gfn_ref[...] = gfn_ref[...] + jnp.broadcast_to(jnp.sum(dy * xh2, axis=0, keepdims=True), gfn_ref.shape)
        gy = dy * fw
        dx2 = r2 * (gy - xh2 * jnp.mean(gy * xh2, axis=-1, keepdims=True))
        dx2_ref[...] = dx2
        dx2_b = dx2.astype(MXU)
        keep(5, dx2_b)
        dmg = _dot_nt(dx2_b, wo_ref[...])

        dgates_ref[:, o_ma:o_ma + D] = (dmg * y_a * sa * (1.0 - sa)).astype(dgates_ref.dtype)
        dgates_ref[:, o_mb:o_mb + D] = (dmg * y_b * sb * (1.0 - sb)).astype(dgates_ref.dtype)
        dya_b = (dmg * sa).astype(MXU)
        dyb_b = (dmg * sb).astype(MXU)
        keep(3, dya_b)
        keep(4, dyb_b)
        doag = _dot_nt(dya_b, wa_ref[...])
        dobg = _dot_nt(dyb_b, wb_ref[...])

        @pl.when(i % nbuf == nbuf - 1)
        def _():
            gwa_ref[...] = gwa_ref[...] + _dot_tn(buf_ref[0], buf_ref[3])
            gwb_ref[...] = gwb_ref[...] + _dot_tn(buf_ref[1], buf_ref[4])
            gwo_ref[...] = gwo_ref[...] + _dot_tn(buf_ref[2], buf_ref[5])

        doa_ref[...] = doag * silu_a
        dgates_ref[:, o_ag:o_ag + D] = (doag * oa * (sg_a * (1.0 + ag * (1.0 - sg_a)))).astype(dgates_ref.dtype)
        dobn = dobg * silu_b
        dgates_ref[:, o_bg:o_bg + D] = (dobg * obn * (sg_b * (1.0 + bg * (1.0 - sg_b)))).astype(dgates_ref.dtype)
        gg = dobn * wbn
        gbn = jnp.zeros((1, B_DV), F32)
        for h in range(B_HEADS):
            sl = slice(h * B_DV, (h + 1) * B_DV)
            gbn = gbn + jnp.sum(dobn[:, sl] * obhats[h], axis=0, keepdims=True)
            ggh = gg[:, sl]
            dob_ref[:, sl] = rbs[h] * (ggh - obhats[h] * jnp.mean(ggh * obhats[h], axis=-1, keepdims=True))
        gbn_ref[...] = gbn_ref[...] + jnp.broadcast_to(gbn, gbn_ref.shape)

    assert (T // tT) % nbuf == 0
    tile = pl.BlockSpec((tT, D), lambda i: (i, 0))
    row = pl.BlockSpec((1, D), lambda i: (0, 0))
    acc8 = pl.BlockSpec((8, D), lambda i: (0, 0))
    return pl.pallas_call(
        body, name="mid", grid=(T // tT,),
        in_specs=[tile, tile, tile, tile, pl.BlockSpec((tT, W_GATES), lambda i: (i, C_GATES // W_GATES)),
                  _vmem(), _vmem(), _vmem(), row, row],
        out_specs=[tile, tile, tile, pl.BlockSpec((tT, W_GATES), lambda i: (i, 0)), _vmem(), _vmem(), _vmem(),
                   acc8, pl.BlockSpec((8, B_DV), lambda i: (0, 0)), pl.BlockSpec((8, LANE), lambda i: (0, 0))],
        out_shape=[
            jax.ShapeDtypeStruct((T, D), F32),
            jax.ShapeDtypeStruct((T, D), F32),
            jax.ShapeDtypeStruct((T, D), F32),
            jax.ShapeDtypeStruct((T, W_GATES), MXU),
            jax.ShapeDtypeStruct((D, D), F32),
            jax.ShapeDtypeStruct((D, D), F32),
            jax.ShapeDtypeStruct((D, D), F32),
            jax.ShapeDtypeStruct((8, D), F32),
            jax.ShapeDtypeStruct((8, B_DV), F32),
            jax.ShapeDtypeStruct((8, LANE), F32),
        ],
        scratch_shapes=[pltpu.VMEM((6, nbuf * tT, D), MXU)],
        compiler_params=_cp(("arbitrary",)),
    )(x, target, o_a, o_b, proj, w_a, w_b, w_out, w_bn4, fnw)


DH = D // 2


_GW_TILES = (("q", 0, 512, 0), ("q", 1, 512, 512), ("kv", 0, 256, 1024), ("bl", 0, RANK, 5376),
             ("gla", 0, 512, 3328), ("gla", 1, 512, 3840), ("gla", 2, 512, 2304), ("gla", 3, 512, 2816),
             ("gates", 0, 512, 1280), ("gates", 1, 512, 1792), ("gates", 2, 512, 4352), ("gates", 3, 512, 4864),
             ("gates", 4, 512, 5392), ("gates", 5, 512, 5904), ("gates", 6, 512, 6416), ("gates", 7, 512, 6928))


def _gw_unpermute(piece, t):
    if piece == "q":
        parts = []
        for blk in range(t.shape[0] // LANE):
            g = [t[blk * LANE + 32 * i:blk * LANE + 32 * (i + 1)] for i in range(4)]
            parts += [g[0], g[2], g[1], g[3]]
        return jnp.concatenate(parts, axis=0)
    if piece == "kv":
        k = [t[64 * i:64 * i + 32] + t[64 * i + 32:64 * i + 64] for i in range(4)]
        v = [t[256 + 128 * g:256 + 128 * g + 64] + t[256 + 128 * g + 64:256 + 128 * (g + 1)] for g in range(2)]
        return jnp.concatenate(k + v, axis=0)
    if piece == "bl":
        return t[:RANK]
    return t


def _gw_half(h, pieces, half, after=None):
    T = h.shape[0]
    steps = len(_GW_TILES)

    def body(*refs):
        h_ref = refs[0]
        srcs = dict(zip(("q", "kv", "bl", "gla", "gates"), refs[1:6]))
        o_ref, stage, sems = refs[-3:]
        j = pl.program_id(0)

        def out_copy(k):
            _, _, n, off = _GW_TILES[k]
            return pltpu.make_async_copy(stage.at[k % 2, 0:n], o_ref.at[pl.ds(off, n)], sems.at[k % 2])

        for k, (piece, _, n, _) in enumerate(_GW_TILES):
            @pl.when(j == k)
            def _(k=k, piece=piece, n=n):
                if k >= 2:
                    out_copy(k - 2).wait()
                t = _gw_unpermute(piece, _dot_tn(srcs[piece][...], h_ref[...]))
                stage[k % 2, 0:n, :] = t.astype(stage.dtype)
                out_copy(k).start()

        @pl.when(j == steps - 1)
        def _():
            out_copy(steps - 2).wait()
            out_copy(steps - 1).wait()

    def tile_of(lo, hi):
        return lambda j: (0, jnp.clip(j - lo, 0, hi - lo - 1))

    in_specs = [pl.BlockSpec((T, DH), lambda j: (0, half)),
                pl.BlockSpec((T, 512), tile_of(0, 2)), pl.BlockSpec((T, 512), lambda j: (0, 0)),
                pl.BlockSpec((T, W_BL), lambda j: (0, 0)),
                pl.BlockSpec((T, 512), tile_of(4, 8)), pl.BlockSpec((T, 512), tile_of(8, 16))]
    args = [h, *pieces]
    if after is not None:
        in_specs.append(_any())
        args.append(after)
    return pl.pallas_call(
        body, name=f"gw_in_half{half}", grid=(steps,),
        in_specs=in_specs, out_specs=_any(),
        out_shape=jax.ShapeDtypeStruct((IN_WIDTH, DH), WIRE),
        scratch_shapes=[pltpu.VMEM((2, 512, DH), WIRE), pltpu.SemaphoreType.DMA((2,))],
        compiler_params=_cp(("arbitrary",)),
    )(*args)


def _chip_copies(s_ref, got_ref, send_sems, recv_sems):
    x, y, c = _place()
    chips = [(1 - x, y), (x, 1 - y), (1 - x, 1 - y)]
    return [pltpu.make_async_remote_copy(
        src_ref=s_ref.at[2 * px + py], dst_ref=got_ref.at[j],
        send_sem=send_sems.at[j], recv_sem=recv_sems.at[j], device_id=(px, py, c), device_id_type=MESH)
        for j, (px, py) in enumerate(chips)]


_EFFECT = pltpu.SideEffectType.DATAFLOW_SIDE_EFFECTING


def _hbm():
    return pl.BlockSpec(memory_space=pltpu.HBM)


def _sem():
    return pl.BlockSpec(memory_space=pltpu.SEMAPHORE)


def _chip_start(sums, half):
    land = pltpu.with_memory_space_constraint(lax.empty((3,) + sums.shape[1:], sums.dtype), pltpu.HBM)

    def body(s_ref, land_ref, send_sems, recv_sems, s_thru, land_thru, token):
        for cp in _chip_copies(s_ref, land_ref, send_sems, recv_sems):
            cp.start()
        token[...] = jnp.zeros_like(token)

    return pl.pallas_call(
        body, name=f"chip_start{half}",
        out_shape=(pltpu.SemaphoreType.DMA((3,)), pltpu.SemaphoreType.DMA((3,)),
                   pltpu.HBM(sums.shape, sums.dtype), pltpu.HBM(land.shape, land.dtype),
                   jax.ShapeDtypeStruct((8, LANE), F32)),
        in_specs=(_hbm(), _hbm()), out_specs=(_sem(), _sem(), _hbm(), _hbm(), _vmem()),
        input_output_aliases={0: 2, 1: 3},
        compiler_params=pltpu.CompilerParams(has_side_effects=_EFFECT),
    )(pltpu.with_memory_space_constraint(sums, pltpu.HBM), land)


def _chip_wait(send_sems, recv_sems, s_thru, land_thru, after, half):
    def body(s_ref, land_ref, send_sems, recv_sems, after_ref, s_out, got_ref):
        copies = _chip_copies(s_ref, land_ref, send_sems, recv_sems)
        for cp in copies:
            cp.wait_send()
        for cp in copies:
            cp.wait_recv()

    return pl.pallas_call(
        body, name=f"chip_wait{half}",
        out_shape=(pltpu.HBM(s_thru.shape, s_thru.dtype), pltpu.HBM(land_thru.shape, land_thru.dtype)),
        in_specs=(_hbm(), _hbm(), _sem(), _sem(), _any()), out_specs=(_hbm(), _hbm()),
        input_output_aliases={0: 0, 1: 1},
        compiler_params=pltpu.CompilerParams(has_side_effects=_EFFECT),
    )(s_thru, land_thru, send_sems, recv_sems, after)


def _dh_norm(pieces, offsets, wf, x, dx2, norm_w, after):
    T = x.shape[0]
    tT = min(T, 256)
    widths = [p.shape[1] for p in pieces]
    npc = len(pieces)

    def body(*refs):
        dp_refs = refs[:npc]
        wf_ref, x_ref, dx2_ref, nw_ref, _, gx_ref, gnw_ref = refs[npc:]

        @pl.when(pl.program_id(0) == 0)
        def _():
            gnw_ref[...] = jnp.zeros_like(gnw_ref)

        dh = jnp.zeros((tT, D), F32)
        for dp_ref, off, w in zip(dp_refs, offsets, widths):
            dh = dh + _dot(dp_ref[...], wf_ref[off:off + w, :])
        xv = x_ref[...]
        r = lax.rsqrt(jnp.mean(xv * xv, axis=-1, keepdims=True) + EPS)
        xh = xv * r
        gnw_ref[...] = gnw_ref[...] + jnp.broadcast_to(jnp.sum(dh * xh, axis=0, keepdims=True), gnw_ref.shape)
        g = dh * nw_ref[...]
        gx_ref[...] = r * (g - xh * jnp.mean(g * xh, axis=-1, keepdims=True)) + dx2_ref[...]

    tile = pl.BlockSpec((tT, D), lambda i: (i, 0))
    return pl.pallas_call(
        body, name="dh_norm", grid=(T // tT,),
        in_specs=[pl.BlockSpec((tT, w), lambda i: (i, 0)) for w in widths]
        + [_vmem(), tile, tile, pl.BlockSpec((1, D), lambda i: (0, 0)), _any()],
        out_specs=[tile, pl.BlockSpec((8, D), lambda i: (0, 0))],
        out_shape=[jax.ShapeDtypeStruct((T, D), F32), jax.ShapeDtypeStruct((8, D), F32)],
        compiler_params=_cp(("arbitrary",)),
    )(*pieces, wf, x, dx2, norm_w, after)


def _adamw_math(w, g, m, v):
    m = ADAM_B1 * m + (1.0 - ADAM_B1) * g
    v = ADAM_B2 * v + (1.0 - ADAM_B2) * (g * g)
    m_hat = m / (1.0 - ADAM_B1 ** ADAM_STEP)
    v_hat = v / (1.0 - ADAM_B2 ** ADAM_STEP)
    delta = -ADAM_LR * (m_hat / (jnp.sqrt(v_hat) + ADAM_EPS) + ADAM_WD * w)
    return delta, m, v


def _fetch_partials(s_ref, got_ref, buf, sems):
    x, y, _ = _place()
    cps = [pltpu.make_async_copy(s_ref.at[2 * x + y], buf.at[0], sems.at[0])]
    cps += [pltpu.make_async_copy(got_ref.at[j], buf.at[1 + j], sems.at[1 + j]) for j in range(3)]
    for cp in cps:
        cp.start()
    for cp in cps:
        cp.wait()


SMALL_AT = dict(norm_w=0, fnw=8, bias=16, bn=24, sinks=32, loss=40)
ROW_AT = (R_IN, R_A, R_B, R_O)


def _finish_small(ws, ms, vs, smalls):
    names = ["norm_w", "fnw", "bias", "bn", "sinks"]
    widths = [ws[n].shape[1] for n in names]

    def body(*refs):
        w_refs, m_refs, v_refs = refs[0:5], refs[5:10], refs[10:15]
        smalls_ref, loss_ref = refs[15], refs[16]
        outs, tot = refs[17:37], refs[37]
        acc = smalls_ref[0]
        for d in range(1, NDEV):
            acc = acc + smalls_ref[d]
        tot[...] = acc
        loss_ref[...] = tot[SMALL_AT["loss"]:SMALL_AT["loss"] + 1, 0:1]
        for p, (nm_, wd) in enumerate(zip(names, widths)):
            r = SMALL_AT[nm_]
            g = tot[r:r + 1, 0:wd]
            d, nm, nv = _adamw_math(w_refs[p][...], g, m_refs[p][...], v_refs[p][...])
            for o, val in zip(outs[4 * p:4 * p + 4], (g, d, nm, nv)):
                o[...] = val

    res = pl.pallas_call(
        body, name="finish_small",
        in_specs=[_vmem()] * 16, out_specs=[_vmem()] * 21,
        out_shape=[jax.ShapeDtypeStruct((1, 1), F32)]
        + [jax.ShapeDtypeStruct((1, wd), F32) for wd in widths for _ in range(4)],
        scratch_shapes=[pltpu.VMEM((SMALL_ROWS, D), F32)],
        compiler_params=_cp(),
    )(*[ws[n] for n in names], *[ms[n] for n in names], *[vs[n] for n in names], smalls)
    return res[0], {n: tuple(res[1 + 4 * p:5 + 4 * p]) for p, n in enumerate(names)}


def _finish(w_rows, m_rows, v_rows, gu_w, gu_m, gu_v, sums, got):
    shapes = [w.shape for w in w_rows]

    def body(*refs):
        wr_refs, mr_refs, vr_refs = refs[0:4], refs[4:8], refs[8:12]
        guw_ref, gum_ref, guv_ref = refs[12:15]
        s_refs, got_refs = refs[15:17], refs[17:19]
        row_outs = refs[19:35]
        gu_outs = refs[35:39]
        buf, gsh, sems = refs[39:]
        x, y, c = _place()
        me_slot = 4 * x + 2 * y + c
        unshift = lax.rem(SHARD_PAD - 2 * me_slot, SHARD_PAD)
        for hf in range(2):
            _fetch_partials(s_refs[hf], got_refs[hf], buf, sems)
            for p in range(4):
                n, off = shapes[p][0], ROW_AT[p]
                nf = SHARD_PAD if p == 0 else n
                for cc in range(DH // LANE):
                    src = slice(cc * LANE, (cc + 1) * LANE)
                    cols = slice(hf * DH + cc * LANE, hf * DH + (cc + 1) * LANE)
                    g = buf[0, off:off + nf, src].astype(F32)
                    for j in range(1, 4):
                        g = g + buf[j, off:off + nf, src].astype(F32)
                    if p == 0:
                        gsh[...] = pltpu.roll(g, unshift, 0)
                        g = gsh[0:n, :]
                    d, nm, nv = _adamw_math(wr_refs[p][:, cols], g, mr_refs[p][:, cols], vr_refs[p][:, cols])
                    for o, val in zip(row_outs[4 * p:4 * p + 4], (g, d, nm, nv)):
                        o[:, cols] = val
            if hf == 0:
                g = buf[0, R_GU:R_GU + RANK, 0:64].astype(F32)
                for j in range(1, 4):
                    g = g + buf[j, R_GU:R_GU + RANK, 0:64].astype(F32)
                d, nm, nv = _adamw_math(guw_ref[...], g, gum_ref[...], guv_ref[...])
                for o, val in zip(gu_outs, (g, d, nm, nv)):
                    o[...] = val

    res = pl.pallas_call(
        body, name="finish",
        in_specs=[_vmem()] * 15 + [_any()] * 4,
        out_specs=[_vmem()] * 20,
        out_shape=[jax.ShapeDtypeStruct(s, F32) for s in shapes for _ in range(4)]
        + [jax.ShapeDtypeStruct((RANK, 64), F32)] * 4,
        scratch_shapes=[pltpu.VMEM((4, ROWS, DH), sums[0].dtype), pltpu.VMEM((SHARD_PAD, LANE), F32),
                        pltpu.SemaphoreType.DMA((4,))],
        compiler_params=_cp(),
    )(*w_rows, *m_rows, *v_rows, gu_w, gu_m, gu_v, *sums, *got)
    return tuple(res[0:16]), tuple(res[16:20])


def _place():
    x, y, c = lax.axis_index("x"), lax.axis_index("y"), lax.axis_index("c")
    return x, y, c


def _peers(x, y, c):
    return [(x ^ dx, y ^ dy, c ^ dc) for dx in range(2) for dy in range(2) for dc in range(2) if dx + dy + dc]


def _late_gather_start(blk, after, name="late_gather"):
    land = pltpu.with_memory_space_constraint(lax.empty((NDEV,) + blk.shape, blk.dtype), pltpu.HBM)

    def body(b_ref, land_ref, after_ref, send_sems, recv_sems, b_thru, land_thru, token):
        x, y, c = _place()
        for k, to in enumerate(_peers(x, y, c)):
            pltpu.make_async_remote_copy(
                src_ref=b_ref, dst_ref=land_ref.at[4 * x + 2 * y + c], send_sem=send_sems.at[k],
                recv_sem=recv_sems.at[k], device_id=to, device_id_type=MESH).start()
        token[...] = jnp.zeros_like(token)

    return pl.pallas_call(
        body, name=name + "_start",
        out_shape=(pltpu.SemaphoreType.DMA((7,)), pltpu.SemaphoreType.DMA((7,)),
                   pltpu.HBM(blk.shape, blk.dtype), pltpu.HBM(land.shape, land.dtype),
                   jax.ShapeDtypeStruct((8, LANE), F32)),
        in_specs=(_hbm(), _hbm(), _any()), out_specs=(_sem(), _sem(), _hbm(), _hbm(), _vmem()),
        input_output_aliases={0: 2, 1: 3},
        compiler_params=pltpu.CompilerParams(has_side_effects=_EFFECT),
    )(pltpu.with_memory_space_constraint(blk, pltpu.HBM), land, after)


def _late_gather_wait(send_sems, recv_sems, b_thru, land_thru, after, after2, name="late_gather"):
    def body(b_ref, land_ref, send_sems, recv_sems, after_ref, after2_ref, b_out, got_ref):
        x, y, c = _place()
        copies = [pltpu.make_async_remote_copy(
            src_ref=b_ref, dst_ref=land_ref.at[4 * x + 2 * y + c], send_sem=send_sems.at[k],
            recv_sem=recv_sems.at[k], device_id=to, device_id_type=MESH)
            for k, to in enumerate(_peers(x, y, c))]
        for cp in copies:
            cp.wait_send()
        for cp in copies:
            cp.wait_recv()

    return pl.pallas_call(
        body, name=name + "_wait",
        out_shape=(pltpu.HBM(b_thru.shape, b_thru.dtype), pltpu.HBM(land_thru.shape, land_thru.dtype)),
        in_specs=(_hbm(), _hbm(), _sem(), _sem(), _any(), _any()), out_specs=(_hbm(), _hbm()),
        input_output_aliases={0: 0, 1: 1},
        compiler_params=pltpu.CompilerParams(has_side_effects=_EFFECT),
    )(b_thru, land_thru, send_sems, recv_sems, after, after2)


G_ROWS = SHARD_PAD + RANK


def _gather_blocks(w_in_t, gu_s, xs, norm_w, pos_col):
    rows, cols = G_ROWS, D
    T = xs.shape[0]
    tT = min(T, 256)
    inv_row, sign_row = _rope_rows()

    def body(wi_ref, gu_ref, xs_ref, nw_ref, pos_ref, inv_ref, sign_ref,
             out_ref, h_ref, cos_ref, sin_ref, x_ref, frame_ref, send_sems, recv_sems, local_sem):
        x, y, c = _place()
        me, sibling = (x, y, c), (x, y, 1 - c)
        chips = [(1 - x, y), (x, 1 - y), (1 - x, 1 - y)]
        shift = 2 * (4 * x + 2 * y + c)
        frame_ref[SHARD - SHARD % 8:, :] = jnp.zeros((SHARD_PAD - SHARD + SHARD % 8, D), F32)
        frame_ref[:SHARD, :] = wi_ref[...]
        for cc in range(D // LANE):
            cs = slice(cc * LANE, (cc + 1) * LANE)
            x_ref[0:SHARD_PAD, cs] = pltpu.roll(frame_ref[:, cs], shift, 0).astype(x_ref.dtype)
        x_ref[SHARD_PAD:G_ROWS, :] = jnp.zeros((RANK, D), x_ref.dtype)
        x_ref[SHARD_PAD:G_ROWS, 0:64] = gu_ref[...].astype(x_ref.dtype)

        def slot(px, py, pc):
            return out_ref.at[4 * px + 2 * py + pc]

        def copy(k, block, to, src=None):
            return pltpu.make_async_remote_copy(
                src_ref=slot(*block) if src is None else src, dst_ref=slot(*block),
                send_sem=send_sems.at[k], recv_sem=recv_sems.at[k], device_id=to, device_id_type=MESH)

        mine = pltpu.make_async_copy(x_ref, slot(*me), local_sem)
        mine.start()
        first = [copy(0, me, sibling, src=x_ref)]
        first += [copy(1 + j, me, (*chip, c), src=x_ref) for j, chip in enumerate(chips)]
        for cp in first:
            cp.start()

        @pl.loop(0, T // tT)
        def _(i):
            rows_i = pl.ds(pl.multiple_of(i * tT, tT), tT)
            _prologue_rows(rows_i, xs_ref, nw_ref, pos_ref, inv_ref, sign_ref, h_ref, cos_ref, sin_ref)

        passed = [copy(4 + j, (*chip, c), sibling) for j, chip in enumerate(chips)]
        for j, chip in enumerate(chips):
            copy(1 + j, (*chip, c), me).wait_recv()
            passed[j].start()
        copy(0, sibling, me).wait_recv()
        for j, chip in enumerate(chips):
            copy(4 + j, (*chip, 1 - c), me).wait_recv()
        for cp in first + passed:
            cp.wait_send()
        mine.wait()

    return pl.pallas_call(
        body, name="gather_weights",
        in_specs=[_vmem()] * 7, out_specs=[_any()] + [_vmem()] * 3,
        out_shape=[jax.ShapeDtypeStruct((NDEV, rows, cols), WIRE), jax.ShapeDtypeStruct((T, D), MXU),
                   jax.ShapeDtypeStruct((T, LANE), F32), jax.ShapeDtypeStruct((T, LANE), F32)],
        scratch_shapes=[pltpu.VMEM((rows, cols), WIRE), pltpu.VMEM((SHARD_PAD, D), F32),
                        pltpu.SemaphoreType.DMA((7,)), pltpu.SemaphoreType.DMA((7,)), pltpu.SemaphoreType.DMA],
        compiler_params=_cp(),
    )(w_in_t, gu_s, xs, norm_w, pos_col, inv_row, sign_row)


def _pair_reduce(gwt, tail):
    n = gwt.shape[1]
    blk = (4, ROWS, n)

    def body(g_ref, t_ref, out_ref, got, own, send_sems, recv_sems, own_sems):
        x, y, c = _place()

        def parts(d, dst):
            frame = g_ref.at[pl.ds(pl.multiple_of(FRAME * d, 16), SHARD_PAD)]
            return [(frame, dst.at[0:SHARD_PAD]), (t_ref.at[d], dst.at[SHARD_PAD:ROWS])]

        sends, loads = [], []
        for chip in range(4):
            sends.append([pltpu.make_async_remote_copy(
                src_ref=s, dst_ref=d_, send_sem=send_sems.at[chip, k], recv_sem=recv_sems.at[chip, k],
                device_id=(x, y, 1 - c), device_id_type=MESH)
                for k, (s, d_) in enumerate(parts(2 * chip + (1 - c), got.at[chip]))])
            loads.append([pltpu.make_async_copy(s, d_, own_sems.at[chip, k])
                          for k, (s, d_) in enumerate(parts(2 * chip + c, own.at[chip]))])
        for group in sends + loads:
            for cp in group:
                cp.start()
        for chip in range(4):
            for cp in loads[chip]:
                cp.wait()
            for cp in sends[chip]:
                cp.wait_recv()
            out_ref[chip] = (own[chip].astype(F32) + got[chip].astype(F32)).astype(out_ref.dtype)
        for group in sends:
            for cp in group:
                cp.wait_send()

    return pl.pallas_call(
        body, name="pair_reduce",
        in_specs=[_any(), _any()], out_specs=_vmem(),
        out_shape=jax.ShapeDtypeStruct(blk, gwt.dtype),
        scratch_shapes=[pltpu.VMEM(blk, gwt.dtype), pltpu.VMEM(blk, gwt.dtype),
                        pltpu.SemaphoreType.DMA((4, 2)), pltpu.SemaphoreType.DMA((4, 2)), pltpu.SemaphoreType.DMA((4, 2))],
        compiler_params=_cp(),
    )(gwt, tail)


def _pad_cols(a, cols):
    return jnp.pad(a, ((0, 0), (0, cols - a.shape[1])))


def _pad_rows(a, rows):
    return jnp.pad(a, ((0, rows - a.shape[0]), (0, 0)))


FRAME = 928


def _join_frames(frames):
    head = frames[:, :FRAME].at[1:, :16].add(frames[:-1, FRAME:])
    return jnp.concatenate([head.reshape(NDEV * FRAME, D), frames[NDEV - 1, FRAME:]], axis=0)


def _build_wft(wt):
    q = wt[0:1024].reshape(8, 2, 2, 32, D).transpose(0, 2, 1, 3, 4).reshape(1024, D)
    k = wt[1024:1152].reshape(2, 2, 1, 32, D)
    kd = jnp.broadcast_to(k, (2, 2, 2, 32, D)).reshape(256, D)
    v = wt[1152:1280].reshape(2, 1, 64, D)
    vd = jnp.broadcast_to(v, (2, 2, 64, D)).reshape(256, D)
    ag, bq, bk = wt[1280:2304], wt[2304:2816], wt[2816:3328]
    bv, bg, bl = wt[3328:4352], wt[4352:5376], wt[5376:5392]
    ma, mb = wt[5392:6416], wt[6416:7440]
    return jnp.concatenate([q, kd, vd, _pad_rows(bl, C_GLA - C_BL), bv, bq, bk, ag, bg, ma, mb], axis=0)


def kernel(x, positions, norm_w, w_in, a_sinks, b_gate_up, b_gate_bias, b_out_norm_w, w_a_proj, w_b_proj, w_out, final_norm_w, loss_target, m_norm_w, m_w_in, m_a_sinks, m_b_gate_up, m_b_gate_bias, m_b_out_norm_w, m_w_a_proj, m_w_b_proj, m_w_out, m_final_norm_w, v_norm_w, v_w_in, v_a_sinks, v_b_gate_up, v_b_gate_bias, v_b_out_norm_w, v_w_a_proj, v_w_b_proj, v_w_out, v_final_norm_w):
    T = x.shape[1]
    xs, target = x[0], loss_target[0]
    fnw = final_norm_w.reshape(1, D)
    me = 4 * lax.axis_index("x") + 2 * lax.axis_index("y") + lax.axis_index("c")
    allw, h, cos, sin = _gather_blocks(w_in[0].T, b_gate_up[0], xs, norm_w, positions.reshape(T, 1))
    late_blk = jnp.concatenate([w_a_proj[0], w_b_proj[0], w_out[0]], axis=0).astype(WIRE)
    l_send, l_recv, l_blk, l_land, l_started = _late_gather_start(late_blk, cos)
    wf = _build_wft(_join_frames(allw[:, :SHARD_PAD]))
    gu = allw[:, SHARD_PAD:G_ROWS, :64].transpose(1, 0, 2).reshape(RANK, 512)
    gu_pad = _pad_rows(gu, W_BL)

    proj = _proj(h, wf, l_started)
    o_a, lse = _swa_fwd(proj, cos, sin, a_sinks)
    o_b, states = _gla_fwd(proj, gu_pad, b_gate_bias)
    l_blk, l_land = _late_gather_wait(l_send, l_recv, l_blk, l_land, states, lse)
    late = lax.dynamic_update_slice(l_land, l_blk[None], (me, 0, 0))
    w_a, w_b, w_o = (late[:, 128 * i:128 * (i + 1), :].reshape(D, D) for i in range(3))
    (dx2, do_a, do_b, d_gates, g_wa, g_wb, g_wo, g_fn, g_bn, loss_part) = _mid(
        xs, target, proj, o_a, o_b, w_a, w_b, w_o, jnp.tile(b_out_norm_w, (1, B_HEADS)), fnw)
    d_q, d_kv, g_sinks = _swa_bwd(proj, cos, sin, a_sinks, do_a, o_a, lse, cos)
    d_gla, d_bl, g_gu, g_bias = _gla_bwd(proj, gu_pad, b_gate_bias, states, do_b)
    pieces = [d_q, d_kv, d_bl, d_gla, d_gates]
    offsets = [C_Q, C_KD, C_BL, C_GLA, C_GATES]

    ggu = g_gu[:RANK].reshape(RANK, NDEV, 64).transpose(1, 0, 2)
    ggu_half = [jnp.pad(ggu, ((0, 0), (0, 0), (0, DH - 64))), jnp.zeros((NDEV, RANK, DH), F32)]

    def tail(hf):
        cols = slice(hf * DH, (hf + 1) * DH)
        return jnp.concatenate([g[:, cols].reshape(NDEV, 128, DH) for g in (g_wa, g_wb, g_wo)]
                               + [ggu_half[hf]], axis=1).astype(WIRE)

    send0, recv0, s_thru0, land0, started0 = _chip_start(_pair_reduce(_gw_half(h, pieces, 0), tail(0)), 0)
    send1, recv1, s_thru1, land1, started1 = _chip_start(
        _pair_reduce(_gw_half(h, pieces, 1, after=started0), tail(1)), 1)
    grad_x, g_nw = _dh_norm(pieces, offsets, wf, xs, dx2, norm_w, started1)
    small = jnp.concatenate([g_nw, g_fn, _pad_cols(g_bias, D), _pad_cols(g_bn, D), _pad_cols(g_sinks, D),
                             _pad_cols(loss_part, D)], axis=0)
    sm_send, sm_recv, sm_blk, sm_land, sm_started = _late_gather_start(small, g_nw, name="small_gather")
    sums0, got0 = _chip_wait(send0, recv0, s_thru0, land0, sm_started, 0)
    sums1, got1 = _chip_wait(send1, recv1, s_thru1, land1, got0, 1)
    sums, from_chips = [sums0, sums1], [got0, got1]

    ws = dict(norm_w=norm_w, fnw=fnw, bias=b_gate_bias, bn=b_out_norm_w, sinks=a_sinks)
    ms = dict(norm_w=m_norm_w, fnw=m_final_norm_w.reshape(1, D), bias=m_b_gate_bias, bn=m_b_out_norm_w,
              sinks=m_a_sinks)
    vs = dict(norm_w=v_norm_w, fnw=v_final_norm_w.reshape(1, D), bias=v_b_gate_bias, bn=v_b_out_norm_w,
              sinks=v_a_sinks)
    t_rows, t_gu = _finish(
        [w_in[0].T, w_a_proj[0], w_b_proj[0], w_out[0]], [m_w_in[0].T, m_w_a_proj[0], m_w_b_proj[0], m_w_out[0]],
        [v_w_in[0].T, v_w_a_proj[0], v_w_b_proj[0], v_w_out[0]],
        b_gate_up[0], m_b_gate_up[0], v_b_gate_up[0], sums, from_chips)
    sm_blk, sm_land = _late_gather_wait(sm_send, sm_recv, sm_blk, sm_land, t_rows[0], t_gu[0], name="small_gather")
    loss, sm = _finish_small(ws, ms, vs, lax.dynamic_update_slice(sm_land, sm_blk[None], (me, 0, 0)))

    def outputs(k):
        return [sm["norm_w"][k], t_rows[k].T[None], sm["sinks"][k], t_gu[k][None], sm["bias"][k], sm["bn"][k],
                t_rows[4 + k][None], t_rows[8 + k][None], t_rows[12 + k][None], sm["fnw"][k].reshape(D)]

    return (loss[0, 0], grad_x[None], *outputs(0), *outputs(1), *outputs(2), *outputs(3))
```

```python
import functools

import numpy as np
import jax
import jax.numpy as jnp
from jax import lax
from jax.experimental import pallas as pl
from jax.experimental.pallas import tpu as pltpu

F32 = jnp.float32
MXU = jnp.bfloat16
WIRE = jnp.bfloat16

D = 1024
A_HEADS, A_KV, A_HD = 16, 2, 64
BLK = 128
B_HEADS, B_DK, B_DV = 4, 128, 256
RANK, TAU, CHUNK = 16, 16.0, 64
EPS, NEG = 1e-5, -1e30
ROPE_THETA = 10000.0
IN_WIDTH, NDEV = 7440, 8
SHARD = IN_WIDTH // NDEV
LANE = 128

C_Q, C_KD, C_VD, C_BL = 0, 1024, 1280, 1536
C_BV, C_BQ, C_BK = 2048, 3072, 3584
C_AG, C_BG, C_MA, C_MB = 4096, 5120, 6144, 7168
C_GLA, W_GLA, C_GATES, W_GATES = 2048, 2048, 4096, 4096
NF = 8192
W_BL = 128

SHARD_PAD = 944
R_IN, R_A, R_B, R_O, R_GU, ROWS = 0, 944, 1072, 1200, 1328, 1344
SMALL_ROWS = 48

ADAM_LR, ADAM_B1, ADAM_B2, ADAM_EPS, ADAM_WD, ADAM_STEP = 0.001, 0.9, 0.999, 1e-08, 0.01, 10

MESH = pl.DeviceIdType.MESH
VMEM_LIMIT = 56 * 1024 * 1024


def _cp(sem=None, **kw):
    if sem is not None:
        kw["dimension_semantics"] = sem
    return pltpu.CompilerParams(vmem_limit_bytes=VMEM_LIMIT, **kw)


def _dot(a, b):
    return jnp.dot(a, b, preferred_element_type=F32)


def _dot_nt(a, b):
    return lax.dot_general(a, b, (((1,), (1,)), ((), ())), preferred_element_type=F32)


def _dot_tn(a, b):
    return lax.dot_general(a, b, (((0,), (0,)), ((), ())), preferred_element_type=F32)


def _dot_f32(a, b):
    return jnp.dot(a, b, preferred_element_type=F32, precision=lax.Precision.HIGHEST)


def _sigmoid(z):
    return 0.5 * jnp.tanh(0.5 * z) + 0.5


def _rope(xp, cos, sin):
    return xp * cos + pltpu.roll(xp, 64, 1) * sin


def _rope_bwd(dy, cos, sin):
    return dy * cos - pltpu.roll(dy, 64, 1) * sin


def _vmem():
    return pl.BlockSpec(memory_space=pltpu.VMEM)


def _any():
    return pl.BlockSpec(memory_space=pl.ANY)


def _rope_rows():
    half = A_HD // 2
    inv = (np.float32(ROPE_THETA) ** (-np.arange(half, dtype=np.float32) / np.float32(half))).astype(np.float32)
    inv_row = jnp.asarray(np.tile(inv, 4)[None, :])
    sign_row = jnp.asarray(np.concatenate([-np.ones(64, np.float32), np.ones(64, np.float32)])[None, :])
    return inv_row, sign_row


def _prologue_rows(rows, x_ref, nw_ref, pos_ref, inv_ref, sign_ref, h_ref, cos_ref, sin_ref):
    xv = x_ref[rows, :]
    r = lax.rsqrt(jnp.mean(xv * xv, axis=-1, keepdims=True) + EPS)
    h_ref[rows, :] = ((xv * r) * nw_ref[...]).astype(h_ref.dtype)
    ang = pos_ref[rows, :].astype(F32) * inv_ref[...]
    cos_ref[rows, :] = jnp.cos(ang)
    sin_ref[rows, :] = jnp.sin(ang) * sign_ref[...]


def _proj(h, wft, after):
    T = h.shape[0]
    tT, tN = T, 512

    def body(h_ref, w_ref, after_ref, o_ref):
        o_ref[...] = _dot_nt(h_ref[...], w_ref[...])

    return pl.pallas_call(
        body, name="proj", grid=(T // tT, NF // tN),
        in_specs=[pl.BlockSpec((tT, D), lambda i, j: (i, 0)), pl.BlockSpec((tN, D), lambda i, j: (j, 0)), _any()],
        out_specs=pl.BlockSpec((tT, tN), lambda i, j: (i, j)),
        out_shape=jax.ShapeDtypeStruct((T, NF), F32),
        compiler_params=_cp(("parallel", "parallel")),
    )(h, wft, after)


def _swa_masks():
    lane = lax.broadcasted_iota(jnp.int32, (BLK, LANE), 1)
    rope_sub0 = ((lane // 32) % 2) == 0
    std_sub0 = lane < 64
    return lane, rope_sub0, std_sub0


def _swa_tri():
    qi = lax.broadcasted_iota(jnp.int32, (BLK, BLK), 0)
    kj = lax.broadcasted_iota(jnp.int32, (BLK, BLK), 1)
    return kj <= qi


def _swa_fold(full, tri):
    return jnp.where(tri, full[:, BLK:], full[:, :BLK])


def _swa_unfold(sq, tri):
    return jnp.concatenate([jnp.where(tri, 0.0, sq), jnp.where(tri, sq, 0.0)], axis=1)


def _swa_keys(kc_ref, kp_ref, vc_ref, vp_ref, cq, sq, cp, sp):
    def ropek(kref, c, s):
        kv = kref[...]
        return jnp.concatenate([_rope(kv[:, :LANE], c, s), _rope(kv[:, LANE:], c, s)], axis=1)

    K = jnp.concatenate([ropek(kp_ref, cp, sp), ropek(kc_ref, cq, sq)], axis=0).astype(MXU)
    V = jnp.concatenate([vp_ref[...], vc_ref[...]], axis=0).astype(MXU)
    return K, V


def _swa_in_specs(nb, last):
    def cur(n):
        return jnp.minimum(n, last)

    def prev(n):
        return jnp.maximum(cur(n) - 1, 0)

    kd, vd = C_KD // 256, C_VD // 256
    return [
        pl.BlockSpec((BLK, D), lambda n: (cur(n), C_Q // D)),
        pl.BlockSpec((BLK, 256), lambda n: (cur(n), kd)),
        pl.BlockSpec((BLK, 256), lambda n: (prev(n), kd)),
        pl.BlockSpec((BLK, 256), lambda n: (cur(n), vd)),
        pl.BlockSpec((BLK, 256), lambda n: (prev(n), vd)),
        pl.BlockSpec((BLK, LANE), lambda n: (cur(n), 0)),
        pl.BlockSpec((BLK, LANE), lambda n: (cur(n), 0)),
        pl.BlockSpec((BLK, LANE), lambda n: (prev(n), 0)),
        pl.BlockSpec((BLK, LANE), lambda n: (prev(n), 0)),
    ]


def _swa_fwd(proj, cos, sin, sinks):
    T = proj.shape[0]
    nb = T // BLK
    scale = A_HD ** -0.5

    def body(sinks_ref, q_ref, kc_ref, kp_ref, vc_ref, vp_ref, cq_ref, sq_ref, cp_ref, sp_ref, o_ref, l_ref):
        n = pl.program_id(0)
        cq, sq = cq_ref[...], sq_ref[...]
        K, V = _swa_keys(kc_ref, kp_ref, vc_ref, vp_ref, cq, sq, cp_ref[...], sp_ref[...])
        tri = _swa_tri()
        valid = tri | (n > 0)
        lane, rope_sub0, std_sub0 = _swa_masks()
        group = A_HEADS // A_KV
        roped, lses = {}, []

        def products(head):
            pb, sub, g = head // 2, head % 2, head // group
            if sub == 0:
                roped[pb] = _rope(q_ref[:, pb * LANE:(pb + 1) * LANE], cq, sq)
            qm = jnp.where(rope_sub0 if sub == 0 else ~rope_sub0, roped[pb], 0.0).astype(MXU)
            return _dot_nt(qm, K[:, g * LANE:(g + 1) * LANE])

        def softmax(head, s_full):
            s = jnp.where(valid, _swa_fold(s_full, tri) * scale, NEG)
            sink = sinks_ref[0, head]
            m = jnp.maximum(jnp.max(s, axis=1, keepdims=True), sink)
            e = jnp.exp(s - m)
            den = jnp.sum(e, axis=1, keepdims=True) + jnp.exp(sink - m)
            lses.append(m + jnp.log(den))
            return _swa_unfold(e / den, tri).astype(MXU)

        outs = {}
        st1 = {0: products(0), 1: products(1)}
        st2 = {0: softmax(0, st1.pop(0))}
        for head in range(A_HEADS):
            if head + 2 < A_HEADS:
                st1[head + 2] = products(head + 2)
            if head + 1 < A_HEADS:
                st2[head + 1] = softmax(head + 1, st1.pop(head + 1))
            g = head // group
            outs[head] = _dot(st2.pop(head), V[:, g * LANE:(g + 1) * LANE])
            if head % 2 == 1:
                pb = head // 2
                o_ref[:, pb * LANE:(pb + 1) * LANE] = jnp.where(std_sub0, outs[head - 1], outs[head])
        lacc = jnp.zeros((BLK, LANE), F32)
        for head in range(A_HEADS):
            lacc = jnp.where(lane == head, lses[head], lacc)
        l_ref[...] = lacc

    return pl.pallas_call(
        body, name="swa_fwd", grid=(nb,),
        in_specs=[pl.BlockSpec(memory_space=pltpu.SMEM)] + _swa_in_specs(nb, nb - 1),
        out_specs=[pl.BlockSpec((BLK, D), lambda n: (n, 0)), pl.BlockSpec((BLK, LANE), lambda n: (n, 0))],
        out_shape=[jax.ShapeDtypeStruct((T, D), F32), jax.ShapeDtypeStruct((T, LANE), F32)],
        compiler_params=_cp(("parallel",)),
    )(sinks, proj, proj, proj, proj, proj, cos, sin, cos, sin)


def _swa_bwd(proj, cos, sin, sinks, do_a, o_a, lse, after):
    T = proj.shape[0]
    nb = T // BLK
    scale = A_HD ** -0.5

    def body(sinks_ref, q_ref, kc_ref, kp_ref, vc_ref, vp_ref, cq_ref, sq_ref, cp_ref, sp_ref,
             do_ref, o_ref, l_ref, after_ref, dq_ref, dkv_ref, ds_ref, ckv_ref):
        n = pl.program_id(0)

        @pl.when(n == 0)
        def _():
            ckv_ref[...] = jnp.zeros_like(ckv_ref)
            ds_ref[...] = jnp.zeros_like(ds_ref)

        @pl.when(n < nb)
        def _():
            cq, sq, cp, sp = cq_ref[...], sq_ref[...], cp_ref[...], sp_ref[...]
            K, V = _swa_keys(kc_ref, kp_ref, vc_ref, vp_ref, cq, sq, cp, sp)
            tri = _swa_tri()
            valid = tri | (n > 0)
            lane, rope_sub0, std_sub0 = _swa_masks()
            lane_row = lax.broadcasted_iota(jnp.int32, (1, LANE), 1)
            lse_v = l_ref[...]
            dKt = [jnp.zeros((LANE, 2 * BLK), F32) for _ in range(A_KV)]
            dVt = [jnp.zeros((LANE, 2 * BLK), F32) for _ in range(A_KV)]
            dsinks, roped, roped_t, do_t = [], {}, {}, {}
            group = A_HEADS // A_KV
            dim = lax.broadcasted_iota(jnp.int32, (LANE, BLK), 0)
            rope_row0, std_row0 = ((dim // 32) % 2) == 0, dim < 64

            def products(head):
                pb, sub, g = head // 2, head % 2, head // group
                cols = slice(pb * LANE, (pb + 1) * LANE)
                Kg, Vg = K[:, g * LANE:(g + 1) * LANE], V[:, g * LANE:(g + 1) * LANE]
                if sub == 0:
                    roped[pb] = _rope(q_ref[:, cols], cq, sq)
                    roped_t[pb] = roped[pb].T
                    do_t[pb] = do_ref[:, cols].T
                qm = jnp.where(rope_sub0 if sub == 0 else ~rope_sub0, roped[pb], 0.0).astype(MXU)
                qmt = jnp.where(rope_row0 if sub == 0 else ~rope_row0, roped_t[pb], 0.0).astype(MXU)
                dov = jnp.where(std_sub0 if sub == 0 else ~std_sub0, do_ref[:, cols], 0.0)
                dovt = jnp.where(std_row0 if sub == 0 else ~std_row0, do_t[pb], 0.0).astype(MXU)
                delta = jnp.sum(dov * o_ref[:, cols], axis=1, keepdims=True)
                return qmt, dovt, delta, _dot_nt(qm, Kg), _dot_nt(dov.astype(MXU), Vg)

            def scores(head, qmt, dovt, delta, s_full, dp_full):
                lh = jnp.sum(jnp.where(lane == head, lse_v, 0.0), axis=1, keepdims=True)
                p = jnp.where(valid, jnp.exp(_swa_fold(s_full, tri) * scale - lh), 0.0)
                psink = jnp.exp(sinks_ref[0, head] - lh)
                dsinks.append(jnp.sum(-psink * delta, axis=0, keepdims=True))
                dsq = (p * (_swa_fold(dp_full, tri) - delta)) * scale
                return qmt, dovt, _swa_unfold(p, tri).astype(MXU), _swa_unfold(dsq, tri).astype(MXU)

            def grads(head, qmt, dovt, pb16, dsc):
                g = head // group
                dKt[g] = dKt[g] + _dot(qmt, dsc)
                dVt[g] = dVt[g] + _dot(dovt, pb16)
                return _dot(dsc, K[:, g * LANE:(g + 1) * LANE])

            dqs = {}
            st1 = {0: products(0), 1: products(1)}
            st2 = {0: scores(0, *st1.pop(0))}
            for head in range(A_HEADS):
                if head + 2 < A_HEADS:
                    st1[head + 2] = products(head + 2)
                if head + 1 < A_HEADS:
                    st2[head + 1] = scores(head + 1, *st1.pop(head + 1))
                dqs[head] = grads(head, *st2.pop(head))
                if head % 2 == 1:
                    pb = head // 2
                    dqp = jnp.where(rope_sub0, dqs[head - 1], dqs[head])
                    dq_ref[:, pb * LANE:(pb + 1) * LANE] = _rope_bwd(dqp, cq, sq).astype(dq_ref.dtype)
            dsink = jnp.zeros((1, LANE), F32)
            for head in range(A_HEADS):
                dsink = jnp.where(lane_row == head, dsinks[head], dsink)
            dK, dV = [a.T for a in dKt], [a.T for a in dVt]
            prev = ([_rope_bwd(dK[g][:BLK], cp, sp) for g in range(A_KV)] + [dV[g][:BLK] for g in range(A_KV)])
            cur_ = ([_rope_bwd(dK[g][BLK:], cq, sq) for g in range(A_KV)] + [dV[g][BLK:] for g in range(A_KV)])
            dkv_ref[...] = (ckv_ref[...] + jnp.concatenate(prev, axis=1)).astype(dkv_ref.dtype)
            ckv_ref[...] = jnp.concatenate(cur_, axis=1)
            ds_ref[...] = ds_ref[...] + jnp.broadcast_to(dsink, ds_ref.shape)

        @pl.when(n == nb)
        def _():
            dkv_ref[...] = ckv_ref[...].astype(dkv_ref.dtype)

    last = nb - 1

    def cur(n):
        return jnp.minimum(n, last)

    def out_kv(n):
        return (jnp.maximum(n - 1, 0), 0)

    return pl.pallas_call(
        body, name="swa_bwd", grid=(nb + 1,),
        in_specs=[pl.BlockSpec(memory_space=pltpu.SMEM)] + _swa_in_specs(nb, last) + [
            pl.BlockSpec((BLK, D), lambda n: (cur(n), 0)),
            pl.BlockSpec((BLK, D), lambda n: (cur(n), 0)),
            pl.BlockSpec((BLK, LANE), lambda n: (cur(n), 0)),
            _any(),
        ],
        out_specs=[
            pl.BlockSpec((BLK, D), lambda n: (cur(n), 0)),
            pl.BlockSpec((BLK, 512), out_kv),
            pl.BlockSpec((8, LANE), lambda n: (0, 0)),
        ],
        out_shape=[
            jax.ShapeDtypeStruct((T, D), MXU),
            jax.ShapeDtypeStruct((T, 512), MXU),
            jax.ShapeDtypeStruct((8, LANE), F32),
        ],
        scratch_shapes=[pltpu.VMEM((BLK, 512), F32)],
        compiler_params=_cp(("arbitrary",)),
    )(sinks, proj, proj, proj, proj, proj, cos, sin, cos, sin, do_a, o_a, lse, after)


GSTEP = 2 * CHUNK
ST_ROWS = B_HEADS * B_DV


def _gla_gate(bl_ref, gu_ref, bias_ref):
    gk = _dot(bl_ref[...].astype(MXU), gu_ref[...]) + bias_ref[...]
    la = (jnp.minimum(gk, 0.0) - jnp.log(1.0 + jnp.exp(-jnp.abs(gk)))) / TAU
    ri = lax.broadcasted_iota(jnp.int32, (GSTEP, GSTEP), 0)
    ci = lax.broadcasted_iota(jnp.int32, (GSTEP, GSTEP), 1)
    same = (ri // CHUNK) == (ci // CHUNK)
    lower, upper = same & (ci <= ri), same & (ci >= ri)
    b = _dot_f32(jnp.where(lower, 1.0, 0.0).astype(F32), la)
    first = lax.broadcasted_iota(jnp.int32, (GSTEP, 1), 0) < CHUNK
    return gk, la, b, lower, upper, first


def _gla_head(q_ref, k_ref, la, b, first, h):
    sl = slice(h * B_DK, (h + 1) * B_DK)
    bh, lah = b[:, sl], la[:, sl]
    bl_a = jnp.sum(lah[:CHUNK], axis=0, keepdims=True)
    bl_b = jnp.sum(lah[CHUNK:], axis=0, keepdims=True)
    blast = jnp.where(first, bl_a, bl_b)
    qc = q_ref[:, sl] * (B_DK ** -0.5)
    kh = k_ref[:, sl]
    eb, enb, esb = jnp.exp(bh), jnp.exp(-bh), jnp.exp(blast - bh)
    return qc * eb, kh * enb, kh * esb, eb, enb, esb, (jnp.exp(bl_a), jnp.exp(bl_b))


def _gla_specs(step_of):
    return [
        pl.BlockSpec((GSTEP, 512), lambda i: (step_of(i), C_BQ // 512)),
        pl.BlockSpec((GSTEP, 512), lambda i: (step_of(i), C_BK // 512)),
        pl.BlockSpec((GSTEP, D), lambda i: (step_of(i), C_BV // D)),
        pl.BlockSpec((GSTEP, W_BL), lambda i: (step_of(i), C_BL // W_BL)),
        pl.BlockSpec((W_BL, 512), lambda i: (0, 0)),
        pl.BlockSpec((1, 512), lambda i: (0, 0)),
    ]


def _gla_fwd(proj, gu_pad, bias):
    T = proj.shape[0]
    ns = T // GSTEP

    def body(q_ref, k_ref, v_ref, bl_ref, gu_ref, bias_ref, o_ref, st_ref, state_ref):
        @pl.when(pl.program_id(0) == 0)
        def _():
            state_ref[...] = jnp.zeros_like(state_ref)

        _, la, b, lower, _, first = _gla_gate(bl_ref, gu_ref, bias_ref)
        st_ref[0:ST_ROWS, :] = state_ref[...]

        def within(h):
            q_e, k_e, k_s, _, _, _, decays = _gla_head(q_ref, k_ref, la, b, first, h)
            vh = v_ref[:, h * B_DV:(h + 1) * B_DV].astype(MXU)
            q_eb = q_e.astype(MXU)
            att = jnp.where(lower, _dot_nt(q_eb, k_e.astype(MXU)), 0.0)
            return vh, q_eb, k_s.astype(MXU), _dot(att.astype(MXU), vh), decays

        def across(h, vh, q_eb, k_sb, o_intra, decays):
            rows = slice(h * B_DV, (h + 1) * B_DV)
            s0 = state_ref[rows, :]
            o_a = o_intra[:CHUNK] + _dot_nt(q_eb[:CHUNK], s0.astype(MXU))
            s1 = s0 * decays[0] + _dot_tn(vh[:CHUNK], k_sb[:CHUNK])
            st_ref[ST_ROWS + h * B_DV:ST_ROWS + (h + 1) * B_DV, :] = s1
            o_b = o_intra[CHUNK:] + _dot_nt(q_eb[CHUNK:], s1.astype(MXU))
            state_ref[rows, :] = s1 * decays[1] + _dot_tn(vh[CHUNK:], k_sb[CHUNK:])
            o_ref[:, rows] = jnp.concatenate([o_a, o_b], axis=0)

        for h in range(B_HEADS):
            across(h, *within(h))

    return pl.pallas_call(
        body, name="gla_fwd", grid=(ns,),
        in_specs=_gla_specs(lambda i: i),
        out_specs=[pl.BlockSpec((GSTEP, D), lambda i: (i, 0)),
                   pl.BlockSpec((2 * ST_ROWS, B_DK), lambda i: (i, 0))],
        out_shape=[jax.ShapeDtypeStruct((T, D), F32),
                   jax.ShapeDtypeStruct((ns * 2 * ST_ROWS, B_DK), F32)],
        scratch_shapes=[pltpu.VMEM((ST_ROWS, B_DK), F32)],
        compiler_params=_cp(("arbitrary",)),
    )(proj, proj, proj, proj, gu_pad, bias)


def _gla_bwd(proj, gu_pad, bias, states, do_b):
    T = proj.shape[0]
    ns = T // GSTEP
    o_q, o_k = C_BQ - C_GLA, C_BK - C_GLA

    def body(q_ref, k_ref, v_ref, bl_ref, gu_ref, bias_ref, st_ref, do_ref,
             dg_ref, dbl_ref, ggu_ref, gbias_ref, gt_ref):
        @pl.when(pl.program_id(0) == 0)
        def _():
            gt_ref[...] = jnp.zeros_like(gt_ref)
            ggu_ref[...] = jnp.zeros_like(ggu_ref)
            gbias_ref[...] = jnp.zeros_like(gbias_ref)

        gk, la, b, lower, upper_mask, first = _gla_gate(bl_ref, gu_ref, bias_ref)
        upper = jnp.where(upper_mask, 1.0, 0.0).astype(F32)
        lo, hi = slice(0, CHUNK), slice(CHUNK, GSTEP)
        dla_parts = []

        def within(h):
            q_e, k_e, k_s, eb, enb, esb, decays = _gla_head(q_ref, k_ref, la, b, first, h)
            vh = v_ref[:, h * B_DV:(h + 1) * B_DV].astype(MXU)
            doh = do_ref[:, h * B_DV:(h + 1) * B_DV].astype(MXU)
            q_eb, k_eb = q_e.astype(MXU), k_e.astype(MXU)
            att = jnp.where(lower, _dot_nt(q_eb, k_eb), 0.0).astype(MXU)
            datt = jnp.where(lower, _dot_nt(doh, vh), 0.0).astype(MXU)
            return (q_e, k_e, k_s, eb, enb, esb, decays, vh, doh, q_eb, k_s.astype(MXU),
                    _dot(datt, k_eb), _dot_tn(datt, q_eb), _dot_tn(att, doh))

        def across(h, q_e, k_e, k_s, eb, enb, esb, decays, vh, doh, q_eb, k_sb, dq_i, dk_e, dv_i):
            dec_a, dec_b = decays
            rows = slice(h * B_DV, (h + 1) * B_DV)
            s0 = st_ref[rows, :]
            s1 = st_ref[ST_ROWS + h * B_DV:ST_ROWS + (h + 1) * B_DV, :]
            g2 = gt_ref[rows, :]
            g2b = g2.astype(MXU)
            dq_b = dq_i[hi] + _dot(doh[hi], s1.astype(MXU))
            dks_b = _dot(vh[hi], g2b)
            dv_b = dv_i[hi] + _dot_nt(k_sb[hi], g2b)
            ddec_b = jnp.sum(g2 * s1, axis=0, keepdims=True)
            g1 = g2 * dec_b + _dot_tn(doh[hi], q_eb[hi])
            g1b = g1.astype(MXU)
            dq_a = dq_i[lo] + _dot(doh[lo], s0.astype(MXU))
            dks_a = _dot(vh[lo], g1b)
            dv_a = dv_i[lo] + _dot_nt(k_sb[lo], g1b)
            ddec_a = jnp.sum(g1 * s0, axis=0, keepdims=True)
            gt_ref[rows, :] = g1 * dec_a + _dot_tn(doh[lo], q_eb[lo])
            dq_e = jnp.concatenate([dq_a, dq_b], axis=0)
            dk_s = jnp.concatenate([dks_a, dks_b], axis=0)
            dg_ref[:, rows] = jnp.concatenate([dv_a, dv_b], axis=0).astype(dg_ref.dtype)
            dg_ref[:, o_q + h * B_DK:o_q + (h + 1) * B_DK] = (dq_e * eb * (B_DK ** -0.5)).astype(dg_ref.dtype)
            dg_ref[:, o_k + h * B_DK:o_k + (h + 1) * B_DK] = (dk_e * enb + dk_s * esb).astype(dg_ref.dtype)
            dks_ks = dk_s * k_s
            db = dq_e * q_e - dk_e * k_e - dks_ks
            dbl_a = jnp.sum(dks_ks[lo], axis=0, keepdims=True) + ddec_a * dec_a
            dbl_b = jnp.sum(dks_ks[hi], axis=0, keepdims=True) + ddec_b * dec_b
            dla_parts.append(_dot_f32(upper, db) + jnp.where(first, dbl_a, dbl_b))

        for h in range(B_HEADS):
            across(h, *within(h))
        dla = jnp.concatenate(dla_parts, axis=1)
        dgk = dla * (1.0 / TAU) * _sigmoid(-gk)
        dgkb = dgk.astype(MXU)
        dbl_ref[...] = _dot_nt(dgkb, gu_ref[...]).astype(dbl_ref.dtype)
        ggu_ref[...] = ggu_ref[...] + _dot_tn(bl_ref[...].astype(MXU), dgkb)
        gbias_ref[...] = gbias_ref[...] + jnp.broadcast_to(jnp.sum(dgk, axis=0, keepdims=True), gbias_ref.shape)

    def rev(i):
        return ns - 1 - i

    return pl.pallas_call(
        body, name="gla_bwd", grid=(ns,),
        in_specs=_gla_specs(rev) + [
            pl.BlockSpec((2 * ST_ROWS, B_DK), lambda i: (rev(i), 0)),
            pl.BlockSpec((GSTEP, D), lambda i: (rev(i), 0)),
        ],
        out_specs=[
            pl.BlockSpec((GSTEP, W_GLA), lambda i: (rev(i), 0)),
            pl.BlockSpec((GSTEP, W_BL), lambda i: (rev(i), 0)),
            pl.BlockSpec((W_BL, 512), lambda i: (0, 0)),
            pl.BlockSpec((8, 512), lambda i: (0, 0)),
        ],
        out_shape=[
            jax.ShapeDtypeStruct((T, W_GLA), MXU),
            jax.ShapeDtypeStruct((T, W_BL), MXU),
            jax.ShapeDtypeStruct((W_BL, 512), F32),
            jax.ShapeDtypeStruct((8, 512), F32),
        ],
        scratch_shapes=[pltpu.VMEM((B_HEADS * B_DV, B_DK), F32)],
        compiler_params=_cp(("arbitrary",)),
    )(proj, proj, proj, proj, gu_pad, bias, states, do_b)


def _mid(x, target, proj, o_a, o_b, w_a, w_b, w_out, w_bn4, fnw):
    T = x.shape[0]
    tT = min(T, 128)
    nbuf = 4
    o_ag, o_bg, o_ma, o_mb = (c - C_GATES for c in (C_AG, C_BG, C_MA, C_MB))

    def body(x_ref, t_ref, oa_ref, ob_ref, gates_ref, wa_ref, wb_ref, wo_ref, wbn_ref, fnw_ref,
             dx2_ref, doa_ref, dob_ref, dgates_ref,
             gwa_ref, gwb_ref, gwo_ref, gfn_ref, gbn_ref, loss_ref, buf_ref):
        i = pl.program_id(0)

        @pl.when(i == 0)
        def _():
            for r in (gwa_ref, gwb_ref, gwo_ref, gfn_ref, gbn_ref, loss_ref):
                r[...] = jnp.zeros_like(r)

        rows = pl.ds(pl.multiple_of((i % nbuf) * tT, tT), tT)

        def keep(k, val):
            buf_ref[k, rows, :] = val

        oa, ag = oa_ref[...], gates_ref[:, o_ag:o_ag + D]
        sg_a = _sigmoid(ag)
        silu_a = ag * sg_a
        oag_b = (oa * silu_a).astype(MXU)
        keep(0, oag_b)
        y_a = _dot(oag_b, wa_ref[...])

        ob, bg = ob_ref[...], gates_ref[:, o_bg:o_bg + D]
        rbs, obhats = [], []
        for h in range(B_HEADS):
            obh = ob[:, h * B_DV:(h + 1) * B_DV]
            rb = lax.rsqrt(jnp.mean(obh * obh, axis=-1, keepdims=True) + EPS)
            rbs.append(rb)
            obhats.append(obh * rb)
        obhat = jnp.concatenate(obhats, axis=1)
        wbn = wbn_ref[...]
        obn = obhat * wbn
        sg_b = _sigmoid(bg)
        silu_b = bg * sg_b
        obg_b = (obn * silu_b).astype(MXU)
        keep(1, obg_b)
        y_b = _dot(obg_b, wb_ref[...])

        sa, sb = _sigmoid(gates_ref[:, o_ma:o_ma + D]), _sigmoid(gates_ref[:, o_mb:o_mb + D])
        mg_b = (sa * y_a + sb * y_b).astype(MXU)
        keep(2, mg_b)
        x2 = x_ref[...] + _dot(mg_b, wo_ref[...])
        r2 = lax.rsqrt(jnp.mean(x2 * x2, axis=-1, keepdims=True) + EPS)
        xh2 = x2 * r2
        fw = fnw_ref[...]
        err = xh2 * fw - t_ref[...]
        tok = jnp.mean(err * err, axis=-1, keepdims=True)
        loss_ref[...] = loss_ref[...] + 0.5 * jnp.sum(tok, axis=0, keepdims=True)

        dy = err * (1.0 / D)
        gfn_ref[...] = gfn_ref[...] + jnp.broadcast_to(jnp.sum(dy * xh2, axis=0, keepdims=True), gfn_ref.shape)
        gy = dy * fw
        dx2 = r2 * (gy - xh2 * jnp.mean(gy * xh2, axis=-1, keepdims=True))
        dx2_ref[...] = dx2
        dx2_b = dx2.astype(MXU)
        keep(5, dx2_b)
        dmg = _dot_nt(dx2_b, wo_ref[...])

        dgates_ref[:, o_ma:o_ma + D] = (dmg * y_a * sa * (1.0 - sa)).astype(dgates_ref.dtype)
        dgates_ref[:, o_mb:o_mb + D] = (dmg * y_b * sb * (1.0 - sb)).astype(dgates_ref.dtype)
        dya_b = (dmg * sa).astype(MXU)
        dyb_b = (dmg * sb).astype(MXU)
        keep(3, dya_b)
        keep(4, dyb_b)
        doag = _dot_nt(dya_b, wa_ref[...])
        dobg = _dot_nt(dyb_b, wb_ref[...])

        @pl.when(i % nbuf == nbuf - 1)
        def _():
            gwa_ref[...] = gwa_ref[...] + _dot_tn(buf_ref[0], buf_ref[3])
            gwb_ref[...] = gwb_ref[...] + _dot_tn(buf_ref[1], buf_ref[4])
            gwo_ref[...] = gwo_ref[...] + _dot_tn(buf_ref[2], buf_ref[5])

        doa_ref[...] = doag * silu_a
        dgates_ref[:, o_ag:o_ag + D] = (doag * oa * (sg_a * (1.0 + ag * (1.0 - sg_a)))).astype(dgates_ref.dtype)
        dobn = dobg * silu_b
        dgates_ref[:, o_bg:o_bg + D] = (dobg * obn * (sg_b * (1.0 + bg * (1.0 - sg_b)))).astype(dgates_ref.dtype)
        gg = dobn * wbn
        gbn = jnp.zeros((1, B_DV), F32)
        for h in range(B_HEADS):
            sl = slice(h * B_DV, (h + 1) * B_DV)
            gbn = gbn + jnp.sum(dobn[:, sl] * obhats[h], axis=0, keepdims=True)
            ggh = gg[:, sl]
            dob_ref[:, sl] = rbs[h] * (ggh - obhats[h] * jnp.mean(ggh * obhats[h], axis=-1, keepdims=True))
        gbn_ref[...] = gbn_ref[...] + jnp.broadcast_to(gbn, gbn_ref.shape)

    assert (T // tT) % nbuf == 0
    tile = pl.BlockSpec((tT, D), lambda i: (i, 0))
    row = pl.BlockSpec((1, D), lambda i: (0, 0))
    acc8 = pl.BlockSpec((8, D), lambda i: (0, 0))
    return pl.pallas_call(
        body, name="mid", grid=(T // tT,),
        in_specs=[tile, tile, tile, tile, pl.BlockSpec((tT, W_GATES), lambda i: (i, C_GATES // W_GATES)),
                  _vmem(), _vmem(), _vmem(), row, row],
        out_specs=[tile, tile, tile, pl.BlockSpec((tT, W_GATES), lambda i: (i, 0)), _vmem(), _vmem(), _vmem(),
                   acc8, pl.BlockSpec((8, B_DV), lambda i: (0, 0)), pl.BlockSpec((8, LANE), lambda i: (0, 0))],
        out_shape=[
            jax.ShapeDtypeStruct((T, D), F32),
            jax.ShapeDtypeStruct((T, D), F32),
            jax.ShapeDtypeStruct((T, D), F32),
            jax.ShapeDtypeStruct((T, W_GATES), MXU),
            jax.ShapeDtypeStruct((D, D), F32),
            jax.ShapeDtypeStruct((D, D), F32),
            jax.ShapeDtypeStruct((D, D), F32),
            jax.ShapeDtypeStruct((8, D), F32),
            jax.ShapeDtypeStruct((8, B_DV), F32),
            jax.ShapeDtypeStruct((8, LANE), F32),
        ],
        scratch_shapes=[pltpu.VMEM((6, nbuf * tT, D), MXU)],
        compiler_params=_cp(("arbitrary",)),
    )(x, target, o_a, o_b, proj, w_a, w_b, w_out, w_bn4, fnw)


DH = D // 2


_GW_TILES = (("q", 0, 512, 0), ("q", 1, 512, 512), ("kv", 0, 256, 1024), ("bl", 0, RANK, 5376),
             ("gla", 0, 512, 3328), ("gla", 1, 512, 3840), ("gla", 2, 512, 2304), ("gla", 3, 512, 2816),
             ("gates", 0, 512, 1280), ("gates", 1, 512, 1792), ("gates", 2, 512, 4352), ("gates", 3, 512, 4864),
             ("gates", 4, 512, 5392), ("gates", 5, 512, 5904), ("gates", 6, 512, 6416), ("gates", 7, 512, 6928))


def _gw_unpermute(piece, t):
    if piece == "q":
        parts = []
        for blk in range(t.shape[0] // LANE):
            g = [t[blk * LANE + 32 * i:blk * LANE + 32 * (i + 1)] for i in range(4)]
            parts += [g[0], g[2], g[1], g[3]]
        return jnp.concatenate(parts, axis=0)
    if piece == "kv":
        k = [t[64 * i:64 * i + 32] + t[64 * i + 32:64 * i + 64] for i in range(4)]
        v = [t[256 + 128 * g:256 + 128 * g + 64] + t[256 + 128 * g + 64:256 + 128 * (g + 1)] for g in range(2)]
        return jnp.concatenate(k + v, axis=0)
    if piece == "bl":
        return t[:RANK]
    return t


def _gw_half(h, pieces, half, after=None):
    T = h.shape[0]
    steps = len(_GW_TILES)

    def body(*refs):
        h_ref = refs[0]
        srcs = dict(zip(("q", "kv", "bl", "gla", "gates"), refs[1:6]))
        o_ref, stage, sems = refs[-3:]
        j = pl.program_id(0)

        def out_copy(k):
            _, _, n, off = _GW_TILES[k]
            return pltpu.make_async_copy(stage.at[k % 2, 0:n], o_ref.at[pl.ds(off, n)], sems.at[k % 2])

        for k, (piece, _, n, _) in enumerate(_GW_TILES):
            @pl.when(j == k)
            def _(k=k, piece=piece, n=n):
                if k >= 2:
                    out_copy(k - 2).wait()
                t = _gw_unpermute(piece, _dot_tn(srcs[piece][...], h_ref[...]))
                stage[k % 2, 0:n, :] = t.astype(stage.dtype)
                out_copy(k).start()

        @pl.when(j == steps - 1)
        def _():
            out_copy(steps - 2).wait()
            out_copy(steps - 1).wait()

    def tile_of(lo, hi):
        return lambda j: (0, jnp.clip(j - lo, 0, hi - lo - 1))

    in_specs = [pl.BlockSpec((T, DH), lambda j: (0, half)),
                pl.BlockSpec((T, 512), tile_of(0, 2)), pl.BlockSpec((T, 512), lambda j: (0, 0)),
                pl.BlockSpec((T, W_BL), lambda j: (0, 0)),
                pl.BlockSpec((T, 512), tile_of(4, 8)), pl.BlockSpec((T, 512), tile_of(8, 16))]
    args = [h, *pieces]
    if after is not None:
        in_specs.append(_any())
        args.append(after)
    return pl.pallas_call(
        body, name=f"gw_in_half{half}", grid=(steps,),
        in_specs=in_specs, out_specs=_any(),
        out_shape=jax.ShapeDtypeStruct((IN_WIDTH, DH), WIRE),
        scratch_shapes=[pltpu.VMEM((2, 512, DH), WIRE), pltpu.SemaphoreType.DMA((2,))],
        compiler_params=_cp(("arbitrary",)),
    )(*args)


def _chip_copies(s_ref, got_ref, send_sems, recv_sems):
    x, y, c = _place()
    chips = [(1 - x, y), (x, 1 - y), (1 - x, 1 - y)]
    return [pltpu.make_async_remote_copy(
        src_ref=s_ref.at[2 * px + py], dst_ref=got_ref.at[j],
        send_sem=send_sems.at[j], recv_sem=recv_sems.at[j], device_id=(px, py, c), device_id_type=MESH)
        for j, (px, py) in enumerate(chips)]


_EFFECT = pltpu.SideEffectType.DATAFLOW_SIDE_EFFECTING


def _hbm():
    return pl.BlockSpec(memory_space=pltpu.HBM)


def _sem():
    return pl.BlockSpec(memory_space=pltpu.SEMAPHORE)


def _chip_start(sums, half):
    land = pltpu.with_memory_space_constraint(lax.empty((3,) + sums.shape[1:], sums.dtype), pltpu.HBM)

    def body(s_ref, land_ref, send_sems, recv_sems, s_thru, land_thru, token):
        for cp in _chip_copies(s_ref, land_ref, send_sems, recv_sems):
            cp.start()
        token[...] = jnp.zeros_like(token)

    return pl.pallas_call(
        body, name=f"chip_start{half}",
        out_shape=(pltpu.SemaphoreType.DMA((3,)), pltpu.SemaphoreType.DMA((3,)),
                   pltpu.HBM(sums.shape, sums.dtype), pltpu.HBM(land.shape, land.dtype),
                   jax.ShapeDtypeStruct((8, LANE), F32)),
        in_specs=(_hbm(), _hbm()), out_specs=(_sem(), _sem(), _hbm(), _hbm(), _vmem()),
        input_output_aliases={0: 2, 1: 3},
        compiler_params=pltpu.CompilerParams(has_side_effects=_EFFECT),
    )(pltpu.with_memory_space_constraint(sums, pltpu.HBM), land)


def _chip_wait(send_sems, recv_sems, s_thru, land_thru, after, half):
    def body(s_ref, land_ref, send_sems, recv_sems, after_ref, s_out, got_ref):
        copies = _chip_copies(s_ref, land_ref, send_sems, recv_sems)
        for cp in copies:
            cp.wait_send()
        for cp in copies:
            cp.wait_recv()

    return pl.pallas_call(
        body, name=f"chip_wait{half}",
        out_shape=(pltpu.HBM(s_thru.shape, s_thru.dtype), pltpu.HBM(land_thru.shape, land_thru.dtype)),
        in_specs=(_hbm(), _hbm(), _sem(), _sem(), _any()), out_specs=(_hbm(), _hbm()),
        input_output_aliases={0: 0, 1: 1},
        compiler_params=pltpu.CompilerParams(has_side_effects=_EFFECT),
    )(s_thru, land_thru, send_sems, recv_sems, after)


def _dh_norm(pieces, offsets, wf, x, dx2, norm_w, after):
    T = x.shape[0]
    tT = min(T, 256)
    widths = [p.shape[1] for p in pieces]
    npc = len(pieces)

    def body(*refs):
        dp_refs = refs[:npc]
        wf_ref, x_ref, dx2_ref, nw_ref, _, gx_ref, gnw_ref = refs[npc:]

        @pl.when(pl.program_id(0) == 0)
        def _():
            gnw_ref[...] = jnp.zeros_like(gnw_ref)

        dh = jnp.zeros((tT, D), F32)
        for dp_ref, off, w in zip(dp_refs, offsets, widths):
            dh = dh + _dot(dp_ref[...], wf_ref[off:off + w, :])
        xv = x_ref[...]
        r = lax.rsqrt(jnp.mean(xv * xv, axis=-1, keepdims=True) + EPS)
        xh = xv * r
        gnw_ref[...] = gnw_ref[...] + jnp.broadcast_to(jnp.sum(dh * xh, axis=0, keepdims=True), gnw_ref.shape)
        g = dh * nw_ref[...]
        gx_ref[...] = r * (g - xh * jnp.mean(g * xh, axis=-1, keepdims=True)) + dx2_ref[...]

    tile = pl.BlockSpec((tT, D), lambda i: (i, 0))
    return pl.pallas_call(
        body, name="dh_norm", grid=(T // tT,),
        in_specs=[pl.BlockSpec((tT, w), lambda i: (i, 0)) for w in widths]
        + [_vmem(), tile, tile, pl.BlockSpec((1, D), lambda i: (0, 0)), _any()],
        out_specs=[tile, pl.BlockSpec((8, D), lambda i: (0, 0))],
        out_shape=[jax.ShapeDtypeStruct((T, D), F32), jax.ShapeDtypeStruct((8, D), F32)],
        compiler_params=_cp(("arbitrary",)),
    )(*pieces, wf, x, dx2, norm_w, after)


def _adamw_math(w, g, m, v):
    m = ADAM_B1 * m + (1.0 - ADAM_B1) * g
    v = ADAM_B2 * v + (1.0 - ADAM_B2) * (g * g)
    m_hat = m / (1.0 - ADAM_B1 ** ADAM_STEP)
    v_hat = v / (1.0 - ADAM_B2 ** ADAM_STEP)
    delta = -ADAM_LR * (m_hat / (jnp.sqrt(v_hat) + ADAM_EPS) + ADAM_WD * w)
    return delta, m, v


def _fetch_partials(s_ref, got_ref, buf, sems):
    x, y, _ = _place()
    cps = [pltpu.make_async_copy(s_ref.at[2 * x + y], buf.at[0], sems.at[0])]
    cps += [pltpu.make_async_copy(got_ref.at[j], buf.at[1 + j], sems.at[1 + j]) for j in range(3)]
    for cp in cps:
        cp.start()
    for cp in cps:
        cp.wait()


SMALL_AT = dict(norm_w=0, fnw=8, bias=16, bn=24, sinks=32, loss=40)
ROW_AT = (R_IN, R_A, R_B, R_O)


def _finish_small(ws, ms, vs, smalls):
    names = ["norm_w", "fnw", "bias", "bn", "sinks"]
    widths = [ws[n].shape[1] for n in names]

    def body(*refs):
        w_refs, m_refs, v_refs = refs[0:5], refs[5:10], refs[10:15]
        smalls_ref, loss_ref = refs[15], refs[16]
        outs, tot = refs[17:37], refs[37]
        acc = smalls_ref[0]
        for d in range(1, NDEV):
            acc = acc + smalls_ref[d]
        tot[...] = acc
        loss_ref[...] = tot[SMALL_AT["loss"]:SMALL_AT["loss"] + 1, 0:1]
        for p, (nm_, wd) in enumerate(zip(names, widths)):
            r = SMALL_AT[nm_]
            g = tot[r:r + 1, 0:wd]
            d, nm, nv = _adamw_math(w_refs[p][...], g, m_refs[p][...], v_refs[p][...])
            for o, val in zip(outs[4 * p:4 * p + 4], (g, d, nm, nv)):
                o[...] = val

    res = pl.pallas_call(
        body, name="finish_small",
        in_specs=[_vmem()] * 16, out_specs=[_vmem()] * 21,
        out_shape=[jax.ShapeDtypeStruct((1, 1), F32)]
        + [jax.ShapeDtypeStruct((1, wd), F32) for wd in widths for _ in range(4)],
        scratch_shapes=[pltpu.VMEM((SMALL_ROWS, D), F32)],
        compiler_params=_cp(),
    )(*[ws[n] for n in names], *[ms[n] for n in names], *[vs[n] for n in names], smalls)
    return res[0], {n: tuple(res[1 + 4 * p:5 + 4 * p]) for p, n in enumerate(names)}


def _finish(w_rows, m_rows, v_rows, gu_w, gu_m, gu_v, sums, got):
    shapes = [w.shape for w in w_rows]

    def body(*refs):
        wr_refs, mr_refs, vr_refs = refs[0:4], refs[4:8], refs[8:12]
        guw_ref, gum_ref, guv_ref = refs[12:15]
        s_refs, got_refs = refs[15:17], refs[17:19]
        row_outs = refs[19:35]
        gu_outs = refs[35:39]
        buf, gsh, sems = refs[39:]
        x, y, c = _place()
        me_slot = 4 * x + 2 * y + c
        unshift = lax.rem(SHARD_PAD - 2 * me_slot, SHARD_PAD)
        for hf in range(2):
            _fetch_partials(s_refs[hf], got_refs[hf], buf, sems)
            for p in range(4):
                n, off = shapes[p][0], ROW_AT[p]
                nf = SHARD_PAD if p == 0 else n
                for cc in range(DH // LANE):
                    src = slice(cc * LANE, (cc + 1) * LANE)
                    cols = slice(hf * DH + cc * LANE, hf * DH + (cc + 1) * LANE)
                    g = buf[0, off:off + nf, src].astype(F32)
                    for j in range(1, 4):
                        g = g + buf[j, off:off + nf, src].astype(F32)
                    if p == 0:
                        gsh[...] = pltpu.roll(g, unshift, 0)
                        g = gsh[0:n, :]
                    d, nm, nv = _adamw_math(wr_refs[p][:, cols], g, mr_refs[p][:, cols], vr_refs[p][:, cols])
                    for o, val in zip(row_outs[4 * p:4 * p + 4], (g, d, nm, nv)):
                        o[:, cols] = val
            if hf == 0:
                g = buf[0, R_GU:R_GU + RANK, 0:64].astype(F32)
                for j in range(1, 4):
                    g = g + buf[j, R_GU:R_GU + RANK, 0:64].astype(F32)
                d, nm, nv = _adamw_math(guw_ref[...], g, gum_ref[...], guv_ref[...])
                for o, val in zip(gu_outs, (g, d, nm, nv)):
                    o[...] = val

    res = pl.pallas_call(
        body, name="finish",
        in_specs=[_vmem()] * 15 + [_any()] * 4,
        out_specs=[_vmem()] * 20,
        out_shape=[jax.ShapeDtypeStruct(s, F32) for s in shapes for _ in range(4)]
        + [jax.ShapeDtypeStruct((RANK, 64), F32)] * 4,
        scratch_shapes=[pltpu.VMEM((4, ROWS, DH), sums[0].dtype), pltpu.VMEM((SHARD_PAD, LANE), F32),
                        pltpu.SemaphoreType.DMA((4,))],
        compiler_params=_cp(),
    )(*w_rows, *m_rows, *v_rows, gu_w, gu_m, gu_v, *sums, *got)
    return tuple(res[0:16]), tuple(res[16:20])


def _place():
    x, y, c = lax.axis_index("x"), lax.axis_index("y"), lax.axis_index("c")
    return x, y, c


def _peers(x, y, c):
    return [(x ^ dx, y ^ dy, c ^ dc) for dx in range(2) for dy in range(2) for dc in range(2) if dx + dy + dc]


def _late_gather_start(blk, after, name="late_gather"):
    land = pltpu.with_memory_space_constraint(lax.empty((NDEV,) + blk.shape, blk.dtype), pltpu.HBM)

    def body(b_ref, land_ref, after_ref, send_sems, recv_sems, b_thru, land_thru, token):
        x, y, c = _place()
        for k, to in enumerate(_peers(x, y, c)):
            pltpu.make_async_remote_copy(
                src_ref=b_ref, dst_ref=land_ref.at[4 * x + 2 * y + c], send_sem=send_sems.at[k],
                recv_sem=recv_sems.at[k], device_id=to, device_id_type=MESH).start()
        token[...] = jnp.zeros_like(token)

    return pl.pallas_call(
        body, name=name + "_start",
        out_shape=(pltpu.SemaphoreType.DMA((7,)), pltpu.SemaphoreType.DMA((7,)),
                   pltpu.HBM(blk.shape, blk.dtype), pltpu.HBM(land.shape, land.dtype),
                   jax.ShapeDtypeStruct((8, LANE), F32)),
        in_specs=(_hbm(), _hbm(), _any()), out_specs=(_sem(), _sem(), _hbm(), _hbm(), _vmem()),
        input_output_aliases={0: 2, 1: 3},
        compiler_params=pltpu.CompilerParams(has_side_effects=_EFFECT),
    )(pltpu.with_memory_space_constraint(blk, pltpu.HBM), land, after)


def _late_gather_wait(send_sems, recv_sems, b_thru, land_thru, after, after2, name="late_gather"):
    def body(b_ref, land_ref, send_sems, recv_sems, after_ref, after2_ref, b_out, got_ref):
        x, y, c = _place()
        copies = [pltpu.make_async_remote_copy(
            src_ref=b_ref, dst_ref=land_ref.at[4 * x + 2 * y + c], send_sem=send_sems.at[k],
            recv_sem=recv_sems.at[k], device_id=to, device_id_type=MESH)
            for k, to in enumerate(_peers(x, y, c))]
        for cp in copies:
            cp.wait_send()
        for cp in copies:
            cp.wait_recv()

    return pl.pallas_call(
        body, name=name + "_wait",
        out_shape=(pltpu.HBM(b_thru.shape, b_thru.dtype), pltpu.HBM(land_thru.shape, land_thru.dtype)),
        in_specs=(_hbm(), _hbm(), _sem(), _sem(), _any(), _any()), out_specs=(_hbm(), _hbm()),
        input_output_aliases={0: 0, 1: 1},
        compiler_params=pltpu.CompilerParams(has_side_effects=_EFFECT),
    )(b_thru, land_thru, send_sems, recv_sems, after, after2)


G_ROWS = SHARD_PAD + RANK


def _gather_blocks(w_in_t, gu_s, xs, norm_w, pos_col):
    rows, cols = G_ROWS, D
    T = xs.shape[0]
    tT = min(T, 256)
    inv_row, sign_row = _rope_rows()

    def body(wi_ref, gu_ref, xs_ref, nw_ref, pos_ref, inv_ref, sign_ref,
             out_ref, h_ref, cos_ref, sin_ref, x_ref, frame_ref, send_sems, recv_sems, local_sem):
        x, y, c = _place()
        me, sibling = (x, y, c), (x, y, 1 - c)
        chips = [(1 - x, y), (x, 1 - y), (1 - x, 1 - y)]
        shift = 2 * (4 * x + 2 * y + c)
        frame_ref[SHARD - SHARD % 8:, :] = jnp.zeros((SHARD_PAD - SHARD + SHARD % 8, D), F32)
        frame_ref[:SHARD, :] = wi_ref[...]
        for cc in range(D // LANE):
            cs = slice(cc * LANE, (cc + 1) * LANE)
            x_ref[0:SHARD_PAD, cs] = pltpu.roll(frame_ref[:, cs], shift, 0).astype(x_ref.dtype)
        x_ref[SHARD_PAD:G_ROWS, :] = jnp.zeros((RANK, D), x_ref.dtype)
        x_ref[SHARD_PAD:G_ROWS, 0:64] = gu_ref[...].astype(x_ref.dtype)

        def slot(px, py, pc):
            return out_ref.at[4 * px + 2 * py + pc]

        def copy(k, block, to, src=None):
            return pltpu.make_async_remote_copy(
                src_ref=slot(*block) if src is None else src, dst_ref=slot(*block),
                send_sem=send_sems.at[k], recv_sem=recv_sems.at[k], device_id=to, device_id_type=MESH)

        mine = pltpu.make_async_copy(x_ref, slot(*me), local_sem)
        mine.start()
        first = [copy(0, me, sibling, src=x_ref)]
        first += [copy(1 + j, me, (*chip, c), src=x_ref) for j, chip in enumerate(chips)]
        for cp in first:
            cp.start()

        @pl.loop(0, T // tT)
        def _(i):
            rows_i = pl.ds(pl.multiple_of(i * tT, tT), tT)
            _prologue_rows(rows_i, xs_ref, nw_ref, pos_ref, inv_ref, sign_ref, h_ref, cos_ref, sin_ref)

        passed = [copy(4 + j, (*chip, c), sibling) for j, chip in enumerate(chips)]
        for j, chip in enumerate(chips):
            copy(1 + j, (*chip, c), me).wait_recv()
            passed[j].start()
        copy(0, sibling, me).wait_recv()
        for j, chip in enumerate(chips):
            copy(4 + j, (*chip, 1 - c), me).wait_recv()
        for cp in first + passed:
            cp.wait_send()
        mine.wait()

    return pl.pallas_call(
        body, name="gather_weights",
        in_specs=[_vmem()] * 7, out_specs=[_any()] + [_vmem()] * 3,
        out_shape=[jax.ShapeDtypeStruct((NDEV, rows, cols), WIRE), jax.ShapeDtypeStruct((T, D), MXU),
                   jax.ShapeDtypeStruct((T, LANE), F32), jax.ShapeDtypeStruct((T, LANE), F32)],
        scratch_shapes=[pltpu.VMEM((rows, cols), WIRE), pltpu.VMEM((SHARD_PAD, D), F32),
                        pltpu.SemaphoreType.DMA((7,)), pltpu.SemaphoreType.DMA((7,)), pltpu.SemaphoreType.DMA],
        compiler_params=_cp(),
    )(w_in_t, gu_s, xs, norm_w, pos_col, inv_row, sign_row)


def _pair_reduce(gwt, tail):
    n = gwt.shape[1]
    blk = (4, ROWS, n)

    def body(g_ref, t_ref, out_ref, got, own, send_sems, recv_sems, own_sems):
        x, y, c = _place()

        def parts(d, dst):
            frame = g_ref.at[pl.ds(pl.multiple_of(FRAME * d, 16), SHARD_PAD)]
            return [(frame, dst.at[0:SHARD_PAD]), (t_ref.at[d], dst.at[SHARD_PAD:ROWS])]

        sends, loads = [], []
        for chip in range(4):
            sends.append([pltpu.make_async_remote_copy(
                src_ref=s, dst_ref=d_, send_sem=send_sems.at[chip, k], recv_sem=recv_sems.at[chip, k],
                device_id=(x, y, 1 - c), device_id_type=MESH)
                for k, (s, d_) in enumerate(parts(2 * chip + (1 - c), got.at[chip]))])
            loads.append([pltpu.make_async_copy(s, d_, own_sems.at[chip, k])
                          for k, (s, d_) in enumerate(parts(2 * chip + c, own.at[chip]))])
        for group in sends + loads:
            for cp in group:
                cp.start()
        for chip in range(4):
            for cp in loads[chip]:
                cp.wait()
            for cp in sends[chip]:
                cp.wait_recv()
            out_ref[chip] = (own[chip].astype(F32) + got[chip].astype(F32)).astype(out_ref.dtype)
        for group in sends:
            for cp in group:
                cp.wait_send()

    return pl.pallas_call(
        body, name="pair_reduce",
        in_specs=[_any(), _any()], out_specs=_vmem(),
        out_shape=jax.ShapeDtypeStruct(blk, gwt.dtype),
        scratch_shapes=[pltpu.VMEM(blk, gwt.dtype), pltpu.VMEM(blk, gwt.dtype),
                        pltpu.SemaphoreType.DMA((4, 2)), pltpu.SemaphoreType.DMA((4, 2)), pltpu.SemaphoreType.DMA((4, 2))],
        compiler_params=_cp(),
    )(gwt, tail)


def _pad_cols(a, cols):
    return jnp.pad(a, ((0, 0), (0, cols - a.shape[1])))


def _pad_rows(a, rows):
    return jnp.pad(a, ((0, rows - a.shape[0]), (0, 0)))


FRAME = 928


def _join_frames(frames):
    head = frames[:, :FRAME].at[1:, :16].add(frames[:-1, FRAME:])
    return jnp.concatenate([head.reshape(NDEV * FRAME, D), frames[NDEV - 1, FRAME:]], axis=0)


def _build_wft(wt):
    q = wt[0:1024].reshape(8, 2, 2, 32, D).transpose(0, 2, 1, 3, 4).reshape(1024, D)
    k = wt[1024:1152].reshape(2, 2, 1, 32, D)
    kd = jnp.broadcast_to(k, (2, 2, 2, 32, D)).reshape(256, D)
    v = wt[1152:1280].reshape(2, 1, 64, D)
    vd = jnp.broadcast_to(v, (2, 2, 64, D)).reshape(256, D)
    ag, bq, bk = wt[1280:2304], wt[2304:2816], wt[2816:3328]
    bv, bg, bl = wt[3328:4352], wt[4352:5376], wt[5376:5392]
    ma, mb = wt[5392:6416], wt[6416:7440]
    return jnp.concatenate([q, kd, vd, _pad_rows(bl, C_GLA - C_BL), bv, bq, bk, ag, bg, ma, mb], axis=0)


def _wft_plan():
    moves = []
    for blk in range(8):
        for half in range(2):
            for sub in range(2):
                moves.append((C_Q + 128 * blk + 32 * (2 * half + sub), 128 * blk + 32 * (2 * sub + half), 32))
    for idx in range(4):
        for dup in range(2):
            moves.append((C_KD + 64 * idx + 32 * dup, 1024 + 32 * idx, 32))
    for g in range(2):
        for dup in range(2):
            moves.append((C_VD + 128 * g + 64 * dup, 1152 + 64 * g, 64))
    moves += [(C_BL, 5376, RANK), (C_BV, 3328, 1024), (C_BQ, 2304, 512), (C_BK, 2816, 512),
              (C_AG, 1280, 1024), (C_BG, 4352, 1024), (C_MA, 5392, 1024), (C_MB, 6416, 1024)]
    bulk, seams = [], []
    for dst, src, n in moves:
        r = src
        while r < src + n:
            f = min(r // FRAME, NDEV - 1)
            local = r - FRAME * f
            if f > 0 and local < 16:
                assert local == 0
                seams.append((f, dst + r - src))
                step = 16
            else:
                step = min(src + n, FRAME * (f + 1) if f < NDEV - 1 else IN_WIDTH) - r
                bulk.append((f, local, dst + r - src, step))
            r += step
    assert sorted(f for f, _ in seams) == list(range(1, NDEV))
    return bulk, seams, [(C_BL + RANK, C_GLA - C_BL - RANK)]


def _build_wft_copies(frames):
    bulk, seams, zeros = _wft_plan()
    (z0, zn), = zeros

    def body(f_ref, o_ref, edge, zbuf, sems, esems):
        zbuf[...] = jnp.zeros_like(zbuf)
        copies = [pltpu.make_async_copy(zbuf, o_ref.at[pl.ds(z0, zn)], sems.at[0])]
        copies += [pltpu.make_async_copy(f_ref.at[f, pl.ds(l0, n)], o_ref.at[pl.ds(dst, n)], sems.at[1 + i])
                   for i, (f, l0, dst, n) in enumerate(bulk)]
        loads = []
        for i, (f, _) in enumerate(seams):
            loads.append(pltpu.make_async_copy(f_ref.at[f, pl.ds(0, 16)], edge.at[i, 0], esems.at[i, 0]))
            loads.append(pltpu.make_async_copy(f_ref.at[f - 1, pl.ds(FRAME, 16)], edge.at[i, 1], esems.at[i, 1]))
        for cp in copies + loads:
            cp.start()
        for cp in loads:
            cp.wait()
        stores = []
        for i, (_, dst) in enumerate(seams):
            edge[i, 2] = edge[i, 0] + edge[i, 1]
            stores.append(pltpu.make_async_copy(edge.at[i, 2], o_ref.at[pl.ds(dst, 16)], esems.at[i, 2]))
        for cp in stores:
            cp.start()
        for cp in copies + stores:
            cp.wait()

    return pl.pallas_call(
        body, name="build_wft",
        in_specs=[_any()], out_specs=_any(),
        out_shape=jax.ShapeDtypeStruct((NF, D), frames.dtype),
        scratch_shapes=[pltpu.VMEM((len(seams), 3, 16, D), frames.dtype), pltpu.VMEM((zn, D), frames.dtype),
                        pltpu.SemaphoreType.DMA((1 + len(bulk),)), pltpu.SemaphoreType.DMA((len(seams), 3))],
        compiler_params=_cp(),
    )(frames)


def kernel(x, positions, norm_w, w_in, a_sinks, b_gate_up, b_gate_bias, b_out_norm_w, w_a_proj, w_b_proj, w_out, final_norm_w, loss_target, m_norm_w, m_w_in, m_a_sinks, m_b_gate_up, m_b_gate_bias, m_b_out_norm_w, m_w_a_proj, m_w_b_proj, m_w_out, m_final_norm_w, v_norm_w, v_w_in, v_a_sinks, v_b_gate_up, v_b_gate_bias, v_b_out_norm_w, v_w_a_proj, v_w_b_proj, v_w_out, v_final_norm_w):
    T = x.shape[1]
    xs, target = x[0], loss_target[0]
    fnw = final_norm_w.reshape(1, D)
    me = 4 * lax.axis_index("x") + 2 * lax.axis_index("y") + lax.axis_index("c")
    allw, h, cos, sin = _gather_blocks(w_in[0].T, b_gate_up[0], xs, norm_w, positions.reshape(T, 1))
    late_blk = jnp.concatenate([w_a_proj[0], w_b_proj[0], w_out[0]], axis=0).astype(WIRE)
    l_send, l_recv, l_blk, l_land, l_started = _late_gather_start(late_blk, cos)
    wf = _build_wft_copies(allw)
    gu = allw[:, SHARD_PAD:G_ROWS, :64].transpose(1, 0, 2).reshape(RANK, 512)
    gu_pad = _pad_rows(gu, W_BL)

    proj = _proj(h, wf, l_started)
    o_a, lse = _swa_fwd(proj, cos, sin, a_sinks)
    o_b, states = _gla_fwd(proj, gu_pad, b_gate_bias)
    l_blk, l_land = _late_gather_wait(l_send, l_recv, l_blk, l_land, states, lse)
    late = lax.dynamic_update_slice(l_land, l_blk[None], (me, 0, 0))
    w_a, w_b, w_o = (late[:, 128 * i:128 * (i + 1), :].reshape(D, D) for i in range(3))
    (dx2, do_a, do_b, d_gates, g_wa, g_wb, g_wo, g_fn, g_bn, loss_part) = _mid(
        xs, target, proj, o_a, o_b, w_a, w_b, w_o, jnp.tile(b_out_norm_w, (1, B_HEADS)), fnw)
    d_q, d_kv, g_sinks = _swa_bwd(proj, cos, sin, a_sinks, do_a, o_a, lse, cos)
    d_gla, d_bl, g_gu, g_bias = _gla_bwd(proj, gu_pad, b_gate_bias, states, do_b)
    pieces = [d_q, d_kv, d_bl, d_gla, d_gates]
    offsets = [C_Q, C_KD, C_BL, C_GLA, C_GATES]

    ggu = g_gu[:RANK].reshape(RANK, NDEV, 64).transpose(1, 0, 2)
    ggu_half = [jnp.pad(ggu, ((0, 0), (0, 0), (0, DH - 64))), jnp.zeros((NDEV, RANK, DH), F32)]

    def tail(hf):
        cols = slice(hf * DH, (hf + 1) * DH)
        return jnp.concatenate([g[:, cols].reshape(NDEV, 128, DH) for g in (g_wa, g_wb, g_wo)]
                               + [ggu_half[hf]], axis=1).astype(WIRE)

    send0, recv0, s_thru0, land0, started0 = _chip_start(_pair_reduce(_gw_half(h, pieces, 0), tail(0)), 0)
    send1, recv1, s_thru1, land1, started1 = _chip_start(
        _pair_reduce(_gw_half(h, pieces, 1, after=started0), tail(1)), 1)
    grad_x, g_nw = _dh_norm(pieces, offsets, wf, xs, dx2, norm_w, started1)
    small = jnp.concatenate([g_nw, g_fn, _pad_cols(g_bias, D), _pad_cols(g_bn, D), _pad_cols(g_sinks, D),
                             _pad_cols(loss_part, D)], axis=0)
    sm_send, sm_recv, sm_blk, sm_land, sm_started = _late_gather_start(small, g_nw, name="small_gather")
    sums0, got0 = _chip_wait(send0, recv0, s_thru0, land0, sm_started, 0)
    sums1, got1 = _chip_wait(send1, recv1, s_thru1, land1, got0, 1)
    sums, from_chips = [sums0, sums1], [got0, got1]

    ws = dict(norm_w=norm_w, fnw=fnw, bias=b_gate_bias, bn=b_out_norm_w, sinks=a_sinks)
    ms = dict(norm_w=m_norm_w, fnw=m_final_norm_w.reshape(1, D), bias=m_b_gate_bias, bn=m_b_out_norm_w,
              sinks=m_a_sinks)
    vs = dict(norm_w=v_norm_w, fnw=v_final_norm_w.reshape(1, D), bias=v_b_gate_bias, bn=v_b_out_norm_w,
              sinks=v_a_sinks)
    t_rows, t_gu = _finish(
        [w_in[0].T, w_a_proj[0], w_b_proj[0], w_out[0]], [m_w_in[0].T, m_w_a_proj[0], m_w_b_proj[0], m_w_out[0]],
        [v_w_in[0].T, v_w_a_proj[0], v_w_b_proj[0], v_w_out[0]],
        b_gate_up[0], m_b_gate_up[0], v_b_gate_up[0], sums, from_chips)
    sm_blk, sm_land = _late_gather_wait(sm_send, sm_recv, sm_blk, sm_land, t_rows[0], t_gu[0], name="small_gather")
    loss, sm = _finish_small(ws, ms, vs, lax.dynamic_update_slice(sm_land, sm_blk[None], (me, 0, 0)))

    def outputs(k):
        return [sm["norm_w"][k], t_rows[k].T[None], sm["sinks"][k], t_gu[k][None], sm["bias"][k], sm["bn"][k],
                t_rows[4 + k][None], t_rows[8 + k][None], t_rows[12 + k][None], sm["fnw"][k].reshape(D)]

    return (loss[0, 0], grad_x[None], *outputs(0), *outputs(1), *outputs(2), *outputs(3))
```

```python
import functools

import numpy as np
import jax
import jax.numpy as jnp
from jax import lax
from jax.experimental import pallas as pl
from jax.experimental.pallas import tpu as pltpu

F32 = jnp.float32
MXU = jnp.bfloat16
WIRE = jnp.bfloat16

D = 1024
A_HEADS, A_KV, A_HD = 16, 2, 64
BLK = 128
B_HEADS, B_DK, B_DV = 4, 128, 256
RANK, TAU, CHUNK = 16, 16.0, 64
EPS, NEG = 1e-5, -1e30
ROPE_THETA = 10000.0
IN_WIDTH, NDEV = 7440, 8
SHARD = IN_WIDTH // NDEV
LANE = 128

C_Q, C_KD, C_VD, C_BL = 0, 1024, 1280, 1536
C_BV, C_BQ, C_BK = 2048, 3072, 3584
C_AG, C_BG, C_MA, C_MB = 4096, 5120, 6144, 7168
C_GLA, W_GLA, C_GATES, W_GATES = 2048, 2048, 4096, 4096
NF = 8192
W_BL = 128

SHARD_PAD = 944
R_IN, R_A, R_B, R_O, R_GU, ROWS = 0, 944, 1072, 1200, 1328, 1344
SMALL_ROWS = 48

ADAM_LR, ADAM_B1, ADAM_B2, ADAM_EPS, ADAM_WD, ADAM_STEP = 0.001, 0.9, 0.999, 1e-08, 0.01, 10

MESH = pl.DeviceIdType.MESH
VMEM_LIMIT = 56 * 1024 * 1024


def _cp(sem=None, **kw):
    if sem is not None:
        kw["dimension_semantics"] = sem
    return pltpu.CompilerParams(vmem_limit_bytes=VMEM_LIMIT, **kw)


def _dot(a, b):
    return jnp.dot(a, b, preferred_element_type=F32)


def _dot_nt(a, b):
    return lax.dot_general(a, b, (((1,), (1,)), ((), ())), preferred_element_type=F32)


def _dot_tn(a, b):
    return lax.dot_general(a, b, (((0,), (0,)), ((), ())), preferred_element_type=F32)


def _dot_f32(a, b):
    return jnp.dot(a, b, preferred_element_type=F32, precision=lax.Precision.HIGHEST)


def _sigmoid(z):
    return 0.5 * jnp.tanh(0.5 * z) + 0.5


def _rope(xp, cos, sin):
    return xp * cos + pltpu.roll(xp, 64, 1) * sin


def _rope_bwd(dy, cos, sin):
    return dy * cos - pltpu.roll(dy, 64, 1) * sin


def _vmem():
    return pl.BlockSpec(memory_space=pltpu.VMEM)


def _any():
    return pl.BlockSpec(memory_space=pl.ANY)


def _rope_rows():
    half = A_HD // 2
    inv = (np.float32(ROPE_THETA) ** (-np.arange(half, dtype=np.float32) / np.float32(half))).astype(np.float32)
    inv_row = jnp.asarray(np.tile(inv, 4)[None, :])
    sign_row = jnp.asarray(np.concatenate([-np.ones(64, np.float32), np.ones(64, np.float32)])[None, :])
    return inv_row, sign_row


def _prologue_rows(rows, x_ref, nw_ref, pos_ref, inv_ref, sign_ref, h_ref, cos_ref, sin_ref):
    xv = x_ref[rows, :]
    r = lax.rsqrt(jnp.mean(xv * xv, axis=-1, keepdims=True) + EPS)
    h_ref[rows, :] = ((xv * r) * nw_ref[...]).astype(h_ref.dtype)
    ang = pos_ref[rows, :].astype(F32) * inv_ref[...]
    cos_ref[rows, :] = jnp.cos(ang)
    sin_ref[rows, :] = jnp.sin(ang) * sign_ref[...]


def _proj(h, wft, after):
    T = h.shape[0]
    tT, tN = T, 512

    def body(h_ref, w_ref, after_ref, o_ref):
        o_ref[...] = _dot_nt(h_ref[...], w_ref[...])

    return pl.pallas_call(
        body, name="proj", grid=(T // tT, NF // tN),
        in_specs=[pl.BlockSpec((tT, D), lambda i, j: (i, 0)), pl.BlockSpec((tN, D), lambda i, j: (j, 0)), _any()],
        out_specs=pl.BlockSpec((tT, tN), lambda i, j: (i, j)),
        out_shape=jax.ShapeDtypeStruct((T, NF), F32),
        compiler_params=_cp(("parallel", "parallel")),
    )(h, wft, after)


def _swa_masks():
    lane = lax.broadcasted_iota(jnp.int32, (BLK, LANE), 1)
    rope_sub0 = ((lane // 32) % 2) == 0
    std_sub0 = lane < 64
    return lane, rope_sub0, std_sub0


def _swa_tri():
    qi = lax.broadcasted_iota(jnp.int32, (BLK, BLK), 0)
    kj = lax.broadcasted_iota(jnp.int32, (BLK, BLK), 1)
    return kj <= qi


def _swa_fold(full, tri):
    return jnp.where(tri, full[:, BLK:], full[:, :BLK])


def _swa_unfold(sq, tri):
    return jnp.concatenate([jnp.where(tri, 0.0, sq), jnp.where(tri, sq, 0.0)], axis=1)


def _swa_keys(kc_ref, kp_ref, vc_ref, vp_ref, cq, sq, cp, sp):
    def ropek(kref, c, s):
        kv = kref[...]
        return jnp.concatenate([_rope(kv[:, :LANE], c, s), _rope(kv[:, LANE:], c, s)], axis=1)

    K = jnp.concatenate([ropek(kp_ref, cp, sp), ropek(kc_ref, cq, sq)], axis=0).astype(MXU)
    V = jnp.concatenate([vp_ref[...], vc_ref[...]], axis=0).astype(MXU)
    return K, V


def _swa_in_specs(nb, last):
    def cur(n):
        return jnp.minimum(n, last)

    def prev(n):
        return jnp.maximum(cur(n) - 1, 0)

    kd, vd = C_KD // 256, C_VD // 256
    return [
        pl.BlockSpec((BLK, D), lambda n: (cur(n), C_Q // D)),
        pl.BlockSpec((BLK, 256), lambda n: (cur(n), kd)),
        pl.BlockSpec((BLK, 256), lambda n: (prev(n), kd)),
        pl.BlockSpec((BLK, 256), lambda n: (cur(n), vd)),
        pl.BlockSpec((BLK, 256), lambda n: (prev(n), vd)),
        pl.BlockSpec((BLK, LANE), lambda n: (cur(n), 0)),
        pl.BlockSpec((BLK, LANE), lambda n: (cur(n), 0)),
        pl.BlockSpec((BLK, LANE), lambda n: (prev(n), 0)),
        pl.BlockSpec((BLK, LANE), lambda n: (prev(n), 0)),
    ]


def _swa_fwd(proj, cos, sin, sinks):
    T = proj.shape[0]
    nb = T // BLK
    scale = A_HD ** -0.5

    def body(sinks_ref, q_ref, kc_ref, kp_ref, vc_ref, vp_ref, cq_ref, sq_ref, cp_ref, sp_ref, o_ref, l_ref):
        n = pl.program_id(0)
        cq, sq = cq_ref[...], sq_ref[...]
        K, V = _swa_keys(kc_ref, kp_ref, vc_ref, vp_ref, cq, sq, cp_ref[...], sp_ref[...])
        tri = _swa_tri()
        valid = tri | (n > 0)
        lane, rope_sub0, std_sub0 = _swa_masks()
        group = A_HEADS // A_KV
        roped, lses = {}, []

        def products(head):
            pb, sub, g = head // 2, head % 2, head // group
            if sub == 0:
                roped[pb] = _rope(q_ref[:, pb * LANE:(pb + 1) * LANE], cq, sq)
            qm = jnp.where(rope_sub0 if sub == 0 else ~rope_sub0, roped[pb], 0.0).astype(MXU)
            return _dot_nt(qm, K[:, g * LANE:(g + 1) * LANE])

        def softmax(head, s_full):
            s = jnp.where(valid, _swa_fold(s_full, tri) * scale, NEG)
            sink = sinks_ref[0, head]
            m = jnp.maximum(jnp.max(s, axis=1, keepdims=True), sink)
            e = jnp.exp(s - m)
            den = jnp.sum(e, axis=1, keepdims=True) + jnp.exp(sink - m)
            lses.append(m + jnp.log(den))
            return _swa_unfold(e / den, tri).astype(MXU)

        outs = {}
        st1 = {0: products(0), 1: products(1)}
        st2 = {0: softmax(0, st1.pop(0))}
        for head in range(A_HEADS):
            if head + 2 < A_HEADS:
                st1[head + 2] = products(head + 2)
            if head + 1 < A_HEADS:
                st2[head + 1] = softmax(head + 1, st1.pop(head + 1))
            g = head // group
            outs[head] = _dot(st2.pop(head), V[:, g * LANE:(g + 1) * LANE])
            if head % 2 == 1:
                pb = head // 2
                o_ref[:, pb * LANE:(pb + 1) * LANE] = jnp.where(std_sub0, outs[head - 1], outs[head])
        lacc = jnp.zeros((BLK, LANE), F32)
        for head in range(A_HEADS):
            lacc = jnp.where(lane == head, lses[head], lacc)
        l_ref[...] = lacc

    return pl.pallas_call(
        body, name="swa_fwd", grid=(nb,),
        in_specs=[pl.BlockSpec(memory_space=pltpu.SMEM)] + _swa_in_specs(nb, nb - 1),
        out_specs=[pl.BlockSpec((BLK, D), lambda n: (n, 0)), pl.BlockSpec((BLK, LANE), lambda n: (n, 0))],
        out_shape=[jax.ShapeDtypeStruct((T, D), F32), jax.ShapeDtypeStruct((T, LANE), F32)],
        compiler_params=_cp(("parallel",)),
    )(sinks, proj, proj, proj, proj, proj, cos, sin, cos, sin)


def _swa_bwd(proj, cos, sin, sinks, do_a, o_a, lse, after):
    T = proj.shape[0]
    nb = T // BLK
    scale = A_HD ** -0.5

    def body(sinks_ref, q_ref, kc_ref, kp_ref, vc_ref, vp_ref, cq_ref, sq_ref, cp_ref, sp_ref,
             do_ref, o_ref, l_ref, after_ref, dq_ref, dkv_ref, ds_ref, ckv_ref):
        n = pl.program_id(0)

        @pl.when(n == 0)
        def _():
            ckv_ref[...] = jnp.zeros_like(ckv_ref)
            ds_ref[...] = jnp.zeros_like(ds_ref)

        @pl.when(n < nb)
        def _():
            cq, sq, cp, sp = cq_ref[...], sq_ref[...], cp_ref[...], sp_ref[...]
            K, V = _swa_keys(kc_ref, kp_ref, vc_ref, vp_ref, cq, sq, cp, sp)
            tri = _swa_tri()
            valid = tri | (n > 0)
            lane, rope_sub0, std_sub0 = _swa_masks()
            lane_row = lax.broadcasted_iota(jnp.int32, (1, LANE), 1)
            lse_v = l_ref[...]
            dKt = [jnp.zeros((LANE, 2 * BLK), F32) for _ in range(A_KV)]
            dVt = [jnp.zeros((LANE, 2 * BLK), F32) for _ in range(A_KV)]
            dsinks, roped, roped_t, do_t = [], {}, {}, {}
            group = A_HEADS // A_KV
            dim = lax.broadcasted_iota(jnp.int32, (LANE, BLK), 0)
            rope_row0, std_row0 = ((dim // 32) % 2) == 0, dim < 64

            def products(head):
                pb, sub, g = head // 2, head % 2, head // group
                cols = slice(pb * LANE, (pb + 1) * LANE)
                Kg, Vg = K[:, g * LANE:(g + 1) * LANE], V[:, g * LANE:(g + 1) * LANE]
                if sub == 0:
                    roped[pb] = _rope(q_ref[:, cols], cq, sq)
                    roped_t[pb] = roped[pb].T
                    do_t[pb] = do_ref[:, cols].T
                qm = jnp.where(rope_sub0 if sub == 0 else ~rope_sub0, roped[pb], 0.0).astype(MXU)
                qmt = jnp.where(rope_row0 if sub == 0 else ~rope_row0, roped_t[pb], 0.0).astype(MXU)
                dov = jnp.where(std_sub0 if sub == 0 else ~std_sub0, do_ref[:, cols], 0.0)
                dovt = jnp.where(std_row0 if sub == 0 else ~std_row0, do_t[pb], 0.0).astype(MXU)
                delta = jnp.sum(dov * o_ref[:, cols], axis=1, keepdims=True)
                return qmt, dovt, delta, _dot_nt(qm, Kg), _dot_nt(dov.astype(MXU), Vg)

            def scores(head, qmt, dovt, delta, s_full, dp_full):
                lh = jnp.sum(jnp.where(lane == head, lse_v, 0.0), axis=1, keepdims=True)
                p = jnp.where(valid, jnp.exp(_swa_fold(s_full, tri) * scale - lh), 0.0)
                psink = jnp.exp(sinks_ref[0, head] - lh)
                dsinks.append(jnp.sum(-psink * delta, axis=0, keepdims=True))
                dsq = (p * (_swa_fold(dp_full, tri) - delta)) * scale
                return qmt, dovt, _swa_unfold(p, tri).astype(MXU), _swa_unfold(dsq, tri).astype(MXU)

            def grads(head, qmt, dovt, pb16, dsc):
                g = head // group
                dKt[g] = dKt[g] + _dot(qmt, dsc)
                dVt[g] = dVt[g] + _dot(dovt, pb16)
                return _dot(dsc, K[:, g * LANE:(g + 1) * LANE])

            dqs = {}
            st1 = {0: products(0), 1: products(1)}
            st2 = {0: scores(0, *st1.pop(0))}
            for head in range(A_HEADS):
                if head + 2 < A_HEADS:
                    st1[head + 2] = products(head + 2)
                if head + 1 < A_HEADS:
                    st2[head + 1] = scores(head + 1, *st1.pop(head + 1))
                dqs[head] = grads(head, *st2.pop(head))
                if head % 2 == 1:
                    pb = head // 2
                    dqp = jnp.where(rope_sub0, dqs[head - 1], dqs[head])
                    dq_ref[:, pb * LANE:(pb + 1) * LANE] = _rope_bwd(dqp, cq, sq).astype(dq_ref.dtype)
            dsink = jnp.zeros((1, LANE), F32)
            for head in range(A_HEADS):
                dsink = jnp.where(lane_row == head, dsinks[head], dsink)
            dK, dV = [a.T for a in dKt], [a.T for a in dVt]
            prev = ([_rope_bwd(dK[g][:BLK], cp, sp) for g in range(A_KV)] + [dV[g][:BLK] for g in range(A_KV)])
            cur_ = ([_rope_bwd(dK[g][BLK:], cq, sq) for g in range(A_KV)] + [dV[g][BLK:] for g in range(A_KV)])
            dkv_ref[...] = (ckv_ref[...] + jnp.concatenate(prev, axis=1)).astype(dkv_ref.dtype)
            ckv_ref[...] = jnp.concatenate(cur_, axis=1)
            ds_ref[...] = ds_ref[...] + jnp.broadcast_to(dsink, ds_ref.shape)

        @pl.when(n == nb)
        def _():
            dkv_ref[...] = ckv_ref[...].astype(dkv_ref.dtype)

    last = nb - 1

    def cur(n):
        return jnp.minimum(n, last)

    def out_kv(n):
        return (jnp.maximum(n - 1, 0), 0)

    return pl.pallas_call(
        body, name="swa_bwd", grid=(nb + 1,),
        in_specs=[pl.BlockSpec(memory_space=pltpu.SMEM)] + _swa_in_specs(nb, last) + [
            pl.BlockSpec((BLK, D), lambda n: (cur(n), 0)),
            pl.BlockSpec((BLK, D), lambda n: (cur(n), 0)),
            pl.BlockSpec((BLK, LANE), lambda n: (cur(n), 0)),
            _any(),
        ],
        out_specs=[
            pl.BlockSpec((BLK, D), lambda n: (cur(n), 0)),
            pl.BlockSpec((BLK, 512), out_kv),
            pl.BlockSpec((8, LANE), lambda n: (0, 0)),
        ],
        out_shape=[
            jax.ShapeDtypeStruct((T, D), MXU),
            jax.ShapeDtypeStruct((T, 512), MXU),
            jax.ShapeDtypeStruct((8, LANE), F32),
        ],
        scratch_shapes=[pltpu.VMEM((BLK, 512), F32)],
        compiler_params=_cp(("arbitrary",)),
    )(sinks, proj, proj, proj, proj, proj, cos, sin, cos, sin, do_a, o_a, lse, after)


GSTEP = 2 * CHUNK
ST_ROWS = B_HEADS * B_DV


def _gla_gate(bl_ref, gu_ref, bias_ref):
    gk = _dot(bl_ref[...].astype(MXU), gu_ref[...]) + bias_ref[...]
    la = (jnp.minimum(gk, 0.0) - jnp.log(1.0 + jnp.exp(-jnp.abs(gk)))) / TAU
    ri = lax.broadcasted_iota(jnp.int32, (GSTEP, GSTEP), 0)
    ci = lax.broadcasted_iota(jnp.int32, (GSTEP, GSTEP), 1)
    same = (ri // CHUNK) == (ci // CHUNK)
    lower, upper = same & (ci <= ri), same & (ci >= ri)
    b = _dot_f32(jnp.where(lower, 1.0, 0.0).astype(F32), la)
    first = lax.broadcasted_iota(jnp.int32, (GSTEP, 1), 0) < CHUNK
    return gk, la, b, lower, upper, first


def _gla_head(q_ref, k_ref, la, b, first, h):
    sl = slice(h * B_DK, (h + 1) * B_DK)
    bh, lah = b[:, sl], la[:, sl]
    bl_a = jnp.sum(lah[:CHUNK], axis=0, keepdims=True)
    bl_b = jnp.sum(lah[CHUNK:], axis=0, keepdims=True)
    blast = jnp.where(first, bl_a, bl_b)
    qc = q_ref[:, sl] * (B_DK ** -0.5)
    kh = k_ref[:, sl]
    eb, enb, esb = jnp.exp(bh), jnp.exp(-bh), jnp.exp(blast - bh)
    return qc * eb, kh * enb, kh * esb, eb, enb, esb, (jnp.exp(bl_a), jnp.exp(bl_b))


def _gla_specs(step_of):
    return [
        pl.BlockSpec((GSTEP, 512), lambda i: (step_of(i), C_BQ // 512)),
        pl.BlockSpec((GSTEP, 512), lambda i: (step_of(i), C_BK // 512)),
        pl.BlockSpec((GSTEP, D), lambda i: (step_of(i), C_BV // D)),
        pl.BlockSpec((GSTEP, W_BL), lambda i: (step_of(i), C_BL // W_BL)),
        pl.BlockSpec((W_BL, 512), lambda i: (0, 0)),
        pl.BlockSpec((1, 512), lambda i: (0, 0)),
    ]


def _gla_fwd(proj, gu_pad, bias):
    T = proj.shape[0]
    ns = T // GSTEP

    def body(q_ref, k_ref, v_ref, bl_ref, gu_ref, bias_ref, o_ref, st_ref, state_ref):
        @pl.when(pl.program_id(0) == 0)
        def _():
            state_ref[...] = jnp.zeros_like(state_ref)

        _, la, b, lower, _, first = _gla_gate(bl_ref, gu_ref, bias_ref)
        st_ref[0:ST_ROWS, :] = state_ref[...]

        def within(h):
            q_e, k_e, k_s, _, _, _, decays = _gla_head(q_ref, k_ref, la, b, first, h)
            vh = v_ref[:, h * B_DV:(h + 1) * B_DV].astype(MXU)
            q_eb = q_e.astype(MXU)
            att = jnp.where(lower, _dot_nt(q_eb, k_e.astype(MXU)), 0.0)
            return vh, q_eb, k_s.astype(MXU), _dot(att.astype(MXU), vh), decays

        def across(h, vh, q_eb, k_sb, o_intra, decays):
            rows = slice(h * B_DV, (h + 1) * B_DV)
            s0 = state_ref[rows, :]
            o_a = o_intra[:CHUNK] + _dot_nt(q_eb[:CHUNK], s0.astype(MXU))
            s1 = s0 * decays[0] + _dot_tn(vh[:CHUNK], k_sb[:CHUNK])
            st_ref[ST_ROWS + h * B_DV:ST_ROWS + (h + 1) * B_DV, :] = s1
            o_b = o_intra[CHUNK:] + _dot_nt(q_eb[CHUNK:], s1.astype(MXU))
            state_ref[rows, :] = s1 * decays[1] + _dot_tn(vh[CHUNK:], k_sb[CHUNK:])
            o_ref[:, rows] = jnp.concatenate([o_a, o_b], axis=0)

        for h in range(B_HEADS):
            across(h, *within(h))

    return pl.pallas_call(
        body, name="gla_fwd", grid=(ns,),
        in_specs=_gla_specs(lambda i: i),
        out_specs=[pl.BlockSpec((GSTEP, D), lambda i: (i, 0)),
                   pl.BlockSpec((2 * ST_ROWS, B_DK), lambda i: (i, 0))],
        out_shape=[jax.ShapeDtypeStruct((T, D), F32),
                   jax.ShapeDtypeStruct((ns * 2 * ST_ROWS, B_DK), F32)],
        scratch_shapes=[pltpu.VMEM((ST_ROWS, B_DK), F32)],
        compiler_params=_cp(("arbitrary",)),
    )(proj, proj, proj, proj, gu_pad, bias)


def _gla_bwd(proj, gu_pad, bias, states, do_b):
    T = proj.shape[0]
    ns = T // GSTEP
    o_q, o_k = C_BQ - C_GLA, C_BK - C_GLA

    def body(q_ref, k_ref, v_ref, bl_ref, gu_ref, bias_ref, st_ref, do_ref,
             dg_ref, dbl_ref, ggu_ref, gbias_ref, gt_ref):
        @pl.when(pl.program_id(0) == 0)
        def _():
            gt_ref[...] = jnp.zeros_like(gt_ref)
            ggu_ref[...] = jnp.zeros_like(ggu_ref)
            gbias_ref[...] = jnp.zeros_like(gbias_ref)

        gk, la, b, lower, upper_mask, first = _gla_gate(bl_ref, gu_ref, bias_ref)
        upper = jnp.where(upper_mask, 1.0, 0.0).astype(F32)
        lo, hi = slice(0, CHUNK), slice(CHUNK, GSTEP)
        dla_parts = []

        def within(h):
            q_e, k_e, k_s, eb, enb, esb, decays = _gla_head(q_ref, k_ref, la, b, first, h)
            vh = v_ref[:, h * B_DV:(h + 1) * B_DV].astype(MXU)
            doh = do_ref[:, h * B_DV:(h + 1) * B_DV].astype(MXU)
            q_eb, k_eb = q_e.astype(MXU), k_e.astype(MXU)
            att = jnp.where(lower, _dot_nt(q_eb, k_eb), 0.0).astype(MXU)
            datt = jnp.where(lower, _dot_nt(doh, vh), 0.0).astype(MXU)
            return (q_e, k_e, k_s, eb, enb, esb, decays, vh, doh, q_eb, k_s.astype(MXU),
                    _dot(datt, k_eb), _dot_tn(datt, q_eb), _dot_tn(att, doh))

        def across(h, q_e, k_e, k_s, eb, enb, esb, decays, vh, doh, q_eb, k_sb, dq_i, dk_e, dv_i):
            dec_a, dec_b = decays
            rows = slice(h * B_DV, (h + 1) * B_DV)
            s0 = st_ref[rows, :]
            s1 = st_ref[ST_ROWS + h * B_DV:ST_ROWS + (h + 1) * B_DV, :]
            g2 = gt_ref[rows, :]
            g2b = g2.astype(MXU)
            dq_b = dq_i[hi] + _dot(doh[hi], s1.astype(MXU))
            dks_b = _dot(vh[hi], g2b)
            dv_b = dv_i[hi] + _dot_nt(k_sb[hi], g2b)
            ddec_b = jnp.sum(g2 * s1, axis=0, keepdims=True)
            g1 = g2 * dec_b + _dot_tn(doh[hi], q_eb[hi])
            g1b = g1.astype(MXU)
            dq_a = dq_i[lo] + _dot(doh[lo], s0.astype(MXU))
            dks_a = _dot(vh[lo], g1b)
            dv_a = dv_i[lo] + _dot_nt(k_sb[lo], g1b)
            ddec_a = jnp.sum(g1 * s0, axis=0, keepdims=True)
            gt_ref[rows, :] = g1 * dec_a + _dot_tn(doh[lo], q_eb[lo])
            dq_e = jnp.concatenate([dq_a, dq_b], axis=0)
            dk_s = jnp.concatenate([dks_a, dks_b], axis=0)
            dg_ref[:, rows] = jnp.concatenate([dv_a, dv_b], axis=0).astype(dg_ref.dtype)
            dg_ref[:, o_q + h * B_DK:o_q + (h + 1) * B_DK] = (dq_e * eb * (B_DK ** -0.5)).astype(dg_ref.dtype)
            dg_ref[:, o_k + h * B_DK:o_k + (h + 1) * B_DK] = (dk_e * enb + dk_s * esb).astype(dg_ref.dtype)
            dks_ks = dk_s * k_s
            db = dq_e * q_e - dk_e * k_e - dks_ks
            dbl_a = jnp.sum(dks_ks[lo], axis=0, keepdims=True) + ddec_a * dec_a
            dbl_b = jnp.sum(dks_ks[hi], axis=0, keepdims=True) + ddec_b * dec_b
            dla_parts.append(_dot_f32(upper, db) + jnp.where(first, dbl_a, dbl_b))

        for h in range(B_HEADS):
            across(h, *within(h))
        dla = jnp.concatenate(dla_parts, axis=1)
        dgk = dla * (1.0 / TAU) * _sigmoid(-gk)
        dgkb = dgk.astype(MXU)
        dbl_ref[...] = _dot_nt(dgkb, gu_ref[...]).astype(dbl_ref.dtype)
        ggu_ref[...] = ggu_ref[...] + _dot_tn(bl_ref[...].astype(MXU), dgkb)
        gbias_ref[...] = gbias_ref[...] + jnp.broadcast_to(jnp.sum(dgk, axis=0, keepdims=True), gbias_ref.shape)

    def rev(i):
        return ns - 1 - i

    return pl.pallas_call(
        body, name="gla_bwd", grid=(ns,),
        in_specs=_gla_specs(rev) + [
            pl.BlockSpec((2 * ST_ROWS, B_DK), lambda i: (rev(i), 0)),
            pl.BlockSpec((GSTEP, D), lambda i: (rev(i), 0)),
        ],
        out_specs=[
            pl.BlockSpec((GSTEP, W_GLA), lambda i: (rev(i), 0)),
            pl.BlockSpec((GSTEP, W_BL), lambda i: (rev(i), 0)),
            pl.BlockSpec((W_BL, 512), lambda i: (0, 0)),
            pl.BlockSpec((8, 512), lambda i: (0, 0)),
        ],
        out_shape=[
            jax.ShapeDtypeStruct((T, W_GLA), MXU),
            jax.ShapeDtypeStruct((T, W_BL), MXU),
            jax.ShapeDtypeStruct((W_BL, 512), F32),
            jax.ShapeDtypeStruct((8, 512), F32),
        ],
        scratch_shapes=[pltpu.VMEM((B_HEADS * B_DV, B_DK), F32)],
        compiler_params=_cp(("arbitrary",)),
    )(proj, proj, proj, proj, gu_pad, bias, states, do_b)


def _mid(x, target, proj, o_a, o_b, w_a, w_b, w_out, w_bn4, fnw):
    T = x.shape[0]
    tT = min(T, 128)
    nbuf = 4
    o_ag, o_bg, o_ma, o_mb = (c - C_GATES for c in (C_AG, C_BG, C_MA, C_MB))

    def body(x_ref, t_ref, oa_ref, ob_ref, gates_ref, wa_ref, wb_ref, wo_ref, wbn_ref, fnw_ref,
             dx2_ref, doa_ref, dob_ref, dgates_ref,
             gwa_ref, gwb_ref, gwo_ref, gfn_ref, gbn_ref, loss_ref, buf_ref):
        i = pl.program_id(0)

        @pl.when(i == 0)
        def _():
            for r in (gwa_ref, gwb_ref, gwo_ref, gfn_ref, gbn_ref, loss_ref):
                r[...] = jnp.zeros_like(r)

        rows = pl.ds(pl.multiple_of((i % nbuf) * tT, tT), tT)

        def keep(k, val):
            buf_ref[k, rows, :] = val

        oa, ag = oa_ref[...], gates_ref[:, o_ag:o_ag + D]
        sg_a = _sigmoid(ag)
        silu_a = ag * sg_a
        oag_b = (oa * silu_a).astype(MXU)
        keep(0, oag_b)
        y_a = _dot(oag_b, wa_ref[...])

        ob, bg = ob_ref[...], gates_ref[:, o_bg:o_bg + D]
        rbs, obhats = [], []
        for h in range(B_HEADS):
            obh = ob[:, h * B_DV:(h + 1) * B_DV]
            rb = lax.rsqrt(jnp.mean(obh * obh, axis=-1, keepdims=True) + EPS)
            rbs.append(rb)
            obhats.append(obh * rb)
        obhat = jnp.concatenate(obhats, axis=1)
        wbn = wbn_ref[...]
        obn = obhat * wbn
        sg_b = _sigmoid(bg)
        silu_b = bg * sg_b
        obg_b = (obn * silu_b).astype(MXU)
        keep(1, obg_b)
        y_b = _dot(obg_b, wb_ref[...])

        sa, sb = _sigmoid(gates_ref[:, o_ma:o_ma + D]), _sigmoid(gates_ref[:, o_mb:o_mb + D])
        mg_b = (sa * y_a + sb * y_b).astype(MXU)
        keep(2, mg_b)
        x2 = x_ref[...] + _dot(mg_b, wo_ref[...])
        r2 = lax.rsqrt(jnp.mean(x2 * x2, axis=-1, keepdims=True) + EPS)
        xh2 = x2 * r2
        fw = fnw_ref[...]
        err = xh2 * fw - t_ref[...]
        tok = jnp.mean(err * err, axis=-1, keepdims=True)
        loss_ref[...] = loss_ref[...] + 0.5 * jnp.sum(tok, axis=0, keepdims=True)

        dy = err * (1.0 / D)
        gfn_ref[...] = gfn_ref[...] + jnp.broadcast_to(jnp.sum(dy * xh2, axis=0, keepdims=True), gfn_ref.shape)
        gy = dy * fw
        dx2 = r2 * (gy - xh2 * jnp.mean(gy * xh2, axis=-1, keepdims=True))
        dx2_ref[...] = dx2
        dx2_b = dx2.astype(MXU)
        keep(5, dx2_b)
        dmg = _dot_nt(dx2_b, wo_ref[...])

        dgates_ref[:, o_ma:o_ma + D] = (dmg * y_a * sa * (1.0 - sa)).astype(dgates_ref.dtype)
        dgates_ref[:, o_mb:o_mb + D] = (dmg * y_b * sb * (1.0 - sb)).astype(dgates_ref.dtype)
        dya_b = (dmg * sa).astype(MXU)
        dyb_b = (dmg * sb).astype(MXU)
        keep(3, dya_b)
        keep(4, dyb_b)
        doag = _dot_nt(dya_b, wa_ref[...])
        dobg = _dot_nt(dyb_b, wb_ref[...])

        @pl.when(i % nbuf == nbuf - 1)
        def _():
            gwa_ref[...] = gwa_ref[...] + _dot_tn(buf_ref[0], buf_ref[3])
            gwb_ref[...] = gwb_ref[...] + _dot_tn(buf_ref[1], buf_ref[4])
            gwo_ref[...] = gwo_ref[...] + _dot_tn(buf_ref[2], buf_ref[5])

        doa_ref[...] = doag * silu_a
        dgates_ref[:, o_ag:o_ag + D] = (doag * oa * (sg_a * (1.0 + ag * (1.0 - sg_a)))).astype(dgates_ref.dtype)
        dobn = dobg * silu_b
        dgates_ref[:, o_bg:o_bg + D] = (dobg * obn * (sg_b * (1.0 + bg * (1.0 - sg_b)))).astype(dgates_ref.dtype)
        gg = dobn * wbn
        gbn = jnp.zeros((1, B_DV), F32)
        for h in range(B_HEADS):
            sl = slice(h * B_DV, (h + 1) * B_DV)
            gbn = gbn + jnp.sum(dobn[:, sl] * obhats[h], axis=0, keepdims=True)
            ggh = gg[:, sl]
            dob_ref[:, sl] = rbs[h] * (ggh - obhats[h] * jnp.mean(ggh * obhats[h], axis=-1, keepdims=True))
        gbn_ref[...] = gbn_ref[...] + jnp.broadcast_to(gbn, gbn_ref.shape)

    assert (T // tT) % nbuf == 0
    tile = pl.BlockSpec((tT, D), lambda i: (i, 0))
    row = pl.BlockSpec((1, D), lambda i: (0, 0))
    acc8 = pl.BlockSpec((8, D), lambda i: (0, 0))
    return pl.pallas_call(
        body, name="mid", grid=(T // tT,),
        in_specs=[tile, tile, tile, tile, pl.BlockSpec((tT, W_GATES), lambda i: (i, C_GATES // W_GATES)),
                  _vmem(), _vmem(), _vmem(), row, row],
        out_specs=[tile, tile, tile, pl.BlockSpec((tT, W_GATES), lambda i: (i, 0)), _vmem(), _vmem(), _vmem(),
                   acc8, pl.BlockSpec((8, B_DV), lambda i: (0, 0)), pl.BlockSpec((8, LANE), lambda i: (0, 0))],
        out_shape=[
            jax.ShapeDtypeStruct((T, D), F32),
            jax.ShapeDtypeStruct((T, D), F32),
            jax.ShapeDtypeStruct((T, D), F32),
            jax.ShapeDtypeStruct((T, W_GATES), MXU),
            jax.ShapeDtypeStruct((D, D), F32),
            jax.ShapeDtypeStruct((D, D), F32),
            jax.ShapeDtypeStruct((D, D), F32),
            jax.ShapeDtypeStruct((8, D), F32),
            jax.ShapeDtypeStruct((8, B_DV), F32),
            jax.ShapeDtypeStruct((8, LANE), F32),
        ],
        scratch_shapes=[pltpu.VMEM((6, nbuf * tT, D), MXU)],
        compiler_params=_cp(("arbitrary",)),
    )(x, target, o_a, o_b, proj, w_a, w_b, w_out, w_bn4, fnw)


DH = D // 2


_GW_TILES = (("q", 0, 512, 0), ("q", 1, 512, 512), ("kv", 0, 256, 1024), ("bl", 0, RANK, 5376),
             ("gla", 0, 512, 3328), ("gla", 1, 512, 3840), ("gla", 2, 512, 2304), ("gla", 3, 512, 2816),
             ("gates", 0, 512, 1280), ("gates", 1, 512, 1792), ("gates", 2, 512, 4352), ("gates", 3, 512, 4864),
             ("gates", 4, 512, 5392), ("gates", 5, 512, 5904), ("gates", 6, 512, 6416), ("gates", 7, 512, 6928))


def _gw_unpermute(piece, t):
    if piece == "q":
        parts = []
        for blk in range(t.shape[0] // LANE):
            g = [t[blk * LANE + 32 * i:blk * LANE + 32 * (i + 1)] for i in range(4)]
            parts += [g[0], g[2], g[1], g[3]]
        return jnp.concatenate(parts, axis=0)
    if piece == "kv":
        k = [t[64 * i:64 * i + 32] + t[64 * i + 32:64 * i + 64] for i in range(4)]
        v = [t[256 + 128 * g:256 + 128 * g + 64] + t[256 + 128 * g + 64:256 + 128 * (g + 1)] for g in range(2)]
        return jnp.concatenate(k + v, axis=0)
    if piece == "bl":
        return t[:RANK]
    return t


def _gw_half(h, pieces, half, after=None):
    T = h.shape[0]
    steps = len(_GW_TILES)

    def body(*refs):
        h_ref = refs[0]
        srcs = dict(zip(("q", "kv", "bl", "gla", "gates"), refs[1:6]))
        o_ref, stage, sems = refs[-3:]
        j = pl.program_id(0)

        def out_copy(k):
            _, _, n, off = _GW_TILES[k]
            return pltpu.make_async_copy(stage.at[k % 2, 0:n], o_ref.at[pl.ds(off, n)], sems.at[k % 2])

        for k, (piece, _, n, _) in enumerate(_GW_TILES):
            @pl.when(j == k)
            def _(k=k, piece=piece, n=n):
                if k >= 2:
                    out_copy(k - 2).wait()
                t = _gw_unpermute(piece, _dot_tn(srcs[piece][...], h_ref[...]))
                stage[k % 2, 0:n, :] = t.astype(stage.dtype)
                out_copy(k).start()

        @pl.when(j == steps - 1)
        def _():
            out_copy(steps - 2).wait()
            out_copy(steps - 1).wait()

    def tile_of(lo, hi):
        return lambda j: (0, jnp.clip(j - lo, 0, hi - lo - 1))

    in_specs = [pl.BlockSpec((T, DH), lambda j: (0, half)),
                pl.BlockSpec((T, 512), tile_of(0, 2)), pl.BlockSpec((T, 512), lambda j: (0, 0)),
                pl.BlockSpec((T, W_BL), lambda j: (0, 0)),
                pl.BlockSpec((T, 512), tile_of(4, 8)), pl.BlockSpec((T, 512), tile_of(8, 16))]
    args = [h, *pieces]
    if after is not None:
        in_specs.append(_any())
        args.append(after)
    return pl.pallas_call(
        body, name=f"gw_in_half{half}", grid=(steps,),
        in_specs=in_specs, out_specs=_any(),
        out_shape=jax.ShapeDtypeStruct((IN_WIDTH, DH), WIRE),
        scratch_shapes=[pltpu.VMEM((2, 512, DH), WIRE), pltpu.SemaphoreType.DMA((2,))],
        compiler_params=_cp(("arbitrary",)),
    )(*args)


def _chip_copies(s_ref, got_ref, send_sems, recv_sems):
    x, y, c = _place()
    chips = [(1 - x, y), (x, 1 - y), (1 - x, 1 - y)]
    return [pltpu.make_async_remote_copy(
        src_ref=s_ref.at[2 * px + py], dst_ref=got_ref.at[j],
        send_sem=send_sems.at[j], recv_sem=recv_sems.at[j], device_id=(px, py, c), device_id_type=MESH)
        for j, (px, py) in enumerate(chips)]


_EFFECT = pltpu.SideEffectType.DATAFLOW_SIDE_EFFECTING


def _hbm():
    return pl.BlockSpec(memory_space=pltpu.HBM)


def _sem():
    return pl.BlockSpec(memory_space=pltpu.SEMAPHORE)


def _chip_start(sums, half):
    land = pltpu.with_memory_space_constraint(lax.empty((3,) + sums.shape[1:], sums.dtype), pltpu.HBM)

    def body(s_ref, land_ref, send_sems, recv_sems, s_thru, land_thru, token):
        for cp in _chip_copies(s_ref, land_ref, send_sems, recv_sems):
            cp.start()
        token[...] = jnp.zeros_like(token)

    return pl.pallas_call(
        body, name=f"chip_start{half}",
        out_shape=(pltpu.SemaphoreType.DMA((3,)), pltpu.SemaphoreType.DMA((3,)),
                   pltpu.HBM(sums.shape, sums.dtype), pltpu.HBM(land.shape, land.dtype),
                   jax.ShapeDtypeStruct((8, LANE), F32)),
        in_specs=(_hbm(), _hbm()), out_specs=(_sem(), _sem(), _hbm(), _hbm(), _vmem()),
        input_output_aliases={0: 2, 1: 3},
        compiler_params=pltpu.CompilerParams(has_side_effects=_EFFECT),
    )(pltpu.with_memory_space_constraint(sums, pltpu.HBM), land)


def _chip_wait(send_sems, recv_sems, s_thru, land_thru, after, half):
    def body(s_ref, land_ref, send_sems, recv_sems, after_ref, s_out, got_ref):
        copies = _chip_copies(s_ref, land_ref, send_sems, recv_sems)
        for cp in copies:
            cp.wait_send()
        for cp in copies:
            cp.wait_recv()

    return pl.pallas_call(
        body, name=f"chip_wait{half}",
        out_shape=(pltpu.HBM(s_thru.shape, s_thru.dtype), pltpu.HBM(land_thru.shape, land_thru.dtype)),
        in_specs=(_hbm(), _hbm(), _sem(), _sem(), _any()), out_specs=(_hbm(), _hbm()),
        input_output_aliases={0: 0, 1: 1},
        compiler_params=pltpu.CompilerParams(has_side_effects=_EFFECT),
    )(s_thru, land_thru, send_sems, recv_sems, after)


def _dh_norm(pieces, offsets, wf, x, dx2, norm_w, after):
    T = x.shape[0]
    tT = min(T, 256)
    widths = [p.shape[1] for p in pieces]
    npc = len(pieces)

    def body(*refs):
        dp_refs = refs[:npc]
        wf_ref, x_ref, dx2_ref, nw_ref, _, gx_ref, gnw_ref = refs[npc:]

        @pl.when(pl.program_id(0) == 0)
        def _():
            gnw_ref[...] = jnp.zeros_like(gnw_ref)

        dh = jnp.zeros((tT, D), F32)
        for dp_ref, off, w in zip(dp_refs, offsets, widths):
            dh = dh + _dot(dp_ref[...], wf_ref[off:off + w, :])
        xv = x_ref[...]
        r = lax.rsqrt(jnp.mean(xv * xv, axis=-1, keepdims=True) + EPS)
        xh = xv * r
        gnw_ref[...] = gnw_ref[...] + jnp.broadcast_to(jnp.sum(dh * xh, axis=0, keepdims=True), gnw_ref.shape)
        g = dh * nw_ref[...]
        gx_ref[...] = r * (g - xh * jnp.mean(g * xh, axis=-1, keepdims=True)) + dx2_ref[...]

    tile = pl.BlockSpec((tT, D), lambda i: (i, 0))
    return pl.pallas_call(
        body, name="dh_norm", grid=(T // tT,),
        in_specs=[pl.BlockSpec((tT, w), lambda i: (i, 0)) for w in widths]
        + [_vmem(), tile, tile, pl.BlockSpec((1, D), lambda i: (0, 0)), _any()],
        out_specs=[tile, pl.BlockSpec((8, D), lambda i: (0, 0))],
        out_shape=[jax.ShapeDtypeStruct((T, D), F32), jax.ShapeDtypeStruct((8, D), F32)],
        compiler_params=_cp(("arbitrary",)),
    )(*pieces, wf, x, dx2, norm_w, after)


def _adamw_math(w, g, m, v):
    m = ADAM_B1 * m + (1.0 - ADAM_B1) * g
    v = ADAM_B2 * v + (1.0 - ADAM_B2) * (g * g)
    m_hat = m / (1.0 - ADAM_B1 ** ADAM_STEP)
    v_hat = v / (1.0 - ADAM_B2 ** ADAM_STEP)
    delta = -ADAM_LR * (m_hat / (jnp.sqrt(v_hat) + ADAM_EPS) + ADAM_WD * w)
    return delta, m, v


def _fetch_partials(s_ref, got_ref, buf, sems):
    x, y, _ = _place()
    cps = [pltpu.make_async_copy(s_ref.at[2 * x + y], buf.at[0], sems.at[0])]
    cps += [pltpu.make_async_copy(got_ref.at[j], buf.at[1 + j], sems.at[1 + j]) for j in range(3)]
    for cp in cps:
        cp.start()
    for cp in cps:
        cp.wait()


SMALL_AT = dict(norm_w=0, fnw=8, bias=16, bn=24, sinks=32, loss=40)
ROW_AT = (R_IN, R_A, R_B, R_O)


def _finish_small(ws, ms, vs, smalls):
    names = ["norm_w", "fnw", "bias", "bn", "sinks"]
    widths = [ws[n].shape[1] for n in names]

    def body(*refs):
        w_refs, m_refs, v_refs = refs[0:5], refs[5:10], refs[10:15]
        smalls_ref, loss_ref = refs[15], refs[16]
        outs, tot = refs[17:37], refs[37]
        acc = smalls_ref[0]
        for d in range(1, NDEV):
            acc = acc + smalls_ref[d]
        tot[...] = acc
        loss_ref[...] = tot[SMALL_AT["loss"]:SMALL_AT["loss"] + 1, 0:1]
        for p, (nm_, wd) in enumerate(zip(names, widths)):
            r = SMALL_AT[nm_]
            g = tot[r:r + 1, 0:wd]
            d, nm, nv = _adamw_math(w_refs[p][...], g, m_refs[p][...], v_refs[p][...])
            for o, val in zip(outs[4 * p:4 * p + 4], (g, d, nm, nv)):
                o[...] = val

    res = pl.pallas_call(
        body, name="finish_small",
        in_specs=[_vmem()] * 16, out_specs=[_vmem()] * 21,
        out_shape=[jax.ShapeDtypeStruct((1, 1), F32)]
        + [jax.ShapeDtypeStruct((1, wd), F32) for wd in widths for _ in range(4)],
        scratch_shapes=[pltpu.VMEM((SMALL_ROWS, D), F32)],
        compiler_params=_cp(),
    )(*[ws[n] for n in names], *[ms[n] for n in names], *[vs[n] for n in names], smalls)
    return res[0], {n: tuple(res[1 + 4 * p:5 + 4 * p]) for p, n in enumerate(names)}


def _finish(w_rows, m_rows, v_rows, gu_w, gu_m, gu_v, sums, got):
    shapes = [w.shape for w in w_rows]

    def body(*refs):
        wr_refs, mr_refs, vr_refs = refs[0:4], refs[4:8], refs[8:12]
        guw_ref, gum_ref, guv_ref = refs[12:15]
        s_refs, got_refs = refs[15:17], refs[17:19]
        row_outs = refs[19:35]
        gu_outs = refs[35:39]
        buf, gsh, sems = refs[39:]
        x, y, c = _place()
        me_slot = 4 * x + 2 * y + c
        unshift = lax.rem(SHARD_PAD - 2 * me_slot, SHARD_PAD)
        for hf in range(2):
            _fetch_partials(s_refs[hf], got_refs[hf], buf, sems)
            for p in range(4):
                n, off = shapes[p][0], ROW_AT[p]
                nf = SHARD_PAD if p == 0 else n
                for cc in range(DH // LANE):
                    src = slice(cc * LANE, (cc + 1) * LANE)
                    cols = slice(hf * DH + cc * LANE, hf * DH + (cc + 1) * LANE)
                    g = buf[0, off:off + nf, src].astype(F32)
                    for j in range(1, 4):
                        g = g + buf[j, off:off + nf, src].astype(F32)
                    if p == 0:
                        gsh[...] = pltpu.roll(g, unshift, 0)
                        g = gsh[0:n, :]
                    d, nm, nv = _adamw_math(wr_refs[p][:, cols], g, mr_refs[p][:, cols], vr_refs[p][:, cols])
                    for o, val in zip(row_outs[4 * p:4 * p + 4], (g, d, nm, nv)):
                        o[:, cols] = val
            if hf == 0:
                g = buf[0, R_GU:R_GU + RANK, 0:64].astype(F32)
                for j in range(1, 4):
                    g = g + buf[j, R_GU:R_GU + RANK, 0:64].astype(F32)
                d, nm, nv = _adamw_math(guw_ref[...], g, gum_ref[...], guv_ref[...])
                for o, val in zip(gu_outs, (g, d, nm, nv)):
                    o[...] = val

    res = pl.pallas_call(
        body, name="finish",
        in_specs=[_vmem()] * 15 + [_any()] * 4,
        out_specs=[_vmem()] * 20,
        out_shape=[jax.ShapeDtypeStruct(s, F32) for s in shapes for _ in range(4)]
        + [jax.ShapeDtypeStruct((RANK, 64), F32)] * 4,
        scratch_shapes=[pltpu.VMEM((4, ROWS, DH), sums[0].dtype), pltpu.VMEM((SHARD_PAD, LANE), F32),
                        pltpu.SemaphoreType.DMA((4,))],
        compiler_params=_cp(),
    )(*w_rows, *m_rows, *v_rows, gu_w, gu_m, gu_v, *sums, *got)
    return tuple(res[0:16]), tuple(res[16:20])


def _place():
    x, y, c = lax.axis_index("x"), lax.axis_index("y"), lax.axis_index("c")
    return x, y, c


def _peers(x, y, c):
    return [(x ^ dx, y ^ dy, c ^ dc) for dx in range(2) for dy in range(2) for dc in range(2) if dx + dy + dc]


def _late_gather_start(blk, after, name="late_gather"):
    land = pltpu.with_memory_space_constraint(lax.empty((NDEV,) + blk.shape, blk.dtype), pltpu.HBM)

    def body(b_ref, land_ref, after_ref, send_sems, recv_sems, b_thru, land_thru, token):
        x, y, c = _place()
        for k, to in enumerate(_peers(x, y, c)):
            pltpu.make_async_remote_copy(
                src_ref=b_ref, dst_ref=land_ref.at[4 * x + 2 * y + c], send_sem=send_sems.at[k],
                recv_sem=recv_sems.at[k], device_id=to, device_id_type=MESH).start()
        token[...] = jnp.zeros_like(token)

    return pl.pallas_call(
        body, name=name + "_start",
        out_shape=(pltpu.SemaphoreType.DMA((7,)), pltpu.SemaphoreType.DMA((7,)),
                   pltpu.HBM(blk.shape, blk.dtype), pltpu.HBM(land.shape, land.dtype),
                   jax.ShapeDtypeStruct((8, LANE), F32)),
        in_specs=(_hbm(), _hbm(), _any()), out_specs=(_sem(), _sem(), _hbm(), _hbm(), _vmem()),
        input_output_aliases={0: 2, 1: 3},
        compiler_params=pltpu.CompilerParams(has_side_effects=_EFFECT),
    )(pltpu.with_memory_space_constraint(blk, pltpu.HBM), land, after)


def _late_gather_wait(send_sems, recv_sems, b_thru, land_thru, after, after2, name="late_gather"):
    def body(b_ref, land_ref, send_sems, recv_sems, after_ref, after2_ref, b_out, got_ref):
        x, y, c = _place()
        copies = [pltpu.make_async_remote_copy(
            src_ref=b_ref, dst_ref=land_ref.at[4 * x + 2 * y + c], send_sem=send_sems.at[k],
            recv_sem=recv_sems.at[k], device_id=to, device_id_type=MESH)
            for k, to in enumerate(_peers(x, y, c))]
        for cp in copies:
            cp.wait_send()
        for cp in copies:
            cp.wait_recv()

    return pl.pallas_call(
        body, name=name + "_wait",
        out_shape=(pltpu.HBM(b_thru.shape, b_thru.dtype), pltpu.HBM(land_thru.shape, land_thru.dtype)),
        in_specs=(_hbm(), _hbm(), _sem(), _sem(), _any(), _any()), out_specs=(_hbm(), _hbm()),
        input_output_aliases={0: 0, 1: 1},
        compiler_params=pltpu.CompilerParams(has_side_effects=_EFFECT),
    )(b_thru, land_thru, send_sems, recv_sems, after, after2)


G_ROWS = SHARD_PAD + RANK


def _gather_blocks(w_in_t, gu_s, xs, norm_w, pos_col):
    rows, cols = G_ROWS, D
    T = xs.shape[0]
    tT = min(T, 256)
    inv_row, sign_row = _rope_rows()

    def body(wi_ref, gu_ref, xs_ref, nw_ref, pos_ref, inv_ref, sign_ref,
             out_ref, h_ref, cos_ref, sin_ref, x_ref, frame_ref, send_sems, recv_sems, local_sem):
        x, y, c = _place()
        me, sibling = (x, y, c), (x, y, 1 - c)
        chips = [(1 - x, y), (x, 1 - y), (1 - x, 1 - y)]
        shift = 2 * (4 * x + 2 * y + c)
        frame_ref[SHARD - SHARD % 8:, :] = jnp.zeros((SHARD_PAD - SHARD + SHARD % 8, D), F32)
        frame_ref[:SHARD, :] = wi_ref[...]
        for cc in range(D // LANE):
            cs = slice(cc * LANE, (cc + 1) * LANE)
            x_ref[0:SHARD_PAD, cs] = pltpu.roll(frame_ref[:, cs], shift, 0).astype(x_ref.dtype)
        x_ref[SHARD_PAD:G_ROWS, :] = jnp.zeros((RANK, D), x_ref.dtype)
        x_ref[SHARD_PAD:G_ROWS, 0:64] = gu_ref[...].astype(x_ref.dtype)

        def slot(px, py, pc):
            return out_ref.at[4 * px + 2 * py + pc]

        def copy(k, block, to, src=None):
            return pltpu.make_async_remote_copy(
                src_ref=slot(*block) if src is None else src, dst_ref=slot(*block),
                send_sem=send_sems.at[k], recv_sem=recv_sems.at[k], device_id=to, device_id_type=MESH)

        mine = pltpu.make_async_copy(x_ref, slot(*me), local_sem)
        mine.start()
        first = [copy(0, me, sibling, src=x_ref)]
        first += [copy(1 + j, me, (*chip, c), src=x_ref) for j, chip in enumerate(chips)]
        for cp in first:
            cp.start()

        @pl.loop(0, T // tT)
        def _(i):
            rows_i = pl.ds(pl.multiple_of(i * tT, tT), tT)
            _prologue_rows(rows_i, xs_ref, nw_ref, pos_ref, inv_ref, sign_ref, h_ref, cos_ref, sin_ref)

        passed = [copy(4 + j, (*chip, c), sibling) for j, chip in enumerate(chips)]
        for j, chip in enumerate(chips):
            copy(1 + j, (*chip, c), me).wait_recv()
            passed[j].start()
        copy(0, sibling, me).wait_recv()
        for j, chip in enumerate(chips):
            copy(4 + j, (*chip, 1 - c), me).wait_recv()
        for cp in first + passed:
            cp.wait_send()
        mine.wait()

    return pl.pallas_call(
        body, name="gather_weights",
        in_specs=[_vmem()] * 7, out_specs=[_any()] + [_vmem()] * 3,
        out_shape=[jax.ShapeDtypeStruct((NDEV, rows, cols), WIRE), jax.ShapeDtypeStruct((T, D), MXU),
                   jax.ShapeDtypeStruct((T, LANE), F32), jax.ShapeDtypeStruct((T, LANE), F32)],
        scratch_shapes=[pltpu.VMEM((rows, cols), WIRE), pltpu.VMEM((SHARD_PAD, D), F32),
                        pltpu.SemaphoreType.DMA((7,)), pltpu.SemaphoreType.DMA((7,)), pltpu.SemaphoreType.DMA],
        compiler_params=_cp(),
    )(w_in_t, gu_s, xs, norm_w, pos_col, inv_row, sign_row)


def _pair_reduce(gwt, tail):
    n = gwt.shape[1]
    blk = (4, ROWS, n)

    def body(g_ref, t_ref, out_ref, got, own, send_sems, recv_sems, own_sems):
        x, y, c = _place()

        def parts(d, dst):
            frame = g_ref.at[pl.ds(pl.multiple_of(FRAME * d, 16), SHARD_PAD)]
            return [(frame, dst.at[0:SHARD_PAD]), (t_ref.at[d], dst.at[SHARD_PAD:ROWS])]

        sends, loads = [], []
        for chip in range(4):
            sends.append([pltpu.make_async_remote_copy(
                src_ref=s, dst_ref=d_, send_sem=send_sems.at[chip, k], recv_sem=recv_sems.at[chip, k],
                device_id=(x, y, 1 - c), device_id_type=MESH)
                for k, (s, d_) in enumerate(parts(2 * chip + (1 - c), got.at[chip]))])
            loads.append([pltpu.make_async_copy(s, d_, own_sems.at[chip, k])
                          for k, (s, d_) in enumerate(parts(2 * chip + c, own.at[chip]))])
        for group in sends + loads:
            for cp in group:
                cp.start()
        for chip in range(4):
            for cp in loads[chip]:
                cp.wait()
            for cp in sends[chip]:
                cp.wait_recv()
            out_ref[chip] = (own[chip].astype(F32) + got[chip].astype(F32)).astype(out_ref.dtype)
        for group in sends:
            for cp in group:
                cp.wait_send()

    return pl.pallas_call(
        body, name="pair_reduce",
        in_specs=[_any(), _any()], out_specs=_vmem(),
        out_shape=jax.ShapeDtypeStruct(blk, gwt.dtype),
        scratch_shapes=[pltpu.VMEM(blk, gwt.dtype), pltpu.VMEM(blk, gwt.dtype),
                        pltpu.SemaphoreType.DMA((4, 2)), pltpu.SemaphoreType.DMA((4, 2)), pltpu.SemaphoreType.DMA((4, 2))],
        compiler_params=_cp(),
    )(gwt, tail)


def _pad_cols(a, cols):
    return jnp.pad(a, ((0, 0), (0, cols - a.shape[1])))


def _pad_rows(a, rows):
    return jnp.pad(a, ((0, rows - a.shape[0]), (0, 0)))


FRAME = 928


def _join_frames(frames):
    head = frames[:, :FRAME].at[1:, :16].add(frames[:-1, FRAME:])
    return jnp.concatenate([head.reshape(NDEV * FRAME, D), frames[NDEV - 1, FRAME:]], axis=0)


def _build_wft(wt):
    q = wt[0:1024].reshape(8, 2, 2, 32, D).transpose(0, 2, 1, 3, 4).reshape(1024, D)
    k = wt[1024:1152].reshape(2, 2, 1, 32, D)
    kd = jnp.broadcast_to(k, (2, 2, 2, 32, D)).reshape(256, D)
    v = wt[1152:1280].reshape(2, 1, 64, D)
    vd = jnp.broadcast_to(v, (2, 2, 64, D)).reshape(256, D)
    ag, bq, bk = wt[1280:2304], wt[2304:2816], wt[2816:3328]
    bv, bg, bl = wt[3328:4352], wt[4352:5376], wt[5376:5392]
    ma, mb = wt[5392:6416], wt[6416:7440]
    return jnp.concatenate([q, kd, vd, _pad_rows(bl, C_GLA - C_BL), bv, bq, bk, ag, bg, ma, mb], axis=0)


def _wft_plan():
    moves = []
    for blk in range(8):
        for half in range(2):
            for sub in range(2):
                moves.append((C_Q + 128 * blk + 32 * (2 * half + sub), 128 * blk + 32 * (2 * sub + half), 32))
    for idx in range(4):
        for dup in range(2):
            moves.append((C_KD + 64 * idx + 32 * dup, 1024 + 32 * idx, 32))
    for g in range(2):
        for dup in range(2):
            moves.append((C_VD + 128 * g + 64 * dup, 1152 + 64 * g, 64))
    moves += [(C_BL, 5376, RANK), (C_BV, 3328, 1024), (C_BQ, 2304, 512), (C_BK, 2816, 512),
              (C_AG, 1280, 1024), (C_BG, 4352, 1024), (C_MA, 5392, 1024), (C_MB, 6416, 1024)]
    bulk, seams = [], []
    for dst, src, n in moves:
        r = src
        while r < src + n:
            f = min(r // FRAME, NDEV - 1)
            local = r - FRAME * f
            if f > 0 and local < 16:
                assert local == 0
                seams.append((f, dst + r - src))
                step = 16
            else:
                step = min(src + n, FRAME * (f + 1) if f < NDEV - 1 else IN_WIDTH) - r
                bulk.append((f, local, dst + r - src, step))
            r += step
    assert sorted(f for f, _ in seams) == list(range(1, NDEV))
    return bulk, seams, [(C_BL + RANK, C_GLA - C_BL - RANK)]


def _build_wft_copies(frames):
    bulk, seams, zeros = _wft_plan()
    (z0, zn), = zeros

    def body(f_ref, o_ref, edge, sems, esems):
        copies = [pltpu.make_async_copy(f_ref.at[f, pl.ds(l0, n)], o_ref.at[pl.ds(dst, n)], sems.at[i])
                  for i, (f, l0, dst, n) in enumerate(bulk)]
        loads = []
        for i, (f, _) in enumerate(seams):
            loads.append(pltpu.make_async_copy(f_ref.at[f, pl.ds(0, 16)], edge.at[i, 0], esems.at[i, 0]))
            loads.append(pltpu.make_async_copy(f_ref.at[f - 1, pl.ds(FRAME, 16)], edge.at[i, 1], esems.at[i, 1]))
        for cp in copies + loads:
            cp.start()
        o_ref[z0:z0 + zn, :] = jnp.zeros((zn, D), o_ref.dtype)
        for cp in loads:
            cp.wait()
        for i, (_, dst) in enumerate(seams):
            o_ref[dst:dst + 16, :] = edge[i, 0] + edge[i, 1]
        for cp in copies:
            cp.wait()

    return pl.pallas_call(
        body, name="build_wft",
        in_specs=[_any()], out_specs=_vmem(),
        out_shape=jax.ShapeDtypeStruct((NF, D), frames.dtype),
        scratch_shapes=[pltpu.VMEM((len(seams), 2, 16, D), frames.dtype),
                        pltpu.SemaphoreType.DMA((len(bulk),)), pltpu.SemaphoreType.DMA((len(seams), 2))],
        compiler_params=_cp(),
    )(frames)


def kernel(x, positions, norm_w, w_in, a_sinks, b_gate_up, b_gate_bias, b_out_norm_w, w_a_proj, w_b_proj, w_out, final_norm_w, loss_target, m_norm_w, m_w_in, m_a_sinks, m_b_gate_up, m_b_gate_bias, m_b_out_norm_w, m_w_a_proj, m_w_b_proj, m_w_out, m_final_norm_w, v_norm_w, v_w_in, v_a_sinks, v_b_gate_up, v_b_gate_bias, v_b_out_norm_w, v_w_a_proj, v_w_b_proj, v_w_out, v_final_norm_w):
    T = x.shape[1]
    xs, target = x[0], loss_target[0]
    fnw = final_norm_w.reshape(1, D)
    me = 4 * lax.axis_index("x") + 2 * lax.axis_index("y") + lax.axis_index("c")
    allw, h, cos, sin = _gather_blocks(w_in[0].T, b_gate_up[0], xs, norm_w, positions.reshape(T, 1))
    late_blk = jnp.concatenate([w_a_proj[0], w_b_proj[0], w_out[0]], axis=0).astype(WIRE)
    l_send, l_recv, l_blk, l_land, l_started = _late_gather_start(late_blk, cos)
    wf = _build_wft_copies(allw)
    gu = allw[:, SHARD_PAD:G_ROWS, :64].transpose(1, 0, 2).reshape(RANK, 512)
    gu_pad = _pad_rows(gu, W_BL)

    proj = _proj(h, wf, l_started)
    o_a, lse = _swa_fwd(proj, cos, sin, a_sinks)
    o_b, states = _gla_fwd(proj, gu_pad, b_gate_bias)
    l_blk, l_land = _late_gather_wait(l_send, l_recv, l_blk, l_land, states, lse)
    late = lax.dynamic_update_slice(l_land, l_blk[None], (me, 0, 0))
    w_a, w_b, w_o = (late[:, 128 * i:128 * (i + 1), :].reshape(D, D) for i in range(3))
    (dx2, do_a, do_b, d_gates, g_wa, g_wb, g_wo, g_fn, g_bn, loss_part) = _mid(
        xs, target, proj, o_a, o_b, w_a, w_b, w_o, jnp.tile(b_out_norm_w, (1, B_HEADS)), fnw)
    d_q, d_kv, g_sinks = _swa_bwd(proj, cos, sin, a_sinks, do_a, o_a, lse, cos)
    d_gla, d_bl, g_gu, g_bias = _gla_bwd(proj, gu_pad, b_gate_bias, states, do_b)
    pieces = [d_q, d_kv, d_bl, d_gla, d_gates]
    offsets = [C_Q, C_KD, C_BL, C_GLA, C_GATES]

    ggu = g_gu[:RANK].reshape(RANK, NDEV, 64).transpose(1, 0, 2)
    ggu_half = [jnp.pad(ggu, ((0, 0), (0, 0), (0, DH - 64))), jnp.zeros((NDEV, RANK, DH), F32)]

    def tail(hf):
        cols = slice(hf * DH, (hf + 1) * DH)
        return jnp.concatenate([g[:, cols].reshape(NDEV, 128, DH) for g in (g_wa, g_wb, g_wo)]
                               + [ggu_half[hf]], axis=1).astype(WIRE)

    send0, recv0, s_thru0, land0, started0 = _chip_start(_pair_reduce(_gw_half(h, pieces, 0), tail(0)), 0)
    send1, recv1, s_thru1, land1, started1 = _chip_start(
        _pair_reduce(_gw_half(h, pieces, 1, after=started0), tail(1)), 1)
    grad_x, g_nw = _dh_norm(pieces, offsets, wf, xs, dx2, norm_w, started1)
    small = jnp.concatenate([g_nw, g_fn, _pad_cols(g_bias, D), _pad_cols(g_bn, D), _pad_cols(g_sinks, D),
                             _pad_cols(loss_part, D)], axis=0)
    sm_send, sm_recv, sm_blk, sm_land, sm_started = _late_gather_start(small, g_nw, name="small_gather")
    sums0, got0 = _chip_wait(send0, recv0, s_thru0, land0, sm_started, 0)
    sums1, got1 = _chip_wait(send1, recv1, s_thru1, land1, got0, 1)
    sums, from_chips = [sums0, sums1], [got0, got1]

    ws = dict(norm_w=norm_w, fnw=fnw, bias=b_gate_bias, bn=b_out_norm_w, sinks=a_sinks)
    ms = dict(norm_w=m_norm_w, fnw=m_final_norm_w.reshape(1, D), bias=m_b_gate_bias, bn=m_b_out_norm_w,
              sinks=m_a_sinks)
    vs = dict(norm_w=v_norm_w, fnw=v_final_norm_w.reshape(1, D), bias=v_b_gate_bias, bn=v_b_out_norm_w,
              sinks=v_a_sinks)
    t_rows, t_gu = _finish(
        [w_in[0].T, w_a_proj[0], w_b_proj[0], w_out[0]], [m_w_in[0].T, m_w_a_proj[0], m_w_b_proj[0], m_w_out[0]],
        [v_w_in[0].T, v_w_a_proj[0], v_w_b_proj[0], v_w_out[0]],
        b_gate_up[0], m_b_gate_up[0], v_b_gate_up[0], sums, from_chips)
    sm_blk, sm_land = _late_gather_wait(sm_send, sm_recv, sm_blk, sm_land, t_rows[0], t_gu[0], name="small_gather")
    loss, sm = _finish_small(ws, ms, vs, lax.dynamic_update_slice(sm_land, sm_blk[None], (me, 0, 0)))

    def outputs(k):
        return [sm["norm_w"][k], t_rows[k].T[None], sm["sinks"][k], t_gu[k][None], sm["bias"][k], sm["bn"][k],
                t_rows[4 + k][None], t_rows[8 + k][None], t_rows[12 + k][None], sm["fnw"][k].reshape(D)]

    return (loss[0, 0], grad_x[None], *outputs(0), *outputs(1), *outputs(2), *outputs(3))
```

```python
import functools

import numpy as np
import jax
import jax.numpy as jnp
from jax import lax
from jax.experimental import pallas as pl
from jax.experimental.pallas import tpu as pltpu

F32 = jnp.float32
MXU = jnp.bfloat16
WIRE = jnp.bfloat16

D = 1024
A_HEADS, A_KV, A_HD = 16, 2, 64
BLK = 128
B_HEADS, B_DK, B_DV = 4, 128, 256
RANK, TAU, CHUNK = 16, 16.0, 64
EPS, NEG = 1e-5, -1e30
ROPE_THETA = 10000.0
IN_WIDTH, NDEV = 7440, 8
SHARD = IN_WIDTH // NDEV
LANE = 128

C_Q, C_KD, C_VD, C_BL = 0, 1024, 1280, 1536
C_BV, C_BQ, C_BK = 2048, 3072, 3584
C_AG, C_BG, C_MA, C_MB = 4096, 5120, 6144, 7168
C_GLA, W_GLA, C_GATES, W_GATES = 2048, 2048, 4096, 4096
NF = 8192
W_BL = 128

SHARD_PAD = 944
R_IN, R_A, R_B, R_O, R_GU, ROWS = 0, 944, 1072, 1200, 1328, 1344
SMALL_ROWS = 48

ADAM_LR, ADAM_B1, ADAM_B2, ADAM_EPS, ADAM_WD, ADAM_STEP = 0.001, 0.9, 0.999, 1e-08, 0.01, 10

MESH = pl.DeviceIdType.MESH
VMEM_LIMIT = 56 * 1024 * 1024


def _cp(sem=None, **kw):
    if sem is not None:
        kw["dimension_semantics"] = sem
    return pltpu.CompilerParams(vmem_limit_bytes=VMEM_LIMIT, **kw)


def _dot(a, b):
    return jnp.dot(a, b, preferred_element_type=F32)


def _dot_nt(a, b):
    return lax.dot_general(a, b, (((1,), (1,)), ((), ())), preferred_element_type=F32)


def _dot_tn(a, b):
    return lax.dot_general(a, b, (((0,), (0,)), ((), ())), preferred_element_type=F32)


def _dot_f32(a, b):
    return jnp.dot(a, b, preferred_element_type=F32, precision=lax.Precision.HIGHEST)


def _sigmoid(z):
    return 0.5 * jnp.tanh(0.5 * z) + 0.5


def _rope(xp, cos, sin):
    return xp * cos + pltpu.roll(xp, 64, 1) * sin


def _rope_bwd(dy, cos, sin):
    return dy * cos - pltpu.roll(dy, 64, 1) * sin


def _vmem():
    return pl.BlockSpec(memory_space=pltpu.VMEM)


def _any():
    return pl.BlockSpec(memory_space=pl.ANY)


def _rope_rows():
    half = A_HD // 2
    inv = (np.float32(ROPE_THETA) ** (-np.arange(half, dtype=np.float32) / np.float32(half))).astype(np.float32)
    inv_row = jnp.asarray(np.tile(inv, 4)[None, :])
    sign_row = jnp.asarray(np.concatenate([-np.ones(64, np.float32), np.ones(64, np.float32)])[None, :])
    return inv_row, sign_row


def _prologue_rows(rows, x_ref, nw_ref, pos_ref, inv_ref, sign_ref, h_ref, cos_ref, sin_ref):
    xv = x_ref[rows, :]
    r = lax.rsqrt(jnp.mean(xv * xv, axis=-1, keepdims=True) + EPS)
    h_ref[rows, :] = ((xv * r) * nw_ref[...]).astype(h_ref.dtype)
    ang = pos_ref[rows, :].astype(F32) * inv_ref[...]
    cos_ref[rows, :] = jnp.cos(ang)
    sin_ref[rows, :] = jnp.sin(ang) * sign_ref[...]


def _proj(h, wft, after):
    T = h.shape[0]
    tT, tN = T, 512

    def body(h_ref, w_ref, after_ref, o_ref):
        o_ref[...] = _dot_nt(h_ref[...], w_ref[...])

    return pl.pallas_call(
        body, name="proj", grid=(T // tT, NF // tN),
        in_specs=[pl.BlockSpec((tT, D), lambda i, j: (i, 0)), pl.BlockSpec((tN, D), lambda i, j: (j, 0)), _any()],
        out_specs=pl.BlockSpec((tT, tN), lambda i, j: (i, j)),
        out_shape=jax.ShapeDtypeStruct((T, NF), F32),
        compiler_params=_cp(("parallel", "parallel")),
    )(h, wft, after)


def _swa_masks():
    lane = lax.broadcasted_iota(jnp.int32, (BLK, LANE), 1)
    rope_sub0 = ((lane // 32) % 2) == 0
    std_sub0 = lane < 64
    return lane, rope_sub0, std_sub0


def _swa_tri():
    qi = lax.broadcasted_iota(jnp.int32, (BLK, BLK), 0)
    kj = lax.broadcasted_iota(jnp.int32, (BLK, BLK), 1)
    return kj <= qi


def _swa_fold(full, tri):
    return jnp.where(tri, full[:, BLK:], full[:, :BLK])


def _swa_unfold(sq, tri):
    return jnp.concatenate([jnp.where(tri, 0.0, sq), jnp.where(tri, sq, 0.0)], axis=1)


def _swa_keys(kc_ref, kp_ref, vc_ref, vp_ref, cq, sq, cp, sp):
    def ropek(kref, c, s):
        kv = kref[...]
        return jnp.concatenate([_rope(kv[:, :LANE], c, s), _rope(kv[:, LANE:], c, s)], axis=1)

    K = jnp.concatenate([ropek(kp_ref, cp, sp), ropek(kc_ref, cq, sq)], axis=0).astype(MXU)
    V = jnp.concatenate([vp_ref[...], vc_ref[...]], axis=0).astype(MXU)
    return K, V


def _swa_in_specs(nb, last):
    def cur(n):
        return jnp.minimum(n, last)

    def prev(n):
        return jnp.maximum(cur(n) - 1, 0)

    kd, vd = C_KD // 256, C_VD // 256
    return [
        pl.BlockSpec((BLK, D), lambda n: (cur(n), C_Q // D)),
        pl.BlockSpec((BLK, 256), lambda n: (cur(n), kd)),
        pl.BlockSpec((BLK, 256), lambda n: (prev(n), kd)),
        pl.BlockSpec((BLK, 256), lambda n: (cur(n), vd)),
        pl.BlockSpec((BLK, 256), lambda n: (prev(n), vd)),
        pl.BlockSpec((BLK, LANE), lambda n: (cur(n), 0)),
        pl.BlockSpec((BLK, LANE), lambda n: (cur(n), 0)),
        pl.BlockSpec((BLK, LANE), lambda n: (prev(n), 0)),
        pl.BlockSpec((BLK, LANE), lambda n: (prev(n), 0)),
    ]


def _swa_fwd(proj, cos, sin, sinks):
    T = proj.shape[0]
    nb = T // BLK
    scale = A_HD ** -0.5

    def body(sinks_ref, q_ref, kc_ref, kp_ref, vc_ref, vp_ref, cq_ref, sq_ref, cp_ref, sp_ref, o_ref, l_ref):
        n = pl.program_id(0)
        cq, sq = cq_ref[...], sq_ref[...]
        K, V = _swa_keys(kc_ref, kp_ref, vc_ref, vp_ref, cq, sq, cp_ref[...], sp_ref[...])
        tri = _swa_tri()
        valid = tri | (n > 0)
        lane, rope_sub0, std_sub0 = _swa_masks()
        group = A_HEADS // A_KV
        roped, lses = {}, []

        def products(head):
            pb, sub, g = head // 2, head % 2, head // group
            if sub == 0:
                roped[pb] = _rope(q_ref[:, pb * LANE:(pb + 1) * LANE], cq, sq)
            qm = jnp.where(rope_sub0 if sub == 0 else ~rope_sub0, roped[pb], 0.0).astype(MXU)
            return _dot_nt(qm, K[:, g * LANE:(g + 1) * LANE])

        def softmax(head, s_full):
            s = jnp.where(valid, _swa_fold(s_full, tri) * scale, NEG)
            sink = sinks_ref[0, head]
            m = jnp.maximum(jnp.max(s, axis=1, keepdims=True), sink)
            e = jnp.exp(s - m)
            den = jnp.sum(e, axis=1, keepdims=True) + jnp.exp(sink - m)
            lses.append(m + jnp.log(den))
            return _swa_unfold(e / den, tri).astype(MXU)

        outs = {}
        st1 = {0: products(0), 1: products(1)}
        st2 = {0: softmax(0, st1.pop(0))}
        for head in range(A_HEADS):
            if head + 2 < A_HEADS:
                st1[head + 2] = products(head + 2)
            if head + 1 < A_HEADS:
                st2[head + 1] = softmax(head + 1, st1.pop(head + 1))
            g = head // group
            outs[head] = _dot(st2.pop(head), V[:, g * LANE:(g + 1) * LANE])
            if head % 2 == 1:
                pb = head // 2
                o_ref[:, pb * LANE:(pb + 1) * LANE] = jnp.where(std_sub0, outs[head - 1], outs[head])
        lacc = jnp.zeros((BLK, LANE), F32)
        for head in range(A_HEADS):
            lacc = jnp.where(lane == head, lses[head], lacc)
        l_ref[...] = lacc

    return pl.pallas_call(
        body, name="swa_fwd", grid=(nb,),
        in_specs=[pl.BlockSpec(memory_space=pltpu.SMEM)] + _swa_in_specs(nb, nb - 1),
        out_specs=[pl.BlockSpec((BLK, D), lambda n: (n, 0)), pl.BlockSpec((BLK, LANE), lambda n: (n, 0))],
        out_shape=[jax.ShapeDtypeStruct((T, D), F32), jax.ShapeDtypeStruct((T, LANE), F32)],
        compiler_params=_cp(("parallel",)),
    )(sinks, proj, proj, proj, proj, proj, cos, sin, cos, sin)


def _swa_bwd(proj, cos, sin, sinks, do_a, o_a, lse, after):
    T = proj.shape[0]
    nb = T // BLK
    scale = A_HD ** -0.5

    def body(sinks_ref, q_ref, kc_ref, kp_ref, vc_ref, vp_ref, cq_ref, sq_ref, cp_ref, sp_ref,
             do_ref, o_ref, l_ref, after_ref, dq_ref, dkv_ref, ds_ref, ckv_ref):
        n = pl.program_id(0)

        @pl.when(n == 0)
        def _():
            ckv_ref[...] = jnp.zeros_like(ckv_ref)
            ds_ref[...] = jnp.zeros_like(ds_ref)

        @pl.when(n < nb)
        def _():
            cq, sq, cp, sp = cq_ref[...], sq_ref[...], cp_ref[...], sp_ref[...]
            K, V = _swa_keys(kc_ref, kp_ref, vc_ref, vp_ref, cq, sq, cp, sp)
            tri = _swa_tri()
            valid = tri | (n > 0)
            lane, rope_sub0, std_sub0 = _swa_masks()
            lane_row = lax.broadcasted_iota(jnp.int32, (1, LANE), 1)
            lse_v = l_ref[...]
            dKt = [jnp.zeros((LANE, 2 * BLK), F32) for _ in range(A_KV)]
            dVt = [jnp.zeros((LANE, 2 * BLK), F32) for _ in range(A_KV)]
            dsinks, roped, roped_t, do_t = [], {}, {}, {}
            group = A_HEADS // A_KV
            dim = lax.broadcasted_iota(jnp.int32, (LANE, BLK), 0)
            rope_row0, std_row0 = ((dim // 32) % 2) == 0, dim < 64

            def products(head):
                pb, sub, g = head // 2, head % 2, head // group
                cols = slice(pb * LANE, (pb + 1) * LANE)
                Kg, Vg = K[:, g * LANE:(g + 1) * LANE], V[:, g * LANE:(g + 1) * LANE]
                if sub == 0:
                    roped[pb] = _rope(q_ref[:, cols], cq, sq)
                    roped_t[pb] = roped[pb].T
                    do_t[pb] = do_ref[:, cols].T
                qm = jnp.where(rope_sub0 if sub == 0 else ~rope_sub0, roped[pb], 0.0).astype(MXU)
                qmt = jnp.where(rope_row0 if sub == 0 else ~rope_row0, roped_t[pb], 0.0).astype(MXU)
                dov = jnp.where(std_sub0 if sub == 0 else ~std_sub0, do_ref[:, cols], 0.0)
                dovt = jnp.where(std_row0 if sub == 0 else ~std_row0, do_t[pb], 0.0).astype(MXU)
                delta = jnp.sum(dov * o_ref[:, cols], axis=1, keepdims=True)
                return qmt, dovt, delta, _dot_nt(qm, Kg), _dot_nt(dov.astype(MXU), Vg)

            def scores(head, qmt, dovt, delta, s_full, dp_full):
                lh = jnp.sum(jnp.where(lane == head, lse_v, 0.0), axis=1, keepdims=True)
                p = jnp.where(valid, jnp.exp(_swa_fold(s_full, tri) * scale - lh), 0.0)
                psink = jnp.exp(sinks_ref[0, head] - lh)
                dsinks.append(jnp.sum(-psink * delta, axis=0, keepdims=True))
                dsq = (p * (_swa_fold(dp_full, tri) - delta)) * scale
                return qmt, dovt, _swa_unfold(p, tri).astype(MXU), _swa_unfold(dsq, tri).astype(MXU)

            def grads(head, qmt, dovt, pb16, dsc):
                g = head // group
                dKt[g] = dKt[g] + _dot(qmt, dsc)
                dVt[g] = dVt[g] + _dot(dovt, pb16)
                return _dot(dsc, K[:, g * LANE:(g + 1) * LANE])

            dqs = {}
            st1 = {0: products(0), 1: products(1)}
            st2 = {0: scores(0, *st1.pop(0))}
            for head in range(A_HEADS):
                if head + 2 < A_HEADS:
                    st1[head + 2] = products(head + 2)
                if head + 1 < A_HEADS:
                    st2[head + 1] = scores(head + 1, *st1.pop(head + 1))
                dqs[head] = grads(head, *st2.pop(head))
                if head % 2 == 1:
                    pb = head // 2
                    dqp = jnp.where(rope_sub0, dqs[head - 1], dqs[head])
                    dq_ref[:, pb * LANE:(pb + 1) * LANE] = _rope_bwd(dqp, cq, sq).astype(dq_ref.dtype)
            dsink = jnp.zeros((1, LANE), F32)
            for head in range(A_HEADS):
                dsink = jnp.where(lane_row == head, dsinks[head], dsink)
            dK, dV = [a.T for a in dKt], [a.T for a in dVt]
            prev = ([_rope_bwd(dK[g][:BLK], cp, sp) for g in range(A_KV)] + [dV[g][:BLK] for g in range(A_KV)])
            cur_ = ([_rope_bwd(dK[g][BLK:], cq, sq) for g in range(A_KV)] + [dV[g][BLK:] for g in range(A_KV)])
            dkv_ref[...] = (ckv_ref[...] + jnp.concatenate(prev, axis=1)).astype(dkv_ref.dtype)
            ckv_ref[...] = jnp.concatenate(cur_, axis=1)
            ds_ref[...] = ds_ref[...] + jnp.broadcast_to(dsink, ds_ref.shape)

        @pl.when(n == nb)
        def _():
            dkv_ref[...] = ckv_ref[...].astype(dkv_ref.dtype)

    last = nb - 1

    def cur(n):
        return jnp.minimum(n, last)

    def out_kv(n):
        return (jnp.maximum(n - 1, 0), 0)

    return pl.pallas_call(
        body, name="swa_bwd", grid=(nb + 1,),
        in_specs=[pl.BlockSpec(memory_space=pltpu.SMEM)] + _swa_in_specs(nb, last) + [
            pl.BlockSpec((BLK, D), lambda n: (cur(n), 0)),
            pl.BlockSpec((BLK, D), lambda n: (cur(n), 0)),
            pl.BlockSpec((BLK, LANE), lambda n: (cur(n), 0)),
            _any(),
        ],
        out_specs=[
            pl.BlockSpec((BLK, D), lambda n: (cur(n), 0)),
            pl.BlockSpec((BLK, 512), out_kv),
            pl.BlockSpec((8, LANE), lambda n: (0, 0)),
        ],
        out_shape=[
            jax.ShapeDtypeStruct((T, D), MXU),
            jax.ShapeDtypeStruct((T, 512), MXU),
            jax.ShapeDtypeStruct((8, LANE), F32),
        ],
        scratch_shapes=[pltpu.VMEM((BLK, 512), F32)],
        compiler_params=_cp(("arbitrary",)),
    )(sinks, proj, proj, proj, proj, proj, cos, sin, cos, sin, do_a, o_a, lse, after)


GSTEP = 2 * CHUNK
ST_ROWS = B_HEADS * B_DV


def _gla_gate(bl_ref, gu_ref, bias_ref):
    gk = _dot(bl_ref[...].astype(MXU), gu_ref[...]) + bias_ref[...]
    la = (jnp.minimum(gk, 0.0) - jnp.log(1.0 + jnp.exp(-jnp.abs(gk)))) / TAU
    ri = lax.broadcasted_iota(jnp.int32, (GSTEP, GSTEP), 0)
    ci = lax.broadcasted_iota(jnp.int32, (GSTEP, GSTEP), 1)
    same = (ri // CHUNK) == (ci // CHUNK)
    lower, upper = same & (ci <= ri), same & (ci >= ri)
    b = _dot_f32(jnp.where(lower, 1.0, 0.0).astype(F32), la)
    first = lax.broadcasted_iota(jnp.int32, (GSTEP, 1), 0) < CHUNK
    return gk, la, b, lower, upper, first


def _gla_head(q_ref, k_ref, la, b, first, h):
    sl = slice(h * B_DK, (h + 1) * B_DK)
    bh, lah = b[:, sl], la[:, sl]
    bl_a = jnp.sum(lah[:CHUNK], axis=0, keepdims=True)
    bl_b = jnp.sum(lah[CHUNK:], axis=0, keepdims=True)
    blast = jnp.where(first, bl_a, bl_b)
    qc = q_ref[:, sl] * (B_DK ** -0.5)
    kh = k_ref[:, sl]
    eb, enb, esb = jnp.exp(bh), jnp.exp(-bh), jnp.exp(blast - bh)
    return qc * eb, kh * enb, kh * esb, eb, enb, esb, (jnp.exp(bl_a), jnp.exp(bl_b))


def _gla_specs(step_of):
    return [
        pl.BlockSpec((GSTEP, 512), lambda i: (step_of(i), C_BQ // 512)),
        pl.BlockSpec((GSTEP, 512), lambda i: (step_of(i), C_BK // 512)),
        pl.BlockSpec((GSTEP, D), lambda i: (step_of(i), C_BV // D)),
        pl.BlockSpec((GSTEP, W_BL), lambda i: (step_of(i), C_BL // W_BL)),
        pl.BlockSpec((W_BL, 512), lambda i: (0, 0)),
        pl.BlockSpec((1, 512), lambda i: (0, 0)),
    ]


def _gla_fwd(proj, gu_pad, bias):
    T = proj.shape[0]
    ns = T // GSTEP

    def body(q_ref, k_ref, v_ref, bl_ref, gu_ref, bias_ref, o_ref, st_ref, state_ref):
        @pl.when(pl.program_id(0) == 0)
        def _():
            state_ref[...] = jnp.zeros_like(state_ref)

        _, la, b, lower, _, first = _gla_gate(bl_ref, gu_ref, bias_ref)
        st_ref[0:ST_ROWS, :] = state_ref[...]

        def within(h):
            q_e, k_e, k_s, _, _, _, decays = _gla_head(q_ref, k_ref, la, b, first, h)
            vh = v_ref[:, h * B_DV:(h + 1) * B_DV].astype(MXU)
            q_eb = q_e.astype(MXU)
            att = jnp.where(lower, _dot_nt(q_eb, k_e.astype(MXU)), 0.0)
            return vh, q_eb, k_s.astype(MXU), _dot(att.astype(MXU), vh), decays

        def across(h, vh, q_eb, k_sb, o_intra, decays):
            rows = slice(h * B_DV, (h + 1) * B_DV)
            s0 = state_ref[rows, :]
            o_a = o_intra[:CHUNK] + _dot_nt(q_eb[:CHUNK], s0.astype(MXU))
            s1 = s0 * decays[0] + _dot_tn(vh[:CHUNK], k_sb[:CHUNK])
            st_ref[ST_ROWS + h * B_DV:ST_ROWS + (h + 1) * B_DV, :] = s1
            o_b = o_intra[CHUNK:] + _dot_nt(q_eb[CHUNK:], s1.astype(MXU))
            state_ref[rows, :] = s1 * decays[1] + _dot_tn(vh[CHUNK:], k_sb[CHUNK:])
            o_ref[:, rows] = jnp.concatenate([o_a, o_b], axis=0)

        for h in range(B_HEADS):
            across(h, *within(h))

    return pl.pallas_call(
        body, name="gla_fwd", grid=(ns,),
        in_specs=_gla_specs(lambda i: i),
        out_specs=[pl.BlockSpec((GSTEP, D), lambda i: (i, 0)),
                   pl.BlockSpec((2 * ST_ROWS, B_DK), lambda i: (i, 0))],
        out_shape=[jax.ShapeDtypeStruct((T, D), F32),
                   jax.ShapeDtypeStruct((ns * 2 * ST_ROWS, B_DK), F32)],
        scratch_shapes=[pltpu.VMEM((ST_ROWS, B_DK), F32)],
        compiler_params=_cp(("arbitrary",)),
    )(proj, proj, proj, proj, gu_pad, bias)


def _gla_bwd(proj, gu_pad, bias, states, do_b):
    T = proj.shape[0]
    ns = T // GSTEP
    o_q, o_k = C_BQ - C_GLA, C_BK - C_GLA

    def body(q_ref, k_ref, v_ref, bl_ref, gu_ref, bias_ref, st_ref, do_ref,
             dg_ref, dbl_ref, ggu_ref, gbias_ref, gt_ref):
        @pl.when(pl.program_id(0) == 0)
        def _():
            gt_ref[...] = jnp.zeros_like(gt_ref)
            ggu_ref[...] = jnp.zeros_like(ggu_ref)
            gbias_ref[...] = jnp.zeros_like(gbias_ref)

        gk, la, b, lower, upper_mask, first = _gla_gate(bl_ref, gu_ref, bias_ref)
        upper = jnp.where(upper_mask, 1.0, 0.0).astype(F32)
        lo, hi = slice(0, CHUNK), slice(CHUNK, GSTEP)
        dla_parts = []

        def within(h):
            q_e, k_e, k_s, eb, enb, esb, decays = _gla_head(q_ref, k_ref, la, b, first, h)
            vh = v_ref[:, h * B_DV:(h + 1) * B_DV].astype(MXU)
            doh = do_ref[:, h * B_DV:(h + 1) * B_DV].astype(MXU)
            q_eb, k_eb = q_e.astype(MXU), k_e.astype(MXU)
            att = jnp.where(lower, _dot_nt(q_eb, k_eb), 0.0).astype(MXU)
            datt = jnp.where(lower, _dot_nt(doh, vh), 0.0).astype(MXU)
            return (q_e, k_e, k_s, eb, enb, esb, decays, vh, doh, q_eb, k_s.astype(MXU),
                    _dot(datt, k_eb), _dot_tn(datt, q_eb), _dot_tn(att, doh))

        def across(h, q_e, k_e, k_s, eb, enb, esb, decays, vh, doh, q_eb, k_sb, dq_i, dk_e, dv_i):
            dec_a, dec_b = decays
            rows = slice(h * B_DV, (h + 1) * B_DV)
            s0 = st_ref[rows, :]
            s1 = st_ref[ST_ROWS + h * B_DV:ST_ROWS + (h + 1) * B_DV, :]
            g2 = gt_ref[rows, :]
            g2b = g2.astype(MXU)
            dq_b = dq_i[hi] + _dot(doh[hi], s1.astype(MXU))
            dks_b = _dot(vh[hi], g2b)
            dv_b = dv_i[hi] + _dot_nt(k_sb[hi], g2b)
            ddec_b = jnp.sum(g2 * s1, axis=0, keepdims=True)
            g1 = g2 * dec_b + _dot_tn(doh[hi], q_eb[hi])
            g1b = g1.astype(MXU)
            dq_a = dq_i[lo] + _dot(doh[lo], s0.astype(MXU))
            dks_a = _dot(vh[lo], g1b)
            dv_a = dv_i[lo] + _dot_nt(k_sb[lo], g1b)
            ddec_a = jnp.sum(g1 * s0, axis=0, keepdims=True)
            gt_ref[rows, :] = g1 * dec_a + _dot_tn(doh[lo], q_eb[lo])
            dq_e = jnp.concatenate([dq_a, dq_b], axis=0)
            dk_s = jnp.concatenate([dks_a, dks_b], axis=0)
            dg_ref[:, rows] = jnp.concatenate([dv_a, dv_b], axis=0).astype(dg_ref.dtype)
            dg_ref[:, o_q + h * B_DK:o_q + (h + 1) * B_DK] = (dq_e * eb * (B_DK ** -0.5)).astype(dg_ref.dtype)
            dg_ref[:, o_k + h * B_DK:o_k + (h + 1) * B_DK] = (dk_e * enb + dk_s * esb).astype(dg_ref.dtype)
            dks_ks = dk_s * k_s
            db = dq_e * q_e - dk_e * k_e - dks_ks
            dbl_a = jnp.sum(dks_ks[lo], axis=0, keepdims=True) + ddec_a * dec_a
            dbl_b = jnp.sum(dks_ks[hi], axis=0, keepdims=True) + ddec_b * dec_b
            dla_parts.append(_dot_f32(upper, db) + jnp.where(first, dbl_a, dbl_b))

        for h in range(B_HEADS):
            across(h, *within(h))
        dla = jnp.concatenate(dla_parts, axis=1)
        dgk = dla * (1.0 / TAU) * _sigmoid(-gk)
        dgkb = dgk.astype(MXU)
        dbl_ref[...] = _dot_nt(dgkb, gu_ref[...]).astype(dbl_ref.dtype)
        ggu_ref[...] = ggu_ref[...] + _dot_tn(bl_ref[...].astype(MXU), dgkb)
        gbias_ref[...] = gbias_ref[...] + jnp.broadcast_to(jnp.sum(dgk, axis=0, keepdims=True), gbias_ref.shape)

    def rev(i):
        return ns - 1 - i

    return pl.pallas_call(
        body, name="gla_bwd", grid=(ns,),
        in_specs=_gla_specs(rev) + [
            pl.BlockSpec((2 * ST_ROWS, B_DK), lambda i: (rev(i), 0)),
            pl.BlockSpec((GSTEP, D), lambda i: (rev(i), 0)),
        ],
        out_specs=[
            pl.BlockSpec((GSTEP, W_GLA), lambda i: (rev(i), 0)),
            pl.BlockSpec((GSTEP, W_BL), lambda i: (rev(i), 0)),
            pl.BlockSpec((W_BL, 512), lambda i: (0, 0)),
            pl.BlockSpec((8, 512), lambda i: (0, 0)),
        ],
        out_shape=[
            jax.ShapeDtypeStruct((T, W_GLA), MXU),
            jax.ShapeDtypeStruct((T, W_BL), MXU),
            jax.ShapeDtypeStruct((W_BL, 512), F32),
            jax.ShapeDtypeStruct((8, 512), F32),
        ],
        scratch_shapes=[pltpu.VMEM((B_HEADS * B_DV, B_DK), F32)],
        compiler_params=_cp(("arbitrary",)),
    )(proj, proj, proj, proj, gu_pad, bias, states, do_b)


def _mid(x, target, proj, o_a, o_b, w_a, w_b, w_out, w_bn4, fnw):
    T = x.shape[0]
    tT = min(T, 128)
    nbuf = 4
    o_ag, o_bg, o_ma, o_mb = (c - C_GATES for c in (C_AG, C_BG, C_MA, C_MB))

    def body(x_ref, t_ref, oa_ref, ob_ref, gates_ref, wa_ref, wb_ref, wo_ref, wbn_ref, fnw_ref,
             dx2_ref, doa_ref, dob_ref, dgates_ref,
             gwa_ref, gwb_ref, gwo_ref, gfn_ref, gbn_ref, loss_ref, buf_ref):
        i = pl.program_id(0)

        @pl.when(i == 0)
        def _():
            for r in (gwa_ref, gwb_ref, gwo_ref, gfn_ref, gbn_ref, loss_ref):
                r[...] = jnp.zeros_like(r)

        rows = pl.ds(pl.multiple_of((i % nbuf) * tT, tT), tT)

        def keep(k, val):
            buf_ref[k, rows, :] = val

        oa, ag = oa_ref[...], gates_ref[:, o_ag:o_ag + D]
        sg_a = _sigmoid(ag)
        silu_a = ag * sg_a
        oag_b = (oa * silu_a).astype(MXU)
        keep(0, oag_b)
        y_a = _dot(oag_b, wa_ref[...])

        ob, bg = ob_ref[...], gates_ref[:, o_bg:o_bg + D]
        rbs, obhats = [], []
        for h in range(B_HEADS):
            obh = ob[:, h * B_DV:(h + 1) * B_DV]
            rb = lax.rsqrt(jnp.mean(obh * obh, axis=-1, keepdims=True) + EPS)
            rbs.append(rb)
            obhats.append(obh * rb)
        obhat = jnp.concatenate(obhats, axis=1)
        wbn = wbn_ref[...]
        obn = obhat * wbn
        sg_b = _sigmoid(bg)
        silu_b = bg * sg_b
        obg_b = (obn * silu_b).astype(MXU)
        keep(1, obg_b)
        y_b = _dot(obg_b, wb_ref[...])

        sa, sb = _sigmoid(gates_ref[:, o_ma:o_ma + D]), _sigmoid(gates_ref[:, o_mb:o_mb + D])
        mg_b = (sa * y_a + sb * y_b).astype(MXU)
        keep(2, mg_b)
        x2 = x_ref[...] + _dot(mg_b, wo_ref[...])
        r2 = lax.rsqrt(jnp.mean(x2 * x2, axis=-1, keepdims=True) + EPS)
        xh2 = x2 * r2
        fw = fnw_ref[...]
        err = xh2 * fw - t_ref[...]
        tok = jnp.mean(err * err, axis=-1, keepdims=True)
        loss_ref[...] = loss_ref[...] + 0.5 * jnp.sum(tok, axis=0, keepdims=True)

        dy = err * (1.0 / D)
        gfn_ref[...] = gfn_ref[...] + jnp.broadcast_to(jnp.sum(dy * xh2, axis=0, keepdims=True), gfn_ref.shape)
        gy = dy * fw
        dx2 = r2 * (gy - xh2 * jnp.mean(gy * xh2, axis=-1, keepdims=True))
        dx2_ref[...] = dx2
        dx2_b = dx2.astype(MXU)
        keep(5, dx2_b)
        dmg = _dot_nt(dx2_b, wo_ref[...])

        dgates_ref[:, o_ma:o_ma + D] = (dmg * y_a * sa * (1.0 - sa)).astype(dgates_ref.dtype)
        dgates_ref[:, o_mb:o_mb + D] = (dmg * y_b * sb * (1.0 - sb)).astype(dgates_ref.dtype)
        dya_b = (dmg * sa).astype(MXU)
        dyb_b = (dmg * sb).astype(MXU)
        keep(3, dya_b)
        keep(4, dyb_b)
        doag = _dot_nt(dya_b, wa_ref[...])
        dobg = _dot_nt(dyb_b, wb_ref[...])

        @pl.when(i % nbuf == nbuf - 1)
        def _():
            gwa_ref[...] = gwa_ref[...] + _dot_tn(buf_ref[0], buf_ref[3])
            gwb_ref[...] = gwb_ref[...] + _dot_tn(buf_ref[1], buf_ref[4])
            gwo_ref[...] = gwo_ref[...] + _dot_tn(buf_ref[2], buf_ref[5])

        doa_ref[...] = doag * silu_a
        dgates_ref[:, o_ag:o_ag + D] = (doag * oa * (sg_a * (1.0 + ag * (1.0 - sg_a)))).astype(dgates_ref.dtype)
        dobn = dobg * silu_b
        dgates_ref[:, o_bg:o_bg + D] = (dobg * obn * (sg_b * (1.0 + bg * (1.0 - sg_b)))).astype(dgates_ref.dtype)
        gg = dobn * wbn
        gbn = jnp.zeros((1, B_DV), F32)
        for h in range(B_HEADS):
            sl = slice(h * B_DV, (h + 1) * B_DV)
            gbn = gbn + jnp.sum(dobn[:, sl] * obhats[h], axis=0, keepdims=True)
            ggh = gg[:, sl]
            dob_ref[:, sl] = rbs[h] * (ggh - obhats[h] * jnp.mean(ggh * obhats[h], axis=-1, keepdims=True))
        gbn_ref[...] = gbn_ref[...] + jnp.broadcast_to(gbn, gbn_ref.shape)

    assert (T // tT) % nbuf == 0
    tile = pl.BlockSpec((tT, D), lambda i: (i, 0))
    row = pl.BlockSpec((1, D), lambda i: (0, 0))
    acc8 = pl.BlockSpec((8, D), lambda i: (0, 0))
    return pl.pallas_call(
        body, name="mid", grid=(T // tT,),
        in_specs=[tile, tile, tile, tile, pl.BlockSpec((tT, W_GATES), lambda i: (i, C_GATES // W_GATES)),
                  _vmem(), _vmem(), _vmem(), row, row],
        out_specs=[tile, tile, tile, pl.BlockSpec((tT, W_GATES), lambda i: (i, 0)), _vmem(), _vmem(), _vmem(),
                   acc8, pl.BlockSpec((8, B_DV), lambda i: (0, 0)), pl.BlockSpec((8, LANE), lambda i: (0, 0))],
        out_shape=[
            jax.ShapeDtypeStruct((T, D), F32),
            jax.ShapeDtypeStruct((T, D), F32),
            jax.ShapeDtypeStruct((T, D), F32),
            jax.ShapeDtypeStruct((T, W_GATES), MXU),
            jax.ShapeDtypeStruct((D, D), F32),
            jax.ShapeDtypeStruct((D, D), F32),
            jax.ShapeDtypeStruct((D, D), F32),
            jax.ShapeDtypeStruct((8, D), F32),
            jax.ShapeDtypeStruct((8, B_DV), F32),
            jax.ShapeDtypeStruct((8, LANE), F32),
        ],
        scratch_shapes=[pltpu.VMEM((6, nbuf * tT, D), MXU)],
        compiler_params=_cp(("arbitrary",)),
    )(x, target, o_a, o_b, proj, w_a, w_b, w_out, w_bn4, fnw)


DH = D // 2


_GW_TILES = (("q", 0, 512, 0), ("q", 1, 512, 512), ("kv", 0, 256, 1024), ("bl", 0, RANK, 5376),
             ("gla", 0, 512, 3328), ("gla", 1, 512, 3840), ("gla", 2, 512, 2304), ("gla", 3, 512, 2816),
             ("gates", 0, 512, 1280), ("gates", 1, 512, 1792), ("gates", 2, 512, 4352), ("gates", 3, 512, 4864),
             ("gates", 4, 512, 5392), ("gates", 5, 512, 5904), ("gates", 6, 512, 6416), ("gates", 7, 512, 6928))


def _gw_unpermute(piece, t):
    if piece == "q":
        parts = []
        for blk in range(t.shape[0] // LANE):
            g = [t[blk * LANE + 32 * i:blk * LANE + 32 * (i + 1)] for i in range(4)]
            parts += [g[0], g[2], g[1], g[3]]
        return jnp.concatenate(parts, axis=0)
    if piece == "kv":
        k = [t[64 * i:64 * i + 32] + t[64 * i + 32:64 * i + 64] for i in range(4)]
        v = [t[256 + 128 * g:256 + 128 * g + 64] + t[256 + 128 * g + 64:256 + 128 * (g + 1)] for g in range(2)]
        return jnp.concatenate(k + v, axis=0)
    if piece == "bl":
        return t[:RANK]
    return t


def _gw_half(h, pieces, half, after=None):
    T = h.shape[0]
    steps = len(_GW_TILES)

    def body(*refs):
        h_ref = refs[0]
        srcs = dict(zip(("q", "kv", "bl", "gla", "gates"), refs[1:6]))
        o_ref, stage, sems = refs[-3:]
        j = pl.program_id(0)

        def out_copy(k):
            _, _, n, off = _GW_TILES[k]
            return pltpu.make_async_copy(stage.at[k % 2, 0:n], o_ref.at[pl.ds(off, n)], sems.at[k % 2])

        for k, (piece, _, n, _) in enumerate(_GW_TILES):
            @pl.when(j == k)
            def _(k=k, piece=piece, n=n):
                if k >= 2:
                    out_copy(k - 2).wait()
                t = _gw_unpermute(piece, _dot_tn(srcs[piece][...], h_ref[...]))
                stage[k % 2, 0:n, :] = t.astype(stage.dtype)
                out_copy(k).start()

        @pl.when(j == steps - 1)
        def _():
            out_copy(steps - 2).wait()
            out_copy(steps - 1).wait()

    def tile_of(lo, hi):
        return lambda j: (0, jnp.clip(j - lo, 0, hi - lo - 1))

    in_specs = [pl.BlockSpec((T, DH), lambda j: (0, half)),
                pl.BlockSpec((T, 512), tile_of(0, 2)), pl.BlockSpec((T, 512), lambda j: (0, 0)),
                pl.BlockSpec((T, W_BL), lambda j: (0, 0)),
                pl.BlockSpec((T, 512), tile_of(4, 8)), pl.BlockSpec((T, 512), tile_of(8, 16))]
    args = [h, *pieces]
    if after is not None:
        in_specs.append(_any())
        args.append(after)
    return pl.pallas_call(
        body, name=f"gw_in_half{half}", grid=(steps,),
        in_specs=in_specs, out_specs=_any(),
        out_shape=jax.ShapeDtypeStruct((IN_WIDTH, DH), WIRE),
        scratch_shapes=[pltpu.VMEM((2, 512, DH), WIRE), pltpu.SemaphoreType.DMA((2,))],
        compiler_params=_cp(("arbitrary",)),
    )(*args)


def _chip_copies(s_ref, got_ref, send_sems, recv_sems):
    x, y, c = _place()
    chips = [(1 - x, y), (x, 1 - y), (1 - x, 1 - y)]
    return [pltpu.make_async_remote_copy(
        src_ref=s_ref.at[2 * px + py], dst_ref=got_ref.at[j],
        send_sem=send_sems.at[j], recv_sem=recv_sems.at[j], device_id=(px, py, c), device_id_type=MESH)
        for j, (px, py) in enumerate(chips)]


_EFFECT = pltpu.SideEffectType.DATAFLOW_SIDE_EFFECTING


def _hbm():
    return pl.BlockSpec(memory_space=pltpu.HBM)


def _sem():
    return pl.BlockSpec(memory_space=pltpu.SEMAPHORE)


def _chip_start(sums, half):
    land = pltpu.with_memory_space_constraint(lax.empty((3,) + sums.shape[1:], sums.dtype), pltpu.HBM)

    def body(s_ref, land_ref, send_sems, recv_sems, s_thru, land_thru, token):
        for cp in _chip_copies(s_ref, land_ref, send_sems, recv_sems):
            cp.start()
        token[...] = jnp.zeros_like(token)

    return pl.pallas_call(
        body, name=f"chip_start{half}",
        out_shape=(pltpu.SemaphoreType.DMA((3,)), pltpu.SemaphoreType.DMA((3,)),
                   pltpu.HBM(sums.shape, sums.dtype), pltpu.HBM(land.shape, land.dtype),
                   jax.ShapeDtypeStruct((8, LANE), F32)),
        in_specs=(_hbm(), _hbm()), out_specs=(_sem(), _sem(), _hbm(), _hbm(), _vmem()),
        input_output_aliases={0: 2, 1: 3},
        compiler_params=pltpu.CompilerParams(has_side_effects=_EFFECT),
    )(pltpu.with_memory_space_constraint(sums, pltpu.HBM), land)


def _chip_wait(send_sems, recv_sems, s_thru, land_thru, after, half):
    def body(s_ref, land_ref, send_sems, recv_sems, after_ref, s_out, got_ref):
        copies = _chip_copies(s_ref, land_ref, send_sems, recv_sems)
        for cp in copies:
            cp.wait_send()
        for cp in copies:
            cp.wait_recv()

    return pl.pallas_call(
        body, name=f"chip_wait{half}",
        out_shape=(pltpu.HBM(s_thru.shape, s_thru.dtype), pltpu.HBM(land_thru.shape, land_thru.dtype)),
        in_specs=(_hbm(), _hbm(), _sem(), _sem(), _any()), out_specs=(_hbm(), _hbm()),
        input_output_aliases={0: 0, 1: 1},
        compiler_params=pltpu.CompilerParams(has_side_effects=_EFFECT),
    )(s_thru, land_thru, send_sems, recv_sems, after)


def _dh_norm(pieces, offsets, wf, x, dx2, norm_w, after):
    T = x.shape[0]
    tT = min(T, 256)
    widths = [p.shape[1] for p in pieces]
    npc = len(pieces)

    def body(*refs):
        dp_refs = refs[:npc]
        wf_ref, x_ref, dx2_ref, nw_ref, _, gx_ref, gnw_ref = refs[npc:]

        @pl.when(pl.program_id(0) == 0)
        def _():
            gnw_ref[...] = jnp.zeros_like(gnw_ref)

        dh = jnp.zeros((tT, D), F32)
        for dp_ref, off, w in zip(dp_refs, offsets, widths):
            dh = dh + _dot(dp_ref[...], wf_ref[off:off + w, :])
        xv = x_ref[...]
        r = lax.rsqrt(jnp.mean(xv * xv, axis=-1, keepdims=True) + EPS)
        xh = xv * r
        gnw_ref[...] = gnw_ref[...] + jnp.broadcast_to(jnp.sum(dh * xh, axis=0, keepdims=True), gnw_ref.shape)
        g = dh * nw_ref[...]
        gx_ref[...] = r * (g - xh * jnp.mean(g * xh, axis=-1, keepdims=True)) + dx2_ref[...]

    tile = pl.BlockSpec((tT, D), lambda i: (i, 0))
    return pl.pallas_call(
        body, name="dh_norm", grid=(T // tT,),
        in_specs=[pl.BlockSpec((tT, w), lambda i: (i, 0)) for w in widths]
        + [_vmem(), tile, tile, pl.BlockSpec((1, D), lambda i: (0, 0)), _any()],
        out_specs=[tile, pl.BlockSpec((8, D), lambda i: (0, 0))],
        out_shape=[jax.ShapeDtypeStruct((T, D), F32), jax.ShapeDtypeStruct((8, D), F32)],
        compiler_params=_cp(("arbitrary",)),
    )(*pieces, wf, x, dx2, norm_w, after)


def _adamw_math(w, g, m, v):
    m = ADAM_B1 * m + (1.0 - ADAM_B1) * g
    v = ADAM_B2 * v + (1.0 - ADAM_B2) * (g * g)
    m_hat = m / (1.0 - ADAM_B1 ** ADAM_STEP)
    v_hat = v / (1.0 - ADAM_B2 ** ADAM_STEP)
    delta = -ADAM_LR * (m_hat / (jnp.sqrt(v_hat) + ADAM_EPS) + ADAM_WD * w)
    return delta, m, v


def _fetch_partials(s_ref, got_ref, buf, sems):
    x, y, _ = _place()
    cps = [pltpu.make_async_copy(s_ref.at[2 * x + y], buf.at[0], sems.at[0])]
    cps += [pltpu.make_async_copy(got_ref.at[j], buf.at[1 + j], sems.at[1 + j]) for j in range(3)]
    for cp in cps:
        cp.start()
    for cp in cps:
        cp.wait()


SMALL_AT = dict(norm_w=0, fnw=8, bias=16, bn=24, sinks=32, loss=40)
ROW_AT = (R_IN, R_A, R_B, R_O)


def _finish_small(ws, ms, vs, smalls):
    names = ["norm_w", "fnw", "bias", "bn", "sinks"]
    widths = [ws[n].shape[1] for n in names]

    def body(*refs):
        w_refs, m_refs, v_refs = refs[0:5], refs[5:10], refs[10:15]
        smalls_ref, loss_ref = refs[15], refs[16]
        outs, tot = refs[17:37], refs[37]
        acc = smalls_ref[0]
        for d in range(1, NDEV):
            acc = acc + smalls_ref[d]
        tot[...] = acc
        loss_ref[...] = tot[SMALL_AT["loss"]:SMALL_AT["loss"] + 1, 0:1]
        for p, (nm_, wd) in enumerate(zip(names, widths)):
            r = SMALL_AT[nm_]
            g = tot[r:r + 1, 0:wd]
            d, nm, nv = _adamw_math(w_refs[p][...], g, m_refs[p][...], v_refs[p][...])
            for o, val in zip(outs[4 * p:4 * p + 4], (g, d, nm, nv)):
                o[...] = val

    res = pl.pallas_call(
        body, name="finish_small",
        in_specs=[_vmem()] * 16, out_specs=[_vmem()] * 21,
        out_shape=[jax.ShapeDtypeStruct((1, 1), F32)]
        + [jax.ShapeDtypeStruct((1, wd), F32) for wd in widths for _ in range(4)],
        scratch_shapes=[pltpu.VMEM((SMALL_ROWS, D), F32)],
        compiler_params=_cp(),
    )(*[ws[n] for n in names], *[ms[n] for n in names], *[vs[n] for n in names], smalls)
    return res[0], {n: tuple(res[1 + 4 * p:5 + 4 * p]) for p, n in enumerate(names)}


def _finish(w_rows, m_rows, v_rows, gu_w, gu_m, gu_v, sums, got):
    shapes = [w.shape for w in w_rows]

    def body(*refs):
        wr_refs, mr_refs, vr_refs = refs[0:4], refs[4:8], refs[8:12]
        guw_ref, gum_ref, guv_ref = refs[12:15]
        s_refs, got_refs = refs[15:17], refs[17:19]
        row_outs = refs[19:35]
        gu_outs = refs[35:39]
        buf, gsh, sems = refs[39:]
        x, y, c = _place()
        me_slot = 4 * x + 2 * y + c
        unshift = lax.rem(SHARD_PAD - 2 * me_slot, SHARD_PAD)

        def total(rows, cols):
            g = buf[0, rows, cols].astype(F32)
            for j in range(1, 4):
                g = g + buf[j, rows, cols].astype(F32)
            return g

        def update(p, g, cols):
            d, nm, nv = _adamw_math(wr_refs[p][:, cols], g, mr_refs[p][:, cols], vr_refs[p][:, cols])
            for o, val in zip(row_outs[4 * p:4 * p + 4], (g, d, nm, nv)):
                o[:, cols] = val

        for hf in range(2):
            _fetch_partials(s_refs[hf], got_refs[hf], buf.at[:, 0:HALF_ROWS[hf]], sems)
            for cc in range(DH // LANE):
                src = slice(cc * LANE, (cc + 1) * LANE)
                gsh[...] = pltpu.roll(total(slice(0, SHARD_PAD), src), unshift, 0)
                update(0, gsh[0:SHARD, :], slice(hf * DH + cc * LANE, hf * DH + (cc + 1) * LANE))
            if hf == 0:
                g = total(slice(SHARD_PAD, G_ROWS), slice(0, 64))
                d, nm, nv = _adamw_math(guw_ref[...], g, gum_ref[...], guv_ref[...])
                for o, val in zip(gu_outs, (g, d, nm, nv)):
                    o[...] = val
                for p in range(1, 4):
                    for part in range(2):
                        r0 = G_ROWS + 256 * (p - 1) + 128 * part
                        for cc in range(DH // LANE):
                            update(p, total(slice(r0, r0 + 128), slice(cc * LANE, (cc + 1) * LANE)),
                                   slice(part * DH + cc * LANE, part * DH + (cc + 1) * LANE))

    res = pl.pallas_call(
        body, name="finish",
        in_specs=[_vmem()] * 15 + [_any()] * 4,
        out_specs=[_vmem()] * 20,
        out_shape=[jax.ShapeDtypeStruct(s, F32) for s in shapes for _ in range(4)]
        + [jax.ShapeDtypeStruct((RANK, 64), F32)] * 4,
        scratch_shapes=[pltpu.VMEM((4, HALF_ROWS[0], DH), sums[0].dtype), pltpu.VMEM((SHARD_PAD, LANE), F32),
                        pltpu.SemaphoreType.DMA((4,))],
        compiler_params=_cp(),
    )(*w_rows, *m_rows, *v_rows, gu_w, gu_m, gu_v, *sums, *got)
    return tuple(res[0:16]), tuple(res[16:20])


def _place():
    x, y, c = lax.axis_index("x"), lax.axis_index("y"), lax.axis_index("c")
    return x, y, c


def _peers(x, y, c):
    return [(x ^ dx, y ^ dy, c ^ dc) for dx in range(2) for dy in range(2) for dc in range(2) if dx + dy + dc]


def _late_gather_start(blk, after, name="late_gather"):
    land = pltpu.with_memory_space_constraint(lax.empty((NDEV,) + blk.shape, blk.dtype), pltpu.HBM)

    def body(b_ref, land_ref, after_ref, send_sems, recv_sems, b_thru, land_thru, token):
        x, y, c = _place()
        for k, to in enumerate(_peers(x, y, c)):
            pltpu.make_async_remote_copy(
                src_ref=b_ref, dst_ref=land_ref.at[4 * x + 2 * y + c], send_sem=send_sems.at[k],
                recv_sem=recv_sems.at[k], device_id=to, device_id_type=MESH).start()
        token[...] = jnp.zeros_like(token)

    return pl.pallas_call(
        body, name=name + "_start",
        out_shape=(pltpu.SemaphoreType.DMA((7,)), pltpu.SemaphoreType.DMA((7,)),
                   pltpu.HBM(blk.shape, blk.dtype), pltpu.HBM(land.shape, land.dtype),
                   jax.ShapeDtypeStruct((8, LANE), F32)),
        in_specs=(_hbm(), _hbm(), _any()), out_specs=(_sem(), _sem(), _hbm(), _hbm(), _vmem()),
        input_output_aliases={0: 2, 1: 3},
        compiler_params=pltpu.CompilerParams(has_side_effects=_EFFECT),
    )(pltpu.with_memory_space_constraint(blk, pltpu.HBM), land, after)


def _late_gather_wait(send_sems, recv_sems, b_thru, land_thru, after, after2, name="late_gather"):
    def body(b_ref, land_ref, send_sems, recv_sems, after_ref, after2_ref, b_out, got_ref):
        x, y, c = _place()
        copies = [pltpu.make_async_remote_copy(
            src_ref=b_ref, dst_ref=land_ref.at[4 * x + 2 * y + c], send_sem=send_sems.at[k],
            recv_sem=recv_sems.at[k], device_id=to, device_id_type=MESH)
            for k, to in enumerate(_peers(x, y, c))]
        for cp in copies:
            cp.wait_send()
        for cp in copies:
            cp.wait_recv()

    return pl.pallas_call(
        body, name=name + "_wait",
        out_shape=(pltpu.HBM(b_thru.shape, b_thru.dtype), pltpu.HBM(land_thru.shape, land_thru.dtype)),
        in_specs=(_hbm(), _hbm(), _sem(), _sem(), _any(), _any()), out_specs=(_hbm(), _hbm()),
        input_output_aliases={0: 0, 1: 1},
        compiler_params=pltpu.CompilerParams(has_side_effects=_EFFECT),
    )(b_thru, land_thru, send_sems, recv_sems, after, after2)


G_ROWS = SHARD_PAD + RANK
HALF_ROWS = (G_ROWS + 3 * 256, SHARD_PAD)


def _gather_blocks(w_in_t, gu_s, xs, norm_w, pos_col):
    rows, cols = G_ROWS, D
    T = xs.shape[0]
    tT = min(T, 256)
    inv_row, sign_row = _rope_rows()

    def body(wi_ref, gu_ref, xs_ref, nw_ref, pos_ref, inv_ref, sign_ref,
             out_ref, h_ref, cos_ref, sin_ref, x_ref, frame_ref, send_sems, recv_sems, local_sem):
        x, y, c = _place()
        me, sibling = (x, y, c), (x, y, 1 - c)
        chips = [(1 - x, y), (x, 1 - y), (1 - x, 1 - y)]
        shift = 2 * (4 * x + 2 * y + c)
        frame_ref[SHARD - SHARD % 8:, :] = jnp.zeros((SHARD_PAD - SHARD + SHARD % 8, D), F32)
        frame_ref[:SHARD, :] = wi_ref[...]
        for cc in range(D // LANE):
            cs = slice(cc * LANE, (cc + 1) * LANE)
            x_ref[0:SHARD_PAD, cs] = pltpu.roll(frame_ref[:, cs], shift, 0).astype(x_ref.dtype)
        x_ref[SHARD_PAD:G_ROWS, :] = jnp.zeros((RANK, D), x_ref.dtype)
        x_ref[SHARD_PAD:G_ROWS, 0:64] = gu_ref[...].astype(x_ref.dtype)

        def slot(px, py, pc):
            return out_ref.at[4 * px + 2 * py + pc]

        def copy(k, block, to, src=None):
            return pltpu.make_async_remote_copy(
                src_ref=slot(*block) if src is None else src, dst_ref=slot(*block),
                send_sem=send_sems.at[k], recv_sem=recv_sems.at[k], device_id=to, device_id_type=MESH)

        mine = pltpu.make_async_copy(x_ref, slot(*me), local_sem)
        mine.start()
        first = [copy(0, me, sibling, src=x_ref)]
        first += [copy(1 + j, me, (*chip, c), src=x_ref) for j, chip in enumerate(chips)]
        for cp in first:
            cp.start()

        @pl.loop(0, T // tT)
        def _(i):
            rows_i = pl.ds(pl.multiple_of(i * tT, tT), tT)
            _prologue_rows(rows_i, xs_ref, nw_ref, pos_ref, inv_ref, sign_ref, h_ref, cos_ref, sin_ref)

        passed = [copy(4 + j, (*chip, c), sibling) for j, chip in enumerate(chips)]
        for j, chip in enumerate(chips):
            copy(1 + j, (*chip, c), me).wait_recv()
            passed[j].start()
        copy(0, sibling, me).wait_recv()
        for j, chip in enumerate(chips):
            copy(4 + j, (*chip, 1 - c), me).wait_recv()
        for cp in first + passed:
            cp.wait_send()
        mine.wait()

    return pl.pallas_call(
        body, name="gather_weights",
        in_specs=[_vmem()] * 7, out_specs=[_any()] + [_vmem()] * 3,
        out_shape=[jax.ShapeDtypeStruct((NDEV, rows, cols), WIRE), jax.ShapeDtypeStruct((T, D), MXU),
                   jax.ShapeDtypeStruct((T, LANE), F32), jax.ShapeDtypeStruct((T, LANE), F32)],
        scratch_shapes=[pltpu.VMEM((rows, cols), WIRE), pltpu.VMEM((SHARD_PAD, D), F32),
                        pltpu.SemaphoreType.DMA((7,)), pltpu.SemaphoreType.DMA((7,)), pltpu.SemaphoreType.DMA],
        compiler_params=_cp(),
    )(w_in_t, gu_s, xs, norm_w, pos_col, inv_row, sign_row)


def _pair_reduce(gwt, tail):
    n = gwt.shape[1]
    rows = SHARD_PAD + (0 if tail is None else tail.shape[1])
    blk = (4, rows, n)

    def body(*refs):
        g_ref = refs[0]
        t_ref = None if tail is None else refs[1]
        out_ref, got, own, send_sems, recv_sems, own_sems = refs[-6:]
        x, y, c = _place()

        def parts(d, dst):
            frame = g_ref.at[pl.ds(pl.multiple_of(FRAME * d, 16), SHARD_PAD)]
            pieces = [(frame, dst.at[0:SHARD_PAD])]
            return pieces if tail is None else pieces + [(t_ref.at[d], dst.at[SHARD_PAD:rows])]

        sends, loads = [], []
        for chip in range(4):
            sends.append([pltpu.make_async_remote_copy(
                src_ref=s, dst_ref=d_, send_sem=send_sems.at[chip, k], recv_sem=recv_sems.at[chip, k],
                device_id=(x, y, 1 - c), device_id_type=MESH)
                for k, (s, d_) in enumerate(parts(2 * chip + (1 - c), got.at[chip]))])
            loads.append([pltpu.make_async_copy(s, d_, own_sems.at[chip, k])
                          for k, (s, d_) in enumerate(parts(2 * chip + c, own.at[chip]))])
        for group in sends + loads:
            for cp in group:
                cp.start()
        for chip in range(4):
            for cp in loads[chip]:
                cp.wait()
            for cp in sends[chip]:
                cp.wait_recv()
            out_ref[chip] = (own[chip].astype(F32) + got[chip].astype(F32)).astype(out_ref.dtype)
        for group in sends:
            for cp in group:
                cp.wait_send()

    args = [gwt] if tail is None else [gwt, tail]
    return pl.pallas_call(
        body, name="pair_reduce",
        in_specs=[_any()] * len(args), out_specs=_vmem(),
        out_shape=jax.ShapeDtypeStruct(blk, gwt.dtype),
        scratch_shapes=[pltpu.VMEM(blk, gwt.dtype), pltpu.VMEM(blk, gwt.dtype),
                        pltpu.SemaphoreType.DMA((4, 2)), pltpu.SemaphoreType.DMA((4, 2)), pltpu.SemaphoreType.DMA((4, 2))],
        compiler_params=_cp(),
    )(*args)


def _pad_cols(a, cols):
    return jnp.pad(a, ((0, 0), (0, cols - a.shape[1])))


def _pad_rows(a, rows):
    return jnp.pad(a, ((0, rows - a.shape[0]), (0, 0)))


FRAME = 928


def _join_frames(frames):
    head = frames[:, :FRAME].at[1:, :16].add(frames[:-1, FRAME:])
    return jnp.concatenate([head.reshape(NDEV * FRAME, D), frames[NDEV - 1, FRAME:]], axis=0)


def _build_wft(wt):
    q = wt[0:1024].reshape(8, 2, 2, 32, D).transpose(0, 2, 1, 3, 4).reshape(1024, D)
    k = wt[1024:1152].reshape(2, 2, 1, 32, D)
    kd = jnp.broadcast_to(k, (2, 2, 2, 32, D)).reshape(256, D)
    v = wt[1152:1280].reshape(2, 1, 64, D)
    vd = jnp.broadcast_to(v, (2, 2, 64, D)).reshape(256, D)
    ag, bq, bk = wt[1280:2304], wt[2304:2816], wt[2816:3328]
    bv, bg, bl = wt[3328:4352], wt[4352:5376], wt[5376:5392]
    ma, mb = wt[5392:6416], wt[6416:7440]
    return jnp.concatenate([q, kd, vd, _pad_rows(bl, C_GLA - C_BL), bv, bq, bk, ag, bg, ma, mb], axis=0)


def _wft_plan():
    moves = []
    for blk in range(8):
        for half in range(2):
            for sub in range(2):
                moves.append((C_Q + 128 * blk + 32 * (2 * half + sub), 128 * blk + 32 * (2 * sub + half), 32))
    for idx in range(4):
        for dup in range(2):
            moves.append((C_KD + 64 * idx + 32 * dup, 1024 + 32 * idx, 32))
    for g in range(2):
        for dup in range(2):
            moves.append((C_VD + 128 * g + 64 * dup, 1152 + 64 * g, 64))
    moves += [(C_BL, 5376, RANK), (C_BV, 3328, 1024), (C_BQ, 2304, 512), (C_BK, 2816, 512),
              (C_AG, 1280, 1024), (C_BG, 4352, 1024), (C_MA, 5392, 1024), (C_MB, 6416, 1024)]
    bulk, seams = [], []
    for dst, src, n in moves:
        r = src
        while r < src + n:
            f = min(r // FRAME, NDEV - 1)
            local = r - FRAME * f
            if f > 0 and local < 16:
                assert local == 0
                seams.append((f, dst + r - src))
                step = 16
            else:
                step = min(src + n, FRAME * (f + 1) if f < NDEV - 1 else IN_WIDTH) - r
                bulk.append((f, local, dst + r - src, step))
            r += step
    assert sorted(f for f, _ in seams) == list(range(1, NDEV))
    return bulk, seams, [(C_BL + RANK, C_GLA - C_BL - RANK)]


def _build_wft_copies(frames):
    bulk, seams, zeros = _wft_plan()
    (z0, zn), = zeros

    def body(f_ref, o_ref, edge, sems, esems):
        copies = [pltpu.make_async_copy(f_ref.at[f, pl.ds(l0, n)], o_ref.at[pl.ds(dst, n)], sems.at[i])
                  for i, (f, l0, dst, n) in enumerate(bulk)]
        loads = []
        for i, (f, _) in enumerate(seams):
            loads.append(pltpu.make_async_copy(f_ref.at[f, pl.ds(0, 16)], edge.at[i, 0], esems.at[i, 0]))
            loads.append(pltpu.make_async_copy(f_ref.at[f - 1, pl.ds(FRAME, 16)], edge.at[i, 1], esems.at[i, 1]))
        for cp in copies + loads:
            cp.start()
        o_ref[z0:z0 + zn, :] = jnp.zeros((zn, D), o_ref.dtype)
        for cp in loads:
            cp.wait()
        for i, (_, dst) in enumerate(seams):
            o_ref[dst:dst + 16, :] = edge[i, 0] + edge[i, 1]
        for cp in copies:
            cp.wait()

    return pl.pallas_call(
        body, name="build_wft",
        in_specs=[_any()], out_specs=_vmem(),
        out_shape=jax.ShapeDtypeStruct((NF, D), frames.dtype),
        scratch_shapes=[pltpu.VMEM((len(seams), 2, 16, D), frames.dtype),
                        pltpu.SemaphoreType.DMA((len(bulk),)), pltpu.SemaphoreType.DMA((len(seams), 2))],
        compiler_params=_cp(),
    )(frames)


def kernel(x, positions, norm_w, w_in, a_sinks, b_gate_up, b_gate_bias, b_out_norm_w, w_a_proj, w_b_proj, w_out, final_norm_w, loss_target, m_norm_w, m_w_in, m_a_sinks, m_b_gate_up, m_b_gate_bias, m_b_out_norm_w, m_w_a_proj, m_w_b_proj, m_w_out, m_final_norm_w, v_norm_w, v_w_in, v_a_sinks, v_b_gate_up, v_b_gate_bias, v_b_out_norm_w, v_w_a_proj, v_w_b_proj, v_w_out, v_final_norm_w):
    T = x.shape[1]
    xs, target = x[0], loss_target[0]
    fnw = final_norm_w.reshape(1, D)
    me = 4 * lax.axis_index("x") + 2 * lax.axis_index("y") + lax.axis_index("c")
    allw, h, cos, sin = _gather_blocks(w_in[0].T, b_gate_up[0], xs, norm_w, positions.reshape(T, 1))
    late_blk = jnp.concatenate([w_a_proj[0], w_b_proj[0], w_out[0]], axis=0).astype(WIRE)
    l_send, l_recv, l_blk, l_land, l_started = _late_gather_start(late_blk, cos)
    wf = _build_wft_copies(allw)
    gu = allw[:, SHARD_PAD:G_ROWS, :64].transpose(1, 0, 2).reshape(RANK, 512)
    gu_pad = _pad_rows(gu, W_BL)

    proj = _proj(h, wf, l_started)
    o_a, lse = _swa_fwd(proj, cos, sin, a_sinks)
    o_b, states = _gla_fwd(proj, gu_pad, b_gate_bias)
    l_blk, l_land = _late_gather_wait(l_send, l_recv, l_blk, l_land, states, lse)
    late = lax.dynamic_update_slice(l_land, l_blk[None], (me, 0, 0))
    w_a, w_b, w_o = (late[:, 128 * i:128 * (i + 1), :].reshape(D, D) for i in range(3))
    (dx2, do_a, do_b, d_gates, g_wa, g_wb, g_wo, g_fn, g_bn, loss_part) = _mid(
        xs, target, proj, o_a, o_b, w_a, w_b, w_o, jnp.tile(b_out_norm_w, (1, B_HEADS)), fnw)
    d_q, d_kv, g_sinks = _swa_bwd(proj, cos, sin, a_sinks, do_a, o_a, lse, cos)
    d_gla, d_bl, g_gu, g_bias = _gla_bwd(proj, gu_pad, b_gate_bias, states, do_b)
    pieces = [d_q, d_kv, d_bl, d_gla, d_gates]
    offsets = [C_Q, C_KD, C_BL, C_GLA, C_GATES]

    ggu = g_gu[:RANK].reshape(RANK, NDEV, 64).transpose(1, 0, 2)
    tail0 = jnp.concatenate(
        [jnp.pad(ggu, ((0, 0), (0, 0), (0, DH - 64)))]
        + [g[:, cols].reshape(NDEV, 128, DH) for g in (g_wa, g_wb, g_wo) for cols in (slice(0, DH), slice(DH, D))],
        axis=1).astype(WIRE)

    send0, recv0, s_thru0, land0, started0 = _chip_start(_pair_reduce(_gw_half(h, pieces, 0), tail0), 0)
    send1, recv1, s_thru1, land1, started1 = _chip_start(
        _pair_reduce(_gw_half(h, pieces, 1, after=started0), None), 1)
    grad_x, g_nw = _dh_norm(pieces, offsets, wf, xs, dx2, norm_w, started1)
    small = jnp.concatenate([g_nw, g_fn, _pad_cols(g_bias, D), _pad_cols(g_bn, D), _pad_cols(g_sinks, D),
                             _pad_cols(loss_part, D)], axis=0)
    sm_send, sm_recv, sm_blk, sm_land, sm_started = _late_gather_start(small, g_nw, name="small_gather")
    sums0, got0 = _chip_wait(send0, recv0, s_thru0, land0, sm_started, 0)
    sums1, got1 = _chip_wait(send1, recv1, s_thru1, land1, got0, 1)
    sums, from_chips = [sums0, sums1], [got0, got1]

    ws = dict(norm_w=norm_w, fnw=fnw, bias=b_gate_bias, bn=b_out_norm_w, sinks=a_sinks)
    ms = dict(norm_w=m_norm_w, fnw=m_final_norm_w.reshape(1, D), bias=m_b_gate_bias, bn=m_b_out_norm_w,
              sinks=m_a_sinks)
    vs = dict(norm_w=v_norm_w, fnw=v_final_norm_w.reshape(1, D), bias=v_b_gate_bias, bn=v_b_out_norm_w,
              sinks=v_a_sinks)
    t_rows, t_gu = _finish(
        [w_in[0].T, w_a_proj[0], w_b_proj[0], w_out[0]], [m_w_in[0].T, m_w_a_proj[0], m_w_b_proj[0], m_w_out[0]],
        [v_w_in[0].T, v_w_a_proj[0], v_w_b_proj[0], v_w_out[0]],
        b_gate_up[0], m_b_gate_up[0], v_b_gate_up[0], sums, from_chips)
    sm_blk, sm_land = _late_gather_wait(sm_send, sm_recv, sm_blk, sm_land, t_rows[0], t_gu[0], name="small_gather")
    loss, sm = _finish_small(ws, ms, vs, lax.dynamic_update_slice(sm_land, sm_blk[None], (me, 0, 0)))

    def outputs(k):
        return [sm["norm_w"][k], t_rows[k].T[None], sm["sinks"][k], t_gu[k][None], sm["bias"][k], sm["bn"][k],
                t_rows[4 + k][None], t_rows[8 + k][None], t_rows[12 + k][None], sm["fnw"][k].reshape(D)]

    return (loss[0, 0], grad_x[None], *outputs(0), *outputs(1), *outputs(2), *outputs(3))
```

```python
import functools

import numpy as np
import jax
import jax.numpy as jnp
from jax import lax
from jax.experimental import pallas as pl
from jax.experimental.pallas import tpu as pltpu

F32 = jnp.float32
MXU = jnp.bfloat16
WIRE = jnp.bfloat16

D = 1024
A_HEADS, A_KV, A_HD = 16, 2, 64
BLK = 128
B_HEADS, B_DK, B_DV = 4, 128, 256
RANK, TAU, CHUNK = 16, 16.0, 64
EPS, NEG = 1e-5, -1e30
ROPE_THETA = 10000.0
IN_WIDTH, NDEV = 7440, 8
SHARD = IN_WIDTH // NDEV
LANE = 128

C_Q, C_KD, C_VD, C_BL = 0, 1024, 1280, 1536
C_BV, C_BQ, C_BK = 2048, 3072, 3584
C_AG, C_BG, C_MA, C_MB = 4096, 5120, 6144, 7168
C_GLA, W_GLA, C_GATES, W_GATES = 2048, 2048, 4096, 4096
NF = 8192
W_BL = 128

SHARD_PAD = 944
R_IN, R_A, R_B, R_O, R_GU, ROWS = 0, 944, 1072, 1200, 1328, 1344
SMALL_ROWS = 48

ADAM_LR, ADAM_B1, ADAM_B2, ADAM_EPS, ADAM_WD, ADAM_STEP = 0.001, 0.9, 0.999, 1e-08, 0.01, 10

MESH = pl.DeviceIdType.MESH
VMEM_LIMIT = 56 * 1024 * 1024


def _cp(sem=None, **kw):
    if sem is not None:
        kw["dimension_semantics"] = sem
    return pltpu.CompilerParams(vmem_limit_bytes=VMEM_LIMIT, **kw)


def _dot(a, b):
    return jnp.dot(a, b, preferred_element_type=F32)


def _dot_nt(a, b):
    return lax.dot_general(a, b, (((1,), (1,)), ((), ())), preferred_element_type=F32)


def _dot_tn(a, b):
    return lax.dot_general(a, b, (((0,), (0,)), ((), ())), preferred_element_type=F32)


def _dot_f32(a, b):
    return jnp.dot(a, b, preferred_element_type=F32, precision=lax.Precision.HIGHEST)


def _sigmoid(z):
    return 0.5 * jnp.tanh(0.5 * z) + 0.5


def _rope(xp, cos, sin):
    return xp * cos + pltpu.roll(xp, 64, 1) * sin


def _rope_bwd(dy, cos, sin):
    return dy * cos - pltpu.roll(dy, 64, 1) * sin


def _vmem():
    return pl.BlockSpec(memory_space=pltpu.VMEM)


def _any():
    return pl.BlockSpec(memory_space=pl.ANY)


def _rope_rows():
    half = A_HD // 2
    inv = (np.float32(ROPE_THETA) ** (-np.arange(half, dtype=np.float32) / np.float32(half))).astype(np.float32)
    inv_row = jnp.asarray(np.tile(inv, 4)[None, :])
    sign_row = jnp.asarray(np.concatenate([-np.ones(64, np.float32), np.ones(64, np.float32)])[None, :])
    return inv_row, sign_row


def _prologue_rows(rows, x_ref, nw_ref, pos_ref, inv_ref, sign_ref, h_ref, cos_ref, sin_ref):
    xv = x_ref[rows, :]
    r = lax.rsqrt(jnp.mean(xv * xv, axis=-1, keepdims=True) + EPS)
    h_ref[rows, :] = ((xv * r) * nw_ref[...]).astype(h_ref.dtype)
    ang = pos_ref[rows, :].astype(F32) * inv_ref[...]
    cos_ref[rows, :] = jnp.cos(ang)
    sin_ref[rows, :] = jnp.sin(ang) * sign_ref[...]


def _proj(h, wft, after):
    T = h.shape[0]
    tT, tN = T, 512

    def body(h_ref, w_ref, after_ref, o_ref):
        o_ref[...] = _dot_nt(h_ref[...], w_ref[...])

    return pl.pallas_call(
        body, name="proj", grid=(T // tT, NF // tN),
        in_specs=[pl.BlockSpec((tT, D), lambda i, j: (i, 0)), pl.BlockSpec((tN, D), lambda i, j: (j, 0)), _any()],
        out_specs=pl.BlockSpec((tT, tN), lambda i, j: (i, j)),
        out_shape=jax.ShapeDtypeStruct((T, NF), F32),
        compiler_params=_cp(("parallel", "parallel")),
    )(h, wft, after)


def _swa_masks():
    lane = lax.broadcasted_iota(jnp.int32, (BLK, LANE), 1)
    rope_sub0 = ((lane // 32) % 2) == 0
    std_sub0 = lane < 64
    return lane, rope_sub0, std_sub0


def _swa_tri():
    qi = lax.broadcasted_iota(jnp.int32, (BLK, BLK), 0)
    kj = lax.broadcasted_iota(jnp.int32, (BLK, BLK), 1)
    return kj <= qi


def _swa_fold(full, tri):
    return jnp.where(tri, full[:, BLK:], full[:, :BLK])


def _swa_unfold(sq, tri):
    return jnp.concatenate([jnp.where(tri, 0.0, sq), jnp.where(tri, sq, 0.0)], axis=1)


def _swa_keys(kc_ref, kp_ref, vc_ref, vp_ref, cq, sq, cp, sp):
    def ropek(kref, c, s):
        kv = kref[...]
        return jnp.concatenate([_rope(kv[:, :LANE], c, s), _rope(kv[:, LANE:], c, s)], axis=1)

    K = jnp.concatenate([ropek(kp_ref, cp, sp), ropek(kc_ref, cq, sq)], axis=0).astype(MXU)
    V = jnp.concatenate([vp_ref[...], vc_ref[...]], axis=0).astype(MXU)
    return K, V


def _swa_in_specs(nb, last):
    def cur(n):
        return jnp.minimum(n, last)

    def prev(n):
        return jnp.maximum(cur(n) - 1, 0)

    kd, vd = C_KD // 256, C_VD // 256
    return [
        pl.BlockSpec((BLK, D), lambda n: (cur(n), C_Q // D)),
        pl.BlockSpec((BLK, 256), lambda n: (cur(n), kd)),
        pl.BlockSpec((BLK, 256), lambda n: (prev(n), kd)),
        pl.BlockSpec((BLK, 256), lambda n: (cur(n), vd)),
        pl.BlockSpec((BLK, 256), lambda n: (prev(n), vd)),
        pl.BlockSpec((BLK, LANE), lambda n: (cur(n), 0)),
        pl.BlockSpec((BLK, LANE), lambda n: (cur(n), 0)),
        pl.BlockSpec((BLK, LANE), lambda n: (prev(n), 0)),
        pl.BlockSpec((BLK, LANE), lambda n: (prev(n), 0)),
    ]


def _swa_fwd(proj, cos, sin, sinks):
    T = proj.shape[0]
    nb = T // BLK
    scale = A_HD ** -0.5

    def body(sinks_ref, q_ref, kc_ref, kp_ref, vc_ref, vp_ref, cq_ref, sq_ref, cp_ref, sp_ref, o_ref, l_ref):
        n = pl.program_id(0)
        cq, sq = cq_ref[...], sq_ref[...]
        K, V = _swa_keys(kc_ref, kp_ref, vc_ref, vp_ref, cq, sq, cp_ref[...], sp_ref[...])
        tri = _swa_tri()
        valid = tri | (n > 0)
        lane, rope_sub0, std_sub0 = _swa_masks()
        group = A_HEADS // A_KV
        roped, lses = {}, []

        def products(head):
            pb, sub, g = head // 2, head % 2, head // group
            if sub == 0:
                roped[pb] = _rope(q_ref[:, pb * LANE:(pb + 1) * LANE], cq, sq)
            qm = jnp.where(rope_sub0 if sub == 0 else ~rope_sub0, roped[pb], 0.0).astype(MXU)
            return _dot_nt(qm, K[:, g * LANE:(g + 1) * LANE])

        def softmax(head, s_full):
            s = jnp.where(valid, _swa_fold(s_full, tri) * scale, NEG)
            sink = sinks_ref[0, head]
            m = jnp.maximum(jnp.max(s, axis=1, keepdims=True), sink)
            e = jnp.exp(s - m)
            den = jnp.sum(e, axis=1, keepdims=True) + jnp.exp(sink - m)
            lses.append(m + jnp.log(den))
            return _swa_unfold(e / den, tri).astype(MXU)

        outs = {}
        st1 = {0: products(0), 1: products(1)}
        st2 = {0: softmax(0, st1.pop(0))}
        for head in range(A_HEADS):
            if head + 2 < A_HEADS:
                st1[head + 2] = products(head + 2)
            if head + 1 < A_HEADS:
                st2[head + 1] = softmax(head + 1, st1.pop(head + 1))
            g = head // group
            outs[head] = _dot(st2.pop(head), V[:, g * LANE:(g + 1) * LANE])
            if head % 2 == 1:
                pb = head // 2
                o_ref[:, pb * LANE:(pb + 1) * LANE] = jnp.where(std_sub0, outs[head - 1], outs[head])
        lacc = jnp.zeros((BLK, LANE), F32)
        for head in range(A_HEADS):
            lacc = jnp.where(lane == head, lses[head], lacc)
        l_ref[...] = lacc

    return pl.pallas_call(
        body, name="swa_fwd", grid=(nb,),
        in_specs=[pl.BlockSpec(memory_space=pltpu.SMEM)] + _swa_in_specs(nb, nb - 1),
        out_specs=[pl.BlockSpec((BLK, D), lambda n: (n, 0)), pl.BlockSpec((BLK, LANE), lambda n: (n, 0))],
        out_shape=[jax.ShapeDtypeStruct((T, D), F32), jax.ShapeDtypeStruct((T, LANE), F32)],
        compiler_params=_cp(("parallel",)),
    )(sinks, proj, proj, proj, proj, proj, cos, sin, cos, sin)


def _swa_bwd(proj, cos, sin, sinks, do_a, o_a, lse, after):
    T = proj.shape[0]
    nb = T // BLK
    scale = A_HD ** -0.5

    def body(sinks_ref, q_ref, kc_ref, kp_ref, vc_ref, vp_ref, cq_ref, sq_ref, cp_ref, sp_ref,
             do_ref, o_ref, l_ref, after_ref, dq_ref, dkv_ref, ds_ref, ckv_ref):
        n = pl.program_id(0)

        @pl.when(n == 0)
        def _():
            ckv_ref[...] = jnp.zeros_like(ckv_ref)
            ds_ref[...] = jnp.zeros_like(ds_ref)

        @pl.when(n < nb)
        def _():
            cq, sq, cp, sp = cq_ref[...], sq_ref[...], cp_ref[...], sp_ref[...]
            K, V = _swa_keys(kc_ref, kp_ref, vc_ref, vp_ref, cq, sq, cp, sp)
            tri = _swa_tri()
            valid = tri | (n > 0)
            lane, rope_sub0, std_sub0 = _swa_masks()
            lane_row = lax.broadcasted_iota(jnp.int32, (1, LANE), 1)
            lse_v = l_ref[...]
            dKt = [jnp.zeros((LANE, 2 * BLK), F32) for _ in range(A_KV)]
            dVt = [jnp.zeros((LANE, 2 * BLK), F32) for _ in range(A_KV)]
            dsinks, roped, roped_t, do_t = [], {}, {}, {}
            group = A_HEADS // A_KV
            dim = lax.broadcasted_iota(jnp.int32, (LANE, BLK), 0)
            rope_row0, std_row0 = ((dim // 32) % 2) == 0, dim < 64

            def products(head):
                pb, sub, g = head // 2, head % 2, head // group
                cols = slice(pb * LANE, (pb + 1) * LANE)
                Kg, Vg = K[:, g * LANE:(g + 1) * LANE], V[:, g * LANE:(g + 1) * LANE]
                if sub == 0:
                    roped[pb] = _rope(q_ref[:, cols], cq, sq)
                    roped_t[pb] = roped[pb].T
                    do_t[pb] = do_ref[:, cols].T
                qm = jnp.where(rope_sub0 if sub == 0 else ~rope_sub0, roped[pb], 0.0).astype(MXU)
                qmt = jnp.where(rope_row0 if sub == 0 else ~rope_row0, roped_t[pb], 0.0).astype(MXU)
                dov = jnp.where(std_sub0 if sub == 0 else ~std_sub0, do_ref[:, cols], 0.0)
                dovt = jnp.where(std_row0 if sub == 0 else ~std_row0, do_t[pb], 0.0).astype(MXU)
                delta = jnp.sum(dov * o_ref[:, cols], axis=1, keepdims=True)
                return qmt, dovt, delta, _dot_nt(qm, Kg), _dot_nt(dov.astype(MXU), Vg)

            def scores(head, qmt, dovt, delta, s_full, dp_full):
                lh = jnp.sum(jnp.where(lane == head, lse_v, 0.0), axis=1, keepdims=True)
                p = jnp.where(valid, jnp.exp(_swa_fold(s_full, tri) * scale - lh), 0.0)
                psink = jnp.exp(sinks_ref[0, head] - lh)
                dsinks.append(jnp.sum(-psink * delta, axis=0, keepdims=True))
                dsq = (p * (_swa_fold(dp_full, tri) - delta)) * scale
                return qmt, dovt, _swa_unfold(p, tri).astype(MXU), _swa_unfold(dsq, tri).astype(MXU)

            def grads(head, qmt, dovt, pb16, dsc):
                g = head // group
                dKt[g] = dKt[g] + _dot(qmt, dsc)
                dVt[g] = dVt[g] + _dot(dovt, pb16)
                return _dot(dsc, K[:, g * LANE:(g + 1) * LANE])

            dqs = {}
            st1 = {0: products(0), 1: products(1)}
            st2 = {0: scores(0, *st1.pop(0))}
            for head in range(A_HEADS):
                if head + 2 < A_HEADS:
                    st1[head + 2] = products(head + 2)
                if head + 1 < A_HEADS:
                    st2[head + 1] = scores(head + 1, *st1.pop(head + 1))
                dqs[head] = grads(head, *st2.pop(head))
                if head % 2 == 1:
                    pb = head // 2
                    dqp = jnp.where(rope_sub0, dqs[head - 1], dqs[head])
                    dq_ref[:, pb * LANE:(pb + 1) * LANE] = _rope_bwd(dqp, cq, sq).astype(dq_ref.dtype)
            dsink = jnp.zeros((1, LANE), F32)
            for head in range(A_HEADS):
                dsink = jnp.where(lane_row == head, dsinks[head], dsink)
            dK, dV = [a.T for a in dKt], [a.T for a in dVt]
            prev = ([_rope_bwd(dK[g][:BLK], cp, sp) for g in range(A_KV)] + [dV[g][:BLK] for g in range(A_KV)])
            cur_ = ([_rope_bwd(dK[g][BLK:], cq, sq) for g in range(A_KV)] + [dV[g][BLK:] for g in range(A_KV)])
            dkv_ref[...] = (ckv_ref[...] + jnp.concatenate(prev, axis=1)).astype(dkv_ref.dtype)
            ckv_ref[...] = jnp.concatenate(cur_, axis=1)
            ds_ref[...] = ds_ref[...] + jnp.broadcast_to(dsink, ds_ref.shape)

        @pl.when(n == nb)
        def _():
            dkv_ref[...] = ckv_ref[...].astype(dkv_ref.dtype)

    last = nb - 1

    def cur(n):
        return jnp.minimum(n, last)

    def out_kv(n):
        return (jnp.maximum(n - 1, 0), 0)

    return pl.pallas_call(
        body, name="swa_bwd", grid=(nb + 1,),
        in_specs=[pl.BlockSpec(memory_space=pltpu.SMEM)] + _swa_in_specs(nb, last) + [
            pl.BlockSpec((BLK, D), lambda n: (cur(n), 0)),
            pl.BlockSpec((BLK, D), lambda n: (cur(n), 0)),
            pl.BlockSpec((BLK, LANE), lambda n: (cur(n), 0)),
            _any(),
        ],
        out_specs=[
            pl.BlockSpec((BLK, D), lambda n: (cur(n), 0)),
            pl.BlockSpec((BLK, 512), out_kv),
            pl.BlockSpec((8, LANE), lambda n: (0, 0)),
        ],
        out_shape=[
            jax.ShapeDtypeStruct((T, D), MXU),
            jax.ShapeDtypeStruct((T, 512), MXU),
            jax.ShapeDtypeStruct((8, LANE), F32),
        ],
        scratch_shapes=[pltpu.VMEM((BLK, 512), F32)],
        compiler_params=_cp(("arbitrary",)),
    )(sinks, proj, proj, proj, proj, proj, cos, sin, cos, sin, do_a, o_a, lse, after)


GSTEP = 2 * CHUNK
ST_ROWS = B_HEADS * B_DV


def _gla_gate(bl_ref, gu_ref, bias_ref):
    gk = _dot(bl_ref[...].astype(MXU), gu_ref[...]) + bias_ref[...]
    la = (jnp.minimum(gk, 0.0) - jnp.log(1.0 + jnp.exp(-jnp.abs(gk)))) / TAU
    ri = lax.broadcasted_iota(jnp.int32, (GSTEP, GSTEP), 0)
    ci = lax.broadcasted_iota(jnp.int32, (GSTEP, GSTEP), 1)
    same = (ri // CHUNK) == (ci // CHUNK)
    lower, upper = same & (ci <= ri), same & (ci >= ri)
    b = _dot_f32(jnp.where(lower, 1.0, 0.0).astype(F32), la)
    first = lax.broadcasted_iota(jnp.int32, (GSTEP, 1), 0) < CHUNK
    return gk, la, b, lower, upper, first


def _gla_head(q_ref, k_ref, la, b, first, h):
    sl = slice(h * B_DK, (h + 1) * B_DK)
    bh, lah = b[:, sl], la[:, sl]
    bl_a = jnp.sum(lah[:CHUNK], axis=0, keepdims=True)
    bl_b = jnp.sum(lah[CHUNK:], axis=0, keepdims=True)
    blast = jnp.where(first, bl_a, bl_b)
    qc = q_ref[:, sl] * (B_DK ** -0.5)
    kh = k_ref[:, sl]
    eb, enb, esb = jnp.exp(bh), jnp.exp(-bh), jnp.exp(blast - bh)
    return qc * eb, kh * enb, kh * esb, eb, enb, esb, (jnp.exp(bl_a), jnp.exp(bl_b))


def _gla_specs(step_of):
    return [
        pl.BlockSpec((GSTEP, 512), lambda i: (step_of(i), C_BQ // 512)),
        pl.BlockSpec((GSTEP, 512), lambda i: (step_of(i), C_BK // 512)),
        pl.BlockSpec((GSTEP, D), lambda i: (step_of(i), C_BV // D)),
        pl.BlockSpec((GSTEP, W_BL), lambda i: (step_of(i), C_BL // W_BL)),
        pl.BlockSpec((W_BL, 512), lambda i: (0, 0)),
        pl.BlockSpec((1, 512), lambda i: (0, 0)),
    ]


def _gla_fwd(proj, gu_pad, bias):
    T = proj.shape[0]
    ns = T // GSTEP

    def body(q_ref, k_ref, v_ref, bl_ref, gu_ref, bias_ref, o_ref, st_ref, state_ref):
        @pl.when(pl.program_id(0) == 0)
        def _():
            state_ref[...] = jnp.zeros_like(state_ref)

        _, la, b, lower, _, first = _gla_gate(bl_ref, gu_ref, bias_ref)
        st_ref[0:ST_ROWS, :] = state_ref[...]

        def within(h):
            q_e, k_e, k_s, _, _, _, decays = _gla_head(q_ref, k_ref, la, b, first, h)
            vh = v_ref[:, h * B_DV:(h + 1) * B_DV].astype(MXU)
            q_eb = q_e.astype(MXU)
            att = jnp.where(lower, _dot_nt(q_eb, k_e.astype(MXU)), 0.0)
            return vh, q_eb, k_s.astype(MXU), _dot(att.astype(MXU), vh), decays

        def across(h, vh, q_eb, k_sb, o_intra, decays):
            rows = slice(h * B_DV, (h + 1) * B_DV)
            s0 = state_ref[rows, :]
            o_a = o_intra[:CHUNK] + _dot_nt(q_eb[:CHUNK], s0.astype(MXU))
            s1 = s0 * decays[0] + _dot_tn(vh[:CHUNK], k_sb[:CHUNK])
            st_ref[ST_ROWS + h * B_DV:ST_ROWS + (h + 1) * B_DV, :] = s1
            o_b = o_intra[CHUNK:] + _dot_nt(q_eb[CHUNK:], s1.astype(MXU))
            state_ref[rows, :] = s1 * decays[1] + _dot_tn(vh[CHUNK:], k_sb[CHUNK:])
            o_ref[:, rows] = jnp.concatenate([o_a, o_b], axis=0)

        for h in range(B_HEADS):
            across(h, *within(h))

    return pl.pallas_call(
        body, name="gla_fwd", grid=(ns,),
        in_specs=_gla_specs(lambda i: i),
        out_specs=[pl.BlockSpec((GSTEP, D), lambda i: (i, 0)),
                   pl.BlockSpec((2 * ST_ROWS, B_DK), lambda i: (i, 0))],
        out_shape=[jax.ShapeDtypeStruct((T, D), F32),
                   jax.ShapeDtypeStruct((ns * 2 * ST_ROWS, B_DK), F32)],
        scratch_shapes=[pltpu.VMEM((ST_ROWS, B_DK), F32)],
        compiler_params=_cp(("arbitrary",)),
    )(proj, proj, proj, proj, gu_pad, bias)


def _gla_bwd(proj, gu_pad, bias, states, do_b):
    T = proj.shape[0]
    ns = T // GSTEP
    o_q, o_k = C_BQ - C_GLA, C_BK - C_GLA

    def body(q_ref, k_ref, v_ref, bl_ref, gu_ref, bias_ref, st_ref, do_ref,
             dg_ref, dbl_ref, ggu_ref, gbias_ref, gt_ref):
        @pl.when(pl.program_id(0) == 0)
        def _():
            gt_ref[...] = jnp.zeros_like(gt_ref)
            ggu_ref[...] = jnp.zeros_like(ggu_ref)
            gbias_ref[...] = jnp.zeros_like(gbias_ref)

        gk, la, b, lower, upper_mask, first = _gla_gate(bl_ref, gu_ref, bias_ref)
        upper = jnp.where(upper_mask, 1.0, 0.0).astype(F32)
        lo, hi = slice(0, CHUNK), slice(CHUNK, GSTEP)
        dla_parts = []

        def within(h):
            q_e, k_e, k_s, eb, enb, esb, decays = _gla_head(q_ref, k_ref, la, b, first, h)
            vh = v_ref[:, h * B_DV:(h + 1) * B_DV].astype(MXU)
            doh = do_ref[:, h * B_DV:(h + 1) * B_DV].astype(MXU)
            q_eb, k_eb = q_e.astype(MXU), k_e.astype(MXU)
            att = jnp.where(lower, _dot_nt(q_eb, k_eb), 0.0).astype(MXU)
            datt = jnp.where(lower, _dot_nt(doh, vh), 0.0).astype(MXU)
            return (q_e, k_e, k_s, eb, enb, esb, decays, vh, doh, q_eb, k_s.astype(MXU),
                    _dot(datt, k_eb), _dot_tn(datt, q_eb), _dot_tn(att, doh))

        def across(h, q_e, k_e, k_s, eb, enb, esb, decays, vh, doh, q_eb, k_sb, dq_i, dk_e, dv_i):
            dec_a, dec_b = decays
            rows = slice(h * B_DV, (h + 1) * B_DV)
            s0 = st_ref[rows, :]
            s1 = st_ref[ST_ROWS + h * B_DV:ST_ROWS + (h + 1) * B_DV, :]
            g2 = gt_ref[rows, :]
            g2b = g2.astype(MXU)
            dq_b = dq_i[hi] + _dot(doh[hi], s1.astype(MXU))
            dks_b = _dot(vh[hi], g2b)
            dv_b = dv_i[hi] + _dot_nt(k_sb[hi], g2b)
            ddec_b = jnp.sum(g2 * s1, axis=0, keepdims=True)
            g1 = g2 * dec_b + _dot_tn(doh[hi], q_eb[hi])
            g1b = g1.astype(MXU)
            dq_a = dq_i[lo] + _dot(doh[lo], s0.astype(MXU))
            dks_a = _dot(vh[lo], g1b)
            dv_a = dv_i[lo] + _dot_nt(k_sb[lo], g1b)
            ddec_a = jnp.sum(g1 * s0, axis=0, keepdims=True)
            gt_ref[rows, :] = g1 * dec_a + _dot_tn(doh[lo], q_eb[lo])
            dq_e = jnp.concatenate([dq_a, dq_b], axis=0)
            dk_s = jnp.concatenate([dks_a, dks_b], axis=0)
            dg_ref[:, rows] = jnp.concatenate([dv_a, dv_b], axis=0).astype(dg_ref.dtype)
            dg_ref[:, o_q + h * B_DK:o_q + (h + 1) * B_DK] = (dq_e * eb * (B_DK ** -0.5)).astype(dg_ref.dtype)
            dg_ref[:, o_k + h * B_DK:o_k + (h + 1) * B_DK] = (dk_e * enb + dk_s * esb).astype(dg_ref.dtype)
            dks_ks = dk_s * k_s
            db = dq_e * q_e - dk_e * k_e - dks_ks
            dbl_a = jnp.sum(dks_ks[lo], axis=0, keepdims=True) + ddec_a * dec_a
            dbl_b = jnp.sum(dks_ks[hi], axis=0, keepdims=True) + ddec_b * dec_b
            dla_parts.append(_dot_f32(upper, db) + jnp.where(first, dbl_a, dbl_b))

        for h in range(B_HEADS):
            across(h, *within(h))
        dla = jnp.concatenate(dla_parts, axis=1)
        dgk = dla * (1.0 / TAU) * _sigmoid(-gk)
        dgkb = dgk.astype(MXU)
        dbl_ref[...] = _dot_nt(dgkb, gu_ref[...]).astype(dbl_ref.dtype)
        ggu_ref[...] = ggu_ref[...] + _dot_tn(bl_ref[...].astype(MXU), dgkb)
        gbias_ref[...] = gbias_ref[...] + jnp.broadcast_to(jnp.sum(dgk, axis=0, keepdims=True), gbias_ref.shape)

    def rev(i):
        return ns - 1 - i

    return pl.pallas_call(
        body, name="gla_bwd", grid=(ns,),
        in_specs=_gla_specs(rev) + [
            pl.BlockSpec((2 * ST_ROWS, B_DK), lambda i: (rev(i), 0)),
            pl.BlockSpec((GSTEP, D), lambda i: (rev(i), 0)),
        ],
        out_specs=[
            pl.BlockSpec((GSTEP, W_GLA), lambda i: (rev(i), 0)),
            pl.BlockSpec((GSTEP, W_BL), lambda i: (rev(i), 0)),
            pl.BlockSpec((W_BL, 512), lambda i: (0, 0)),
            pl.BlockSpec((8, 512), lambda i: (0, 0)),
        ],
        out_shape=[
            jax.ShapeDtypeStruct((T, W_GLA), MXU),
            jax.ShapeDtypeStruct((T, W_BL), MXU),
            jax.ShapeDtypeStruct((W_BL, 512), F32),
            jax.ShapeDtypeStruct((8, 512), F32),
        ],
        scratch_shapes=[pltpu.VMEM((B_HEADS * B_DV, B_DK), F32)],
        compiler_params=_cp(("arbitrary",)),
    )(proj, proj, proj, proj, gu_pad, bias, states, do_b)


def _mid(x, target, proj, o_a, o_b, w_a, w_b, w_out, w_bn4, fnw):
    T = x.shape[0]
    tT = min(T, 128)
    nbuf = 4
    o_ag, o_bg, o_ma, o_mb = (c - C_GATES for c in (C_AG, C_BG, C_MA, C_MB))

    def body(x_ref, t_ref, oa_ref, ob_ref, gates_ref, wa_ref, wb_ref, wo_ref, wbn_ref, fnw_ref,
             dx2_ref, doa_ref, dob_ref, dgates_ref,
             gwa_ref, gwb_ref, gwo_ref, gfn_ref, gbn_ref, loss_ref, buf_ref):
        i = pl.program_id(0)

        @pl.when(i == 0)
        def _():
            for r in (gwa_ref, gwb_ref, gwo_ref, gfn_ref, gbn_ref, loss_ref):
                r[...] = jnp.zeros_like(r)

        rows = pl.ds(pl.multiple_of((i % nbuf) * tT, tT), tT)

        def keep(k, val):
            buf_ref[k, rows, :] = val

        oa, ag = oa_ref[...], gates_ref[:, o_ag:o_ag + D]
        sg_a = _sigmoid(ag)
        silu_a = ag * sg_a
        oag_b = (oa * silu_a).astype(MXU)
        keep(0, oag_b)
        y_a = _dot(oag_b, wa_ref[...])

        ob, bg = ob_ref[...], gates_ref[:, o_bg:o_bg + D]
        rbs, obhats = [], []
        for h in range(B_HEADS):
            obh = ob[:, h * B_DV:(h + 1) * B_DV]
            rb = lax.rsqrt(jnp.mean(obh * obh, axis=-1, keepdims=True) + EPS)
            rbs.append(rb)
            obhats.append(obh * rb)
        obhat = jnp.concatenate(obhats, axis=1)
        wbn = wbn_ref[...]
        obn = obhat * wbn
        sg_b = _sigmoid(bg)
        silu_b = bg * sg_b
        obg_b = (obn * silu_b).astype(MXU)
        keep(1, obg_b)
        y_b = _dot(obg_b, wb_ref[...])

        sa, sb = _sigmoid(gates_ref[:, o_ma:o_ma + D]), _sigmoid(gates_ref[:, o_mb:o_mb + D])
        mg_b = (sa * y_a + sb * y_b).astype(MXU)
        keep(2, mg_b)
        x2 = x_ref[...] + _dot(mg_b, wo_ref[...])
        r2 = lax.rsqrt(jnp.mean(x2 * x2, axis=-1, keepdims=True) + EPS)
        xh2 = x2 * r2
        fw = fnw_ref[...]
        err = xh2 * fw - t_ref[...]
        tok = jnp.mean(err * err, axis=-1, keepdims=True)
        loss_ref[...] = loss_ref[...] + 0.5 * jnp.sum(tok, axis=0, keepdims=True)

        dy = err * (1.0 / D)
        gfn_ref[...] = gfn_ref[...] + jnp.broadcast_to(jnp.sum(dy * xh2, axis=0, keepdims=True), gfn_ref.shape)
        gy = dy * fw
        dx2 = r2 * (gy - xh2 * jnp.mean(gy * xh2, axis=-1, keepdims=True))
        dx2_ref[...] = dx2
        dx2_b = dx2.astype(MXU)
        keep(5, dx2_b)
        dmg = _dot_nt(dx2_b, wo_ref[...])

        dgates_ref[:, o_ma:o_ma + D] = (dmg * y_a * sa * (1.0 - sa)).astype(dgates_ref.dtype)
        dgates_ref[:, o_mb:o_mb + D] = (dmg * y_b * sb * (1.0 - sb)).astype(dgates_ref.dtype)
        dya_b = (dmg * sa).astype(MXU)
        dyb_b = (dmg * sb).astype(MXU)
        keep(3, dya_b)
        keep(4, dyb_b)
        doag = _dot_nt(dya_b, wa_ref[...])
        dobg = _dot_nt(dyb_b, wb_ref[...])

        @pl.when(i % nbuf == nbuf - 1)
        def _():
            gwa_ref[...] = gwa_ref[...] + _dot_tn(buf_ref[0], buf_ref[3])
            gwb_ref[...] = gwb_ref[...] + _dot_tn(buf_ref[1], buf_ref[4])
            gwo_ref[...] = gwo_ref[...] + _dot_tn(buf_ref[2], buf_ref[5])

        doa_ref[...] = doag * silu_a
        dgates_ref[:, o_ag:o_ag + D] = (doag * oa * (sg_a * (1.0 + ag * (1.0 - sg_a)))).astype(dgates_ref.dtype)
        dobn = dobg * silu_b
        dgates_ref[:, o_bg:o_bg + D] = (dobg * obn * (sg_b * (1.0 + bg * (1.0 - sg_b)))).astype(dgates_ref.dtype)
        gg = dobn * wbn
        gbn = jnp.zeros((1, B_DV), F32)
        for h in range(B_HEADS):
            sl = slice(h * B_DV, (h + 1) * B_DV)
            gbn = gbn + jnp.sum(dobn[:, sl] * obhats[h], axis=0, keepdims=True)
            ggh = gg[:, sl]
            dob_ref[:, sl] = rbs[h] * (ggh - obhats[h] * jnp.mean(ggh * obhats[h], axis=-1, keepdims=True))
        gbn_ref[...] = gbn_ref[...] + jnp.broadcast_to(gbn, gbn_ref.shape)

    assert (T // tT) % nbuf == 0
    tile = pl.BlockSpec((tT, D), lambda i: (i, 0))
    row = pl.BlockSpec((1, D), lambda i: (0, 0))
    acc8 = pl.BlockSpec((8, D), lambda i: (0, 0))
    return pl.pallas_call(
        body, name="mid", grid=(T // tT,),
        in_specs=[tile, tile, tile, tile, pl.BlockSpec((tT, W_GATES), lambda i: (i, C_GATES // W_GATES)),
                  _vmem(), _vmem(), _vmem(), row, row],
        out_specs=[tile, tile, tile, pl.BlockSpec((tT, W_GATES), lambda i: (i, 0)), _vmem(), _vmem(), _vmem(),
                   acc8, pl.BlockSpec((8, B_DV), lambda i: (0, 0)), pl.BlockSpec((8, LANE), lambda i: (0, 0))],
        out_shape=[
            jax.ShapeDtypeStruct((T, D), F32),
            jax.ShapeDtypeStruct((T, D), F32),
            jax.ShapeDtypeStruct((T, D), F32),
            jax.ShapeDtypeStruct((T, W_GATES), MXU),
            jax.ShapeDtypeStruct((D, D), F32),
            jax.ShapeDtypeStruct((D, D), F32),
            jax.ShapeDtypeStruct((D, D), F32),
            jax.ShapeDtypeStruct((8, D), F32),
            jax.ShapeDtypeStruct((8, B_DV), F32),
            jax.ShapeDtypeStruct((8, LANE), F32),
        ],
        scratch_shapes=[pltpu.VMEM((6, nbuf * tT, D), MXU)],
        compiler_params=_cp(("arbitrary",)),
    )(x, target, o_a, o_b, proj, w_a, w_b, w_out, w_bn4, fnw)


DH = D // 2


_GW_TILES = (("q", 0, 512, 0), ("q", 1, 512, 512), ("kv", 0, 256, 1024), ("bl", 0, RANK, 5376),
             ("gla", 0, 512, 3328), ("gla", 1, 512, 3840), ("gla", 2, 512, 2304), ("gla", 3, 512, 2816),
             ("gates", 0, 512, 1280), ("gates", 1, 512, 1792), ("gates", 2, 512, 4352), ("gates", 3, 512, 4864),
             ("gates", 4, 512, 5392), ("gates", 5, 512, 5904), ("gates", 6, 512, 6416), ("gates", 7, 512, 6928))


def _gw_unpermute(piece, t):
    if piece == "q":
        parts = []
        for blk in range(t.shape[0] // LANE):
            g = [t[blk * LANE + 32 * i:blk * LANE + 32 * (i + 1)] for i in range(4)]
            parts += [g[0], g[2], g[1], g[3]]
        return jnp.concatenate(parts, axis=0)
    if piece == "kv":
        k = [t[64 * i:64 * i + 32] + t[64 * i + 32:64 * i + 64] for i in range(4)]
        v = [t[256 + 128 * g:256 + 128 * g + 64] + t[256 + 128 * g + 64:256 + 128 * (g + 1)] for g in range(2)]
        return jnp.concatenate(k + v, axis=0)
    if piece == "bl":
        return t[:RANK]
    return t


def _gw_half(h, pieces, half, after=None):
    T = h.shape[0]
    steps = len(_GW_TILES)

    def body(*refs):
        h_ref = refs[0]
        srcs = dict(zip(("q", "kv", "bl", "gla", "gates"), refs[1:6]))
        o_ref, stage, sems = refs[-3:]
        j = pl.program_id(0)

        def out_copy(k):
            _, _, n, off = _GW_TILES[k]
            return pltpu.make_async_copy(stage.at[k % 2, 0:n], o_ref.at[pl.ds(off, n)], sems.at[k % 2])

        for k, (piece, _, n, _) in enumerate(_GW_TILES):
            @pl.when(j == k)
            def _(k=k, piece=piece, n=n):
                if k >= 2:
                    out_copy(k - 2).wait()
                t = _gw_unpermute(piece, _dot_tn(srcs[piece][...], h_ref[...]))
                stage[k % 2, 0:n, :] = t.astype(stage.dtype)
                out_copy(k).start()

        @pl.when(j == steps - 1)
        def _():
            out_copy(steps - 2).wait()
            out_copy(steps - 1).wait()

    def tile_of(lo, hi):
        return lambda j: (0, jnp.clip(j - lo, 0, hi - lo - 1))

    in_specs = [pl.BlockSpec((T, DH), lambda j: (0, half)),
                pl.BlockSpec((T, 512), tile_of(0, 2)), pl.BlockSpec((T, 512), lambda j: (0, 0)),
                pl.BlockSpec((T, W_BL), lambda j: (0, 0)),
                pl.BlockSpec((T, 512), tile_of(4, 8)), pl.BlockSpec((T, 512), tile_of(8, 16))]
    args = [h, *pieces]
    if after is not None:
        in_specs.append(_any())
        args.append(after)
    return pl.pallas_call(
        body, name=f"gw_in_half{half}", grid=(steps,),
        in_specs=in_specs, out_specs=_any(),
        out_shape=jax.ShapeDtypeStruct((IN_WIDTH, DH), WIRE),
        scratch_shapes=[pltpu.VMEM((2, 512, DH), WIRE), pltpu.SemaphoreType.DMA((2,))],
        compiler_params=_cp(("arbitrary",)),
    )(*args)


def _chip_copies(s_ref, got_ref, send_sems, recv_sems):
    x, y, c = _place()
    chips = [(1 - x, y), (x, 1 - y), (1 - x, 1 - y)]
    return [pltpu.make_async_remote_copy(
        src_ref=s_ref.at[2 * px + py], dst_ref=got_ref.at[j],
        send_sem=send_sems.at[j], recv_sem=recv_sems.at[j], device_id=(px, py, c), device_id_type=MESH)
        for j, (px, py) in enumerate(chips)]


_EFFECT = pltpu.SideEffectType.DATAFLOW_SIDE_EFFECTING


def _hbm():
    return pl.BlockSpec(memory_space=pltpu.HBM)


def _sem():
    return pl.BlockSpec(memory_space=pltpu.SEMAPHORE)


def _chip_start(sums, half):
    land = pltpu.with_memory_space_constraint(lax.empty((3,) + sums.shape[1:], sums.dtype), pltpu.HBM)

    def body(s_ref, land_ref, send_sems, recv_sems, s_thru, land_thru, token):
        for cp in _chip_copies(s_ref, land_ref, send_sems, recv_sems):
            cp.start()
        token[...] = jnp.zeros_like(token)

    return pl.pallas_call(
        body, name=f"chip_start{half}",
        out_shape=(pltpu.SemaphoreType.DMA((3,)), pltpu.SemaphoreType.DMA((3,)),
                   pltpu.HBM(sums.shape, sums.dtype), pltpu.HBM(land.shape, land.dtype),
                   jax.ShapeDtypeStruct((8, LANE), F32)),
        in_specs=(_hbm(), _hbm()), out_specs=(_sem(), _sem(), _hbm(), _hbm(), _vmem()),
        input_output_aliases={0: 2, 1: 3},
        compiler_params=pltpu.CompilerParams(has_side_effects=_EFFECT),
    )(pltpu.with_memory_space_constraint(sums, pltpu.HBM), land)


def _chip_wait(send_sems, recv_sems, s_thru, land_thru, after, half):
    def body(s_ref, land_ref, send_sems, recv_sems, after_ref, s_out, got_ref):
        copies = _chip_copies(s_ref, land_ref, send_sems, recv_sems)
        for cp in copies:
            cp.wait_send()
        for cp in copies:
            cp.wait_recv()

    return pl.pallas_call(
        body, name=f"chip_wait{half}",
        out_shape=(pltpu.HBM(s_thru.shape, s_thru.dtype), pltpu.HBM(land_thru.shape, land_thru.dtype)),
        in_specs=(_hbm(), _hbm(), _sem(), _sem(), _any()), out_specs=(_hbm(), _hbm()),
        input_output_aliases={0: 0, 1: 1},
        compiler_params=pltpu.CompilerParams(has_side_effects=_EFFECT),
    )(s_thru, land_thru, send_sems, recv_sems, after)


def _dh_norm(pieces, offsets, wf, x, dx2, norm_w, after):
    T = x.shape[0]
    tT = min(T, 256)
    widths = [p.shape[1] for p in pieces]
    npc = len(pieces)

    def body(*refs):
        dp_refs = refs[:npc]
        wf_ref, x_ref, dx2_ref, nw_ref, _, gx_ref, gnw_ref = refs[npc:]

        @pl.when(pl.program_id(0) == 0)
        def _():
            gnw_ref[...] = jnp.zeros_like(gnw_ref)

        dh = jnp.zeros((tT, D), F32)
        for dp_ref, off, w in zip(dp_refs, offsets, widths):
            dh = dh + _dot(dp_ref[...], wf_ref[off:off + w, :])
        xv = x_ref[...]
        r = lax.rsqrt(jnp.mean(xv * xv, axis=-1, keepdims=True) + EPS)
        xh = xv * r
        gnw_ref[...] = gnw_ref[...] + jnp.broadcast_to(jnp.sum(dh * xh, axis=0, keepdims=True), gnw_ref.shape)
        g = dh * nw_ref[...]
        gx_ref[...] = r * (g - xh * jnp.mean(g * xh, axis=-1, keepdims=True)) + dx2_ref[...]

    tile = pl.BlockSpec((tT, D), lambda i: (i, 0))
    return pl.pallas_call(
        body, name="dh_norm", grid=(T // tT,),
        in_specs=[pl.BlockSpec((tT, w), lambda i: (i, 0)) for w in widths]
        + [_vmem(), tile, tile, pl.BlockSpec((1, D), lambda i: (0, 0)), _any()],
        out_specs=[tile, pl.BlockSpec((8, D), lambda i: (0, 0))],
        out_shape=[jax.ShapeDtypeStruct((T, D), F32), jax.ShapeDtypeStruct((8, D), F32)],
        compiler_params=_cp(("arbitrary",)),
    )(*pieces, wf, x, dx2, norm_w, after)


def _adamw_math(w, g, m, v):
    m = ADAM_B1 * m + (1.0 - ADAM_B1) * g
    v = ADAM_B2 * v + (1.0 - ADAM_B2) * (g * g)
    m_hat = m * (1.0 / (1.0 - ADAM_B1 ** ADAM_STEP))
    v_hat = v * (1.0 / (1.0 - ADAM_B2 ** ADAM_STEP))
    delta = -ADAM_LR * (m_hat / (jnp.sqrt(v_hat) + ADAM_EPS) + ADAM_WD * w)
    return delta, m, v


def _fetch_partials(s_ref, got_ref, buf, sems):
    x, y, _ = _place()
    cps = [pltpu.make_async_copy(s_ref.at[2 * x + y], buf.at[0], sems.at[0])]
    cps += [pltpu.make_async_copy(got_ref.at[j], buf.at[1 + j], sems.at[1 + j]) for j in range(3)]
    for cp in cps:
        cp.start()
    for cp in cps:
        cp.wait()


SMALL_AT = dict(norm_w=0, fnw=8, bias=16, bn=24, sinks=32, loss=40)
ROW_AT = (R_IN, R_A, R_B, R_O)


def _finish_small(ws, ms, vs, smalls):
    names = ["norm_w", "fnw", "bias", "bn", "sinks"]
    widths = [ws[n].shape[1] for n in names]

    def body(*refs):
        w_refs, m_refs, v_refs = refs[0:5], refs[5:10], refs[10:15]
        smalls_ref, loss_ref = refs[15], refs[16]
        outs, tot = refs[17:37], refs[37]
        acc = smalls_ref[0]
        for d in range(1, NDEV):
            acc = acc + smalls_ref[d]
        tot[...] = acc
        loss_ref[...] = tot[SMALL_AT["loss"]:SMALL_AT["loss"] + 1, 0:1]
        for p, (nm_, wd) in enumerate(zip(names, widths)):
            r = SMALL_AT[nm_]
            g = tot[r:r + 1, 0:wd]
            d, nm, nv = _adamw_math(w_refs[p][...], g, m_refs[p][...], v_refs[p][...])
            for o, val in zip(outs[4 * p:4 * p + 4], (g, d, nm, nv)):
                o[...] = val

    res = pl.pallas_call(
        body, name="finish_small",
        in_specs=[_vmem()] * 16, out_specs=[_vmem()] * 21,
        out_shape=[jax.ShapeDtypeStruct((1, 1), F32)]
        + [jax.ShapeDtypeStruct((1, wd), F32) for wd in widths for _ in range(4)],
        scratch_shapes=[pltpu.VMEM((SMALL_ROWS, D), F32)],
        compiler_params=_cp(),
    )(*[ws[n] for n in names], *[ms[n] for n in names], *[vs[n] for n in names], smalls)
    return res[0], {n: tuple(res[1 + 4 * p:5 + 4 * p]) for p, n in enumerate(names)}


def _finish(w_rows, m_rows, v_rows, gu_w, gu_m, gu_v, sums, got):
    shapes = [w.shape for w in w_rows]

    def body(*refs):
        wr_refs, mr_refs, vr_refs = refs[0:4], refs[4:8], refs[8:12]
        guw_ref, gum_ref, guv_ref = refs[12:15]
        s_refs, got_refs = refs[15:17], refs[17:19]
        row_outs = refs[19:35]
        gu_outs = refs[35:39]
        buf, gsh, sems = refs[39:]
        x, y, c = _place()
        me_slot = 4 * x + 2 * y + c
        unshift = lax.rem(SHARD_PAD - 2 * me_slot, SHARD_PAD)

        def total(rows, cols):
            g = buf[0, rows, cols].astype(F32)
            for j in range(1, 4):
                g = g + buf[j, rows, cols].astype(F32)
            return g

        def update(p, g, cols):
            d, nm, nv = _adamw_math(wr_refs[p][:, cols], g, mr_refs[p][:, cols], vr_refs[p][:, cols])
            for o, val in zip(row_outs[4 * p:4 * p + 4], (g, d, nm, nv)):
                o[:, cols] = val

        for hf in range(2):
            _fetch_partials(s_refs[hf], got_refs[hf], buf, sems)
            for cc in range(DH // LANE):
                src = slice(cc * LANE, (cc + 1) * LANE)
                cols = slice(hf * DH + cc * LANE, hf * DH + (cc + 1) * LANE)
                gsh[...] = pltpu.roll(total(slice(0, SHARD_PAD), src), unshift, 0)
                update(0, gsh[0:SHARD, :], cols)
                for p in range(1, 4):
                    update(p, total(slice(ROW_AT[p], ROW_AT[p] + 128), src), cols)
            if hf == 0:
                g = total(slice(R_GU, R_GU + RANK), slice(0, 64))
                d, nm, nv = _adamw_math(guw_ref[...], g, gum_ref[...], guv_ref[...])
                for o, val in zip(gu_outs, (g, d, nm, nv)):
                    o[...] = val

    res = pl.pallas_call(
        body, name="finish",
        in_specs=[_vmem()] * 15 + [_any()] * 4,
        out_specs=[_vmem()] * 20,
        out_shape=[jax.ShapeDtypeStruct(s, F32) for s in shapes for _ in range(4)]
        + [jax.ShapeDtypeStruct((RANK, 64), F32)] * 4,
        scratch_shapes=[pltpu.VMEM((4, ROWS, DH), sums[0].dtype), pltpu.VMEM((SHARD_PAD, LANE), F32),
                        pltpu.SemaphoreType.DMA((4,))],
        compiler_params=_cp(),
    )(*w_rows, *m_rows, *v_rows, gu_w, gu_m, gu_v, *sums, *got)
    return tuple(res[0:16]), tuple(res[16:20])


def _place():
    x, y, c = lax.axis_index("x"), lax.axis_index("y"), lax.axis_index("c")
    return x, y, c


def _peers(x, y, c):
    return [(x ^ dx, y ^ dy, c ^ dc) for dx in range(2) for dy in range(2) for dc in range(2) if dx + dy + dc]


def _late_gather_start(blk, after, name="late_gather"):
    land = pltpu.with_memory_space_constraint(lax.empty((NDEV,) + blk.shape, blk.dtype), pltpu.HBM)

    def body(b_ref, land_ref, after_ref, send_sems, recv_sems, b_thru, land_thru, token):
        x, y, c = _place()
        for k, to in enumerate(_peers(x, y, c)):
            pltpu.make_async_remote_copy(
                src_ref=b_ref, dst_ref=land_ref.at[4 * x + 2 * y + c], send_sem=send_sems.at[k],
                recv_sem=recv_sems.at[k], device_id=to, device_id_type=MESH).start()
        token[...] = jnp.zeros_like(token)

    return pl.pallas_call(
        body, name=name + "_start",
        out_shape=(pltpu.SemaphoreType.DMA((7,)), pltpu.SemaphoreType.DMA((7,)),
                   pltpu.HBM(blk.shape, blk.dtype), pltpu.HBM(land.shape, land.dtype),
                   jax.ShapeDtypeStruct((8, LANE), F32)),
        in_specs=(_hbm(), _hbm(), _any()), out_specs=(_sem(), _sem(), _hbm(), _hbm(), _vmem()),
        input_output_aliases={0: 2, 1: 3},
        compiler_params=pltpu.CompilerParams(has_side_effects=_EFFECT),
    )(pltpu.with_memory_space_constraint(blk, pltpu.HBM), land, after)


def _late_gather_wait(send_sems, recv_sems, b_thru, land_thru, after, after2, name="late_gather"):
    def body(b_ref, land_ref, send_sems, recv_sems, after_ref, after2_ref, b_out, got_ref):
        x, y, c = _place()
        copies = [pltpu.make_async_remote_copy(
            src_ref=b_ref, dst_ref=land_ref.at[4 * x + 2 * y + c], send_sem=send_sems.at[k],
            recv_sem=recv_sems.at[k], device_id=to, device_id_type=MESH)
            for k, to in enumerate(_peers(x, y, c))]
        for cp in copies:
            cp.wait_send()
        for cp in copies:
            cp.wait_recv()

    return pl.pallas_call(
        body, name=name + "_wait",
        out_shape=(pltpu.HBM(b_thru.shape, b_thru.dtype), pltpu.HBM(land_thru.shape, land_thru.dtype)),
        in_specs=(_hbm(), _hbm(), _sem(), _sem(), _any(), _any()), out_specs=(_hbm(), _hbm()),
        input_output_aliases={0: 0, 1: 1},
        compiler_params=pltpu.CompilerParams(has_side_effects=_EFFECT),
    )(b_thru, land_thru, send_sems, recv_sems, after, after2)


G_ROWS = SHARD_PAD + RANK


def _gather_blocks(w_in_t, gu_s, xs, norm_w, pos_col):
    rows, cols = G_ROWS, D
    T = xs.shape[0]
    tT = min(T, 256)
    inv_row, sign_row = _rope_rows()

    def body(wi_ref, gu_ref, xs_ref, nw_ref, pos_ref, inv_ref, sign_ref,
             out_ref, h_ref, cos_ref, sin_ref, x_ref, frame_ref, send_sems, recv_sems, local_sem):
        x, y, c = _place()
        me, sibling = (x, y, c), (x, y, 1 - c)
        chips = [(1 - x, y), (x, 1 - y), (1 - x, 1 - y)]
        shift = 2 * (4 * x + 2 * y + c)
        frame_ref[SHARD - SHARD % 8:, :] = jnp.zeros((SHARD_PAD - SHARD + SHARD % 8, D), F32)
        frame_ref[:SHARD, :] = wi_ref[...]
        for cc in range(D // LANE):
            cs = slice(cc * LANE, (cc + 1) * LANE)
            x_ref[0:SHARD_PAD, cs] = pltpu.roll(frame_ref[:, cs], shift, 0).astype(x_ref.dtype)
        x_ref[SHARD_PAD:G_ROWS, :] = jnp.zeros((RANK, D), x_ref.dtype)
        x_ref[SHARD_PAD:G_ROWS, 0:64] = gu_ref[...].astype(x_ref.dtype)

        def slot(px, py, pc):
            return out_ref.at[4 * px + 2 * py + pc]

        def copy(k, block, to, src=None):
            return pltpu.make_async_remote_copy(
                src_ref=slot(*block) if src is None else src, dst_ref=slot(*block),
                send_sem=send_sems.at[k], recv_sem=recv_sems.at[k], device_id=to, device_id_type=MESH)

        mine = pltpu.make_async_copy(x_ref, slot(*me), local_sem)
        mine.start()
        first = [copy(0, me, sibling, src=x_ref)]
        first += [copy(1 + j, me, (*chip, c), src=x_ref) for j, chip in enumerate(chips)]
        for cp in first:
            cp.start()

        @pl.loop(0, T // tT)
        def _(i):
            rows_i = pl.ds(pl.multiple_of(i * tT, tT), tT)
            _prologue_rows(rows_i, xs_ref, nw_ref, pos_ref, inv_ref, sign_ref, h_ref, cos_ref, sin_ref)

        passed = [copy(4 + j, (*chip, c), sibling) for j, chip in enumerate(chips)]
        for j, chip in enumerate(chips):
            copy(1 + j, (*chip, c), me).wait_recv()
            passed[j].start()
        copy(0, sibling, me).wait_recv()
        for j, chip in enumerate(chips):
            copy(4 + j, (*chip, 1 - c), me).wait_recv()
        for cp in first + passed:
            cp.wait_send()
        mine.wait()

    return pl.pallas_call(
        body, name="gather_weights",
        in_specs=[_vmem()] * 7, out_specs=[_any()] + [_vmem()] * 3,
        out_shape=[jax.ShapeDtypeStruct((NDEV, rows, cols), WIRE), jax.ShapeDtypeStruct((T, D), MXU),
                   jax.ShapeDtypeStruct((T, LANE), F32), jax.ShapeDtypeStruct((T, LANE), F32)],
        scratch_shapes=[pltpu.VMEM((rows, cols), WIRE), pltpu.VMEM((SHARD_PAD, D), F32),
                        pltpu.SemaphoreType.DMA((7,)), pltpu.SemaphoreType.DMA((7,)), pltpu.SemaphoreType.DMA],
        compiler_params=_cp(),
    )(w_in_t, gu_s, xs, norm_w, pos_col, inv_row, sign_row)


def _pair_reduce(gwt, tail):
    n = gwt.shape[1]
    rows = SHARD_PAD + (0 if tail is None else tail.shape[1])
    blk = (4, rows, n)

    def body(*refs):
        g_ref = refs[0]
        t_ref = None if tail is None else refs[1]
        out_ref, got, own, send_sems, recv_sems, own_sems = refs[-6:]
        x, y, c = _place()

        def parts(d, dst):
            frame = g_ref.at[pl.ds(pl.multiple_of(FRAME * d, 16), SHARD_PAD)]
            pieces = [(frame, dst.at[0:SHARD_PAD])]
            return pieces if tail is None else pieces + [(t_ref.at[d], dst.at[SHARD_PAD:rows])]

        sends, loads = [], []
        for chip in range(4):
            sends.append([pltpu.make_async_remote_copy(
                src_ref=s, dst_ref=d_, send_sem=send_sems.at[chip, k], recv_sem=recv_sems.at[chip, k],
                device_id=(x, y, 1 - c), device_id_type=MESH)
                for k, (s, d_) in enumerate(parts(2 * chip + (1 - c), got.at[chip]))])
            loads.append([pltpu.make_async_copy(s, d_, own_sems.at[chip, k])
                          for k, (s, d_) in enumerate(parts(2 * chip + c, own.at[chip]))])
        for group in sends + loads:
            for cp in group:
                cp.start()
        for chip in range(4):
            for cp in loads[chip]:
                cp.wait()
            for cp in sends[chip]:
                cp.wait_recv()
            out_ref[chip] = (own[chip].astype(F32) + got[chip].astype(F32)).astype(out_ref.dtype)
        for group in sends:
            for cp in group:
                cp.wait_send()

    args = [gwt] if tail is None else [gwt, tail]
    return pl.pallas_call(
        body, name="pair_reduce",
        in_specs=[_any()] * len(args), out_specs=_vmem(),
        out_shape=jax.ShapeDtypeStruct(blk, gwt.dtype),
        scratch_shapes=[pltpu.VMEM(blk, gwt.dtype), pltpu.VMEM(blk, gwt.dtype),
                        pltpu.SemaphoreType.DMA((4, 2)), pltpu.SemaphoreType.DMA((4, 2)), pltpu.SemaphoreType.DMA((4, 2))],
        compiler_params=_cp(),
    )(*args)


def _pad_cols(a, cols):
    return jnp.pad(a, ((0, 0), (0, cols - a.shape[1])))


def _pad_rows(a, rows):
    return jnp.pad(a, ((0, rows - a.shape[0]), (0, 0)))


FRAME = 928


def _wft_plan():
    moves = []
    for blk in range(8):
        for half in range(2):
            for sub in range(2):
                moves.append((C_Q + 128 * blk + 32 * (2 * half + sub), 128 * blk + 32 * (2 * sub + half), 32))
    for idx in range(4):
        for dup in range(2):
            moves.append((C_KD + 64 * idx + 32 * dup, 1024 + 32 * idx, 32))
    for g in range(2):
        for dup in range(2):
            moves.append((C_VD + 128 * g + 64 * dup, 1152 + 64 * g, 64))
    moves += [(C_BL, 5376, RANK), (C_BV, 3328, 1024), (C_BQ, 2304, 512), (C_BK, 2816, 512),
              (C_AG, 1280, 1024), (C_BG, 4352, 1024), (C_MA, 5392, 1024), (C_MB, 6416, 1024)]
    bulk, seams = [], []
    for dst, src, n in moves:
        r = src
        while r < src + n:
            f = min(r // FRAME, NDEV - 1)
            local = r - FRAME * f
            if f > 0 and local < 16:
                assert local == 0
                seams.append((f, dst + r - src))
                step = 16
            else:
                step = min(src + n, FRAME * (f + 1) if f < NDEV - 1 else IN_WIDTH) - r
                bulk.append((f, local, dst + r - src, step))
            r += step
    assert sorted(f for f, _ in seams) == list(range(1, NDEV))
    return bulk, seams, [(C_BL + RANK, C_GLA - C_BL - RANK)]


def _build_wft_copies(frames):
    bulk, seams, zeros = _wft_plan()
    (z0, zn), = zeros

    def body(f_ref, o_ref, edge, sems, esems):
        copies = [pltpu.make_async_copy(f_ref.at[f, pl.ds(l0, n)], o_ref.at[pl.ds(dst, n)], sems.at[i])
                  for i, (f, l0, dst, n) in enumerate(bulk)]
        loads = []
        for i, (f, _) in enumerate(seams):
            loads.append(pltpu.make_async_copy(f_ref.at[f, pl.ds(0, 16)], edge.at[i, 0], esems.at[i, 0]))
            loads.append(pltpu.make_async_copy(f_ref.at[f - 1, pl.ds(FRAME, 16)], edge.at[i, 1], esems.at[i, 1]))
        for cp in copies + loads:
            cp.start()
        o_ref[z0:z0 + zn, :] = jnp.zeros((zn, D), o_ref.dtype)
        for cp in loads:
            cp.wait()
        for i, (_, dst) in enumerate(seams):
            o_ref[dst:dst + 16, :] = edge[i, 0] + edge[i, 1]
        for cp in copies:
            cp.wait()

    return pl.pallas_call(
        body, name="build_wft",
        in_specs=[_any()], out_specs=_vmem(),
        out_shape=jax.ShapeDtypeStruct((NF, D), frames.dtype),
        scratch_shapes=[pltpu.VMEM((len(seams), 2, 16, D), frames.dtype),
                        pltpu.SemaphoreType.DMA((len(bulk),)), pltpu.SemaphoreType.DMA((len(seams), 2))],
        compiler_params=_cp(),
    )(frames)


def kernel(x, positions, norm_w, w_in, a_sinks, b_gate_up, b_gate_bias, b_out_norm_w, w_a_proj, w_b_proj, w_out, final_norm_w, loss_target, m_norm_w, m_w_in, m_a_sinks, m_b_gate_up, m_b_gate_bias, m_b_out_norm_w, m_w_a_proj, m_w_b_proj, m_w_out, m_final_norm_w, v_norm_w, v_w_in, v_a_sinks, v_b_gate_up, v_b_gate_bias, v_b_out_norm_w, v_w_a_proj, v_w_b_proj, v_w_out, v_final_norm_w):
    T = x.shape[1]
    xs, target = x[0], loss_target[0]
    fnw = final_norm_w.reshape(1, D)
    me = 4 * lax.axis_index("x") + 2 * lax.axis_index("y") + lax.axis_index("c")
    allw, h, cos, sin = _gather_blocks(w_in[0].T, b_gate_up[0], xs, norm_w, positions.reshape(T, 1))
    late_blk = jnp.concatenate([w_a_proj[0], w_b_proj[0], w_out[0]], axis=0).astype(WIRE)
    l_send, l_recv, l_blk, l_land, l_started = _late_gather_start(late_blk, cos)
    wf = _build_wft_copies(allw)
    gu = allw[:, SHARD_PAD:G_ROWS, :64].transpose(1, 0, 2).reshape(RANK, 512)
    gu_pad = _pad_rows(gu, W_BL)

    proj = _proj(h, wf, l_started)
    o_a, lse = _swa_fwd(proj, cos, sin, a_sinks)
    o_b, states = _gla_fwd(proj, gu_pad, b_gate_bias)
    l_blk, l_land = _late_gather_wait(l_send, l_recv, l_blk, l_land, states, lse)
    late = lax.dynamic_update_slice(l_land, l_blk[None], (me, 0, 0))
    w_a, w_b, w_o = (late[:, 128 * i:128 * (i + 1), :].reshape(D, D) for i in range(3))
    (dx2, do_a, do_b, d_gates, g_wa, g_wb, g_wo, g_fn, g_bn, loss_part) = _mid(
        xs, target, proj, o_a, o_b, w_a, w_b, w_o, jnp.tile(b_out_norm_w, (1, B_HEADS)), fnw)
    d_q, d_kv, g_sinks = _swa_bwd(proj, cos, sin, a_sinks, do_a, o_a, lse, cos)
    d_gla, d_bl, g_gu, g_bias = _gla_bwd(proj, gu_pad, b_gate_bias, states, do_b)
    pieces = [d_q, d_kv, d_bl, d_gla, d_gates]
    offsets = [C_Q, C_KD, C_BL, C_GLA, C_GATES]

    ggu = g_gu[:RANK].reshape(RANK, NDEV, 64).transpose(1, 0, 2)
    ggu_half = [jnp.pad(ggu, ((0, 0), (0, 0), (0, DH - 64))), jnp.zeros((NDEV, RANK, DH), F32)]

    def tail(hf):
        cols = slice(hf * DH, (hf + 1) * DH)
        return jnp.concatenate([g[:, cols].reshape(NDEV, 128, DH) for g in (g_wa, g_wb, g_wo)]
                               + [ggu_half[hf]], axis=1).astype(WIRE)

    send0, recv0, s_thru0, land0, started0 = _chip_start(_pair_reduce(_gw_half(h, pieces, 0), tail(0)), 0)
    send1, recv1, s_thru1, land1, started1 = _chip_start(
        _pair_reduce(_gw_half(h, pieces, 1, after=started0), tail(1)), 1)
    grad_x, g_nw = _dh_norm(pieces, offsets, wf, xs, dx2, norm_w, started1)
    small = jnp.concatenate([g_nw, g_fn, _pad_cols(g_bias, D), _pad_cols(g_bn, D), _pad_cols(g_sinks, D),
                             _pad_cols(loss_part, D)], axis=0)
    sm_send, sm_recv, sm_blk, sm_land, sm_started = _late_gather_start(small, g_nw, name="small_gather")
    sums0, got0 = _chip_wait(send0, recv0, s_thru0, land0, sm_started, 0)
    sums1, got1 = _chip_wait(send1, recv1, s_thru1, land1, got0, 1)
    sums, from_chips = [sums0, sums1], [got0, got1]

    ws = dict(norm_w=norm_w, fnw=fnw, bias=b_gate_bias, bn=b_out_norm_w, sinks=a_sinks)
    ms = dict(norm_w=m_norm_w, fnw=m_final_norm_w.reshape(1, D), bias=m_b_gate_bias, bn=m_b_out_norm_w,
              sinks=m_a_sinks)
    vs = dict(norm_w=v_norm_w, fnw=v_final_norm_w.reshape(1, D), bias=v_b_gate_bias, bn=v_b_out_norm_w,
              sinks=v_a_sinks)
    t_rows, t_gu = _finish(
        [w_in[0].T, w_a_proj[0], w_b_proj[0], w_out[0]], [m_w_in[0].T, m_w_a_proj[0], m_w_b_proj[0], m_w_out[0]],
        [v_w_in[0].T, v_w_a_proj[0], v_w_b_proj[0], v_w_out[0]],
        b_gate_up[0], m_b_gate_up[0], v_b_gate_up[0], sums, from_chips)
    sm_blk, sm_land = _late_gather_wait(sm_send, sm_recv, sm_blk, sm_land, t_rows[0], t_gu[0], name="small_gather")
    loss, sm = _finish_small(ws, ms, vs, lax.dynamic_update_slice(sm_land, sm_blk[None], (me, 0, 0)))

    def outputs(k):
        return [sm["norm_w"][k], t_rows[k].T[None], sm["sinks"][k], t_gu[k][None], sm["bias"][k], sm["bn"][k],
                t_rows[4 + k][None], t_rows[8 + k][None], t_rows[12 + k][None], sm["fnw"][k].reshape(D)]

    return (loss[0, 0], grad_x[None], *outputs(0), *outputs(1), *outputs(2), *outputs(3))
```

```python
import functools

import numpy as np
import jax
import jax.numpy as jnp
from jax import lax
from jax.experimental import pallas as pl
from jax.experimental.pallas import tpu as pltpu

F32 = jnp.float32
MXU = jnp.bfloat16
WIRE = jnp.bfloat16

D = 1024
A_HEADS, A_KV, A_HD = 16, 2, 64
BLK = 128
B_HEADS, B_DK, B_DV = 4, 128, 256
RANK, TAU, CHUNK = 16, 16.0, 64
EPS, NEG = 1e-5, -1e30
ROPE_THETA = 10000.0
IN_WIDTH, NDEV = 7440, 8
SHARD = IN_WIDTH // NDEV
LANE = 128

C_Q, C_KD, C_VD, C_BL = 0, 1024, 1280, 1536
C_BV, C_BQ, C_BK = 2048, 3072, 3584
C_AG, C_BG, C_MA, C_MB = 4096, 5120, 6144, 7168
C_GLA, W_GLA, C_GATES, W_GATES = 2048, 2048, 4096, 4096
NF = 8192
W_BL = 128

SHARD_PAD = 944
R_IN, R_A, R_B, R_O, R_GU, ROWS = 0, 944, 1072, 1200, 1328, 1344
SMALL_ROWS = 48

ADAM_LR, ADAM_B1, ADAM_B2, ADAM_EPS, ADAM_WD, ADAM_STEP = 0.001, 0.9, 0.999, 1e-08, 0.01, 10

MESH = pl.DeviceIdType.MESH
VMEM_LIMIT = 56 * 1024 * 1024


def _cp(sem=None, **kw):
    if sem is not None:
        kw["dimension_semantics"] = sem
    return pltpu.CompilerParams(vmem_limit_bytes=VMEM_LIMIT, **kw)


def _dot(a, b):
    return jnp.dot(a, b, preferred_element_type=F32)


def _dot_nt(a, b):
    return lax.dot_general(a, b, (((1,), (1,)), ((), ())), preferred_element_type=F32)


def _dot_tn(a, b):
    return lax.dot_general(a, b, (((0,), (0,)), ((), ())), preferred_element_type=F32)


def _dot_f32(a, b):
    return jnp.dot(a, b, preferred_element_type=F32, precision=lax.Precision.HIGHEST)


def _sigmoid(z):
    return 0.5 * jnp.tanh(0.5 * z) + 0.5


def _rope(xp, cos, sin):
    return xp * cos + pltpu.roll(xp, 64, 1) * sin


def _rope_bwd(dy, cos, sin):
    return dy * cos - pltpu.roll(dy, 64, 1) * sin


def _vmem():
    return pl.BlockSpec(memory_space=pltpu.VMEM)


def _any():
    return pl.BlockSpec(memory_space=pl.ANY)


def _rope_rows():
    half = A_HD // 2
    inv = (np.float32(ROPE_THETA) ** (-np.arange(half, dtype=np.float32) / np.float32(half))).astype(np.float32)
    inv_row = jnp.asarray(np.tile(inv, 4)[None, :])
    sign_row = jnp.asarray(np.concatenate([-np.ones(64, np.float32), np.ones(64, np.float32)])[None, :])
    return inv_row, sign_row


def _prologue_rows(rows, x_ref, nw_ref, pos_ref, inv_ref, sign_ref, h_ref, cos_ref, sin_ref):
    xv = x_ref[rows, :]
    r = lax.rsqrt(jnp.mean(xv * xv, axis=-1, keepdims=True) + EPS)
    h_ref[rows, :] = ((xv * r) * nw_ref[...]).astype(h_ref.dtype)
    ang = pos_ref[rows, :].astype(F32) * inv_ref[...]
    cos_ref[rows, :] = jnp.cos(ang)
    sin_ref[rows, :] = jnp.sin(ang) * sign_ref[...]


def _proj(h, wft, after):
    T = h.shape[0]
    tT, tN = T, 512

    def body(h_ref, w_ref, after_ref, o_ref):
        o_ref[...] = _dot_nt(h_ref[...], w_ref[...])

    return pl.pallas_call(
        body, name="proj", grid=(T // tT, NF // tN),
        in_specs=[pl.BlockSpec((tT, D), lambda i, j: (i, 0)), pl.BlockSpec((tN, D), lambda i, j: (j, 0)), _any()],
        out_specs=pl.BlockSpec((tT, tN), lambda i, j: (i, j)),
        out_shape=jax.ShapeDtypeStruct((T, NF), F32),
        compiler_params=_cp(("parallel", "parallel")),
    )(h, wft, after)


def _swa_masks():
    lane = lax.broadcasted_iota(jnp.int32, (BLK, LANE), 1)
    rope_sub0 = ((lane // 32) % 2) == 0
    std_sub0 = lane < 64
    return lane, rope_sub0, std_sub0


def _swa_tri():
    qi = lax.broadcasted_iota(jnp.int32, (BLK, BLK), 0)
    kj = lax.broadcasted_iota(jnp.int32, (BLK, BLK), 1)
    return kj <= qi


def _swa_fold(full, tri):
    return jnp.where(tri, full[:, BLK:], full[:, :BLK])


def _swa_unfold(sq, tri):
    return jnp.concatenate([jnp.where(tri, 0.0, sq), jnp.where(tri, sq, 0.0)], axis=1)


def _swa_keys(kc_ref, kp_ref, vc_ref, vp_ref, cq, sq, cp, sp):
    def ropek(kref, c, s):
        kv = kref[...]
        return jnp.concatenate([_rope(kv[:, :LANE], c, s), _rope(kv[:, LANE:], c, s)], axis=1)

    K = jnp.concatenate([ropek(kp_ref, cp, sp), ropek(kc_ref, cq, sq)], axis=0).astype(MXU)
    V = jnp.concatenate([vp_ref[...], vc_ref[...]], axis=0).astype(MXU)
    return K, V


def _swa_in_specs(nb, last):
    def cur(n):
        return jnp.minimum(n, last)

    def prev(n):
        return jnp.maximum(cur(n) - 1, 0)

    kd, vd = C_KD // 256, C_VD // 256
    return [
        pl.BlockSpec((BLK, D), lambda n: (cur(n), C_Q // D)),
        pl.BlockSpec((BLK, 256), lambda n: (cur(n), kd)),
        pl.BlockSpec((BLK, 256), lambda n: (prev(n), kd)),
        pl.BlockSpec((BLK, 256), lambda n: (cur(n), vd)),
        pl.BlockSpec((BLK, 256), lambda n: (prev(n), vd)),
        pl.BlockSpec((BLK, LANE), lambda n: (cur(n), 0)),
        pl.BlockSpec((BLK, LANE), lambda n: (cur(n), 0)),
        pl.BlockSpec((BLK, LANE), lambda n: (prev(n), 0)),
        pl.BlockSpec((BLK, LANE), lambda n: (prev(n), 0)),
    ]


def _swa_fwd(proj, cos, sin, sinks):
    T = proj.shape[0]
    nb = T // BLK
    scale = A_HD ** -0.5

    def body(sinks_ref, q_ref, kc_ref, kp_ref, vc_ref, vp_ref, cq_ref, sq_ref, cp_ref, sp_ref, o_ref, l_ref):
        n = pl.program_id(0)
        cq, sq = cq_ref[...], sq_ref[...]
        K, V = _swa_keys(kc_ref, kp_ref, vc_ref, vp_ref, cq, sq, cp_ref[...], sp_ref[...])
        tri = _swa_tri()
        valid = tri | (n > 0)
        lane, rope_sub0, std_sub0 = _swa_masks()
        group = A_HEADS // A_KV
        roped, lses = {}, []

        def products(head):
            pb, sub, g = head // 2, head % 2, head // group
            if sub == 0:
                roped[pb] = _rope(q_ref[:, pb * LANE:(pb + 1) * LANE], cq, sq)
            qm = jnp.where(rope_sub0 if sub == 0 else ~rope_sub0, roped[pb], 0.0).astype(MXU)
            return _dot_nt(qm, K[:, g * LANE:(g + 1) * LANE])

        def softmax(head, s_full):
            s = jnp.where(valid, _swa_fold(s_full, tri) * scale, NEG)
            sink = sinks_ref[0, head]
            m = jnp.maximum(jnp.max(s, axis=1, keepdims=True), sink)
            e = jnp.exp(s - m)
            den = jnp.sum(e, axis=1, keepdims=True) + jnp.exp(sink - m)
            lses.append(m + jnp.log(den))
            return _swa_unfold(e / den, tri).astype(MXU)

        outs = {}
        st1 = {0: products(0), 1: products(1)}
        st2 = {0: softmax(0, st1.pop(0))}
        for head in range(A_HEADS):
            if head + 2 < A_HEADS:
                st1[head + 2] = products(head + 2)
            if head + 1 < A_HEADS:
                st2[head + 1] = softmax(head + 1, st1.pop(head + 1))
            g = head // group
            outs[head] = _dot(st2.pop(head), V[:, g * LANE:(g + 1) * LANE])
            if head % 2 == 1:
                pb = head // 2
                o_ref[:, pb * LANE:(pb + 1) * LANE] = jnp.where(std_sub0, outs[head - 1], outs[head])
        lacc = jnp.zeros((BLK, LANE), F32)
        for head in range(A_HEADS):
            lacc = jnp.where(lane == head, lses[head], lacc)
        l_ref[...] = lacc

    return pl.pallas_call(
        body, name="swa_fwd", grid=(nb,),
        in_specs=[pl.BlockSpec(memory_space=pltpu.SMEM)] + _swa_in_specs(nb, nb - 1),
        out_specs=[pl.BlockSpec((BLK, D), lambda n: (n, 0)), pl.BlockSpec((BLK, LANE), lambda n: (n, 0))],
        out_shape=[jax.ShapeDtypeStruct((T, D), F32), jax.ShapeDtypeStruct((T, LANE), F32)],
        compiler_params=_cp(("parallel",)),
    )(sinks, proj, proj, proj, proj, proj, cos, sin, cos, sin)


def _swa_bwd(proj, cos, sin, sinks, do_a, o_a, lse, after):
    T = proj.shape[0]
    nb = T // BLK
    scale = A_HD ** -0.5

    def body(sinks_ref, q_ref, kc_ref, kp_ref, vc_ref, vp_ref, cq_ref, sq_ref, cp_ref, sp_ref,
             do_ref, o_ref, l_ref, after_ref, dq_ref, dkv_ref, ds_ref, ckv_ref):
        n = pl.program_id(0)

        @pl.when(n == 0)
        def _():
            ckv_ref[...] = jnp.zeros_like(ckv_ref)
            ds_ref[...] = jnp.zeros_like(ds_ref)

        @pl.when(n < nb)
        def _():
            cq, sq, cp, sp = cq_ref[...], sq_ref[...], cp_ref[...], sp_ref[...]
            K, V = _swa_keys(kc_ref, kp_ref, vc_ref, vp_ref, cq, sq, cp, sp)
            tri = _swa_tri()
            valid = tri | (n > 0)
            lane, rope_sub0, std_sub0 = _swa_masks()
            lane_row = lax.broadcasted_iota(jnp.int32, (1, LANE), 1)
            lse_v = l_ref[...]
            dKt = [jnp.zeros((LANE, 2 * BLK), F32) for _ in range(A_KV)]
            dVt = [jnp.zeros((LANE, 2 * BLK), F32) for _ in range(A_KV)]
            dsinks, roped, roped_t, do_t = [], {}, {}, {}
            group = A_HEADS // A_KV
            dim = lax.broadcasted_iota(jnp.int32, (LANE, BLK), 0)
            rope_row0, std_row0 = ((dim // 32) % 2) == 0, dim < 64

            def products(head):
                pb, sub, g = head // 2, head % 2, head // group
                cols = slice(pb * LANE, (pb + 1) * LANE)
                Kg, Vg = K[:, g * LANE:(g + 1) * LANE], V[:, g * LANE:(g + 1) * LANE]
                if sub == 0:
                    roped[pb] = _rope(q_ref[:, cols], cq, sq)
                    roped_t[pb] = roped[pb].T
                    do_t[pb] = do_ref[:, cols].T
                qm = jnp.where(rope_sub0 if sub == 0 else ~rope_sub0, roped[pb], 0.0).astype(MXU)
                qmt = jnp.where(rope_row0 if sub == 0 else ~rope_row0, roped_t[pb], 0.0).astype(MXU)
                dov = jnp.where(std_sub0 if sub == 0 else ~std_sub0, do_ref[:, cols], 0.0)
                dovt = jnp.where(std_row0 if sub == 0 else ~std_row0, do_t[pb], 0.0).astype(MXU)
                delta = jnp.sum(dov * o_ref[:, cols], axis=1, keepdims=True)
                return qmt, dovt, delta, _dot_nt(qm, Kg), _dot_nt(dov.astype(MXU), Vg)

            def scores(head, qmt, dovt, delta, s_full, dp_full):
                lh = jnp.sum(jnp.where(lane == head, lse_v, 0.0), axis=1, keepdims=True)
                p = jnp.where(valid, jnp.exp(_swa_fold(s_full, tri) * scale - lh), 0.0)
                psink = jnp.exp(sinks_ref[0, head] - lh)
                dsinks.append(jnp.sum(-psink * delta, axis=0, keepdims=True))
                dsq = (p * (_swa_fold(dp_full, tri) - delta)) * scale
                return qmt, dovt, _swa_unfold(p, tri).astype(MXU), _swa_unfold(dsq, tri).astype(MXU)

            def grads(head, qmt, dovt, pb16, dsc):
                g = head // group
                dKt[g] = dKt[g] + _dot(qmt, dsc)
                dVt[g] = dVt[g] + _dot(dovt, pb16)
                return _dot(dsc, K[:, g * LANE:(g + 1) * LANE])

            dqs = {}
            st1 = {0: products(0), 1: products(1)}
            st2 = {0: scores(0, *st1.pop(0))}
            for head in range(A_HEADS):
                if head + 2 < A_HEADS:
                    st1[head + 2] = products(head + 2)
                if head + 1 < A_HEADS:
                    st2[head + 1] = scores(head + 1, *st1.pop(head + 1))
                dqs[head] = grads(head, *st2.pop(head))
                if head % 2 == 1:
                    pb = head // 2
                    dqp = jnp.where(rope_sub0, dqs[head - 1], dqs[head])
                    dq_ref[:, pb * LANE:(pb + 1) * LANE] = _rope_bwd(dqp, cq, sq).astype(dq_ref.dtype)
            dsink = jnp.zeros((1, LANE), F32)
            for head in range(A_HEADS):
                dsink = jnp.where(lane_row == head, dsinks[head], dsink)
            dK, dV = [a.T for a in dKt], [a.T for a in dVt]
            prev = ([_rope_bwd(dK[g][:BLK], cp, sp) for g in range(A_KV)] + [dV[g][:BLK] for g in range(A_KV)])
            cur_ = ([_rope_bwd(dK[g][BLK:], cq, sq) for g in range(A_KV)] + [dV[g][BLK:] for g in range(A_KV)])
            dkv_ref[...] = (ckv_ref[...] + jnp.concatenate(prev, axis=1)).astype(dkv_ref.dtype)
            ckv_ref[...] = jnp.concatenate(cur_, axis=1)
            ds_ref[...] = ds_ref[...] + jnp.broadcast_to(dsink, ds_ref.shape)

        @pl.when(n == nb)
        def _():
            dkv_ref[...] = ckv_ref[...].astype(dkv_ref.dtype)

    last = nb - 1

    def cur(n):
        return jnp.minimum(n, last)

    def out_kv(n):
        return (jnp.maximum(n - 1, 0), 0)

    return pl.pallas_call(
        body, name="swa_bwd", grid=(nb + 1,),
        in_specs=[pl.BlockSpec(memory_space=pltpu.SMEM)] + _swa_in_specs(nb, last) + [
            pl.BlockSpec((BLK, D), lambda n: (cur(n), 0)),
            pl.BlockSpec((BLK, D), lambda n: (cur(n), 0)),
            pl.BlockSpec((BLK, LANE), lambda n: (cur(n), 0)),
            _any(),
        ],
        out_specs=[
            pl.BlockSpec((BLK, D), lambda n: (cur(n), 0)),
            pl.BlockSpec((BLK, 512), out_kv),
            pl.BlockSpec((8, LANE), lambda n: (0, 0)),
        ],
        out_shape=[
            jax.ShapeDtypeStruct((T, D), MXU),
            jax.ShapeDtypeStruct((T, 512), MXU),
            jax.ShapeDtypeStruct((8, LANE), F32),
        ],
        scratch_shapes=[pltpu.VMEM((BLK, 512), F32)],
        compiler_params=_cp(("arbitrary",)),
    )(sinks, proj, proj, proj, proj, proj, cos, sin, cos, sin, do_a, o_a, lse, after)


NCH = 4
GSTEP = NCH * CHUNK
ST_ROWS = B_HEADS * B_DV


def _chunk_rows(c):
    return slice(c * CHUNK, (c + 1) * CHUNK)


def _per_chunk(which, vals):
    out = vals[-1]
    for c in range(NCH - 2, -1, -1):
        out = jnp.where(which == c, vals[c], out)
    return out


def _gla_gate(bl_ref, gu_ref, bias_ref):
    gk = _dot(bl_ref[...].astype(MXU), gu_ref[...]) + bias_ref[...]
    la = (jnp.minimum(gk, 0.0) - jnp.log(1.0 + jnp.exp(-jnp.abs(gk)))) / TAU
    ri = lax.broadcasted_iota(jnp.int32, (GSTEP, GSTEP), 0)
    ci = lax.broadcasted_iota(jnp.int32, (GSTEP, GSTEP), 1)
    same = (ri // CHUNK) == (ci // CHUNK)
    lower, upper = same & (ci <= ri), same & (ci >= ri)
    b = _dot_f32(jnp.where(lower, 1.0, 0.0).astype(F32), la)
    which = lax.broadcasted_iota(jnp.int32, (GSTEP, 1), 0) // CHUNK
    return gk, la, b, lower, upper, which


def _gla_head(q_ref, k_ref, la, b, which, h):
    sl = slice(h * B_DK, (h + 1) * B_DK)
    bh, lah = b[:, sl], la[:, sl]
    bls = [jnp.sum(lah[_chunk_rows(c)], axis=0, keepdims=True) for c in range(NCH)]
    blast = _per_chunk(which, bls)
    qc = q_ref[:, sl] * (B_DK ** -0.5)
    kh = k_ref[:, sl]
    eb, enb, esb = jnp.exp(bh), jnp.exp(-bh), jnp.exp(blast - bh)
    return qc * eb, kh * enb, kh * esb, eb, enb, esb, [jnp.exp(v) for v in bls]


def _gla_specs(step_of):
    return [
        pl.BlockSpec((GSTEP, 512), lambda i: (step_of(i), C_BQ // 512)),
        pl.BlockSpec((GSTEP, 512), lambda i: (step_of(i), C_BK // 512)),
        pl.BlockSpec((GSTEP, D), lambda i: (step_of(i), C_BV // D)),
        pl.BlockSpec((GSTEP, W_BL), lambda i: (step_of(i), C_BL // W_BL)),
        pl.BlockSpec((W_BL, 512), lambda i: (0, 0)),
        pl.BlockSpec((1, 512), lambda i: (0, 0)),
    ]


def _gla_fwd(proj, gu_pad, bias):
    T = proj.shape[0]
    ns = T // GSTEP

    def body(q_ref, k_ref, v_ref, bl_ref, gu_ref, bias_ref, o_ref, st_ref, state_ref):
        @pl.when(pl.program_id(0) == 0)
        def _():
            state_ref[...] = jnp.zeros_like(state_ref)

        _, la, b, lower, _, which = _gla_gate(bl_ref, gu_ref, bias_ref)

        def within(h):
            q_e, k_e, k_s, _, _, _, decays = _gla_head(q_ref, k_ref, la, b, which, h)
            vh = v_ref[:, h * B_DV:(h + 1) * B_DV].astype(MXU)
            q_eb = q_e.astype(MXU)
            att = jnp.where(lower, _dot_nt(q_eb, k_e.astype(MXU)), 0.0)
            return vh, q_eb, k_s.astype(MXU), _dot(att.astype(MXU), vh), decays

        def across(h, vh, q_eb, k_sb, o_intra, decays):
            rows = slice(h * B_DV, (h + 1) * B_DV)
            s = state_ref[rows, :]
            outs = []
            for c in range(NCH):
                cr = _chunk_rows(c)
                st_ref[c * ST_ROWS + h * B_DV:c * ST_ROWS + (h + 1) * B_DV, :] = s
                outs.append(o_intra[cr] + _dot_nt(q_eb[cr], s.astype(MXU)))
                s = s * decays[c] + _dot_tn(vh[cr], k_sb[cr])
            state_ref[rows, :] = s
            o_ref[:, rows] = jnp.concatenate(outs, axis=0)

        for h in range(B_HEADS):
            across(h, *within(h))

    return pl.pallas_call(
        body, name="gla_fwd", grid=(ns,),
        in_specs=_gla_specs(lambda i: i),
        out_specs=[pl.BlockSpec((GSTEP, D), lambda i: (i, 0)),
                   pl.BlockSpec((NCH * ST_ROWS, B_DK), lambda i: (i, 0))],
        out_shape=[jax.ShapeDtypeStruct((T, D), F32),
                   jax.ShapeDtypeStruct((ns * NCH * ST_ROWS, B_DK), F32)],
        scratch_shapes=[pltpu.VMEM((ST_ROWS, B_DK), F32)],
        compiler_params=_cp(("arbitrary",)),
    )(proj, proj, proj, proj, gu_pad, bias)


def _gla_bwd(proj, gu_pad, bias, states, do_b):
    T = proj.shape[0]
    ns = T // GSTEP
    o_q, o_k = C_BQ - C_GLA, C_BK - C_GLA

    def body(q_ref, k_ref, v_ref, bl_ref, gu_ref, bias_ref, st_ref, do_ref,
             dg_ref, dbl_ref, ggu_ref, gbias_ref, gt_ref):
        @pl.when(pl.program_id(0) == 0)
        def _():
            gt_ref[...] = jnp.zeros_like(gt_ref)
            ggu_ref[...] = jnp.zeros_like(ggu_ref)
            gbias_ref[...] = jnp.zeros_like(gbias_ref)

        gk, la, b, lower, upper_mask, which = _gla_gate(bl_ref, gu_ref, bias_ref)
        upper = jnp.where(upper_mask, 1.0, 0.0).astype(F32)
        dla_parts = []

        def within(h):
            q_e, k_e, k_s, eb, enb, esb, decays = _gla_head(q_ref, k_ref, la, b, which, h)
            vh = v_ref[:, h * B_DV:(h + 1) * B_DV].astype(MXU)
            doh = do_ref[:, h * B_DV:(h + 1) * B_DV].astype(MXU)
            q_eb, k_eb = q_e.astype(MXU), k_e.astype(MXU)
            att = jnp.where(lower, _dot_nt(q_eb, k_eb), 0.0).astype(MXU)
            datt = jnp.where(lower, _dot_nt(doh, vh), 0.0).astype(MXU)
            return (q_e, k_e, k_s, eb, enb, esb, decays, vh, doh, q_eb, k_s.astype(MXU),
                    _dot(datt, k_eb), _dot_tn(datt, q_eb), _dot_tn(att, doh))

        def across(h, q_e, k_e, k_s, eb, enb, esb, decays, vh, doh, q_eb, k_sb, dq_i, dk_e, dv_i):
            rows = slice(h * B_DV, (h + 1) * B_DV)
            g = gt_ref[rows, :]
            dq_c, dks_c, dv_c, ddec = [None] * NCH, [None] * NCH, [None] * NCH, [None] * NCH
            for c in range(NCH - 1, -1, -1):
                cr = _chunk_rows(c)
                s = st_ref[c * ST_ROWS + h * B_DV:c * ST_ROWS + (h + 1) * B_DV, :]
                gb = g.astype(MXU)
                dq_c[c] = dq_i[cr] + _dot(doh[cr], s.astype(MXU))
                dks_c[c] = _dot(vh[cr], gb)
                dv_c[c] = dv_i[cr] + _dot_nt(k_sb[cr], gb)
                ddec[c] = jnp.sum(g * s, axis=0, keepdims=True)
                g = g * decays[c] + _dot_tn(doh[cr], q_eb[cr])
            gt_ref[rows, :] = g
            dq_e = jnp.concatenate(dq_c, axis=0)
            dk_s = jnp.concatenate(dks_c, axis=0)
            dg_ref[:, rows] = jnp.concatenate(dv_c, axis=0).astype(dg_ref.dtype)
            dg_ref[:, o_q + h * B_DK:o_q + (h + 1) * B_DK] = (dq_e * eb * (B_DK ** -0.5)).astype(dg_ref.dtype)
            dg_ref[:, o_k + h * B_DK:o_k + (h + 1) * B_DK] = (dk_e * enb + dk_s * esb).astype(dg_ref.dtype)
            dks_ks = dk_s * k_s
            db = dq_e * q_e - dk_e * k_e - dks_ks
            dbl = [jnp.sum(dks_ks[_chunk_rows(c)], axis=0, keepdims=True) + ddec[c] * decays[c] for c in range(NCH)]
            dla_parts.append(_dot_f32(upper, db) + _per_chunk(which, dbl))

        for h in range(B_HEADS):
            across(h, *within(h))
        dla = jnp.concatenate(dla_parts, axis=1)
        dgk = dla * (1.0 / TAU) * _sigmoid(-gk)
        dgkb = dgk.astype(MXU)
        dbl_ref[...] = _dot_nt(dgkb, gu_ref[...]).astype(dbl_ref.dtype)
        ggu_ref[...] = ggu_ref[...] + _dot_tn(bl_ref[...].astype(MXU), dgkb)
        gbias_ref[...] = gbias_ref[...] + jnp.broadcast_to(jnp.sum(dgk, axis=0, keepdims=True), gbias_ref.shape)

    def rev(i):
        return ns - 1 - i

    return pl.pallas_call(
        body, name="gla_bwd", grid=(ns,),
        in_specs=_gla_specs(rev) + [
            pl.BlockSpec((NCH * ST_ROWS, B_DK), lambda i: (rev(i), 0)),
            pl.BlockSpec((GSTEP, D), lambda i: (rev(i), 0)),
        ],
        out_specs=[
            pl.BlockSpec((GSTEP, W_GLA), lambda i: (rev(i), 0)),
            pl.BlockSpec((GSTEP, W_BL), lambda i: (rev(i), 0)),
            pl.BlockSpec((W_BL, 512), lambda i: (0, 0)),
            pl.BlockSpec((8, 512), lambda i: (0, 0)),
        ],
        out_shape=[
            jax.ShapeDtypeStruct((T, W_GLA), MXU),
            jax.ShapeDtypeStruct((T, W_BL), MXU),
            jax.ShapeDtypeStruct((W_BL, 512), F32),
            jax.ShapeDtypeStruct((8, 512), F32),
        ],
        scratch_shapes=[pltpu.VMEM((B_HEADS * B_DV, B_DK), F32)],
        compiler_params=_cp(("arbitrary",)),
    )(proj, proj, proj, proj, gu_pad, bias, states, do_b)


def _mid(x, target, proj, o_a, o_b, w_a, w_b, w_out, w_bn4, fnw):
    T = x.shape[0]
    tT = min(T, 128)
    nbuf = 4
    o_ag, o_bg, o_ma, o_mb = (c - C_GATES for c in (C_AG, C_BG, C_MA, C_MB))

    def body(x_ref, t_ref, oa_ref, ob_ref, gates_ref, wa_ref, wb_ref, wo_ref, wbn_ref, fnw_ref,
             dx2_ref, doa_ref, dob_ref, dgates_ref,
             gwa_ref, gwb_ref, gwo_ref, gfn_ref, gbn_ref, loss_ref, buf_ref):
        i = pl.program_id(0)

        @pl.when(i == 0)
        def _():
            for r in (gwa_ref, gwb_ref, gwo_ref, gfn_ref, gbn_ref, loss_ref):
                r[...] = jnp.zeros_like(r)

        rows = pl.ds(pl.multiple_of((i % nbuf) * tT, tT), tT)

        def keep(k, val):
            buf_ref[k, rows, :] = val

        oa, ag = oa_ref[...], gates_ref[:, o_ag:o_ag + D]
        sg_a = _sigmoid(ag)
        silu_a = ag * sg_a
        oag_b = (oa * silu_a).astype(MXU)
        keep(0, oag_b)
        y_a = _dot(oag_b, wa_ref[...])

        ob, bg = ob_ref[...], gates_ref[:, o_bg:o_bg + D]
        rbs, obhats = [], []
        for h in range(B_HEADS):
            obh = ob[:, h * B_DV:(h + 1) * B_DV]
            rb = lax.rsqrt(jnp.mean(obh * obh, axis=-1, keepdims=True) + EPS)
            rbs.append(rb)
            obhats.append(obh * rb)
        obhat = jnp.concatenate(obhats, axis=1)
        wbn = wbn_ref[...]
        obn = obhat * wbn
        sg_b = _sigmoid(bg)
        silu_b = bg * sg_b
        obg_b = (obn * silu_b).astype(MXU)
        keep(1, obg_b)
        y_b = _dot(obg_b, wb_ref[...])

        sa, sb = _sigmoid(gates_ref[:, o_ma:o_ma + D]), _sigmoid(gates_ref[:, o_mb:o_mb + D])
        mg_b = (sa * y_a + sb * y_b).astype(MXU)
        keep(2, mg_b)
        x2 = x_ref[...] + _dot(mg_b, wo_ref[...])
        r2 = lax.rsqrt(jnp.mean(x2 * x2, axis=-1, keepdims=True) + EPS)
        xh2 = x2 * r2
        fw = fnw_ref[...]
        err = xh2 * fw - t_ref[...]
        tok = jnp.mean(err * err, axis=-1, keepdims=True)
        loss_ref[...] = loss_ref[...] + 0.5 * jnp.sum(tok, axis=0, keepdims=True)

        dy = err * (1.0 / D)
        gfn_ref[...] = gfn_ref[...] + jnp.broadcast_to(jnp.sum(dy * xh2, axis=0, keepdims=True), gfn_ref.shape)
        gy = dy * fw
        dx2 = r2 * (gy - xh2 * jnp.mean(gy * xh2, axis=-1, keepdims=True))
        dx2_ref[...] = dx2
        dx2_b = dx2.astype(MXU)
        keep(5, dx2_b)
        dmg = _dot_nt(dx2_b, wo_ref[...])

        dgates_ref[:, o_ma:o_ma + D] = (dmg * y_a * sa * (1.0 - sa)).astype(dgates_ref.dtype)
        dgates_ref[:, o_mb:o_mb + D] = (dmg * y_b * sb * (1.0 - sb)).astype(dgates_ref.dtype)
        dya_b = (dmg * sa).astype(MXU)
        dyb_b = (dmg * sb).astype(MXU)
        keep(3, dya_b)
        keep(4, dyb_b)
        doag = _dot_nt(dya_b, wa_ref[...])
        dobg = _dot_nt(dyb_b, wb_ref[...])

        @pl.when(i % nbuf == nbuf - 1)
        def _():
            gwa_ref[...] = gwa_ref[...] + _dot_tn(buf_ref[0], buf_ref[3])
            gwb_ref[...] = gwb_ref[...] + _dot_tn(buf_ref[1], buf_ref[4])
            gwo_ref[...] = gwo_ref[...] + _dot_tn(buf_ref[2], buf_ref[5])

        doa_ref[...] = doag * silu_a
        dgates_ref[:, o_ag:o_ag + D] = (doag * oa * (sg_a * (1.0 + ag * (1.0 - sg_a)))).astype(dgates_ref.dtype)
        dobn = dobg * silu_b
        dgates_ref[:, o_bg:o_bg + D] = (dobg * obn * (sg_b * (1.0 + bg * (1.0 - sg_b)))).astype(dgates_ref.dtype)
        gg = dobn * wbn
        gbn = jnp.zeros((1, B_DV), F32)
        for h in range(B_HEADS):
            sl = slice(h * B_DV, (h + 1) * B_DV)
            gbn = gbn + jnp.sum(dobn[:, sl] * obhats[h], axis=0, keepdims=True)
            ggh = gg[:, sl]
            dob_ref[:, sl] = rbs[h] * (ggh - obhats[h] * jnp.mean(ggh * obhats[h], axis=-1, keepdims=True))
        gbn_ref[...] = gbn_ref[...] + jnp.broadcast_to(gbn, gbn_ref.shape)

    assert (T // tT) % nbuf == 0
    tile = pl.BlockSpec((tT, D), lambda i: (i, 0))
    row = pl.BlockSpec((1, D), lambda i: (0, 0))
    acc8 = pl.BlockSpec((8, D), lambda i: (0, 0))
    return pl.pallas_call(
        body, name="mid", grid=(T // tT,),
        in_specs=[tile, tile, tile, tile, pl.BlockSpec((tT, W_GATES), lambda i: (i, C_GATES // W_GATES)),
                  _vmem(), _vmem(), _vmem(), row, row],
        out_specs=[tile, tile, tile, pl.BlockSpec((tT, W_GATES), lambda i: (i, 0)), _vmem(), _vmem(), _vmem(),
                   acc8, pl.BlockSpec((8, B_DV), lambda i: (0, 0)), pl.BlockSpec((8, LANE), lambda i: (0, 0))],
        out_shape=[
            jax.ShapeDtypeStruct((T, D), F32),
            jax.ShapeDtypeStruct((T, D), F32),
            jax.ShapeDtypeStruct((T, D), F32),
            jax.ShapeDtypeStruct((T, W_GATES), MXU),
            jax.ShapeDtypeStruct((D, D), F32),
            jax.ShapeDtypeStruct((D, D), F32),
            jax.ShapeDtypeStruct((D, D), F32),
            jax.ShapeDtypeStruct((8, D), F32),
            jax.ShapeDtypeStruct((8, B_DV), F32),
            jax.ShapeDtypeStruct((8, LANE), F32),
        ],
        scratch_shapes=[pltpu.VMEM((6, nbuf * tT, D), MXU)],
        compiler_params=_cp(("arbitrary",)),
    )(x, target, o_a, o_b, proj, w_a, w_b, w_out, w_bn4, fnw)


DH = D // 2


_GW_TILES = (("q", 0, 512, 0), ("q", 1, 512, 512), ("kv", 0, 256, 1024), ("bl", 0, RANK, 5376),
             ("gla", 0, 512, 3328), ("gla", 1, 512, 3840), ("gla", 2, 512, 2304), ("gla", 3, 512, 2816),
             ("gates", 0, 512, 1280), ("gates", 1, 512, 1792), ("gates", 2, 512, 4352), ("gates", 3, 512, 4864),
             ("gates", 4, 512, 5392), ("gates", 5, 512, 5904), ("gates", 6, 512, 6416), ("gates", 7, 512, 6928))


def _gw_unpermute(piece, t):
    if piece == "q":
        parts = []
        for blk in range(t.shape[0] // LANE):
            g = [t[blk * LANE + 32 * i:blk * LANE + 32 * (i + 1)] for i in range(4)]
            parts += [g[0], g[2], g[1], g[3]]
        return jnp.concatenate(parts, axis=0)
    if piece == "kv":
        k = [t[64 * i:64 * i + 32] + t[64 * i + 32:64 * i + 64] for i in range(4)]
        v = [t[256 + 128 * g:256 + 128 * g + 64] + t[256 + 128 * g + 64:256 + 128 * (g + 1)] for g in range(2)]
        return jnp.concatenate(k + v, axis=0)
    if piece == "bl":
        return t[:RANK]
    return t


def _gw_half(h, pieces, half, after=None):
    T = h.shape[0]
    steps = len(_GW_TILES)

    def body(*refs):
        h_ref = refs[0]
        srcs = dict(zip(("q", "kv", "bl", "gla", "gates"), refs[1:6]))
        o_ref, stage, sems = refs[-3:]
        j = pl.program_id(0)

        def out_copy(k):
            _, _, n, off = _GW_TILES[k]
            return pltpu.make_async_copy(stage.at[k % 2, 0:n], o_ref.at[pl.ds(off, n)], sems.at[k % 2])

        for k, (piece, _, n, _) in enumerate(_GW_TILES):
            @pl.when(j == k)
            def _(k=k, piece=piece, n=n):
                if k >= 2:
                    out_copy(k - 2).wait()
                t = _gw_unpermute(piece, _dot_tn(srcs[piece][...], h_ref[...]))
                stage[k % 2, 0:n, :] = t.astype(stage.dtype)
                out_copy(k).start()

        @pl.when(j == steps - 1)
        def _():
            out_copy(steps - 2).wait()
            out_copy(steps - 1).wait()

    def tile_of(lo, hi):
        return lambda j: (0, jnp.clip(j - lo, 0, hi - lo - 1))

    in_specs = [pl.BlockSpec((T, DH), lambda j: (0, half)),
                pl.BlockSpec((T, 512), tile_of(0, 2)), pl.BlockSpec((T, 512), lambda j: (0, 0)),
                pl.BlockSpec((T, W_BL), lambda j: (0, 0)),
                pl.BlockSpec((T, 512), tile_of(4, 8)), pl.BlockSpec((T, 512), tile_of(8, 16))]
    args = [h, *pieces]
    if after is not None:
        in_specs.append(_any())
        args.append(after)
    return pl.pallas_call(
        body, name=f"gw_in_half{half}", grid=(steps,),
        in_specs=in_specs, out_specs=_any(),
        out_shape=jax.ShapeDtypeStruct((IN_WIDTH, DH), WIRE),
        scratch_shapes=[pltpu.VMEM((2, 512, DH), WIRE), pltpu.SemaphoreType.DMA((2,))],
        compiler_params=_cp(("arbitrary",)),
    )(*args)


def _chip_copies(s_ref, got_ref, send_sems, recv_sems):
    x, y, c = _place()
    chips = [(1 - x, y), (x, 1 - y), (1 - x, 1 - y)]
    return [pltpu.make_async_remote_copy(
        src_ref=s_ref.at[2 * px + py], dst_ref=got_ref.at[j],
        send_sem=send_sems.at[j], recv_sem=recv_sems.at[j], device_id=(px, py, c), device_id_type=MESH)
        for j, (px, py) in enumerate(chips)]


_EFFECT = pltpu.SideEffectType.DATAFLOW_SIDE_EFFECTING


def _hbm():
    return pl.BlockSpec(memory_space=pltpu.HBM)


def _sem():
    return pl.BlockSpec(memory_space=pltpu.SEMAPHORE)


def _chip_start(sums, half):
    land = pltpu.with_memory_space_constraint(lax.empty((3,) + sums.shape[1:], sums.dtype), pltpu.HBM)

    def body(s_ref, land_ref, send_sems, recv_sems, s_thru, land_thru, token):
        for cp in _chip_copies(s_ref, land_ref, send_sems, recv_sems):
            cp.start()
        token[...] = jnp.zeros_like(token)

    return pl.pallas_call(
        body, name=f"chip_start{half}",
        out_shape=(pltpu.SemaphoreType.DMA((3,)), pltpu.SemaphoreType.DMA((3,)),
                   pltpu.HBM(sums.shape, sums.dtype), pltpu.HBM(land.shape, land.dtype),
                   jax.ShapeDtypeStruct((8, LANE), F32)),
        in_specs=(_hbm(), _hbm()), out_specs=(_sem(), _sem(), _hbm(), _hbm(), _vmem()),
        input_output_aliases={0: 2, 1: 3},
        compiler_params=pltpu.CompilerParams(has_side_effects=_EFFECT),
    )(pltpu.with_memory_space_constraint(sums, pltpu.HBM), land)


def _chip_wait(send_sems, recv_sems, s_thru, land_thru, after, half):
    def body(s_ref, land_ref, send_sems, recv_sems, after_ref, s_out, got_ref):
        copies = _chip_copies(s_ref, land_ref, send_sems, recv_sems)
        for cp in copies:
            cp.wait_send()
        for cp in copies:
            cp.wait_recv()

    return pl.pallas_call(
        body, name=f"chip_wait{half}",
        out_shape=(pltpu.HBM(s_thru.shape, s_thru.dtype), pltpu.HBM(land_thru.shape, land_thru.dtype)),
        in_specs=(_hbm(), _hbm(), _sem(), _sem(), _any()), out_specs=(_hbm(), _hbm()),
        input_output_aliases={0: 0, 1: 1},
        compiler_params=pltpu.CompilerParams(has_side_effects=_EFFECT),
    )(s_thru, land_thru, send_sems, recv_sems, after)


def _dh_norm(pieces, offsets, wf, x, dx2, norm_w, after):
    T = x.shape[0]
    tT = min(T, 256)
    widths = [p.shape[1] for p in pieces]
    npc = len(pieces)

    def body(*refs):
        dp_refs = refs[:npc]
        wf_ref, x_ref, dx2_ref, nw_ref, _, gx_ref, gnw_ref = refs[npc:]

        @pl.when(pl.program_id(0) == 0)
        def _():
            gnw_ref[...] = jnp.zeros_like(gnw_ref)

        dh = jnp.zeros((tT, D), F32)
        for dp_ref, off, w in zip(dp_refs, offsets, widths):
            dh = dh + _dot(dp_ref[...], wf_ref[off:off + w, :])
        xv = x_ref[...]
        r = lax.rsqrt(jnp.mean(xv * xv, axis=-1, keepdims=True) + EPS)
        xh = xv * r
        gnw_ref[...] = gnw_ref[...] + jnp.broadcast_to(jnp.sum(dh * xh, axis=0, keepdims=True), gnw_ref.shape)
        g = dh * nw_ref[...]
        gx_ref[...] = r * (g - xh * jnp.mean(g * xh, axis=-1, keepdims=True)) + dx2_ref[...]

    tile = pl.BlockSpec((tT, D), lambda i: (i, 0))
    return pl.pallas_call(
        body, name="dh_norm", grid=(T // tT,),
        in_specs=[pl.BlockSpec((tT, w), lambda i: (i, 0)) for w in widths]
        + [_vmem(), tile, tile, pl.BlockSpec((1, D), lambda i: (0, 0)), _any()],
        out_specs=[tile, pl.BlockSpec((8, D), lambda i: (0, 0))],
        out_shape=[jax.ShapeDtypeStruct((T, D), F32), jax.ShapeDtypeStruct((8, D), F32)],
        compiler_params=_cp(("arbitrary",)),
    )(*pieces, wf, x, dx2, norm_w, after)


def _adamw_math(w, g, m, v):
    m = ADAM_B1 * m + (1.0 - ADAM_B1) * g
    v = ADAM_B2 * v + (1.0 - ADAM_B2) * (g * g)
    m_hat = m * (1.0 / (1.0 - ADAM_B1 ** ADAM_STEP))
    v_hat = v * (1.0 / (1.0 - ADAM_B2 ** ADAM_STEP))
    delta = -ADAM_LR * (m_hat / (jnp.sqrt(v_hat) + ADAM_EPS) + ADAM_WD * w)
    return delta, m, v


def _fetch_partials(s_ref, got_ref, buf, sems):
    x, y, _ = _place()
    cps = [pltpu.make_async_copy(s_ref.at[2 * x + y], buf.at[0], sems.at[0])]
    cps += [pltpu.make_async_copy(got_ref.at[j], buf.at[1 + j], sems.at[1 + j]) for j in range(3)]
    for cp in cps:
        cp.start()
    for cp in cps:
        cp.wait()


SMALL_AT = dict(norm_w=0, fnw=8, bias=16, bn=24, sinks=32, loss=40)
ROW_AT = (R_IN, R_A, R_B, R_O)


def _finish_small(ws, ms, vs, smalls):
    names = ["norm_w", "fnw", "bias", "bn", "sinks"]
    widths = [ws[n].shape[1] for n in names]

    def body(*refs):
        w_refs, m_refs, v_refs = refs[0:5], refs[5:10], refs[10:15]
        smalls_ref, loss_ref = refs[15], refs[16]
        outs, tot = refs[17:37], refs[37]
        acc = smalls_ref[0]
        for d in range(1, NDEV):
            acc = acc + smalls_ref[d]
        tot[...] = acc
        loss_ref[...] = tot[SMALL_AT["loss"]:SMALL_AT["loss"] + 1, 0:1]
        for p, (nm_, wd) in enumerate(zip(names, widths)):
            r = SMALL_AT[nm_]
            g = tot[r:r + 1, 0:wd]
            d, nm, nv = _adamw_math(w_refs[p][...], g, m_refs[p][...], v_refs[p][...])
            for o, val in zip(outs[4 * p:4 * p + 4], (g, d, nm, nv)):
                o[...] = val

    res = pl.pallas_call(
        body, name="finish_small",
        in_specs=[_vmem()] * 16, out_specs=[_vmem()] * 21,
        out_shape=[jax.ShapeDtypeStruct((1, 1), F32)]
        + [jax.ShapeDtypeStruct((1, wd), F32) for wd in widths for _ in range(4)],
        scratch_shapes=[pltpu.VMEM((SMALL_ROWS, D), F32)],
        compiler_params=_cp(),
    )(*[ws[n] for n in names], *[ms[n] for n in names], *[vs[n] for n in names], smalls)
    return res[0], {n: tuple(res[1 + 4 * p:5 + 4 * p]) for p, n in enumerate(names)}


def _finish(w_rows, m_rows, v_rows, gu_w, gu_m, gu_v, sums, got):
    shapes = [w.shape for w in w_rows]

    def body(*refs):
        wr_refs, mr_refs, vr_refs = refs[0:4], refs[4:8], refs[8:12]
        guw_ref, gum_ref, guv_ref = refs[12:15]
        s_refs, got_refs = refs[15:17], refs[17:19]
        row_outs = refs[19:35]
        gu_outs = refs[35:39]
        buf, gsh, sems = refs[39:]
        x, y, c = _place()
        me_slot = 4 * x + 2 * y + c
        unshift = lax.rem(SHARD_PAD - 2 * me_slot, SHARD_PAD)

        def total(rows, cols):
            g = buf[0, rows, cols].astype(F32)
            for j in range(1, 4):
                g = g + buf[j, rows, cols].astype(F32)
            return g

        def update(p, g, cols):
            d, nm, nv = _adamw_math(wr_refs[p][:, cols], g, mr_refs[p][:, cols], vr_refs[p][:, cols])
            for o, val in zip(row_outs[4 * p:4 * p + 4], (g, d, nm, nv)):
                o[:, cols] = val

        for hf in range(2):
            _fetch_partials(s_refs[hf], got_refs[hf], buf, sems)
            for cc in range(DH // LANE):
                src = slice(cc * LANE, (cc + 1) * LANE)
                cols = slice(hf * DH + cc * LANE, hf * DH + (cc + 1) * LANE)
                gsh[...] = pltpu.roll(total(slice(0, SHARD_PAD), src), unshift, 0)
                update(0, gsh[0:SHARD, :], cols)
                for p in range(1, 4):
                    update(p, total(slice(ROW_AT[p], ROW_AT[p] + 128), src), cols)
            if hf == 0:
                g = total(slice(R_GU, R_GU + RANK), slice(0, 64))
                d, nm, nv = _adamw_math(guw_ref[...], g, gum_ref[...], guv_ref[...])
                for o, val in zip(gu_outs, (g, d, nm, nv)):
                    o[...] = val

    res = pl.pallas_call(
        body, name="finish",
        in_specs=[_vmem()] * 15 + [_any()] * 4,
        out_specs=[_vmem()] * 20,
        out_shape=[jax.ShapeDtypeStruct(s, F32) for s in shapes for _ in range(4)]
        + [jax.ShapeDtypeStruct((RANK, 64), F32)] * 4,
        scratch_shapes=[pltpu.VMEM((4, ROWS, DH), sums[0].dtype), pltpu.VMEM((SHARD_PAD, LANE), F32),
                        pltpu.SemaphoreType.DMA((4,))],
        compiler_params=_cp(),
    )(*w_rows, *m_rows, *v_rows, gu_w, gu_m, gu_v, *sums, *got)
    return tuple(res[0:16]), tuple(res[16:20])


def _place():
    x, y, c = lax.axis_index("x"), lax.axis_index("y"), lax.axis_index("c")
    return x, y, c


def _peers(x, y, c):
    return [(x ^ dx, y ^ dy, c ^ dc) for dx in range(2) for dy in range(2) for dc in range(2) if dx + dy + dc]


def _late_gather_start(blk, after, name="late_gather"):
    land = pltpu.with_memory_space_constraint(lax.empty((NDEV,) + blk.shape, blk.dtype), pltpu.HBM)

    def body(b_ref, land_ref, after_ref, send_sems, recv_sems, b_thru, land_thru, token):
        x, y, c = _place()
        for k, to in enumerate(_peers(x, y, c)):
            pltpu.make_async_remote_copy(
                src_ref=b_ref, dst_ref=land_ref.at[4 * x + 2 * y + c], send_sem=send_sems.at[k],
                recv_sem=recv_sems.at[k], device_id=to, device_id_type=MESH).start()
        token[...] = jnp.zeros_like(token)

    return pl.pallas_call(
        body, name=name + "_start",
        out_shape=(pltpu.SemaphoreType.DMA((7,)), pltpu.SemaphoreType.DMA((7,)),
                   pltpu.HBM(blk.shape, blk.dtype), pltpu.HBM(land.shape, land.dtype),
                   jax.ShapeDtypeStruct((8, LANE), F32)),
        in_specs=(_hbm(), _hbm(), _any()), out_specs=(_sem(), _sem(), _hbm(), _hbm(), _vmem()),
        input_output_aliases={0: 2, 1: 3},
        compiler_params=pltpu.CompilerParams(has_side_effects=_EFFECT),
    )(pltpu.with_memory_space_constraint(blk, pltpu.HBM), land, after)


def _late_gather_wait(send_sems, recv_sems, b_thru, land_thru, after, after2, name="late_gather"):
    def body(b_ref, land_ref, send_sems, recv_sems, after_ref, after2_ref, b_out, got_ref):
        x, y, c = _place()
        copies = [pltpu.make_async_remote_copy(
            src_ref=b_ref, dst_ref=land_ref.at[4 * x + 2 * y + c], send_sem=send_sems.at[k],
            recv_sem=recv_sems.at[k], device_id=to, device_id_type=MESH)
            for k, to in enumerate(_peers(x, y, c))]
        for cp in copies:
            cp.wait_send()
        for cp in copies:
            cp.wait_recv()

    return pl.pallas_call(
        body, name=name + "_wait",
        out_shape=(pltpu.HBM(b_thru.shape, b_thru.dtype), pltpu.HBM(land_thru.shape, land_thru.dtype)),
        in_specs=(_hbm(), _hbm(), _sem(), _sem(), _any(), _any()), out_specs=(_hbm(), _hbm()),
        input_output_aliases={0: 0, 1: 1},
        compiler_params=pltpu.CompilerParams(has_side_effects=_EFFECT),
    )(b_thru, land_thru, send_sems, recv_sems, after, after2)


G_ROWS = SHARD_PAD + RANK


def _gather_blocks(w_in_t, gu_s, xs, norm_w, pos_col):
    rows, cols = G_ROWS, D
    T = xs.shape[0]
    tT = min(T, 256)
    inv_row, sign_row = _rope_rows()

    def body(wi_ref, gu_ref, xs_ref, nw_ref, pos_ref, inv_ref, sign_ref,
             out_ref, h_ref, cos_ref, sin_ref, x_ref, frame_ref, send_sems, recv_sems, local_sem):
        x, y, c = _place()
        me, sibling = (x, y, c), (x, y, 1 - c)
        chips = [(1 - x, y), (x, 1 - y), (1 - x, 1 - y)]
        shift = 2 * (4 * x + 2 * y + c)
        frame_ref[SHARD - SHARD % 8:, :] = jnp.zeros((SHARD_PAD - SHARD + SHARD % 8, D), F32)
        frame_ref[:SHARD, :] = wi_ref[...]
        for cc in range(D // LANE):
            cs = slice(cc * LANE, (cc + 1) * LANE)
            x_ref[0:SHARD_PAD, cs] = pltpu.roll(frame_ref[:, cs], shift, 0).astype(x_ref.dtype)
        x_ref[SHARD_PAD:G_ROWS, :] = jnp.zeros((RANK, D), x_ref.dtype)
        x_ref[SHARD_PAD:G_ROWS, 0:64] = gu_ref[...].astype(x_ref.dtype)

        def slot(px, py, pc):
            return out_ref.at[4 * px + 2 * py + pc]

        def copy(k, block, to, src=None):
            return pltpu.make_async_remote_copy(
                src_ref=slot(*block) if src is None else src, dst_ref=slot(*block),
                send_sem=send_sems.at[k], recv_sem=recv_sems.at[k], device_id=to, device_id_type=MESH)

        mine = pltpu.make_async_copy(x_ref, slot(*me), local_sem)
        mine.start()
        first = [copy(0, me, sibling, src=x_ref)]
        first += [copy(1 + j, me, (*chip, c), src=x_ref) for j, chip in enumerate(chips)]
        for cp in first:
            cp.start()

        @pl.loop(0, T // tT)
        def _(i):
            rows_i = pl.ds(pl.multiple_of(i * tT, tT), tT)
            _prologue_rows(rows_i, xs_ref, nw_ref, pos_ref, inv_ref, sign_ref, h_ref, cos_ref, sin_ref)

        passed = [copy(4 + j, (*chip, c), sibling) for j, chip in enumerate(chips)]
        for j, chip in enumerate(chips):
            copy(1 + j, (*chip, c), me).wait_recv()
            passed[j].start()
        copy(0, sibling, me).wait_recv()
        for j, chip in enumerate(chips):
            copy(4 + j, (*chip, 1 - c), me).wait_recv()
        for cp in first + passed:
            cp.wait_send()
        mine.wait()

    return pl.pallas_call(
        body, name="gather_weights",
        in_specs=[_vmem()] * 7, out_specs=[_any()] + [_vmem()] * 3,
        out_shape=[jax.ShapeDtypeStruct((NDEV, rows, cols), WIRE), jax.ShapeDtypeStruct((T, D), MXU),
                   jax.ShapeDtypeStruct((T, LANE), F32), jax.ShapeDtypeStruct((T, LANE), F32)],
        scratch_shapes=[pltpu.VMEM((rows, cols), WIRE), pltpu.VMEM((SHARD_PAD, D), F32),
                        pltpu.SemaphoreType.DMA((7,)), pltpu.SemaphoreType.DMA((7,)), pltpu.SemaphoreType.DMA],
        compiler_params=_cp(),
    )(w_in_t, gu_s, xs, norm_w, pos_col, inv_row, sign_row)


def _pair_reduce(gwt, tail):
    n = gwt.shape[1]
    rows = SHARD_PAD + (0 if tail is None else tail.shape[1])
    blk = (4, rows, n)

    def body(*refs):
        g_ref = refs[0]
        t_ref = None if tail is None else refs[1]
        out_ref, got, own, send_sems, recv_sems, own_sems = refs[-6:]
        x, y, c = _place()

        def parts(d, dst):
            frame = g_ref.at[pl.ds(pl.multiple_of(FRAME * d, 16), SHARD_PAD)]
            pieces = [(frame, dst.at[0:SHARD_PAD])]
            return pieces if tail is None else pieces + [(t_ref.at[d], dst.at[SHARD_PAD:rows])]

        sends, loads = [], []
        for chip in range(4):
            sends.append([pltpu.make_async_remote_copy(
                src_ref=s, dst_ref=d_, send_sem=send_sems.at[chip, k], recv_sem=recv_sems.at[chip, k],
                device_id=(x, y, 1 - c), device_id_type=MESH)
                for k, (s, d_) in enumerate(parts(2 * chip + (1 - c), got.at[chip]))])
            loads.append([pltpu.make_async_copy(s, d_, own_sems.at[chip, k])
                          for k, (s, d_) in enumerate(parts(2 * chip + c, own.at[chip]))])
        for group in sends + loads:
            for cp in group:
                cp.start()
        for chip in range(4):
            for cp in loads[chip]:
                cp.wait()
            for cp in sends[chip]:
                cp.wait_recv()
            out_ref[chip] = (own[chip].astype(F32) + got[chip].astype(F32)).astype(out_ref.dtype)
        for group in sends:
            for cp in group:
                cp.wait_send()

    args = [gwt] if tail is None else [gwt, tail]
    return pl.pallas_call(
        body, name="pair_reduce",
        in_specs=[_any()] * len(args), out_specs=_vmem(),
        out_shape=jax.ShapeDtypeStruct(blk, gwt.dtype),
        scratch_shapes=[pltpu.VMEM(blk, gwt.dtype), pltpu.VMEM(blk, gwt.dtype),
                        pltpu.SemaphoreType.DMA((4, 2)), pltpu.SemaphoreType.DMA((4, 2)), pltpu.SemaphoreType.DMA((4, 2))],
        compiler_params=_cp(),
    )(*args)


def _pad_cols(a, cols):
    return jnp.pad(a, ((0, 0), (0, cols - a.shape[1])))


def _pad_rows(a, rows):
    return jnp.pad(a, ((0, rows - a.shape[0]), (0, 0)))


FRAME = 928


def _wft_plan():
    moves = []
    for blk in range(8):
        for half in range(2):
            for sub in range(2):
                moves.append((C_Q + 128 * blk + 32 * (2 * half + sub), 128 * blk + 32 * (2 * sub + half), 32))
    for idx in range(4):
        for dup in range(2):
            moves.append((C_KD + 64 * idx + 32 * dup, 1024 + 32 * idx, 32))
    for g in range(2):
        for dup in range(2):
            moves.append((C_VD + 128 * g + 64 * dup, 1152 + 64 * g, 64))
    moves += [(C_BL, 5376, RANK), (C_BV, 3328, 1024), (C_BQ, 2304, 512), (C_BK, 2816, 512),
              (C_AG, 1280, 1024), (C_BG, 4352, 1024), (C_MA, 5392, 1024), (C_MB, 6416, 1024)]
    bulk, seams = [], []
    for dst, src, n in moves:
        r = src
        while r < src + n:
            f = min(r // FRAME, NDEV - 1)
            local = r - FRAME * f
            if f > 0 and local < 16:
                assert local == 0
                seams.append((f, dst + r - src))
                step = 16
            else:
                step = min(src + n, FRAME * (f + 1) if f < NDEV - 1 else IN_WIDTH) - r
                bulk.append((f, local, dst + r - src, step))
            r += step
    assert sorted(f for f, _ in seams) == list(range(1, NDEV))
    return bulk, seams, [(C_BL + RANK, C_GLA - C_BL - RANK)]


def _build_wft_copies(frames):
    bulk, seams, zeros = _wft_plan()
    (z0, zn), = zeros

    def body(f_ref, o_ref, edge, sems, esems):
        copies = [pltpu.make_async_copy(f_ref.at[f, pl.ds(l0, n)], o_ref.at[pl.ds(dst, n)], sems.at[i])
                  for i, (f, l0, dst, n) in enumerate(bulk)]
        loads = []
        for i, (f, _) in enumerate(seams):
            loads.append(pltpu.make_async_copy(f_ref.at[f, pl.ds(0, 16)], edge.at[i, 0], esems.at[i, 0]))
            loads.append(pltpu.make_async_copy(f_ref.at[f - 1, pl.ds(FRAME, 16)], edge.at[i, 1], esems.at[i, 1]))
        for cp in copies + loads:
            cp.start()
        o_ref[z0:z0 + zn, :] = jnp.zeros((zn, D), o_ref.dtype)
        for cp in loads:
            cp.wait()
        for i, (_, dst) in enumerate(seams):
            o_ref[dst:dst + 16, :] = edge[i, 0] + edge[i, 1]
        for cp in copies:
            cp.wait()

    return pl.pallas_call(
        body, name="build_wft",
        in_specs=[_any()], out_specs=_vmem(),
        out_shape=jax.ShapeDtypeStruct((NF, D), frames.dtype),
        scratch_shapes=[pltpu.VMEM((len(seams), 2, 16, D), frames.dtype),
                        pltpu.SemaphoreType.DMA((len(bulk),)), pltpu.SemaphoreType.DMA((len(seams), 2))],
        compiler_params=_cp(),
    )(frames)


def kernel(x, positions, norm_w, w_in, a_sinks, b_gate_up, b_gate_bias, b_out_norm_w, w_a_proj, w_b_proj, w_out, final_norm_w, loss_target, m_norm_w, m_w_in, m_a_sinks, m_b_gate_up, m_b_gate_bias, m_b_out_norm_w, m_w_a_proj, m_w_b_proj, m_w_out, m_final_norm_w, v_norm_w, v_w_in, v_a_sinks, v_b_gate_up, v_b_gate_bias, v_b_out_norm_w, v_w_a_proj, v_w_b_proj, v_w_out, v_final_norm_w):
    T = x.shape[1]
    xs, target = x[0], loss_target[0]
    fnw = final_norm_w.reshape(1, D)
    me = 4 * lax.axis_index("x") + 2 * lax.axis_index("y") + lax.axis_index("c")
    allw, h, cos, sin = _gather_blocks(w_in[0].T, b_gate_up[0], xs, norm_w, positions.reshape(T, 1))
    late_blk = jnp.concatenate([w_a_proj[0], w_b_proj[0], w_out[0]], axis=0).astype(WIRE)
    l_send, l_recv, l_blk, l_land, l_started = _late_gather_start(late_blk, cos)
    wf = _build_wft_copies(allw)
    gu = allw[:, SHARD_PAD:G_ROWS, :64].transpose(1, 0, 2).reshape(RANK, 512)
    gu_pad = _pad_rows(gu, W_BL)

    proj = _proj(h, wf, l_started)
    o_a, lse = _swa_fwd(proj, cos, sin, a_sinks)
    o_b, states = _gla_fwd(proj, gu_pad, b_gate_bias)
    l_blk, l_land = _late_gather_wait(l_send, l_recv, l_blk, l_land, states, lse)
    late = lax.dynamic_update_slice(l_land, l_blk[None], (me, 0, 0))
    w_a, w_b, w_o = (late[:, 128 * i:128 * (i + 1), :].reshape(D, D) for i in range(3))
    (dx2, do_a, do_b, d_gates, g_wa, g_wb, g_wo, g_fn, g_bn, loss_part) = _mid(
        xs, target, proj, o_a, o_b, w_a, w_b, w_o, jnp.tile(b_out_norm_w, (1, B_HEADS)), fnw)
    d_q, d_kv, g_sinks = _swa_bwd(proj, cos, sin, a_sinks, do_a, o_a, lse, cos)
    d_gla, d_bl, g_gu, g_bias = _gla_bwd(proj, gu_pad, b_gate_bias, states, do_b)
    pieces = [d_q, d_kv, d_bl, d_gla, d_gates]
    offsets = [C_Q, C_KD, C_BL, C_GLA, C_GATES]

    ggu = g_gu[:RANK].reshape(RANK, NDEV, 64).transpose(1, 0, 2)
    ggu_half = [jnp.pad(ggu, ((0, 0), (0, 0), (0, DH - 64))), jnp.zeros((NDEV, RANK, DH), F32)]

    def tail(hf):
        cols = slice(hf * DH, (hf + 1) * DH)
        return jnp.concatenate([g[:, cols].reshape(NDEV, 128, DH) for g in (g_wa, g_wb, g_wo)]
                               + [ggu_half[hf]], axis=1).astype(WIRE)

    send0, recv0, s_thru0, land0, started0 = _chip_start(_pair_reduce(_gw_half(h, pieces, 0), tail(0)), 0)
    send1, recv1, s_thru1, land1, started1 = _chip_start(
        _pair_reduce(_gw_half(h, pieces, 1, after=started0), tail(1)), 1)
    grad_x, g_nw = _dh_norm(pieces, offsets, wf, xs, dx2, norm_w, started1)
    small = jnp.concatenate([g_nw, g_fn, _pad_cols(g_bias, D), _pad_cols(g_bn, D), _pad_cols(g_sinks, D),
                             _pad_cols(loss_part, D)], axis=0)
    sm_send, sm_recv, sm_blk, sm_land, sm_started = _late_gather_start(small, g_nw, name="small_gather")
    sums0, got0 = _chip_wait(send0, recv0, s_thru0, land0, sm_started, 0)
    sums1, got1 = _chip_wait(send1, recv1, s_thru1, land1, got0, 1)
    sums, from_chips = [sums0, sums1], [got0, got1]

    ws = dict(norm_w=norm_w, fnw=fnw, bias=b_gate_bias, bn=b_out_norm_w, sinks=a_sinks)
    ms = dict(norm_w=m_norm_w, fnw=m_final_norm_w.reshape(1, D), bias=m_b_gate_bias, bn=m_b_out_norm_w,
              sinks=m_a_sinks)
    vs = dict(norm_w=v_norm_w, fnw=v_final_norm_w.reshape(1, D), bias=v_b_gate_bias, bn=v_b_out_norm_w,
              sinks=v_a_sinks)
    t_rows, t_gu = _finish(
        [w_in[0].T, w_a_proj[0], w_b_proj[0], w_out[0]], [m_w_in[0].T, m_w_a_proj[0], m_w_b_proj[0], m_w_out[0]],
        [v_w_in[0].T, v_w_a_proj[0], v_w_b_proj[0], v_w_out[0]],
        b_gate_up[0], m_b_gate_up[0], v_b_gate_up[0], sums, from_chips)
    sm_blk, sm_land = _late_gather_wait(sm_send, sm_recv, sm_blk, sm_land, t_rows[0], t_gu[0], name="small_gather")
    loss, sm = _finish_small(ws, ms, vs, lax.dynamic_update_slice(sm_land, sm_blk[None], (me, 0, 0)))

    def outputs(k):
        return [sm["norm_w"][k], t_rows[k].T[None], sm["sinks"][k], t_gu[k][None], sm["bias"][k], sm["bn"][k],
                t_rows[4 + k][None], t_rows[8 + k][None], t_rows[12 + k][None], sm["fnw"][k].reshape(D)]

    return (loss[0, 0], grad_x[None], *outputs(0), *outputs(1), *outputs(2), *outputs(3))
```

```python
import functools

import numpy as np
import jax
import jax.numpy as jnp
from jax import lax
from jax.experimental import pallas as pl
from jax.experimental.pallas import tpu as pltpu

F32 = jnp.float32
MXU = jnp.bfloat16
WIRE = jnp.bfloat16

D = 1024
A_HEADS, A_KV, A_HD = 16, 2, 64
BLK = 128
B_HEADS, B_DK, B_DV = 4, 128, 256
RANK, TAU, CHUNK = 16, 16.0, 64
EPS, NEG = 1e-5, -1e30
ROPE_THETA = 10000.0
IN_WIDTH, NDEV = 7440, 8
SHARD = IN_WIDTH // NDEV
LANE = 128

C_Q, C_KD, C_VD, C_BL = 0, 1024, 1280, 1536
C_BV, C_BQ, C_BK = 2048, 3072, 3584
C_AG, C_BG, C_MA, C_MB = 4096, 5120, 6144, 7168
C_GLA, W_GLA, C_GATES, W_GATES = 2048, 2048, 4096, 4096
NF = 8192
W_BL = 128

SHARD_PAD = 944
R_IN, R_A, R_B, R_O, R_GU, ROWS = 0, 944, 1072, 1200, 1328, 1344
SMALL_ROWS = 48

ADAM_LR, ADAM_B1, ADAM_B2, ADAM_EPS, ADAM_WD, ADAM_STEP = 0.001, 0.9, 0.999, 1e-08, 0.01, 10

MESH = pl.DeviceIdType.MESH
VMEM_LIMIT = 56 * 1024 * 1024


def _cp(sem=None, **kw):
    if sem is not None:
        kw["dimension_semantics"] = sem
    return pltpu.CompilerParams(vmem_limit_bytes=VMEM_LIMIT, **kw)


def _dot(a, b):
    return jnp.dot(a, b, preferred_element_type=F32)


def _dot_nt(a, b):
    return lax.dot_general(a, b, (((1,), (1,)), ((), ())), preferred_element_type=F32)


def _dot_tn(a, b):
    return lax.dot_general(a, b, (((0,), (0,)), ((), ())), preferred_element_type=F32)


def _dot_f32(a, b):
    return jnp.dot(a, b, preferred_element_type=F32, precision=lax.Precision.HIGHEST)


def _sigmoid(z):
    return 0.5 * jnp.tanh(0.5 * z) + 0.5


def _rope(xp, cos, sin):
    return xp * cos + pltpu.roll(xp, 64, 1) * sin


def _rope_bwd(dy, cos, sin):
    return dy * cos - pltpu.roll(dy, 64, 1) * sin


def _vmem():
    return pl.BlockSpec(memory_space=pltpu.VMEM)


def _any():
    return pl.BlockSpec(memory_space=pl.ANY)


def _rope_rows():
    half = A_HD // 2
    inv = (np.float32(ROPE_THETA) ** (-np.arange(half, dtype=np.float32) / np.float32(half))).astype(np.float32)
    inv_row = jnp.asarray(np.tile(inv, 4)[None, :])
    sign_row = jnp.asarray(np.concatenate([-np.ones(64, np.float32), np.ones(64, np.float32)])[None, :])
    return inv_row, sign_row


def _prologue_rows(rows, x_ref, nw_ref, pos_ref, inv_ref, sign_ref, h_ref, cos_ref, sin_ref):
    xv = x_ref[rows, :]
    r = lax.rsqrt(jnp.mean(xv * xv, axis=-1, keepdims=True) + EPS)
    h_ref[rows, :] = ((xv * r) * nw_ref[...]).astype(h_ref.dtype)
    ang = pos_ref[rows, :].astype(F32) * inv_ref[...]
    cos_ref[rows, :] = jnp.cos(ang)
    sin_ref[rows, :] = jnp.sin(ang) * sign_ref[...]


def _proj(h, wft, after):
    T = h.shape[0]
    tT, tN = T, 512

    def body(h_ref, w_ref, after_ref, o_ref):
        o_ref[...] = _dot_nt(h_ref[...], w_ref[...])

    return pl.pallas_call(
        body, name="proj", grid=(T // tT, NF // tN),
        in_specs=[pl.BlockSpec((tT, D), lambda i, j: (i, 0)), pl.BlockSpec((tN, D), lambda i, j: (j, 0)), _any()],
        out_specs=pl.BlockSpec((tT, tN), lambda i, j: (i, j)),
        out_shape=jax.ShapeDtypeStruct((T, NF), F32),
        compiler_params=_cp(("parallel", "parallel")),
    )(h, wft, after)


def _swa_masks():
    lane = lax.broadcasted_iota(jnp.int32, (BLK, LANE), 1)
    rope_sub0 = ((lane // 32) % 2) == 0
    std_sub0 = lane < 64
    return lane, rope_sub0, std_sub0


def _swa_tri():
    qi = lax.broadcasted_iota(jnp.int32, (BLK, BLK), 0)
    kj = lax.broadcasted_iota(jnp.int32, (BLK, BLK), 1)
    return kj <= qi


def _swa_fold(full, tri):
    return jnp.where(tri, full[:, BLK:], full[:, :BLK])


def _swa_unfold(sq, tri):
    return jnp.concatenate([jnp.where(tri, 0.0, sq), jnp.where(tri, sq, 0.0)], axis=1)


def _swa_keys(kc_ref, kp_ref, vc_ref, vp_ref, cq, sq, cp, sp):
    def ropek(kref, c, s):
        kv = kref[...]
        return jnp.concatenate([_rope(kv[:, :LANE], c, s), _rope(kv[:, LANE:], c, s)], axis=1)

    K = jnp.concatenate([ropek(kp_ref, cp, sp), ropek(kc_ref, cq, sq)], axis=0).astype(MXU)
    V = jnp.concatenate([vp_ref[...], vc_ref[...]], axis=0).astype(MXU)
    return K, V


def _swa_in_specs(nb, last):
    def cur(n):
        return jnp.minimum(n, last)

    def prev(n):
        return jnp.maximum(cur(n) - 1, 0)

    kd, vd = C_KD // 256, C_VD // 256
    return [
        pl.BlockSpec((BLK, D), lambda n: (cur(n), C_Q // D)),
        pl.BlockSpec((BLK, 256), lambda n: (cur(n), kd)),
        pl.BlockSpec((BLK, 256), lambda n: (prev(n), kd)),
        pl.BlockSpec((BLK, 256), lambda n: (cur(n), vd)),
        pl.BlockSpec((BLK, 256), lambda n: (prev(n), vd)),
        pl.BlockSpec((BLK, LANE), lambda n: (cur(n), 0)),
        pl.BlockSpec((BLK, LANE), lambda n: (cur(n), 0)),
        pl.BlockSpec((BLK, LANE), lambda n: (prev(n), 0)),
        pl.BlockSpec((BLK, LANE), lambda n: (prev(n), 0)),
    ]


def _swa_fwd(proj, cos, sin, sinks):
    T = proj.shape[0]
    nb = T // BLK
    scale = A_HD ** -0.5

    def body(sinks_ref, q_ref, kc_ref, kp_ref, vc_ref, vp_ref, cq_ref, sq_ref, cp_ref, sp_ref, o_ref, l_ref):
        n = pl.program_id(0)
        cq, sq = cq_ref[...], sq_ref[...]
        K, V = _swa_keys(kc_ref, kp_ref, vc_ref, vp_ref, cq, sq, cp_ref[...], sp_ref[...])
        tri = _swa_tri()
        valid = tri | (n > 0)
        lane, rope_sub0, std_sub0 = _swa_masks()
        group = A_HEADS // A_KV
        roped, lses = {}, []

        def products(head):
            pb, sub, g = head // 2, head % 2, head // group
            if sub == 0:
                roped[pb] = _rope(q_ref[:, pb * LANE:(pb + 1) * LANE], cq, sq)
            qm = jnp.where(rope_sub0 if sub == 0 else ~rope_sub0, roped[pb], 0.0).astype(MXU)
            return _dot_nt(qm, K[:, g * LANE:(g + 1) * LANE])

        def softmax(head, s_full):
            s = jnp.where(valid, _swa_fold(s_full, tri) * scale, NEG)
            sink = sinks_ref[0, head]
            m = jnp.maximum(jnp.max(s, axis=1, keepdims=True), sink)
            e = jnp.exp(s - m)
            den = jnp.sum(e, axis=1, keepdims=True) + jnp.exp(sink - m)
            lses.append(m + jnp.log(den))
            return _swa_unfold(e / den, tri).astype(MXU)

        outs = {}
        st1 = {0: products(0), 1: products(1)}
        st2 = {0: softmax(0, st1.pop(0))}
        for head in range(A_HEADS):
            if head + 2 < A_HEADS:
                st1[head + 2] = products(head + 2)
            if head + 1 < A_HEADS:
                st2[head + 1] = softmax(head + 1, st1.pop(head + 1))
            g = head // group
            outs[head] = _dot(st2.pop(head), V[:, g * LANE:(g + 1) * LANE])
            if head % 2 == 1:
                pb = head // 2
                o_ref[:, pb * LANE:(pb + 1) * LANE] = jnp.where(std_sub0, outs[head - 1], outs[head])
        lacc = jnp.zeros((BLK, LANE), F32)
        for head in range(A_HEADS):
            lacc = jnp.where(lane == head, lses[head], lacc)
        l_ref[...] = lacc

    return pl.pallas_call(
        body, name="swa_fwd", grid=(nb,),
        in_specs=[pl.BlockSpec(memory_space=pltpu.SMEM)] + _swa_in_specs(nb, nb - 1),
        out_specs=[pl.BlockSpec((BLK, D), lambda n: (n, 0)), pl.BlockSpec((BLK, LANE), lambda n: (n, 0))],
        out_shape=[jax.ShapeDtypeStruct((T, D), F32), jax.ShapeDtypeStruct((T, LANE), F32)],
        compiler_params=_cp(("parallel",)),
    )(sinks, proj, proj, proj, proj, proj, cos, sin, cos, sin)


def _swa_bwd(proj, cos, sin, sinks, do_a, o_a, lse, after):
    T = proj.shape[0]
    nb = T // BLK
    scale = A_HD ** -0.5

    def body(sinks_ref, q_ref, kc_ref, kp_ref, vc_ref, vp_ref, cq_ref, sq_ref, cp_ref, sp_ref,
             do_ref, o_ref, l_ref, after_ref, dq_ref, dkv_ref, ds_ref, ckv_ref):
        n = pl.program_id(0)

        @pl.when(n == 0)
        def _():
            ckv_ref[...] = jnp.zeros_like(ckv_ref)
            ds_ref[...] = jnp.zeros_like(ds_ref)

        @pl.when(n < nb)
        def _():
            cq, sq, cp, sp = cq_ref[...], sq_ref[...], cp_ref[...], sp_ref[...]
            K, V = _swa_keys(kc_ref, kp_ref, vc_ref, vp_ref, cq, sq, cp, sp)
            tri = _swa_tri()
            valid = tri | (n > 0)
            lane, rope_sub0, std_sub0 = _swa_masks()
            lane_row = lax.broadcasted_iota(jnp.int32, (1, LANE), 1)
            lse_v = l_ref[...]
            dKt = [jnp.zeros((LANE, 2 * BLK), F32) for _ in range(A_KV)]
            dVt = [jnp.zeros((LANE, 2 * BLK), F32) for _ in range(A_KV)]
            dsinks, roped, roped_t, do_t = [], {}, {}, {}
            group = A_HEADS // A_KV
            dim = lax.broadcasted_iota(jnp.int32, (LANE, BLK), 0)
            rope_row0, std_row0 = ((dim // 32) % 2) == 0, dim < 64

            def products(head):
                pb, sub, g = head // 2, head % 2, head // group
                cols = slice(pb * LANE, (pb + 1) * LANE)
                Kg, Vg = K[:, g * LANE:(g + 1) * LANE], V[:, g * LANE:(g + 1) * LANE]
                if sub == 0:
                    roped[pb] = _rope(q_ref[:, cols], cq, sq)
                    roped_t[pb] = roped[pb].T
                    do_t[pb] = do_ref[:, cols].T
                qm = jnp.where(rope_sub0 if sub == 0 else ~rope_sub0, roped[pb], 0.0).astype(MXU)
                qmt = jnp.where(rope_row0 if sub == 0 else ~rope_row0, roped_t[pb], 0.0).astype(MXU)
                dov = jnp.where(std_sub0 if sub == 0 else ~std_sub0, do_ref[:, cols], 0.0)
                dovt = jnp.where(std_row0 if sub == 0 else ~std_row0, do_t[pb], 0.0).astype(MXU)
                delta = jnp.sum(dov * o_ref[:, cols], axis=1, keepdims=True)
                return qmt, dovt, delta, _dot_nt(qm, Kg), _dot_nt(dov.astype(MXU), Vg)

            def scores(head, qmt, dovt, delta, s_full, dp_full):
                lh = jnp.sum(jnp.where(lane == head, lse_v, 0.0), axis=1, keepdims=True)
                p = jnp.where(valid, jnp.exp(_swa_fold(s_full, tri) * scale - lh), 0.0)
                psink = jnp.exp(sinks_ref[0, head] - lh)
                dsinks.append(jnp.sum(-psink * delta, axis=0, keepdims=True))
                dsq = (p * (_swa_fold(dp_full, tri) - delta)) * scale
                return qmt, dovt, _swa_unfold(p, tri).astype(MXU), _swa_unfold(dsq, tri).astype(MXU)

            def grads(head, qmt, dovt, pb16, dsc):
                g = head // group
                dKt[g] = dKt[g] + _dot(qmt, dsc)
                dVt[g] = dVt[g] + _dot(dovt, pb16)
                return _dot(dsc, K[:, g * LANE:(g + 1) * LANE])

            dqs = {}
            st1 = {0: products(0), 1: products(1)}
            st2 = {0: scores(0, *st1.pop(0))}
            for head in range(A_HEADS):
                if head + 2 < A_HEADS:
                    st1[head + 2] = products(head + 2)
                if head + 1 < A_HEADS:
                    st2[head + 1] = scores(head + 1, *st1.pop(head + 1))
                dqs[head] = grads(head, *st2.pop(head))
                if head % 2 == 1:
                    pb = head // 2
                    dqp = jnp.where(rope_sub0, dqs[head - 1], dqs[head])
                    dq_ref[:, pb * LANE:(pb + 1) * LANE] = _rope_bwd(dqp, cq, sq).astype(dq_ref.dtype)
            dsink = jnp.zeros((1, LANE), F32)
            for head in range(A_HEADS):
                dsink = jnp.where(lane_row == head, dsinks[head], dsink)
            dK, dV = [a.T for a in dKt], [a.T for a in dVt]
            prev = ([_rope_bwd(dK[g][:BLK], cp, sp) for g in range(A_KV)] + [dV[g][:BLK] for g in range(A_KV)])
            cur_ = ([_rope_bwd(dK[g][BLK:], cq, sq) for g in range(A_KV)] + [dV[g][BLK:] for g in range(A_KV)])
            dkv_ref[...] = (ckv_ref[...] + jnp.concatenate(prev, axis=1)).astype(dkv_ref.dtype)
            ckv_ref[...] = jnp.concatenate(cur_, axis=1)
            ds_ref[...] = ds_ref[...] + jnp.broadcast_to(dsink, ds_ref.shape)

        @pl.when(n == nb)
        def _():
            dkv_ref[...] = ckv_ref[...].astype(dkv_ref.dtype)

    last = nb - 1

    def cur(n):
        return jnp.minimum(n, last)

    def out_kv(n):
        return (jnp.maximum(n - 1, 0), 0)

    return pl.pallas_call(
        body, name="swa_bwd", grid=(nb + 1,),
        in_specs=[pl.BlockSpec(memory_space=pltpu.SMEM)] + _swa_in_specs(nb, last) + [
            pl.BlockSpec((BLK, D), lambda n: (cur(n), 0)),
            pl.BlockSpec((BLK, D), lambda n: (cur(n), 0)),
            pl.BlockSpec((BLK, LANE), lambda n: (cur(n), 0)),
            _any(),
        ],
        out_specs=[
            pl.BlockSpec((BLK, D), lambda n: (cur(n), 0)),
            pl.BlockSpec((BLK, 512), out_kv),
            pl.BlockSpec((8, LANE), lambda n: (0, 0)),
        ],
        out_shape=[
            jax.ShapeDtypeStruct((T, D), MXU),
            jax.ShapeDtypeStruct((T, 512), MXU),
            jax.ShapeDtypeStruct((8, LANE), F32),
        ],
        scratch_shapes=[pltpu.VMEM((BLK, 512), F32)],
        compiler_params=_cp(("arbitrary",)),
    )(sinks, proj, proj, proj, proj, proj, cos, sin, cos, sin, do_a, o_a, lse, after)


NCH = 4
GSTEP = NCH * CHUNK
ST_ROWS = B_HEADS * B_DV


def _chunk_rows(c):
    return slice(c * CHUNK, (c + 1) * CHUNK)


def _per_chunk(which, vals):
    out = vals[-1]
    for c in range(NCH - 2, -1, -1):
        out = jnp.where(which == c, vals[c], out)
    return out


def _gla_gate(bl_ref, gu_ref, bias_ref):
    gk = _dot(bl_ref[...].astype(MXU), gu_ref[...]) + bias_ref[...]
    la = (jnp.minimum(gk, 0.0) - jnp.log(1.0 + jnp.exp(-jnp.abs(gk)))) / TAU
    ri = lax.broadcasted_iota(jnp.int32, (GSTEP, GSTEP), 0)
    ci = lax.broadcasted_iota(jnp.int32, (GSTEP, GSTEP), 1)
    same = (ri // CHUNK) == (ci // CHUNK)
    lower, upper = same & (ci <= ri), same & (ci >= ri)
    b = _dot_f32(jnp.where(lower, 1.0, 0.0).astype(F32), la)
    which = lax.broadcasted_iota(jnp.int32, (GSTEP, 1), 0) // CHUNK
    return gk, la, b, lower, upper, which


def _gla_head(q_ref, k_ref, la, b, which, h):
    sl = slice(h * B_DK, (h + 1) * B_DK)
    bh, lah = b[:, sl], la[:, sl]
    bls = [jnp.sum(lah[_chunk_rows(c)], axis=0, keepdims=True) for c in range(NCH)]
    blast = _per_chunk(which, bls)
    qc = q_ref[:, sl] * (B_DK ** -0.5)
    kh = k_ref[:, sl]
    eb, enb, esb = jnp.exp(bh), jnp.exp(-bh), jnp.exp(blast - bh)
    return qc * eb, kh * enb, kh * esb, eb, enb, esb, [jnp.exp(v) for v in bls]


def _gla_specs(step_of):
    return [
        pl.BlockSpec((GSTEP, 512), lambda i: (step_of(i), C_BQ // 512)),
        pl.BlockSpec((GSTEP, 512), lambda i: (step_of(i), C_BK // 512)),
        pl.BlockSpec((GSTEP, D), lambda i: (step_of(i), C_BV // D)),
        pl.BlockSpec((GSTEP, W_BL), lambda i: (step_of(i), C_BL // W_BL)),
        pl.BlockSpec((W_BL, 512), lambda i: (0, 0)),
        pl.BlockSpec((1, 512), lambda i: (0, 0)),
    ]


def _gla_fwd(proj, gu_pad, bias):
    T = proj.shape[0]
    ns = T // GSTEP

    def body(q_ref, k_ref, v_ref, bl_ref, gu_ref, bias_ref, o_ref, st_ref, state_ref):
        @pl.when(pl.program_id(0) == 0)
        def _():
            state_ref[...] = jnp.zeros_like(state_ref)

        _, la, b, lower, _, which = _gla_gate(bl_ref, gu_ref, bias_ref)

        def within(h):
            q_e, k_e, k_s, _, _, _, decays = _gla_head(q_ref, k_ref, la, b, which, h)
            vh = v_ref[:, h * B_DV:(h + 1) * B_DV].astype(MXU)
            q_eb = q_e.astype(MXU)
            att = jnp.where(lower, _dot_nt(q_eb, k_e.astype(MXU)), 0.0)
            return vh, q_eb, k_s.astype(MXU), _dot(att.astype(MXU), vh), decays

        def across(h, vh, q_eb, k_sb, o_intra, decays):
            rows = slice(h * B_DV, (h + 1) * B_DV)
            s = state_ref[rows, :]
            outs = []
            for c in range(NCH):
                cr = _chunk_rows(c)
                st_ref[c * ST_ROWS + h * B_DV:c * ST_ROWS + (h + 1) * B_DV, :] = s
                outs.append(o_intra[cr] + _dot_nt(q_eb[cr], s.astype(MXU)))
                s = s * decays[c] + _dot_tn(vh[cr], k_sb[cr])
            state_ref[rows, :] = s
            o_ref[:, rows] = jnp.concatenate(outs, axis=0)

        for h in range(B_HEADS):
            across(h, *within(h))

    return pl.pallas_call(
        body, name="gla_fwd", grid=(ns,),
        in_specs=_gla_specs(lambda i: i),
        out_specs=[pl.BlockSpec((GSTEP, D), lambda i: (i, 0)),
                   pl.BlockSpec((NCH * ST_ROWS, B_DK), lambda i: (i, 0))],
        out_shape=[jax.ShapeDtypeStruct((T, D), F32),
                   jax.ShapeDtypeStruct((ns * NCH * ST_ROWS, B_DK), F32)],
        scratch_shapes=[pltpu.VMEM((ST_ROWS, B_DK), F32)],
        compiler_params=_cp(("arbitrary",)),
    )(proj, proj, proj, proj, gu_pad, bias)


def _gla_bwd(proj, gu_pad, bias, states, do_b):
    T = proj.shape[0]
    ns = T // GSTEP
    o_q, o_k = C_BQ - C_GLA, C_BK - C_GLA

    def body(q_ref, k_ref, v_ref, bl_ref, gu_ref, bias_ref, st_ref, do_ref,
             dg_ref, dbl_ref, ggu_ref, gbias_ref, gt_ref):
        @pl.when(pl.program_id(0) == 0)
        def _():
            gt_ref[...] = jnp.zeros_like(gt_ref)
            ggu_ref[...] = jnp.zeros_like(ggu_ref)
            gbias_ref[...] = jnp.zeros_like(gbias_ref)

        gk, la, b, lower, upper_mask, which = _gla_gate(bl_ref, gu_ref, bias_ref)
        upper = jnp.where(upper_mask, 1.0, 0.0).astype(F32)
        dla_parts = []

        def within(h):
            q_e, k_e, k_s, eb, enb, esb, decays = _gla_head(q_ref, k_ref, la, b, which, h)
            vh = v_ref[:, h * B_DV:(h + 1) * B_DV].astype(MXU)
            doh = do_ref[:, h * B_DV:(h + 1) * B_DV].astype(MXU)
            q_eb, k_eb = q_e.astype(MXU), k_e.astype(MXU)
            att = jnp.where(lower, _dot_nt(q_eb, k_eb), 0.0).astype(MXU)
            datt = jnp.where(lower, _dot_nt(doh, vh), 0.0).astype(MXU)
            return (q_e, k_e, k_s, eb, enb, esb, decays, vh, doh, q_eb, k_s.astype(MXU),
                    _dot(datt, k_eb), _dot_tn(datt, q_eb), _dot_tn(att, doh))

        def across(h, q_e, k_e, k_s, eb, enb, esb, decays, vh, doh, q_eb, k_sb, dq_i, dk_e, dv_i):
            rows = slice(h * B_DV, (h + 1) * B_DV)
            g = gt_ref[rows, :]
            dq_c, dks_c, dv_c, ddec = [None] * NCH, [None] * NCH, [None] * NCH, [None] * NCH
            for c in range(NCH - 1, -1, -1):
                cr = _chunk_rows(c)
                s = st_ref[c * ST_ROWS + h * B_DV:c * ST_ROWS + (h + 1) * B_DV, :]
                gb = g.astype(MXU)
                dq_c[c] = dq_i[cr] + _dot(doh[cr], s.astype(MXU))
                dks_c[c] = _dot(vh[cr], gb)
                dv_c[c] = dv_i[cr] + _dot_nt(k_sb[cr], gb)
                ddec[c] = jnp.sum(g * s, axis=0, keepdims=True)
                g = g * decays[c] + _dot_tn(doh[cr], q_eb[cr])
            gt_ref[rows, :] = g
            dq_e = jnp.concatenate(dq_c, axis=0)
            dk_s = jnp.concatenate(dks_c, axis=0)
            dg_ref[:, rows] = jnp.concatenate(dv_c, axis=0).astype(dg_ref.dtype)
            dg_ref[:, o_q + h * B_DK:o_q + (h + 1) * B_DK] = (dq_e * eb * (B_DK ** -0.5)).astype(dg_ref.dtype)
            dg_ref[:, o_k + h * B_DK:o_k + (h + 1) * B_DK] = (dk_e * enb + dk_s * esb).astype(dg_ref.dtype)
            dks_ks = dk_s * k_s
            db = dq_e * q_e - dk_e * k_e - dks_ks
            dbl = [jnp.sum(dks_ks[_chunk_rows(c)], axis=0, keepdims=True) + ddec[c] * decays[c] for c in range(NCH)]
            dla_parts.append(_dot_f32(upper, db) + _per_chunk(which, dbl))

        for h in range(B_HEADS):
            across(h, *within(h))
        dla = jnp.concatenate(dla_parts, axis=1)
        dgk = dla * (1.0 / TAU) * _sigmoid(-gk)
        dgkb = dgk.astype(MXU)
        dbl_ref[...] = _dot_nt(dgkb, gu_ref[...]).astype(dbl_ref.dtype)
        ggu_ref[...] = ggu_ref[...] + _dot_tn(bl_ref[...].astype(MXU), dgkb)
        gbias_ref[...] = gbias_ref[...] + jnp.broadcast_to(jnp.sum(dgk, axis=0, keepdims=True), gbias_ref.shape)

    def rev(i):
        return ns - 1 - i

    return pl.pallas_call(
        body, name="gla_bwd", grid=(ns,),
        in_specs=_gla_specs(rev) + [
            pl.BlockSpec((NCH * ST_ROWS, B_DK), lambda i: (rev(i), 0)),
            pl.BlockSpec((GSTEP, D), lambda i: (rev(i), 0)),
        ],
        out_specs=[
            pl.BlockSpec((GSTEP, W_GLA), lambda i: (rev(i), 0)),
            pl.BlockSpec((GSTEP, W_BL), lambda i: (rev(i), 0)),
            pl.BlockSpec((W_BL, 512), lambda i: (0, 0)),
            pl.BlockSpec((8, 512), lambda i: (0, 0)),
        ],
        out_shape=[
            jax.ShapeDtypeStruct((T, W_GLA), MXU),
            jax.ShapeDtypeStruct((T, W_BL), MXU),
            jax.ShapeDtypeStruct((W_BL, 512), F32),
            jax.ShapeDtypeStruct((8, 512), F32),
        ],
        scratch_shapes=[pltpu.VMEM((B_HEADS * B_DV, B_DK), F32)],
        compiler_params=_cp(("arbitrary",)),
    )(proj, proj, proj, proj, gu_pad, bias, states, do_b)


def _mid(x, target, proj, o_a, o_b, w_a, w_b, w_out, w_bn4, fnw):
    T = x.shape[0]
    tT = min(T, 128)
    nbuf = 4
    o_ag, o_bg, o_ma, o_mb = (c - C_GATES for c in (C_AG, C_BG, C_MA, C_MB))

    def body(x_ref, t_ref, oa_ref, ob_ref, gates_ref, wa_ref, wb_ref, wo_ref, wbn_ref, fnw_ref,
             dx2_ref, doa_ref, dob_ref, dgates_ref,
             gwa_ref, gwb_ref, gwo_ref, gfn_ref, gbn_ref, loss_ref, buf_ref):
        i = pl.program_id(0)

        @pl.when(i == 0)
        def _():
            for r in (gwa_ref, gwb_ref, gwo_ref, gfn_ref, gbn_ref, loss_ref):
                r[...] = jnp.zeros_like(r)

        rows = pl.ds(pl.multiple_of((i % nbuf) * tT, tT), tT)

        def keep(k, val):
            buf_ref[k, rows, :] = val

        oa, ag = oa_ref[...], gates_ref[:, o_ag:o_ag + D]
        sg_a = _sigmoid(ag)
        silu_a = ag * sg_a
        oag_b = (oa * silu_a).astype(MXU)
        keep(0, oag_b)
        y_a = _dot(oag_b, wa_ref[...])

        ob, bg = ob_ref[...], gates_ref[:, o_bg:o_bg + D]
        rbs, obhats = [], []
        for h in range(B_HEADS):
            obh = ob[:, h * B_DV:(h + 1) * B_DV]
            rb = lax.rsqrt(jnp.mean(obh * obh, axis=-1, keepdims=True) + EPS)
            rbs.append(rb)
            obhats.append(obh * rb)
        obhat = jnp.concatenate(obhats, axis=1)
        wbn = wbn_ref[...]
        obn = obhat * wbn
        sg_b = _sigmoid(bg)
        silu_b = bg * sg_b
        obg_b = (obn * silu_b).astype(MXU)
        keep(1, obg_b)
        y_b = _dot(obg_b, wb_ref[...])

        sa, sb = _sigmoid(gates_ref[:, o_ma:o_ma + D]), _sigmoid(gates_ref[:, o_mb:o_mb + D])
        mg_b = (sa * y_a + sb * y_b).astype(MXU)
        keep(2, mg_b)
        x2 = x_ref[...] + _dot(mg_b, wo_ref[...])
        r2 = lax.rsqrt(jnp.mean(x2 * x2, axis=-1, keepdims=True) + EPS)
        xh2 = x2 * r2
        fw = fnw_ref[...]
        err = xh2 * fw - t_ref[...]
        tok = jnp.mean(err * err, axis=-1, keepdims=True)
        loss_ref[...] = loss_ref[...] + 0.5 * jnp.sum(tok, axis=0, keepdims=True)

        dy = err * (1.0 / D)
        gfn_ref[...] = gfn_ref[...] + jnp.broadcast_to(jnp.sum(dy * xh2, axis=0, keepdims=True), gfn_ref.shape)
        gy = dy * fw
        dx2 = r2 * (gy - xh2 * jnp.mean(gy * xh2, axis=-1, keepdims=True))
        dx2_ref[...] = dx2
        dx2_b = dx2.astype(MXU)
        keep(5, dx2_b)
        dmg = _dot_nt(dx2_b, wo_ref[...])

        dgates_ref[:, o_ma:o_ma + D] = (dmg * y_a * sa * (1.0 - sa)).astype(dgates_ref.dtype)
        dgates_ref[:, o_mb:o_mb + D] = (dmg * y_b * sb * (1.0 - sb)).astype(dgates_ref.dtype)
        dya_b = (dmg * sa).astype(MXU)
        dyb_b = (dmg * sb).astype(MXU)
        keep(3, dya_b)
        keep(4, dyb_b)
        doag = _dot_nt(dya_b, wa_ref[...])
        dobg = _dot_nt(dyb_b, wb_ref[...])

        @pl.when(i % nbuf == nbuf - 1)
        def _():
            gwa_ref[...] = gwa_ref[...] + _dot_tn(buf_ref[0], buf_ref[3])
            gwb_ref[...] = gwb_ref[...] + _dot_tn(buf_ref[1], buf_ref[4])
            gwo_ref[...] = gwo_ref[...] + _dot_tn(buf_ref[2], buf_ref[5])

        doa_ref[...] = doag * silu_a
        dgates_ref[:, o_ag:o_ag + D] = (doag * oa * (sg_a * (1.0 + ag * (1.0 - sg_a)))).astype(dgates_ref.dtype)
        dobn = dobg * silu_b
        dgates_ref[:, o_bg:o_bg + D] = (dobg * obn * (sg_b * (1.0 + bg * (1.0 - sg_b)))).astype(dgates_ref.dtype)
        gg = dobn * wbn
        gbn = jnp.zeros((1, B_DV), F32)
        for h in range(B_HEADS):
            sl = slice(h * B_DV, (h + 1) * B_DV)
            gbn = gbn + jnp.sum(dobn[:, sl] * obhats[h], axis=0, keepdims=True)
            ggh = gg[:, sl]
            dob_ref[:, sl] = rbs[h] * (ggh - obhats[h] * jnp.mean(ggh * obhats[h], axis=-1, keepdims=True))
        gbn_ref[...] = gbn_ref[...] + jnp.broadcast_to(gbn, gbn_ref.shape)

    assert (T // tT) % nbuf == 0
    tile = pl.BlockSpec((tT, D), lambda i: (i, 0))
    row = pl.BlockSpec((1, D), lambda i: (0, 0))
    acc8 = pl.BlockSpec((8, D), lambda i: (0, 0))
    return pl.pallas_call(
        body, name="mid", grid=(T // tT,),
        in_specs=[tile, tile, tile, tile, pl.BlockSpec((tT, W_GATES), lambda i: (i, C_GATES // W_GATES)),
                  _vmem(), _vmem(), _vmem(), row, row],
        out_specs=[tile, tile, tile, pl.BlockSpec((tT, W_GATES), lambda i: (i, 0)), _vmem(), _vmem(), _vmem(),
                   acc8, pl.BlockSpec((8, B_DV), lambda i: (0, 0)), pl.BlockSpec((8, LANE), lambda i: (0, 0))],
        out_shape=[
            jax.ShapeDtypeStruct((T, D), F32),
            jax.ShapeDtypeStruct((T, D), F32),
            jax.ShapeDtypeStruct((T, D), F32),
            jax.ShapeDtypeStruct((T, W_GATES), MXU),
            jax.ShapeDtypeStruct((D, D), F32),
            jax.ShapeDtypeStruct((D, D), F32),
            jax.ShapeDtypeStruct((D, D), F32),
            jax.ShapeDtypeStruct((8, D), F32),
            jax.ShapeDtypeStruct((8, B_DV), F32),
            jax.ShapeDtypeStruct((8, LANE), F32),
        ],
        scratch_shapes=[pltpu.VMEM((6, nbuf * tT, D), MXU)],
        compiler_params=_cp(("arbitrary",)),
    )(x, target, o_a, o_b, proj, w_a, w_b, w_out, w_bn4, fnw)


DH = D // 2


_GW_TILES = (("q", 0, 512, 0), ("q", 1, 512, 512), ("kv", 0, 256, 1024), ("bl", 0, RANK, 5376),
             ("gla", 0, 512, 3328), ("gla", 1, 512, 3840), ("gla", 2, 512, 2304), ("gla", 3, 512, 2816),
             ("gates", 0, 512, 1280), ("gates", 1, 512, 1792), ("gates", 2, 512, 4352), ("gates", 3, 512, 4864),
             ("gates", 4, 512, 5392), ("gates", 5, 512, 5904), ("gates", 6, 512, 6416), ("gates", 7, 512, 6928))


def _gw_unpermute(piece, t):
    if piece == "q":
        parts = []
        for blk in range(t.shape[0] // LANE):
            g = [t[blk * LANE + 32 * i:blk * LANE + 32 * (i + 1)] for i in range(4)]
            parts += [g[0], g[2], g[1], g[3]]
        return jnp.concatenate(parts, axis=0)
    if piece == "kv":
        k = [t[64 * i:64 * i + 32] + t[64 * i + 32:64 * i + 64] for i in range(4)]
        v = [t[256 + 128 * g:256 + 128 * g + 64] + t[256 + 128 * g + 64:256 + 128 * (g + 1)] for g in range(2)]
        return jnp.concatenate(k + v, axis=0)
    if piece == "bl":
        return t[:RANK]
    return t


def _gw_half(h, pieces, half, after=None):
    T = h.shape[0]
    steps = len(_GW_TILES)

    def body(*refs):
        h_ref = refs[0]
        srcs = dict(zip(("q", "kv", "bl", "gla", "gates"), refs[1:6]))
        o_ref, stage, sems = refs[-3:]
        j = pl.program_id(0)

        def out_copy(k):
            _, _, n, off = _GW_TILES[k]
            return pltpu.make_async_copy(stage.at[k % 2, 0:n], o_ref.at[pl.ds(off, n)], sems.at[k % 2])

        for k, (piece, _, n, _) in enumerate(_GW_TILES):
            @pl.when(j == k)
            def _(k=k, piece=piece, n=n):
                if k >= 2:
                    out_copy(k - 2).wait()
                t = _gw_unpermute(piece, _dot_tn(srcs[piece][...], h_ref[...]))
                stage[k % 2, 0:n, :] = t.astype(stage.dtype)
                out_copy(k).start()

        @pl.when(j == steps - 1)
        def _():
            out_copy(steps - 2).wait()
            out_copy(steps - 1).wait()

    def tile_of(lo, hi):
        return lambda j: (0, jnp.clip(j - lo, 0, hi - lo - 1))

    in_specs = [pl.BlockSpec((T, DH), lambda j: (0, half)),
                pl.BlockSpec((T, 512), tile_of(0, 2)), pl.BlockSpec((T, 512), lambda j: (0, 0)),
                pl.BlockSpec((T, W_BL), lambda j: (0, 0)),
                pl.BlockSpec((T, 512), tile_of(4, 8)), pl.BlockSpec((T, 512), tile_of(8, 16))]
    args = [h, *pieces]
    if after is not None:
        in_specs.append(_any())
        args.append(after)
    return pl.pallas_call(
        body, name=f"gw_in_half{half}", grid=(steps,),
        in_specs=in_specs, out_specs=_any(),
        out_shape=jax.ShapeDtypeStruct((IN_WIDTH, DH), WIRE),
        scratch_shapes=[pltpu.VMEM((2, 512, DH), WIRE), pltpu.SemaphoreType.DMA((2,))],
        compiler_params=_cp(("arbitrary",)),
    )(*args)


def _chip_copies(s_ref, got_ref, send_sems, recv_sems):
    x, y, c = _place()
    chips = [(1 - x, y), (x, 1 - y), (1 - x, 1 - y)]
    return [pltpu.make_async_remote_copy(
        src_ref=s_ref.at[2 * px + py], dst_ref=got_ref.at[j],
        send_sem=send_sems.at[j], recv_sem=recv_sems.at[j], device_id=(px, py, c), device_id_type=MESH)
        for j, (px, py) in enumerate(chips)]


_EFFECT = pltpu.SideEffectType.DATAFLOW_SIDE_EFFECTING


def _hbm():
    return pl.BlockSpec(memory_space=pltpu.HBM)


def _sem():
    return pl.BlockSpec(memory_space=pltpu.SEMAPHORE)


def _chip_start(sums, half):
    land = pltpu.with_memory_space_constraint(lax.empty((3,) + sums.shape[1:], sums.dtype), pltpu.HBM)

    def body(s_ref, land_ref, send_sems, recv_sems, s_thru, land_thru, token):
        for cp in _chip_copies(s_ref, land_ref, send_sems, recv_sems):
            cp.start()
        token[...] = jnp.zeros_like(token)

    return pl.pallas_call(
        body, name=f"chip_start{half}",
        out_shape=(pltpu.SemaphoreType.DMA((3,)), pltpu.SemaphoreType.DMA((3,)),
                   pltpu.HBM(sums.shape, sums.dtype), pltpu.HBM(land.shape, land.dtype),
                   jax.ShapeDtypeStruct((8, LANE), F32)),
        in_specs=(_hbm(), _hbm()), out_specs=(_sem(), _sem(), _hbm(), _hbm(), _vmem()),
        input_output_aliases={0: 2, 1: 3},
        compiler_params=pltpu.CompilerParams(has_side_effects=_EFFECT),
    )(pltpu.with_memory_space_constraint(sums, pltpu.HBM), land)


def _chip_wait(send_sems, recv_sems, s_thru, land_thru, after, half):
    def body(s_ref, land_ref, send_sems, recv_sems, after_ref, s_out, got_ref):
        copies = _chip_copies(s_ref, land_ref, send_sems, recv_sems)
        for cp in copies:
            cp.wait_send()
        for cp in copies:
            cp.wait_recv()

    return pl.pallas_call(
        body, name=f"chip_wait{half}",
        out_shape=(pltpu.HBM(s_thru.shape, s_thru.dtype), pltpu.HBM(land_thru.shape, land_thru.dtype)),
        in_specs=(_hbm(), _hbm(), _sem(), _sem(), _any()), out_specs=(_hbm(), _hbm()),
        input_output_aliases={0: 0, 1: 1},
        compiler_params=pltpu.CompilerParams(has_side_effects=_EFFECT),
    )(s_thru, land_thru, send_sems, recv_sems, after)


def _dh_norm(pieces, offsets, wf, x, dx2, norm_w, after):
    T = x.shape[0]
    tT = min(T, 256)
    widths = [p.shape[1] for p in pieces]
    npc = len(pieces)

    def body(*refs):
        dp_refs = refs[:npc]
        wf_ref, x_ref, dx2_ref, nw_ref, _, gx_ref, gnw_ref = refs[npc:]

        @pl.when(pl.program_id(0) == 0)
        def _():
            gnw_ref[...] = jnp.zeros_like(gnw_ref)

        dh = jnp.zeros((tT, D), F32)
        for dp_ref, off, w in zip(dp_refs, offsets, widths):
            dh = dh + _dot(dp_ref[...], wf_ref[off:off + w, :])
        xv = x_ref[...]
        r = lax.rsqrt(jnp.mean(xv * xv, axis=-1, keepdims=True) + EPS)
        xh = xv * r
        gnw_ref[...] = gnw_ref[...] + jnp.broadcast_to(jnp.sum(dh * xh, axis=0, keepdims=True), gnw_ref.shape)
        g = dh * nw_ref[...]
        gx_ref[...] = r * (g - xh * jnp.mean(g * xh, axis=-1, keepdims=True)) + dx2_ref[...]

    tile = pl.BlockSpec((tT, D), lambda i: (i, 0))
    return pl.pallas_call(
        body, name="dh_norm", grid=(T // tT,),
        in_specs=[pl.BlockSpec((tT, w), lambda i: (i, 0)) for w in widths]
        + [_vmem(), tile, tile, pl.BlockSpec((1, D), lambda i: (0, 0)), _any()],
        out_specs=[tile, pl.BlockSpec((8, D), lambda i: (0, 0))],
        out_shape=[jax.ShapeDtypeStruct((T, D), F32), jax.ShapeDtypeStruct((8, D), F32)],
        compiler_params=_cp(("arbitrary",)),
    )(*pieces, wf, x, dx2, norm_w, after)


def _adamw_math(w, g, m, v):
    m = ADAM_B1 * m + (1.0 - ADAM_B1) * g
    v = ADAM_B2 * v + (1.0 - ADAM_B2) * (g * g)
    m_hat = m * (1.0 / (1.0 - ADAM_B1 ** ADAM_STEP))
    v_hat = v * (1.0 / (1.0 - ADAM_B2 ** ADAM_STEP))
    delta = -ADAM_LR * (m_hat / (jnp.sqrt(v_hat) + ADAM_EPS) + ADAM_WD * w)
    return delta, m, v


def _fetch_partials(s_ref, got_ref, buf, sems):
    x, y, _ = _place()
    cps = [pltpu.make_async_copy(s_ref.at[2 * x + y], buf.at[0], sems.at[0])]
    cps += [pltpu.make_async_copy(got_ref.at[j], buf.at[1 + j], sems.at[1 + j]) for j in range(3)]
    for cp in cps:
        cp.start()
    for cp in cps:
        cp.wait()


SMALL_AT = dict(norm_w=0, fnw=8, bias=16, bn=24, sinks=32, loss=40)
ROW_AT = (R_IN, R_A, R_B, R_O)


def _finish_small(ws, ms, vs, smalls):
    names = ["norm_w", "fnw", "bias", "bn", "sinks"]
    widths = [ws[n].shape[1] for n in names]

    def body(*refs):
        w_refs, m_refs, v_refs = refs[0:5], refs[5:10], refs[10:15]
        smalls_ref, loss_ref = refs[15], refs[16]
        outs, tot = refs[17:37], refs[37]
        acc = smalls_ref[0]
        for d in range(1, NDEV):
            acc = acc + smalls_ref[d]
        tot[...] = acc
        loss_ref[...] = tot[SMALL_AT["loss"]:SMALL_AT["loss"] + 1, 0:1]
        for p, (nm_, wd) in enumerate(zip(names, widths)):
            r = SMALL_AT[nm_]
            g = tot[r:r + 1, 0:wd]
            d, nm, nv = _adamw_math(w_refs[p][...], g, m_refs[p][...], v_refs[p][...])
            for o, val in zip(outs[4 * p:4 * p + 4], (g, d, nm, nv)):
                o[...] = val

    res = pl.pallas_call(
        body, name="finish_small",
        in_specs=[_vmem()] * 16, out_specs=[_vmem()] * 21,
        out_shape=[jax.ShapeDtypeStruct((1, 1), F32)]
        + [jax.ShapeDtypeStruct((1, wd), F32) for wd in widths for _ in range(4)],
        scratch_shapes=[pltpu.VMEM((SMALL_ROWS, D), F32)],
        compiler_params=_cp(),
    )(*[ws[n] for n in names], *[ms[n] for n in names], *[vs[n] for n in names], smalls)
    return res[0], {n: tuple(res[1 + 4 * p:5 + 4 * p]) for p, n in enumerate(names)}


def _finish(w_rows, m_rows, v_rows, gu_w, gu_m, gu_v, sums, got):
    shapes = [w.shape for w in w_rows]

    def body(*refs):
        wr_refs, mr_refs, vr_refs = refs[0:4], refs[4:8], refs[8:12]
        guw_ref, gum_ref, guv_ref = refs[12:15]
        s_refs, got_refs = refs[15:17], refs[17:19]
        row_outs = refs[19:35]
        gu_outs = refs[35:39]
        buf, gsh, sems = refs[39:]
        x, y, c = _place()
        me_slot = 4 * x + 2 * y + c
        unshift = lax.rem(SHARD_PAD - 2 * me_slot, SHARD_PAD)

        def total(rows, cols):
            g = buf[0, rows, cols].astype(F32)
            for j in range(1, 4):
                g = g + buf[j, rows, cols].astype(F32)
            return g

        def update(p, g, cols):
            d, nm, nv = _adamw_math(wr_refs[p][:, cols], g, mr_refs[p][:, cols], vr_refs[p][:, cols])
            for o, val in zip(row_outs[4 * p:4 * p + 4], (g, d, nm, nv)):
                o[:, cols] = val

        for hf in range(2):
            _fetch_partials(s_refs[hf], got_refs[hf], buf, sems)
            for cc in range(DH // LANE):
                src = slice(cc * LANE, (cc + 1) * LANE)
                cols = slice(hf * DH + cc * LANE, hf * DH + (cc + 1) * LANE)
                gsh[...] = pltpu.roll(total(slice(0, SHARD_PAD), src), unshift, 0)
                update(0, gsh[0:SHARD, :], cols)
                for p in range(1, 4):
                    update(p, total(slice(ROW_AT[p], ROW_AT[p] + 128), src), cols)
            if hf == 0:
                g = total(slice(R_GU, R_GU + RANK), slice(0, 64))
                d, nm, nv = _adamw_math(guw_ref[...], g, gum_ref[...], guv_ref[...])
                for o, val in zip(gu_outs, (g, d, nm, nv)):
                    o[...] = val

    res = pl.pallas_call(
        body, name="finish",
        in_specs=[_vmem()] * 15 + [_any()] * 4,
        out_specs=[_vmem()] * 20,
        out_shape=[jax.ShapeDtypeStruct(s, F32) for s in shapes for _ in range(4)]
        + [jax.ShapeDtypeStruct((RANK, 64), F32)] * 4,
        scratch_shapes=[pltpu.VMEM((4, ROWS, DH), sums[0].dtype), pltpu.VMEM((SHARD_PAD, LANE), F32),
                        pltpu.SemaphoreType.DMA((4,))],
        compiler_params=_cp(),
    )(*w_rows, *m_rows, *v_rows, gu_w, gu_m, gu_v, *sums, *got)
    return tuple(res[0:16]), tuple(res[16:20])


def _place():
    x, y, c = lax.axis_index("x"), lax.axis_index("y"), lax.axis_index("c")
    return x, y, c


def _peers(x, y, c):
    return [(x ^ dx, y ^ dy, c ^ dc) for dx in range(2) for dy in range(2) for dc in range(2) if dx + dy + dc]


def _late_gather_start(blk, after, name="late_gather"):
    land = pltpu.with_memory_space_constraint(lax.empty((NDEV,) + blk.shape, blk.dtype), pltpu.HBM)

    def body(b_ref, land_ref, after_ref, send_sems, recv_sems, b_thru, land_thru, token):
        x, y, c = _place()
        for k, to in enumerate(_peers(x, y, c)):
            pltpu.make_async_remote_copy(
                src_ref=b_ref, dst_ref=land_ref.at[4 * x + 2 * y + c], send_sem=send_sems.at[k],
                recv_sem=recv_sems.at[k], device_id=to, device_id_type=MESH).start()
        token[...] = jnp.zeros_like(token)

    return pl.pallas_call(
        body, name=name + "_start",
        out_shape=(pltpu.SemaphoreType.DMA((7,)), pltpu.SemaphoreType.DMA((7,)),
                   pltpu.HBM(blk.shape, blk.dtype), pltpu.HBM(land.shape, land.dtype),
                   jax.ShapeDtypeStruct((8, LANE), F32)),
        in_specs=(_hbm(), _hbm(), _any()), out_specs=(_sem(), _sem(), _hbm(), _hbm(), _vmem()),
        input_output_aliases={0: 2, 1: 3},
        compiler_params=pltpu.CompilerParams(has_side_effects=_EFFECT),
    )(pltpu.with_memory_space_constraint(blk, pltpu.HBM), land, after)


def _late_gather_wait(send_sems, recv_sems, b_thru, land_thru, after, after2, name="late_gather"):
    def body(b_ref, land_ref, send_sems, recv_sems, after_ref, after2_ref, b_out, got_ref):
        x, y, c = _place()
        copies = [pltpu.make_async_remote_copy(
            src_ref=b_ref, dst_ref=land_ref.at[4 * x + 2 * y + c], send_sem=send_sems.at[k],
            recv_sem=recv_sems.at[k], device_id=to, device_id_type=MESH)
            for k, to in enumerate(_peers(x, y, c))]
        for cp in copies:
            cp.wait_send()
        for cp in copies:
            cp.wait_recv()

    return pl.pallas_call(
        body, name=name + "_wait",
        out_shape=(pltpu.HBM(b_thru.shape, b_thru.dtype), pltpu.HBM(land_thru.shape, land_thru.dtype)),
        in_specs=(_hbm(), _hbm(), _sem(), _sem(), _any(), _any()), out_specs=(_hbm(), _hbm()),
        input_output_aliases={0: 0, 1: 1},
        compiler_params=pltpu.CompilerParams(has_side_effects=_EFFECT),
    )(b_thru, land_thru, send_sems, recv_sems, after, after2)


G_ROWS = SHARD_PAD + RANK


def _gather_blocks(w_in_t, gu_s, xs, norm_w, pos_col):
    rows, cols = G_ROWS, D
    T = xs.shape[0]
    tT = min(T, 256)
    inv_row, sign_row = _rope_rows()

    def body(wi_ref, gu_ref, xs_hbm, nw_ref, pos_ref, inv_ref, sign_ref,
             out_ref, h_ref, cos_ref, sin_ref, x_ref, frame_ref, xs_ref, send_sems, recv_sems, local_sem, xs_sem):
        load_xs = pltpu.make_async_copy(xs_hbm, xs_ref, xs_sem)
        load_xs.start()
        x, y, c = _place()
        me, sibling = (x, y, c), (x, y, 1 - c)
        chips = [(1 - x, y), (x, 1 - y), (1 - x, 1 - y)]
        shift = 2 * (4 * x + 2 * y + c)
        frame_ref[SHARD - SHARD % 8:, :] = jnp.zeros((SHARD_PAD - SHARD + SHARD % 8, D), F32)
        frame_ref[:SHARD, :] = wi_ref[...]
        for cc in range(D // LANE):
            cs = slice(cc * LANE, (cc + 1) * LANE)
            x_ref[0:SHARD_PAD, cs] = pltpu.roll(frame_ref[:, cs], shift, 0).astype(x_ref.dtype)
        x_ref[SHARD_PAD:G_ROWS, :] = jnp.zeros((RANK, D), x_ref.dtype)
        x_ref[SHARD_PAD:G_ROWS, 0:64] = gu_ref[...].astype(x_ref.dtype)

        def slot(px, py, pc):
            return out_ref.at[4 * px + 2 * py + pc]

        def copy(k, block, to, src=None):
            return pltpu.make_async_remote_copy(
                src_ref=slot(*block) if src is None else src, dst_ref=slot(*block),
                send_sem=send_sems.at[k], recv_sem=recv_sems.at[k], device_id=to, device_id_type=MESH)

        mine = pltpu.make_async_copy(x_ref, slot(*me), local_sem)
        mine.start()
        first = [copy(0, me, sibling, src=x_ref)]
        first += [copy(1 + j, me, (*chip, c), src=x_ref) for j, chip in enumerate(chips)]
        for cp in first:
            cp.start()
        load_xs.wait()

        @pl.loop(0, T // tT)
        def _(i):
            rows_i = pl.ds(pl.multiple_of(i * tT, tT), tT)
            _prologue_rows(rows_i, xs_ref, nw_ref, pos_ref, inv_ref, sign_ref, h_ref, cos_ref, sin_ref)

        passed = [copy(4 + j, (*chip, c), sibling) for j, chip in enumerate(chips)]
        for j, chip in enumerate(chips):
            copy(1 + j, (*chip, c), me).wait_recv()
            passed[j].start()
        copy(0, sibling, me).wait_recv()
        for j, chip in enumerate(chips):
            copy(4 + j, (*chip, 1 - c), me).wait_recv()
        for cp in first + passed:
            cp.wait_send()
        mine.wait()

    return pl.pallas_call(
        body, name="gather_weights",
        in_specs=[_vmem(), _vmem(), _any()] + [_vmem()] * 4, out_specs=[_any()] + [_vmem()] * 3,
        out_shape=[jax.ShapeDtypeStruct((NDEV, rows, cols), WIRE), jax.ShapeDtypeStruct((T, D), MXU),
                   jax.ShapeDtypeStruct((T, LANE), F32), jax.ShapeDtypeStruct((T, LANE), F32)],
        scratch_shapes=[pltpu.VMEM((rows, cols), WIRE), pltpu.VMEM((SHARD_PAD, D), F32), pltpu.VMEM((T, D), F32),
                        pltpu.SemaphoreType.DMA((7,)), pltpu.SemaphoreType.DMA((7,)), pltpu.SemaphoreType.DMA,
                        pltpu.SemaphoreType.DMA],
        compiler_params=_cp(),
    )(w_in_t, gu_s, xs, norm_w, pos_col, inv_row, sign_row)


def _pair_reduce(gwt, tail, half):
    n = gwt.shape[1]
    rows = SHARD_PAD + tail.shape[1]
    blk = (4, rows, n)

    def body(g_ref, t_ref, out_ref, acc, got, own, send_sems, recv_sems, own_sems, out_sems):
        x, y, c = _place()

        def parts(d, dst):
            frame = g_ref.at[pl.ds(pl.multiple_of(FRAME * d, 16), SHARD_PAD)]
            return [(frame, dst.at[0:SHARD_PAD]), (t_ref.at[d], dst.at[SHARD_PAD:rows])]

        sends, loads, stores = [], [], []
        for chip in range(4):
            sends.append([pltpu.make_async_remote_copy(
                src_ref=s, dst_ref=d_, send_sem=send_sems.at[chip, k], recv_sem=recv_sems.at[chip, k],
                device_id=(x, y, 1 - c), device_id_type=MESH)
                for k, (s, d_) in enumerate(parts(2 * chip + (1 - c), got.at[chip]))])
            loads.append([pltpu.make_async_copy(s, d_, own_sems.at[chip, k])
                          for k, (s, d_) in enumerate(parts(2 * chip + c, own.at[chip]))])
            stores.append(pltpu.make_async_copy(acc.at[chip], out_ref.at[chip], out_sems.at[chip]))
        for group in sends + loads:
            for cp in group:
                cp.start()
        for chip in range(4):
            for cp in loads[chip]:
                cp.wait()
            for cp in sends[chip]:
                cp.wait_recv()
            acc[chip] = (own[chip].astype(F32) + got[chip].astype(F32)).astype(acc.dtype)
            stores[chip].start()
        for cp in stores:
            cp.wait()
        for group in sends:
            for cp in group:
                cp.wait_send()

    return pl.pallas_call(
        body, name=f"pair_reduce{half}",
        in_specs=[_any(), _any()], out_specs=_any(),
        out_shape=jax.ShapeDtypeStruct(blk, gwt.dtype),
        scratch_shapes=[pltpu.VMEM(blk, gwt.dtype), pltpu.VMEM(blk, gwt.dtype), pltpu.VMEM(blk, gwt.dtype),
                        pltpu.SemaphoreType.DMA((4, 2)), pltpu.SemaphoreType.DMA((4, 2)),
                        pltpu.SemaphoreType.DMA((4, 2)), pltpu.SemaphoreType.DMA((4,))],
        compiler_params=_cp(),
    )(gwt, tail)


def _pad_cols(a, cols):
    return jnp.pad(a, ((0, 0), (0, cols - a.shape[1])))


def _pad_rows(a, rows):
    return jnp.pad(a, ((0, rows - a.shape[0]), (0, 0)))


FRAME = 928


def _wft_plan():
    moves = []
    for blk in range(8):
        for half in range(2):
            for sub in range(2):
                moves.append((C_Q + 128 * blk + 32 * (2 * half + sub), 128 * blk + 32 * (2 * sub + half), 32))
    for idx in range(4):
        for dup in range(2):
            moves.append((C_KD + 64 * idx + 32 * dup, 1024 + 32 * idx, 32))
    for g in range(2):
        for dup in range(2):
            moves.append((C_VD + 128 * g + 64 * dup, 1152 + 64 * g, 64))
    moves += [(C_BL, 5376, RANK), (C_BV, 3328, 1024), (C_BQ, 2304, 512), (C_BK, 2816, 512),
              (C_AG, 1280, 1024), (C_BG, 4352, 1024), (C_MA, 5392, 1024), (C_MB, 6416, 1024)]
    bulk, seams = [], []
    for dst, src, n in moves:
        r = src
        while r < src + n:
            f = min(r // FRAME, NDEV - 1)
            local = r - FRAME * f
            if f > 0 and local < 16:
                assert local == 0
                seams.append((f, dst + r - src))
                step = 16
            else:
                step = min(src + n, FRAME * (f + 1) if f < NDEV - 1 else IN_WIDTH) - r
                bulk.append((f, local, dst + r - src, step))
            r += step
    assert sorted(f for f, _ in seams) == list(range(1, NDEV))
    return bulk, seams, [(C_BL + RANK, C_GLA - C_BL - RANK)]


def _build_wft_copies(frames):
    bulk, seams, zeros = _wft_plan()
    (z0, zn), = zeros

    def body(f_ref, o_ref, edge, sems, esems):
        copies = [pltpu.make_async_copy(f_ref.at[f, pl.ds(l0, n)], o_ref.at[pl.ds(dst, n)], sems.at[i])
                  for i, (f, l0, dst, n) in enumerate(bulk)]
        loads = []
        for i, (f, _) in enumerate(seams):
            loads.append(pltpu.make_async_copy(f_ref.at[f, pl.ds(0, 16)], edge.at[i, 0], esems.at[i, 0]))
            loads.append(pltpu.make_async_copy(f_ref.at[f - 1, pl.ds(FRAME, 16)], edge.at[i, 1], esems.at[i, 1]))
        for cp in copies + loads:
            cp.start()
        o_ref[z0:z0 + zn, :] = jnp.zeros((zn, D), o_ref.dtype)
        for cp in loads:
            cp.wait()
        for i, (_, dst) in enumerate(seams):
            o_ref[dst:dst + 16, :] = edge[i, 0] + edge[i, 1]
        for cp in copies:
            cp.wait()

    return pl.pallas_call(
        body, name="build_wft",
        in_specs=[_any()], out_specs=_vmem(),
        out_shape=jax.ShapeDtypeStruct((NF, D), frames.dtype),
        scratch_shapes=[pltpu.VMEM((len(seams), 2, 16, D), frames.dtype),
                        pltpu.SemaphoreType.DMA((len(bulk),)), pltpu.SemaphoreType.DMA((len(seams), 2))],
        compiler_params=_cp(),
    )(frames)


def kernel(x, positions, norm_w, w_in, a_sinks, b_gate_up, b_gate_bias, b_out_norm_w, w_a_proj, w_b_proj, w_out, final_norm_w, loss_target, m_norm_w, m_w_in, m_a_sinks, m_b_gate_up, m_b_gate_bias, m_b_out_norm_w, m_w_a_proj, m_w_b_proj, m_w_out, m_final_norm_w, v_norm_w, v_w_in, v_a_sinks, v_b_gate_up, v_b_gate_bias, v_b_out_norm_w, v_w_a_proj, v_w_b_proj, v_w_out, v_final_norm_w):
    T = x.shape[1]
    xs, target = x[0], loss_target[0]
    fnw = final_norm_w.reshape(1, D)
    me = 4 * lax.axis_index("x") + 2 * lax.axis_index("y") + lax.axis_index("c")
    allw, h, cos, sin = _gather_blocks(w_in[0].T, b_gate_up[0], xs, norm_w, positions.reshape(T, 1))
    late_blk = jnp.concatenate([w_a_proj[0], w_b_proj[0], w_out[0]], axis=0).astype(WIRE)
    l_send, l_recv, l_blk, l_land, l_started = _late_gather_start(late_blk, cos)
    wf = _build_wft_copies(allw)
    gu = allw[:, SHARD_PAD:G_ROWS, :64].transpose(1, 0, 2).reshape(RANK, 512)
    gu_pad = _pad_rows(gu, W_BL)

    proj = _proj(h, wf, l_started)
    o_a, lse = _swa_fwd(proj, cos, sin, a_sinks)
    o_b, states = _gla_fwd(proj, gu_pad, b_gate_bias)
    l_blk, l_land = _late_gather_wait(l_send, l_recv, l_blk, l_land, states, lse)
    late = lax.dynamic_update_slice(l_land, l_blk[None], (me, 0, 0))
    w_a, w_b, w_o = (late[:, 128 * i:128 * (i + 1), :].reshape(D, D) for i in range(3))
    (dx2, do_a, do_b, d_gates, g_wa, g_wb, g_wo, g_fn, g_bn, loss_part) = _mid(
        xs, target, proj, o_a, o_b, w_a, w_b, w_o, jnp.tile(b_out_norm_w, (1, B_HEADS)), fnw)
    d_q, d_kv, g_sinks = _swa_bwd(proj, cos, sin, a_sinks, do_a, o_a, lse, cos)
    d_gla, d_bl, g_gu, g_bias = _gla_bwd(proj, gu_pad, b_gate_bias, states, do_b)
    pieces = [d_q, d_kv, d_bl, d_gla, d_gates]
    offsets = [C_Q, C_KD, C_BL, C_GLA, C_GATES]

    ggu = g_gu[:RANK].reshape(RANK, NDEV, 64).transpose(1, 0, 2)
    ggu_half = [jnp.pad(ggu, ((0, 0), (0, 0), (0, DH - 64))), jnp.zeros((NDEV, RANK, DH), F32)]

    def tail(hf):
        cols = slice(hf * DH, (hf + 1) * DH)
        return jnp.concatenate([g[:, cols].reshape(NDEV, 128, DH) for g in (g_wa, g_wb, g_wo)]
                               + [ggu_half[hf]], axis=1).astype(WIRE)

    send0, recv0, s_thru0, land0, started0 = _chip_start(_pair_reduce(_gw_half(h, pieces, 0), tail(0), 0), 0)
    send1, recv1, s_thru1, land1, started1 = _chip_start(
        _pair_reduce(_gw_half(h, pieces, 1, after=started0), tail(1), 1), 1)
    grad_x, g_nw = _dh_norm(pieces, offsets, wf, xs, dx2, norm_w, started1)
    small = jnp.concatenate([g_nw, g_fn, _pad_cols(g_bias, D), _pad_cols(g_bn, D), _pad_cols(g_sinks, D),
                             _pad_cols(loss_part, D)], axis=0)
    sm_send, sm_recv, sm_blk, sm_land, sm_started = _late_gather_start(small, g_nw, name="small_gather")
    sums0, got0 = _chip_wait(send0, recv0, s_thru0, land0, sm_started, 0)
    sums1, got1 = _chip_wait(send1, recv1, s_thru1, land1, got0, 1)
    sums, from_chips = [sums0, sums1], [got0, got1]

    ws = dict(norm_w=norm_w, fnw=fnw, bias=b_gate_bias, bn=b_out_norm_w, sinks=a_sinks)
    ms = dict(norm_w=m_norm_w, fnw=m_final_norm_w.reshape(1, D), bias=m_b_gate_bias, bn=m_b_out_norm_w,
              sinks=m_a_sinks)
    vs = dict(norm_w=v_norm_w, fnw=v_final_norm_w.reshape(1, D), bias=v_b_gate_bias, bn=v_b_out_norm_w,
              sinks=v_a_sinks)
    t_rows, t_gu = _finish(
        [w_in[0].T, w_a_proj[0], w_b_proj[0], w_out[0]], [m_w_in[0].T, m_w_a_proj[0], m_w_b_proj[0], m_w_out[0]],
        [v_w_in[0].T, v_w_a_proj[0], v_w_b_proj[0], v_w_out[0]],
        b_gate_up[0], m_b_gate_up[0], v_b_gate_up[0], sums, from_chips)
    sm_blk, sm_land = _late_gather_wait(sm_send, sm_recv, sm_blk, sm_land, t_rows[0], t_gu[0], name="small_gather")
    loss, sm = _finish_small(ws, ms, vs, lax.dynamic_update_slice(sm_land, sm_blk[None], (me, 0, 0)))

    def outputs(k):
        return [sm["norm_w"][k], t_rows[k].T[None], sm["sinks"][k], t_gu[k][None], sm["bias"][k], sm["bn"][k],
                t_rows[4 + k][None], t_rows[8 + k][None], t_rows[12 + k][None], sm["fnw"][k].reshape(D)]

    return (loss[0, 0], grad_x[None], *outputs(0), *outputs(1), *outputs(2), *outputs(3))
```

```python
import functools

import numpy as np
import jax
import jax.numpy as jnp
from jax import lax
from jax.experimental import pallas as pl
from jax.experimental.pallas import tpu as pltpu

F32 = jnp.float32
MXU = jnp.bfloat16
WIRE = jnp.bfloat16

D = 1024
A_HEADS, A_KV, A_HD = 16, 2, 64
BLK = 128
B_HEADS, B_DK, B_DV = 4, 128, 256
RANK, TAU, CHUNK = 16, 16.0, 64
EPS, NEG = 1e-5, -1e30
ROPE_THETA = 10000.0
IN_WIDTH, NDEV = 7440, 8
SHARD = IN_WIDTH // NDEV
LANE = 128
LANE_TILES = D // LANE


def _by_lane_tile(a):
    return jnp.transpose(a, (2, 0, 1)).reshape(SHARD * LANE_TILES, LANE)


C_Q, C_KD, C_VD, C_BL = 0, 1024, 1280, 1536
C_BV, C_BQ, C_BK = 2048, 3072, 3584
C_AG, C_BG, C_MA, C_MB = 4096, 5120, 6144, 7168
C_GLA, W_GLA, C_GATES, W_GATES = 2048, 2048, 4096, 4096
NF = 8192
W_BL = 128

SHARD_PAD = 944
R_IN, R_A, R_B, R_O, R_GU, ROWS = 0, 944, 1072, 1200, 1328, 1344
SMALL_ROWS = 48

ADAM_LR, ADAM_B1, ADAM_B2, ADAM_EPS, ADAM_WD, ADAM_STEP = 0.001, 0.9, 0.999, 1e-08, 0.01, 10

MESH = pl.DeviceIdType.MESH
VMEM_LIMIT = 56 * 1024 * 1024


def _cp(sem=None, **kw):
    if sem is not None:
        kw["dimension_semantics"] = sem
    return pltpu.CompilerParams(vmem_limit_bytes=VMEM_LIMIT, **kw)


def _dot(a, b):
    return jnp.dot(a, b, preferred_element_type=F32)


def _dot_nt(a, b):
    return lax.dot_general(a, b, (((1,), (1,)), ((), ())), preferred_element_type=F32)


def _dot_tn(a, b):
    return lax.dot_general(a, b, (((0,), (0,)), ((), ())), preferred_element_type=F32)


def _dot_f32(a, b):
    return jnp.dot(a, b, preferred_element_type=F32, precision=lax.Precision.HIGHEST)


def _sigmoid(z):
    return 0.5 * jnp.tanh(0.5 * z) + 0.5


def _rope(xp, cos, sin):
    return xp * cos + pltpu.roll(xp, 64, 1) * sin


def _rope_bwd(dy, cos, sin):
    return dy * cos - pltpu.roll(dy, 64, 1) * sin


def _vmem():
    return pl.BlockSpec(memory_space=pltpu.VMEM)


def _any():
    return pl.BlockSpec(memory_space=pl.ANY)


def _rope_rows():
    half = A_HD // 2
    inv = (np.float32(ROPE_THETA) ** (-np.arange(half, dtype=np.float32) / np.float32(half))).astype(np.float32)
    inv_row = jnp.asarray(np.tile(inv, 4)[None, :])
    sign_row = jnp.asarray(np.concatenate([-np.ones(64, np.float32), np.ones(64, np.float32)])[None, :])
    return inv_row, sign_row


def _prologue_rows(rows, x_ref, nw_ref, pos_ref, inv_ref, sign_ref, h_ref, cos_ref, sin_ref):
    xv = x_ref[rows, :]
    r = lax.rsqrt(jnp.mean(xv * xv, axis=-1, keepdims=True) + EPS)
    h_ref[rows, :] = ((xv * r) * nw_ref[...]).astype(h_ref.dtype)
    ang = pos_ref[rows, :].astype(F32) * inv_ref[...]
    cos_ref[rows, :] = jnp.cos(ang)
    sin_ref[rows, :] = jnp.sin(ang) * sign_ref[...]


def _proj(h, wft, after):
    T = h.shape[0]
    tT, tN = T, 512

    def body(h_ref, w_ref, after_ref, o_ref):
        o_ref[...] = _dot_nt(h_ref[...], w_ref[...])

    return pl.pallas_call(
        body, name="proj", grid=(T // tT, NF // tN),
        in_specs=[pl.BlockSpec((tT, D), lambda i, j: (i, 0)), pl.BlockSpec((tN, D), lambda i, j: (j, 0)), _any()],
        out_specs=pl.BlockSpec((tT, tN), lambda i, j: (i, j)),
        out_shape=jax.ShapeDtypeStruct((T, NF), F32),
        compiler_params=_cp(("parallel", "parallel")),
    )(h, wft, after)


def _swa_masks():
    lane = lax.broadcasted_iota(jnp.int32, (BLK, LANE), 1)
    rope_sub0 = ((lane // 32) % 2) == 0
    std_sub0 = lane < 64
    return lane, rope_sub0, std_sub0


def _swa_tri():
    qi = lax.broadcasted_iota(jnp.int32, (BLK, BLK), 0)
    kj = lax.broadcasted_iota(jnp.int32, (BLK, BLK), 1)
    return kj <= qi


def _swa_fold(full, tri):
    return jnp.where(tri, full[:, BLK:], full[:, :BLK])


def _swa_unfold(sq, tri):
    return jnp.concatenate([jnp.where(tri, 0.0, sq), jnp.where(tri, sq, 0.0)], axis=1)


def _swa_keys(kc_ref, kp_ref, vc_ref, vp_ref, cq, sq, cp, sp):
    def ropek(kref, c, s):
        kv = kref[...]
        return jnp.concatenate([_rope(kv[:, :LANE], c, s), _rope(kv[:, LANE:], c, s)], axis=1)

    K = jnp.concatenate([ropek(kp_ref, cp, sp), ropek(kc_ref, cq, sq)], axis=0).astype(MXU)
    V = jnp.concatenate([vp_ref[...], vc_ref[...]], axis=0).astype(MXU)
    return K, V


def _swa_in_specs(nb, last):
    def cur(n):
        return jnp.minimum(n, last)

    def prev(n):
        return jnp.maximum(cur(n) - 1, 0)

    kd, vd = C_KD // 256, C_VD // 256
    return [
        pl.BlockSpec((BLK, D), lambda n: (cur(n), C_Q // D)),
        pl.BlockSpec((BLK, 256), lambda n: (cur(n), kd)),
        pl.BlockSpec((BLK, 256), lambda n: (prev(n), kd)),
        pl.BlockSpec((BLK, 256), lambda n: (cur(n), vd)),
        pl.BlockSpec((BLK, 256), lambda n: (prev(n), vd)),
        pl.BlockSpec((BLK, LANE), lambda n: (cur(n), 0)),
        pl.BlockSpec((BLK, LANE), lambda n: (cur(n), 0)),
        pl.BlockSpec((BLK, LANE), lambda n: (prev(n), 0)),
        pl.BlockSpec((BLK, LANE), lambda n: (prev(n), 0)),
    ]


def _swa_fwd(proj, cos, sin, sinks):
    T = proj.shape[0]
    nb = T // BLK
    scale = A_HD ** -0.5

    def body(sinks_ref, q_ref, kc_ref, kp_ref, vc_ref, vp_ref, cq_ref, sq_ref, cp_ref, sp_ref, o_ref, l_ref):
        n = pl.program_id(0)
        cq, sq = cq_ref[...], sq_ref[...]
        K, V = _swa_keys(kc_ref, kp_ref, vc_ref, vp_ref, cq, sq, cp_ref[...], sp_ref[...])
        tri = _swa_tri()
        valid = tri | (n > 0)
        lane, rope_sub0, std_sub0 = _swa_masks()
        group = A_HEADS // A_KV
        roped, lses = {}, []

        def products(head):
            pb, sub, g = head // 2, head % 2, head // group
            if sub == 0:
                roped[pb] = _rope(q_ref[:, pb * LANE:(pb + 1) * LANE], cq, sq)
            qm = jnp.where(rope_sub0 if sub == 0 else ~rope_sub0, roped[pb], 0.0).astype(MXU)
            return _dot_nt(qm, K[:, g * LANE:(g + 1) * LANE])

        def softmax(head, s_full):
            s = jnp.where(valid, _swa_fold(s_full, tri) * scale, NEG)
            sink = sinks_ref[0, head]
            m = jnp.maximum(jnp.max(s, axis=1, keepdims=True), sink)
            e = jnp.exp(s - m)
            den = jnp.sum(e, axis=1, keepdims=True) + jnp.exp(sink - m)
            lses.append(m + jnp.log(den))
            return _swa_unfold(e / den, tri).astype(MXU)

        outs = {}
        st1 = {0: products(0), 1: products(1)}
        st2 = {0: softmax(0, st1.pop(0))}
        for head in range(A_HEADS):
            if head + 2 < A_HEADS:
                st1[head + 2] = products(head + 2)
            if head + 1 < A_HEADS:
                st2[head + 1] = softmax(head + 1, st1.pop(head + 1))
            g = head // group
            outs[head] = _dot(st2.pop(head), V[:, g * LANE:(g + 1) * LANE])
            if head % 2 == 1:
                pb = head // 2
                o_ref[:, pb * LANE:(pb + 1) * LANE] = jnp.where(std_sub0, outs[head - 1], outs[head])
        lacc = jnp.zeros((BLK, LANE), F32)
        for head in range(A_HEADS):
            lacc = jnp.where(lane == head, lses[head], lacc)
        l_ref[...] = lacc

    return pl.pallas_call(
        body, name="swa_fwd", grid=(nb,),
        in_specs=[pl.BlockSpec(memory_space=pltpu.SMEM)] + _swa_in_specs(nb, nb - 1),
        out_specs=[pl.BlockSpec((BLK, D), lambda n: (n, 0)), pl.BlockSpec((BLK, LANE), lambda n: (n, 0))],
        out_shape=[jax.ShapeDtypeStruct((T, D), F32), jax.ShapeDtypeStruct((T, LANE), F32)],
        compiler_params=_cp(("parallel",)),
    )(sinks, proj, proj, proj, proj, proj, cos, sin, cos, sin)


def _swa_bwd(proj, cos, sin, sinks, do_a, o_a, lse, after):
    T = proj.shape[0]
    nb = T // BLK
    scale = A_HD ** -0.5

    def body(sinks_ref, q_ref, kc_ref, kp_ref, vc_ref, vp_ref, cq_ref, sq_ref, cp_ref, sp_ref,
             do_ref, o_ref, l_ref, after_ref, dq_ref, dkv_ref, ds_ref, ckv_ref):
        n = pl.program_id(0)

        @pl.when(n == 0)
        def _():
            ckv_ref[...] = jnp.zeros_like(ckv_ref)
            ds_ref[...] = jnp.zeros_like(ds_ref)

        @pl.when(n < nb)
        def _():
            cq, sq, cp, sp = cq_ref[...], sq_ref[...], cp_ref[...], sp_ref[...]
            K, V = _swa_keys(kc_ref, kp_ref, vc_ref, vp_ref, cq, sq, cp, sp)
            tri = _swa_tri()
            valid = tri | (n > 0)
            lane, rope_sub0, std_sub0 = _swa_masks()
            lane_row = lax.broadcasted_iota(jnp.int32, (1, LANE), 1)
            lse_v = l_ref[...]
            dKt = [jnp.zeros((LANE, 2 * BLK), F32) for _ in range(A_KV)]
            dVt = [jnp.zeros((LANE, 2 * BLK), F32) for _ in range(A_KV)]
            dsinks, roped, roped_t, do_t = [], {}, {}, {}
            group = A_HEADS // A_KV
            dim = lax.broadcasted_iota(jnp.int32, (LANE, BLK), 0)
            rope_row0, std_row0 = ((dim // 32) % 2) == 0, dim < 64

            def products(head):
                pb, sub, g = head // 2, head % 2, head // group
                cols = slice(pb * LANE, (pb + 1) * LANE)
                Kg, Vg = K[:, g * LANE:(g + 1) * LANE], V[:, g * LANE:(g + 1) * LANE]
                if sub == 0:
                    roped[pb] = _rope(q_ref[:, cols], cq, sq)
                    roped_t[pb] = roped[pb].T
                    do_t[pb] = do_ref[:, cols].T
                qm = jnp.where(rope_sub0 if sub == 0 else ~rope_sub0, roped[pb], 0.0).astype(MXU)
                qmt = jnp.where(rope_row0 if sub == 0 else ~rope_row0, roped_t[pb], 0.0).astype(MXU)
                dov = jnp.where(std_sub0 if sub == 0 else ~std_sub0, do_ref[:, cols], 0.0)
                dovt = jnp.where(std_row0 if sub == 0 else ~std_row0, do_t[pb], 0.0).astype(MXU)
                delta = jnp.sum(dov * o_ref[:, cols], axis=1, keepdims=True)
                return qmt, dovt, delta, _dot_nt(qm, Kg), _dot_nt(dov.astype(MXU), Vg)

            def scores(head, qmt, dovt, delta, s_full, dp_full):
                lh = jnp.sum(jnp.where(lane == head, lse_v, 0.0), axis=1, keepdims=True)
                p = jnp.where(valid, jnp.exp(_swa_fold(s_full, tri) * scale - lh), 0.0)
                psink = jnp.exp(sinks_ref[0, head] - lh)
                dsinks.append(jnp.sum(-psink * delta, axis=0, keepdims=True))
                dsq = (p * (_swa_fold(dp_full, tri) - delta)) * scale
                return qmt, dovt, _swa_unfold(p, tri).astype(MXU), _swa_unfold(dsq, tri).astype(MXU)

            def grads(head, qmt, dovt, pb16, dsc):
                g = head // group
                dKt[g] = dKt[g] + _dot(qmt, dsc)
                dVt[g] = dVt[g] + _dot(dovt, pb16)
                return _dot(dsc, K[:, g * LANE:(g + 1) * LANE])

            dqs = {}
            st1 = {0: products(0), 1: products(1)}
            st2 = {0: scores(0, *st1.pop(0))}
            for head in range(A_HEADS):
                if head + 2 < A_HEADS:
                    st1[head + 2] = products(head + 2)
                if head + 1 < A_HEADS:
                    st2[head + 1] = scores(head + 1, *st1.pop(head + 1))
                dqs[head] = grads(head, *st2.pop(head))
                if head % 2 == 1:
                    pb = head // 2
                    dqp = jnp.where(rope_sub0, dqs[head - 1], dqs[head])
                    dq_ref[:, pb * LANE:(pb + 1) * LANE] = _rope_bwd(dqp, cq, sq).astype(dq_ref.dtype)
            dsink = jnp.zeros((1, LANE), F32)
            for head in range(A_HEADS):
                dsink = jnp.where(lane_row == head, dsinks[head], dsink)
            dK, dV = [a.T for a in dKt], [a.T for a in dVt]
            prev = ([_rope_bwd(dK[g][:BLK], cp, sp) for g in range(A_KV)] + [dV[g][:BLK] for g in range(A_KV)])
            cur_ = ([_rope_bwd(dK[g][BLK:], cq, sq) for g in range(A_KV)] + [dV[g][BLK:] for g in range(A_KV)])
            dkv_ref[...] = (ckv_ref[...] + jnp.concatenate(prev, axis=1)).astype(dkv_ref.dtype)
            ckv_ref[...] = jnp.concatenate(cur_, axis=1)
            ds_ref[...] = ds_ref[...] + jnp.broadcast_to(dsink, ds_ref.shape)

        @pl.when(n == nb)
        def _():
            dkv_ref[...] = ckv_ref[...].astype(dkv_ref.dtype)

    last = nb - 1

    def cur(n):
        return jnp.minimum(n, last)

    def out_kv(n):
        return (jnp.maximum(n - 1, 0), 0)

    return pl.pallas_call(
        body, name="swa_bwd", grid=(nb + 1,),
        in_specs=[pl.BlockSpec(memory_space=pltpu.SMEM)] + _swa_in_specs(nb, last) + [
            pl.BlockSpec((BLK, D), lambda n: (cur(n), 0)),
            pl.BlockSpec((BLK, D), lambda n: (cur(n), 0)),
            pl.BlockSpec((BLK, LANE), lambda n: (cur(n), 0)),
            _any(),
        ],
        out_specs=[
            pl.BlockSpec((BLK, D), lambda n: (cur(n), 0)),
            pl.BlockSpec((BLK, 512), out_kv),
            pl.BlockSpec((8, LANE), lambda n: (0, 0)),
        ],
        out_shape=[
            jax.ShapeDtypeStruct((T, D), MXU),
            jax.ShapeDtypeStruct((T, 512), MXU),
            jax.ShapeDtypeStruct((8, LANE), F32),
        ],
        scratch_shapes=[pltpu.VMEM((BLK, 512), F32)],
        compiler_params=_cp(("arbitrary",)),
    )(sinks, proj, proj, proj, proj, proj, cos, sin, cos, sin, do_a, o_a, lse, after)


NCH = 4
GSTEP = NCH * CHUNK
ST_ROWS = B_HEADS * B_DV


def _chunk_rows(c):
    return slice(c * CHUNK, (c + 1) * CHUNK)


def _per_chunk(which, vals):
    out = vals[-1]
    for c in range(NCH - 2, -1, -1):
        out = jnp.where(which == c, vals[c], out)
    return out


def _gla_gate(bl_ref, gu_ref, bias_ref):
    gk = _dot(bl_ref[...].astype(MXU), gu_ref[...]) + bias_ref[...]
    la = (jnp.minimum(gk, 0.0) - jnp.log(1.0 + jnp.exp(-jnp.abs(gk)))) / TAU
    ri = lax.broadcasted_iota(jnp.int32, (GSTEP, GSTEP), 0)
    ci = lax.broadcasted_iota(jnp.int32, (GSTEP, GSTEP), 1)
    same = (ri // CHUNK) == (ci // CHUNK)
    lower, upper = same & (ci <= ri), same & (ci >= ri)
    b = _dot_f32(jnp.where(lower, 1.0, 0.0).astype(F32), la)
    which = lax.broadcasted_iota(jnp.int32, (GSTEP, 1), 0) // CHUNK
    return gk, la, b, lower, upper, which


def _gla_head(q_ref, k_ref, la, b, which, h):
    sl = slice(h * B_DK, (h + 1) * B_DK)
    bh, lah = b[:, sl], la[:, sl]
    bls = [jnp.sum(lah[_chunk_rows(c)], axis=0, keepdims=True) for c in range(NCH)]
    blast = _per_chunk(which, bls)
    qc = q_ref[:, sl] * (B_DK ** -0.5)
    kh = k_ref[:, sl]
    eb, enb, esb = jnp.exp(bh), jnp.exp(-bh), jnp.exp(blast - bh)
    return qc * eb, kh * enb, kh * esb, eb, enb, esb, [jnp.exp(v) for v in bls]


def _gla_specs(step_of):
    return [
        pl.BlockSpec((GSTEP, 512), lambda i: (step_of(i), C_BQ // 512)),
        pl.BlockSpec((GSTEP, 512), lambda i: (step_of(i), C_BK // 512)),
        pl.BlockSpec((GSTEP, D), lambda i: (step_of(i), C_BV // D)),
        pl.BlockSpec((GSTEP, W_BL), lambda i: (step_of(i), C_BL // W_BL)),
        pl.BlockSpec((W_BL, 512), lambda i: (0, 0)),
        pl.BlockSpec((1, 512), lambda i: (0, 0)),
    ]


def _gla_fwd(proj, gu_pad, bias):
    T = proj.shape[0]
    ns = T // GSTEP

    def body(q_ref, k_ref, v_ref, bl_ref, gu_ref, bias_ref, o_ref, st_ref, state_ref):
        @pl.when(pl.program_id(0) == 0)
        def _():
            state_ref[...] = jnp.zeros_like(state_ref)

        _, la, b, lower, _, which = _gla_gate(bl_ref, gu_ref, bias_ref)

        def within(h):
            q_e, k_e, k_s, _, _, _, decays = _gla_head(q_ref, k_ref, la, b, which, h)
            vh = v_ref[:, h * B_DV:(h + 1) * B_DV].astype(MXU)
            q_eb = q_e.astype(MXU)
            att = jnp.where(lower, _dot_nt(q_eb, k_e.astype(MXU)), 0.0)
            return vh, q_eb, k_s.astype(MXU), _dot(att.astype(MXU), vh), decays

        def across(h, vh, q_eb, k_sb, o_intra, decays):
            rows = slice(h * B_DV, (h + 1) * B_DV)
            s = state_ref[rows, :]
            outs = []
            for c in range(NCH):
                cr = _chunk_rows(c)
                st_ref[c * ST_ROWS + h * B_DV:c * ST_ROWS + (h + 1) * B_DV, :] = s
                outs.append(o_intra[cr] + _dot_nt(q_eb[cr], s.astype(MXU)))
                s = s * decays[c] + _dot_tn(vh[cr], k_sb[cr])
            state_ref[rows, :] = s
            o_ref[:, rows] = jnp.concatenate(outs, axis=0)

        for h in range(B_HEADS):
            across(h, *within(h))

    return pl.pallas_call(
        body, name="gla_fwd", grid=(ns,),
        in_specs=_gla_specs(lambda i: i),
        out_specs=[pl.BlockSpec((GSTEP, D), lambda i: (i, 0)),
                   pl.BlockSpec((NCH * ST_ROWS, B_DK), lambda i: (i, 0))],
        out_shape=[jax.ShapeDtypeStruct((T, D), F32),
                   jax.ShapeDtypeStruct((ns * NCH * ST_ROWS, B_DK), F32)],
        scratch_shapes=[pltpu.VMEM((ST_ROWS, B_DK), F32)],
        compiler_params=_cp(("arbitrary",)),
    )(proj, proj, proj, proj, gu_pad, bias)


def _gla_bwd(proj, gu_pad, bias, states, do_b):
    T = proj.shape[0]
    ns = T // GSTEP
    o_q, o_k = C_BQ - C_GLA, C_BK - C_GLA

    def body(q_ref, k_ref, v_ref, bl_ref, gu_ref, bias_ref, st_ref, do_ref,
             dg_ref, dbl_ref, ggu_ref, gbias_ref, gt_ref):
        @pl.when(pl.program_id(0) == 0)
        def _():
            gt_ref[...] = jnp.zeros_like(gt_ref)
            ggu_ref[...] = jnp.zeros_like(ggu_ref)
            gbias_ref[...] = jnp.zeros_like(gbias_ref)

        gk, la, b, lower, upper_mask, which = _gla_gate(bl_ref, gu_ref, bias_ref)
        upper = jnp.where(upper_mask, 1.0, 0.0).astype(F32)
        dla_parts = []

        def within(h):
            q_e, k_e, k_s, eb, enb, esb, decays = _gla_head(q_ref, k_ref, la, b, which, h)
            vh = v_ref[:, h * B_DV:(h + 1) * B_DV].astype(MXU)
            doh = do_ref[:, h * B_DV:(h + 1) * B_DV].astype(MXU)
            q_eb, k_eb = q_e.astype(MXU), k_e.astype(MXU)
            att = jnp.where(lower, _dot_nt(q_eb, k_eb), 0.0).astype(MXU)
            datt = jnp.where(lower, _dot_nt(doh, vh), 0.0).astype(MXU)
            return (q_e, k_e, k_s, eb, enb, esb, decays, vh, doh, q_eb, k_s.astype(MXU),
                    _dot(datt, k_eb), _dot_tn(datt, q_eb), _dot_tn(att, doh))

        def across(h, q_e, k_e, k_s, eb, enb, esb, decays, vh, doh, q_eb, k_sb, dq_i, dk_e, dv_i):
            rows = slice(h * B_DV, (h + 1) * B_DV)
            g = gt_ref[rows, :]
            dq_c, dks_c, dv_c, ddec = [None] * NCH, [None] * NCH, [None] * NCH, [None] * NCH
            for c in range(NCH - 1, -1, -1):
                cr = _chunk_rows(c)
                s = st_ref[c * ST_ROWS + h * B_DV:c * ST_ROWS + (h + 1) * B_DV, :]
                gb = g.astype(MXU)
                dq_c[c] = dq_i[cr] + _dot(doh[cr], s.astype(MXU))
                dks_c[c] = _dot(vh[cr], gb)
                dv_c[c] = dv_i[cr] + _dot_nt(k_sb[cr], gb)
                ddec[c] = jnp.sum(g * s, axis=0, keepdims=True)
                g = g * decays[c] + _dot_tn(doh[cr], q_eb[cr])
            gt_ref[rows, :] = g
            dq_e = jnp.concatenate(dq_c, axis=0)
            dk_s = jnp.concatenate(dks_c, axis=0)
            dg_ref[:, rows] = jnp.concatenate(dv_c, axis=0).astype(dg_ref.dtype)
            dg_ref[:, o_q + h * B_DK:o_q + (h + 1) * B_DK] = (dq_e * eb * (B_DK ** -0.5)).astype(dg_ref.dtype)
            dg_ref[:, o_k + h * B_DK:o_k + (h + 1) * B_DK] = (dk_e * enb + dk_s * esb).astype(dg_ref.dtype)
            dks_ks = dk_s * k_s
            db = dq_e * q_e - dk_e * k_e - dks_ks
            dbl = [jnp.sum(dks_ks[_chunk_rows(c)], axis=0, keepdims=True) + ddec[c] * decays[c] for c in range(NCH)]
            dla_parts.append(_dot_f32(upper, db) + _per_chunk(which, dbl))

        for h in range(B_HEADS):
            across(h, *within(h))
        dla = jnp.concatenate(dla_parts, axis=1)
        dgk = dla * (1.0 / TAU) * _sigmoid(-gk)
        dgkb = dgk.astype(MXU)
        dbl_ref[...] = _dot_nt(dgkb, gu_ref[...]).astype(dbl_ref.dtype)
        ggu_ref[...] = ggu_ref[...] + _dot_tn(bl_ref[...].astype(MXU), dgkb)
        gbias_ref[...] = gbias_ref[...] + jnp.broadcast_to(jnp.sum(dgk, axis=0, keepdims=True), gbias_ref.shape)

    def rev(i):
        return ns - 1 - i

    return pl.pallas_call(
        body, name="gla_bwd", grid=(ns,),
        in_specs=_gla_specs(rev) + [
            pl.BlockSpec((NCH * ST_ROWS, B_DK), lambda i: (rev(i), 0)),
            pl.BlockSpec((GSTEP, D), lambda i: (rev(i), 0)),
        ],
        out_specs=[
            pl.BlockSpec((GSTEP, W_GLA), lambda i: (rev(i), 0)),
            pl.BlockSpec((GSTEP, W_BL), lambda i: (rev(i), 0)),
            pl.BlockSpec((W_BL, 512), lambda i: (0, 0)),
            pl.BlockSpec((8, 512), lambda i: (0, 0)),
        ],
        out_shape=[
            jax.ShapeDtypeStruct((T, W_GLA), MXU),
            jax.ShapeDtypeStruct((T, W_BL), MXU),
            jax.ShapeDtypeStruct((W_BL, 512), F32),
            jax.ShapeDtypeStruct((8, 512), F32),
        ],
        scratch_shapes=[pltpu.VMEM((B_HEADS * B_DV, B_DK), F32)],
        compiler_params=_cp(("arbitrary",)),
    )(proj, proj, proj, proj, gu_pad, bias, states, do_b)


def _mid(x, target, proj, o_a, o_b, w_a, w_b, w_out, w_bn4, fnw):
    T = x.shape[0]
    tT = min(T, 128)
    nbuf = 4
    o_ag, o_bg, o_ma, o_mb = (c - C_GATES for c in (C_AG, C_BG, C_MA, C_MB))

    def body(x_ref, t_ref, oa_ref, ob_ref, gates_ref, wa_ref, wb_ref, wo_ref, wbn_ref, fnw_ref,
             dx2_ref, doa_ref, dob_ref, dgates_ref,
             gwa_ref, gwb_ref, gwo_ref, gfn_ref, gbn_ref, loss_ref, buf_ref):
        i = pl.program_id(0)

        @pl.when(i == 0)
        def _():
            for r in (gwa_ref, gwb_ref, gwo_ref, gfn_ref, gbn_ref, loss_ref):
                r[...] = jnp.zeros_like(r)

        rows = pl.ds(pl.multiple_of((i % nbuf) * tT, tT), tT)

        def keep(k, val):
            buf_ref[k, rows, :] = val

        oa, ag = oa_ref[...], gates_ref[:, o_ag:o_ag + D]
        sg_a = _sigmoid(ag)
        silu_a = ag * sg_a
        oag_b = (oa * silu_a).astype(MXU)
        keep(0, oag_b)
        y_a = _dot(oag_b, wa_ref[...])

        ob, bg = ob_ref[...], gates_ref[:, o_bg:o_bg + D]
        rbs, obhats = [], []
        for h in range(B_HEADS):
            obh = ob[:, h * B_DV:(h + 1) * B_DV]
            rb = lax.rsqrt(jnp.mean(obh * obh, axis=-1, keepdims=True) + EPS)
            rbs.append(rb)
            obhats.append(obh * rb)
        obhat = jnp.concatenate(obhats, axis=1)
        wbn = wbn_ref[...]
        obn = obhat * wbn
        sg_b = _sigmoid(bg)
        silu_b = bg * sg_b
        obg_b = (obn * silu_b).astype(MXU)
        keep(1, obg_b)
        y_b = _dot(obg_b, wb_ref[...])

        sa, sb = _sigmoid(gates_ref[:, o_ma:o_ma + D]), _sigmoid(gates_ref[:, o_mb:o_mb + D])
        mg_b = (sa * y_a + sb * y_b).astype(MXU)
        keep(2, mg_b)
        x2 = x_ref[...] + _dot(mg_b, wo_ref[...])
        r2 = lax.rsqrt(jnp.mean(x2 * x2, axis=-1, keepdims=True) + EPS)
        xh2 = x2 * r2
        fw = fnw_ref[...]
        err = xh2 * fw - t_ref[...]
        tok = jnp.mean(err * err, axis=-1, keepdims=True)
        loss_ref[...] = loss_ref[...] + 0.5 * jnp.sum(tok, axis=0, keepdims=True)

        dy = err * (1.0 / D)
        gfn_ref[...] = gfn_ref[...] + jnp.broadcast_to(jnp.sum(dy * xh2, axis=0, keepdims=True), gfn_ref.shape)
        gy = dy * fw
        dx2 = r2 * (gy - xh2 * jnp.mean(gy * xh2, axis=-1, keepdims=True))
        dx2_ref[...] = dx2
        dx2_b = dx2.astype(MXU)
        keep(5, dx2_b)
        dmg = _dot_nt(dx2_b, wo_ref[...])

        dgates_ref[:, o_ma:o_ma + D] = (dmg * y_a * sa * (1.0 - sa)).astype(dgates_ref.dtype)
        dgates_ref[:, o_mb:o_mb + D] = (dmg * y_b * sb * (1.0 - sb)).astype(dgates_ref.dtype)
        dya_b = (dmg * sa).astype(MXU)
        dyb_b = (dmg * sb).astype(MXU)
        keep(3, dya_b)
        keep(4, dyb_b)
        doag = _dot_nt(dya_b, wa_ref[...])
        dobg = _dot_nt(dyb_b, wb_ref[...])

        @pl.when(i % nbuf == nbuf - 1)
        def _():
            gwa_ref[...] = gwa_ref[...] + _dot_tn(buf_ref[0], buf_ref[3])
            gwb_ref[...] = gwb_ref[...] + _dot_tn(buf_ref[1], buf_ref[4])
            gwo_ref[...] = gwo_ref[...] + _dot_tn(buf_ref[2], buf_ref[5])

        doa_ref[...] = doag * silu_a
        dgates_ref[:, o_ag:o_ag + D] = (doag * oa * (sg_a * (1.0 + ag * (1.0 - sg_a)))).astype(dgates_ref.dtype)
        dobn = dobg * silu_b
        dgates_ref[:, o_bg:o_bg + D] = (dobg * obn * (sg_b * (1.0 + bg * (1.0 - sg_b)))).astype(dgates_ref.dtype)
        gg = dobn * wbn
        gbn = jnp.zeros((1, B_DV), F32)
        for h in range(B_HEADS):
            sl = slice(h * B_DV, (h + 1) * B_DV)
            gbn = gbn + jnp.sum(dobn[:, sl] * obhats[h], axis=0, keepdims=True)
            ggh = gg[:, sl]
            dob_ref[:, sl] = rbs[h] * (ggh - obhats[h] * jnp.mean(ggh * obhats[h], axis=-1, keepdims=True))
        gbn_ref[...] = gbn_ref[...] + jnp.broadcast_to(gbn, gbn_ref.shape)

    assert (T // tT) % nbuf == 0
    tile = pl.BlockSpec((tT, D), lambda i: (i, 0))
    row = pl.BlockSpec((1, D), lambda i: (0, 0))
    acc8 = pl.BlockSpec((8, D), lambda i: (0, 0))
    return pl.pallas_call(
        body, name="mid", grid=(T // tT,),
        in_specs=[tile, tile, tile, tile, pl.BlockSpec((tT, W_GATES), lambda i: (i, C_GATES // W_GATES)),
                  _vmem(), _vmem(), _vmem(), row, row],
        out_specs=[tile, tile, tile, pl.BlockSpec((tT, W_GATES), lambda i: (i, 0)), _vmem(), _vmem(), _vmem(),
                   acc8, pl.BlockSpec((8, B_DV), lambda i: (0, 0)), pl.BlockSpec((8, LANE), lambda i: (0, 0))],
        out_shape=[
            jax.ShapeDtypeStruct((T, D), F32),
            jax.ShapeDtypeStruct((T, D), F32),
            jax.ShapeDtypeStruct((T, D), F32),
            jax.ShapeDtypeStruct((T, W_GATES), MXU),
            jax.ShapeDtypeStruct((D, D), F32),
            jax.ShapeDtypeStruct((D, D), F32),
            jax.ShapeDtypeStruct((D, D), F32),
            jax.ShapeDtypeStruct((8, D), F32),
            jax.ShapeDtypeStruct((8, B_DV), F32),
            jax.ShapeDtypeStruct((8, LANE), F32),
        ],
        scratch_shapes=[pltpu.VMEM((6, nbuf * tT, D), MXU)],
        compiler_params=_cp(("arbitrary",)),
    )(x, target, o_a, o_b, proj, w_a, w_b, w_out, w_bn4, fnw)


DH = D // 2


_GW_TILES = (("q", 0, 512, 0), ("q", 1, 512, 512), ("kv", 0, 256, 1024), ("bl", 0, RANK, 5376),
             ("gla", 0, 512, 3328), ("gla", 1, 512, 3840), ("gla", 2, 512, 2304), ("gla", 3, 512, 2816),
             ("gates", 0, 512, 1280), ("gates", 1, 512, 1792), ("gates", 2, 512, 4352), ("gates", 3, 512, 4864),
             ("gates", 4, 512, 5392), ("gates", 5, 512, 5904), ("gates", 6, 512, 6416), ("gates", 7, 512, 6928))


def _gw_unpermute(piece, t):
    if piece == "q":
        parts = []
        for blk in range(t.shape[0] // LANE):
            g = [t[blk * LANE + 32 * i:blk * LANE + 32 * (i + 1)] for i in range(4)]
            parts += [g[0], g[2], g[1], g[3]]
        return jnp.concatenate(parts, axis=0)
    if piece == "kv":
        k = [t[64 * i:64 * i + 32] + t[64 * i + 32:64 * i + 64] for i in range(4)]
        v = [t[256 + 128 * g:256 + 128 * g + 64] + t[256 + 128 * g + 64:256 + 128 * (g + 1)] for g in range(2)]
        return jnp.concatenate(k + v, axis=0)
    if piece == "bl":
        return t[:RANK]
    return t


def _gw_half(h, pieces, half, after=None):
    T = h.shape[0]
    steps = len(_GW_TILES)

    def body(*refs):
        h_ref = refs[0]
        srcs = dict(zip(("q", "kv", "bl", "gla", "gates"), refs[1:6]))
        o_ref, stage, sems = refs[-3:]
        j = pl.program_id(0)

        def out_copy(k):
            _, _, n, off = _GW_TILES[k]
            return pltpu.make_async_copy(stage.at[k % 2, 0:n], o_ref.at[pl.ds(off, n)], sems.at[k % 2])

        for k, (piece, _, n, _) in enumerate(_GW_TILES):
            @pl.when(j == k)
            def _(k=k, piece=piece, n=n):
                if k >= 2:
                    out_copy(k - 2).wait()
                t = _gw_unpermute(piece, _dot_tn(srcs[piece][...], h_ref[...]))
                stage[k % 2, 0:n, :] = t.astype(stage.dtype)
                out_copy(k).start()

        @pl.when(j == steps - 1)
        def _():
            out_copy(steps - 2).wait()
            out_copy(steps - 1).wait()

    def tile_of(lo, hi):
        return lambda j: (0, jnp.clip(j - lo, 0, hi - lo - 1))

    in_specs = [pl.BlockSpec((T, DH), lambda j: (0, half)),
                pl.BlockSpec((T, 512), tile_of(0, 2)), pl.BlockSpec((T, 512), lambda j: (0, 0)),
                pl.BlockSpec((T, W_BL), lambda j: (0, 0)),
                pl.BlockSpec((T, 512), tile_of(4, 8)), pl.BlockSpec((T, 512), tile_of(8, 16))]
    args = [h, *pieces]
    if after is not None:
        in_specs.append(_any())
        args.append(after)
    return pl.pallas_call(
        body, name=f"gw_in_half{half}", grid=(steps,),
        in_specs=in_specs, out_specs=_any(),
        out_shape=jax.ShapeDtypeStruct((IN_WIDTH, DH), WIRE),
        scratch_shapes=[pltpu.VMEM((2, 512, DH), WIRE), pltpu.SemaphoreType.DMA((2,))],
        compiler_params=_cp(("arbitrary",)),
    )(*args)


def _chip_copies(s_ref, got_ref, send_sems, recv_sems):
    x, y, c = _place()
    chips = [(1 - x, y), (x, 1 - y), (1 - x, 1 - y)]
    return [pltpu.make_async_remote_copy(
        src_ref=s_ref.at[2 * px + py], dst_ref=got_ref.at[j],
        send_sem=send_sems.at[j], recv_sem=recv_sems.at[j], device_id=(px, py, c), device_id_type=MESH)
        for j, (px, py) in enumerate(chips)]


_EFFECT = pltpu.SideEffectType.DATAFLOW_SIDE_EFFECTING


def _hbm():
    return pl.BlockSpec(memory_space=pltpu.HBM)


def _sem():
    return pl.BlockSpec(memory_space=pltpu.SEMAPHORE)


def _chip_start(sums, half):
    land = pltpu.with_memory_space_constraint(lax.empty((3,) + sums.shape[1:], sums.dtype), pltpu.HBM)

    def body(s_ref, land_ref, send_sems, recv_sems, s_thru, land_thru, token):
        for cp in _chip_copies(s_ref, land_ref, send_sems, recv_sems):
            cp.start()
        token[...] = jnp.zeros_like(token)

    return pl.pallas_call(
        body, name=f"chip_start{half}",
        out_shape=(pltpu.SemaphoreType.DMA((3,)), pltpu.SemaphoreType.DMA((3,)),
                   pltpu.HBM(sums.shape, sums.dtype), pltpu.HBM(land.shape, land.dtype),
                   jax.ShapeDtypeStruct((8, LANE), F32)),
        in_specs=(_hbm(), _hbm()), out_specs=(_sem(), _sem(), _hbm(), _hbm(), _vmem()),
        input_output_aliases={0: 2, 1: 3},
        compiler_params=pltpu.CompilerParams(has_side_effects=_EFFECT),
    )(pltpu.with_memory_space_constraint(sums, pltpu.HBM), land)


def _chip_wait(send_sems, recv_sems, s_thru, land_thru, after, half):
    def body(s_ref, land_ref, send_sems, recv_sems, after_ref, s_out, got_ref):
        copies = _chip_copies(s_ref, land_ref, send_sems, recv_sems)
        for cp in copies:
            cp.wait_send()
        for cp in copies:
            cp.wait_recv()

    return pl.pallas_call(
        body, name=f"chip_wait{half}",
        out_shape=(pltpu.HBM(s_thru.shape, s_thru.dtype), pltpu.HBM(land_thru.shape, land_thru.dtype)),
        in_specs=(_hbm(), _hbm(), _sem(), _sem(), _any()), out_specs=(_hbm(), _hbm()),
        input_output_aliases={0: 0, 1: 1},
        compiler_params=pltpu.CompilerParams(has_side_effects=_EFFECT),
    )(s_thru, land_thru, send_sems, recv_sems, after)


def _dh_norm(pieces, offsets, wf, x, dx2, norm_w, after):
    T = x.shape[0]
    tT = min(T, 256)
    widths = [p.shape[1] for p in pieces]
    npc = len(pieces)

    def body(*refs):
        dp_refs = refs[:npc]
        wf_ref, x_ref, dx2_ref, nw_ref, _, gx_ref, gnw_ref = refs[npc:]

        @pl.when(pl.program_id(0) == 0)
        def _():
            gnw_ref[...] = jnp.zeros_like(gnw_ref)

        dh = jnp.zeros((tT, D), F32)
        for dp_ref, off, w in zip(dp_refs, offsets, widths):
            dh = dh + _dot(dp_ref[...], wf_ref[off:off + w, :])
        xv = x_ref[...]
        r = lax.rsqrt(jnp.mean(xv * xv, axis=-1, keepdims=True) + EPS)
        xh = xv * r
        gnw_ref[...] = gnw_ref[...] + jnp.broadcast_to(jnp.sum(dh * xh, axis=0, keepdims=True), gnw_ref.shape)
        g = dh * nw_ref[...]
        gx_ref[...] = r * (g - xh * jnp.mean(g * xh, axis=-1, keepdims=True)) + dx2_ref[...]

    tile = pl.BlockSpec((tT, D), lambda i: (i, 0))
    return pl.pallas_call(
        body, name="dh_norm", grid=(T // tT,),
        in_specs=[pl.BlockSpec((tT, w), lambda i: (i, 0)) for w in widths]
        + [_vmem(), tile, tile, pl.BlockSpec((1, D), lambda i: (0, 0)), _any()],
        out_specs=[tile, pl.BlockSpec((8, D), lambda i: (0, 0))],
        out_shape=[jax.ShapeDtypeStruct((T, D), F32), jax.ShapeDtypeStruct((8, D), F32)],
        compiler_params=_cp(("arbitrary",)),
    )(*pieces, wf, x, dx2, norm_w, after)


def _adamw_math(w, g, m, v):
    m = ADAM_B1 * m + (1.0 - ADAM_B1) * g
    v = ADAM_B2 * v + (1.0 - ADAM_B2) * (g * g)
    m_hat = m * (1.0 / (1.0 - ADAM_B1 ** ADAM_STEP))
    v_hat = v * (1.0 / (1.0 - ADAM_B2 ** ADAM_STEP))
    delta = -ADAM_LR * (m_hat / (jnp.sqrt(v_hat) + ADAM_EPS) + ADAM_WD * w)
    return delta, m, v


def _fetch_partials(s_ref, got_ref, buf, sems):
    x, y, _ = _place()
    cps = [pltpu.make_async_copy(s_ref.at[2 * x + y], buf.at[0], sems.at[0])]
    cps += [pltpu.make_async_copy(got_ref.at[j], buf.at[1 + j], sems.at[1 + j]) for j in range(3)]
    for cp in cps:
        cp.start()
    for cp in cps:
        cp.wait()


SMALL_AT = dict(norm_w=0, fnw=8, bias=16, bn=24, sinks=32, loss=40)
ROW_AT = (R_IN, R_A, R_B, R_O)


def _finish_small(ws, ms, vs, smalls):
    names = ["norm_w", "fnw", "bias", "bn", "sinks"]
    widths = [ws[n].shape[1] for n in names]

    def body(*refs):
        w_refs, m_refs, v_refs = refs[0:5], refs[5:10], refs[10:15]
        smalls_ref, loss_ref = refs[15], refs[16]
        outs, tot = refs[17:37], refs[37]
        acc = smalls_ref[0]
        for d in range(1, NDEV):
            acc = acc + smalls_ref[d]
        tot[...] = acc
        loss_ref[...] = tot[SMALL_AT["loss"]:SMALL_AT["loss"] + 1, 0:1]
        for p, (nm_, wd) in enumerate(zip(names, widths)):
            r = SMALL_AT[nm_]
            g = tot[r:r + 1, 0:wd]
            d, nm, nv = _adamw_math(w_refs[p][...], g, m_refs[p][...], v_refs[p][...])
            for o, val in zip(outs[4 * p:4 * p + 4], (g, d, nm, nv)):
                o[...] = val

    res = pl.pallas_call(
        body, name="finish_small",
        in_specs=[_vmem()] * 16, out_specs=[_vmem()] * 21,
        out_shape=[jax.ShapeDtypeStruct((1, 1), F32)]
        + [jax.ShapeDtypeStruct((1, wd), F32) for wd in widths for _ in range(4)],
        scratch_shapes=[pltpu.VMEM((SMALL_ROWS, D), F32)],
        compiler_params=_cp(),
    )(*[ws[n] for n in names], *[ms[n] for n in names], *[vs[n] for n in names], smalls)
    return res[0], {n: tuple(res[1 + 4 * p:5 + 4 * p]) for p, n in enumerate(names)}


def _finish(w_rows, m_rows, v_rows, gu_w, gu_m, gu_v, sums, got):
    shapes = [(SHARD, 1, D)] + [w.shape for w in w_rows[1:]]

    def columns(ref, p, cols):
        if p:
            return ref, (slice(None), cols)
        flat = ref if ref.shape == (SHARD * LANE_TILES, LANE) else ref.reshape(SHARD * LANE_TILES, LANE)
        return flat, (pl.ds(cols.start // LANE, SHARD, stride=LANE_TILES), slice(None))

    def read(ref, p, cols):
        ref, at = columns(ref, p, cols)
        return ref[at]

    def body(*refs):
        wr_refs, mr_refs, vr_refs = refs[0:4], refs[4:8], refs[8:12]
        guw_ref, gum_ref, guv_ref = refs[12:15]
        s_refs, got_refs = refs[15:17], refs[17:19]
        row_outs = refs[19:35]
        gu_outs = refs[35:39]
        buf, gsh, sems = refs[39:]
        x, y, c = _place()
        me_slot = 4 * x + 2 * y + c
        unshift = lax.rem(SHARD_PAD - 2 * me_slot, SHARD_PAD)

        def total(rows, cols):
            g = buf[0, rows, cols].astype(F32)
            for j in range(1, 4):
                g = g + buf[j, rows, cols].astype(F32)
            return g

        def update(p, g, cols):
            d, nm, nv = _adamw_math(read(wr_refs[p], p, cols), g, read(mr_refs[p], p, cols), read(vr_refs[p], p, cols))
            for o, val in zip(row_outs[4 * p:4 * p + 4], (g, d, nm, nv)):
                o, at = columns(o, p, cols)
                o[at] = val

        for hf in range(2):
            _fetch_partials(s_refs[hf], got_refs[hf], buf, sems)
            for cc in range(DH // LANE):
                src = slice(cc * LANE, (cc + 1) * LANE)
                cols = slice(hf * DH + cc * LANE, hf * DH + (cc + 1) * LANE)
                gsh[...] = pltpu.roll(total(slice(0, SHARD_PAD), src), unshift, 0)
                update(0, gsh[0:SHARD, :], cols)
                for p in range(1, 4):
                    update(p, total(slice(ROW_AT[p], ROW_AT[p] + 128), src), cols)
            if hf == 0:
                g = total(slice(R_GU, R_GU + RANK), slice(0, 64))
                d, nm, nv = _adamw_math(guw_ref[...], g, gum_ref[...], guv_ref[...])
                for o, val in zip(gu_outs, (g, d, nm, nv)):
                    o[...] = val

    res = pl.pallas_call(
        body, name="finish",
        in_specs=[_vmem()] * 15 + [_any()] * 4,
        out_specs=[_vmem()] * 20,
        out_shape=[jax.ShapeDtypeStruct(s, F32) for s in shapes for _ in range(4)]
        + [jax.ShapeDtypeStruct((RANK, 64), F32)] * 4,
        scratch_shapes=[pltpu.VMEM((4, ROWS, DH), sums[0].dtype), pltpu.VMEM((SHARD_PAD, LANE), F32),
                        pltpu.SemaphoreType.DMA((4,))],
        compiler_params=_cp(),
    )(*w_rows, *m_rows, *v_rows, gu_w, gu_m, gu_v, *sums, *got)
    return tuple(res[0:16]), tuple(res[16:20])


def _place():
    x, y, c = lax.axis_index("x"), lax.axis_index("y"), lax.axis_index("c")
    return x, y, c


def _peers(x, y, c):
    return [(x ^ dx, y ^ dy, c ^ dc) for dx in range(2) for dy in range(2) for dc in range(2) if dx + dy + dc]


def _late_gather_start(blk, after, name="late_gather"):
    land = pltpu.with_memory_space_constraint(lax.empty((NDEV,) + blk.shape, blk.dtype), pltpu.HBM)

    def body(b_ref, land_ref, after_ref, send_sems, recv_sems, b_thru, land_thru, token):
        x, y, c = _place()
        for k, to in enumerate(_peers(x, y, c)):
            pltpu.make_async_remote_copy(
                src_ref=b_ref, dst_ref=land_ref.at[4 * x + 2 * y + c], send_sem=send_sems.at[k],
                recv_sem=recv_sems.at[k], device_id=to, device_id_type=MESH).start()
        token[...] = jnp.zeros_like(token)

    return pl.pallas_call(
        body, name=name + "_start",
        out_shape=(pltpu.SemaphoreType.DMA((7,)), pltpu.SemaphoreType.DMA((7,)),
                   pltpu.HBM(blk.shape, blk.dtype), pltpu.HBM(land.shape, land.dtype),
                   jax.ShapeDtypeStruct((8, LANE), F32)),
        in_specs=(_hbm(), _hbm(), _any()), out_specs=(_sem(), _sem(), _hbm(), _hbm(), _vmem()),
        input_output_aliases={0: 2, 1: 3},
        compiler_params=pltpu.CompilerParams(has_side_effects=_EFFECT),
    )(pltpu.with_memory_space_constraint(blk, pltpu.HBM), land, after)


def _late_gather_wait(send_sems, recv_sems, b_thru, land_thru, after, after2, name="late_gather"):
    def body(b_ref, land_ref, send_sems, recv_sems, after_ref, after2_ref, b_out, got_ref):
        x, y, c = _place()
        copies = [pltpu.make_async_remote_copy(
            src_ref=b_ref, dst_ref=land_ref.at[4 * x + 2 * y + c], send_sem=send_sems.at[k],
            recv_sem=recv_sems.at[k], device_id=to, device_id_type=MESH)
            for k, to in enumerate(_peers(x, y, c))]
        for cp in copies:
            cp.wait_send()
        for cp in copies:
            cp.wait_recv()

    return pl.pallas_call(
        body, name=name + "_wait",
        out_shape=(pltpu.HBM(b_thru.shape, b_thru.dtype), pltpu.HBM(land_thru.shape, land_thru.dtype)),
        in_specs=(_hbm(), _hbm(), _sem(), _sem(), _any(), _any()), out_specs=(_hbm(), _hbm()),
        input_output_aliases={0: 0, 1: 1},
        compiler_params=pltpu.CompilerParams(has_side_effects=_EFFECT),
    )(b_thru, land_thru, send_sems, recv_sems, after, after2)


G_ROWS = SHARD_PAD + RANK


def _gather_blocks(w_in_t, gu_s, xs, norm_w, pos_col):
    rows, cols = G_ROWS, D
    T = xs.shape[0]
    tT = min(T, 256)
    inv_row, sign_row = _rope_rows()

    def body(wi_ref, gu_ref, xs_hbm, nw_ref, pos_ref, inv_ref, sign_ref,
             out_ref, h_ref, cos_ref, sin_ref, x_ref, frame_ref, xs_ref, send_sems, recv_sems, local_sem, xs_sem):
        load_xs = pltpu.make_async_copy(xs_hbm, xs_ref, xs_sem)
        load_xs.start()
        x, y, c = _place()
        me, sibling = (x, y, c), (x, y, 1 - c)
        chips = [(1 - x, y), (x, 1 - y), (1 - x, 1 - y)]
        shift = 2 * (4 * x + 2 * y + c)
        frame_ref[SHARD - SHARD % 8:, :] = jnp.zeros((SHARD_PAD - SHARD + SHARD % 8, LANE), F32)
        for cc in range(LANE_TILES):
            cs = slice(cc * LANE, (cc + 1) * LANE)
            frame_ref[:SHARD, :] = wi_ref[pl.ds(cc, SHARD, stride=LANE_TILES), :]
            x_ref[0:SHARD_PAD, cs] = pltpu.roll(frame_ref[...], shift, 0).astype(x_ref.dtype)
        x_ref[SHARD_PAD:G_ROWS, :] = jnp.zeros((RANK, D), x_ref.dtype)
        x_ref[SHARD_PAD:G_ROWS, 0:64] = gu_ref[...].astype(x_ref.dtype)

        def slot(px, py, pc):
            return out_ref.at[4 * px + 2 * py + pc]

        def copy(k, block, to, src=None):
            return pltpu.make_async_remote_copy(
                src_ref=slot(*block) if src is None else src, dst_ref=slot(*block),
                send_sem=send_sems.at[k], recv_sem=recv_sems.at[k], device_id=to, device_id_type=MESH)

        mine = pltpu.make_async_copy(x_ref, slot(*me), local_sem)
        mine.start()
        first = [copy(0, me, sibling, src=x_ref)]
        first += [copy(1 + j, me, (*chip, c), src=x_ref) for j, chip in enumerate(chips)]
        for cp in first:
            cp.start()
        load_xs.wait()

        @pl.loop(0, T // tT)
        def _(i):
            rows_i = pl.ds(pl.multiple_of(i * tT, tT), tT)
            _prologue_rows(rows_i, xs_ref, nw_ref, pos_ref, inv_ref, sign_ref, h_ref, cos_ref, sin_ref)

        passed = [copy(4 + j, (*chip, c), sibling) for j, chip in enumerate(chips)]
        for j, chip in enumerate(chips):
            copy(1 + j, (*chip, c), me).wait_recv()
            passed[j].start()
        copy(0, sibling, me).wait_recv()
        for j, chip in enumerate(chips):
            copy(4 + j, (*chip, 1 - c), me).wait_recv()
        for cp in first + passed:
            cp.wait_send()
        mine.wait()

    return pl.pallas_call(
        body, name="gather_weights",
        in_specs=[_vmem(), _vmem(), _any()] + [_vmem()] * 4, out_specs=[_any()] + [_vmem()] * 3,
        out_shape=[jax.ShapeDtypeStruct((NDEV, rows, cols), WIRE), jax.ShapeDtypeStruct((T, D), MXU),
                   jax.ShapeDtypeStruct((T, LANE), F32), jax.ShapeDtypeStruct((T, LANE), F32)],
        scratch_shapes=[pltpu.VMEM((rows, cols), WIRE), pltpu.VMEM((SHARD_PAD, LANE), F32), pltpu.VMEM((T, D), F32),
                        pltpu.SemaphoreType.DMA((7,)), pltpu.SemaphoreType.DMA((7,)), pltpu.SemaphoreType.DMA,
                        pltpu.SemaphoreType.DMA],
        compiler_params=_cp(),
    )(w_in_t, gu_s, xs, norm_w, pos_col, inv_row, sign_row)


def _pair_reduce(gwt, tail, half):
    n = gwt.shape[1]
    rows = SHARD_PAD + tail.shape[1]
    blk = (4, rows, n)

    def body(g_ref, t_ref, out_ref, acc, got, own, send_sems, recv_sems, own_sems, out_sems):
        x, y, c = _place()

        def parts(d, dst):
            frame = g_ref.at[pl.ds(pl.multiple_of(FRAME * d, 16), SHARD_PAD)]
            return [(frame, dst.at[0:SHARD_PAD]), (t_ref.at[d], dst.at[SHARD_PAD:rows])]

        sends, loads, stores = [], [], []
        for chip in range(4):
            sends.append([pltpu.make_async_remote_copy(
                src_ref=s, dst_ref=d_, send_sem=send_sems.at[chip, k], recv_sem=recv_sems.at[chip, k],
                device_id=(x, y, 1 - c), device_id_type=MESH)
                for k, (s, d_) in enumerate(parts(2 * chip + (1 - c), got.at[chip]))])
            loads.append([pltpu.make_async_copy(s, d_, own_sems.at[chip, k])
                          for k, (s, d_) in enumerate(parts(2 * chip + c, own.at[chip]))])
            stores.append(pltpu.make_async_copy(acc.at[chip], out_ref.at[chip], out_sems.at[chip]))
        for group in sends + loads:
            for cp in group:
                cp.start()
        for chip in range(4):
            for cp in loads[chip]:
                cp.wait()
            for cp in sends[chip]:
                cp.wait_recv()
            acc[chip] = (own[chip].astype(F32) + got[chip].astype(F32)).astype(acc.dtype)
            stores[chip].start()
        for cp in stores:
            cp.wait()
        for group in sends:
            for cp in group:
                cp.wait_send()

    return pl.pallas_call(
        body, name=f"pair_reduce{half}",
        in_specs=[_any(), _any()], out_specs=_any(),
        out_shape=jax.ShapeDtypeStruct(blk, gwt.dtype),
        scratch_shapes=[pltpu.VMEM(blk, gwt.dtype), pltpu.VMEM(blk, gwt.dtype), pltpu.VMEM(blk, gwt.dtype),
                        pltpu.SemaphoreType.DMA((4, 2)), pltpu.SemaphoreType.DMA((4, 2)),
                        pltpu.SemaphoreType.DMA((4, 2)), pltpu.SemaphoreType.DMA((4,))],
        compiler_params=_cp(),
    )(gwt, tail)


def _pad_cols(a, cols):
    return jnp.pad(a, ((0, 0), (0, cols - a.shape[1])))


def _pad_rows(a, rows):
    return jnp.pad(a, ((0, rows - a.shape[0]), (0, 0)))


FRAME = 928


def _wft_plan():
    moves = []
    for blk in range(8):
        for half in range(2):
            for sub in range(2):
                moves.append((C_Q + 128 * blk + 32 * (2 * half + sub), 128 * blk + 32 * (2 * sub + half), 32))
    for idx in range(4):
        for dup in range(2):
            moves.append((C_KD + 64 * idx + 32 * dup, 1024 + 32 * idx, 32))
    for g in range(2):
        for dup in range(2):
            moves.append((C_VD + 128 * g + 64 * dup, 1152 + 64 * g, 64))
    moves += [(C_BL, 5376, RANK), (C_BV, 3328, 1024), (C_BQ, 2304, 512), (C_BK, 2816, 512),
              (C_AG, 1280, 1024), (C_BG, 4352, 1024), (C_MA, 5392, 1024), (C_MB, 6416, 1024)]
    bulk, seams = [], []
    for dst, src, n in moves:
        r = src
        while r < src + n:
            f = min(r // FRAME, NDEV - 1)
            local = r - FRAME * f
            if f > 0 and local < 16:
                assert local == 0
                seams.append((f, dst + r - src))
                step = 16
            else:
                step = min(src + n, FRAME * (f + 1) if f < NDEV - 1 else IN_WIDTH) - r
                bulk.append((f, local, dst + r - src, step))
            r += step
    assert sorted(f for f, _ in seams) == list(range(1, NDEV))
    return bulk, seams, [(C_BL + RANK, C_GLA - C_BL - RANK)]


def _build_wft_copies(frames):
    bulk, seams, zeros = _wft_plan()
    (z0, zn), = zeros

    def body(f_ref, o_ref, edge, sems, esems):
        copies = [pltpu.make_async_copy(f_ref.at[f, pl.ds(l0, n)], o_ref.at[pl.ds(dst, n)], sems.at[i])
                  for i, (f, l0, dst, n) in enumerate(bulk)]
        loads = []
        for i, (f, _) in enumerate(seams):
            loads.append(pltpu.make_async_copy(f_ref.at[f, pl.ds(0, 16)], edge.at[i, 0], esems.at[i, 0]))
            loads.append(pltpu.make_async_copy(f_ref.at[f - 1, pl.ds(FRAME, 16)], edge.at[i, 1], esems.at[i, 1]))
        for cp in copies + loads:
            cp.start()
        o_ref[z0:z0 + zn, :] = jnp.zeros((zn, D), o_ref.dtype)
        for cp in loads:
            cp.wait()
        for i, (_, dst) in enumerate(seams):
            o_ref[dst:dst + 16, :] = edge[i, 0] + edge[i, 1]
        for cp in copies:
            cp.wait()

    return pl.pallas_call(
        body, name="build_wft",
        in_specs=[_any()], out_specs=_vmem(),
        out_shape=jax.ShapeDtypeStruct((NF, D), frames.dtype),
        scratch_shapes=[pltpu.VMEM((len(seams), 2, 16, D), frames.dtype),
                        pltpu.SemaphoreType.DMA((len(bulk),)), pltpu.SemaphoreType.DMA((len(seams), 2))],
        compiler_params=_cp(),
    )(frames)


def kernel(x, positions, norm_w, w_in, a_sinks, b_gate_up, b_gate_bias, b_out_norm_w, w_a_proj, w_b_proj, w_out, final_norm_w, loss_target, m_norm_w, m_w_in, m_a_sinks, m_b_gate_up, m_b_gate_bias, m_b_out_norm_w, m_w_a_proj, m_w_b_proj, m_w_out, m_final_norm_w, v_norm_w, v_w_in, v_a_sinks, v_b_gate_up, v_b_gate_bias, v_b_out_norm_w, v_w_a_proj, v_w_b_proj, v_w_out, v_final_norm_w):
    T = x.shape[1]
    xs, target = x[0], loss_target[0]
    fnw = final_norm_w.reshape(1, D)
    me = 4 * lax.axis_index("x") + 2 * lax.axis_index("y") + lax.axis_index("c")
    allw, h, cos, sin = _gather_blocks(_by_lane_tile(w_in), b_gate_up[0], xs, norm_w, positions.reshape(T, 1))
    late_blk = jnp.concatenate([w_a_proj[0], w_b_proj[0], w_out[0]], axis=0).astype(WIRE)
    l_send, l_recv, l_blk, l_land, l_started = _late_gather_start(late_blk, cos)
    wf = _build_wft_copies(allw)
    gu = allw[:, SHARD_PAD:G_ROWS, :64].transpose(1, 0, 2).reshape(RANK, 512)
    gu_pad = _pad_rows(gu, W_BL)

    proj = _proj(h, wf, l_started)
    o_a, lse = _swa_fwd(proj, cos, sin, a_sinks)
    o_b, states = _gla_fwd(proj, gu_pad, b_gate_bias)
    l_blk, l_land = _late_gather_wait(l_send, l_recv, l_blk, l_land, states, lse)
    late = lax.dynamic_update_slice(l_land, l_blk[None], (me, 0, 0))
    w_a, w_b, w_o = (late[:, 128 * i:128 * (i + 1), :].reshape(D, D) for i in range(3))
    (dx2, do_a, do_b, d_gates, g_wa, g_wb, g_wo, g_fn, g_bn, loss_part) = _mid(
        xs, target, proj, o_a, o_b, w_a, w_b, w_o, jnp.tile(b_out_norm_w, (1, B_HEADS)), fnw)
    d_q, d_kv, g_sinks = _swa_bwd(proj, cos, sin, a_sinks, do_a, o_a, lse, cos)
    d_gla, d_bl, g_gu, g_bias = _gla_bwd(proj, gu_pad, b_gate_bias, states, do_b)
    pieces = [d_q, d_kv, d_bl, d_gla, d_gates]
    offsets = [C_Q, C_KD, C_BL, C_GLA, C_GATES]

    ggu = g_gu[:RANK].reshape(RANK, NDEV, 64).transpose(1, 0, 2)
    ggu_half = [jnp.pad(ggu, ((0, 0), (0, 0), (0, DH - 64))), jnp.zeros((NDEV, RANK, DH), F32)]

    def tail(hf):
        cols = slice(hf * DH, (hf + 1) * DH)
        return jnp.concatenate([g[:, cols].reshape(NDEV, 128, DH) for g in (g_wa, g_wb, g_wo)]
                               + [ggu_half[hf]], axis=1).astype(WIRE)

    send0, recv0, s_thru0, land0, started0 = _chip_start(_pair_reduce(_gw_half(h, pieces, 0), tail(0), 0), 0)
    send1, recv1, s_thru1, land1, started1 = _chip_start(
        _pair_reduce(_gw_half(h, pieces, 1, after=started0), tail(1), 1), 1)
    grad_x, g_nw = _dh_norm(pieces, offsets, wf, xs, dx2, norm_w, started1)
    small = jnp.concatenate([g_nw, g_fn, _pad_cols(g_bias, D), _pad_cols(g_bn, D), _pad_cols(g_sinks, D),
                             _pad_cols(loss_part, D)], axis=0)
    sm_send, sm_recv, sm_blk, sm_land, sm_started = _late_gather_start(small, g_nw, name="small_gather")
    sums0, got0 = _chip_wait(send0, recv0, s_thru0, land0, sm_started, 0)
    sums1, got1 = _chip_wait(send1, recv1, s_thru1, land1, got0, 1)
    sums, from_chips = [sums0, sums1], [got0, got1]

    ws = dict(norm_w=norm_w, fnw=fnw, bias=b_gate_bias, bn=b_out_norm_w, sinks=a_sinks)
    ms = dict(norm_w=m_norm_w, fnw=m_final_norm_w.reshape(1, D), bias=m_b_gate_bias, bn=m_b_out_norm_w,
              sinks=m_a_sinks)
    vs = dict(norm_w=v_norm_w, fnw=v_final_norm_w.reshape(1, D), bias=v_b_gate_bias, bn=v_b_out_norm_w,
              sinks=v_a_sinks)
    t_rows, t_gu = _finish(
        [_by_lane_tile(w_in), w_a_proj[0], w_b_proj[0], w_out[0]],
        [_by_lane_tile(m_w_in), m_w_a_proj[0], m_w_b_proj[0], m_w_out[0]],
        [_by_lane_tile(v_w_in), v_w_a_proj[0], v_w_b_proj[0], v_w_out[0]],
        b_gate_up[0], m_b_gate_up[0], v_b_gate_up[0], sums, from_chips)
    sm_blk, sm_land = _late_gather_wait(sm_send, sm_recv, sm_blk, sm_land, t_rows[0], t_gu[0], name="small_gather")
    loss, sm = _finish_small(ws, ms, vs, lax.dynamic_update_slice(sm_land, sm_blk[None], (me, 0, 0)))

    def outputs(k):
        return [sm["norm_w"][k], jnp.transpose(t_rows[k], (1, 2, 0)), sm["sinks"][k], t_gu[k][None], sm["bias"][k], sm["bn"][k],
                t_rows[4 + k][None], t_rows[8 + k][None], t_rows[12 + k][None], sm["fnw"][k].reshape(D)]

    return (loss[0, 0], grad_x[None], *outputs(0), *outputs(1), *outputs(2), *outputs(3))
```

```python
import functools

import numpy as np
import jax
import jax.numpy as jnp
from jax import lax
from jax.experimental import pallas as pl
from jax.experimental.pallas import tpu as pltpu

F32 = jnp.float32
MXU = jnp.bfloat16
WIRE = jnp.bfloat16

D = 1024
A_HEADS, A_KV, A_HD = 16, 2, 64
BLK = 128
B_HEADS, B_DK, B_DV = 4, 128, 256
RANK, TAU, CHUNK = 16, 16.0, 64
EPS, NEG = 1e-5, -1e30
ROPE_THETA = 10000.0
IN_WIDTH, NDEV = 7440, 8
SHARD = IN_WIDTH // NDEV
LANE = 128
LANE_TILES = D // LANE


def _by_lane_tile(a):
    return jnp.transpose(a, (2, 0, 1)).reshape(SHARD * LANE_TILES, LANE)


C_Q, C_KD, C_VD, C_BL = 0, 1024, 1280, 1536
C_BV, C_BQ, C_BK = 2048, 3072, 3584
C_AG, C_BG, C_MA, C_MB = 4096, 5120, 6144, 7168
C_GLA, W_GLA, C_GATES, W_GATES = 2048, 2048, 4096, 4096
NF = 8192
W_BL = 128

SHARD_PAD = 944
R_IN, R_A, R_B, R_O, R_GU, ROWS = 0, 944, 1072, 1200, 1328, 1344
SMALL_ROWS = 48

ADAM_LR, ADAM_B1, ADAM_B2, ADAM_EPS, ADAM_WD, ADAM_STEP = 0.001, 0.9, 0.999, 1e-08, 0.01, 10

MESH = pl.DeviceIdType.MESH
VMEM_LIMIT = 56 * 1024 * 1024


def _cp(sem=None, **kw):
    if sem is not None:
        kw["dimension_semantics"] = sem
    return pltpu.CompilerParams(vmem_limit_bytes=VMEM_LIMIT, **kw)


def _dot(a, b):
    return jnp.dot(a, b, preferred_element_type=F32)


def _dot_nt(a, b):
    return lax.dot_general(a, b, (((1,), (1,)), ((), ())), preferred_element_type=F32)


def _dot_tn(a, b):
    return lax.dot_general(a, b, (((0,), (0,)), ((), ())), preferred_element_type=F32)


def _dot_f32(a, b):
    return jnp.dot(a, b, preferred_element_type=F32, precision=lax.Precision.HIGHEST)


def _sigmoid(z):
    return 0.5 * jnp.tanh(0.5 * z) + 0.5


def _rope(xp, cos, sin):
    return xp * cos + pltpu.roll(xp, 64, 1) * sin


def _rope_bwd(dy, cos, sin):
    return dy * cos - pltpu.roll(dy, 64, 1) * sin


def _vmem():
    return pl.BlockSpec(memory_space=pltpu.VMEM)


def _any():
    return pl.BlockSpec(memory_space=pl.ANY)


def _rope_rows():
    half = A_HD // 2
    inv = (np.float32(ROPE_THETA) ** (-np.arange(half, dtype=np.float32) / np.float32(half))).astype(np.float32)
    inv_row = jnp.asarray(np.tile(inv, 4)[None, :])
    sign_row = jnp.asarray(np.concatenate([-np.ones(64, np.float32), np.ones(64, np.float32)])[None, :])
    return inv_row, sign_row


def _prologue_rows(rows, x_ref, nw_ref, pos_ref, inv_ref, sign_ref, h_ref, cos_ref, sin_ref):
    xv = x_ref[rows, :]
    r = lax.rsqrt(jnp.mean(xv * xv, axis=-1, keepdims=True) + EPS)
    h_ref[rows, :] = ((xv * r) * nw_ref[...]).astype(h_ref.dtype)
    ang = pos_ref[rows, :].astype(F32) * inv_ref[...]
    cos_ref[rows, :] = jnp.cos(ang)
    sin_ref[rows, :] = jnp.sin(ang) * sign_ref[...]


def _proj(h, wft, after):
    T = h.shape[0]
    tT, tN = T, 512

    def body(h_ref, w_ref, after_ref, o_ref):
        o_ref[...] = _dot_nt(h_ref[...], w_ref[...])

    return pl.pallas_call(
        body, name="proj", grid=(T // tT, NF // tN),
        in_specs=[pl.BlockSpec((tT, D), lambda i, j: (i, 0)), pl.BlockSpec((tN, D), lambda i, j: (j, 0)), _any()],
        out_specs=pl.BlockSpec((tT, tN), lambda i, j: (i, j)),
        out_shape=jax.ShapeDtypeStruct((T, NF), F32),
        compiler_params=_cp(("parallel", "parallel")),
    )(h, wft, after)


def _swa_masks():
    lane = lax.broadcasted_iota(jnp.int32, (BLK, LANE), 1)
    rope_sub0 = ((lane // 32) % 2) == 0
    std_sub0 = lane < 64
    return lane, rope_sub0, std_sub0


def _swa_tri():
    qi = lax.broadcasted_iota(jnp.int32, (BLK, BLK), 0)
    kj = lax.broadcasted_iota(jnp.int32, (BLK, BLK), 1)
    return kj <= qi


def _swa_fold(full, tri):
    return jnp.where(tri, full[:, BLK:], full[:, :BLK])


def _swa_unfold(sq, tri):
    return jnp.concatenate([jnp.where(tri, 0.0, sq), jnp.where(tri, sq, 0.0)], axis=1)


def _swa_keys(kc_ref, kp_ref, vc_ref, vp_ref, cq, sq, cp, sp):
    def ropek(kref, c, s):
        kv = kref[...]
        return jnp.concatenate([_rope(kv[:, :LANE], c, s), _rope(kv[:, LANE:], c, s)], axis=1)

    K = jnp.concatenate([ropek(kp_ref, cp, sp), ropek(kc_ref, cq, sq)], axis=0).astype(MXU)
    V = jnp.concatenate([vp_ref[...], vc_ref[...]], axis=0).astype(MXU)
    return K, V


def _swa_in_specs(nb, last):
    def cur(n):
        return jnp.minimum(n, last)

    def prev(n):
        return jnp.maximum(cur(n) - 1, 0)

    kd, vd = C_KD // 256, C_VD // 256
    return [
        pl.BlockSpec((BLK, D), lambda n: (cur(n), C_Q // D)),
        pl.BlockSpec((BLK, 256), lambda n: (cur(n), kd)),
        pl.BlockSpec((BLK, 256), lambda n: (prev(n), kd)),
        pl.BlockSpec((BLK, 256), lambda n: (cur(n), vd)),
        pl.BlockSpec((BLK, 256), lambda n: (prev(n), vd)),
        pl.BlockSpec((BLK, LANE), lambda n: (cur(n), 0)),
        pl.BlockSpec((BLK, LANE), lambda n: (cur(n), 0)),
        pl.BlockSpec((BLK, LANE), lambda n: (prev(n), 0)),
        pl.BlockSpec((BLK, LANE), lambda n: (prev(n), 0)),
    ]


def _swa_fwd(proj, cos, sin, sinks):
    T = proj.shape[0]
    nb = T // BLK
    scale = A_HD ** -0.5

    def body(sinks_ref, q_ref, kc_ref, kp_ref, vc_ref, vp_ref, cq_ref, sq_ref, cp_ref, sp_ref, o_ref, l_ref):
        n = pl.program_id(0)
        cq, sq = cq_ref[...], sq_ref[...]
        K, V = _swa_keys(kc_ref, kp_ref, vc_ref, vp_ref, cq, sq, cp_ref[...], sp_ref[...])
        tri = _swa_tri()
        valid = tri | (n > 0)
        lane, rope_sub0, std_sub0 = _swa_masks()
        group = A_HEADS // A_KV
        roped, lses = {}, []

        def products(head):
            pb, sub, g = head // 2, head % 2, head // group
            if sub == 0:
                roped[pb] = _rope(q_ref[:, pb * LANE:(pb + 1) * LANE], cq, sq)
            qm = jnp.where(rope_sub0 if sub == 0 else ~rope_sub0, roped[pb], 0.0).astype(MXU)
            return _dot_nt(qm, K[:, g * LANE:(g + 1) * LANE])

        def softmax(head, s_full):
            s = jnp.where(valid, _swa_fold(s_full, tri) * scale, NEG)
            sink = sinks_ref[0, head]
            m = jnp.maximum(jnp.max(s, axis=1, keepdims=True), sink)
            e = jnp.exp(s - m)
            den = jnp.sum(e, axis=1, keepdims=True) + jnp.exp(sink - m)
            lses.append(m + jnp.log(den))
            return _swa_unfold(e / den, tri).astype(MXU)

        outs = {}
        st1 = {0: products(0), 1: products(1)}
        st2 = {0: softmax(0, st1.pop(0))}
        for head in range(A_HEADS):
            if head + 2 < A_HEADS:
                st1[head + 2] = products(head + 2)
            if head + 1 < A_HEADS:
                st2[head + 1] = softmax(head + 1, st1.pop(head + 1))
            g = head // group
            outs[head] = _dot(st2.pop(head), V[:, g * LANE:(g + 1) * LANE])
            if head % 2 == 1:
                pb = head // 2
                o_ref[:, pb * LANE:(pb + 1) * LANE] = jnp.where(std_sub0, outs[head - 1], outs[head])
        lacc = jnp.zeros((BLK, LANE), F32)
        for head in range(A_HEADS):
            lacc = jnp.where(lane == head, lses[head], lacc)
        l_ref[...] = lacc

    return pl.pallas_call(
        body, name="swa_fwd", grid=(nb,),
        in_specs=[pl.BlockSpec(memory_space=pltpu.SMEM)] + _swa_in_specs(nb, nb - 1),
        out_specs=[pl.BlockSpec((BLK, D), lambda n: (n, 0)), pl.BlockSpec((BLK, LANE), lambda n: (n, 0))],
        out_shape=[jax.ShapeDtypeStruct((T, D), F32), jax.ShapeDtypeStruct((T, LANE), F32)],
        compiler_params=_cp(("parallel",)),
    )(sinks, proj, proj, proj, proj, proj, cos, sin, cos, sin)


def _swa_bwd(proj, cos, sin, sinks, do_a, o_a, lse, after):
    T = proj.shape[0]
    nb = T // BLK
    scale = A_HD ** -0.5

    def body(sinks_ref, q_ref, kc_ref, kp_ref, vc_ref, vp_ref, cq_ref, sq_ref, cp_ref, sp_ref,
             do_ref, o_ref, l_ref, after_ref, dq_ref, dkv_ref, ds_ref, ckv_ref):
        n = pl.program_id(0)

        @pl.when(n == 0)
        def _():
            ckv_ref[...] = jnp.zeros_like(ckv_ref)
            ds_ref[...] = jnp.zeros_like(ds_ref)

        @pl.when(n < nb)
        def _():
            cq, sq, cp, sp = cq_ref[...], sq_ref[...], cp_ref[...], sp_ref[...]
            K, V = _swa_keys(kc_ref, kp_ref, vc_ref, vp_ref, cq, sq, cp, sp)
            tri = _swa_tri()
            valid = tri | (n > 0)
            lane, rope_sub0, std_sub0 = _swa_masks()
            lane_row = lax.broadcasted_iota(jnp.int32, (1, LANE), 1)
            lse_v = l_ref[...]
            dKt = [jnp.zeros((LANE, 2 * BLK), F32) for _ in range(A_KV)]
            dVt = [jnp.zeros((LANE, 2 * BLK), F32) for _ in range(A_KV)]
            dsinks, roped, roped_t, do_t = [], {}, {}, {}
            group = A_HEADS // A_KV
            dim = lax.broadcasted_iota(jnp.int32, (LANE, BLK), 0)
            rope_row0, std_row0 = ((dim // 32) % 2) == 0, dim < 64

            def products(head):
                pb, sub, g = head // 2, head % 2, head // group
                cols = slice(pb * LANE, (pb + 1) * LANE)
                Kg, Vg = K[:, g * LANE:(g + 1) * LANE], V[:, g * LANE:(g + 1) * LANE]
                if sub == 0:
                    roped[pb] = _rope(q_ref[:, cols], cq, sq)
                    roped_t[pb] = roped[pb].T
                    do_t[pb] = do_ref[:, cols].T
                qm = jnp.where(rope_sub0 if sub == 0 else ~rope_sub0, roped[pb], 0.0).astype(MXU)
                qmt = jnp.where(rope_row0 if sub == 0 else ~rope_row0, roped_t[pb], 0.0).astype(MXU)
                dov = jnp.where(std_sub0 if sub == 0 else ~std_sub0, do_ref[:, cols], 0.0)
                dovt = jnp.where(std_row0 if sub == 0 else ~std_row0, do_t[pb], 0.0).astype(MXU)
                delta = jnp.sum(dov * o_ref[:, cols], axis=1, keepdims=True)
                return qmt, dovt, delta, _dot_nt(qm, Kg), _dot_nt(dov.astype(MXU), Vg)

            def scores(head, qmt, dovt, delta, s_full, dp_full):
                lh = jnp.sum(jnp.where(lane == head, lse_v, 0.0), axis=1, keepdims=True)
                p = jnp.where(valid, jnp.exp(_swa_fold(s_full, tri) * scale - lh), 0.0)
                psink = jnp.exp(sinks_ref[0, head] - lh)
                dsinks.append(jnp.sum(-psink * delta, axis=0, keepdims=True))
                dsq = (p * (_swa_fold(dp_full, tri) - delta)) * scale
                return qmt, dovt, _swa_unfold(p, tri).astype(MXU), _swa_unfold(dsq, tri).astype(MXU)

            def grads(head, qmt, dovt, pb16, dsc):
                g = head // group
                dKt[g] = dKt[g] + _dot(qmt, dsc)
                dVt[g] = dVt[g] + _dot(dovt, pb16)
                return _dot(dsc, K[:, g * LANE:(g + 1) * LANE])

            dqs = {}
            st1 = {0: products(0), 1: products(1)}
            st2 = {0: scores(0, *st1.pop(0))}
            for head in range(A_HEADS):
                if head + 2 < A_HEADS:
                    st1[head + 2] = products(head + 2)
                if head + 1 < A_HEADS:
                    st2[head + 1] = scores(head + 1, *st1.pop(head + 1))
                dqs[head] = grads(head, *st2.pop(head))
                if head % 2 == 1:
                    pb = head // 2
                    dqp = jnp.where(rope_sub0, dqs[head - 1], dqs[head])
                    dq_ref[:, pb * LANE:(pb + 1) * LANE] = _rope_bwd(dqp, cq, sq).astype(dq_ref.dtype)
            dsink = jnp.zeros((1, LANE), F32)
            for head in range(A_HEADS):
                dsink = jnp.where(lane_row == head, dsinks[head], dsink)
            dK, dV = [a.T for a in dKt], [a.T for a in dVt]
            prev = ([_rope_bwd(dK[g][:BLK], cp, sp) for g in range(A_KV)] + [dV[g][:BLK] for g in range(A_KV)])
            cur_ = ([_rope_bwd(dK[g][BLK:], cq, sq) for g in range(A_KV)] + [dV[g][BLK:] for g in range(A_KV)])
            dkv_ref[...] = (ckv_ref[...] + jnp.concatenate(prev, axis=1)).astype(dkv_ref.dtype)
            ckv_ref[...] = jnp.concatenate(cur_, axis=1)
            ds_ref[...] = ds_ref[...] + jnp.broadcast_to(dsink, ds_ref.shape)

        @pl.when(n == nb)
        def _():
            dkv_ref[...] = ckv_ref[...].astype(dkv_ref.dtype)

    last = nb - 1

    def cur(n):
        return jnp.minimum(n, last)

    def out_kv(n):
        return (jnp.maximum(n - 1, 0), 0)

    return pl.pallas_call(
        body, name="swa_bwd", grid=(nb + 1,),
        in_specs=[pl.BlockSpec(memory_space=pltpu.SMEM)] + _swa_in_specs(nb, last) + [
            pl.BlockSpec((BLK, D), lambda n: (cur(n), 0)),
            pl.BlockSpec((BLK, D), lambda n: (cur(n), 0)),
            pl.BlockSpec((BLK, LANE), lambda n: (cur(n), 0)),
            _any(),
        ],
        out_specs=[
            pl.BlockSpec((BLK, D), lambda n: (cur(n), 0)),
            pl.BlockSpec((BLK, 512), out_kv),
            pl.BlockSpec((8, LANE), lambda n: (0, 0)),
        ],
        out_shape=[
            jax.ShapeDtypeStruct((T, D), MXU),
            jax.ShapeDtypeStruct((T, 512), MXU),
            jax.ShapeDtypeStruct((8, LANE), F32),
        ],
        scratch_shapes=[pltpu.VMEM((BLK, 512), F32)],
        compiler_params=_cp(("arbitrary",)),
    )(sinks, proj, proj, proj, proj, proj, cos, sin, cos, sin, do_a, o_a, lse, after)


NCH = 4
GSTEP = NCH * CHUNK
ST_ROWS = B_HEADS * B_DV


def _chunk_rows(c):
    return slice(c * CHUNK, (c + 1) * CHUNK)


def _per_chunk(which, vals):
    out = vals[-1]
    for c in range(NCH - 2, -1, -1):
        out = jnp.where(which == c, vals[c], out)
    return out


def _gla_gate(bl_ref, gu_ref, bias_ref):
    gk = _dot(bl_ref[...].astype(MXU), gu_ref[...]) + bias_ref[...]
    la = (jnp.minimum(gk, 0.0) - jnp.log(1.0 + jnp.exp(-jnp.abs(gk)))) / TAU
    ri = lax.broadcasted_iota(jnp.int32, (GSTEP, GSTEP), 0)
    ci = lax.broadcasted_iota(jnp.int32, (GSTEP, GSTEP), 1)
    same = (ri // CHUNK) == (ci // CHUNK)
    lower, upper = same & (ci <= ri), same & (ci >= ri)
    b = _dot_f32(jnp.where(lower, 1.0, 0.0).astype(F32), la)
    which = lax.broadcasted_iota(jnp.int32, (GSTEP, 1), 0) // CHUNK
    return gk, la, b, lower, upper, which


def _gla_head(q_ref, k_ref, la, b, which, h):
    sl = slice(h * B_DK, (h + 1) * B_DK)
    bh, lah = b[:, sl], la[:, sl]
    bls = [jnp.sum(lah[_chunk_rows(c)], axis=0, keepdims=True) for c in range(NCH)]
    blast = _per_chunk(which, bls)
    qc = q_ref[:, sl] * (B_DK ** -0.5)
    kh = k_ref[:, sl]
    eb, enb, esb = jnp.exp(bh), jnp.exp(-bh), jnp.exp(blast - bh)
    return qc * eb, kh * enb, kh * esb, eb, enb, esb, [jnp.exp(v) for v in bls]


def _gla_specs(step_of):
    return [
        pl.BlockSpec((GSTEP, 512), lambda i: (step_of(i), C_BQ // 512)),
        pl.BlockSpec((GSTEP, 512), lambda i: (step_of(i), C_BK // 512)),
        pl.BlockSpec((GSTEP, D), lambda i: (step_of(i), C_BV // D)),
        pl.BlockSpec((GSTEP, W_BL), lambda i: (step_of(i), C_BL // W_BL)),
        pl.BlockSpec((W_BL, 512), lambda i: (0, 0)),
        pl.BlockSpec((1, 512), lambda i: (0, 0)),
    ]


def _gla_fwd(proj, gu_pad, bias):
    T = proj.shape[0]
    ns = T // GSTEP

    def body(q_ref, k_ref, v_ref, bl_ref, gu_ref, bias_ref, o_ref, st_ref, state_ref):
        @pl.when(pl.program_id(0) == 0)
        def _():
            state_ref[...] = jnp.zeros_like(state_ref)

        _, la, b, lower, _, which = _gla_gate(bl_ref, gu_ref, bias_ref)

        def within(h):
            q_e, k_e, k_s, _, _, _, decays = _gla_head(q_ref, k_ref, la, b, which, h)
            vh = v_ref[:, h * B_DV:(h + 1) * B_DV].astype(MXU)
            q_eb = q_e.astype(MXU)
            att = jnp.where(lower, _dot_nt(q_eb, k_e.astype(MXU)), 0.0)
            return vh, q_eb, k_s.astype(MXU), _dot(att.astype(MXU), vh), decays

        def across(h, vh, q_eb, k_sb, o_intra, decays):
            rows = slice(h * B_DV, (h + 1) * B_DV)
            s = state_ref[rows, :]
            outs = []
            for c in range(NCH):
                cr = _chunk_rows(c)
                st_ref[c * ST_ROWS + h * B_DV:c * ST_ROWS + (h + 1) * B_DV, :] = s
                outs.append(o_intra[cr] + _dot_nt(q_eb[cr], s.astype(MXU)))
                s = s * decays[c] + _dot_tn(vh[cr], k_sb[cr])
            state_ref[rows, :] = s
            o_ref[:, rows] = jnp.concatenate(outs, axis=0)

        for h in range(B_HEADS):
            across(h, *within(h))

    return pl.pallas_call(
        body, name="gla_fwd", grid=(ns,),
        in_specs=_gla_specs(lambda i: i),
        out_specs=[pl.BlockSpec((GSTEP, D), lambda i: (i, 0)),
                   pl.BlockSpec((NCH * ST_ROWS, B_DK), lambda i: (i, 0))],
        out_shape=[jax.ShapeDtypeStruct((T, D), F32),
                   jax.ShapeDtypeStruct((ns * NCH * ST_ROWS, B_DK), F32)],
        scratch_shapes=[pltpu.VMEM((ST_ROWS, B_DK), F32)],
        compiler_params=_cp(("arbitrary",)),
    )(proj, proj, proj, proj, gu_pad, bias)


def _gla_bwd(proj, gu_pad, bias, states, do_b):
    T = proj.shape[0]
    ns = T // GSTEP
    o_q, o_k = C_BQ - C_GLA, C_BK - C_GLA

    def body(q_ref, k_ref, v_ref, bl_ref, gu_ref, bias_ref, st_ref, do_ref,
             dg_ref, dbl_ref, ggu_ref, gbias_ref, gt_ref):
        @pl.when(pl.program_id(0) == 0)
        def _():
            gt_ref[...] = jnp.zeros_like(gt_ref)
            ggu_ref[...] = jnp.zeros_like(ggu_ref)
            gbias_ref[...] = jnp.zeros_like(gbias_ref)

        gk, la, b, lower, upper_mask, which = _gla_gate(bl_ref, gu_ref, bias_ref)
        upper = jnp.where(upper_mask, 1.0, 0.0).astype(F32)
        dla_parts = []

        def within(h):
            q_e, k_e, k_s, eb, enb, esb, decays = _gla_head(q_ref, k_ref, la, b, which, h)
            vh = v_ref[:, h * B_DV:(h + 1) * B_DV].astype(MXU)
            doh = do_ref[:, h * B_DV:(h + 1) * B_DV].astype(MXU)
            q_eb, k_eb = q_e.astype(MXU), k_e.astype(MXU)
            att = jnp.where(lower, _dot_nt(q_eb, k_eb), 0.0).astype(MXU)
            datt = jnp.where(lower, _dot_nt(doh, vh), 0.0).astype(MXU)
            return (q_e, k_e, k_s, eb, enb, esb, decays, vh, doh, q_eb, k_s.astype(MXU),
                    _dot(datt, k_eb), _dot_tn(datt, q_eb), _dot_tn(att, doh))

        def across(h, q_e, k_e, k_s, eb, enb, esb, decays, vh, doh, q_eb, k_sb, dq_i, dk_e, dv_i):
            rows = slice(h * B_DV, (h + 1) * B_DV)
            g = gt_ref[rows, :]
            dq_c, dks_c, dv_c, ddec = [None] * NCH, [None] * NCH, [None] * NCH, [None] * NCH
            for c in range(NCH - 1, -1, -1):
                cr = _chunk_rows(c)
                s = st_ref[c * ST_ROWS + h * B_DV:c * ST_ROWS + (h + 1) * B_DV, :]
                gb = g.astype(MXU)
                dq_c[c] = dq_i[cr] + _dot(doh[cr], s.astype(MXU))
                dks_c[c] = _dot(vh[cr], gb)
                dv_c[c] = dv_i[cr] + _dot_nt(k_sb[cr], gb)
                ddec[c] = jnp.sum(g * s, axis=0, keepdims=True)
                g = g * decays[c] + _dot_tn(doh[cr], q_eb[cr])
            gt_ref[rows, :] = g
            dq_e = jnp.concatenate(dq_c, axis=0)
            dk_s = jnp.concatenate(dks_c, axis=0)
            dg_ref[:, rows] = jnp.concatenate(dv_c, axis=0).astype(dg_ref.dtype)
            dg_ref[:, o_q + h * B_DK:o_q + (h + 1) * B_DK] = (dq_e * eb * (B_DK ** -0.5)).astype(dg_ref.dtype)
            dg_ref[:, o_k + h * B_DK:o_k + (h + 1) * B_DK] = (dk_e * enb + dk_s * esb).astype(dg_ref.dtype)
            dks_ks = dk_s * k_s
            db = dq_e * q_e - dk_e * k_e - dks_ks
            dbl = [jnp.sum(dks_ks[_chunk_rows(c)], axis=0, keepdims=True) + ddec[c] * decays[c] for c in range(NCH)]
            dla_parts.append(_dot_f32(upper, db) + _per_chunk(which, dbl))

        for h in range(B_HEADS):
            across(h, *within(h))
        dla = jnp.concatenate(dla_parts, axis=1)
        dgk = dla * (1.0 / TAU) * _sigmoid(-gk)
        dgkb = dgk.astype(MXU)
        dbl_ref[...] = _dot_nt(dgkb, gu_ref[...]).astype(dbl_ref.dtype)
        ggu_ref[...] = ggu_ref[...] + _dot_tn(bl_ref[...].astype(MXU), dgkb)
        gbias_ref[...] = gbias_ref[...] + jnp.broadcast_to(jnp.sum(dgk, axis=0, keepdims=True), gbias_ref.shape)

    def rev(i):
        return ns - 1 - i

    return pl.pallas_call(
        body, name="gla_bwd", grid=(ns,),
        in_specs=_gla_specs(rev) + [
            pl.BlockSpec((NCH * ST_ROWS, B_DK), lambda i: (rev(i), 0)),
            pl.BlockSpec((GSTEP, D), lambda i: (rev(i), 0)),
        ],
        out_specs=[
            pl.BlockSpec((GSTEP, W_GLA), lambda i: (rev(i), 0)),
            pl.BlockSpec((GSTEP, W_BL), lambda i: (rev(i), 0)),
            pl.BlockSpec((W_BL, 512), lambda i: (0, 0)),
            pl.BlockSpec((8, 512), lambda i: (0, 0)),
        ],
        out_shape=[
            jax.ShapeDtypeStruct((T, W_GLA), MXU),
            jax.ShapeDtypeStruct((T, W_BL), MXU),
            jax.ShapeDtypeStruct((W_BL, 512), F32),
            jax.ShapeDtypeStruct((8, 512), F32),
        ],
        scratch_shapes=[pltpu.VMEM((B_HEADS * B_DV, B_DK), F32)],
        compiler_params=_cp(("arbitrary",)),
    )(proj, proj, proj, proj, gu_pad, bias, states, do_b)


def _mid(x, target, proj, o_a, o_b, w_a, w_b, w_out, w_bn4, fnw):
    T = x.shape[0]
    tT = min(T, 128)
    nbuf = 4
    o_ag, o_bg, o_ma, o_mb = (c - C_GATES for c in (C_AG, C_BG, C_MA, C_MB))

    def body(x_ref, t_ref, oa_ref, ob_ref, gates_ref, wa_ref, wb_ref, wo_ref, wbn_ref, fnw_ref,
             dx2_ref, doa_ref, dob_ref, dgates_ref,
             tail0_ref, tail1_ref, gfn_ref, gbn_ref, loss_ref, buf_ref, gw_ref):
        i = pl.program_id(0)

        @pl.when(i == 0)
        def _():
            for r in (gw_ref, gfn_ref, gbn_ref, loss_ref):
                r[...] = jnp.zeros_like(r)

        rows = pl.ds(pl.multiple_of((i % nbuf) * tT, tT), tT)

        def keep(k, val):
            buf_ref[k, rows, :] = val

        oa, ag = oa_ref[...], gates_ref[:, o_ag:o_ag + D]
        sg_a = _sigmoid(ag)
        silu_a = ag * sg_a
        oag_b = (oa * silu_a).astype(MXU)
        keep(0, oag_b)
        y_a = _dot(oag_b, wa_ref[...])

        ob, bg = ob_ref[...], gates_ref[:, o_bg:o_bg + D]
        rbs, obhats = [], []
        for h in range(B_HEADS):
            obh = ob[:, h * B_DV:(h + 1) * B_DV]
            rb = lax.rsqrt(jnp.mean(obh * obh, axis=-1, keepdims=True) + EPS)
            rbs.append(rb)
            obhats.append(obh * rb)
        obhat = jnp.concatenate(obhats, axis=1)
        wbn = wbn_ref[...]
        obn = obhat * wbn
        sg_b = _sigmoid(bg)
        silu_b = bg * sg_b
        obg_b = (obn * silu_b).astype(MXU)
        keep(1, obg_b)
        y_b = _dot(obg_b, wb_ref[...])

        sa, sb = _sigmoid(gates_ref[:, o_ma:o_ma + D]), _sigmoid(gates_ref[:, o_mb:o_mb + D])
        mg_b = (sa * y_a + sb * y_b).astype(MXU)
        keep(2, mg_b)
        x2 = x_ref[...] + _dot(mg_b, wo_ref[...])
        r2 = lax.rsqrt(jnp.mean(x2 * x2, axis=-1, keepdims=True) + EPS)
        xh2 = x2 * r2
        fw = fnw_ref[...]
        err = xh2 * fw - t_ref[...]
        tok = jnp.mean(err * err, axis=-1, keepdims=True)
        loss_ref[...] = loss_ref[...] + 0.5 * jnp.sum(tok, axis=0, keepdims=True)

        dy = err * (1.0 / D)
        gfn_ref[...] = gfn_ref[...] + jnp.broadcast_to(jnp.sum(dy * xh2, axis=0, keepdims=True), gfn_ref.shape)
        gy = dy * fw
        dx2 = r2 * (gy - xh2 * jnp.mean(gy * xh2, axis=-1, keepdims=True))
        dx2_ref[...] = dx2
        dx2_b = dx2.astype(MXU)
        keep(5, dx2_b)
        dmg = _dot_nt(dx2_b, wo_ref[...])

        dgates_ref[:, o_ma:o_ma + D] = (dmg * y_a * sa * (1.0 - sa)).astype(dgates_ref.dtype)
        dgates_ref[:, o_mb:o_mb + D] = (dmg * y_b * sb * (1.0 - sb)).astype(dgates_ref.dtype)
        dya_b = (dmg * sa).astype(MXU)
        dyb_b = (dmg * sb).astype(MXU)
        keep(3, dya_b)
        keep(4, dyb_b)
        doag = _dot_nt(dya_b, wa_ref[...])
        dobg = _dot_nt(dyb_b, wb_ref[...])

        @pl.when(i % nbuf == nbuf - 1)
        def _():
            for p in range(3):
                gw_ref[p] = gw_ref[p] + _dot_tn(buf_ref[p], buf_ref[3 + p])

        @pl.when(i == pl.num_programs(0) - 1)
        def _():
            for hf, tail_ref in enumerate((tail0_ref, tail1_ref)):
                for d in range(NDEV):
                    for p in range(3):
                        tail_ref[d, 128 * p:128 * (p + 1), :] = (
                            gw_ref[p, 128 * d:128 * (d + 1), hf * DH:(hf + 1) * DH].astype(tail_ref.dtype))

        doa_ref[...] = doag * silu_a
        dgates_ref[:, o_ag:o_ag + D] = (doag * oa * (sg_a * (1.0 + ag * (1.0 - sg_a)))).astype(dgates_ref.dtype)
        dobn = dobg * silu_b
        dgates_ref[:, o_bg:o_bg + D] = (dobg * obn * (sg_b * (1.0 + bg * (1.0 - sg_b)))).astype(dgates_ref.dtype)
        gg = dobn * wbn
        gbn = jnp.zeros((1, B_DV), F32)
        for h in range(B_HEADS):
            sl = slice(h * B_DV, (h + 1) * B_DV)
            gbn = gbn + jnp.sum(dobn[:, sl] * obhats[h], axis=0, keepdims=True)
            ggh = gg[:, sl]
            dob_ref[:, sl] = rbs[h] * (ggh - obhats[h] * jnp.mean(ggh * obhats[h], axis=-1, keepdims=True))
        gbn_ref[...] = gbn_ref[...] + jnp.broadcast_to(gbn, gbn_ref.shape)

    assert (T // tT) % nbuf == 0
    tile = pl.BlockSpec((tT, D), lambda i: (i, 0))
    row = pl.BlockSpec((1, D), lambda i: (0, 0))
    acc8 = pl.BlockSpec((8, D), lambda i: (0, 0))
    return pl.pallas_call(
        body, name="mid", grid=(T // tT,),
        in_specs=[tile, tile, tile, tile, pl.BlockSpec((tT, W_GATES), lambda i: (i, C_GATES // W_GATES)),
                  _vmem(), _vmem(), _vmem(), row, row],
        out_specs=[tile, tile, tile, pl.BlockSpec((tT, W_GATES), lambda i: (i, 0)), _vmem(), _vmem(),
                   acc8, pl.BlockSpec((8, B_DV), lambda i: (0, 0)), pl.BlockSpec((8, LANE), lambda i: (0, 0))],
        out_shape=[
            jax.ShapeDtypeStruct((T, D), F32),
            jax.ShapeDtypeStruct((T, D), F32),
            jax.ShapeDtypeStruct((T, D), F32),
            jax.ShapeDtypeStruct((T, W_GATES), MXU),
            jax.ShapeDtypeStruct((NDEV, 384, DH), WIRE),
            jax.ShapeDtypeStruct((NDEV, 384, DH), WIRE),
            jax.ShapeDtypeStruct((8, D), F32),
            jax.ShapeDtypeStruct((8, B_DV), F32),
            jax.ShapeDtypeStruct((8, LANE), F32),
        ],
        scratch_shapes=[pltpu.VMEM((6, nbuf * tT, D), MXU), pltpu.VMEM((3, D, D), F32)],
        compiler_params=_cp(("arbitrary",)),
    )(x, target, o_a, o_b, proj, w_a, w_b, w_out, w_bn4, fnw)


DH = D // 2


_GW_TILES = (("q", 0, 512, 0), ("q", 1, 512, 512), ("kv", 0, 256, 1024), ("bl", 0, RANK, 5376),
             ("gla", 0, 512, 3328), ("gla", 1, 512, 3840), ("gla", 2, 512, 2304), ("gla", 3, 512, 2816),
             ("gates", 0, 512, 1280), ("gates", 1, 512, 1792), ("gates", 2, 512, 4352), ("gates", 3, 512, 4864),
             ("gates", 4, 512, 5392), ("gates", 5, 512, 5904), ("gates", 6, 512, 6416), ("gates", 7, 512, 6928))


def _gw_unpermute(piece, t):
    if piece == "q":
        parts = []
        for blk in range(t.shape[0] // LANE):
            g = [t[blk * LANE + 32 * i:blk * LANE + 32 * (i + 1)] for i in range(4)]
            parts += [g[0], g[2], g[1], g[3]]
        return jnp.concatenate(parts, axis=0)
    if piece == "kv":
        k = [t[64 * i:64 * i + 32] + t[64 * i + 32:64 * i + 64] for i in range(4)]
        v = [t[256 + 128 * g:256 + 128 * g + 64] + t[256 + 128 * g + 64:256 + 128 * (g + 1)] for g in range(2)]
        return jnp.concatenate(k + v, axis=0)
    if piece == "bl":
        return t[:RANK]
    return t


def _gw_half(h, pieces, half, after=None):
    T = h.shape[0]
    steps = len(_GW_TILES)

    def body(*refs):
        h_ref = refs[0]
        srcs = dict(zip(("q", "kv", "bl", "gla", "gates"), refs[1:6]))
        o_ref, stage, sems = refs[-3:]
        j = pl.program_id(0)

        def out_copy(k):
            _, _, n, off = _GW_TILES[k]
            return pltpu.make_async_copy(stage.at[k % 2, 0:n], o_ref.at[pl.ds(off, n)], sems.at[k % 2])

        for k, (piece, _, n, _) in enumerate(_GW_TILES):
            @pl.when(j == k)
            def _(k=k, piece=piece, n=n):
                if k >= 2:
                    out_copy(k - 2).wait()
                t = _gw_unpermute(piece, _dot_tn(srcs[piece][...], h_ref[...]))
                stage[k % 2, 0:n, :] = t.astype(stage.dtype)
                out_copy(k).start()

        @pl.when(j == steps - 1)
        def _():
            out_copy(steps - 2).wait()
            out_copy(steps - 1).wait()

    def tile_of(lo, hi):
        return lambda j: (0, jnp.clip(j - lo, 0, hi - lo - 1))

    in_specs = [pl.BlockSpec((T, DH), lambda j: (0, half)),
                pl.BlockSpec((T, 512), tile_of(0, 2)), pl.BlockSpec((T, 512), lambda j: (0, 0)),
                pl.BlockSpec((T, W_BL), lambda j: (0, 0)),
                pl.BlockSpec((T, 512), tile_of(4, 8)), pl.BlockSpec((T, 512), tile_of(8, 16))]
    args = [h, *pieces]
    if after is not None:
        in_specs.append(_any())
        args.append(after)
    return pl.pallas_call(
        body, name=f"gw_in_half{half}", grid=(steps,),
        in_specs=in_specs, out_specs=_any(),
        out_shape=jax.ShapeDtypeStruct((IN_WIDTH, DH), WIRE),
        scratch_shapes=[pltpu.VMEM((2, 512, DH), WIRE), pltpu.SemaphoreType.DMA((2,))],
        compiler_params=_cp(("arbitrary",)),
    )(*args)


def _chip_copies(s_ref, got_ref, send_sems, recv_sems):
    x, y, c = _place()
    chips = [(1 - x, y), (x, 1 - y), (1 - x, 1 - y)]
    return [pltpu.make_async_remote_copy(
        src_ref=s_ref.at[2 * px + py], dst_ref=got_ref.at[j],
        send_sem=send_sems.at[j], recv_sem=recv_sems.at[j], device_id=(px, py, c), device_id_type=MESH)
        for j, (px, py) in enumerate(chips)]


_EFFECT = pltpu.SideEffectType.DATAFLOW_SIDE_EFFECTING


def _hbm():
    return pl.BlockSpec(memory_space=pltpu.HBM)


def _sem():
    return pl.BlockSpec(memory_space=pltpu.SEMAPHORE)


def _chip_start(sums, half):
    land = pltpu.with_memory_space_constraint(lax.empty((3,) + sums.shape[1:], sums.dtype), pltpu.HBM)

    def body(s_ref, land_ref, send_sems, recv_sems, s_thru, land_thru, token):
        for cp in _chip_copies(s_ref, land_ref, send_sems, recv_sems):
            cp.start()
        token[...] = jnp.zeros_like(token)

    return pl.pallas_call(
        body, name=f"chip_start{half}",
        out_shape=(pltpu.SemaphoreType.DMA((3,)), pltpu.SemaphoreType.DMA((3,)),
                   pltpu.HBM(sums.shape, sums.dtype), pltpu.HBM(land.shape, land.dtype),
                   jax.ShapeDtypeStruct((8, LANE), F32)),
        in_specs=(_hbm(), _hbm()), out_specs=(_sem(), _sem(), _hbm(), _hbm(), _vmem()),
        input_output_aliases={0: 2, 1: 3},
        compiler_params=pltpu.CompilerParams(has_side_effects=_EFFECT),
    )(pltpu.with_memory_space_constraint(sums, pltpu.HBM), land)


def _chip_wait(send_sems, recv_sems, s_thru, land_thru, after, half):
    def body(s_ref, land_ref, send_sems, recv_sems, after_ref, s_out, got_ref):
        copies = _chip_copies(s_ref, land_ref, send_sems, recv_sems)
        for cp in copies:
            cp.wait_send()
        for cp in copies:
            cp.wait_recv()

    return pl.pallas_call(
        body, name=f"chip_wait{half}",
        out_shape=(pltpu.HBM(s_thru.shape, s_thru.dtype), pltpu.HBM(land_thru.shape, land_thru.dtype)),
        in_specs=(_hbm(), _hbm(), _sem(), _sem(), _any()), out_specs=(_hbm(), _hbm()),
        input_output_aliases={0: 0, 1: 1},
        compiler_params=pltpu.CompilerParams(has_side_effects=_EFFECT),
    )(s_thru, land_thru, send_sems, recv_sems, after)


def _dh_norm(pieces, offsets, wf, x, dx2, norm_w, after):
    T = x.shape[0]
    tT = min(T, 256)
    widths = [p.shape[1] for p in pieces]
    npc = len(pieces)

    def body(*refs):
        dp_refs = refs[:npc]
        wf_ref, x_ref, dx2_ref, nw_ref, _, gx_ref, gnw_ref = refs[npc:]

        @pl.when(pl.program_id(0) == 0)
        def _():
            gnw_ref[...] = jnp.zeros_like(gnw_ref)

        dh = jnp.zeros((tT, D), F32)
        for dp_ref, off, w in zip(dp_refs, offsets, widths):
            dh = dh + _dot(dp_ref[...], wf_ref[off:off + w, :])
        xv = x_ref[...]
        r = lax.rsqrt(jnp.mean(xv * xv, axis=-1, keepdims=True) + EPS)
        xh = xv * r
        gnw_ref[...] = gnw_ref[...] + jnp.broadcast_to(jnp.sum(dh * xh, axis=0, keepdims=True), gnw_ref.shape)
        g = dh * nw_ref[...]
        gx_ref[...] = r * (g - xh * jnp.mean(g * xh, axis=-1, keepdims=True)) + dx2_ref[...]

    tile = pl.BlockSpec((tT, D), lambda i: (i, 0))
    return pl.pallas_call(
        body, name="dh_norm", grid=(T // tT,),
        in_specs=[pl.BlockSpec((tT, w), lambda i: (i, 0)) for w in widths]
        + [_vmem(), tile, tile, pl.BlockSpec((1, D), lambda i: (0, 0)), _any()],
        out_specs=[tile, pl.BlockSpec((8, D), lambda i: (0, 0))],
        out_shape=[jax.ShapeDtypeStruct((T, D), F32), jax.ShapeDtypeStruct((8, D), F32)],
        compiler_params=_cp(("arbitrary",)),
    )(*pieces, wf, x, dx2, norm_w, after)


def _adamw_math(w, g, m, v):
    m = ADAM_B1 * m + (1.0 - ADAM_B1) * g
    v = ADAM_B2 * v + (1.0 - ADAM_B2) * (g * g)
    m_hat = m * (1.0 / (1.0 - ADAM_B1 ** ADAM_STEP))
    v_hat = v * (1.0 / (1.0 - ADAM_B2 ** ADAM_STEP))
    delta = -ADAM_LR * (m_hat / (jnp.sqrt(v_hat) + ADAM_EPS) + ADAM_WD * w)
    return delta, m, v


def _fetch_partials(s_ref, got_ref, buf, sems):
    x, y, _ = _place()
    cps = [pltpu.make_async_copy(s_ref.at[2 * x + y], buf.at[0], sems.at[0])]
    cps += [pltpu.make_async_copy(got_ref.at[j], buf.at[1 + j], sems.at[1 + j]) for j in range(3)]
    for cp in cps:
        cp.start()
    for cp in cps:
        cp.wait()


SMALL_AT = dict(norm_w=0, fnw=8, bias=16, bn=24, sinks=32, loss=40)
ROW_AT = (R_IN, R_A, R_B, R_O)


def _finish_small(ws, ms, vs, smalls):
    names = ["norm_w", "fnw", "bias", "bn", "sinks"]
    widths = [ws[n].shape[1] for n in names]

    def body(*refs):
        w_refs, m_refs, v_refs = refs[0:5], refs[5:10], refs[10:15]
        smalls_ref, loss_ref = refs[15], refs[16]
        outs, tot = refs[17:37], refs[37]
        acc = smalls_ref[0]
        for d in range(1, NDEV):
            acc = acc + smalls_ref[d]
        tot[...] = acc
        loss_ref[...] = tot[SMALL_AT["loss"]:SMALL_AT["loss"] + 1, 0:1]
        for p, (nm_, wd) in enumerate(zip(names, widths)):
            r = SMALL_AT[nm_]
            g = tot[r:r + 1, 0:wd]
            d, nm, nv = _adamw_math(w_refs[p][...], g, m_refs[p][...], v_refs[p][...])
            for o, val in zip(outs[4 * p:4 * p + 4], (g, d, nm, nv)):
                o[...] = val

    res = pl.pallas_call(
        body, name="finish_small",
        in_specs=[_vmem()] * 16, out_specs=[_vmem()] * 21,
        out_shape=[jax.ShapeDtypeStruct((1, 1), F32)]
        + [jax.ShapeDtypeStruct((1, wd), F32) for wd in widths for _ in range(4)],
        scratch_shapes=[pltpu.VMEM((SMALL_ROWS, D), F32)],
        compiler_params=_cp(),
    )(*[ws[n] for n in names], *[ms[n] for n in names], *[vs[n] for n in names], smalls)
    return res[0], {n: tuple(res[1 + 4 * p:5 + 4 * p]) for p, n in enumerate(names)}


def _finish(w_rows, m_rows, v_rows, gu_w, gu_m, gu_v, sums, got):
    shapes = [(SHARD, 1, D)] + [w.shape for w in w_rows[1:]]

    def columns(ref, p, cols):
        if p:
            return ref, (slice(None), cols)
        flat = ref if ref.shape == (SHARD * LANE_TILES, LANE) else ref.reshape(SHARD * LANE_TILES, LANE)
        return flat, (pl.ds(cols.start // LANE, SHARD, stride=LANE_TILES), slice(None))

    def read(ref, p, cols):
        ref, at = columns(ref, p, cols)
        return ref[at]

    def body(*refs):
        wr_refs, mr_refs, vr_refs = refs[0:4], refs[4:8], refs[8:12]
        guw_ref, gum_ref, guv_ref = refs[12:15]
        s_refs, got_refs = refs[15:17], refs[17:19]
        row_outs = refs[19:35]
        gu_outs = refs[35:39]
        buf, gsh, sems = refs[39:]
        x, y, c = _place()
        me_slot = 4 * x + 2 * y + c
        unshift = lax.rem(SHARD_PAD - 2 * me_slot, SHARD_PAD)

        def total(rows, cols):
            g = buf[0, rows, cols].astype(F32)
            for j in range(1, 4):
                g = g + buf[j, rows, cols].astype(F32)
            return g

        def update(p, g, cols):
            d, nm, nv = _adamw_math(read(wr_refs[p], p, cols), g, read(mr_refs[p], p, cols), read(vr_refs[p], p, cols))
            for o, val in zip(row_outs[4 * p:4 * p + 4], (g, d, nm, nv)):
                o, at = columns(o, p, cols)
                o[at] = val

        for hf in range(2):
            _fetch_partials(s_refs[hf], got_refs[hf], buf, sems)
            for cc in range(DH // LANE):
                src = slice(cc * LANE, (cc + 1) * LANE)
                cols = slice(hf * DH + cc * LANE, hf * DH + (cc + 1) * LANE)
                gsh[...] = pltpu.roll(total(slice(0, SHARD_PAD), src), unshift, 0)
                update(0, gsh[0:SHARD, :], cols)
                for p in range(1, 4):
                    update(p, total(slice(ROW_AT[p], ROW_AT[p] + 128), src), cols)
            if hf == 0:
                g = total(slice(R_GU, R_GU + RANK), slice(0, 64))
                d, nm, nv = _adamw_math(guw_ref[...], g, gum_ref[...], guv_ref[...])
                for o, val in zip(gu_outs, (g, d, nm, nv)):
                    o[...] = val

    res = pl.pallas_call(
        body, name="finish",
        in_specs=[_vmem()] * 15 + [_any()] * 4,
        out_specs=[_vmem()] * 20,
        out_shape=[jax.ShapeDtypeStruct(s, F32) for s in shapes for _ in range(4)]
        + [jax.ShapeDtypeStruct((RANK, 64), F32)] * 4,
        scratch_shapes=[pltpu.VMEM((4, ROWS, DH), sums[0].dtype), pltpu.VMEM((SHARD_PAD, LANE), F32),
                        pltpu.SemaphoreType.DMA((4,))],
        compiler_params=_cp(),
    )(*w_rows, *m_rows, *v_rows, gu_w, gu_m, gu_v, *sums, *got)
    return tuple(res[0:16]), tuple(res[16:20])


def _place():
    x, y, c = lax.axis_index("x"), lax.axis_index("y"), lax.axis_index("c")
    return x, y, c


def _peers(x, y, c):
    return [(x ^ dx, y ^ dy, c ^ dc) for dx in range(2) for dy in range(2) for dc in range(2) if dx + dy + dc]


def _late_gather_start(blk, after, name="late_gather"):
    land = pltpu.with_memory_space_constraint(lax.empty((NDEV,) + blk.shape, blk.dtype), pltpu.HBM)

    def body(b_ref, land_ref, after_ref, send_sems, recv_sems, b_thru, land_thru, token):
        x, y, c = _place()
        for k, to in enumerate(_peers(x, y, c)):
            pltpu.make_async_remote_copy(
                src_ref=b_ref, dst_ref=land_ref.at[4 * x + 2 * y + c], send_sem=send_sems.at[k],
                recv_sem=recv_sems.at[k], device_id=to, device_id_type=MESH).start()
        token[...] = jnp.zeros_like(token)

    return pl.pallas_call(
        body, name=name + "_start",
        out_shape=(pltpu.SemaphoreType.DMA((7,)), pltpu.SemaphoreType.DMA((7,)),
                   pltpu.HBM(blk.shape, blk.dtype), pltpu.HBM(land.shape, land.dtype),
                   jax.ShapeDtypeStruct((8, LANE), F32)),
        in_specs=(_hbm(), _hbm(), _any()), out_specs=(_sem(), _sem(), _hbm(), _hbm(), _vmem()),
        input_output_aliases={0: 2, 1: 3},
        compiler_params=pltpu.CompilerParams(has_side_effects=_EFFECT),
    )(pltpu.with_memory_space_constraint(blk, pltpu.HBM), land, after)


def _late_gather_wait(send_sems, recv_sems, b_thru, land_thru, after, after2, name="late_gather"):
    def body(b_ref, land_ref, send_sems, recv_sems, after_ref, after2_ref, b_out, got_ref):
        x, y, c = _place()
        copies = [pltpu.make_async_remote_copy(
            src_ref=b_ref, dst_ref=land_ref.at[4 * x + 2 * y + c], send_sem=send_sems.at[k],
            recv_sem=recv_sems.at[k], device_id=to, device_id_type=MESH)
            for k, to in enumerate(_peers(x, y, c))]
        for cp in copies:
            cp.wait_send()
        for cp in copies:
            cp.wait_recv()

    return pl.pallas_call(
        body, name=name + "_wait",
        out_shape=(pltpu.HBM(b_thru.shape, b_thru.dtype), pltpu.HBM(land_thru.shape, land_thru.dtype)),
        in_specs=(_hbm(), _hbm(), _sem(), _sem(), _any(), _any()), out_specs=(_hbm(), _hbm()),
        input_output_aliases={0: 0, 1: 1},
        compiler_params=pltpu.CompilerParams(has_side_effects=_EFFECT),
    )(b_thru, land_thru, send_sems, recv_sems, after, after2)


G_ROWS = SHARD_PAD + RANK


def _gather_blocks(w_in_t, gu_s, xs, norm_w, pos_col):
    rows, cols = G_ROWS, D
    T = xs.shape[0]
    tT = min(T, 256)
    inv_row, sign_row = _rope_rows()

    def body(wi_ref, gu_ref, xs_hbm, nw_ref, pos_ref, inv_ref, sign_ref,
             out_ref, h_ref, cos_ref, sin_ref, x_ref, frame_ref, xs_ref, send_sems, recv_sems, local_sem, xs_sem):
        load_xs = pltpu.make_async_copy(xs_hbm, xs_ref, xs_sem)
        load_xs.start()
        x, y, c = _place()
        me, sibling = (x, y, c), (x, y, 1 - c)
        chips = [(1 - x, y), (x, 1 - y), (1 - x, 1 - y)]
        shift = 2 * (4 * x + 2 * y + c)
        frame_ref[SHARD - SHARD % 8:, :] = jnp.zeros((SHARD_PAD - SHARD + SHARD % 8, LANE), F32)
        for cc in range(LANE_TILES):
            cs = slice(cc * LANE, (cc + 1) * LANE)
            frame_ref[:SHARD, :] = wi_ref[pl.ds(cc, SHARD, stride=LANE_TILES), :]
            x_ref[0:SHARD_PAD, cs] = pltpu.roll(frame_ref[...], shift, 0).astype(x_ref.dtype)
        x_ref[SHARD_PAD:G_ROWS, :] = jnp.zeros((RANK, D), x_ref.dtype)
        x_ref[SHARD_PAD:G_ROWS, 0:64] = gu_ref[...].astype(x_ref.dtype)

        def slot(px, py, pc):
            return out_ref.at[4 * px + 2 * py + pc]

        def copy(k, block, to, src=None):
            return pltpu.make_async_remote_copy(
                src_ref=slot(*block) if src is None else src, dst_ref=slot(*block),
                send_sem=send_sems.at[k], recv_sem=recv_sems.at[k], device_id=to, device_id_type=MESH)

        mine = pltpu.make_async_copy(x_ref, slot(*me), local_sem)
        mine.start()
        first = [copy(0, me, sibling, src=x_ref)]
        first += [copy(1 + j, me, (*chip, c), src=x_ref) for j, chip in enumerate(chips)]
        for cp in first:
            cp.start()
        load_xs.wait()

        @pl.loop(0, T // tT)
        def _(i):
            rows_i = pl.ds(pl.multiple_of(i * tT, tT), tT)
            _prologue_rows(rows_i, xs_ref, nw_ref, pos_ref, inv_ref, sign_ref, h_ref, cos_ref, sin_ref)

        passed = [copy(4 + j, (*chip, c), sibling) for j, chip in enumerate(chips)]
        for j, chip in enumerate(chips):
            copy(1 + j, (*chip, c), me).wait_recv()
            passed[j].start()
        copy(0, sibling, me).wait_recv()
        for j, chip in enumerate(chips):
            copy(4 + j, (*chip, 1 - c), me).wait_recv()
        for cp in first + passed:
            cp.wait_send()
        mine.wait()

    return pl.pallas_call(
        body, name="gather_weights",
        in_specs=[_vmem(), _vmem(), _any()] + [_vmem()] * 4, out_specs=[_any()] + [_vmem()] * 3,
        out_shape=[jax.ShapeDtypeStruct((NDEV, rows, cols), WIRE), jax.ShapeDtypeStruct((T, D), MXU),
                   jax.ShapeDtypeStruct((T, LANE), F32), jax.ShapeDtypeStruct((T, LANE), F32)],
        scratch_shapes=[pltpu.VMEM((rows, cols), WIRE), pltpu.VMEM((SHARD_PAD, LANE), F32), pltpu.VMEM((T, D), F32),
                        pltpu.SemaphoreType.DMA((7,)), pltpu.SemaphoreType.DMA((7,)), pltpu.SemaphoreType.DMA,
                        pltpu.SemaphoreType.DMA],
        compiler_params=_cp(),
    )(w_in_t, gu_s, xs, norm_w, pos_col, inv_row, sign_row)


def _pair_reduce(gwt, tails, half):
    n = gwt.shape[1]
    starts = [SHARD_PAD]
    for t in tails:
        starts.append(starts[-1] + t.shape[1])
    rows = starts[-1]
    blk = (4, rows, n)
    npart = 1 + len(tails)

    def body(*refs):
        g_ref, t_refs = refs[0], refs[1:npart]
        out_ref, acc, got, own, send_sems, recv_sems, own_sems, out_sems = refs[npart:]
        x, y, c = _place()

        def parts(d, dst):
            frame = g_ref.at[pl.ds(pl.multiple_of(FRAME * d, 16), SHARD_PAD)]
            return [(frame, dst.at[0:SHARD_PAD])] + [
                (t_ref.at[d], dst.at[starts[k]:starts[k + 1]]) for k, t_ref in enumerate(t_refs)]

        sends, loads, stores = [], [], []
        for chip in range(4):
            sends.append([pltpu.make_async_remote_copy(
                src_ref=s, dst_ref=d_, send_sem=send_sems.at[chip, k], recv_sem=recv_sems.at[chip, k],
                device_id=(x, y, 1 - c), device_id_type=MESH)
                for k, (s, d_) in enumerate(parts(2 * chip + (1 - c), got.at[chip]))])
            loads.append([pltpu.make_async_copy(s, d_, own_sems.at[chip, k])
                          for k, (s, d_) in enumerate(parts(2 * chip + c, own.at[chip]))])
            stores.append(pltpu.make_async_copy(acc.at[chip], out_ref.at[chip], out_sems.at[chip]))
        for group in sends + loads:
            for cp in group:
                cp.start()
        for chip in range(4):
            for cp in loads[chip]:
                cp.wait()
            for cp in sends[chip]:
                cp.wait_recv()
            acc[chip] = (own[chip].astype(F32) + got[chip].astype(F32)).astype(acc.dtype)
            stores[chip].start()
        for cp in stores:
            cp.wait()
        for group in sends:
            for cp in group:
                cp.wait_send()

    return pl.pallas_call(
        body, name=f"pair_reduce{half}",
        in_specs=[_any()] * npart, out_specs=_any(),
        out_shape=jax.ShapeDtypeStruct(blk, gwt.dtype),
        scratch_shapes=[pltpu.VMEM(blk, gwt.dtype), pltpu.VMEM(blk, gwt.dtype), pltpu.VMEM(blk, gwt.dtype),
                        pltpu.SemaphoreType.DMA((4, npart)), pltpu.SemaphoreType.DMA((4, npart)),
                        pltpu.SemaphoreType.DMA((4, npart)), pltpu.SemaphoreType.DMA((4,))],
        compiler_params=_cp(),
    )(gwt, *tails)


def _pad_cols(a, cols):
    return jnp.pad(a, ((0, 0), (0, cols - a.shape[1])))


def _pad_rows(a, rows):
    return jnp.pad(a, ((0, rows - a.shape[0]), (0, 0)))


FRAME = 928


def _wft_plan():
    moves = []
    for blk in range(8):
        for half in range(2):
            for sub in range(2):
                moves.append((C_Q + 128 * blk + 32 * (2 * half + sub), 128 * blk + 32 * (2 * sub + half), 32))
    for idx in range(4):
        for dup in range(2):
            moves.append((C_KD + 64 * idx + 32 * dup, 1024 + 32 * idx, 32))
    for g in range(2):
        for dup in range(2):
            moves.append((C_VD + 128 * g + 64 * dup, 1152 + 64 * g, 64))
    moves += [(C_BL, 5376, RANK), (C_BV, 3328, 1024), (C_BQ, 2304, 512), (C_BK, 2816, 512),
              (C_AG, 1280, 1024), (C_BG, 4352, 1024), (C_MA, 5392, 1024), (C_MB, 6416, 1024)]
    bulk, seams = [], []
    for dst, src, n in moves:
        r = src
        while r < src + n:
            f = min(r // FRAME, NDEV - 1)
            local = r - FRAME * f
            if f > 0 and local < 16:
                assert local == 0
                seams.append((f, dst + r - src))
                step = 16
            else:
                step = min(src + n, FRAME * (f + 1) if f < NDEV - 1 else IN_WIDTH) - r
                bulk.append((f, local, dst + r - src, step))
            r += step
    assert sorted(f for f, _ in seams) == list(range(1, NDEV))
    return bulk, seams, [(C_BL + RANK, C_GLA - C_BL - RANK)]


def _build_wft_copies(frames):
    bulk, seams, zeros = _wft_plan()
    (z0, zn), = zeros

    def body(f_ref, o_ref, edge, sems, esems):
        copies = [pltpu.make_async_copy(f_ref.at[f, pl.ds(l0, n)], o_ref.at[pl.ds(dst, n)], sems.at[i])
                  for i, (f, l0, dst, n) in enumerate(bulk)]
        loads = []
        for i, (f, _) in enumerate(seams):
            loads.append(pltpu.make_async_copy(f_ref.at[f, pl.ds(0, 16)], edge.at[i, 0], esems.at[i, 0]))
            loads.append(pltpu.make_async_copy(f_ref.at[f - 1, pl.ds(FRAME, 16)], edge.at[i, 1], esems.at[i, 1]))
        for cp in copies + loads:
            cp.start()
        o_ref[z0:z0 + zn, :] = jnp.zeros((zn, D), o_ref.dtype)
        for cp in loads:
            cp.wait()
        for i, (_, dst) in enumerate(seams):
            o_ref[dst:dst + 16, :] = edge[i, 0] + edge[i, 1]
        for cp in copies:
            cp.wait()

    return pl.pallas_call(
        body, name="build_wft",
        in_specs=[_any()], out_specs=_vmem(),
        out_shape=jax.ShapeDtypeStruct((NF, D), frames.dtype),
        scratch_shapes=[pltpu.VMEM((len(seams), 2, 16, D), frames.dtype),
                        pltpu.SemaphoreType.DMA((len(bulk),)), pltpu.SemaphoreType.DMA((len(seams), 2))],
        compiler_params=_cp(),
    )(frames)


def kernel(x, positions, norm_w, w_in, a_sinks, b_gate_up, b_gate_bias, b_out_norm_w, w_a_proj, w_b_proj, w_out, final_norm_w, loss_target, m_norm_w, m_w_in, m_a_sinks, m_b_gate_up, m_b_gate_bias, m_b_out_norm_w, m_w_a_proj, m_w_b_proj, m_w_out, m_final_norm_w, v_norm_w, v_w_in, v_a_sinks, v_b_gate_up, v_b_gate_bias, v_b_out_norm_w, v_w_a_proj, v_w_b_proj, v_w_out, v_final_norm_w):
    T = x.shape[1]
    xs, target = x[0], loss_target[0]
    fnw = final_norm_w.reshape(1, D)
    me = 4 * lax.axis_index("x") + 2 * lax.axis_index("y") + lax.axis_index("c")
    allw, h, cos, sin = _gather_blocks(_by_lane_tile(w_in), b_gate_up[0], xs, norm_w, positions.reshape(T, 1))
    late_blk = jnp.concatenate([w_a_proj[0], w_b_proj[0], w_out[0]], axis=0).astype(WIRE)
    l_send, l_recv, l_blk, l_land, l_started = _late_gather_start(late_blk, cos)
    wf = _build_wft_copies(allw)
    gu = allw[:, SHARD_PAD:G_ROWS, :64].transpose(1, 0, 2).reshape(RANK, 512)
    gu_pad = _pad_rows(gu, W_BL)

    proj = _proj(h, wf, l_started)
    o_a, lse = _swa_fwd(proj, cos, sin, a_sinks)
    o_b, states = _gla_fwd(proj, gu_pad, b_gate_bias)
    l_blk, l_land = _late_gather_wait(l_send, l_recv, l_blk, l_land, states, lse)
    late = lax.dynamic_update_slice(l_land, l_blk[None], (me, 0, 0))
    w_a, w_b, w_o = (late[:, 128 * i:128 * (i + 1), :].reshape(D, D) for i in range(3))
    (dx2, do_a, do_b, d_gates, g_late0, g_late1, g_fn, g_bn, loss_part) = _mid(
        xs, target, proj, o_a, o_b, w_a, w_b, w_o, jnp.tile(b_out_norm_w, (1, B_HEADS)), fnw)
    d_q, d_kv, g_sinks = _swa_bwd(proj, cos, sin, a_sinks, do_a, o_a, lse, cos)
    d_gla, d_bl, g_gu, g_bias = _gla_bwd(proj, gu_pad, b_gate_bias, states, do_b)
    pieces = [d_q, d_kv, d_bl, d_gla, d_gates]
    offsets = [C_Q, C_KD, C_BL, C_GLA, C_GATES]

    ggu = g_gu[:RANK].reshape(RANK, NDEV, 64).transpose(1, 0, 2)
    ggu_half = [jnp.pad(ggu, ((0, 0), (0, 0), (0, DH - 64))).astype(WIRE), jnp.zeros((NDEV, RANK, DH), WIRE)]
    tails = [[g_late0, ggu_half[0]], [g_late1, ggu_half[1]]]

    send0, recv0, s_thru0, land0, started0 = _chip_start(_pair_reduce(_gw_half(h, pieces, 0), tails[0], 0), 0)
    send1, recv1, s_thru1, land1, started1 = _chip_start(
        _pair_reduce(_gw_half(h, pieces, 1, after=started0), tails[1], 1), 1)
    grad_x, g_nw = _dh_norm(pieces, offsets, wf, xs, dx2, norm_w, started1)
    small = jnp.concatenate([g_nw, g_fn, _pad_cols(g_bias, D), _pad_cols(g_bn, D), _pad_cols(g_sinks, D),
                             _pad_cols(loss_part, D)], axis=0)
    sm_send, sm_recv, sm_blk, sm_land, sm_started = _late_gather_start(small, g_nw, name="small_gather")
    sums0, got0 = _chip_wait(send0, recv0, s_thru0, land0, sm_started, 0)
    sums1, got1 = _chip_wait(send1, recv1, s_thru1, land1, got0, 1)
    sums, from_chips = [sums0, sums1], [got0, got1]

    ws = dict(norm_w=norm_w, fnw=fnw, bias=b_gate_bias, bn=b_out_norm_w, sinks=a_sinks)
    ms = dict(norm_w=m_norm_w, fnw=m_final_norm_w.reshape(1, D), bias=m_b_gate_bias, bn=m_b_out_norm_w,
              sinks=m_a_sinks)
    vs = dict(norm_w=v_norm_w, fnw=v_final_norm_w.reshape(1, D), bias=v_b_gate_bias, bn=v_b_out_norm_w,
              sinks=v_a_sinks)
    t_rows, t_gu = _finish(
        [_by_lane_tile(w_in), w_a_proj[0], w_b_proj[0], w_out[0]],
        [_by_lane_tile(m_w_in), m_w_a_proj[0], m_w_b_proj[0], m_w_out[0]],
        [_by_lane_tile(v_w_in), v_w_a_proj[0], v_w_b_proj[0], v_w_out[0]],
        b_gate_up[0], m_b_gate_up[0], v_b_gate_up[0], sums, from_chips)
    sm_blk, sm_land = _late_gather_wait(sm_send, sm_recv, sm_blk, sm_land, t_rows[0], t_gu[0], name="small_gather")
    loss, sm = _finish_small(ws, ms, vs, lax.dynamic_update_slice(sm_land, sm_blk[None], (me, 0, 0)))

    def outputs(k):
        return [sm["norm_w"][k], jnp.transpose(t_rows[k], (1, 2, 0)), sm["sinks"][k], t_gu[k][None], sm["bias"][k], sm["bn"][k],
                t_rows[4 + k][None], t_rows[8 + k][None], t_rows[12 + k][None], sm["fnw"][k].reshape(D)]

    return (loss[0, 0], grad_x[None], *outputs(0), *outputs(1), *outputs(2), *outputs(3))
```

```python
import functools

import numpy as np
import jax
import jax.numpy as jnp
from jax import lax
from jax.experimental import pallas as pl
from jax.experimental.pallas import tpu as pltpu

F32 = jnp.float32
MXU = jnp.bfloat16
WIRE = jnp.bfloat16

D = 1024
A_HEADS, A_KV, A_HD = 16, 2, 64
BLK = 128
B_HEADS, B_DK, B_DV = 4, 128, 256
RANK, TAU, CHUNK = 16, 16.0, 64
EPS, NEG = 1e-5, -1e30
ROPE_THETA = 10000.0
IN_WIDTH, NDEV = 7440, 8
SHARD = IN_WIDTH // NDEV
LANE = 128
LANE_TILES = D // LANE


def _by_lane_tile(a):
    return jnp.transpose(a, (2, 0, 1)).reshape(SHARD * LANE_TILES, LANE)


C_Q, C_KD, C_VD, C_BL = 0, 1024, 1280, 1536
C_BV, C_BQ, C_BK = 2048, 3072, 3584
C_AG, C_BG, C_MA, C_MB = 4096, 5120, 6144, 7168
C_GLA, W_GLA, C_GATES, W_GATES = 2048, 2048, 4096, 4096
NF = 8192
W_BL = 128

SHARD_PAD = 944
R_IN, R_A, R_B, R_O, R_GU, ROWS = 0, 944, 1072, 1200, 1328, 1344
SMALL_ROWS = 48

ADAM_LR, ADAM_B1, ADAM_B2, ADAM_EPS, ADAM_WD, ADAM_STEP = 0.001, 0.9, 0.999, 1e-08, 0.01, 10

MESH = pl.DeviceIdType.MESH
VMEM_LIMIT = 56 * 1024 * 1024


def _cp(sem=None, **kw):
    if sem is not None:
        kw["dimension_semantics"] = sem
    return pltpu.CompilerParams(vmem_limit_bytes=VMEM_LIMIT, **kw)


def _dot(a, b):
    return jnp.dot(a, b, preferred_element_type=F32)


def _dot_nt(a, b):
    return lax.dot_general(a, b, (((1,), (1,)), ((), ())), preferred_element_type=F32)


def _dot_tn(a, b):
    return lax.dot_general(a, b, (((0,), (0,)), ((), ())), preferred_element_type=F32)


def _dot_f32(a, b):
    return jnp.dot(a, b, preferred_element_type=F32, precision=lax.Precision.HIGHEST)


def _sigmoid(z):
    return 0.5 * jnp.tanh(0.5 * z) + 0.5


def _rope(xp, cos, sin):
    return xp * cos + pltpu.roll(xp, 64, 1) * sin


def _rope_bwd(dy, cos, sin):
    return dy * cos - pltpu.roll(dy, 64, 1) * sin


def _vmem():
    return pl.BlockSpec(memory_space=pltpu.VMEM)


def _any():
    return pl.BlockSpec(memory_space=pl.ANY)


def _rope_rows():
    half = A_HD // 2
    inv = (np.float32(ROPE_THETA) ** (-np.arange(half, dtype=np.float32) / np.float32(half))).astype(np.float32)
    inv_row = jnp.asarray(np.tile(inv, 4)[None, :])
    sign_row = jnp.asarray(np.concatenate([-np.ones(64, np.float32), np.ones(64, np.float32)])[None, :])
    return inv_row, sign_row


def _prologue_rows(rows, x_ref, nw_ref, pos_ref, inv_ref, sign_ref, h_ref, cos_ref, sin_ref):
    xv = x_ref[rows, :]
    r = lax.rsqrt(jnp.mean(xv * xv, axis=-1, keepdims=True) + EPS)
    h_ref[rows, :] = ((xv * r) * nw_ref[...]).astype(h_ref.dtype)
    ang = pos_ref[rows, :].astype(F32) * inv_ref[...]
    cos_ref[rows, :] = jnp.cos(ang)
    sin_ref[rows, :] = jnp.sin(ang) * sign_ref[...]


def _proj(h, wft, after):
    T = h.shape[0]
    tT, tN = T, 512

    def body(h_ref, w_ref, after_ref, o_ref):
        o_ref[...] = _dot_nt(h_ref[...], w_ref[...])

    return pl.pallas_call(
        body, name="proj", grid=(T // tT, NF // tN),
        in_specs=[pl.BlockSpec((tT, D), lambda i, j: (i, 0)), pl.BlockSpec((tN, D), lambda i, j: (j, 0)), _any()],
        out_specs=pl.BlockSpec((tT, tN), lambda i, j: (i, j)),
        out_shape=jax.ShapeDtypeStruct((T, NF), F32),
        compiler_params=_cp(("parallel", "parallel")),
    )(h, wft, after)


def _swa_masks():
    lane = lax.broadcasted_iota(jnp.int32, (BLK, LANE), 1)
    rope_sub0 = ((lane // 32) % 2) == 0
    std_sub0 = lane < 64
    return lane, rope_sub0, std_sub0


def _swa_tri():
    qi = lax.broadcasted_iota(jnp.int32, (BLK, BLK), 0)
    kj = lax.broadcasted_iota(jnp.int32, (BLK, BLK), 1)
    return kj <= qi


def _swa_fold(full, tri):
    return jnp.where(tri, full[:, BLK:], full[:, :BLK])


def _swa_unfold(sq, tri):
    return jnp.concatenate([jnp.where(tri, 0.0, sq), jnp.where(tri, sq, 0.0)], axis=1)


def _swa_keys(kc_ref, kp_ref, vc_ref, vp_ref, cq, sq, cp, sp):
    def ropek(kref, c, s):
        kv = kref[...]
        return jnp.concatenate([_rope(kv[:, :LANE], c, s), _rope(kv[:, LANE:], c, s)], axis=1)

    K = jnp.concatenate([ropek(kp_ref, cp, sp), ropek(kc_ref, cq, sq)], axis=0).astype(MXU)
    V = jnp.concatenate([vp_ref[...], vc_ref[...]], axis=0).astype(MXU)
    return K, V


def _swa_in_specs(nb, last):
    def cur(n):
        return jnp.minimum(n, last)

    def prev(n):
        return jnp.maximum(cur(n) - 1, 0)

    kd, vd = C_KD // 256, C_VD // 256
    return [
        pl.BlockSpec((BLK, D), lambda n: (cur(n), C_Q // D)),
        pl.BlockSpec((BLK, 256), lambda n: (cur(n), kd)),
        pl.BlockSpec((BLK, 256), lambda n: (prev(n), kd)),
        pl.BlockSpec((BLK, 256), lambda n: (cur(n), vd)),
        pl.BlockSpec((BLK, 256), lambda n: (prev(n), vd)),
        pl.BlockSpec((BLK, LANE), lambda n: (cur(n), 0)),
        pl.BlockSpec((BLK, LANE), lambda n: (cur(n), 0)),
        pl.BlockSpec((BLK, LANE), lambda n: (prev(n), 0)),
        pl.BlockSpec((BLK, LANE), lambda n: (prev(n), 0)),
    ]


def _swa_fwd(proj, cos, sin, sinks):
    T = proj.shape[0]
    nb = T // BLK
    scale = A_HD ** -0.5

    def body(sinks_ref, q_ref, kc_ref, kp_ref, vc_ref, vp_ref, cq_ref, sq_ref, cp_ref, sp_ref, o_ref, l_ref):
        n = pl.program_id(0)
        cq, sq = cq_ref[...], sq_ref[...]
        K, V = _swa_keys(kc_ref, kp_ref, vc_ref, vp_ref, cq, sq, cp_ref[...], sp_ref[...])
        tri = _swa_tri()
        valid = tri | (n > 0)
        lane, rope_sub0, std_sub0 = _swa_masks()
        group = A_HEADS // A_KV
        roped, lses = {}, []

        def products(head):
            pb, sub, g = head // 2, head % 2, head // group
            if sub == 0:
                roped[pb] = _rope(q_ref[:, pb * LANE:(pb + 1) * LANE], cq, sq)
            qm = jnp.where(rope_sub0 if sub == 0 else ~rope_sub0, roped[pb], 0.0).astype(MXU)
            return _dot_nt(qm, K[:, g * LANE:(g + 1) * LANE])

        def softmax(head, s_full):
            s = jnp.where(valid, _swa_fold(s_full, tri) * scale, NEG)
            sink = sinks_ref[0, head]
            m = jnp.maximum(jnp.max(s, axis=1, keepdims=True), sink)
            e = jnp.exp(s - m)
            den = jnp.sum(e, axis=1, keepdims=True) + jnp.exp(sink - m)
            lses.append(m + jnp.log(den))
            return _swa_unfold(e / den, tri).astype(MXU)

        outs = {}
        st1 = {0: products(0), 1: products(1)}
        st2 = {0: softmax(0, st1.pop(0))}
        for head in range(A_HEADS):
            if head + 2 < A_HEADS:
                st1[head + 2] = products(head + 2)
            if head + 1 < A_HEADS:
                st2[head + 1] = softmax(head + 1, st1.pop(head + 1))
            g = head // group
            outs[head] = _dot(st2.pop(head), V[:, g * LANE:(g + 1) * LANE])
            if head % 2 == 1:
                pb = head // 2
                o_ref[:, pb * LANE:(pb + 1) * LANE] = jnp.where(std_sub0, outs[head - 1], outs[head])
        lacc = jnp.zeros((BLK, LANE), F32)
        for head in range(A_HEADS):
            lacc = jnp.where(lane == head, lses[head], lacc)
        l_ref[...] = lacc

    return pl.pallas_call(
        body, name="swa_fwd", grid=(nb,),
        in_specs=[pl.BlockSpec(memory_space=pltpu.SMEM)] + _swa_in_specs(nb, nb - 1),
        out_specs=[pl.BlockSpec((BLK, D), lambda n: (n, 0)), pl.BlockSpec((BLK, LANE), lambda n: (n, 0))],
        out_shape=[jax.ShapeDtypeStruct((T, D), F32), jax.ShapeDtypeStruct((T, LANE), F32)],
        compiler_params=_cp(("parallel",)),
    )(sinks, proj, proj, proj, proj, proj, cos, sin, cos, sin)


def _swa_bwd(proj, cos, sin, sinks, do_a, o_a, lse, after):
    T = proj.shape[0]
    nb = T // BLK
    scale = A_HD ** -0.5

    def body(sinks_ref, q_ref, kc_ref, kp_ref, vc_ref, vp_ref, cq_ref, sq_ref, cp_ref, sp_ref,
             do_ref, o_ref, l_ref, after_ref, dq_ref, dkv_ref, ds_ref, ckv_ref):
        n = pl.program_id(0)

        @pl.when(n == 0)
        def _():
            ckv_ref[...] = jnp.zeros_like(ckv_ref)
            ds_ref[...] = jnp.zeros_like(ds_ref)

        @pl.when(n < nb)
        def _():
            cq, sq, cp, sp = cq_ref[...], sq_ref[...], cp_ref[...], sp_ref[...]
            K, V = _swa_keys(kc_ref, kp_ref, vc_ref, vp_ref, cq, sq, cp, sp)
            tri = _swa_tri()
            valid = tri | (n > 0)
            lane, rope_sub0, std_sub0 = _swa_masks()
            lane_row = lax.broadcasted_iota(jnp.int32, (1, LANE), 1)
            lse_v = l_ref[...]
            dKt = [jnp.zeros((LANE, 2 * BLK), F32) for _ in range(A_KV)]
            dVt = [jnp.zeros((LANE, 2 * BLK), F32) for _ in range(A_KV)]
            dsinks, roped, roped_t, do_t = [], {}, {}, {}
            group = A_HEADS // A_KV
            dim = lax.broadcasted_iota(jnp.int32, (LANE, BLK), 0)
            rope_row0, std_row0 = ((dim // 32) % 2) == 0, dim < 64

            def products(head):
                pb, sub, g = head // 2, head % 2, head // group
                cols = slice(pb * LANE, (pb + 1) * LANE)
                Kg, Vg = K[:, g * LANE:(g + 1) * LANE], V[:, g * LANE:(g + 1) * LANE]
                if sub == 0:
                    roped[pb] = _rope(q_ref[:, cols], cq, sq)
                    roped_t[pb] = roped[pb].T
                    do_t[pb] = do_ref[:, cols].T
                qm = jnp.where(rope_sub0 if sub == 0 else ~rope_sub0, roped[pb], 0.0).astype(MXU)
                qmt = jnp.where(rope_row0 if sub == 0 else ~rope_row0, roped_t[pb], 0.0).astype(MXU)
                dov = jnp.where(std_sub0 if sub == 0 else ~std_sub0, do_ref[:, cols], 0.0)
                dovt = jnp.where(std_row0 if sub == 0 else ~std_row0, do_t[pb], 0.0).astype(MXU)
                delta = jnp.sum(dov * o_ref[:, cols], axis=1, keepdims=True)
                return qmt, dovt, delta, _dot_nt(qm, Kg), _dot_nt(dov.astype(MXU), Vg)

            def scores(head, qmt, dovt, delta, s_full, dp_full):
                lh = jnp.sum(jnp.where(lane == head, lse_v, 0.0), axis=1, keepdims=True)
                p = jnp.where(valid, jnp.exp(_swa_fold(s_full, tri) * scale - lh), 0.0)
                psink = jnp.exp(sinks_ref[0, head] - lh)
                dsinks.append(jnp.sum(-psink * delta, axis=0, keepdims=True))
                dsq = (p * (_swa_fold(dp_full, tri) - delta)) * scale
                return qmt, dovt, _swa_unfold(p, tri).astype(MXU), _swa_unfold(dsq, tri).astype(MXU)

            def grads(head, qmt, dovt, pb16, dsc):
                g = head // group
                dKt[g] = dKt[g] + _dot(qmt, dsc)
                dVt[g] = dVt[g] + _dot(dovt, pb16)
                return _dot(dsc, K[:, g * LANE:(g + 1) * LANE])

            dqs = {}
            st1 = {0: products(0), 1: products(1)}
            st2 = {0: scores(0, *st1.pop(0))}
            for head in range(A_HEADS):
                if head + 2 < A_HEADS:
                    st1[head + 2] = products(head + 2)
                if head + 1 < A_HEADS:
                    st2[head + 1] = scores(head + 1, *st1.pop(head + 1))
                dqs[head] = grads(head, *st2.pop(head))
                if head % 2 == 1:
                    pb = head // 2
                    dqp = jnp.where(rope_sub0, dqs[head - 1], dqs[head])
                    dq_ref[:, pb * LANE:(pb + 1) * LANE] = _rope_bwd(dqp, cq, sq).astype(dq_ref.dtype)
            dsink = jnp.zeros((1, LANE), F32)
            for head in range(A_HEADS):
                dsink = jnp.where(lane_row == head, dsinks[head], dsink)
            dK, dV = [a.T for a in dKt], [a.T for a in dVt]
            prev = ([_rope_bwd(dK[g][:BLK], cp, sp) for g in range(A_KV)] + [dV[g][:BLK] for g in range(A_KV)])
            cur_ = ([_rope_bwd(dK[g][BLK:], cq, sq) for g in range(A_KV)] + [dV[g][BLK:] for g in range(A_KV)])
            dkv_ref[...] = (ckv_ref[...] + jnp.concatenate(prev, axis=1)).astype(dkv_ref.dtype)
            ckv_ref[...] = jnp.concatenate(cur_, axis=1)
            ds_ref[...] = ds_ref[...] + jnp.broadcast_to(dsink, ds_ref.shape)

        @pl.when(n == nb)
        def _():
            dkv_ref[...] = ckv_ref[...].astype(dkv_ref.dtype)

    last = nb - 1

    def cur(n):
        return jnp.minimum(n, last)

    def out_kv(n):
        return (jnp.maximum(n - 1, 0), 0)

    return pl.pallas_call(
        body, name="swa_bwd", grid=(nb + 1,),
        in_specs=[pl.BlockSpec(memory_space=pltpu.SMEM)] + _swa_in_specs(nb, last) + [
            pl.BlockSpec((BLK, D), lambda n: (cur(n), 0)),
            pl.BlockSpec((BLK, D), lambda n: (cur(n), 0)),
            pl.BlockSpec((BLK, LANE), lambda n: (cur(n), 0)),
            _any(),
        ],
        out_specs=[
            pl.BlockSpec((BLK, D), lambda n: (cur(n), 0)),
            pl.BlockSpec((BLK, 512), out_kv),
            pl.BlockSpec((8, LANE), lambda n: (0, 0)),
        ],
        out_shape=[
            jax.ShapeDtypeStruct((T, D), MXU),
            jax.ShapeDtypeStruct((T, 512), MXU),
            jax.ShapeDtypeStruct((8, LANE), F32),
        ],
        scratch_shapes=[pltpu.VMEM((BLK, 512), F32)],
        compiler_params=_cp(("arbitrary",)),
    )(sinks, proj, proj, proj, proj, proj, cos, sin, cos, sin, do_a, o_a, lse, after)


NCH = 4
GSTEP = NCH * CHUNK
ST_ROWS = B_HEADS * B_DV


def _chunk_rows(c):
    return slice(c * CHUNK, (c + 1) * CHUNK)


def _per_chunk(which, vals):
    out = vals[-1]
    for c in range(NCH - 2, -1, -1):
        out = jnp.where(which == c, vals[c], out)
    return out


def _gla_gate(bl_ref, gu_ref, bias_ref):
    gk = _dot(bl_ref[...].astype(MXU), gu_ref[...]) + bias_ref[...]
    la = (jnp.minimum(gk, 0.0) - jnp.log(1.0 + jnp.exp(-jnp.abs(gk)))) / TAU
    ri = lax.broadcasted_iota(jnp.int32, (GSTEP, GSTEP), 0)
    ci = lax.broadcasted_iota(jnp.int32, (GSTEP, GSTEP), 1)
    same = (ri // CHUNK) == (ci // CHUNK)
    lower, upper = same & (ci <= ri), same & (ci >= ri)
    b = _dot_f32(jnp.where(lower, 1.0, 0.0).astype(F32), la)
    which = lax.broadcasted_iota(jnp.int32, (GSTEP, 1), 0) // CHUNK
    return gk, la, b, lower, upper, which


def _gla_head(q_ref, k_ref, la, b, which, h):
    sl = slice(h * B_DK, (h + 1) * B_DK)
    bh, lah = b[:, sl], la[:, sl]
    bls = [jnp.sum(lah[_chunk_rows(c)], axis=0, keepdims=True) for c in range(NCH)]
    blast = _per_chunk(which, bls)
    qc = q_ref[:, sl] * (B_DK ** -0.5)
    kh = k_ref[:, sl]
    eb, enb, esb = jnp.exp(bh), jnp.exp(-bh), jnp.exp(blast - bh)
    return qc * eb, kh * enb, kh * esb, eb, enb, esb, [jnp.exp(v) for v in bls]


def _gla_specs(step_of):
    return [
        pl.BlockSpec((GSTEP, 512), lambda i: (step_of(i), C_BQ // 512)),
        pl.BlockSpec((GSTEP, 512), lambda i: (step_of(i), C_BK // 512)),
        pl.BlockSpec((GSTEP, D), lambda i: (step_of(i), C_BV // D)),
        pl.BlockSpec((GSTEP, W_BL), lambda i: (step_of(i), C_BL // W_BL)),
        pl.BlockSpec((W_BL, 512), lambda i: (0, 0)),
        pl.BlockSpec((1, 512), lambda i: (0, 0)),
    ]


def _gla_fwd(proj, gu_pad, bias):
    T = proj.shape[0]
    ns = T // GSTEP

    def body(q_ref, k_ref, v_ref, bl_ref, gu_ref, bias_ref, o_ref, st_ref, state_ref):
        @pl.when(pl.program_id(0) == 0)
        def _():
            state_ref[...] = jnp.zeros_like(state_ref)

        _, la, b, lower, _, which = _gla_gate(bl_ref, gu_ref, bias_ref)

        def within(h):
            q_e, k_e, k_s, _, _, _, decays = _gla_head(q_ref, k_ref, la, b, which, h)
            vh = v_ref[:, h * B_DV:(h + 1) * B_DV].astype(MXU)
            q_eb = q_e.astype(MXU)
            att = jnp.where(lower, _dot_nt(q_eb, k_e.astype(MXU)), 0.0)
            return vh, q_eb, k_s.astype(MXU), _dot(att.astype(MXU), vh), decays

        def across(h, vh, q_eb, k_sb, o_intra, decays):
            rows = slice(h * B_DV, (h + 1) * B_DV)
            s = state_ref[rows, :]
            outs = []
            for c in range(NCH):
                cr = _chunk_rows(c)
                st_ref[c * ST_ROWS + h * B_DV:c * ST_ROWS + (h + 1) * B_DV, :] = s
                outs.append(o_intra[cr] + _dot_nt(q_eb[cr], s.astype(MXU)))
                s = s * decays[c] + _dot_tn(vh[cr], k_sb[cr])
            state_ref[rows, :] = s
            o_ref[:, rows] = jnp.concatenate(outs, axis=0)

        for h in range(B_HEADS):
            across(h, *within(h))

    return pl.pallas_call(
        body, name="gla_fwd", grid=(ns,),
        in_specs=_gla_specs(lambda i: i),
        out_specs=[pl.BlockSpec((GSTEP, D), lambda i: (i, 0)),
                   pl.BlockSpec((NCH * ST_ROWS, B_DK), lambda i: (i, 0))],
        out_shape=[jax.ShapeDtypeStruct((T, D), F32),
                   jax.ShapeDtypeStruct((ns * NCH * ST_ROWS, B_DK), F32)],
        scratch_shapes=[pltpu.VMEM((ST_ROWS, B_DK), F32)],
        compiler_params=_cp(("arbitrary",)),
    )(proj, proj, proj, proj, gu_pad, bias)


def _gla_bwd(proj, gu_pad, bias, states, do_b):
    T = proj.shape[0]
    ns = T // GSTEP
    o_q, o_k = C_BQ - C_GLA, C_BK - C_GLA

    def body(q_ref, k_ref, v_ref, bl_ref, gu_ref, bias_ref, st_ref, do_ref,
             dg_ref, dbl_ref, ggu_ref, gbias_ref, gt_ref):
        @pl.when(pl.program_id(0) == 0)
        def _():
            gt_ref[...] = jnp.zeros_like(gt_ref)
            ggu_ref[...] = jnp.zeros_like(ggu_ref)
            gbias_ref[...] = jnp.zeros_like(gbias_ref)

        gk, la, b, lower, upper_mask, which = _gla_gate(bl_ref, gu_ref, bias_ref)
        upper = jnp.where(upper_mask, 1.0, 0.0).astype(F32)
        dla_parts = []

        def within(h):
            q_e, k_e, k_s, eb, enb, esb, decays = _gla_head(q_ref, k_ref, la, b, which, h)
            vh = v_ref[:, h * B_DV:(h + 1) * B_DV].astype(MXU)
            doh = do_ref[:, h * B_DV:(h + 1) * B_DV].astype(MXU)
            q_eb, k_eb = q_e.astype(MXU), k_e.astype(MXU)
            att = jnp.where(lower, _dot_nt(q_eb, k_eb), 0.0).astype(MXU)
            datt = jnp.where(lower, _dot_nt(doh, vh), 0.0).astype(MXU)
            return (q_e, k_e, k_s, eb, enb, esb, decays, vh, doh, q_eb, k_s.astype(MXU),
                    _dot(datt, k_eb), _dot_tn(datt, q_eb), _dot_tn(att, doh))

        def across(h, q_e, k_e, k_s, eb, enb, esb, decays, vh, doh, q_eb, k_sb, dq_i, dk_e, dv_i):
            rows = slice(h * B_DV, (h + 1) * B_DV)
            g = gt_ref[rows, :]
            dq_c, dks_c, dv_c, ddec = [None] * NCH, [None] * NCH, [None] * NCH, [None] * NCH
            for c in range(NCH - 1, -1, -1):
                cr = _chunk_rows(c)
                s = st_ref[c * ST_ROWS + h * B_DV:c * ST_ROWS + (h + 1) * B_DV, :]
                gb = g.astype(MXU)
                dq_c[c] = dq_i[cr] + _dot(doh[cr], s.astype(MXU))
                dks_c[c] = _dot(vh[cr], gb)
                dv_c[c] = dv_i[cr] + _dot_nt(k_sb[cr], gb)
                ddec[c] = jnp.sum(g * s, axis=0, keepdims=True)
                g = g * decays[c] + _dot_tn(doh[cr], q_eb[cr])
            gt_ref[rows, :] = g
            dq_e = jnp.concatenate(dq_c, axis=0)
            dk_s = jnp.concatenate(dks_c, axis=0)
            dg_ref[:, rows] = jnp.concatenate(dv_c, axis=0).astype(dg_ref.dtype)
            dg_ref[:, o_q + h * B_DK:o_q + (h + 1) * B_DK] = (dq_e * eb * (B_DK ** -0.5)).astype(dg_ref.dtype)
            dg_ref[:, o_k + h * B_DK:o_k + (h + 1) * B_DK] = (dk_e * enb + dk_s * esb).astype(dg_ref.dtype)
            dks_ks = dk_s * k_s
            db = dq_e * q_e - dk_e * k_e - dks_ks
            dbl = [jnp.sum(dks_ks[_chunk_rows(c)], axis=0, keepdims=True) + ddec[c] * decays[c] for c in range(NCH)]
            dla_parts.append(_dot_f32(upper, db) + _per_chunk(which, dbl))

        for h in range(B_HEADS):
            across(h, *within(h))
        dla = jnp.concatenate(dla_parts, axis=1)
        dgk = dla * (1.0 / TAU) * _sigmoid(-gk)
        dgkb = dgk.astype(MXU)
        dbl_ref[...] = _dot_nt(dgkb, gu_ref[...]).astype(dbl_ref.dtype)
        ggu_ref[...] = ggu_ref[...] + _dot_tn(bl_ref[...].astype(MXU), dgkb)
        gbias_ref[...] = gbias_ref[...] + jnp.broadcast_to(jnp.sum(dgk, axis=0, keepdims=True), gbias_ref.shape)

    def rev(i):
        return ns - 1 - i

    return pl.pallas_call(
        body, name="gla_bwd", grid=(ns,),
        in_specs=_gla_specs(rev) + [
            pl.BlockSpec((NCH * ST_ROWS, B_DK), lambda i: (rev(i), 0)),
            pl.BlockSpec((GSTEP, D), lambda i: (rev(i), 0)),
        ],
        out_specs=[
            pl.BlockSpec((GSTEP, W_GLA), lambda i: (rev(i), 0)),
            pl.BlockSpec((GSTEP, W_BL), lambda i: (rev(i), 0)),
            pl.BlockSpec((W_BL, 512), lambda i: (0, 0)),
            pl.BlockSpec((8, 512), lambda i: (0, 0)),
        ],
        out_shape=[
            jax.ShapeDtypeStruct((T, W_GLA), MXU),
            jax.ShapeDtypeStruct((T, W_BL), MXU),
            jax.ShapeDtypeStruct((W_BL, 512), F32),
            jax.ShapeDtypeStruct((8, 512), F32),
        ],
        scratch_shapes=[pltpu.VMEM((B_HEADS * B_DV, B_DK), F32)],
        compiler_params=_cp(("arbitrary",)),
    )(proj, proj, proj, proj, gu_pad, bias, states, do_b)


def _mid(x, target, proj, o_a, o_b, late, w_bn4, fnw):
    T = x.shape[0]
    tT = min(T, 128)
    nbuf = 4
    o_ag, o_bg, o_ma, o_mb = (c - C_GATES for c in (C_AG, C_BG, C_MA, C_MB))

    def body(x_ref, t_ref, oa_ref, ob_ref, gates_ref, late_ref, wbn_ref, fnw_ref,
             dx2_ref, doa_ref, dob_ref, dgates_ref,
             tail0_ref, tail1_ref, gfn_ref, gbn_ref, loss_ref, buf_ref, gw_ref):
        i = pl.program_id(0)

        def weight(p):
            return late_ref[:, 128 * p:128 * (p + 1), :].reshape(D, D)

        @pl.when(i == 0)
        def _():
            for r in (gw_ref, gfn_ref, gbn_ref, loss_ref):
                r[...] = jnp.zeros_like(r)

        rows = pl.ds(pl.multiple_of((i % nbuf) * tT, tT), tT)

        def keep(k, val):
            buf_ref[k, rows, :] = val

        oa, ag = oa_ref[...], gates_ref[:, o_ag:o_ag + D]
        sg_a = _sigmoid(ag)
        silu_a = ag * sg_a
        oag_b = (oa * silu_a).astype(MXU)
        keep(0, oag_b)
        y_a = _dot(oag_b, weight(0))

        ob, bg = ob_ref[...], gates_ref[:, o_bg:o_bg + D]
        rbs, obhats = [], []
        for h in range(B_HEADS):
            obh = ob[:, h * B_DV:(h + 1) * B_DV]
            rb = lax.rsqrt(jnp.mean(obh * obh, axis=-1, keepdims=True) + EPS)
            rbs.append(rb)
            obhats.append(obh * rb)
        obhat = jnp.concatenate(obhats, axis=1)
        wbn = wbn_ref[...]
        obn = obhat * wbn
        sg_b = _sigmoid(bg)
        silu_b = bg * sg_b
        obg_b = (obn * silu_b).astype(MXU)
        keep(1, obg_b)
        y_b = _dot(obg_b, weight(1))

        sa, sb = _sigmoid(gates_ref[:, o_ma:o_ma + D]), _sigmoid(gates_ref[:, o_mb:o_mb + D])
        mg_b = (sa * y_a + sb * y_b).astype(MXU)
        keep(2, mg_b)
        x2 = x_ref[...] + _dot(mg_b, weight(2))
        r2 = lax.rsqrt(jnp.mean(x2 * x2, axis=-1, keepdims=True) + EPS)
        xh2 = x2 * r2
        fw = fnw_ref[...]
        err = xh2 * fw - t_ref[...]
        tok = jnp.mean(err * err, axis=-1, keepdims=True)
        loss_ref[...] = loss_ref[...] + 0.5 * jnp.sum(tok, axis=0, keepdims=True)

        dy = err * (1.0 / D)
        gfn_ref[...] = gfn_ref[...] + jnp.broadcast_to(jnp.sum(dy * xh2, axis=0, keepdims=True), gfn_ref.shape)
        gy = dy * fw
        dx2 = r2 * (gy - xh2 * jnp.mean(gy * xh2, axis=-1, keepdims=True))
        dx2_ref[...] = dx2
        dx2_b = dx2.astype(MXU)
        keep(5, dx2_b)
        dmg = _dot_nt(dx2_b, weight(2))

        dgates_ref[:, o_ma:o_ma + D] = (dmg * y_a * sa * (1.0 - sa)).astype(dgates_ref.dtype)
        dgates_ref[:, o_mb:o_mb + D] = (dmg * y_b * sb * (1.0 - sb)).astype(dgates_ref.dtype)
        dya_b = (dmg * sa).astype(MXU)
        dyb_b = (dmg * sb).astype(MXU)
        keep(3, dya_b)
        keep(4, dyb_b)
        doag = _dot_nt(dya_b, weight(0))
        dobg = _dot_nt(dyb_b, weight(1))

        @pl.when(i % nbuf == nbuf - 1)
        def _():
            for p in range(3):
                gw_ref[p] = gw_ref[p] + _dot_tn(buf_ref[p], buf_ref[3 + p])

        @pl.when(i == pl.num_programs(0) - 1)
        def _():
            for hf, tail_ref in enumerate((tail0_ref, tail1_ref)):
                for d in range(NDEV):
                    for p in range(3):
                        tail_ref[d, 128 * p:128 * (p + 1), :] = (
                            gw_ref[p, 128 * d:128 * (d + 1), hf * DH:(hf + 1) * DH].astype(tail_ref.dtype))

        doa_ref[...] = doag * silu_a
        dgates_ref[:, o_ag:o_ag + D] = (doag * oa * (sg_a * (1.0 + ag * (1.0 - sg_a)))).astype(dgates_ref.dtype)
        dobn = dobg * silu_b
        dgates_ref[:, o_bg:o_bg + D] = (dobg * obn * (sg_b * (1.0 + bg * (1.0 - sg_b)))).astype(dgates_ref.dtype)
        gg = dobn * wbn
        gbn = jnp.zeros((1, B_DV), F32)
        for h in range(B_HEADS):
            sl = slice(h * B_DV, (h + 1) * B_DV)
            gbn = gbn + jnp.sum(dobn[:, sl] * obhats[h], axis=0, keepdims=True)
            ggh = gg[:, sl]
            dob_ref[:, sl] = rbs[h] * (ggh - obhats[h] * jnp.mean(ggh * obhats[h], axis=-1, keepdims=True))
        gbn_ref[...] = gbn_ref[...] + jnp.broadcast_to(gbn, gbn_ref.shape)

    assert (T // tT) % nbuf == 0
    tile = pl.BlockSpec((tT, D), lambda i: (i, 0))
    row = pl.BlockSpec((1, D), lambda i: (0, 0))
    acc8 = pl.BlockSpec((8, D), lambda i: (0, 0))
    return pl.pallas_call(
        body, name="mid", grid=(T // tT,),
        in_specs=[tile, tile, tile, tile, pl.BlockSpec((tT, W_GATES), lambda i: (i, C_GATES // W_GATES)),
                  _vmem(), row, row],
        out_specs=[tile, tile, tile, pl.BlockSpec((tT, W_GATES), lambda i: (i, 0)), _vmem(), _vmem(),
                   acc8, pl.BlockSpec((8, B_DV), lambda i: (0, 0)), pl.BlockSpec((8, LANE), lambda i: (0, 0))],
        out_shape=[
            jax.ShapeDtypeStruct((T, D), F32),
            jax.ShapeDtypeStruct((T, D), F32),
            jax.ShapeDtypeStruct((T, D), F32),
            jax.ShapeDtypeStruct((T, W_GATES), MXU),
            jax.ShapeDtypeStruct((NDEV, 384, DH), WIRE),
            jax.ShapeDtypeStruct((NDEV, 384, DH), WIRE),
            jax.ShapeDtypeStruct((8, D), F32),
            jax.ShapeDtypeStruct((8, B_DV), F32),
            jax.ShapeDtypeStruct((8, LANE), F32),
        ],
        scratch_shapes=[pltpu.VMEM((6, nbuf * tT, D), MXU), pltpu.VMEM((3, D, D), F32)],
        compiler_params=_cp(("arbitrary",)),
    )(x, target, o_a, o_b, proj, late, w_bn4, fnw)


DH = D // 2


_GW_TILES = (("q", 0, 512, 0), ("q", 1, 512, 512), ("kv", 0, 256, 1024), ("bl", 0, RANK, 5376),
             ("gla", 0, 512, 3328), ("gla", 1, 512, 3840), ("gla", 2, 512, 2304), ("gla", 3, 512, 2816),
             ("gates", 0, 512, 1280), ("gates", 1, 512, 1792), ("gates", 2, 512, 4352), ("gates", 3, 512, 4864),
             ("gates", 4, 512, 5392), ("gates", 5, 512, 5904), ("gates", 6, 512, 6416), ("gates", 7, 512, 6928))


def _gw_unpermute(piece, t):
    if piece == "q":
        parts = []
        for blk in range(t.shape[0] // LANE):
            g = [t[blk * LANE + 32 * i:blk * LANE + 32 * (i + 1)] for i in range(4)]
            parts += [g[0], g[2], g[1], g[3]]
        return jnp.concatenate(parts, axis=0)
    if piece == "kv":
        k = [t[64 * i:64 * i + 32] + t[64 * i + 32:64 * i + 64] for i in range(4)]
        v = [t[256 + 128 * g:256 + 128 * g + 64] + t[256 + 128 * g + 64:256 + 128 * (g + 1)] for g in range(2)]
        return jnp.concatenate(k + v, axis=0)
    if piece == "bl":
        return t[:RANK]
    return t


def _gw_half(h, pieces, half, after=None):
    T = h.shape[0]
    steps = len(_GW_TILES)

    def body(*refs):
        h_ref = refs[0]
        srcs = dict(zip(("q", "kv", "bl", "gla", "gates"), refs[1:6]))
        o_ref, stage, sems = refs[-3:]
        j = pl.program_id(0)

        def out_copy(k):
            _, _, n, off = _GW_TILES[k]
            return pltpu.make_async_copy(stage.at[k % 2, 0:n], o_ref.at[pl.ds(off, n)], sems.at[k % 2])

        for k, (piece, _, n, _) in enumerate(_GW_TILES):
            @pl.when(j == k)
            def _(k=k, piece=piece, n=n):
                if k >= 2:
                    out_copy(k - 2).wait()
                t = _gw_unpermute(piece, _dot_tn(srcs[piece][...], h_ref[...]))
                stage[k % 2, 0:n, :] = t.astype(stage.dtype)
                out_copy(k).start()

        @pl.when(j == steps - 1)
        def _():
            out_copy(steps - 2).wait()
            out_copy(steps - 1).wait()

    def tile_of(lo, hi):
        return lambda j: (0, jnp.clip(j - lo, 0, hi - lo - 1))

    in_specs = [pl.BlockSpec((T, DH), lambda j: (0, half)),
                pl.BlockSpec((T, 512), tile_of(0, 2)), pl.BlockSpec((T, 512), lambda j: (0, 0)),
                pl.BlockSpec((T, W_BL), lambda j: (0, 0)),
                pl.BlockSpec((T, 512), tile_of(4, 8)), pl.BlockSpec((T, 512), tile_of(8, 16))]
    args = [h, *pieces]
    if after is not None:
        in_specs.append(_any())
        args.append(after)
    return pl.pallas_call(
        body, name=f"gw_in_half{half}", grid=(steps,),
        in_specs=in_specs, out_specs=_any(),
        out_shape=jax.ShapeDtypeStruct((IN_WIDTH, DH), WIRE),
        scratch_shapes=[pltpu.VMEM((2, 512, DH), WIRE), pltpu.SemaphoreType.DMA((2,))],
        compiler_params=_cp(("arbitrary",)),
    )(*args)


def _chip_copies(s_ref, got_ref, send_sems, recv_sems):
    x, y, c = _place()
    chips = [(1 - x, y), (x, 1 - y), (1 - x, 1 - y)]
    return [pltpu.make_async_remote_copy(
        src_ref=s_ref.at[2 * px + py], dst_ref=got_ref.at[j],
        send_sem=send_sems.at[j], recv_sem=recv_sems.at[j], device_id=(px, py, c), device_id_type=MESH)
        for j, (px, py) in enumerate(chips)]


_EFFECT = pltpu.SideEffectType.DATAFLOW_SIDE_EFFECTING


def _hbm():
    return pl.BlockSpec(memory_space=pltpu.HBM)


def _sem():
    return pl.BlockSpec(memory_space=pltpu.SEMAPHORE)


def _chip_start(sums, half):
    land = pltpu.with_memory_space_constraint(lax.empty((3,) + sums.shape[1:], sums.dtype), pltpu.HBM)

    def body(s_ref, land_ref, send_sems, recv_sems, s_thru, land_thru, token):
        for cp in _chip_copies(s_ref, land_ref, send_sems, recv_sems):
            cp.start()
        token[...] = jnp.zeros_like(token)

    return pl.pallas_call(
        body, name=f"chip_start{half}",
        out_shape=(pltpu.SemaphoreType.DMA((3,)), pltpu.SemaphoreType.DMA((3,)),
                   pltpu.HBM(sums.shape, sums.dtype), pltpu.HBM(land.shape, land.dtype),
                   jax.ShapeDtypeStruct((8, LANE), F32)),
        in_specs=(_hbm(), _hbm()), out_specs=(_sem(), _sem(), _hbm(), _hbm(), _vmem()),
        input_output_aliases={0: 2, 1: 3},
        compiler_params=pltpu.CompilerParams(has_side_effects=_EFFECT),
    )(pltpu.with_memory_space_constraint(sums, pltpu.HBM), land)


def _chip_wait(send_sems, recv_sems, s_thru, land_thru, after, half):
    def body(s_ref, land_ref, send_sems, recv_sems, after_ref, s_out, got_ref):
        copies = _chip_copies(s_ref, land_ref, send_sems, recv_sems)
        for cp in copies:
            cp.wait_send()
        for cp in copies:
            cp.wait_recv()

    return pl.pallas_call(
        body, name=f"chip_wait{half}",
        out_shape=(pltpu.HBM(s_thru.shape, s_thru.dtype), pltpu.HBM(land_thru.shape, land_thru.dtype)),
        in_specs=(_hbm(), _hbm(), _sem(), _sem(), _any()), out_specs=(_hbm(), _hbm()),
        input_output_aliases={0: 0, 1: 1},
        compiler_params=pltpu.CompilerParams(has_side_effects=_EFFECT),
    )(s_thru, land_thru, send_sems, recv_sems, after)


def _dh_norm(pieces, offsets, wf, x, dx2, norm_w, after):
    T = x.shape[0]
    tT = min(T, 256)
    widths = [p.shape[1] for p in pieces]
    npc = len(pieces)

    def body(*refs):
        dp_refs = refs[:npc]
        wf_ref, x_ref, dx2_ref, nw_ref, _, gx_ref, gnw_ref = refs[npc:]

        @pl.when(pl.program_id(0) == 0)
        def _():
            gnw_ref[...] = jnp.zeros_like(gnw_ref)

        dh = jnp.zeros((tT, D), F32)
        for dp_ref, off, w in zip(dp_refs, offsets, widths):
            dh = dh + _dot(dp_ref[...], wf_ref[off:off + w, :])
        xv = x_ref[...]
        r = lax.rsqrt(jnp.mean(xv * xv, axis=-1, keepdims=True) + EPS)
        xh = xv * r
        gnw_ref[...] = gnw_ref[...] + jnp.broadcast_to(jnp.sum(dh * xh, axis=0, keepdims=True), gnw_ref.shape)
        g = dh * nw_ref[...]
        gx_ref[...] = r * (g - xh * jnp.mean(g * xh, axis=-1, keepdims=True)) + dx2_ref[...]

    tile = pl.BlockSpec((tT, D), lambda i: (i, 0))
    return pl.pallas_call(
        body, name="dh_norm", grid=(T // tT,),
        in_specs=[pl.BlockSpec((tT, w), lambda i: (i, 0)) for w in widths]
        + [_vmem(), tile, tile, pl.BlockSpec((1, D), lambda i: (0, 0)), _any()],
        out_specs=[tile, pl.BlockSpec((8, D), lambda i: (0, 0))],
        out_shape=[jax.ShapeDtypeStruct((T, D), F32), jax.ShapeDtypeStruct((8, D), F32)],
        compiler_params=_cp(("arbitrary",)),
    )(*pieces, wf, x, dx2, norm_w, after)


def _adamw_math(w, g, m, v):
    m = ADAM_B1 * m + (1.0 - ADAM_B1) * g
    v = ADAM_B2 * v + (1.0 - ADAM_B2) * (g * g)
    m_hat = m * (1.0 / (1.0 - ADAM_B1 ** ADAM_STEP))
    v_hat = v * (1.0 / (1.0 - ADAM_B2 ** ADAM_STEP))
    delta = -ADAM_LR * (m_hat / (jnp.sqrt(v_hat) + ADAM_EPS) + ADAM_WD * w)
    return delta, m, v


def _fetch_partials(s_ref, got_ref, buf, sems):
    x, y, _ = _place()
    cps = [pltpu.make_async_copy(s_ref.at[2 * x + y], buf.at[0], sems.at[0])]
    cps += [pltpu.make_async_copy(got_ref.at[j], buf.at[1 + j], sems.at[1 + j]) for j in range(3)]
    for cp in cps:
        cp.start()
    for cp in cps:
        cp.wait()


SMALL_AT = dict(norm_w=0, fnw=8, bias=16, bn=24, sinks=32, loss=40)
ROW_AT = (R_IN, R_A, R_B, R_O)


def _finish_small(ws, ms, vs, smalls):
    names = ["norm_w", "fnw", "bias", "bn", "sinks"]
    widths = [ws[n].shape[1] for n in names]

    def body(*refs):
        w_refs, m_refs, v_refs = refs[0:5], refs[5:10], refs[10:15]
        smalls_ref, loss_ref = refs[15], refs[16]
        outs, tot = refs[17:37], refs[37]
        acc = smalls_ref[0]
        for d in range(1, NDEV):
            acc = acc + smalls_ref[d]
        tot[...] = acc
        loss_ref[...] = tot[SMALL_AT["loss"]:SMALL_AT["loss"] + 1, 0:1]
        for p, (nm_, wd) in enumerate(zip(names, widths)):
            r = SMALL_AT[nm_]
            g = tot[r:r + 1, 0:wd]
            d, nm, nv = _adamw_math(w_refs[p][...], g, m_refs[p][...], v_refs[p][...])
            for o, val in zip(outs[4 * p:4 * p + 4], (g, d, nm, nv)):
                o[...] = val

    res = pl.pallas_call(
        body, name="finish_small",
        in_specs=[_vmem()] * 16, out_specs=[_vmem()] * 21,
        out_shape=[jax.ShapeDtypeStruct((1, 1), F32)]
        + [jax.ShapeDtypeStruct((1, wd), F32) for wd in widths for _ in range(4)],
        scratch_shapes=[pltpu.VMEM((SMALL_ROWS, D), F32)],
        compiler_params=_cp(),
    )(*[ws[n] for n in names], *[ms[n] for n in names], *[vs[n] for n in names], smalls)
    return res[0], {n: tuple(res[1 + 4 * p:5 + 4 * p]) for p, n in enumerate(names)}


def _finish(w_rows, m_rows, v_rows, gu_w, gu_m, gu_v, sums, got):
    shapes = [(SHARD, 1, D)] + [w.shape for w in w_rows[1:]]

    def columns(ref, p, cols):
        if p:
            return ref, (slice(None), cols)
        flat = ref if ref.shape == (SHARD * LANE_TILES, LANE) else ref.reshape(SHARD * LANE_TILES, LANE)
        return flat, (pl.ds(cols.start // LANE, SHARD, stride=LANE_TILES), slice(None))

    def read(ref, p, cols):
        ref, at = columns(ref, p, cols)
        return ref[at]

    def body(*refs):
        wr_refs, mr_refs, vr_refs = refs[0:4], refs[4:8], refs[8:12]
        guw_ref, gum_ref, guv_ref = refs[12:15]
        s_refs, got_refs = refs[15:17], refs[17:19]
        row_outs = refs[19:35]
        gu_outs = refs[35:39]
        buf, gsh, sems = refs[39:]
        x, y, c = _place()
        me_slot = 4 * x + 2 * y + c
        unshift = lax.rem(SHARD_PAD - 2 * me_slot, SHARD_PAD)

        def total(rows, cols):
            g = buf[0, rows, cols].astype(F32)
            for j in range(1, 4):
                g = g + buf[j, rows, cols].astype(F32)
            return g

        def update(p, g, cols):
            d, nm, nv = _adamw_math(read(wr_refs[p], p, cols), g, read(mr_refs[p], p, cols), read(vr_refs[p], p, cols))
            for o, val in zip(row_outs[4 * p:4 * p + 4], (g, d, nm, nv)):
                o, at = columns(o, p, cols)
                o[at] = val

        for hf in range(2):
            _fetch_partials(s_refs[hf], got_refs[hf], buf, sems)
            for cc in range(DH // LANE):
                src = slice(cc * LANE, (cc + 1) * LANE)
                cols = slice(hf * DH + cc * LANE, hf * DH + (cc + 1) * LANE)
                gsh[...] = pltpu.roll(total(slice(0, SHARD_PAD), src), unshift, 0)
                update(0, gsh[0:SHARD, :], cols)
                for p in range(1, 4):
                    update(p, total(slice(ROW_AT[p], ROW_AT[p] + 128), src), cols)
            if hf == 0:
                g = total(slice(R_GU, R_GU + RANK), slice(0, 64))
                d, nm, nv = _adamw_math(guw_ref[...], g, gum_ref[...], guv_ref[...])
                for o, val in zip(gu_outs, (g, d, nm, nv)):
                    o[...] = val

    res = pl.pallas_call(
        body, name="finish",
        in_specs=[_vmem()] * 15 + [_any()] * 4,
        out_specs=[_vmem()] * 20,
        out_shape=[jax.ShapeDtypeStruct(s, F32) for s in shapes for _ in range(4)]
        + [jax.ShapeDtypeStruct((RANK, 64), F32)] * 4,
        scratch_shapes=[pltpu.VMEM((4, ROWS, DH), sums[0].dtype), pltpu.VMEM((SHARD_PAD, LANE), F32),
                        pltpu.SemaphoreType.DMA((4,))],
        compiler_params=_cp(),
    )(*w_rows, *m_rows, *v_rows, gu_w, gu_m, gu_v, *sums, *got)
    return tuple(res[0:16]), tuple(res[16:20])


def _place():
    x, y, c = lax.axis_index("x"), lax.axis_index("y"), lax.axis_index("c")
    return x, y, c


def _peers(x, y, c):
    return [(x ^ dx, y ^ dy, c ^ dc) for dx in range(2) for dy in range(2) for dc in range(2) if dx + dy + dc]


def _late_gather_start(blk, after, name="late_gather"):
    land = pltpu.with_memory_space_constraint(lax.empty((NDEV,) + blk.shape, blk.dtype), pltpu.HBM)

    def body(b_ref, land_ref, after_ref, send_sems, recv_sems, b_thru, land_thru, token):
        x, y, c = _place()
        for k, to in enumerate(_peers(x, y, c)):
            pltpu.make_async_remote_copy(
                src_ref=b_ref, dst_ref=land_ref.at[4 * x + 2 * y + c], send_sem=send_sems.at[k],
                recv_sem=recv_sems.at[k], device_id=to, device_id_type=MESH).start()
        token[...] = jnp.zeros_like(token)

    return pl.pallas_call(
        body, name=name + "_start",
        out_shape=(pltpu.SemaphoreType.DMA((7,)), pltpu.SemaphoreType.DMA((7,)),
                   pltpu.HBM(blk.shape, blk.dtype), pltpu.HBM(land.shape, land.dtype),
                   jax.ShapeDtypeStruct((8, LANE), F32)),
        in_specs=(_hbm(), _hbm(), _any()), out_specs=(_sem(), _sem(), _hbm(), _hbm(), _vmem()),
        input_output_aliases={0: 2, 1: 3},
        compiler_params=pltpu.CompilerParams(has_side_effects=_EFFECT),
    )(pltpu.with_memory_space_constraint(blk, pltpu.HBM), land, after)


def _late_gather_wait(send_sems, recv_sems, b_thru, land_thru, after, after2, name="late_gather"):
    def body(b_ref, land_ref, send_sems, recv_sems, after_ref, after2_ref, b_out, got_ref):
        x, y, c = _place()
        copies = [pltpu.make_async_remote_copy(
            src_ref=b_ref, dst_ref=land_ref.at[4 * x + 2 * y + c], send_sem=send_sems.at[k],
            recv_sem=recv_sems.at[k], device_id=to, device_id_type=MESH)
            for k, to in enumerate(_peers(x, y, c))]
        for cp in copies:
            cp.wait_send()
        for cp in copies:
            cp.wait_recv()

    return pl.pallas_call(
        body, name=name + "_wait",
        out_shape=(pltpu.HBM(b_thru.shape, b_thru.dtype), pltpu.HBM(land_thru.shape, land_thru.dtype)),
        in_specs=(_hbm(), _hbm(), _sem(), _sem(), _any(), _any()), out_specs=(_hbm(), _hbm()),
        input_output_aliases={0: 0, 1: 1},
        compiler_params=pltpu.CompilerParams(has_side_effects=_EFFECT),
    )(b_thru, land_thru, send_sems, recv_sems, after, after2)


G_ROWS = SHARD_PAD + RANK


def _gather_blocks(w_in_t, gu_s, xs, norm_w, pos_col):
    rows, cols = G_ROWS, D
    T = xs.shape[0]
    tT = min(T, 256)
    inv_row, sign_row = _rope_rows()

    def body(wi_ref, gu_ref, xs_hbm, nw_ref, pos_ref, inv_ref, sign_ref,
             out_ref, h_ref, cos_ref, sin_ref, x_ref, frame_ref, xs_ref, send_sems, recv_sems, local_sem, xs_sem):
        load_xs = pltpu.make_async_copy(xs_hbm, xs_ref, xs_sem)
        load_xs.start()
        x, y, c = _place()
        me, sibling = (x, y, c), (x, y, 1 - c)
        chips = [(1 - x, y), (x, 1 - y), (1 - x, 1 - y)]
        shift = 2 * (4 * x + 2 * y + c)
        frame_ref[SHARD - SHARD % 8:, :] = jnp.zeros((SHARD_PAD - SHARD + SHARD % 8, LANE), F32)
        for cc in range(LANE_TILES):
            cs = slice(cc * LANE, (cc + 1) * LANE)
            frame_ref[:SHARD, :] = wi_ref[pl.ds(cc, SHARD, stride=LANE_TILES), :]
            x_ref[0:SHARD_PAD, cs] = pltpu.roll(frame_ref[...], shift, 0).astype(x_ref.dtype)
        x_ref[SHARD_PAD:G_ROWS, :] = jnp.zeros((RANK, D), x_ref.dtype)
        x_ref[SHARD_PAD:G_ROWS, 0:64] = gu_ref[...].astype(x_ref.dtype)

        def slot(px, py, pc):
            return out_ref.at[4 * px + 2 * py + pc]

        def copy(k, block, to, src=None):
            return pltpu.make_async_remote_copy(
                src_ref=slot(*block) if src is None else src, dst_ref=slot(*block),
                send_sem=send_sems.at[k], recv_sem=recv_sems.at[k], device_id=to, device_id_type=MESH)

        mine = pltpu.make_async_copy(x_ref, slot(*me), local_sem)
        mine.start()
        first = [copy(0, me, sibling, src=x_ref)]
        first += [copy(1 + j, me, (*chip, c), src=x_ref) for j, chip in enumerate(chips)]
        for cp in first:
            cp.start()
        load_xs.wait()

        @pl.loop(0, T // tT)
        def _(i):
            rows_i = pl.ds(pl.multiple_of(i * tT, tT), tT)
            _prologue_rows(rows_i, xs_ref, nw_ref, pos_ref, inv_ref, sign_ref, h_ref, cos_ref, sin_ref)

        passed = [copy(4 + j, (*chip, c), sibling) for j, chip in enumerate(chips)]
        for j, chip in enumerate(chips):
            copy(1 + j, (*chip, c), me).wait_recv()
            passed[j].start()
        copy(0, sibling, me).wait_recv()
        for j, chip in enumerate(chips):
            copy(4 + j, (*chip, 1 - c), me).wait_recv()
        for cp in first + passed:
            cp.wait_send()
        mine.wait()

    return pl.pallas_call(
        body, name="gather_weights",
        in_specs=[_vmem(), _vmem(), _any()] + [_vmem()] * 4, out_specs=[_any()] + [_vmem()] * 3,
        out_shape=[jax.ShapeDtypeStruct((NDEV, rows, cols), WIRE), jax.ShapeDtypeStruct((T, D), MXU),
                   jax.ShapeDtypeStruct((T, LANE), F32), jax.ShapeDtypeStruct((T, LANE), F32)],
        scratch_shapes=[pltpu.VMEM((rows, cols), WIRE), pltpu.VMEM((SHARD_PAD, LANE), F32), pltpu.VMEM((T, D), F32),
                        pltpu.SemaphoreType.DMA((7,)), pltpu.SemaphoreType.DMA((7,)), pltpu.SemaphoreType.DMA,
                        pltpu.SemaphoreType.DMA],
        compiler_params=_cp(),
    )(w_in_t, gu_s, xs, norm_w, pos_col, inv_row, sign_row)


def _pair_reduce(gwt, tails, half):
    n = gwt.shape[1]
    starts = [SHARD_PAD]
    for t in tails:
        starts.append(starts[-1] + t.shape[1])
    rows = starts[-1]
    blk = (4, rows, n)
    npart = 1 + len(tails)

    def body(*refs):
        g_ref, t_refs = refs[0], refs[1:npart]
        out_ref, acc, got, own, send_sems, recv_sems, own_sems, out_sems = refs[npart:]
        x, y, c = _place()

        def parts(d, dst):
            frame = g_ref.at[pl.ds(pl.multiple_of(FRAME * d, 16), SHARD_PAD)]
            return [(frame, dst.at[0:SHARD_PAD])] + [
                (t_ref.at[d], dst.at[starts[k]:starts[k + 1]]) for k, t_ref in enumerate(t_refs)]

        sends, loads, stores = [], [], []
        for chip in range(4):
            sends.append([pltpu.make_async_remote_copy(
                src_ref=s, dst_ref=d_, send_sem=send_sems.at[chip, k], recv_sem=recv_sems.at[chip, k],
                device_id=(x, y, 1 - c), device_id_type=MESH)
                for k, (s, d_) in enumerate(parts(2 * chip + (1 - c), got.at[chip]))])
            loads.append([pltpu.make_async_copy(s, d_, own_sems.at[chip, k])
                          for k, (s, d_) in enumerate(parts(2 * chip + c, own.at[chip]))])
            stores.append(pltpu.make_async_copy(acc.at[chip], out_ref.at[chip], out_sems.at[chip]))
        for group in sends + loads:
            for cp in group:
                cp.start()
        for chip in range(4):
            for cp in loads[chip]:
                cp.wait()
            for cp in sends[chip]:
                cp.wait_recv()
            acc[chip] = (own[chip].astype(F32) + got[chip].astype(F32)).astype(acc.dtype)
            stores[chip].start()
        for cp in stores:
            cp.wait()
        for group in sends:
            for cp in group:
                cp.wait_send()

    return pl.pallas_call(
        body, name=f"pair_reduce{half}",
        in_specs=[_any()] * npart, out_specs=_any(),
        out_shape=jax.ShapeDtypeStruct(blk, gwt.dtype),
        scratch_shapes=[pltpu.VMEM(blk, gwt.dtype), pltpu.VMEM(blk, gwt.dtype), pltpu.VMEM(blk, gwt.dtype),
                        pltpu.SemaphoreType.DMA((4, npart)), pltpu.SemaphoreType.DMA((4, npart)),
                        pltpu.SemaphoreType.DMA((4, npart)), pltpu.SemaphoreType.DMA((4,))],
        compiler_params=_cp(),
    )(gwt, *tails)


def _pad_cols(a, cols):
    return jnp.pad(a, ((0, 0), (0, cols - a.shape[1])))


def _pad_rows(a, rows):
    return jnp.pad(a, ((0, rows - a.shape[0]), (0, 0)))


FRAME = 928


def _wft_plan():
    moves = []
    for blk in range(8):
        for half in range(2):
            for sub in range(2):
                moves.append((C_Q + 128 * blk + 32 * (2 * half + sub), 128 * blk + 32 * (2 * sub + half), 32))
    for idx in range(4):
        for dup in range(2):
            moves.append((C_KD + 64 * idx + 32 * dup, 1024 + 32 * idx, 32))
    for g in range(2):
        for dup in range(2):
            moves.append((C_VD + 128 * g + 64 * dup, 1152 + 64 * g, 64))
    moves += [(C_BL, 5376, RANK), (C_BV, 3328, 1024), (C_BQ, 2304, 512), (C_BK, 2816, 512),
              (C_AG, 1280, 1024), (C_BG, 4352, 1024), (C_MA, 5392, 1024), (C_MB, 6416, 1024)]
    bulk, seams = [], []
    for dst, src, n in moves:
        r = src
        while r < src + n:
            f = min(r // FRAME, NDEV - 1)
            local = r - FRAME * f
            if f > 0 and local < 16:
                assert local == 0
                seams.append((f, dst + r - src))
                step = 16
            else:
                step = min(src + n, FRAME * (f + 1) if f < NDEV - 1 else IN_WIDTH) - r
                bulk.append((f, local, dst + r - src, step))
            r += step
    assert sorted(f for f, _ in seams) == list(range(1, NDEV))
    return bulk, seams, [(C_BL + RANK, C_GLA - C_BL - RANK)]


def _build_wft_copies(frames):
    bulk, seams, zeros = _wft_plan()
    (z0, zn), = zeros

    def body(f_ref, o_ref, edge, sems, esems):
        copies = [pltpu.make_async_copy(f_ref.at[f, pl.ds(l0, n)], o_ref.at[pl.ds(dst, n)], sems.at[i])
                  for i, (f, l0, dst, n) in enumerate(bulk)]
        loads = []
        for i, (f, _) in enumerate(seams):
            loads.append(pltpu.make_async_copy(f_ref.at[f, pl.ds(0, 16)], edge.at[i, 0], esems.at[i, 0]))
            loads.append(pltpu.make_async_copy(f_ref.at[f - 1, pl.ds(FRAME, 16)], edge.at[i, 1], esems.at[i, 1]))
        for cp in copies + loads:
            cp.start()
        o_ref[z0:z0 + zn, :] = jnp.zeros((zn, D), o_ref.dtype)
        for cp in loads:
            cp.wait()
        for i, (_, dst) in enumerate(seams):
            o_ref[dst:dst + 16, :] = edge[i, 0] + edge[i, 1]
        for cp in copies:
            cp.wait()

    return pl.pallas_call(
        body, name="build_wft",
        in_specs=[_any()], out_specs=_vmem(),
        out_shape=jax.ShapeDtypeStruct((NF, D), frames.dtype),
        scratch_shapes=[pltpu.VMEM((len(seams), 2, 16, D), frames.dtype),
                        pltpu.SemaphoreType.DMA((len(bulk),)), pltpu.SemaphoreType.DMA((len(seams), 2))],
        compiler_params=_cp(),
    )(frames)


def kernel(x, positions, norm_w, w_in, a_sinks, b_gate_up, b_gate_bias, b_out_norm_w, w_a_proj, w_b_proj, w_out, final_norm_w, loss_target, m_norm_w, m_w_in, m_a_sinks, m_b_gate_up, m_b_gate_bias, m_b_out_norm_w, m_w_a_proj, m_w_b_proj, m_w_out, m_final_norm_w, v_norm_w, v_w_in, v_a_sinks, v_b_gate_up, v_b_gate_bias, v_b_out_norm_w, v_w_a_proj, v_w_b_proj, v_w_out, v_final_norm_w):
    T = x.shape[1]
    xs, target = x[0], loss_target[0]
    fnw = final_norm_w.reshape(1, D)
    me = 4 * lax.axis_index("x") + 2 * lax.axis_index("y") + lax.axis_index("c")
    allw, h, cos, sin = _gather_blocks(_by_lane_tile(w_in), b_gate_up[0], xs, norm_w, positions.reshape(T, 1))
    late_blk = jnp.concatenate([w_a_proj[0], w_b_proj[0], w_out[0]], axis=0).astype(WIRE)
    l_send, l_recv, l_blk, l_land, l_started = _late_gather_start(late_blk, cos)
    wf = _build_wft_copies(allw)
    gu = allw[:, SHARD_PAD:G_ROWS, :64].transpose(1, 0, 2).reshape(RANK, 512)
    gu_pad = _pad_rows(gu, W_BL)

    proj = _proj(h, wf, l_started)
    o_a, lse = _swa_fwd(proj, cos, sin, a_sinks)
    o_b, states = _gla_fwd(proj, gu_pad, b_gate_bias)
    l_blk, l_land = _late_gather_wait(l_send, l_recv, l_blk, l_land, states, lse)
    late = lax.dynamic_update_slice(l_land, l_blk[None], (me, 0, 0))
    (dx2, do_a, do_b, d_gates, g_late0, g_late1, g_fn, g_bn, loss_part) = _mid(
        xs, target, proj, o_a, o_b, late, jnp.tile(b_out_norm_w, (1, B_HEADS)), fnw)
    d_q, d_kv, g_sinks = _swa_bwd(proj, cos, sin, a_sinks, do_a, o_a, lse, cos)
    d_gla, d_bl, g_gu, g_bias = _gla_bwd(proj, gu_pad, b_gate_bias, states, do_b)
    pieces = [d_q, d_kv, d_bl, d_gla, d_gates]
    offsets = [C_Q, C_KD, C_BL, C_GLA, C_GATES]

    ggu = g_gu[:RANK].reshape(RANK, NDEV, 64).transpose(1, 0, 2)
    ggu_half = [jnp.pad(ggu, ((0, 0), (0, 0), (0, DH - 64))).astype(WIRE), jnp.zeros((NDEV, RANK, DH), WIRE)]
    tails = [[g_late0, ggu_half[0]], [g_late1, ggu_half[1]]]

    send0, recv0, s_thru0, land0, started0 = _chip_start(_pair_reduce(_gw_half(h, pieces, 0), tails[0], 0), 0)
    send1, recv1, s_thru1, land1, started1 = _chip_start(
        _pair_reduce(_gw_half(h, pieces, 1, after=started0), tails[1], 1), 1)
    grad_x, g_nw = _dh_norm(pieces, offsets, wf, xs, dx2, norm_w, started1)
    small = jnp.concatenate([g_nw, g_fn, _pad_cols(g_bias, D), _pad_cols(g_bn, D), _pad_cols(g_sinks, D),
                             _pad_cols(loss_part, D)], axis=0)
    sm_send, sm_recv, sm_blk, sm_land, sm_started = _late_gather_start(small, g_nw, name="small_gather")
    sums0, got0 = _chip_wait(send0, recv0, s_thru0, land0, sm_started, 0)
    sums1, got1 = _chip_wait(send1, recv1, s_thru1, land1, got0, 1)
    sums, from_chips = [sums0, sums1], [got0, got1]

    ws = dict(norm_w=norm_w, fnw=fnw, bias=b_gate_bias, bn=b_out_norm_w, sinks=a_sinks)
    ms = dict(norm_w=m_norm_w, fnw=m_final_norm_w.reshape(1, D), bias=m_b_gate_bias, bn=m_b_out_norm_w,
              sinks=m_a_sinks)
    vs = dict(norm_w=v_norm_w, fnw=v_final_norm_w.reshape(1, D), bias=v_b_gate_bias, bn=v_b_out_norm_w,
              sinks=v_a_sinks)
    t_rows, t_gu = _finish(
        [_by_lane_tile(w_in), w_a_proj[0], w_b_proj[0], w_out[0]],
        [_by_lane_tile(m_w_in), m_w_a_proj[0], m_w_b_proj[0], m_w_out[0]],
        [_by_lane_tile(v_w_in), v_w_a_proj[0], v_w_b_proj[0], v_w_out[0]],
        b_gate_up[0], m_b_gate_up[0], v_b_gate_up[0], sums, from_chips)
    sm_blk, sm_land = _late_gather_wait(sm_send, sm_recv, sm_blk, sm_land, t_rows[0], t_gu[0], name="small_gather")
    loss, sm = _finish_small(ws, ms, vs, lax.dynamic_update_slice(sm_land, sm_blk[None], (me, 0, 0)))

    def outputs(k):
        return [sm["norm_w"][k], jnp.transpose(t_rows[k], (1, 2, 0)), sm["sinks"][k], t_gu[k][None], sm["bias"][k], sm["bn"][k],
                t_rows[4 + k][None], t_rows[8 + k][None], t_rows[12 + k][None], sm["fnw"][k].reshape(D)]

    return (loss[0, 0], grad_x[None], *outputs(0), *outputs(1), *outputs(2), *outputs(3))
```

```python
import functools

import numpy as np
import jax
import jax.numpy as jnp
from jax import lax
from jax.experimental import pallas as pl
from jax.experimental.pallas import tpu as pltpu

F32 = jnp.float32
MXU = jnp.bfloat16
WIRE = jnp.bfloat16

D = 1024
A_HEADS, A_KV, A_HD = 16, 2, 64
BLK = 128
B_HEADS, B_DK, B_DV = 4, 128, 256
RANK, TAU, CHUNK = 16, 16.0, 64
EPS, NEG = 1e-5, -1e30
ROPE_THETA = 10000.0
IN_WIDTH, NDEV = 7440, 8
SHARD = IN_WIDTH // NDEV
LANE = 128
LANE_TILES = D // LANE


def _by_lane_tile(a):
    return jnp.transpose(a, (2, 0, 1)).reshape(SHARD * LANE_TILES, LANE)


C_Q, C_KD, C_VD, C_BL = 0, 1024, 1280, 1536
C_BV, C_BQ, C_BK = 2048, 3072, 3584
C_AG, C_BG, C_MA, C_MB = 4096, 5120, 6144, 7168
C_GLA, W_GLA, C_GATES, W_GATES = 2048, 2048, 4096, 4096
NF = 8192
W_BL = 128

SHARD_PAD = 944
R_IN, R_A, R_B, R_O, R_GU, ROWS = 0, 944, 1072, 1200, 1328, 1344
SMALL_ROWS = 48

ADAM_LR, ADAM_B1, ADAM_B2, ADAM_EPS, ADAM_WD, ADAM_STEP = 0.001, 0.9, 0.999, 1e-08, 0.01, 10

MESH = pl.DeviceIdType.MESH
VMEM_LIMIT = 56 * 1024 * 1024


def _cp(sem=None, **kw):
    if sem is not None:
        kw["dimension_semantics"] = sem
    return pltpu.CompilerParams(vmem_limit_bytes=VMEM_LIMIT, **kw)


def _dot(a, b):
    return jnp.dot(a, b, preferred_element_type=F32)


def _dot_nt(a, b):
    return lax.dot_general(a, b, (((1,), (1,)), ((), ())), preferred_element_type=F32)


def _dot_tn(a, b):
    return lax.dot_general(a, b, (((0,), (0,)), ((), ())), preferred_element_type=F32)


def _dot_f32(a, b):
    return jnp.dot(a, b, preferred_element_type=F32, precision=lax.Precision.HIGHEST)


def _sigmoid(z):
    return 0.5 * jnp.tanh(0.5 * z) + 0.5


def _rope(xp, cos, sin):
    return xp * cos + pltpu.roll(xp, 64, 1) * sin


def _rope_bwd(dy, cos, sin):
    return dy * cos - pltpu.roll(dy, 64, 1) * sin


def _vmem():
    return pl.BlockSpec(memory_space=pltpu.VMEM)


def _any():
    return pl.BlockSpec(memory_space=pl.ANY)


def _rope_rows():
    half = A_HD // 2
    inv = (np.float32(ROPE_THETA) ** (-np.arange(half, dtype=np.float32) / np.float32(half))).astype(np.float32)
    inv_row = jnp.asarray(np.tile(inv, 4)[None, :])
    sign_row = jnp.asarray(np.concatenate([-np.ones(64, np.float32), np.ones(64, np.float32)])[None, :])
    return inv_row, sign_row


def _prologue_rows(rows, x_ref, nw_ref, pos_ref, inv_ref, sign_ref, h_ref, cos_ref, sin_ref):
    xv = x_ref[rows, :]
    r = lax.rsqrt(jnp.mean(xv * xv, axis=-1, keepdims=True) + EPS)
    h_ref[rows, :] = ((xv * r) * nw_ref[...]).astype(h_ref.dtype)
    ang = pos_ref[rows, :].astype(F32) * inv_ref[...]
    cos_ref[rows, :] = jnp.cos(ang)
    sin_ref[rows, :] = jnp.sin(ang) * sign_ref[...]


def _proj(h, wft, after):
    T = h.shape[0]
    tT, tN = T, 512

    def body(h_ref, w_ref, after_ref, o_ref):
        o_ref[...] = _dot_nt(h_ref[...], w_ref[...])

    return pl.pallas_call(
        body, name="proj", grid=(T // tT, NF // tN),
        in_specs=[pl.BlockSpec((tT, D), lambda i, j: (i, 0)), pl.BlockSpec((tN, D), lambda i, j: (j, 0)), _any()],
        out_specs=pl.BlockSpec((tT, tN), lambda i, j: (i, j)),
        out_shape=jax.ShapeDtypeStruct((T, NF), F32),
        compiler_params=_cp(("parallel", "parallel")),
    )(h, wft, after)


def _swa_masks():
    lane = lax.broadcasted_iota(jnp.int32, (BLK, LANE), 1)
    rope_sub0 = ((lane // 32) % 2) == 0
    std_sub0 = lane < 64
    return lane, rope_sub0, std_sub0


def _swa_tri():
    qi = lax.broadcasted_iota(jnp.int32, (BLK, BLK), 0)
    kj = lax.broadcasted_iota(jnp.int32, (BLK, BLK), 1)
    return kj <= qi


def _swa_fold(full, tri):
    return jnp.where(tri, full[:, BLK:], full[:, :BLK])


def _swa_unfold(sq, tri):
    return jnp.concatenate([jnp.where(tri, 0.0, sq), jnp.where(tri, sq, 0.0)], axis=1)


def _swa_keys(kc_ref, kp_ref, vc_ref, vp_ref, cq, sq, cp, sp):
    def ropek(kref, c, s):
        kv = kref[...]
        return jnp.concatenate([_rope(kv[:, :LANE], c, s), _rope(kv[:, LANE:], c, s)], axis=1)

    K = jnp.concatenate([ropek(kp_ref, cp, sp), ropek(kc_ref, cq, sq)], axis=0).astype(MXU)
    V = jnp.concatenate([vp_ref[...], vc_ref[...]], axis=0).astype(MXU)
    return K, V


def _swa_in_specs(nb, last):
    def cur(n):
        return jnp.minimum(n, last)

    def prev(n):
        return jnp.maximum(cur(n) - 1, 0)

    kd, vd = C_KD // 256, C_VD // 256
    return [
        pl.BlockSpec((BLK, D), lambda n: (cur(n), C_Q // D)),
        pl.BlockSpec((BLK, 256), lambda n: (cur(n), kd)),
        pl.BlockSpec((BLK, 256), lambda n: (prev(n), kd)),
        pl.BlockSpec((BLK, 256), lambda n: (cur(n), vd)),
        pl.BlockSpec((BLK, 256), lambda n: (prev(n), vd)),
        pl.BlockSpec((BLK, LANE), lambda n: (cur(n), 0)),
        pl.BlockSpec((BLK, LANE), lambda n: (cur(n), 0)),
        pl.BlockSpec((BLK, LANE), lambda n: (prev(n), 0)),
        pl.BlockSpec((BLK, LANE), lambda n: (prev(n), 0)),
    ]


def _swa_fwd(proj, cos, sin, sinks):
    T = proj.shape[0]
    nb = T // BLK
    scale = A_HD ** -0.5

    def body(sinks_ref, q_ref, kc_ref, kp_ref, vc_ref, vp_ref, cq_ref, sq_ref, cp_ref, sp_ref, o_ref, l_ref):
        n = pl.program_id(0)
        cq, sq = cq_ref[...], sq_ref[...]
        K, V = _swa_keys(kc_ref, kp_ref, vc_ref, vp_ref, cq, sq, cp_ref[...], sp_ref[...])
        tri = _swa_tri()
        valid = tri | (n > 0)
        lane, rope_sub0, std_sub0 = _swa_masks()
        group = A_HEADS // A_KV
        roped, lses = {}, []

        def products(head):
            pb, sub, g = head // 2, head % 2, head // group
            if sub == 0:
                roped[pb] = _rope(q_ref[:, pb * LANE:(pb + 1) * LANE], cq, sq)
            qm = jnp.where(rope_sub0 if sub == 0 else ~rope_sub0, roped[pb], 0.0).astype(MXU)
            return _dot_nt(qm, K[:, g * LANE:(g + 1) * LANE])

        def softmax(head, s_full):
            s = jnp.where(valid, _swa_fold(s_full, tri) * scale, NEG)
            sink = sinks_ref[0, head]
            m = jnp.maximum(jnp.max(s, axis=1, keepdims=True), sink)
            e = jnp.exp(s - m)
            den = jnp.sum(e, axis=1, keepdims=True) + jnp.exp(sink - m)
            lses.append(m + jnp.log(den))
            return _swa_unfold(e / den, tri).astype(MXU)

        outs = {}
        st1 = {0: products(0), 1: products(1)}
        st2 = {0: softmax(0, st1.pop(0))}
        for head in range(A_HEADS):
            if head + 2 < A_HEADS:
                st1[head + 2] = products(head + 2)
            if head + 1 < A_HEADS:
                st2[head + 1] = softmax(head + 1, st1.pop(head + 1))
            g = head // group
            outs[head] = _dot(st2.pop(head), V[:, g * LANE:(g + 1) * LANE])
            if head % 2 == 1:
                pb = head // 2
                o_ref[:, pb * LANE:(pb + 1) * LANE] = jnp.where(std_sub0, outs[head - 1], outs[head])
        lacc = jnp.zeros((BLK, LANE), F32)
        for head in range(A_HEADS):
            lacc = jnp.where(lane == head, lses[head], lacc)
        l_ref[...] = lacc

    return pl.pallas_call(
        body, name="swa_fwd", grid=(nb,),
        in_specs=[pl.BlockSpec(memory_space=pltpu.SMEM)] + _swa_in_specs(nb, nb - 1),
        out_specs=[pl.BlockSpec((BLK, D), lambda n: (n, 0)), pl.BlockSpec((BLK, LANE), lambda n: (n, 0))],
        out_shape=[jax.ShapeDtypeStruct((T, D), F32), jax.ShapeDtypeStruct((T, LANE), F32)],
        compiler_params=_cp(("parallel",)),
    )(sinks, proj, proj, proj, proj, proj, cos, sin, cos, sin)


def _swa_bwd(proj, cos, sin, sinks, do_a, o_a, lse, after):
    T = proj.shape[0]
    nb = T // BLK
    scale = A_HD ** -0.5

    def body(sinks_ref, q_ref, kc_ref, kp_ref, vc_ref, vp_ref, cq_ref, sq_ref, cp_ref, sp_ref,
             do_ref, o_ref, l_ref, after_ref, dq_ref, dkv_ref, ds_ref, ckv_ref):
        n = pl.program_id(0)

        @pl.when(n == 0)
        def _():
            ckv_ref[...] = jnp.zeros_like(ckv_ref)
            ds_ref[...] = jnp.zeros_like(ds_ref)

        @pl.when(n < nb)
        def _():
            cq, sq, cp, sp = cq_ref[...], sq_ref[...], cp_ref[...], sp_ref[...]
            K, V = _swa_keys(kc_ref, kp_ref, vc_ref, vp_ref, cq, sq, cp, sp)
            tri = _swa_tri()
            valid = tri | (n > 0)
            lane, rope_sub0, std_sub0 = _swa_masks()
            lane_row = lax.broadcasted_iota(jnp.int32, (1, LANE), 1)
            lse_v = l_ref[...]
            dKt = [jnp.zeros((LANE, 2 * BLK), F32) for _ in range(A_KV)]
            dVt = [jnp.zeros((LANE, 2 * BLK), F32) for _ in range(A_KV)]
            dsinks, roped, roped_t, do_t = [], {}, {}, {}
            group = A_HEADS // A_KV
            dim = lax.broadcasted_iota(jnp.int32, (LANE, BLK), 0)
            rope_row0, std_row0 = ((dim // 32) % 2) == 0, dim < 64

            def products(head):
                pb, sub, g = head // 2, head % 2, head // group
                cols = slice(pb * LANE, (pb + 1) * LANE)
                Kg, Vg = K[:, g * LANE:(g + 1) * LANE], V[:, g * LANE:(g + 1) * LANE]
                if sub == 0:
                    roped[pb] = _rope(q_ref[:, cols], cq, sq)
                    roped_t[pb] = roped[pb].T
                    do_t[pb] = do_ref[:, cols].T
                qm = jnp.where(rope_sub0 if sub == 0 else ~rope_sub0, roped[pb], 0.0).astype(MXU)
                qmt = jnp.where(rope_row0 if sub == 0 else ~rope_row0, roped_t[pb], 0.0).astype(MXU)
                dov = jnp.where(std_sub0 if sub == 0 else ~std_sub0, do_ref[:, cols], 0.0)
                dovt = jnp.where(std_row0 if sub == 0 else ~std_row0, do_t[pb], 0.0).astype(MXU)
                delta = jnp.sum(dov * o_ref[:, cols], axis=1, keepdims=True)
                return qmt, dovt, delta, _dot_nt(qm, Kg), _dot_nt(dov.astype(MXU), Vg)

            def scores(head, qmt, dovt, delta, s_full, dp_full):
                lh = jnp.sum(jnp.where(lane == head, lse_v, 0.0), axis=1, keepdims=True)
                p = jnp.where(valid, jnp.exp(_swa_fold(s_full, tri) * scale - lh), 0.0)
                psink = jnp.exp(sinks_ref[0, head] - lh)
                dsinks.append(jnp.sum(-psink * delta, axis=0, keepdims=True))
                dsq = (p * (_swa_fold(dp_full, tri) - delta)) * scale
                return qmt, dovt, _swa_unfold(p, tri).astype(MXU), _swa_unfold(dsq, tri).astype(MXU)

            def grads(head, qmt, dovt, pb16, dsc):
                g = head // group
                dKt[g] = dKt[g] + _dot(qmt, dsc)
                dVt[g] = dVt[g] + _dot(dovt, pb16)
                return _dot(dsc, K[:, g * LANE:(g + 1) * LANE])

            dqs = {}
            st1 = {0: products(0), 1: products(1)}
            st2 = {0: scores(0, *st1.pop(0))}
            for head in range(A_HEADS):
                if head + 2 < A_HEADS:
                    st1[head + 2] = products(head + 2)
                if head + 1 < A_HEADS:
                    st2[head + 1] = scores(head + 1, *st1.pop(head + 1))
                dqs[head] = grads(head, *st2.pop(head))
                if head % 2 == 1:
                    pb = head // 2
                    dqp = jnp.where(rope_sub0, dqs[head - 1], dqs[head])
                    dq_ref[:, pb * LANE:(pb + 1) * LANE] = _rope_bwd(dqp, cq, sq).astype(dq_ref.dtype)
            dsink = jnp.zeros((1, LANE), F32)
            for head in range(A_HEADS):
                dsink = jnp.where(lane_row == head, dsinks[head], dsink)
            dK, dV = [a.T for a in dKt], [a.T for a in dVt]
            prev = ([_rope_bwd(dK[g][:BLK], cp, sp) for g in range(A_KV)] + [dV[g][:BLK] for g in range(A_KV)])
            cur_ = ([_rope_bwd(dK[g][BLK:], cq, sq) for g in range(A_KV)] + [dV[g][BLK:] for g in range(A_KV)])
            dkv_ref[...] = (ckv_ref[...] + jnp.concatenate(prev, axis=1)).astype(dkv_ref.dtype)
            ckv_ref[...] = jnp.concatenate(cur_, axis=1)
            ds_ref[...] = ds_ref[...] + jnp.broadcast_to(dsink, ds_ref.shape)

        @pl.when(n == nb)
        def _():
            dkv_ref[...] = ckv_ref[...].astype(dkv_ref.dtype)

    last = nb - 1

    def cur(n):
        return jnp.minimum(n, last)

    def out_kv(n):
        return (jnp.maximum(n - 1, 0), 0)

    return pl.pallas_call(
        body, name="swa_bwd", grid=(nb + 1,),
        in_specs=[pl.BlockSpec(memory_space=pltpu.SMEM)] + _swa_in_specs(nb, last) + [
            pl.BlockSpec((BLK, D), lambda n: (cur(n), 0)),
            pl.BlockSpec((BLK, D), lambda n: (cur(n), 0)),
            pl.BlockSpec((BLK, LANE), lambda n: (cur(n), 0)),
            _any(),
        ],
        out_specs=[
            pl.BlockSpec((BLK, D), lambda n: (cur(n), 0)),
            pl.BlockSpec((BLK, 512), out_kv),
            pl.BlockSpec((8, LANE), lambda n: (0, 0)),
        ],
        out_shape=[
            jax.ShapeDtypeStruct((T, D), MXU),
            jax.ShapeDtypeStruct((T, 512), MXU),
            jax.ShapeDtypeStruct((8, LANE), F32),
        ],
        scratch_shapes=[pltpu.VMEM((BLK, 512), F32)],
        compiler_params=_cp(("arbitrary",)),
    )(sinks, proj, proj, proj, proj, proj, cos, sin, cos, sin, do_a, o_a, lse, after)


NCH = 4
GSTEP = NCH * CHUNK
ST_ROWS = B_HEADS * B_DV


def _chunk_rows(c):
    return slice(c * CHUNK, (c + 1) * CHUNK)


def _per_chunk(which, vals):
    out = vals[-1]
    for c in range(NCH - 2, -1, -1):
        out = jnp.where(which == c, vals[c], out)
    return out


def _gla_gate(bl_ref, gu_ref, bias_ref):
    gk = _dot(bl_ref[...].astype(MXU), gu_ref[...]) + bias_ref[...]
    la = (jnp.minimum(gk, 0.0) - jnp.log(1.0 + jnp.exp(-jnp.abs(gk)))) / TAU
    ri = lax.broadcasted_iota(jnp.int32, (GSTEP, GSTEP), 0)
    ci = lax.broadcasted_iota(jnp.int32, (GSTEP, GSTEP), 1)
    same = (ri // CHUNK) == (ci // CHUNK)
    lower, upper = same & (ci <= ri), same & (ci >= ri)
    b = _dot_f32(jnp.where(lower, 1.0, 0.0).astype(F32), la)
    which = lax.broadcasted_iota(jnp.int32, (GSTEP, 1), 0) // CHUNK
    return gk, la, b, lower, upper, which


def _gla_head(q_ref, k_ref, la, b, which, h):
    sl = slice(h * B_DK, (h + 1) * B_DK)
    bh, lah = b[:, sl], la[:, sl]
    bls = [jnp.sum(lah[_chunk_rows(c)], axis=0, keepdims=True) for c in range(NCH)]
    blast = _per_chunk(which, bls)
    qc = q_ref[:, sl] * (B_DK ** -0.5)
    kh = k_ref[:, sl]
    eb, enb, esb = jnp.exp(bh), jnp.exp(-bh), jnp.exp(blast - bh)
    return qc * eb, kh * enb, kh * esb, eb, enb, esb, [jnp.exp(v) for v in bls]


def _gla_specs(step_of):
    return [
        pl.BlockSpec((GSTEP, 512), lambda i: (step_of(i), C_BQ // 512)),
        pl.BlockSpec((GSTEP, 512), lambda i: (step_of(i), C_BK // 512)),
        pl.BlockSpec((GSTEP, D), lambda i: (step_of(i), C_BV // D)),
        pl.BlockSpec((GSTEP, W_BL), lambda i: (step_of(i), C_BL // W_BL)),
        pl.BlockSpec((W_BL, 512), lambda i: (0, 0)),
        pl.BlockSpec((1, 512), lambda i: (0, 0)),
    ]


def _gla_fwd(proj, gu_pad, bias):
    T = proj.shape[0]
    ns = T // GSTEP

    def body(q_ref, k_ref, v_ref, bl_ref, gu_ref, bias_ref, o_ref, st_ref, state_ref):
        @pl.when(pl.program_id(0) == 0)
        def _():
            state_ref[...] = jnp.zeros_like(state_ref)

        _, la, b, lower, _, which = _gla_gate(bl_ref, gu_ref, bias_ref)

        def within(h):
            q_e, k_e, k_s, _, _, _, decays = _gla_head(q_ref, k_ref, la, b, which, h)
            vh = v_ref[:, h * B_DV:(h + 1) * B_DV].astype(MXU)
            q_eb = q_e.astype(MXU)
            att = jnp.where(lower, _dot_nt(q_eb, k_e.astype(MXU)), 0.0)
            return vh, q_eb, k_s.astype(MXU), _dot(att.astype(MXU), vh), decays

        def across(h, vh, q_eb, k_sb, o_intra, decays):
            rows = slice(h * B_DV, (h + 1) * B_DV)
            s = state_ref[rows, :]
            outs = []
            for c in range(NCH):
                cr = _chunk_rows(c)
                st_ref[c * ST_ROWS + h * B_DV:c * ST_ROWS + (h + 1) * B_DV, :] = s
                outs.append(o_intra[cr] + _dot_nt(q_eb[cr], s.astype(MXU)))
                s = s * decays[c] + _dot_tn(vh[cr], k_sb[cr])
            state_ref[rows, :] = s
            o_ref[:, rows] = jnp.concatenate(outs, axis=0)

        for h in range(B_HEADS):
            across(h, *within(h))

    return pl.pallas_call(
        body, name="gla_fwd", grid=(ns,),
        in_specs=_gla_specs(lambda i: i),
        out_specs=[pl.BlockSpec((GSTEP, D), lambda i: (i, 0)),
                   pl.BlockSpec((NCH * ST_ROWS, B_DK), lambda i: (i, 0))],
        out_shape=[jax.ShapeDtypeStruct((T, D), F32),
                   jax.ShapeDtypeStruct((ns * NCH * ST_ROWS, B_DK), F32)],
        scratch_shapes=[pltpu.VMEM((ST_ROWS, B_DK), F32)],
        compiler_params=_cp(("arbitrary",)),
    )(proj, proj, proj, proj, gu_pad, bias)


def _gla_bwd(proj, gu_pad, bias, states, do_b):
    T = proj.shape[0]
    ns = T // GSTEP
    o_q, o_k = C_BQ - C_GLA, C_BK - C_GLA

    def body(q_ref, k_ref, v_ref, bl_ref, gu_ref, bias_ref, st_ref, do_ref,
             dg_ref, dbl_ref, ggu_ref, gbias_ref, gt_ref):
        @pl.when(pl.program_id(0) == 0)
        def _():
            gt_ref[...] = jnp.zeros_like(gt_ref)
            ggu_ref[...] = jnp.zeros_like(ggu_ref)
            gbias_ref[...] = jnp.zeros_like(gbias_ref)

        gk, la, b, lower, upper_mask, which = _gla_gate(bl_ref, gu_ref, bias_ref)
        upper = jnp.where(upper_mask, 1.0, 0.0).astype(F32)
        dla_parts = []

        def within(h):
            q_e, k_e, k_s, eb, enb, esb, decays = _gla_head(q_ref, k_ref, la, b, which, h)
            vh = v_ref[:, h * B_DV:(h + 1) * B_DV].astype(MXU)
            doh = do_ref[:, h * B_DV:(h + 1) * B_DV].astype(MXU)
            q_eb, k_eb = q_e.astype(MXU), k_e.astype(MXU)
            att = jnp.where(lower, _dot_nt(q_eb, k_eb), 0.0).astype(MXU)
            datt = jnp.where(lower, _dot_nt(doh, vh), 0.0).astype(MXU)
            return (q_e, k_e, k_s, eb, enb, esb, decays, vh, doh, q_eb, k_s.astype(MXU),
                    _dot(datt, k_eb), _dot_tn(datt, q_eb), _dot_tn(att, doh))

        def across(h, q_e, k_e, k_s, eb, enb, esb, decays, vh, doh, q_eb, k_sb, dq_i, dk_e, dv_i):
            rows = slice(h * B_DV, (h + 1) * B_DV)
            g = gt_ref[rows, :]
            dq_c, dks_c, dv_c, ddec = [None] * NCH, [None] * NCH, [None] * NCH, [None] * NCH
            for c in range(NCH - 1, -1, -1):
                cr = _chunk_rows(c)
                s = st_ref[c * ST_ROWS + h * B_DV:c * ST_ROWS + (h + 1) * B_DV, :]
                gb = g.astype(MXU)
                dq_c[c] = dq_i[cr] + _dot(doh[cr], s.astype(MXU))
                dks_c[c] = _dot(vh[cr], gb)
                dv_c[c] = dv_i[cr] + _dot_nt(k_sb[cr], gb)
                ddec[c] = jnp.sum(g * s, axis=0, keepdims=True)
                g = g * decays[c] + _dot_tn(doh[cr], q_eb[cr])
            gt_ref[rows, :] = g
            dq_e = jnp.concatenate(dq_c, axis=0)
            dk_s = jnp.concatenate(dks_c, axis=0)
            dg_ref[:, rows] = jnp.concatenate(dv_c, axis=0).astype(dg_ref.dtype)
            dg_ref[:, o_q + h * B_DK:o_q + (h + 1) * B_DK] = (dq_e * eb * (B_DK ** -0.5)).astype(dg_ref.dtype)
            dg_ref[:, o_k + h * B_DK:o_k + (h + 1) * B_DK] = (dk_e * enb + dk_s * esb).astype(dg_ref.dtype)
            dks_ks = dk_s * k_s
            db = dq_e * q_e - dk_e * k_e - dks_ks
            dbl = [jnp.sum(dks_ks[_chunk_rows(c)], axis=0, keepdims=True) + ddec[c] * decays[c] for c in range(NCH)]
            dla_parts.append(_dot_f32(upper, db) + _per_chunk(which, dbl))

        for h in range(B_HEADS):
            across(h, *within(h))
        dla = jnp.concatenate(dla_parts, axis=1)
        dgk = dla * (1.0 / TAU) * _sigmoid(-gk)
        dgkb = dgk.astype(MXU)
        dbl_ref[...] = _dot_nt(dgkb, gu_ref[...]).astype(dbl_ref.dtype)
        ggu_ref[...] = ggu_ref[...] + _dot_tn(bl_ref[...].astype(MXU), dgkb)
        gbias_ref[...] = gbias_ref[...] + jnp.broadcast_to(jnp.sum(dgk, axis=0, keepdims=True), gbias_ref.shape)

    def rev(i):
        return ns - 1 - i

    return pl.pallas_call(
        body, name="gla_bwd", grid=(ns,),
        in_specs=_gla_specs(rev) + [
            pl.BlockSpec((NCH * ST_ROWS, B_DK), lambda i: (rev(i), 0)),
            pl.BlockSpec((GSTEP, D), lambda i: (rev(i), 0)),
        ],
        out_specs=[
            pl.BlockSpec((GSTEP, W_GLA), lambda i: (rev(i), 0)),
            pl.BlockSpec((GSTEP, W_BL), lambda i: (rev(i), 0)),
            pl.BlockSpec((W_BL, 512), lambda i: (0, 0)),
            pl.BlockSpec((8, 512), lambda i: (0, 0)),
        ],
        out_shape=[
            jax.ShapeDtypeStruct((T, W_GLA), MXU),
            jax.ShapeDtypeStruct((T, W_BL), MXU),
            jax.ShapeDtypeStruct((W_BL, 512), F32),
            jax.ShapeDtypeStruct((8, 512), F32),
        ],
        scratch_shapes=[pltpu.VMEM((B_HEADS * B_DV, B_DK), F32)],
        compiler_params=_cp(("arbitrary",)),
    )(proj, proj, proj, proj, gu_pad, bias, states, do_b)


def _mid(x, target, proj, o_a, o_b, late, w_bn4, fnw):
    T = x.shape[0]
    tT = min(T, 128)
    nbuf = 4
    o_ag, o_bg, o_ma, o_mb = (c - C_GATES for c in (C_AG, C_BG, C_MA, C_MB))

    def body(x_ref, t_ref, oa_ref, ob_ref, gates_ref, late_ref, wbn_ref, fnw_ref,
             dx2_ref, doa_ref, dob_ref, dgates_ref,
             tail0_ref, tail1_ref, gfn_ref, gbn_ref, loss_ref, buf_ref, gw_ref):
        i = pl.program_id(0)

        def weight(p):
            return late_ref[:, 128 * p:128 * (p + 1), :].reshape(D, D)

        @pl.when(i == 0)
        def _():
            for r in (gw_ref, gfn_ref, gbn_ref, loss_ref):
                r[...] = jnp.zeros_like(r)

        rows = pl.ds(pl.multiple_of((i % nbuf) * tT, tT), tT)

        def keep(k, val):
            buf_ref[k, rows, :] = val

        oa, ag = oa_ref[...], gates_ref[:, o_ag:o_ag + D]
        sg_a = _sigmoid(ag)
        silu_a = ag * sg_a
        oag_b = (oa * silu_a).astype(MXU)
        keep(0, oag_b)
        y_a = _dot(oag_b, weight(0))

        ob, bg = ob_ref[...], gates_ref[:, o_bg:o_bg + D]
        rbs, obhats = [], []
        for h in range(B_HEADS):
            obh = ob[:, h * B_DV:(h + 1) * B_DV]
            rb = lax.rsqrt(jnp.mean(obh * obh, axis=-1, keepdims=True) + EPS)
            rbs.append(rb)
            obhats.append(obh * rb)
        obhat = jnp.concatenate(obhats, axis=1)
        wbn = wbn_ref[...]
        obn = obhat * wbn
        sg_b = _sigmoid(bg)
        silu_b = bg * sg_b
        obg_b = (obn * silu_b).astype(MXU)
        keep(1, obg_b)
        y_b = _dot(obg_b, weight(1))

        sa, sb = _sigmoid(gates_ref[:, o_ma:o_ma + D]), _sigmoid(gates_ref[:, o_mb:o_mb + D])
        mg_b = (sa * y_a + sb * y_b).astype(MXU)
        keep(2, mg_b)
        x2 = x_ref[...] + _dot(mg_b, weight(2))
        r2 = lax.rsqrt(jnp.mean(x2 * x2, axis=-1, keepdims=True) + EPS)
        xh2 = x2 * r2
        fw = fnw_ref[...]
        err = xh2 * fw - t_ref[...]
        tok = jnp.mean(err * err, axis=-1, keepdims=True)
        loss_ref[...] = loss_ref[...] + 0.5 * jnp.sum(tok, axis=0, keepdims=True)

        dy = err * (1.0 / D)
        gfn_ref[...] = gfn_ref[...] + jnp.broadcast_to(jnp.sum(dy * xh2, axis=0, keepdims=True), gfn_ref.shape)
        gy = dy * fw
        dx2 = r2 * (gy - xh2 * jnp.mean(gy * xh2, axis=-1, keepdims=True))
        dx2_ref[...] = dx2
        dx2_b = dx2.astype(MXU)
        keep(5, dx2_b)
        dmg = _dot_nt(dx2_b, weight(2))

        dgates_ref[:, o_ma:o_ma + D] = (dmg * y_a * sa * (1.0 - sa)).astype(dgates_ref.dtype)
        dgates_ref[:, o_mb:o_mb + D] = (dmg * y_b * sb * (1.0 - sb)).astype(dgates_ref.dtype)
        dya_b = (dmg * sa).astype(MXU)
        dyb_b = (dmg * sb).astype(MXU)
        keep(3, dya_b)
        keep(4, dyb_b)
        doag = _dot_nt(dya_b, weight(0))
        dobg = _dot_nt(dyb_b, weight(1))

        @pl.when(i % nbuf == nbuf - 1)
        def _():
            for p in range(3):
                gw_ref[p] = gw_ref[p] + _dot_tn(buf_ref[p], buf_ref[3 + p])

        @pl.when(i == pl.num_programs(0) - 1)
        def _():
            for hf, tail_ref in enumerate((tail0_ref, tail1_ref)):
                for d in range(NDEV):
                    for p in range(3):
                        tail_ref[d, 128 * p:128 * (p + 1), :] = (
                            gw_ref[p, 128 * d:128 * (d + 1), hf * DH:(hf + 1) * DH].astype(tail_ref.dtype))

        doa_ref[...] = doag * silu_a
        dgates_ref[:, o_ag:o_ag + D] = (doag * oa * (sg_a * (1.0 + ag * (1.0 - sg_a)))).astype(dgates_ref.dtype)
        dobn = dobg * silu_b
        dgates_ref[:, o_bg:o_bg + D] = (dobg * obn * (sg_b * (1.0 + bg * (1.0 - sg_b)))).astype(dgates_ref.dtype)
        gg = dobn * wbn
        gbn = jnp.zeros((1, B_DV), F32)
        for h in range(B_HEADS):
            sl = slice(h * B_DV, (h + 1) * B_DV)
            gbn = gbn + jnp.sum(dobn[:, sl] * obhats[h], axis=0, keepdims=True)
            ggh = gg[:, sl]
            dob_ref[:, sl] = rbs[h] * (ggh - obhats[h] * jnp.mean(ggh * obhats[h], axis=-1, keepdims=True))
        gbn_ref[...] = gbn_ref[...] + jnp.broadcast_to(gbn, gbn_ref.shape)

    assert (T // tT) % nbuf == 0
    tile = pl.BlockSpec((tT, D), lambda i: (i, 0))
    row = pl.BlockSpec((1, D), lambda i: (0, 0))
    acc8 = pl.BlockSpec((8, D), lambda i: (0, 0))
    return pl.pallas_call(
        body, name="mid", grid=(T // tT,),
        in_specs=[tile, tile, tile, tile, pl.BlockSpec((tT, W_GATES), lambda i: (i, C_GATES // W_GATES)),
                  _vmem(), row, row],
        out_specs=[tile, tile, tile, pl.BlockSpec((tT, W_GATES), lambda i: (i, 0)), _vmem(), _vmem(),
                   acc8, pl.BlockSpec((8, B_DV), lambda i: (0, 0)), pl.BlockSpec((8, LANE), lambda i: (0, 0))],
        out_shape=[
            jax.ShapeDtypeStruct((T, D), F32),
            jax.ShapeDtypeStruct((T, D), F32),
            jax.ShapeDtypeStruct((T, D), F32),
            jax.ShapeDtypeStruct((T, W_GATES), MXU),
            jax.ShapeDtypeStruct((NDEV, 384, DH), WIRE),
            jax.ShapeDtypeStruct((NDEV, 384, DH), WIRE),
            jax.ShapeDtypeStruct((8, D), F32),
            jax.ShapeDtypeStruct((8, B_DV), F32),
            jax.ShapeDtypeStruct((8, LANE), F32),
        ],
        scratch_shapes=[pltpu.VMEM((6, nbuf * tT, D), MXU), pltpu.VMEM((3, D, D), F32)],
        compiler_params=_cp(("arbitrary",)),
    )(x, target, o_a, o_b, proj, late, w_bn4, fnw)


DH = D // 2


_GW_TILES = (("q", 0, 512, 0), ("q", 1, 512, 512), ("kv", 0, 256, 1024), ("bl", 0, RANK, 5376),
             ("gla", 0, 512, 3328), ("gla", 1, 512, 3840), ("gla", 2, 512, 2304), ("gla", 3, 512, 2816),
             ("gates", 0, 512, 1280), ("gates", 1, 512, 1792), ("gates", 2, 512, 4352), ("gates", 3, 512, 4864),
             ("gates", 4, 512, 5392), ("gates", 5, 512, 5904), ("gates", 6, 512, 6416), ("gates", 7, 512, 6928))


def _gw_unpermute(piece, t):
    if piece == "q":
        parts = []
        for blk in range(t.shape[0] // LANE):
            g = [t[blk * LANE + 32 * i:blk * LANE + 32 * (i + 1)] for i in range(4)]
            parts += [g[0], g[2], g[1], g[3]]
        return jnp.concatenate(parts, axis=0)
    if piece == "kv":
        k = [t[64 * i:64 * i + 32] + t[64 * i + 32:64 * i + 64] for i in range(4)]
        v = [t[256 + 128 * g:256 + 128 * g + 64] + t[256 + 128 * g + 64:256 + 128 * (g + 1)] for g in range(2)]
        return jnp.concatenate(k + v, axis=0)
    if piece == "bl":
        return t[:RANK]
    return t


def _gw_half(h, pieces, half, after=None):
    T = h.shape[0]
    steps = len(_GW_TILES)

    def body(*refs):
        h_ref = refs[0]
        srcs = dict(zip(("q", "kv", "bl", "gla", "gates"), refs[1:6]))
        o_ref, stage, sems = refs[-3:]
        j = pl.program_id(0)

        def out_copy(k):
            _, _, n, off = _GW_TILES[k]
            return pltpu.make_async_copy(stage.at[k % 2, 0:n], o_ref.at[pl.ds(off, n)], sems.at[k % 2])

        for k, (piece, _, n, _) in enumerate(_GW_TILES):
            @pl.when(j == k)
            def _(k=k, piece=piece, n=n):
                if k >= 2:
                    out_copy(k - 2).wait()
                t = _gw_unpermute(piece, _dot_tn(srcs[piece][...], h_ref[...]))
                stage[k % 2, 0:n, :] = t.astype(stage.dtype)
                out_copy(k).start()

        @pl.when(j == steps - 1)
        def _():
            out_copy(steps - 2).wait()
            out_copy(steps - 1).wait()

    def tile_of(lo, hi):
        return lambda j: (0, jnp.clip(j - lo, 0, hi - lo - 1))

    in_specs = [pl.BlockSpec((T, DH), lambda j: (0, half)),
                pl.BlockSpec((T, 512), tile_of(0, 2)), pl.BlockSpec((T, 512), lambda j: (0, 0)),
                pl.BlockSpec((T, W_BL), lambda j: (0, 0)),
                pl.BlockSpec((T, 512), tile_of(4, 8)), pl.BlockSpec((T, 512), tile_of(8, 16))]
    args = [h, *pieces]
    if after is not None:
        in_specs.append(_any())
        args.append(after)
    return pl.pallas_call(
        body, name=f"gw_in_half{half}", grid=(steps,),
        in_specs=in_specs, out_specs=_any(),
        out_shape=jax.ShapeDtypeStruct((IN_WIDTH, DH), WIRE),
        scratch_shapes=[pltpu.VMEM((2, 512, DH), WIRE), pltpu.SemaphoreType.DMA((2,))],
        compiler_params=_cp(("arbitrary",)),
    )(*args)


def _chip_copies(s_ref, got_ref, send_sems, recv_sems):
    x, y, c = _place()
    chips = [(1 - x, y), (x, 1 - y), (1 - x, 1 - y)]
    return [pltpu.make_async_remote_copy(
        src_ref=s_ref.at[2 * px + py], dst_ref=got_ref.at[j],
        send_sem=send_sems.at[j], recv_sem=recv_sems.at[j], device_id=(px, py, c), device_id_type=MESH)
        for j, (px, py) in enumerate(chips)]


_EFFECT = pltpu.SideEffectType.DATAFLOW_SIDE_EFFECTING


def _hbm():
    return pl.BlockSpec(memory_space=pltpu.HBM)


def _sem():
    return pl.BlockSpec(memory_space=pltpu.SEMAPHORE)


def _chip_start(sums, half):
    land = pltpu.with_memory_space_constraint(lax.empty((3,) + sums.shape[1:], sums.dtype), pltpu.HBM)

    def body(s_ref, land_ref, send_sems, recv_sems, s_thru, land_thru, token):
        for cp in _chip_copies(s_ref, land_ref, send_sems, recv_sems):
            cp.start()
        token[...] = jnp.zeros_like(token)

    return pl.pallas_call(
        body, name=f"chip_start{half}",
        out_shape=(pltpu.SemaphoreType.DMA((3,)), pltpu.SemaphoreType.DMA((3,)),
                   pltpu.HBM(sums.shape, sums.dtype), pltpu.HBM(land.shape, land.dtype),
                   jax.ShapeDtypeStruct((8, LANE), F32)),
        in_specs=(_hbm(), _hbm()), out_specs=(_sem(), _sem(), _hbm(), _hbm(), _vmem()),
        input_output_aliases={0: 2, 1: 3},
        compiler_params=pltpu.CompilerParams(has_side_effects=_EFFECT),
    )(pltpu.with_memory_space_constraint(sums, pltpu.HBM), land)


def _chip_wait(send_sems, recv_sems, s_thru, land_thru, after, half):
    def body(s_ref, land_ref, send_sems, recv_sems, after_ref, s_out, got_ref):
        copies = _chip_copies(s_ref, land_ref, send_sems, recv_sems)
        for cp in copies:
            cp.wait_send()
        for cp in copies:
            cp.wait_recv()

    return pl.pallas_call(
        body, name=f"chip_wait{half}",
        out_shape=(pltpu.HBM(s_thru.shape, s_thru.dtype), pltpu.HBM(land_thru.shape, land_thru.dtype)),
        in_specs=(_hbm(), _hbm(), _sem(), _sem(), _any()), out_specs=(_hbm(), _hbm()),
        input_output_aliases={0: 0, 1: 1},
        compiler_params=pltpu.CompilerParams(has_side_effects=_EFFECT),
    )(s_thru, land_thru, send_sems, recv_sems, after)


def _dh_norm(pieces, offsets, wf, x, dx2, norm_w, after):
    T = x.shape[0]
    tT = min(T, 256)
    widths = [p.shape[1] for p in pieces]
    npc = len(pieces)

    def body(*refs):
        dp_refs = refs[:npc]
        wf_ref, x_ref, dx2_ref, nw_ref, _, gx_ref, gnw_ref = refs[npc:]

        @pl.when(pl.program_id(0) == 0)
        def _():
            gnw_ref[...] = jnp.zeros_like(gnw_ref)

        dh = jnp.zeros((tT, D), F32)
        for dp_ref, off, w in zip(dp_refs, offsets, widths):
            dh = dh + _dot(dp_ref[...], wf_ref[off:off + w, :])
        xv = x_ref[...]
        r = lax.rsqrt(jnp.mean(xv * xv, axis=-1, keepdims=True) + EPS)
        xh = xv * r
        gnw_ref[...] = gnw_ref[...] + jnp.broadcast_to(jnp.sum(dh * xh, axis=0, keepdims=True), gnw_ref.shape)
        g = dh * nw_ref[...]
        gx_ref[...] = r * (g - xh * jnp.mean(g * xh, axis=-1, keepdims=True)) + dx2_ref[...]

    tile = pl.BlockSpec((tT, D), lambda i: (i, 0))
    return pl.pallas_call(
        body, name="dh_norm", grid=(T // tT,),
        in_specs=[pl.BlockSpec((tT, w), lambda i: (i, 0)) for w in widths]
        + [_vmem(), tile, tile, pl.BlockSpec((1, D), lambda i: (0, 0)), _any()],
        out_specs=[tile, pl.BlockSpec((8, D), lambda i: (0, 0))],
        out_shape=[jax.ShapeDtypeStruct((T, D), F32), jax.ShapeDtypeStruct((8, D), F32)],
        compiler_params=_cp(("arbitrary",)),
    )(*pieces, wf, x, dx2, norm_w, after)


def _adamw_math(w, g, m, v):
    m = ADAM_B1 * m + (1.0 - ADAM_B1) * g
    v = ADAM_B2 * v + (1.0 - ADAM_B2) * (g * g)
    m_hat = m * (1.0 / (1.0 - ADAM_B1 ** ADAM_STEP))
    v_hat = v * (1.0 / (1.0 - ADAM_B2 ** ADAM_STEP))
    delta = -ADAM_LR * (m_hat / (jnp.sqrt(v_hat) + ADAM_EPS) + ADAM_WD * w)
    return delta, m, v


def _fetch_partials(s_ref, got_ref, buf, sems):
    x, y, _ = _place()
    cps = [pltpu.make_async_copy(s_ref.at[2 * x + y], buf.at[0], sems.at[0])]
    cps += [pltpu.make_async_copy(got_ref.at[j], buf.at[1 + j], sems.at[1 + j]) for j in range(3)]
    for cp in cps:
        cp.start()
    for cp in cps:
        cp.wait()


SMALL_AT = dict(norm_w=0, fnw=8, bias=16, bn=24, sinks=32, loss=40)
ROW_AT = (R_IN, R_A, R_B, R_O)


def _finish_small(ws, ms, vs, smalls):
    names = ["norm_w", "fnw", "bias", "bn", "sinks"]
    widths = [ws[n].shape[1] for n in names]

    def body(*refs):
        w_refs, m_refs, v_refs = refs[0:5], refs[5:10], refs[10:15]
        smalls_ref, loss_ref = refs[15], refs[16]
        outs, tot = refs[17:37], refs[37]
        acc = smalls_ref[0]
        for d in range(1, NDEV):
            acc = acc + smalls_ref[d]
        tot[...] = acc
        loss_ref[...] = tot[SMALL_AT["loss"]:SMALL_AT["loss"] + 1, 0:1]
        for p, (nm_, wd) in enumerate(zip(names, widths)):
            r = SMALL_AT[nm_]
            g = tot[r:r + 1, 0:wd]
            d, nm, nv = _adamw_math(w_refs[p][...], g, m_refs[p][...], v_refs[p][...])
            for o, val in zip(outs[4 * p:4 * p + 4], (g, d, nm, nv)):
                o[...] = val

    res = pl.pallas_call(
        body, name="finish_small",
        in_specs=[_vmem()] * 16, out_specs=[_vmem()] * 21,
        out_shape=[jax.ShapeDtypeStruct((1, 1), F32)]
        + [jax.ShapeDtypeStruct((1, wd), F32) for wd in widths for _ in range(4)],
        scratch_shapes=[pltpu.VMEM((SMALL_ROWS, D), F32)],
        compiler_params=_cp(),
    )(*[ws[n] for n in names], *[ms[n] for n in names], *[vs[n] for n in names], smalls)
    return res[0], {n: tuple(res[1 + 4 * p:5 + 4 * p]) for p, n in enumerate(names)}


def _finish(w_rows, m_rows, v_rows, gu_w, gu_m, gu_v, sums, got):
    shapes = [(SHARD, 1, D)] + [w.shape for w in w_rows[1:]]

    row_block = 64

    def columns(ref, p, cols, r0, n):
        if p:
            return ref, (slice(r0, r0 + n), cols)
        flat = ref if ref.shape == (SHARD * LANE_TILES, LANE) else ref.reshape(SHARD * LANE_TILES, LANE)
        return flat, (pl.ds(cols.start // LANE + LANE_TILES * r0, n, stride=LANE_TILES), slice(None))

    def read(ref, p, cols, r0, n):
        ref, at = columns(ref, p, cols, r0, n)
        return ref[at]

    def body(*refs):
        wr_refs, mr_refs, vr_refs = refs[0:4], refs[4:8], refs[8:12]
        guw_ref, gum_ref, guv_ref = refs[12:15]
        s_refs, got_refs = refs[15:17], refs[17:19]
        row_outs = refs[19:35]
        gu_outs = refs[35:39]
        buf, gsh, sems = refs[39:]
        x, y, c = _place()
        me_slot = 4 * x + 2 * y + c
        down = 2 * me_slot

        def total(rows, cols):
            g = buf[0, rows, cols].astype(F32)
            for j in range(1, 4):
                g = g + buf[j, rows, cols].astype(F32)
            return g

        def update(p, grad, cols):
            nrows = shapes[p][0]
            for r0 in range(0, nrows, row_block):
                n = min(row_block, nrows - r0)
                g = grad(r0, n)
                d, nm, nv = _adamw_math(read(wr_refs[p], p, cols, r0, n), g, read(mr_refs[p], p, cols, r0, n),
                                        read(vr_refs[p], p, cols, r0, n))
                for o, val in zip(row_outs[4 * p:4 * p + 4], (g, d, nm, nv)):
                    o, at = columns(o, p, cols, r0, n)
                    o[at] = val

        for hf in range(2):
            _fetch_partials(s_refs[hf], got_refs[hf], buf, sems)
            for cc in range(DH // LANE):
                src = slice(cc * LANE, (cc + 1) * LANE)
                cols = slice(hf * DH + cc * LANE, hf * DH + (cc + 1) * LANE)
                for r0 in range(0, SHARD_PAD, row_block):
                    rows = slice(r0, min(r0 + row_block, SHARD_PAD))
                    gsh[rows, :] = total(rows, src)
                update(0, lambda r0, n: gsh[pl.ds(down + r0, n), :], cols)
                for p in range(1, 4):
                    update(p, lambda r0, n, p=p: total(slice(ROW_AT[p] + r0, ROW_AT[p] + r0 + n), src), cols)
            if hf == 0:
                g = total(slice(R_GU, R_GU + RANK), slice(0, 64))
                d, nm, nv = _adamw_math(guw_ref[...], g, gum_ref[...], guv_ref[...])
                for o, val in zip(gu_outs, (g, d, nm, nv)):
                    o[...] = val

    res = pl.pallas_call(
        body, name="finish",
        in_specs=[_vmem()] * 15 + [_any()] * 4,
        out_specs=[_vmem()] * 20,
        out_shape=[jax.ShapeDtypeStruct(s, F32) for s in shapes for _ in range(4)]
        + [jax.ShapeDtypeStruct((RANK, 64), F32)] * 4,
        scratch_shapes=[pltpu.VMEM((4, ROWS, DH), sums[0].dtype), pltpu.VMEM((SHARD_PAD, LANE), F32),
                        pltpu.SemaphoreType.DMA((4,))],
        compiler_params=_cp(),
    )(*w_rows, *m_rows, *v_rows, gu_w, gu_m, gu_v, *sums, *got)
    return tuple(res[0:16]), tuple(res[16:20])


def _place():
    x, y, c = lax.axis_index("x"), lax.axis_index("y"), lax.axis_index("c")
    return x, y, c


def _peers(x, y, c):
    return [(x ^ dx, y ^ dy, c ^ dc) for dx in range(2) for dy in range(2) for dc in range(2) if dx + dy + dc]


def _late_gather_start(blk, after, name="late_gather"):
    land = pltpu.with_memory_space_constraint(lax.empty((NDEV,) + blk.shape, blk.dtype), pltpu.HBM)

    def body(b_ref, land_ref, after_ref, send_sems, recv_sems, b_thru, land_thru, token):
        x, y, c = _place()
        for k, to in enumerate(_peers(x, y, c)):
            pltpu.make_async_remote_copy(
                src_ref=b_ref, dst_ref=land_ref.at[4 * x + 2 * y + c], send_sem=send_sems.at[k],
                recv_sem=recv_sems.at[k], device_id=to, device_id_type=MESH).start()
        token[...] = jnp.zeros_like(token)

    return pl.pallas_call(
        body, name=name + "_start",
        out_shape=(pltpu.SemaphoreType.DMA((7,)), pltpu.SemaphoreType.DMA((7,)),
                   pltpu.HBM(blk.shape, blk.dtype), pltpu.HBM(land.shape, land.dtype),
                   jax.ShapeDtypeStruct((8, LANE), F32)),
        in_specs=(_hbm(), _hbm(), _any()), out_specs=(_sem(), _sem(), _hbm(), _hbm(), _vmem()),
        input_output_aliases={0: 2, 1: 3},
        compiler_params=pltpu.CompilerParams(has_side_effects=_EFFECT),
    )(pltpu.with_memory_space_constraint(blk, pltpu.HBM), land, after)


def _late_gather_wait(send_sems, recv_sems, b_thru, land_thru, after, after2, name="late_gather"):
    def body(b_ref, land_ref, send_sems, recv_sems, after_ref, after2_ref, b_out, got_ref):
        x, y, c = _place()
        copies = [pltpu.make_async_remote_copy(
            src_ref=b_ref, dst_ref=land_ref.at[4 * x + 2 * y + c], send_sem=send_sems.at[k],
            recv_sem=recv_sems.at[k], device_id=to, device_id_type=MESH)
            for k, to in enumerate(_peers(x, y, c))]
        for cp in copies:
            cp.wait_send()
        for cp in copies:
            cp.wait_recv()

    return pl.pallas_call(
        body, name=name + "_wait",
        out_shape=(pltpu.HBM(b_thru.shape, b_thru.dtype), pltpu.HBM(land_thru.shape, land_thru.dtype)),
        in_specs=(_hbm(), _hbm(), _sem(), _sem(), _any(), _any()), out_specs=(_hbm(), _hbm()),
        input_output_aliases={0: 0, 1: 1},
        compiler_params=pltpu.CompilerParams(has_side_effects=_EFFECT),
    )(b_thru, land_thru, send_sems, recv_sems, after, after2)


G_ROWS = SHARD_PAD + RANK


def _gather_blocks(w_in_t, gu_s, xs, norm_w, pos_col):
    rows, cols = G_ROWS, D
    T = xs.shape[0]
    tT = min(T, 256)
    inv_row, sign_row = _rope_rows()

    def body(wi_ref, gu_ref, xs_hbm, nw_ref, pos_ref, inv_ref, sign_ref,
             out_ref, h_ref, cos_ref, sin_ref, x_ref, frame_ref, xs_ref, send_sems, recv_sems, local_sem, xs_sem):
        load_xs = pltpu.make_async_copy(xs_hbm, xs_ref, xs_sem)
        load_xs.start()
        x, y, c = _place()
        me, sibling = (x, y, c), (x, y, 1 - c)
        chips = [(1 - x, y), (x, 1 - y), (1 - x, 1 - y)]
        shift = 2 * (4 * x + 2 * y + c)
        frame_ref[SHARD - SHARD % 8:, :] = jnp.zeros((SHARD_PAD - SHARD + SHARD % 8, LANE), F32)
        for cc in range(LANE_TILES):
            cs = slice(cc * LANE, (cc + 1) * LANE)
            frame_ref[:SHARD, :] = wi_ref[pl.ds(cc, SHARD, stride=LANE_TILES), :]
            x_ref[0:SHARD_PAD, cs] = pltpu.roll(frame_ref[...], shift, 0).astype(x_ref.dtype)
        x_ref[SHARD_PAD:G_ROWS, :] = jnp.zeros((RANK, D), x_ref.dtype)
        x_ref[SHARD_PAD:G_ROWS, 0:64] = gu_ref[...].astype(x_ref.dtype)

        def slot(px, py, pc):
            return out_ref.at[4 * px + 2 * py + pc]

        def copy(k, block, to, src=None):
            return pltpu.make_async_remote_copy(
                src_ref=slot(*block) if src is None else src, dst_ref=slot(*block),
                send_sem=send_sems.at[k], recv_sem=recv_sems.at[k], device_id=to, device_id_type=MESH)

        mine = pltpu.make_async_copy(x_ref, slot(*me), local_sem)
        mine.start()
        first = [copy(0, me, sibling, src=x_ref)]
        first += [copy(1 + j, me, (*chip, c), src=x_ref) for j, chip in enumerate(chips)]
        for cp in first:
            cp.start()
        load_xs.wait()

        @pl.loop(0, T // tT)
        def _(i):
            rows_i = pl.ds(pl.multiple_of(i * tT, tT), tT)
            _prologue_rows(rows_i, xs_ref, nw_ref, pos_ref, inv_ref, sign_ref, h_ref, cos_ref, sin_ref)

        passed = [copy(4 + j, (*chip, c), sibling) for j, chip in enumerate(chips)]
        for j, chip in enumerate(chips):
            copy(1 + j, (*chip, c), me).wait_recv()
            passed[j].start()
        copy(0, sibling, me).wait_recv()
        for j, chip in enumerate(chips):
            copy(4 + j, (*chip, 1 - c), me).wait_recv()
        for cp in first + passed:
            cp.wait_send()
        mine.wait()

    return pl.pallas_call(
        body, name="gather_weights",
        in_specs=[_vmem(), _vmem(), _any()] + [_vmem()] * 4, out_specs=[_any()] + [_vmem()] * 3,
        out_shape=[jax.ShapeDtypeStruct((NDEV, rows, cols), WIRE), jax.ShapeDtypeStruct((T, D), MXU),
                   jax.ShapeDtypeStruct((T, LANE), F32), jax.ShapeDtypeStruct((T, LANE), F32)],
        scratch_shapes=[pltpu.VMEM((rows, cols), WIRE), pltpu.VMEM((SHARD_PAD, LANE), F32), pltpu.VMEM((T, D), F32),
                        pltpu.SemaphoreType.DMA((7,)), pltpu.SemaphoreType.DMA((7,)), pltpu.SemaphoreType.DMA,
                        pltpu.SemaphoreType.DMA],
        compiler_params=_cp(),
    )(w_in_t, gu_s, xs, norm_w, pos_col, inv_row, sign_row)


def _pair_reduce(gwt, tails, half):
    n = gwt.shape[1]
    starts = [SHARD_PAD]
    for t in tails:
        starts.append(starts[-1] + t.shape[1])
    rows = starts[-1]
    blk = (4, rows, n)
    npart = 1 + len(tails)

    def body(*refs):
        g_ref, t_refs = refs[0], refs[1:npart]
        out_ref, acc, got, own, send_sems, recv_sems, own_sems, out_sems = refs[npart:]
        x, y, c = _place()

        def parts(d, dst):
            frame = g_ref.at[pl.ds(pl.multiple_of(FRAME * d, 16), SHARD_PAD)]
            return [(frame, dst.at[0:SHARD_PAD])] + [
                (t_ref.at[d], dst.at[starts[k]:starts[k + 1]]) for k, t_ref in enumerate(t_refs)]

        sends, loads, stores = [], [], []
        for chip in range(4):
            sends.append([pltpu.make_async_remote_copy(
                src_ref=s, dst_ref=d_, send_sem=send_sems.at[chip, k], recv_sem=recv_sems.at[chip, k],
                device_id=(x, y, 1 - c), device_id_type=MESH)
                for k, (s, d_) in enumerate(parts(2 * chip + (1 - c), got.at[chip]))])
            loads.append([pltpu.make_async_copy(s, d_, own_sems.at[chip, k])
                          for k, (s, d_) in enumerate(parts(2 * chip + c, own.at[chip]))])
            stores.append(pltpu.make_async_copy(acc.at[chip], out_ref.at[chip], out_sems.at[chip]))
        for group in sends + loads:
            for cp in group:
                cp.start()
        for chip in range(4):
            for cp in loads[chip]:
                cp.wait()
            for cp in sends[chip]:
                cp.wait_recv()
            acc[chip] = (own[chip].astype(F32) + got[chip].astype(F32)).astype(acc.dtype)
            stores[chip].start()
        for cp in stores:
            cp.wait()
        for group in sends:
            for cp in group:
                cp.wait_send()

    return pl.pallas_call(
        body, name=f"pair_reduce{half}",
        in_specs=[_any()] * npart, out_specs=_any(),
        out_shape=jax.ShapeDtypeStruct(blk, gwt.dtype),
        scratch_shapes=[pltpu.VMEM(blk, gwt.dtype), pltpu.VMEM(blk, gwt.dtype), pltpu.VMEM(blk, gwt.dtype),
                        pltpu.SemaphoreType.DMA((4, npart)), pltpu.SemaphoreType.DMA((4, npart)),
                        pltpu.SemaphoreType.DMA((4, npart)), pltpu.SemaphoreType.DMA((4,))],
        compiler_params=_cp(),
    )(gwt, *tails)


def _pad_cols(a, cols):
    return jnp.pad(a, ((0, 0), (0, cols - a.shape[1])))


def _pad_rows(a, rows):
    return jnp.pad(a, ((0, rows - a.shape[0]), (0, 0)))


FRAME = 928


def _wft_plan():
    moves = []
    for blk in range(8):
        for half in range(2):
            for sub in range(2):
                moves.append((C_Q + 128 * blk + 32 * (2 * half + sub), 128 * blk + 32 * (2 * sub + half), 32))
    for idx in range(4):
        for dup in range(2):
            moves.append((C_KD + 64 * idx + 32 * dup, 1024 + 32 * idx, 32))
    for g in range(2):
        for dup in range(2):
            moves.append((C_VD + 128 * g + 64 * dup, 1152 + 64 * g, 64))
    moves += [(C_BL, 5376, RANK), (C_BV, 3328, 1024), (C_BQ, 2304, 512), (C_BK, 2816, 512),
              (C_AG, 1280, 1024), (C_BG, 4352, 1024), (C_MA, 5392, 1024), (C_MB, 6416, 1024)]
    bulk, seams = [], []
    for dst, src, n in moves:
        r = src
        while r < src + n:
            f = min(r // FRAME, NDEV - 1)
            local = r - FRAME * f
            if f > 0 and local < 16:
                assert local == 0
                seams.append((f, dst + r - src))
                step = 16
            else:
                step = min(src + n, FRAME * (f + 1) if f < NDEV - 1 else IN_WIDTH) - r
                bulk.append((f, local, dst + r - src, step))
            r += step
    assert sorted(f for f, _ in seams) == list(range(1, NDEV))
    return bulk, seams, [(C_BL + RANK, C_GLA - C_BL - RANK)]


def _build_wft_copies(frames):
    bulk, seams, zeros = _wft_plan()
    (z0, zn), = zeros

    def body(f_ref, o_ref, edge, sems, esems):
        copies = [pltpu.make_async_copy(f_ref.at[f, pl.ds(l0, n)], o_ref.at[pl.ds(dst, n)], sems.at[i])
                  for i, (f, l0, dst, n) in enumerate(bulk)]
        loads = []
        for i, (f, _) in enumerate(seams):
            loads.append(pltpu.make_async_copy(f_ref.at[f, pl.ds(0, 16)], edge.at[i, 0], esems.at[i, 0]))
            loads.append(pltpu.make_async_copy(f_ref.at[f - 1, pl.ds(FRAME, 16)], edge.at[i, 1], esems.at[i, 1]))
        for cp in copies + loads:
            cp.start()
        o_ref[z0:z0 + zn, :] = jnp.zeros((zn, D), o_ref.dtype)
        for cp in loads:
            cp.wait()
        for i, (_, dst) in enumerate(seams):
            o_ref[dst:dst + 16, :] = edge[i, 0] + edge[i, 1]
        for cp in copies:
            cp.wait()

    return pl.pallas_call(
        body, name="build_wft",
        in_specs=[_any()], out_specs=_vmem(),
        out_shape=jax.ShapeDtypeStruct((NF, D), frames.dtype),
        scratch_shapes=[pltpu.VMEM((len(seams), 2, 16, D), frames.dtype),
                        pltpu.SemaphoreType.DMA((len(bulk),)), pltpu.SemaphoreType.DMA((len(seams), 2))],
        compiler_params=_cp(),
    )(frames)


def kernel(x, positions, norm_w, w_in, a_sinks, b_gate_up, b_gate_bias, b_out_norm_w, w_a_proj, w_b_proj, w_out, final_norm_w, loss_target, m_norm_w, m_w_in, m_a_sinks, m_b_gate_up, m_b_gate_bias, m_b_out_norm_w, m_w_a_proj, m_w_b_proj, m_w_out, m_final_norm_w, v_norm_w, v_w_in, v_a_sinks, v_b_gate_up, v_b_gate_bias, v_b_out_norm_w, v_w_a_proj, v_w_b_proj, v_w_out, v_final_norm_w):
    T = x.shape[1]
    xs, target = x[0], loss_target[0]
    fnw = final_norm_w.reshape(1, D)
    me = 4 * lax.axis_index("x") + 2 * lax.axis_index("y") + lax.axis_index("c")
    allw, h, cos, sin = _gather_blocks(_by_lane_tile(w_in), b_gate_up[0], xs, norm_w, positions.reshape(T, 1))
    late_blk = jnp.concatenate([w_a_proj[0], w_b_proj[0], w_out[0]], axis=0).astype(WIRE)
    l_send, l_recv, l_blk, l_land, l_started = _late_gather_start(late_blk, cos)
    wf = _build_wft_copies(allw)
    gu = allw[:, SHARD_PAD:G_ROWS, :64].transpose(1, 0, 2).reshape(RANK, 512)
    gu_pad = _pad_rows(gu, W_BL)

    proj = _proj(h, wf, l_started)
    o_a, lse = _swa_fwd(proj, cos, sin, a_sinks)
    o_b, states = _gla_fwd(proj, gu_pad, b_gate_bias)
    l_blk, l_land = _late_gather_wait(l_send, l_recv, l_blk, l_land, states, lse)
    late = lax.dynamic_update_slice(l_land, l_blk[None], (me, 0, 0))
    (dx2, do_a, do_b, d_gates, g_late0, g_late1, g_fn, g_bn, loss_part) = _mid(
        xs, target, proj, o_a, o_b, late, jnp.tile(b_out_norm_w, (1, B_HEADS)), fnw)
    d_q, d_kv, g_sinks = _swa_bwd(proj, cos, sin, a_sinks, do_a, o_a, lse, cos)
    d_gla, d_bl, g_gu, g_bias = _gla_bwd(proj, gu_pad, b_gate_bias, states, do_b)
    pieces = [d_q, d_kv, d_bl, d_gla, d_gates]
    offsets = [C_Q, C_KD, C_BL, C_GLA, C_GATES]

    ggu = g_gu[:RANK].reshape(RANK, NDEV, 64).transpose(1, 0, 2)
    ggu_half = [jnp.pad(ggu, ((0, 0), (0, 0), (0, DH - 64))).astype(WIRE), jnp.zeros((NDEV, RANK, DH), WIRE)]
    tails = [[g_late0, ggu_half[0]], [g_late1, ggu_half[1]]]

    send0, recv0, s_thru0, land0, started0 = _chip_start(_pair_reduce(_gw_half(h, pieces, 0), tails[0], 0), 0)
    send1, recv1, s_thru1, land1, started1 = _chip_start(
        _pair_reduce(_gw_half(h, pieces, 1, after=started0), tails[1], 1), 1)
    grad_x, g_nw = _dh_norm(pieces, offsets, wf, xs, dx2, norm_w, started1)
    small = jnp.concatenate([g_nw, g_fn, _pad_cols(g_bias, D), _pad_cols(g_bn, D), _pad_cols(g_sinks, D),
                             _pad_cols(loss_part, D)], axis=0)
    sm_send, sm_recv, sm_blk, sm_land, sm_started = _late_gather_start(small, g_nw, name="small_gather")
    sums0, got0 = _chip_wait(send0, recv0, s_thru0, land0, sm_started, 0)
    sums1, got1 = _chip_wait(send1, recv1, s_thru1, land1, got0, 1)
    sums, from_chips = [sums0, sums1], [got0, got1]

    ws = dict(norm_w=norm_w, fnw=fnw, bias=b_gate_bias, bn=b_out_norm_w, sinks=a_sinks)
    ms = dict(norm_w=m_norm_w, fnw=m_final_norm_w.reshape(1, D), bias=m_b_gate_bias, bn=m_b_out_norm_w,
              sinks=m_a_sinks)
    vs = dict(norm_w=v_norm_w, fnw=v_final_norm_w.reshape(1, D), bias=v_b_gate_bias, bn=v_b_out_norm_w,
              sinks=v_a_sinks)
    t_rows, t_gu = _finish(
        [_by_lane_tile(w_in), w_a_proj[0], w_b_proj[0], w_out[0]],
        [_by_lane_tile(m_w_in), m_w_a_proj[0], m_w_b_proj[0], m_w_out[0]],
        [_by_lane_tile(v_w_in), v_w_a_proj[0], v_w_b_proj[0], v_w_out[0]],
        b_gate_up[0], m_b_gate_up[0], v_b_gate_up[0], sums, from_chips)
    sm_blk, sm_land = _late_gather_wait(sm_send, sm_recv, sm_blk, sm_land, t_rows[0], t_gu[0], name="small_gather")
    loss, sm = _finish_small(ws, ms, vs, lax.dynamic_update_slice(sm_land, sm_blk[None], (me, 0, 0)))

    def outputs(k):
        return [sm["norm_w"][k], jnp.transpose(t_rows[k], (1, 2, 0)), sm["sinks"][k], t_gu[k][None], sm["bias"][k], sm["bn"][k],
                t_rows[4 + k][None], t_rows[8 + k][None], t_rows[12 + k][None], sm["fnw"][k].reshape(D)]

    return (loss[0, 0], grad_x[None], *outputs(0), *outputs(1), *outputs(2), *outputs(3))
```

```python
import functools

import numpy as np
import jax
import jax.numpy as jnp
from jax import lax
from jax.experimental import pallas as pl
from jax.experimental.pallas import tpu as pltpu

F32 = jnp.float32
MXU = jnp.bfloat16
WIRE = jnp.bfloat16

D = 1024
A_HEADS, A_KV, A_HD = 16, 2, 64
BLK = 128
B_HEADS, B_DK, B_DV = 4, 128, 256
RANK, TAU, CHUNK = 16, 16.0, 64
EPS, NEG = 1e-5, -1e30
ROPE_THETA = 10000.0
IN_WIDTH, NDEV = 7440, 8
SHARD = IN_WIDTH // NDEV
LANE = 128
LANE_TILES = D // LANE


def _by_lane_tile(a):
    return jnp.transpose(a, (2, 0, 1)).reshape(SHARD * LANE_TILES, LANE)


C_Q, C_KD, C_VD, C_BL = 0, 1024, 1280, 1536
C_BV, C_BQ, C_BK = 2048, 3072, 3584
C_AG, C_BG, C_MA, C_MB = 4096, 5120, 6144, 7168
C_GLA, W_GLA, C_GATES, W_GATES = 2048, 2048, 4096, 4096
NF = 8192
W_BL = 128

SHARD_PAD = 944
R_IN, R_A, R_B, R_O, R_GU, ROWS = 0, 944, 1072, 1200, 1328, 1344
SMALL_ROWS = 48

ADAM_LR, ADAM_B1, ADAM_B2, ADAM_EPS, ADAM_WD, ADAM_STEP = 0.001, 0.9, 0.999, 1e-08, 0.01, 10

MESH = pl.DeviceIdType.MESH
VMEM_LIMIT = 56 * 1024 * 1024


def _cp(sem=None, **kw):
    if sem is not None:
        kw["dimension_semantics"] = sem
    return pltpu.CompilerParams(vmem_limit_bytes=VMEM_LIMIT, **kw)


def _dot(a, b):
    return jnp.dot(a, b, preferred_element_type=F32)


def _dot_nt(a, b):
    return lax.dot_general(a, b, (((1,), (1,)), ((), ())), preferred_element_type=F32)


def _dot_tn(a, b):
    return lax.dot_general(a, b, (((0,), (0,)), ((), ())), preferred_element_type=F32)


def _dot_f32(a, b):
    return jnp.dot(a, b, preferred_element_type=F32, precision=lax.Precision.HIGHEST)


def _sigmoid(z):
    return 0.5 * jnp.tanh(0.5 * z) + 0.5


def _rope(xp, cos, sin):
    return xp * cos + pltpu.roll(xp, 64, 1) * sin


def _rope_bwd(dy, cos, sin):
    return dy * cos - pltpu.roll(dy, 64, 1) * sin


def _vmem():
    return pl.BlockSpec(memory_space=pltpu.VMEM)


def _any():
    return pl.BlockSpec(memory_space=pl.ANY)


def _rope_rows():
    half = A_HD // 2
    inv = (np.float32(ROPE_THETA) ** (-np.arange(half, dtype=np.float32) / np.float32(half))).astype(np.float32)
    inv_row = jnp.asarray(np.tile(inv, 4)[None, :])
    sign_row = jnp.asarray(np.concatenate([-np.ones(64, np.float32), np.ones(64, np.float32)])[None, :])
    return inv_row, sign_row


def _prologue_rows(rows, x_ref, nw_ref, pos_ref, inv_ref, sign_ref, h_ref, cos_ref, sin_ref):
    xv = x_ref[rows, :]
    r = lax.rsqrt(jnp.mean(xv * xv, axis=-1, keepdims=True) + EPS)
    h_ref[rows, :] = ((xv * r) * nw_ref[...]).astype(h_ref.dtype)
    ang = pos_ref[rows, :].astype(F32) * inv_ref[...]
    cos_ref[rows, :] = jnp.cos(ang)
    sin_ref[rows, :] = jnp.sin(ang) * sign_ref[...]


def _proj(h, wft, after):
    T = h.shape[0]
    tT, tN = T, 512

    def body(h_ref, w_ref, after_ref, o_ref):
        o_ref[...] = _dot_nt(h_ref[...], w_ref[...])

    return pl.pallas_call(
        body, name="proj", grid=(T // tT, NF // tN),
        in_specs=[pl.BlockSpec((tT, D), lambda i, j: (i, 0)), pl.BlockSpec((tN, D), lambda i, j: (j, 0)), _any()],
        out_specs=pl.BlockSpec((tT, tN), lambda i, j: (i, j)),
        out_shape=jax.ShapeDtypeStruct((T, NF), F32),
        compiler_params=_cp(("parallel", "parallel")),
    )(h, wft, after)


def _swa_masks():
    lane = lax.broadcasted_iota(jnp.int32, (BLK, LANE), 1)
    rope_sub0 = ((lane // 32) % 2) == 0
    std_sub0 = lane < 64
    return lane, rope_sub0, std_sub0


def _swa_tri():
    qi = lax.broadcasted_iota(jnp.int32, (BLK, BLK), 0)
    kj = lax.broadcasted_iota(jnp.int32, (BLK, BLK), 1)
    return kj <= qi


def _swa_fold(full, tri):
    return jnp.where(tri, full[:, BLK:], full[:, :BLK])


def _swa_unfold(sq, tri):
    return jnp.concatenate([jnp.where(tri, 0.0, sq), jnp.where(tri, sq, 0.0)], axis=1)


def _swa_keys(kc_ref, kp_ref, vc_ref, vp_ref, cq, sq, cp, sp):
    def ropek(kref, c, s):
        kv = kref[...]
        return jnp.concatenate([_rope(kv[:, :LANE], c, s), _rope(kv[:, LANE:], c, s)], axis=1)

    K = jnp.concatenate([ropek(kp_ref, cp, sp), ropek(kc_ref, cq, sq)], axis=0).astype(MXU)
    V = jnp.concatenate([vp_ref[...], vc_ref[...]], axis=0).astype(MXU)
    return K, V


def _swa_in_specs(nb, last):
    def cur(n):
        return jnp.minimum(n, last)

    def prev(n):
        return jnp.maximum(cur(n) - 1, 0)

    kd, vd = C_KD // 256, C_VD // 256
    return [
        pl.BlockSpec((BLK, D), lambda n: (cur(n), C_Q // D)),
        pl.BlockSpec((BLK, 256), lambda n: (cur(n), kd)),
        pl.BlockSpec((BLK, 256), lambda n: (prev(n), kd)),
        pl.BlockSpec((BLK, 256), lambda n: (cur(n), vd)),
        pl.BlockSpec((BLK, 256), lambda n: (prev(n), vd)),
        pl.BlockSpec((BLK, LANE), lambda n: (cur(n), 0)),
        pl.BlockSpec((BLK, LANE), lambda n: (cur(n), 0)),
        pl.BlockSpec((BLK, LANE), lambda n: (prev(n), 0)),
        pl.BlockSpec((BLK, LANE), lambda n: (prev(n), 0)),
    ]


def _swa_fwd(proj, cos, sin, sinks):
    T = proj.shape[0]
    nb = T // BLK
    scale = A_HD ** -0.5

    def body(sinks_ref, q_ref, kc_ref, kp_ref, vc_ref, vp_ref, cq_ref, sq_ref, cp_ref, sp_ref, o_ref, l_ref):
        n = pl.program_id(0)
        cq, sq = cq_ref[...], sq_ref[...]
        K, V = _swa_keys(kc_ref, kp_ref, vc_ref, vp_ref, cq, sq, cp_ref[...], sp_ref[...])
        tri = _swa_tri()
        valid = tri | (n > 0)
        lane, rope_sub0, std_sub0 = _swa_masks()
        group = A_HEADS // A_KV
        roped, lses = {}, []

        def products(head):
            pb, sub, g = head // 2, head % 2, head // group
            if sub == 0:
                roped[pb] = _rope(q_ref[:, pb * LANE:(pb + 1) * LANE], cq, sq)
            qm = jnp.where(rope_sub0 if sub == 0 else ~rope_sub0, roped[pb], 0.0).astype(MXU)
            return _dot_nt(qm, K[:, g * LANE:(g + 1) * LANE])

        def softmax(head, s_full):
            s = jnp.where(valid, _swa_fold(s_full, tri) * scale, NEG)
            sink = sinks_ref[0, head]
            m = jnp.maximum(jnp.max(s, axis=1, keepdims=True), sink)
            e = jnp.exp(s - m)
            den = jnp.sum(e, axis=1, keepdims=True) + jnp.exp(sink - m)
            lses.append(m + jnp.log(den))
            return _swa_unfold(e / den, tri).astype(MXU)

        outs = {}
        st1 = {0: products(0), 1: products(1)}
        st2 = {0: softmax(0, st1.pop(0))}
        for head in range(A_HEADS):
            if head + 2 < A_HEADS:
                st1[head + 2] = products(head + 2)
            if head + 1 < A_HEADS:
                st2[head + 1] = softmax(head + 1, st1.pop(head + 1))
            g = head // group
            outs[head] = _dot(st2.pop(head), V[:, g * LANE:(g + 1) * LANE])
            if head % 2 == 1:
                pb = head // 2
                o_ref[:, pb * LANE:(pb + 1) * LANE] = jnp.where(std_sub0, outs[head - 1], outs[head])
        lacc = jnp.zeros((BLK, LANE), F32)
        for head in range(A_HEADS):
            lacc = jnp.where(lane == head, lses[head], lacc)
        l_ref[...] = lacc

    return pl.pallas_call(
        body, name="swa_fwd", grid=(nb,),
        in_specs=[pl.BlockSpec(memory_space=pltpu.SMEM)] + _swa_in_specs(nb, nb - 1),
        out_specs=[pl.BlockSpec((BLK, D), lambda n: (n, 0)), pl.BlockSpec((BLK, LANE), lambda n: (n, 0))],
        out_shape=[jax.ShapeDtypeStruct((T, D), F32), jax.ShapeDtypeStruct((T, LANE), F32)],
        compiler_params=_cp(("parallel",)),
    )(sinks, proj, proj, proj, proj, proj, cos, sin, cos, sin)


def _swa_bwd(proj, cos, sin, sinks, do_a, o_a, lse, after):
    T = proj.shape[0]
    nb = T // BLK
    scale = A_HD ** -0.5

    def body(sinks_ref, q_ref, kc_ref, kp_ref, vc_ref, vp_ref, cq_ref, sq_ref, cp_ref, sp_ref,
             do_ref, o_ref, l_ref, after_ref, dq_ref, dkv_ref, ds_ref, ckv_ref):
        n = pl.program_id(0)

        @pl.when(n == 0)
        def _():
            ckv_ref[...] = jnp.zeros_like(ckv_ref)
            ds_ref[...] = jnp.zeros_like(ds_ref)

        @pl.when(n < nb)
        def _():
            cq, sq, cp, sp = cq_ref[...], sq_ref[...], cp_ref[...], sp_ref[...]
            K, V = _swa_keys(kc_ref, kp_ref, vc_ref, vp_ref, cq, sq, cp, sp)
            tri = _swa_tri()
            valid = tri | (n > 0)
            lane, rope_sub0, std_sub0 = _swa_masks()
            lane_row = lax.broadcasted_iota(jnp.int32, (1, LANE), 1)
            lse_v = l_ref[...]
            dKt = [jnp.zeros((LANE, 2 * BLK), F32) for _ in range(A_KV)]
            dVt = [jnp.zeros((LANE, 2 * BLK), F32) for _ in range(A_KV)]
            dsinks, roped, roped_t, do_t = [], {}, {}, {}
            group = A_HEADS // A_KV
            dim = lax.broadcasted_iota(jnp.int32, (LANE, BLK), 0)
            rope_row0, std_row0 = ((dim // 32) % 2) == 0, dim < 64

            def products(head):
                pb, sub, g = head // 2, head % 2, head // group
                cols = slice(pb * LANE, (pb + 1) * LANE)
                Kg, Vg = K[:, g * LANE:(g + 1) * LANE], V[:, g * LANE:(g + 1) * LANE]
                if sub == 0:
                    roped[pb] = _rope(q_ref[:, cols], cq, sq)
                    roped_t[pb] = roped[pb].T
                    do_t[pb] = do_ref[:, cols].T
                qm = jnp.where(rope_sub0 if sub == 0 else ~rope_sub0, roped[pb], 0.0).astype(MXU)
                qmt = jnp.where(rope_row0 if sub == 0 else ~rope_row0, roped_t[pb], 0.0).astype(MXU)
                dov = jnp.where(std_sub0 if sub == 0 else ~std_sub0, do_ref[:, cols], 0.0)
                dovt = jnp.where(std_row0 if sub == 0 else ~std_row0, do_t[pb], 0.0).astype(MXU)
                delta = jnp.sum(dov * o_ref[:, cols], axis=1, keepdims=True)
                return qmt, dovt, delta, _dot_nt(qm, Kg), _dot_nt(dov.astype(MXU), Vg)

            def scores(head, qmt, dovt, delta, s_full, dp_full):
                lh = jnp.sum(jnp.where(lane == head, lse_v, 0.0), axis=1, keepdims=True)
                p = jnp.where(valid, jnp.exp(_swa_fold(s_full, tri) * scale - lh), 0.0)
                psink = jnp.exp(sinks_ref[0, head] - lh)
                dsinks.append(jnp.sum(-psink * delta, axis=0, keepdims=True))
                dsq = (p * (_swa_fold(dp_full, tri) - delta)) * scale
                return qmt, dovt, _swa_unfold(p, tri).astype(MXU), _swa_unfold(dsq, tri).astype(MXU)

            def grads(head, qmt, dovt, pb16, dsc):
                g = head // group
                dKt[g] = dKt[g] + _dot(qmt, dsc)
                dVt[g] = dVt[g] + _dot(dovt, pb16)
                return _dot(dsc, K[:, g * LANE:(g + 1) * LANE])

            dqs = {}
            st1 = {0: products(0), 1: products(1)}
            st2 = {0: scores(0, *st1.pop(0))}
            for head in range(A_HEADS):
                if head + 2 < A_HEADS:
                    st1[head + 2] = products(head + 2)
                if head + 1 < A_HEADS:
                    st2[head + 1] = scores(head + 1, *st1.pop(head + 1))
                dqs[head] = grads(head, *st2.pop(head))
                if head % 2 == 1:
                    pb = head // 2
                    dqp = jnp.where(rope_sub0, dqs[head - 1], dqs[head])
                    dq_ref[:, pb * LANE:(pb + 1) * LANE] = _rope_bwd(dqp, cq, sq).astype(dq_ref.dtype)
            dsink = jnp.zeros((1, LANE), F32)
            for head in range(A_HEADS):
                dsink = jnp.where(lane_row == head, dsinks[head], dsink)
            dK, dV = [a.T for a in dKt], [a.T for a in dVt]
            prev = ([_rope_bwd(dK[g][:BLK], cp, sp) for g in range(A_KV)] + [dV[g][:BLK] for g in range(A_KV)])
            cur_ = ([_rope_bwd(dK[g][BLK:], cq, sq) for g in range(A_KV)] + [dV[g][BLK:] for g in range(A_KV)])
            dkv_ref[...] = (ckv_ref[...] + jnp.concatenate(prev, axis=1)).astype(dkv_ref.dtype)
            ckv_ref[...] = jnp.concatenate(cur_, axis=1)
            ds_ref[...] = ds_ref[...] + jnp.broadcast_to(dsink, ds_ref.shape)

        @pl.when(n == nb)
        def _():
            dkv_ref[...] = ckv_ref[...].astype(dkv_ref.dtype)

    last = nb - 1

    def cur(n):
        return jnp.minimum(n, last)

    def out_kv(n):
        return (jnp.maximum(n - 1, 0), 0)

    return pl.pallas_call(
        body, name="swa_bwd", grid=(nb + 1,),
        in_specs=[pl.BlockSpec(memory_space=pltpu.SMEM)] + _swa_in_specs(nb, last) + [
            pl.BlockSpec((BLK, D), lambda n: (cur(n), 0)),
            pl.BlockSpec((BLK, D), lambda n: (cur(n), 0)),
            pl.BlockSpec((BLK, LANE), lambda n: (cur(n), 0)),
            _any(),
        ],
        out_specs=[
            pl.BlockSpec((BLK, D), lambda n: (cur(n), 0)),
            pl.BlockSpec((BLK, 512), out_kv),
            pl.BlockSpec((8, LANE), lambda n: (0, 0)),
        ],
        out_shape=[
            jax.ShapeDtypeStruct((T, D), MXU),
            jax.ShapeDtypeStruct((T, 512), MXU),
            jax.ShapeDtypeStruct((8, LANE), F32),
        ],
        scratch_shapes=[pltpu.VMEM((BLK, 512), F32)],
        compiler_params=_cp(("arbitrary",)),
    )(sinks, proj, proj, proj, proj, proj, cos, sin, cos, sin, do_a, o_a, lse, after)


NCH = 4
GSTEP = NCH * CHUNK
ST_ROWS = B_HEADS * B_DV


def _chunk_rows(c):
    return slice(c * CHUNK, (c + 1) * CHUNK)


def _per_chunk(which, vals):
    out = vals[-1]
    for c in range(NCH - 2, -1, -1):
        out = jnp.where(which == c, vals[c], out)
    return out


def _gla_gate(bl_ref, gu_ref, bias_ref):
    gk = _dot(bl_ref[...].astype(MXU), gu_ref[...]) + bias_ref[...]
    la = (jnp.minimum(gk, 0.0) - jnp.log(1.0 + jnp.exp(-jnp.abs(gk)))) / TAU
    ri = lax.broadcasted_iota(jnp.int32, (GSTEP, GSTEP), 0)
    ci = lax.broadcasted_iota(jnp.int32, (GSTEP, GSTEP), 1)
    same = (ri // CHUNK) == (ci // CHUNK)
    lower, upper = same & (ci <= ri), same & (ci >= ri)
    b = _dot_f32(jnp.where(lower, 1.0, 0.0).astype(F32), la)
    which = lax.broadcasted_iota(jnp.int32, (GSTEP, 1), 0) // CHUNK
    return gk, la, b, lower, upper, which


def _gla_head(q_ref, k_ref, la, b, which, h):
    sl = slice(h * B_DK, (h + 1) * B_DK)
    bh, lah = b[:, sl], la[:, sl]
    bls = [jnp.sum(lah[_chunk_rows(c)], axis=0, keepdims=True) for c in range(NCH)]
    blast = _per_chunk(which, bls)
    qc = q_ref[:, sl] * (B_DK ** -0.5)
    kh = k_ref[:, sl]
    eb, enb, esb = jnp.exp(bh), jnp.exp(-bh), jnp.exp(blast - bh)
    return qc * eb, kh * enb, kh * esb, eb, enb, esb, [jnp.exp(v) for v in bls]


def _gla_specs(step_of):
    return [
        pl.BlockSpec((GSTEP, 512), lambda i: (step_of(i), C_BQ // 512)),
        pl.BlockSpec((GSTEP, 512), lambda i: (step_of(i), C_BK // 512)),
        pl.BlockSpec((GSTEP, D), lambda i: (step_of(i), C_BV // D)),
        pl.BlockSpec((GSTEP, W_BL), lambda i: (step_of(i), C_BL // W_BL)),
        pl.BlockSpec((W_BL, 512), lambda i: (0, 0)),
        pl.BlockSpec((1, 512), lambda i: (0, 0)),
    ]


def _gla_fwd(proj, gu_pad, bias):
    T = proj.shape[0]
    ns = T // GSTEP

    def body(q_ref, k_ref, v_ref, bl_ref, gu_ref, bias_ref, o_ref, st_ref, state_ref):
        @pl.when(pl.program_id(0) == 0)
        def _():
            state_ref[...] = jnp.zeros_like(state_ref)

        _, la, b, lower, _, which = _gla_gate(bl_ref, gu_ref, bias_ref)

        def within(h):
            q_e, k_e, k_s, _, _, _, decays = _gla_head(q_ref, k_ref, la, b, which, h)
            vh = v_ref[:, h * B_DV:(h + 1) * B_DV].astype(MXU)
            q_eb = q_e.astype(MXU)
            att = jnp.where(lower, _dot_nt(q_eb, k_e.astype(MXU)), 0.0)
            return vh, q_eb, k_s.astype(MXU), _dot(att.astype(MXU), vh), decays

        def across(h, vh, q_eb, k_sb, o_intra, decays):
            rows = slice(h * B_DV, (h + 1) * B_DV)
            s = state_ref[rows, :]
            outs = []
            for c in range(NCH):
                cr = _chunk_rows(c)
                st_ref[c * ST_ROWS + h * B_DV:c * ST_ROWS + (h + 1) * B_DV, :] = s
                outs.append(o_intra[cr] + _dot_nt(q_eb[cr], s.astype(MXU)))
                s = s * decays[c] + _dot_tn(vh[cr], k_sb[cr])
            state_ref[rows, :] = s
            o_ref[:, rows] = jnp.concatenate(outs, axis=0)

        for h in range(B_HEADS):
            across(h, *within(h))

    return pl.pallas_call(
        body, name="gla_fwd", grid=(ns,),
        in_specs=_gla_specs(lambda i: i),
        out_specs=[pl.BlockSpec((GSTEP, D), lambda i: (i, 0)),
                   pl.BlockSpec((NCH * ST_ROWS, B_DK), lambda i: (i, 0))],
        out_shape=[jax.ShapeDtypeStruct((T, D), F32),
                   jax.ShapeDtypeStruct((ns * NCH * ST_ROWS, B_DK), F32)],
        scratch_shapes=[pltpu.VMEM((ST_ROWS, B_DK), F32)],
        compiler_params=_cp(("arbitrary",)),
    )(proj, proj, proj, proj, gu_pad, bias)


def _gla_bwd(proj, gu_pad, bias, states, do_b):
    T = proj.shape[0]
    ns = T // GSTEP
    o_q, o_k = C_BQ - C_GLA, C_BK - C_GLA

    def body(q_ref, k_ref, v_ref, bl_ref, gu_ref, bias_ref, st_ref, do_ref,
             dg_ref, dbl_ref, ggu_ref, gbias_ref, gt_ref):
        @pl.when(pl.program_id(0) == 0)
        def _():
            gt_ref[...] = jnp.zeros_like(gt_ref)
            ggu_ref[...] = jnp.zeros_like(ggu_ref)
            gbias_ref[...] = jnp.zeros_like(gbias_ref)

        gk, la, b, lower, upper_mask, which = _gla_gate(bl_ref, gu_ref, bias_ref)
        upper = jnp.where(upper_mask, 1.0, 0.0).astype(F32)
        dla_parts = []

        def within(h):
            q_e, k_e, k_s, eb, enb, esb, decays = _gla_head(q_ref, k_ref, la, b, which, h)
            vh = v_ref[:, h * B_DV:(h + 1) * B_DV].astype(MXU)
            doh = do_ref[:, h * B_DV:(h + 1) * B_DV].astype(MXU)
            q_eb, k_eb = q_e.astype(MXU), k_e.astype(MXU)
            att = jnp.where(lower, _dot_nt(q_eb, k_eb), 0.0).astype(MXU)
            datt = jnp.where(lower, _dot_nt(doh, vh), 0.0).astype(MXU)
            return (q_e, k_e, k_s, eb, enb, esb, decays, vh, doh, q_eb, k_s.astype(MXU),
                    _dot(datt, k_eb), _dot_tn(datt, q_eb), _dot_tn(att, doh))

        def across(h, q_e, k_e, k_s, eb, enb, esb, decays, vh, doh, q_eb, k_sb, dq_i, dk_e, dv_i):
            rows = slice(h * B_DV, (h + 1) * B_DV)
            g = gt_ref[rows, :]
            dq_c, dks_c, dv_c, ddec = [None] * NCH, [None] * NCH, [None] * NCH, [None] * NCH
            for c in range(NCH - 1, -1, -1):
                cr = _chunk_rows(c)
                s = st_ref[c * ST_ROWS + h * B_DV:c * ST_ROWS + (h + 1) * B_DV, :]
                gb = g.astype(MXU)
                dq_c[c] = dq_i[cr] + _dot(doh[cr], s.astype(MXU))
                dks_c[c] = _dot(vh[cr], gb)
                dv_c[c] = dv_i[cr] + _dot_nt(k_sb[cr], gb)
                ddec[c] = jnp.sum(g * s, axis=0, keepdims=True)
                g = g * decays[c] + _dot_tn(doh[cr], q_eb[cr])
            gt_ref[rows, :] = g
            dq_e = jnp.concatenate(dq_c, axis=0)
            dk_s = jnp.concatenate(dks_c, axis=0)
            dg_ref[:, rows] = jnp.concatenate(dv_c, axis=0).astype(dg_ref.dtype)
            dg_ref[:, o_q + h * B_DK:o_q + (h + 1) * B_DK] = (dq_e * eb * (B_DK ** -0.5)).astype(dg_ref.dtype)
            dg_ref[:, o_k + h * B_DK:o_k + (h + 1) * B_DK] = (dk_e * enb + dk_s * esb).astype(dg_ref.dtype)
            dks_ks = dk_s * k_s
            db = dq_e * q_e - dk_e * k_e - dks_ks
            dbl = [jnp.sum(dks_ks[_chunk_rows(c)], axis=0, keepdims=True) + ddec[c] * decays[c] for c in range(NCH)]
            dla_parts.append(_dot_f32(upper, db) + _per_chunk(which, dbl))

        for h in range(B_HEADS):
            across(h, *within(h))
        dla = jnp.concatenate(dla_parts, axis=1)
        dgk = dla * (1.0 / TAU) * _sigmoid(-gk)
        dgkb = dgk.astype(MXU)
        dbl_ref[...] = _dot_nt(dgkb, gu_ref[...]).astype(dbl_ref.dtype)
        ggu_ref[...] = ggu_ref[...] + _dot_tn(bl_ref[...].astype(MXU), dgkb)
        gbias_ref[...] = gbias_ref[...] + jnp.broadcast_to(jnp.sum(dgk, axis=0, keepdims=True), gbias_ref.shape)

    def rev(i):
        return ns - 1 - i

    return pl.pallas_call(
        body, name="gla_bwd", grid=(ns,),
        in_specs=_gla_specs(rev) + [
            pl.BlockSpec((NCH * ST_ROWS, B_DK), lambda i: (rev(i), 0)),
            pl.BlockSpec((GSTEP, D), lambda i: (rev(i), 0)),
        ],
        out_specs=[
            pl.BlockSpec((GSTEP, W_GLA), lambda i: (rev(i), 0)),
            pl.BlockSpec((GSTEP, W_BL), lambda i: (rev(i), 0)),
            pl.BlockSpec((W_BL, 512), lambda i: (0, 0)),
            pl.BlockSpec((8, 512), lambda i: (0, 0)),
        ],
        out_shape=[
            jax.ShapeDtypeStruct((T, W_GLA), MXU),
            jax.ShapeDtypeStruct((T, W_BL), MXU),
            jax.ShapeDtypeStruct((W_BL, 512), F32),
            jax.ShapeDtypeStruct((8, 512), F32),
        ],
        scratch_shapes=[pltpu.VMEM((B_HEADS * B_DV, B_DK), F32)],
        compiler_params=_cp(("arbitrary",)),
    )(proj, proj, proj, proj, gu_pad, bias, states, do_b)


def _mid(x, target, proj, o_a, o_b, late, w_bn4, fnw):
    T = x.shape[0]
    tT = min(T, 128)
    nbuf = 4
    o_ag, o_bg, o_ma, o_mb = (c - C_GATES for c in (C_AG, C_BG, C_MA, C_MB))

    def body(x_ref, t_ref, oa_ref, ob_ref, gates_ref, late_ref, wbn_ref, fnw_ref,
             dx2_ref, doa_ref, dob_ref, dgates_ref,
             tail0_ref, tail1_ref, gfn_ref, gbn_ref, loss_ref, buf_ref, gw_ref):
        i = pl.program_id(0)

        def weight(p):
            return late_ref[:, 128 * p:128 * (p + 1), :].reshape(D, D)

        @pl.when(i == 0)
        def _():
            for r in (gw_ref, gfn_ref, gbn_ref, loss_ref):
                r[...] = jnp.zeros_like(r)

        rows = pl.ds(pl.multiple_of((i % nbuf) * tT, tT), tT)

        def keep(k, val):
            buf_ref[k, rows, :] = val

        oa, ag = oa_ref[...], gates_ref[:, o_ag:o_ag + D]
        sg_a = _sigmoid(ag)
        silu_a = ag * sg_a
        oag_b = (oa * silu_a).astype(MXU)
        keep(0, oag_b)
        y_a = _dot(oag_b, weight(0))

        ob, bg = ob_ref[...], gates_ref[:, o_bg:o_bg + D]
        rbs, obhats = [], []
        for h in range(B_HEADS):
            obh = ob[:, h * B_DV:(h + 1) * B_DV]
            rb = lax.rsqrt(jnp.mean(obh * obh, axis=-1, keepdims=True) + EPS)
            rbs.append(rb)
            obhats.append(obh * rb)
        obhat = jnp.concatenate(obhats, axis=1)
        wbn = wbn_ref[...]
        obn = obhat * wbn
        sg_b = _sigmoid(bg)
        silu_b = bg * sg_b
        obg_b = (obn * silu_b).astype(MXU)
        keep(1, obg_b)
        y_b = _dot(obg_b, weight(1))

        sa, sb = _sigmoid(gates_ref[:, o_ma:o_ma + D]), _sigmoid(gates_ref[:, o_mb:o_mb + D])
        mg_b = (sa * y_a + sb * y_b).astype(MXU)
        keep(2, mg_b)
        x2 = x_ref[...] + _dot(mg_b, weight(2))
        r2 = lax.rsqrt(jnp.mean(x2 * x2, axis=-1, keepdims=True) + EPS)
        xh2 = x2 * r2
        fw = fnw_ref[...]
        err = xh2 * fw - t_ref[...]
        tok = jnp.mean(err * err, axis=-1, keepdims=True)
        loss_ref[...] = loss_ref[...] + 0.5 * jnp.sum(tok, axis=0, keepdims=True)

        dy = err * (1.0 / D)
        gfn_ref[...] = gfn_ref[...] + jnp.broadcast_to(jnp.sum(dy * xh2, axis=0, keepdims=True), gfn_ref.shape)
        gy = dy * fw
        dx2 = r2 * (gy - xh2 * jnp.mean(gy * xh2, axis=-1, keepdims=True))
        dx2_ref[...] = dx2
        dx2_b = dx2.astype(MXU)
        keep(5, dx2_b)
        dmg = _dot_nt(dx2_b, weight(2))

        dgates_ref[:, o_ma:o_ma + D] = (dmg * y_a * sa * (1.0 - sa)).astype(dgates_ref.dtype)
        dgates_ref[:, o_mb:o_mb + D] = (dmg * y_b * sb * (1.0 - sb)).astype(dgates_ref.dtype)
        dya_b = (dmg * sa).astype(MXU)
        dyb_b = (dmg * sb).astype(MXU)
        keep(3, dya_b)
        keep(4, dyb_b)
        doag = _dot_nt(dya_b, weight(0))
        dobg = _dot_nt(dyb_b, weight(1))

        @pl.when(i % nbuf == nbuf - 1)
        def _():
            for p in range(3):
                gw_ref[p] = gw_ref[p] + _dot_tn(buf_ref[p], buf_ref[3 + p])

        @pl.when(i == pl.num_programs(0) - 1)
        def _():
            for hf, tail_ref in enumerate((tail0_ref, tail1_ref)):
                for d in range(NDEV):
                    for p in range(3):
                        tail_ref[d, 128 * p:128 * (p + 1), :] = (
                            gw_ref[p, 128 * d:128 * (d + 1), hf * DH:(hf + 1) * DH].astype(tail_ref.dtype))

        doa_ref[...] = doag * silu_a
        dgates_ref[:, o_ag:o_ag + D] = (doag * oa * (sg_a * (1.0 + ag * (1.0 - sg_a)))).astype(dgates_ref.dtype)
        dobn = dobg * silu_b
        dgates_ref[:, o_bg:o_bg + D] = (dobg * obn * (sg_b * (1.0 + bg * (1.0 - sg_b)))).astype(dgates_ref.dtype)
        gg = dobn * wbn
        gbn = jnp.zeros((1, B_DV), F32)
        for h in range(B_HEADS):
            sl = slice(h * B_DV, (h + 1) * B_DV)
            gbn = gbn + jnp.sum(dobn[:, sl] * obhats[h], axis=0, keepdims=True)
            ggh = gg[:, sl]
            dob_ref[:, sl] = rbs[h] * (ggh - obhats[h] * jnp.mean(ggh * obhats[h], axis=-1, keepdims=True))
        gbn_ref[...] = gbn_ref[...] + jnp.broadcast_to(gbn, gbn_ref.shape)

    assert (T // tT) % nbuf == 0
    tile = pl.BlockSpec((tT, D), lambda i: (i, 0))
    row = pl.BlockSpec((1, D), lambda i: (0, 0))
    acc8 = pl.BlockSpec((8, D), lambda i: (0, 0))
    return pl.pallas_call(
        body, name="mid", grid=(T // tT,),
        in_specs=[tile, tile, tile, tile, pl.BlockSpec((tT, W_GATES), lambda i: (i, C_GATES // W_GATES)),
                  _vmem(), row, row],
        out_specs=[tile, tile, tile, pl.BlockSpec((tT, W_GATES), lambda i: (i, 0)), _vmem(), _vmem(),
                   acc8, pl.BlockSpec((8, B_DV), lambda i: (0, 0)), pl.BlockSpec((8, LANE), lambda i: (0, 0))],
        out_shape=[
            jax.ShapeDtypeStruct((T, D), F32),
            jax.ShapeDtypeStruct((T, D), F32),
            jax.ShapeDtypeStruct((T, D), F32),
            jax.ShapeDtypeStruct((T, W_GATES), MXU),
            jax.ShapeDtypeStruct((NDEV, 384, DH), WIRE),
            jax.ShapeDtypeStruct((NDEV, 384, DH), WIRE),
            jax.ShapeDtypeStruct((8, D), F32),
            jax.ShapeDtypeStruct((8, B_DV), F32),
            jax.ShapeDtypeStruct((8, LANE), F32),
        ],
        scratch_shapes=[pltpu.VMEM((6, nbuf * tT, D), MXU), pltpu.VMEM((3, D, D), F32)],
        compiler_params=_cp(("arbitrary",)),
    )(x, target, o_a, o_b, proj, late, w_bn4, fnw)


DH = D // 2


_GW_TILES = (("q", 0, 512, 0), ("q", 1, 512, 512), ("kv", 0, 256, 1024), ("bl", 0, RANK, 5376),
             ("gla", 0, 512, 3328), ("gla", 1, 512, 3840), ("gla", 2, 512, 2304), ("gla", 3, 512, 2816),
             ("gates", 0, 512, 1280), ("gates", 1, 512, 1792), ("gates", 2, 512, 4352), ("gates", 3, 512, 4864),
             ("gates", 4, 512, 5392), ("gates", 5, 512, 5904), ("gates", 6, 512, 6416), ("gates", 7, 512, 6928))


def _gw_unpermute(piece, t):
    if piece == "q":
        parts = []
        for blk in range(t.shape[0] // LANE):
            g = [t[blk * LANE + 32 * i:blk * LANE + 32 * (i + 1)] for i in range(4)]
            parts += [g[0], g[2], g[1], g[3]]
        return jnp.concatenate(parts, axis=0)
    if piece == "kv":
        k = [t[64 * i:64 * i + 32] + t[64 * i + 32:64 * i + 64] for i in range(4)]
        v = [t[256 + 128 * g:256 + 128 * g + 64] + t[256 + 128 * g + 64:256 + 128 * (g + 1)] for g in range(2)]
        return jnp.concatenate(k + v, axis=0)
    if piece == "bl":
        return t[:RANK]
    return t


def _gw_half(h, pieces, half, after=None):
    T = h.shape[0]
    steps = len(_GW_TILES)

    def body(*refs):
        h_ref = refs[0]
        srcs = dict(zip(("q", "kv", "bl", "gla", "gates"), refs[1:6]))
        o_ref, stage, sems = refs[-3:]
        j = pl.program_id(0)

        def out_copy(k):
            _, _, n, off = _GW_TILES[k]
            return pltpu.make_async_copy(stage.at[k % 2, 0:n], o_ref.at[pl.ds(off, n)], sems.at[k % 2])

        for k, (piece, _, n, _) in enumerate(_GW_TILES):
            @pl.when(j == k)
            def _(k=k, piece=piece, n=n):
                if k >= 2:
                    out_copy(k - 2).wait()
                t = _gw_unpermute(piece, _dot_tn(srcs[piece][...], h_ref[...]))
                stage[k % 2, 0:n, :] = t.astype(stage.dtype)
                out_copy(k).start()

        @pl.when(j == steps - 1)
        def _():
            out_copy(steps - 2).wait()
            out_copy(steps - 1).wait()

    def tile_of(lo, hi):
        return lambda j: (0, jnp.clip(j - lo, 0, hi - lo - 1))

    in_specs = [pl.BlockSpec((T, DH), lambda j: (0, half)),
                pl.BlockSpec((T, 512), tile_of(0, 2)), pl.BlockSpec((T, 512), lambda j: (0, 0)),
                pl.BlockSpec((T, W_BL), lambda j: (0, 0)),
                pl.BlockSpec((T, 512), tile_of(4, 8)), pl.BlockSpec((T, 512), tile_of(8, 16))]
    args = [h, *pieces]
    if after is not None:
        in_specs.append(_any())
        args.append(after)
    return pl.pallas_call(
        body, name=f"gw_in_half{half}", grid=(steps,),
        in_specs=in_specs, out_specs=_any(),
        out_shape=jax.ShapeDtypeStruct((IN_WIDTH, DH), WIRE),
        scratch_shapes=[pltpu.VMEM((2, 512, DH), WIRE), pltpu.SemaphoreType.DMA((2,))],
        compiler_params=_cp(("arbitrary",)),
    )(*args)


def _chip_copies(s_ref, got_ref, send_sems, recv_sems):
    x, y, c = _place()
    chips = [(1 - x, y), (x, 1 - y), (1 - x, 1 - y)]
    return [pltpu.make_async_remote_copy(
        src_ref=s_ref.at[2 * px + py], dst_ref=got_ref.at[j],
        send_sem=send_sems.at[j], recv_sem=recv_sems.at[j], device_id=(px, py, c), device_id_type=MESH)
        for j, (px, py) in enumerate(chips)]


_EFFECT = pltpu.SideEffectType.DATAFLOW_SIDE_EFFECTING


def _hbm():
    return pl.BlockSpec(memory_space=pltpu.HBM)


def _sem():
    return pl.BlockSpec(memory_space=pltpu.SEMAPHORE)


def _chip_start(sums, half):
    land = pltpu.with_memory_space_constraint(lax.empty((3,) + sums.shape[1:], sums.dtype), pltpu.HBM)

    def body(s_ref, land_ref, send_sems, recv_sems, s_thru, land_thru, token):
        for cp in _chip_copies(s_ref, land_ref, send_sems, recv_sems):
            cp.start()
        token[...] = jnp.zeros_like(token)

    return pl.pallas_call(
        body, name=f"chip_start{half}",
        out_shape=(pltpu.SemaphoreType.DMA((3,)), pltpu.SemaphoreType.DMA((3,)),
                   pltpu.HBM(sums.shape, sums.dtype), pltpu.HBM(land.shape, land.dtype),
                   jax.ShapeDtypeStruct((8, LANE), F32)),
        in_specs=(_hbm(), _hbm()), out_specs=(_sem(), _sem(), _hbm(), _hbm(), _vmem()),
        input_output_aliases={0: 2, 1: 3},
        compiler_params=pltpu.CompilerParams(has_side_effects=_EFFECT),
    )(pltpu.with_memory_space_constraint(sums, pltpu.HBM), land)


def _chip_wait(send_sems, recv_sems, s_thru, land_thru, after, half):
    def body(s_ref, land_ref, send_sems, recv_sems, after_ref, s_out, got_ref):
        copies = _chip_copies(s_ref, land_ref, send_sems, recv_sems)
        for cp in copies:
            cp.wait_send()
        for cp in copies:
            cp.wait_recv()

    return pl.pallas_call(
        body, name=f"chip_wait{half}",
        out_shape=(pltpu.HBM(s_thru.shape, s_thru.dtype), pltpu.HBM(land_thru.shape, land_thru.dtype)),
        in_specs=(_hbm(), _hbm(), _sem(), _sem(), _any()), out_specs=(_hbm(), _hbm()),
        input_output_aliases={0: 0, 1: 1},
        compiler_params=pltpu.CompilerParams(has_side_effects=_EFFECT),
    )(s_thru, land_thru, send_sems, recv_sems, after)


def _dh_norm(pieces, offsets, wf, x, dx2, norm_w, after):
    T = x.shape[0]
    tT = min(T, 256)
    widths = [p.shape[1] for p in pieces]
    npc = len(pieces)

    def body(*refs):
        dp_refs = refs[:npc]
        wf_ref, x_ref, dx2_ref, nw_ref, _, gx_ref, gnw_ref = refs[npc:]

        @pl.when(pl.program_id(0) == 0)
        def _():
            gnw_ref[...] = jnp.zeros_like(gnw_ref)

        dh = jnp.zeros((tT, D), F32)
        for dp_ref, off, w in zip(dp_refs, offsets, widths):
            dh = dh + _dot(dp_ref[...], wf_ref[off:off + w, :])
        xv = x_ref[...]
        r = lax.rsqrt(jnp.mean(xv * xv, axis=-1, keepdims=True) + EPS)
        xh = xv * r
        gnw_ref[...] = gnw_ref[...] + jnp.broadcast_to(jnp.sum(dh * xh, axis=0, keepdims=True), gnw_ref.shape)
        g = dh * nw_ref[...]
        gx_ref[...] = r * (g - xh * jnp.mean(g * xh, axis=-1, keepdims=True)) + dx2_ref[...]

    tile = pl.BlockSpec((tT, D), lambda i: (i, 0))
    return pl.pallas_call(
        body, name="dh_norm", grid=(T // tT,),
        in_specs=[pl.BlockSpec((tT, w), lambda i: (i, 0)) for w in widths]
        + [_vmem(), tile, tile, pl.BlockSpec((1, D), lambda i: (0, 0)), _any()],
        out_specs=[tile, pl.BlockSpec((8, D), lambda i: (0, 0))],
        out_shape=[jax.ShapeDtypeStruct((T, D), F32), jax.ShapeDtypeStruct((8, D), F32)],
        compiler_params=_cp(("arbitrary",)),
    )(*pieces, wf, x, dx2, norm_w, after)


def _adamw_math(w, g, m, v):
    m = ADAM_B1 * m + (1.0 - ADAM_B1) * g
    v = ADAM_B2 * v + (1.0 - ADAM_B2) * (g * g)
    m_hat = m * (1.0 / (1.0 - ADAM_B1 ** ADAM_STEP))
    v_hat = v * (1.0 / (1.0 - ADAM_B2 ** ADAM_STEP))
    delta = -ADAM_LR * (m_hat / (jnp.sqrt(v_hat) + ADAM_EPS) + ADAM_WD * w)
    return delta, m, v


def _fetch_partials(s_ref, got_ref, buf, sems):
    x, y, _ = _place()
    cps = [pltpu.make_async_copy(s_ref.at[2 * x + y], buf.at[0], sems.at[0])]
    cps += [pltpu.make_async_copy(got_ref.at[j], buf.at[1 + j], sems.at[1 + j]) for j in range(3)]
    for cp in cps:
        cp.start()
    for cp in cps:
        cp.wait()


SMALL_AT = dict(norm_w=0, fnw=8, bias=16, bn=24, sinks=32, loss=40)
ROW_AT = (R_IN, R_A, R_B, R_O)


def _finish_small(ws, ms, vs, smalls):
    names = ["norm_w", "fnw", "bias", "bn", "sinks"]
    widths = [ws[n].shape[1] for n in names]

    def body(*refs):
        w_refs, m_refs, v_refs = refs[0:5], refs[5:10], refs[10:15]
        smalls_ref, loss_ref = refs[15], refs[16]
        outs, tot = refs[17:37], refs[37]
        acc = smalls_ref[0]
        for d in range(1, NDEV):
            acc = acc + smalls_ref[d]
        tot[...] = acc
        loss_ref[...] = tot[SMALL_AT["loss"]:SMALL_AT["loss"] + 1, 0:1]
        for p, (nm_, wd) in enumerate(zip(names, widths)):
            r = SMALL_AT[nm_]
            g = tot[r:r + 1, 0:wd]
            d, nm, nv = _adamw_math(w_refs[p][...], g, m_refs[p][...], v_refs[p][...])
            for o, val in zip(outs[4 * p:4 * p + 4], (g, d, nm, nv)):
                o[...] = val

    res = pl.pallas_call(
        body, name="finish_small",
        in_specs=[_vmem()] * 16, out_specs=[_vmem()] * 21,
        out_shape=[jax.ShapeDtypeStruct((1, 1), F32)]
        + [jax.ShapeDtypeStruct((1, wd), F32) for wd in widths for _ in range(4)],
        scratch_shapes=[pltpu.VMEM((SMALL_ROWS, D), F32)],
        compiler_params=_cp(),
    )(*[ws[n] for n in names], *[ms[n] for n in names], *[vs[n] for n in names], smalls)
    return res[0], {n: tuple(res[1 + 4 * p:5 + 4 * p]) for p, n in enumerate(names)}


def _finish(w_rows, m_rows, v_rows, gu_w, gu_m, gu_v, sums, got):
    shapes = [(SHARD, 1, D)] + [w.shape for w in w_rows[1:]]

    row_block = 64

    def columns(ref, p, cols, r0, n):
        if p:
            return ref, (slice(r0, r0 + n), cols)
        flat = ref if ref.shape == (SHARD * LANE_TILES, LANE) else ref.reshape(SHARD * LANE_TILES, LANE)
        return flat, (pl.ds(cols.start // LANE + LANE_TILES * r0, n, stride=LANE_TILES), slice(None))

    def read(ref, p, cols, r0, n):
        ref, at = columns(ref, p, cols, r0, n)
        return ref[at]

    def body(*refs):
        wr_refs, mr_refs, vr_refs = refs[0:4], refs[4:8], refs[8:12]
        guw_ref, gum_ref, guv_ref = refs[12:15]
        s_refs, got_refs = refs[15:17], refs[17:19]
        row_outs = refs[19:35]
        gu_outs = refs[35:39]
        buf, gsh, sems, big, big_sems = refs[39:]
        loads = [pltpu.make_async_copy(r[0], big.at[k], big_sems.at[k]) for k, r in enumerate((wr_refs, mr_refs, vr_refs))]
        for cp in loads:
            cp.start()
        wr_refs, mr_refs, vr_refs = ((big.at[k],) + tuple(r[1:]) for k, r in enumerate((wr_refs, mr_refs, vr_refs)))
        x, y, c = _place()
        me_slot = 4 * x + 2 * y + c
        down = 2 * me_slot

        def total(rows, cols):
            g = buf[0, rows, cols].astype(F32)
            for j in range(1, 4):
                g = g + buf[j, rows, cols].astype(F32)
            return g

        def update(p, grad, cols):
            nrows = shapes[p][0]
            for r0 in range(0, nrows, row_block):
                n = min(row_block, nrows - r0)
                g = grad(r0, n)
                d, nm, nv = _adamw_math(read(wr_refs[p], p, cols, r0, n), g, read(mr_refs[p], p, cols, r0, n),
                                        read(vr_refs[p], p, cols, r0, n))
                for o, val in zip(row_outs[4 * p:4 * p + 4], (g, d, nm, nv)):
                    o, at = columns(o, p, cols, r0, n)
                    o[at] = val

        for hf in range(2):
            _fetch_partials(s_refs[hf], got_refs[hf], buf, sems)
            for cc in range(DH // LANE):
                src = slice(cc * LANE, (cc + 1) * LANE)
                cols = slice(hf * DH + cc * LANE, hf * DH + (cc + 1) * LANE)
                for r0 in range(0, SHARD_PAD, row_block):
                    rows = slice(r0, min(r0 + row_block, SHARD_PAD))
                    gsh[rows, :] = total(rows, src)
                if hf == 0 and cc == 0:
                    for cp in loads:
                        cp.wait()
                update(0, lambda r0, n: gsh[pl.ds(down + r0, n), :], cols)
                for p in range(1, 4):
                    update(p, lambda r0, n, p=p: total(slice(ROW_AT[p] + r0, ROW_AT[p] + r0 + n), src), cols)
            if hf == 0:
                g = total(slice(R_GU, R_GU + RANK), slice(0, 64))
                d, nm, nv = _adamw_math(guw_ref[...], g, gum_ref[...], guv_ref[...])
                for o, val in zip(gu_outs, (g, d, nm, nv)):
                    o[...] = val

    res = pl.pallas_call(
        body, name="finish",
        in_specs=([_any()] + [_vmem()] * 3) * 3 + [_vmem()] * 3 + [_any()] * 4,
        out_specs=[_vmem()] * 20,
        out_shape=[jax.ShapeDtypeStruct(s, F32) for s in shapes for _ in range(4)]
        + [jax.ShapeDtypeStruct((RANK, 64), F32)] * 4,
        scratch_shapes=[pltpu.VMEM((4, ROWS, DH), sums[0].dtype), pltpu.VMEM((SHARD_PAD, LANE), F32),
                        pltpu.SemaphoreType.DMA((4,)),
                        pltpu.VMEM((3, SHARD * LANE_TILES, LANE), F32), pltpu.SemaphoreType.DMA((3,))],
        compiler_params=_cp(),
    )(*w_rows, *m_rows, *v_rows, gu_w, gu_m, gu_v, *sums, *got)
    return tuple(res[0:16]), tuple(res[16:20])


def _place():
    x, y, c = lax.axis_index("x"), lax.axis_index("y"), lax.axis_index("c")
    return x, y, c


def _peers(x, y, c):
    return [(x ^ dx, y ^ dy, c ^ dc) for dx in range(2) for dy in range(2) for dc in range(2) if dx + dy + dc]


def _late_gather_start(blk, after, name="late_gather"):
    land = pltpu.with_memory_space_constraint(lax.empty((NDEV,) + blk.shape, blk.dtype), pltpu.HBM)

    def body(b_ref, land_ref, after_ref, send_sems, recv_sems, b_thru, land_thru, token):
        x, y, c = _place()
        for k, to in enumerate(_peers(x, y, c)):
            pltpu.make_async_remote_copy(
                src_ref=b_ref, dst_ref=land_ref.at[4 * x + 2 * y + c], send_sem=send_sems.at[k],
                recv_sem=recv_sems.at[k], device_id=to, device_id_type=MESH).start()
        token[...] = jnp.zeros_like(token)

    return pl.pallas_call(
        body, name=name + "_start",
        out_shape=(pltpu.SemaphoreType.DMA((7,)), pltpu.SemaphoreType.DMA((7,)),
                   pltpu.HBM(blk.shape, blk.dtype), pltpu.HBM(land.shape, land.dtype),
                   jax.ShapeDtypeStruct((8, LANE), F32)),
        in_specs=(_hbm(), _hbm(), _any()), out_specs=(_sem(), _sem(), _hbm(), _hbm(), _vmem()),
        input_output_aliases={0: 2, 1: 3},
        compiler_params=pltpu.CompilerParams(has_side_effects=_EFFECT),
    )(pltpu.with_memory_space_constraint(blk, pltpu.HBM), land, after)


def _late_gather_wait(send_sems, recv_sems, b_thru, land_thru, after, after2, name="late_gather"):
    def body(b_ref, land_ref, send_sems, recv_sems, after_ref, after2_ref, b_out, got_ref):
        x, y, c = _place()
        copies = [pltpu.make_async_remote_copy(
            src_ref=b_ref, dst_ref=land_ref.at[4 * x + 2 * y + c], send_sem=send_sems.at[k],
            recv_sem=recv_sems.at[k], device_id=to, device_id_type=MESH)
            for k, to in enumerate(_peers(x, y, c))]
        for cp in copies:
            cp.wait_send()
        for cp in copies:
            cp.wait_recv()

    return pl.pallas_call(
        body, name=name + "_wait",
        out_shape=(pltpu.HBM(b_thru.shape, b_thru.dtype), pltpu.HBM(land_thru.shape, land_thru.dtype)),
        in_specs=(_hbm(), _hbm(), _sem(), _sem(), _any(), _any()), out_specs=(_hbm(), _hbm()),
        input_output_aliases={0: 0, 1: 1},
        compiler_params=pltpu.CompilerParams(has_side_effects=_EFFECT),
    )(b_thru, land_thru, send_sems, recv_sems, after, after2)


G_ROWS = SHARD_PAD + RANK


def _gather_blocks(w_in_t, gu_s, xs, norm_w, pos_col):
    rows, cols = G_ROWS, D
    T = xs.shape[0]
    tT = min(T, 256)
    inv_row, sign_row = _rope_rows()

    def body(wi_ref, gu_ref, xs_hbm, nw_ref, pos_ref, inv_ref, sign_ref,
             out_ref, h_ref, cos_ref, sin_ref, x_ref, frame_ref, xs_ref, send_sems, recv_sems, local_sem, xs_sem):
        load_xs = pltpu.make_async_copy(xs_hbm, xs_ref, xs_sem)
        load_xs.start()
        x, y, c = _place()
        me, sibling = (x, y, c), (x, y, 1 - c)
        chips = [(1 - x, y), (x, 1 - y), (1 - x, 1 - y)]
        shift = 2 * (4 * x + 2 * y + c)
        frame_ref[SHARD - SHARD % 8:, :] = jnp.zeros((SHARD_PAD - SHARD + SHARD % 8, LANE), F32)
        for cc in range(LANE_TILES):
            cs = slice(cc * LANE, (cc + 1) * LANE)
            frame_ref[:SHARD, :] = wi_ref[pl.ds(cc, SHARD, stride=LANE_TILES), :]
            x_ref[0:SHARD_PAD, cs] = pltpu.roll(frame_ref[...], shift, 0).astype(x_ref.dtype)
        x_ref[SHARD_PAD:G_ROWS, :] = jnp.zeros((RANK, D), x_ref.dtype)
        x_ref[SHARD_PAD:G_ROWS, 0:64] = gu_ref[...].astype(x_ref.dtype)

        def slot(px, py, pc):
            return out_ref.at[4 * px + 2 * py + pc]

        def copy(k, block, to, src=None):
            return pltpu.make_async_remote_copy(
                src_ref=slot(*block) if src is None else src, dst_ref=slot(*block),
                send_sem=send_sems.at[k], recv_sem=recv_sems.at[k], device_id=to, device_id_type=MESH)

        mine = pltpu.make_async_copy(x_ref, slot(*me), local_sem)
        mine.start()
        first = [copy(0, me, sibling, src=x_ref)]
        first += [copy(1 + j, me, (*chip, c), src=x_ref) for j, chip in enumerate(chips)]
        for cp in first:
            cp.start()
        load_xs.wait()

        @pl.loop(0, T // tT)
        def _(i):
            rows_i = pl.ds(pl.multiple_of(i * tT, tT), tT)
            _prologue_rows(rows_i, xs_ref, nw_ref, pos_ref, inv_ref, sign_ref, h_ref, cos_ref, sin_ref)

        passed = [copy(4 + j, (*chip, c), sibling) for j, chip in enumerate(chips)]
        for j, chip in enumerate(chips):
            copy(1 + j, (*chip, c), me).wait_recv()
            passed[j].start()
        copy(0, sibling, me).wait_recv()
        for j, chip in enumerate(chips):
            copy(4 + j, (*chip, 1 - c), me).wait_recv()
        for cp in first + passed:
            cp.wait_send()
        mine.wait()

    return pl.pallas_call(
        body, name="gather_weights",
        in_specs=[_vmem(), _vmem(), _any()] + [_vmem()] * 4, out_specs=[_any()] + [_vmem()] * 3,
        out_shape=[jax.ShapeDtypeStruct((NDEV, rows, cols), WIRE), jax.ShapeDtypeStruct((T, D), MXU),
                   jax.ShapeDtypeStruct((T, LANE), F32), jax.ShapeDtypeStruct((T, LANE), F32)],
        scratch_shapes=[pltpu.VMEM((rows, cols), WIRE), pltpu.VMEM((SHARD_PAD, LANE), F32), pltpu.VMEM((T, D), F32),
                        pltpu.SemaphoreType.DMA((7,)), pltpu.SemaphoreType.DMA((7,)), pltpu.SemaphoreType.DMA,
                        pltpu.SemaphoreType.DMA],
        compiler_params=_cp(),
    )(w_in_t, gu_s, xs, norm_w, pos_col, inv_row, sign_row)


def _pair_reduce(gwt, tails, half):
    n = gwt.shape[1]
    starts = [SHARD_PAD]
    for t in tails:
        starts.append(starts[-1] + t.shape[1])
    rows = starts[-1]
    blk = (4, rows, n)
    npart = 1 + len(tails)

    def body(*refs):
        g_ref, t_refs = refs[0], refs[1:npart]
        out_ref, acc, got, own, send_sems, recv_sems, own_sems, out_sems = refs[npart:]
        x, y, c = _place()

        def parts(d, dst):
            frame = g_ref.at[pl.ds(pl.multiple_of(FRAME * d, 16), SHARD_PAD)]
            return [(frame, dst.at[0:SHARD_PAD])] + [
                (t_ref.at[d], dst.at[starts[k]:starts[k + 1]]) for k, t_ref in enumerate(t_refs)]

        sends, loads, stores = [], [], []
        for chip in range(4):
            sends.append([pltpu.make_async_remote_copy(
                src_ref=s, dst_ref=d_, send_sem=send_sems.at[chip, k], recv_sem=recv_sems.at[chip, k],
                device_id=(x, y, 1 - c), device_id_type=MESH)
                for k, (s, d_) in enumerate(parts(2 * chip + (1 - c), got.at[chip]))])
            loads.append([pltpu.make_async_copy(s, d_, own_sems.at[chip, k])
                          for k, (s, d_) in enumerate(parts(2 * chip + c, own.at[chip]))])
            stores.append(pltpu.make_async_copy(acc.at[chip], out_ref.at[chip], out_sems.at[chip]))
        for group in sends + loads:
            for cp in group:
                cp.start()
        for chip in range(4):
            for cp in loads[chip]:
                cp.wait()
            for cp in sends[chip]:
                cp.wait_recv()
            acc[chip] = (own[chip].astype(F32) + got[chip].astype(F32)).astype(acc.dtype)
            stores[chip].start()
        for cp in stores:
            cp.wait()
        for group in sends:
            for cp in group:
                cp.wait_send()

    return pl.pallas_call(
        body, name=f"pair_reduce{half}",
        in_specs=[_any()] * npart, out_specs=_any(),
        out_shape=jax.ShapeDtypeStruct(blk, gwt.dtype),
        scratch_shapes=[pltpu.VMEM(blk, gwt.dtype), pltpu.VMEM(blk, gwt.dtype), pltpu.VMEM(blk, gwt.dtype),
                        pltpu.SemaphoreType.DMA((4, npart)), pltpu.SemaphoreType.DMA((4, npart)),
                        pltpu.SemaphoreType.DMA((4, npart)), pltpu.SemaphoreType.DMA((4,))],
        compiler_params=_cp(),
    )(gwt, *tails)


def _pad_cols(a, cols):
    return jnp.pad(a, ((0, 0), (0, cols - a.shape[1])))


def _pad_rows(a, rows):
    return jnp.pad(a, ((0, rows - a.shape[0]), (0, 0)))


FRAME = 928


def _wft_plan():
    moves = []
    for blk in range(8):
        for half in range(2):
            for sub in range(2):
                moves.append((C_Q + 128 * blk + 32 * (2 * half + sub), 128 * blk + 32 * (2 * sub + half), 32))
    for idx in range(4):
        for dup in range(2):
            moves.append((C_KD + 64 * idx + 32 * dup, 1024 + 32 * idx, 32))
    for g in range(2):
        for dup in range(2):
            moves.append((C_VD + 128 * g + 64 * dup, 1152 + 64 * g, 64))
    moves += [(C_BL, 5376, RANK), (C_BV, 3328, 1024), (C_BQ, 2304, 512), (C_BK, 2816, 512),
              (C_AG, 1280, 1024), (C_BG, 4352, 1024), (C_MA, 5392, 1024), (C_MB, 6416, 1024)]
    bulk, seams = [], []
    for dst, src, n in moves:
        r = src
        while r < src + n:
            f = min(r // FRAME, NDEV - 1)
            local = r - FRAME * f
            if f > 0 and local < 16:
                assert local == 0
                seams.append((f, dst + r - src))
                step = 16
            else:
                step = min(src + n, FRAME * (f + 1) if f < NDEV - 1 else IN_WIDTH) - r
                bulk.append((f, local, dst + r - src, step))
            r += step
    assert sorted(f for f, _ in seams) == list(range(1, NDEV))
    return bulk, seams, [(C_BL + RANK, C_GLA - C_BL - RANK)]


def _build_wft_copies(frames):
    bulk, seams, zeros = _wft_plan()
    (z0, zn), = zeros

    def body(f_ref, o_ref, edge, sems, esems):
        copies = [pltpu.make_async_copy(f_ref.at[f, pl.ds(l0, n)], o_ref.at[pl.ds(dst, n)], sems.at[i])
                  for i, (f, l0, dst, n) in enumerate(bulk)]
        loads = []
        for i, (f, _) in enumerate(seams):
            loads.append(pltpu.make_async_copy(f_ref.at[f, pl.ds(0, 16)], edge.at[i, 0], esems.at[i, 0]))
            loads.append(pltpu.make_async_copy(f_ref.at[f - 1, pl.ds(FRAME, 16)], edge.at[i, 1], esems.at[i, 1]))
        for cp in copies + loads:
            cp.start()
        o_ref[z0:z0 + zn, :] = jnp.zeros((zn, D), o_ref.dtype)
        for cp in loads:
            cp.wait()
        for i, (_, dst) in enumerate(seams):
            o_ref[dst:dst + 16, :] = edge[i, 0] + edge[i, 1]
        for cp in copies:
            cp.wait()

    return pl.pallas_call(
        body, name="build_wft",
        in_specs=[_any()], out_specs=_vmem(),
        out_shape=jax.ShapeDtypeStruct((NF, D), frames.dtype),
        scratch_shapes=[pltpu.VMEM((len(seams), 2, 16, D), frames.dtype),
                        pltpu.SemaphoreType.DMA((len(bulk),)), pltpu.SemaphoreType.DMA((len(seams), 2))],
        compiler_params=_cp(),
    )(frames)


def kernel(x, positions, norm_w, w_in, a_sinks, b_gate_up, b_gate_bias, b_out_norm_w, w_a_proj, w_b_proj, w_out, final_norm_w, loss_target, m_norm_w, m_w_in, m_a_sinks, m_b_gate_up, m_b_gate_bias, m_b_out_norm_w, m_w_a_proj, m_w_b_proj, m_w_out, m_final_norm_w, v_norm_w, v_w_in, v_a_sinks, v_b_gate_up, v_b_gate_bias, v_b_out_norm_w, v_w_a_proj, v_w_b_proj, v_w_out, v_final_norm_w):
    T = x.shape[1]
    xs, target = x[0], loss_target[0]
    fnw = final_norm_w.reshape(1, D)
    me = 4 * lax.axis_index("x") + 2 * lax.axis_index("y") + lax.axis_index("c")
    allw, h, cos, sin = _gather_blocks(_by_lane_tile(w_in), b_gate_up[0], xs, norm_w, positions.reshape(T, 1))
    late_blk = jnp.concatenate([w_a_proj[0], w_b_proj[0], w_out[0]], axis=0).astype(WIRE)
    l_send, l_recv, l_blk, l_land, l_started = _late_gather_start(late_blk, cos)
    wf = _build_wft_copies(allw)
    gu = allw[:, SHARD_PAD:G_ROWS, :64].transpose(1, 0, 2).reshape(RANK, 512)
    gu_pad = _pad_rows(gu, W_BL)

    proj = _proj(h, wf, l_started)
    o_a, lse = _swa_fwd(proj, cos, sin, a_sinks)
    o_b, states = _gla_fwd(proj, gu_pad, b_gate_bias)
    l_blk, l_land = _late_gather_wait(l_send, l_recv, l_blk, l_land, states, lse)
    late = lax.dynamic_update_slice(l_land, l_blk[None], (me, 0, 0))
    (dx2, do_a, do_b, d_gates, g_late0, g_late1, g_fn, g_bn, loss_part) = _mid(
        xs, target, proj, o_a, o_b, late, jnp.tile(b_out_norm_w, (1, B_HEADS)), fnw)
    d_q, d_kv, g_sinks = _swa_bwd(proj, cos, sin, a_sinks, do_a, o_a, lse, cos)
    d_gla, d_bl, g_gu, g_bias = _gla_bwd(proj, gu_pad, b_gate_bias, states, do_b)
    pieces = [d_q, d_kv, d_bl, d_gla, d_gates]
    offsets = [C_Q, C_KD, C_BL, C_GLA, C_GATES]

    ggu = g_gu[:RANK].reshape(RANK, NDEV, 64).transpose(1, 0, 2)
    ggu_half = [jnp.pad(ggu, ((0, 0), (0, 0), (0, DH - 64))).astype(WIRE), jnp.zeros((NDEV, RANK, DH), WIRE)]
    tails = [[g_late0, ggu_half[0]], [g_late1, ggu_half[1]]]

    send0, recv0, s_thru0, land0, started0 = _chip_start(_pair_reduce(_gw_half(h, pieces, 0), tails[0], 0), 0)
    send1, recv1, s_thru1, land1, started1 = _chip_start(
        _pair_reduce(_gw_half(h, pieces, 1, after=started0), tails[1], 1), 1)
    grad_x, g_nw = _dh_norm(pieces, offsets, wf, xs, dx2, norm_w, started1)
    small = jnp.concatenate([g_nw, g_fn, _pad_cols(g_bias, D), _pad_cols(g_bn, D), _pad_cols(g_sinks, D),
                             _pad_cols(loss_part, D)], axis=0)
    sm_send, sm_recv, sm_blk, sm_land, sm_started = _late_gather_start(small, g_nw, name="small_gather")
    sums0, got0 = _chip_wait(send0, recv0, s_thru0, land0, sm_started, 0)
    sums1, got1 = _chip_wait(send1, recv1, s_thru1, land1, got0, 1)
    sums, from_chips = [sums0, sums1], [got0, got1]

    ws = dict(norm_w=norm_w, fnw=fnw, bias=b_gate_bias, bn=b_out_norm_w, sinks=a_sinks)
    ms = dict(norm_w=m_norm_w, fnw=m_final_norm_w.reshape(1, D), bias=m_b_gate_bias, bn=m_b_out_norm_w,
              sinks=m_a_sinks)
    vs = dict(norm_w=v_norm_w, fnw=v_final_norm_w.reshape(1, D), bias=v_b_gate_bias, bn=v_b_out_norm_w,
              sinks=v_a_sinks)
    t_rows, t_gu = _finish(
        [_by_lane_tile(w_in), w_a_proj[0], w_b_proj[0], w_out[0]],
        [_by_lane_tile(m_w_in), m_w_a_proj[0], m_w_b_proj[0], m_w_out[0]],
        [_by_lane_tile(v_w_in), v_w_a_proj[0], v_w_b_proj[0], v_w_out[0]],
        b_gate_up[0], m_b_gate_up[0], v_b_gate_up[0], sums, from_chips)
    sm_blk, sm_land = _late_gather_wait(sm_send, sm_recv, sm_blk, sm_land, t_rows[0], t_gu[0], name="small_gather")
    loss, sm = _finish_small(ws, ms, vs, lax.dynamic_update_slice(sm_land, sm_blk[None], (me, 0, 0)))

    def outputs(k):
        return [sm["norm_w"][k], jnp.transpose(t_rows[k], (1, 2, 0)), sm["sinks"][k], t_gu[k][None], sm["bias"][k], sm["bn"][k],
                t_rows[4 + k][None], t_rows[8 + k][None], t_rows[12 + k][None], sm["fnw"][k].reshape(D)]

    return (loss[0, 0], grad_x[None], *outputs(0), *outputs(1), *outputs(2), *outputs(3))
```

```python
import functools

import numpy as np
import jax
import jax.numpy as jnp
from jax import lax
from jax.experimental import pallas as pl
from jax.experimental.pallas import tpu as pltpu

F32 = jnp.float32
MXU = jnp.bfloat16
WIRE = jnp.bfloat16

D = 1024
A_HEADS, A_KV, A_HD = 16, 2, 64
BLK = 128
B_HEADS, B_DK, B_DV = 4, 128, 256
RANK, TAU, CHUNK = 16, 16.0, 64
EPS, NEG = 1e-5, -1e30
ROPE_THETA = 10000.0
IN_WIDTH, NDEV = 7440, 8
SHARD = IN_WIDTH // NDEV
LANE = 128
LANE_TILES = D // LANE


def _by_lane_tile(a):
    return jnp.transpose(a, (2, 0, 1)).reshape(SHARD * LANE_TILES, LANE)


C_Q, C_KD, C_VD, C_BL = 0, 1024, 1280, 1536
C_BV, C_BQ, C_BK = 2048, 3072, 3584
C_AG, C_BG, C_MA, C_MB = 4096, 5120, 6144, 7168
C_GLA, W_GLA, C_GATES, W_GATES = 2048, 2048, 4096, 4096
NF = 8192
W_BL = 128

SHARD_PAD = 944
R_IN, R_A, R_B, R_O, R_GU, ROWS = 0, 944, 1072, 1200, 1328, 1344
SMALL_ROWS = 48

ADAM_LR, ADAM_B1, ADAM_B2, ADAM_EPS, ADAM_WD, ADAM_STEP = 0.001, 0.9, 0.999, 1e-08, 0.01, 10

MESH = pl.DeviceIdType.MESH
VMEM_LIMIT = 56 * 1024 * 1024


def _cp(sem=None, **kw):
    if sem is not None:
        kw["dimension_semantics"] = sem
    return pltpu.CompilerParams(vmem_limit_bytes=VMEM_LIMIT, **kw)


def _dot(a, b):
    return jnp.dot(a, b, preferred_element_type=F32)


def _dot_nt(a, b):
    return lax.dot_general(a, b, (((1,), (1,)), ((), ())), preferred_element_type=F32)


def _dot_tn(a, b):
    return lax.dot_general(a, b, (((0,), (0,)), ((), ())), preferred_element_type=F32)


def _dot_f32(a, b):
    return jnp.dot(a, b, preferred_element_type=F32, precision=lax.Precision.HIGHEST)


def _sigmoid(z):
    return 0.5 * jnp.tanh(0.5 * z) + 0.5


def _rope(xp, cos, sin):
    return xp * cos + pltpu.roll(xp, 64, 1) * sin


def _rope_bwd(dy, cos, sin):
    return dy * cos - pltpu.roll(dy, 64, 1) * sin


def _vmem():
    return pl.BlockSpec(memory_space=pltpu.VMEM)


def _any():
    return pl.BlockSpec(memory_space=pl.ANY)


def _rope_rows():
    half = A_HD // 2
    inv = (np.float32(ROPE_THETA) ** (-np.arange(half, dtype=np.float32) / np.float32(half))).astype(np.float32)
    inv_row = jnp.asarray(np.tile(inv, 4)[None, :])
    sign_row = jnp.asarray(np.concatenate([-np.ones(64, np.float32), np.ones(64, np.float32)])[None, :])
    return inv_row, sign_row


def _prologue_rows(rows, x_ref, nw_ref, pos_ref, inv_ref, sign_ref, h_ref, cos_ref, sin_ref):
    xv = x_ref[rows, :]
    r = lax.rsqrt(jnp.mean(xv * xv, axis=-1, keepdims=True) + EPS)
    h_ref[rows, :] = ((xv * r) * nw_ref[...]).astype(h_ref.dtype)
    ang = pos_ref[rows, :].astype(F32) * inv_ref[...]
    cos_ref[rows, :] = jnp.cos(ang)
    sin_ref[rows, :] = jnp.sin(ang) * sign_ref[...]


def _proj(h, wft, after):
    T = h.shape[0]
    tT, tN = T, 512

    def body(h_ref, w_ref, after_ref, o_ref):
        o_ref[...] = _dot_nt(h_ref[...], w_ref[...])

    return pl.pallas_call(
        body, name="proj", grid=(T // tT, NF // tN),
        in_specs=[pl.BlockSpec((tT, D), lambda i, j: (i, 0)), pl.BlockSpec((tN, D), lambda i, j: (j, 0)), _any()],
        out_specs=pl.BlockSpec((tT, tN), lambda i, j: (i, j)),
        out_shape=jax.ShapeDtypeStruct((T, NF), F32),
        compiler_params=_cp(("parallel", "parallel")),
    )(h, wft, after)


def _swa_masks():
    lane = lax.broadcasted_iota(jnp.int32, (BLK, LANE), 1)
    rope_sub0 = ((lane // 32) % 2) == 0
    std_sub0 = lane < 64
    return lane, rope_sub0, std_sub0


def _swa_tri():
    qi = lax.broadcasted_iota(jnp.int32, (BLK, BLK), 0)
    kj = lax.broadcasted_iota(jnp.int32, (BLK, BLK), 1)
    return kj <= qi


def _swa_fold(full, tri):
    return jnp.where(tri, full[:, BLK:], full[:, :BLK])


def _swa_unfold(sq, tri):
    return jnp.concatenate([jnp.where(tri, 0.0, sq), jnp.where(tri, sq, 0.0)], axis=1)


def _swa_keys(kc_ref, kp_ref, vc_ref, vp_ref, cq, sq, cp, sp):
    def ropek(kref, c, s):
        kv = kref[...]
        return jnp.concatenate([_rope(kv[:, :LANE], c, s), _rope(kv[:, LANE:], c, s)], axis=1)

    K = jnp.concatenate([ropek(kp_ref, cp, sp), ropek(kc_ref, cq, sq)], axis=0).astype(MXU)
    V = jnp.concatenate([vp_ref[...], vc_ref[...]], axis=0).astype(MXU)
    return K, V


def _swa_in_specs(nb, last):
    def cur(n):
        return jnp.minimum(n, last)

    def prev(n):
        return jnp.maximum(cur(n) - 1, 0)

    kd, vd = C_KD // 256, C_VD // 256
    return [
        pl.BlockSpec((BLK, D), lambda n: (cur(n), C_Q // D)),
        pl.BlockSpec((BLK, 256), lambda n: (cur(n), kd)),
        pl.BlockSpec((BLK, 256), lambda n: (prev(n), kd)),
        pl.BlockSpec((BLK, 256), lambda n: (cur(n), vd)),
        pl.BlockSpec((BLK, 256), lambda n: (prev(n), vd)),
        pl.BlockSpec((BLK, LANE), lambda n: (cur(n), 0)),
        pl.BlockSpec((BLK, LANE), lambda n: (cur(n), 0)),
        pl.BlockSpec((BLK, LANE), lambda n: (prev(n), 0)),
        pl.BlockSpec((BLK, LANE), lambda n: (prev(n), 0)),
    ]


def _swa_fwd(proj, cos, sin, sinks):
    T = proj.shape[0]
    nb = T // BLK
    scale = A_HD ** -0.5

    def body(sinks_ref, q_ref, kc_ref, kp_ref, vc_ref, vp_ref, cq_ref, sq_ref, cp_ref, sp_ref, o_ref, l_ref):
        n = pl.program_id(0)
        cq, sq = cq_ref[...], sq_ref[...]
        K, V = _swa_keys(kc_ref, kp_ref, vc_ref, vp_ref, cq, sq, cp_ref[...], sp_ref[...])
        tri = _swa_tri()
        valid = tri | (n > 0)
        lane, rope_sub0, std_sub0 = _swa_masks()
        group = A_HEADS // A_KV
        roped, lses = {}, []

        def products(head):
            pb, sub, g = head // 2, head % 2, head // group
            if sub == 0:
                roped[pb] = _rope(q_ref[:, pb * LANE:(pb + 1) * LANE], cq, sq)
            qm = jnp.where(rope_sub0 if sub == 0 else ~rope_sub0, roped[pb], 0.0).astype(MXU)
            return _dot_nt(qm, K[:, g * LANE:(g + 1) * LANE])

        def softmax(head, s_full):
            s = jnp.where(valid, _swa_fold(s_full, tri) * scale, NEG)
            sink = sinks_ref[0, head]
            m = jnp.maximum(jnp.max(s, axis=1, keepdims=True), sink)
            e = jnp.exp(s - m)
            den = jnp.sum(e, axis=1, keepdims=True) + jnp.exp(sink - m)
            lses.append(m + jnp.log(den))
            return _swa_unfold(e / den, tri).astype(MXU)

        outs = {}
        st1 = {0: products(0), 1: products(1)}
        st2 = {0: softmax(0, st1.pop(0))}
        for head in range(A_HEADS):
            if head + 2 < A_HEADS:
                st1[head + 2] = products(head + 2)
            if head + 1 < A_HEADS:
                st2[head + 1] = softmax(head + 1, st1.pop(head + 1))
            g = head // group
            outs[head] = _dot(st2.pop(head), V[:, g * LANE:(g + 1) * LANE])
            if head % 2 == 1:
                pb = head // 2
                o_ref[:, pb * LANE:(pb + 1) * LANE] = jnp.where(std_sub0, outs[head - 1], outs[head])
        lacc = jnp.zeros((BLK, LANE), F32)
        for head in range(A_HEADS):
            lacc = jnp.where(lane == head, lses[head], lacc)
        l_ref[...] = lacc

    return pl.pallas_call(
        body, name="swa_fwd", grid=(nb,),
        in_specs=[pl.BlockSpec(memory_space=pltpu.SMEM)] + _swa_in_specs(nb, nb - 1),
        out_specs=[pl.BlockSpec((BLK, D), lambda n: (n, 0)), pl.BlockSpec((BLK, LANE), lambda n: (n, 0))],
        out_shape=[jax.ShapeDtypeStruct((T, D), F32), jax.ShapeDtypeStruct((T, LANE), F32)],
        compiler_params=_cp(("parallel",)),
    )(sinks, proj, proj, proj, proj, proj, cos, sin, cos, sin)


def _swa_bwd(proj, cos, sin, sinks, do_a, o_a, lse, after):
    T = proj.shape[0]
    nb = T // BLK
    scale = A_HD ** -0.5

    def body(sinks_ref, q_ref, kc_ref, kp_ref, vc_ref, vp_ref, cq_ref, sq_ref, cp_ref, sp_ref,
             do_ref, o_ref, l_ref, after_ref, dq_ref, dkv_ref, ds_ref, ckv_ref):
        n = pl.program_id(0)

        @pl.when(n == 0)
        def _():
            ckv_ref[...] = jnp.zeros_like(ckv_ref)
            ds_ref[...] = jnp.zeros_like(ds_ref)

        @pl.when(n < nb)
        def _():
            cq, sq, cp, sp = cq_ref[...], sq_ref[...], cp_ref[...], sp_ref[...]
            K, V = _swa_keys(kc_ref, kp_ref, vc_ref, vp_ref, cq, sq, cp, sp)
            tri = _swa_tri()
            valid = tri | (n > 0)
            lane, rope_sub0, std_sub0 = _swa_masks()
            lane_row = lax.broadcasted_iota(jnp.int32, (1, LANE), 1)
            lse_v = l_ref[...]
            dKt = [jnp.zeros((LANE, 2 * BLK), F32) for _ in range(A_KV)]
            dVt = [jnp.zeros((LANE, 2 * BLK), F32) for _ in range(A_KV)]
            dsinks, roped, roped_t, do_t = [], {}, {}, {}
            group = A_HEADS // A_KV
            dim = lax.broadcasted_iota(jnp.int32, (LANE, BLK), 0)
            rope_row0, std_row0 = ((dim // 32) % 2) == 0, dim < 64

            def products(head):
                pb, sub, g = head // 2, head % 2, head // group
                cols = slice(pb * LANE, (pb + 1) * LANE)
                Kg, Vg = K[:, g * LANE:(g + 1) * LANE], V[:, g * LANE:(g + 1) * LANE]
                if sub == 0:
                    roped[pb] = _rope(q_ref[:, cols], cq, sq)
                    roped_t[pb] = roped[pb].T
                    do_t[pb] = do_ref[:, cols].T
                qm = jnp.where(rope_sub0 if sub == 0 else ~rope_sub0, roped[pb], 0.0).astype(MXU)
                qmt = jnp.where(rope_row0 if sub == 0 else ~rope_row0, roped_t[pb], 0.0).astype(MXU)
                dov = jnp.where(std_sub0 if sub == 0 else ~std_sub0, do_ref[:, cols], 0.0)
                dovt = jnp.where(std_row0 if sub == 0 else ~std_row0, do_t[pb], 0.0).astype(MXU)
                delta = jnp.sum(dov * o_ref[:, cols], axis=1, keepdims=True)
                return qmt, dovt, delta, _dot_nt(qm, Kg), _dot_nt(dov.astype(MXU), Vg)

            def scores(head, qmt, dovt, delta, s_full, dp_full):
                lh = jnp.sum(jnp.where(lane == head, lse_v, 0.0), axis=1, keepdims=True)
                p = jnp.where(valid, jnp.exp(_swa_fold(s_full, tri) * scale - lh), 0.0)
                psink = jnp.exp(sinks_ref[0, head] - lh)
                dsinks.append(jnp.sum(-psink * delta, axis=0, keepdims=True))
                dsq = (p * (_swa_fold(dp_full, tri) - delta)) * scale
                return qmt, dovt, _swa_unfold(p, tri).astype(MXU), _swa_unfold(dsq, tri).astype(MXU)

            def grads(head, qmt, dovt, pb16, dsc):
                g = head // group
                dKt[g] = dKt[g] + _dot(qmt, dsc)
                dVt[g] = dVt[g] + _dot(dovt, pb16)
                return _dot(dsc, K[:, g * LANE:(g + 1) * LANE])

            dqs = {}
            st1 = {0: products(0), 1: products(1)}
            st2 = {0: scores(0, *st1.pop(0))}
            for head in range(A_HEADS):
                if head + 2 < A_HEADS:
                    st1[head + 2] = products(head + 2)
                if head + 1 < A_HEADS:
                    st2[head + 1] = scores(head + 1, *st1.pop(head + 1))
                dqs[head] = grads(head, *st2.pop(head))
                if head % 2 == 1:
                    pb = head // 2
                    dqp = jnp.where(rope_sub0, dqs[head - 1], dqs[head])
                    dq_ref[:, pb * LANE:(pb + 1) * LANE] = _rope_bwd(dqp, cq, sq).astype(dq_ref.dtype)
            dsink = jnp.zeros((1, LANE), F32)
            for head in range(A_HEADS):
                dsink = jnp.where(lane_row == head, dsinks[head], dsink)
            dK, dV = [a.T for a in dKt], [a.T for a in dVt]
            prev = ([_rope_bwd(dK[g][:BLK], cp, sp) for g in range(A_KV)] + [dV[g][:BLK] for g in range(A_KV)])
            cur_ = ([_rope_bwd(dK[g][BLK:], cq, sq) for g in range(A_KV)] + [dV[g][BLK:] for g in range(A_KV)])
            dkv_ref[...] = (ckv_ref[...] + jnp.concatenate(prev, axis=1)).astype(dkv_ref.dtype)
            ckv_ref[...] = jnp.concatenate(cur_, axis=1)
            ds_ref[...] = ds_ref[...] + jnp.broadcast_to(dsink, ds_ref.shape)

        @pl.when(n == nb)
        def _():
            dkv_ref[...] = ckv_ref[...].astype(dkv_ref.dtype)

    last = nb - 1

    def cur(n):
        return jnp.minimum(n, last)

    def out_kv(n):
        return (jnp.maximum(n - 1, 0), 0)

    return pl.pallas_call(
        body, name="swa_bwd", grid=(nb + 1,),
        in_specs=[pl.BlockSpec(memory_space=pltpu.SMEM)] + _swa_in_specs(nb, last) + [
            pl.BlockSpec((BLK, D), lambda n: (cur(n), 0)),
            pl.BlockSpec((BLK, D), lambda n: (cur(n), 0)),
            pl.BlockSpec((BLK, LANE), lambda n: (cur(n), 0)),
            _any(),
        ],
        out_specs=[
            pl.BlockSpec((BLK, D), lambda n: (cur(n), 0)),
            pl.BlockSpec((BLK, 512), out_kv),
            pl.BlockSpec((8, LANE), lambda n: (0, 0)),
        ],
        out_shape=[
            jax.ShapeDtypeStruct((T, D), MXU),
            jax.ShapeDtypeStruct((T, 512), MXU),
            jax.ShapeDtypeStruct((8, LANE), F32),
        ],
        scratch_shapes=[pltpu.VMEM((BLK, 512), F32)],
        compiler_params=_cp(("arbitrary",)),
    )(sinks, proj, proj, proj, proj, proj, cos, sin, cos, sin, do_a, o_a, lse, after)


NCH = 4
GSTEP = NCH * CHUNK
ST_ROWS = B_HEADS * B_DV


def _chunk_rows(c):
    return slice(c * CHUNK, (c + 1) * CHUNK)


def _per_chunk(which, vals):
    out = vals[-1]
    for c in range(NCH - 2, -1, -1):
        out = jnp.where(which == c, vals[c], out)
    return out


def _gla_gate(bl_ref, gu_ref, bias_ref):
    gk = _dot(bl_ref[...].astype(MXU), gu_ref[...]) + bias_ref[...]
    la = (jnp.minimum(gk, 0.0) - jnp.log(1.0 + jnp.exp(-jnp.abs(gk)))) / TAU
    ri = lax.broadcasted_iota(jnp.int32, (GSTEP, GSTEP), 0)
    ci = lax.broadcasted_iota(jnp.int32, (GSTEP, GSTEP), 1)
    same = (ri // CHUNK) == (ci // CHUNK)
    lower, upper = same & (ci <= ri), same & (ci >= ri)
    b = _dot_f32(jnp.where(lower, 1.0, 0.0).astype(F32), la)
    which = lax.broadcasted_iota(jnp.int32, (GSTEP, 1), 0) // CHUNK
    return gk, la, b, lower, upper, which


def _gla_head(q_ref, k_ref, la, b, which, h):
    sl = slice(h * B_DK, (h + 1) * B_DK)
    bh, lah = b[:, sl], la[:, sl]
    bls = [jnp.sum(lah[_chunk_rows(c)], axis=0, keepdims=True) for c in range(NCH)]
    blast = _per_chunk(which, bls)
    qc = q_ref[:, sl] * (B_DK ** -0.5)
    kh = k_ref[:, sl]
    eb, enb, esb = jnp.exp(bh), jnp.exp(-bh), jnp.exp(blast - bh)
    return qc * eb, kh * enb, kh * esb, eb, enb, esb, [jnp.exp(v) for v in bls]


def _gla_specs(step_of):
    return [
        pl.BlockSpec((GSTEP, 512), lambda i: (step_of(i), C_BQ // 512)),
        pl.BlockSpec((GSTEP, 512), lambda i: (step_of(i), C_BK // 512)),
        pl.BlockSpec((GSTEP, D), lambda i: (step_of(i), C_BV // D)),
        pl.BlockSpec((GSTEP, W_BL), lambda i: (step_of(i), C_BL // W_BL)),
        pl.BlockSpec((W_BL, 512), lambda i: (0, 0)),
        pl.BlockSpec((1, 512), lambda i: (0, 0)),
    ]


def _gla_fwd(proj, gu_pad, bias):
    T = proj.shape[0]
    ns = T // GSTEP

    def body(q_ref, k_ref, v_ref, bl_ref, gu_ref, bias_ref, o_ref, st_ref, state_ref):
        @pl.when(pl.program_id(0) == 0)
        def _():
            state_ref[...] = jnp.zeros_like(state_ref)

        _, la, b, lower, _, which = _gla_gate(bl_ref, gu_ref, bias_ref)

        def within(h):
            q_e, k_e, k_s, _, _, _, decays = _gla_head(q_ref, k_ref, la, b, which, h)
            vh = v_ref[:, h * B_DV:(h + 1) * B_DV].astype(MXU)
            q_eb = q_e.astype(MXU)
            att = jnp.where(lower, _dot_nt(q_eb, k_e.astype(MXU)), 0.0)
            return vh, q_eb, k_s.astype(MXU), _dot(att.astype(MXU), vh), decays

        def across(h, vh, q_eb, k_sb, o_intra, decays):
            rows = slice(h * B_DV, (h + 1) * B_DV)
            s = state_ref[rows, :]
            outs = []
            for c in range(NCH):
                cr = _chunk_rows(c)
                st_ref[c * ST_ROWS + h * B_DV:c * ST_ROWS + (h + 1) * B_DV, :] = s
                outs.append(o_intra[cr] + _dot_nt(q_eb[cr], s.astype(MXU)))
                s = s * decays[c] + _dot_tn(vh[cr], k_sb[cr])
            state_ref[rows, :] = s
            o_ref[:, rows] = jnp.concatenate(outs, axis=0)

        for h in range(B_HEADS):
            across(h, *within(h))

    return pl.pallas_call(
        body, name="gla_fwd", grid=(ns,),
        in_specs=_gla_specs(lambda i: i),
        out_specs=[pl.BlockSpec((GSTEP, D), lambda i: (i, 0)),
                   pl.BlockSpec((NCH * ST_ROWS, B_DK), lambda i: (i, 0))],
        out_shape=[jax.ShapeDtypeStruct((T, D), F32),
                   jax.ShapeDtypeStruct((ns * NCH * ST_ROWS, B_DK), F32)],
        scratch_shapes=[pltpu.VMEM((ST_ROWS, B_DK), F32)],
        compiler_params=_cp(("arbitrary",)),
    )(proj, proj, proj, proj, gu_pad, bias)


def _gla_bwd(proj, gu_pad, bias, states, do_b):
    T = proj.shape[0]
    ns = T // GSTEP
    o_q, o_k = C_BQ - C_GLA, C_BK - C_GLA

    def body(q_ref, k_ref, v_ref, bl_ref, gu_ref, bias_ref, st_ref, do_ref,
             dg_ref, dbl_ref, ggu_ref, gbias_ref, gt_ref):
        @pl.when(pl.program_id(0) == 0)
        def _():
            gt_ref[...] = jnp.zeros_like(gt_ref)
            ggu_ref[...] = jnp.zeros_like(ggu_ref)
            gbias_ref[...] = jnp.zeros_like(gbias_ref)

        gk, la, b, lower, upper_mask, which = _gla_gate(bl_ref, gu_ref, bias_ref)
        upper = jnp.where(upper_mask, 1.0, 0.0).astype(F32)
        dla_parts = []

        def within(h):
            q_e, k_e, k_s, eb, enb, esb, decays = _gla_head(q_ref, k_ref, la, b, which, h)
            vh = v_ref[:, h * B_DV:(h + 1) * B_DV].astype(MXU)
            doh = do_ref[:, h * B_DV:(h + 1) * B_DV].astype(MXU)
            q_eb, k_eb = q_e.astype(MXU), k_e.astype(MXU)
            att = jnp.where(lower, _dot_nt(q_eb, k_eb), 0.0).astype(MXU)
            datt = jnp.where(lower, _dot_nt(doh, vh), 0.0).astype(MXU)
            return (q_e, k_e, k_s, eb, enb, esb, decays, vh, doh, q_eb, k_s.astype(MXU),
                    _dot(datt, k_eb), _dot_tn(datt, q_eb), _dot_tn(att, doh))

        def across(h, q_e, k_e, k_s, eb, enb, esb, decays, vh, doh, q_eb, k_sb, dq_i, dk_e, dv_i):
            rows = slice(h * B_DV, (h + 1) * B_DV)
            g = gt_ref[rows, :]
            dq_c, dks_c, dv_c, ddec = [None] * NCH, [None] * NCH, [None] * NCH, [None] * NCH
            for c in range(NCH - 1, -1, -1):
                cr = _chunk_rows(c)
                s = st_ref[c * ST_ROWS + h * B_DV:c * ST_ROWS + (h + 1) * B_DV, :]
                gb = g.astype(MXU)
                dq_c[c] = dq_i[cr] + _dot(doh[cr], s.astype(MXU))
                dks_c[c] = _dot(vh[cr], gb)
                dv_c[c] = dv_i[cr] + _dot_nt(k_sb[cr], gb)
                ddec[c] = jnp.sum(g * s, axis=0, keepdims=True)
                g = g * decays[c] + _dot_tn(doh[cr], q_eb[cr])
            gt_ref[rows, :] = g
            dq_e = jnp.concatenate(dq_c, axis=0)
            dk_s = jnp.concatenate(dks_c, axis=0)
            dg_ref[:, rows] = jnp.concatenate(dv_c, axis=0).astype(dg_ref.dtype)
            dg_ref[:, o_q + h * B_DK:o_q + (h + 1) * B_DK] = (dq_e * eb * (B_DK ** -0.5)).astype(dg_ref.dtype)
            dg_ref[:, o_k + h * B_DK:o_k + (h + 1) * B_DK] = (dk_e * enb + dk_s * esb).astype(dg_ref.dtype)
            dks_ks = dk_s * k_s
            db = dq_e * q_e - dk_e * k_e - dks_ks
            dbl = [jnp.sum(dks_ks[_chunk_rows(c)], axis=0, keepdims=True) + ddec[c] * decays[c] for c in range(NCH)]
            dla_parts.append(_dot_f32(upper, db) + _per_chunk(which, dbl))

        for h in range(B_HEADS):
            across(h, *within(h))
        dla = jnp.concatenate(dla_parts, axis=1)
        dgk = dla * (1.0 / TAU) * _sigmoid(-gk)
        dgkb = dgk.astype(MXU)
        dbl_ref[...] = _dot_nt(dgkb, gu_ref[...]).astype(dbl_ref.dtype)
        ggu_ref[...] = ggu_ref[...] + _dot_tn(bl_ref[...].astype(MXU), dgkb)
        gbias_ref[...] = gbias_ref[...] + jnp.broadcast_to(jnp.sum(dgk, axis=0, keepdims=True), gbias_ref.shape)

    def rev(i):
        return ns - 1 - i

    return pl.pallas_call(
        body, name="gla_bwd", grid=(ns,),
        in_specs=_gla_specs(rev) + [
            pl.BlockSpec((NCH * ST_ROWS, B_DK), lambda i: (rev(i), 0)),
            pl.BlockSpec((GSTEP, D), lambda i: (rev(i), 0)),
        ],
        out_specs=[
            pl.BlockSpec((GSTEP, W_GLA), lambda i: (rev(i), 0)),
            pl.BlockSpec((GSTEP, W_BL), lambda i: (rev(i), 0)),
            pl.BlockSpec((W_BL, 512), lambda i: (0, 0)),
            pl.BlockSpec((8, 512), lambda i: (0, 0)),
        ],
        out_shape=[
            jax.ShapeDtypeStruct((T, W_GLA), MXU),
            jax.ShapeDtypeStruct((T, W_BL), MXU),
            jax.ShapeDtypeStruct((W_BL, 512), F32),
            jax.ShapeDtypeStruct((8, 512), F32),
        ],
        scratch_shapes=[pltpu.VMEM((B_HEADS * B_DV, B_DK), F32)],
        compiler_params=_cp(("arbitrary",)),
    )(proj, proj, proj, proj, gu_pad, bias, states, do_b)


def _mid(x, target, proj, o_a, o_b, late, w_bn4, fnw):
    T = x.shape[0]
    tT = min(T, 128)
    nbuf = 4
    o_ag, o_bg, o_ma, o_mb = (c - C_GATES for c in (C_AG, C_BG, C_MA, C_MB))

    def body(x_ref, t_ref, oa_ref, ob_ref, gates_ref, late_ref, wbn_ref, fnw_ref,
             dx2_ref, doa_ref, dob_ref, dgates_ref,
             tail0_ref, tail1_ref, gfn_ref, gbn_ref, loss_ref, buf_ref, gw_ref):
        i = pl.program_id(0)

        def weight(p):
            return late_ref[:, 128 * p:128 * (p + 1), :].reshape(D, D)

        @pl.when(i == 0)
        def _():
            for r in (gw_ref, gfn_ref, gbn_ref, loss_ref):
                r[...] = jnp.zeros_like(r)

        rows = pl.ds(pl.multiple_of((i % nbuf) * tT, tT), tT)

        def keep(k, val):
            buf_ref[k, rows, :] = val

        oa, ag = oa_ref[...], gates_ref[:, o_ag:o_ag + D]
        sg_a = _sigmoid(ag)
        silu_a = ag * sg_a
        oag_b = (oa * silu_a).astype(MXU)
        keep(0, oag_b)
        y_a = _dot(oag_b, weight(0))

        ob, bg = ob_ref[...], gates_ref[:, o_bg:o_bg + D]
        rbs, obhats = [], []
        for h in range(B_HEADS):
            obh = ob[:, h * B_DV:(h + 1) * B_DV]
            rb = lax.rsqrt(jnp.mean(obh * obh, axis=-1, keepdims=True) + EPS)
            rbs.append(rb)
            obhats.append(obh * rb)
        obhat = jnp.concatenate(obhats, axis=1)
        wbn = wbn_ref[...]
        obn = obhat * wbn
        sg_b = _sigmoid(bg)
        silu_b = bg * sg_b
        obg_b = (obn * silu_b).astype(MXU)
        keep(1, obg_b)
        y_b = _dot(obg_b, weight(1))

        sa, sb = _sigmoid(gates_ref[:, o_ma:o_ma + D]), _sigmoid(gates_ref[:, o_mb:o_mb + D])
        mg_b = (sa * y_a + sb * y_b).astype(MXU)
        keep(2, mg_b)
        x2 = x_ref[...] + _dot(mg_b, weight(2))
        r2 = lax.rsqrt(jnp.mean(x2 * x2, axis=-1, keepdims=True) + EPS)
        xh2 = x2 * r2
        fw = fnw_ref[...]
        err = xh2 * fw - t_ref[...]
        tok = jnp.mean(err * err, axis=-1, keepdims=True)
        loss_ref[...] = loss_ref[...] + 0.5 * jnp.sum(tok, axis=0, keepdims=True)

        dy = err * (1.0 / D)
        gfn_ref[...] = gfn_ref[...] + jnp.broadcast_to(jnp.sum(dy * xh2, axis=0, keepdims=True), gfn_ref.shape)
        gy = dy * fw
        dx2 = r2 * (gy - xh2 * jnp.mean(gy * xh2, axis=-1, keepdims=True))
        dx2_ref[...] = dx2
        dx2_b = dx2.astype(MXU)
        keep(5, dx2_b)
        dmg = _dot_nt(dx2_b, weight(2))

        dgates_ref[:, o_ma:o_ma + D] = (dmg * y_a * sa * (1.0 - sa)).astype(dgates_ref.dtype)
        dgates_ref[:, o_mb:o_mb + D] = (dmg * y_b * sb * (1.0 - sb)).astype(dgates_ref.dtype)
        dya_b = (dmg * sa).astype(MXU)
        dyb_b = (dmg * sb).astype(MXU)
        keep(3, dya_b)
        keep(4, dyb_b)
        doag = _dot_nt(dya_b, weight(0))
        dobg = _dot_nt(dyb_b, weight(1))

        @pl.when(i % nbuf == nbuf - 1)
        def _():
            for p in range(3):
                gw_ref[p] = gw_ref[p] + _dot_tn(buf_ref[p], buf_ref[3 + p])

        @pl.when(i == pl.num_programs(0) - 1)
        def _():
            for hf, tail_ref in enumerate((tail0_ref, tail1_ref)):
                for d in range(NDEV):
                    for p in range(3):
                        tail_ref[d, 128 * p:128 * (p + 1), :] = (
                            gw_ref[p, 128 * d:128 * (d + 1), hf * DH:(hf + 1) * DH].astype(tail_ref.dtype))

        doa_ref[...] = doag * silu_a
        dgates_ref[:, o_ag:o_ag + D] = (doag * oa * (sg_a * (1.0 + ag * (1.0 - sg_a)))).astype(dgates_ref.dtype)
        dobn = dobg * silu_b
        dgates_ref[:, o_bg:o_bg + D] = (dobg * obn * (sg_b * (1.0 + bg * (1.0 - sg_b)))).astype(dgates_ref.dtype)
        gg = dobn * wbn
        gbn = jnp.zeros((1, B_DV), F32)
        for h in range(B_HEADS):
            sl = slice(h * B_DV, (h + 1) * B_DV)
            gbn = gbn + jnp.sum(dobn[:, sl] * obhats[h], axis=0, keepdims=True)
            ggh = gg[:, sl]
            dob_ref[:, sl] = rbs[h] * (ggh - obhats[h] * jnp.mean(ggh * obhats[h], axis=-1, keepdims=True))
        gbn_ref[...] = gbn_ref[...] + jnp.broadcast_to(gbn, gbn_ref.shape)

    assert (T // tT) % nbuf == 0
    tile = pl.BlockSpec((tT, D), lambda i: (i, 0))
    row = pl.BlockSpec((1, D), lambda i: (0, 0))
    acc8 = pl.BlockSpec((8, D), lambda i: (0, 0))
    return pl.pallas_call(
        body, name="mid", grid=(T // tT,),
        in_specs=[tile, tile, tile, tile, pl.BlockSpec((tT, W_GATES), lambda i: (i, C_GATES // W_GATES)),
                  _vmem(), row, row],
        out_specs=[tile, tile, tile, pl.BlockSpec((tT, W_GATES), lambda i: (i, 0)), _vmem(), _vmem(),
                   acc8, pl.BlockSpec((8, B_DV), lambda i: (0, 0)), pl.BlockSpec((8, LANE), lambda i: (0, 0))],
        out_shape=[
            jax.ShapeDtypeStruct((T, D), F32),
            jax.ShapeDtypeStruct((T, D), F32),
            jax.ShapeDtypeStruct((T, D), F32),
            jax.ShapeDtypeStruct((T, W_GATES), MXU),
            jax.ShapeDtypeStruct((NDEV, 384, DH), WIRE),
            jax.ShapeDtypeStruct((NDEV, 384, DH), WIRE),
            jax.ShapeDtypeStruct((8, D), F32),
            jax.ShapeDtypeStruct((8, B_DV), F32),
            jax.ShapeDtypeStruct((8, LANE), F32),
        ],
        scratch_shapes=[pltpu.VMEM((6, nbuf * tT, D), MXU), pltpu.VMEM((3, D, D), F32)],
        compiler_params=_cp(("arbitrary",)),
    )(x, target, o_a, o_b, proj, late, w_bn4, fnw)


DH = D // 2


_GW_TILES = (("q", 0, 512, 0), ("q", 1, 512, 512), ("kv", 0, 256, 1024), ("bl", 0, RANK, 5376),
             ("gla", 0, 512, 3328), ("gla", 1, 512, 3840), ("gla", 2, 512, 2304), ("gla", 3, 512, 2816),
             ("gates", 0, 512, 1280), ("gates", 1, 512, 1792), ("gates", 2, 512, 4352), ("gates", 3, 512, 4864),
             ("gates", 4, 512, 5392), ("gates", 5, 512, 5904), ("gates", 6, 512, 6416), ("gates", 7, 512, 6928))


def _gw_unpermute(piece, t):
    if piece == "q":
        parts = []
        for blk in range(t.shape[0] // LANE):
            g = [t[blk * LANE + 32 * i:blk * LANE + 32 * (i + 1)] for i in range(4)]
            parts += [g[0], g[2], g[1], g[3]]
        return jnp.concatenate(parts, axis=0)
    if piece == "kv":
        k = [t[64 * i:64 * i + 32] + t[64 * i + 32:64 * i + 64] for i in range(4)]
        v = [t[256 + 128 * g:256 + 128 * g + 64] + t[256 + 128 * g + 64:256 + 128 * (g + 1)] for g in range(2)]
        return jnp.concatenate(k + v, axis=0)
    if piece == "bl":
        return t[:RANK]
    return t


def _gw_half(h, pieces, half, after=None):
    T = h.shape[0]
    steps = len(_GW_TILES)

    def body(*refs):
        h_ref = refs[0]
        srcs = dict(zip(("q", "kv", "bl", "gla", "gates"), refs[1:6]))
        o_ref, stage, sems = refs[-3:]
        j = pl.program_id(0)

        def out_copy(k):
            _, _, n, off = _GW_TILES[k]
            return pltpu.make_async_copy(stage.at[k % 2, 0:n], o_ref.at[pl.ds(off, n)], sems.at[k % 2])

        for k, (piece, _, n, _) in enumerate(_GW_TILES):
            @pl.when(j == k)
            def _(k=k, piece=piece, n=n):
                if k >= 2:
                    out_copy(k - 2).wait()
                t = _gw_unpermute(piece, _dot_tn(srcs[piece][...], h_ref[...]))
                stage[k % 2, 0:n, :] = t.astype(stage.dtype)
                out_copy(k).start()

        @pl.when(j == steps - 1)
        def _():
            out_copy(steps - 2).wait()
            out_copy(steps - 1).wait()

    def tile_of(lo, hi):
        return lambda j: (0, jnp.clip(j - lo, 0, hi - lo - 1))

    in_specs = [pl.BlockSpec((T, DH), lambda j: (0, half)),
                pl.BlockSpec((T, 512), tile_of(0, 2)), pl.BlockSpec((T, 512), lambda j: (0, 0)),
                pl.BlockSpec((T, W_BL), lambda j: (0, 0)),
                pl.BlockSpec((T, 512), tile_of(4, 8)), pl.BlockSpec((T, 512), tile_of(8, 16))]
    args = [h, *pieces]
    if after is not None:
        in_specs.append(_any())
        args.append(after)
    return pl.pallas_call(
        body, name=f"gw_in_half{half}", grid=(steps,),
        in_specs=in_specs, out_specs=_any(),
        out_shape=jax.ShapeDtypeStruct((IN_WIDTH, DH), WIRE),
        scratch_shapes=[pltpu.VMEM((2, 512, DH), WIRE), pltpu.SemaphoreType.DMA((2,))],
        compiler_params=_cp(("arbitrary",)),
    )(*args)


def _chip_copies(s_ref, got_ref, send_sems, recv_sems):
    x, y, c = _place()
    chips = [(1 - x, y), (x, 1 - y), (1 - x, 1 - y)]
    return [pltpu.make_async_remote_copy(
        src_ref=s_ref.at[2 * px + py], dst_ref=got_ref.at[j],
        send_sem=send_sems.at[j], recv_sem=recv_sems.at[j], device_id=(px, py, c), device_id_type=MESH)
        for j, (px, py) in enumerate(chips)]


_EFFECT = pltpu.SideEffectType.DATAFLOW_SIDE_EFFECTING


def _hbm():
    return pl.BlockSpec(memory_space=pltpu.HBM)


def _sem():
    return pl.BlockSpec(memory_space=pltpu.SEMAPHORE)


def _chip_start(sums, half):
    land = pltpu.with_memory_space_constraint(lax.empty((3,) + sums.shape[1:], sums.dtype), pltpu.HBM)

    def body(s_ref, land_ref, send_sems, recv_sems, s_thru, land_thru, token):
        for cp in _chip_copies(s_ref, land_ref, send_sems, recv_sems):
            cp.start()
        token[...] = jnp.zeros_like(token)

    return pl.pallas_call(
        body, name=f"chip_start{half}",
        out_shape=(pltpu.SemaphoreType.DMA((3,)), pltpu.SemaphoreType.DMA((3,)),
                   pltpu.HBM(sums.shape, sums.dtype), pltpu.HBM(land.shape, land.dtype),
                   jax.ShapeDtypeStruct((8, LANE), F32)),
        in_specs=(_hbm(), _hbm()), out_specs=(_sem(), _sem(), _hbm(), _hbm(), _vmem()),
        input_output_aliases={0: 2, 1: 3},
        compiler_params=pltpu.CompilerParams(has_side_effects=_EFFECT),
    )(pltpu.with_memory_space_constraint(sums, pltpu.HBM), land)


def _chip_wait(send_sems, recv_sems, s_thru, land_thru, after, half):
    def body(s_ref, land_ref, send_sems, recv_sems, after_ref, s_out, got_ref):
        copies = _chip_copies(s_ref, land_ref, send_sems, recv_sems)
        for cp in copies:
            cp.wait_send()
        for cp in copies:
            cp.wait_recv()

    return pl.pallas_call(
        body, name=f"chip_wait{half}",
        out_shape=(pltpu.HBM(s_thru.shape, s_thru.dtype), pltpu.HBM(land_thru.shape, land_thru.dtype)),
        in_specs=(_hbm(), _hbm(), _sem(), _sem(), _any()), out_specs=(_hbm(), _hbm()),
        input_output_aliases={0: 0, 1: 1},
        compiler_params=pltpu.CompilerParams(has_side_effects=_EFFECT),
    )(s_thru, land_thru, send_sems, recv_sems, after)


def _dh_norm(pieces, offsets, wf, x, dx2, norm_w, after):
    T = x.shape[0]
    tT = min(T, 256)
    widths = [p.shape[1] for p in pieces]
    npc = len(pieces)

    def body(*refs):
        dp_refs = refs[:npc]
        wf_ref, x_ref, dx2_ref, nw_ref, _, gx_ref, gnw_ref = refs[npc:]

        @pl.when(pl.program_id(0) == 0)
        def _():
            gnw_ref[...] = jnp.zeros_like(gnw_ref)

        dh = jnp.zeros((tT, D), F32)
        for dp_ref, off, w in zip(dp_refs, offsets, widths):
            dh = dh + _dot(dp_ref[...], wf_ref[off:off + w, :])
        xv = x_ref[...]
        r = lax.rsqrt(jnp.mean(xv * xv, axis=-1, keepdims=True) + EPS)
        xh = xv * r
        gnw_ref[...] = gnw_ref[...] + jnp.broadcast_to(jnp.sum(dh * xh, axis=0, keepdims=True), gnw_ref.shape)
        g = dh * nw_ref[...]
        gx_ref[...] = r * (g - xh * jnp.mean(g * xh, axis=-1, keepdims=True)) + dx2_ref[...]

    tile = pl.BlockSpec((tT, D), lambda i: (i, 0))
    return pl.pallas_call(
        body, name="dh_norm", grid=(T // tT,),
        in_specs=[pl.BlockSpec((tT, w), lambda i: (i, 0)) for w in widths]
        + [_vmem(), tile, tile, pl.BlockSpec((1, D), lambda i: (0, 0)), _any()],
        out_specs=[tile, pl.BlockSpec((8, D), lambda i: (0, 0))],
        out_shape=[jax.ShapeDtypeStruct((T, D), F32), jax.ShapeDtypeStruct((8, D), F32)],
        compiler_params=_cp(("arbitrary",)),
    )(*pieces, wf, x, dx2, norm_w, after)


def _adamw_math(w, g, m, v):
    m = ADAM_B1 * m + (1.0 - ADAM_B1) * g
    v = ADAM_B2 * v + (1.0 - ADAM_B2) * (g * g)
    m_hat = m * (1.0 / (1.0 - ADAM_B1 ** ADAM_STEP))
    v_hat = v * (1.0 / (1.0 - ADAM_B2 ** ADAM_STEP))
    delta = -ADAM_LR * (m_hat / (jnp.sqrt(v_hat) + ADAM_EPS) + ADAM_WD * w)
    return delta, m, v


def _fetch_partials(s_ref, got_ref, buf, sems):
    x, y, _ = _place()
    cps = [pltpu.make_async_copy(s_ref.at[2 * x + y], buf.at[0], sems.at[0])]
    cps += [pltpu.make_async_copy(got_ref.at[j], buf.at[1 + j], sems.at[1 + j]) for j in range(3)]
    for cp in cps:
        cp.start()
    for cp in cps:
        cp.wait()


SMALL_AT = dict(norm_w=0, fnw=8, bias=16, bn=24, sinks=32, loss=40)
ROW_AT = (R_IN, R_A, R_B, R_O)


def _finish_small(ws, ms, vs, smalls):
    names = ["norm_w", "fnw", "bias", "bn", "sinks"]
    widths = [ws[n].shape[1] for n in names]

    def body(*refs):
        w_refs, m_refs, v_refs = refs[0:5], refs[5:10], refs[10:15]
        smalls_ref, loss_ref = refs[15], refs[16]
        outs, tot = refs[17:37], refs[37]
        acc = smalls_ref[0]
        for d in range(1, NDEV):
            acc = acc + smalls_ref[d]
        tot[...] = acc
        loss_ref[...] = tot[SMALL_AT["loss"]:SMALL_AT["loss"] + 1, 0:1]
        for p, (nm_, wd) in enumerate(zip(names, widths)):
            r = SMALL_AT[nm_]
            g = tot[r:r + 1, 0:wd]
            d, nm, nv = _adamw_math(w_refs[p][...], g, m_refs[p][...], v_refs[p][...])
            for o, val in zip(outs[4 * p:4 * p + 4], (g, d, nm, nv)):
                o[...] = val

    res = pl.pallas_call(
        body, name="finish_small",
        in_specs=[_vmem()] * 16, out_specs=[_vmem()] * 21,
        out_shape=[jax.ShapeDtypeStruct((1, 1), F32)]
        + [jax.ShapeDtypeStruct((1, wd), F32) for wd in widths for _ in range(4)],
        scratch_shapes=[pltpu.VMEM((SMALL_ROWS, D), F32)],
        compiler_params=_cp(),
    )(*[ws[n] for n in names], *[ms[n] for n in names], *[vs[n] for n in names], smalls)
    return res[0], {n: tuple(res[1 + 4 * p:5 + 4 * p]) for p, n in enumerate(names)}


def _finish(w_rows, m_rows, v_rows, gu_w, gu_m, gu_v, sums, got):
    shapes = [(SHARD, 1, D)] + [w.shape for w in w_rows[1:]]

    row_block = 128

    def columns(ref, p, cols, r0, n):
        if p:
            return ref, (slice(r0, r0 + n), cols)
        flat = ref if ref.shape == (SHARD * LANE_TILES, LANE) else ref.reshape(SHARD * LANE_TILES, LANE)
        return flat, (pl.ds(cols.start // LANE + LANE_TILES * r0, n, stride=LANE_TILES), slice(None))

    def read(ref, p, cols, r0, n):
        ref, at = columns(ref, p, cols, r0, n)
        return ref[at]

    def body(*refs):
        wr_refs, mr_refs, vr_refs = refs[0:4], refs[4:8], refs[8:12]
        guw_ref, gum_ref, guv_ref = refs[12:15]
        s_refs, got_refs = refs[15:17], refs[17:19]
        row_outs = refs[19:35]
        gu_outs = refs[35:39]
        buf, gsh, sems, big, big_sems = refs[39:]
        loads = [pltpu.make_async_copy(r[0], big.at[k], big_sems.at[k]) for k, r in enumerate((wr_refs, mr_refs, vr_refs))]
        for cp in loads:
            cp.start()
        wr_refs, mr_refs, vr_refs = ((big.at[k],) + tuple(r[1:]) for k, r in enumerate((wr_refs, mr_refs, vr_refs)))
        x, y, c = _place()
        me_slot = 4 * x + 2 * y + c
        down = 2 * me_slot

        def total(rows, cols):
            g = buf[0, rows, cols].astype(F32)
            for j in range(1, 4):
                g = g + buf[j, rows, cols].astype(F32)
            return g

        def update(p, grad, cols):
            nrows = shapes[p][0]
            for r0 in range(0, nrows, row_block):
                n = min(row_block, nrows - r0)
                g = grad(r0, n)
                d, nm, nv = _adamw_math(read(wr_refs[p], p, cols, r0, n), g, read(mr_refs[p], p, cols, r0, n),
                                        read(vr_refs[p], p, cols, r0, n))
                for o, val in zip(row_outs[4 * p:4 * p + 4], (g, d, nm, nv)):
                    o, at = columns(o, p, cols, r0, n)
                    o[at] = val

        for hf in range(2):
            _fetch_partials(s_refs[hf], got_refs[hf], buf, sems)
            for cc in range(DH // LANE):
                src = slice(cc * LANE, (cc + 1) * LANE)
                cols = slice(hf * DH + cc * LANE, hf * DH + (cc + 1) * LANE)
                for r0 in range(0, SHARD_PAD, row_block):
                    rows = slice(r0, min(r0 + row_block, SHARD_PAD))
                    gsh[rows, :] = total(rows, src)
                if hf == 0 and cc == 0:
                    for cp in loads:
                        cp.wait()
                update(0, lambda r0, n: gsh[pl.ds(down + r0, n), :], cols)
                for p in range(1, 4):
                    update(p, lambda r0, n, p=p: total(slice(ROW_AT[p] + r0, ROW_AT[p] + r0 + n), src), cols)
            if hf == 0:
                g = total(slice(R_GU, R_GU + RANK), slice(0, 64))
                d, nm, nv = _adamw_math(guw_ref[...], g, gum_ref[...], guv_ref[...])
                for o, val in zip(gu_outs, (g, d, nm, nv)):
                    o[...] = val

    res = pl.pallas_call(
        body, name="finish",
        in_specs=([_any()] + [_vmem()] * 3) * 3 + [_vmem()] * 3 + [_any()] * 4,
        out_specs=[_vmem()] * 20,
        out_shape=[jax.ShapeDtypeStruct(s, F32) for s in shapes for _ in range(4)]
        + [jax.ShapeDtypeStruct((RANK, 64), F32)] * 4,
        scratch_shapes=[pltpu.VMEM((4, ROWS, DH), sums[0].dtype), pltpu.VMEM((SHARD_PAD, LANE), F32),
                        pltpu.SemaphoreType.DMA((4,)),
                        pltpu.VMEM((3, SHARD * LANE_TILES, LANE), F32), pltpu.SemaphoreType.DMA((3,))],
        compiler_params=_cp(),
    )(*w_rows, *m_rows, *v_rows, gu_w, gu_m, gu_v, *sums, *got)
    return tuple(res[0:16]), tuple(res[16:20])


def _place():
    x, y, c = lax.axis_index("x"), lax.axis_index("y"), lax.axis_index("c")
    return x, y, c


def _peers(x, y, c):
    return [(x ^ dx, y ^ dy, c ^ dc) for dx in range(2) for dy in range(2) for dc in range(2) if dx + dy + dc]


def _late_gather_start(blk, after, name="late_gather"):
    land = pltpu.with_memory_space_constraint(lax.empty((NDEV,) + blk.shape, blk.dtype), pltpu.HBM)

    def body(b_ref, land_ref, after_ref, send_sems, recv_sems, b_thru, land_thru, token):
        x, y, c = _place()
        for k, to in enumerate(_peers(x, y, c)):
            pltpu.make_async_remote_copy(
                src_ref=b_ref, dst_ref=land_ref.at[4 * x + 2 * y + c], send_sem=send_sems.at[k],
                recv_sem=recv_sems.at[k], device_id=to, device_id_type=MESH).start()
        token[...] = jnp.zeros_like(token)

    return pl.pallas_call(
        body, name=name + "_start",
        out_shape=(pltpu.SemaphoreType.DMA((7,)), pltpu.SemaphoreType.DMA((7,)),
                   pltpu.HBM(blk.shape, blk.dtype), pltpu.HBM(land.shape, land.dtype),
                   jax.ShapeDtypeStruct((8, LANE), F32)),
        in_specs=(_hbm(), _hbm(), _any()), out_specs=(_sem(), _sem(), _hbm(), _hbm(), _vmem()),
        input_output_aliases={0: 2, 1: 3},
        compiler_params=pltpu.CompilerParams(has_side_effects=_EFFECT),
    )(pltpu.with_memory_space_constraint(blk, pltpu.HBM), land, after)


def _late_gather_wait(send_sems, recv_sems, b_thru, land_thru, after, after2, name="late_gather"):
    def body(b_ref, land_ref, send_sems, recv_sems, after_ref, after2_ref, b_out, got_ref):
        x, y, c = _place()
        copies = [pltpu.make_async_remote_copy(
            src_ref=b_ref, dst_ref=land_ref.at[4 * x + 2 * y + c], send_sem=send_sems.at[k],
            recv_sem=recv_sems.at[k], device_id=to, device_id_type=MESH)
            for k, to in enumerate(_peers(x, y, c))]
        for cp in copies:
            cp.wait_send()
        for cp in copies:
            cp.wait_recv()

    return pl.pallas_call(
        body, name=name + "_wait",
        out_shape=(pltpu.HBM(b_thru.shape, b_thru.dtype), pltpu.HBM(land_thru.shape, land_thru.dtype)),
        in_specs=(_hbm(), _hbm(), _sem(), _sem(), _any(), _any()), out_specs=(_hbm(), _hbm()),
        input_output_aliases={0: 0, 1: 1},
        compiler_params=pltpu.CompilerParams(has_side_effects=_EFFECT),
    )(b_thru, land_thru, send_sems, recv_sems, after, after2)


G_ROWS = SHARD_PAD + RANK


def _gather_blocks(w_in_t, gu_s, xs, norm_w, pos_col):
    rows, cols = G_ROWS, D
    T = xs.shape[0]
    tT = min(T, 256)
    inv_row, sign_row = _rope_rows()

    def body(wi_ref, gu_ref, xs_hbm, nw_ref, pos_ref, inv_ref, sign_ref,
             out_ref, h_ref, cos_ref, sin_ref, x_ref, frame_ref, xs_ref, send_sems, recv_sems, local_sem, xs_sem):
        load_xs = pltpu.make_async_copy(xs_hbm, xs_ref, xs_sem)
        load_xs.start()
        x, y, c = _place()
        me, sibling = (x, y, c), (x, y, 1 - c)
        chips = [(1 - x, y), (x, 1 - y), (1 - x, 1 - y)]
        shift = 2 * (4 * x + 2 * y + c)
        frame_ref[SHARD - SHARD % 8:, :] = jnp.zeros((SHARD_PAD - SHARD + SHARD % 8, LANE), F32)
        for cc in range(LANE_TILES):
            cs = slice(cc * LANE, (cc + 1) * LANE)
            frame_ref[:SHARD, :] = wi_ref[pl.ds(cc, SHARD, stride=LANE_TILES), :]
            x_ref[0:SHARD_PAD, cs] = pltpu.roll(frame_ref[...], shift, 0).astype(x_ref.dtype)
        x_ref[SHARD_PAD:G_ROWS, :] = jnp.zeros((RANK, D), x_ref.dtype)
        x_ref[SHARD_PAD:G_ROWS, 0:64] = gu_ref[...].astype(x_ref.dtype)

        def slot(px, py, pc):
            return out_ref.at[4 * px + 2 * py + pc]

        def copy(k, block, to, src=None):
            return pltpu.make_async_remote_copy(
                src_ref=slot(*block) if src is None else src, dst_ref=slot(*block),
                send_sem=send_sems.at[k], recv_sem=recv_sems.at[k], device_id=to, device_id_type=MESH)

        mine = pltpu.make_async_copy(x_ref, slot(*me), local_sem)
        mine.start()
        first = [copy(0, me, sibling, src=x_ref)]
        first += [copy(1 + j, me, (*chip, c), src=x_ref) for j, chip in enumerate(chips)]
        for cp in first:
            cp.start()
        load_xs.wait()

        @pl.loop(0, T // tT)
        def _(i):
            rows_i = pl.ds(pl.multiple_of(i * tT, tT), tT)
            _prologue_rows(rows_i, xs_ref, nw_ref, pos_ref, inv_ref, sign_ref, h_ref, cos_ref, sin_ref)

        passed = [copy(4 + j, (*chip, c), sibling) for j, chip in enumerate(chips)]
        for j, chip in enumerate(chips):
            copy(1 + j, (*chip, c), me).wait_recv()
            passed[j].start()
        copy(0, sibling, me).wait_recv()
        for j, chip in enumerate(chips):
            copy(4 + j, (*chip, 1 - c), me).wait_recv()
        for cp in first + passed:
            cp.wait_send()
        mine.wait()

    return pl.pallas_call(
        body, name="gather_weights",
        in_specs=[_vmem(), _vmem(), _any()] + [_vmem()] * 4, out_specs=[_any()] + [_vmem()] * 3,
        out_shape=[jax.ShapeDtypeStruct((NDEV, rows, cols), WIRE), jax.ShapeDtypeStruct((T, D), MXU),
                   jax.ShapeDtypeStruct((T, LANE), F32), jax.ShapeDtypeStruct((T, LANE), F32)],
        scratch_shapes=[pltpu.VMEM((rows, cols), WIRE), pltpu.VMEM((SHARD_PAD, LANE), F32), pltpu.VMEM((T, D), F32),
                        pltpu.SemaphoreType.DMA((7,)), pltpu.SemaphoreType.DMA((7,)), pltpu.SemaphoreType.DMA,
                        pltpu.SemaphoreType.DMA],
        compiler_params=_cp(),
    )(w_in_t, gu_s, xs, norm_w, pos_col, inv_row, sign_row)


def _pair_reduce(gwt, tails, half):
    n = gwt.shape[1]
    starts = [SHARD_PAD]
    for t in tails:
        starts.append(starts[-1] + t.shape[1])
    rows = starts[-1]
    blk = (4, rows, n)
    npart = 1 + len(tails)

    def body(*refs):
        g_ref, t_refs = refs[0], refs[1:npart]
        out_ref, acc, got, own, send_sems, recv_sems, own_sems, out_sems = refs[npart:]
        x, y, c = _place()

        def parts(d, dst):
            frame = g_ref.at[pl.ds(pl.multiple_of(FRAME * d, 16), SHARD_PAD)]
            return [(frame, dst.at[0:SHARD_PAD])] + [
                (t_ref.at[d], dst.at[starts[k]:starts[k + 1]]) for k, t_ref in enumerate(t_refs)]

        sends, loads, stores = [], [], []
        for chip in range(4):
            sends.append([pltpu.make_async_remote_copy(
                src_ref=s, dst_ref=d_, send_sem=send_sems.at[chip, k], recv_sem=recv_sems.at[chip, k],
                device_id=(x, y, 1 - c), device_id_type=MESH)
                for k, (s, d_) in enumerate(parts(2 * chip + (1 - c), got.at[chip]))])
            loads.append([pltpu.make_async_copy(s, d_, own_sems.at[chip, k])
                          for k, (s, d_) in enumerate(parts(2 * chip + c, own.at[chip]))])
            stores.append(pltpu.make_async_copy(acc.at[chip], out_ref.at[chip], out_sems.at[chip]))
        for group in sends + loads:
            for cp in group:
                cp.start()
        for chip in range(4):
            for cp in loads[chip]:
                cp.wait()
            for cp in sends[chip]:
                cp.wait_recv()
            acc[chip] = (own[chip].astype(F32) + got[chip].astype(F32)).astype(acc.dtype)
            stores[chip].start()
        for cp in stores:
            cp.wait()
        for group in sends:
            for cp in group:
                cp.wait_send()

    return pl.pallas_call(
        body, name=f"pair_reduce{half}",
        in_specs=[_any()] * npart, out_specs=_any(),
        out_shape=jax.ShapeDtypeStruct(blk, gwt.dtype),
        scratch_shapes=[pltpu.VMEM(blk, gwt.dtype), pltpu.VMEM(blk, gwt.dtype), pltpu.VMEM(blk, gwt.dtype),
                        pltpu.SemaphoreType.DMA((4, npart)), pltpu.SemaphoreType.DMA((4, npart)),
                        pltpu.SemaphoreType.DMA((4, npart)), pltpu.SemaphoreType.DMA((4,))],
        compiler_params=_cp(),
    )(gwt, *tails)


def _pad_cols(a, cols):
    return jnp.pad(a, ((0, 0), (0, cols - a.shape[1])))


def _pad_rows(a, rows):
    return jnp.pad(a, ((0, rows - a.shape[0]), (0, 0)))


FRAME = 928


def _wft_plan():
    moves = []
    for blk in range(8):
        for half in range(2):
            for sub in range(2):
                moves.append((C_Q + 128 * blk + 32 * (2 * half + sub), 128 * blk + 32 * (2 * sub + half), 32))
    for idx in range(4):
        for dup in range(2):
            moves.append((C_KD + 64 * idx + 32 * dup, 1024 + 32 * idx, 32))
    for g in range(2):
        for dup in range(2):
            moves.append((C_VD + 128 * g + 64 * dup, 1152 + 64 * g, 64))
    moves += [(C_BL, 5376, RANK), (C_BV, 3328, 1024), (C_BQ, 2304, 512), (C_BK, 2816, 512),
              (C_AG, 1280, 1024), (C_BG, 4352, 1024), (C_MA, 5392, 1024), (C_MB, 6416, 1024)]
    bulk, seams = [], []
    for dst, src, n in moves:
        r = src
        while r < src + n:
            f = min(r // FRAME, NDEV - 1)
            local = r - FRAME * f
            if f > 0 and local < 16:
                assert local == 0
                seams.append((f, dst + r - src))
                step = 16
            else:
                step = min(src + n, FRAME * (f + 1) if f < NDEV - 1 else IN_WIDTH) - r
                bulk.append((f, local, dst + r - src, step))
            r += step
    assert sorted(f for f, _ in seams) == list(range(1, NDEV))
    return bulk, seams, [(C_BL + RANK, C_GLA - C_BL - RANK)]


def _build_wft_copies(frames):
    bulk, seams, zeros = _wft_plan()
    (z0, zn), = zeros

    def body(f_ref, o_ref, edge, sems, esems):
        copies = [pltpu.make_async_copy(f_ref.at[f, pl.ds(l0, n)], o_ref.at[pl.ds(dst, n)], sems.at[i])
                  for i, (f, l0, dst, n) in enumerate(bulk)]
        loads = []
        for i, (f, _) in enumerate(seams):
            loads.append(pltpu.make_async_copy(f_ref.at[f, pl.ds(0, 16)], edge.at[i, 0], esems.at[i, 0]))
            loads.append(pltpu.make_async_copy(f_ref.at[f - 1, pl.ds(FRAME, 16)], edge.at[i, 1], esems.at[i, 1]))
        for cp in copies + loads:
            cp.start()
        o_ref[z0:z0 + zn, :] = jnp.zeros((zn, D), o_ref.dtype)
        for cp in loads:
            cp.wait()
        for i, (_, dst) in enumerate(seams):
            o_ref[dst:dst + 16, :] = edge[i, 0] + edge[i, 1]
        for cp in copies:
            cp.wait()

    return pl.pallas_call(
        body, name="build_wft",
        in_specs=[_any()], out_specs=_vmem(),
        out_shape=jax.ShapeDtypeStruct((NF, D), frames.dtype),
        scratch_shapes=[pltpu.VMEM((len(seams), 2, 16, D), frames.dtype),
                        pltpu.SemaphoreType.DMA((len(bulk),)), pltpu.SemaphoreType.DMA((len(seams), 2))],
        compiler_params=_cp(),
    )(frames)


def kernel(x, positions, norm_w, w_in, a_sinks, b_gate_up, b_gate_bias, b_out_norm_w, w_a_proj, w_b_proj, w_out, final_norm_w, loss_target, m_norm_w, m_w_in, m_a_sinks, m_b_gate_up, m_b_gate_bias, m_b_out_norm_w, m_w_a_proj, m_w_b_proj, m_w_out, m_final_norm_w, v_norm_w, v_w_in, v_a_sinks, v_b_gate_up, v_b_gate_bias, v_b_out_norm_w, v_w_a_proj, v_w_b_proj, v_w_out, v_final_norm_w):
    T = x.shape[1]
    xs, target = x[0], loss_target[0]
    fnw = final_norm_w.reshape(1, D)
    me = 4 * lax.axis_index("x") + 2 * lax.axis_index("y") + lax.axis_index("c")
    allw, h, cos, sin = _gather_blocks(_by_lane_tile(w_in), b_gate_up[0], xs, norm_w, positions.reshape(T, 1))
    late_blk = jnp.concatenate([w_a_proj[0], w_b_proj[0], w_out[0]], axis=0).astype(WIRE)
    l_send, l_recv, l_blk, l_land, l_started = _late_gather_start(late_blk, cos)
    wf = _build_wft_copies(allw)
    gu = allw[:, SHARD_PAD:G_ROWS, :64].transpose(1, 0, 2).reshape(RANK, 512)
    gu_pad = _pad_rows(gu, W_BL)

    proj = _proj(h, wf, l_started)
    o_a, lse = _swa_fwd(proj, cos, sin, a_sinks)
    o_b, states = _gla_fwd(proj, gu_pad, b_gate_bias)
    l_blk, l_land = _late_gather_wait(l_send, l_recv, l_blk, l_land, states, lse)
    late = lax.dynamic_update_slice(l_land, l_blk[None], (me, 0, 0))
    (dx2, do_a, do_b, d_gates, g_late0, g_late1, g_fn, g_bn, loss_part) = _mid(
        xs, target, proj, o_a, o_b, late, jnp.tile(b_out_norm_w, (1, B_HEADS)), fnw)
    d_q, d_kv, g_sinks = _swa_bwd(proj, cos, sin, a_sinks, do_a, o_a, lse, cos)
    d_gla, d_bl, g_gu, g_bias = _gla_bwd(proj, gu_pad, b_gate_bias, states, do_b)
    pieces = [d_q, d_kv, d_bl, d_gla, d_gates]
    offsets = [C_Q, C_KD, C_BL, C_GLA, C_GATES]

    ggu = g_gu[:RANK].reshape(RANK, NDEV, 64).transpose(1, 0, 2)
    ggu_half = [jnp.pad(ggu, ((0, 0), (0, 0), (0, DH - 64))).astype(WIRE), jnp.zeros((NDEV, RANK, DH), WIRE)]
    tails = [[g_late0, ggu_half[0]], [g_late1, ggu_half[1]]]

    send0, recv0, s_thru0, land0, started0 = _chip_start(_pair_reduce(_gw_half(h, pieces, 0), tails[0], 0), 0)
    send1, recv1, s_thru1, land1, started1 = _chip_start(
        _pair_reduce(_gw_half(h, pieces, 1, after=started0), tails[1], 1), 1)
    grad_x, g_nw = _dh_norm(pieces, offsets, wf, xs, dx2, norm_w, started1)
    small = jnp.concatenate([g_nw, g_fn, _pad_cols(g_bias, D), _pad_cols(g_bn, D), _pad_cols(g_sinks, D),
                             _pad_cols(loss_part, D)], axis=0)
    sm_send, sm_recv, sm_blk, sm_land, sm_started = _late_gather_start(small, g_nw, name="small_gather")
    sums0, got0 = _chip_wait(send0, recv0, s_thru0, land0, sm_started, 0)
    sums1, got1 = _chip_wait(send1, recv1, s_thru1, land1, got0, 1)
    sums, from_chips = [sums0, sums1], [got0, got1]

    ws = dict(norm_w=norm_w, fnw=fnw, bias=b_gate_bias, bn=b_out_norm_w, sinks=a_sinks)
    ms = dict(norm_w=m_norm_w, fnw=m_final_norm_w.reshape(1, D), bias=m_b_gate_bias, bn=m_b_out_norm_w,
              sinks=m_a_sinks)
    vs = dict(norm_w=v_norm_w, fnw=v_final_norm_w.reshape(1, D), bias=v_b_gate_bias, bn=v_b_out_norm_w,
              sinks=v_a_sinks)
    t_rows, t_gu = _finish(
        [_by_lane_tile(w_in), w_a_proj[0], w_b_proj[0], w_out[0]],
        [_by_lane_tile(m_w_in), m_w_a_proj[0], m_w_b_proj[0], m_w_out[0]],
        [_by_lane_tile(v_w_in), v_w_a_proj[0], v_w_b_proj[0], v_w_out[0]],
        b_gate_up[0], m_b_gate_up[0], v_b_gate_up[0], sums, from_chips)
    sm_blk, sm_land = _late_gather_wait(sm_send, sm_recv, sm_blk, sm_land, t_rows[0], t_gu[0], name="small_gather")
    loss, sm = _finish_small(ws, ms, vs, lax.dynamic_update_slice(sm_land, sm_blk[None], (me, 0, 0)))

    def outputs(k):
        return [sm["norm_w"][k], jnp.transpose(t_rows[k], (1, 2, 0)), sm["sinks"][k], t_gu[k][None], sm["bias"][k], sm["bn"][k],
                t_rows[4 + k][None], t_rows[8 + k][None], t_rows[12 + k][None], sm["fnw"][k].reshape(D)]

    return (loss[0, 0], grad_x[None], *outputs(0), *outputs(1), *outputs(2), *outputs(3))
```

```python
import functools

import numpy as np
import jax
import jax.numpy as jnp
from jax import lax
from jax.experimental import pallas as pl
from jax.experimental.pallas import tpu as pltpu

F32 = jnp.float32
MXU = jnp.bfloat16
WIRE = jnp.bfloat16

D = 1024
A_HEADS, A_KV, A_HD = 16, 2, 64
BLK = 128
B_HEADS, B_DK, B_DV = 4, 128, 256
RANK, TAU, CHUNK = 16, 16.0, 64
EPS, NEG = 1e-5, -1e30
ROPE_THETA = 10000.0
IN_WIDTH, NDEV = 7440, 8
SHARD = IN_WIDTH // NDEV
LANE = 128
LANE_TILES = D // LANE


def _by_lane_tile(a):
    return jnp.transpose(a, (2, 0, 1)).reshape(SHARD * LANE_TILES, LANE)


C_Q, C_KD, C_VD, C_BL = 0, 1024, 1280, 1536
C_BV, C_BQ, C_BK = 2048, 3072, 3584
C_AG, C_BG, C_MA, C_MB = 4096, 5120, 6144, 7168
C_GLA, W_GLA, C_GATES, W_GATES = 2048, 2048, 4096, 4096
NF = 8192
W_BL = 128

SHARD_PAD = 944
R_IN, R_A, R_B, R_O, R_GU, ROWS = 0, 944, 1072, 1200, 1328, 1344
SMALL_ROWS = 48

ADAM_LR, ADAM_B1, ADAM_B2, ADAM_EPS, ADAM_WD, ADAM_STEP = 0.001, 0.9, 0.999, 1e-08, 0.01, 10

MESH = pl.DeviceIdType.MESH
VMEM_LIMIT = 56 * 1024 * 1024


def _cp(sem=None, **kw):
    if sem is not None:
        kw["dimension_semantics"] = sem
    return pltpu.CompilerParams(vmem_limit_bytes=VMEM_LIMIT, **kw)


def _dot(a, b):
    return jnp.dot(a, b, preferred_element_type=F32)


def _dot_nt(a, b):
    return lax.dot_general(a, b, (((1,), (1,)), ((), ())), preferred_element_type=F32)


def _dot_tn(a, b):
    return lax.dot_general(a, b, (((0,), (0,)), ((), ())), preferred_element_type=F32)


def _dot_f32(a, b):
    return jnp.dot(a, b, preferred_element_type=F32, precision=lax.Precision.HIGHEST)


def _sigmoid(z):
    return 0.5 * jnp.tanh(0.5 * z) + 0.5


def _rope(xp, cos, sin):
    return xp * cos + pltpu.roll(xp, 64, 1) * sin


def _rope_bwd(dy, cos, sin):
    return dy * cos - pltpu.roll(dy, 64, 1) * sin


def _vmem():
    return pl.BlockSpec(memory_space=pltpu.VMEM)


def _any():
    return pl.BlockSpec(memory_space=pl.ANY)


def _rope_rows():
    half = A_HD // 2
    inv = (np.float32(ROPE_THETA) ** (-np.arange(half, dtype=np.float32) / np.float32(half))).astype(np.float32)
    inv_row = jnp.asarray(np.tile(inv, 4)[None, :])
    sign_row = jnp.asarray(np.concatenate([-np.ones(64, np.float32), np.ones(64, np.float32)])[None, :])
    return inv_row, sign_row


def _prologue_rows(rows, x_ref, nw_ref, pos_ref, inv_ref, sign_ref, h_ref, cos_ref, sin_ref):
    xv = x_ref[rows, :]
    r = lax.rsqrt(jnp.mean(xv * xv, axis=-1, keepdims=True) + EPS)
    h_ref[rows, :] = ((xv * r) * nw_ref[...]).astype(h_ref.dtype)
    ang = pos_ref[rows, :].astype(F32) * inv_ref[...]
    cos_ref[rows, :] = jnp.cos(ang)
    sin_ref[rows, :] = jnp.sin(ang) * sign_ref[...]


def _proj(h, wft, after):
    T = h.shape[0]
    tT, tN = T, 512

    def body(h_ref, w_ref, after_ref, o_ref):
        o_ref[...] = _dot_nt(h_ref[...], w_ref[...])

    return pl.pallas_call(
        body, name="proj", grid=(T // tT, NF // tN),
        in_specs=[pl.BlockSpec((tT, D), lambda i, j: (i, 0)), pl.BlockSpec((tN, D), lambda i, j: (j, 0)), _any()],
        out_specs=pl.BlockSpec((tT, tN), lambda i, j: (i, j)),
        out_shape=jax.ShapeDtypeStruct((T, NF), F32),
        compiler_params=_cp(("parallel", "parallel")),
    )(h, wft, after)


def _swa_masks():
    lane = lax.broadcasted_iota(jnp.int32, (BLK, LANE), 1)
    rope_sub0 = ((lane // 32) % 2) == 0
    std_sub0 = lane < 64
    return lane, rope_sub0, std_sub0


def _swa_tri():
    qi = lax.broadcasted_iota(jnp.int32, (BLK, BLK), 0)
    kj = lax.broadcasted_iota(jnp.int32, (BLK, BLK), 1)
    return kj <= qi


def _swa_fold(full, tri):
    return jnp.where(tri, full[:, BLK:], full[:, :BLK])


def _swa_unfold(sq, tri):
    return jnp.concatenate([jnp.where(tri, 0.0, sq), jnp.where(tri, sq, 0.0)], axis=1)


def _swa_keys(kc_ref, kp_ref, vc_ref, vp_ref, cq, sq, cp, sp):
    def ropek(kref, c, s):
        kv = kref[...]
        return jnp.concatenate([_rope(kv[:, :LANE], c, s), _rope(kv[:, LANE:], c, s)], axis=1)

    K = jnp.concatenate([ropek(kp_ref, cp, sp), ropek(kc_ref, cq, sq)], axis=0).astype(MXU)
    V = jnp.concatenate([vp_ref[...], vc_ref[...]], axis=0).astype(MXU)
    return K, V


def _swa_in_specs(nb, last):
    def cur(n):
        return jnp.minimum(n, last)

    def prev(n):
        return jnp.maximum(cur(n) - 1, 0)

    kd, vd = C_KD // 256, C_VD // 256
    return [
        pl.BlockSpec((BLK, D), lambda n: (cur(n), C_Q // D)),
        pl.BlockSpec((BLK, 256), lambda n: (cur(n), kd)),
        pl.BlockSpec((BLK, 256), lambda n: (prev(n), kd)),
        pl.BlockSpec((BLK, 256), lambda n: (cur(n), vd)),
        pl.BlockSpec((BLK, 256), lambda n: (prev(n), vd)),
        pl.BlockSpec((BLK, LANE), lambda n: (cur(n), 0)),
        pl.BlockSpec((BLK, LANE), lambda n: (cur(n), 0)),
        pl.BlockSpec((BLK, LANE), lambda n: (prev(n), 0)),
        pl.BlockSpec((BLK, LANE), lambda n: (prev(n), 0)),
    ]


def _swa_fwd(proj, cos, sin, sinks):
    T = proj.shape[0]
    nb = T // BLK
    scale = A_HD ** -0.5

    def body(sinks_ref, q_ref, kc_ref, kp_ref, vc_ref, vp_ref, cq_ref, sq_ref, cp_ref, sp_ref, o_ref, l_ref):
        n = pl.program_id(0)
        cq, sq = cq_ref[...], sq_ref[...]
        K, V = _swa_keys(kc_ref, kp_ref, vc_ref, vp_ref, cq, sq, cp_ref[...], sp_ref[...])
        tri = _swa_tri()
        valid = tri | (n > 0)
        lane, rope_sub0, std_sub0 = _swa_masks()
        group = A_HEADS // A_KV
        roped, lses = {}, []

        def products(head):
            pb, sub, g = head // 2, head % 2, head // group
            if sub == 0:
                roped[pb] = _rope(q_ref[:, pb * LANE:(pb + 1) * LANE], cq, sq)
            qm = jnp.where(rope_sub0 if sub == 0 else ~rope_sub0, roped[pb], 0.0).astype(MXU)
            return _dot_nt(qm, K[:, g * LANE:(g + 1) * LANE])

        def softmax(head, s_full):
            s = jnp.where(valid, _swa_fold(s_full, tri) * scale, NEG)
            sink = sinks_ref[0, head]
            m = jnp.maximum(jnp.max(s, axis=1, keepdims=True), sink)
            e = jnp.exp(s - m)
            den = jnp.sum(e, axis=1, keepdims=True) + jnp.exp(sink - m)
            lses.append(m + jnp.log(den))
            return _swa_unfold(e / den, tri).astype(MXU)

        outs = {}
        st1 = {0: products(0), 1: products(1)}
        st2 = {0: softmax(0, st1.pop(0))}
        for head in range(A_HEADS):
            if head + 2 < A_HEADS:
                st1[head + 2] = products(head + 2)
            if head + 1 < A_HEADS:
                st2[head + 1] = softmax(head + 1, st1.pop(head + 1))
            g = head // group
            outs[head] = _dot(st2.pop(head), V[:, g * LANE:(g + 1) * LANE])
            if head % 2 == 1:
                pb = head // 2
                o_ref[:, pb * LANE:(pb + 1) * LANE] = jnp.where(std_sub0, outs[head - 1], outs[head])
        lacc = jnp.zeros((BLK, LANE), F32)
        for head in range(A_HEADS):
            lacc = jnp.where(lane == head, lses[head], lacc)
        l_ref[...] = lacc

    return pl.pallas_call(
        body, name="swa_fwd", grid=(nb,),
        in_specs=[pl.BlockSpec(memory_space=pltpu.SMEM)] + _swa_in_specs(nb, nb - 1),
        out_specs=[pl.BlockSpec((BLK, D), lambda n: (n, 0)), pl.BlockSpec((BLK, LANE), lambda n: (n, 0))],
        out_shape=[jax.ShapeDtypeStruct((T, D), F32), jax.ShapeDtypeStruct((T, LANE), F32)],
        compiler_params=_cp(("parallel",)),
    )(sinks, proj, proj, proj, proj, proj, cos, sin, cos, sin)


def _swa_bwd(proj, cos, sin, sinks, do_a, o_a, lse, after):
    T = proj.shape[0]
    nb = T // BLK
    scale = A_HD ** -0.5

    def body(sinks_ref, q_ref, kc_ref, kp_ref, vc_ref, vp_ref, cq_ref, sq_ref, cp_ref, sp_ref,
             do_ref, o_ref, l_ref, after_ref, dq_ref, dkv_ref, ds_ref, ckv_ref):
        n = pl.program_id(0)

        @pl.when(n == 0)
        def _():
            ckv_ref[...] = jnp.zeros_like(ckv_ref)
            ds_ref[...] = jnp.zeros_like(ds_ref)

        @pl.when(n < nb)
        def _():
            cq, sq, cp, sp = cq_ref[...], sq_ref[...], cp_ref[...], sp_ref[...]
            K, V = _swa_keys(kc_ref, kp_ref, vc_ref, vp_ref, cq, sq, cp, sp)
            tri = _swa_tri()
            valid = tri | (n > 0)
            lane, rope_sub0, std_sub0 = _swa_masks()
            lane_row = lax.broadcasted_iota(jnp.int32, (1, LANE), 1)
            lse_v = l_ref[...]
            dKt = [jnp.zeros((LANE, 2 * BLK), F32) for _ in range(A_KV)]
            dVt = [jnp.zeros((LANE, 2 * BLK), F32) for _ in range(A_KV)]
            dsinks, roped, roped_t, do_t = [], {}, {}, {}
            group = A_HEADS // A_KV
            dim = lax.broadcasted_iota(jnp.int32, (LANE, BLK), 0)
            rope_row0, std_row0 = ((dim // 32) % 2) == 0, dim < 64

            def products(head):
                pb, sub, g = head // 2, head % 2, head // group
                cols = slice(pb * LANE, (pb + 1) * LANE)
                Kg, Vg = K[:, g * LANE:(g + 1) * LANE], V[:, g * LANE:(g + 1) * LANE]
                if sub == 0:
                    roped[pb] = _rope(q_ref[:, cols], cq, sq)
                    roped_t[pb] = roped[pb].T
                    do_t[pb] = do_ref[:, cols].T
                qm = jnp.where(rope_sub0 if sub == 0 else ~rope_sub0, roped[pb], 0.0).astype(MXU)
                qmt = jnp.where(rope_row0 if sub == 0 else ~rope_row0, roped_t[pb], 0.0).astype(MXU)
                dov = jnp.where(std_sub0 if sub == 0 else ~std_sub0, do_ref[:, cols], 0.0)
                dovt = jnp.where(std_row0 if sub == 0 else ~std_row0, do_t[pb], 0.0).astype(MXU)
                delta = jnp.sum(dov * o_ref[:, cols], axis=1, keepdims=True)
                return qmt, dovt, delta, _dot_nt(qm, Kg), _dot_nt(dov.astype(MXU), Vg)

            def scores(head, qmt, dovt, delta, s_full, dp_full):
                lh = jnp.sum(jnp.where(lane == head, lse_v, 0.0), axis=1, keepdims=True)
                p = jnp.where(valid, jnp.exp(_swa_fold(s_full, tri) * scale - lh), 0.0)
                psink = jnp.exp(sinks_ref[0, head] - lh)
                dsinks.append(jnp.sum(-psink * delta, axis=0, keepdims=True))
                dsq = (p * (_swa_fold(dp_full, tri) - delta)) * scale
                return qmt, dovt, _swa_unfold(p, tri).astype(MXU), _swa_unfold(dsq, tri).astype(MXU)

            def grads(head, qmt, dovt, pb16, dsc):
                g = head // group
                dKt[g] = dKt[g] + _dot(qmt, dsc)
                dVt[g] = dVt[g] + _dot(dovt, pb16)
                return _dot(dsc, K[:, g * LANE:(g + 1) * LANE])

            dqs = {}
            st1 = {0: products(0), 1: products(1)}
            st2 = {0: scores(0, *st1.pop(0))}
            for head in range(A_HEADS):
                if head + 2 < A_HEADS:
                    st1[head + 2] = products(head + 2)
                if head + 1 < A_HEADS:
                    st2[head + 1] = scores(head + 1, *st1.pop(head + 1))
                dqs[head] = grads(head, *st2.pop(head))
                if head % 2 == 1:
                    pb = head // 2
                    dqp = jnp.where(rope_sub0, dqs[head - 1], dqs[head])
                    dq_ref[:, pb * LANE:(pb + 1) * LANE] = _rope_bwd(dqp, cq, sq).astype(dq_ref.dtype)
            dsink = jnp.zeros((1, LANE), F32)
            for head in range(A_HEADS):
                dsink = jnp.where(lane_row == head, dsinks[head], dsink)
            dK, dV = [a.T for a in dKt], [a.T for a in dVt]
            prev = ([_rope_bwd(dK[g][:BLK], cp, sp) for g in range(A_KV)] + [dV[g][:BLK] for g in range(A_KV)])
            cur_ = ([_rope_bwd(dK[g][BLK:], cq, sq) for g in range(A_KV)] + [dV[g][BLK:] for g in range(A_KV)])
            dkv_ref[...] = (ckv_ref[...] + jnp.concatenate(prev, axis=1)).astype(dkv_ref.dtype)
            ckv_ref[...] = jnp.concatenate(cur_, axis=1)
            ds_ref[...] = ds_ref[...] + jnp.broadcast_to(dsink, ds_ref.shape)

        @pl.when(n == nb)
        def _():
            dkv_ref[...] = ckv_ref[...].astype(dkv_ref.dtype)

    last = nb - 1

    def cur(n):
        return jnp.minimum(n, last)

    def out_kv(n):
        return (jnp.maximum(n - 1, 0), 0)

    return pl.pallas_call(
        body, name="swa_bwd", grid=(nb + 1,),
        in_specs=[pl.BlockSpec(memory_space=pltpu.SMEM)] + _swa_in_specs(nb, last) + [
            pl.BlockSpec((BLK, D), lambda n: (cur(n), 0)),
            pl.BlockSpec((BLK, D), lambda n: (cur(n), 0)),
            pl.BlockSpec((BLK, LANE), lambda n: (cur(n), 0)),
            _any(),
        ],
        out_specs=[
            pl.BlockSpec((BLK, D), lambda n: (cur(n), 0)),
            pl.BlockSpec((BLK, 512), out_kv),
            pl.BlockSpec((8, LANE), lambda n: (0, 0)),
        ],
        out_shape=[
            jax.ShapeDtypeStruct((T, D), MXU),
            jax.ShapeDtypeStruct((T, 512), MXU),
            jax.ShapeDtypeStruct((8, LANE), F32),
        ],
        scratch_shapes=[pltpu.VMEM((BLK, 512), F32)],
        compiler_params=_cp(("arbitrary",)),
    )(sinks, proj, proj, proj, proj, proj, cos, sin, cos, sin, do_a, o_a, lse, after)


NCH = 4
GSTEP = NCH * CHUNK
ST_ROWS = B_HEADS * B_DV


def _chunk_rows(c):
    return slice(c * CHUNK, (c + 1) * CHUNK)


def _per_chunk(which, vals):
    out = vals[-1]
    for c in range(NCH - 2, -1, -1):
        out = jnp.where(which == c, vals[c], out)
    return out


def _gla_gate(bl_ref, gu_ref, bias_ref):
    gk = _dot(bl_ref[...].astype(MXU), gu_ref[...]) + bias_ref[...]
    la = (jnp.minimum(gk, 0.0) - jnp.log(1.0 + jnp.exp(-jnp.abs(gk)))) / TAU
    ri = lax.broadcasted_iota(jnp.int32, (GSTEP, GSTEP), 0)
    ci = lax.broadcasted_iota(jnp.int32, (GSTEP, GSTEP), 1)
    same = (ri // CHUNK) == (ci // CHUNK)
    lower, upper = same & (ci <= ri), same & (ci >= ri)
    b = _dot_f32(jnp.where(lower, 1.0, 0.0).astype(F32), la)
    which = lax.broadcasted_iota(jnp.int32, (GSTEP, 1), 0) // CHUNK
    return gk, la, b, lower, upper, which


def _gla_head(q_ref, k_ref, la, b, which, h):
    sl = slice(h * B_DK, (h + 1) * B_DK)
    bh, lah = b[:, sl], la[:, sl]
    bls = [jnp.sum(lah[_chunk_rows(c)], axis=0, keepdims=True) for c in range(NCH)]
    blast = _per_chunk(which, bls)
    qc = q_ref[:, sl] * (B_DK ** -0.5)
    kh = k_ref[:, sl]
    eb, enb, esb = jnp.exp(bh), jnp.exp(-bh), jnp.exp(blast - bh)
    return qc * eb, kh * enb, kh * esb, eb, enb, esb, [jnp.exp(v) for v in bls]


def _gla_specs(step_of):
    return [
        pl.BlockSpec((GSTEP, 512), lambda i: (step_of(i), C_BQ // 512)),
        pl.BlockSpec((GSTEP, 512), lambda i: (step_of(i), C_BK // 512)),
        pl.BlockSpec((GSTEP, D), lambda i: (step_of(i), C_BV // D)),
        pl.BlockSpec((GSTEP, W_BL), lambda i: (step_of(i), C_BL // W_BL)),
        pl.BlockSpec((W_BL, 512), lambda i: (0, 0)),
        pl.BlockSpec((1, 512), lambda i: (0, 0)),
    ]


def _gla_fwd(proj, gu_pad, bias):
    T = proj.shape[0]
    ns = T // GSTEP

    def body(q_ref, k_ref, v_ref, bl_ref, gu_ref, bias_ref, o_ref, st_ref, state_ref):
        @pl.when(pl.program_id(0) == 0)
        def _():
            state_ref[...] = jnp.zeros_like(state_ref)

        _, la, b, lower, _, which = _gla_gate(bl_ref, gu_ref, bias_ref)

        def within(h):
            q_e, k_e, k_s, _, _, _, decays = _gla_head(q_ref, k_ref, la, b, which, h)
            vh = v_ref[:, h * B_DV:(h + 1) * B_DV].astype(MXU)
            q_eb = q_e.astype(MXU)
            att = jnp.where(lower, _dot_nt(q_eb, k_e.astype(MXU)), 0.0)
            return vh, q_eb, k_s.astype(MXU), _dot(att.astype(MXU), vh), decays

        def across(h, vh, q_eb, k_sb, o_intra, decays):
            rows = slice(h * B_DV, (h + 1) * B_DV)
            s = state_ref[rows, :]
            outs = []
            for c in range(NCH):
                cr = _chunk_rows(c)
                st_ref[c * ST_ROWS + h * B_DV:c * ST_ROWS + (h + 1) * B_DV, :] = s
                outs.append(o_intra[cr] + _dot_nt(q_eb[cr], s.astype(MXU)))
                s = s * decays[c] + _dot_tn(vh[cr], k_sb[cr])
            state_ref[rows, :] = s
            o_ref[:, rows] = jnp.concatenate(outs, axis=0)

        for h in range(B_HEADS):
            across(h, *within(h))

    return pl.pallas_call(
        body, name="gla_fwd", grid=(ns,),
        in_specs=_gla_specs(lambda i: i),
        out_specs=[pl.BlockSpec((GSTEP, D), lambda i: (i, 0)),
                   pl.BlockSpec((NCH * ST_ROWS, B_DK), lambda i: (i, 0))],
        out_shape=[jax.ShapeDtypeStruct((T, D), F32),
                   jax.ShapeDtypeStruct((ns * NCH * ST_ROWS, B_DK), F32)],
        scratch_shapes=[pltpu.VMEM((ST_ROWS, B_DK), F32)],
        compiler_params=_cp(("arbitrary",)),
    )(proj, proj, proj, proj, gu_pad, bias)


def _gla_bwd(proj, gu_pad, bias, states, do_b):
    T = proj.shape[0]
    ns = T // GSTEP
    o_q, o_k = C_BQ - C_GLA, C_BK - C_GLA

    def body(q_ref, k_ref, v_ref, bl_ref, gu_ref, bias_ref, st_ref, do_ref,
             dg_ref, dbl_ref, ggu_ref, gbias_ref, gt_ref):
        @pl.when(pl.program_id(0) == 0)
        def _():
            gt_ref[...] = jnp.zeros_like(gt_ref)
            ggu_ref[...] = jnp.zeros_like(ggu_ref)
            gbias_ref[...] = jnp.zeros_like(gbias_ref)

        gk, la, b, lower, upper_mask, which = _gla_gate(bl_ref, gu_ref, bias_ref)
        upper = jnp.where(upper_mask, 1.0, 0.0).astype(F32)
        dla_parts = []

        def within(h):
            q_e, k_e, k_s, eb, enb, esb, decays = _gla_head(q_ref, k_ref, la, b, which, h)
            vh = v_ref[:, h * B_DV:(h + 1) * B_DV].astype(MXU)
            doh = do_ref[:, h * B_DV:(h + 1) * B_DV].astype(MXU)
            q_eb, k_eb = q_e.astype(MXU), k_e.astype(MXU)
            att = jnp.where(lower, _dot_nt(q_eb, k_eb), 0.0).astype(MXU)
            datt = jnp.where(lower, _dot_nt(doh, vh), 0.0).astype(MXU)
            return (q_e, k_e, k_s, eb, enb, esb, decays, vh, doh, q_eb, k_s.astype(MXU),
                    _dot(datt, k_eb), _dot_tn(datt, q_eb), _dot_tn(att, doh))

        def across(h, q_e, k_e, k_s, eb, enb, esb, decays, vh, doh, q_eb, k_sb, dq_i, dk_e, dv_i):
            rows = slice(h * B_DV, (h + 1) * B_DV)
            g = gt_ref[rows, :]
            dq_c, dks_c, dv_c, ddec = [None] * NCH, [None] * NCH, [None] * NCH, [None] * NCH
            for c in range(NCH - 1, -1, -1):
                cr = _chunk_rows(c)
                s = st_ref[c * ST_ROWS + h * B_DV:c * ST_ROWS + (h + 1) * B_DV, :]
                gb = g.astype(MXU)
                dq_c[c] = dq_i[cr] + _dot(doh[cr], s.astype(MXU))
                dks_c[c] = _dot(vh[cr], gb)
                dv_c[c] = dv_i[cr] + _dot_nt(k_sb[cr], gb)
                ddec[c] = jnp.sum(g * s, axis=0, keepdims=True)
                g = g * decays[c] + _dot_tn(doh[cr], q_eb[cr])
            gt_ref[rows, :] = g
            dq_e = jnp.concatenate(dq_c, axis=0)
            dk_s = jnp.concatenate(dks_c, axis=0)
            dg_ref[:, rows] = jnp.concatenate(dv_c, axis=0).astype(dg_ref.dtype)
            dg_ref[:, o_q + h * B_DK:o_q + (h + 1) * B_DK] = (dq_e * eb * (B_DK ** -0.5)).astype(dg_ref.dtype)
            dg_ref[:, o_k + h * B_DK:o_k + (h + 1) * B_DK] = (dk_e * enb + dk_s * esb).astype(dg_ref.dtype)
            dks_ks = dk_s * k_s
            db = dq_e * q_e - dk_e * k_e - dks_ks
            dbl = [jnp.sum(dks_ks[_chunk_rows(c)], axis=0, keepdims=True) + ddec[c] * decays[c] for c in range(NCH)]
            dla_parts.append(_dot_f32(upper, db) + _per_chunk(which, dbl))

        for h in range(B_HEADS):
            across(h, *within(h))
        dla = jnp.concatenate(dla_parts, axis=1)
        dgk = dla * (1.0 / TAU) * _sigmoid(-gk)
        dgkb = dgk.astype(MXU)
        dbl_ref[...] = _dot_nt(dgkb, gu_ref[...]).astype(dbl_ref.dtype)
        ggu_ref[...] = ggu_ref[...] + _dot_tn(bl_ref[...].astype(MXU), dgkb)
        gbias_ref[...] = gbias_ref[...] + jnp.broadcast_to(jnp.sum(dgk, axis=0, keepdims=True), gbias_ref.shape)

    def rev(i):
        return ns - 1 - i

    return pl.pallas_call(
        body, name="gla_bwd", grid=(ns,),
        in_specs=_gla_specs(rev) + [
            pl.BlockSpec((NCH * ST_ROWS, B_DK), lambda i: (rev(i), 0)),
            pl.BlockSpec((GSTEP, D), lambda i: (rev(i), 0)),
        ],
        out_specs=[
            pl.BlockSpec((GSTEP, W_GLA), lambda i: (rev(i), 0)),
            pl.BlockSpec((GSTEP, W_BL), lambda i: (rev(i), 0)),
            pl.BlockSpec((W_BL, 512), lambda i: (0, 0)),
            pl.BlockSpec((8, 512), lambda i: (0, 0)),
        ],
        out_shape=[
            jax.ShapeDtypeStruct((T, W_GLA), MXU),
            jax.ShapeDtypeStruct((T, W_BL), MXU),
            jax.ShapeDtypeStruct((W_BL, 512), F32),
            jax.ShapeDtypeStruct((8, 512), F32),
        ],
        scratch_shapes=[pltpu.VMEM((B_HEADS * B_DV, B_DK), F32)],
        compiler_params=_cp(("arbitrary",)),
    )(proj, proj, proj, proj, gu_pad, bias, states, do_b)


def _mid(x, target, proj, o_a, o_b, late, w_bn4, fnw):
    T = x.shape[0]
    tT = min(T, 128)
    nbuf = 4
    o_ag, o_bg, o_ma, o_mb = (c - C_GATES for c in (C_AG, C_BG, C_MA, C_MB))

    def body(x_ref, t_ref, oa_ref, ob_ref, gates_ref, late_ref, wbn_ref, fnw_ref,
             dx2_ref, doa_ref, dob_ref, dgates_ref,
             tail0_ref, tail1_ref, gfn_ref, gbn_ref, loss_ref, buf_ref, gw_ref):
        i = pl.program_id(0)

        def weight(p):
            return late_ref[:, 128 * p:128 * (p + 1), :].reshape(D, D)

        @pl.when(i == 0)
        def _():
            for r in (gw_ref, gfn_ref, gbn_ref, loss_ref):
                r[...] = jnp.zeros_like(r)

        rows = pl.ds(pl.multiple_of((i % nbuf) * tT, tT), tT)

        def keep(k, val):
            buf_ref[k, rows, :] = val

        oa, ag = oa_ref[...], gates_ref[:, o_ag:o_ag + D]
        sg_a = _sigmoid(ag)
        silu_a = ag * sg_a
        oag_b = (oa * silu_a).astype(MXU)
        keep(0, oag_b)
        y_a = _dot(oag_b, weight(0))

        ob, bg = ob_ref[...], gates_ref[:, o_bg:o_bg + D]
        rbs, obhats = [], []
        for h in range(B_HEADS):
            obh = ob[:, h * B_DV:(h + 1) * B_DV]
            rb = lax.rsqrt(jnp.mean(obh * obh, axis=-1, keepdims=True) + EPS)
            rbs.append(rb)
            obhats.append(obh * rb)
        obhat = jnp.concatenate(obhats, axis=1)
        wbn = wbn_ref[...]
        obn = obhat * wbn
        sg_b = _sigmoid(bg)
        silu_b = bg * sg_b
        obg_b = (obn * silu_b).astype(MXU)
        keep(1, obg_b)
        y_b = _dot(obg_b, weight(1))

        sa, sb = _sigmoid(gates_ref[:, o_ma:o_ma + D]), _sigmoid(gates_ref[:, o_mb:o_mb + D])
        mg_b = (sa * y_a + sb * y_b).astype(MXU)
        keep(2, mg_b)
        x2 = x_ref[...] + _dot(mg_b, weight(2))
        r2 = lax.rsqrt(jnp.mean(x2 * x2, axis=-1, keepdims=True) + EPS)
        xh2 = x2 * r2
        fw = fnw_ref[...]
        err = xh2 * fw - t_ref[...]
        tok = jnp.mean(err * err, axis=-1, keepdims=True)
        loss_ref[...] = loss_ref[...] + 0.5 * jnp.sum(tok, axis=0, keepdims=True)

        dy = err * (1.0 / D)
        gfn_ref[...] = gfn_ref[...] + jnp.broadcast_to(jnp.sum(dy * xh2, axis=0, keepdims=True), gfn_ref.shape)
        gy = dy * fw
        dx2 = r2 * (gy - xh2 * jnp.mean(gy * xh2, axis=-1, keepdims=True))
        dx2_ref[...] = dx2
        dx2_b = dx2.astype(MXU)
        keep(5, dx2_b)
        dmg = _dot_nt(dx2_b, weight(2))

        dgates_ref[:, o_ma:o_ma + D] = (dmg * y_a * sa * (1.0 - sa)).astype(dgates_ref.dtype)
        dgates_ref[:, o_mb:o_mb + D] = (dmg * y_b * sb * (1.0 - sb)).astype(dgates_ref.dtype)
        dya_b = (dmg * sa).astype(MXU)
        dyb_b = (dmg * sb).astype(MXU)
        keep(3, dya_b)
        keep(4, dyb_b)
        doag = _dot_nt(dya_b, weight(0))
        dobg = _dot_nt(dyb_b, weight(1))

        @pl.when(i % nbuf == nbuf - 1)
        def _():
            for p in range(3):
                gw_ref[p] = gw_ref[p] + _dot_tn(buf_ref[p], buf_ref[3 + p])

        @pl.when(i == pl.num_programs(0) - 1)
        def _():
            for hf, tail_ref in enumerate((tail0_ref, tail1_ref)):
                for d in range(NDEV):
                    for p in range(3):
                        tail_ref[d, 128 * p:128 * (p + 1), :] = (
                            gw_ref[p, 128 * d:128 * (d + 1), hf * DH:(hf + 1) * DH].astype(tail_ref.dtype))

        doa_ref[...] = doag * silu_a
        dgates_ref[:, o_ag:o_ag + D] = (doag * oa * (sg_a * (1.0 + ag * (1.0 - sg_a)))).astype(dgates_ref.dtype)
        dobn = dobg * silu_b
        dgates_ref[:, o_bg:o_bg + D] = (dobg * obn * (sg_b * (1.0 + bg * (1.0 - sg_b)))).astype(dgates_ref.dtype)
        gg = dobn * wbn
        gbn = jnp.zeros((1, B_DV), F32)
        for h in range(B_HEADS):
            sl = slice(h * B_DV, (h + 1) * B_DV)
            gbn = gbn + jnp.sum(dobn[:, sl] * obhats[h], axis=0, keepdims=True)
            ggh = gg[:, sl]
            dob_ref[:, sl] = rbs[h] * (ggh - obhats[h] * jnp.mean(ggh * obhats[h], axis=-1, keepdims=True))
        gbn_ref[...] = gbn_ref[...] + jnp.broadcast_to(gbn, gbn_ref.shape)

    assert (T // tT) % nbuf == 0
    tile = pl.BlockSpec((tT, D), lambda i: (i, 0))
    row = pl.BlockSpec((1, D), lambda i: (0, 0))
    acc8 = pl.BlockSpec((8, D), lambda i: (0, 0))
    return pl.pallas_call(
        body, name="mid", grid=(T // tT,),
        in_specs=[tile, tile, tile, tile, pl.BlockSpec((tT, W_GATES), lambda i: (i, C_GATES // W_GATES)),
                  _vmem(), row, row],
        out_specs=[tile, tile, tile, pl.BlockSpec((tT, W_GATES), lambda i: (i, 0)), _vmem(), _vmem(),
                   acc8, pl.BlockSpec((8, B_DV), lambda i: (0, 0)), pl.BlockSpec((8, LANE), lambda i: (0, 0))],
        out_shape=[
            jax.ShapeDtypeStruct((T, D), F32),
            jax.ShapeDtypeStruct((T, D), F32),
            jax.ShapeDtypeStruct((T, D), F32),
            jax.ShapeDtypeStruct((T, W_GATES), MXU),
            jax.ShapeDtypeStruct((NDEV, 384, DH), WIRE),
            jax.ShapeDtypeStruct((NDEV, 384, DH), WIRE),
            jax.ShapeDtypeStruct((8, D), F32),
            jax.ShapeDtypeStruct((8, B_DV), F32),
            jax.ShapeDtypeStruct((8, LANE), F32),
        ],
        scratch_shapes=[pltpu.VMEM((6, nbuf * tT, D), MXU), pltpu.VMEM((3, D, D), F32)],
        compiler_params=_cp(("arbitrary",)),
    )(x, target, o_a, o_b, proj, late, w_bn4, fnw)


DH = D // 2


_GW_TILES = (("q", 0, 512, 0), ("q", 1, 512, 512), ("kv", 0, 256, 1024), ("bl", 0, RANK, 5376),
             ("gla", 0, 512, 3328), ("gla", 1, 512, 3840), ("gla", 2, 512, 2304), ("gla", 3, 512, 2816),
             ("gates", 0, 512, 1280), ("gates", 1, 512, 1792), ("gates", 2, 512, 4352), ("gates", 3, 512, 4864),
             ("gates", 4, 512, 5392), ("gates", 5, 512, 5904), ("gates", 6, 512, 6416), ("gates", 7, 512, 6928))


def _gw_unpermute(piece, t):
    if piece == "q":
        parts = []
        for blk in range(t.shape[0] // LANE):
            g = [t[blk * LANE + 32 * i:blk * LANE + 32 * (i + 1)] for i in range(4)]
            parts += [g[0], g[2], g[1], g[3]]
        return jnp.concatenate(parts, axis=0)
    if piece == "kv":
        k = [t[64 * i:64 * i + 32] + t[64 * i + 32:64 * i + 64] for i in range(4)]
        v = [t[256 + 128 * g:256 + 128 * g + 64] + t[256 + 128 * g + 64:256 + 128 * (g + 1)] for g in range(2)]
        return jnp.concatenate(k + v, axis=0)
    if piece == "bl":
        return t[:RANK]
    return t


def _gw_half(h, pieces, half, after=None):
    T = h.shape[0]
    steps = len(_GW_TILES)

    def body(*refs):
        h_ref = refs[0]
        srcs = dict(zip(("q", "kv", "bl", "gla", "gates"), refs[1:6]))
        o_ref, stage, sems = refs[-3:]
        j = pl.program_id(0)

        def out_copy(k):
            _, _, n, off = _GW_TILES[k]
            return pltpu.make_async_copy(stage.at[k % 2, 0:n], o_ref.at[pl.ds(off, n)], sems.at[k % 2])

        for k, (piece, _, n, _) in enumerate(_GW_TILES):
            @pl.when(j == k)
            def _(k=k, piece=piece, n=n):
                if k >= 2:
                    out_copy(k - 2).wait()
                t = _gw_unpermute(piece, _dot_tn(srcs[piece][...], h_ref[...]))
                stage[k % 2, 0:n, :] = t.astype(stage.dtype)
                out_copy(k).start()

        @pl.when(j == steps - 1)
        def _():
            out_copy(steps - 2).wait()
            out_copy(steps - 1).wait()

    def tile_of(lo, hi):
        return lambda j: (0, jnp.clip(j - lo, 0, hi - lo - 1))

    in_specs = [pl.BlockSpec((T, DH), lambda j: (0, half)),
                pl.BlockSpec((T, 512), tile_of(0, 2)), pl.BlockSpec((T, 512), lambda j: (0, 0)),
                pl.BlockSpec((T, W_BL), lambda j: (0, 0)),
                pl.BlockSpec((T, 512), tile_of(4, 8)), pl.BlockSpec((T, 512), tile_of(8, 16))]
    args = [h, *pieces]
    if after is not None:
        in_specs.append(_any())
        args.append(after)
    return pl.pallas_call(
        body, name=f"gw_in_half{half}", grid=(steps,),
        in_specs=in_specs, out_specs=_any(),
        out_shape=jax.ShapeDtypeStruct((IN_WIDTH, DH), WIRE),
        scratch_shapes=[pltpu.VMEM((2, 512, DH), WIRE), pltpu.SemaphoreType.DMA((2,))],
        compiler_params=_cp(("arbitrary",)),
    )(*args)


def _chip_copies(s_ref, got_ref, send_sems, recv_sems):
    x, y, c = _place()
    chips = [(1 - x, y), (x, 1 - y), (1 - x, 1 - y)]
    return [pltpu.make_async_remote_copy(
        src_ref=s_ref.at[2 * px + py], dst_ref=got_ref.at[j],
        send_sem=send_sems.at[j], recv_sem=recv_sems.at[j], device_id=(px, py, c), device_id_type=MESH)
        for j, (px, py) in enumerate(chips)]


_EFFECT = pltpu.SideEffectType.DATAFLOW_SIDE_EFFECTING


def _hbm():
    return pl.BlockSpec(memory_space=pltpu.HBM)


def _sem():
    return pl.BlockSpec(memory_space=pltpu.SEMAPHORE)


def _chip_start(sums, half):
    land = pltpu.with_memory_space_constraint(lax.empty((3,) + sums.shape[1:], sums.dtype), pltpu.HBM)

    def body(s_ref, land_ref, send_sems, recv_sems, s_thru, land_thru, token):
        for cp in _chip_copies(s_ref, land_ref, send_sems, recv_sems):
            cp.start()
        token[...] = jnp.zeros_like(token)

    return pl.pallas_call(
        body, name=f"chip_start{half}",
        out_shape=(pltpu.SemaphoreType.DMA((3,)), pltpu.SemaphoreType.DMA((3,)),
                   pltpu.HBM(sums.shape, sums.dtype), pltpu.HBM(land.shape, land.dtype),
                   jax.ShapeDtypeStruct((8, LANE), F32)),
        in_specs=(_hbm(), _hbm()), out_specs=(_sem(), _sem(), _hbm(), _hbm(), _vmem()),
        input_output_aliases={0: 2, 1: 3},
        compiler_params=pltpu.CompilerParams(has_side_effects=_EFFECT),
    )(pltpu.with_memory_space_constraint(sums, pltpu.HBM), land)


def _chip_wait(send_sems, recv_sems, s_thru, land_thru, after, half):
    def body(s_ref, land_ref, send_sems, recv_sems, after_ref, s_out, got_ref):
        copies = _chip_copies(s_ref, land_ref, send_sems, recv_sems)
        for cp in copies:
            cp.wait_send()
        for cp in copies:
            cp.wait_recv()

    return pl.pallas_call(
        body, name=f"chip_wait{half}",
        out_shape=(pltpu.HBM(s_thru.shape, s_thru.dtype), pltpu.HBM(land_thru.shape, land_thru.dtype)),
        in_specs=(_hbm(), _hbm(), _sem(), _sem(), _any()), out_specs=(_hbm(), _hbm()),
        input_output_aliases={0: 0, 1: 1},
        compiler_params=pltpu.CompilerParams(has_side_effects=_EFFECT),
    )(s_thru, land_thru, send_sems, recv_sems, after)


def _dh_norm(pieces, offsets, wf, x, dx2, norm_w, after):
    T = x.shape[0]
    tT = min(T, 256)
    widths = [p.shape[1] for p in pieces]
    npc = len(pieces)

    def body(*refs):
        dp_refs = refs[:npc]
        wf_ref, x_ref, dx2_ref, nw_ref, _, gx_ref, gnw_ref = refs[npc:]

        @pl.when(pl.program_id(0) == 0)
        def _():
            gnw_ref[...] = jnp.zeros_like(gnw_ref)

        dh = jnp.zeros((tT, D), F32)
        for dp_ref, off, w in zip(dp_refs, offsets, widths):
            dh = dh + _dot(dp_ref[...], wf_ref[off:off + w, :])
        xv = x_ref[...]
        r = lax.rsqrt(jnp.mean(xv * xv, axis=-1, keepdims=True) + EPS)
        xh = xv * r
        gnw_ref[...] = gnw_ref[...] + jnp.broadcast_to(jnp.sum(dh * xh, axis=0, keepdims=True), gnw_ref.shape)
        g = dh * nw_ref[...]
        gx_ref[...] = r * (g - xh * jnp.mean(g * xh, axis=-1, keepdims=True)) + dx2_ref[...]

    tile = pl.BlockSpec((tT, D), lambda i: (i, 0))
    return pl.pallas_call(
        body, name="dh_norm", grid=(T // tT,),
        in_specs=[pl.BlockSpec((tT, w), lambda i: (i, 0)) for w in widths]
        + [_vmem(), tile, tile, pl.BlockSpec((1, D), lambda i: (0, 0)), _any()],
        out_specs=[tile, pl.BlockSpec((8, D), lambda i: (0, 0))],
        out_shape=[jax.ShapeDtypeStruct((T, D), F32), jax.ShapeDtypeStruct((8, D), F32)],
        compiler_params=_cp(("arbitrary",)),
    )(*pieces, wf, x, dx2, norm_w, after)


def _adamw_math(w, g, m, v):
    m = ADAM_B1 * m + (1.0 - ADAM_B1) * g
    v = ADAM_B2 * v + (1.0 - ADAM_B2) * (g * g)
    m_hat = m * (1.0 / (1.0 - ADAM_B1 ** ADAM_STEP))
    v_hat = v * (1.0 / (1.0 - ADAM_B2 ** ADAM_STEP))
    delta = -ADAM_LR * (m_hat / (jnp.sqrt(v_hat) + ADAM_EPS) + ADAM_WD * w)
    return delta, m, v


def _fetch_partials(s_ref, got_ref, buf, sems):
    x, y, _ = _place()
    cps = [pltpu.make_async_copy(s_ref.at[2 * x + y], buf.at[0], sems.at[0])]
    cps += [pltpu.make_async_copy(got_ref.at[j], buf.at[1 + j], sems.at[1 + j]) for j in range(3)]
    for cp in cps:
        cp.start()
    for cp in cps:
        cp.wait()


SMALL_AT = dict(norm_w=0, fnw=8, bias=16, bn=24, sinks=32, loss=40)
ROW_AT = (R_IN, R_A, R_B, R_O)


def _finish_small(ws, ms, vs, smalls):
    names = ["norm_w", "fnw", "bias", "bn", "sinks"]
    widths = [ws[n].shape[1] for n in names]

    def body(*refs):
        w_refs, m_refs, v_refs = refs[0:5], refs[5:10], refs[10:15]
        smalls_ref, loss_ref = refs[15], refs[16]
        outs, tot = refs[17:37], refs[37]
        acc = smalls_ref[0]
        for d in range(1, NDEV):
            acc = acc + smalls_ref[d]
        tot[...] = acc
        loss_ref[...] = tot[SMALL_AT["loss"]:SMALL_AT["loss"] + 1, 0:1]
        for p, (nm_, wd) in enumerate(zip(names, widths)):
            r = SMALL_AT[nm_]
            g = tot[r:r + 1, 0:wd]
            d, nm, nv = _adamw_math(w_refs[p][...], g, m_refs[p][...], v_refs[p][...])
            for o, val in zip(outs[4 * p:4 * p + 4], (g, d, nm, nv)):
                o[...] = val

    res = pl.pallas_call(
        body, name="finish_small",
        in_specs=[_vmem()] * 16, out_specs=[_vmem()] * 21,
        out_shape=[jax.ShapeDtypeStruct((1, 1), F32)]
        + [jax.ShapeDtypeStruct((1, wd), F32) for wd in widths for _ in range(4)],
        scratch_shapes=[pltpu.VMEM((SMALL_ROWS, D), F32)],
        compiler_params=_cp(),
    )(*[ws[n] for n in names], *[ms[n] for n in names], *[vs[n] for n in names], smalls)
    return res[0], {n: tuple(res[1 + 4 * p:5 + 4 * p]) for p, n in enumerate(names)}


def _finish(w_rows, m_rows, v_rows, gu_w, gu_m, gu_v, sums, got):
    shapes = [(SHARD, 1, D)] + [w.shape for w in w_rows[1:]]

    row_block = 96

    def columns(ref, p, cols, r0, n):
        if p:
            return ref, (slice(r0, r0 + n), cols)
        flat = ref if ref.shape == (SHARD * LANE_TILES, LANE) else ref.reshape(SHARD * LANE_TILES, LANE)
        return flat, (pl.ds(cols.start // LANE + LANE_TILES * r0, n, stride=LANE_TILES), slice(None))

    def read(ref, p, cols, r0, n):
        ref, at = columns(ref, p, cols, r0, n)
        return ref[at]

    def body(*refs):
        wr_refs, mr_refs, vr_refs = refs[0:4], refs[4:8], refs[8:12]
        guw_ref, gum_ref, guv_ref = refs[12:15]
        s_refs, got_refs = refs[15:17], refs[17:19]
        row_outs = refs[19:35]
        gu_outs = refs[35:39]
        buf, gsh, sems, big, big_sems = refs[39:]
        loads = [pltpu.make_async_copy(r[0], big.at[k], big_sems.at[k]) for k, r in enumerate((wr_refs, mr_refs, vr_refs))]
        for cp in loads:
            cp.start()
        wr_refs, mr_refs, vr_refs = ((big.at[k],) + tuple(r[1:]) for k, r in enumerate((wr_refs, mr_refs, vr_refs)))
        x, y, c = _place()
        me_slot = 4 * x + 2 * y + c
        down = 2 * me_slot

        def total(rows, cols):
            g = buf[0, rows, cols].astype(F32)
            for j in range(1, 4):
                g = g + buf[j, rows, cols].astype(F32)
            return g

        def update(p, grad, cols):
            nrows = shapes[p][0]
            for r0 in range(0, nrows, row_block):
                n = min(row_block, nrows - r0)
                g = grad(r0, n)
                d, nm, nv = _adamw_math(read(wr_refs[p], p, cols, r0, n), g, read(mr_refs[p], p, cols, r0, n),
                                        read(vr_refs[p], p, cols, r0, n))
                for o, val in zip(row_outs[4 * p:4 * p + 4], (g, d, nm, nv)):
                    o, at = columns(o, p, cols, r0, n)
                    o[at] = val

        for hf in range(2):
            _fetch_partials(s_refs[hf], got_refs[hf], buf, sems)
            for cc in range(DH // LANE):
                src = slice(cc * LANE, (cc + 1) * LANE)
                cols = slice(hf * DH + cc * LANE, hf * DH + (cc + 1) * LANE)
                for r0 in range(0, SHARD_PAD, row_block):
                    rows = slice(r0, min(r0 + row_block, SHARD_PAD))
                    gsh[rows, :] = total(rows, src)
                if hf == 0 and cc == 0:
                    for cp in loads:
                        cp.wait()
                update(0, lambda r0, n: gsh[pl.ds(down + r0, n), :], cols)
                for p in range(1, 4):
                    update(p, lambda r0, n, p=p: total(slice(ROW_AT[p] + r0, ROW_AT[p] + r0 + n), src), cols)
            if hf == 0:
                g = total(slice(R_GU, R_GU + RANK), slice(0, 64))
                d, nm, nv = _adamw_math(guw_ref[...], g, gum_ref[...], guv_ref[...])
                for o, val in zip(gu_outs, (g, d, nm, nv)):
                    o[...] = val

    res = pl.pallas_call(
        body, name="finish",
        in_specs=([_any()] + [_vmem()] * 3) * 3 + [_vmem()] * 3 + [_any()] * 4,
        out_specs=[_vmem()] * 20,
        out_shape=[jax.ShapeDtypeStruct(s, F32) for s in shapes for _ in range(4)]
        + [jax.ShapeDtypeStruct((RANK, 64), F32)] * 4,
        scratch_shapes=[pltpu.VMEM((4, ROWS, DH), sums[0].dtype), pltpu.VMEM((SHARD_PAD, LANE), F32),
                        pltpu.SemaphoreType.DMA((4,)),
                        pltpu.VMEM((3, SHARD * LANE_TILES, LANE), F32), pltpu.SemaphoreType.DMA((3,))],
        compiler_params=_cp(),
    )(*w_rows, *m_rows, *v_rows, gu_w, gu_m, gu_v, *sums, *got)
    return tuple(res[0:16]), tuple(res[16:20])


def _place():
    x, y, c = lax.axis_index("x"), lax.axis_index("y"), lax.axis_index("c")
    return x, y, c


def _peers(x, y, c):
    return [(x ^ dx, y ^ dy, c ^ dc) for dx in range(2) for dy in range(2) for dc in range(2) if dx + dy + dc]


def _late_gather_start(blk, after, name="late_gather"):
    land = pltpu.with_memory_space_constraint(lax.empty((NDEV,) + blk.shape, blk.dtype), pltpu.HBM)

    def body(b_ref, land_ref, after_ref, send_sems, recv_sems, b_thru, land_thru, token):
        x, y, c = _place()
        for k, to in enumerate(_peers(x, y, c)):
            pltpu.make_async_remote_copy(
                src_ref=b_ref, dst_ref=land_ref.at[4 * x + 2 * y + c], send_sem=send_sems.at[k],
                recv_sem=recv_sems.at[k], device_id=to, device_id_type=MESH).start()
        token[...] = jnp.zeros_like(token)

    return pl.pallas_call(
        body, name=name + "_start",
        out_shape=(pltpu.SemaphoreType.DMA((7,)), pltpu.SemaphoreType.DMA((7,)),
                   pltpu.HBM(blk.shape, blk.dtype), pltpu.HBM(land.shape, land.dtype),
                   jax.ShapeDtypeStruct((8, LANE), F32)),
        in_specs=(_hbm(), _hbm(), _any()), out_specs=(_sem(), _sem(), _hbm(), _hbm(), _vmem()),
        input_output_aliases={0: 2, 1: 3},
        compiler_params=pltpu.CompilerParams(has_side_effects=_EFFECT),
    )(pltpu.with_memory_space_constraint(blk, pltpu.HBM), land, after)


def _late_gather_wait(send_sems, recv_sems, b_thru, land_thru, after, after2, name="late_gather"):
    def body(b_ref, land_ref, send_sems, recv_sems, after_ref, after2_ref, b_out, got_ref):
        x, y, c = _place()
        copies = [pltpu.make_async_remote_copy(
            src_ref=b_ref, dst_ref=land_ref.at[4 * x + 2 * y + c], send_sem=send_sems.at[k],
            recv_sem=recv_sems.at[k], device_id=to, device_id_type=MESH)
            for k, to in enumerate(_peers(x, y, c))]
        for cp in copies:
            cp.wait_send()
        for cp in copies:
            cp.wait_recv()

    return pl.pallas_call(
        body, name=name + "_wait",
        out_shape=(pltpu.HBM(b_thru.shape, b_thru.dtype), pltpu.HBM(land_thru.shape, land_thru.dtype)),
        in_specs=(_hbm(), _hbm(), _sem(), _sem(), _any(), _any()), out_specs=(_hbm(), _hbm()),
        input_output_aliases={0: 0, 1: 1},
        compiler_params=pltpu.CompilerParams(has_side_effects=_EFFECT),
    )(b_thru, land_thru, send_sems, recv_sems, after, after2)


G_ROWS = SHARD_PAD + RANK


def _gather_blocks(w_in_t, gu_s, xs, norm_w, pos_col):
    rows, cols = G_ROWS, D
    T = xs.shape[0]
    tT = min(T, 256)
    inv_row, sign_row = _rope_rows()

    def body(wi_ref, gu_ref, xs_hbm, nw_ref, pos_ref, inv_ref, sign_ref,
             out_ref, h_ref, cos_ref, sin_ref, x_ref, frame_ref, xs_ref, send_sems, recv_sems, local_sem, xs_sem):
        load_xs = pltpu.make_async_copy(xs_hbm, xs_ref, xs_sem)
        load_xs.start()
        x, y, c = _place()
        me, sibling = (x, y, c), (x, y, 1 - c)
        chips = [(1 - x, y), (x, 1 - y), (1 - x, 1 - y)]
        shift = 2 * (4 * x + 2 * y + c)
        frame_ref[SHARD - SHARD % 8:, :] = jnp.zeros((SHARD_PAD - SHARD + SHARD % 8, LANE), F32)
        for cc in range(LANE_TILES):
            cs = slice(cc * LANE, (cc + 1) * LANE)
            frame_ref[:SHARD, :] = wi_ref[pl.ds(cc, SHARD, stride=LANE_TILES), :]
            x_ref[0:SHARD_PAD, cs] = pltpu.roll(frame_ref[...], shift, 0).astype(x_ref.dtype)
        x_ref[SHARD_PAD:G_ROWS, :] = jnp.zeros((RANK, D), x_ref.dtype)
        x_ref[SHARD_PAD:G_ROWS, 0:64] = gu_ref[...].astype(x_ref.dtype)

        def slot(px, py, pc):
            return out_ref.at[4 * px + 2 * py + pc]

        def copy(k, block, to, src=None):
            return pltpu.make_async_remote_copy(
                src_ref=slot(*block) if src is None else src, dst_ref=slot(*block),
                send_sem=send_sems.at[k], recv_sem=recv_sems.at[k], device_id=to, device_id_type=MESH)

        mine = pltpu.make_async_copy(x_ref, slot(*me), local_sem)
        mine.start()
        first = [copy(0, me, sibling, src=x_ref)]
        first += [copy(1 + j, me, (*chip, c), src=x_ref) for j, chip in enumerate(chips)]
        for cp in first:
            cp.start()
        load_xs.wait()

        @pl.loop(0, T // tT)
        def _(i):
            rows_i = pl.ds(pl.multiple_of(i * tT, tT), tT)
            _prologue_rows(rows_i, xs_ref, nw_ref, pos_ref, inv_ref, sign_ref, h_ref, cos_ref, sin_ref)

        passed = [copy(4 + j, (*chip, c), sibling) for j, chip in enumerate(chips)]
        for j, chip in enumerate(chips):
            copy(1 + j, (*chip, c), me).wait_recv()
            passed[j].start()
        copy(0, sibling, me).wait_recv()
        for j, chip in enumerate(chips):
            copy(4 + j, (*chip, 1 - c), me).wait_recv()
        for cp in first + passed:
            cp.wait_send()
        mine.wait()

    return pl.pallas_call(
        body, name="gather_weights",
        in_specs=[_vmem(), _vmem(), _any()] + [_vmem()] * 4, out_specs=[_any()] + [_vmem()] * 3,
        out_shape=[jax.ShapeDtypeStruct((NDEV, rows, cols), WIRE), jax.ShapeDtypeStruct((T, D), MXU),
                   jax.ShapeDtypeStruct((T, LANE), F32), jax.ShapeDtypeStruct((T, LANE), F32)],
        scratch_shapes=[pltpu.VMEM((rows, cols), WIRE), pltpu.VMEM((SHARD_PAD, LANE), F32), pltpu.VMEM((T, D), F32),
                        pltpu.SemaphoreType.DMA((7,)), pltpu.SemaphoreType.DMA((7,)), pltpu.SemaphoreType.DMA,
                        pltpu.SemaphoreType.DMA],
        compiler_params=_cp(),
    )(w_in_t, gu_s, xs, norm_w, pos_col, inv_row, sign_row)


def _pair_reduce(gwt, tails, half):
    n = gwt.shape[1]
    starts = [SHARD_PAD]
    for t in tails:
        starts.append(starts[-1] + t.shape[1])
    rows = starts[-1]
    blk = (4, rows, n)
    npart = 1 + len(tails)

    def body(*refs):
        g_ref, t_refs = refs[0], refs[1:npart]
        out_ref, acc, got, own, send_sems, recv_sems, own_sems, out_sems = refs[npart:]
        x, y, c = _place()

        def parts(d, dst):
            frame = g_ref.at[pl.ds(pl.multiple_of(FRAME * d, 16), SHARD_PAD)]
            return [(frame, dst.at[0:SHARD_PAD])] + [
                (t_ref.at[d], dst.at[starts[k]:starts[k + 1]]) for k, t_ref in enumerate(t_refs)]

        sends, loads, stores = [], [], []
        for chip in range(4):
            sends.append([pltpu.make_async_remote_copy(
                src_ref=s, dst_ref=d_, send_sem=send_sems.at[chip, k], recv_sem=recv_sems.at[chip, k],
                device_id=(x, y, 1 - c), device_id_type=MESH)
                for k, (s, d_) in enumerate(parts(2 * chip + (1 - c), got.at[chip]))])
            loads.append([pltpu.make_async_copy(s, d_, own_sems.at[chip, k])
                          for k, (s, d_) in enumerate(parts(2 * chip + c, own.at[chip]))])
            stores.append(pltpu.make_async_copy(acc.at[chip], out_ref.at[chip], out_sems.at[chip]))
        for group in sends + loads:
            for cp in group:
                cp.start()
        for chip in range(4):
            for cp in loads[chip]:
                cp.wait()
            for cp in sends[chip]:
                cp.wait_recv()
            acc[chip] = (own[chip].astype(F32) + got[chip].astype(F32)).astype(acc.dtype)
            stores[chip].start()
        for cp in stores:
            cp.wait()
        for group in sends:
            for cp in group:
                cp.wait_send()

    return pl.pallas_call(
        body, name=f"pair_reduce{half}",
        in_specs=[_any()] * npart, out_specs=_any(),
        out_shape=jax.ShapeDtypeStruct(blk, gwt.dtype),
        scratch_shapes=[pltpu.VMEM(blk, gwt.dtype), pltpu.VMEM(blk, gwt.dtype), pltpu.VMEM(blk, gwt.dtype),
                        pltpu.SemaphoreType.DMA((4, npart)), pltpu.SemaphoreType.DMA((4, npart)),
                        pltpu.SemaphoreType.DMA((4, npart)), pltpu.SemaphoreType.DMA((4,))],
        compiler_params=_cp(),
    )(gwt, *tails)


def _pad_cols(a, cols):
    return jnp.pad(a, ((0, 0), (0, cols - a.shape[1])))


def _pad_rows(a, rows):
    return jnp.pad(a, ((0, rows - a.shape[0]), (0, 0)))


FRAME = 928


def _wft_plan():
    moves = []
    for blk in range(8):
        for half in range(2):
            for sub in range(2):
                moves.append((C_Q + 128 * blk + 32 * (2 * half + sub), 128 * blk + 32 * (2 * sub + half), 32))
    for idx in range(4):
        for dup in range(2):
            moves.append((C_KD + 64 * idx + 32 * dup, 1024 + 32 * idx, 32))
    for g in range(2):
        for dup in range(2):
            moves.append((C_VD + 128 * g + 64 * dup, 1152 + 64 * g, 64))
    moves += [(C_BL, 5376, RANK), (C_BV, 3328, 1024), (C_BQ, 2304, 512), (C_BK, 2816, 512),
              (C_AG, 1280, 1024), (C_BG, 4352, 1024), (C_MA, 5392, 1024), (C_MB, 6416, 1024)]
    bulk, seams = [], []
    for dst, src, n in moves:
        r = src
        while r < src + n:
            f = min(r // FRAME, NDEV - 1)
            local = r - FRAME * f
            if f > 0 and local < 16:
                assert local == 0
                seams.append((f, dst + r - src))
                step = 16
            else:
                step = min(src + n, FRAME * (f + 1) if f < NDEV - 1 else IN_WIDTH) - r
                bulk.append((f, local, dst + r - src, step))
            r += step
    assert sorted(f for f, _ in seams) == list(range(1, NDEV))
    return bulk, seams, [(C_BL + RANK, C_GLA - C_BL - RANK)]


def _build_wft_copies(frames):
    bulk, seams, zeros = _wft_plan()
    (z0, zn), = zeros

    def body(f_ref, o_ref, edge, sems, esems):
        copies = [pltpu.make_async_copy(f_ref.at[f, pl.ds(l0, n)], o_ref.at[pl.ds(dst, n)], sems.at[i])
                  for i, (f, l0, dst, n) in enumerate(bulk)]
        loads = []
        for i, (f, _) in enumerate(seams):
            loads.append(pltpu.make_async_copy(f_ref.at[f, pl.ds(0, 16)], edge.at[i, 0], esems.at[i, 0]))
            loads.append(pltpu.make_async_copy(f_ref.at[f - 1, pl.ds(FRAME, 16)], edge.at[i, 1], esems.at[i, 1]))
        for cp in copies + loads:
            cp.start()
        o_ref[z0:z0 + zn, :] = jnp.zeros((zn, D), o_ref.dtype)
        for cp in loads:
            cp.wait()
        for i, (_, dst) in enumerate(seams):
            o_ref[dst:dst + 16, :] = edge[i, 0] + edge[i, 1]
        for cp in copies:
            cp.wait()

    return pl.pallas_call(
        body, name="build_wft",
        in_specs=[_any()], out_specs=_vmem(),
        out_shape=jax.ShapeDtypeStruct((NF, D), frames.dtype),
        scratch_shapes=[pltpu.VMEM((len(seams), 2, 16, D), frames.dtype),
                        pltpu.SemaphoreType.DMA((len(bulk),)), pltpu.SemaphoreType.DMA((len(seams), 2))],
        compiler_params=_cp(),
    )(frames)


def kernel(x, positions, norm_w, w_in, a_sinks, b_gate_up, b_gate_bias, b_out_norm_w, w_a_proj, w_b_proj, w_out, final_norm_w, loss_target, m_norm_w, m_w_in, m_a_sinks, m_b_gate_up, m_b_gate_bias, m_b_out_norm_w, m_w_a_proj, m_w_b_proj, m_w_out, m_final_norm_w, v_norm_w, v_w_in, v_a_sinks, v_b_gate_up, v_b_gate_bias, v_b_out_norm_w, v_w_a_proj, v_w_b_proj, v_w_out, v_final_norm_w):
    T = x.shape[1]
    xs, target = x[0], loss_target[0]
    fnw = final_norm_w.reshape(1, D)
    me = 4 * lax.axis_index("x") + 2 * lax.axis_index("y") + lax.axis_index("c")
    allw, h, cos, sin = _gather_blocks(_by_lane_tile(w_in), b_gate_up[0], xs, norm_w, positions.reshape(T, 1))
    late_blk = jnp.concatenate([w_a_proj[0], w_b_proj[0], w_out[0]], axis=0).astype(WIRE)
    l_send, l_recv, l_blk, l_land, l_started = _late_gather_start(late_blk, cos)
    wf = _build_wft_copies(allw)
    gu = allw[:, SHARD_PAD:G_ROWS, :64].transpose(1, 0, 2).reshape(RANK, 512)
    gu_pad = _pad_rows(gu, W_BL)

    proj = _proj(h, wf, l_started)
    o_a, lse = _swa_fwd(proj, cos, sin, a_sinks)
    o_b, states = _gla_fwd(proj, gu_pad, b_gate_bias)
    l_blk, l_land = _late_gather_wait(l_send, l_recv, l_blk, l_land, states, lse)
    late = lax.dynamic_update_slice(l_land, l_blk[None], (me, 0, 0))
    (dx2, do_a, do_b, d_gates, g_late0, g_late1, g_fn, g_bn, loss_part) = _mid(
        xs, target, proj, o_a, o_b, late, jnp.tile(b_out_norm_w, (1, B_HEADS)), fnw)
    d_q, d_kv, g_sinks = _swa_bwd(proj, cos, sin, a_sinks, do_a, o_a, lse, cos)
    d_gla, d_bl, g_gu, g_bias = _gla_bwd(proj, gu_pad, b_gate_bias, states, do_b)
    pieces = [d_q, d_kv, d_bl, d_gla, d_gates]
    offsets = [C_Q, C_KD, C_BL, C_GLA, C_GATES]

    ggu = g_gu[:RANK].reshape(RANK, NDEV, 64).transpose(1, 0, 2)
    ggu_half = [jnp.pad(ggu, ((0, 0), (0, 0), (0, DH - 64))).astype(WIRE), jnp.zeros((NDEV, RANK, DH), WIRE)]
    tails = [[g_late0, ggu_half[0]], [g_late1, ggu_half[1]]]

    send0, recv0, s_thru0, land0, started0 = _chip_start(_pair_reduce(_gw_half(h, pieces, 0), tails[0], 0), 0)
    send1, recv1, s_thru1, land1, started1 = _chip_start(
        _pair_reduce(_gw_half(h, pieces, 1, after=started0), tails[1], 1), 1)
    grad_x, g_nw = _dh_norm(pieces, offsets, wf, xs, dx2, norm_w, started1)
    small = jnp.concatenate([g_nw, g_fn, _pad_cols(g_bias, D), _pad_cols(g_bn, D), _pad_cols(g_sinks, D),
                             _pad_cols(loss_part, D)], axis=0)
    sm_send, sm_recv, sm_blk, sm_land, sm_started = _late_gather_start(small, g_nw, name="small_gather")
    sums0, got0 = _chip_wait(send0, recv0, s_thru0, land0, sm_started, 0)
    sums1, got1 = _chip_wait(send1, recv1, s_thru1, land1, got0, 1)
    sums, from_chips = [sums0, sums1], [got0, got1]

    ws = dict(norm_w=norm_w, fnw=fnw, bias=b_gate_bias, bn=b_out_norm_w, sinks=a_sinks)
    ms = dict(norm_w=m_norm_w, fnw=m_final_norm_w.reshape(1, D), bias=m_b_gate_bias, bn=m_b_out_norm_w,
              sinks=m_a_sinks)
    vs = dict(norm_w=v_norm_w, fnw=v_final_norm_w.reshape(1, D), bias=v_b_gate_bias, bn=v_b_out_norm_w,
              sinks=v_a_sinks)
    t_rows, t_gu = _finish(
        [_by_lane_tile(w_in), w_a_proj[0], w_b_proj[0], w_out[0]],
        [_by_lane_tile(m_w_in), m_w_a_proj[0], m_w_b_proj[0], m_w_out[0]],
        [_by_lane_tile(v_w_in), v_w_a_proj[0], v_w_b_proj[0], v_w_out[0]],
        b_gate_up[0], m_b_gate_up[0], v_b_gate_up[0], sums, from_chips)
    sm_blk, sm_land = _late_gather_wait(sm_send, sm_recv, sm_blk, sm_land, t_rows[0], t_gu[0], name="small_gather")
    loss, sm = _finish_small(ws, ms, vs, lax.dynamic_update_slice(sm_land, sm_blk[None], (me, 0, 0)))

    def outputs(k):
        return [sm["norm_w"][k], jnp.transpose(t_rows[k], (1, 2, 0)), sm["sinks"][k], t_gu[k][None], sm["bias"][k], sm["bn"][k],
                t_rows[4 + k][None], t_rows[8 + k][None], t_rows[12 + k][None], sm["fnw"][k].reshape(D)]

    return (loss[0, 0], grad_x[None], *outputs(0), *outputs(1), *outputs(2), *outputs(3))
```

```python
import functools

import numpy as np
import jax
import jax.numpy as jnp
from jax import lax
from jax.experimental import pallas as pl
from jax.experimental.pallas import tpu as pltpu

F32 = jnp.float32
MXU = jnp.bfloat16
WIRE = jnp.bfloat16

D = 1024
A_HEADS, A_KV, A_HD = 16, 2, 64
BLK = 128
B_HEADS, B_DK, B_DV = 4, 128, 256
RANK, TAU, CHUNK = 16, 16.0, 64
EPS, NEG = 1e-5, -1e30
ROPE_THETA = 10000.0
IN_WIDTH, NDEV = 7440, 8
SHARD = IN_WIDTH // NDEV
LANE = 128
LANE_TILES = D // LANE


def _by_lane_tile(a):
    return jnp.transpose(a, (2, 0, 1)).reshape(SHARD * LANE_TILES, LANE)


C_Q, C_KD, C_VD, C_BL = 0, 1024, 1280, 1536
C_BV, C_BQ, C_BK = 2048, 3072, 3584
C_AG, C_BG, C_MA, C_MB = 4096, 5120, 6144, 7168
C_GLA, W_GLA, C_GATES, W_GATES = 2048, 2048, 4096, 4096
NF = 8192
W_BL = 128

SHARD_PAD = 944
R_IN, R_A, R_B, R_O, R_GU, ROWS = 0, 944, 1072, 1200, 1328, 1344
SMALL_ROWS = 48

ADAM_LR, ADAM_B1, ADAM_B2, ADAM_EPS, ADAM_WD, ADAM_STEP = 0.001, 0.9, 0.999, 1e-08, 0.01, 10

MESH = pl.DeviceIdType.MESH
VMEM_LIMIT = 56 * 1024 * 1024


def _cp(sem=None, **kw):
    if sem is not None:
        kw["dimension_semantics"] = sem
    return pltpu.CompilerParams(vmem_limit_bytes=VMEM_LIMIT, **kw)


def _dot(a, b):
    return jnp.dot(a, b, preferred_element_type=F32)


def _dot_nt(a, b):
    return lax.dot_general(a, b, (((1,), (1,)), ((), ())), preferred_element_type=F32)


def _dot_tn(a, b):
    return lax.dot_general(a, b, (((0,), (0,)), ((), ())), preferred_element_type=F32)


def _dot_f32(a, b):
    return jnp.dot(a, b, preferred_element_type=F32, precision=lax.Precision.HIGHEST)


def _sigmoid(z):
    return 0.5 * jnp.tanh(0.5 * z) + 0.5


def _rope(xp, cos, sin):
    return xp * cos + pltpu.roll(xp, 64, 1) * sin


def _rope_bwd(dy, cos, sin):
    return dy * cos - pltpu.roll(dy, 64, 1) * sin


def _vmem():
    return pl.BlockSpec(memory_space=pltpu.VMEM)


def _any():
    return pl.BlockSpec(memory_space=pl.ANY)


def _rope_rows():
    half = A_HD // 2
    inv = (np.float32(ROPE_THETA) ** (-np.arange(half, dtype=np.float32) / np.float32(half))).astype(np.float32)
    inv_row = jnp.asarray(np.tile(inv, 4)[None, :])
    sign_row = jnp.asarray(np.concatenate([-np.ones(64, np.float32), np.ones(64, np.float32)])[None, :])
    return inv_row, sign_row


def _prologue_rows(rows, x_ref, nw_ref, pos_ref, inv_ref, sign_ref, h_ref, cos_ref, sin_ref):
    xv = x_ref[rows, :]
    r = lax.rsqrt(jnp.mean(xv * xv, axis=-1, keepdims=True) + EPS)
    h_ref[rows, :] = ((xv * r) * nw_ref[...]).astype(h_ref.dtype)
    ang = pos_ref[rows, :].astype(F32) * inv_ref[...]
    cos_ref[rows, :] = jnp.cos(ang)
    sin_ref[rows, :] = jnp.sin(ang) * sign_ref[...]


def _proj(h, wft, after):
    T = h.shape[0]
    tT, tN = T, 512

    def body(h_ref, w_ref, after_ref, o_ref):
        o_ref[...] = _dot_nt(h_ref[...], w_ref[...])

    return pl.pallas_call(
        body, name="proj", grid=(T // tT, NF // tN),
        in_specs=[pl.BlockSpec((tT, D), lambda i, j: (i, 0)), pl.BlockSpec((tN, D), lambda i, j: (j, 0)), _any()],
        out_specs=pl.BlockSpec((tT, tN), lambda i, j: (i, j)),
        out_shape=jax.ShapeDtypeStruct((T, NF), F32),
        compiler_params=_cp(("parallel", "parallel")),
    )(h, wft, after)


def _swa_masks():
    lane = lax.broadcasted_iota(jnp.int32, (BLK, LANE), 1)
    rope_sub0 = ((lane // 32) % 2) == 0
    std_sub0 = lane < 64
    return lane, rope_sub0, std_sub0


def _swa_tri():
    qi = lax.broadcasted_iota(jnp.int32, (BLK, BLK), 0)
    kj = lax.broadcasted_iota(jnp.int32, (BLK, BLK), 1)
    return kj <= qi


def _swa_fold(full, tri):
    return jnp.where(tri, full[:, BLK:], full[:, :BLK])


def _swa_unfold(sq, tri):
    return jnp.concatenate([jnp.where(tri, 0.0, sq), jnp.where(tri, sq, 0.0)], axis=1)


def _swa_keys(kc_ref, kp_ref, vc_ref, vp_ref, cq, sq, cp, sp):
    def ropek(kref, c, s):
        kv = kref[...]
        return jnp.concatenate([_rope(kv[:, :LANE], c, s), _rope(kv[:, LANE:], c, s)], axis=1)

    K = jnp.concatenate([ropek(kp_ref, cp, sp), ropek(kc_ref, cq, sq)], axis=0).astype(MXU)
    V = jnp.concatenate([vp_ref[...], vc_ref[...]], axis=0).astype(MXU)
    return K, V


def _swa_in_specs(nb, last):
    def cur(n):
        return jnp.minimum(n, last)

    def prev(n):
        return jnp.maximum(cur(n) - 1, 0)

    kd, vd = C_KD // 256, C_VD // 256
    return [
        pl.BlockSpec((BLK, D), lambda n: (cur(n), C_Q // D)),
        pl.BlockSpec((BLK, 256), lambda n: (cur(n), kd)),
        pl.BlockSpec((BLK, 256), lambda n: (prev(n), kd)),
        pl.BlockSpec((BLK, 256), lambda n: (cur(n), vd)),
        pl.BlockSpec((BLK, 256), lambda n: (prev(n), vd)),
        pl.BlockSpec((BLK, LANE), lambda n: (cur(n), 0)),
        pl.BlockSpec((BLK, LANE), lambda n: (cur(n), 0)),
        pl.BlockSpec((BLK, LANE), lambda n: (prev(n), 0)),
        pl.BlockSpec((BLK, LANE), lambda n: (prev(n), 0)),
    ]


def _swa_fwd(proj, cos, sin, sinks):
    T = proj.shape[0]
    nb = T // BLK
    scale = A_HD ** -0.5

    def body(sinks_ref, q_ref, kc_ref, kp_ref, vc_ref, vp_ref, cq_ref, sq_ref, cp_ref, sp_ref, o_ref, l_ref):
        n = pl.program_id(0)
        cq, sq = cq_ref[...], sq_ref[...]
        K, V = _swa_keys(kc_ref, kp_ref, vc_ref, vp_ref, cq, sq, cp_ref[...], sp_ref[...])
        tri = _swa_tri()
        valid = tri | (n > 0)
        lane, rope_sub0, std_sub0 = _swa_masks()
        group = A_HEADS // A_KV
        roped, lses = {}, []

        def products(head):
            pb, sub, g = head // 2, head % 2, head // group
            if sub == 0:
                roped[pb] = _rope(q_ref[:, pb * LANE:(pb + 1) * LANE], cq, sq)
            qm = jnp.where(rope_sub0 if sub == 0 else ~rope_sub0, roped[pb], 0.0).astype(MXU)
            return _dot_nt(qm, K[:, g * LANE:(g + 1) * LANE])

        def softmax(head, s_full):
            s = jnp.where(valid, _swa_fold(s_full, tri) * scale, NEG)
            sink = sinks_ref[0, head]
            m = jnp.maximum(jnp.max(s, axis=1, keepdims=True), sink)
            e = jnp.exp(s - m)
            den = jnp.sum(e, axis=1, keepdims=True) + jnp.exp(sink - m)
            lses.append(m + jnp.log(den))
            return _swa_unfold(e / den, tri).astype(MXU)

        outs = {}
        st1 = {0: products(0), 1: products(1)}
        st2 = {0: softmax(0, st1.pop(0))}
        for head in range(A_HEADS):
            if head + 2 < A_HEADS:
                st1[head + 2] = products(head + 2)
            if head + 1 < A_HEADS:
                st2[head + 1] = softmax(head + 1, st1.pop(head + 1))
            g = head // group
            outs[head] = _dot(st2.pop(head), V[:, g * LANE:(g + 1) * LANE])
            if head % 2 == 1:
                pb = head // 2
                o_ref[:, pb * LANE:(pb + 1) * LANE] = jnp.where(std_sub0, outs[head - 1], outs[head])
        lacc = jnp.zeros((BLK, LANE), F32)
        for head in range(A_HEADS):
            lacc = jnp.where(lane == head, lses[head], lacc)
        l_ref[...] = lacc

    return pl.pallas_call(
        body, name="swa_fwd", grid=(nb,),
        in_specs=[pl.BlockSpec(memory_space=pltpu.SMEM)] + _swa_in_specs(nb, nb - 1),
        out_specs=[pl.BlockSpec((BLK, D), lambda n: (n, 0)), pl.BlockSpec((BLK, LANE), lambda n: (n, 0))],
        out_shape=[jax.ShapeDtypeStruct((T, D), F32), jax.ShapeDtypeStruct((T, LANE), F32)],
        compiler_params=_cp(("parallel",)),
    )(sinks, proj, proj, proj, proj, proj, cos, sin, cos, sin)


def _swa_bwd(proj, cos, sin, sinks, do_a, o_a, lse, after):
    T = proj.shape[0]
    nb = T // BLK
    scale = A_HD ** -0.5

    def body(sinks_ref, q_ref, kc_ref, kp_ref, vc_ref, vp_ref, cq_ref, sq_ref, cp_ref, sp_ref,
             do_ref, o_ref, l_ref, after_ref, dq_ref, dkv_ref, ds_ref, ckv_ref):
        n = pl.program_id(0)

        @pl.when(n == 0)
        def _():
            ckv_ref[...] = jnp.zeros_like(ckv_ref)
            ds_ref[...] = jnp.zeros_like(ds_ref)

        @pl.when(n < nb)
        def _():
            cq, sq, cp, sp = cq_ref[...], sq_ref[...], cp_ref[...], sp_ref[...]
            K, V = _swa_keys(kc_ref, kp_ref, vc_ref, vp_ref, cq, sq, cp, sp)
            tri = _swa_tri()
            valid = tri | (n > 0)
            lane, rope_sub0, std_sub0 = _swa_masks()
            lane_row = lax.broadcasted_iota(jnp.int32, (1, LANE), 1)
            lse_v = l_ref[...]
            dKt = [jnp.zeros((LANE, 2 * BLK), F32) for _ in range(A_KV)]
            dVt = [jnp.zeros((LANE, 2 * BLK), F32) for _ in range(A_KV)]
            dsinks, roped, roped_t, do_t = [], {}, {}, {}
            group = A_HEADS // A_KV
            dim = lax.broadcasted_iota(jnp.int32, (LANE, BLK), 0)
            rope_row0, std_row0 = ((dim // 32) % 2) == 0, dim < 64

            def products(head):
                pb, sub, g = head // 2, head % 2, head // group
                cols = slice(pb * LANE, (pb + 1) * LANE)
                Kg, Vg = K[:, g * LANE:(g + 1) * LANE], V[:, g * LANE:(g + 1) * LANE]
                if sub == 0:
                    roped[pb] = _rope(q_ref[:, cols], cq, sq)
                    roped_t[pb] = roped[pb].T
                    do_t[pb] = do_ref[:, cols].T
                qm = jnp.where(rope_sub0 if sub == 0 else ~rope_sub0, roped[pb], 0.0).astype(MXU)
                qmt = jnp.where(rope_row0 if sub == 0 else ~rope_row0, roped_t[pb], 0.0).astype(MXU)
                dov = jnp.where(std_sub0 if sub == 0 else ~std_sub0, do_ref[:, cols], 0.0)
                dovt = jnp.where(std_row0 if sub == 0 else ~std_row0, do_t[pb], 0.0).astype(MXU)
                delta = jnp.sum(dov * o_ref[:, cols], axis=1, keepdims=True)
                return qmt, dovt, delta, _dot_nt(qm, Kg), _dot_nt(dov.astype(MXU), Vg)

            def scores(head, qmt, dovt, delta, s_full, dp_full):
                lh = jnp.sum(jnp.where(lane == head, lse_v, 0.0), axis=1, keepdims=True)
                p = jnp.where(valid, jnp.exp(_swa_fold(s_full, tri) * scale - lh), 0.0)
                psink = jnp.exp(sinks_ref[0, head] - lh)
                dsinks.append(jnp.sum(-psink * delta, axis=0, keepdims=True))
                dsq = (p * (_swa_fold(dp_full, tri) - delta)) * scale
                return qmt, dovt, _swa_unfold(p, tri).astype(MXU), _swa_unfold(dsq, tri).astype(MXU)

            def grads(head, qmt, dovt, pb16, dsc):
                g = head // group
                dKt[g] = dKt[g] + _dot(qmt, dsc)
                dVt[g] = dVt[g] + _dot(dovt, pb16)
                return _dot(dsc, K[:, g * LANE:(g + 1) * LANE])

            dqs = {}
            st1 = {0: products(0), 1: products(1)}
            st2 = {0: scores(0, *st1.pop(0))}
            for head in range(A_HEADS):
                if head + 2 < A_HEADS:
                    st1[head + 2] = products(head + 2)
                if head + 1 < A_HEADS:
                    st2[head + 1] = scores(head + 1, *st1.pop(head + 1))
                dqs[head] = grads(head, *st2.pop(head))
                if head % 2 == 1:
                    pb = head // 2
                    dqp = jnp.where(rope_sub0, dqs[head - 1], dqs[head])
                    dq_ref[:, pb * LANE:(pb + 1) * LANE] = _rope_bwd(dqp, cq, sq).astype(dq_ref.dtype)
            dsink = jnp.zeros((1, LANE), F32)
            for head in range(A_HEADS):
                dsink = jnp.where(lane_row == head, dsinks[head], dsink)
            dK, dV = [a.T for a in dKt], [a.T for a in dVt]
            prev = ([_rope_bwd(dK[g][:BLK], cp, sp) for g in range(A_KV)] + [dV[g][:BLK] for g in range(A_KV)])
            cur_ = ([_rope_bwd(dK[g][BLK:], cq, sq) for g in range(A_KV)] + [dV[g][BLK:] for g in range(A_KV)])
            dkv_ref[...] = (ckv_ref[...] + jnp.concatenate(prev, axis=1)).astype(dkv_ref.dtype)
            ckv_ref[...] = jnp.concatenate(cur_, axis=1)
            ds_ref[...] = ds_ref[...] + jnp.broadcast_to(dsink, ds_ref.shape)

        @pl.when(n == nb)
        def _():
            dkv_ref[...] = ckv_ref[...].astype(dkv_ref.dtype)

    last = nb - 1

    def cur(n):
        return jnp.minimum(n, last)

    def out_kv(n):
        return (jnp.maximum(n - 1, 0), 0)

    return pl.pallas_call(
        body, name="swa_bwd", grid=(nb + 1,),
        in_specs=[pl.BlockSpec(memory_space=pltpu.SMEM)] + _swa_in_specs(nb, last) + [
            pl.BlockSpec((BLK, D), lambda n: (cur(n), 0)),
            pl.BlockSpec((BLK, D), lambda n: (cur(n), 0)),
            pl.BlockSpec((BLK, LANE), lambda n: (cur(n), 0)),
            _any(),
        ],
        out_specs=[
            pl.BlockSpec((BLK, D), lambda n: (cur(n), 0)),
            pl.BlockSpec((BLK, 512), out_kv),
            pl.BlockSpec((8, LANE), lambda n: (0, 0)),
        ],
        out_shape=[
            jax.ShapeDtypeStruct((T, D), MXU),
            jax.ShapeDtypeStruct((T, 512), MXU),
            jax.ShapeDtypeStruct((8, LANE), F32),
        ],
        scratch_shapes=[pltpu.VMEM((BLK, 512), F32)],
        compiler_params=_cp(("arbitrary",)),
    )(sinks, proj, proj, proj, proj, proj, cos, sin, cos, sin, do_a, o_a, lse, after)


NCH = 4
GSTEP = NCH * CHUNK
ST_ROWS = B_HEADS * B_DV


def _chunk_rows(c):
    return slice(c * CHUNK, (c + 1) * CHUNK)


def _per_chunk(which, vals):
    out = vals[-1]
    for c in range(NCH - 2, -1, -1):
        out = jnp.where(which == c, vals[c], out)
    return out


def _gla_gate(bl_ref, gu_ref, bias_ref):
    gk = _dot(bl_ref[...].astype(MXU), gu_ref[...]) + bias_ref[...]
    la = (jnp.minimum(gk, 0.0) - jnp.log(1.0 + jnp.exp(-jnp.abs(gk)))) / TAU
    ri = lax.broadcasted_iota(jnp.int32, (GSTEP, GSTEP), 0)
    ci = lax.broadcasted_iota(jnp.int32, (GSTEP, GSTEP), 1)
    same = (ri // CHUNK) == (ci // CHUNK)
    lower, upper = same & (ci <= ri), same & (ci >= ri)
    b = _dot_f32(jnp.where(lower, 1.0, 0.0).astype(F32), la)
    which = lax.broadcasted_iota(jnp.int32, (GSTEP, 1), 0) // CHUNK
    return gk, la, b, lower, upper, which


def _gla_head(q_ref, k_ref, la, b, which, h):
    sl = slice(h * B_DK, (h + 1) * B_DK)
    bh, lah = b[:, sl], la[:, sl]
    bls = [jnp.sum(lah[_chunk_rows(c)], axis=0, keepdims=True) for c in range(NCH)]
    blast = _per_chunk(which, bls)
    qc = q_ref[:, sl] * (B_DK ** -0.5)
    kh = k_ref[:, sl]
    eb, enb, esb = jnp.exp(bh), jnp.exp(-bh), jnp.exp(blast - bh)
    return qc * eb, kh * enb, kh * esb, eb, enb, esb, [jnp.exp(v) for v in bls]


def _gla_specs(step_of):
    return [
        pl.BlockSpec((GSTEP, 512), lambda i: (step_of(i), C_BQ // 512)),
        pl.BlockSpec((GSTEP, 512), lambda i: (step_of(i), C_BK // 512)),
        pl.BlockSpec((GSTEP, D), lambda i: (step_of(i), C_BV // D)),
        pl.BlockSpec((GSTEP, W_BL), lambda i: (step_of(i), C_BL // W_BL)),
        pl.BlockSpec((W_BL, 512), lambda i: (0, 0)),
        pl.BlockSpec((1, 512), lambda i: (0, 0)),
    ]


def _gla_fwd(proj, gu_pad, bias):
    T = proj.shape[0]
    ns = T // GSTEP

    def body(q_ref, k_ref, v_ref, bl_ref, gu_ref, bias_ref, o_ref, st_ref, state_ref):
        @pl.when(pl.program_id(0) == 0)
        def _():
            state_ref[...] = jnp.zeros_like(state_ref)

        _, la, b, lower, _, which = _gla_gate(bl_ref, gu_ref, bias_ref)

        def within(h):
            q_e, k_e, k_s, _, _, _, decays = _gla_head(q_ref, k_ref, la, b, which, h)
            vh = v_ref[:, h * B_DV:(h + 1) * B_DV].astype(MXU)
            q_eb = q_e.astype(MXU)
            att = jnp.where(lower, _dot_nt(q_eb, k_e.astype(MXU)), 0.0)
            return vh, q_eb, k_s.astype(MXU), _dot(att.astype(MXU), vh), decays

        def across(h, vh, q_eb, k_sb, o_intra, decays):
            rows = slice(h * B_DV, (h + 1) * B_DV)
            s = state_ref[rows, :]
            outs = []
            for c in range(NCH):
                cr = _chunk_rows(c)
                st_ref[c * ST_ROWS + h * B_DV:c * ST_ROWS + (h + 1) * B_DV, :] = s
                outs.append(o_intra[cr] + _dot_nt(q_eb[cr], s.astype(MXU)))
                s = s * decays[c] + _dot_tn(vh[cr], k_sb[cr])
            state_ref[rows, :] = s
            o_ref[:, rows] = jnp.concatenate(outs, axis=0)

        for h in range(B_HEADS):
            across(h, *within(h))

    return pl.pallas_call(
        body, name="gla_fwd", grid=(ns,),
        in_specs=_gla_specs(lambda i: i),
        out_specs=[pl.BlockSpec((GSTEP, D), lambda i: (i, 0)),
                   pl.BlockSpec((NCH * ST_ROWS, B_DK), lambda i: (i, 0))],
        out_shape=[jax.ShapeDtypeStruct((T, D), F32),
                   jax.ShapeDtypeStruct((ns * NCH * ST_ROWS, B_DK), F32)],
        scratch_shapes=[pltpu.VMEM((ST_ROWS, B_DK), F32)],
        compiler_params=_cp(("arbitrary",)),
    )(proj, proj, proj, proj, gu_pad, bias)


def _gla_bwd(proj, gu_pad, bias, states, do_b):
    T = proj.shape[0]
    ns = T // GSTEP
    o_q, o_k = C_BQ - C_GLA, C_BK - C_GLA

    def body(q_ref, k_ref, v_ref, bl_ref, gu_ref, bias_ref, st_ref, do_ref,
             dg_ref, dbl_ref, ggu_ref, gbias_ref, gt_ref):
        @pl.when(pl.program_id(0) == 0)
        def _():
            gt_ref[...] = jnp.zeros_like(gt_ref)
            ggu_ref[...] = jnp.zeros_like(ggu_ref)
            gbias_ref[...] = jnp.zeros_like(gbias_ref)

        gk, la, b, lower, upper_mask, which = _gla_gate(bl_ref, gu_ref, bias_ref)
        upper = jnp.where(upper_mask, 1.0, 0.0).astype(F32)
        dla_parts = []

        def within(h):
            q_e, k_e, k_s, eb, enb, esb, decays = _gla_head(q_ref, k_ref, la, b, which, h)
            vh = v_ref[:, h * B_DV:(h + 1) * B_DV].astype(MXU)
            doh = do_ref[:, h * B_DV:(h + 1) * B_DV].astype(MXU)
            q_eb, k_eb = q_e.astype(MXU), k_e.astype(MXU)
            att = jnp.where(lower, _dot_nt(q_eb, k_eb), 0.0).astype(MXU)
            datt = jnp.where(lower, _dot_nt(doh, vh), 0.0).astype(MXU)
            return (q_e, k_e, k_s, eb, enb, esb, decays, vh, doh, q_eb, k_s.astype(MXU),
                    _dot(datt, k_eb), _dot_tn(datt, q_eb), _dot_tn(att, doh))

        def across(h, q_e, k_e, k_s, eb, enb, esb, decays, vh, doh, q_eb, k_sb, dq_i, dk_e, dv_i):
            rows = slice(h * B_DV, (h + 1) * B_DV)
            g = gt_ref[rows, :]
            dq_c, dks_c, dv_c, ddec = [None] * NCH, [None] * NCH, [None] * NCH, [None] * NCH
            for c in range(NCH - 1, -1, -1):
                cr = _chunk_rows(c)
                s = st_ref[c * ST_ROWS + h * B_DV:c * ST_ROWS + (h + 1) * B_DV, :]
                gb = g.astype(MXU)
                dq_c[c] = dq_i[cr] + _dot(doh[cr], s.astype(MXU))
                dks_c[c] = _dot(vh[cr], gb)
                dv_c[c] = dv_i[cr] + _dot_nt(k_sb[cr], gb)
                ddec[c] = jnp.sum(g * s, axis=0, keepdims=True)
                g = g * decays[c] + _dot_tn(doh[cr], q_eb[cr])
            gt_ref[rows, :] = g
            dq_e = jnp.concatenate(dq_c, axis=0)
            dk_s = jnp.concatenate(dks_c, axis=0)
            dg_ref[:, rows] = jnp.concatenate(dv_c, axis=0).astype(dg_ref.dtype)
            dg_ref[:, o_q + h * B_DK:o_q + (h + 1) * B_DK] = (dq_e * eb * (B_DK ** -0.5)).astype(dg_ref.dtype)
            dg_ref[:, o_k + h * B_DK:o_k + (h + 1) * B_DK] = (dk_e * enb + dk_s * esb).astype(dg_ref.dtype)
            dks_ks = dk_s * k_s
            db = dq_e * q_e - dk_e * k_e - dks_ks
            dbl = [jnp.sum(dks_ks[_chunk_rows(c)], axis=0, keepdims=True) + ddec[c] * decays[c] for c in range(NCH)]
            dla_parts.append(_dot_f32(upper, db) + _per_chunk(which, dbl))

        for h in range(B_HEADS):
            across(h, *within(h))
        dla = jnp.concatenate(dla_parts, axis=1)
        dgk = dla * (1.0 / TAU) * _sigmoid(-gk)
        dgkb = dgk.astype(MXU)
        dbl_ref[...] = _dot_nt(dgkb, gu_ref[...]).astype(dbl_ref.dtype)
        ggu_ref[...] = ggu_ref[...] + _dot_tn(bl_ref[...].astype(MXU), dgkb)
        gbias_ref[...] = gbias_ref[...] + jnp.broadcast_to(jnp.sum(dgk, axis=0, keepdims=True), gbias_ref.shape)

    def rev(i):
        return ns - 1 - i

    return pl.pallas_call(
        body, name="gla_bwd", grid=(ns,),
        in_specs=_gla_specs(rev) + [
            pl.BlockSpec((NCH * ST_ROWS, B_DK), lambda i: (rev(i), 0)),
            pl.BlockSpec((GSTEP, D), lambda i: (rev(i), 0)),
        ],
        out_specs=[
            pl.BlockSpec((GSTEP, W_GLA), lambda i: (rev(i), 0)),
            pl.BlockSpec((GSTEP, W_BL), lambda i: (rev(i), 0)),
            pl.BlockSpec((W_BL, 512), lambda i: (0, 0)),
            pl.BlockSpec((8, 512), lambda i: (0, 0)),
        ],
        out_shape=[
            jax.ShapeDtypeStruct((T, W_GLA), MXU),
            jax.ShapeDtypeStruct((T, W_BL), MXU),
            jax.ShapeDtypeStruct((W_BL, 512), F32),
            jax.ShapeDtypeStruct((8, 512), F32),
        ],
        scratch_shapes=[pltpu.VMEM((B_HEADS * B_DV, B_DK), F32)],
        compiler_params=_cp(("arbitrary",)),
    )(proj, proj, proj, proj, gu_pad, bias, states, do_b)


def _mid(x, target, proj, o_a, o_b, late, w_bn4, fnw):
    T = x.shape[0]
    tT = min(T, 128)
    nbuf = 4
    o_ag, o_bg, o_ma, o_mb = (c - C_GATES for c in (C_AG, C_BG, C_MA, C_MB))

    def body(x_ref, t_ref, oa_ref, ob_ref, gates_ref, late_ref, wbn_ref, fnw_ref,
             dx2_ref, doa_ref, dob_ref, dgates_ref,
             tail0_ref, tail1_ref, gfn_ref, gbn_ref, loss_ref, buf_ref, gw_ref):
        i = pl.program_id(0)

        def weight(p):
            return late_ref[:, 128 * p:128 * (p + 1), :].reshape(D, D)

        @pl.when(i == 0)
        def _():
            for r in (gw_ref, gfn_ref, gbn_ref, loss_ref):
                r[...] = jnp.zeros_like(r)

        rows = pl.ds(pl.multiple_of((i % nbuf) * tT, tT), tT)

        def keep(k, val):
            buf_ref[k, rows, :] = val

        oa, ag = oa_ref[...], gates_ref[:, o_ag:o_ag + D]
        sg_a = _sigmoid(ag)
        silu_a = ag * sg_a
        oag_b = (oa * silu_a).astype(MXU)
        keep(0, oag_b)
        y_a = _dot(oag_b, weight(0))

        ob, bg = ob_ref[...], gates_ref[:, o_bg:o_bg + D]
        rbs, obhats = [], []
        for h in range(B_HEADS):
            obh = ob[:, h * B_DV:(h + 1) * B_DV]
            rb = lax.rsqrt(jnp.mean(obh * obh, axis=-1, keepdims=True) + EPS)
            rbs.append(rb)
            obhats.append(obh * rb)
        obhat = jnp.concatenate(obhats, axis=1)
        wbn = wbn_ref[...]
        obn = obhat * wbn
        sg_b = _sigmoid(bg)
        silu_b = bg * sg_b
        obg_b = (obn * silu_b).astype(MXU)
        keep(1, obg_b)
        y_b = _dot(obg_b, weight(1))

        sa, sb = _sigmoid(gates_ref[:, o_ma:o_ma + D]), _sigmoid(gates_ref[:, o_mb:o_mb + D])
        mg_b = (sa * y_a + sb * y_b).astype(MXU)
        keep(2, mg_b)
        x2 = x_ref[...] + _dot(mg_b, weight(2))
        r2 = lax.rsqrt(jnp.mean(x2 * x2, axis=-1, keepdims=True) + EPS)
        xh2 = x2 * r2
        fw = fnw_ref[...]
        err = xh2 * fw - t_ref[...]
        tok = jnp.mean(err * err, axis=-1, keepdims=True)
        loss_ref[...] = loss_ref[...] + 0.5 * jnp.sum(tok, axis=0, keepdims=True)

        dy = err * (1.0 / D)
        gfn_ref[...] = gfn_ref[...] + jnp.broadcast_to(jnp.sum(dy * xh2, axis=0, keepdims=True), gfn_ref.shape)
        gy = dy * fw
        dx2 = r2 * (gy - xh2 * jnp.mean(gy * xh2, axis=-1, keepdims=True))
        dx2_ref[...] = dx2
        dx2_b = dx2.astype(MXU)
        keep(5, dx2_b)
        dmg = _dot_nt(dx2_b, weight(2))

        dgates_ref[:, o_ma:o_ma + D] = (dmg * y_a * sa * (1.0 - sa)).astype(dgates_ref.dtype)
        dgates_ref[:, o_mb:o_mb + D] = (dmg * y_b * sb * (1.0 - sb)).astype(dgates_ref.dtype)
        dya_b = (dmg * sa).astype(MXU)
        dyb_b = (dmg * sb).astype(MXU)
        keep(3, dya_b)
        keep(4, dyb_b)
        doag = _dot_nt(dya_b, weight(0))
        dobg = _dot_nt(dyb_b, weight(1))

        @pl.when(i % nbuf == nbuf - 1)
        def _():
            for p in range(3):
                gw_ref[p] = gw_ref[p] + _dot_tn(buf_ref[p], buf_ref[3 + p])

        @pl.when(i == pl.num_programs(0) - 1)
        def _():
            for hf, tail_ref in enumerate((tail0_ref, tail1_ref)):
                for d in range(NDEV):
                    for p in range(3):
                        tail_ref[d, 128 * p:128 * (p + 1), :] = (
                            gw_ref[p, 128 * d:128 * (d + 1), hf * DH:(hf + 1) * DH].astype(tail_ref.dtype))

        doa_ref[...] = doag * silu_a
        dgates_ref[:, o_ag:o_ag + D] = (doag * oa * (sg_a * (1.0 + ag * (1.0 - sg_a)))).astype(dgates_ref.dtype)
        dobn = dobg * silu_b
        dgates_ref[:, o_bg:o_bg + D] = (dobg * obn * (sg_b * (1.0 + bg * (1.0 - sg_b)))).astype(dgates_ref.dtype)
        gg = dobn * wbn
        gbn = jnp.zeros((1, B_DV), F32)
        for h in range(B_HEADS):
            sl = slice(h * B_DV, (h + 1) * B_DV)
            gbn = gbn + jnp.sum(dobn[:, sl] * obhats[h], axis=0, keepdims=True)
            ggh = gg[:, sl]
            dob_ref[:, sl] = rbs[h] * (ggh - obhats[h] * jnp.mean(ggh * obhats[h], axis=-1, keepdims=True))
        gbn_ref[...] = gbn_ref[...] + jnp.broadcast_to(gbn, gbn_ref.shape)

    assert (T // tT) % nbuf == 0
    tile = pl.BlockSpec((tT, D), lambda i: (i, 0))
    row = pl.BlockSpec((1, D), lambda i: (0, 0))
    acc8 = pl.BlockSpec((8, D), lambda i: (0, 0))
    return pl.pallas_call(
        body, name="mid", grid=(T // tT,),
        in_specs=[tile, tile, tile, tile, pl.BlockSpec((tT, W_GATES), lambda i: (i, C_GATES // W_GATES)),
                  _vmem(), row, row],
        out_specs=[tile, tile, tile, pl.BlockSpec((tT, W_GATES), lambda i: (i, 0)), _vmem(), _vmem(),
                   acc8, pl.BlockSpec((8, B_DV), lambda i: (0, 0)), pl.BlockSpec((8, LANE), lambda i: (0, 0))],
        out_shape=[
            jax.ShapeDtypeStruct((T, D), F32),
            jax.ShapeDtypeStruct((T, D), F32),
            jax.ShapeDtypeStruct((T, D), F32),
            jax.ShapeDtypeStruct((T, W_GATES), MXU),
            jax.ShapeDtypeStruct((NDEV, 384, DH), WIRE),
            jax.ShapeDtypeStruct((NDEV, 384, DH), WIRE),
            jax.ShapeDtypeStruct((8, D), F32),
            jax.ShapeDtypeStruct((8, B_DV), F32),
            jax.ShapeDtypeStruct((8, LANE), F32),
        ],
        scratch_shapes=[pltpu.VMEM((6, nbuf * tT, D), MXU), pltpu.VMEM((3, D, D), F32)],
        compiler_params=_cp(("arbitrary",)),
    )(x, target, o_a, o_b, proj, late, w_bn4, fnw)


DH = D // 2


_GW_TILES = (("q", 0, 512, 0), ("q", 1, 512, 512), ("kv", 0, 256, 1024), ("bl", 0, RANK, 5376),
             ("gla", 0, 512, 3328), ("gla", 1, 512, 3840), ("gla", 2, 512, 2304), ("gla", 3, 512, 2816),
             ("gates", 0, 512, 1280), ("gates", 1, 512, 1792), ("gates", 2, 512, 4352), ("gates", 3, 512, 4864),
             ("gates", 4, 512, 5392), ("gates", 5, 512, 5904), ("gates", 6, 512, 6416), ("gates", 7, 512, 6928))


def _gw_unpermute(piece, t):
    if piece == "q":
        parts = []
        for blk in range(t.shape[0] // LANE):
            g = [t[blk * LANE + 32 * i:blk * LANE + 32 * (i + 1)] for i in range(4)]
            parts += [g[0], g[2], g[1], g[3]]
        return jnp.concatenate(parts, axis=0)
    if piece == "kv":
        k = [t[64 * i:64 * i + 32] + t[64 * i + 32:64 * i + 64] for i in range(4)]
        v = [t[256 + 128 * g:256 + 128 * g + 64] + t[256 + 128 * g + 64:256 + 128 * (g + 1)] for g in range(2)]
        return jnp.concatenate(k + v, axis=0)
    if piece == "bl":
        return t[:RANK]
    return t


def _gw_half(h, pieces, half, after=None):
    T = h.shape[0]
    steps = len(_GW_TILES)

    def body(*refs):
        h_ref = refs[0]
        srcs = dict(zip(("q", "kv", "bl", "gla", "gates"), refs[1:6]))
        o_ref, stage, sems = refs[-3:]
        j = pl.program_id(0)

        def out_copy(k):
            _, _, n, off = _GW_TILES[k]
            return pltpu.make_async_copy(stage.at[k % 2, 0:n], o_ref.at[pl.ds(off, n)], sems.at[k % 2])

        for k, (piece, _, n, _) in enumerate(_GW_TILES):
            @pl.when(j == k)
            def _(k=k, piece=piece, n=n):
                if k >= 2:
                    out_copy(k - 2).wait()
                t = _gw_unpermute(piece, _dot_tn(srcs[piece][...], h_ref[...]))
                stage[k % 2, 0:n, :] = t.astype(stage.dtype)
                out_copy(k).start()

        @pl.when(j == steps - 1)
        def _():
            out_copy(steps - 2).wait()
            out_copy(steps - 1).wait()

    def tile_of(lo, hi):
        return lambda j: (0, jnp.clip(j - lo, 0, hi - lo - 1))

    in_specs = [pl.BlockSpec((T, DH), lambda j: (0, half)),
                pl.BlockSpec((T, 512), tile_of(0, 2)), pl.BlockSpec((T, 512), lambda j: (0, 0)),
                pl.BlockSpec((T, W_BL), lambda j: (0, 0)),
                pl.BlockSpec((T, 512), tile_of(4, 8)), pl.BlockSpec((T, 512), tile_of(8, 16))]
    args = [h, *pieces]
    if after is not None:
        in_specs.append(_any())
        args.append(after)
    in_specs.append(_any())
    args.append(pltpu.with_memory_space_constraint(lax.empty((IN_WIDTH, DH), WIRE), pltpu.HBM))
    return pl.pallas_call(
        body, name=f"gw_in_half{half}", grid=(steps,),
        in_specs=in_specs, out_specs=_any(),
        input_output_aliases={len(args) - 1: 0},
        out_shape=jax.ShapeDtypeStruct((IN_WIDTH, DH), WIRE),
        scratch_shapes=[pltpu.VMEM((2, 512, DH), WIRE), pltpu.SemaphoreType.DMA((2,))],
        compiler_params=_cp(("arbitrary",)),
    )(*args)


def _chip_copies(s_ref, got_ref, send_sems, recv_sems):
    x, y, c = _place()
    chips = [(1 - x, y), (x, 1 - y), (1 - x, 1 - y)]
    return [pltpu.make_async_remote_copy(
        src_ref=s_ref.at[2 * px + py], dst_ref=got_ref.at[j],
        send_sem=send_sems.at[j], recv_sem=recv_sems.at[j], device_id=(px, py, c), device_id_type=MESH)
        for j, (px, py) in enumerate(chips)]


_EFFECT = pltpu.SideEffectType.DATAFLOW_SIDE_EFFECTING


def _hbm():
    return pl.BlockSpec(memory_space=pltpu.HBM)


def _sem():
    return pl.BlockSpec(memory_space=pltpu.SEMAPHORE)


def _chip_start(sums, half):
    land = pltpu.with_memory_space_constraint(lax.empty((3,) + sums.shape[1:], sums.dtype), pltpu.HBM)

    def body(s_ref, land_ref, send_sems, recv_sems, s_thru, land_thru, token):
        for cp in _chip_copies(s_ref, land_ref, send_sems, recv_sems):
            cp.start()
        token[...] = jnp.zeros_like(token)

    return pl.pallas_call(
        body, name=f"chip_start{half}",
        out_shape=(pltpu.SemaphoreType.DMA((3,)), pltpu.SemaphoreType.DMA((3,)),
                   pltpu.HBM(sums.shape, sums.dtype), pltpu.HBM(land.shape, land.dtype),
                   jax.ShapeDtypeStruct((8, LANE), F32)),
        in_specs=(_hbm(), _hbm()), out_specs=(_sem(), _sem(), _hbm(), _hbm(), _vmem()),
        input_output_aliases={0: 2, 1: 3},
        compiler_params=pltpu.CompilerParams(has_side_effects=_EFFECT),
    )(pltpu.with_memory_space_constraint(sums, pltpu.HBM), land)


def _chip_wait(send_sems, recv_sems, s_thru, land_thru, after, half):
    def body(s_ref, land_ref, send_sems, recv_sems, after_ref, s_out, got_ref):
        copies = _chip_copies(s_ref, land_ref, send_sems, recv_sems)
        for cp in copies:
            cp.wait_send()
        for cp in copies:
            cp.wait_recv()

    return pl.pallas_call(
        body, name=f"chip_wait{half}",
        out_shape=(pltpu.HBM(s_thru.shape, s_thru.dtype), pltpu.HBM(land_thru.shape, land_thru.dtype)),
        in_specs=(_hbm(), _hbm(), _sem(), _sem(), _any()), out_specs=(_hbm(), _hbm()),
        input_output_aliases={0: 0, 1: 1},
        compiler_params=pltpu.CompilerParams(has_side_effects=_EFFECT),
    )(s_thru, land_thru, send_sems, recv_sems, after)


def _dh_norm(pieces, offsets, wf, x, dx2, norm_w, after):
    T = x.shape[0]
    tT = min(T, 256)
    widths = [p.shape[1] for p in pieces]
    npc = len(pieces)

    def body(*refs):
        dp_refs = refs[:npc]
        wf_ref, x_ref, dx2_ref, nw_ref, _, gx_ref, gnw_ref = refs[npc:]

        @pl.when(pl.program_id(0) == 0)
        def _():
            gnw_ref[...] = jnp.zeros_like(gnw_ref)

        dh = jnp.zeros((tT, D), F32)
        for dp_ref, off, w in zip(dp_refs, offsets, widths):
            dh = dh + _dot(dp_ref[...], wf_ref[off:off + w, :])
        xv = x_ref[...]
        r = lax.rsqrt(jnp.mean(xv * xv, axis=-1, keepdims=True) + EPS)
        xh = xv * r
        gnw_ref[...] = gnw_ref[...] + jnp.broadcast_to(jnp.sum(dh * xh, axis=0, keepdims=True), gnw_ref.shape)
        g = dh * nw_ref[...]
        gx_ref[...] = r * (g - xh * jnp.mean(g * xh, axis=-1, keepdims=True)) + dx2_ref[...]

    tile = pl.BlockSpec((tT, D), lambda i: (i, 0))
    return pl.pallas_call(
        body, name="dh_norm", grid=(T // tT,),
        in_specs=[pl.BlockSpec((tT, w), lambda i: (i, 0)) for w in widths]
        + [_vmem(), tile, tile, pl.BlockSpec((1, D), lambda i: (0, 0)), _any()],
        out_specs=[tile, pl.BlockSpec((8, D), lambda i: (0, 0))],
        out_shape=[jax.ShapeDtypeStruct((T, D), F32), jax.ShapeDtypeStruct((8, D), F32)],
        compiler_params=_cp(("arbitrary",)),
    )(*pieces, wf, x, dx2, norm_w, after)


def _adamw_math(w, g, m, v):
    m = ADAM_B1 * m + (1.0 - ADAM_B1) * g
    v = ADAM_B2 * v + (1.0 - ADAM_B2) * (g * g)
    m_hat = m * (1.0 / (1.0 - ADAM_B1 ** ADAM_STEP))
    v_hat = v * (1.0 / (1.0 - ADAM_B2 ** ADAM_STEP))
    delta = -ADAM_LR * (m_hat / (jnp.sqrt(v_hat) + ADAM_EPS) + ADAM_WD * w)
    return delta, m, v


def _fetch_partials(s_ref, got_ref, buf, sems):
    x, y, _ = _place()
    cps = [pltpu.make_async_copy(s_ref.at[2 * x + y], buf.at[0], sems.at[0])]
    cps += [pltpu.make_async_copy(got_ref.at[j], buf.at[1 + j], sems.at[1 + j]) for j in range(3)]
    for cp in cps:
        cp.start()
    for cp in cps:
        cp.wait()


SMALL_AT = dict(norm_w=0, fnw=8, bias=16, bn=24, sinks=32, loss=40)
ROW_AT = (R_IN, R_A, R_B, R_O)


def _finish_small(ws, ms, vs, smalls):
    names = ["norm_w", "fnw", "bias", "bn", "sinks"]
    widths = [ws[n].shape[1] for n in names]

    def body(*refs):
        w_refs, m_refs, v_refs = refs[0:5], refs[5:10], refs[10:15]
        smalls_ref, loss_ref = refs[15], refs[16]
        outs, tot = refs[17:37], refs[37]
        acc = smalls_ref[0]
        for d in range(1, NDEV):
            acc = acc + smalls_ref[d]
        tot[...] = acc
        loss_ref[...] = tot[SMALL_AT["loss"]:SMALL_AT["loss"] + 1, 0:1]
        for p, (nm_, wd) in enumerate(zip(names, widths)):
            r = SMALL_AT[nm_]
            g = tot[r:r + 1, 0:wd]
            d, nm, nv = _adamw_math(w_refs[p][...], g, m_refs[p][...], v_refs[p][...])
            for o, val in zip(outs[4 * p:4 * p + 4], (g, d, nm, nv)):
                o[...] = val

    res = pl.pallas_call(
        body, name="finish_small",
        in_specs=[_vmem()] * 16, out_specs=[_vmem()] * 21,
        out_shape=[jax.ShapeDtypeStruct((1, 1), F32)]
        + [jax.ShapeDtypeStruct((1, wd), F32) for wd in widths for _ in range(4)],
        scratch_shapes=[pltpu.VMEM((SMALL_ROWS, D), F32)],
        compiler_params=_cp(),
    )(*[ws[n] for n in names], *[ms[n] for n in names], *[vs[n] for n in names], smalls)
    return res[0], {n: tuple(res[1 + 4 * p:5 + 4 * p]) for p, n in enumerate(names)}


def _finish(w_rows, m_rows, v_rows, gu_w, gu_m, gu_v, sums, got):
    shapes = [(SHARD, 1, D)] + [w.shape for w in w_rows[1:]]

    row_block = 96

    def columns(ref, p, cols, r0, n):
        if p:
            return ref, (slice(r0, r0 + n), cols)
        flat = ref if ref.shape == (SHARD * LANE_TILES, LANE) else ref.reshape(SHARD * LANE_TILES, LANE)
        return flat, (pl.ds(cols.start // LANE + LANE_TILES * r0, n, stride=LANE_TILES), slice(None))

    def read(ref, p, cols, r0, n):
        ref, at = columns(ref, p, cols, r0, n)
        return ref[at]

    def body(*refs):
        wr_refs, mr_refs, vr_refs = refs[0:4], refs[4:8], refs[8:12]
        guw_ref, gum_ref, guv_ref = refs[12:15]
        s_refs, got_refs = refs[15:17], refs[17:19]
        row_outs = refs[19:35]
        gu_outs = refs[35:39]
        buf, gsh, sems, big, big_sems = refs[39:]
        loads = [pltpu.make_async_copy(r[0], big.at[k], big_sems.at[k]) for k, r in enumerate((wr_refs, mr_refs, vr_refs))]
        for cp in loads:
            cp.start()
        wr_refs, mr_refs, vr_refs = ((big.at[k],) + tuple(r[1:]) for k, r in enumerate((wr_refs, mr_refs, vr_refs)))
        x, y, c = _place()
        me_slot = 4 * x + 2 * y + c
        down = 2 * me_slot

        def total(rows, cols):
            g = buf[0, rows, cols].astype(F32)
            for j in range(1, 4):
                g = g + buf[j, rows, cols].astype(F32)
            return g

        def update(p, grad, cols):
            nrows = shapes[p][0]
            for r0 in range(0, nrows, row_block):
                n = min(row_block, nrows - r0)
                g = grad(r0, n)
                d, nm, nv = _adamw_math(read(wr_refs[p], p, cols, r0, n), g, read(mr_refs[p], p, cols, r0, n),
                                        read(vr_refs[p], p, cols, r0, n))
                for o, val in zip(row_outs[4 * p:4 * p + 4], (g, d, nm, nv)):
                    o, at = columns(o, p, cols, r0, n)
                    o[at] = val

        for hf in range(2):
            _fetch_partials(s_refs[hf], got_refs[hf], buf, sems)
            for cc in range(DH // LANE):
                src = slice(cc * LANE, (cc + 1) * LANE)
                cols = slice(hf * DH + cc * LANE, hf * DH + (cc + 1) * LANE)
                for r0 in range(0, SHARD_PAD, row_block):
                    rows = slice(r0, min(r0 + row_block, SHARD_PAD))
                    gsh[rows, :] = total(rows, src)
                if hf == 0 and cc == 0:
                    for cp in loads:
                        cp.wait()
                update(0, lambda r0, n: gsh[pl.ds(down + r0, n), :], cols)
                for p in range(1, 4):
                    update(p, lambda r0, n, p=p: total(slice(ROW_AT[p] + r0, ROW_AT[p] + r0 + n), src), cols)
            if hf == 0:
                g = total(slice(R_GU, R_GU + RANK), slice(0, 64))
                d, nm, nv = _adamw_math(guw_ref[...], g, gum_ref[...], guv_ref[...])
                for o, val in zip(gu_outs, (g, d, nm, nv)):
                    o[...] = val

    res = pl.pallas_call(
        body, name="finish",
        in_specs=([_any()] + [_vmem()] * 3) * 3 + [_vmem()] * 3 + [_any()] * 4,
        out_specs=[_vmem()] * 20,
        out_shape=[jax.ShapeDtypeStruct(s, F32) for s in shapes for _ in range(4)]
        + [jax.ShapeDtypeStruct((RANK, 64), F32)] * 4,
        scratch_shapes=[pltpu.VMEM((4, ROWS, DH), sums[0].dtype), pltpu.VMEM((SHARD_PAD, LANE), F32),
                        pltpu.SemaphoreType.DMA((4,)),
                        pltpu.VMEM((3, SHARD * LANE_TILES, LANE), F32), pltpu.SemaphoreType.DMA((3,))],
        compiler_params=_cp(),
    )(*w_rows, *m_rows, *v_rows, gu_w, gu_m, gu_v, *sums, *got)
    return tuple(res[0:16]), tuple(res[16:20])


def _place():
    x, y, c = lax.axis_index("x"), lax.axis_index("y"), lax.axis_index("c")
    return x, y, c


def _peers(x, y, c):
    return [(x ^ dx, y ^ dy, c ^ dc) for dx in range(2) for dy in range(2) for dc in range(2) if dx + dy + dc]


def _late_gather_start(blk, after, name="late_gather"):
    land = pltpu.with_memory_space_constraint(lax.empty((NDEV,) + blk.shape, blk.dtype), pltpu.HBM)

    def body(b_ref, land_ref, after_ref, send_sems, recv_sems, b_thru, land_thru, token):
        x, y, c = _place()
        for k, to in enumerate(_peers(x, y, c)):
            pltpu.make_async_remote_copy(
                src_ref=b_ref, dst_ref=land_ref.at[4 * x + 2 * y + c], send_sem=send_sems.at[k],
                recv_sem=recv_sems.at[k], device_id=to, device_id_type=MESH).start()
        token[...] = jnp.zeros_like(token)

    return pl.pallas_call(
        body, name=name + "_start",
        out_shape=(pltpu.SemaphoreType.DMA((7,)), pltpu.SemaphoreType.DMA((7,)),
                   pltpu.HBM(blk.shape, blk.dtype), pltpu.HBM(land.shape, land.dtype),
                   jax.ShapeDtypeStruct((8, LANE), F32)),
        in_specs=(_hbm(), _hbm(), _any()), out_specs=(_sem(), _sem(), _hbm(), _hbm(), _vmem()),
        input_output_aliases={0: 2, 1: 3},
        compiler_params=pltpu.CompilerParams(has_side_effects=_EFFECT),
    )(pltpu.with_memory_space_constraint(blk, pltpu.HBM), land, after)


def _late_gather_wait(send_sems, recv_sems, b_thru, land_thru, after, after2, name="late_gather"):
    def body(b_ref, land_ref, send_sems, recv_sems, after_ref, after2_ref, b_out, got_ref):
        x, y, c = _place()
        copies = [pltpu.make_async_remote_copy(
            src_ref=b_ref, dst_ref=land_ref.at[4 * x + 2 * y + c], send_sem=send_sems.at[k],
            recv_sem=recv_sems.at[k], device_id=to, device_id_type=MESH)
            for k, to in enumerate(_peers(x, y, c))]
        for cp in copies:
            cp.wait_send()
        for cp in copies:
            cp.wait_recv()

    return pl.pallas_call(
        body, name=name + "_wait",
        out_shape=(pltpu.HBM(b_thru.shape, b_thru.dtype), pltpu.HBM(land_thru.shape, land_thru.dtype)),
        in_specs=(_hbm(), _hbm(), _sem(), _sem(), _any(), _any()), out_specs=(_hbm(), _hbm()),
        input_output_aliases={0: 0, 1: 1},
        compiler_params=pltpu.CompilerParams(has_side_effects=_EFFECT),
    )(b_thru, land_thru, send_sems, recv_sems, after, after2)


G_ROWS = SHARD_PAD + RANK


def _gather_blocks(w_in_t, gu_s, xs, norm_w, pos_col):
    rows, cols = G_ROWS, D
    T = xs.shape[0]
    tT = min(T, 256)
    inv_row, sign_row = _rope_rows()

    def body(wi_ref, gu_ref, xs_hbm, nw_ref, pos_ref, inv_ref, sign_ref,
             out_ref, h_ref, cos_ref, sin_ref, x_ref, frame_ref, xs_ref, send_sems, recv_sems, local_sem, xs_sem):
        load_xs = pltpu.make_async_copy(xs_hbm, xs_ref, xs_sem)
        load_xs.start()
        x, y, c = _place()
        me, sibling = (x, y, c), (x, y, 1 - c)
        chips = [(1 - x, y), (x, 1 - y), (1 - x, 1 - y)]
        shift = 2 * (4 * x + 2 * y + c)
        frame_ref[SHARD - SHARD % 8:, :] = jnp.zeros((SHARD_PAD - SHARD + SHARD % 8, LANE), F32)
        for cc in range(LANE_TILES):
            cs = slice(cc * LANE, (cc + 1) * LANE)
            frame_ref[:SHARD, :] = wi_ref[pl.ds(cc, SHARD, stride=LANE_TILES), :]
            x_ref[0:SHARD_PAD, cs] = pltpu.roll(frame_ref[...], shift, 0).astype(x_ref.dtype)
        x_ref[SHARD_PAD:G_ROWS, :] = jnp.zeros((RANK, D), x_ref.dtype)
        x_ref[SHARD_PAD:G_ROWS, 0:64] = gu_ref[...].astype(x_ref.dtype)

        def slot(px, py, pc):
            return out_ref.at[4 * px + 2 * py + pc]

        def copy(k, block, to, src=None):
            return pltpu.make_async_remote_copy(
                src_ref=slot(*block) if src is None else src, dst_ref=slot(*block),
                send_sem=send_sems.at[k], recv_sem=recv_sems.at[k], device_id=to, device_id_type=MESH)

        mine = pltpu.make_async_copy(x_ref, slot(*me), local_sem)
        mine.start()
        first = [copy(0, me, sibling, src=x_ref)]
        first += [copy(1 + j, me, (*chip, c), src=x_ref) for j, chip in enumerate(chips)]
        for cp in first:
            cp.start()
        load_xs.wait()

        @pl.loop(0, T // tT)
        def _(i):
            rows_i = pl.ds(pl.multiple_of(i * tT, tT), tT)
            _prologue_rows(rows_i, xs_ref, nw_ref, pos_ref, inv_ref, sign_ref, h_ref, cos_ref, sin_ref)

        passed = [copy(4 + j, (*chip, c), sibling) for j, chip in enumerate(chips)]
        for j, chip in enumerate(chips):
            copy(1 + j, (*chip, c), me).wait_recv()
            passed[j].start()
        copy(0, sibling, me).wait_recv()
        for j, chip in enumerate(chips):
            copy(4 + j, (*chip, 1 - c), me).wait_recv()
        for cp in first + passed:
            cp.wait_send()
        mine.wait()

    return pl.pallas_call(
        body, name="gather_weights",
        in_specs=[_vmem(), _vmem(), _any()] + [_vmem()] * 4, out_specs=[_any()] + [_vmem()] * 3,
        out_shape=[jax.ShapeDtypeStruct((NDEV, rows, cols), WIRE), jax.ShapeDtypeStruct((T, D), MXU),
                   jax.ShapeDtypeStruct((T, LANE), F32), jax.ShapeDtypeStruct((T, LANE), F32)],
        scratch_shapes=[pltpu.VMEM((rows, cols), WIRE), pltpu.VMEM((SHARD_PAD, LANE), F32), pltpu.VMEM((T, D), F32),
                        pltpu.SemaphoreType.DMA((7,)), pltpu.SemaphoreType.DMA((7,)), pltpu.SemaphoreType.DMA,
                        pltpu.SemaphoreType.DMA],
        compiler_params=_cp(),
    )(w_in_t, gu_s, xs, norm_w, pos_col, inv_row, sign_row)


def _pair_reduce(gwt, tails, half):
    n = gwt.shape[1]
    starts = [SHARD_PAD]
    for t in tails:
        starts.append(starts[-1] + t.shape[1])
    rows = starts[-1]
    blk = (4, rows, n)
    npart = 1 + len(tails)

    def body(*refs):
        g_ref, t_refs = refs[0], refs[1:npart]
        out_ref, acc, got, own, send_sems, recv_sems, own_sems, out_sems = refs[npart:]
        x, y, c = _place()

        def parts(d, dst):
            frame = g_ref.at[pl.ds(pl.multiple_of(FRAME * d, 16), SHARD_PAD)]
            return [(frame, dst.at[0:SHARD_PAD])] + [
                (t_ref.at[d], dst.at[starts[k]:starts[k + 1]]) for k, t_ref in enumerate(t_refs)]

        sends, loads, stores = [], [], []
        for chip in range(4):
            sends.append([pltpu.make_async_remote_copy(
                src_ref=s, dst_ref=d_, send_sem=send_sems.at[chip, k], recv_sem=recv_sems.at[chip, k],
                device_id=(x, y, 1 - c), device_id_type=MESH)
                for k, (s, d_) in enumerate(parts(2 * chip + (1 - c), got.at[chip]))])
            loads.append([pltpu.make_async_copy(s, d_, own_sems.at[chip, k])
                          for k, (s, d_) in enumerate(parts(2 * chip + c, own.at[chip]))])
            stores.append(pltpu.make_async_copy(acc.at[chip], out_ref.at[chip], out_sems.at[chip]))
        for group in sends + loads:
            for cp in group:
                cp.start()
        for chip in range(4):
            for cp in loads[chip]:
                cp.wait()
            for cp in sends[chip]:
                cp.wait_recv()
            acc[chip] = (own[chip].astype(F32) + got[chip].astype(F32)).astype(acc.dtype)
            stores[chip].start()
        for cp in stores:
            cp.wait()
        for group in sends:
            for cp in group:
                cp.wait_send()

    return pl.pallas_call(
        body, name=f"pair_reduce{half}",
        in_specs=[_any()] * npart, out_specs=_any(),
        out_shape=jax.ShapeDtypeStruct(blk, gwt.dtype),
        scratch_shapes=[pltpu.VMEM(blk, gwt.dtype), pltpu.VMEM(blk, gwt.dtype), pltpu.VMEM(blk, gwt.dtype),
                        pltpu.SemaphoreType.DMA((4, npart)), pltpu.SemaphoreType.DMA((4, npart)),
                        pltpu.SemaphoreType.DMA((4, npart)), pltpu.SemaphoreType.DMA((4,))],
        compiler_params=_cp(),
    )(gwt, *tails)


def _pad_cols(a, cols):
    return jnp.pad(a, ((0, 0), (0, cols - a.shape[1])))


def _pad_rows(a, rows):
    return jnp.pad(a, ((0, rows - a.shape[0]), (0, 0)))


FRAME = 928


def _wft_plan():
    moves = []
    for blk in range(8):
        for half in range(2):
            for sub in range(2):
                moves.append((C_Q + 128 * blk + 32 * (2 * half + sub), 128 * blk + 32 * (2 * sub + half), 32))
    for idx in range(4):
        for dup in range(2):
            moves.append((C_KD + 64 * idx + 32 * dup, 1024 + 32 * idx, 32))
    for g in range(2):
        for dup in range(2):
            moves.append((C_VD + 128 * g + 64 * dup, 1152 + 64 * g, 64))
    moves += [(C_BL, 5376, RANK), (C_BV, 3328, 1024), (C_BQ, 2304, 512), (C_BK, 2816, 512),
              (C_AG, 1280, 1024), (C_BG, 4352, 1024), (C_MA, 5392, 1024), (C_MB, 6416, 1024)]
    bulk, seams = [], []
    for dst, src, n in moves:
        r = src
        while r < src + n:
            f = min(r // FRAME, NDEV - 1)
            local = r - FRAME * f
            if f > 0 and local < 16:
                assert local == 0
                seams.append((f, dst + r - src))
                step = 16
            else:
                step = min(src + n, FRAME * (f + 1) if f < NDEV - 1 else IN_WIDTH) - r
                bulk.append((f, local, dst + r - src, step))
            r += step
    assert sorted(f for f, _ in seams) == list(range(1, NDEV))
    return bulk, seams, [(C_BL + RANK, C_GLA - C_BL - RANK)]


def _build_wft_copies(frames):
    bulk, seams, zeros = _wft_plan()
    (z0, zn), = zeros

    def body(f_ref, o_ref, edge, sems, esems):
        copies = [pltpu.make_async_copy(f_ref.at[f, pl.ds(l0, n)], o_ref.at[pl.ds(dst, n)], sems.at[i])
                  for i, (f, l0, dst, n) in enumerate(bulk)]
        loads = []
        for i, (f, _) in enumerate(seams):
            loads.append(pltpu.make_async_copy(f_ref.at[f, pl.ds(0, 16)], edge.at[i, 0], esems.at[i, 0]))
            loads.append(pltpu.make_async_copy(f_ref.at[f - 1, pl.ds(FRAME, 16)], edge.at[i, 1], esems.at[i, 1]))
        for cp in copies + loads:
            cp.start()
        o_ref[z0:z0 + zn, :] = jnp.zeros((zn, D), o_ref.dtype)
        for cp in loads:
            cp.wait()
        for i, (_, dst) in enumerate(seams):
            o_ref[dst:dst + 16, :] = edge[i, 0] + edge[i, 1]
        for cp in copies:
            cp.wait()

    return pl.pallas_call(
        body, name="build_wft",
        in_specs=[_any()], out_specs=_vmem(),
        out_shape=jax.ShapeDtypeStruct((NF, D), frames.dtype),
        scratch_shapes=[pltpu.VMEM((len(seams), 2, 16, D), frames.dtype),
                        pltpu.SemaphoreType.DMA((len(bulk),)), pltpu.SemaphoreType.DMA((len(seams), 2))],
        compiler_params=_cp(),
    )(frames)


def kernel(x, positions, norm_w, w_in, a_sinks, b_gate_up, b_gate_bias, b_out_norm_w, w_a_proj, w_b_proj, w_out, final_norm_w, loss_target, m_norm_w, m_w_in, m_a_sinks, m_b_gate_up, m_b_gate_bias, m_b_out_norm_w, m_w_a_proj, m_w_b_proj, m_w_out, m_final_norm_w, v_norm_w, v_w_in, v_a_sinks, v_b_gate_up, v_b_gate_bias, v_b_out_norm_w, v_w_a_proj, v_w_b_proj, v_w_out, v_final_norm_w):
    T = x.shape[1]
    xs, target = x[0], loss_target[0]
    fnw = final_norm_w.reshape(1, D)
    me = 4 * lax.axis_index("x") + 2 * lax.axis_index("y") + lax.axis_index("c")
    allw, h, cos, sin = _gather_blocks(_by_lane_tile(w_in), b_gate_up[0], xs, norm_w, positions.reshape(T, 1))
    late_blk = jnp.concatenate([w_a_proj[0], w_b_proj[0], w_out[0]], axis=0).astype(WIRE)
    l_send, l_recv, l_blk, l_land, l_started = _late_gather_start(late_blk, cos)
    wf = _build_wft_copies(allw)
    gu = allw[:, SHARD_PAD:G_ROWS, :64].transpose(1, 0, 2).reshape(RANK, 512)
    gu_pad = _pad_rows(gu, W_BL)

    proj = _proj(h, wf, l_started)
    o_a, lse = _swa_fwd(proj, cos, sin, a_sinks)
    o_b, states = _gla_fwd(proj, gu_pad, b_gate_bias)
    l_blk, l_land = _late_gather_wait(l_send, l_recv, l_blk, l_land, states, lse)
    late = lax.dynamic_update_slice(l_land, l_blk[None], (me, 0, 0))
    (dx2, do_a, do_b, d_gates, g_late0, g_late1, g_fn, g_bn, loss_part) = _mid(
        xs, target, proj, o_a, o_b, late, jnp.tile(b_out_norm_w, (1, B_HEADS)), fnw)
    d_q, d_kv, g_sinks = _swa_bwd(proj, cos, sin, a_sinks, do_a, o_a, lse, cos)
    d_gla, d_bl, g_gu, g_bias = _gla_bwd(proj, gu_pad, b_gate_bias, states, do_b)
    pieces = [d_q, d_kv, d_bl, d_gla, d_gates]
    offsets = [C_Q, C_KD, C_BL, C_GLA, C_GATES]

    ggu = g_gu[:RANK].reshape(RANK, NDEV, 64).transpose(1, 0, 2)
    ggu_half = [jnp.pad(ggu, ((0, 0), (0, 0), (0, DH - 64))).astype(WIRE), jnp.zeros((NDEV, RANK, DH), WIRE)]
    tails = [[g_late0, ggu_half[0]], [g_late1, ggu_half[1]]]

    send0, recv0, s_thru0, land0, started0 = _chip_start(_pair_reduce(_gw_half(h, pieces, 0), tails[0], 0), 0)
    send1, recv1, s_thru1, land1, started1 = _chip_start(
        _pair_reduce(_gw_half(h, pieces, 1, after=started0), tails[1], 1), 1)
    grad_x, g_nw = _dh_norm(pieces, offsets, wf, xs, dx2, norm_w, started1)
    small = jnp.concatenate([g_nw, g_fn, _pad_cols(g_bias, D), _pad_cols(g_bn, D), _pad_cols(g_sinks, D),
                             _pad_cols(loss_part, D)], axis=0)
    sm_send, sm_recv, sm_blk, sm_land, sm_started = _late_gather_start(small, g_nw, name="small_gather")
    sums0, got0 = _chip_wait(send0, recv0, s_thru0, land0, sm_started, 0)
    sums1, got1 = _chip_wait(send1, recv1, s_thru1, land1, got0, 1)
    sums, from_chips = [sums0, sums1], [got0, got1]

    ws = dict(norm_w=norm_w, fnw=fnw, bias=b_gate_bias, bn=b_out_norm_w, sinks=a_sinks)
    ms = dict(norm_w=m_norm_w, fnw=m_final_norm_w.reshape(1, D), bias=m_b_gate_bias, bn=m_b_out_norm_w,
              sinks=m_a_sinks)
    vs = dict(norm_w=v_norm_w, fnw=v_final_norm_w.reshape(1, D), bias=v_b_gate_bias, bn=v_b_out_norm_w,
              sinks=v_a_sinks)
    t_rows, t_gu = _finish(
        [_by_lane_tile(w_in), w_a_proj[0], w_b_proj[0], w_out[0]],
        [_by_lane_tile(m_w_in), m_w_a_proj[0], m_w_b_proj[0], m_w_out[0]],
        [_by_lane_tile(v_w_in), v_w_a_proj[0], v_w_b_proj[0], v_w_out[0]],
        b_gate_up[0], m_b_gate_up[0], v_b_gate_up[0], sums, from_chips)
    sm_blk, sm_land = _late_gather_wait(sm_send, sm_recv, sm_blk, sm_land, t_rows[0], t_gu[0], name="small_gather")
    loss, sm = _finish_small(ws, ms, vs, lax.dynamic_update_slice(sm_land, sm_blk[None], (me, 0, 0)))

    def outputs(k):
        return [sm["norm_w"][k], jnp.transpose(t_rows[k], (1, 2, 0)), sm["sinks"][k], t_gu[k][None], sm["bias"][k], sm["bn"][k],
                t_rows[4 + k][None], t_rows[8 + k][None], t_rows[12 + k][None], sm["fnw"][k].reshape(D)]

    return (loss[0, 0], grad_x[None], *outputs(0), *outputs(1), *outputs(2), *outputs(3))
```

```python
import functools

import numpy as np
import jax
import jax.numpy as jnp
from jax import lax
from jax.experimental import pallas as pl
from jax.experimental.pallas import tpu as pltpu

F32 = jnp.float32
MXU = jnp.bfloat16
WIRE = jnp.bfloat16

D = 1024
A_HEADS, A_KV, A_HD = 16, 2, 64
BLK = 128
B_HEADS, B_DK, B_DV = 4, 128, 256
RANK, TAU, CHUNK = 16, 16.0, 64
EPS, NEG = 1e-5, -1e30
ROPE_THETA = 10000.0
IN_WIDTH, NDEV = 7440, 8
SHARD = IN_WIDTH // NDEV
LANE = 128
LANE_TILES = D // LANE


def _by_lane_tile(a):
    return jnp.transpose(a, (2, 0, 1)).reshape(SHARD * LANE_TILES, LANE)


C_Q, C_KD, C_VD, C_BL = 0, 1024, 1280, 1536
C_BV, C_BQ, C_BK = 2048, 3072, 3584
C_AG, C_BG, C_MA, C_MB = 4096, 5120, 6144, 7168
C_GLA, W_GLA, C_GATES, W_GATES = 2048, 2048, 4096, 4096
NF = 8192
W_BL = 128

SHARD_PAD = 944
R_IN, R_A, R_B, R_O, R_GU, ROWS = 0, 944, 1072, 1200, 1328, 1344
SMALL_ROWS = 48

ADAM_LR, ADAM_B1, ADAM_B2, ADAM_EPS, ADAM_WD, ADAM_STEP = 0.001, 0.9, 0.999, 1e-08, 0.01, 10

MESH = pl.DeviceIdType.MESH
VMEM_LIMIT = 56 * 1024 * 1024


def _cp(sem=None, **kw):
    if sem is not None:
        kw["dimension_semantics"] = sem
    return pltpu.CompilerParams(vmem_limit_bytes=VMEM_LIMIT, **kw)


def _dot(a, b):
    return jnp.dot(a, b, preferred_element_type=F32)


def _dot_nt(a, b):
    return lax.dot_general(a, b, (((1,), (1,)), ((), ())), preferred_element_type=F32)


def _dot_tn(a, b):
    return lax.dot_general(a, b, (((0,), (0,)), ((), ())), preferred_element_type=F32)


def _dot_f32(a, b):
    return jnp.dot(a, b, preferred_element_type=F32, precision=lax.Precision.HIGHEST)


def _sigmoid(z):
    return 0.5 * jnp.tanh(0.5 * z) + 0.5


def _rope(xp, cos, sin):
    return xp * cos + pltpu.roll(xp, 64, 1) * sin


def _rope_bwd(dy, cos, sin):
    return dy * cos - pltpu.roll(dy, 64, 1) * sin


def _vmem():
    return pl.BlockSpec(memory_space=pltpu.VMEM)


def _any():
    return pl.BlockSpec(memory_space=pl.ANY)


def _rope_rows():
    half = A_HD // 2
    inv = (np.float32(ROPE_THETA) ** (-np.arange(half, dtype=np.float32) / np.float32(half))).astype(np.float32)
    inv_row = jnp.asarray(np.tile(inv, 4)[None, :])
    sign_row = jnp.asarray(np.concatenate([-np.ones(64, np.float32), np.ones(64, np.float32)])[None, :])
    return inv_row, sign_row


def _prologue_rows(rows, x_ref, nw_ref, pos_ref, inv_ref, sign_ref, h_ref, cos_ref, sin_ref):
    xv = x_ref[rows, :]
    r = lax.rsqrt(jnp.mean(xv * xv, axis=-1, keepdims=True) + EPS)
    h_ref[rows, :] = ((xv * r) * nw_ref[...]).astype(h_ref.dtype)
    ang = pos_ref[rows, :].astype(F32) * inv_ref[...]
    cos_ref[rows, :] = jnp.cos(ang)
    sin_ref[rows, :] = jnp.sin(ang) * sign_ref[...]


def _proj(h, wft, after):
    T = h.shape[0]
    tT, tN = T, 512

    def body(h_ref, w_ref, after_ref, o_ref):
        o_ref[...] = _dot_nt(h_ref[...], w_ref[...])

    return pl.pallas_call(
        body, name="proj", grid=(T // tT, NF // tN),
        in_specs=[pl.BlockSpec((tT, D), lambda i, j: (i, 0)), pl.BlockSpec((tN, D), lambda i, j: (j, 0)), _any()],
        out_specs=pl.BlockSpec((tT, tN), lambda i, j: (i, j)),
        out_shape=jax.ShapeDtypeStruct((T, NF), F32),
        compiler_params=_cp(("parallel", "parallel")),
    )(h, wft, after)


def _swa_masks():
    lane = lax.broadcasted_iota(jnp.int32, (BLK, LANE), 1)
    rope_sub0 = ((lane // 32) % 2) == 0
    std_sub0 = lane < 64
    return lane, rope_sub0, std_sub0


def _swa_tri():
    qi = lax.broadcasted_iota(jnp.int32, (BLK, BLK), 0)
    kj = lax.broadcasted_iota(jnp.int32, (BLK, BLK), 1)
    return kj <= qi


def _swa_fold(full, tri):
    return jnp.where(tri, full[:, BLK:], full[:, :BLK])


def _swa_unfold(sq, tri):
    return jnp.concatenate([jnp.where(tri, 0.0, sq), jnp.where(tri, sq, 0.0)], axis=1)


def _swa_keys(kc_ref, kp_ref, vc_ref, vp_ref, cq, sq, cp, sp):
    def ropek(kref, c, s):
        kv = kref[...]
        return jnp.concatenate([_rope(kv[:, :LANE], c, s), _rope(kv[:, LANE:], c, s)], axis=1)

    K = jnp.concatenate([ropek(kp_ref, cp, sp), ropek(kc_ref, cq, sq)], axis=0).astype(MXU)
    V = jnp.concatenate([vp_ref[...], vc_ref[...]], axis=0).astype(MXU)
    return K, V


def _swa_in_specs(nb, last):
    def cur(n):
        return jnp.minimum(n, last)

    def prev(n):
        return jnp.maximum(cur(n) - 1, 0)

    kd, vd = C_KD // 256, C_VD // 256
    return [
        pl.BlockSpec((BLK, D), lambda n: (cur(n), C_Q // D)),
        pl.BlockSpec((BLK, 256), lambda n: (cur(n), kd)),
        pl.BlockSpec((BLK, 256), lambda n: (prev(n), kd)),
        pl.BlockSpec((BLK, 256), lambda n: (cur(n), vd)),
        pl.BlockSpec((BLK, 256), lambda n: (prev(n), vd)),
        pl.BlockSpec((BLK, LANE), lambda n: (cur(n), 0)),
        pl.BlockSpec((BLK, LANE), lambda n: (cur(n), 0)),
        pl.BlockSpec((BLK, LANE), lambda n: (prev(n), 0)),
        pl.BlockSpec((BLK, LANE), lambda n: (prev(n), 0)),
    ]


def _swa_fwd(proj, cos, sin, sinks):
    T = proj.shape[0]
    nb = T // BLK
    scale = A_HD ** -0.5

    def body(sinks_ref, q_ref, kc_ref, kp_ref, vc_ref, vp_ref, cq_ref, sq_ref, cp_ref, sp_ref, o_ref, l_ref):
        n = pl.program_id(0)
        cq, sq = cq_ref[...], sq_ref[...]
        K, V = _swa_keys(kc_ref, kp_ref, vc_ref, vp_ref, cq, sq, cp_ref[...], sp_ref[...])
        tri = _swa_tri()
        valid = tri | (n > 0)
        lane, rope_sub0, std_sub0 = _swa_masks()
        group = A_HEADS // A_KV
        roped, lses = {}, []

        def products(head):
            pb, sub, g = head // 2, head % 2, head // group
            if sub == 0:
                roped[pb] = _rope(q_ref[:, pb * LANE:(pb + 1) * LANE], cq, sq)
            qm = jnp.where(rope_sub0 if sub == 0 else ~rope_sub0, roped[pb], 0.0).astype(MXU)
            return _dot_nt(qm, K[:, g * LANE:(g + 1) * LANE])

        def softmax(head, s_full):
            s = jnp.where(valid, _swa_fold(s_full, tri) * scale, NEG)
            sink = sinks_ref[0, head]
            m = jnp.maximum(jnp.max(s, axis=1, keepdims=True), sink)
            e = jnp.exp(s - m)
            den = jnp.sum(e, axis=1, keepdims=True) + jnp.exp(sink - m)
            lses.append(m + jnp.log(den))
            return _swa_unfold(e / den, tri).astype(MXU)

        outs = {}
        st1 = {0: products(0), 1: products(1)}
        st2 = {0: softmax(0, st1.pop(0))}
        for head in range(A_HEADS):
            if head + 2 < A_HEADS:
                st1[head + 2] = products(head + 2)
            if head + 1 < A_HEADS:
                st2[head + 1] = softmax(head + 1, st1.pop(head + 1))
            g = head // group
            outs[head] = _dot(st2.pop(head), V[:, g * LANE:(g + 1) * LANE])
            if head % 2 == 1:
                pb = head // 2
                o_ref[:, pb * LANE:(pb + 1) * LANE] = jnp.where(std_sub0, outs[head - 1], outs[head])
        lacc = jnp.zeros((BLK, LANE), F32)
        for head in range(A_HEADS):
            lacc = jnp.where(lane == head, lses[head], lacc)
        l_ref[...] = lacc

    return pl.pallas_call(
        body, name="swa_fwd", grid=(nb,),
        in_specs=[pl.BlockSpec(memory_space=pltpu.SMEM)] + _swa_in_specs(nb, nb - 1),
        out_specs=[pl.BlockSpec((BLK, D), lambda n: (n, 0)), pl.BlockSpec((BLK, LANE), lambda n: (n, 0))],
        out_shape=[jax.ShapeDtypeStruct((T, D), F32), jax.ShapeDtypeStruct((T, LANE), F32)],
        compiler_params=_cp(("parallel",)),
    )(sinks, proj, proj, proj, proj, proj, cos, sin, cos, sin)


def _swa_bwd(proj, cos, sin, sinks, do_a, o_a, lse, after):
    T = proj.shape[0]
    nb = T // BLK
    scale = A_HD ** -0.5

    def body(sinks_ref, q_ref, kc_ref, kp_ref, vc_ref, vp_ref, cq_ref, sq_ref, cp_ref, sp_ref,
             do_ref, o_ref, l_ref, after_ref, dq_ref, dkv_ref, ds_ref, ckv_ref):
        n = pl.program_id(0)

        @pl.when(n == 0)
        def _():
            ckv_ref[...] = jnp.zeros_like(ckv_ref)
            ds_ref[...] = jnp.zeros_like(ds_ref)

        @pl.when(n < nb)
        def _():
            cq, sq, cp, sp = cq_ref[...], sq_ref[...], cp_ref[...], sp_ref[...]
            K, V = _swa_keys(kc_ref, kp_ref, vc_ref, vp_ref, cq, sq, cp, sp)
            tri = _swa_tri()
            valid = tri | (n > 0)
            lane, rope_sub0, std_sub0 = _swa_masks()
            lane_row = lax.broadcasted_iota(jnp.int32, (1, LANE), 1)
            lse_v = l_ref[...]
            dKt = [jnp.zeros((LANE, 2 * BLK), F32) for _ in range(A_KV)]
            dVt = [jnp.zeros((LANE, 2 * BLK), F32) for _ in range(A_KV)]
            dsinks, roped, roped_t, do_t = [], {}, {}, {}
            group = A_HEADS // A_KV
            dim = lax.broadcasted_iota(jnp.int32, (LANE, BLK), 0)
            rope_row0, std_row0 = ((dim // 32) % 2) == 0, dim < 64

            def products(head):
                pb, sub, g = head // 2, head % 2, head // group
                cols = slice(pb * LANE, (pb + 1) * LANE)
                Kg, Vg = K[:, g * LANE:(g + 1) * LANE], V[:, g * LANE:(g + 1) * LANE]
                if sub == 0:
                    roped[pb] = _rope(q_ref[:, cols], cq, sq)
                    roped_t[pb] = roped[pb].T
                    do_t[pb] = do_ref[:, cols].T
                qm = jnp.where(rope_sub0 if sub == 0 else ~rope_sub0, roped[pb], 0.0).astype(MXU)
                qmt = jnp.where(rope_row0 if sub == 0 else ~rope_row0, roped_t[pb], 0.0).astype(MXU)
                dov = jnp.where(std_sub0 if sub == 0 else ~std_sub0, do_ref[:, cols], 0.0)
                dovt = jnp.where(std_row0 if sub == 0 else ~std_row0, do_t[pb], 0.0).astype(MXU)
                delta = jnp.sum(dov * o_ref[:, cols], axis=1, keepdims=True)
                return qmt, dovt, delta, _dot_nt(qm, Kg), _dot_nt(dov.astype(MXU), Vg)

            def scores(head, qmt, dovt, delta, s_full, dp_full):
                lh = jnp.sum(jnp.where(lane == head, lse_v, 0.0), axis=1, keepdims=True)
                p = jnp.where(valid, jnp.exp(_swa_fold(s_full, tri) * scale - lh), 0.0)
                psink = jnp.exp(sinks_ref[0, head] - lh)
                dsinks.append(jnp.sum(-psink * delta, axis=0, keepdims=True))
                dsq = (p * (_swa_fold(dp_full, tri) - delta)) * scale
                return qmt, dovt, _swa_unfold(p, tri).astype(MXU), _swa_unfold(dsq, tri).astype(MXU)

            def grads(head, qmt, dovt, pb16, dsc):
                g = head // group
                dKt[g] = dKt[g] + _dot(qmt, dsc)
                dVt[g] = dVt[g] + _dot(dovt, pb16)
                return _dot(dsc, K[:, g * LANE:(g + 1) * LANE])

            dqs = {}
            st1 = {0: products(0), 1: products(1)}
            st2 = {0: scores(0, *st1.pop(0))}
            for head in range(A_HEADS):
                if head + 2 < A_HEADS:
                    st1[head + 2] = products(head + 2)
                if head + 1 < A_HEADS:
                    st2[head + 1] = scores(head + 1, *st1.pop(head + 1))
                dqs[head] = grads(head, *st2.pop(head))
                if head % 2 == 1:
                    pb = head // 2
                    dqp = jnp.where(rope_sub0, dqs[head - 1], dqs[head])
                    dq_ref[:, pb * LANE:(pb + 1) * LANE] = _rope_bwd(dqp, cq, sq).astype(dq_ref.dtype)
            dsink = jnp.zeros((1, LANE), F32)
            for head in range(A_HEADS):
                dsink = jnp.where(lane_row == head, dsinks[head], dsink)
            dK, dV = [a.T for a in dKt], [a.T for a in dVt]
            prev = ([_rope_bwd(dK[g][:BLK], cp, sp) for g in range(A_KV)] + [dV[g][:BLK] for g in range(A_KV)])
            cur_ = ([_rope_bwd(dK[g][BLK:], cq, sq) for g in range(A_KV)] + [dV[g][BLK:] for g in range(A_KV)])
            dkv_ref[...] = (ckv_ref[...] + jnp.concatenate(prev, axis=1)).astype(dkv_ref.dtype)
            ckv_ref[...] = jnp.concatenate(cur_, axis=1)
            ds_ref[...] = ds_ref[...] + jnp.broadcast_to(dsink, ds_ref.shape)

        @pl.when(n == nb)
        def _():
            dkv_ref[...] = ckv_ref[...].astype(dkv_ref.dtype)

    last = nb - 1

    def cur(n):
        return jnp.minimum(n, last)

    def out_kv(n):
        return (jnp.maximum(n - 1, 0), 0)

    return pl.pallas_call(
        body, name="swa_bwd", grid=(nb + 1,),
        in_specs=[pl.BlockSpec(memory_space=pltpu.SMEM)] + _swa_in_specs(nb, last) + [
            pl.BlockSpec((BLK, D), lambda n: (cur(n), 0)),
            pl.BlockSpec((BLK, D), lambda n: (cur(n), 0)),
            pl.BlockSpec((BLK, LANE), lambda n: (cur(n), 0)),
            _any(),
        ],
        out_specs=[
            pl.BlockSpec((BLK, D), lambda n: (cur(n), 0)),
            pl.BlockSpec((BLK, 512), out_kv),
            pl.BlockSpec((8, LANE), lambda n: (0, 0)),
        ],
        out_shape=[
            jax.ShapeDtypeStruct((T, D), MXU),
            jax.ShapeDtypeStruct((T, 512), MXU),
            jax.ShapeDtypeStruct((8, LANE), F32),
        ],
        scratch_shapes=[pltpu.VMEM((BLK, 512), F32)],
        compiler_params=_cp(("arbitrary",)),
    )(sinks, proj, proj, proj, proj, proj, cos, sin, cos, sin, do_a, o_a, lse, after)


NCH = 4
GSTEP = NCH * CHUNK
ST_ROWS = B_HEADS * B_DV


def _chunk_rows(c):
    return slice(c * CHUNK, (c + 1) * CHUNK)


def _per_chunk(which, vals):
    out = vals[-1]
    for c in range(NCH - 2, -1, -1):
        out = jnp.where(which == c, vals[c], out)
    return out


def _gla_gate(bl_ref, gu_ref, bias_ref):
    gk = _dot(bl_ref[...].astype(MXU), gu_ref[...]) + bias_ref[...]
    la = (jnp.minimum(gk, 0.0) - jnp.log(1.0 + jnp.exp(-jnp.abs(gk)))) / TAU
    ri = lax.broadcasted_iota(jnp.int32, (GSTEP, GSTEP), 0)
    ci = lax.broadcasted_iota(jnp.int32, (GSTEP, GSTEP), 1)
    same = (ri // CHUNK) == (ci // CHUNK)
    lower, upper = same & (ci <= ri), same & (ci >= ri)
    b = _dot_f32(jnp.where(lower, 1.0, 0.0).astype(F32), la)
    which = lax.broadcasted_iota(jnp.int32, (GSTEP, 1), 0) // CHUNK
    return gk, la, b, lower, upper, which


def _gla_head(q_ref, k_ref, la, b, which, h):
    sl = slice(h * B_DK, (h + 1) * B_DK)
    bh, lah = b[:, sl], la[:, sl]
    bls = [jnp.sum(lah[_chunk_rows(c)], axis=0, keepdims=True) for c in range(NCH)]
    blast = _per_chunk(which, bls)
    qc = q_ref[:, sl] * (B_DK ** -0.5)
    kh = k_ref[:, sl]
    eb, enb, esb = jnp.exp(bh), jnp.exp(-bh), jnp.exp(blast - bh)
    return qc * eb, kh * enb, kh * esb, eb, enb, esb, [jnp.exp(v) for v in bls]


def _gla_specs(step_of):
    return [
        pl.BlockSpec((GSTEP, 512), lambda i: (step_of(i), C_BQ // 512)),
        pl.BlockSpec((GSTEP, 512), lambda i: (step_of(i), C_BK // 512)),
        pl.BlockSpec((GSTEP, D), lambda i: (step_of(i), C_BV // D)),
        pl.BlockSpec((GSTEP, W_BL), lambda i: (step_of(i), C_BL // W_BL)),
        pl.BlockSpec((W_BL, 512), lambda i: (0, 0)),
        pl.BlockSpec((1, 512), lambda i: (0, 0)),
    ]


def _gla_fwd(proj, gu_pad, bias):
    T = proj.shape[0]
    ns = T // GSTEP

    def body(q_ref, k_ref, v_ref, bl_ref, gu_ref, bias_ref, o_ref, st_ref, state_ref):
        @pl.when(pl.program_id(0) == 0)
        def _():
            state_ref[...] = jnp.zeros_like(state_ref)

        _, la, b, lower, _, which = _gla_gate(bl_ref, gu_ref, bias_ref)

        def within(h):
            q_e, k_e, k_s, _, _, _, decays = _gla_head(q_ref, k_ref, la, b, which, h)
            vh = v_ref[:, h * B_DV:(h + 1) * B_DV].astype(MXU)
            q_eb = q_e.astype(MXU)
            att = jnp.where(lower, _dot_nt(q_eb, k_e.astype(MXU)), 0.0)
            return vh, q_eb, k_s.astype(MXU), _dot(att.astype(MXU), vh), decays

        def across(h, vh, q_eb, k_sb, o_intra, decays):
            rows = slice(h * B_DV, (h + 1) * B_DV)
            s = state_ref[rows, :]
            outs = []
            for c in range(NCH):
                cr = _chunk_rows(c)
                st_ref[c * ST_ROWS + h * B_DV:c * ST_ROWS + (h + 1) * B_DV, :] = s
                outs.append(o_intra[cr] + _dot_nt(q_eb[cr], s.astype(MXU)))
                s = s * decays[c] + _dot_tn(vh[cr], k_sb[cr])
            state_ref[rows, :] = s
            o_ref[:, rows] = jnp.concatenate(outs, axis=0)

        for h in range(B_HEADS):
            across(h, *within(h))

    return pl.pallas_call(
        body, name="gla_fwd", grid=(ns,),
        in_specs=_gla_specs(lambda i: i),
        out_specs=[pl.BlockSpec((GSTEP, D), lambda i: (i, 0)),
                   pl.BlockSpec((NCH * ST_ROWS, B_DK), lambda i: (i, 0))],
        out_shape=[jax.ShapeDtypeStruct((T, D), F32),
                   jax.ShapeDtypeStruct((ns * NCH * ST_ROWS, B_DK), F32)],
        scratch_shapes=[pltpu.VMEM((ST_ROWS, B_DK), F32)],
        compiler_params=_cp(("arbitrary",)),
    )(proj, proj, proj, proj, gu_pad, bias)


def _gla_bwd(proj, gu_pad, bias, states, do_b):
    T = proj.shape[0]
    ns = T // GSTEP
    o_q, o_k = C_BQ - C_GLA, C_BK - C_GLA

    def body(q_ref, k_ref, v_ref, bl_ref, gu_ref, bias_ref, st_ref, do_ref,
             dg_ref, dbl_ref, ggu_ref, gbias_ref, gt_ref):
        @pl.when(pl.program_id(0) == 0)
        def _():
            gt_ref[...] = jnp.zeros_like(gt_ref)
            ggu_ref[...] = jnp.zeros_like(ggu_ref)
            gbias_ref[...] = jnp.zeros_like(gbias_ref)

        gk, la, b, lower, upper_mask, which = _gla_gate(bl_ref, gu_ref, bias_ref)
        upper = jnp.where(upper_mask, 1.0, 0.0).astype(F32)
        dla_parts = []

        def within(h):
            q_e, k_e, k_s, eb, enb, esb, decays = _gla_head(q_ref, k_ref, la, b, which, h)
            vh = v_ref[:, h * B_DV:(h + 1) * B_DV].astype(MXU)
            doh = do_ref[:, h * B_DV:(h + 1) * B_DV].astype(MXU)
            q_eb, k_eb = q_e.astype(MXU), k_e.astype(MXU)
            att = jnp.where(lower, _dot_nt(q_eb, k_eb), 0.0).astype(MXU)
            datt = jnp.where(lower, _dot_nt(doh, vh), 0.0).astype(MXU)
            return (q_e, k_e, k_s, eb, enb, esb, decays, vh, doh, q_eb, k_s.astype(MXU),
                    _dot(datt, k_eb), _dot_tn(datt, q_eb), _dot_tn(att, doh))

        def across(h, q_e, k_e, k_s, eb, enb, esb, decays, vh, doh, q_eb, k_sb, dq_i, dk_e, dv_i):
            rows = slice(h * B_DV, (h + 1) * B_DV)
            g = gt_ref[rows, :]
            dq_c, dks_c, dv_c, ddec = [None] * NCH, [None] * NCH, [None] * NCH, [None] * NCH
            for c in range(NCH - 1, -1, -1):
                cr = _chunk_rows(c)
                s = st_ref[c * ST_ROWS + h * B_DV:c * ST_ROWS + (h + 1) * B_DV, :]
                gb = g.astype(MXU)
                dq_c[c] = dq_i[cr] + _dot(doh[cr], s.astype(MXU))
                dks_c[c] = _dot(vh[cr], gb)
                dv_c[c] = dv_i[cr] + _dot_nt(k_sb[cr], gb)
                ddec[c] = jnp.sum(g * s, axis=0, keepdims=True)
                g = g * decays[c] + _dot_tn(doh[cr], q_eb[cr])
            gt_ref[rows, :] = g
            dq_e = jnp.concatenate(dq_c, axis=0)
            dk_s = jnp.concatenate(dks_c, axis=0)
            dg_ref[:, rows] = jnp.concatenate(dv_c, axis=0).astype(dg_ref.dtype)
            dg_ref[:, o_q + h * B_DK:o_q + (h + 1) * B_DK] = (dq_e * eb * (B_DK ** -0.5)).astype(dg_ref.dtype)
            dg_ref[:, o_k + h * B_DK:o_k + (h + 1) * B_DK] = (dk_e * enb + dk_s * esb).astype(dg_ref.dtype)
            dks_ks = dk_s * k_s
            db = dq_e * q_e - dk_e * k_e - dks_ks
            dbl = [jnp.sum(dks_ks[_chunk_rows(c)], axis=0, keepdims=True) + ddec[c] * decays[c] for c in range(NCH)]
            dla_parts.append(_dot_f32(upper, db) + _per_chunk(which, dbl))

        for h in range(B_HEADS):
            across(h, *within(h))
        dla = jnp.concatenate(dla_parts, axis=1)
        dgk = dla * (1.0 / TAU) * _sigmoid(-gk)
        dgkb = dgk.astype(MXU)
        dbl_ref[...] = _dot_nt(dgkb, gu_ref[...]).astype(dbl_ref.dtype)
        ggu_ref[...] = ggu_ref[...] + _dot_tn(bl_ref[...].astype(MXU), dgkb)
        gbias_ref[...] = gbias_ref[...] + jnp.broadcast_to(jnp.sum(dgk, axis=0, keepdims=True), gbias_ref.shape)

    def rev(i):
        return ns - 1 - i

    return pl.pallas_call(
        body, name="gla_bwd", grid=(ns,),
        in_specs=_gla_specs(rev) + [
            pl.BlockSpec((NCH * ST_ROWS, B_DK), lambda i: (rev(i), 0)),
            pl.BlockSpec((GSTEP, D), lambda i: (rev(i), 0)),
        ],
        out_specs=[
            pl.BlockSpec((GSTEP, W_GLA), lambda i: (rev(i), 0)),
            pl.BlockSpec((GSTEP, W_BL), lambda i: (rev(i), 0)),
            pl.BlockSpec((W_BL, 512), lambda i: (0, 0)),
            pl.BlockSpec((8, 512), lambda i: (0, 0)),
        ],
        out_shape=[
            jax.ShapeDtypeStruct((T, W_GLA), MXU),
            jax.ShapeDtypeStruct((T, W_BL), MXU),
            jax.ShapeDtypeStruct((W_BL, 512), F32),
            jax.ShapeDtypeStruct((8, 512), F32),
        ],
        scratch_shapes=[pltpu.VMEM((B_HEADS * B_DV, B_DK), F32)],
        compiler_params=_cp(("arbitrary",)),
    )(proj, proj, proj, proj, gu_pad, bias, states, do_b)


def _mid(x, target, proj, o_a, o_b, late, w_bn4, fnw):
    T = x.shape[0]
    tT = min(T, 128)
    nbuf = 4
    o_ag, o_bg, o_ma, o_mb = (c - C_GATES for c in (C_AG, C_BG, C_MA, C_MB))

    def body(x_ref, t_ref, oa_ref, ob_ref, gates_ref, late_ref, wbn_ref, fnw_ref,
             dx2_ref, doa_ref, dob_ref, dgates_ref,
             tail0_ref, tail1_ref, gfn_ref, gbn_ref, loss_ref, buf_ref, gw_ref):
        i = pl.program_id(0)

        def weight(p):
            return late_ref[:, 128 * p:128 * (p + 1), :].reshape(D, D)

        @pl.when(i == 0)
        def _():
            for r in (gw_ref, gfn_ref, gbn_ref, loss_ref):
                r[...] = jnp.zeros_like(r)

        rows = pl.ds(pl.multiple_of((i % nbuf) * tT, tT), tT)

        def keep(k, val):
            buf_ref[k, rows, :] = val

        oa, ag = oa_ref[...], gates_ref[:, o_ag:o_ag + D]
        sg_a = _sigmoid(ag)
        silu_a = ag * sg_a
        oag_b = (oa * silu_a).astype(MXU)
        keep(0, oag_b)
        y_a = _dot(oag_b, weight(0))

        ob, bg = ob_ref[...], gates_ref[:, o_bg:o_bg + D]
        rbs, obhats = [], []
        for h in range(B_HEADS):
            obh = ob[:, h * B_DV:(h + 1) * B_DV]
            rb = lax.rsqrt(jnp.mean(obh * obh, axis=-1, keepdims=True) + EPS)
            rbs.append(rb)
            obhats.append(obh * rb)
        obhat = jnp.concatenate(obhats, axis=1)
        wbn = wbn_ref[...]
        obn = obhat * wbn
        sg_b = _sigmoid(bg)
        silu_b = bg * sg_b
        obg_b = (obn * silu_b).astype(MXU)
        keep(1, obg_b)
        y_b = _dot(obg_b, weight(1))

        sa, sb = _sigmoid(gates_ref[:, o_ma:o_ma + D]), _sigmoid(gates_ref[:, o_mb:o_mb + D])
        mg_b = (sa * y_a + sb * y_b).astype(MXU)
        keep(2, mg_b)
        x2 = x_ref[...] + _dot(mg_b, weight(2))
        r2 = lax.rsqrt(jnp.mean(x2 * x2, axis=-1, keepdims=True) + EPS)
        xh2 = x2 * r2
        fw = fnw_ref[...]
        err = xh2 * fw - t_ref[...]
        tok = jnp.mean(err * err, axis=-1, keepdims=True)
        loss_ref[...] = loss_ref[...] + 0.5 * jnp.sum(tok, axis=0, keepdims=True)

        dy = err * (1.0 / D)
        gfn_ref[...] = gfn_ref[...] + jnp.broadcast_to(jnp.sum(dy * xh2, axis=0, keepdims=True), gfn_ref.shape)
        gy = dy * fw
        dx2 = r2 * (gy - xh2 * jnp.mean(gy * xh2, axis=-1, keepdims=True))
        dx2_ref[...] = dx2
        dx2_b = dx2.astype(MXU)
        keep(5, dx2_b)
        dmg = _dot_nt(dx2_b, weight(2))

        dgates_ref[:, o_ma:o_ma + D] = (dmg * y_a * sa * (1.0 - sa)).astype(dgates_ref.dtype)
        dgates_ref[:, o_mb:o_mb + D] = (dmg * y_b * sb * (1.0 - sb)).astype(dgates_ref.dtype)
        dya_b = (dmg * sa).astype(MXU)
        dyb_b = (dmg * sb).astype(MXU)
        keep(3, dya_b)
        keep(4, dyb_b)
        doag = _dot_nt(dya_b, weight(0))
        dobg = _dot_nt(dyb_b, weight(1))

        @pl.when(i % nbuf == nbuf - 1)
        def _():
            for p in range(3):
                gw_ref[p] = gw_ref[p] + _dot_tn(buf_ref[p], buf_ref[3 + p])

        @pl.when(i == pl.num_programs(0) - 1)
        def _():
            for hf, tail_ref in enumerate((tail0_ref, tail1_ref)):
                for d in range(NDEV):
                    for p in range(3):
                        tail_ref[d, 128 * p:128 * (p + 1), :] = (
                            gw_ref[p, 128 * d:128 * (d + 1), hf * DH:(hf + 1) * DH].astype(tail_ref.dtype))

        doa_ref[...] = doag * silu_a
        dgates_ref[:, o_ag:o_ag + D] = (doag * oa * (sg_a * (1.0 + ag * (1.0 - sg_a)))).astype(dgates_ref.dtype)
        dobn = dobg * silu_b
        dgates_ref[:, o_bg:o_bg + D] = (dobg * obn * (sg_b * (1.0 + bg * (1.0 - sg_b)))).astype(dgates_ref.dtype)
        gg = dobn * wbn
        gbn = jnp.zeros((1, B_DV), F32)
        for h in range(B_HEADS):
            sl = slice(h * B_DV, (h + 1) * B_DV)
            gbn = gbn + jnp.sum(dobn[:, sl] * obhats[h], axis=0, keepdims=True)
            ggh = gg[:, sl]
            dob_ref[:, sl] = rbs[h] * (ggh - obhats[h] * jnp.mean(ggh * obhats[h], axis=-1, keepdims=True))
        gbn_ref[...] = gbn_ref[...] + jnp.broadcast_to(gbn, gbn_ref.shape)

    assert (T // tT) % nbuf == 0
    tile = pl.BlockSpec((tT, D), lambda i: (i, 0))
    row = pl.BlockSpec((1, D), lambda i: (0, 0))
    acc8 = pl.BlockSpec((8, D), lambda i: (0, 0))
    return pl.pallas_call(
        body, name="mid", grid=(T // tT,),
        in_specs=[tile, tile, tile, tile, pl.BlockSpec((tT, W_GATES), lambda i: (i, C_GATES // W_GATES)),
                  _vmem(), row, row],
        out_specs=[tile, tile, tile, pl.BlockSpec((tT, W_GATES), lambda i: (i, 0)), _vmem(), _vmem(),
                   acc8, pl.BlockSpec((8, B_DV), lambda i: (0, 0)), pl.BlockSpec((8, LANE), lambda i: (0, 0))],
        out_shape=[
            jax.ShapeDtypeStruct((T, D), F32),
            jax.ShapeDtypeStruct((T, D), F32),
            jax.ShapeDtypeStruct((T, D), F32),
            jax.ShapeDtypeStruct((T, W_GATES), MXU),
            jax.ShapeDtypeStruct((NDEV, 384, DH), WIRE),
            jax.ShapeDtypeStruct((NDEV, 384, DH), WIRE),
            jax.ShapeDtypeStruct((8, D), F32),
            jax.ShapeDtypeStruct((8, B_DV), F32),
            jax.ShapeDtypeStruct((8, LANE), F32),
        ],
        scratch_shapes=[pltpu.VMEM((6, nbuf * tT, D), MXU), pltpu.VMEM((3, D, D), F32)],
        compiler_params=_cp(("arbitrary",)),
    )(x, target, o_a, o_b, proj, late, w_bn4, fnw)


DH = D // 2


_GW_TILES = (("q", 0, 512, 0), ("q", 1, 512, 512), ("kv", 0, 256, 1024), ("bl", 0, RANK, 5376),
             ("gla", 0, 512, 3328), ("gla", 1, 512, 3840), ("gla", 2, 512, 2304), ("gla", 3, 512, 2816),
             ("gates", 0, 512, 1280), ("gates", 1, 512, 1792), ("gates", 2, 512, 4352), ("gates", 3, 512, 4864),
             ("gates", 4, 512, 5392), ("gates", 5, 512, 5904), ("gates", 6, 512, 6416), ("gates", 7, 512, 6928))


def _gw_unpermute(piece, t):
    if piece == "q":
        parts = []
        for blk in range(t.shape[0] // LANE):
            g = [t[blk * LANE + 32 * i:blk * LANE + 32 * (i + 1)] for i in range(4)]
            parts += [g[0], g[2], g[1], g[3]]
        return jnp.concatenate(parts, axis=0)
    if piece == "kv":
        k = [t[64 * i:64 * i + 32] + t[64 * i + 32:64 * i + 64] for i in range(4)]
        v = [t[256 + 128 * g:256 + 128 * g + 64] + t[256 + 128 * g + 64:256 + 128 * (g + 1)] for g in range(2)]
        return jnp.concatenate(k + v, axis=0)
    if piece == "bl":
        return t[:RANK]
    return t


def _gw_half(h, pieces, half, after=None):
    T = h.shape[0]
    steps = len(_GW_TILES)

    def body(*refs):
        h_ref = refs[0]
        srcs = dict(zip(("q", "kv", "bl", "gla", "gates"), refs[1:6]))
        o_ref, stage, sems = refs[-3:]
        j = pl.program_id(0)

        def out_copy(k):
            _, _, n, off = _GW_TILES[k]
            return pltpu.make_async_copy(stage.at[k % 2, 0:n], o_ref.at[pl.ds(off, n)], sems.at[k % 2])

        for k, (piece, _, n, _) in enumerate(_GW_TILES):
            @pl.when(j == k)
            def _(k=k, piece=piece, n=n):
                if k >= 2:
                    out_copy(k - 2).wait()
                t = _gw_unpermute(piece, _dot_tn(srcs[piece][...], h_ref[...]))
                stage[k % 2, 0:n, :] = t.astype(stage.dtype)
                out_copy(k).start()

        @pl.when(j == steps - 1)
        def _():
            out_copy(steps - 2).wait()
            out_copy(steps - 1).wait()

    def tile_of(lo, hi):
        return lambda j: (0, jnp.clip(j - lo, 0, hi - lo - 1))

    in_specs = [pl.BlockSpec((T, DH), lambda j: (0, half)),
                pl.BlockSpec((T, 512), tile_of(0, 2)), pl.BlockSpec((T, 512), lambda j: (0, 0)),
                pl.BlockSpec((T, W_BL), lambda j: (0, 0)),
                pl.BlockSpec((T, 512), tile_of(4, 8)), pl.BlockSpec((T, 512), tile_of(8, 16))]
    args = [h, *pieces]
    if after is not None:
        in_specs.append(_any())
        args.append(after)
    return pl.pallas_call(
        body, name=f"gw_in_half{half}", grid=(steps,),
        in_specs=in_specs, out_specs=_any(),
        out_shape=jax.ShapeDtypeStruct((IN_WIDTH, DH), WIRE),
        scratch_shapes=[pltpu.VMEM((2, 512, DH), WIRE), pltpu.SemaphoreType.DMA((2,))],
        compiler_params=_cp(("arbitrary",)),
    )(*args)


def _chip_copies(s_ref, got_ref, send_sems, recv_sems):
    x, y, c = _place()
    chips = [(1 - x, y), (x, 1 - y), (1 - x, 1 - y)]
    return [pltpu.make_async_remote_copy(
        src_ref=s_ref.at[2 * px + py], dst_ref=got_ref.at[j],
        send_sem=send_sems.at[j], recv_sem=recv_sems.at[j], device_id=(px, py, c), device_id_type=MESH)
        for j, (px, py) in enumerate(chips)]


_EFFECT = pltpu.SideEffectType.DATAFLOW_SIDE_EFFECTING


def _hbm():
    return pl.BlockSpec(memory_space=pltpu.HBM)


def _sem():
    return pl.BlockSpec(memory_space=pltpu.SEMAPHORE)


def _chip_start(sums, half):
    land = pltpu.with_memory_space_constraint(lax.empty((3,) + sums.shape[1:], sums.dtype), pltpu.HBM)

    def body(s_ref, land_ref, send_sems, recv_sems, s_thru, land_thru, token):
        for cp in _chip_copies(s_ref, land_ref, send_sems, recv_sems):
            cp.start()
        token[...] = jnp.zeros_like(token)

    return pl.pallas_call(
        body, name=f"chip_start{half}",
        out_shape=(pltpu.SemaphoreType.DMA((3,)), pltpu.SemaphoreType.DMA((3,)),
                   pltpu.HBM(sums.shape, sums.dtype), pltpu.HBM(land.shape, land.dtype),
                   jax.ShapeDtypeStruct((8, LANE), F32)),
        in_specs=(_hbm(), _hbm()), out_specs=(_sem(), _sem(), _hbm(), _hbm(), _vmem()),
        input_output_aliases={0: 2, 1: 3},
        compiler_params=pltpu.CompilerParams(has_side_effects=_EFFECT),
    )(pltpu.with_memory_space_constraint(sums, pltpu.HBM), land)


def _chip_wait(send_sems, recv_sems, s_thru, land_thru, after, half):
    def body(s_ref, land_ref, send_sems, recv_sems, after_ref, s_out, got_ref):
        copies = _chip_copies(s_ref, land_ref, send_sems, recv_sems)
        for cp in copies:
            cp.wait_send()
        for cp in copies:
            cp.wait_recv()

    return pl.pallas_call(
        body, name=f"chip_wait{half}",
        out_shape=(pltpu.HBM(s_thru.shape, s_thru.dtype), pltpu.HBM(land_thru.shape, land_thru.dtype)),
        in_specs=(_hbm(), _hbm(), _sem(), _sem(), _any()), out_specs=(_hbm(), _hbm()),
        input_output_aliases={0: 0, 1: 1},
        compiler_params=pltpu.CompilerParams(has_side_effects=_EFFECT),
    )(s_thru, land_thru, send_sems, recv_sems, after)


def _dh_norm(pieces, offsets, wf, x, dx2, norm_w, after):
    T = x.shape[0]
    tT = min(T, 256)
    widths = [p.shape[1] for p in pieces]
    npc = len(pieces)

    def body(*refs):
        dp_refs = refs[:npc]
        wf_ref, x_ref, dx2_ref, nw_ref, _, gx_ref, gnw_ref = refs[npc:]

        @pl.when(pl.program_id(0) == 0)
        def _():
            gnw_ref[...] = jnp.zeros_like(gnw_ref)

        dh = jnp.zeros((tT, D), F32)
        for dp_ref, off, w in zip(dp_refs, offsets, widths):
            dh = dh + _dot(dp_ref[...], wf_ref[off:off + w, :])
        xv = x_ref[...]
        r = lax.rsqrt(jnp.mean(xv * xv, axis=-1, keepdims=True) + EPS)
        xh = xv * r
        gnw_ref[...] = gnw_ref[...] + jnp.broadcast_to(jnp.sum(dh * xh, axis=0, keepdims=True), gnw_ref.shape)
        g = dh * nw_ref[...]
        gx_ref[...] = r * (g - xh * jnp.mean(g * xh, axis=-1, keepdims=True)) + dx2_ref[...]

    tile = pl.BlockSpec((tT, D), lambda i: (i, 0))
    return pl.pallas_call(
        body, name="dh_norm", grid=(T // tT,),
        in_specs=[pl.BlockSpec((tT, w), lambda i: (i, 0)) for w in widths]
        + [_vmem(), tile, tile, pl.BlockSpec((1, D), lambda i: (0, 0)), _any()],
        out_specs=[tile, pl.BlockSpec((8, D), lambda i: (0, 0))],
        out_shape=[jax.ShapeDtypeStruct((T, D), F32), jax.ShapeDtypeStruct((8, D), F32)],
        compiler_params=_cp(("arbitrary",)),
    )(*pieces, wf, x, dx2, norm_w, after)


def _adamw_math(w, g, m, v):
    m = ADAM_B1 * m + (1.0 - ADAM_B1) * g
    v = ADAM_B2 * v + (1.0 - ADAM_B2) * (g * g)
    m_hat = m * (1.0 / (1.0 - ADAM_B1 ** ADAM_STEP))
    v_hat = v * (1.0 / (1.0 - ADAM_B2 ** ADAM_STEP))
    delta = -ADAM_LR * (m_hat / (jnp.sqrt(v_hat) + ADAM_EPS) + ADAM_WD * w)
    return delta, m, v


def _fetch_partials(s_ref, got_ref, buf, sems):
    x, y, _ = _place()
    cps = [pltpu.make_async_copy(s_ref.at[2 * x + y], buf.at[0], sems.at[0])]
    cps += [pltpu.make_async_copy(got_ref.at[j], buf.at[1 + j], sems.at[1 + j]) for j in range(3)]
    for cp in cps:
        cp.start()
    for cp in cps:
        cp.wait()


SMALL_AT = dict(norm_w=0, fnw=8, bias=16, bn=24, sinks=32, loss=40)
ROW_AT = (R_IN, R_A, R_B, R_O)


def _finish_small(ws, ms, vs, smalls):
    names = ["norm_w", "fnw", "bias", "bn", "sinks"]
    widths = [ws[n].shape[1] for n in names]

    def body(*refs):
        w_refs, m_refs, v_refs = refs[0:5], refs[5:10], refs[10:15]
        smalls_ref, loss_ref = refs[15], refs[16]
        outs, tot = refs[17:37], refs[37]
        acc = smalls_ref[0]
        for d in range(1, NDEV):
            acc = acc + smalls_ref[d]
        tot[...] = acc
        loss_ref[...] = tot[SMALL_AT["loss"]:SMALL_AT["loss"] + 1, 0:1]
        for p, (nm_, wd) in enumerate(zip(names, widths)):
            r = SMALL_AT[nm_]
            g = tot[r:r + 1, 0:wd]
            d, nm, nv = _adamw_math(w_refs[p][...], g, m_refs[p][...], v_refs[p][...])
            for o, val in zip(outs[4 * p:4 * p + 4], (g, d, nm, nv)):
                o[...] = val

    res = pl.pallas_call(
        body, name="finish_small",
        in_specs=[_vmem()] * 16, out_specs=[_vmem()] * 21,
        out_shape=[jax.ShapeDtypeStruct((1, 1), F32)]
        + [jax.ShapeDtypeStruct((1, wd), F32) for wd in widths for _ in range(4)],
        scratch_shapes=[pltpu.VMEM((SMALL_ROWS, D), F32)],
        compiler_params=_cp(),
    )(*[ws[n] for n in names], *[ms[n] for n in names], *[vs[n] for n in names], smalls)
    return res[0], {n: tuple(res[1 + 4 * p:5 + 4 * p]) for p, n in enumerate(names)}


def _finish(w_rows, m_rows, v_rows, gu_w, gu_m, gu_v, sums, got):
    shapes = [(SHARD, 1, D)] + [w.shape for w in w_rows[1:]]

    row_block = 96

    def columns(ref, p, cols, r0, n):
        if p:
            return ref, (slice(r0, r0 + n), cols)
        flat = ref if ref.shape == (SHARD * LANE_TILES, LANE) else ref.reshape(SHARD * LANE_TILES, LANE)
        return flat, (pl.ds(cols.start // LANE + LANE_TILES * r0, n, stride=LANE_TILES), slice(None))

    def read(ref, p, cols, r0, n):
        ref, at = columns(ref, p, cols, r0, n)
        return ref[at]

    def body(*refs):
        wr_refs, mr_refs, vr_refs = refs[0:4], refs[4:8], refs[8:12]
        guw_ref, gum_ref, guv_ref = refs[12:15]
        s_refs, got_refs = refs[15:17], refs[17:19]
        row_outs = refs[19:35]
        gu_outs = refs[35:39]
        buf, gsh, sems, big, big_sems = refs[39:]
        loads = [pltpu.make_async_copy(r[0], big.at[k], big_sems.at[k]) for k, r in enumerate((wr_refs, mr_refs, vr_refs))]
        for cp in loads:
            cp.start()
        wr_refs, mr_refs, vr_refs = ((big.at[k],) + tuple(r[1:]) for k, r in enumerate((wr_refs, mr_refs, vr_refs)))
        x, y, c = _place()
        me_slot = 4 * x + 2 * y + c
        down = 2 * me_slot

        def total(rows, cols):
            g = buf[0, rows, cols].astype(F32)
            for j in range(1, 4):
                g = g + buf[j, rows, cols].astype(F32)
            return g

        def update(p, grad, cols):
            nrows = shapes[p][0]
            for r0 in range(0, nrows, row_block):
                n = min(row_block, nrows - r0)
                g = grad(r0, n)
                d, nm, nv = _adamw_math(read(wr_refs[p], p, cols, r0, n), g, read(mr_refs[p], p, cols, r0, n),
                                        read(vr_refs[p], p, cols, r0, n))
                for o, val in zip(row_outs[4 * p:4 * p + 4], (g, d, nm, nv)):
                    o, at = columns(o, p, cols, r0, n)
                    o[at] = val

        for hf in range(2):
            _fetch_partials(s_refs[hf], got_refs[hf], buf, sems)
            for cc in range(DH // LANE):
                src = slice(cc * LANE, (cc + 1) * LANE)
                cols = slice(hf * DH + cc * LANE, hf * DH + (cc + 1) * LANE)
                for r0 in range(0, SHARD_PAD, row_block):
                    rows = slice(r0, min(r0 + row_block, SHARD_PAD))
                    gsh[rows, :] = total(rows, src)
                if hf == 0 and cc == 0:
                    for cp in loads:
                        cp.wait()
                update(0, lambda r0, n: gsh[pl.ds(down + r0, n), :], cols)
                for p in range(1, 4):
                    update(p, lambda r0, n, p=p: total(slice(ROW_AT[p] + r0, ROW_AT[p] + r0 + n), src), cols)
            if hf == 0:
                g = total(slice(R_GU, R_GU + RANK), slice(0, 64))
                d, nm, nv = _adamw_math(guw_ref[...], g, gum_ref[...], guv_ref[...])
                for o, val in zip(gu_outs, (g, d, nm, nv)):
                    o[...] = val

    res = pl.pallas_call(
        body, name="finish",
        in_specs=([_any()] + [_vmem()] * 3) * 3 + [_vmem()] * 3 + [_any()] * 4,
        out_specs=[_vmem()] * 20,
        out_shape=[jax.ShapeDtypeStruct(s, F32) for s in shapes for _ in range(4)]
        + [jax.ShapeDtypeStruct((RANK, 64), F32)] * 4,
        scratch_shapes=[pltpu.VMEM((4, ROWS, DH), sums[0].dtype), pltpu.VMEM((SHARD_PAD, LANE), F32),
                        pltpu.SemaphoreType.DMA((4,)),
                        pltpu.VMEM((3, SHARD * LANE_TILES, LANE), F32), pltpu.SemaphoreType.DMA((3,))],
        compiler_params=_cp(),
    )(*w_rows, *m_rows, *v_rows, gu_w, gu_m, gu_v, *sums, *got)
    return tuple(res[0:16]), tuple(res[16:20])


def _place():
    x, y, c = lax.axis_index("x"), lax.axis_index("y"), lax.axis_index("c")
    return x, y, c


def _peers(x, y, c):
    return [(x ^ dx, y ^ dy, c ^ dc) for dx in range(2) for dy in range(2) for dc in range(2) if dx + dy + dc]


def _late_gather_start(blk, after, name="late_gather"):
    land = pltpu.with_memory_space_constraint(lax.empty((NDEV,) + blk.shape, blk.dtype), pltpu.HBM)

    def body(b_ref, land_ref, after_ref, send_sems, recv_sems, b_thru, land_thru, token):
        x, y, c = _place()
        for k, to in enumerate(_peers(x, y, c)):
            pltpu.make_async_remote_copy(
                src_ref=b_ref, dst_ref=land_ref.at[4 * x + 2 * y + c], send_sem=send_sems.at[k],
                recv_sem=recv_sems.at[k], device_id=to, device_id_type=MESH).start()
        token[...] = jnp.zeros_like(token)

    return pl.pallas_call(
        body, name=name + "_start",
        out_shape=(pltpu.SemaphoreType.DMA((7,)), pltpu.SemaphoreType.DMA((7,)),
                   pltpu.HBM(blk.shape, blk.dtype), pltpu.HBM(land.shape, land.dtype),
                   jax.ShapeDtypeStruct((8, LANE), F32)),
        in_specs=(_hbm(), _hbm(), _any()), out_specs=(_sem(), _sem(), _hbm(), _hbm(), _vmem()),
        input_output_aliases={0: 2, 1: 3},
        compiler_params=pltpu.CompilerParams(has_side_effects=_EFFECT),
    )(pltpu.with_memory_space_constraint(blk, pltpu.HBM), land, after)


def _late_gather_wait(send_sems, recv_sems, b_thru, land_thru, after, after2, name="late_gather"):
    def body(b_ref, land_ref, send_sems, recv_sems, after_ref, after2_ref, b_out, got_ref):
        x, y, c = _place()
        copies = [pltpu.make_async_remote_copy(
            src_ref=b_ref, dst_ref=land_ref.at[4 * x + 2 * y + c], send_sem=send_sems.at[k],
            recv_sem=recv_sems.at[k], device_id=to, device_id_type=MESH)
            for k, to in enumerate(_peers(x, y, c))]
        for cp in copies:
            cp.wait_send()
        for cp in copies:
            cp.wait_recv()

    return pl.pallas_call(
        body, name=name + "_wait",
        out_shape=(pltpu.HBM(b_thru.shape, b_thru.dtype), pltpu.HBM(land_thru.shape, land_thru.dtype)),
        in_specs=(_hbm(), _hbm(), _sem(), _sem(), _any(), _any()), out_specs=(_hbm(), _hbm()),
        input_output_aliases={0: 0, 1: 1},
        compiler_params=pltpu.CompilerParams(has_side_effects=_EFFECT),
    )(b_thru, land_thru, send_sems, recv_sems, after, after2)


G_ROWS = SHARD_PAD + RANK


def _gather_blocks(w_in_t, gu_s, xs, norm_w, pos_col):
    rows, cols = G_ROWS, D
    T = xs.shape[0]
    tT = min(T, 256)
    inv_row, sign_row = _rope_rows()

    def body(wi_ref, gu_ref, xs_hbm, nw_ref, pos_ref, inv_ref, sign_ref,
             out_ref, h_ref, cos_ref, sin_ref, x_ref, frame_ref, xs_ref, send_sems, recv_sems, local_sem, xs_sem):
        load_xs = pltpu.make_async_copy(xs_hbm, xs_ref, xs_sem)
        load_xs.start()
        x, y, c = _place()
        me, sibling = (x, y, c), (x, y, 1 - c)
        chips = [(1 - x, y), (x, 1 - y), (1 - x, 1 - y)]
        shift = 2 * (4 * x + 2 * y + c)
        frame_ref[SHARD - SHARD % 8:, :] = jnp.zeros((SHARD_PAD - SHARD + SHARD % 8, LANE), F32)
        for cc in range(LANE_TILES):
            cs = slice(cc * LANE, (cc + 1) * LANE)
            frame_ref[:SHARD, :] = wi_ref[pl.ds(cc, SHARD, stride=LANE_TILES), :]
            x_ref[0:SHARD_PAD, cs] = pltpu.roll(frame_ref[...], shift, 0).astype(x_ref.dtype)
        x_ref[SHARD_PAD:G_ROWS, :] = jnp.zeros((RANK, D), x_ref.dtype)
        x_ref[SHARD_PAD:G_ROWS, 0:64] = gu_ref[...].astype(x_ref.dtype)

        def slot(px, py, pc):
            return out_ref.at[4 * px + 2 * py + pc]

        def copy(k, block, to, src=None):
            return pltpu.make_async_remote_copy(
                src_ref=slot(*block) if src is None else src, dst_ref=slot(*block),
                send_sem=send_sems.at[k], recv_sem=recv_sems.at[k], device_id=to, device_id_type=MESH)

        mine = pltpu.make_async_copy(x_ref, slot(*me), local_sem)
        mine.start()
        first = [copy(0, me, sibling, src=x_ref)]
        first += [copy(1 + j, me, (*chip, c), src=x_ref) for j, chip in enumerate(chips)]
        for cp in first:
            cp.start()
        load_xs.wait()

        @pl.loop(0, T // tT)
        def _(i):
            rows_i = pl.ds(pl.multiple_of(i * tT, tT), tT)
            _prologue_rows(rows_i, xs_ref, nw_ref, pos_ref, inv_ref, sign_ref, h_ref, cos_ref, sin_ref)

        passed = [copy(4 + j, (*chip, c), sibling) for j, chip in enumerate(chips)]
        for j, chip in enumerate(chips):
            copy(1 + j, (*chip, c), me).wait_recv()
            passed[j].start()
        copy(0, sibling, me).wait_recv()
        for j, chip in enumerate(chips):
            copy(4 + j, (*chip, 1 - c), me).wait_recv()
        for cp in first + passed:
            cp.wait_send()
        mine.wait()

    return pl.pallas_call(
        body, name="gather_weights",
        in_specs=[_vmem(), _vmem(), _any()] + [_vmem()] * 4, out_specs=[_any()] + [_vmem()] * 3,
        out_shape=[jax.ShapeDtypeStruct((NDEV, rows, cols), WIRE), jax.ShapeDtypeStruct((T, D), MXU),
                   jax.ShapeDtypeStruct((T, LANE), F32), jax.ShapeDtypeStruct((T, LANE), F32)],
        scratch_shapes=[pltpu.VMEM((rows, cols), WIRE), pltpu.VMEM((SHARD_PAD, LANE), F32), pltpu.VMEM((T, D), F32),
                        pltpu.SemaphoreType.DMA((7,)), pltpu.SemaphoreType.DMA((7,)), pltpu.SemaphoreType.DMA,
                        pltpu.SemaphoreType.DMA],
        compiler_params=_cp(),
    )(w_in_t, gu_s, xs, norm_w, pos_col, inv_row, sign_row)


def _pair_reduce(gwt, tails, half):
    n = gwt.shape[1]
    starts = [SHARD_PAD]
    for t in tails:
        starts.append(starts[-1] + t.shape[1])
    rows = starts[-1]
    blk = (4, rows, n)
    npart = 1 + len(tails)

    def body(*refs):
        g_ref, t_refs = refs[0], refs[1:npart]
        out_ref, acc, got, own, send_sems, recv_sems, own_sems, out_sems = refs[npart:]
        x, y, c = _place()

        def parts(d, dst):
            frame = g_ref.at[pl.ds(pl.multiple_of(FRAME * d, 16), SHARD_PAD)]
            return [(frame, dst.at[0:SHARD_PAD])] + [
                (t_ref.at[d], dst.at[starts[k]:starts[k + 1]]) for k, t_ref in enumerate(t_refs)]

        sends, loads, stores = [], [], []
        for chip in range(4):
            sends.append([pltpu.make_async_remote_copy(
                src_ref=s, dst_ref=d_, send_sem=send_sems.at[chip, k], recv_sem=recv_sems.at[chip, k],
                device_id=(x, y, 1 - c), device_id_type=MESH)
                for k, (s, d_) in enumerate(parts(2 * chip + (1 - c), got.at[chip]))])
            loads.append([pltpu.make_async_copy(s, d_, own_sems.at[chip, k])
                          for k, (s, d_) in enumerate(parts(2 * chip + c, own.at[chip]))])
            stores.append(pltpu.make_async_copy(acc.at[chip], out_ref.at[chip], out_sems.at[chip]))
        for group in sends + loads:
            for cp in group:
                cp.start()
        for chip in range(4):
            for cp in loads[chip]:
                cp.wait()
            for cp in sends[chip]:
                cp.wait_recv()
            acc[chip] = (own[chip].astype(F32) + got[chip].astype(F32)).astype(acc.dtype)
            stores[chip].start()
        for cp in stores:
            cp.wait()
        for group in sends:
            for cp in group:
                cp.wait_send()

    return pl.pallas_call(
        body, name=f"pair_reduce{half}",
        in_specs=[_any()] * npart, out_specs=_any(),
        out_shape=jax.ShapeDtypeStruct(blk, gwt.dtype),
        scratch_shapes=[pltpu.VMEM(blk, gwt.dtype), pltpu.VMEM(blk, gwt.dtype), pltpu.VMEM(blk, gwt.dtype),
                        pltpu.SemaphoreType.DMA((4, npart)), pltpu.SemaphoreType.DMA((4, npart)),
                        pltpu.SemaphoreType.DMA((4, npart)), pltpu.SemaphoreType.DMA((4,))],
        compiler_params=_cp(),
    )(gwt, *tails)


def _pad_cols(a, cols):
    return jnp.pad(a, ((0, 0), (0, cols - a.shape[1])))


def _pad_rows(a, rows):
    return jnp.pad(a, ((0, rows - a.shape[0]), (0, 0)))


FRAME = 928


def _wft_plan():
    moves = []
    for blk in range(8):
        for half in range(2):
            for sub in range(2):
                moves.append((C_Q + 128 * blk + 32 * (2 * half + sub), 128 * blk + 32 * (2 * sub + half), 32))
    for idx in range(4):
        for dup in range(2):
            moves.append((C_KD + 64 * idx + 32 * dup, 1024 + 32 * idx, 32))
    for g in range(2):
        for dup in range(2):
            moves.append((C_VD + 128 * g + 64 * dup, 1152 + 64 * g, 64))
    moves += [(C_BL, 5376, RANK), (C_BV, 3328, 1024), (C_BQ, 2304, 512), (C_BK, 2816, 512),
              (C_AG, 1280, 1024), (C_BG, 4352, 1024), (C_MA, 5392, 1024), (C_MB, 6416, 1024)]
    bulk, seams = [], []
    for dst, src, n in moves:
        r = src
        while r < src + n:
            f = min(r // FRAME, NDEV - 1)
            local = r - FRAME * f
            if f > 0 and local < 16:
                assert local == 0
                seams.append((f, dst + r - src))
                step = 16
            else:
                step = min(src + n, FRAME * (f + 1) if f < NDEV - 1 else IN_WIDTH) - r
                bulk.append((f, local, dst + r - src, step))
            r += step
    assert sorted(f for f, _ in seams) == list(range(1, NDEV))
    return bulk, seams, [(C_BL + RANK, C_GLA - C_BL - RANK)]


def _build_wft_copies(frames):
    bulk, seams, zeros = _wft_plan()
    (z0, zn), = zeros

    def body(f_ref, o_ref, edge, sems, esems):
        copies = [pltpu.make_async_copy(f_ref.at[f, pl.ds(l0, n)], o_ref.at[pl.ds(dst, n)], sems.at[i])
                  for i, (f, l0, dst, n) in enumerate(bulk)]
        loads = []
        for i, (f, _) in enumerate(seams):
            loads.append(pltpu.make_async_copy(f_ref.at[f, pl.ds(0, 16)], edge.at[i, 0], esems.at[i, 0]))
            loads.append(pltpu.make_async_copy(f_ref.at[f - 1, pl.ds(FRAME, 16)], edge.at[i, 1], esems.at[i, 1]))
        for cp in copies + loads:
            cp.start()
        o_ref[z0:z0 + zn, :] = jnp.zeros((zn, D), o_ref.dtype)
        for cp in loads:
            cp.wait()
        for i, (_, dst) in enumerate(seams):
            o_ref[dst:dst + 16, :] = edge[i, 0] + edge[i, 1]
        for cp in copies:
            cp.wait()

    return pl.pallas_call(
        body, name="build_wft",
        in_specs=[_any()], out_specs=_vmem(),
        out_shape=jax.ShapeDtypeStruct((NF, D), frames.dtype),
        scratch_shapes=[pltpu.VMEM((len(seams), 2, 16, D), frames.dtype),
                        pltpu.SemaphoreType.DMA((len(bulk),)), pltpu.SemaphoreType.DMA((len(seams), 2))],
        compiler_params=_cp(),
    )(frames)


def kernel(x, positions, norm_w, w_in, a_sinks, b_gate_up, b_gate_bias, b_out_norm_w, w_a_proj, w_b_proj, w_out, final_norm_w, loss_target, m_norm_w, m_w_in, m_a_sinks, m_b_gate_up, m_b_gate_bias, m_b_out_norm_w, m_w_a_proj, m_w_b_proj, m_w_out, m_final_norm_w, v_norm_w, v_w_in, v_a_sinks, v_b_gate_up, v_b_gate_bias, v_b_out_norm_w, v_w_a_proj, v_w_b_proj, v_w_out, v_final_norm_w):
    T = x.shape[1]
    xs, target = x[0], loss_target[0]
    fnw = final_norm_w.reshape(1, D)
    me = 4 * lax.axis_index("x") + 2 * lax.axis_index("y") + lax.axis_index("c")
    allw, h, cos, sin = _gather_blocks(_by_lane_tile(w_in), b_gate_up[0], xs, norm_w, positions.reshape(T, 1))
    late_blk = jnp.concatenate([w_a_proj[0], w_b_proj[0], w_out[0]], axis=0).astype(WIRE)
    l_send, l_recv, l_blk, l_land, l_started = _late_gather_start(late_blk, cos)
    wf = _build_wft_copies(allw)
    gu = allw[:, SHARD_PAD:G_ROWS, :64].transpose(1, 0, 2).reshape(RANK, 512)
    gu_pad = _pad_rows(gu, W_BL)

    proj = _proj(h, wf, l_started)
    o_a, lse = _swa_fwd(proj, cos, sin, a_sinks)
    o_b, states = _gla_fwd(proj, gu_pad, b_gate_bias)
    l_blk, l_land = _late_gather_wait(l_send, l_recv, l_blk, l_land, states, lse)
    late = lax.dynamic_update_slice(l_land, l_blk[None], (me, 0, 0))
    (dx2, do_a, do_b, d_gates, g_late0, g_late1, g_fn, g_bn, loss_part) = _mid(
        xs, target, proj, o_a, o_b, late, jnp.tile(b_out_norm_w, (1, B_HEADS)), fnw)
    d_q, d_kv, g_sinks = _swa_bwd(proj, cos, sin, a_sinks, do_a, o_a, lse, cos)
    d_gla, d_bl, g_gu, g_bias = _gla_bwd(proj, gu_pad, b_gate_bias, states, do_b)
    pieces = [d_q, d_kv, d_bl, d_gla, d_gates]
    offsets = [C_Q, C_KD, C_BL, C_GLA, C_GATES]

    ggu = g_gu[:RANK].reshape(RANK, NDEV, 64).transpose(1, 0, 2)
    ggu_half = [jnp.pad(ggu, ((0, 0), (0, 0), (0, DH - 64))).astype(WIRE), jnp.zeros((NDEV, RANK, DH), WIRE)]
    tails = [[g_late0, ggu_half[0]], [g_late1, ggu_half[1]]]

    send0, recv0, s_thru0, land0, started0 = _chip_start(
        _pair_reduce(_gw_half(h, pieces, 0, after=g_bias), tails[0], 0), 0)
    send1, recv1, s_thru1, land1, started1 = _chip_start(
        _pair_reduce(_gw_half(h, pieces, 1, after=started0), tails[1], 1), 1)
    grad_x, g_nw = _dh_norm(pieces, offsets, wf, xs, dx2, norm_w, started1)
    small = jnp.concatenate([g_nw, g_fn, _pad_cols(g_bias, D), _pad_cols(g_bn, D), _pad_cols(g_sinks, D),
                             _pad_cols(loss_part, D)], axis=0)
    sm_send, sm_recv, sm_blk, sm_land, sm_started = _late_gather_start(small, g_nw, name="small_gather")
    sums0, got0 = _chip_wait(send0, recv0, s_thru0, land0, sm_started, 0)
    sums1, got1 = _chip_wait(send1, recv1, s_thru1, land1, got0, 1)
    sums, from_chips = [sums0, sums1], [got0, got1]

    ws = dict(norm_w=norm_w, fnw=fnw, bias=b_gate_bias, bn=b_out_norm_w, sinks=a_sinks)
    ms = dict(norm_w=m_norm_w, fnw=m_final_norm_w.reshape(1, D), bias=m_b_gate_bias, bn=m_b_out_norm_w,
              sinks=m_a_sinks)
    vs = dict(norm_w=v_norm_w, fnw=v_final_norm_w.reshape(1, D), bias=v_b_gate_bias, bn=v_b_out_norm_w,
              sinks=v_a_sinks)
    t_rows, t_gu = _finish(
        [_by_lane_tile(w_in), w_a_proj[0], w_b_proj[0], w_out[0]],
        [_by_lane_tile(m_w_in), m_w_a_proj[0], m_w_b_proj[0], m_w_out[0]],
        [_by_lane_tile(v_w_in), v_w_a_proj[0], v_w_b_proj[0], v_w_out[0]],
        b_gate_up[0], m_b_gate_up[0], v_b_gate_up[0], sums, from_chips)
    sm_blk, sm_land = _late_gather_wait(sm_send, sm_recv, sm_blk, sm_land, t_rows[0], t_gu[0], name="small_gather")
    loss, sm = _finish_small(ws, ms, vs, lax.dynamic_update_slice(sm_land, sm_blk[None], (me, 0, 0)))

    def outputs(k):
        return [sm["norm_w"][k], jnp.transpose(t_rows[k], (1, 2, 0)), sm["sinks"][k], t_gu[k][None], sm["bias"][k], sm["bn"][k],
                t_rows[4 + k][None], t_rows[8 + k][None], t_rows[12 + k][None], sm["fnw"][k].reshape(D)]

    return (loss[0, 0], grad_x[None], *outputs(0), *outputs(1), *outputs(2), *outputs(3))
```

```python
import functools

import numpy as np
import jax
import jax.numpy as jnp
from jax import lax
from jax.experimental import pallas as pl
from jax.experimental.pallas import tpu as pltpu

F32 = jnp.float32
MXU = jnp.bfloat16
WIRE = jnp.bfloat16

D = 1024
A_HEADS, A_KV, A_HD = 16, 2, 64
BLK = 128
B_HEADS, B_DK, B_DV = 4, 128, 256
RANK, TAU, CHUNK = 16, 16.0, 64
EPS, NEG = 1e-5, -1e30
ROPE_THETA = 10000.0
IN_WIDTH, NDEV = 7440, 8
SHARD = IN_WIDTH // NDEV
LANE = 128
LANE_TILES = D // LANE


def _by_lane_tile(a):
    return jnp.transpose(a, (2, 0, 1)).reshape(SHARD * LANE_TILES, LANE)


C_Q, C_KD, C_VD, C_BL = 0, 1024, 1280, 1536
C_BV, C_BQ, C_BK = 2048, 3072, 3584
C_AG, C_BG, C_MA, C_MB = 4096, 5120, 6144, 7168
C_GLA, W_GLA, C_GATES, W_GATES = 2048, 2048, 4096, 4096
NF = 8192
W_BL = 128

SHARD_PAD = 944
R_IN, R_A, R_B, R_O, R_GU, ROWS = 0, 944, 1072, 1200, 1328, 1344
SMALL_ROWS = 48

ADAM_LR, ADAM_B1, ADAM_B2, ADAM_EPS, ADAM_WD, ADAM_STEP = 0.001, 0.9, 0.999, 1e-08, 0.01, 10

MESH = pl.DeviceIdType.MESH
VMEM_LIMIT = 56 * 1024 * 1024


def _cp(sem=None, **kw):
    if sem is not None:
        kw["dimension_semantics"] = sem
    return pltpu.CompilerParams(vmem_limit_bytes=VMEM_LIMIT, **kw)


def _dot(a, b):
    return jnp.dot(a, b, preferred_element_type=F32)


def _dot_nt(a, b):
    return lax.dot_general(a, b, (((1,), (1,)), ((), ())), preferred_element_type=F32)


def _dot_tn(a, b):
    return lax.dot_general(a, b, (((0,), (0,)), ((), ())), preferred_element_type=F32)


def _dot_f32(a, b):
    return jnp.dot(a, b, preferred_element_type=F32, precision=lax.Precision.HIGHEST)


def _sigmoid(z):
    return 0.5 * jnp.tanh(0.5 * z) + 0.5


def _rope(xp, cos, sin):
    return xp * cos + pltpu.roll(xp, 64, 1) * sin


def _rope_bwd(dy, cos, sin):
    return dy * cos - pltpu.roll(dy, 64, 1) * sin


def _vmem():
    return pl.BlockSpec(memory_space=pltpu.VMEM)


def _any():
    return pl.BlockSpec(memory_space=pl.ANY)


def _rope_rows():
    half = A_HD // 2
    inv = (np.float32(ROPE_THETA) ** (-np.arange(half, dtype=np.float32) / np.float32(half))).astype(np.float32)
    inv_row = jnp.asarray(np.tile(inv, 4)[None, :])
    sign_row = jnp.asarray(np.concatenate([-np.ones(64, np.float32), np.ones(64, np.float32)])[None, :])
    return inv_row, sign_row


def _prologue_rows(rows, x_ref, nw_ref, pos_ref, inv_ref, sign_ref, h_ref, cos_ref, sin_ref):
    xv = x_ref[rows, :]
    r = lax.rsqrt(jnp.mean(xv * xv, axis=-1, keepdims=True) + EPS)
    h_ref[rows, :] = ((xv * r) * nw_ref[...]).astype(h_ref.dtype)
    ang = pos_ref[rows, :].astype(F32) * inv_ref[...]
    cos_ref[rows, :] = jnp.cos(ang)
    sin_ref[rows, :] = jnp.sin(ang) * sign_ref[...]


def _proj(h, wft, after):
    T = h.shape[0]
    tT, tN = T, 512

    def body(h_ref, w_ref, after_ref, o_ref):
        o_ref[...] = _dot_nt(h_ref[...], w_ref[...])

    return pl.pallas_call(
        body, name="proj", grid=(T // tT, NF // tN),
        in_specs=[pl.BlockSpec((tT, D), lambda i, j: (i, 0)), pl.BlockSpec((tN, D), lambda i, j: (j, 0)), _any()],
        out_specs=pl.BlockSpec((tT, tN), lambda i, j: (i, j)),
        out_shape=jax.ShapeDtypeStruct((T, NF), F32),
        compiler_params=_cp(("parallel", "parallel")),
    )(h, wft, after)


def _swa_masks():
    lane = lax.broadcasted_iota(jnp.int32, (BLK, LANE), 1)
    rope_sub0 = ((lane // 32) % 2) == 0
    std_sub0 = lane < 64
    return lane, rope_sub0, std_sub0


def _swa_tri():
    qi = lax.broadcasted_iota(jnp.int32, (BLK, BLK), 0)
    kj = lax.broadcasted_iota(jnp.int32, (BLK, BLK), 1)
    return kj <= qi


def _swa_fold(full, tri):
    return jnp.where(tri, full[:, BLK:], full[:, :BLK])


def _swa_unfold(sq, tri):
    return jnp.concatenate([jnp.where(tri, 0.0, sq), jnp.where(tri, sq, 0.0)], axis=1)


def _swa_keys(kc_ref, kp_ref, vc_ref, vp_ref, cq, sq, cp, sp):
    def ropek(kref, c, s):
        kv = kref[...]
        return jnp.concatenate([_rope(kv[:, :LANE], c, s), _rope(kv[:, LANE:], c, s)], axis=1)

    K = jnp.concatenate([ropek(kp_ref, cp, sp), ropek(kc_ref, cq, sq)], axis=0).astype(MXU)
    V = jnp.concatenate([vp_ref[...], vc_ref[...]], axis=0).astype(MXU)
    return K, V


def _swa_in_specs(nb, last):
    def cur(n):
        return jnp.minimum(n, last)

    def prev(n):
        return jnp.maximum(cur(n) - 1, 0)

    kd, vd = C_KD // 256, C_VD // 256
    return [
        pl.BlockSpec((BLK, D), lambda n: (cur(n), C_Q // D)),
        pl.BlockSpec((BLK, 256), lambda n: (cur(n), kd)),
        pl.BlockSpec((BLK, 256), lambda n: (prev(n), kd)),
        pl.BlockSpec((BLK, 256), lambda n: (cur(n), vd)),
        pl.BlockSpec((BLK, 256), lambda n: (prev(n), vd)),
        pl.BlockSpec((BLK, LANE), lambda n: (cur(n), 0)),
        pl.BlockSpec((BLK, LANE), lambda n: (cur(n), 0)),
        pl.BlockSpec((BLK, LANE), lambda n: (prev(n), 0)),
        pl.BlockSpec((BLK, LANE), lambda n: (prev(n), 0)),
    ]


def _swa_fwd(proj, cos, sin, sinks):
    T = proj.shape[0]
    nb = T // BLK
    scale = A_HD ** -0.5

    def body(sinks_ref, q_ref, kc_ref, kp_ref, vc_ref, vp_ref, cq_ref, sq_ref, cp_ref, sp_ref, o_ref, l_ref):
        n = pl.program_id(0)
        cq, sq = cq_ref[...], sq_ref[...]
        K, V = _swa_keys(kc_ref, kp_ref, vc_ref, vp_ref, cq, sq, cp_ref[...], sp_ref[...])
        tri = _swa_tri()
        valid = tri | (n > 0)
        lane, rope_sub0, std_sub0 = _swa_masks()
        group = A_HEADS // A_KV
        roped, lses = {}, []

        def products(head):
            pb, sub, g = head // 2, head % 2, head // group
            if sub == 0:
                roped[pb] = _rope(q_ref[:, pb * LANE:(pb + 1) * LANE], cq, sq)
            qm = jnp.where(rope_sub0 if sub == 0 else ~rope_sub0, roped[pb], 0.0).astype(MXU)
            return _dot_nt(qm, K[:, g * LANE:(g + 1) * LANE])

        def softmax(head, s_full):
            s = jnp.where(valid, _swa_fold(s_full, tri) * scale, NEG)
            sink = sinks_ref[0, head]
            m = jnp.maximum(jnp.max(s, axis=1, keepdims=True), sink)
            e = jnp.exp(s - m)
            den = jnp.sum(e, axis=1, keepdims=True) + jnp.exp(sink - m)
            lses.append(m + jnp.log(den))
            return _swa_unfold(e / den, tri).astype(MXU)

        outs = {}
        st1 = {0: products(0), 1: products(1)}
        st2 = {0: softmax(0, st1.pop(0))}
        for head in range(A_HEADS):
            if head + 2 < A_HEADS:
                st1[head + 2] = products(head + 2)
            if head + 1 < A_HEADS:
                st2[head + 1] = softmax(head + 1, st1.pop(head + 1))
            g = head // group
            outs[head] = _dot(st2.pop(head), V[:, g * LANE:(g + 1) * LANE])
            if head % 2 == 1:
                pb = head // 2
                o_ref[:, pb * LANE:(pb + 1) * LANE] = jnp.where(std_sub0, outs[head - 1], outs[head])
        lacc = jnp.zeros((BLK, LANE), F32)
        for head in range(A_HEADS):
            lacc = jnp.where(lane == head, lses[head], lacc)
        l_ref[...] = lacc

    return pl.pallas_call(
        body, name="swa_fwd", grid=(nb,),
        in_specs=[pl.BlockSpec(memory_space=pltpu.SMEM)] + _swa_in_specs(nb, nb - 1),
        out_specs=[pl.BlockSpec((BLK, D), lambda n: (n, 0)), pl.BlockSpec((BLK, LANE), lambda n: (n, 0))],
        out_shape=[jax.ShapeDtypeStruct((T, D), F32), jax.ShapeDtypeStruct((T, LANE), F32)],
        compiler_params=_cp(("parallel",)),
    )(sinks, proj, proj, proj, proj, proj, cos, sin, cos, sin)


def _swa_bwd(proj, cos, sin, sinks, do_a, o_a, lse, after):
    T = proj.shape[0]
    nb = T // BLK
    scale = A_HD ** -0.5

    def body(sinks_ref, q_ref, kc_ref, kp_ref, vc_ref, vp_ref, cq_ref, sq_ref, cp_ref, sp_ref,
             do_ref, o_ref, l_ref, after_ref, dq_ref, dkv_ref, ds_ref, ckv_ref):
        n = pl.program_id(0)

        @pl.when(n == 0)
        def _():
            ckv_ref[...] = jnp.zeros_like(ckv_ref)
            ds_ref[...] = jnp.zeros_like(ds_ref)

        @pl.when(n < nb)
        def _():
            cq, sq, cp, sp = cq_ref[...], sq_ref[...], cp_ref[...], sp_ref[...]
            K, V = _swa_keys(kc_ref, kp_ref, vc_ref, vp_ref, cq, sq, cp, sp)
            tri = _swa_tri()
            valid = tri | (n > 0)
            lane, rope_sub0, std_sub0 = _swa_masks()
            lane_row = lax.broadcasted_iota(jnp.int32, (1, LANE), 1)
            lse_v = l_ref[...]
            dKt = [jnp.zeros((LANE, 2 * BLK), F32) for _ in range(A_KV)]
            dVt = [jnp.zeros((LANE, 2 * BLK), F32) for _ in range(A_KV)]
            dsinks, roped, roped_t, do_t = [], {}, {}, {}
            group = A_HEADS // A_KV
            dim = lax.broadcasted_iota(jnp.int32, (LANE, BLK), 0)
            rope_row0, std_row0 = ((dim // 32) % 2) == 0, dim < 64

            def products(head):
                pb, sub, g = head // 2, head % 2, head // group
                cols = slice(pb * LANE, (pb + 1) * LANE)
                Kg, Vg = K[:, g * LANE:(g + 1) * LANE], V[:, g * LANE:(g + 1) * LANE]
                if sub == 0:
                    roped[pb] = _rope(q_ref[:, cols], cq, sq)
                    roped_t[pb] = roped[pb].T
                    do_t[pb] = do_ref[:, cols].T
                qm = jnp.where(rope_sub0 if sub == 0 else ~rope_sub0, roped[pb], 0.0).astype(MXU)
                qmt = jnp.where(rope_row0 if sub == 0 else ~rope_row0, roped_t[pb], 0.0).astype(MXU)
                dov = jnp.where(std_sub0 if sub == 0 else ~std_sub0, do_ref[:, cols], 0.0)
                dovt = jnp.where(std_row0 if sub == 0 else ~std_row0, do_t[pb], 0.0).astype(MXU)
                delta = jnp.sum(dov * o_ref[:, cols], axis=1, keepdims=True)
                return qmt, dovt, delta, _dot_nt(qm, Kg), _dot_nt(dov.astype(MXU), Vg)

            def scores(head, qmt, dovt, delta, s_full, dp_full):
                lh = jnp.sum(jnp.where(lane == head, lse_v, 0.0), axis=1, keepdims=True)
                p = jnp.where(valid, jnp.exp(_swa_fold(s_full, tri) * scale - lh), 0.0)
                psink = jnp.exp(sinks_ref[0, head] - lh)
                dsinks.append(jnp.sum(-psink * delta, axis=0, keepdims=True))
                dsq = (p * (_swa_fold(dp_full, tri) - delta)) * scale
                return qmt, dovt, _swa_unfold(p, tri).astype(MXU), _swa_unfold(dsq, tri).astype(MXU)

            def grads(head, qmt, dovt, pb16, dsc):
                g = head // group
                dKt[g] = dKt[g] + _dot(qmt, dsc)
                dVt[g] = dVt[g] + _dot(dovt, pb16)
                return _dot(dsc, K[:, g * LANE:(g + 1) * LANE])

            dqs = {}
            st1 = {0: products(0), 1: products(1)}
            st2 = {0: scores(0, *st1.pop(0))}
            for head in range(A_HEADS):
                if head + 2 < A_HEADS:
                    st1[head + 2] = products(head + 2)
                if head + 1 < A_HEADS:
                    st2[head + 1] = scores(head + 1, *st1.pop(head + 1))
                dqs[head] = grads(head, *st2.pop(head))
                if head % 2 == 1:
                    pb = head // 2
                    dqp = jnp.where(rope_sub0, dqs[head - 1], dqs[head])
                    dq_ref[:, pb * LANE:(pb + 1) * LANE] = _rope_bwd(dqp, cq, sq).astype(dq_ref.dtype)
            dsink = jnp.zeros((1, LANE), F32)
            for head in range(A_HEADS):
                dsink = jnp.where(lane_row == head, dsinks[head], dsink)
            dK, dV = [a.T for a in dKt], [a.T for a in dVt]
            prev = ([_rope_bwd(dK[g][:BLK], cp, sp) for g in range(A_KV)] + [dV[g][:BLK] for g in range(A_KV)])
            cur_ = ([_rope_bwd(dK[g][BLK:], cq, sq) for g in range(A_KV)] + [dV[g][BLK:] for g in range(A_KV)])
            dkv_ref[...] = (ckv_ref[...] + jnp.concatenate(prev, axis=1)).astype(dkv_ref.dtype)
            ckv_ref[...] = jnp.concatenate(cur_, axis=1)
            ds_ref[...] = ds_ref[...] + jnp.broadcast_to(dsink, ds_ref.shape)

        @pl.when(n == nb)
        def _():
            dkv_ref[...] = ckv_ref[...].astype(dkv_ref.dtype)

    last = nb - 1

    def cur(n):
        return jnp.minimum(n, last)

    def out_kv(n):
        return (jnp.maximum(n - 1, 0), 0)

    return pl.pallas_call(
        body, name="swa_bwd", grid=(nb + 1,),
        in_specs=[pl.BlockSpec(memory_space=pltpu.SMEM)] + _swa_in_specs(nb, last) + [
            pl.BlockSpec((BLK, D), lambda n: (cur(n), 0)),
            pl.BlockSpec((BLK, D), lambda n: (cur(n), 0)),
            pl.BlockSpec((BLK, LANE), lambda n: (cur(n), 0)),
            _any(),
        ],
        out_specs=[
            pl.BlockSpec((BLK, D), lambda n: (cur(n), 0)),
            pl.BlockSpec((BLK, 512), out_kv),
            pl.BlockSpec((8, LANE), lambda n: (0, 0)),
        ],
        out_shape=[
            jax.ShapeDtypeStruct((T, D), MXU),
            jax.ShapeDtypeStruct((T, 512), MXU),
            jax.ShapeDtypeStruct((8, LANE), F32),
        ],
        scratch_shapes=[pltpu.VMEM((BLK, 512), F32)],
        compiler_params=_cp(("arbitrary",)),
    )(sinks, proj, proj, proj, proj, proj, cos, sin, cos, sin, do_a, o_a, lse, after)


NCH = 4
GSTEP = NCH * CHUNK
ST_ROWS = B_HEADS * B_DV


def _chunk_rows(c):
    return slice(c * CHUNK, (c + 1) * CHUNK)


def _per_chunk(which, vals):
    out = vals[-1]
    for c in range(NCH - 2, -1, -1):
        out = jnp.where(which == c, vals[c], out)
    return out


def _gla_gate(bl_ref, gu_ref, bias_ref):
    gk = _dot(bl_ref[...].astype(MXU), gu_ref[...]) + bias_ref[...]
    la = (jnp.minimum(gk, 0.0) - jnp.log(1.0 + jnp.exp(-jnp.abs(gk)))) / TAU
    ri = lax.broadcasted_iota(jnp.int32, (GSTEP, GSTEP), 0)
    ci = lax.broadcasted_iota(jnp.int32, (GSTEP, GSTEP), 1)
    same = (ri // CHUNK) == (ci // CHUNK)
    lower, upper = same & (ci <= ri), same & (ci >= ri)
    b = _dot_f32(jnp.where(lower, 1.0, 0.0).astype(F32), la)
    which = lax.broadcasted_iota(jnp.int32, (GSTEP, 1), 0) // CHUNK
    return gk, la, b, lower, upper, which


def _gla_head(q_ref, k_ref, la, b, which, h):
    sl = slice(h * B_DK, (h + 1) * B_DK)
    bh, lah = b[:, sl], la[:, sl]
    bls = [jnp.sum(lah[_chunk_rows(c)], axis=0, keepdims=True) for c in range(NCH)]
    blast = _per_chunk(which, bls)
    qc = q_ref[:, sl] * (B_DK ** -0.5)
    kh = k_ref[:, sl]
    eb, enb, esb = jnp.exp(bh), jnp.exp(-bh), jnp.exp(blast - bh)
    return qc * eb, kh * enb, kh * esb, eb, enb, esb, [jnp.exp(v) for v in bls]


def _gla_specs(step_of):
    return [
        pl.BlockSpec((GSTEP, 512), lambda i: (step_of(i), C_BQ // 512)),
        pl.BlockSpec((GSTEP, 512), lambda i: (step_of(i), C_BK // 512)),
        pl.BlockSpec((GSTEP, D), lambda i: (step_of(i), C_BV // D)),
        pl.BlockSpec((GSTEP, W_BL), lambda i: (step_of(i), C_BL // W_BL)),
        pl.BlockSpec((W_BL, 512), lambda i: (0, 0)),
        pl.BlockSpec((1, 512), lambda i: (0, 0)),
    ]


def _gla_fwd(proj, gu_pad, bias):
    T = proj.shape[0]
    ns = T // GSTEP

    def body(q_ref, k_ref, v_ref, bl_ref, gu_ref, bias_ref, o_ref, st_ref, state_ref):
        @pl.when(pl.program_id(0) == 0)
        def _():
            state_ref[...] = jnp.zeros_like(state_ref)

        _, la, b, lower, _, which = _gla_gate(bl_ref, gu_ref, bias_ref)

        def within(h):
            q_e, k_e, k_s, _, _, _, decays = _gla_head(q_ref, k_ref, la, b, which, h)
            vh = v_ref[:, h * B_DV:(h + 1) * B_DV].astype(MXU)
            q_eb = q_e.astype(MXU)
            att = jnp.where(lower, _dot_nt(q_eb, k_e.astype(MXU)), 0.0)
            return vh, q_eb, k_s.astype(MXU), _dot(att.astype(MXU), vh), decays

        def across(h, vh, q_eb, k_sb, o_intra, decays):
            rows = slice(h * B_DV, (h + 1) * B_DV)
            s = state_ref[rows, :]
            outs = []
            for c in range(NCH):
                cr = _chunk_rows(c)
                st_ref[c * ST_ROWS + h * B_DV:c * ST_ROWS + (h + 1) * B_DV, :] = s
                outs.append(o_intra[cr] + _dot_nt(q_eb[cr], s.astype(MXU)))
                s = s * decays[c] + _dot_tn(vh[cr], k_sb[cr])
            state_ref[rows, :] = s
            o_ref[:, rows] = jnp.concatenate(outs, axis=0)

        for h in range(B_HEADS):
            across(h, *within(h))

    return pl.pallas_call(
        body, name="gla_fwd", grid=(ns,),
        in_specs=_gla_specs(lambda i: i),
        out_specs=[pl.BlockSpec((GSTEP, D), lambda i: (i, 0)),
                   pl.BlockSpec((NCH * ST_ROWS, B_DK), lambda i: (i, 0))],
        out_shape=[jax.ShapeDtypeStruct((T, D), F32),
                   jax.ShapeDtypeStruct((ns * NCH * ST_ROWS, B_DK), F32)],
        scratch_shapes=[pltpu.VMEM((ST_ROWS, B_DK), F32)],
        compiler_params=_cp(("arbitrary",)),
    )(proj, proj, proj, proj, gu_pad, bias)


def _gla_bwd(proj, gu_pad, bias, states, do_b):
    T = proj.shape[0]
    ns = T // GSTEP
    o_q, o_k = C_BQ - C_GLA, C_BK - C_GLA

    def body(q_ref, k_ref, v_ref, bl_ref, gu_ref, bias_ref, st_ref, do_ref,
             dg_ref, dbl_ref, ggu_ref, gbias_ref, gt_ref):
        @pl.when(pl.program_id(0) == 0)
        def _():
            gt_ref[...] = jnp.zeros_like(gt_ref)
            ggu_ref[...] = jnp.zeros_like(ggu_ref)
            gbias_ref[...] = jnp.zeros_like(gbias_ref)

        gk, la, b, lower, upper_mask, which = _gla_gate(bl_ref, gu_ref, bias_ref)
        upper = jnp.where(upper_mask, 1.0, 0.0).astype(F32)
        dla_parts = []

        def within(h):
            q_e, k_e, k_s, eb, enb, esb, decays = _gla_head(q_ref, k_ref, la, b, which, h)
            vh = v_ref[:, h * B_DV:(h + 1) * B_DV].astype(MXU)
            doh = do_ref[:, h * B_DV:(h + 1) * B_DV].astype(MXU)
            q_eb, k_eb = q_e.astype(MXU), k_e.astype(MXU)
            att = jnp.where(lower, _dot_nt(q_eb, k_eb), 0.0).astype(MXU)
            datt = jnp.where(lower, _dot_nt(doh, vh), 0.0).astype(MXU)
            return (q_e, k_e, k_s, eb, enb, esb, decays, vh, doh, q_eb, k_s.astype(MXU),
                    _dot(datt, k_eb), _dot_tn(datt, q_eb), _dot_tn(att, doh))

        def across(h, q_e, k_e, k_s, eb, enb, esb, decays, vh, doh, q_eb, k_sb, dq_i, dk_e, dv_i):
            rows = slice(h * B_DV, (h + 1) * B_DV)
            g = gt_ref[rows, :]
            dq_c, dks_c, dv_c, ddec = [None] * NCH, [None] * NCH, [None] * NCH, [None] * NCH
            for c in range(NCH - 1, -1, -1):
                cr = _chunk_rows(c)
                s = st_ref[c * ST_ROWS + h * B_DV:c * ST_ROWS + (h + 1) * B_DV, :]
                gb = g.astype(MXU)
                dq_c[c] = dq_i[cr] + _dot(doh[cr], s.astype(MXU))
                dks_c[c] = _dot(vh[cr], gb)
                dv_c[c] = dv_i[cr] + _dot_nt(k_sb[cr], gb)
                ddec[c] = jnp.sum(g * s, axis=0, keepdims=True)
                g = g * decays[c] + _dot_tn(doh[cr], q_eb[cr])
            gt_ref[rows, :] = g
            dq_e = jnp.concatenate(dq_c, axis=0)
            dk_s = jnp.concatenate(dks_c, axis=0)
            dg_ref[:, rows] = jnp.concatenate(dv_c, axis=0).astype(dg_ref.dtype)
            dg_ref[:, o_q + h * B_DK:o_q + (h + 1) * B_DK] = (dq_e * eb * (B_DK ** -0.5)).astype(dg_ref.dtype)
            dg_ref[:, o_k + h * B_DK:o_k + (h + 1) * B_DK] = (dk_e * enb + dk_s * esb).astype(dg_ref.dtype)
            dks_ks = dk_s * k_s
            db = dq_e * q_e - dk_e * k_e - dks_ks
            dbl = [jnp.sum(dks_ks[_chunk_rows(c)], axis=0, keepdims=True) + ddec[c] * decays[c] for c in range(NCH)]
            dla_parts.append(_dot_f32(upper, db) + _per_chunk(which, dbl))

        for h in range(B_HEADS):
            across(h, *within(h))
        dla = jnp.concatenate(dla_parts, axis=1)
        dgk = dla * (1.0 / TAU) * _sigmoid(-gk)
        dgkb = dgk.astype(MXU)
        dbl_ref[...] = _dot_nt(dgkb, gu_ref[...]).astype(dbl_ref.dtype)
        ggu_ref[...] = ggu_ref[...] + _dot_tn(bl_ref[...].astype(MXU), dgkb)
        gbias_ref[...] = gbias_ref[...] + jnp.broadcast_to(jnp.sum(dgk, axis=0, keepdims=True), gbias_ref.shape)

    def rev(i):
        return ns - 1 - i

    return pl.pallas_call(
        body, name="gla_bwd", grid=(ns,),
        in_specs=_gla_specs(rev) + [
            pl.BlockSpec((NCH * ST_ROWS, B_DK), lambda i: (rev(i), 0)),
            pl.BlockSpec((GSTEP, D), lambda i: (rev(i), 0)),
        ],
        out_specs=[
            pl.BlockSpec((GSTEP, W_GLA), lambda i: (rev(i), 0)),
            pl.BlockSpec((GSTEP, W_BL), lambda i: (rev(i), 0)),
            pl.BlockSpec((W_BL, 512), lambda i: (0, 0)),
            pl.BlockSpec((8, 512), lambda i: (0, 0)),
        ],
        out_shape=[
            jax.ShapeDtypeStruct((T, W_GLA), MXU),
            jax.ShapeDtypeStruct((T, W_BL), MXU),
            jax.ShapeDtypeStruct((W_BL, 512), F32),
            jax.ShapeDtypeStruct((8, 512), F32),
        ],
        scratch_shapes=[pltpu.VMEM((B_HEADS * B_DV, B_DK), F32)],
        compiler_params=_cp(("arbitrary",)),
    )(proj, proj, proj, proj, gu_pad, bias, states, do_b)


def _mid(x, target, proj, o_a, o_b, late, w_bn4, fnw):
    T = x.shape[0]
    tT = min(T, 128)
    nbuf = 4
    o_ag, o_bg, o_ma, o_mb = (c - C_GATES for c in (C_AG, C_BG, C_MA, C_MB))

    def body(x_ref, t_ref, oa_ref, ob_ref, gates_ref, late_ref, wbn_ref, fnw_ref,
             dx2_ref, doa_ref, dob_ref, dgates_ref,
             tail0_ref, tail1_ref, gfn_ref, gbn_ref, loss_ref, buf_ref, gw_ref):
        i = pl.program_id(0)

        def weight(p):
            return late_ref[:, 128 * p:128 * (p + 1), :].reshape(D, D)

        @pl.when(i == 0)
        def _():
            for r in (gw_ref, gfn_ref, gbn_ref, loss_ref):
                r[...] = jnp.zeros_like(r)

        rows = pl.ds(pl.multiple_of((i % nbuf) * tT, tT), tT)

        def keep(k, val):
            buf_ref[k, rows, :] = val

        oa, ag = oa_ref[...], gates_ref[:, o_ag:o_ag + D]
        sg_a = _sigmoid(ag)
        silu_a = ag * sg_a
        oag_b = (oa * silu_a).astype(MXU)
        keep(0, oag_b)
        y_a = _dot(oag_b, weight(0))

        ob, bg = ob_ref[...], gates_ref[:, o_bg:o_bg + D]
        rbs, obhats = [], []
        for h in range(B_HEADS):
            obh = ob[:, h * B_DV:(h + 1) * B_DV]
            rb = lax.rsqrt(jnp.mean(obh * obh, axis=-1, keepdims=True) + EPS)
            rbs.append(rb)
            obhats.append(obh * rb)
        obhat = jnp.concatenate(obhats, axis=1)
        wbn = wbn_ref[...]
        obn = obhat * wbn
        sg_b = _sigmoid(bg)
        silu_b = bg * sg_b
        obg_b = (obn * silu_b).astype(MXU)
        keep(1, obg_b)
        y_b = _dot(obg_b, weight(1))

        sa, sb = _sigmoid(gates_ref[:, o_ma:o_ma + D]), _sigmoid(gates_ref[:, o_mb:o_mb + D])
        mg_b = (sa * y_a + sb * y_b).astype(MXU)
        keep(2, mg_b)
        x2 = x_ref[...] + _dot(mg_b, weight(2))
        r2 = lax.rsqrt(jnp.mean(x2 * x2, axis=-1, keepdims=True) + EPS)
        xh2 = x2 * r2
        fw = fnw_ref[...]
        err = xh2 * fw - t_ref[...]
        tok = jnp.mean(err * err, axis=-1, keepdims=True)
        loss_ref[...] = loss_ref[...] + 0.5 * jnp.sum(tok, axis=0, keepdims=True)

        dy = err * (1.0 / D)
        gfn_ref[...] = gfn_ref[...] + jnp.broadcast_to(jnp.sum(dy * xh2, axis=0, keepdims=True), gfn_ref.shape)
        gy = dy * fw
        dx2 = r2 * (gy - xh2 * jnp.mean(gy * xh2, axis=-1, keepdims=True))
        dx2_ref[...] = dx2
        dx2_b = dx2.astype(MXU)
        keep(5, dx2_b)
        dmg = _dot_nt(dx2_b, weight(2))

        dgates_ref[:, o_ma:o_ma + D] = (dmg * y_a * sa * (1.0 - sa)).astype(dgates_ref.dtype)
        dgates_ref[:, o_mb:o_mb + D] = (dmg * y_b * sb * (1.0 - sb)).astype(dgates_ref.dtype)
        dya_b = (dmg * sa).astype(MXU)
        dyb_b = (dmg * sb).astype(MXU)
        keep(3, dya_b)
        keep(4, dyb_b)
        doag = _dot_nt(dya_b, weight(0))
        dobg = _dot_nt(dyb_b, weight(1))

        @pl.when(i % nbuf == nbuf - 1)
        def _():
            for p in range(3):
                gw_ref[p] = gw_ref[p] + _dot_tn(buf_ref[p], buf_ref[3 + p])

        @pl.when(i == pl.num_programs(0) - 1)
        def _():
            for hf, tail_ref in enumerate((tail0_ref, tail1_ref)):
                for d in range(NDEV):
                    for p in range(3):
                        tail_ref[d, 128 * p:128 * (p + 1), :] = (
                            gw_ref[p, 128 * d:128 * (d + 1), hf * DH:(hf + 1) * DH].astype(tail_ref.dtype))

        doa_ref[...] = doag * silu_a
        dgates_ref[:, o_ag:o_ag + D] = (doag * oa * (sg_a * (1.0 + ag * (1.0 - sg_a)))).astype(dgates_ref.dtype)
        dobn = dobg * silu_b
        dgates_ref[:, o_bg:o_bg + D] = (dobg * obn * (sg_b * (1.0 + bg * (1.0 - sg_b)))).astype(dgates_ref.dtype)
        gg = dobn * wbn
        gbn = jnp.zeros((1, B_DV), F32)
        for h in range(B_HEADS):
            sl = slice(h * B_DV, (h + 1) * B_DV)
            gbn = gbn + jnp.sum(dobn[:, sl] * obhats[h], axis=0, keepdims=True)
            ggh = gg[:, sl]
            dob_ref[:, sl] = rbs[h] * (ggh - obhats[h] * jnp.mean(ggh * obhats[h], axis=-1, keepdims=True))
        gbn_ref[...] = gbn_ref[...] + jnp.broadcast_to(gbn, gbn_ref.shape)

    assert (T // tT) % nbuf == 0
    tile = pl.BlockSpec((tT, D), lambda i: (i, 0))
    row = pl.BlockSpec((1, D), lambda i: (0, 0))
    acc8 = pl.BlockSpec((8, D), lambda i: (0, 0))
    return pl.pallas_call(
        body, name="mid", grid=(T // tT,),
        in_specs=[tile, tile, tile, tile, pl.BlockSpec((tT, W_GATES), lambda i: (i, C_GATES // W_GATES)),
                  _vmem(), row, row],
        out_specs=[tile, tile, tile, pl.BlockSpec((tT, W_GATES), lambda i: (i, 0)), _vmem(), _vmem(),
                   acc8, pl.BlockSpec((8, B_DV), lambda i: (0, 0)), pl.BlockSpec((8, LANE), lambda i: (0, 0))],
        out_shape=[
            jax.ShapeDtypeStruct((T, D), F32),
            jax.ShapeDtypeStruct((T, D), F32),
            jax.ShapeDtypeStruct((T, D), F32),
            jax.ShapeDtypeStruct((T, W_GATES), MXU),
            jax.ShapeDtypeStruct((NDEV, 384, DH), WIRE),
            jax.ShapeDtypeStruct((NDEV, 384, DH), WIRE),
            jax.ShapeDtypeStruct((8, D), F32),
            jax.ShapeDtypeStruct((8, B_DV), F32),
            jax.ShapeDtypeStruct((8, LANE), F32),
        ],
        scratch_shapes=[pltpu.VMEM((6, nbuf * tT, D), MXU), pltpu.VMEM((3, D, D), F32)],
        compiler_params=_cp(("arbitrary",)),
    )(x, target, o_a, o_b, proj, late, w_bn4, fnw)


DH = D // 2


_GW_TILES = (("q", 0, 512, 0), ("q", 1, 512, 512), ("kv", 0, 256, 1024), ("bl", 0, RANK, 5376),
             ("gla", 0, 512, 3328), ("gla", 1, 512, 3840), ("gla", 2, 512, 2304), ("gla", 3, 512, 2816),
             ("gates", 0, 512, 1280), ("gates", 1, 512, 1792), ("gates", 2, 512, 4352), ("gates", 3, 512, 4864),
             ("gates", 4, 512, 5392), ("gates", 5, 512, 5904), ("gates", 6, 512, 6416), ("gates", 7, 512, 6928))


def _gw_unpermute(piece, t):
    if piece == "q":
        parts = []
        for blk in range(t.shape[0] // LANE):
            g = [t[blk * LANE + 32 * i:blk * LANE + 32 * (i + 1)] for i in range(4)]
            parts += [g[0], g[2], g[1], g[3]]
        return jnp.concatenate(parts, axis=0)
    if piece == "kv":
        k = [t[64 * i:64 * i + 32] + t[64 * i + 32:64 * i + 64] for i in range(4)]
        v = [t[256 + 128 * g:256 + 128 * g + 64] + t[256 + 128 * g + 64:256 + 128 * (g + 1)] for g in range(2)]
        return jnp.concatenate(k + v, axis=0)
    if piece == "bl":
        return t[:RANK]
    return t


def _gw_half(h, pieces, half, after=None):
    T = h.shape[0]
    steps = len(_GW_TILES)

    def body(*refs):
        h_ref = refs[0]
        srcs = dict(zip(("q", "kv", "bl", "gla", "gates"), refs[1:6]))
        o_ref, stage, sems = refs[-3:]
        j = pl.program_id(0)

        def out_copy(k):
            _, _, n, off = _GW_TILES[k]
            return pltpu.make_async_copy(stage.at[k % 2, 0:n], o_ref.at[pl.ds(off, n)], sems.at[k % 2])

        for k, (piece, _, n, _) in enumerate(_GW_TILES):
            @pl.when(j == k)
            def _(k=k, piece=piece, n=n):
                if k >= 2:
                    out_copy(k - 2).wait()
                t = _gw_unpermute(piece, _dot_tn(srcs[piece][...], h_ref[...]))
                stage[k % 2, 0:n, :] = t.astype(stage.dtype)
                out_copy(k).start()

        @pl.when(j == steps - 1)
        def _():
            out_copy(steps - 2).wait()
            out_copy(steps - 1).wait()

    def tile_of(lo, hi):
        return lambda j: (0, jnp.clip(j - lo, 0, hi - lo - 1))

    in_specs = [pl.BlockSpec((T, DH), lambda j: (0, half)),
                pl.BlockSpec((T, 512), tile_of(0, 2)), pl.BlockSpec((T, 512), lambda j: (0, 0)),
                pl.BlockSpec((T, W_BL), lambda j: (0, 0)),
                pl.BlockSpec((T, 512), tile_of(4, 8)), pl.BlockSpec((T, 512), tile_of(8, 16))]
    args = [h, *pieces]
    if after is not None:
        in_specs.append(_any())
        args.append(after)
    return pl.pallas_call(
        body, name=f"gw_in_half{half}", grid=(steps,),
        in_specs=in_specs, out_specs=_any(),
        out_shape=jax.ShapeDtypeStruct((IN_WIDTH, DH), WIRE),
        scratch_shapes=[pltpu.VMEM((2, 512, DH), WIRE), pltpu.SemaphoreType.DMA((2,))],
        compiler_params=_cp(("arbitrary",)),
    )(*args)


def _chip_copies(s_ref, got_ref, send_sems, recv_sems):
    x, y, c = _place()
    chips = [(1 - x, y), (x, 1 - y), (1 - x, 1 - y)]
    return [pltpu.make_async_remote_copy(
        src_ref=s_ref.at[2 * px + py], dst_ref=got_ref.at[j],
        send_sem=send_sems.at[j], recv_sem=recv_sems.at[j], device_id=(px, py, c), device_id_type=MESH)
        for j, (px, py) in enumerate(chips)]


_EFFECT = pltpu.SideEffectType.DATAFLOW_SIDE_EFFECTING


def _hbm():
    return pl.BlockSpec(memory_space=pltpu.HBM)


def _sem():
    return pl.BlockSpec(memory_space=pltpu.SEMAPHORE)


def _chip_start(sums, half):
    land = pltpu.with_memory_space_constraint(lax.empty((3,) + sums.shape[1:], sums.dtype), pltpu.HBM)

    def body(s_ref, land_ref, send_sems, recv_sems, s_thru, land_thru, token):
        for cp in _chip_copies(s_ref, land_ref, send_sems, recv_sems):
            cp.start()
        token[...] = jnp.zeros_like(token)

    return pl.pallas_call(
        body, name=f"chip_start{half}",
        out_shape=(pltpu.SemaphoreType.DMA((3,)), pltpu.SemaphoreType.DMA((3,)),
                   pltpu.HBM(sums.shape, sums.dtype), pltpu.HBM(land.shape, land.dtype),
                   jax.ShapeDtypeStruct((8, LANE), F32)),
        in_specs=(_hbm(), _hbm()), out_specs=(_sem(), _sem(), _hbm(), _hbm(), _vmem()),
        input_output_aliases={0: 2, 1: 3},
        compiler_params=pltpu.CompilerParams(has_side_effects=_EFFECT),
    )(pltpu.with_memory_space_constraint(sums, pltpu.HBM), land)


def _chip_wait(send_sems, recv_sems, s_thru, land_thru, after, half):
    def body(s_ref, land_ref, send_sems, recv_sems, after_ref, s_out, got_ref):
        copies = _chip_copies(s_ref, land_ref, send_sems, recv_sems)
        for cp in copies:
            cp.wait_send()
        for cp in copies:
            cp.wait_recv()

    return pl.pallas_call(
        body, name=f"chip_wait{half}",
        out_shape=(pltpu.HBM(s_thru.shape, s_thru.dtype), pltpu.HBM(land_thru.shape, land_thru.dtype)),
        in_specs=(_hbm(), _hbm(), _sem(), _sem(), _any()), out_specs=(_hbm(), _hbm()),
        input_output_aliases={0: 0, 1: 1},
        compiler_params=pltpu.CompilerParams(has_side_effects=_EFFECT),
    )(s_thru, land_thru, send_sems, recv_sems, after)


def _dh_norm(pieces, offsets, wf, x, dx2, norm_w, after):
    T = x.shape[0]
    tT = min(T, 256)
    widths = [p.shape[1] for p in pieces]
    npc = len(pieces)

    def body(*refs):
        dp_refs = refs[:npc]
        wf_ref, x_ref, dx2_ref, nw_ref, _, gx_ref, gnw_ref = refs[npc:]

        @pl.when(pl.program_id(0) == 0)
        def _():
            gnw_ref[...] = jnp.zeros_like(gnw_ref)

        dh = jnp.zeros((tT, D), F32)
        for dp_ref, off, w in zip(dp_refs, offsets, widths):
            dh = dh + _dot(dp_ref[...], wf_ref[off:off + w, :])
        xv = x_ref[...]
        r = lax.rsqrt(jnp.mean(xv * xv, axis=-1, keepdims=True) + EPS)
        xh = xv * r
        gnw_ref[...] = gnw_ref[...] + jnp.broadcast_to(jnp.sum(dh * xh, axis=0, keepdims=True), gnw_ref.shape)
        g = dh * nw_ref[...]
        gx_ref[...] = r * (g - xh * jnp.mean(g * xh, axis=-1, keepdims=True)) + dx2_ref[...]

    tile = pl.BlockSpec((tT, D), lambda i: (i, 0))
    return pl.pallas_call(
        body, name="dh_norm", grid=(T // tT,),
        in_specs=[pl.BlockSpec((tT, w), lambda i: (i, 0)) for w in widths]
        + [_vmem(), tile, tile, pl.BlockSpec((1, D), lambda i: (0, 0)), _any()],
        out_specs=[tile, pl.BlockSpec((8, D), lambda i: (0, 0))],
        out_shape=[jax.ShapeDtypeStruct((T, D), F32), jax.ShapeDtypeStruct((8, D), F32)],
        compiler_params=_cp(("arbitrary",)),
    )(*pieces, wf, x, dx2, norm_w, after)


def _adamw_math(w, g, m, v):
    m = ADAM_B1 * m + (1.0 - ADAM_B1) * g
    v = ADAM_B2 * v + (1.0 - ADAM_B2) * (g * g)
    m_hat = m * (1.0 / (1.0 - ADAM_B1 ** ADAM_STEP))
    v_hat = v * (1.0 / (1.0 - ADAM_B2 ** ADAM_STEP))
    delta = -ADAM_LR * (m_hat / (jnp.sqrt(v_hat) + ADAM_EPS) + ADAM_WD * w)
    return delta, m, v


def _fetch_partials(s_ref, got_ref, buf, sems):
    x, y, _ = _place()
    cps = [pltpu.make_async_copy(s_ref.at[2 * x + y], buf.at[0], sems.at[0])]
    cps += [pltpu.make_async_copy(got_ref.at[j], buf.at[1 + j], sems.at[1 + j]) for j in range(3)]
    for cp in cps:
        cp.start()
    for cp in cps:
        cp.wait()


SMALL_AT = dict(norm_w=0, fnw=8, bias=16, bn=24, sinks=32, loss=40)
ROW_AT = (R_IN, R_A, R_B, R_O)


def _finish_small(ws, ms, vs, smalls):
    names = ["norm_w", "fnw", "bias", "bn", "sinks"]
    widths = [ws[n].shape[1] for n in names]

    def body(*refs):
        w_refs, m_refs, v_refs = refs[0:5], refs[5:10], refs[10:15]
        smalls_ref, loss_ref = refs[15], refs[16]
        outs, tot = refs[17:37], refs[37]
        acc = smalls_ref[0]
        for d in range(1, NDEV):
            acc = acc + smalls_ref[d]
        tot[...] = acc
        loss_ref[...] = tot[SMALL_AT["loss"]:SMALL_AT["loss"] + 1, 0:1]
        for p, (nm_, wd) in enumerate(zip(names, widths)):
            r = SMALL_AT[nm_]
            g = tot[r:r + 1, 0:wd]
            d, nm, nv = _adamw_math(w_refs[p][...], g, m_refs[p][...], v_refs[p][...])
            for o, val in zip(outs[4 * p:4 * p + 4], (g, d, nm, nv)):
                o[...] = val

    res = pl.pallas_call(
        body, name="finish_small",
        in_specs=[_vmem()] * 16, out_specs=[_vmem()] * 21,
        out_shape=[jax.ShapeDtypeStruct((1, 1), F32)]
        + [jax.ShapeDtypeStruct((1, wd), F32) for wd in widths for _ in range(4)],
        scratch_shapes=[pltpu.VMEM((SMALL_ROWS, D), F32)],
        compiler_params=_cp(),
    )(*[ws[n] for n in names], *[ms[n] for n in names], *[vs[n] for n in names], smalls)
    return res[0], {n: tuple(res[1 + 4 * p:5 + 4 * p]) for p, n in enumerate(names)}


def _finish(w_rows, m_rows, v_rows, gu_w, gu_m, gu_v, sums, got):
    shapes = [(SHARD, 1, D)] + [w.shape for w in w_rows[1:]]

    row_block = 96

    def columns(ref, p, cols, r0, n):
        if p:
            return ref, (slice(r0, r0 + n), cols)
        flat = ref if ref.shape == (SHARD * LANE_TILES, LANE) else ref.reshape(SHARD * LANE_TILES, LANE)
        return flat, (pl.ds(cols.start // LANE + LANE_TILES * r0, n, stride=LANE_TILES), slice(None))

    def read(ref, p, cols, r0, n):
        ref, at = columns(ref, p, cols, r0, n)
        return ref[at]

    def body(*refs):
        wr_refs, mr_refs, vr_refs = refs[0:4], refs[4:8], refs[8:12]
        guw_ref, gum_ref, guv_ref = refs[12:15]
        s_refs, got_refs = refs[15:17], refs[17:19]
        row_outs = refs[19:35]
        gu_outs = refs[35:39]
        buf, gsh, sems, big, big_sems = refs[39:]
        loads = [pltpu.make_async_copy(r[0], big.at[k], big_sems.at[k]) for k, r in enumerate((wr_refs, mr_refs, vr_refs))]
        for cp in loads:
            cp.start()
        wr_refs, mr_refs, vr_refs = ((big.at[k],) + tuple(r[1:]) for k, r in enumerate((wr_refs, mr_refs, vr_refs)))
        x, y, c = _place()
        me_slot = 4 * x + 2 * y + c
        down = 2 * me_slot

        def total(rows, cols):
            g = buf[0, rows, cols].astype(F32)
            for j in range(1, 4):
                g = g + buf[j, rows, cols].astype(F32)
            return g

        def update(p, grad, cols):
            nrows = shapes[p][0]
            for r0 in range(0, nrows, row_block):
                n = min(row_block, nrows - r0)
                g = grad(r0, n)
                d, nm, nv = _adamw_math(read(wr_refs[p], p, cols, r0, n), g, read(mr_refs[p], p, cols, r0, n),
                                        read(vr_refs[p], p, cols, r0, n))
                for o, val in zip(row_outs[4 * p:4 * p + 4], (g, d, nm, nv)):
                    o, at = columns(o, p, cols, r0, n)
                    o[at] = val

        for hf in range(2):
            _fetch_partials(s_refs[hf], got_refs[hf], buf, sems)
            for cc in range(DH // LANE):
                src = slice(cc * LANE, (cc + 1) * LANE)
                cols = slice(hf * DH + cc * LANE, hf * DH + (cc + 1) * LANE)
                for r0 in range(0, SHARD_PAD, row_block):
                    rows = slice(r0, min(r0 + row_block, SHARD_PAD))
                    gsh[rows, :] = total(rows, src)
                if hf == 0 and cc == 0:
                    for cp in loads:
                        cp.wait()
                update(0, lambda r0, n: gsh[pl.ds(down + r0, n), :], cols)
                for p in range(1, 4):
                    update(p, lambda r0, n, p=p: total(slice(ROW_AT[p] + r0, ROW_AT[p] + r0 + n), src), cols)
            if hf == 0:
                g = total(slice(R_GU, R_GU + RANK), slice(0, 64))
                d, nm, nv = _adamw_math(guw_ref[...], g, gum_ref[...], guv_ref[...])
                for o, val in zip(gu_outs, (g, d, nm, nv)):
                    o[...] = val

    res = pl.pallas_call(
        body, name="finish",
        in_specs=([_any()] + [_vmem()] * 3) * 3 + [_vmem()] * 3 + [_any()] * 4,
        out_specs=[_vmem()] * 20,
        out_shape=[jax.ShapeDtypeStruct(s, F32) for s in shapes for _ in range(4)]
        + [jax.ShapeDtypeStruct((RANK, 64), F32)] * 4,
        scratch_shapes=[pltpu.VMEM((4, ROWS, DH), sums[0].dtype), pltpu.VMEM((SHARD_PAD, LANE), F32),
                        pltpu.SemaphoreType.DMA((4,)),
                        pltpu.VMEM((3, SHARD * LANE_TILES, LANE), F32), pltpu.SemaphoreType.DMA((3,))],
        compiler_params=_cp(),
    )(*w_rows, *m_rows, *v_rows, gu_w, gu_m, gu_v, *sums, *got)
    return tuple(res[0:16]), tuple(res[16:20])


def _place():
    x, y, c = lax.axis_index("x"), lax.axis_index("y"), lax.axis_index("c")
    return x, y, c


def _peers(x, y, c):
    return [(x ^ dx, y ^ dy, c ^ dc) for dx in range(2) for dy in range(2) for dc in range(2) if dx + dy + dc]


def _late_gather_start(blk, after, name="late_gather"):
    land = pltpu.with_memory_space_constraint(lax.empty((NDEV,) + blk.shape, blk.dtype), pltpu.HBM)

    def body(b_ref, land_ref, after_ref, send_sems, recv_sems, b_thru, land_thru, token):
        x, y, c = _place()
        for k, to in enumerate(_peers(x, y, c)):
            pltpu.make_async_remote_copy(
                src_ref=b_ref, dst_ref=land_ref.at[4 * x + 2 * y + c], send_sem=send_sems.at[k],
                recv_sem=recv_sems.at[k], device_id=to, device_id_type=MESH).start()
        token[...] = jnp.zeros_like(token)

    return pl.pallas_call(
        body, name=name + "_start",
        out_shape=(pltpu.SemaphoreType.DMA((7,)), pltpu.SemaphoreType.DMA((7,)),
                   pltpu.HBM(blk.shape, blk.dtype), pltpu.HBM(land.shape, land.dtype),
                   jax.ShapeDtypeStruct((8, LANE), F32)),
        in_specs=(_hbm(), _hbm(), _any()), out_specs=(_sem(), _sem(), _hbm(), _hbm(), _vmem()),
        input_output_aliases={0: 2, 1: 3},
        compiler_params=pltpu.CompilerParams(has_side_effects=_EFFECT),
    )(pltpu.with_memory_space_constraint(blk, pltpu.HBM), land, after)


def _late_gather_wait(send_sems, recv_sems, b_thru, land_thru, after, after2, name="late_gather"):
    def body(b_ref, land_ref, send_sems, recv_sems, after_ref, after2_ref, b_out, got_ref):
        x, y, c = _place()
        copies = [pltpu.make_async_remote_copy(
            src_ref=b_ref, dst_ref=land_ref.at[4 * x + 2 * y + c], send_sem=send_sems.at[k],
            recv_sem=recv_sems.at[k], device_id=to, device_id_type=MESH)
            for k, to in enumerate(_peers(x, y, c))]
        for cp in copies:
            cp.wait_send()
        for cp in copies:
            cp.wait_recv()

    return pl.pallas_call(
        body, name=name + "_wait",
        out_shape=(pltpu.HBM(b_thru.shape, b_thru.dtype), pltpu.HBM(land_thru.shape, land_thru.dtype)),
        in_specs=(_hbm(), _hbm(), _sem(), _sem(), _any(), _any()), out_specs=(_hbm(), _hbm()),
        input_output_aliases={0: 0, 1: 1},
        compiler_params=pltpu.CompilerParams(has_side_effects=_EFFECT),
    )(b_thru, land_thru, send_sems, recv_sems, after, after2)


G_ROWS = SHARD_PAD + RANK


def _gather_blocks(w_in_t, gu_s, xs, norm_w, pos_col):
    rows, cols = G_ROWS, D
    T = xs.shape[0]
    tT = min(T, 256)
    inv_row, sign_row = _rope_rows()

    def body(wi_ref, gu_ref, xs_hbm, nw_ref, pos_ref, inv_ref, sign_ref,
             out_ref, h_ref, cos_ref, sin_ref, x_ref, frame_ref, xs_ref, send_sems, recv_sems, local_sem, xs_sem):
        load_xs = pltpu.make_async_copy(xs_hbm, xs_ref, xs_sem)
        load_xs.start()
        x, y, c = _place()
        me, sibling = (x, y, c), (x, y, 1 - c)
        chips = [(1 - x, y), (x, 1 - y), (1 - x, 1 - y)]
        shift = 2 * (4 * x + 2 * y + c)
        frame_ref[SHARD - SHARD % 8:, :] = jnp.zeros((SHARD_PAD - SHARD + SHARD % 8, LANE), F32)
        for cc in range(LANE_TILES):
            cs = slice(cc * LANE, (cc + 1) * LANE)
            frame_ref[:SHARD, :] = wi_ref[pl.ds(cc, SHARD, stride=LANE_TILES), :]
            x_ref[0:SHARD_PAD, cs] = pltpu.roll(frame_ref[...], shift, 0).astype(x_ref.dtype)
        x_ref[SHARD_PAD:G_ROWS, :] = jnp.zeros((RANK, D), x_ref.dtype)
        x_ref[SHARD_PAD:G_ROWS, 0:64] = gu_ref[...].astype(x_ref.dtype)

        def slot(px, py, pc):
            return out_ref.at[4 * px + 2 * py + pc]

        def copy(k, block, to, src=None):
            return pltpu.make_async_remote_copy(
                src_ref=slot(*block) if src is None else src, dst_ref=slot(*block),
                send_sem=send_sems.at[k], recv_sem=recv_sems.at[k], device_id=to, device_id_type=MESH)

        mine = pltpu.make_async_copy(x_ref, slot(*me), local_sem)
        mine.start()
        first = [copy(0, me, sibling, src=x_ref)]
        first += [copy(1 + j, me, (*chip, c), src=x_ref) for j, chip in enumerate(chips)]
        for cp in first:
            cp.start()
        load_xs.wait()

        @pl.loop(0, T // tT)
        def _(i):
            rows_i = pl.ds(pl.multiple_of(i * tT, tT), tT)
            _prologue_rows(rows_i, xs_ref, nw_ref, pos_ref, inv_ref, sign_ref, h_ref, cos_ref, sin_ref)

        passed = [copy(4 + j, (*chip, c), sibling) for j, chip in enumerate(chips)]
        for j, chip in enumerate(chips):
            copy(1 + j, (*chip, c), me).wait_recv()
            passed[j].start()
        copy(0, sibling, me).wait_recv()
        for j, chip in enumerate(chips):
            copy(4 + j, (*chip, 1 - c), me).wait_recv()
        for cp in first + passed:
            cp.wait_send()
        mine.wait()

    return pl.pallas_call(
        body, name="gather_weights",
        in_specs=[_vmem(), _vmem(), _any()] + [_vmem()] * 4, out_specs=[_any()] + [_vmem()] * 3,
        out_shape=[jax.ShapeDtypeStruct((NDEV, rows, cols), WIRE), jax.ShapeDtypeStruct((T, D), MXU),
                   jax.ShapeDtypeStruct((T, LANE), F32), jax.ShapeDtypeStruct((T, LANE), F32)],
        scratch_shapes=[pltpu.VMEM((rows, cols), WIRE), pltpu.VMEM((SHARD_PAD, LANE), F32), pltpu.VMEM((T, D), F32),
                        pltpu.SemaphoreType.DMA((7,)), pltpu.SemaphoreType.DMA((7,)), pltpu.SemaphoreType.DMA,
                        pltpu.SemaphoreType.DMA],
        compiler_params=_cp(),
    )(w_in_t, gu_s, xs, norm_w, pos_col, inv_row, sign_row)


def _pair_reduce(gwt, tails, half):
    n = gwt.shape[1]
    starts = [SHARD_PAD]
    for t in tails:
        starts.append(starts[-1] + t.shape[1])
    rows = starts[-1]
    blk = (4, rows, n)
    npart = 1 + len(tails)

    def body(*refs):
        g_ref, t_refs = refs[0], refs[1:npart]
        out_ref, acc, got, own, send_sems, recv_sems, own_sems, out_sems = refs[npart:]
        x, y, c = _place()

        def parts(d, dst):
            frame = g_ref.at[pl.ds(pl.multiple_of(FRAME * d, 16), SHARD_PAD)]
            return [(frame, dst.at[0:SHARD_PAD])] + [
                (t_ref.at[d], dst.at[starts[k]:starts[k + 1]]) for k, t_ref in enumerate(t_refs)]

        sends, loads, stores = [], [], []
        for chip in range(4):
            sends.append([pltpu.make_async_remote_copy(
                src_ref=s, dst_ref=d_, send_sem=send_sems.at[chip, k], recv_sem=recv_sems.at[chip, k],
                device_id=(x, y, 1 - c), device_id_type=MESH)
                for k, (s, d_) in enumerate(parts(2 * chip + (1 - c), got.at[chip]))])
            loads.append([pltpu.make_async_copy(s, d_, own_sems.at[chip, k])
                          for k, (s, d_) in enumerate(parts(2 * chip + c, own.at[chip]))])
            stores.append(pltpu.make_async_copy(acc.at[chip], out_ref.at[chip], out_sems.at[chip]))
        for group in sends + loads:
            for cp in group:
                cp.start()
        for chip in range(4):
            for cp in loads[chip]:
                cp.wait()
            for cp in sends[chip]:
                cp.wait_recv()
            acc[chip] = (own[chip].astype(F32) + got[chip].astype(F32)).astype(acc.dtype)
            stores[chip].start()
        for cp in stores:
            cp.wait()
        for group in sends:
            for cp in group:
                cp.wait_send()

    return pl.pallas_call(
        body, name=f"pair_reduce{half}",
        in_specs=[_any()] * npart, out_specs=_any(),
        out_shape=jax.ShapeDtypeStruct(blk, gwt.dtype),
        scratch_shapes=[pltpu.VMEM(blk, gwt.dtype), pltpu.VMEM(blk, gwt.dtype), pltpu.VMEM(blk, gwt.dtype),
                        pltpu.SemaphoreType.DMA((4, npart)), pltpu.SemaphoreType.DMA((4, npart)),
                        pltpu.SemaphoreType.DMA((4, npart)), pltpu.SemaphoreType.DMA((4,))],
        compiler_params=_cp(),
    )(gwt, *tails)


def _pad_cols(a, cols):
    return jnp.pad(a, ((0, 0), (0, cols - a.shape[1])))


def _pad_rows(a, rows):
    return jnp.pad(a, ((0, rows - a.shape[0]), (0, 0)))


FRAME = 928


def _wft_plan():
    moves = []
    for blk in range(8):
        for half in range(2):
            for sub in range(2):
                moves.append((C_Q + 128 * blk + 32 * (2 * half + sub), 128 * blk + 32 * (2 * sub + half), 32))
    for idx in range(4):
        for dup in range(2):
            moves.append((C_KD + 64 * idx + 32 * dup, 1024 + 32 * idx, 32))
    for g in range(2):
        for dup in range(2):
            moves.append((C_VD + 128 * g + 64 * dup, 1152 + 64 * g, 64))
    moves += [(C_BL, 5376, RANK), (C_BV, 3328, 1024), (C_BQ, 2304, 512), (C_BK, 2816, 512),
              (C_AG, 1280, 1024), (C_BG, 4352, 1024), (C_MA, 5392, 1024), (C_MB, 6416, 1024)]
    bulk, seams = [], []
    for dst, src, n in moves:
        r = src
        while r < src + n:
            f = min(r // FRAME, NDEV - 1)
            local = r - FRAME * f
            if f > 0 and local < 16:
                assert local == 0
                seams.append((f, dst + r - src))
                step = 16
            else:
                step = min(src + n, FRAME * (f + 1) if f < NDEV - 1 else IN_WIDTH) - r
                bulk.append((f, local, dst + r - src, step))
            r += step
    assert sorted(f for f, _ in seams) == list(range(1, NDEV))
    return bulk, seams, [(C_BL + RANK, C_GLA - C_BL - RANK)]


def _build_wft_copies(frames):
    bulk, seams, zeros = _wft_plan()
    (z0, zn), = zeros

    def body(f_ref, o_ref, edge, sems, esems):
        copies = [pltpu.make_async_copy(f_ref.at[f, pl.ds(l0, n)], o_ref.at[pl.ds(dst, n)], sems.at[i])
                  for i, (f, l0, dst, n) in enumerate(bulk)]
        loads = []
        for i, (f, _) in enumerate(seams):
            loads.append(pltpu.make_async_copy(f_ref.at[f, pl.ds(0, 16)], edge.at[i, 0], esems.at[i, 0]))
            loads.append(pltpu.make_async_copy(f_ref.at[f - 1, pl.ds(FRAME, 16)], edge.at[i, 1], esems.at[i, 1]))
        for cp in copies + loads:
            cp.start()
        o_ref[z0:z0 + zn, :] = jnp.zeros((zn, D), o_ref.dtype)
        for cp in loads:
            cp.wait()
        for i, (_, dst) in enumerate(seams):
            o_ref[dst:dst + 16, :] = edge[i, 0] + edge[i, 1]
        for cp in copies:
            cp.wait()

    return pl.pallas_call(
        body, name="build_wft",
        in_specs=[_any()], out_specs=_vmem(),
        out_shape=jax.ShapeDtypeStruct((NF, D), frames.dtype),
        scratch_shapes=[pltpu.VMEM((len(seams), 2, 16, D), frames.dtype),
                        pltpu.SemaphoreType.DMA((len(bulk),)), pltpu.SemaphoreType.DMA((len(seams), 2))],
        compiler_params=_cp(),
    )(frames)


def kernel(x, positions, norm_w, w_in, a_sinks, b_gate_up, b_gate_bias, b_out_norm_w, w_a_proj, w_b_proj, w_out, final_norm_w, loss_target, m_norm_w, m_w_in, m_a_sinks, m_b_gate_up, m_b_gate_bias, m_b_out_norm_w, m_w_a_proj, m_w_b_proj, m_w_out, m_final_norm_w, v_norm_w, v_w_in, v_a_sinks, v_b_gate_up, v_b_gate_bias, v_b_out_norm_w, v_w_a_proj, v_w_b_proj, v_w_out, v_final_norm_w):
    T = x.shape[1]
    xs, target = x[0], loss_target[0]
    fnw = final_norm_w.reshape(1, D)
    me = 4 * lax.axis_index("x") + 2 * lax.axis_index("y") + lax.axis_index("c")
    allw, h, cos, sin = _gather_blocks(_by_lane_tile(w_in), b_gate_up[0], xs, norm_w, positions.reshape(T, 1))
    late_blk = jnp.concatenate([w_a_proj[0], w_b_proj[0], w_out[0]], axis=0).astype(WIRE)
    l_send, l_recv, l_blk, l_land, l_started = _late_gather_start(late_blk, cos)
    wf = _build_wft_copies(allw)
    gu = allw[:, SHARD_PAD:G_ROWS, :64].transpose(1, 0, 2).reshape(RANK, 512)
    gu_pad = _pad_rows(gu, W_BL)

    proj = _proj(h, wf, l_started)
    o_a, lse = _swa_fwd(proj, cos, sin, a_sinks)
    o_b, states = _gla_fwd(proj, gu_pad, b_gate_bias)
    l_blk, l_land = _late_gather_wait(l_send, l_recv, l_blk, l_land, states, lse)
    late = lax.dynamic_update_slice(l_land, l_blk[None], (me, 0, 0))
    (dx2, do_a, do_b, d_gates, g_late0, g_late1, g_fn, g_bn, loss_part) = _mid(
        xs, target, proj, o_a, o_b, late, jnp.tile(b_out_norm_w, (1, B_HEADS)), fnw)
    d_q, d_kv, g_sinks = _swa_bwd(proj, cos, sin, a_sinks, do_a, o_a, lse, cos)
    d_gla, d_bl, g_gu, g_bias = _gla_bwd(proj, gu_pad, b_gate_bias, states, do_b)
    pieces = [d_q, d_kv, d_bl, d_gla, d_gates]
    offsets = [C_Q, C_KD, C_BL, C_GLA, C_GATES]

    ggu = g_gu[:RANK].reshape(RANK, NDEV, 64).transpose(1, 0, 2)
    ggu_half = [jnp.pad(ggu, ((0, 0), (0, 0), (0, DH - 64))).astype(WIRE), jnp.zeros((NDEV, RANK, DH), WIRE)]
    tails = [[g_late0, ggu_half[0]], [g_late1, ggu_half[1]]]

    send1, recv1, s_thru1, land1, started1 = _chip_start(
        _pair_reduce(_gw_half(h, pieces, 1, after=g_bias), tails[1], 1), 1)
    send0, recv0, s_thru0, land0, started0 = _chip_start(
        _pair_reduce(_gw_half(h, pieces, 0, after=started1), tails[0], 0), 0)
    grad_x, g_nw = _dh_norm(pieces, offsets, wf, xs, dx2, norm_w, started0)
    small = jnp.concatenate([g_nw, g_fn, _pad_cols(g_bias, D), _pad_cols(g_bn, D), _pad_cols(g_sinks, D),
                             _pad_cols(loss_part, D)], axis=0)
    sm_send, sm_recv, sm_blk, sm_land, sm_started = _late_gather_start(small, g_nw, name="small_gather")
    sums1, got1 = _chip_wait(send1, recv1, s_thru1, land1, sm_started, 1)
    sums0, got0 = _chip_wait(send0, recv0, s_thru0, land0, got1, 0)
    sums, from_chips = [sums0, sums1], [got0, got1]

    ws = dict(norm_w=norm_w, fnw=fnw, bias=b_gate_bias, bn=b_out_norm_w, sinks=a_sinks)
    ms = dict(norm_w=m_norm_w, fnw=m_final_norm_w.reshape(1, D), bias=m_b_gate_bias, bn=m_b_out_norm_w,
              sinks=m_a_sinks)
    vs = dict(norm_w=v_norm_w, fnw=v_final_norm_w.reshape(1, D), bias=v_b_gate_bias, bn=v_b_out_norm_w,
              sinks=v_a_sinks)
    t_rows, t_gu = _finish(
        [_by_lane_tile(w_in), w_a_proj[0], w_b_proj[0], w_out[0]],
        [_by_lane_tile(m_w_in), m_w_a_proj[0], m_w_b_proj[0], m_w_out[0]],
        [_by_lane_tile(v_w_in), v_w_a_proj[0], v_w_b_proj[0], v_w_out[0]],
        b_gate_up[0], m_b_gate_up[0], v_b_gate_up[0], sums, from_chips)
    sm_blk, sm_land = _late_gather_wait(sm_send, sm_recv, sm_blk, sm_land, t_rows[0], t_gu[0], name="small_gather")
    loss, sm = _finish_small(ws, ms, vs, lax.dynamic_update_slice(sm_land, sm_blk[None], (me, 0, 0)))

    def outputs(k):
        return [sm["norm_w"][k], jnp.transpose(t_rows[k], (1, 2, 0)), sm["sinks"][k], t_gu[k][None], sm["bias"][k], sm["bn"][k],
                t_rows[4 + k][None], t_rows[8 + k][None], t_rows[12 + k][None], sm["fnw"][k].reshape(D)]

    return (loss[0, 0], grad_x[None], *outputs(0), *outputs(1), *outputs(2), *outputs(3))
```

```python
import functools

import numpy as np
import jax
import jax.numpy as jnp
from jax import lax
from jax.experimental import pallas as pl
from jax.experimental.pallas import tpu as pltpu

F32 = jnp.float32
MXU = jnp.bfloat16
WIRE = jnp.bfloat16

D = 1024
A_HEADS, A_KV, A_HD = 16, 2, 64
BLK = 128
B_HEADS, B_DK, B_DV = 4, 128, 256
RANK, TAU, CHUNK = 16, 16.0, 64
EPS, NEG = 1e-5, -1e30
ROPE_THETA = 10000.0
IN_WIDTH, NDEV = 7440, 8
SHARD = IN_WIDTH // NDEV
LANE = 128
LANE_TILES = D // LANE


def _by_lane_tile(a):
    return jnp.transpose(a, (2, 0, 1)).reshape(SHARD * LANE_TILES, LANE)


C_Q, C_KD, C_VD, C_BL = 0, 1024, 1280, 1536
C_BV, C_BQ, C_BK = 2048, 3072, 3584
C_AG, C_BG, C_MA, C_MB = 4096, 5120, 6144, 7168
C_GLA, W_GLA, C_GATES, W_GATES = 2048, 2048, 4096, 4096
NF = 8192
W_BL = 128

SHARD_PAD = 944
R_IN, R_A, R_B, R_O, R_GU, ROWS = 0, 944, 1072, 1200, 1328, 1344
SMALL_ROWS = 48

ADAM_LR, ADAM_B1, ADAM_B2, ADAM_EPS, ADAM_WD, ADAM_STEP = 0.001, 0.9, 0.999, 1e-08, 0.01, 10

MESH = pl.DeviceIdType.MESH
VMEM_LIMIT = 56 * 1024 * 1024


def _cp(sem=None, **kw):
    if sem is not None:
        kw["dimension_semantics"] = sem
    return pltpu.CompilerParams(vmem_limit_bytes=VMEM_LIMIT, **kw)


def _dot(a, b):
    return jnp.dot(a, b, preferred_element_type=F32)


def _dot_nt(a, b):
    return lax.dot_general(a, b, (((1,), (1,)), ((), ())), preferred_element_type=F32)


def _dot_tn(a, b):
    return lax.dot_general(a, b, (((0,), (0,)), ((), ())), preferred_element_type=F32)


def _dot_f32(a, b):
    return jnp.dot(a, b, preferred_element_type=F32, precision=lax.Precision.HIGHEST)


def _sigmoid(z):
    return 0.5 * jnp.tanh(0.5 * z) + 0.5


def _rope(xp, cos, sin):
    return xp * cos + pltpu.roll(xp, 64, 1) * sin


def _rope_bwd(dy, cos, sin):
    return dy * cos - pltpu.roll(dy, 64, 1) * sin


def _vmem():
    return pl.BlockSpec(memory_space=pltpu.VMEM)


def _any():
    return pl.BlockSpec(memory_space=pl.ANY)


def _rope_rows():
    half = A_HD // 2
    inv = (np.float32(ROPE_THETA) ** (-np.arange(half, dtype=np.float32) / np.float32(half))).astype(np.float32)
    inv_row = jnp.asarray(np.tile(inv, 4)[None, :])
    sign_row = jnp.asarray(np.concatenate([-np.ones(64, np.float32), np.ones(64, np.float32)])[None, :])
    return inv_row, sign_row


def _prologue_rows(rows, x_ref, nw_ref, pos_ref, inv_ref, sign_ref, h_ref, cos_ref, sin_ref):
    xv = x_ref[rows, :]
    r = lax.rsqrt(jnp.mean(xv * xv, axis=-1, keepdims=True) + EPS)
    h_ref[rows, :] = ((xv * r) * nw_ref[...]).astype(h_ref.dtype)
    ang = pos_ref[rows, :].astype(F32) * inv_ref[...]
    cos_ref[rows, :] = jnp.cos(ang)
    sin_ref[rows, :] = jnp.sin(ang) * sign_ref[...]


def _proj(h, wft, after):
    T = h.shape[0]
    tT, tN = T, 512

    def body(h_ref, w_ref, after_ref, o_ref):
        o_ref[...] = _dot_nt(h_ref[...], w_ref[...])

    return pl.pallas_call(
        body, name="proj", grid=(T // tT, NF // tN),
        in_specs=[pl.BlockSpec((tT, D), lambda i, j: (i, 0)), pl.BlockSpec((tN, D), lambda i, j: (j, 0)), _any()],
        out_specs=pl.BlockSpec((tT, tN), lambda i, j: (i, j)),
        out_shape=jax.ShapeDtypeStruct((T, NF), F32),
        compiler_params=_cp(("parallel", "parallel")),
    )(h, wft, after)


def _swa_masks():
    lane = lax.broadcasted_iota(jnp.int32, (BLK, LANE), 1)
    rope_sub0 = ((lane // 32) % 2) == 0
    std_sub0 = lane < 64
    return lane, rope_sub0, std_sub0


def _swa_tri():
    qi = lax.broadcasted_iota(jnp.int32, (BLK, BLK), 0)
    kj = lax.broadcasted_iota(jnp.int32, (BLK, BLK), 1)
    return kj <= qi


def _swa_fold(full, tri):
    return jnp.where(tri, full[:, BLK:], full[:, :BLK])


def _swa_unfold(sq, tri):
    return jnp.concatenate([jnp.where(tri, 0.0, sq), jnp.where(tri, sq, 0.0)], axis=1)


def _swa_keys(kc_ref, kp_ref, vc_ref, vp_ref, cq, sq, cp, sp):
    def ropek(kref, c, s):
        kv = kref[...]
        return jnp.concatenate([_rope(kv[:, :LANE], c, s), _rope(kv[:, LANE:], c, s)], axis=1)

    K = jnp.concatenate([ropek(kp_ref, cp, sp), ropek(kc_ref, cq, sq)], axis=0).astype(MXU)
    V = jnp.concatenate([vp_ref[...], vc_ref[...]], axis=0).astype(MXU)
    return K, V


def _swa_in_specs(nb, last):
    def cur(n):
        return jnp.minimum(n, last)

    def prev(n):
        return jnp.maximum(cur(n) - 1, 0)

    kd, vd = C_KD // 256, C_VD // 256
    return [
        pl.BlockSpec((BLK, D), lambda n: (cur(n), C_Q // D)),
        pl.BlockSpec((BLK, 256), lambda n: (cur(n), kd)),
        pl.BlockSpec((BLK, 256), lambda n: (prev(n), kd)),
        pl.BlockSpec((BLK, 256), lambda n: (cur(n), vd)),
        pl.BlockSpec((BLK, 256), lambda n: (prev(n), vd)),
        pl.BlockSpec((BLK, LANE), lambda n: (cur(n), 0)),
        pl.BlockSpec((BLK, LANE), lambda n: (cur(n), 0)),
        pl.BlockSpec((BLK, LANE), lambda n: (prev(n), 0)),
        pl.BlockSpec((BLK, LANE), lambda n: (prev(n), 0)),
    ]


def _swa_fwd(proj, cos, sin, sinks):
    T = proj.shape[0]
    nb = T // BLK
    scale = A_HD ** -0.5

    def body(sinks_ref, q_ref, kc_ref, kp_ref, vc_ref, vp_ref, cq_ref, sq_ref, cp_ref, sp_ref, o_ref, l_ref):
        n = pl.program_id(0)
        cq, sq = cq_ref[...], sq_ref[...]
        K, V = _swa_keys(kc_ref, kp_ref, vc_ref, vp_ref, cq, sq, cp_ref[...], sp_ref[...])
        tri = _swa_tri()
        valid = tri | (n > 0)
        lane, rope_sub0, std_sub0 = _swa_masks()
        group = A_HEADS // A_KV
        roped, lses = {}, []

        def products(head):
            pb, sub, g = head // 2, head % 2, head // group
            if sub == 0:
                roped[pb] = _rope(q_ref[:, pb * LANE:(pb + 1) * LANE], cq, sq)
            qm = jnp.where(rope_sub0 if sub == 0 else ~rope_sub0, roped[pb], 0.0).astype(MXU)
            return _dot_nt(qm, K[:, g * LANE:(g + 1) * LANE])

        def softmax(head, s_full):
            s = jnp.where(valid, _swa_fold(s_full, tri) * scale, NEG)
            sink = sinks_ref[0, head]
            m = jnp.maximum(jnp.max(s, axis=1, keepdims=True), sink)
            e = jnp.exp(s - m)
            den = jnp.sum(e, axis=1, keepdims=True) + jnp.exp(sink - m)
            lses.append(m + jnp.log(den))
            return _swa_unfold(e / den, tri).astype(MXU)

        outs = {}
        st1 = {0: products(0), 1: products(1)}
        st2 = {0: softmax(0, st1.pop(0))}
        for head in range(A_HEADS):
            if head + 2 < A_HEADS:
                st1[head + 2] = products(head + 2)
            if head + 1 < A_HEADS:
                st2[head + 1] = softmax(head + 1, st1.pop(head + 1))
            g = head // group
            outs[head] = _dot(st2.pop(head), V[:, g * LANE:(g + 1) * LANE])
            if head % 2 == 1:
                pb = head // 2
                o_ref[:, pb * LANE:(pb + 1) * LANE] = jnp.where(std_sub0, outs[head - 1], outs[head])
        lacc = jnp.zeros((BLK, LANE), F32)
        for head in range(A_HEADS):
            lacc = jnp.where(lane == head, lses[head], lacc)
        l_ref[...] = lacc

    return pl.pallas_call(
        body, name="swa_fwd", grid=(nb,),
        in_specs=[pl.BlockSpec(memory_space=pltpu.SMEM)] + _swa_in_specs(nb, nb - 1),
        out_specs=[pl.BlockSpec((BLK, D), lambda n: (n, 0)), pl.BlockSpec((BLK, LANE), lambda n: (n, 0))],
        out_shape=[jax.ShapeDtypeStruct((T, D), F32), jax.ShapeDtypeStruct((T, LANE), F32)],
        compiler_params=_cp(("parallel",)),
    )(sinks, proj, proj, proj, proj, proj, cos, sin, cos, sin)


def _swa_bwd(proj, cos, sin, sinks, do_a, o_a, lse, after):
    T = proj.shape[0]
    nb = T // BLK
    scale = A_HD ** -0.5

    def body(sinks_ref, q_ref, kc_ref, kp_ref, vc_ref, vp_ref, cq_ref, sq_ref, cp_ref, sp_ref,
             do_ref, o_ref, l_ref, after_ref, dq_ref, dkv_ref, ds_ref, ckv_ref):
        n = pl.program_id(0)

        @pl.when(n == 0)
        def _():
            ckv_ref[...] = jnp.zeros_like(ckv_ref)
            ds_ref[...] = jnp.zeros_like(ds_ref)

        @pl.when(n < nb)
        def _():
            cq, sq, cp, sp = cq_ref[...], sq_ref[...], cp_ref[...], sp_ref[...]
            K, V = _swa_keys(kc_ref, kp_ref, vc_ref, vp_ref, cq, sq, cp, sp)
            tri = _swa_tri()
            valid = tri | (n > 0)
            lane, rope_sub0, std_sub0 = _swa_masks()
            lane_row = lax.broadcasted_iota(jnp.int32, (1, LANE), 1)
            lse_v = l_ref[...]
            dKt = [jnp.zeros((LANE, 2 * BLK), F32) for _ in range(A_KV)]
            dVt = [jnp.zeros((LANE, 2 * BLK), F32) for _ in range(A_KV)]
            dsinks, roped, roped_t, do_t = [], {}, {}, {}
            group = A_HEADS // A_KV
            dim = lax.broadcasted_iota(jnp.int32, (LANE, BLK), 0)
            rope_row0, std_row0 = ((dim // 32) % 2) == 0, dim < 64

            def products(head):
                pb, sub, g = head // 2, head % 2, head // group
                cols = slice(pb * LANE, (pb + 1) * LANE)
                Kg, Vg = K[:, g * LANE:(g + 1) * LANE], V[:, g * LANE:(g + 1) * LANE]
                if sub == 0:
                    roped[pb] = _rope(q_ref[:, cols], cq, sq)
                    roped_t[pb] = roped[pb].T
                    do_t[pb] = do_ref[:, cols].T
                qm = jnp.where(rope_sub0 if sub == 0 else ~rope_sub0, roped[pb], 0.0).astype(MXU)
                qmt = jnp.where(rope_row0 if sub == 0 else ~rope_row0, roped_t[pb], 0.0).astype(MXU)
                dov = jnp.where(std_sub0 if sub == 0 else ~std_sub0, do_ref[:, cols], 0.0)
                dovt = jnp.where(std_row0 if sub == 0 else ~std_row0, do_t[pb], 0.0).astype(MXU)
                delta = jnp.sum(dov * o_ref[:, cols], axis=1, keepdims=True)
                return qmt, dovt, delta, _dot_nt(qm, Kg), _dot_nt(dov.astype(MXU), Vg)

            def scores(head, qmt, dovt, delta, s_full, dp_full):
                lh = jnp.sum(jnp.where(lane == head, lse_v, 0.0), axis=1, keepdims=True)
                p = jnp.where(valid, jnp.exp(_swa_fold(s_full, tri) * scale - lh), 0.0)
                psink = jnp.exp(sinks_ref[0, head] - lh)
                dsinks.append(jnp.sum(-psink * delta, axis=0, keepdims=True))
                dsq = (p * (_swa_fold(dp_full, tri) - delta)) * scale
                return qmt, dovt, _swa_unfold(p, tri).astype(MXU), _swa_unfold(dsq, tri).astype(MXU)

            def grads(head, qmt, dovt, pb16, dsc):
                g = head // group
                dKt[g] = dKt[g] + _dot(qmt, dsc)
                dVt[g] = dVt[g] + _dot(dovt, pb16)
                return _dot(dsc, K[:, g * LANE:(g + 1) * LANE])

            dqs = {}
            st1 = {0: products(0), 1: products(1)}
            st2 = {0: scores(0, *st1.pop(0))}
            for head in range(A_HEADS):
                if head + 2 < A_HEADS:
                    st1[head + 2] = products(head + 2)
                if head + 1 < A_HEADS:
                    st2[head + 1] = scores(head + 1, *st1.pop(head + 1))
                dqs[head] = grads(head, *st2.pop(head))
                if head % 2 == 1:
                    pb = head // 2
                    dqp = jnp.where(rope_sub0, dqs[head - 1], dqs[head])
                    dq_ref[:, pb * LANE:(pb + 1) * LANE] = _rope_bwd(dqp, cq, sq).astype(dq_ref.dtype)
            dsink = jnp.zeros((1, LANE), F32)
            for head in range(A_HEADS):
                dsink = jnp.where(lane_row == head, dsinks[head], dsink)
            dK, dV = [a.T for a in dKt], [a.T for a in dVt]
            prev = ([_rope_bwd(dK[g][:BLK], cp, sp) for g in range(A_KV)] + [dV[g][:BLK] for g in range(A_KV)])
            cur_ = ([_rope_bwd(dK[g][BLK:], cq, sq) for g in range(A_KV)] + [dV[g][BLK:] for g in range(A_KV)])
            dkv_ref[...] = (ckv_ref[...] + jnp.concatenate(prev, axis=1)).astype(dkv_ref.dtype)
            ckv_ref[...] = jnp.concatenate(cur_, axis=1)
            ds_ref[...] = ds_ref[...] + jnp.broadcast_to(dsink, ds_ref.shape)

        @pl.when(n == nb)
        def _():
            dkv_ref[...] = ckv_ref[...].astype(dkv_ref.dtype)

    last = nb - 1

    def cur(n):
        return jnp.minimum(n, last)

    def out_kv(n):
        return (jnp.maximum(n - 1, 0), 0)

    return pl.pallas_call(
        body, name="swa_bwd", grid=(nb + 1,),
        in_specs=[pl.BlockSpec(memory_space=pltpu.SMEM)] + _swa_in_specs(nb, last) + [
            pl.BlockSpec((BLK, D), lambda n: (cur(n), 0)),
            pl.BlockSpec((BLK, D), lambda n: (cur(n), 0)),
            pl.BlockSpec((BLK, LANE), lambda n: (cur(n), 0)),
            _any(),
        ],
        out_specs=[
            pl.BlockSpec((BLK, D), lambda n: (cur(n), 0)),
            pl.BlockSpec((BLK, 512), out_kv),
            pl.BlockSpec((8, LANE), lambda n: (0, 0)),
        ],
        out_shape=[
            jax.ShapeDtypeStruct((T, D), MXU),
            jax.ShapeDtypeStruct((T, 512), MXU),
            jax.ShapeDtypeStruct((8, LANE), F32),
        ],
        scratch_shapes=[pltpu.VMEM((BLK, 512), F32)],
        compiler_params=_cp(("arbitrary",)),
    )(sinks, proj, proj, proj, proj, proj, cos, sin, cos, sin, do_a, o_a, lse, after)


NCH = 4
GSTEP = NCH * CHUNK
ST_ROWS = B_HEADS * B_DV


def _chunk_rows(c):
    return slice(c * CHUNK, (c + 1) * CHUNK)


def _per_chunk(which, vals):
    out = vals[-1]
    for c in range(NCH - 2, -1, -1):
        out = jnp.where(which == c, vals[c], out)
    return out


def _gla_gate(bl_ref, gu_ref, bias_ref):
    gk = _dot(bl_ref[...].astype(MXU), gu_ref[...]) + bias_ref[...]
    la = (jnp.minimum(gk, 0.0) - jnp.log(1.0 + jnp.exp(-jnp.abs(gk)))) / TAU
    ri = lax.broadcasted_iota(jnp.int32, (GSTEP, GSTEP), 0)
    ci = lax.broadcasted_iota(jnp.int32, (GSTEP, GSTEP), 1)
    same = (ri // CHUNK) == (ci // CHUNK)
    lower, upper = same & (ci <= ri), same & (ci >= ri)
    b = _dot_f32(jnp.where(lower, 1.0, 0.0).astype(F32), la)
    which = lax.broadcasted_iota(jnp.int32, (GSTEP, 1), 0) // CHUNK
    return gk, la, b, lower, upper, which


def _gla_head(q_ref, k_ref, la, b, which, h):
    sl = slice(h * B_DK, (h + 1) * B_DK)
    bh, lah = b[:, sl], la[:, sl]
    bls = [jnp.sum(lah[_chunk_rows(c)], axis=0, keepdims=True) for c in range(NCH)]
    blast = _per_chunk(which, bls)
    qc = q_ref[:, sl] * (B_DK ** -0.5)
    kh = k_ref[:, sl]
    eb, enb, esb = jnp.exp(bh), jnp.exp(-bh), jnp.exp(blast - bh)
    return qc * eb, kh * enb, kh * esb, eb, enb, esb, [jnp.exp(v) for v in bls]


def _gla_specs(step_of):
    return [
        pl.BlockSpec((GSTEP, 512), lambda i: (step_of(i), C_BQ // 512)),
        pl.BlockSpec((GSTEP, 512), lambda i: (step_of(i), C_BK // 512)),
        pl.BlockSpec((GSTEP, D), lambda i: (step_of(i), C_BV // D)),
        pl.BlockSpec((GSTEP, W_BL), lambda i: (step_of(i), C_BL // W_BL)),
        pl.BlockSpec((W_BL, 512), lambda i: (0, 0)),
        pl.BlockSpec((1, 512), lambda i: (0, 0)),
    ]


def _gla_fwd(proj, gu_pad, bias):
    T = proj.shape[0]
    ns = T // GSTEP

    def body(q_ref, k_ref, v_ref, bl_ref, gu_ref, bias_ref, o_ref, st_ref, state_ref):
        @pl.when(pl.program_id(0) == 0)
        def _():
            state_ref[...] = jnp.zeros_like(state_ref)

        _, la, b, lower, _, which = _gla_gate(bl_ref, gu_ref, bias_ref)

        def within(h):
            q_e, k_e, k_s, _, _, _, decays = _gla_head(q_ref, k_ref, la, b, which, h)
            vh = v_ref[:, h * B_DV:(h + 1) * B_DV].astype(MXU)
            q_eb = q_e.astype(MXU)
            att = jnp.where(lower, _dot_nt(q_eb, k_e.astype(MXU)), 0.0)
            return vh, q_eb, k_s.astype(MXU), _dot(att.astype(MXU), vh), decays

        def across(h, vh, q_eb, k_sb, o_intra, decays):
            rows = slice(h * B_DV, (h + 1) * B_DV)
            s = state_ref[rows, :]
            outs = []
            for c in range(NCH):
                cr = _chunk_rows(c)
                st_ref[c * ST_ROWS + h * B_DV:c * ST_ROWS + (h + 1) * B_DV, :] = s
                outs.append(o_intra[cr] + _dot_nt(q_eb[cr], s.astype(MXU)))
                s = s * decays[c] + _dot_tn(vh[cr], k_sb[cr])
            state_ref[rows, :] = s
            o_ref[:, rows] = jnp.concatenate(outs, axis=0)

        for h in range(B_HEADS):
            across(h, *within(h))

    return pl.pallas_call(
        body, name="gla_fwd", grid=(ns,),
        in_specs=_gla_specs(lambda i: i),
        out_specs=[pl.BlockSpec((GSTEP, D), lambda i: (i, 0)),
                   pl.BlockSpec((NCH * ST_ROWS, B_DK), lambda i: (i, 0))],
        out_shape=[jax.ShapeDtypeStruct((T, D), F32),
                   jax.ShapeDtypeStruct((ns * NCH * ST_ROWS, B_DK), F32)],
        scratch_shapes=[pltpu.VMEM((ST_ROWS, B_DK), F32)],
        compiler_params=_cp(("arbitrary",)),
    )(proj, proj, proj, proj, gu_pad, bias)


def _gla_bwd(proj, gu_pad, bias, states, do_b):
    T = proj.shape[0]
    ns = T // GSTEP
    o_q, o_k = C_BQ - C_GLA, C_BK - C_GLA

    def body(q_ref, k_ref, v_ref, bl_ref, gu_ref, bias_ref, st_ref, do_ref,
             dg_ref, dbl_ref, ggu_ref, gbias_ref, gt_ref):
        @pl.when(pl.program_id(0) == 0)
        def _():
            gt_ref[...] = jnp.zeros_like(gt_ref)
            ggu_ref[...] = jnp.zeros_like(ggu_ref)
            gbias_ref[...] = jnp.zeros_like(gbias_ref)

        gk, la, b, lower, upper_mask, which = _gla_gate(bl_ref, gu_ref, bias_ref)
        upper = jnp.where(upper_mask, 1.0, 0.0).astype(F32)
        dla_parts = []

        def within(h):
            q_e, k_e, k_s, eb, enb, esb, decays = _gla_head(q_ref, k_ref, la, b, which, h)
            vh = v_ref[:, h * B_DV:(h + 1) * B_DV].astype(MXU)
            doh = do_ref[:, h * B_DV:(h + 1) * B_DV].astype(MXU)
            q_eb, k_eb = q_e.astype(MXU), k_e.astype(MXU)
            att = jnp.where(lower, _dot_nt(q_eb, k_eb), 0.0).astype(MXU)
            datt = jnp.where(lower, _dot_nt(doh, vh), 0.0).astype(MXU)
            return (q_e, k_e, k_s, eb, enb, esb, decays, vh, doh, q_eb, k_s.astype(MXU),
                    _dot(datt, k_eb), _dot_tn(datt, q_eb), _dot_tn(att, doh))

        def across(h, q_e, k_e, k_s, eb, enb, esb, decays, vh, doh, q_eb, k_sb, dq_i, dk_e, dv_i):
            rows = slice(h * B_DV, (h + 1) * B_DV)
            g = gt_ref[rows, :]
            dq_c, dks_c, dv_c, ddec = [None] * NCH, [None] * NCH, [None] * NCH, [None] * NCH
            for c in range(NCH - 1, -1, -1):
                cr = _chunk_rows(c)
                s = st_ref[c * ST_ROWS + h * B_DV:c * ST_ROWS + (h + 1) * B_DV, :]
                gb = g.astype(MXU)
                dq_c[c] = dq_i[cr] + _dot(doh[cr], s.astype(MXU))
                dks_c[c] = _dot(vh[cr], gb)
                dv_c[c] = dv_i[cr] + _dot_nt(k_sb[cr], gb)
                ddec[c] = jnp.sum(g * s, axis=0, keepdims=True)
                g = g * decays[c] + _dot_tn(doh[cr], q_eb[cr])
            gt_ref[rows, :] = g
            dq_e = jnp.concatenate(dq_c, axis=0)
            dk_s = jnp.concatenate(dks_c, axis=0)
            dg_ref[:, rows] = jnp.concatenate(dv_c, axis=0).astype(dg_ref.dtype)
            dg_ref[:, o_q + h * B_DK:o_q + (h + 1) * B_DK] = (dq_e * eb * (B_DK ** -0.5)).astype(dg_ref.dtype)
            dg_ref[:, o_k + h * B_DK:o_k + (h + 1) * B_DK] = (dk_e * enb + dk_s * esb).astype(dg_ref.dtype)
            dks_ks = dk_s * k_s
            db = dq_e * q_e - dk_e * k_e - dks_ks
            dbl = [jnp.sum(dks_ks[_chunk_rows(c)], axis=0, keepdims=True) + ddec[c] * decays[c] for c in range(NCH)]
            dla_parts.append(_dot_f32(upper, db) + _per_chunk(which, dbl))

        for h in range(B_HEADS):
            across(h, *within(h))
        dla = jnp.concatenate(dla_parts, axis=1)
        dgk = dla * (1.0 / TAU) * _sigmoid(-gk)
        dgkb = dgk.astype(MXU)
        dbl_ref[...] = _dot_nt(dgkb, gu_ref[...]).astype(dbl_ref.dtype)
        ggu_ref[...] = ggu_ref[...] + _dot_tn(bl_ref[...].astype(MXU), dgkb)
        gbias_ref[...] = gbias_ref[...] + jnp.broadcast_to(jnp.sum(dgk, axis=0, keepdims=True), gbias_ref.shape)

    def rev(i):
        return ns - 1 - i

    return pl.pallas_call(
        body, name="gla_bwd", grid=(ns,),
        in_specs=_gla_specs(rev) + [
            pl.BlockSpec((NCH * ST_ROWS, B_DK), lambda i: (rev(i), 0)),
            pl.BlockSpec((GSTEP, D), lambda i: (rev(i), 0)),
        ],
        out_specs=[
            pl.BlockSpec((GSTEP, W_GLA), lambda i: (rev(i), 0)),
            pl.BlockSpec((GSTEP, W_BL), lambda i: (rev(i), 0)),
            pl.BlockSpec((W_BL, 512), lambda i: (0, 0)),
            pl.BlockSpec((8, 512), lambda i: (0, 0)),
        ],
        out_shape=[
            jax.ShapeDtypeStruct((T, W_GLA), MXU),
            jax.ShapeDtypeStruct((T, W_BL), MXU),
            jax.ShapeDtypeStruct((W_BL, 512), F32),
            jax.ShapeDtypeStruct((8, 512), F32),
        ],
        scratch_shapes=[pltpu.VMEM((B_HEADS * B_DV, B_DK), F32)],
        compiler_params=_cp(("arbitrary",)),
    )(proj, proj, proj, proj, gu_pad, bias, states, do_b)


def _mid(x, target, proj, o_a, o_b, late, w_bn4, fnw):
    T = x.shape[0]
    tT = min(T, 128)
    nbuf = 4
    o_ag, o_bg, o_ma, o_mb = (c - C_GATES for c in (C_AG, C_BG, C_MA, C_MB))

    def body(x_ref, t_ref, oa_ref, ob_ref, gates_ref, late_ref, wbn_ref, fnw_ref,
             dx2_ref, doa_ref, dob_ref, dgates_ref,
             tail0_ref, tail1_ref, gfn_ref, gbn_ref, loss_ref, buf_ref, gw_ref):
        i = pl.program_id(0)

        def weight(p):
            return late_ref[:, 128 * p:128 * (p + 1), :].reshape(D, D)

        @pl.when(i == 0)
        def _():
            for r in (gw_ref, gfn_ref, gbn_ref, loss_ref):
                r[...] = jnp.zeros_like(r)

        rows = pl.ds(pl.multiple_of((i % nbuf) * tT, tT), tT)

        def keep(k, val):
            buf_ref[k, rows, :] = val

        oa, ag = oa_ref[...], gates_ref[:, o_ag:o_ag + D]
        sg_a = _sigmoid(ag)
        silu_a = ag * sg_a
        oag_b = (oa * silu_a).astype(MXU)
        keep(0, oag_b)
        y_a = _dot(oag_b, weight(0))

        ob, bg = ob_ref[...], gates_ref[:, o_bg:o_bg + D]
        rbs, obhats = [], []
        for h in range(B_HEADS):
            obh = ob[:, h * B_DV:(h + 1) * B_DV]
            rb = lax.rsqrt(jnp.mean(obh * obh, axis=-1, keepdims=True) + EPS)
            rbs.append(rb)
            obhats.append(obh * rb)
        obhat = jnp.concatenate(obhats, axis=1)
        wbn = wbn_ref[...]
        obn = obhat * wbn
        sg_b = _sigmoid(bg)
        silu_b = bg * sg_b
        obg_b = (obn * silu_b).astype(MXU)
        keep(1, obg_b)
        y_b = _dot(obg_b, weight(1))

        sa, sb = _sigmoid(gates_ref[:, o_ma:o_ma + D]), _sigmoid(gates_ref[:, o_mb:o_mb + D])
        mg_b = (sa * y_a + sb * y_b).astype(MXU)
        keep(2, mg_b)
        x2 = x_ref[...] + _dot(mg_b, weight(2))
        r2 = lax.rsqrt(jnp.mean(x2 * x2, axis=-1, keepdims=True) + EPS)
        xh2 = x2 * r2
        fw = fnw_ref[...]
        err = xh2 * fw - t_ref[...]
        tok = jnp.mean(err * err, axis=-1, keepdims=True)
        loss_ref[...] = loss_ref[...] + 0.5 * jnp.sum(tok, axis=0, keepdims=True)

        dy = err * (1.0 / D)
        gfn_ref[...] = gfn_ref[...] + jnp.broadcast_to(jnp.sum(dy * xh2, axis=0, keepdims=True), gfn_ref.shape)
        gy = dy * fw
        dx2 = r2 * (gy - xh2 * jnp.mean(gy * xh2, axis=-1, keepdims=True))
        dx2_ref[...] = dx2
        dx2_b = dx2.astype(MXU)
        keep(5, dx2_b)
        dmg = _dot_nt(dx2_b, weight(2))

        dgates_ref[:, o_ma:o_ma + D] = (dmg * y_a * sa * (1.0 - sa)).astype(dgates_ref.dtype)
        dgates_ref[:, o_mb:o_mb + D] = (dmg * y_b * sb * (1.0 - sb)).astype(dgates_ref.dtype)
        dya_b = (dmg * sa).astype(MXU)
        dyb_b = (dmg * sb).astype(MXU)
        keep(3, dya_b)
        keep(4, dyb_b)
        doag = _dot_nt(dya_b, weight(0))
        dobg = _dot_nt(dyb_b, weight(1))

        @pl.when(i % nbuf == nbuf - 1)
        def _():
            for p in range(3):
                gw_ref[p] = gw_ref[p] + _dot_tn(buf_ref[p], buf_ref[3 + p])

        @pl.when(i == pl.num_programs(0) - 1)
        def _():
            for hf, tail_ref in enumerate((tail0_ref, tail1_ref)):
                for d in range(NDEV):
                    for p in range(3):
                        tail_ref[d, 128 * p:128 * (p + 1), :] = (
                            gw_ref[p, 128 * d:128 * (d + 1), hf * DH:(hf + 1) * DH].astype(tail_ref.dtype))

        doa_ref[...] = doag * silu_a
        dgates_ref[:, o_ag:o_ag + D] = (doag * oa * (sg_a * (1.0 + ag * (1.0 - sg_a)))).astype(dgates_ref.dtype)
        dobn = dobg * silu_b
        dgates_ref[:, o_bg:o_bg + D] = (dobg * obn * (sg_b * (1.0 + bg * (1.0 - sg_b)))).astype(dgates_ref.dtype)
        gg = dobn * wbn
        gbn = jnp.zeros((1, B_DV), F32)
        for h in range(B_HEADS):
            sl = slice(h * B_DV, (h + 1) * B_DV)
            gbn = gbn + jnp.sum(dobn[:, sl] * obhats[h], axis=0, keepdims=True)
            ggh = gg[:, sl]
            dob_ref[:, sl] = rbs[h] * (ggh - obhats[h] * jnp.mean(ggh * obhats[h], axis=-1, keepdims=True))
        gbn_ref[...] = gbn_ref[...] + jnp.broadcast_to(gbn, gbn_ref.shape)

    assert (T // tT) % nbuf == 0
    tile = pl.BlockSpec((tT, D), lambda i: (i, 0))
    row = pl.BlockSpec((1, D), lambda i: (0, 0))
    acc8 = pl.BlockSpec((8, D), lambda i: (0, 0))
    return pl.pallas_call(
        body, name="mid", grid=(T // tT,),
        in_specs=[tile, tile, tile, tile, pl.BlockSpec((tT, W_GATES), lambda i: (i, C_GATES // W_GATES)),
                  _vmem(), row, row],
        out_specs=[tile, tile, tile, pl.BlockSpec((tT, W_GATES), lambda i: (i, 0)), _vmem(), _vmem(),
                   acc8, pl.BlockSpec((8, B_DV), lambda i: (0, 0)), pl.BlockSpec((8, LANE), lambda i: (0, 0))],
        out_shape=[
            jax.ShapeDtypeStruct((T, D), F32),
            jax.ShapeDtypeStruct((T, D), F32),
            jax.ShapeDtypeStruct((T, D), F32),
            jax.ShapeDtypeStruct((T, W_GATES), MXU),
            jax.ShapeDtypeStruct((NDEV, 384, DH), WIRE),
            jax.ShapeDtypeStruct((NDEV, 384, DH), WIRE),
            jax.ShapeDtypeStruct((8, D), F32),
            jax.ShapeDtypeStruct((8, B_DV), F32),
            jax.ShapeDtypeStruct((8, LANE), F32),
        ],
        scratch_shapes=[pltpu.VMEM((6, nbuf * tT, D), MXU), pltpu.VMEM((3, D, D), F32)],
        compiler_params=_cp(("arbitrary",)),
    )(x, target, o_a, o_b, proj, late, w_bn4, fnw)


DH = D // 2


_GW_TILES = (("q", 0, 512, 0), ("q", 1, 512, 512), ("kv", 0, 256, 1024), ("bl", 0, RANK, 5376),
             ("gla", 0, 512, 3328), ("gla", 1, 512, 3840), ("gla", 2, 512, 2304), ("gla", 3, 512, 2816),
             ("gates", 0, 512, 1280), ("gates", 1, 512, 1792), ("gates", 2, 512, 4352), ("gates", 3, 512, 4864),
             ("gates", 4, 512, 5392), ("gates", 5, 512, 5904), ("gates", 6, 512, 6416), ("gates", 7, 512, 6928))


def _gw_unpermute(piece, t):
    if piece == "q":
        parts = []
        for blk in range(t.shape[0] // LANE):
            g = [t[blk * LANE + 32 * i:blk * LANE + 32 * (i + 1)] for i in range(4)]
            parts += [g[0], g[2], g[1], g[3]]
        return jnp.concatenate(parts, axis=0)
    if piece == "kv":
        k = [t[64 * i:64 * i + 32] + t[64 * i + 32:64 * i + 64] for i in range(4)]
        v = [t[256 + 128 * g:256 + 128 * g + 64] + t[256 + 128 * g + 64:256 + 128 * (g + 1)] for g in range(2)]
        return jnp.concatenate(k + v, axis=0)
    if piece == "bl":
        return t[:RANK]
    return t


def _gw_half(h, pieces, half, after=None):
    T = h.shape[0]
    steps = len(_GW_TILES)

    def body(*refs):
        h_ref = refs[0]
        srcs = dict(zip(("q", "kv", "bl", "gla", "gates"), refs[1:6]))
        o_ref, stage, sems = refs[-3:]
        j = pl.program_id(0)

        def out_copy(k):
            _, _, n, off = _GW_TILES[k]
            return pltpu.make_async_copy(stage.at[k % 2, 0:n], o_ref.at[pl.ds(off, n)], sems.at[k % 2])

        for k, (piece, _, n, _) in enumerate(_GW_TILES):
            @pl.when(j == k)
            def _(k=k, piece=piece, n=n):
                if k >= 2:
                    out_copy(k - 2).wait()
                t = _gw_unpermute(piece, _dot_tn(srcs[piece][...], h_ref[...]))
                stage[k % 2, 0:n, :] = t.astype(stage.dtype)
                out_copy(k).start()

        @pl.when(j == steps - 1)
        def _():
            out_copy(steps - 2).wait()
            out_copy(steps - 1).wait()

    def tile_of(lo, hi):
        return lambda j: (0, jnp.clip(j - lo, 0, hi - lo - 1))

    in_specs = [pl.BlockSpec((T, DH), lambda j: (0, half)),
                pl.BlockSpec((T, 512), tile_of(0, 2)), pl.BlockSpec((T, 512), lambda j: (0, 0)),
                pl.BlockSpec((T, W_BL), lambda j: (0, 0)),
                pl.BlockSpec((T, 512), tile_of(4, 8)), pl.BlockSpec((T, 512), tile_of(8, 16))]
    args = [h, *pieces]
    if after is not None:
        in_specs.append(_any())
        args.append(after)
    return pl.pallas_call(
        body, name=f"gw_in_half{half}", grid=(steps,),
        in_specs=in_specs, out_specs=_any(),
        out_shape=jax.ShapeDtypeStruct((IN_WIDTH, DH), WIRE),
        scratch_shapes=[pltpu.VMEM((2, 512, DH), WIRE), pltpu.SemaphoreType.DMA((2,))],
        compiler_params=_cp(("arbitrary",)),
    )(*args)


def _chip_copies(s_ref, got_ref, send_sems, recv_sems):
    x, y, c = _place()
    chips = [(1 - x, y), (x, 1 - y), (1 - x, 1 - y)]
    return [pltpu.make_async_remote_copy(
        src_ref=s_ref.at[2 * px + py], dst_ref=got_ref.at[j],
        send_sem=send_sems.at[j], recv_sem=recv_sems.at[j], device_id=(px, py, c), device_id_type=MESH)
        for j, (px, py) in enumerate(chips)]


_EFFECT = pltpu.SideEffectType.DATAFLOW_SIDE_EFFECTING


def _hbm():
    return pl.BlockSpec(memory_space=pltpu.HBM)


def _sem():
    return pl.BlockSpec(memory_space=pltpu.SEMAPHORE)


def _chip_start(sums, half):
    land = pltpu.with_memory_space_constraint(lax.empty((3,) + sums.shape[1:], sums.dtype), pltpu.HBM)

    def body(s_ref, land_ref, send_sems, recv_sems, s_thru, land_thru, token):
        for cp in _chip_copies(s_ref, land_ref, send_sems, recv_sems):
            cp.start()
        token[...] = jnp.zeros_like(token)

    return pl.pallas_call(
        body, name=f"chip_start{half}",
        out_shape=(pltpu.SemaphoreType.DMA((3,)), pltpu.SemaphoreType.DMA((3,)),
                   pltpu.HBM(sums.shape, sums.dtype), pltpu.HBM(land.shape, land.dtype),
                   jax.ShapeDtypeStruct((8, LANE), F32)),
        in_specs=(_hbm(), _hbm()), out_specs=(_sem(), _sem(), _hbm(), _hbm(), _vmem()),
        input_output_aliases={0: 2, 1: 3},
        compiler_params=pltpu.CompilerParams(has_side_effects=_EFFECT),
    )(pltpu.with_memory_space_constraint(sums, pltpu.HBM), land)


def _chip_wait(send_sems, recv_sems, s_thru, land_thru, after, half):
    def body(s_ref, land_ref, send_sems, recv_sems, after_ref, s_out, got_ref):
        copies = _chip_copies(s_ref, land_ref, send_sems, recv_sems)
        for cp in copies:
            cp.wait_send()
        for cp in copies:
            cp.wait_recv()

    return pl.pallas_call(
        body, name=f"chip_wait{half}",
        out_shape=(pltpu.HBM(s_thru.shape, s_thru.dtype), pltpu.HBM(land_thru.shape, land_thru.dtype)),
        in_specs=(_hbm(), _hbm(), _sem(), _sem(), _any()), out_specs=(_hbm(), _hbm()),
        input_output_aliases={0: 0, 1: 1},
        compiler_params=pltpu.CompilerParams(has_side_effects=_EFFECT),
    )(s_thru, land_thru, send_sems, recv_sems, after)


def _dh_norm(pieces, offsets, wf, x, dx2, norm_w, after):
    T = x.shape[0]
    tT = min(T, 256)
    widths = [p.shape[1] for p in pieces]
    npc = len(pieces)

    def body(*refs):
        dp_refs = refs[:npc]
        wf_ref, x_ref, dx2_ref, nw_ref, _, gx_ref, gnw_ref = refs[npc:]

        @pl.when(pl.program_id(0) == 0)
        def _():
            gnw_ref[...] = jnp.zeros_like(gnw_ref)

        dh = jnp.zeros((tT, D), F32)
        for dp_ref, off, w in zip(dp_refs, offsets, widths):
            dh = dh + _dot(dp_ref[...], wf_ref[off:off + w, :])
        xv = x_ref[...]
        r = lax.rsqrt(jnp.mean(xv * xv, axis=-1, keepdims=True) + EPS)
        xh = xv * r
        gnw_ref[...] = gnw_ref[...] + jnp.broadcast_to(jnp.sum(dh * xh, axis=0, keepdims=True), gnw_ref.shape)
        g = dh * nw_ref[...]
        gx_ref[...] = r * (g - xh * jnp.mean(g * xh, axis=-1, keepdims=True)) + dx2_ref[...]

    tile = pl.BlockSpec((tT, D), lambda i: (i, 0))
    return pl.pallas_call(
        body, name="dh_norm", grid=(T // tT,),
        in_specs=[pl.BlockSpec((tT, w), lambda i: (i, 0)) for w in widths]
        + [_vmem(), tile, tile, pl.BlockSpec((1, D), lambda i: (0, 0)), _any()],
        out_specs=[tile, pl.BlockSpec((8, D), lambda i: (0, 0))],
        out_shape=[jax.ShapeDtypeStruct((T, D), F32), jax.ShapeDtypeStruct((8, D), F32)],
        compiler_params=_cp(("arbitrary",)),
    )(*pieces, wf, x, dx2, norm_w, after)


def _adamw_math(w, g, m, v):
    m = ADAM_B1 * m + (1.0 - ADAM_B1) * g
    v = ADAM_B2 * v + (1.0 - ADAM_B2) * (g * g)
    m_hat = m * (1.0 / (1.0 - ADAM_B1 ** ADAM_STEP))
    v_hat = v * (1.0 / (1.0 - ADAM_B2 ** ADAM_STEP))
    delta = -ADAM_LR * (m_hat / (jnp.sqrt(v_hat) + ADAM_EPS) + ADAM_WD * w)
    return delta, m, v


def _fetch_partials(s_ref, got_ref, buf, sems):
    x, y, _ = _place()
    cps = [pltpu.make_async_copy(s_ref.at[2 * x + y], buf.at[0], sems.at[0])]
    cps += [pltpu.make_async_copy(got_ref.at[j], buf.at[1 + j], sems.at[1 + j]) for j in range(3)]
    for cp in cps:
        cp.start()
    for cp in cps:
        cp.wait()


SMALL_AT = dict(norm_w=0, fnw=8, bias=16, bn=24, sinks=32, loss=40)
ROW_AT = (R_IN, R_A, R_B, R_O)


def _finish_small(ws, ms, vs, smalls):
    names = ["norm_w", "fnw", "bias", "bn", "sinks"]
    widths = [ws[n].shape[1] for n in names]

    def body(*refs):
        w_refs, m_refs, v_refs = refs[0:5], refs[5:10], refs[10:15]
        smalls_ref, loss_ref = refs[15], refs[16]
        outs, tot = refs[17:37], refs[37]
        acc = smalls_ref[0]
        for d in range(1, NDEV):
            acc = acc + smalls_ref[d]
        tot[...] = acc
        loss_ref[...] = tot[SMALL_AT["loss"]:SMALL_AT["loss"] + 1, 0:1]
        for p, (nm_, wd) in enumerate(zip(names, widths)):
            r = SMALL_AT[nm_]
            g = tot[r:r + 1, 0:wd]
            d, nm, nv = _adamw_math(w_refs[p][...], g, m_refs[p][...], v_refs[p][...])
            for o, val in zip(outs[4 * p:4 * p + 4], (g, d, nm, nv)):
                o[...] = val

    res = pl.pallas_call(
        body, name="finish_small",
        in_specs=[_vmem()] * 16, out_specs=[_vmem()] * 21,
        out_shape=[jax.ShapeDtypeStruct((1, 1), F32)]
        + [jax.ShapeDtypeStruct((1, wd), F32) for wd in widths for _ in range(4)],
        scratch_shapes=[pltpu.VMEM((SMALL_ROWS, D), F32)],
        compiler_params=_cp(),
    )(*[ws[n] for n in names], *[ms[n] for n in names], *[vs[n] for n in names], smalls)
    return res[0], {n: tuple(res[1 + 4 * p:5 + 4 * p]) for p, n in enumerate(names)}


def _finish(w_rows, m_rows, v_rows, gu_w, gu_m, gu_v, sums, got):
    shapes = [(SHARD, 1, D)] + [w.shape for w in w_rows[1:]]

    row_block = 96

    def columns(ref, p, cols, r0, n):
        if p:
            return ref, (slice(r0, r0 + n), cols)
        flat = ref if ref.shape == (SHARD * LANE_TILES, LANE) else ref.reshape(SHARD * LANE_TILES, LANE)
        return flat, (pl.ds(cols.start // LANE + LANE_TILES * r0, n, stride=LANE_TILES), slice(None))

    def read(ref, p, cols, r0, n):
        ref, at = columns(ref, p, cols, r0, n)
        return ref[at]

    def body(*refs):
        wr_refs, mr_refs, vr_refs = refs[0:4], refs[4:8], refs[8:12]
        guw_ref, gum_ref, guv_ref = refs[12:15]
        s_refs, got_refs = refs[15:17], refs[17:19]
        row_outs = refs[19:35]
        gu_outs = refs[35:39]
        buf, gsh, sems, big, big_sems = refs[39:44]
        staged, out_sems = refs[44:48], refs[48]
        writes = [[pltpu.make_async_copy(st.at[:, :, pl.ds(hf * DH, DH)], o.at[:, :, pl.ds(hf * DH, DH)], out_sems.at[k, hf])
                   for k, (st, o) in enumerate(zip(staged, row_outs[0:4]))] for hf in range(2)]
        row_outs = tuple(staged) + tuple(row_outs[4:])
        loads = [pltpu.make_async_copy(r[0], big.at[k], big_sems.at[k]) for k, r in enumerate((wr_refs, mr_refs, vr_refs))]
        for cp in loads:
            cp.start()
        wr_refs, mr_refs, vr_refs = ((big.at[k],) + tuple(r[1:]) for k, r in enumerate((wr_refs, mr_refs, vr_refs)))
        x, y, c = _place()
        me_slot = 4 * x + 2 * y + c
        down = 2 * me_slot

        def total(rows, cols):
            g = buf[0, rows, cols].astype(F32)
            for j in range(1, 4):
                g = g + buf[j, rows, cols].astype(F32)
            return g

        def update(p, grad, cols):
            nrows = shapes[p][0]
            for r0 in range(0, nrows, row_block):
                n = min(row_block, nrows - r0)
                g = grad(r0, n)
                d, nm, nv = _adamw_math(read(wr_refs[p], p, cols, r0, n), g, read(mr_refs[p], p, cols, r0, n),
                                        read(vr_refs[p], p, cols, r0, n))
                for o, val in zip(row_outs[4 * p:4 * p + 4], (g, d, nm, nv)):
                    o, at = columns(o, p, cols, r0, n)
                    o[at] = val

        for hf in range(2):
            _fetch_partials(s_refs[hf], got_refs[hf], buf, sems)
            for cc in range(DH // LANE):
                src = slice(cc * LANE, (cc + 1) * LANE)
                cols = slice(hf * DH + cc * LANE, hf * DH + (cc + 1) * LANE)
                for r0 in range(0, SHARD_PAD, row_block):
                    rows = slice(r0, min(r0 + row_block, SHARD_PAD))
                    gsh[rows, :] = total(rows, src)
                if hf == 0 and cc == 0:
                    for cp in loads:
                        cp.wait()
                update(0, lambda r0, n: gsh[pl.ds(down + r0, n), :], cols)
                for p in range(1, 4):
                    update(p, lambda r0, n, p=p: total(slice(ROW_AT[p] + r0, ROW_AT[p] + r0 + n), src), cols)
            for cp in writes[hf]:
                cp.start()
            if hf == 0:
                g = total(slice(R_GU, R_GU + RANK), slice(0, 64))
                d, nm, nv = _adamw_math(guw_ref[...], g, gum_ref[...], guv_ref[...])
                for o, val in zip(gu_outs, (g, d, nm, nv)):
                    o[...] = val
        for cp in writes[0] + writes[1]:
            cp.wait()

    res = pl.pallas_call(
        body, name="finish",
        in_specs=([_any()] + [_vmem()] * 3) * 3 + [_vmem()] * 3 + [_any()] * 4,
        out_specs=[_any()] * 4 + [_vmem()] * 16,
        out_shape=[jax.ShapeDtypeStruct(s, F32) for s in shapes for _ in range(4)]
        + [jax.ShapeDtypeStruct((RANK, 64), F32)] * 4,
        scratch_shapes=[pltpu.VMEM((4, ROWS, DH), sums[0].dtype), pltpu.VMEM((SHARD_PAD, LANE), F32),
                        pltpu.SemaphoreType.DMA((4,)),
                        pltpu.VMEM((3, SHARD * LANE_TILES, LANE), F32), pltpu.SemaphoreType.DMA((3,))]
        + [pltpu.VMEM((SHARD, 1, D), F32)] * 4 + [pltpu.SemaphoreType.DMA((4, 2))],
        compiler_params=_cp(),
    )(*w_rows, *m_rows, *v_rows, gu_w, gu_m, gu_v, *sums, *got)
    return tuple(res[0:16]), tuple(res[16:20])


def _place():
    x, y, c = lax.axis_index("x"), lax.axis_index("y"), lax.axis_index("c")
    return x, y, c


def _peers(x, y, c):
    return [(x ^ dx, y ^ dy, c ^ dc) for dx in range(2) for dy in range(2) for dc in range(2) if dx + dy + dc]


def _late_gather_start(blk, after, name="late_gather"):
    land = pltpu.with_memory_space_constraint(lax.empty((NDEV,) + blk.shape, blk.dtype), pltpu.HBM)

    def body(b_ref, land_ref, after_ref, send_sems, recv_sems, b_thru, land_thru, token):
        x, y, c = _place()
        for k, to in enumerate(_peers(x, y, c)):
            pltpu.make_async_remote_copy(
                src_ref=b_ref, dst_ref=land_ref.at[4 * x + 2 * y + c], send_sem=send_sems.at[k],
                recv_sem=recv_sems.at[k], device_id=to, device_id_type=MESH).start()
        token[...] = jnp.zeros_like(token)

    return pl.pallas_call(
        body, name=name + "_start",
        out_shape=(pltpu.SemaphoreType.DMA((7,)), pltpu.SemaphoreType.DMA((7,)),
                   pltpu.HBM(blk.shape, blk.dtype), pltpu.HBM(land.shape, land.dtype),
                   jax.ShapeDtypeStruct((8, LANE), F32)),
        in_specs=(_hbm(), _hbm(), _any()), out_specs=(_sem(), _sem(), _hbm(), _hbm(), _vmem()),
        input_output_aliases={0: 2, 1: 3},
        compiler_params=pltpu.CompilerParams(has_side_effects=_EFFECT),
    )(pltpu.with_memory_space_constraint(blk, pltpu.HBM), land, after)


def _late_gather_wait(send_sems, recv_sems, b_thru, land_thru, after, after2, name="late_gather"):
    def body(b_ref, land_ref, send_sems, recv_sems, after_ref, after2_ref, b_out, got_ref):
        x, y, c = _place()
        copies = [pltpu.make_async_remote_copy(
            src_ref=b_ref, dst_ref=land_ref.at[4 * x + 2 * y + c], send_sem=send_sems.at[k],
            recv_sem=recv_sems.at[k], device_id=to, device_id_type=MESH)
            for k, to in enumerate(_peers(x, y, c))]
        for cp in copies:
            cp.wait_send()
        for cp in copies:
            cp.wait_recv()

    return pl.pallas_call(
        body, name=name + "_wait",
        out_shape=(pltpu.HBM(b_thru.shape, b_thru.dtype), pltpu.HBM(land_thru.shape, land_thru.dtype)),
        in_specs=(_hbm(), _hbm(), _sem(), _sem(), _any(), _any()), out_specs=(_hbm(), _hbm()),
        input_output_aliases={0: 0, 1: 1},
        compiler_params=pltpu.CompilerParams(has_side_effects=_EFFECT),
    )(b_thru, land_thru, send_sems, recv_sems, after, after2)


G_ROWS = SHARD_PAD + RANK


def _gather_blocks(w_in_t, gu_s, xs, norm_w, pos_col):
    rows, cols = G_ROWS, D
    T = xs.shape[0]
    tT = min(T, 256)
    inv_row, sign_row = _rope_rows()

    def body(wi_ref, gu_ref, xs_hbm, nw_ref, pos_ref, inv_ref, sign_ref,
             out_ref, h_ref, cos_ref, sin_ref, x_ref, frame_ref, xs_ref, send_sems, recv_sems, local_sem, xs_sem):
        load_xs = pltpu.make_async_copy(xs_hbm, xs_ref, xs_sem)
        load_xs.start()
        x, y, c = _place()
        me, sibling = (x, y, c), (x, y, 1 - c)
        chips = [(1 - x, y), (x, 1 - y), (1 - x, 1 - y)]
        shift = 2 * (4 * x + 2 * y + c)
        frame_ref[SHARD - SHARD % 8:, :] = jnp.zeros((SHARD_PAD - SHARD + SHARD % 8, LANE), F32)
        for cc in range(LANE_TILES):
            cs = slice(cc * LANE, (cc + 1) * LANE)
            frame_ref[:SHARD, :] = wi_ref[pl.ds(cc, SHARD, stride=LANE_TILES), :]
            x_ref[0:SHARD_PAD, cs] = pltpu.roll(frame_ref[...], shift, 0).astype(x_ref.dtype)
        x_ref[SHARD_PAD:G_ROWS, :] = jnp.zeros((RANK, D), x_ref.dtype)
        x_ref[SHARD_PAD:G_ROWS, 0:64] = gu_ref[...].astype(x_ref.dtype)

        def slot(px, py, pc):
            return out_ref.at[4 * px + 2 * py + pc]

        def copy(k, block, to, src=None):
            return pltpu.make_async_remote_copy(
                src_ref=slot(*block) if src is None else src, dst_ref=slot(*block),
                send_sem=send_sems.at[k], recv_sem=recv_sems.at[k], device_id=to, device_id_type=MESH)

        mine = pltpu.make_async_copy(x_ref, slot(*me), local_sem)
        mine.start()
        first = [copy(0, me, sibling, src=x_ref)]
        first += [copy(1 + j, me, (*chip, c), src=x_ref) for j, chip in enumerate(chips)]
        for cp in first:
            cp.start()
        load_xs.wait()

        @pl.loop(0, T // tT)
        def _(i):
            rows_i = pl.ds(pl.multiple_of(i * tT, tT), tT)
            _prologue_rows(rows_i, xs_ref, nw_ref, pos_ref, inv_ref, sign_ref, h_ref, cos_ref, sin_ref)

        passed = [copy(4 + j, (*chip, c), sibling) for j, chip in enumerate(chips)]
        for j, chip in enumerate(chips):
            copy(1 + j, (*chip, c), me).wait_recv()
            passed[j].start()
        copy(0, sibling, me).wait_recv()
        for j, chip in enumerate(chips):
            copy(4 + j, (*chip, 1 - c), me).wait_recv()
        for cp in first + passed:
            cp.wait_send()
        mine.wait()

    return pl.pallas_call(
        body, name="gather_weights",
        in_specs=[_vmem(), _vmem(), _any()] + [_vmem()] * 4, out_specs=[_any()] + [_vmem()] * 3,
        out_shape=[jax.ShapeDtypeStruct((NDEV, rows, cols), WIRE), jax.ShapeDtypeStruct((T, D), MXU),
                   jax.ShapeDtypeStruct((T, LANE), F32), jax.ShapeDtypeStruct((T, LANE), F32)],
        scratch_shapes=[pltpu.VMEM((rows, cols), WIRE), pltpu.VMEM((SHARD_PAD, LANE), F32), pltpu.VMEM((T, D), F32),
                        pltpu.SemaphoreType.DMA((7,)), pltpu.SemaphoreType.DMA((7,)), pltpu.SemaphoreType.DMA,
                        pltpu.SemaphoreType.DMA],
        compiler_params=_cp(),
    )(w_in_t, gu_s, xs, norm_w, pos_col, inv_row, sign_row)


def _pair_reduce(gwt, tails, half):
    n = gwt.shape[1]
    starts = [SHARD_PAD]
    for t in tails:
        starts.append(starts[-1] + t.shape[1])
    rows = starts[-1]
    blk = (4, rows, n)
    npart = 1 + len(tails)

    def body(*refs):
        g_ref, t_refs = refs[0], refs[1:npart]
        out_ref, acc, got, own, send_sems, recv_sems, own_sems, out_sems = refs[npart:]
        x, y, c = _place()

        def parts(d, dst):
            frame = g_ref.at[pl.ds(pl.multiple_of(FRAME * d, 16), SHARD_PAD)]
            return [(frame, dst.at[0:SHARD_PAD])] + [
                (t_ref.at[d], dst.at[starts[k]:starts[k + 1]]) for k, t_ref in enumerate(t_refs)]

        sends, loads, stores = [], [], []
        for chip in range(4):
            sends.append([pltpu.make_async_remote_copy(
                src_ref=s, dst_ref=d_, send_sem=send_sems.at[chip, k], recv_sem=recv_sems.at[chip, k],
                device_id=(x, y, 1 - c), device_id_type=MESH)
                for k, (s, d_) in enumerate(parts(2 * chip + (1 - c), got.at[chip]))])
            loads.append([pltpu.make_async_copy(s, d_, own_sems.at[chip, k])
                          for k, (s, d_) in enumerate(parts(2 * chip + c, own.at[chip]))])
            stores.append(pltpu.make_async_copy(acc.at[chip], out_ref.at[chip], out_sems.at[chip]))
        for group in sends + loads:
            for cp in group:
                cp.start()
        for chip in range(4):
            for cp in loads[chip]:
                cp.wait()
            for cp in sends[chip]:
                cp.wait_recv()
            acc[chip] = (own[chip].astype(F32) + got[chip].astype(F32)).astype(acc.dtype)
            stores[chip].start()
        for cp in stores:
            cp.wait()
        for group in sends:
            for cp in group:
                cp.wait_send()

    return pl.pallas_call(
        body, name=f"pair_reduce{half}",
        in_specs=[_any()] * npart, out_specs=_any(),
        out_shape=jax.ShapeDtypeStruct(blk, gwt.dtype),
        scratch_shapes=[pltpu.VMEM(blk, gwt.dtype), pltpu.VMEM(blk, gwt.dtype), pltpu.VMEM(blk, gwt.dtype),
                        pltpu.SemaphoreType.DMA((4, npart)), pltpu.SemaphoreType.DMA((4, npart)),
                        pltpu.SemaphoreType.DMA((4, npart)), pltpu.SemaphoreType.DMA((4,))],
        compiler_params=_cp(),
    )(gwt, *tails)


def _pad_cols(a, cols):
    return jnp.pad(a, ((0, 0), (0, cols - a.shape[1])))


def _pad_rows(a, rows):
    return jnp.pad(a, ((0, rows - a.shape[0]), (0, 0)))


FRAME = 928


def _wft_plan():
    moves = []
    for blk in range(8):
        for half in range(2):
            for sub in range(2):
                moves.append((C_Q + 128 * blk + 32 * (2 * half + sub), 128 * blk + 32 * (2 * sub + half), 32))
    for idx in range(4):
        for dup in range(2):
            moves.append((C_KD + 64 * idx + 32 * dup, 1024 + 32 * idx, 32))
    for g in range(2):
        for dup in range(2):
            moves.append((C_VD + 128 * g + 64 * dup, 1152 + 64 * g, 64))
    moves += [(C_BL, 5376, RANK), (C_BV, 3328, 1024), (C_BQ, 2304, 512), (C_BK, 2816, 512),
              (C_AG, 1280, 1024), (C_BG, 4352, 1024), (C_MA, 5392, 1024), (C_MB, 6416, 1024)]
    bulk, seams = [], []
    for dst, src, n in moves:
        r = src
        while r < src + n:
            f = min(r // FRAME, NDEV - 1)
            local = r - FRAME * f
            if f > 0 and local < 16:
                assert local == 0
                seams.append((f, dst + r - src))
                step = 16
            else:
                step = min(src + n, FRAME * (f + 1) if f < NDEV - 1 else IN_WIDTH) - r
                bulk.append((f, local, dst + r - src, step))
            r += step
    assert sorted(f for f, _ in seams) == list(range(1, NDEV))
    return bulk, seams, [(C_BL + RANK, C_GLA - C_BL - RANK)]


def _build_wft_copies(frames):
    bulk, seams, zeros = _wft_plan()
    (z0, zn), = zeros

    def body(f_ref, o_ref, edge, sems, esems):
        copies = [pltpu.make_async_copy(f_ref.at[f, pl.ds(l0, n)], o_ref.at[pl.ds(dst, n)], sems.at[i])
                  for i, (f, l0, dst, n) in enumerate(bulk)]
        loads = []
        for i, (f, _) in enumerate(seams):
            loads.append(pltpu.make_async_copy(f_ref.at[f, pl.ds(0, 16)], edge.at[i, 0], esems.at[i, 0]))
            loads.append(pltpu.make_async_copy(f_ref.at[f - 1, pl.ds(FRAME, 16)], edge.at[i, 1], esems.at[i, 1]))
        for cp in copies + loads:
            cp.start()
        o_ref[z0:z0 + zn, :] = jnp.zeros((zn, D), o_ref.dtype)
        for cp in loads:
            cp.wait()
        for i, (_, dst) in enumerate(seams):
            o_ref[dst:dst + 16, :] = edge[i, 0] + edge[i, 1]
        for cp in copies:
            cp.wait()

    return pl.pallas_call(
        body, name="build_wft",
        in_specs=[_any()], out_specs=_vmem(),
        out_shape=jax.ShapeDtypeStruct((NF, D), frames.dtype),
        scratch_shapes=[pltpu.VMEM((len(seams), 2, 16, D), frames.dtype),
                        pltpu.SemaphoreType.DMA((len(bulk),)), pltpu.SemaphoreType.DMA((len(seams), 2))],
        compiler_params=_cp(),
    )(frames)


def kernel(x, positions, norm_w, w_in, a_sinks, b_gate_up, b_gate_bias, b_out_norm_w, w_a_proj, w_b_proj, w_out, final_norm_w, loss_target, m_norm_w, m_w_in, m_a_sinks, m_b_gate_up, m_b_gate_bias, m_b_out_norm_w, m_w_a_proj, m_w_b_proj, m_w_out, m_final_norm_w, v_norm_w, v_w_in, v_a_sinks, v_b_gate_up, v_b_gate_bias, v_b_out_norm_w, v_w_a_proj, v_w_b_proj, v_w_out, v_final_norm_w):
    T = x.shape[1]
    xs, target = x[0], loss_target[0]
    fnw = final_norm_w.reshape(1, D)
    me = 4 * lax.axis_index("x") + 2 * lax.axis_index("y") + lax.axis_index("c")
    allw, h, cos, sin = _gather_blocks(_by_lane_tile(w_in), b_gate_up[0], xs, norm_w, positions.reshape(T, 1))
    late_blk = jnp.concatenate([w_a_proj[0], w_b_proj[0], w_out[0]], axis=0).astype(WIRE)
    l_send, l_recv, l_blk, l_land, l_started = _late_gather_start(late_blk, cos)
    wf = _build_wft_copies(allw)
    gu = allw[:, SHARD_PAD:G_ROWS, :64].transpose(1, 0, 2).reshape(RANK, 512)
    gu_pad = _pad_rows(gu, W_BL)

    proj = _proj(h, wf, l_started)
    o_a, lse = _swa_fwd(proj, cos, sin, a_sinks)
    o_b, states = _gla_fwd(proj, gu_pad, b_gate_bias)
    l_blk, l_land = _late_gather_wait(l_send, l_recv, l_blk, l_land, states, lse)
    late = lax.dynamic_update_slice(l_land, l_blk[None], (me, 0, 0))
    (dx2, do_a, do_b, d_gates, g_late0, g_late1, g_fn, g_bn, loss_part) = _mid(
        xs, target, proj, o_a, o_b, late, jnp.tile(b_out_norm_w, (1, B_HEADS)), fnw)
    d_q, d_kv, g_sinks = _swa_bwd(proj, cos, sin, a_sinks, do_a, o_a, lse, cos)
    d_gla, d_bl, g_gu, g_bias = _gla_bwd(proj, gu_pad, b_gate_bias, states, do_b)
    pieces = [d_q, d_kv, d_bl, d_gla, d_gates]
    offsets = [C_Q, C_KD, C_BL, C_GLA, C_GATES]

    ggu = g_gu[:RANK].reshape(RANK, NDEV, 64).transpose(1, 0, 2)
    ggu_half = [jnp.pad(ggu, ((0, 0), (0, 0), (0, DH - 64))).astype(WIRE), jnp.zeros((NDEV, RANK, DH), WIRE)]
    tails = [[g_late0, ggu_half[0]], [g_late1, ggu_half[1]]]

    send0, recv0, s_thru0, land0, started0 = _chip_start(
        _pair_reduce(_gw_half(h, pieces, 0, after=g_bias), tails[0], 0), 0)
    send1, recv1, s_thru1, land1, started1 = _chip_start(
        _pair_reduce(_gw_half(h, pieces, 1, after=started0), tails[1], 1), 1)
    grad_x, g_nw = _dh_norm(pieces, offsets, wf, xs, dx2, norm_w, started1)
    small = jnp.concatenate([g_nw, g_fn, _pad_cols(g_bias, D), _pad_cols(g_bn, D), _pad_cols(g_sinks, D),
                             _pad_cols(loss_part, D)], axis=0)
    sm_send, sm_recv, sm_blk, sm_land, sm_started = _late_gather_start(small, g_nw, name="small_gather")
    sums0, got0 = _chip_wait(send0, recv0, s_thru0, land0, sm_started, 0)
    sums1, got1 = _chip_wait(send1, recv1, s_thru1, land1, got0, 1)
    sums, from_chips = [sums0, sums1], [got0, got1]

    ws = dict(norm_w=norm_w, fnw=fnw, bias=b_gate_bias, bn=b_out_norm_w, sinks=a_sinks)
    ms = dict(norm_w=m_norm_w, fnw=m_final_norm_w.reshape(1, D), bias=m_b_gate_bias, bn=m_b_out_norm_w,
              sinks=m_a_sinks)
    vs = dict(norm_w=v_norm_w, fnw=v_final_norm_w.reshape(1, D), bias=v_b_gate_bias, bn=v_b_out_norm_w,
              sinks=v_a_sinks)
    t_rows, t_gu = _finish(
        [_by_lane_tile(w_in), w_a_proj[0], w_b_proj[0], w_out[0]],
        [_by_lane_tile(m_w_in), m_w_a_proj[0], m_w_b_proj[0], m_w_out[0]],
        [_by_lane_tile(v_w_in), v_w_a_proj[0], v_w_b_proj[0], v_w_out[0]],
        b_gate_up[0], m_b_gate_up[0], v_b_gate_up[0], sums, from_chips)
    sm_blk, sm_land = _late_gather_wait(sm_send, sm_recv, sm_blk, sm_land, t_rows[0], t_gu[0], name="small_gather")
    loss, sm = _finish_small(ws, ms, vs, lax.dynamic_update_slice(sm_land, sm_blk[None], (me, 0, 0)))

    def outputs(k):
        return [sm["norm_w"][k], jnp.transpose(t_rows[k], (1, 2, 0)), sm["sinks"][k], t_gu[k][None], sm["bias"][k], sm["bn"][k],
                t_rows[4 + k][None], t_rows[8 + k][None], t_rows[12 + k][None], sm["fnw"][k].reshape(D)]

    return (loss[0, 0], grad_x[None], *outputs(0), *outputs(1), *outputs(2), *outputs(3))
```

```python
import functools

import numpy as np
import jax
import jax.numpy as jnp
from jax import lax
from jax.experimental import pallas as pl
from jax.experimental.pallas import tpu as pltpu

F32 = jnp.float32
MXU = jnp.bfloat16
WIRE = jnp.bfloat16

D = 1024
A_HEADS, A_KV, A_HD = 16, 2, 64
BLK = 128
B_HEADS, B_DK, B_DV = 4, 128, 256
RANK, TAU, CHUNK = 16, 16.0, 64
EPS, NEG = 1e-5, -1e30
ROPE_THETA = 10000.0
IN_WIDTH, NDEV = 7440, 8
SHARD = IN_WIDTH // NDEV
LANE = 128
LANE_TILES = D // LANE


def _by_lane_tile(a):
    return jnp.transpose(a, (2, 0, 1)).reshape(SHARD * LANE_TILES, LANE)


C_Q, C_KD, C_VD, C_BL = 0, 1024, 1280, 1536
C_BV, C_BQ, C_BK = 2048, 3072, 3584
C_AG, C_BG, C_MA, C_MB = 4096, 5120, 6144, 7168
C_GLA, W_GLA, C_GATES, W_GATES = 2048, 2048, 4096, 4096
NF = 8192
W_BL = 128

SHARD_PAD = 944
R_IN, R_A, R_B, R_O, R_GU, ROWS = 0, 944, 1072, 1200, 1328, 1344
SMALL_ROWS = 48

ADAM_LR, ADAM_B1, ADAM_B2, ADAM_EPS, ADAM_WD, ADAM_STEP = 0.001, 0.9, 0.999, 1e-08, 0.01, 10

MESH = pl.DeviceIdType.MESH
VMEM_LIMIT = 56 * 1024 * 1024


def _cp(sem=None, **kw):
    if sem is not None:
        kw["dimension_semantics"] = sem
    return pltpu.CompilerParams(vmem_limit_bytes=VMEM_LIMIT, **kw)


def _dot(a, b):
    return jnp.dot(a, b, preferred_element_type=F32)


def _dot_nt(a, b):
    return lax.dot_general(a, b, (((1,), (1,)), ((), ())), preferred_element_type=F32)


def _dot_tn(a, b):
    return lax.dot_general(a, b, (((0,), (0,)), ((), ())), preferred_element_type=F32)


def _dot_f32(a, b):
    return jnp.dot(a, b, preferred_element_type=F32, precision=lax.Precision.HIGHEST)


def _sigmoid(z):
    return 0.5 * jnp.tanh(0.5 * z) + 0.5


def _rope(xp, cos, sin):
    return xp * cos + pltpu.roll(xp, 64, 1) * sin


def _rope_bwd(dy, cos, sin):
    return dy * cos - pltpu.roll(dy, 64, 1) * sin


def _vmem():
    return pl.BlockSpec(memory_space=pltpu.VMEM)


def _any():
    return pl.BlockSpec(memory_space=pl.ANY)


def _rope_rows():
    half = A_HD // 2
    inv = (np.float32(ROPE_THETA) ** (-np.arange(half, dtype=np.float32) / np.float32(half))).astype(np.float32)
    inv_row = jnp.asarray(np.tile(inv, 4)[None, :])
    sign_row = jnp.asarray(np.concatenate([-np.ones(64, np.float32), np.ones(64, np.float32)])[None, :])
    return inv_row, sign_row


def _prologue_rows(rows, x_ref, nw_ref, pos_ref, inv_ref, sign_ref, h_ref, cos_ref, sin_ref):
    xv = x_ref[rows, :]
    r = lax.rsqrt(jnp.mean(xv * xv, axis=-1, keepdims=True) + EPS)
    h_ref[rows, :] = ((xv * r) * nw_ref[...]).astype(h_ref.dtype)
    ang = pos_ref[rows, :].astype(F32) * inv_ref[...]
    cos_ref[rows, :] = jnp.cos(ang)
    sin_ref[rows, :] = jnp.sin(ang) * sign_ref[...]


def _proj(h, wft, after):
    T = h.shape[0]
    tT, tN = T, 512

    def body(h_ref, w_ref, after_ref, o_ref):
        o_ref[...] = _dot_nt(h_ref[...], w_ref[...])

    return pl.pallas_call(
        body, name="proj", grid=(T // tT, NF // tN),
        in_specs=[pl.BlockSpec((tT, D), lambda i, j: (i, 0)), pl.BlockSpec((tN, D), lambda i, j: (j, 0)), _any()],
        out_specs=pl.BlockSpec((tT, tN), lambda i, j: (i, j)),
        out_shape=jax.ShapeDtypeStruct((T, NF), F32),
        compiler_params=_cp(("parallel", "parallel")),
    )(h, wft, after)


def _swa_masks():
    lane = lax.broadcasted_iota(jnp.int32, (BLK, LANE), 1)
    rope_sub0 = ((lane // 32) % 2) == 0
    std_sub0 = lane < 64
    return lane, rope_sub0, std_sub0


def _swa_tri():
    qi = lax.broadcasted_iota(jnp.int32, (BLK, BLK), 0)
    kj = lax.broadcasted_iota(jnp.int32, (BLK, BLK), 1)
    return kj <= qi


def _swa_fold(full, tri):
    return jnp.where(tri, full[:, BLK:], full[:, :BLK])


def _swa_unfold(sq, tri):
    return jnp.concatenate([jnp.where(tri, 0.0, sq), jnp.where(tri, sq, 0.0)], axis=1)


def _swa_keys(kc_ref, kp_ref, vc_ref, vp_ref, cq, sq, cp, sp):
    def ropek(kref, c, s):
        kv = kref[...]
        return jnp.concatenate([_rope(kv[:, :LANE], c, s), _rope(kv[:, LANE:], c, s)], axis=1)

    K = jnp.concatenate([ropek(kp_ref, cp, sp), ropek(kc_ref, cq, sq)], axis=0).astype(MXU)
    V = jnp.concatenate([vp_ref[...], vc_ref[...]], axis=0).astype(MXU)
    return K, V


def _swa_in_specs(nb, last):
    def cur(n):
        return jnp.minimum(n, last)

    def prev(n):
        return jnp.maximum(cur(n) - 1, 0)

    kd, vd = C_KD // 256, C_VD // 256
    return [
        pl.BlockSpec((BLK, D), lambda n: (cur(n), C_Q // D)),
        pl.BlockSpec((BLK, 256), lambda n: (cur(n), kd)),
        pl.BlockSpec((BLK, 256), lambda n: (prev(n), kd)),
        pl.BlockSpec((BLK, 256), lambda n: (cur(n), vd)),
        pl.BlockSpec((BLK, 256), lambda n: (prev(n), vd)),
        pl.BlockSpec((BLK, LANE), lambda n: (cur(n), 0)),
        pl.BlockSpec((BLK, LANE), lambda n: (cur(n), 0)),
        pl.BlockSpec((BLK, LANE), lambda n: (prev(n), 0)),
        pl.BlockSpec((BLK, LANE), lambda n: (prev(n), 0)),
    ]


def _swa_fwd(proj, cos, sin, sinks):
    T = proj.shape[0]
    nb = T // BLK
    scale = A_HD ** -0.5

    def body(sinks_ref, q_ref, kc_ref, kp_ref, vc_ref, vp_ref, cq_ref, sq_ref, cp_ref, sp_ref, o_ref, l_ref):
        n = pl.program_id(0)
        cq, sq = cq_ref[...], sq_ref[...]
        K, V = _swa_keys(kc_ref, kp_ref, vc_ref, vp_ref, cq, sq, cp_ref[...], sp_ref[...])
        tri = _swa_tri()
        valid = tri | (n > 0)
        lane, rope_sub0, std_sub0 = _swa_masks()
        group = A_HEADS // A_KV
        roped, lses = {}, []

        def products(head):
            pb, sub, g = head // 2, head % 2, head // group
            if sub == 0:
                roped[pb] = _rope(q_ref[:, pb * LANE:(pb + 1) * LANE], cq, sq)
            qm = jnp.where(rope_sub0 if sub == 0 else ~rope_sub0, roped[pb], 0.0).astype(MXU)
            return _dot_nt(qm, K[:, g * LANE:(g + 1) * LANE])

        def softmax(head, s_full):
            s = jnp.where(valid, _swa_fold(s_full, tri) * scale, NEG)
            sink = sinks_ref[0, head]
            m = jnp.maximum(jnp.max(s, axis=1, keepdims=True), sink)
            e = jnp.exp(s - m)
            den = jnp.sum(e, axis=1, keepdims=True) + jnp.exp(sink - m)
            lses.append(m + jnp.log(den))
            return _swa_unfold(e / den, tri).astype(MXU)

        outs = {}
        st1 = {0: products(0), 1: products(1)}
        st2 = {0: softmax(0, st1.pop(0))}
        for head in range(A_HEADS):
            if head + 2 < A_HEADS:
                st1[head + 2] = products(head + 2)
            if head + 1 < A_HEADS:
                st2[head + 1] = softmax(head + 1, st1.pop(head + 1))
            g = head // group
            outs[head] = _dot(st2.pop(head), V[:, g * LANE:(g + 1) * LANE])
            if head % 2 == 1:
                pb = head // 2
                o_ref[:, pb * LANE:(pb + 1) * LANE] = jnp.where(std_sub0, outs[head - 1], outs[head])
        lacc = jnp.zeros((BLK, LANE), F32)
        for head in range(A_HEADS):
            lacc = jnp.where(lane == head, lses[head], lacc)
        l_ref[...] = lacc

    return pl.pallas_call(
        body, name="swa_fwd", grid=(nb,),
        in_specs=[pl.BlockSpec(memory_space=pltpu.SMEM)] + _swa_in_specs(nb, nb - 1),
        out_specs=[pl.BlockSpec((BLK, D), lambda n: (n, 0)), pl.BlockSpec((BLK, LANE), lambda n: (n, 0))],
        out_shape=[jax.ShapeDtypeStruct((T, D), F32), jax.ShapeDtypeStruct((T, LANE), F32)],
        compiler_params=_cp(("parallel",)),
    )(sinks, proj, proj, proj, proj, proj, cos, sin, cos, sin)


def _swa_bwd(proj, cos, sin, sinks, do_a, o_a, lse, after):
    T = proj.shape[0]
    nb = T // BLK
    scale = A_HD ** -0.5

    def body(sinks_ref, q_ref, kc_ref, kp_ref, vc_ref, vp_ref, cq_ref, sq_ref, cp_ref, sp_ref,
             do_ref, o_ref, l_ref, after_ref, dq_ref, dkv_ref, ds_ref, ckv_ref):
        n = pl.program_id(0)

        @pl.when(n == 0)
        def _():
            ckv_ref[...] = jnp.zeros_like(ckv_ref)
            ds_ref[...] = jnp.zeros_like(ds_ref)

        @pl.when(n < nb)
        def _():
            cq, sq, cp, sp = cq_ref[...], sq_ref[...], cp_ref[...], sp_ref[...]
            K, V = _swa_keys(kc_ref, kp_ref, vc_ref, vp_ref, cq, sq, cp, sp)
            tri = _swa_tri()
            valid = tri | (n > 0)
            lane, rope_sub0, std_sub0 = _swa_masks()
            lane_row = lax.broadcasted_iota(jnp.int32, (1, LANE), 1)
            lse_v = l_ref[...]
            dKt = [jnp.zeros((LANE, 2 * BLK), F32) for _ in range(A_KV)]
            dVt = [jnp.zeros((LANE, 2 * BLK), F32) for _ in range(A_KV)]
            dsinks, roped, roped_t, do_t = [], {}, {}, {}
            group = A_HEADS // A_KV
            dim = lax.broadcasted_iota(jnp.int32, (LANE, BLK), 0)
            rope_row0, std_row0 = ((dim // 32) % 2) == 0, dim < 64

            def products(head):
                pb, sub, g = head // 2, head % 2, head // group
                cols = slice(pb * LANE, (pb + 1) * LANE)
                Kg, Vg = K[:, g * LANE:(g + 1) * LANE], V[:, g * LANE:(g + 1) * LANE]
                if sub == 0:
                    roped[pb] = _rope(q_ref[:, cols], cq, sq)
                    roped_t[pb] = roped[pb].T
                    do_t[pb] = do_ref[:, cols].T
                qm = jnp.where(rope_sub0 if sub == 0 else ~rope_sub0, roped[pb], 0.0).astype(MXU)
                qmt = jnp.where(rope_row0 if sub == 0 else ~rope_row0, roped_t[pb], 0.0).astype(MXU)
                dov = jnp.where(std_sub0 if sub == 0 else ~std_sub0, do_ref[:, cols], 0.0)
                dovt = jnp.where(std_row0 if sub == 0 else ~std_row0, do_t[pb], 0.0).astype(MXU)
                delta = jnp.sum(dov * o_ref[:, cols], axis=1, keepdims=True)
                return qmt, dovt, delta, _dot_nt(qm, Kg), _dot_nt(dov.astype(MXU), Vg)

            def scores(head, qmt, dovt, delta, s_full, dp_full):
                lh = jnp.sum(jnp.where(lane == head, lse_v, 0.0), axis=1, keepdims=True)
                p = jnp.where(valid, jnp.exp(_swa_fold(s_full, tri) * scale - lh), 0.0)
                psink = jnp.exp(sinks_ref[0, head] - lh)
                dsinks.append(jnp.sum(-psink * delta, axis=0, keepdims=True))
                dsq = (p * (_swa_fold(dp_full, tri) - delta)) * scale
                return qmt, dovt, _swa_unfold(p, tri).astype(MXU), _swa_unfold(dsq, tri).astype(MXU)

            def grads(head, qmt, dovt, pb16, dsc):
                g = head // group
                dKt[g] = dKt[g] + _dot(qmt, dsc)
                dVt[g] = dVt[g] + _dot(dovt, pb16)
                return _dot(dsc, K[:, g * LANE:(g + 1) * LANE])

            dqs = {}
            st1 = {0: products(0), 1: products(1)}
            st2 = {0: scores(0, *st1.pop(0))}
            for head in range(A_HEADS):
                if head + 2 < A_HEADS:
                    st1[head + 2] = products(head + 2)
                if head + 1 < A_HEADS:
                    st2[head + 1] = scores(head + 1, *st1.pop(head + 1))
                dqs[head] = grads(head, *st2.pop(head))
                if head % 2 == 1:
                    pb = head // 2
                    dqp = jnp.where(rope_sub0, dqs[head - 1], dqs[head])
                    dq_ref[:, pb * LANE:(pb + 1) * LANE] = _rope_bwd(dqp, cq, sq).astype(dq_ref.dtype)
            dsink = jnp.zeros((1, LANE), F32)
            for head in range(A_HEADS):
                dsink = jnp.where(lane_row == head, dsinks[head], dsink)
            dK, dV = [a.T for a in dKt], [a.T for a in dVt]
            prev = ([_rope_bwd(dK[g][:BLK], cp, sp) for g in range(A_KV)] + [dV[g][:BLK] for g in range(A_KV)])
            cur_ = ([_rope_bwd(dK[g][BLK:], cq, sq) for g in range(A_KV)] + [dV[g][BLK:] for g in range(A_KV)])
            dkv_ref[...] = (ckv_ref[...] + jnp.concatenate(prev, axis=1)).astype(dkv_ref.dtype)
            ckv_ref[...] = jnp.concatenate(cur_, axis=1)
            ds_ref[...] = ds_ref[...] + jnp.broadcast_to(dsink, ds_ref.shape)

        @pl.when(n == nb)
        def _():
            dkv_ref[...] = ckv_ref[...].astype(dkv_ref.dtype)

    last = nb - 1

    def cur(n):
        return jnp.minimum(n, last)

    def out_kv(n):
        return (jnp.maximum(n - 1, 0), 0)

    return pl.pallas_call(
        body, name="swa_bwd", grid=(nb + 1,),
        in_specs=[pl.BlockSpec(memory_space=pltpu.SMEM)] + _swa_in_specs(nb, last) + [
            pl.BlockSpec((BLK, D), lambda n: (cur(n), 0)),
            pl.BlockSpec((BLK, D), lambda n: (cur(n), 0)),
            pl.BlockSpec((BLK, LANE), lambda n: (cur(n), 0)),
            _any(),
        ],
        out_specs=[
            pl.BlockSpec((BLK, D), lambda n: (cur(n), 0)),
            pl.BlockSpec((BLK, 512), out_kv),
            pl.BlockSpec((8, LANE), lambda n: (0, 0)),
        ],
        out_shape=[
            jax.ShapeDtypeStruct((T, D), MXU),
            jax.ShapeDtypeStruct((T, 512), MXU),
            jax.ShapeDtypeStruct((8, LANE), F32),
        ],
        scratch_shapes=[pltpu.VMEM((BLK, 512), F32)],
        compiler_params=_cp(("arbitrary",)),
    )(sinks, proj, proj, proj, proj, proj, cos, sin, cos, sin, do_a, o_a, lse, after)


NCH = 4
GSTEP = NCH * CHUNK
ST_ROWS = B_HEADS * B_DV


def _chunk_rows(c):
    return slice(c * CHUNK, (c + 1) * CHUNK)


def _per_chunk(which, vals):
    out = vals[-1]
    for c in range(NCH - 2, -1, -1):
        out = jnp.where(which == c, vals[c], out)
    return out


def _gla_gate(bl_ref, gu_ref, bias_ref):
    gk = _dot(bl_ref[...].astype(MXU), gu_ref[...]) + bias_ref[...]
    la = (jnp.minimum(gk, 0.0) - jnp.log(1.0 + jnp.exp(-jnp.abs(gk)))) / TAU
    ri = lax.broadcasted_iota(jnp.int32, (GSTEP, GSTEP), 0)
    ci = lax.broadcasted_iota(jnp.int32, (GSTEP, GSTEP), 1)
    same = (ri // CHUNK) == (ci // CHUNK)
    lower, upper = same & (ci <= ri), same & (ci >= ri)
    b = _dot_f32(jnp.where(lower, 1.0, 0.0).astype(F32), la)
    which = lax.broadcasted_iota(jnp.int32, (GSTEP, 1), 0) // CHUNK
    return gk, la, b, lower, upper, which


def _gla_head(q_ref, k_ref, la, b, which, h):
    sl = slice(h * B_DK, (h + 1) * B_DK)
    bh, lah = b[:, sl], la[:, sl]
    bls = [jnp.sum(lah[_chunk_rows(c)], axis=0, keepdims=True) for c in range(NCH)]
    blast = _per_chunk(which, bls)
    qc = q_ref[:, sl] * (B_DK ** -0.5)
    kh = k_ref[:, sl]
    eb, enb, esb = jnp.exp(bh), jnp.exp(-bh), jnp.exp(blast - bh)
    return qc * eb, kh * enb, kh * esb, eb, enb, esb, [jnp.exp(v) for v in bls]


def _gla_specs(step_of):
    return [
        pl.BlockSpec((GSTEP, 512), lambda i: (step_of(i), C_BQ // 512)),
        pl.BlockSpec((GSTEP, 512), lambda i: (step_of(i), C_BK // 512)),
        pl.BlockSpec((GSTEP, D), lambda i: (step_of(i), C_BV // D)),
        pl.BlockSpec((GSTEP, W_BL), lambda i: (step_of(i), C_BL // W_BL)),
        pl.BlockSpec((W_BL, 512), lambda i: (0, 0)),
        pl.BlockSpec((1, 512), lambda i: (0, 0)),
    ]


def _gla_fwd(proj, gu_pad, bias):
    T = proj.shape[0]
    ns = T // GSTEP

    def body(q_ref, k_ref, v_ref, bl_ref, gu_ref, bias_ref, o_ref, st_ref, state_ref):
        @pl.when(pl.program_id(0) == 0)
        def _():
            state_ref[...] = jnp.zeros_like(state_ref)

        _, la, b, lower, _, which = _gla_gate(bl_ref, gu_ref, bias_ref)

        def within(h):
            q_e, k_e, k_s, _, _, _, decays = _gla_head(q_ref, k_ref, la, b, which, h)
            vh = v_ref[:, h * B_DV:(h + 1) * B_DV].astype(MXU)
            q_eb = q_e.astype(MXU)
            att = jnp.where(lower, _dot_nt(q_eb, k_e.astype(MXU)), 0.0)
            return vh, q_eb, k_s.astype(MXU), _dot(att.astype(MXU), vh), decays

        def across(h, vh, q_eb, k_sb, o_intra, decays):
            rows = slice(h * B_DV, (h + 1) * B_DV)
            s = state_ref[rows, :]
            outs = []
            for c in range(NCH):
                cr = _chunk_rows(c)
                st_ref[c * ST_ROWS + h * B_DV:c * ST_ROWS + (h + 1) * B_DV, :] = s
                outs.append(o_intra[cr] + _dot_nt(q_eb[cr], s.astype(MXU)))
                s = s * decays[c] + _dot_tn(vh[cr], k_sb[cr])
            state_ref[rows, :] = s
            o_ref[:, rows] = jnp.concatenate(outs, axis=0)

        for h in range(B_HEADS):
            across(h, *within(h))

    return pl.pallas_call(
        body, name="gla_fwd", grid=(ns,),
        in_specs=_gla_specs(lambda i: i),
        out_specs=[pl.BlockSpec((GSTEP, D), lambda i: (i, 0)),
                   pl.BlockSpec((NCH * ST_ROWS, B_DK), lambda i: (i, 0))],
        out_shape=[jax.ShapeDtypeStruct((T, D), F32),
                   jax.ShapeDtypeStruct((ns * NCH * ST_ROWS, B_DK), F32)],
        scratch_shapes=[pltpu.VMEM((ST_ROWS, B_DK), F32)],
        compiler_params=_cp(("arbitrary",)),
    )(proj, proj, proj, proj, gu_pad, bias)


def _gla_bwd(proj, gu_pad, bias, states, do_b):
    T = proj.shape[0]
    ns = T // GSTEP
    o_q, o_k = C_BQ - C_GLA, C_BK - C_GLA

    def body(q_ref, k_ref, v_ref, bl_ref, gu_ref, bias_ref, st_ref, do_ref,
             dg_ref, dbl_ref, ggu_ref, gbias_ref, gt_ref):
        @pl.when(pl.program_id(0) == 0)
        def _():
            gt_ref[...] = jnp.zeros_like(gt_ref)
            ggu_ref[...] = jnp.zeros_like(ggu_ref)
            gbias_ref[...] = jnp.zeros_like(gbias_ref)

        gk, la, b, lower, upper_mask, which = _gla_gate(bl_ref, gu_ref, bias_ref)
        upper = jnp.where(upper_mask, 1.0, 0.0).astype(F32)
        dla_parts = []

        def within(h):
            q_e, k_e, k_s, eb, enb, esb, decays = _gla_head(q_ref, k_ref, la, b, which, h)
            vh = v_ref[:, h * B_DV:(h + 1) * B_DV].astype(MXU)
            doh = do_ref[:, h * B_DV:(h + 1) * B_DV].astype(MXU)
            q_eb, k_eb = q_e.astype(MXU), k_e.astype(MXU)
            att = jnp.where(lower, _dot_nt(q_eb, k_eb), 0.0).astype(MXU)
            datt = jnp.where(lower, _dot_nt(doh, vh), 0.0).astype(MXU)
            return (q_e, k_e, k_s, eb, enb, esb, decays, vh, doh, q_eb, k_s.astype(MXU),
                    _dot(datt, k_eb), _dot_tn(datt, q_eb), _dot_tn(att, doh))

        def across(h, q_e, k_e, k_s, eb, enb, esb, decays, vh, doh, q_eb, k_sb, dq_i, dk_e, dv_i):
            rows = slice(h * B_DV, (h + 1) * B_DV)
            g = gt_ref[rows, :]
            dq_c, dks_c, dv_c, ddec = [None] * NCH, [None] * NCH, [None] * NCH, [None] * NCH
            for c in range(NCH - 1, -1, -1):
                cr = _chunk_rows(c)
                s = st_ref[c * ST_ROWS + h * B_DV:c * ST_ROWS + (h + 1) * B_DV, :]
                gb = g.astype(MXU)
                dq_c[c] = dq_i[cr] + _dot(doh[cr], s.astype(MXU))
                dks_c[c] = _dot(vh[cr], gb)
                dv_c[c] = dv_i[cr] + _dot_nt(k_sb[cr], gb)
                ddec[c] = jnp.sum(g * s, axis=0, keepdims=True)
                g = g * decays[c] + _dot_tn(doh[cr], q_eb[cr])
            gt_ref[rows, :] = g
            dq_e = jnp.concatenate(dq_c, axis=0)
            dk_s = jnp.concatenate(dks_c, axis=0)
            dg_ref[:, rows] = jnp.concatenate(dv_c, axis=0).astype(dg_ref.dtype)
            dg_ref[:, o_q + h * B_DK:o_q + (h + 1) * B_DK] = (dq_e * eb * (B_DK ** -0.5)).astype(dg_ref.dtype)
            dg_ref[:, o_k + h * B_DK:o_k + (h + 1) * B_DK] = (dk_e * enb + dk_s * esb).astype(dg_ref.dtype)
            dks_ks = dk_s * k_s
            db = dq_e * q_e - dk_e * k_e - dks_ks
            dbl = [jnp.sum(dks_ks[_chunk_rows(c)], axis=0, keepdims=True) + ddec[c] * decays[c] for c in range(NCH)]
            dla_parts.append(_dot_f32(upper, db) + _per_chunk(which, dbl))

        for h in range(B_HEADS):
            across(h, *within(h))
        dla = jnp.concatenate(dla_parts, axis=1)
        dgk = dla * (1.0 / TAU) * _sigmoid(-gk)
        dgkb = dgk.astype(MXU)
        dbl_ref[...] = _dot_nt(dgkb, gu_ref[...]).astype(dbl_ref.dtype)
        ggu_ref[...] = ggu_ref[...] + _dot_tn(bl_ref[...].astype(MXU), dgkb)
        gbias_ref[...] = gbias_ref[...] + jnp.broadcast_to(jnp.sum(dgk, axis=0, keepdims=True), gbias_ref.shape)

    def rev(i):
        return ns - 1 - i

    return pl.pallas_call(
        body, name="gla_bwd", grid=(ns,),
        in_specs=_gla_specs(rev) + [
            pl.BlockSpec((NCH * ST_ROWS, B_DK), lambda i: (rev(i), 0)),
            pl.BlockSpec((GSTEP, D), lambda i: (rev(i), 0)),
        ],
        out_specs=[
            pl.BlockSpec((GSTEP, W_GLA), lambda i: (rev(i), 0)),
            pl.BlockSpec((GSTEP, W_BL), lambda i: (rev(i), 0)),
            pl.BlockSpec((W_BL, 512), lambda i: (0, 0)),
            pl.BlockSpec((8, 512), lambda i: (0, 0)),
        ],
        out_shape=[
            jax.ShapeDtypeStruct((T, W_GLA), MXU),
            jax.ShapeDtypeStruct((T, W_BL), MXU),
            jax.ShapeDtypeStruct((W_BL, 512), F32),
            jax.ShapeDtypeStruct((8, 512), F32),
        ],
        scratch_shapes=[pltpu.VMEM((B_HEADS * B_DV, B_DK), F32)],
        compiler_params=_cp(("arbitrary",)),
    )(proj, proj, proj, proj, gu_pad, bias, states, do_b)


def _mid(x, target, proj, o_a, o_b, late, w_bn4, fnw):
    T = x.shape[0]
    tT = min(T, 128)
    nbuf = 4
    o_ag, o_bg, o_ma, o_mb = (c - C_GATES for c in (C_AG, C_BG, C_MA, C_MB))

    def body(x_ref, t_ref, oa_ref, ob_ref, gates_ref, late_ref, wbn_ref, fnw_ref,
             dx2_ref, doa_ref, dob_ref, dgates_ref,
             tail0_ref, tail1_ref, gfn_ref, gbn_ref, loss_ref, buf_ref, gw_ref):
        i = pl.program_id(0)

        def weight(p):
            return late_ref[:, 128 * p:128 * (p + 1), :].reshape(D, D)

        @pl.when(i == 0)
        def _():
            for r in (gw_ref, gfn_ref, gbn_ref, loss_ref):
                r[...] = jnp.zeros_like(r)

        rows = pl.ds(pl.multiple_of((i % nbuf) * tT, tT), tT)

        def keep(k, val):
            buf_ref[k, rows, :] = val

        oa, ag = oa_ref[...], gates_ref[:, o_ag:o_ag + D]
        sg_a = _sigmoid(ag)
        silu_a = ag * sg_a
        oag_b = (oa * silu_a).astype(MXU)
        keep(0, oag_b)
        y_a = _dot(oag_b, weight(0))

        ob, bg = ob_ref[...], gates_ref[:, o_bg:o_bg + D]
        rbs, obhats = [], []
        for h in range(B_HEADS):
            obh = ob[:, h * B_DV:(h + 1) * B_DV]
            rb = lax.rsqrt(jnp.mean(obh * obh, axis=-1, keepdims=True) + EPS)
            rbs.append(rb)
            obhats.append(obh * rb)
        obhat = jnp.concatenate(obhats, axis=1)
        wbn = wbn_ref[...]
        obn = obhat * wbn
        sg_b = _sigmoid(bg)
        silu_b = bg * sg_b
        obg_b = (obn * silu_b).astype(MXU)
        keep(1, obg_b)
        y_b = _dot(obg_b, weight(1))

        sa, sb = _sigmoid(gates_ref[:, o_ma:o_ma + D]), _sigmoid(gates_ref[:, o_mb:o_mb + D])
        mg_b = (sa * y_a + sb * y_b).astype(MXU)
        keep(2, mg_b)
        x2 = x_ref[...] + _dot(mg_b, weight(2))
        r2 = lax.rsqrt(jnp.mean(x2 * x2, axis=-1, keepdims=True) + EPS)
        xh2 = x2 * r2
        fw = fnw_ref[...]
        err = xh2 * fw - t_ref[...]
        tok = jnp.mean(err * err, axis=-1, keepdims=True)
        loss_ref[...] = loss_ref[...] + 0.5 * jnp.sum(tok, axis=0, keepdims=True)

        dy = err * (1.0 / D)
        gfn_ref[...] = gfn_ref[...] + jnp.broadcast_to(jnp.sum(dy * xh2, axis=0, keepdims=True), gfn_ref.shape)
        gy = dy * fw
        dx2 = r2 * (gy - xh2 * jnp.mean(gy * xh2, axis=-1, keepdims=True))
        dx2_ref[...] = dx2
        dx2_b = dx2.astype(MXU)
        keep(5, dx2_b)
        dmg = _dot_nt(dx2_b, weight(2))

        dgates_ref[:, o_ma:o_ma + D] = (dmg * y_a * sa * (1.0 - sa)).astype(dgates_ref.dtype)
        dgates_ref[:, o_mb:o_mb + D] = (dmg * y_b * sb * (1.0 - sb)).astype(dgates_ref.dtype)
        dya_b = (dmg * sa).astype(MXU)
        dyb_b = (dmg * sb).astype(MXU)
        keep(3, dya_b)
        keep(4, dyb_b)
        doag = _dot_nt(dya_b, weight(0))
        dobg = _dot_nt(dyb_b, weight(1))

        @pl.when(i % nbuf == nbuf - 1)
        def _():
            for p in range(3):
                gw_ref[p] = gw_ref[p] + _dot_tn(buf_ref[p], buf_ref[3 + p])

        @pl.when(i == pl.num_programs(0) - 1)
        def _():
            for hf, tail_ref in enumerate((tail0_ref, tail1_ref)):
                for d in range(NDEV):
                    for p in range(3):
                        tail_ref[d, 128 * p:128 * (p + 1), :] = (
                            gw_ref[p, 128 * d:128 * (d + 1), hf * DH:(hf + 1) * DH].astype(tail_ref.dtype))

        doa_ref[...] = doag * silu_a
        dgates_ref[:, o_ag:o_ag + D] = (doag * oa * (sg_a * (1.0 + ag * (1.0 - sg_a)))).astype(dgates_ref.dtype)
        dobn = dobg * silu_b
        dgates_ref[:, o_bg:o_bg + D] = (dobg * obn * (sg_b * (1.0 + bg * (1.0 - sg_b)))).astype(dgates_ref.dtype)
        gg = dobn * wbn
        gbn = jnp.zeros((1, B_DV), F32)
        for h in range(B_HEADS):
            sl = slice(h * B_DV, (h + 1) * B_DV)
            gbn = gbn + jnp.sum(dobn[:, sl] * obhats[h], axis=0, keepdims=True)
            ggh = gg[:, sl]
            dob_ref[:, sl] = rbs[h] * (ggh - obhats[h] * jnp.mean(ggh * obhats[h], axis=-1, keepdims=True))
        gbn_ref[...] = gbn_ref[...] + jnp.broadcast_to(gbn, gbn_ref.shape)

    assert (T // tT) % nbuf == 0
    tile = pl.BlockSpec((tT, D), lambda i: (i, 0))
    row = pl.BlockSpec((1, D), lambda i: (0, 0))
    acc8 = pl.BlockSpec((8, D), lambda i: (0, 0))
    return pl.pallas_call(
        body, name="mid", grid=(T // tT,),
        in_specs=[tile, tile, tile, tile, pl.BlockSpec((tT, W_GATES), lambda i: (i, C_GATES // W_GATES)),
                  _vmem(), row, row],
        out_specs=[tile, tile, tile, pl.BlockSpec((tT, W_GATES), lambda i: (i, 0)), _vmem(), _vmem(),
                   acc8, pl.BlockSpec((8, B_DV), lambda i: (0, 0)), pl.BlockSpec((8, LANE), lambda i: (0, 0))],
        out_shape=[
            jax.ShapeDtypeStruct((T, D), F32),
            jax.ShapeDtypeStruct((T, D), F32),
            jax.ShapeDtypeStruct((T, D), F32),
            jax.ShapeDtypeStruct((T, W_GATES), MXU),
            jax.ShapeDtypeStruct((NDEV, 384, DH), WIRE),
            jax.ShapeDtypeStruct((NDEV, 384, DH), WIRE),
            jax.ShapeDtypeStruct((8, D), F32),
            jax.ShapeDtypeStruct((8, B_DV), F32),
            jax.ShapeDtypeStruct((8, LANE), F32),
        ],
        scratch_shapes=[pltpu.VMEM((6, nbuf * tT, D), MXU), pltpu.VMEM((3, D, D), F32)],
        compiler_params=_cp(("arbitrary",)),
    )(x, target, o_a, o_b, proj, late, w_bn4, fnw)


DH = D // 2


_GW_TILES = (("q", 0, 512, 0), ("q", 1, 512, 512), ("kv", 0, 256, 1024), ("bl", 0, RANK, 5376),
             ("gla", 0, 512, 3328), ("gla", 1, 512, 3840), ("gla", 2, 512, 2304), ("gla", 3, 512, 2816),
             ("gates", 0, 512, 1280), ("gates", 1, 512, 1792), ("gates", 2, 512, 4352), ("gates", 3, 512, 4864),
             ("gates", 4, 512, 5392), ("gates", 5, 512, 5904), ("gates", 6, 512, 6416), ("gates", 7, 512, 6928))


def _gw_unpermute(piece, t):
    if piece == "q":
        parts = []
        for blk in range(t.shape[0] // LANE):
            g = [t[blk * LANE + 32 * i:blk * LANE + 32 * (i + 1)] for i in range(4)]
            parts += [g[0], g[2], g[1], g[3]]
        return jnp.concatenate(parts, axis=0)
    if piece == "kv":
        k = [t[64 * i:64 * i + 32] + t[64 * i + 32:64 * i + 64] for i in range(4)]
        v = [t[256 + 128 * g:256 + 128 * g + 64] + t[256 + 128 * g + 64:256 + 128 * (g + 1)] for g in range(2)]
        return jnp.concatenate(k + v, axis=0)
    if piece == "bl":
        return t[:RANK]
    return t


def _gw_half(h, pieces, half, after=None):
    T = h.shape[0]
    steps = len(_GW_TILES)

    def body(*refs):
        h_ref = refs[0]
        srcs = dict(zip(("q", "kv", "bl", "gla", "gates"), refs[1:6]))
        o_ref, stage, sems = refs[-3:]
        j = pl.program_id(0)

        def out_copy(k):
            _, _, n, off = _GW_TILES[k]
            return pltpu.make_async_copy(stage.at[k % 2, 0:n], o_ref.at[pl.ds(off, n)], sems.at[k % 2])

        for k, (piece, _, n, _) in enumerate(_GW_TILES):
            @pl.when(j == k)
            def _(k=k, piece=piece, n=n):
                if k >= 2:
                    out_copy(k - 2).wait()
                t = _gw_unpermute(piece, _dot_tn(srcs[piece][...], h_ref[...]))
                stage[k % 2, 0:n, :] = t.astype(stage.dtype)
                out_copy(k).start()

        @pl.when(j == steps - 1)
        def _():
            out_copy(steps - 2).wait()
            out_copy(steps - 1).wait()

    def tile_of(lo, hi):
        return lambda j: (0, jnp.clip(j - lo, 0, hi - lo - 1))

    in_specs = [pl.BlockSpec((T, DH), lambda j: (0, half)),
                pl.BlockSpec((T, 512), tile_of(0, 2)), pl.BlockSpec((T, 512), lambda j: (0, 0)),
                pl.BlockSpec((T, W_BL), lambda j: (0, 0)),
                pl.BlockSpec((T, 512), tile_of(4, 8)), pl.BlockSpec((T, 512), tile_of(8, 16))]
    args = [h, *pieces]
    if after is not None:
        in_specs.append(_any())
        args.append(after)
    return pl.pallas_call(
        body, name=f"gw_in_half{half}", grid=(steps,),
        in_specs=in_specs, out_specs=_any(),
        out_shape=jax.ShapeDtypeStruct((IN_WIDTH, DH), WIRE),
        scratch_shapes=[pltpu.VMEM((2, 512, DH), WIRE), pltpu.SemaphoreType.DMA((2,))],
        compiler_params=_cp(("arbitrary",)),
    )(*args)


def _chip_copies(s_ref, got_ref, send_sems, recv_sems):
    x, y, c = _place()
    chips = [(1 - x, y), (x, 1 - y), (1 - x, 1 - y)]
    return [pltpu.make_async_remote_copy(
        src_ref=s_ref.at[2 * px + py], dst_ref=got_ref.at[j],
        send_sem=send_sems.at[j], recv_sem=recv_sems.at[j], device_id=(px, py, c), device_id_type=MESH)
        for j, (px, py) in enumerate(chips)]


_EFFECT = pltpu.SideEffectType.DATAFLOW_SIDE_EFFECTING


def _hbm():
    return pl.BlockSpec(memory_space=pltpu.HBM)


def _sem():
    return pl.BlockSpec(memory_space=pltpu.SEMAPHORE)


def _chip_start(sums, half):
    land = pltpu.with_memory_space_constraint(lax.empty((3,) + sums.shape[1:], sums.dtype), pltpu.HBM)

    def body(s_ref, land_ref, send_sems, recv_sems, s_thru, land_thru, token):
        for cp in _chip_copies(s_ref, land_ref, send_sems, recv_sems):
            cp.start()
        token[...] = jnp.zeros_like(token)

    return pl.pallas_call(
        body, name=f"chip_start{half}",
        out_shape=(pltpu.SemaphoreType.DMA((3,)), pltpu.SemaphoreType.DMA((3,)),
                   pltpu.HBM(sums.shape, sums.dtype), pltpu.HBM(land.shape, land.dtype),
                   jax.ShapeDtypeStruct((8, LANE), F32)),
        in_specs=(_hbm(), _hbm()), out_specs=(_sem(), _sem(), _hbm(), _hbm(), _vmem()),
        input_output_aliases={0: 2, 1: 3},
        compiler_params=pltpu.CompilerParams(has_side_effects=_EFFECT),
    )(pltpu.with_memory_space_constraint(sums, pltpu.HBM), land)


def _chip_wait(send_sems, recv_sems, s_thru, land_thru, after, half):
    def body(s_ref, land_ref, send_sems, recv_sems, after_ref, s_out, got_ref):
        copies = _chip_copies(s_ref, land_ref, send_sems, recv_sems)
        for cp in copies:
            cp.wait_send()
        for cp in copies:
            cp.wait_recv()

    return pl.pallas_call(
        body, name=f"chip_wait{half}",
        out_shape=(pltpu.HBM(s_thru.shape, s_thru.dtype), pltpu.HBM(land_thru.shape, land_thru.dtype)),
        in_specs=(_hbm(), _hbm(), _sem(), _sem(), _any()), out_specs=(_hbm(), _hbm()),
        input_output_aliases={0: 0, 1: 1},
        compiler_params=pltpu.CompilerParams(has_side_effects=_EFFECT),
    )(s_thru, land_thru, send_sems, recv_sems, after)


def _dh_norm(pieces, offsets, wf, x, dx2, norm_w, after):
    T = x.shape[0]
    tT = min(T, 256)
    widths = [p.shape[1] for p in pieces]
    npc = len(pieces)

    def body(*refs):
        dp_refs = refs[:npc]
        wf_ref, x_ref, dx2_ref, nw_ref, _, gx_ref, gnw_ref = refs[npc:]

        @pl.when(pl.program_id(0) == 0)
        def _():
            gnw_ref[...] = jnp.zeros_like(gnw_ref)

        dh = jnp.zeros((tT, D), F32)
        for dp_ref, off, w in zip(dp_refs, offsets, widths):
            dh = dh + _dot(dp_ref[...], wf_ref[off:off + w, :])
        xv = x_ref[...]
        r = lax.rsqrt(jnp.mean(xv * xv, axis=-1, keepdims=True) + EPS)
        xh = xv * r
        gnw_ref[...] = gnw_ref[...] + jnp.broadcast_to(jnp.sum(dh * xh, axis=0, keepdims=True), gnw_ref.shape)
        g = dh * nw_ref[...]
        gx_ref[...] = r * (g - xh * jnp.mean(g * xh, axis=-1, keepdims=True)) + dx2_ref[...]

    tile = pl.BlockSpec((tT, D), lambda i: (i, 0))
    return pl.pallas_call(
        body, name="dh_norm", grid=(T // tT,),
        in_specs=[pl.BlockSpec((tT, w), lambda i: (i, 0)) for w in widths]
        + [_vmem(), tile, tile, pl.BlockSpec((1, D), lambda i: (0, 0)), _any()],
        out_specs=[tile, pl.BlockSpec((8, D), lambda i: (0, 0))],
        out_shape=[jax.ShapeDtypeStruct((T, D), F32), jax.ShapeDtypeStruct((8, D), F32)],
        compiler_params=_cp(("arbitrary",)),
    )(*pieces, wf, x, dx2, norm_w, after)


def _adamw_math(w, g, m, v):
    m = ADAM_B1 * m + (1.0 - ADAM_B1) * g
    v = ADAM_B2 * v + (1.0 - ADAM_B2) * (g * g)
    m_hat = m * (1.0 / (1.0 - ADAM_B1 ** ADAM_STEP))
    v_hat = v * (1.0 / (1.0 - ADAM_B2 ** ADAM_STEP))
    delta = -ADAM_LR * (m_hat / (jnp.sqrt(v_hat) + ADAM_EPS) + ADAM_WD * w)
    return delta, m, v


def _fetch_partials(s_ref, got_ref, buf, sems):
    x, y, _ = _place()
    cps = [pltpu.make_async_copy(s_ref.at[2 * x + y], buf.at[0], sems.at[0])]
    cps += [pltpu.make_async_copy(got_ref.at[j], buf.at[1 + j], sems.at[1 + j]) for j in range(3)]
    for cp in cps:
        cp.start()
    return cps


SMALL_AT = dict(norm_w=0, fnw=8, bias=16, bn=24, sinks=32, loss=40)
ROW_AT = (R_IN, R_A, R_B, R_O)


def _finish_small(ws, ms, vs, smalls):
    names = ["norm_w", "fnw", "bias", "bn", "sinks"]
    widths = [ws[n].shape[1] for n in names]

    def body(*refs):
        w_refs, m_refs, v_refs = refs[0:5], refs[5:10], refs[10:15]
        smalls_ref, loss_ref = refs[15], refs[16]
        outs, tot = refs[17:37], refs[37]
        acc = smalls_ref[0]
        for d in range(1, NDEV):
            acc = acc + smalls_ref[d]
        tot[...] = acc
        loss_ref[...] = tot[SMALL_AT["loss"]:SMALL_AT["loss"] + 1, 0:1]
        for p, (nm_, wd) in enumerate(zip(names, widths)):
            r = SMALL_AT[nm_]
            g = tot[r:r + 1, 0:wd]
            d, nm, nv = _adamw_math(w_refs[p][...], g, m_refs[p][...], v_refs[p][...])
            for o, val in zip(outs[4 * p:4 * p + 4], (g, d, nm, nv)):
                o[...] = val

    res = pl.pallas_call(
        body, name="finish_small",
        in_specs=[_vmem()] * 16, out_specs=[_vmem()] * 21,
        out_shape=[jax.ShapeDtypeStruct((1, 1), F32)]
        + [jax.ShapeDtypeStruct((1, wd), F32) for wd in widths for _ in range(4)],
        scratch_shapes=[pltpu.VMEM((SMALL_ROWS, D), F32)],
        compiler_params=_cp(),
    )(*[ws[n] for n in names], *[ms[n] for n in names], *[vs[n] for n in names], smalls)
    return res[0], {n: tuple(res[1 + 4 * p:5 + 4 * p]) for p, n in enumerate(names)}


def _finish(w_rows, m_rows, v_rows, gu_w, gu_m, gu_v, sums, got):
    shapes = [(SHARD, 1, D)] + [w.shape for w in w_rows[1:]]

    row_block = 96

    def columns(ref, p, cols, r0, n):
        if p:
            return ref, (slice(r0, r0 + n), cols)
        flat = ref if ref.shape == (SHARD * LANE_TILES, LANE) else ref.reshape(SHARD * LANE_TILES, LANE)
        return flat, (pl.ds(cols.start // LANE + LANE_TILES * r0, n, stride=LANE_TILES), slice(None))

    def read(ref, p, cols, r0, n):
        ref, at = columns(ref, p, cols, r0, n)
        return ref[at]

    def body(*refs):
        wr_refs, mr_refs, vr_refs = refs[0:4], refs[4:8], refs[8:12]
        guw_ref, gum_ref, guv_ref = refs[12:15]
        s_refs, got_refs = refs[15:17], refs[17:19]
        row_outs = refs[19:35]
        gu_outs = refs[35:39]
        bufs, gsh, sems, big, big_sems = refs[39:]
        loads = [pltpu.make_async_copy(r[0], big.at[k], big_sems.at[k]) for k, r in enumerate((wr_refs, mr_refs, vr_refs))]
        for cp in loads:
            cp.start()
        wr_refs, mr_refs, vr_refs = ((big.at[k],) + tuple(r[1:]) for k, r in enumerate((wr_refs, mr_refs, vr_refs)))
        x, y, c = _place()
        me_slot = 4 * x + 2 * y + c
        down = 2 * me_slot

        fetches = [_fetch_partials(s_refs[hf], got_refs[hf], bufs.at[hf], sems.at[hf]) for hf in range(2)]

        def total(rows, cols):
            g = buf[0, rows, cols].astype(F32)
            for j in range(1, 4):
                g = g + buf[j, rows, cols].astype(F32)
            return g

        def update(p, grad, cols):
            nrows = shapes[p][0]
            for r0 in range(0, nrows, row_block):
                n = min(row_block, nrows - r0)
                g = grad(r0, n)
                d, nm, nv = _adamw_math(read(wr_refs[p], p, cols, r0, n), g, read(mr_refs[p], p, cols, r0, n),
                                        read(vr_refs[p], p, cols, r0, n))
                for o, val in zip(row_outs[4 * p:4 * p + 4], (g, d, nm, nv)):
                    o, at = columns(o, p, cols, r0, n)
                    o[at] = val

        for hf in range(2):
            for cp in fetches[hf]:
                cp.wait()
            buf = bufs.at[hf]
            for cc in range(DH // LANE):
                src = slice(cc * LANE, (cc + 1) * LANE)
                cols = slice(hf * DH + cc * LANE, hf * DH + (cc + 1) * LANE)
                for r0 in range(0, SHARD_PAD, row_block):
                    rows = slice(r0, min(r0 + row_block, SHARD_PAD))
                    gsh[rows, :] = total(rows, src)
                if hf == 0 and cc == 0:
                    for cp in loads:
                        cp.wait()
                update(0, lambda r0, n: gsh[pl.ds(down + r0, n), :], cols)
                for p in range(1, 4):
                    update(p, lambda r0, n, p=p: total(slice(ROW_AT[p] + r0, ROW_AT[p] + r0 + n), src), cols)
            if hf == 0:
                g = total(slice(R_GU, R_GU + RANK), slice(0, 64))
                d, nm, nv = _adamw_math(guw_ref[...], g, gum_ref[...], guv_ref[...])
                for o, val in zip(gu_outs, (g, d, nm, nv)):
                    o[...] = val

    res = pl.pallas_call(
        body, name="finish",
        in_specs=([_any()] + [_vmem()] * 3) * 3 + [_vmem()] * 3 + [_any()] * 4,
        out_specs=[_vmem()] * 20,
        out_shape=[jax.ShapeDtypeStruct(s, F32) for s in shapes for _ in range(4)]
        + [jax.ShapeDtypeStruct((RANK, 64), F32)] * 4,
        scratch_shapes=[pltpu.VMEM((2, 4, ROWS, DH), sums[0].dtype), pltpu.VMEM((SHARD_PAD, LANE), F32),
                        pltpu.SemaphoreType.DMA((2, 4)),
                        pltpu.VMEM((3, SHARD * LANE_TILES, LANE), F32), pltpu.SemaphoreType.DMA((3,))],
        compiler_params=_cp(),
    )(*w_rows, *m_rows, *v_rows, gu_w, gu_m, gu_v, *sums, *got)
    return tuple(res[0:16]), tuple(res[16:20])


def _place():
    x, y, c = lax.axis_index("x"), lax.axis_index("y"), lax.axis_index("c")
    return x, y, c


def _peers(x, y, c):
    return [(x ^ dx, y ^ dy, c ^ dc) for dx in range(2) for dy in range(2) for dc in range(2) if dx + dy + dc]


def _late_gather_start(blk, after, name="late_gather"):
    land = pltpu.with_memory_space_constraint(lax.empty((NDEV,) + blk.shape, blk.dtype), pltpu.HBM)

    def body(b_ref, land_ref, after_ref, send_sems, recv_sems, b_thru, land_thru, token):
        x, y, c = _place()
        for k, to in enumerate(_peers(x, y, c)):
            pltpu.make_async_remote_copy(
                src_ref=b_ref, dst_ref=land_ref.at[4 * x + 2 * y + c], send_sem=send_sems.at[k],
                recv_sem=recv_sems.at[k], device_id=to, device_id_type=MESH).start()
        token[...] = jnp.zeros_like(token)

    return pl.pallas_call(
        body, name=name + "_start",
        out_shape=(pltpu.SemaphoreType.DMA((7,)), pltpu.SemaphoreType.DMA((7,)),
                   pltpu.HBM(blk.shape, blk.dtype), pltpu.HBM(land.shape, land.dtype),
                   jax.ShapeDtypeStruct((8, LANE), F32)),
        in_specs=(_hbm(), _hbm(), _any()), out_specs=(_sem(), _sem(), _hbm(), _hbm(), _vmem()),
        input_output_aliases={0: 2, 1: 3},
        compiler_params=pltpu.CompilerParams(has_side_effects=_EFFECT),
    )(pltpu.with_memory_space_constraint(blk, pltpu.HBM), land, after)


def _late_gather_wait(send_sems, recv_sems, b_thru, land_thru, after, after2, name="late_gather"):
    def body(b_ref, land_ref, send_sems, recv_sems, after_ref, after2_ref, b_out, got_ref):
        x, y, c = _place()
        copies = [pltpu.make_async_remote_copy(
            src_ref=b_ref, dst_ref=land_ref.at[4 * x + 2 * y + c], send_sem=send_sems.at[k],
            recv_sem=recv_sems.at[k], device_id=to, device_id_type=MESH)
            for k, to in enumerate(_peers(x, y, c))]
        for cp in copies:
            cp.wait_send()
        for cp in copies:
            cp.wait_recv()

    return pl.pallas_call(
        body, name=name + "_wait",
        out_shape=(pltpu.HBM(b_thru.shape, b_thru.dtype), pltpu.HBM(land_thru.shape, land_thru.dtype)),
        in_specs=(_hbm(), _hbm(), _sem(), _sem(), _any(), _any()), out_specs=(_hbm(), _hbm()),
        input_output_aliases={0: 0, 1: 1},
        compiler_params=pltpu.CompilerParams(has_side_effects=_EFFECT),
    )(b_thru, land_thru, send_sems, recv_sems, after, after2)


G_ROWS = SHARD_PAD + RANK


def _gather_blocks(w_in_t, gu_s, xs, norm_w, pos_col):
    rows, cols = G_ROWS, D
    T = xs.shape[0]
    tT = min(T, 256)
    inv_row, sign_row = _rope_rows()

    def body(wi_ref, gu_ref, xs_hbm, nw_ref, pos_ref, inv_ref, sign_ref,
             out_ref, h_ref, cos_ref, sin_ref, x_ref, frame_ref, xs_ref, send_sems, recv_sems, local_sem, xs_sem):
        load_xs = pltpu.make_async_copy(xs_hbm, xs_ref, xs_sem)
        load_xs.start()
        x, y, c = _place()
        me, sibling = (x, y, c), (x, y, 1 - c)
        chips = [(1 - x, y), (x, 1 - y), (1 - x, 1 - y)]
        shift = 2 * (4 * x + 2 * y + c)
        frame_ref[SHARD - SHARD % 8:, :] = jnp.zeros((SHARD_PAD - SHARD + SHARD % 8, LANE), F32)
        for cc in range(LANE_TILES):
            cs = slice(cc * LANE, (cc + 1) * LANE)
            frame_ref[:SHARD, :] = wi_ref[pl.ds(cc, SHARD, stride=LANE_TILES), :]
            x_ref[0:SHARD_PAD, cs] = pltpu.roll(frame_ref[...], shift, 0).astype(x_ref.dtype)
        x_ref[SHARD_PAD:G_ROWS, :] = jnp.zeros((RANK, D), x_ref.dtype)
        x_ref[SHARD_PAD:G_ROWS, 0:64] = gu_ref[...].astype(x_ref.dtype)

        def slot(px, py, pc):
            return out_ref.at[4 * px + 2 * py + pc]

        def copy(k, block, to, src=None):
            return pltpu.make_async_remote_copy(
                src_ref=slot(*block) if src is None else src, dst_ref=slot(*block),
                send_sem=send_sems.at[k], recv_sem=recv_sems.at[k], device_id=to, device_id_type=MESH)

        mine = pltpu.make_async_copy(x_ref, slot(*me), local_sem)
        mine.start()
        first = [copy(0, me, sibling, src=x_ref)]
        first += [copy(1 + j, me, (*chip, c), src=x_ref) for j, chip in enumerate(chips)]
        for cp in first:
            cp.start()
        load_xs.wait()

        @pl.loop(0, T // tT)
        def _(i):
            rows_i = pl.ds(pl.multiple_of(i * tT, tT), tT)
            _prologue_rows(rows_i, xs_ref, nw_ref, pos_ref, inv_ref, sign_ref, h_ref, cos_ref, sin_ref)

        passed = [copy(4 + j, (*chip, c), sibling) for j, chip in enumerate(chips)]
        for j, chip in enumerate(chips):
            copy(1 + j, (*chip, c), me).wait_recv()
            passed[j].start()
        copy(0, sibling, me).wait_recv()
        for j, chip in enumerate(chips):
            copy(4 + j, (*chip, 1 - c), me).wait_recv()
        for cp in first + passed:
            cp.wait_send()
        mine.wait()

    return pl.pallas_call(
        body, name="gather_weights",
        in_specs=[_vmem(), _vmem(), _any()] + [_vmem()] * 4, out_specs=[_any()] + [_vmem()] * 3,
        out_shape=[jax.ShapeDtypeStruct((NDEV, rows, cols), WIRE), jax.ShapeDtypeStruct((T, D), MXU),
                   jax.ShapeDtypeStruct((T, LANE), F32), jax.ShapeDtypeStruct((T, LANE), F32)],
        scratch_shapes=[pltpu.VMEM((rows, cols), WIRE), pltpu.VMEM((SHARD_PAD, LANE), F32), pltpu.VMEM((T, D), F32),
                        pltpu.SemaphoreType.DMA((7,)), pltpu.SemaphoreType.DMA((7,)), pltpu.SemaphoreType.DMA,
                        pltpu.SemaphoreType.DMA],
        compiler_params=_cp(),
    )(w_in_t, gu_s, xs, norm_w, pos_col, inv_row, sign_row)


def _pair_reduce(gwt, tails, half):
    n = gwt.shape[1]
    starts = [SHARD_PAD]
    for t in tails:
        starts.append(starts[-1] + t.shape[1])
    rows = starts[-1]
    blk = (4, rows, n)
    npart = 1 + len(tails)

    def body(*refs):
        g_ref, t_refs = refs[0], refs[1:npart]
        out_ref, acc, got, own, send_sems, recv_sems, own_sems, out_sems = refs[npart:]
        x, y, c = _place()

        def parts(d, dst):
            frame = g_ref.at[pl.ds(pl.multiple_of(FRAME * d, 16), SHARD_PAD)]
            return [(frame, dst.at[0:SHARD_PAD])] + [
                (t_ref.at[d], dst.at[starts[k]:starts[k + 1]]) for k, t_ref in enumerate(t_refs)]

        sends, loads, stores = [], [], []
        for chip in range(4):
            sends.append([pltpu.make_async_remote_copy(
                src_ref=s, dst_ref=d_, send_sem=send_sems.at[chip, k], recv_sem=recv_sems.at[chip, k],
                device_id=(x, y, 1 - c), device_id_type=MESH)
                for k, (s, d_) in enumerate(parts(2 * chip + (1 - c), got.at[chip]))])
            loads.append([pltpu.make_async_copy(s, d_, own_sems.at[chip, k])
                          for k, (s, d_) in enumerate(parts(2 * chip + c, own.at[chip]))])
            stores.append(pltpu.make_async_copy(acc.at[chip], out_ref.at[chip], out_sems.at[chip]))
        for group in sends + loads:
            for cp in group:
                cp.start()
        for chip in range(4):
            for cp in loads[chip]:
                cp.wait()
            for cp in sends[chip]:
                cp.wait_recv()
            acc[chip] = (own[chip].astype(F32) + got[chip].astype(F32)).astype(acc.dtype)
            stores[chip].start()
        for cp in stores:
            cp.wait()
        for group in sends:
            for cp in group:
                cp.wait_send()

    return pl.pallas_call(
        body, name=f"pair_reduce{half}",
        in_specs=[_any()] * npart, out_specs=_any(),
        out_shape=jax.ShapeDtypeStruct(blk, gwt.dtype),
        scratch_shapes=[pltpu.VMEM(blk, gwt.dtype), pltpu.VMEM(blk, gwt.dtype), pltpu.VMEM(blk, gwt.dtype),
                        pltpu.SemaphoreType.DMA((4, npart)), pltpu.SemaphoreType.DMA((4, npart)),
                        pltpu.SemaphoreType.DMA((4, npart)), pltpu.SemaphoreType.DMA((4,))],
        compiler_params=_cp(),
    )(gwt, *tails)


def _pad_cols(a, cols):
    return jnp.pad(a, ((0, 0), (0, cols - a.shape[1])))


def _pad_rows(a, rows):
    return jnp.pad(a, ((0, rows - a.shape[0]), (0, 0)))


FRAME = 928


def _wft_plan():
    moves = []
    for blk in range(8):
        for half in range(2):
            for sub in range(2):
                moves.append((C_Q + 128 * blk + 32 * (2 * half + sub), 128 * blk + 32 * (2 * sub + half), 32))
    for idx in range(4):
        for dup in range(2):
            moves.append((C_KD + 64 * idx + 32 * dup, 1024 + 32 * idx, 32))
    for g in range(2):
        for dup in range(2):
            moves.append((C_VD + 128 * g + 64 * dup, 1152 + 64 * g, 64))
    moves += [(C_BL, 5376, RANK), (C_BV, 3328, 1024), (C_BQ, 2304, 512), (C_BK, 2816, 512),
              (C_AG, 1280, 1024), (C_BG, 4352, 1024), (C_MA, 5392, 1024), (C_MB, 6416, 1024)]
    bulk, seams = [], []
    for dst, src, n in moves:
        r = src
        while r < src + n:
            f = min(r // FRAME, NDEV - 1)
            local = r - FRAME * f
            if f > 0 and local < 16:
                assert local == 0
                seams.append((f, dst + r - src))
                step = 16
            else:
                step = min(src + n, FRAME * (f + 1) if f < NDEV - 1 else IN_WIDTH) - r
                bulk.append((f, local, dst + r - src, step))
            r += step
    assert sorted(f for f, _ in seams) == list(range(1, NDEV))
    return bulk, seams, [(C_BL + RANK, C_GLA - C_BL - RANK)]


def _build_wft_copies(frames):
    bulk, seams, zeros = _wft_plan()
    (z0, zn), = zeros

    def body(f_ref, o_ref, edge, sems, esems):
        copies = [pltpu.make_async_copy(f_ref.at[f, pl.ds(l0, n)], o_ref.at[pl.ds(dst, n)], sems.at[i])
                  for i, (f, l0, dst, n) in enumerate(bulk)]
        loads = []
        for i, (f, _) in enumerate(seams):
            loads.append(pltpu.make_async_copy(f_ref.at[f, pl.ds(0, 16)], edge.at[i, 0], esems.at[i, 0]))
            loads.append(pltpu.make_async_copy(f_ref.at[f - 1, pl.ds(FRAME, 16)], edge.at[i, 1], esems.at[i, 1]))
        for cp in copies + loads:
            cp.start()
        o_ref[z0:z0 + zn, :] = jnp.zeros((zn, D), o_ref.dtype)
        for cp in loads:
            cp.wait()
        for i, (_, dst) in enumerate(seams):
            o_ref[dst:dst + 16, :] = edge[i, 0] + edge[i, 1]
        for cp in copies:
            cp.wait()

    return pl.pallas_call(
        body, name="build_wft",
        in_specs=[_any()], out_specs=_vmem(),
        out_shape=jax.ShapeDtypeStruct((NF, D), frames.dtype),
        scratch_shapes=[pltpu.VMEM((len(seams), 2, 16, D), frames.dtype),
                        pltpu.SemaphoreType.DMA((len(bulk),)), pltpu.SemaphoreType.DMA((len(seams), 2))],
        compiler_params=_cp(),
    )(frames)


def kernel(x, positions, norm_w, w_in, a_sinks, b_gate_up, b_gate_bias, b_out_norm_w, w_a_proj, w_b_proj, w_out, final_norm_w, loss_target, m_norm_w, m_w_in, m_a_sinks, m_b_gate_up, m_b_gate_bias, m_b_out_norm_w, m_w_a_proj, m_w_b_proj, m_w_out, m_final_norm_w, v_norm_w, v_w_in, v_a_sinks, v_b_gate_up, v_b_gate_bias, v_b_out_norm_w, v_w_a_proj, v_w_b_proj, v_w_out, v_final_norm_w):
    T = x.shape[1]
    xs, target = x[0], loss_target[0]
    fnw = final_norm_w.reshape(1, D)
    me = 4 * lax.axis_index("x") + 2 * lax.axis_index("y") + lax.axis_index("c")
    allw, h, cos, sin = _gather_blocks(_by_lane_tile(w_in), b_gate_up[0], xs, norm_w, positions.reshape(T, 1))
    late_blk = jnp.concatenate([w_a_proj[0], w_b_proj[0], w_out[0]], axis=0).astype(WIRE)
    l_send, l_recv, l_blk, l_land, l_started = _late_gather_start(late_blk, cos)
    wf = _build_wft_copies(allw)
    gu = allw[:, SHARD_PAD:G_ROWS, :64].transpose(1, 0, 2).reshape(RANK, 512)
    gu_pad = _pad_rows(gu, W_BL)

    proj = _proj(h, wf, l_started)
    o_a, lse = _swa_fwd(proj, cos, sin, a_sinks)
    o_b, states = _gla_fwd(proj, gu_pad, b_gate_bias)
    l_blk, l_land = _late_gather_wait(l_send, l_recv, l_blk, l_land, states, lse)
    late = lax.dynamic_update_slice(l_land, l_blk[None], (me, 0, 0))
    (dx2, do_a, do_b, d_gates, g_late0, g_late1, g_fn, g_bn, loss_part) = _mid(
        xs, target, proj, o_a, o_b, late, jnp.tile(b_out_norm_w, (1, B_HEADS)), fnw)
    d_q, d_kv, g_sinks = _swa_bwd(proj, cos, sin, a_sinks, do_a, o_a, lse, cos)
    d_gla, d_bl, g_gu, g_bias = _gla_bwd(proj, gu_pad, b_gate_bias, states, do_b)
    pieces = [d_q, d_kv, d_bl, d_gla, d_gates]
    offsets = [C_Q, C_KD, C_BL, C_GLA, C_GATES]

    ggu = g_gu[:RANK].reshape(RANK, NDEV, 64).transpose(1, 0, 2)
    ggu_half = [jnp.pad(ggu, ((0, 0), (0, 0), (0, DH - 64))).astype(WIRE), jnp.zeros((NDEV, RANK, DH), WIRE)]
    tails = [[g_late0, ggu_half[0]], [g_late1, ggu_half[1]]]

    send0, recv0, s_thru0, land0, started0 = _chip_start(
        _pair_reduce(_gw_half(h, pieces, 0, after=g_bias), tails[0], 0), 0)
    send1, recv1, s_thru1, land1, started1 = _chip_start(
        _pair_reduce(_gw_half(h, pieces, 1, after=started0), tails[1], 1), 1)
    grad_x, g_nw = _dh_norm(pieces, offsets, wf, xs, dx2, norm_w, started1)
    small = jnp.concatenate([g_nw, g_fn, _pad_cols(g_bias, D), _pad_cols(g_bn, D), _pad_cols(g_sinks, D),
                             _pad_cols(loss_part, D)], axis=0)
    sm_send, sm_recv, sm_blk, sm_land, sm_started = _late_gather_start(small, g_nw, name="small_gather")
    sums0, got0 = _chip_wait(send0, recv0, s_thru0, land0, sm_started, 0)
    sums1, got1 = _chip_wait(send1, recv1, s_thru1, land1, got0, 1)
    sums, from_chips = [sums0, sums1], [got0, got1]

    ws = dict(norm_w=norm_w, fnw=fnw, bias=b_gate_bias, bn=b_out_norm_w, sinks=a_sinks)
    ms = dict(norm_w=m_norm_w, fnw=m_final_norm_w.reshape(1, D), bias=m_b_gate_bias, bn=m_b_out_norm_w,
              sinks=m_a_sinks)
    vs = dict(norm_w=v_norm_w, fnw=v_final_norm_w.reshape(1, D), bias=v_b_gate_bias, bn=v_b_out_norm_w,
              sinks=v_a_sinks)
    t_rows, t_gu = _finish(
        [_by_lane_tile(w_in), w_a_proj[0], w_b_proj[0], w_out[0]],
        [_by_lane_tile(m_w_in), m_w_a_proj[0], m_w_b_proj[0], m_w_out[0]],
        [_by_lane_tile(v_w_in), v_w_a_proj[0], v_w_b_proj[0], v_w_out[0]],
        b_gate_up[0], m_b_gate_up[0], v_b_gate_up[0], sums, from_chips)
    sm_blk, sm_land = _late_gather_wait(sm_send, sm_recv, sm_blk, sm_land, t_rows[0], t_gu[0], name="small_gather")
    loss, sm = _finish_small(ws, ms, vs, lax.dynamic_update_slice(sm_land, sm_blk[None], (me, 0, 0)))

    def outputs(k):
        return [sm["norm_w"][k], jnp.transpose(t_rows[k], (1, 2, 0)), sm["sinks"][k], t_gu[k][None], sm["bias"][k], sm["bn"][k],
                t_rows[4 + k][None], t_rows[8 + k][None], t_rows[12 + k][None], sm["fnw"][k].reshape(D)]

    return (loss[0, 0], grad_x[None], *outputs(0), *outputs(1), *outputs(2), *outputs(3))
```

```python
import functools

import numpy as np
import jax
import jax.numpy as jnp
from jax import lax
from jax.experimental import pallas as pl
from jax.experimental.pallas import tpu as pltpu

F32 = jnp.float32
MXU = jnp.bfloat16
WIRE = jnp.bfloat16

D = 1024
A_HEADS, A_KV, A_HD = 16, 2, 64
BLK = 128
B_HEADS, B_DK, B_DV = 4, 128, 256
RANK, TAU, CHUNK = 16, 16.0, 64
EPS, NEG = 1e-5, -1e30
ROPE_THETA = 10000.0
IN_WIDTH, NDEV = 7440, 8
SHARD = IN_WIDTH // NDEV
LANE = 128
LANE_TILES = D // LANE


def _by_lane_tile(a):
    return jnp.transpose(a, (2, 0, 1)).reshape(SHARD * LANE_TILES, LANE)


C_Q, C_KD, C_VD, C_BL = 0, 1024, 1280, 1536
C_BV, C_BQ, C_BK = 2048, 3072, 3584
C_AG, C_BG, C_MA, C_MB = 4096, 5120, 6144, 7168
C_GLA, W_GLA, C_GATES, W_GATES = 2048, 2048, 4096, 4096
NF = 8192
W_BL = 128

SHARD_PAD = 944
R_IN, R_A, R_B, R_O, R_GU, ROWS = 0, 944, 1072, 1200, 1328, 1344
SMALL_ROWS = 48

ADAM_LR, ADAM_B1, ADAM_B2, ADAM_EPS, ADAM_WD, ADAM_STEP = 0.001, 0.9, 0.999, 1e-08, 0.01, 10

MESH = pl.DeviceIdType.MESH
VMEM_LIMIT = 56 * 1024 * 1024


def _cp(sem=None, **kw):
    if sem is not None:
        kw["dimension_semantics"] = sem
    return pltpu.CompilerParams(vmem_limit_bytes=VMEM_LIMIT, **kw)


def _dot(a, b):
    return jnp.dot(a, b, preferred_element_type=F32)


def _dot_nt(a, b):
    return lax.dot_general(a, b, (((1,), (1,)), ((), ())), preferred_element_type=F32)


def _dot_tn(a, b):
    return lax.dot_general(a, b, (((0,), (0,)), ((), ())), preferred_element_type=F32)


def _dot_f32(a, b):
    return jnp.dot(a, b, preferred_element_type=F32, precision=lax.Precision.HIGHEST)


def _sigmoid(z):
    return 0.5 * jnp.tanh(0.5 * z) + 0.5


def _rope(xp, cos, sin):
    return xp * cos + pltpu.roll(xp, 64, 1) * sin


def _rope_bwd(dy, cos, sin):
    return dy * cos - pltpu.roll(dy, 64, 1) * sin


def _vmem():
    return pl.BlockSpec(memory_space=pltpu.VMEM)


def _any():
    return pl.BlockSpec(memory_space=pl.ANY)


def _rope_rows():
    half = A_HD // 2
    inv = (np.float32(ROPE_THETA) ** (-np.arange(half, dtype=np.float32) / np.float32(half))).astype(np.float32)
    inv_row = jnp.asarray(np.tile(inv, 4)[None, :])
    sign_row = jnp.asarray(np.concatenate([-np.ones(64, np.float32), np.ones(64, np.float32)])[None, :])
    return inv_row, sign_row


def _prologue_rows(rows, x_ref, nw_ref, pos_ref, inv_ref, sign_ref, h_ref, cos_ref, sin_ref):
    xv = x_ref[rows, :]
    r = lax.rsqrt(jnp.mean(xv * xv, axis=-1, keepdims=True) + EPS)
    h_ref[rows, :] = ((xv * r) * nw_ref[...]).astype(h_ref.dtype)
    ang = pos_ref[rows, :].astype(F32) * inv_ref[...]
    cos_ref[rows, :] = jnp.cos(ang)
    sin_ref[rows, :] = jnp.sin(ang) * sign_ref[...]


def _proj(h, wft, after):
    T = h.shape[0]
    tT, tN = T, 512

    def body(h_ref, w_ref, after_ref, o_ref):
        o_ref[...] = _dot_nt(h_ref[...], w_ref[...])

    return pl.pallas_call(
        body, name="proj", grid=(T // tT, NF // tN),
        in_specs=[pl.BlockSpec((tT, D), lambda i, j: (i, 0)), pl.BlockSpec((tN, D), lambda i, j: (j, 0)), _any()],
        out_specs=pl.BlockSpec((tT, tN), lambda i, j: (i, j)),
        out_shape=jax.ShapeDtypeStruct((T, NF), F32),
        compiler_params=_cp(("parallel", "parallel")),
    )(h, wft, after)


def _swa_masks():
    lane = lax.broadcasted_iota(jnp.int32, (BLK, LANE), 1)
    rope_sub0 = ((lane // 32) % 2) == 0
    std_sub0 = lane < 64
    return lane, rope_sub0, std_sub0


def _swa_tri():
    qi = lax.broadcasted_iota(jnp.int32, (BLK, BLK), 0)
    kj = lax.broadcasted_iota(jnp.int32, (BLK, BLK), 1)
    return kj <= qi


def _swa_fold(full, tri):
    return jnp.where(tri, full[:, BLK:], full[:, :BLK])


def _swa_unfold(sq, tri):
    return jnp.concatenate([jnp.where(tri, 0.0, sq), jnp.where(tri, sq, 0.0)], axis=1)


def _swa_keys(kc_ref, kp_ref, vc_ref, vp_ref, cq, sq, cp, sp):
    def ropek(kref, c, s):
        kv = kref[...]
        return jnp.concatenate([_rope(kv[:, :LANE], c, s), _rope(kv[:, LANE:], c, s)], axis=1)

    K = jnp.concatenate([ropek(kp_ref, cp, sp), ropek(kc_ref, cq, sq)], axis=0).astype(MXU)
    V = jnp.concatenate([vp_ref[...], vc_ref[...]], axis=0).astype(MXU)
    return K, V


def _swa_in_specs(nb, last):
    def cur(n):
        return jnp.minimum(n, last)

    def prev(n):
        return jnp.maximum(cur(n) - 1, 0)

    kd, vd = C_KD // 256, C_VD // 256
    return [
        pl.BlockSpec((BLK, D), lambda n: (cur(n), C_Q // D)),
        pl.BlockSpec((BLK, 256), lambda n: (cur(n), kd)),
        pl.BlockSpec((BLK, 256), lambda n: (prev(n), kd)),
        pl.BlockSpec((BLK, 256), lambda n: (cur(n), vd)),
        pl.BlockSpec((BLK, 256), lambda n: (prev(n), vd)),
        pl.BlockSpec((BLK, LANE), lambda n: (cur(n), 0)),
        pl.BlockSpec((BLK, LANE), lambda n: (cur(n), 0)),
        pl.BlockSpec((BLK, LANE), lambda n: (prev(n), 0)),
        pl.BlockSpec((BLK, LANE), lambda n: (prev(n), 0)),
    ]


def _swa_fwd(proj, cos, sin, sinks):
    T = proj.shape[0]
    nb = T // BLK
    scale = A_HD ** -0.5

    def body(sinks_ref, q_ref, kc_ref, kp_ref, vc_ref, vp_ref, cq_ref, sq_ref, cp_ref, sp_ref, o_ref, l_ref):
        n = pl.program_id(0)
        cq, sq = cq_ref[...], sq_ref[...]
        K, V = _swa_keys(kc_ref, kp_ref, vc_ref, vp_ref, cq, sq, cp_ref[...], sp_ref[...])
        tri = _swa_tri()
        valid = tri | (n > 0)
        lane, rope_sub0, std_sub0 = _swa_masks()
        group = A_HEADS // A_KV
        roped, lses = {}, []

        def products(head):
            pb, sub, g = head // 2, head % 2, head // group
            if sub == 0:
                roped[pb] = _rope(q_ref[:, pb * LANE:(pb + 1) * LANE], cq, sq)
            qm = jnp.where(rope_sub0 if sub == 0 else ~rope_sub0, roped[pb], 0.0).astype(MXU)
            return _dot_nt(qm, K[:, g * LANE:(g + 1) * LANE])

        def softmax(head, s_full):
            s = jnp.where(valid, _swa_fold(s_full, tri) * scale, NEG)
            sink = sinks_ref[0, head]
            m = jnp.maximum(jnp.max(s, axis=1, keepdims=True), sink)
            e = jnp.exp(s - m)
            den = jnp.sum(e, axis=1, keepdims=True) + jnp.exp(sink - m)
            lses.append(m + jnp.log(den))
            return _swa_unfold(e / den, tri).astype(MXU)

        outs = {}
        st1 = {0: products(0), 1: products(1)}
        st2 = {0: softmax(0, st1.pop(0))}
        for head in range(A_HEADS):
            if head + 2 < A_HEADS:
                st1[head + 2] = products(head + 2)
            if head + 1 < A_HEADS:
                st2[head + 1] = softmax(head + 1, st1.pop(head + 1))
            g = head // group
            outs[head] = _dot(st2.pop(head), V[:, g * LANE:(g + 1) * LANE])
            if head % 2 == 1:
                pb = head // 2
                o_ref[:, pb * LANE:(pb + 1) * LANE] = jnp.where(std_sub0, outs[head - 1], outs[head])
        lacc = jnp.zeros((BLK, LANE), F32)
        for head in range(A_HEADS):
            lacc = jnp.where(lane == head, lses[head], lacc)
        l_ref[...] = lacc

    return pl.pallas_call(
        body, name="swa_fwd", grid=(nb,),
        in_specs=[pl.BlockSpec(memory_space=pltpu.SMEM)] + _swa_in_specs(nb, nb - 1),
        out_specs=[pl.BlockSpec((BLK, D), lambda n: (n, 0)), pl.BlockSpec((BLK, LANE), lambda n: (n, 0))],
        out_shape=[jax.ShapeDtypeStruct((T, D), F32), jax.ShapeDtypeStruct((T, LANE), F32)],
        compiler_params=_cp(("parallel",)),
    )(sinks, proj, proj, proj, proj, proj, cos, sin, cos, sin)


def _swa_bwd(proj, cos, sin, sinks, do_a, o_a, lse, after):
    T = proj.shape[0]
    nb = T // BLK
    scale = A_HD ** -0.5

    def body(sinks_ref, q_ref, kc_ref, kp_ref, vc_ref, vp_ref, cq_ref, sq_ref, cp_ref, sp_ref,
             do_ref, o_ref, l_ref, after_ref, dq_ref, dkv_ref, ds_ref, ckv_ref):
        n = pl.program_id(0)

        @pl.when(n == 0)
        def _():
            ckv_ref[...] = jnp.zeros_like(ckv_ref)
            ds_ref[...] = jnp.zeros_like(ds_ref)

        @pl.when(n < nb)
        def _():
            cq, sq, cp, sp = cq_ref[...], sq_ref[...], cp_ref[...], sp_ref[...]
            K, V = _swa_keys(kc_ref, kp_ref, vc_ref, vp_ref, cq, sq, cp, sp)
            tri = _swa_tri()
            valid = tri | (n > 0)
            lane, rope_sub0, std_sub0 = _swa_masks()
            lane_row = lax.broadcasted_iota(jnp.int32, (1, LANE), 1)
            lse_v = l_ref[...]
            dKt = [jnp.zeros((LANE, 2 * BLK), F32) for _ in range(A_KV)]
            dVt = [jnp.zeros((LANE, 2 * BLK), F32) for _ in range(A_KV)]
            dsinks, roped, roped_t, do_t = [], {}, {}, {}
            group = A_HEADS // A_KV
            dim = lax.broadcasted_iota(jnp.int32, (LANE, BLK), 0)
            rope_row0, std_row0 = ((dim // 32) % 2) == 0, dim < 64

            def products(head):
                pb, sub, g = head // 2, head % 2, head // group
                cols = slice(pb * LANE, (pb + 1) * LANE)
                Kg, Vg = K[:, g * LANE:(g + 1) * LANE], V[:, g * LANE:(g + 1) * LANE]
                if sub == 0:
                    roped[pb] = _rope(q_ref[:, cols], cq, sq)
                    roped_t[pb] = roped[pb].T
                    do_t[pb] = do_ref[:, cols].T
                qm = jnp.where(rope_sub0 if sub == 0 else ~rope_sub0, roped[pb], 0.0).astype(MXU)
                qmt = jnp.where(rope_row0 if sub == 0 else ~rope_row0, roped_t[pb], 0.0).astype(MXU)
                dov = jnp.where(std_sub0 if sub == 0 else ~std_sub0, do_ref[:, cols], 0.0)
                dovt = jnp.where(std_row0 if sub == 0 else ~std_row0, do_t[pb], 0.0).astype(MXU)
                delta = jnp.sum(dov * o_ref[:, cols], axis=1, keepdims=True)
                return qmt, dovt, delta, _dot_nt(qm, Kg), _dot_nt(dov.astype(MXU), Vg)

            def scores(head, qmt, dovt, delta, s_full, dp_full):
                lh = jnp.sum(jnp.where(lane == head, lse_v, 0.0), axis=1, keepdims=True)
                p = jnp.where(valid, jnp.exp(_swa_fold(s_full, tri) * scale - lh), 0.0)
                psink = jnp.exp(sinks_ref[0, head] - lh)
                dsinks.append(jnp.sum(-psink * delta, axis=0, keepdims=True))
                dsq = (p * (_swa_fold(dp_full, tri) - delta)) * scale
                return qmt, dovt, _swa_unfold(p, tri).astype(MXU), _swa_unfold(dsq, tri).astype(MXU)

            def grads(head, qmt, dovt, pb16, dsc):
                g = head // group
                dKt[g] = dKt[g] + _dot(qmt, dsc)
                dVt[g] = dVt[g] + _dot(dovt, pb16)
                return _dot(dsc, K[:, g * LANE:(g + 1) * LANE])

            dqs = {}
            st1 = {0: products(0), 1: products(1)}
            st2 = {0: scores(0, *st1.pop(0))}
            for head in range(A_HEADS):
                if head + 2 < A_HEADS:
                    st1[head + 2] = products(head + 2)
                if head + 1 < A_HEADS:
                    st2[head + 1] = scores(head + 1, *st1.pop(head + 1))
                dqs[head] = grads(head, *st2.pop(head))
                if head % 2 == 1:
                    pb = head // 2
                    dqp = jnp.where(rope_sub0, dqs[head - 1], dqs[head])
                    dq_ref[:, pb * LANE:(pb + 1) * LANE] = _rope_bwd(dqp, cq, sq).astype(dq_ref.dtype)
            dsink = jnp.zeros((1, LANE), F32)
            for head in range(A_HEADS):
                dsink = jnp.where(lane_row == head, dsinks[head], dsink)
            dK, dV = [a.T for a in dKt], [a.T for a in dVt]
            prev = ([_rope_bwd(dK[g][:BLK], cp, sp) for g in range(A_KV)] + [dV[g][:BLK] for g in range(A_KV)])
            cur_ = ([_rope_bwd(dK[g][BLK:], cq, sq) for g in range(A_KV)] + [dV[g][BLK:] for g in range(A_KV)])
            dkv_ref[...] = (ckv_ref[...] + jnp.concatenate(prev, axis=1)).astype(dkv_ref.dtype)
            ckv_ref[...] = jnp.concatenate(cur_, axis=1)
            ds_ref[...] = ds_ref[...] + jnp.broadcast_to(dsink, ds_ref.shape)

        @pl.when(n == nb)
        def _():
            dkv_ref[...] = ckv_ref[...].astype(dkv_ref.dtype)

    last = nb - 1

    def cur(n):
        return jnp.minimum(n, last)

    def out_kv(n):
        return (jnp.maximum(n - 1, 0), 0)

    return pl.pallas_call(
        body, name="swa_bwd", grid=(nb + 1,),
        in_specs=[pl.BlockSpec(memory_space=pltpu.SMEM)] + _swa_in_specs(nb, last) + [
            pl.BlockSpec((BLK, D), lambda n: (cur(n), 0)),
            pl.BlockSpec((BLK, D), lambda n: (cur(n), 0)),
            pl.BlockSpec((BLK, LANE), lambda n: (cur(n), 0)),
            _any(),
        ],
        out_specs=[
            pl.BlockSpec((BLK, D), lambda n: (cur(n), 0)),
            pl.BlockSpec((BLK, 512), out_kv),
            pl.BlockSpec((8, LANE), lambda n: (0, 0)),
        ],
        out_shape=[
            jax.ShapeDtypeStruct((T, D), MXU),
            jax.ShapeDtypeStruct((T, 512), MXU),
            jax.ShapeDtypeStruct((8, LANE), F32),
        ],
        scratch_shapes=[pltpu.VMEM((BLK, 512), F32)],
        compiler_params=_cp(("arbitrary",)),
    )(sinks, proj, proj, proj, proj, proj, cos, sin, cos, sin, do_a, o_a, lse, after)


NCH = 4
GSTEP = NCH * CHUNK
ST_ROWS = B_HEADS * B_DV


def _chunk_rows(c):
    return slice(c * CHUNK, (c + 1) * CHUNK)


def _per_chunk(which, vals):
    out = vals[-1]
    for c in range(NCH - 2, -1, -1):
        out = jnp.where(which == c, vals[c], out)
    return out


def _gla_gate(bl_ref, gu_ref, bias_ref):
    gk = _dot(bl_ref[...].astype(MXU), gu_ref[...]) + bias_ref[...]
    la = (jnp.minimum(gk, 0.0) - jnp.log(1.0 + jnp.exp(-jnp.abs(gk)))) / TAU
    ri = lax.broadcasted_iota(jnp.int32, (GSTEP, GSTEP), 0)
    ci = lax.broadcasted_iota(jnp.int32, (GSTEP, GSTEP), 1)
    same = (ri // CHUNK) == (ci // CHUNK)
    lower, upper = same & (ci <= ri), same & (ci >= ri)
    b = _dot_f32(jnp.where(lower, 1.0, 0.0).astype(F32), la)
    which = lax.broadcasted_iota(jnp.int32, (GSTEP, 1), 0) // CHUNK
    return gk, la, b, lower, upper, which


def _gla_head(q_ref, k_ref, la, b, which, h):
    sl = slice(h * B_DK, (h + 1) * B_DK)
    bh, lah = b[:, sl], la[:, sl]
    bls = [jnp.sum(lah[_chunk_rows(c)], axis=0, keepdims=True) for c in range(NCH)]
    blast = _per_chunk(which, bls)
    qc = q_ref[:, sl] * (B_DK ** -0.5)
    kh = k_ref[:, sl]
    eb, enb, esb = jnp.exp(bh), jnp.exp(-bh), jnp.exp(blast - bh)
    return qc * eb, kh * enb, kh * esb, eb, enb, esb, [jnp.exp(v) for v in bls]


def _gla_specs(step_of):
    return [
        pl.BlockSpec((GSTEP, 512), lambda i: (step_of(i), C_BQ // 512)),
        pl.BlockSpec((GSTEP, 512), lambda i: (step_of(i), C_BK // 512)),
        pl.BlockSpec((GSTEP, D), lambda i: (step_of(i), C_BV // D)),
        pl.BlockSpec((GSTEP, W_BL), lambda i: (step_of(i), C_BL // W_BL)),
        pl.BlockSpec((W_BL, 512), lambda i: (0, 0)),
        pl.BlockSpec((1, 512), lambda i: (0, 0)),
    ]


def _gla_fwd(proj, gu_pad, bias):
    T = proj.shape[0]
    ns = T // GSTEP

    def body(q_ref, k_ref, v_ref, bl_ref, gu_ref, bias_ref, o_ref, st_ref, state_ref):
        @pl.when(pl.program_id(0) == 0)
        def _():
            state_ref[...] = jnp.zeros_like(state_ref)

        _, la, b, lower, _, which = _gla_gate(bl_ref, gu_ref, bias_ref)

        def within(h):
            q_e, k_e, k_s, _, _, _, decays = _gla_head(q_ref, k_ref, la, b, which, h)
            vh = v_ref[:, h * B_DV:(h + 1) * B_DV].astype(MXU)
            q_eb = q_e.astype(MXU)
            att = jnp.where(lower, _dot_nt(q_eb, k_e.astype(MXU)), 0.0)
            return vh, q_eb, k_s.astype(MXU), _dot(att.astype(MXU), vh), decays

        def across(h, vh, q_eb, k_sb, o_intra, decays):
            rows = slice(h * B_DV, (h + 1) * B_DV)
            s = state_ref[rows, :]
            outs = []
            for c in range(NCH):
                cr = _chunk_rows(c)
                st_ref[c * ST_ROWS + h * B_DV:c * ST_ROWS + (h + 1) * B_DV, :] = s
                outs.append(o_intra[cr] + _dot_nt(q_eb[cr], s.astype(MXU)))
                s = s * decays[c] + _dot_tn(vh[cr], k_sb[cr])
            state_ref[rows, :] = s
            o_ref[:, rows] = jnp.concatenate(outs, axis=0)

        for h in range(B_HEADS):
            across(h, *within(h))

    return pl.pallas_call(
        body, name="gla_fwd", grid=(ns,),
        in_specs=_gla_specs(lambda i: i),
        out_specs=[pl.BlockSpec((GSTEP, D), lambda i: (i, 0)),
                   pl.BlockSpec((NCH * ST_ROWS, B_DK), lambda i: (i, 0))],
        out_shape=[jax.ShapeDtypeStruct((T, D), F32),
                   jax.ShapeDtypeStruct((ns * NCH * ST_ROWS, B_DK), F32)],
        scratch_shapes=[pltpu.VMEM((ST_ROWS, B_DK), F32)],
        compiler_params=_cp(("arbitrary",)),
    )(proj, proj, proj, proj, gu_pad, bias)


def _gla_bwd(proj, gu_pad, bias, states, do_b):
    T = proj.shape[0]
    ns = T // GSTEP
    o_q, o_k = C_BQ - C_GLA, C_BK - C_GLA

    def body(q_ref, k_ref, v_ref, bl_ref, gu_ref, bias_ref, st_ref, do_ref,
             dg_ref, dbl_ref, ggu_ref, gbias_ref, gt_ref):
        @pl.when(pl.program_id(0) == 0)
        def _():
            gt_ref[...] = jnp.zeros_like(gt_ref)
            ggu_ref[...] = jnp.zeros_like(ggu_ref)
            gbias_ref[...] = jnp.zeros_like(gbias_ref)

        gk, la, b, lower, upper_mask, which = _gla_gate(bl_ref, gu_ref, bias_ref)
        upper = jnp.where(upper_mask, 1.0, 0.0).astype(F32)
        dla_parts = []

        def within(h):
            q_e, k_e, k_s, eb, enb, esb, decays = _gla_head(q_ref, k_ref, la, b, which, h)
            vh = v_ref[:, h * B_DV:(h + 1) * B_DV].astype(MXU)
            doh = do_ref[:, h * B_DV:(h + 1) * B_DV].astype(MXU)
            q_eb, k_eb = q_e.astype(MXU), k_e.astype(MXU)
            att = jnp.where(lower, _dot_nt(q_eb, k_eb), 0.0).astype(MXU)
            datt = jnp.where(lower, _dot_nt(doh, vh), 0.0).astype(MXU)
            return (q_e, k_e, k_s, eb, enb, esb, decays, vh, doh, q_eb, k_s.astype(MXU),
                    _dot(datt, k_eb), _dot_tn(datt, q_eb), _dot_tn(att, doh))

        def across(h, q_e, k_e, k_s, eb, enb, esb, decays, vh, doh, q_eb, k_sb, dq_i, dk_e, dv_i):
            rows = slice(h * B_DV, (h + 1) * B_DV)
            g = gt_ref[rows, :]
            dq_c, dks_c, dv_c, ddec = [None] * NCH, [None] * NCH, [None] * NCH, [None] * NCH
            for c in range(NCH - 1, -1, -1):
                cr = _chunk_rows(c)
                s = st_ref[c * ST_ROWS + h * B_DV:c * ST_ROWS + (h + 1) * B_DV, :]
                gb = g.astype(MXU)
                dq_c[c] = dq_i[cr] + _dot(doh[cr], s.astype(MXU))
                dks_c[c] = _dot(vh[cr], gb)
                dv_c[c] = dv_i[cr] + _dot_nt(k_sb[cr], gb)
                ddec[c] = jnp.sum(g * s, axis=0, keepdims=True)
                g = g * decays[c] + _dot_tn(doh[cr], q_eb[cr])
            gt_ref[rows, :] = g
            dq_e = jnp.concatenate(dq_c, axis=0)
            dk_s = jnp.concatenate(dks_c, axis=0)
            dg_ref[:, rows] = jnp.concatenate(dv_c, axis=0).astype(dg_ref.dtype)
            dg_ref[:, o_q + h * B_DK:o_q + (h + 1) * B_DK] = (dq_e * eb * (B_DK ** -0.5)).astype(dg_ref.dtype)
            dg_ref[:, o_k + h * B_DK:o_k + (h + 1) * B_DK] = (dk_e * enb + dk_s * esb).astype(dg_ref.dtype)
            dks_ks = dk_s * k_s
            db = dq_e * q_e - dk_e * k_e - dks_ks
            dbl = [jnp.sum(dks_ks[_chunk_rows(c)], axis=0, keepdims=True) + ddec[c] * decays[c] for c in range(NCH)]
            dla_parts.append(_dot_f32(upper, db) + _per_chunk(which, dbl))

        for h in range(B_HEADS):
            across(h, *within(h))
        dla = jnp.concatenate(dla_parts, axis=1)
        dgk = dla * (1.0 / TAU) * _sigmoid(-gk)
        dgkb = dgk.astype(MXU)
        dbl_ref[...] = _dot_nt(dgkb, gu_ref[...]).astype(dbl_ref.dtype)
        ggu_ref[...] = ggu_ref[...] + _dot_tn(bl_ref[...].astype(MXU), dgkb)
        gbias_ref[...] = gbias_ref[...] + jnp.broadcast_to(jnp.sum(dgk, axis=0, keepdims=True), gbias_ref.shape)

    def rev(i):
        return ns - 1 - i

    return pl.pallas_call(
        body, name="gla_bwd", grid=(ns,),
        in_specs=_gla_specs(rev) + [
            pl.BlockSpec((NCH * ST_ROWS, B_DK), lambda i: (rev(i), 0)),
            pl.BlockSpec((GSTEP, D), lambda i: (rev(i), 0)),
        ],
        out_specs=[
            pl.BlockSpec((GSTEP, W_GLA), lambda i: (rev(i), 0)),
            pl.BlockSpec((GSTEP, W_BL), lambda i: (rev(i), 0)),
            pl.BlockSpec((W_BL, 512), lambda i: (0, 0)),
            pl.BlockSpec((8, 512), lambda i: (0, 0)),
        ],
        out_shape=[
            jax.ShapeDtypeStruct((T, W_GLA), MXU),
            jax.ShapeDtypeStruct((T, W_BL), MXU),
            jax.ShapeDtypeStruct((W_BL, 512), F32),
            jax.ShapeDtypeStruct((8, 512), F32),
        ],
        scratch_shapes=[pltpu.VMEM((B_HEADS * B_DV, B_DK), F32)],
        compiler_params=_cp(("arbitrary",)),
    )(proj, proj, proj, proj, gu_pad, bias, states, do_b)


def _mid(x, target, proj, o_a, o_b, late, w_bn4, fnw):
    T = x.shape[0]
    tT = min(T, 128)
    nbuf = 4
    o_ag, o_bg, o_ma, o_mb = (c - C_GATES for c in (C_AG, C_BG, C_MA, C_MB))

    def body(x_ref, t_ref, oa_ref, ob_ref, gates_ref, late_ref, wbn_ref, fnw_ref,
             dx2_ref, doa_ref, dob_ref, dgates_ref,
             tail0_ref, tail1_ref, gfn_ref, gbn_ref, loss_ref, buf_ref, gw_ref):
        i = pl.program_id(0)

        def weight(p):
            return late_ref[:, 128 * p:128 * (p + 1), :].reshape(D, D)

        @pl.when(i == 0)
        def _():
            for r in (gw_ref, gfn_ref, gbn_ref, loss_ref):
                r[...] = jnp.zeros_like(r)

        rows = pl.ds(pl.multiple_of((i % nbuf) * tT, tT), tT)

        def keep(k, val):
            buf_ref[k, rows, :] = val

        oa, ag = oa_ref[...], gates_ref[:, o_ag:o_ag + D]
        sg_a = _sigmoid(ag)
        silu_a = ag * sg_a
        oag_b = (oa * silu_a).astype(MXU)
        keep(0, oag_b)
        y_a = _dot(oag_b, weight(0))

        ob, bg = ob_ref[...], gates_ref[:, o_bg:o_bg + D]
        rbs, obhats = [], []
        for h in range(B_HEADS):
            obh = ob[:, h * B_DV:(h + 1) * B_DV]
            rb = lax.rsqrt(jnp.mean(obh * obh, axis=-1, keepdims=True) + EPS)
            rbs.append(rb)
            obhats.append(obh * rb)
        obhat = jnp.concatenate(obhats, axis=1)
        wbn = wbn_ref[...]
        obn = obhat * wbn
        sg_b = _sigmoid(bg)
        silu_b = bg * sg_b
        obg_b = (obn * silu_b).astype(MXU)
        keep(1, obg_b)
        y_b = _dot(obg_b, weight(1))

        sa, sb = _sigmoid(gates_ref[:, o_ma:o_ma + D]), _sigmoid(gates_ref[:, o_mb:o_mb + D])
        mg_b = (sa * y_a + sb * y_b).astype(MXU)
        keep(2, mg_b)
        x2 = x_ref[...] + _dot(mg_b, weight(2))
        r2 = lax.rsqrt(jnp.mean(x2 * x2, axis=-1, keepdims=True) + EPS)
        xh2 = x2 * r2
        fw = fnw_ref[...]
        err = xh2 * fw - t_ref[...]
        tok = jnp.mean(err * err, axis=-1, keepdims=True)
        loss_ref[...] = loss_ref[...] + 0.5 * jnp.sum(tok, axis=0, keepdims=True)

        dy = err * (1.0 / D)
        gfn_ref[...] = gfn_ref[...] + jnp.broadcast_to(jnp.sum(dy * xh2, axis=0, keepdims=True), gfn_ref.shape)
        gy = dy * fw
        dx2 = r2 * (gy - xh2 * jnp.mean(gy * xh2, axis=-1, keepdims=True))
        dx2_ref[...] = dx2
        dx2_b = dx2.astype(MXU)
        keep(5, dx2_b)
        dmg = _dot_nt(dx2_b, weight(2))

        dgates_ref[:, o_ma:o_ma + D] = (dmg * y_a * sa * (1.0 - sa)).astype(dgates_ref.dtype)
        dgates_ref[:, o_mb:o_mb + D] = (dmg * y_b * sb * (1.0 - sb)).astype(dgates_ref.dtype)
        dya_b = (dmg * sa).astype(MXU)
        dyb_b = (dmg * sb).astype(MXU)
        keep(3, dya_b)
        keep(4, dyb_b)
        doag = _dot_nt(dya_b, weight(0))
        dobg = _dot_nt(dyb_b, weight(1))

        @pl.when(i % nbuf == nbuf - 1)
        def _():
            for p in range(3):
                gw_ref[p] = gw_ref[p] + _dot_tn(buf_ref[p], buf_ref[3 + p])

        @pl.when(i == pl.num_programs(0) - 1)
        def _():
            for hf, tail_ref in enumerate((tail0_ref, tail1_ref)):
                for d in range(NDEV):
                    for p in range(3):
                        tail_ref[d, 128 * p:128 * (p + 1), :] = (
                            gw_ref[p, 128 * d:128 * (d + 1), hf * DH:(hf + 1) * DH].astype(tail_ref.dtype))

        doa_ref[...] = doag * silu_a
        dgates_ref[:, o_ag:o_ag + D] = (doag * oa * (sg_a * (1.0 + ag * (1.0 - sg_a)))).astype(dgates_ref.dtype)
        dobn = dobg * silu_b
        dgates_ref[:, o_bg:o_bg + D] = (dobg * obn * (sg_b * (1.0 + bg * (1.0 - sg_b)))).astype(dgates_ref.dtype)
        gg = dobn * wbn
        gbn = jnp.zeros((1, B_DV), F32)
        for h in range(B_HEADS):
            sl = slice(h * B_DV, (h + 1) * B_DV)
            gbn = gbn + jnp.sum(dobn[:, sl] * obhats[h], axis=0, keepdims=True)
            ggh = gg[:, sl]
            dob_ref[:, sl] = rbs[h] * (ggh - obhats[h] * jnp.mean(ggh * obhats[h], axis=-1, keepdims=True))
        gbn_ref[...] = gbn_ref[...] + jnp.broadcast_to(gbn, gbn_ref.shape)

    assert (T // tT) % nbuf == 0
    tile = pl.BlockSpec((tT, D), lambda i: (i, 0))
    row = pl.BlockSpec((1, D), lambda i: (0, 0))
    acc8 = pl.BlockSpec((8, D), lambda i: (0, 0))
    return pl.pallas_call(
        body, name="mid", grid=(T // tT,),
        in_specs=[tile, tile, tile, tile, pl.BlockSpec((tT, W_GATES), lambda i: (i, C_GATES // W_GATES)),
                  _vmem(), row, row],
        out_specs=[tile, tile, tile, pl.BlockSpec((tT, W_GATES), lambda i: (i, 0)), _vmem(), _vmem(),
                   acc8, pl.BlockSpec((8, B_DV), lambda i: (0, 0)), pl.BlockSpec((8, LANE), lambda i: (0, 0))],
        out_shape=[
            jax.ShapeDtypeStruct((T, D), F32),
            jax.ShapeDtypeStruct((T, D), F32),
            jax.ShapeDtypeStruct((T, D), F32),
            jax.ShapeDtypeStruct((T, W_GATES), MXU),
            jax.ShapeDtypeStruct((NDEV, 384, DH), WIRE),
            jax.ShapeDtypeStruct((NDEV, 384, DH), WIRE),
            jax.ShapeDtypeStruct((8, D), F32),
            jax.ShapeDtypeStruct((8, B_DV), F32),
            jax.ShapeDtypeStruct((8, LANE), F32),
        ],
        scratch_shapes=[pltpu.VMEM((6, nbuf * tT, D), MXU), pltpu.VMEM((3, D, D), F32)],
        compiler_params=_cp(("arbitrary",)),
    )(x, target, o_a, o_b, proj, late, w_bn4, fnw)


DH = D // 2


_GW_TILES = (("q", 0, 512, 0), ("q", 1, 512, 512), ("kv", 0, 256, 1024), ("bl", 0, RANK, 5376),
             ("gla", 0, 512, 3328), ("gla", 1, 512, 3840), ("gla", 2, 512, 2304), ("gla", 3, 512, 2816),
             ("gates", 0, 512, 1280), ("gates", 1, 512, 1792), ("gates", 2, 512, 4352), ("gates", 3, 512, 4864),
             ("gates", 4, 512, 5392), ("gates", 5, 512, 5904), ("gates", 6, 512, 6416), ("gates", 7, 512, 6928))


def _gw_unpermute(piece, t):
    if piece == "q":
        parts = []
        for blk in range(t.shape[0] // LANE):
            g = [t[blk * LANE + 32 * i:blk * LANE + 32 * (i + 1)] for i in range(4)]
            parts += [g[0], g[2], g[1], g[3]]
        return jnp.concatenate(parts, axis=0)
    if piece == "kv":
        k = [t[64 * i:64 * i + 32] + t[64 * i + 32:64 * i + 64] for i in range(4)]
        v = [t[256 + 128 * g:256 + 128 * g + 64] + t[256 + 128 * g + 64:256 + 128 * (g + 1)] for g in range(2)]
        return jnp.concatenate(k + v, axis=0)
    if piece == "bl":
        return t[:RANK]
    return t


def _gw_half(h, pieces, half, after=None):
    T = h.shape[0]
    steps = len(_GW_TILES)

    def body(*refs):
        h_ref = refs[0]
        srcs = dict(zip(("q", "kv", "bl", "gla", "gates"), refs[1:6]))
        o_ref, stage, sems = refs[-3:]
        j = pl.program_id(0)

        def out_copy(k):
            _, _, n, off = _GW_TILES[k]
            return pltpu.make_async_copy(stage.at[k % 2, 0:n], o_ref.at[pl.ds(off, n)], sems.at[k % 2])

        for k, (piece, _, n, _) in enumerate(_GW_TILES):
            @pl.when(j == k)
            def _(k=k, piece=piece, n=n):
                if k >= 2:
                    out_copy(k - 2).wait()
                t = _gw_unpermute(piece, _dot_tn(srcs[piece][...], h_ref[...]))
                stage[k % 2, 0:n, :] = t.astype(stage.dtype)
                out_copy(k).start()

        @pl.when(j == steps - 1)
        def _():
            out_copy(steps - 2).wait()
            out_copy(steps - 1).wait()

    def tile_of(lo, hi):
        return lambda j: (0, jnp.clip(j - lo, 0, hi - lo - 1))

    in_specs = [pl.BlockSpec((T, DH), lambda j: (0, half)),
                pl.BlockSpec((T, 512), tile_of(0, 2)), pl.BlockSpec((T, 512), lambda j: (0, 0)),
                pl.BlockSpec((T, W_BL), lambda j: (0, 0)),
                pl.BlockSpec((T, 512), tile_of(4, 8)), pl.BlockSpec((T, 512), tile_of(8, 16))]
    args = [h, *pieces]
    if after is not None:
        in_specs.append(_any())
        args.append(after)
    return pl.pallas_call(
        body, name=f"gw_in_half{half}", grid=(steps,),
        in_specs=in_specs, out_specs=_any(),
        out_shape=jax.ShapeDtypeStruct((IN_WIDTH, DH), WIRE),
        scratch_shapes=[pltpu.VMEM((2, 512, DH), WIRE), pltpu.SemaphoreType.DMA((2,))],
        compiler_params=_cp(("arbitrary",)),
    )(*args)


def _chip_copies(s_ref, got_ref, send_sems, recv_sems):
    x, y, c = _place()
    chips = [(1 - x, y), (x, 1 - y), (1 - x, 1 - y)]
    return [pltpu.make_async_remote_copy(
        src_ref=s_ref.at[2 * px + py], dst_ref=got_ref.at[j],
        send_sem=send_sems.at[j], recv_sem=recv_sems.at[j], device_id=(px, py, c), device_id_type=MESH)
        for j, (px, py) in enumerate(chips)]


_EFFECT = pltpu.SideEffectType.DATAFLOW_SIDE_EFFECTING


def _hbm():
    return pl.BlockSpec(memory_space=pltpu.HBM)


def _sem():
    return pl.BlockSpec(memory_space=pltpu.SEMAPHORE)


def _chip_start(sums, half):
    land = pltpu.with_memory_space_constraint(lax.empty((3,) + sums.shape[1:], sums.dtype), pltpu.HBM)

    def body(s_ref, land_ref, send_sems, recv_sems, s_thru, land_thru, token):
        for cp in _chip_copies(s_ref, land_ref, send_sems, recv_sems):
            cp.start()
        token[...] = jnp.zeros_like(token)

    return pl.pallas_call(
        body, name=f"chip_start{half}",
        out_shape=(pltpu.SemaphoreType.DMA((3,)), pltpu.SemaphoreType.DMA((3,)),
                   pltpu.HBM(sums.shape, sums.dtype), pltpu.HBM(land.shape, land.dtype),
                   jax.ShapeDtypeStruct((8, LANE), F32)),
        in_specs=(_hbm(), _hbm()), out_specs=(_sem(), _sem(), _hbm(), _hbm(), _vmem()),
        input_output_aliases={0: 2, 1: 3},
        compiler_params=pltpu.CompilerParams(has_side_effects=_EFFECT),
    )(pltpu.with_memory_space_constraint(sums, pltpu.HBM), land)


def _chip_wait(send_sems, recv_sems, s_thru, land_thru, after, half):
    def body(s_ref, land_ref, send_sems, recv_sems, after_ref, s_out, got_ref):
        copies = _chip_copies(s_ref, land_ref, send_sems, recv_sems)
        for cp in copies:
            cp.wait_send()
        for cp in copies:
            cp.wait_recv()

    return pl.pallas_call(
        body, name=f"chip_wait{half}",
        out_shape=(pltpu.HBM(s_thru.shape, s_thru.dtype), pltpu.HBM(land_thru.shape, land_thru.dtype)),
        in_specs=(_hbm(), _hbm(), _sem(), _sem(), _any()), out_specs=(_hbm(), _hbm()),
        input_output_aliases={0: 0, 1: 1},
        compiler_params=pltpu.CompilerParams(has_side_effects=_EFFECT),
    )(s_thru, land_thru, send_sems, recv_sems, after)


def _dh_norm(pieces, offsets, wf, x, dx2, norm_w, after):
    T = x.shape[0]
    tT = min(T, 256)
    widths = [p.shape[1] for p in pieces]
    npc = len(pieces)

    def body(*refs):
        dp_refs = refs[:npc]
        wf_ref, x_ref, dx2_ref, nw_ref, _, gx_ref, gnw_ref = refs[npc:]

        @pl.when(pl.program_id(0) == 0)
        def _():
            gnw_ref[...] = jnp.zeros_like(gnw_ref)

        dh = jnp.zeros((tT, D), F32)
        for dp_ref, off, w in zip(dp_refs, offsets, widths):
            dh = dh + _dot(dp_ref[...], wf_ref[off:off + w, :])
        xv = x_ref[...]
        r = lax.rsqrt(jnp.mean(xv * xv, axis=-1, keepdims=True) + EPS)
        xh = xv * r
        gnw_ref[...] = gnw_ref[...] + jnp.broadcast_to(jnp.sum(dh * xh, axis=0, keepdims=True), gnw_ref.shape)
        g = dh * nw_ref[...]
        gx_ref[...] = r * (g - xh * jnp.mean(g * xh, axis=-1, keepdims=True)) + dx2_ref[...]

    tile = pl.BlockSpec((tT, D), lambda i: (i, 0))
    return pl.pallas_call(
        body, name="dh_norm", grid=(T // tT,),
        in_specs=[pl.BlockSpec((tT, w), lambda i: (i, 0)) for w in widths]
        + [_vmem(), tile, tile, pl.BlockSpec((1, D), lambda i: (0, 0)), _any()],
        out_specs=[tile, pl.BlockSpec((8, D), lambda i: (0, 0))],
        out_shape=[jax.ShapeDtypeStruct((T, D), F32), jax.ShapeDtypeStruct((8, D), F32)],
        compiler_params=_cp(("arbitrary",)),
    )(*pieces, wf, x, dx2, norm_w, after)


def _adamw_math(w, g, m, v):
    m = ADAM_B1 * m + (1.0 - ADAM_B1) * g
    v = ADAM_B2 * v + (1.0 - ADAM_B2) * (g * g)
    m_hat = m * (1.0 / (1.0 - ADAM_B1 ** ADAM_STEP))
    v_hat = v * (1.0 / (1.0 - ADAM_B2 ** ADAM_STEP))
    delta = -ADAM_LR * (m_hat / (jnp.sqrt(v_hat) + ADAM_EPS) + ADAM_WD * w)
    return delta, m, v


def _fetch_partials(s_ref, got_ref, buf, sems):
    x, y, _ = _place()
    cps = [pltpu.make_async_copy(s_ref.at[2 * x + y], buf.at[0], sems.at[0])]
    cps += [pltpu.make_async_copy(got_ref.at[j], buf.at[1 + j], sems.at[1 + j]) for j in range(3)]
    for cp in cps:
        cp.start()
    return cps


SMALL_AT = dict(norm_w=0, fnw=8, bias=16, bn=24, sinks=32, loss=40)
ROW_AT = (R_IN, R_A, R_B, R_O)


def _finish_small(ws, ms, vs, smalls):
    names = ["norm_w", "fnw", "bias", "bn", "sinks"]
    widths = [ws[n].shape[1] for n in names]

    def body(*refs):
        w_refs, m_refs, v_refs = refs[0:5], refs[5:10], refs[10:15]
        smalls_ref, loss_ref = refs[15], refs[16]
        outs, tot = refs[17:37], refs[37]
        acc = smalls_ref[0]
        for d in range(1, NDEV):
            acc = acc + smalls_ref[d]
        tot[...] = acc
        loss_ref[...] = tot[SMALL_AT["loss"]:SMALL_AT["loss"] + 1, 0:1]
        for p, (nm_, wd) in enumerate(zip(names, widths)):
            r = SMALL_AT[nm_]
            g = tot[r:r + 1, 0:wd]
            d, nm, nv = _adamw_math(w_refs[p][...], g, m_refs[p][...], v_refs[p][...])
            for o, val in zip(outs[4 * p:4 * p + 4], (g, d, nm, nv)):
                o[...] = val

    res = pl.pallas_call(
        body, name="finish_small",
        in_specs=[_vmem()] * 16, out_specs=[_vmem()] * 21,
        out_shape=[jax.ShapeDtypeStruct((1, 1), F32)]
        + [jax.ShapeDtypeStruct((1, wd), F32) for wd in widths for _ in range(4)],
        scratch_shapes=[pltpu.VMEM((SMALL_ROWS, D), F32)],
        compiler_params=_cp(),
    )(*[ws[n] for n in names], *[ms[n] for n in names], *[vs[n] for n in names], smalls)
    return res[0], {n: tuple(res[1 + 4 * p:5 + 4 * p]) for p, n in enumerate(names)}


def _finish(w_rows, m_rows, v_rows, gu_w, gu_m, gu_v, sums, got):
    shapes = [(SHARD, 1, D)] + [w.shape for w in w_rows[1:]]

    row_block = 96

    def columns(ref, p, cols, r0, n):
        if p:
            return ref, (slice(r0, r0 + n), cols)
        flat = ref if ref.shape == (SHARD * LANE_TILES, LANE) else ref.reshape(SHARD * LANE_TILES, LANE)
        return flat, (pl.ds(cols.start // LANE + LANE_TILES * r0, n, stride=LANE_TILES), slice(None))

    def read(ref, p, cols, r0, n):
        ref, at = columns(ref, p, cols, r0, n)
        return ref[at]

    def body(*refs):
        wr_refs, mr_refs, vr_refs = refs[0:4], refs[4:8], refs[8:12]
        guw_ref, gum_ref, guv_ref = refs[12:15]
        s_refs, got_refs = refs[15:17], refs[17:19]
        row_outs = refs[19:35]
        gu_outs = refs[35:39]
        bufs, gsh, sems, big, big_sems = refs[39:]
        loads = [pltpu.make_async_copy(r[0], big.at[k], big_sems.at[k]) for k, r in enumerate((wr_refs, mr_refs, vr_refs))]
        for cp in loads:
            cp.start()
        wr_refs, mr_refs, vr_refs = ((big.at[k],) + tuple(r[1:]) for k, r in enumerate((wr_refs, mr_refs, vr_refs)))
        x, y, c = _place()
        me_slot = 4 * x + 2 * y + c
        down = 2 * me_slot

        fetches = [_fetch_partials(s_refs[0], got_refs[0], bufs.at[0], sems.at[0])]

        def total(rows, cols):
            g = buf[0, rows, cols].astype(F32)
            for j in range(1, 4):
                g = g + buf[j, rows, cols].astype(F32)
            return g

        def update(p, grad, cols):
            nrows = shapes[p][0]
            for r0 in range(0, nrows, row_block):
                n = min(row_block, nrows - r0)
                g = grad(r0, n)
                d, nm, nv = _adamw_math(read(wr_refs[p], p, cols, r0, n), g, read(mr_refs[p], p, cols, r0, n),
                                        read(vr_refs[p], p, cols, r0, n))
                for o, val in zip(row_outs[4 * p:4 * p + 4], (g, d, nm, nv)):
                    o, at = columns(o, p, cols, r0, n)
                    o[at] = val

        for hf in range(2):
            for cp in fetches[hf]:
                cp.wait()
            buf = bufs.at[hf]
            for cc in range(DH // LANE):
                src = slice(cc * LANE, (cc + 1) * LANE)
                cols = slice(hf * DH + cc * LANE, hf * DH + (cc + 1) * LANE)
                for r0 in range(0, SHARD_PAD, row_block):
                    rows = slice(r0, min(r0 + row_block, SHARD_PAD))
                    gsh[rows, :] = total(rows, src)
                if hf == 0 and cc == 0:
                    for cp in loads:
                        cp.wait()
                    fetches.append(_fetch_partials(s_refs[1], got_refs[1], bufs.at[1], sems.at[1]))
                update(0, lambda r0, n: gsh[pl.ds(down + r0, n), :], cols)
                for p in range(1, 4):
                    update(p, lambda r0, n, p=p: total(slice(ROW_AT[p] + r0, ROW_AT[p] + r0 + n), src), cols)
            if hf == 0:
                g = total(slice(R_GU, R_GU + RANK), slice(0, 64))
                d, nm, nv = _adamw_math(guw_ref[...], g, gum_ref[...], guv_ref[...])
                for o, val in zip(gu_outs, (g, d, nm, nv)):
                    o[...] = val

    res = pl.pallas_call(
        body, name="finish",
        in_specs=([_any()] + [_vmem()] * 3) * 3 + [_vmem()] * 3 + [_any()] * 4,
        out_specs=[_vmem()] * 20,
        out_shape=[jax.ShapeDtypeStruct(s, F32) for s in shapes for _ in range(4)]
        + [jax.ShapeDtypeStruct((RANK, 64), F32)] * 4,
        scratch_shapes=[pltpu.VMEM((2, 4, ROWS, DH), sums[0].dtype), pltpu.VMEM((SHARD_PAD, LANE), F32),
                        pltpu.SemaphoreType.DMA((2, 4)),
                        pltpu.VMEM((3, SHARD * LANE_TILES, LANE), F32), pltpu.SemaphoreType.DMA((3,))],
        compiler_params=_cp(),
    )(*w_rows, *m_rows, *v_rows, gu_w, gu_m, gu_v, *sums, *got)
    return tuple(res[0:16]), tuple(res[16:20])


def _place():
    x, y, c = lax.axis_index("x"), lax.axis_index("y"), lax.axis_index("c")
    return x, y, c


def _peers(x, y, c):
    return [(x ^ dx, y ^ dy, c ^ dc) for dx in range(2) for dy in range(2) for dc in range(2) if dx + dy + dc]


def _late_gather_start(blk, after, name="late_gather"):
    land = pltpu.with_memory_space_constraint(lax.empty((NDEV,) + blk.shape, blk.dtype), pltpu.HBM)

    def body(b_ref, land_ref, after_ref, send_sems, recv_sems, b_thru, land_thru, token):
        x, y, c = _place()
        for k, to in enumerate(_peers(x, y, c)):
            pltpu.make_async_remote_copy(
                src_ref=b_ref, dst_ref=land_ref.at[4 * x + 2 * y + c], send_sem=send_sems.at[k],
                recv_sem=recv_sems.at[k], device_id=to, device_id_type=MESH).start()
        token[...] = jnp.zeros_like(token)

    return pl.pallas_call(
        body, name=name + "_start",
        out_shape=(pltpu.SemaphoreType.DMA((7,)), pltpu.SemaphoreType.DMA((7,)),
                   pltpu.HBM(blk.shape, blk.dtype), pltpu.HBM(land.shape, land.dtype),
                   jax.ShapeDtypeStruct((8, LANE), F32)),
        in_specs=(_hbm(), _hbm(), _any()), out_specs=(_sem(), _sem(), _hbm(), _hbm(), _vmem()),
        input_output_aliases={0: 2, 1: 3},
        compiler_params=pltpu.CompilerParams(has_side_effects=_EFFECT),
    )(pltpu.with_memory_space_constraint(blk, pltpu.HBM), land, after)


def _late_gather_wait(send_sems, recv_sems, b_thru, land_thru, after, after2, name="late_gather"):
    def body(b_ref, land_ref, send_sems, recv_sems, after_ref, after2_ref, b_out, got_ref):
        x, y, c = _place()
        copies = [pltpu.make_async_remote_copy(
            src_ref=b_ref, dst_ref=land_ref.at[4 * x + 2 * y + c], send_sem=send_sems.at[k],
            recv_sem=recv_sems.at[k], device_id=to, device_id_type=MESH)
            for k, to in enumerate(_peers(x, y, c))]
        for cp in copies:
            cp.wait_send()
        for cp in copies:
            cp.wait_recv()

    return pl.pallas_call(
        body, name=name + "_wait",
        out_shape=(pltpu.HBM(b_thru.shape, b_thru.dtype), pltpu.HBM(land_thru.shape, land_thru.dtype)),
        in_specs=(_hbm(), _hbm(), _sem(), _sem(), _any(), _any()), out_specs=(_hbm(), _hbm()),
        input_output_aliases={0: 0, 1: 1},
        compiler_params=pltpu.CompilerParams(has_side_effects=_EFFECT),
    )(b_thru, land_thru, send_sems, recv_sems, after, after2)


G_ROWS = SHARD_PAD + RANK


def _gather_blocks(w_in_t, gu_s, xs, norm_w, pos_col):
    rows, cols = G_ROWS, D
    T = xs.shape[0]
    tT = min(T, 256)
    inv_row, sign_row = _rope_rows()

    def body(wi_ref, gu_ref, xs_hbm, nw_ref, pos_ref, inv_ref, sign_ref,
             out_ref, h_ref, cos_ref, sin_ref, x_ref, frame_ref, xs_ref, send_sems, recv_sems, local_sem, xs_sem):
        load_xs = pltpu.make_async_copy(xs_hbm, xs_ref, xs_sem)
        load_xs.start()
        x, y, c = _place()
        me, sibling = (x, y, c), (x, y, 1 - c)
        chips = [(1 - x, y), (x, 1 - y), (1 - x, 1 - y)]
        shift = 2 * (4 * x + 2 * y + c)
        frame_ref[SHARD - SHARD % 8:, :] = jnp.zeros((SHARD_PAD - SHARD + SHARD % 8, LANE), F32)
        for cc in range(LANE_TILES):
            cs = slice(cc * LANE, (cc + 1) * LANE)
            frame_ref[:SHARD, :] = wi_ref[pl.ds(cc, SHARD, stride=LANE_TILES), :]
            x_ref[0:SHARD_PAD, cs] = pltpu.roll(frame_ref[...], shift, 0).astype(x_ref.dtype)
        x_ref[SHARD_PAD:G_ROWS, :] = jnp.zeros((RANK, D), x_ref.dtype)
        x_ref[SHARD_PAD:G_ROWS, 0:64] = gu_ref[...].astype(x_ref.dtype)

        def slot(px, py, pc):
            return out_ref.at[4 * px + 2 * py + pc]

        def copy(k, block, to, src=None):
            return pltpu.make_async_remote_copy(
                src_ref=slot(*block) if src is None else src, dst_ref=slot(*block),
                send_sem=send_sems.at[k], recv_sem=recv_sems.at[k], device_id=to, device_id_type=MESH)

        mine = pltpu.make_async_copy(x_ref, slot(*me), local_sem)
        mine.start()
        first = [copy(0, me, sibling, src=x_ref)]
        first += [copy(1 + j, me, (*chip, c), src=x_ref) for j, chip in enumerate(chips)]
        for cp in first:
            cp.start()
        load_xs.wait()

        @pl.loop(0, T // tT)
        def _(i):
            rows_i = pl.ds(pl.multiple_of(i * tT, tT), tT)
            _prologue_rows(rows_i, xs_ref, nw_ref, pos_ref, inv_ref, sign_ref, h_ref, cos_ref, sin_ref)

        passed = [copy(4 + j, (*chip, c), sibling) for j, chip in enumerate(chips)]
        for j, chip in enumerate(chips):
            copy(1 + j, (*chip, c), me).wait_recv()
            passed[j].start()
        copy(0, sibling, me).wait_recv()
        for j, chip in enumerate(chips):
            copy(4 + j, (*chip, 1 - c), me).wait_recv()
        for cp in first + passed:
            cp.wait_send()
        mine.wait()

    return pl.pallas_call(
        body, name="gather_weights",
        in_specs=[_vmem(), _vmem(), _any()] + [_vmem()] * 4, out_specs=[_any()] + [_vmem()] * 3,
        out_shape=[jax.ShapeDtypeStruct((NDEV, rows, cols), WIRE), jax.ShapeDtypeStruct((T, D), MXU),
                   jax.ShapeDtypeStruct((T, LANE), F32), jax.ShapeDtypeStruct((T, LANE), F32)],
        scratch_shapes=[pltpu.VMEM((rows, cols), WIRE), pltpu.VMEM((SHARD_PAD, LANE), F32), pltpu.VMEM((T, D), F32),
                        pltpu.SemaphoreType.DMA((7,)), pltpu.SemaphoreType.DMA((7,)), pltpu.SemaphoreType.DMA,
                        pltpu.SemaphoreType.DMA],
        compiler_params=_cp(),
    )(w_in_t, gu_s, xs, norm_w, pos_col, inv_row, sign_row)


def _pair_reduce(gwt, tails, half):
    n = gwt.shape[1]
    starts = [SHARD_PAD]
    for t in tails:
        starts.append(starts[-1] + t.shape[1])
    rows = starts[-1]
    blk = (4, rows, n)
    npart = 1 + len(tails)

    def body(*refs):
        g_ref, t_refs = refs[0], refs[1:npart]
        out_ref, acc, got, own, send_sems, recv_sems, own_sems, out_sems = refs[npart:]
        x, y, c = _place()

        def parts(d, dst):
            frame = g_ref.at[pl.ds(pl.multiple_of(FRAME * d, 16), SHARD_PAD)]
            return [(frame, dst.at[0:SHARD_PAD])] + [
                (t_ref.at[d], dst.at[starts[k]:starts[k + 1]]) for k, t_ref in enumerate(t_refs)]

        sends, loads, stores = [], [], []
        for chip in range(4):
            sends.append([pltpu.make_async_remote_copy(
                src_ref=s, dst_ref=d_, send_sem=send_sems.at[chip, k], recv_sem=recv_sems.at[chip, k],
                device_id=(x, y, 1 - c), device_id_type=MESH)
                for k, (s, d_) in enumerate(parts(2 * chip + (1 - c), got.at[chip]))])
            loads.append([pltpu.make_async_copy(s, d_, own_sems.at[chip, k])
                          for k, (s, d_) in enumerate(parts(2 * chip + c, own.at[chip]))])
            stores.append(pltpu.make_async_copy(acc.at[chip], out_ref.at[chip], out_sems.at[chip]))
        for group in sends + loads:
            for cp in group:
                cp.start()
        for chip in range(4):
            for cp in loads[chip]:
                cp.wait()
            for cp in sends[chip]:
                cp.wait_recv()
            acc[chip] = (own[chip].astype(F32) + got[chip].astype(F32)).astype(acc.dtype)
            stores[chip].start()
        for cp in stores:
            cp.wait()
        for group in sends:
            for cp in group:
                cp.wait_send()

    return pl.pallas_call(
        body, name=f"pair_reduce{half}",
        in_specs=[_any()] * npart, out_specs=_any(),
        out_shape=jax.ShapeDtypeStruct(blk, gwt.dtype),
        scratch_shapes=[pltpu.VMEM(blk, gwt.dtype), pltpu.VMEM(blk, gwt.dtype), pltpu.VMEM(blk, gwt.dtype),
                        pltpu.SemaphoreType.DMA((4, npart)), pltpu.SemaphoreType.DMA((4, npart)),
                        pltpu.SemaphoreType.DMA((4, npart)), pltpu.SemaphoreType.DMA((4,))],
        compiler_params=_cp(),
    )(gwt, *tails)


def _pad_cols(a, cols):
    return jnp.pad(a, ((0, 0), (0, cols - a.shape[1])))


def _pad_rows(a, rows):
    return jnp.pad(a, ((0, rows - a.shape[0]), (0, 0)))


FRAME = 928


def _wft_plan():
    moves = []
    for blk in range(8):
        for half in range(2):
            for sub in range(2):
                moves.append((C_Q + 128 * blk + 32 * (2 * half + sub), 128 * blk + 32 * (2 * sub + half), 32))
    for idx in range(4):
        for dup in range(2):
            moves.append((C_KD + 64 * idx + 32 * dup, 1024 + 32 * idx, 32))
    for g in range(2):
        for dup in range(2):
            moves.append((C_VD + 128 * g + 64 * dup, 1152 + 64 * g, 64))
    moves += [(C_BL, 5376, RANK), (C_BV, 3328, 1024), (C_BQ, 2304, 512), (C_BK, 2816, 512),
              (C_AG, 1280, 1024), (C_BG, 4352, 1024), (C_MA, 5392, 1024), (C_MB, 6416, 1024)]
    bulk, seams = [], []
    for dst, src, n in moves:
        r = src
        while r < src + n:
            f = min(r // FRAME, NDEV - 1)
            local = r - FRAME * f
            if f > 0 and local < 16:
                assert local == 0
                seams.append((f, dst + r - src))
                step = 16
            else:
                step = min(src + n, FRAME * (f + 1) if f < NDEV - 1 else IN_WIDTH) - r
                bulk.append((f, local, dst + r - src, step))
            r += step
    assert sorted(f for f, _ in seams) == list(range(1, NDEV))
    return bulk, seams, [(C_BL + RANK, C_GLA - C_BL - RANK)]


def _build_wft_copies(frames):
    bulk, seams, zeros = _wft_plan()
    (z0, zn), = zeros

    def body(f_ref, o_ref, edge, sems, esems):
        copies = [pltpu.make_async_copy(f_ref.at[f, pl.ds(l0, n)], o_ref.at[pl.ds(dst, n)], sems.at[i])
                  for i, (f, l0, dst, n) in enumerate(bulk)]
        loads = []
        for i, (f, _) in enumerate(seams):
            loads.append(pltpu.make_async_copy(f_ref.at[f, pl.ds(0, 16)], edge.at[i, 0], esems.at[i, 0]))
            loads.append(pltpu.make_async_copy(f_ref.at[f - 1, pl.ds(FRAME, 16)], edge.at[i, 1], esems.at[i, 1]))
        for cp in copies + loads:
            cp.start()
        o_ref[z0:z0 + zn, :] = jnp.zeros((zn, D), o_ref.dtype)
        for cp in loads:
            cp.wait()
        for i, (_, dst) in enumerate(seams):
            o_ref[dst:dst + 16, :] = edge[i, 0] + edge[i, 1]
        for cp in copies:
            cp.wait()

    return pl.pallas_call(
        body, name="build_wft",
        in_specs=[_any()], out_specs=_vmem(),
        out_shape=jax.ShapeDtypeStruct((NF, D), frames.dtype),
        scratch_shapes=[pltpu.VMEM((len(seams), 2, 16, D), frames.dtype),
                        pltpu.SemaphoreType.DMA((len(bulk),)), pltpu.SemaphoreType.DMA((len(seams), 2))],
        compiler_params=_cp(),
    )(frames)


def kernel(x, positions, norm_w, w_in, a_sinks, b_gate_up, b_gate_bias, b_out_norm_w, w_a_proj, w_b_proj, w_out, final_norm_w, loss_target, m_norm_w, m_w_in, m_a_sinks, m_b_gate_up, m_b_gate_bias, m_b_out_norm_w, m_w_a_proj, m_w_b_proj, m_w_out, m_final_norm_w, v_norm_w, v_w_in, v_a_sinks, v_b_gate_up, v_b_gate_bias, v_b_out_norm_w, v_w_a_proj, v_w_b_proj, v_w_out, v_final_norm_w):
    T = x.shape[1]
    xs, target = x[0], loss_target[0]
    fnw = final_norm_w.reshape(1, D)
    me = 4 * lax.axis_index("x") + 2 * lax.axis_index("y") + lax.axis_index("c")
    allw, h, cos, sin = _gather_blocks(_by_lane_tile(w_in), b_gate_up[0], xs, norm_w, positions.reshape(T, 1))
    late_blk = jnp.concatenate([w_a_proj[0], w_b_proj[0], w_out[0]], axis=0).astype(WIRE)
    l_send, l_recv, l_blk, l_land, l_started = _late_gather_start(late_blk, cos)
    wf = _build_wft_copies(allw)
    gu = allw[:, SHARD_PAD:G_ROWS, :64].transpose(1, 0, 2).reshape(RANK, 512)
    gu_pad = _pad_rows(gu, W_BL)

    proj = _proj(h, wf, l_started)
    o_a, lse = _swa_fwd(proj, cos, sin, a_sinks)
    o_b, states = _gla_fwd(proj, gu_pad, b_gate_bias)
    l_blk, l_land = _late_gather_wait(l_send, l_recv, l_blk, l_land, states, lse)
    late = lax.dynamic_update_slice(l_land, l_blk[None], (me, 0, 0))
    (dx2, do_a, do_b, d_gates, g_late0, g_late1, g_fn, g_bn, loss_part) = _mid(
        xs, target, proj, o_a, o_b, late, jnp.tile(b_out_norm_w, (1, B_HEADS)), fnw)
    d_q, d_kv, g_sinks = _swa_bwd(proj, cos, sin, a_sinks, do_a, o_a, lse, cos)
    d_gla, d_bl, g_gu, g_bias = _gla_bwd(proj, gu_pad, b_gate_bias, states, do_b)
    pieces = [d_q, d_kv, d_bl, d_gla, d_gates]
    offsets = [C_Q, C_KD, C_BL, C_GLA, C_GATES]

    ggu = g_gu[:RANK].reshape(RANK, NDEV, 64).transpose(1, 0, 2)
    ggu_half = [jnp.pad(ggu, ((0, 0), (0, 0), (0, DH - 64))).astype(WIRE), jnp.zeros((NDEV, RANK, DH), WIRE)]
    tails = [[g_late0, ggu_half[0]], [g_late1, ggu_half[1]]]

    send0, recv0, s_thru0, land0, started0 = _chip_start(
        _pair_reduce(_gw_half(h, pieces, 0, after=g_bias), tails[0], 0), 0)
    send1, recv1, s_thru1, land1, started1 = _chip_start(
        _pair_reduce(_gw_half(h, pieces, 1, after=started0), tails[1], 1), 1)
    grad_x, g_nw = _dh_norm(pieces, offsets, wf, xs, dx2, norm_w, started1)
    small = jnp.concatenate([g_nw, g_fn, _pad_cols(g_bias, D), _pad_cols(g_bn, D), _pad_cols(g_sinks, D),
                             _pad_cols(loss_part, D)], axis=0)
    sm_send, sm_recv, sm_blk, sm_land, sm_started = _late_gather_start(small, g_nw, name="small_gather")
    sums0, got0 = _chip_wait(send0, recv0, s_thru0, land0, sm_started, 0)
    sums1, got1 = _chip_wait(send1, recv1, s_thru1, land1, got0, 1)
    sums, from_chips = [sums0, sums1], [got0, got1]

    ws = dict(norm_w=norm_w, fnw=fnw, bias=b_gate_bias, bn=b_out_norm_w, sinks=a_sinks)
    ms = dict(norm_w=m_norm_w, fnw=m_final_norm_w.reshape(1, D), bias=m_b_gate_bias, bn=m_b_out_norm_w,
              sinks=m_a_sinks)
    vs = dict(norm_w=v_norm_w, fnw=v_final_norm_w.reshape(1, D), bias=v_b_gate_bias, bn=v_b_out_norm_w,
              sinks=v_a_sinks)
    t_rows, t_gu = _finish(
        [_by_lane_tile(w_in), w_a_proj[0], w_b_proj[0], w_out[0]],
        [_by_lane_tile(m_w_in), m_w_a_proj[0], m_w_b_proj[0], m_w_out[0]],
        [_by_lane_tile(v_w_in), v_w_a_proj[0], v_w_b_proj[0], v_w_out[0]],
        b_gate_up[0], m_b_gate_up[0], v_b_gate_up[0], sums, from_chips)
    sm_blk, sm_land = _late_gather_wait(sm_send, sm_recv, sm_blk, sm_land, t_rows[0], t_gu[0], name="small_gather")
    loss, sm = _finish_small(ws, ms, vs, lax.dynamic_update_slice(sm_land, sm_blk[None], (me, 0, 0)))

    def outputs(k):
        return [sm["norm_w"][k], jnp.transpose(t_rows[k], (1, 2, 0)), sm["sinks"][k], t_gu[k][None], sm["bias"][k], sm["bn"][k],
                t_rows[4 + k][None], t_rows[8 + k][None], t_rows[12 + k][None], sm["fnw"][k].reshape(D)]

    return (loss[0, 0], grad_x[None], *outputs(0), *outputs(1), *outputs(2), *outputs(3))
```
